```python
import math
import jax, jax.numpy as jnp
from jax import lax
import numpy as np

D_MODEL = 1024
BATCH = 8
SEQ = 4096
DEPTH = 4

A_HEAD_DIM = 64
A_Q_HEADS = 8
A_KV_HEADS = 2
WINDOW = 128
ROPE_THETA = 10000.0
B_HEADS = 4
B_HEAD_DIM = 128
B_CONV = 4
B_CHUNK = 64
LRU_WIDTH = D_MODEL
LRU_BLOCKS = 4
LRU_CONV = 4
LRU_C = 8.0
D_FF = 4 * D_MODEL
A_Q_W = A_Q_HEADS * A_HEAD_DIM
A_KV_W = A_KV_HEADS * A_HEAD_DIM
B_W = B_HEADS * B_HEAD_DIM
HYB_SPLITS = (A_Q_W, A_KV_W, A_KV_W, 3 * B_W, B_W, B_HEADS, B_HEADS)
HYB_PROJ = A_Q_W + 2 * A_KV_W + 4 * B_W + 2 * B_HEADS
MIX_W = A_Q_W + B_W
N_HYB = (DEPTH + 1) // 2
N_REC = DEPTH // 2
DN_ALPHA = (2 * DEPTH) ** 0.25
DN_BETA = (8 * DEPTH) ** -0.25
LN_EPS = 1e-5
NORM_EPS = 1e-6

kernel_name = 'hybrid_swa_deltanet_rglru_deepnorm_trunk'


def split_cols(t, sizes):
    out, start = [], 0
    for s in sizes:
        out.append(t[..., start:start + s])
        start += s
    return out


def layer_norm(x, g, b):
    xf = x.astype(jnp.float32)
    mu = jnp.mean(xf, axis=-1, keepdims=True)
    var = jnp.mean(jnp.square(xf - mu), axis=-1, keepdims=True)
    y = (xf - mu) * lax.rsqrt(var + LN_EPS) * g.astype(jnp.float32) + b.astype(jnp.float32)
    return y.astype(x.dtype)


def l2_normalize(x):
    xf = x.astype(jnp.float32)
    return xf * lax.rsqrt(jnp.sum(xf * xf, axis=-1, keepdims=True) + NORM_EPS)


def apply_rope(x):
    T, dh = x.shape[1], x.shape[-1]
    half = dh // 2
    inv_freq = ROPE_THETA ** (-jnp.arange(half, dtype=jnp.float32) / half)
    ang = jnp.arange(T, dtype=jnp.float32)[:, None] * inv_freq[None, :]
    cos = jnp.cos(ang)[None, :, None, :]
    sin = jnp.sin(ang)[None, :, None, :]
    xf = x.astype(jnp.float32)
    x1, x2 = xf[..., :half], xf[..., half:]
    return jnp.concatenate([x1 * cos - x2 * sin, x2 * cos + x1 * sin], axis=-1).astype(x.dtype)


def causal_dwconv(x, w, b=None):
    K = w.shape[0]
    T = x.shape[1]
    xp = jnp.pad(x, ((0, 0), (K - 1, 0), (0, 0)))
    y = sum(xp[:, j:j + T] * w[j] for j in range(K))
    if b is not None:
        y = y + b
    return y


def sliding_window_attention(q, k, v, sinks):
    f32 = jnp.float32
    bsz, T, HQ, DH = q.shape
    HKV = k.shape[2]
    G = HQ // HKV
    W = WINDOW
    NB = T // W
    qb = q.reshape(bsz, NB, W, HKV, G, DH)

    def band(t):
        prev = jnp.pad(t, ((0, 0), (W, 0), (0, 0), (0, 0)))[:, :T]
        return jnp.concatenate([prev.reshape(bsz, NB, W, HKV, DH),
                                t.reshape(bsz, NB, W, HKV, DH)], axis=2)

    kb, vb = band(k), band(v)
    s = jnp.einsum('bnqhgd,bnkhd->bnhgqk', qb, kb).astype(f32) * (DH ** -0.5)
    dist = (jnp.arange(W)[:, None] + W) - jnp.arange(2 * W)[None, :]
    in_window = (dist >= 0) & (dist < W)
    key_pos = (jnp.arange(NB)[:, None] - 1) * W + jnp.arange(2 * W)[None, :]
    mask = in_window[None] & (key_pos >= 0)[:, None, :]
    s = jnp.where(mask[None, :, None, None], s, -jnp.inf)
    sink = jnp.broadcast_to(sinks.astype(f32).reshape(1, 1, HKV, G, 1, 1), s.shape[:-1] + (1,))
    p = jax.nn.softmax(jnp.concatenate([s, sink], axis=-1), axis=-1)[..., :-1]
    o = jnp.einsum('bnhgqk,bnkhd->bnqhgd', p.astype(v.dtype), vb)
    return o.reshape(bsz, T, HQ * DH)


def gated_delta_rule(q, k, v, g, beta):
    f32 = jnp.float32
    bsz, T, H, DK = q.shape
    DV = v.shape[-1]
    C = B_CHUNK
    N = T // C

    def chunk(t):
        t = t.astype(f32).reshape((bsz, N, C, H) + t.shape[3:])
        return jnp.moveaxis(t, 3, 1)

    q = chunk(q) * (DK ** -0.5)
    k = chunk(k)
    v = chunk(v)
    g = jnp.cumsum(chunk(g), axis=-1)
    beta = chunk(beta)
    k_beta = k * beta[..., None]
    v_beta = v * beta[..., None]
    idx = jnp.arange(C)
    incl = idx[:, None] >= idx[None, :]
    strict = idx[:, None] > idx[None, :]
    diff = g[..., :, None] - g[..., None, :]
    decay_incl = jnp.exp(jnp.where(incl, diff, -jnp.inf))
    decay_strict = jnp.where(strict, decay_incl, 0.0)
    a_mat = jnp.einsum('bhnik,bhnjk->bhnij', k_beta, k) * decay_strict
    eye = jnp.eye(C, dtype=f32)
    t_mat = lax.linalg.triangular_solve(a_mat + eye, jnp.broadcast_to(eye, a_mat.shape),
                                        left_side=True, lower=True, unit_diagonal=True)
    u = jnp.einsum('bhnij,bhnjv->bhniv', t_mat, v_beta)
    w = jnp.einsum('bhnij,bhnjk->bhnik', t_mat, k_beta * jnp.exp(g)[..., None])
    qk = jnp.einsum('bhnik,bhnjk->bhnij', q, k) * decay_incl
    q_g = q * jnp.exp(g)[..., None]
    g_last = g[..., -1]
    k_tail = k * jnp.exp(g_last[..., None] - g)[..., None]
    xs = tuple(jnp.moveaxis(t, 2, 0) for t in (q_g, k_tail, u, w, qk, g_last))

    def step(S, inp):
        q_c, kt_c, u_c, w_c, qk_c, gl_c = inp
        v_new = u_c - jnp.einsum('bhck,bhkv->bhcv', w_c, S)
        o_c = jnp.einsum('bhck,bhkv->bhcv', q_c, S) + jnp.einsum('bhij,bhjv->bhiv', qk_c, v_new)
        S = S * jnp.exp(gl_c)[..., None, None] + jnp.einsum('bhck,bhcv->bhkv', kt_c, v_new)
        return S, o_c

    S0 = jnp.zeros((bsz, H, DK, DV), f32)
    _, o = lax.scan(step, S0, xs)
    return jnp.transpose(o, (1, 0, 3, 2, 4)).reshape(bsz, T, H, DV)


def hybrid_mixer(x, w_in, sinks, conv_w, a_log, dt_bias, norm_w, w_out):
    f32 = jnp.float32
    bsz, T, _ = x.shape
    proj = x @ w_in
    qa, ka, va, qkv_b, z, beta_logit, a_logit = split_cols(proj, HYB_SPLITS)
    qa = apply_rope(qa.reshape(bsz, T, A_Q_HEADS, A_HEAD_DIM))
    ka = apply_rope(ka.reshape(bsz, T, A_KV_HEADS, A_HEAD_DIM))
    va = va.reshape(bsz, T, A_KV_HEADS, A_HEAD_DIM)
    o_a = sliding_window_attention(qa, ka, va, sinks)
    qkv_b = jax.nn.silu(causal_dwconv(qkv_b, conv_w))
    qb, kb, vb = split_cols(qkv_b, (B_W, B_W, B_W))
    qb = l2_normalize(qb.reshape(bsz, T, B_HEADS, B_HEAD_DIM))
    kb = l2_normalize(kb.reshape(bsz, T, B_HEADS, B_HEAD_DIM))
    vb = vb.reshape(bsz, T, B_HEADS, B_HEAD_DIM)
    beta = jax.nn.sigmoid(beta_logit.astype(f32))
    g = -jnp.exp(a_log.astype(f32)) * jax.nn.softplus(a_logit.astype(f32) + dt_bias.astype(f32))
    o_b = gated_delta_rule(qb, kb, vb, g, beta)
    o_b = o_b * lax.rsqrt(jnp.mean(o_b * o_b, axis=-1, keepdims=True) + NORM_EPS) * norm_w.astype(f32)
    o_b = o_b * jax.nn.silu(z.astype(f32).reshape(bsz, T, B_HEADS, B_HEAD_DIM))
    o_b = o_b.reshape(bsz, T, B_W).astype(x.dtype)
    return jnp.concatenate([o_a, o_b], axis=-1) @ w_out


def rg_lru(x, w_a, b_a, w_x, b_x, lam):
    f32 = jnp.float32
    bsz, T, W = x.shape
    xh = x.reshape(bsz, T, LRU_BLOCKS, W // LRU_BLOCKS)
    r = jax.nn.sigmoid((jnp.einsum('bthi,hij->bthj', xh, w_a).reshape(bsz, T, W) + b_a).astype(f32))
    i = jax.nn.sigmoid((jnp.einsum('bthi,hij->bthj', xh, w_x).reshape(bsz, T, W) + b_x).astype(f32))
    log_a = -LRU_C * r * jax.nn.softplus(-lam.astype(f32))
    a = jnp.exp(log_a)
    b = jnp.sqrt(-jnp.expm1(2.0 * log_a)) * (i * x.astype(f32))

    def combine(c1, c2):
        a1, b1 = c1
        a2, b2 = c2
        return a1 * a2, a2 * b1 + b2

    _, h = lax.associative_scan(combine, (a, b), axis=1)
    return h.astype(x.dtype)


def recurrent_mixer(x, w_in, conv_w, conv_b, w_a, b_a, w_x, b_x, lam, w_out):
    proj = x @ w_in
    xr, gate = split_cols(proj, (LRU_WIDTH, LRU_WIDTH))
    xr = causal_dwconv(xr, conv_w, conv_b)
    h = rg_lru(xr, w_a, b_a, w_x, b_x, lam)
    return (h * jax.nn.gelu(gate)) @ w_out


def sqrelu_mlp(x, w1, w2):
    h = jax.nn.relu(x @ w1)
    return (h * h) @ w2


def _fwd_setup_inputs(seed: int = 0) -> dict:
    key = jax.random.key(seed)
    ks = jax.random.split(key, 24)
    f32 = jnp.float32

    def nrm(k, shape, scale):
        return jax.random.normal(k, shape, f32) * scale

    bw = LRU_WIDTH // LRU_BLOCKS
    x = nrm(ks[0], (BATCH, SEQ, D_MODEL), 1.0)
    hyb_w_in = nrm(ks[1], (N_HYB, D_MODEL, HYB_PROJ), D_MODEL ** -0.5)
    hyb_sinks = nrm(ks[2], (N_HYB, A_Q_HEADS), 0.5)
    hyb_conv_w = nrm(ks[3], (N_HYB, B_CONV, 3 * B_W), B_CONV ** -0.5)
    hyb_a_log = jnp.log(jax.random.uniform(ks[4], (N_HYB, B_HEADS), f32, 1.0, 16.0))
    dt = jnp.exp(jax.random.uniform(ks[5], (N_HYB, B_HEADS), f32, math.log(1e-3), math.log(1e-1)))
    hyb_dt_bias = dt + jnp.log(-jnp.expm1(-dt))
    hyb_norm_w = 1.0 + nrm(ks[6], (N_HYB, B_HEAD_DIM), 0.02)
    hyb_w_out = nrm(ks[7], (N_HYB, MIX_W, D_MODEL), DN_BETA * MIX_W ** -0.5)
    rec_w_in = nrm(ks[8], (N_REC, D_MODEL, 2 * LRU_WIDTH), D_MODEL ** -0.5)
    rec_conv_w = nrm(ks[9], (N_REC, LRU_CONV, LRU_WIDTH), LRU_CONV ** -0.5)
    rec_conv_b = nrm(ks[10], (N_REC, LRU_WIDTH), 0.02)
    rec_w_a = nrm(ks[11], (N_REC, LRU_BLOCKS, bw, bw), bw ** -0.5)
    rec_b_a = nrm(ks[12], (N_REC, LRU_WIDTH), 0.02)
    rec_w_x = nrm(ks[13], (N_REC, LRU_BLOCKS, bw, bw), bw ** -0.5)
    rec_b_x = nrm(ks[14], (N_REC, LRU_WIDTH), 0.02)
    u = jax.random.uniform(ks[15], (N_REC, LRU_WIDTH), f32, 0.9, 0.999)
    a0 = u ** (1.0 / LRU_C)
    rec_lambda = jnp.log(a0) - jnp.log1p(-a0)
    rec_w_out = nrm(ks[16], (N_REC, LRU_WIDTH, D_MODEL), DN_BETA * LRU_WIDTH ** -0.5)
    ln1_g = 1.0 + nrm(ks[17], (DEPTH, D_MODEL), 0.02)
    ln1_b = nrm(ks[18], (DEPTH, D_MODEL), 0.02)
    mlp_w1 = nrm(ks[19], (DEPTH, D_MODEL, D_FF), D_MODEL ** -0.5)
    mlp_w2 = nrm(ks[20], (DEPTH, D_FF, D_MODEL), DN_BETA * D_FF ** -0.5)
    ln2_g = 1.0 + nrm(ks[21], (DEPTH, D_MODEL), 0.02)
    ln2_b = nrm(ks[22], (DEPTH, D_MODEL), 0.02)
    return {'x': x, 'hyb_w_in': hyb_w_in, 'hyb_sinks': hyb_sinks, 'hyb_conv_w': hyb_conv_w,
            'hyb_a_log': hyb_a_log, 'hyb_dt_bias': hyb_dt_bias, 'hyb_norm_w': hyb_norm_w,
            'hyb_w_out': hyb_w_out, 'rec_w_in': rec_w_in, 'rec_conv_w': rec_conv_w,
            'rec_conv_b': rec_conv_b, 'rec_w_a': rec_w_a, 'rec_b_a': rec_b_a, 'rec_w_x': rec_w_x,
            'rec_b_x': rec_b_x, 'rec_lambda': rec_lambda, 'rec_w_out': rec_w_out,
            'ln1_g': ln1_g, 'ln1_b': ln1_b, 'mlp_w1': mlp_w1, 'mlp_w2': mlp_w2,
            'ln2_g': ln2_g, 'ln2_b': ln2_b}


def _fwd_reference(x, hyb_w_in, hyb_sinks, hyb_conv_w, hyb_a_log, hyb_dt_bias, hyb_norm_w, hyb_w_out,
              rec_w_in, rec_conv_w, rec_conv_b, rec_w_a, rec_b_a, rec_w_x, rec_b_x, rec_lambda,
              rec_w_out, ln1_g, ln1_b, mlp_w1, mlp_w2, ln2_g, ln2_b):
    for layer in range(DEPTH):
        j = layer // 2
        if layer % 2 == 0:
            mix = hybrid_mixer(x, hyb_w_in[j], hyb_sinks[j], hyb_conv_w[j], hyb_a_log[j],
                               hyb_dt_bias[j], hyb_norm_w[j], hyb_w_out[j])
        else:
            mix = recurrent_mixer(x, rec_w_in[j], rec_conv_w[j], rec_conv_b[j], rec_w_a[j],
                                  rec_b_a[j], rec_w_x[j], rec_b_x[j], rec_lambda[j], rec_w_out[j])
        x = layer_norm(DN_ALPHA * x + mix, ln1_g[layer], ln1_b[layer])
        x = layer_norm(DN_ALPHA * x + sqrelu_mlp(x, mlp_w1[layer], mlp_w2[layer]), ln2_g[layer], ln2_b[layer])
    return x


import jax as _jax
import jax.numpy as _jnp

TWIN_FORMAT = 'train_step'
FWD_PARAMS = ['x', 'hyb_w_in', 'hyb_sinks', 'hyb_conv_w', 'hyb_a_log', 'hyb_dt_bias', 'hyb_norm_w', 'hyb_w_out', 'rec_w_in', 'rec_conv_w', 'rec_conv_b', 'rec_w_a', 'rec_b_a', 'rec_w_x', 'rec_b_x', 'rec_lambda', 'rec_w_out', 'ln1_g', 'ln1_b', 'mlp_w1', 'mlp_w2', 'ln2_g', 'ln2_b']
TWIN_WEIGHTS = ['hyb_w_in', 'hyb_sinks', 'hyb_conv_w', 'hyb_a_log', 'hyb_dt_bias', 'hyb_norm_w', 'hyb_w_out', 'rec_w_in', 'rec_conv_w', 'rec_conv_b', 'rec_w_a', 'rec_b_a', 'rec_w_x', 'rec_b_x', 'rec_lambda', 'rec_w_out', 'ln1_g', 'ln1_b', 'mlp_w1', 'mlp_w2', 'ln2_g', 'ln2_b']
TWIN_DIFF_INPUT = 'x'
TWIN_INPUTS = ['x', 'hyb_w_in', 'hyb_sinks', 'hyb_conv_w', 'hyb_a_log', 'hyb_dt_bias', 'hyb_norm_w', 'hyb_w_out', 'rec_w_in', 'rec_conv_w', 'rec_conv_b', 'rec_w_a', 'rec_b_a', 'rec_w_x', 'rec_b_x', 'rec_lambda', 'rec_w_out', 'ln1_g', 'ln1_b', 'mlp_w1', 'mlp_w2', 'ln2_g', 'ln2_b', 'loss_target', 'm_hyb_w_in', 'm_hyb_sinks', 'm_hyb_conv_w', 'm_hyb_a_log', 'm_hyb_dt_bias', 'm_hyb_norm_w', 'm_hyb_w_out', 'm_rec_w_in', 'm_rec_conv_w', 'm_rec_conv_b', 'm_rec_w_a', 'm_rec_b_a', 'm_rec_w_x', 'm_rec_b_x', 'm_rec_lambda', 'm_rec_w_out', 'm_ln1_g', 'm_ln1_b', 'm_mlp_w1', 'm_mlp_w2', 'm_ln2_g', 'm_ln2_b', 'v_hyb_w_in', 'v_hyb_sinks', 'v_hyb_conv_w', 'v_hyb_a_log', 'v_hyb_dt_bias', 'v_hyb_norm_w', 'v_hyb_w_out', 'v_rec_w_in', 'v_rec_conv_w', 'v_rec_conv_b', 'v_rec_w_a', 'v_rec_b_a', 'v_rec_w_x', 'v_rec_b_x', 'v_rec_lambda', 'v_rec_w_out', 'v_ln1_g', 'v_ln1_b', 'v_mlp_w1', 'v_mlp_w2', 'v_ln2_g', 'v_ln2_b']
TWIN_OUTPUTS = ['loss', 'grad_x', 'grad_hyb_w_in', 'grad_hyb_sinks', 'grad_hyb_conv_w', 'grad_hyb_a_log', 'grad_hyb_dt_bias', 'grad_hyb_norm_w', 'grad_hyb_w_out', 'grad_rec_w_in', 'grad_rec_conv_w', 'grad_rec_conv_b', 'grad_rec_w_a', 'grad_rec_b_a', 'grad_rec_w_x', 'grad_rec_b_x', 'grad_rec_lambda', 'grad_rec_w_out', 'grad_ln1_g', 'grad_ln1_b', 'grad_mlp_w1', 'grad_mlp_w2', 'grad_ln2_g', 'grad_ln2_b', 'delta_hyb_w_in', 'delta_hyb_sinks', 'delta_hyb_conv_w', 'delta_hyb_a_log', 'delta_hyb_dt_bias', 'delta_hyb_norm_w', 'delta_hyb_w_out', 'delta_rec_w_in', 'delta_rec_conv_w', 'delta_rec_conv_b', 'delta_rec_w_a', 'delta_rec_b_a', 'delta_rec_w_x', 'delta_rec_b_x', 'delta_rec_lambda', 'delta_rec_w_out', 'delta_ln1_g', 'delta_ln1_b', 'delta_mlp_w1', 'delta_mlp_w2', 'delta_ln2_g', 'delta_ln2_b', 'new_m_hyb_w_in', 'new_m_hyb_sinks', 'new_m_hyb_conv_w', 'new_m_hyb_a_log', 'new_m_hyb_dt_bias', 'new_m_hyb_norm_w', 'new_m_hyb_w_out', 'new_m_rec_w_in', 'new_m_rec_conv_w', 'new_m_rec_conv_b', 'new_m_rec_w_a', 'new_m_rec_b_a', 'new_m_rec_w_x', 'new_m_rec_b_x', 'new_m_rec_lambda', 'new_m_rec_w_out', 'new_m_ln1_g', 'new_m_ln1_b', 'new_m_mlp_w1', 'new_m_mlp_w2', 'new_m_ln2_g', 'new_m_ln2_b', 'new_v_hyb_w_in', 'new_v_hyb_sinks', 'new_v_hyb_conv_w', 'new_v_hyb_a_log', 'new_v_hyb_dt_bias', 'new_v_hyb_norm_w', 'new_v_hyb_w_out', 'new_v_rec_w_in', 'new_v_rec_conv_w', 'new_v_rec_conv_b', 'new_v_rec_w_a', 'new_v_rec_b_a', 'new_v_rec_w_x', 'new_v_rec_b_x', 'new_v_rec_lambda', 'new_v_rec_w_out', 'new_v_ln1_g', 'new_v_ln1_b', 'new_v_mlp_w1', 'new_v_mlp_w2', 'new_v_ln2_g', 'new_v_ln2_b']
TWIN_LEAF_KINDS = {'loss': 'loss', 'grad_x': 'grad_x', 'grad_hyb_w_in': 'grad_w', 'grad_hyb_sinks': 'grad_w', 'grad_hyb_conv_w': 'grad_w', 'grad_hyb_a_log': 'grad_w', 'grad_hyb_dt_bias': 'grad_w', 'grad_hyb_norm_w': 'grad_w', 'grad_hyb_w_out': 'grad_w', 'grad_rec_w_in': 'grad_w', 'grad_rec_conv_w': 'grad_w', 'grad_rec_conv_b': 'grad_w', 'grad_rec_w_a': 'grad_w', 'grad_rec_b_a': 'grad_w', 'grad_rec_w_x': 'grad_w', 'grad_rec_b_x': 'grad_w', 'grad_rec_lambda': 'grad_w', 'grad_rec_w_out': 'grad_w', 'grad_ln1_g': 'grad_w', 'grad_ln1_b': 'grad_w', 'grad_mlp_w1': 'grad_w', 'grad_mlp_w2': 'grad_w', 'grad_ln2_g': 'grad_w', 'grad_ln2_b': 'grad_w', 'delta_hyb_w_in': 'delta_w', 'delta_hyb_sinks': 'delta_w', 'delta_hyb_conv_w': 'delta_w', 'delta_hyb_a_log': 'delta_w', 'delta_hyb_dt_bias': 'delta_w', 'delta_hyb_norm_w': 'delta_w', 'delta_hyb_w_out': 'delta_w', 'delta_rec_w_in': 'delta_w', 'delta_rec_conv_w': 'delta_w', 'delta_rec_conv_b': 'delta_w', 'delta_rec_w_a': 'delta_w', 'delta_rec_b_a': 'delta_w', 'delta_rec_w_x': 'delta_w', 'delta_rec_b_x': 'delta_w', 'delta_rec_lambda': 'delta_w', 'delta_rec_w_out': 'delta_w', 'delta_ln1_g': 'delta_w', 'delta_ln1_b': 'delta_w', 'delta_mlp_w1': 'delta_w', 'delta_mlp_w2': 'delta_w', 'delta_ln2_g': 'delta_w', 'delta_ln2_b': 'delta_w', 'new_m_hyb_w_in': 'new_m', 'new_m_hyb_sinks': 'new_m', 'new_m_hyb_conv_w': 'new_m', 'new_m_hyb_a_log': 'new_m', 'new_m_hyb_dt_bias': 'new_m', 'new_m_hyb_norm_w': 'new_m', 'new_m_hyb_w_out': 'new_m', 'new_m_rec_w_in': 'new_m', 'new_m_rec_conv_w': 'new_m', 'new_m_rec_conv_b': 'new_m', 'new_m_rec_w_a': 'new_m', 'new_m_rec_b_a': 'new_m', 'new_m_rec_w_x': 'new_m', 'new_m_rec_b_x': 'new_m', 'new_m_rec_lambda': 'new_m', 'new_m_rec_w_out': 'new_m', 'new_m_ln1_g': 'new_m', 'new_m_ln1_b': 'new_m', 'new_m_mlp_w1': 'new_m', 'new_m_mlp_w2': 'new_m', 'new_m_ln2_g': 'new_m', 'new_m_ln2_b': 'new_m', 'new_v_hyb_w_in': 'new_v', 'new_v_hyb_sinks': 'new_v', 'new_v_hyb_conv_w': 'new_v', 'new_v_hyb_a_log': 'new_v', 'new_v_hyb_dt_bias': 'new_v', 'new_v_hyb_norm_w': 'new_v', 'new_v_hyb_w_out': 'new_v', 'new_v_rec_w_in': 'new_v', 'new_v_rec_conv_w': 'new_v', 'new_v_rec_conv_b': 'new_v', 'new_v_rec_w_a': 'new_v', 'new_v_rec_b_a': 'new_v', 'new_v_rec_w_x': 'new_v', 'new_v_rec_b_x': 'new_v', 'new_v_rec_lambda': 'new_v', 'new_v_rec_w_out': 'new_v', 'new_v_ln1_g': 'new_v', 'new_v_ln1_b': 'new_v', 'new_v_mlp_w1': 'new_v', 'new_v_mlp_w2': 'new_v', 'new_v_ln2_g': 'new_v', 'new_v_ln2_b': 'new_v'}


def _forward(args):
    return _fwd_reference(*[args[k] for k in FWD_PARAMS])


def _output_shape():
    out = _jax.eval_shape(lambda: _forward(_fwd_setup_inputs(0)))
    return out.shape, out.dtype

N_MICROBATCH = 1
ADAM_LR = 0.001
ADAM_B1 = 0.9
ADAM_B2 = 0.999
ADAM_EPS = 1e-08
ADAM_WD = 0.01
ADAM_STEP = 10
PER_EXAMPLE_BATCH_AXIS = {'x': 0, 'loss_target': 0}
SHARED_INPUTS = []
_WEIGHT_DTYPES = {'hyb_w_in': _jnp.float32, 'hyb_sinks': _jnp.float32, 'hyb_conv_w': _jnp.float32, 'hyb_a_log': _jnp.float32, 'hyb_dt_bias': _jnp.float32, 'hyb_norm_w': _jnp.float32, 'hyb_w_out': _jnp.float32, 'rec_w_in': _jnp.float32, 'rec_conv_w': _jnp.float32, 'rec_conv_b': _jnp.float32, 'rec_w_a': _jnp.float32, 'rec_b_a': _jnp.float32, 'rec_w_x': _jnp.float32, 'rec_b_x': _jnp.float32, 'rec_lambda': _jnp.float32, 'rec_w_out': _jnp.float32, 'ln1_g': _jnp.float32, 'ln1_b': _jnp.float32, 'mlp_w1': _jnp.float32, 'mlp_w2': _jnp.float32, 'ln2_g': _jnp.float32, 'ln2_b': _jnp.float32}
MOMENT_SCALE = {'hyb_w_in': 2.091272e-02, 'hyb_sinks': 9.885709e-03, 'hyb_conv_w': 2.138829e-02, 'hyb_a_log': 3.468093e-01, 'hyb_dt_bias': 3.055823e-01, 'hyb_norm_w': 6.060794e-02, 'hyb_w_out': 5.667734e-02, 'rec_w_in': 4.940852e-02, 'rec_conv_w': 6.218566e-02, 'rec_conv_b': 3.460245e-01, 'rec_w_a': 6.296118e-03, 'rec_b_a': 1.088695e-02, 'rec_w_x': 1.164327e-02, 'rec_b_x': 2.296870e-02, 'rec_lambda': 2.809956e-02, 'rec_w_out': 1.155303e-01, 'ln1_g': 7.730382e-01, 'ln1_b': 5.420525e-01, 'mlp_w1': 3.158853e-02, 'mlp_w2': 1.826274e-01, 'ln2_g': 1.613092e+01, 'ln2_b': 4.033288e+00}


def _to_microbatches(a, axis):
    t = _jnp.moveaxis(a, axis, 0)
    t = t.reshape((N_MICROBATCH, t.shape[0] // N_MICROBATCH) + t.shape[1:])
    return _jnp.moveaxis(t, 1, axis + 1)


def setup_inputs(seed: int = 0) -> dict:
    inp = _fwd_setup_inputs(seed)
    key = _jax.random.fold_in(_jax.random.key(seed), 7919)
    shape, _ = _output_shape()
    out = dict(inp)
    out["loss_target"] = _jax.random.normal(_jax.random.fold_in(key, 0), shape, _jnp.float32)
    for i, name in enumerate(TWIN_WEIGHTS):
        w = inp[name].astype(_jnp.float32)
        if MOMENT_SCALE is None:
            s = _jnp.sqrt(_jnp.mean(_jnp.square(w)) + 1e-30)
        else:
            s = MOMENT_SCALE[name]
        km, kv = _jax.random.split(_jax.random.fold_in(key, i + 1))
        out[name] = w
        out["m_" + name] = s * _jax.random.normal(km, w.shape, _jnp.float32)
        out["v_" + name] = (s * s) * _jax.random.uniform(kv, w.shape, _jnp.float32, 0.5, 1.5)
    if N_MICROBATCH > 1:
        for name, axis in PER_EXAMPLE_BATCH_AXIS.items():
            out[name] = _to_microbatches(out[name], axis)
    return {'x': out['x'], 'hyb_w_in': out['hyb_w_in'], 'hyb_sinks': out['hyb_sinks'], 'hyb_conv_w': out['hyb_conv_w'], 'hyb_a_log': out['hyb_a_log'], 'hyb_dt_bias': out['hyb_dt_bias'], 'hyb_norm_w': out['hyb_norm_w'], 'hyb_w_out': out['hyb_w_out'], 'rec_w_in': out['rec_w_in'], 'rec_conv_w': out['rec_conv_w'], 'rec_conv_b': out['rec_conv_b'], 'rec_w_a': out['rec_w_a'], 'rec_b_a': out['rec_b_a'], 'rec_w_x': out['rec_w_x'], 'rec_b_x': out['rec_b_x'], 'rec_lambda': out['rec_lambda'], 'rec_w_out': out['rec_w_out'], 'ln1_g': out['ln1_g'], 'ln1_b': out['ln1_b'], 'mlp_w1': out['mlp_w1'], 'mlp_w2': out['mlp_w2'], 'ln2_g': out['ln2_g'], 'ln2_b': out['ln2_b'], 'loss_target': out['loss_target'], 'm_hyb_w_in': out['m_hyb_w_in'], 'm_hyb_sinks': out['m_hyb_sinks'], 'm_hyb_conv_w': out['m_hyb_conv_w'], 'm_hyb_a_log': out['m_hyb_a_log'], 'm_hyb_dt_bias': out['m_hyb_dt_bias'], 'm_hyb_norm_w': out['m_hyb_norm_w'], 'm_hyb_w_out': out['m_hyb_w_out'], 'm_rec_w_in': out['m_rec_w_in'], 'm_rec_conv_w': out['m_rec_conv_w'], 'm_rec_conv_b': out['m_rec_conv_b'], 'm_rec_w_a': out['m_rec_w_a'], 'm_rec_b_a': out['m_rec_b_a'], 'm_rec_w_x': out['m_rec_w_x'], 'm_rec_b_x': out['m_rec_b_x'], 'm_rec_lambda': out['m_rec_lambda'], 'm_rec_w_out': out['m_rec_w_out'], 'm_ln1_g': out['m_ln1_g'], 'm_ln1_b': out['m_ln1_b'], 'm_mlp_w1': out['m_mlp_w1'], 'm_mlp_w2': out['m_mlp_w2'], 'm_ln2_g': out['m_ln2_g'], 'm_ln2_b': out['m_ln2_b'], 'v_hyb_w_in': out['v_hyb_w_in'], 'v_hyb_sinks': out['v_hyb_sinks'], 'v_hyb_conv_w': out['v_hyb_conv_w'], 'v_hyb_a_log': out['v_hyb_a_log'], 'v_hyb_dt_bias': out['v_hyb_dt_bias'], 'v_hyb_norm_w': out['v_hyb_norm_w'], 'v_hyb_w_out': out['v_hyb_w_out'], 'v_rec_w_in': out['v_rec_w_in'], 'v_rec_conv_w': out['v_rec_conv_w'], 'v_rec_conv_b': out['v_rec_conv_b'], 'v_rec_w_a': out['v_rec_w_a'], 'v_rec_b_a': out['v_rec_b_a'], 'v_rec_w_x': out['v_rec_w_x'], 'v_rec_b_x': out['v_rec_b_x'], 'v_rec_lambda': out['v_rec_lambda'], 'v_rec_w_out': out['v_rec_w_out'], 'v_ln1_g': out['v_ln1_g'], 'v_ln1_b': out['v_ln1_b'], 'v_mlp_w1': out['v_mlp_w1'], 'v_mlp_w2': out['v_mlp_w2'], 'v_ln2_g': out['v_ln2_g'], 'v_ln2_b': out['v_ln2_b']}


def _loss(weights, diff, rest, loss_target):
    with _jax.named_scope("forward"):
        args = {**rest, TWIN_DIFF_INPUT: diff, **{k: w.astype(_WEIGHT_DTYPES[k]) for k, w in weights.items()}}
        y = _forward(args)
    with _jax.named_scope("loss_head"):
        err = _jnp.square(y.astype(_jnp.float32) - loss_target)
        return 0.5 * _jnp.sum(_jnp.mean(err, axis=-1)) if err.ndim else 0.5 * err


def _adamw(w, g, m, v):
    m = ADAM_B1 * m + (1.0 - ADAM_B1) * g
    v = ADAM_B2 * v + (1.0 - ADAM_B2) * _jnp.square(g)
    m_hat = m / (1.0 - ADAM_B1 ** ADAM_STEP)
    v_hat = v / (1.0 - ADAM_B2 ** ADAM_STEP)
    delta = -ADAM_LR * (m_hat / (_jnp.sqrt(v_hat) + ADAM_EPS) + ADAM_WD * w)
    return delta, m, v


def reference(x, hyb_w_in, hyb_sinks, hyb_conv_w, hyb_a_log, hyb_dt_bias, hyb_norm_w, hyb_w_out, rec_w_in, rec_conv_w, rec_conv_b, rec_w_a, rec_b_a, rec_w_x, rec_b_x, rec_lambda, rec_w_out, ln1_g, ln1_b, mlp_w1, mlp_w2, ln2_g, ln2_b, loss_target, m_hyb_w_in, m_hyb_sinks, m_hyb_conv_w, m_hyb_a_log, m_hyb_dt_bias, m_hyb_norm_w, m_hyb_w_out, m_rec_w_in, m_rec_conv_w, m_rec_conv_b, m_rec_w_a, m_rec_b_a, m_rec_w_x, m_rec_b_x, m_rec_lambda, m_rec_w_out, m_ln1_g, m_ln1_b, m_mlp_w1, m_mlp_w2, m_ln2_g, m_ln2_b, v_hyb_w_in, v_hyb_sinks, v_hyb_conv_w, v_hyb_a_log, v_hyb_dt_bias, v_hyb_norm_w, v_hyb_w_out, v_rec_w_in, v_rec_conv_w, v_rec_conv_b, v_rec_w_a, v_rec_b_a, v_rec_w_x, v_rec_b_x, v_rec_lambda, v_rec_w_out, v_ln1_g, v_ln1_b, v_mlp_w1, v_mlp_w2, v_ln2_g, v_ln2_b):
    given = dict(x=x, hyb_w_in=hyb_w_in, hyb_sinks=hyb_sinks, hyb_conv_w=hyb_conv_w, hyb_a_log=hyb_a_log, hyb_dt_bias=hyb_dt_bias, hyb_norm_w=hyb_norm_w, hyb_w_out=hyb_w_out, rec_w_in=rec_w_in, rec_conv_w=rec_conv_w, rec_conv_b=rec_conv_b, rec_w_a=rec_w_a, rec_b_a=rec_b_a, rec_w_x=rec_w_x, rec_b_x=rec_b_x, rec_lambda=rec_lambda, rec_w_out=rec_w_out, ln1_g=ln1_g, ln1_b=ln1_b, mlp_w1=mlp_w1, mlp_w2=mlp_w2, ln2_g=ln2_g, ln2_b=ln2_b, loss_target=loss_target, m_hyb_w_in=m_hyb_w_in, m_hyb_sinks=m_hyb_sinks, m_hyb_conv_w=m_hyb_conv_w, m_hyb_a_log=m_hyb_a_log, m_hyb_dt_bias=m_hyb_dt_bias, m_hyb_norm_w=m_hyb_norm_w, m_hyb_w_out=m_hyb_w_out, m_rec_w_in=m_rec_w_in, m_rec_conv_w=m_rec_conv_w, m_rec_conv_b=m_rec_conv_b, m_rec_w_a=m_rec_w_a, m_rec_b_a=m_rec_b_a, m_rec_w_x=m_rec_w_x, m_rec_b_x=m_rec_b_x, m_rec_lambda=m_rec_lambda, m_rec_w_out=m_rec_w_out, m_ln1_g=m_ln1_g, m_ln1_b=m_ln1_b, m_mlp_w1=m_mlp_w1, m_mlp_w2=m_mlp_w2, m_ln2_g=m_ln2_g, m_ln2_b=m_ln2_b, v_hyb_w_in=v_hyb_w_in, v_hyb_sinks=v_hyb_sinks, v_hyb_conv_w=v_hyb_conv_w, v_hyb_a_log=v_hyb_a_log, v_hyb_dt_bias=v_hyb_dt_bias, v_hyb_norm_w=v_hyb_norm_w, v_hyb_w_out=v_hyb_w_out, v_rec_w_in=v_rec_w_in, v_rec_conv_w=v_rec_conv_w, v_rec_conv_b=v_rec_conv_b, v_rec_w_a=v_rec_w_a, v_rec_b_a=v_rec_b_a, v_rec_w_x=v_rec_w_x, v_rec_b_x=v_rec_b_x, v_rec_lambda=v_rec_lambda, v_rec_w_out=v_rec_w_out, v_ln1_g=v_ln1_g, v_ln1_b=v_ln1_b, v_mlp_w1=v_mlp_w1, v_mlp_w2=v_mlp_w2, v_ln2_g=v_ln2_g, v_ln2_b=v_ln2_b)
    weights = {n: given[n] for n in TWIN_WEIGHTS}
    shared = {n: given[n] for n in SHARED_INPUTS}
    per_example = {n: given[n] for n in ['x']}
    grad_fn = _jax.value_and_grad(_loss, argnums=(0, 1))

    def one_microbatch(ex, loss_target):
        ex = dict(ex)
        diff = ex.pop(TWIN_DIFF_INPUT)
        return grad_fn(weights, diff, {**shared, **ex}, loss_target)

    if N_MICROBATCH == 1:
        loss, (grad_w, grad_x) = one_microbatch(per_example, given["loss_target"])
    else:
        def body(carry, xs):
            loss_sum, grad_sum = carry
            l_k, (gw_k, gx_k) = one_microbatch(xs[0], xs[1])
            with _jax.named_scope("update"):
                return (loss_sum + l_k, _jax.tree.map(_jnp.add, grad_sum, gw_k)), gx_k

        init = (_jnp.zeros((), _jnp.float32), _jax.tree.map(_jnp.zeros_like, weights))
        (loss, grad_w), grad_x = _jax.lax.scan(body, init, (per_example, given["loss_target"]))
    with _jax.named_scope("update"):
        delta_w, new_m, new_v = {}, {}, {}
        for n in TWIN_WEIGHTS:
            delta_w[n], new_m[n], new_v[n] = _adamw(weights[n], grad_w[n], given["m_" + n], given["v_" + n])
    return (loss, grad_x, *[grad_w[n] for n in TWIN_WEIGHTS], *[delta_w[n] for n in TWIN_WEIGHTS],
            *[new_m[n] for n in TWIN_WEIGHTS], *[new_v[n] for n in TWIN_WEIGHTS])
```

```python
import functools

import jax
import jax.numpy as jnp
from jax import lax
from jax.experimental import pallas as pl
from jax.experimental.pallas import tpu as pltpu

F32 = jnp.float32
MXU_DTYPE = jnp.bfloat16

D_MODEL = 1024
DEPTH = 4
A_HEAD_DIM = 64
A_Q_HEADS = 8
A_KV_HEADS = 2
WINDOW = 128
ROPE_THETA = 10000.0
B_HEADS = 4
B_HEAD_DIM = 128
B_CHUNK = 64
CONV_K = 4
LRU_BLOCKS = 4
LRU_BLOCK_W = D_MODEL // LRU_BLOCKS
LRU_C = 8.0
D_FF = 4 * D_MODEL
A_Q_W = A_Q_HEADS * A_HEAD_DIM
A_KV_W = A_KV_HEADS * A_HEAD_DIM
B_W = B_HEADS * B_HEAD_DIM
HYB_PROJ = A_Q_W + 2 * A_KV_W + 4 * B_W + 2 * B_HEADS
DN_ALPHA = (2 * DEPTH) ** 0.25
LN_EPS = 1e-5
NORM_EPS = 1e-6
ADAM_LR = 0.001
ADAM_B1 = 0.9
ADAM_B2 = 0.999
ADAM_EPS = 1e-08
ADAM_WD = 0.01
ADAM_STEP = 10

HP_Q = 0
HP_QKVB = 512
HP_Z = 2048
HP_K = 2560
HP_V = 2688
HP_BG = 2816
HYB_PAD = 3072

N_CHIPS = 4
N_DEV = 8
V7X_VMEM_LIMIT = 48 * 1024 * 1024
LANES = 128
SUBLANES = 8
NEG_BIG = -1e30

NN = (((1,), (0,)), ((), ()))
NT = (((1,), (1,)), ((), ()))
TN = (((0,), (0,)), ((), ()))


def _cparams(*sem):
    return pltpu.CompilerParams(dimension_semantics=sem, vmem_limit_bytes=V7X_VMEM_LIMIT)


def _dot(a, b, dims=NN):
    return lax.dot_general(a.astype(MXU_DTYPE), b.astype(MXU_DTYPE), dims, preferred_element_type=F32)


def _dotf(a, b, dims=NN):
    return lax.dot_general(a, b, dims, precision=lax.Precision.HIGHEST, preferred_element_type=F32)


def _tile(dim, pref):
    t = min(dim, pref)
    while dim % t:
        t //= 2
    return t


def _sigmoid(x):
    return 1.0 / (1.0 + jnp.exp(-x))


def _silu(x):
    return x * _sigmoid(x)


def _dsilu(x):
    s = _sigmoid(x)
    return s * (1.0 + x * (1.0 - s))


GELU_C = 0.7978845608028654
GELU_A = 0.044715


def _gelu(x):
    return 0.5 * x * (1.0 + jnp.tanh(GELU_C * (x + GELU_A * x * x * x)))


def _dgelu(x):
    t = jnp.tanh(GELU_C * (x + GELU_A * x * x * x))
    return 0.5 * (1.0 + t) + 0.5 * x * (1.0 - t * t) * GELU_C * (1.0 + 3.0 * GELU_A * x * x)


def _matmul(a, b, mode, name, *, tm=512, tn=1024, tk=1024, a_fn=None, epi=None, extra=None):
    if mode == "nn":
        (M, K), (_, N) = a.shape, b.shape
    elif mode == "nt":
        (M, K), (N, _) = a.shape, b.shape
    else:
        (K, M), (_, N) = a.shape, b.shape
    tm, tn, tk = _tile(M, tm), _tile(N, tn), _tile(K, tk)
    nk = K // tk
    if mode == "tn":
        a_spec = pl.BlockSpec((tk, tm), lambda i, j, k: (k, i))
    else:
        a_spec = pl.BlockSpec((tm, tk), lambda i, j, k: (i, k))
    if mode == "nt":
        b_spec = pl.BlockSpec((tn, tk), lambda i, j, k: (j, k))
    else:
        b_spec = pl.BlockSpec((tk, tn), lambda i, j, k: (k, j))
    o_spec = pl.BlockSpec((tm, tn), lambda i, j, k: (i, j))
    dims = {"nn": NN, "nt": NT, "tn": TN}[mode]
    has_extra = extra is not None

    def body(*refs):
        a_ref, b_ref = refs[0], refs[1]
        e_ref = refs[2] if has_extra else None
        o_ref = refs[3] if has_extra else refs[2]
        av = a_ref[...]
        if a_fn is not None:
            av = a_fn(av)
        part = _dot(av, b_ref[...], dims)

        def finish(acc):
            if epi is not None:
                acc = epi(acc, e_ref[...])
            o_ref[...] = acc

        if nk == 1:
            finish(part)
        else:
            acc_ref = refs[-1]
            k = pl.program_id(2)

            @pl.when(k == 0)
            def _():
                acc_ref[...] = part

            @pl.when(k > 0)
            def _():
                acc_ref[...] += part

            @pl.when(k == nk - 1)
            def _():
                finish(acc_ref[...])

    in_specs = [a_spec, b_spec] + ([o_spec] if has_extra else [])
    args = (a, b) + ((extra,) if has_extra else ())
    return pl.pallas_call(
        body, name=name, grid=(M // tm, N // tn, nk), in_specs=in_specs, out_specs=o_spec,
        out_shape=jax.ShapeDtypeStruct((M, N), F32),
        scratch_shapes=[pltpu.VMEM((tm, tn), F32)] if nk > 1 else [],
        compiler_params=_cparams("parallel", "parallel", "arbitrary"),
    )(*args)


def _relu2(v):
    r = jnp.maximum(v, 0.0)
    return r * r


def _epi_drelu2(acc, h):
    return acc * (2.0 * jnp.maximum(h, 0.0))


def _epi_add_residual(acc, du):
    return acc + DN_ALPHA * du


def _ln_fwd(x, y, g, b, name):
    T, D = x.shape
    tr = _tile(T, 512)

    def body(x_ref, y_ref, g_ref, b_ref, o_ref):
        u = DN_ALPHA * x_ref[...] + y_ref[...]
        mu = jnp.mean(u, axis=-1, keepdims=True)
        d = u - mu
        var = jnp.mean(d * d, axis=-1, keepdims=True)
        o_ref[...] = d * lax.rsqrt(var + LN_EPS) * g_ref[...] + b_ref[...]

    row = pl.BlockSpec((tr, D), lambda i: (i, 0))
    vec = pl.BlockSpec((1, D), lambda i: (0, 0))
    return pl.pallas_call(
        body, name=name, grid=(T // tr,), in_specs=[row, row, vec, vec], out_specs=row,
        out_shape=jax.ShapeDtypeStruct((T, D), F32), compiler_params=_cparams("parallel"),
    )(x, y, g, b)


def _ln_bwd(x, y, g, dout, name):
    T, D = x.shape
    tr = _tile(T, 512)

    def body(x_ref, y_ref, g_ref, d_ref, du_ref, dg_ref, db_ref):
        i = pl.program_id(0)
        u = DN_ALPHA * x_ref[...] + y_ref[...]
        mu = jnp.mean(u, axis=-1, keepdims=True)
        d = u - mu
        rstd = lax.rsqrt(jnp.mean(d * d, axis=-1, keepdims=True) + LN_EPS)
        xhat = d * rstd
        dout_v = d_ref[...]
        dxh = dout_v * g_ref[...]
        m1 = jnp.mean(dxh, axis=-1, keepdims=True)
        m2 = jnp.mean(dxh * xhat, axis=-1, keepdims=True)
        du_ref[...] = rstd * (dxh - m1 - xhat * m2)
        pg = jnp.sum(dout_v * xhat, axis=0, keepdims=True)
        pb = jnp.sum(dout_v, axis=0, keepdims=True)

        @pl.when(i == 0)
        def _():
            dg_ref[...] = pg
            db_ref[...] = pb

        @pl.when(i > 0)
        def _():
            dg_ref[...] += pg
            db_ref[...] += pb

    row = pl.BlockSpec((tr, D), lambda i: (i, 0))
    vec = pl.BlockSpec((1, D), lambda i: (0, 0))
    return pl.pallas_call(
        body, name=name, grid=(T // tr,), in_specs=[row, row, vec, row], out_specs=[row, vec, vec],
        out_shape=[jax.ShapeDtypeStruct((T, D), F32), jax.ShapeDtypeStruct((1, D), F32),
                   jax.ShapeDtypeStruct((1, D), F32)],
        compiler_params=_cparams("arbitrary"),
    )(x, y, g, dout)


def _loss_head(y, tgt, name):
    T, D = y.shape
    tr = _tile(T, 512)

    def body(y_ref, t_ref, dy_ref, l_ref):
        i = pl.program_id(0)
        e = y_ref[...] - t_ref[...]
        dy_ref[...] = e * (1.0 / D)
        part = jnp.sum(e * e, axis=(0, 1), keepdims=True) * (0.5 / D)

        @pl.when(i == 0)
        def _():
            l_ref[...] = part

        @pl.when(i > 0)
        def _():
            l_ref[...] += part

    row = pl.BlockSpec((tr, D), lambda i: (i, 0))
    one = pl.BlockSpec((1, 1), lambda i: (0, 0))
    return pl.pallas_call(
        body, name=name, grid=(T // tr,), in_specs=[row, row], out_specs=[row, one],
        out_shape=[jax.ShapeDtypeStruct((T, D), F32), jax.ShapeDtypeStruct((1, 1), F32)],
        compiler_params=_cparams("arbitrary"),
    )(y, tgt)


def _swap_half(x):
    n = x.shape[-1]
    lane = lax.broadcasted_iota(jnp.int32, x.shape, 1)
    first = (lane % A_HEAD_DIM) < (A_HEAD_DIM // 2)
    return jnp.where(first, pltpu.roll(x, n - A_HEAD_DIM // 2, axis=1), pltpu.roll(x, A_HEAD_DIM // 2, axis=1))


def _rope(x, cos, sin_signed):
    return x * cos + _swap_half(x) * sin_signed


def _rope_t(dy, cos, sin_signed):
    return dy * cos + _swap_half(dy * sin_signed)


def _rope_tables(T):
    half = A_HEAD_DIM // 2
    inv_freq = ROPE_THETA ** (-jnp.arange(half, dtype=F32) / half)
    ang = jnp.arange(T, dtype=F32)[:, None] * inv_freq[None, :]
    cos = jnp.tile(jnp.cos(ang), (1, 2 * A_Q_HEADS))
    sin = jnp.sin(ang)
    sin_signed = jnp.tile(jnp.concatenate([-sin, sin], axis=1), (1, A_Q_HEADS))
    return cos, sin_signed


def _band_mask(n):
    qi = lax.broadcasted_iota(jnp.int32, (WINDOW, 2 * WINDOW), 0)
    kj = lax.broadcasted_iota(jnp.int32, (WINDOW, 2 * WINDOW), 1)
    return (kj > qi) & (kj <= qi + WINDOW) & ((n > 0) | (kj >= WINDOW))


def _place(v, src_half, dst_half):
    lane = lax.broadcasted_iota(jnp.int32, v.shape, 1)
    if src_half != dst_half:
        v = pltpu.roll(v, A_HEAD_DIM, axis=1)
    keep = (lane >= A_HEAD_DIM) if dst_half else (lane < A_HEAD_DIM)
    return jnp.where(keep, v, 0.0)


def _attn_specs():
    kb, vb = HP_K // LANES, HP_V // LANES
    prev = lambda n: jnp.maximum(n - 1, 0)
    return dict(
        q=pl.BlockSpec((WINDOW, A_Q_W), lambda n: (n, 0)),
        kc=pl.BlockSpec((WINDOW, LANES), lambda n: (n, kb)),
        kp=pl.BlockSpec((WINDOW, LANES), lambda n: (prev(n), kb)),
        vc=pl.BlockSpec((WINDOW, LANES), lambda n: (n, vb)),
        vp=pl.BlockSpec((WINDOW, LANES), lambda n: (prev(n), vb)),
        tq=pl.BlockSpec((WINDOW, A_Q_W), lambda n: (n, 0)),
        tp=pl.BlockSpec((WINDOW, LANES), lambda n: (prev(n), 0)),
        sink=pl.BlockSpec((A_Q_HEADS, LANES), lambda n: (0, 0)),
        row512=pl.BlockSpec((WINDOW, A_Q_W), lambda n: (n, 0)),
        row128=pl.BlockSpec((WINDOW, LANES), lambda n: (n, 0)),
        lse=pl.BlockSpec((WINDOW, A_Q_HEADS), lambda n: (n, 0)),
    )


def _attn_fwd(proj, cos, sin_s, sinks_b, name):
    T = proj.shape[0]
    sp = _attn_specs()

    def body(q_ref, kc_ref, kp_ref, vc_ref, vp_ref, cq_ref, sq_ref, cp_ref, sp_ref, sink_ref, o_ref, l_ref):
        n = pl.program_id(0)
        cq, sq = cq_ref[...], sq_ref[...]
        q = _rope(q_ref[...], cq, sq) * (A_HEAD_DIM ** -0.5)
        kc = _rope(kc_ref[...], cq[:, :LANES], sq[:, :LANES])
        kp = _rope(kp_ref[...], cp_ref[...], sp_ref[...])
        kk = jnp.concatenate([kp, kc], axis=0)
        vv = jnp.concatenate([vp_ref[...], vc_ref[...]], axis=0)
        mask = _band_mask(n)
        lane = lax.broadcasted_iota(jnp.int32, (WINDOW, LANES), 1)
        outs = []
        lane8 = lax.broadcasted_iota(jnp.int32, (WINDOW, A_Q_HEADS), 1)
        lse = jnp.zeros((WINDOW, A_Q_HEADS), F32)
        for pb in range(A_Q_HEADS // 2):
            kvh = pb // 2
            q2 = q[:, pb * LANES:(pb + 1) * LANES]
            halves = []
            for e in range(2):
                hq = 2 * pb + e
                s = _dot(_place(q2, e, kvh), kk, NT)
                s = jnp.where(mask, s, NEG_BIG)
                sk = sink_ref[hq:hq + 1, 0:1]
                m = jnp.maximum(jnp.max(s, axis=-1, keepdims=True), sk)
                p = jnp.exp(s - m)
                den = jnp.sum(p, axis=-1, keepdims=True) + jnp.exp(sk - m)
                o = _dot(p * (1.0 / den), vv, NN)
                halves.append(_place(o, kvh, e))
                lse = jnp.where(lane8 == hq, m + jnp.log(den), lse)
            outs.append(jnp.where(lane < A_HEAD_DIM, halves[0], halves[1]))
        o_ref[...] = jnp.concatenate(outs, axis=1)
        l_ref[...] = lse

    return pl.pallas_call(
        body, name=name, grid=(T // WINDOW,),
        in_specs=[sp["q"], sp["kc"], sp["kp"], sp["vc"], sp["vp"], sp["tq"], sp["tq"], sp["tp"], sp["tp"], sp["sink"]],
        out_specs=[sp["row512"], sp["lse"]],
        out_shape=[jax.ShapeDtypeStruct((T, A_Q_W), F32), jax.ShapeDtypeStruct((T, A_Q_HEADS), F32)],
        compiler_params=_cparams("parallel"),
    )(proj, proj, proj, proj, proj, cos, sin_s, cos, sin_s, sinks_b)


def _attn_bwd(proj, cos, sin_s, sinks_b, o, lse, dmix, name):
    T = proj.shape[0]
    sp = _attn_specs()

    def body(q_ref, kc_ref, kp_ref, vc_ref, vp_ref, cq_ref, sq_ref, cp_ref, sp_ref, sink_ref, o_ref, l_ref, do_ref,
             dq_ref, dkc_ref, dkp_ref, dvc_ref, dvp_ref, dsink_ref):
        n = pl.program_id(0)
        cq, sq = cq_ref[...], sq_ref[...]
        cp, sps = cp_ref[...], sp_ref[...]
        q = _rope(q_ref[...], cq, sq) * (A_HEAD_DIM ** -0.5)
        kc = _rope(kc_ref[...], cq[:, :LANES], sq[:, :LANES])
        kp = _rope(kp_ref[...], cp, sps)
        kk = jnp.concatenate([kp, kc], axis=0)
        vv = jnp.concatenate([vp_ref[...], vc_ref[...]], axis=0)
        mask = _band_mask(n)
        lane = lax.broadcasted_iota(jnp.int32, (WINDOW, LANES), 1)
        do_all, o_all, l_all = do_ref[...], o_ref[...], l_ref[...]
        dkk = jnp.zeros((2 * WINDOW, LANES), F32)
        dvv = jnp.zeros((2 * WINDOW, LANES), F32)
        dqs = []
        lane8 = lax.broadcasted_iota(jnp.int32, (WINDOW, A_Q_HEADS), 1)
        head8 = lax.broadcasted_iota(jnp.int32, (1, A_Q_HEADS), 1)
        dsk = jnp.zeros((1, A_Q_HEADS), F32)
        for pb in range(A_Q_HEADS // 2):
            kvh = pb // 2
            q2 = q[:, pb * LANES:(pb + 1) * LANES]
            do2 = do_all[:, pb * LANES:(pb + 1) * LANES]
            prod = do2 * o_all[:, pb * LANES:(pb + 1) * LANES]
            halves = []
            for e in range(2):
                hq = 2 * pb + e
                in_half = (lane >= A_HEAD_DIM) if e else (lane < A_HEAD_DIM)
                delta = jnp.sum(jnp.where(in_half, prod, 0.0), axis=-1, keepdims=True)
                qe = _place(q2, e, kvh)
                doe = _place(do2, e, kvh)
                lh = jnp.sum(jnp.where(lane8 == hq, l_all, 0.0), axis=-1, keepdims=True)
                s = _dot(qe, kk, NT)
                p = jnp.where(mask, jnp.exp(jnp.where(mask, s, NEG_BIG) - lh), 0.0)
                dvv = dvv + _dot(p, doe, TN)
                dp = _dot(doe, vv, NT)
                ds = p * (dp - delta)
                dkk = dkk + _dot(ds, qe, TN)
                halves.append(_place(_dot(ds, kk, NN), kvh, e))
                p_sink = jnp.exp(sink_ref[hq:hq + 1, 0:1] - lh)
                dsk = jnp.where(head8 == hq, -jnp.sum(p_sink * delta, axis=(0, 1), keepdims=True), dsk)
            dqs.append(jnp.where(lane < A_HEAD_DIM, halves[0], halves[1]))
        dq = jnp.concatenate(dqs, axis=1) * (A_HEAD_DIM ** -0.5)
        dq_ref[...] = _rope_t(dq, cq, sq)
        dkp_ref[...] = _rope_t(dkk[:WINDOW], cp, sps)
        dkc_ref[...] = _rope_t(dkk[WINDOW:], cq[:, :LANES], sq[:, :LANES])
        dvp_ref[...] = dvv[:WINDOW]
        dvc_ref[...] = dvv[WINDOW:]

        @pl.when(n == 0)
        def _():
            dsink_ref[...] = dsk

        @pl.when(n > 0)
        def _():
            dsink_ref[...] += dsk

    return pl.pallas_call(
        body, name=name, grid=(T // WINDOW,),
        in_specs=[sp["q"], sp["kc"], sp["kp"], sp["vc"], sp["vp"], sp["tq"], sp["tq"], sp["tp"], sp["tp"], sp["sink"],
                  sp["row512"], sp["lse"], sp["row512"]],
        out_specs=[sp["row512"], sp["row128"], sp["row128"], sp["row128"], sp["row128"],
                   pl.BlockSpec((1, A_Q_HEADS), lambda n: (0, 0))],
        out_shape=[jax.ShapeDtypeStruct((T, A_Q_W), F32)] + [jax.ShapeDtypeStruct((T, LANES), F32)] * 4
        + [jax.ShapeDtypeStruct((1, A_Q_HEADS), F32)],
        compiler_params=_cparams("arbitrary"),
    )(proj, proj, proj, proj, proj, cos, sin_s, cos, sin_s, sinks_b, o, lse, dmix)


def _shift_down(x, prev8, k):
    if k == 0:
        return x
    row = lax.broadcasted_iota(jnp.int32, prev8.shape, 0)
    r = pltpu.roll(x, k, axis=0)
    top = jnp.where(row < k, pltpu.roll(prev8, k, axis=0), r[:SUBLANES])
    return jnp.concatenate([top, r[SUBLANES:]], axis=0)


def _shift_up(x, next8, k):
    if k == 0:
        return x
    R = x.shape[0]
    row = lax.broadcasted_iota(jnp.int32, next8.shape, 0)
    r = pltpu.roll(x, R - k, axis=0)
    bot = jnp.where(row >= SUBLANES - k, pltpu.roll(next8, SUBLANES - k, axis=0), r[R - SUBLANES:])
    return jnp.concatenate([r[:R - SUBLANES], bot], axis=0)


def _conv(x, prev8, w):
    y = x * w[CONV_K - 1:CONV_K]
    for j in range(CONV_K - 1):
        y = y + _shift_down(x, prev8, CONV_K - 1 - j) * w[j:j + 1]
    return y


def _conv_bwd(x, prev8, w, dy, next8_dy):
    dx = dy * w[CONV_K - 1:CONV_K]
    dws = []
    for j in range(CONV_K - 1):
        k = CONV_K - 1 - j
        dx = dx + _shift_up(dy, next8_dy, k) * w[j:j + 1]
        dws.append(jnp.sum(dy * _shift_down(x, prev8, k), axis=0, keepdims=True))
    dws.append(jnp.sum(dy * x, axis=0, keepdims=True))
    return dx, dws


def _dnconv_fwd(proj, conv_w, name):
    T = proj.shape[0]
    R = _tile(T, 512)
    cb0 = HP_QKVB // A_Q_W

    def body(x_ref, w_ref, o_ref, prev_ref):
        i = pl.program_id(1)

        @pl.when(i == 0)
        def _():
            prev_ref[...] = jnp.zeros_like(prev_ref)

        x = x_ref[...]
        o_ref[...] = _silu(_conv(x, prev_ref[...], w_ref[...]))
        prev_ref[...] = x[R - SUBLANES:]

    return pl.pallas_call(
        body, name=name, grid=(3, T // R),
        in_specs=[pl.BlockSpec((R, B_W), lambda j, i: (i, cb0 + j)), pl.BlockSpec((CONV_K, B_W), lambda j, i: (0, j))],
        out_specs=pl.BlockSpec((R, B_W), lambda j, i: (i, j)),
        out_shape=jax.ShapeDtypeStruct((T, 3 * B_W), F32),
        scratch_shapes=[pltpu.VMEM((SUBLANES, B_W), F32)],
        compiler_params=_cparams("parallel", "arbitrary"),
    )(proj, conv_w)


def _dnconv_bwd(proj, conv_w, dy, name):
    T = proj.shape[0]
    R = _tile(T, 512)
    nb = T // R
    cb0 = HP_QKVB // A_Q_W
    r8 = R // SUBLANES

    def body(x_ref, xp_ref, w_ref, dy_ref, dx_ref, dw_ref, next_ref):
        i = pl.program_id(1)
        blk = nb - 1 - i

        @pl.when(i == 0)
        def _():
            next_ref[...] = jnp.zeros_like(next_ref)

        x = x_ref[...]
        prev8 = jnp.where(blk > 0, xp_ref[...], 0.0)
        w = w_ref[...]
        dpre = dy_ref[...] * _dsilu(_conv(x, prev8, w))
        dx, dw = _conv_bwd(x, prev8, w, dpre, next_ref[...])
        dx_ref[...] = dx
        next_ref[...] = dpre[:SUBLANES]

        @pl.when(i == 0)
        def _():
            for j in range(CONV_K):
                dw_ref[j:j + 1, :] = dw[j]

        @pl.when(i > 0)
        def _():
            for j in range(CONV_K):
                dw_ref[j:j + 1, :] += dw[j]

    return pl.pallas_call(
        body, name=name, grid=(3, nb),
        in_specs=[pl.BlockSpec((R, B_W), lambda j, i: (nb - 1 - i, cb0 + j)),
                  pl.BlockSpec((SUBLANES, B_W), lambda j, i: (jnp.maximum((nb - 1 - i) * r8 - 1, 0), cb0 + j)),
                  pl.BlockSpec((CONV_K, B_W), lambda j, i: (0, j)),
                  pl.BlockSpec((R, B_W), lambda j, i: (nb - 1 - i, j))],
        out_specs=[pl.BlockSpec((R, B_W), lambda j, i: (nb - 1 - i, j)),
                   pl.BlockSpec((CONV_K, B_W), lambda j, i: (0, j))],
        out_shape=[jax.ShapeDtypeStruct((T, 3 * B_W), F32), jax.ShapeDtypeStruct((CONV_K, 3 * B_W), F32)],
        scratch_shapes=[pltpu.VMEM((SUBLANES, B_W), F32)],
        compiler_params=_cparams("parallel", "arbitrary"),
    )(proj, proj, conv_w, dy)


DK_SCALE = B_HEAD_DIM ** -0.5


def _tri_inv(a):
    C = a.shape[0]
    ri = lax.broadcasted_iota(jnp.int32, (C, C), 0)
    ci = lax.broadcasted_iota(jnp.int32, (C, C), 1)
    x = jnp.where(ri == ci, 1.0, 0.0) - a
    p = _dotf(a, a)
    span = 2
    while span < C:
        x = x + _dotf(x, p)
        span *= 2
        if span < C:
            p = _dotf(p, p)
    return x


def _dn_chunk(qc, kc, v, gcol, grow, bcol, s0):
    C = B_CHUNK
    ri = lax.broadcasted_iota(jnp.int32, (C, C), 0)
    ci = lax.broadcasted_iota(jnp.int32, (C, C), 1)
    incl, strict = ri >= ci, ri > ci
    rq = lax.rsqrt(jnp.sum(qc * qc, axis=-1, keepdims=True) + NORM_EPS)
    rk = lax.rsqrt(jnp.sum(kc * kc, axis=-1, keepdims=True) + NORM_EPS)
    qn = qc * rq
    q = qn * DK_SCALE
    k = kc * rk
    gc_col = jnp.sum(jnp.where(incl, grow, 0.0), axis=1, keepdims=True)
    gc_row = jnp.sum(jnp.where(ri <= ci, gcol, 0.0), axis=0, keepdims=True)
    gl = jnp.sum(gcol, axis=0, keepdims=True)
    dincl = jnp.where(incl, jnp.exp(jnp.where(incl, gc_col - gc_row, 0.0)), 0.0)
    dstrict = jnp.where(strict, dincl, 0.0)
    eg = jnp.exp(gc_col)
    ekt = jnp.exp(gl - gc_col)
    egl = jnp.exp(gl)
    kb = k * bcol
    vb = v * bcol
    kbg = kb * eg
    a = _dot(kb, k, NT) * dstrict
    tm = _tri_inv(a)
    u = _dot(tm, vb)
    w = _dot(tm, kbg)
    vn = u - _dot(w, s0)
    qk = _dot(q, k, NT) * dincl
    qg = q * eg
    kt = k * ekt
    o = _dot(qg, s0) + _dot(qk, vn)
    s1 = s0 * egl + _dot(kt, vn, TN)
    return dict(rq=rq, rk=rk, qn=qn, q=q, k=k, dincl=dincl, dstrict=dstrict, eg=eg, ekt=ekt, egl=egl, kb=kb, vb=vb,
                kbg=kbg, a=a, tm=tm, w=w, vn=vn, qk=qk, qg=qg, kt=kt, o=o, s1=s1, incl=incl, ri=ri, ci=ci)


def _dn_specs(N, rev):
    ix = (lambda n: N - 1 - n) if rev else (lambda n: n)
    zb = HP_Z // LANES
    return dict(
        q=pl.BlockSpec((B_CHUNK, LANES), lambda h, n: (ix(n), h)),
        k=pl.BlockSpec((B_CHUNK, LANES), lambda h, n: (ix(n), B_HEADS + h)),
        v=pl.BlockSpec((B_CHUNK, LANES), lambda h, n: (ix(n), 2 * B_HEADS + h)),
        z=pl.BlockSpec((B_CHUNK, LANES), lambda h, n: (ix(n), zb + h)),
        nw=pl.BlockSpec((1, LANES), lambda h, n: (0, 0)),
        col=pl.BlockSpec((None, None, B_CHUNK, 1), lambda h, n: (h, ix(n), 0, 0)),
        row=pl.BlockSpec((None, None, 1, B_CHUNK), lambda h, n: (h, ix(n), 0, 0)),
        head=pl.BlockSpec((B_CHUNK, LANES), lambda h, n: (ix(n), h)),
        state=pl.BlockSpec((None, None, B_HEAD_DIM, B_HEAD_DIM), lambda h, n: (h, ix(n), 0, 0)),
    )


def _dn_fwd(qkvc, proj, norm_w, gcol, grow, bcol, name):
    T = qkvc.shape[0]
    N = T // B_CHUNK
    sp = _dn_specs(N, False)

    def body(q_ref, k_ref, v_ref, z_ref, nw_ref, gc_ref, gr_ref, bc_ref, o_ref, st_ref, s_ref):
        n = pl.program_id(1)

        @pl.when(n == 0)
        def _():
            s_ref[...] = jnp.zeros_like(s_ref)

        s0 = s_ref[...]
        st_ref[...] = s0
        f = _dn_chunk(q_ref[...], k_ref[...], v_ref[...], gc_ref[...], gr_ref[...], bc_ref[...], s0)
        o = f["o"]
        r = lax.rsqrt(jnp.mean(o * o, axis=-1, keepdims=True) + NORM_EPS)
        o_ref[...] = o * r * nw_ref[...] * _silu(z_ref[...])
        s_ref[...] = f["s1"]

    return pl.pallas_call(
        body, name=name, grid=(B_HEADS, N),
        in_specs=[sp["q"], sp["k"], sp["v"], sp["z"], sp["nw"], sp["col"], sp["row"], sp["col"]],
        out_specs=[sp["head"], sp["state"]],
        out_shape=[jax.ShapeDtypeStruct((T, B_W), F32),
                   jax.ShapeDtypeStruct((B_HEADS, N, B_HEAD_DIM, B_HEAD_DIM), F32)],
        scratch_shapes=[pltpu.VMEM((B_HEAD_DIM, B_HEAD_DIM), F32)],
        compiler_params=_cparams("parallel", "arbitrary"),
    )(qkvc, qkvc, qkvc, proj, norm_w, gcol, grow, bcol)


def _dn_bwd(qkvc, proj, norm_w, gcol, grow, bcol, states, dmix, name):
    T = qkvc.shape[0]
    N = T // B_CHUNK
    sp = _dn_specs(N, True)
    C = B_CHUNK

    def body(q_ref, k_ref, v_ref, z_ref, nw_ref, gc_ref, gr_ref, bc_ref, st_ref, dob_ref,
             dq_ref, dk_ref, dv_ref, dz_ref, dg_ref, db_ref, dnw_ref, ds_ref):
        n = pl.program_id(1)

        @pl.when(n == 0)
        def _():
            ds_ref[...] = jnp.zeros_like(ds_ref)

        s0 = st_ref[...]
        ds1 = ds_ref[...]
        v, z, nw, bcol_v = v_ref[...], z_ref[...], nw_ref[...], bc_ref[...]
        f = _dn_chunk(q_ref[...], k_ref[...], v, gc_ref[...], gr_ref[...], bcol_v, s0)
        o, q, k, qn = f["o"], f["q"], f["k"], f["qn"]
        eg, ekt, egl = f["eg"], f["ekt"], f["egl"]
        tm, w, vn, kb, vb, kbg = f["tm"], f["w"], f["vn"], f["kb"], f["vb"], f["kbg"]
        qg, kt, qk, a = f["qg"], f["kt"], f["qk"], f["a"]
        ri, ci = f["ri"], f["ci"]

        dob_v = dob_ref[...]
        r = lax.rsqrt(jnp.mean(o * o, axis=-1, keepdims=True) + NORM_EPS)
        sz = _silu(z)
        on = o * r
        dnw = jnp.sum(dob_v * sz * on, axis=0, keepdims=True)
        dz_ref[...] = dob_v * on * nw * _dsilu(z)
        d_on = dob_v * sz * nw
        do = r * (d_on - on * jnp.mean(d_on * on, axis=-1, keepdims=True))

        dvn = _dot(qk, do, TN) + _dot(kt, ds1)
        dqk = _dot(do, vn, NT)
        dqg = _dot(do, s0, NT)
        ds_ref[...] = _dot(qg, do, TN) + egl * ds1 - _dot(w, dvn, TN)
        dgl = jnp.sum(s0 * ds1, axis=(0, 1), keepdims=True) * egl
        dkt = _dot(vn, ds1, NT)
        dw = -_dot(dvn, s0, NT)
        dq = dqg * eg
        dgc = jnp.sum(dqg * qg, axis=-1, keepdims=True)
        dk = dkt * ekt
        t_kt = jnp.sum(dkt * kt, axis=-1, keepdims=True)
        dgl = dgl + jnp.sum(t_kt, axis=0, keepdims=True)
        dgc = dgc - t_kt
        dqkr = dqk * f["dincl"]
        dq = dq + _dot(dqkr, k)
        dk = dk + _dot(dqkr, q, TN)
        e_qk = dqk * qk
        dgc = dgc + jnp.sum(e_qk, axis=-1, keepdims=True)
        dgc_row = -jnp.sum(e_qk, axis=0, keepdims=True)
        dtm = _dot(dvn, vb, NT) + _dot(dw, kbg, NT)
        dvb = _dot(tm, dvn, TN)
        dkbg = _dot(tm, dw, TN)
        dkb = dkbg * eg
        dgc = dgc + jnp.sum(dkbg * kbg, axis=-1, keepdims=True)
        da = -_dotf(tm, _dotf(dtm, tm, NT), TN)
        dkk = da * f["dstrict"]
        e_a = da * a
        dgc = dgc + jnp.sum(e_a, axis=-1, keepdims=True)
        dgc_row = dgc_row - jnp.sum(e_a, axis=0, keepdims=True)
        dkb = dkb + _dot(dkk, k)
        dk = dk + _dot(dkk, kb, TN)
        dk = dk + dkb * bcol_v
        db_ref[...] = jnp.sum(dkb * k, axis=-1, keepdims=True) + jnp.sum(dvb * v, axis=-1, keepdims=True)
        dv_ref[...] = dvb * bcol_v
        dgc_row = dgc_row + jnp.sum(jnp.where(ri == ci, dgc, 0.0), axis=0, keepdims=True)
        dg_ref[...] = jnp.sum(jnp.where(ci >= ri, dgc_row, 0.0), axis=1, keepdims=True) + dgl
        dqs = dq * DK_SCALE
        dq_ref[...] = f["rq"] * (dqs - qn * jnp.sum(dqs * qn, axis=-1, keepdims=True))
        dk_ref[...] = f["rk"] * (dk - k * jnp.sum(dk * k, axis=-1, keepdims=True))

        @pl.when(n == 0)
        def _():
            dnw_ref[...] = dnw

        @pl.when(n > 0)
        def _():
            dnw_ref[...] += dnw

    rev = lambda n: N - 1 - n
    return pl.pallas_call(
        body, name=name, grid=(B_HEADS, N),
        in_specs=[sp["q"], sp["k"], sp["v"], sp["z"], sp["nw"], sp["col"], sp["row"], sp["col"], sp["state"],
                  pl.BlockSpec((C, LANES), lambda h, n: (rev(n), A_Q_W // LANES + h))],
        out_specs=[sp["head"], sp["head"], sp["head"], sp["head"], sp["col"], sp["col"],
                   pl.BlockSpec((None, 1, LANES), lambda h, n: (h, 0, 0))],
        out_shape=[jax.ShapeDtypeStruct((T, B_W), F32)] * 4
        + [jax.ShapeDtypeStruct((B_HEADS, N, C, 1), F32), jax.ShapeDtypeStruct((B_HEADS, N, C, 1), F32),
           jax.ShapeDtypeStruct((B_HEADS, 1, LANES), F32)],
        scratch_shapes=[pltpu.VMEM((B_HEAD_DIM, B_HEAD_DIM), F32)],
        compiler_params=_cparams("parallel", "arbitrary"),
    )(qkvc, qkvc, qkvc, proj, norm_w, gcol, grow, bcol, states, dmix)


def _lru_gates(xc, wa_ref, wx_ref, ba, bx, sp):
    pre_r, pre_i = [], []
    for hb in range(LRU_BLOCKS):
        xb = xc[:, hb * LRU_BLOCK_W:(hb + 1) * LRU_BLOCK_W]
        pre_r.append(_dot(xb, wa_ref[hb]))
        pre_i.append(_dot(xb, wx_ref[hb]))
    r = _sigmoid(jnp.concatenate(pre_r, axis=1) + ba)
    i = _sigmoid(jnp.concatenate(pre_i, axis=1) + bx)
    la = -LRU_C * r * sp
    a = jnp.exp(la)
    th = jnp.tanh(la)
    s = jnp.sqrt(-2.0 * th / (1.0 - th))
    return r, i, a, s


def _scan_down(a, b):
    R = a.shape[0]
    row = lax.broadcasted_iota(jnp.int32, a.shape, 0)
    d = 1
    while d < R:
        ok = row >= d
        b = a * jnp.where(ok, pltpu.roll(b, d, axis=0), 0.0) + b
        a = a * jnp.where(ok, pltpu.roll(a, d, axis=0), 1.0)
        d *= 2
    return a, b


def _scan_up(a, b):
    R = a.shape[0]
    row = lax.broadcasted_iota(jnp.int32, a.shape, 0)
    d = 1
    while d < R:
        ok = row < R - d
        b = a * jnp.where(ok, pltpu.roll(b, R - d, axis=0), 0.0) + b
        a = a * jnp.where(ok, pltpu.roll(a, R - d, axis=0), 1.0)
        d *= 2
    return b


def _rglru_fwd(proj, conv_w, conv_b, wa, wx, ba, bx, sp, name):
    T = proj.shape[0]
    R = _tile(T, 256)
    W = D_MODEL

    def body(p_ref, cw_ref, cb_ref, wa_ref, wx_ref, ba_ref, bx_ref, sp_ref, hg_ref, h_ref, prev_ref, hc_ref):
        i = pl.program_id(0)

        @pl.when(i == 0)
        def _():
            prev_ref[...] = jnp.zeros_like(prev_ref)
            hc_ref[...] = jnp.zeros_like(hc_ref)

        xr = p_ref[:, :W]
        gate = p_ref[:, W:]
        xc = _conv(xr, prev_ref[...], cw_ref[...]) + cb_ref[...]
        prev_ref[...] = xr[R - SUBLANES:]
        r, ig, a, s = _lru_gates(xc, wa_ref, wx_ref, ba_ref[...], bx_ref[...], sp_ref[...])
        pa, hb = _scan_down(a, s * ig * xc)
        h = hb + pa * hc_ref[SUBLANES - 1:SUBLANES, :]
        h_ref[...] = h
        hg_ref[...] = h * _gelu(gate)
        hc_ref[...] = h[R - SUBLANES:]

    vec = pl.BlockSpec((1, W), lambda i: (0, 0))
    wsp = pl.BlockSpec((LRU_BLOCKS, LRU_BLOCK_W, LRU_BLOCK_W), lambda i: (0, 0, 0))
    row = pl.BlockSpec((R, W), lambda i: (i, 0))
    return pl.pallas_call(
        body, name=name, grid=(T // R,),
        in_specs=[pl.BlockSpec((R, 2 * W), lambda i: (i, 0)), pl.BlockSpec((CONV_K, W), lambda i: (0, 0)),
                  vec, wsp, wsp, vec, vec, vec],
        out_specs=[row, row],
        out_shape=[jax.ShapeDtypeStruct((T, W), F32)] * 2,
        scratch_shapes=[pltpu.VMEM((SUBLANES, W), F32), pltpu.VMEM((SUBLANES, W), F32)],
        compiler_params=_cparams("arbitrary"),
    )(proj, conv_w, conv_b, wa, wx, ba, bx, sp)


def _rglru_bwd(proj, conv_w, conv_b, wa, wx, ba, bx, sp, h, dhg, name):
    T = proj.shape[0]
    R = _tile(T, 256)
    nb = T // R
    r8 = R // SUBLANES
    W = D_MODEL

    def body(p_ref, pp_ref, cw_ref, cb_ref, wa_ref, wx_ref, ba_ref, bx_ref, sp_ref, h_ref, hp_ref, dhg_ref,
             dp_ref, dcw_ref, dcb_ref, dwa_ref, dwx_ref, dba_ref, dbx_ref, dsp_ref, lam_ref, nxt_ref):
        step = pl.program_id(0)
        blk = nb - 1 - step

        @pl.when(step == 0)
        def _():
            lam_ref[...] = jnp.zeros_like(lam_ref)
            nxt_ref[...] = jnp.zeros_like(nxt_ref)

        xr = p_ref[:, :W]
        gate = p_ref[:, W:]
        first = blk > 0
        prev8 = jnp.where(first, pp_ref[:, :W], 0.0)
        hprev8 = jnp.where(first, hp_ref[...], 0.0)
        cw = cw_ref[...]
        spv = sp_ref[...]
        xc = _conv(xr, prev8, cw) + cb_ref[...]
        r, ig, a, s = _lru_gates(xc, wa_ref, wx_ref, ba_ref[...], bx_ref[...], spv)
        hv = h_ref[...]
        dhg_v = dhg_ref[...]
        dgate = dhg_v * hv * _dgelu(gate)
        dh = dhg_v * _gelu(gate)
        row = lax.broadcasted_iota(jnp.int32, (R, W), 0)
        last = row == R - 1
        a_up = jnp.where(last, 0.0, pltpu.roll(a, R - 1, axis=0))
        lam = _scan_up(a_up, dh + jnp.where(last, lam_ref[0:1, :], 0.0))
        lam_ref[...] = (a * lam)[:SUBLANES]
        h_dn = _shift_down(hv, hprev8, 1)
        da = lam * h_dn
        bx_in = ig * xc
        dsv = lam * bx_in
        dig = lam * s * xc
        dxc = lam * s * ig
        dla = da * a - dsv * (a * a) / s
        dr = dla * (-LRU_C) * spv
        dsp = jnp.sum(dla * (-LRU_C) * r, axis=0, keepdims=True)
        dpr = dr * r * (1.0 - r)
        dpi = dig * ig * (1.0 - ig)
        dxc_parts, dwa_parts, dwx_parts = [], [], []
        for hb in range(LRU_BLOCKS):
            sl = slice(hb * LRU_BLOCK_W, (hb + 1) * LRU_BLOCK_W)
            xb, gr, gi = xc[:, sl], dpr[:, sl], dpi[:, sl]
            dxc_parts.append(_dot(gr, wa_ref[hb], NT) + _dot(gi, wx_ref[hb], NT))
            dwa_parts.append(_dot(xb, gr, TN))
            dwx_parts.append(_dot(xb, gi, TN))
        dxc = dxc + jnp.concatenate(dxc_parts, axis=1)
        dxr, dcw = _conv_bwd(xr, prev8, cw, dxc, nxt_ref[...])
        nxt_ref[...] = dxc[:SUBLANES]
        dp_ref[:, :W] = dxr
        dp_ref[:, W:] = dgate
        dcb = jnp.sum(dxc, axis=0, keepdims=True)
        dba = jnp.sum(dpr, axis=0, keepdims=True)
        dbx = jnp.sum(dpi, axis=0, keepdims=True)

        @pl.when(step == 0)
        def _():
            for j in range(CONV_K):
                dcw_ref[j:j + 1, :] = dcw[j]
            dcb_ref[...] = dcb
            dba_ref[...] = dba
            dbx_ref[...] = dbx
            dsp_ref[...] = dsp
            for hb in range(LRU_BLOCKS):
                dwa_ref[hb] = dwa_parts[hb]
                dwx_ref[hb] = dwx_parts[hb]

        @pl.when(step > 0)
        def _():
            for j in range(CONV_K):
                dcw_ref[j:j + 1, :] += dcw[j]
            dcb_ref[...] += dcb
            dba_ref[...] += dba
            dbx_ref[...] += dbx
            dsp_ref[...] += dsp
            for hb in range(LRU_BLOCKS):
                dwa_ref[hb] += dwa_parts[hb]
                dwx_ref[hb] += dwx_parts[hb]

    rv = lambda i: nb - 1 - i
    before = lambda i: jnp.maximum((nb - 1 - i) * r8 - 1, 0)
    vec = pl.BlockSpec((1, W), lambda i: (0, 0))
    cws = pl.BlockSpec((CONV_K, W), lambda i: (0, 0))
    wsp = pl.BlockSpec((LRU_BLOCKS, LRU_BLOCK_W, LRU_BLOCK_W), lambda i: (0, 0, 0))
    row = pl.BlockSpec((R, W), lambda i: (rv(i), 0))
    wshape = jax.ShapeDtypeStruct((LRU_BLOCKS, LRU_BLOCK_W, LRU_BLOCK_W), F32)
    vshape = jax.ShapeDtypeStruct((1, W), F32)
    return pl.pallas_call(
        body, name=name, grid=(nb,),
        in_specs=[pl.BlockSpec((R, 2 * W), lambda i: (rv(i), 0)), pl.BlockSpec((SUBLANES, 2 * W), lambda i: (before(i), 0)),
                  cws, vec, wsp, wsp, vec, vec, vec, row, pl.BlockSpec((SUBLANES, W), lambda i: (before(i), 0)), row],
        out_specs=[pl.BlockSpec((R, 2 * W), lambda i: (rv(i), 0)), cws, vec, wsp, wsp, vec, vec, vec],
        out_shape=[jax.ShapeDtypeStruct((T, 2 * W), F32), jax.ShapeDtypeStruct((CONV_K, W), F32), vshape,
                   wshape, wshape, vshape, vshape, vshape],
        scratch_shapes=[pltpu.VMEM((SUBLANES, W), F32), pltpu.VMEM((SUBLANES, W), F32)],
        compiler_params=_cparams("arbitrary"),
    )(proj, proj, conv_w, conv_b, wa, wx, ba, bx, sp, h, h, dhg)


MESH = pl.DeviceIdType.MESH
ANY = pl.BlockSpec(memory_space=pl.ANY)


def _position():
    x, y, c = lax.axis_index("x"), lax.axis_index("y"), lax.axis_index("c")
    other_chips = [(1 - x, y), (x, 1 - y), (1 - x, 1 - y)]
    return x, y, c, other_chips


def _all_gather_weights(shards, name):
    n = len(shards)

    def body(*refs):
        ins, outs = refs[:n], refs[n:2 * n]
        send_sems, recv_sems, local_sems = refs[2 * n:]
        x, y, c, chips = _position()
        me = 2 * x + y
        sibling = (x, y, 1 - c)

        def rcopy(t, k, src, dst, to):
            return pltpu.make_async_remote_copy(src_ref=src, dst_ref=dst, send_sem=send_sems.at[t, k],
                                                recv_sem=recv_sems.at[t, k], device_id=to, device_id_type=MESH)

        local = [pltpu.make_async_copy(ins[t], outs[t].at[me], local_sems.at[t]) for t in range(n)]
        for cp in local:
            cp.start()
        started = []
        for t in range(n):
            for j, (cx, cy) in enumerate(chips):
                cp = rcopy(t, j, ins[t].at[c], outs[t].at[me, c], (cx, cy, c))
                cp.start()
                started.append(cp)
        for t in range(n):
            for j, (cx, cy) in enumerate(chips):
                blk = outs[t].at[2 * cx + cy, c]
                rcopy(t, j, blk, blk, (cx, cy, c)).wait_recv()
                cp = rcopy(t, 3 + j, blk, blk, sibling)
                cp.start()
                started.append(cp)
        for t in range(n):
            for j, (cx, cy) in enumerate(chips):
                blk = outs[t].at[2 * cx + cy, 1 - c]
                rcopy(t, 3 + j, blk, blk, sibling).wait_recv()
        for cp in started:
            cp.wait_send()
        for cp in local:
            cp.wait()

    return pl.pallas_call(
        body, name=name, in_specs=[ANY] * n, out_specs=[ANY] * n,
        out_shape=[jax.ShapeDtypeStruct((N_CHIPS,) + s.shape, s.dtype) for s in shards],
        scratch_shapes=[pltpu.SemaphoreType.DMA((n, 6)), pltpu.SemaphoreType.DMA((n, 6)), pltpu.SemaphoreType.DMA((n,))],
    )(*shards)


def _rs_to_sibling(grads, name):
    n = len(grads)

    def body(*refs):
        ins, outs = refs[:n], refs[n:2 * n]
        send_sems, recv_sems = refs[2 * n:]
        x, y, c, _ = _position()
        cps = [pltpu.make_async_remote_copy(src_ref=ins[t].at[:, 1 - c], dst_ref=outs[t], send_sem=send_sems.at[t],
                                            recv_sem=recv_sems.at[t], device_id=(x, y, 1 - c), device_id_type=MESH)
               for t in range(n)]
        for cp in cps:
            cp.start()
        for cp in cps:
            cp.wait()

    return pl.pallas_call(
        body, name=name, in_specs=[ANY] * n, out_specs=[ANY] * n,
        out_shape=[jax.ShapeDtypeStruct((N_CHIPS,) + g.shape[2:], g.dtype) for g in grads],
        scratch_shapes=[pltpu.SemaphoreType.DMA((n,)), pltpu.SemaphoreType.DMA((n,))],
    )(*grads)


def _rs_across_chips(parts, name):
    n = len(parts)

    def body(*refs):
        ins, outs = refs[:n], refs[n:2 * n]
        send_sems, recv_sems, local_sems = refs[2 * n:]
        x, y, c, chips = _position()
        me = 2 * x + y
        local = [pltpu.make_async_copy(ins[t].at[me], outs[t].at[me], local_sems.at[t]) for t in range(n)]
        for cp in local:
            cp.start()
        cps = []
        for t in range(n):
            for j, (cx, cy) in enumerate(chips):
                cps.append(pltpu.make_async_remote_copy(
                    src_ref=ins[t].at[2 * cx + cy], dst_ref=outs[t].at[me], send_sem=send_sems.at[t, j],
                    recv_sem=recv_sems.at[t, j], device_id=(cx, cy, c), device_id_type=MESH))
        for cp in cps:
            cp.start()
        for t in range(n):
            for j, (cx, cy) in enumerate(chips):
                blk = outs[t].at[2 * cx + cy]
                pltpu.make_async_remote_copy(src_ref=blk, dst_ref=blk, send_sem=send_sems.at[t, j],
                                             recv_sem=recv_sems.at[t, j], device_id=(cx, cy, c),
                                             device_id_type=MESH).wait_recv()
        for cp in cps:
            cp.wait_send()
        for cp in local:
            cp.wait()

    return pl.pallas_call(
        body, name=name, in_specs=[ANY] * n, out_specs=[ANY] * n,
        out_shape=[jax.ShapeDtypeStruct(p.shape, p.dtype) for p in parts],
        scratch_shapes=[pltpu.SemaphoreType.DMA((n, 3)), pltpu.SemaphoreType.DMA((n, 3)), pltpu.SemaphoreType.DMA((n,))],
    )(*parts)


def _rs_join_halves(halves, name):
    n = len(halves)

    def body(*refs):
        ins, outs = refs[:n], refs[n:2 * n]
        send_sems, recv_sems, local_sems = refs[2 * n:]
        x, y, c, _ = _position()
        local = [pltpu.make_async_copy(ins[t], outs[t].at[c], local_sems.at[t]) for t in range(n)]
        cps = [pltpu.make_async_remote_copy(src_ref=ins[t], dst_ref=outs[t].at[c], send_sem=send_sems.at[t],
                                            recv_sem=recv_sems.at[t], device_id=(x, y, 1 - c), device_id_type=MESH)
               for t in range(n)]
        for cp in local + cps:
            cp.start()
        for t in range(n):
            blk = outs[t].at[1 - c]
            pltpu.make_async_remote_copy(src_ref=blk, dst_ref=blk, send_sem=send_sems.at[t], recv_sem=recv_sems.at[t],
                                         device_id=(x, y, 1 - c), device_id_type=MESH).wait_recv()
        for cp in cps:
            cp.wait_send()
        for cp in local:
            cp.wait()

    return pl.pallas_call(
        body, name=name, in_specs=[ANY] * n, out_specs=[ANY] * n,
        out_shape=[jax.ShapeDtypeStruct((2,) + h.shape, h.dtype) for h in halves],
        scratch_shapes=[pltpu.SemaphoreType.DMA((n,)), pltpu.SemaphoreType.DMA((n,)), pltpu.SemaphoreType.DMA((n,))],
    )(*halves)


def _all_gather_small(block, name):
    m_per, n = block.shape

    def body(x_ref, out_ref, send_sems, recv_sems, local_sem):
        x, y, c, chips = _position()
        me, sibling = (x, y, c), (x, y, 1 - c)

        def rows(px, py, pc):
            return out_ref.at[pl.ds((4 * px + 2 * py + pc) * m_per, m_per), :]

        def copy(k, blk, to, src=None):
            return pltpu.make_async_remote_copy(
                src_ref=rows(*blk) if src is None else src, dst_ref=rows(*blk), send_sem=send_sems.at[k],
                recv_sem=recv_sems.at[k], device_id=to, device_id_type=MESH)

        mine = pltpu.make_async_copy(x_ref, rows(*me), local_sem)
        mine.start()
        first = [copy(0, me, sibling, src=x_ref)]
        first += [copy(1 + j, me, (*chip, c), src=x_ref) for j, chip in enumerate(chips)]
        for cp in first:
            cp.start()
        passed = [copy(4 + j, (*chip, c), sibling) for j, chip in enumerate(chips)]
        for j, chip in enumerate(chips):
            copy(1 + j, (*chip, c), me).wait_recv()
            passed[j].start()
        copy(0, sibling, me).wait_recv()
        for j, chip in enumerate(chips):
            copy(4 + j, (*chip, 1 - c), me).wait_recv()
        for cp in first + passed:
            cp.wait_send()
        mine.wait()

    return pl.pallas_call(
        body, name=name, out_shape=jax.ShapeDtypeStruct((N_DEV * m_per, n), block.dtype),
        in_specs=[pl.BlockSpec(memory_space=pltpu.VMEM)], out_specs=pl.BlockSpec(memory_space=pltpu.VMEM),
        scratch_shapes=[pltpu.SemaphoreType.DMA((7,)), pltpu.SemaphoreType.DMA((7,)), pltpu.SemaphoreType.DMA],
    )(block)


def _row_tile(R, n):
    budget = 1 << 19
    if R * n <= budget or R % SUBLANES:
        return R
    t = R
    while t * n > budget and t % (2 * SUBLANES) == 0:
        t //= 2
    return t


def _pair_sum(g, recv, c_arr, name):
    _, _, R, n = g.shape
    tr = _row_tile(R, n)

    def body(c_ref, g_ref, r_ref, o_ref):
        o_ref[...] = g_ref[...] + r_ref[...]

    grid_spec = pltpu.PrefetchScalarGridSpec(
        num_scalar_prefetch=1, grid=(N_CHIPS, R // tr),
        in_specs=[pl.BlockSpec((None, None, tr, n), lambda p, i, c: (p, c[0], i, 0)),
                  pl.BlockSpec((None, tr, n), lambda p, i, c: (p, i, 0))],
        out_specs=pl.BlockSpec((None, tr, n), lambda p, i, c: (p, i, 0)))
    return pl.pallas_call(
        body, name=name, grid_spec=grid_spec, out_shape=jax.ShapeDtypeStruct(recv.shape, F32),
        compiler_params=_cparams("parallel", "parallel"),
    )(c_arr, g, recv)


def _chip_sum(parts, name):
    _, R, n = parts.shape
    tr = _row_tile(R, n)

    def body(p_ref, o_ref):
        o_ref[...] = ((p_ref[0] + p_ref[1]) + p_ref[2]) + p_ref[3]

    return pl.pallas_call(
        body, name=name, grid=(R // tr,), in_specs=[pl.BlockSpec((N_CHIPS, tr, n), lambda i: (0, i, 0))],
        out_specs=pl.BlockSpec((tr, n), lambda i: (i, 0)), out_shape=jax.ShapeDtypeStruct((R, n), F32),
        compiler_params=_cparams("parallel"),
    )(parts)


ADAM_C1 = 1.0 / (1.0 - ADAM_B1 ** ADAM_STEP)
ADAM_C2 = 1.0 / (1.0 - ADAM_B2 ** ADAM_STEP)


def _adamw_math(w, g, m, v):
    m = ADAM_B1 * m + (1.0 - ADAM_B1) * g
    v = ADAM_B2 * v + (1.0 - ADAM_B2) * (g * g)
    delta = -ADAM_LR * ((m * ADAM_C1) / (jnp.sqrt(v * ADAM_C2) + ADAM_EPS) + ADAM_WD * w)
    return delta, m, v


def _adamw(w, g, m, v, name):
    R, n = w.shape
    tr = _row_tile(R, n)

    def body(w_ref, g_ref, m_ref, v_ref, d_ref, nm_ref, nv_ref):
        d_ref[...], nm_ref[...], nv_ref[...] = _adamw_math(w_ref[...], g_ref[...], m_ref[...], v_ref[...])

    spec = pl.BlockSpec((tr, n), lambda i: (i, 0))
    return pl.pallas_call(
        body, name=name, grid=(R // tr,), in_specs=[spec] * 4, out_specs=[spec] * 3,
        out_shape=[jax.ShapeDtypeStruct((R, n), F32)] * 3, compiler_params=_cparams("parallel"),
    )(w, g, m, v)


def _adamw_small(w, gall, m, v, name):
    M, n = w.shape

    def body(w_ref, g_ref, m_ref, v_ref, gs_ref, d_ref, nm_ref, nv_ref):
        g = g_ref[0:M, :]
        for d in range(1, N_DEV):
            g = g + g_ref[d * M:(d + 1) * M, :]
        gs_ref[...] = g
        d_ref[...], nm_ref[...], nv_ref[...] = _adamw_math(w_ref[...], g, m_ref[...], v_ref[...])

    return pl.pallas_call(
        body, name=name, out_shape=[jax.ShapeDtypeStruct((M, n), F32)] * 4,
    )(w, gall, m, v)


SMALL_ROWS = 24
MID_ROWS = 4


def _pack_small(ln1_g, ln1_b, ln2_g, ln2_b, norm_w, sinks, a_log, dt_bias):
    mixed = jnp.concatenate([norm_w.reshape(-1), sinks.reshape(-1), a_log.reshape(-1), dt_bias.reshape(-1)])
    mixed = jnp.pad(mixed, (0, D_MODEL - mixed.shape[0]))[None]
    pad = jnp.zeros((SMALL_ROWS - 4 * DEPTH - 1, D_MODEL), F32)
    return jnp.concatenate([ln1_g, ln1_b, ln2_g, ln2_b, mixed, pad], axis=0)


def _unpack_small(p):
    mixed = p[4 * DEPTH]
    return (p[0:4], p[4:8], p[8:12], p[12:16], mixed[0:256].reshape(2, 128), mixed[256:272].reshape(2, 8),
            mixed[272:280].reshape(2, 4), mixed[280:288].reshape(2, 4))


def _pack_mid(conv_w, rconv_w, rconv_b, b_a, b_x, lam):
    flat = jnp.concatenate([conv_w.reshape(2, -1), rconv_w.reshape(2, -1), rconv_b, b_a, b_x, lam], axis=1)
    return jnp.pad(flat, ((0, 0), (0, MID_ROWS * D_MODEL - flat.shape[1]))).reshape(2, MID_ROWS, D_MODEL)


def _unpack_mid(p):
    lead = p.shape[:-2]
    f = p.reshape(lead + (MID_ROWS * D_MODEL,))
    return (f[..., 0:1536].reshape(lead + (4, 384)), f[..., 1536:2560].reshape(lead + (4, 256)),
            f[..., 2560:2816], f[..., 2816:3072], f[..., 3072:3328], f[..., 3328:3584])


def _cols_from_chips(g):
    p, L, R, n = g.shape
    return g.transpose(1, 2, 0, 3).reshape(L, R, p * n)


def _rows_from_chips(g):
    p, L, R, n = g.shape
    return g.transpose(1, 0, 2, 3).reshape(L, p * R, n)


def _cols_to_chips(g):
    L, R, n4 = g.shape
    return g.reshape(L, R, N_CHIPS, n4 // N_CHIPS).transpose(2, 0, 1, 3)


def _rows_to_chips(g):
    L, R4, n = g.shape
    return g.reshape(L, N_CHIPS, R4 // N_CHIPS, n).transpose(1, 0, 2, 3)


def _halves(a):
    return a.reshape(2, -1, a.shape[-1])


def _pad_hyb_cols(w):
    z = jnp.zeros(w.shape[:-1] + (HYB_PAD - HP_BG - 2 * B_HEADS,), w.dtype)
    return jnp.concatenate([w[..., 0:512], w[..., 768:2304], w[..., 2304:2816], w[..., 512:768], w[..., 2816:2824], z], axis=-1)


def _unpad_hyb_cols(w):
    return jnp.concatenate([w[..., 0:512], w[..., 2560:2816], w[..., 512:2048], w[..., 2048:2560], w[..., 2816:2824]], axis=-1)


def _hybrid_fwd(x, W, j, tables, sfx):
    cos, sin_s = tables
    T = x.shape[0]
    N = T // B_CHUNK
    proj = _matmul(x, W["hyb_w_in"][j], "nn", "hyb_in" + sfx)
    sinks_b = jnp.broadcast_to(W["hyb_sinks"][j][:, None], (A_Q_HEADS, LANES))
    o_a, lse = _attn_fwd(proj, cos, sin_s, sinks_b, "attn_fwd" + sfx)
    qkvc = _dnconv_fwd(proj, W["hyb_conv_w"][j], "dnconv_fwd" + sfx)
    bg = proj[:, HP_BG:HP_BG + 2 * B_HEADS]
    beta = jax.nn.sigmoid(bg[:, :B_HEADS])
    pre = bg[:, B_HEADS:] + W["hyb_dt_bias"][j][None]
    g = -jnp.exp(W["hyb_a_log"][j])[None] * jax.nn.softplus(pre)
    gcol = g.T.reshape(B_HEADS, N, B_CHUNK, 1)
    grow = g.T.reshape(B_HEADS, N, 1, B_CHUNK)
    bcol = beta.T.reshape(B_HEADS, N, B_CHUNK, 1)
    nw = W["hyb_norm_w"][j][None]
    o_b, states = _dn_fwd(qkvc, proj, nw, gcol, grow, bcol, "dn_fwd" + sfx)
    mix = jnp.concatenate([o_a, o_b], axis=1)
    y = _matmul(mix, W["hyb_w_out"][j], "nn", "hyb_out" + sfx)
    res = dict(proj=proj, o_a=o_a, lse=lse, qkvc=qkvc, beta=beta, pre=pre, g=g, gcol=gcol, grow=grow, bcol=bcol,
               states=states, mix=mix, sinks_b=sinks_b, nw=nw)
    return y, res


def _hybrid_bwd(x, du, W, j, res, tables, sfx):
    cos, sin_s = tables
    T = x.shape[0]
    proj = res["proj"]
    d_wout = _matmul(res["mix"], du, "tn", "hyb_out_dw" + sfx)
    dmix = _matmul(du, W["hyb_w_out"][j], "nt", "hyb_out_dx" + sfx)
    dq, dkc, dkp, dvc, dvp, dsink = _attn_bwd(proj, cos, sin_s, res["sinks_b"], res["o_a"], res["lse"], dmix,
                                               "attn_bwd" + sfx)
    zpad = jnp.zeros((WINDOW, LANES), F32)
    dk = dkc + jnp.concatenate([dkp[WINDOW:], zpad], axis=0)
    dv = dvc + jnp.concatenate([dvp[WINDOW:], zpad], axis=0)
    dqc, dkcv, dvcv, dz, dg4, dbeta4, dnw = _dn_bwd(res["qkvc"], proj, res["nw"], res["gcol"], res["grow"], res["bcol"],
                                                    res["states"], dmix, "dn_bwd" + sfx)
    dqkvc = jnp.concatenate([dqc, dkcv, dvcv], axis=1)
    dqkvb, dconv = _dnconv_bwd(proj, W["hyb_conv_w"][j], dqkvc, "dnconv_bwd" + sfx)
    dg = dg4.reshape(B_HEADS, T).T
    dbeta = dbeta4.reshape(B_HEADS, T).T
    beta = res["beta"]
    dbeta_logit = dbeta * beta * (1.0 - beta)
    da_logit = dg * (-jnp.exp(W["hyb_a_log"][j]))[None] * jax.nn.sigmoid(res["pre"])
    d_dt_bias = jnp.sum(da_logit, axis=0)
    d_a_log = jnp.sum(dg * res["g"], axis=0)
    zcols = jnp.zeros((T, HYB_PAD - HP_BG - 2 * B_HEADS), F32)
    dproj = jnp.concatenate([dq, dqkvb, dz, dk, dv, dbeta_logit, da_logit, zcols], axis=1)
    d_win = _matmul(x, dproj, "tn", "hyb_in_dw" + sfx)
    dx = _matmul(dproj, W["hyb_w_in"][j], "nt", "hyb_in_dx" + sfx, epi=_epi_add_residual, extra=du)
    grads = dict(hyb_w_in=d_win, hyb_w_out=d_wout, hyb_sinks=dsink[0], hyb_conv_w=dconv, hyb_a_log=d_a_log,
                 hyb_dt_bias=d_dt_bias, hyb_norm_w=jnp.sum(dnw[:, 0, :], axis=0))
    return dx, grads


def _rec_fwd(x, W, j, sfx):
    proj = _matmul(x, W["rec_w_in"][j], "nn", "rec_in" + sfx)
    sp = jax.nn.softplus(-W["rec_lambda"][j])[None]
    hg, h = _rglru_fwd(proj, W["rec_conv_w"][j], W["rec_conv_b"][j][None], W["rec_w_a"][j], W["rec_w_x"][j],
                       W["rec_b_a"][j][None], W["rec_b_x"][j][None], sp, "rglru_fwd" + sfx)
    y = _matmul(hg, W["rec_w_out"][j], "nn", "rec_out" + sfx)
    return y, dict(proj=proj, hg=hg, h=h, sp=sp)


def _rec_bwd(x, du, W, j, res, sfx):
    d_wout = _matmul(res["hg"], du, "tn", "rec_out_dw" + sfx)
    dhg = _matmul(du, W["rec_w_out"][j], "nt", "rec_out_dx" + sfx)
    dproj, dcw, dcb, dwa, dwx, dba, dbx, dsp = _rglru_bwd(
        res["proj"], W["rec_conv_w"][j], W["rec_conv_b"][j][None], W["rec_w_a"][j], W["rec_w_x"][j],
        W["rec_b_a"][j][None], W["rec_b_x"][j][None], res["sp"], res["h"], dhg, "rglru_bwd" + sfx)
    d_lam = dsp[0] * (-jax.nn.sigmoid(-W["rec_lambda"][j]))
    d_win = _matmul(x, dproj, "tn", "rec_in_dw" + sfx)
    dx = _matmul(dproj, W["rec_w_in"][j], "nt", "rec_in_dx" + sfx, epi=_epi_add_residual, extra=du)
    grads = dict(rec_w_in=d_win, rec_w_out=d_wout, rec_conv_w=dcw, rec_conv_b=dcb[0], rec_w_a=dwa, rec_w_x=dwx,
                 rec_b_a=dba[0], rec_b_x=dbx[0], rec_lambda=d_lam)
    return dx, grads


def _local_step(x, tgt, W):
    T = x.shape[0]
    tables = _rope_tables(T)
    acts = []
    for layer in range(DEPTH):
        j, sfx = layer // 2, ""
        if layer % 2 == 0:
            y, res = _hybrid_fwd(x, W, j, tables, sfx)
        else:
            y, res = _rec_fwd(x, W, j, sfx)
        x1 = _ln_fwd(x, y, W["ln1_g"][layer][None], W["ln1_b"][layer][None], "ln_fwd")
        h1 = _matmul(x1, W["mlp_w1"][layer], "nn", "mlp_up")
        y2 = _matmul(h1, W["mlp_w2"][layer], "nn", "mlp_down", a_fn=_relu2)
        x2 = _ln_fwd(x1, y2, W["ln2_g"][layer][None], W["ln2_b"][layer][None], "ln_fwd")
        acts.append(dict(x=x, y=y, res=res, x1=x1, h1=h1, y2=y2))
        x = x2
    dx, loss = _loss_head(x, tgt, "loss_head")
    per_layer = [None] * DEPTH
    for layer in reversed(range(DEPTH)):
        j, a = layer // 2, acts[layer]
        du2, dg2, db2 = _ln_bwd(a["x1"], a["y2"], W["ln2_g"][layer][None], dx, "ln_bwd")
        d_w2 = _matmul(a["h1"], du2, "tn", "mlp_down_dw", a_fn=_relu2)
        dh1 = _matmul(du2, W["mlp_w2"][layer], "nt", "mlp_down_dx", epi=_epi_drelu2, extra=a["h1"])
        d_w1 = _matmul(a["x1"], dh1, "tn", "mlp_up_dw")
        dx1 = _matmul(dh1, W["mlp_w1"][layer], "nt", "mlp_up_dx", epi=_epi_add_residual, extra=du2)
        du1, dg1, db1 = _ln_bwd(a["x"], a["y"], W["ln1_g"][layer][None], dx1, "ln_bwd")
        if layer % 2 == 0:
            dx, g = _hybrid_bwd(a["x"], du1, W, j, a["res"], tables, "")
        else:
            dx, g = _rec_bwd(a["x"], du1, W, j, a["res"], "")
        g.update(ln1_g=dg1[0], ln1_b=db1[0], ln2_g=dg2[0], ln2_b=db2[0], mlp_w1=d_w1, mlp_w2=d_w2)
        per_layer[layer] = g
    grads = {}
    for name in ("ln1_g", "ln1_b", "ln2_g", "ln2_b", "mlp_w1", "mlp_w2"):
        grads[name] = jnp.stack([per_layer[l][name] for l in range(DEPTH)])
    for name in per_layer[0]:
        if name not in grads:
            grads[name] = jnp.stack([per_layer[l][name] for l in (0, 2)])
    for name in per_layer[1]:
        if name not in grads:
            grads[name] = jnp.stack([per_layer[l][name] for l in (1, 3)])
    return loss, dx, grads


BIG = ("hyb_w_in", "hyb_w_out", "rec_w_in", "rec_w_out", "mlp_w1", "mlp_w2", "rec_w_a", "rec_w_x")
COL_SHARDED = ("hyb_w_in", "rec_w_in", "mlp_w1")
MID = ("hyb_conv_w", "rec_conv_w", "rec_conv_b", "rec_b_a", "rec_b_x", "rec_lambda")
SMALL = ("ln1_g", "ln1_b", "ln2_g", "ln2_b", "hyb_norm_w", "hyb_sinks", "hyb_a_log", "hyb_dt_bias")
WEIGHTS = ("hyb_w_in", "hyb_sinks", "hyb_conv_w", "hyb_a_log", "hyb_dt_bias", "hyb_norm_w", "hyb_w_out", "rec_w_in",
           "rec_conv_w", "rec_conv_b", "rec_w_a", "rec_b_a", "rec_w_x", "rec_b_x", "rec_lambda", "rec_w_out", "ln1_g",
           "ln1_b", "mlp_w1", "mlp_w2", "ln2_g", "ln2_b")


def _gather_full_weights(w):
    shards = [_halves(w[k].astype(MXU_DTYPE)) for k in BIG]
    shards.append(_pack_mid(*[w[k] for k in MID]))
    got = _all_gather_weights(shards, "all_gather_weights")
    W = {}
    for k, g in zip(BIG, got[:-1]):
        L = w[k].shape[0]
        if k in ("rec_w_a", "rec_w_x"):
            g5 = g.reshape((N_CHIPS,) + w[k].shape)
            W[k] = g5.transpose(1, 2, 0, 3, 4).reshape(L, LRU_BLOCKS, LRU_BLOCK_W, LRU_BLOCK_W)
        else:
            g4 = g.reshape((N_CHIPS,) + w[k].shape)
            W[k] = _cols_from_chips(g4) if k in COL_SHARDED else _rows_from_chips(g4)
    W["hyb_w_in"] = _pad_hyb_cols(W["hyb_w_in"])
    conv_w, rconv_w, rconv_b, b_a, b_x, lam = _unpack_mid(got[-1])
    W["hyb_conv_w"] = conv_w.transpose(1, 2, 0, 3).reshape(2, CONV_K, 3 * B_W)
    W["rec_conv_w"] = rconv_w.transpose(1, 2, 0, 3).reshape(2, CONV_K, D_MODEL)
    for k, v in (("rec_conv_b", rconv_b), ("rec_b_a", b_a), ("rec_b_x", b_x), ("rec_lambda", lam)):
        W[k] = v.transpose(1, 0, 2).reshape(2, D_MODEL)
    for k in SMALL:
        W[k] = w[k]
    return W


def _grads_by_chip(grads):
    out = []
    g = dict(grads)
    g["hyb_w_in"] = _unpad_hyb_cols(g["hyb_w_in"])
    for k in BIG:
        v = g[k]
        if k in ("rec_w_a", "rec_w_x"):
            L = v.shape[0]
            v = v.reshape(L, LRU_BLOCKS, N_CHIPS, LRU_BLOCK_W // N_CHIPS, LRU_BLOCK_W).transpose(2, 0, 1, 3, 4)
        else:
            v = _cols_to_chips(v) if k in COL_SHARDED else _rows_to_chips(v)
        out.append(v.reshape(N_CHIPS, 2, -1, v.shape[-1]))
    conv_w = g["hyb_conv_w"].reshape(2, CONV_K, N_CHIPS, -1).transpose(2, 0, 1, 3)
    rconv_w = g["rec_conv_w"].reshape(2, CONV_K, N_CHIPS, -1).transpose(2, 0, 1, 3)
    vecs = [g[k].reshape(2, N_CHIPS, -1).transpose(1, 0, 2) for k in ("rec_conv_b", "rec_b_a", "rec_b_x", "rec_lambda")]
    mid = jnp.stack([_pack_mid(conv_w[p], rconv_w[p], *[v[p] for v in vecs]) for p in range(N_CHIPS)])
    out.append(mid)
    return out


def kernel(x, hyb_w_in, hyb_sinks, hyb_conv_w, hyb_a_log, hyb_dt_bias, hyb_norm_w, hyb_w_out, rec_w_in, rec_conv_w, rec_conv_b, rec_w_a, rec_b_a, rec_w_x, rec_b_x, rec_lambda, rec_w_out, ln1_g, ln1_b, mlp_w1, mlp_w2, ln2_g, ln2_b, loss_target, m_hyb_w_in, m_hyb_sinks, m_hyb_conv_w, m_hyb_a_log, m_hyb_dt_bias, m_hyb_norm_w, m_hyb_w_out, m_rec_w_in, m_rec_conv_w, m_rec_conv_b, m_rec_w_a, m_rec_b_a, m_rec_w_x, m_rec_b_x, m_rec_lambda, m_rec_w_out, m_ln1_g, m_ln1_b, m_mlp_w1, m_mlp_w2, m_ln2_g, m_ln2_b, v_hyb_w_in, v_hyb_sinks, v_hyb_conv_w, v_hyb_a_log, v_hyb_dt_bias, v_hyb_norm_w, v_hyb_w_out, v_rec_w_in, v_rec_conv_w, v_rec_conv_b, v_rec_w_a, v_rec_b_a, v_rec_w_x, v_rec_b_x, v_rec_lambda, v_rec_w_out, v_ln1_g, v_ln1_b, v_mlp_w1, v_mlp_w2, v_ln2_g, v_ln2_b):
    args = locals()
    w = {k: args[k] for k in WEIGHTS}
    m = {k: args["m_" + k] for k in WEIGHTS}
    v = {k: args["v_" + k] for k in WEIGHTS}

    W = _gather_full_weights(w)
    loss, dx, grads = _local_step(x[0], loss_target[0], W)
    loss = lax.psum(loss[0, 0], ("x", "y", "c"))

    c_arr = lax.axis_index("c").astype(jnp.int32).reshape(1)
    by_chip = _grads_by_chip(grads)
    from_sibling = _rs_to_sibling(by_chip, "rs_to_sibling")
    pair = [_pair_sum(g, r, c_arr, "pair_sum") for g, r in zip(by_chip, from_sibling)]
    from_chips = _rs_across_chips(pair, "rs_across_chips")
    half = [_chip_sum(p, "chip_sum") for p in from_chips]
    joined = _rs_join_halves(half, "rs_join_halves")

    g_out, d_out, m_out, v_out = {}, {}, {}, {}

    def update(name, g2d):
        shape = w[name].shape
        n = g2d.shape[-1]
        d, nm, nv = _adamw(w[name].reshape(-1, n), g2d, m[name].reshape(-1, n), v[name].reshape(-1, n), "adamw")
        g_out[name], d_out[name] = g2d.reshape(shape), d.reshape(shape)
        m_out[name], v_out[name] = nm.reshape(shape), nv.reshape(shape)

    for k, g in zip(BIG, joined[:-1]):
        update(k, g.reshape(-1, g.shape[-1]))
    mid_w, mid_m, mid_v = (_pack_mid(*[t[k] for k in MID]).reshape(-1, D_MODEL) for t in (w, m, v))
    mid_g = joined[-1].reshape(-1, D_MODEL)
    mid_d, mid_nm, mid_nv = _adamw(mid_w, mid_g, mid_m, mid_v, "adamw_mid")
    for dst, packed in ((g_out, mid_g), (d_out, mid_d), (m_out, mid_nm), (v_out, mid_nv)):
        for k, val in zip(MID, _unpack_mid(packed.reshape(2, MID_ROWS, D_MODEL))):
            dst[k] = val.reshape(w[k].shape)

    small_g = _pack_small(*[grads[k] for k in SMALL])
    small_all = _all_gather_small(small_g, "all_gather_small")
    sw, sm, sv = (_pack_small(*[t[k] for k in SMALL]) for t in (w, m, v))
    sg, sd, snm, snv = _adamw_small(sw, small_all, sm, sv, "adamw_small")
    for dst, packed in ((g_out, sg), (d_out, sd), (m_out, snm), (v_out, snv)):
        for k, val in zip(SMALL, _unpack_small(packed)):
            dst[k] = val

    return (loss, dx[None], *[g_out[k] for k in WEIGHTS], *[d_out[k] for k in WEIGHTS],
            *[m_out[k] for k in WEIGHTS], *[v_out[k] for k in WEIGHTS])
```

```python
import functools

import jax
import jax.numpy as jnp
from jax import lax
from jax.experimental import pallas as pl
from jax.experimental.pallas import tpu as pltpu

F32 = jnp.float32
MXU_DTYPE = jnp.bfloat16

D_MODEL = 1024
DEPTH = 4
A_HEAD_DIM = 64
A_Q_HEADS = 8
A_KV_HEADS = 2
WINDOW = 128
ROPE_THETA = 10000.0
B_HEADS = 4
B_HEAD_DIM = 128
B_CHUNK = 64
CONV_K = 4
LRU_BLOCKS = 4
LRU_BLOCK_W = D_MODEL // LRU_BLOCKS
LRU_C = 8.0
D_FF = 4 * D_MODEL
A_Q_W = A_Q_HEADS * A_HEAD_DIM
A_KV_W = A_KV_HEADS * A_HEAD_DIM
B_W = B_HEADS * B_HEAD_DIM
HYB_PROJ = A_Q_W + 2 * A_KV_W + 4 * B_W + 2 * B_HEADS
DN_ALPHA = (2 * DEPTH) ** 0.25
LN_EPS = 1e-5
NORM_EPS = 1e-6
ADAM_LR = 0.001
ADAM_B1 = 0.9
ADAM_B2 = 0.999
ADAM_EPS = 1e-08
ADAM_WD = 0.01
ADAM_STEP = 10

HP_Q = 0
HP_QKVB = 512
HP_Z = 2048
HP_K = 2560
HP_V = 2688
HP_BG = 2816
HYB_PAD = 3072

N_CHIPS = 4
N_DEV = 8
V7X_VMEM_LIMIT = 48 * 1024 * 1024
LANES = 128
SUBLANES = 8
NEG_BIG = -1e30

NN = (((1,), (0,)), ((), ()))
NT = (((1,), (1,)), ((), ()))
TN = (((0,), (0,)), ((), ()))


def _cparams(*sem):
    return pltpu.CompilerParams(dimension_semantics=sem, vmem_limit_bytes=V7X_VMEM_LIMIT)


def _dot(a, b, dims=NN):
    return lax.dot_general(a.astype(MXU_DTYPE), b.astype(MXU_DTYPE), dims, preferred_element_type=F32)


def _dotf(a, b, dims=NN):
    return lax.dot_general(a, b, dims, precision=lax.Precision.HIGHEST, preferred_element_type=F32)


def _tile(dim, pref):
    t = min(dim, pref)
    while dim % t:
        t //= 2
    return t


def _sigmoid(x):
    return 1.0 / (1.0 + jnp.exp(-x))


def _silu(x):
    return x * _sigmoid(x)


def _dsilu(x):
    s = _sigmoid(x)
    return s * (1.0 + x * (1.0 - s))


GELU_C = 0.7978845608028654
GELU_A = 0.044715


def _gelu(x):
    return 0.5 * x * (1.0 + jnp.tanh(GELU_C * (x + GELU_A * x * x * x)))


def _dgelu(x):
    t = jnp.tanh(GELU_C * (x + GELU_A * x * x * x))
    return 0.5 * (1.0 + t) + 0.5 * x * (1.0 - t * t) * GELU_C * (1.0 + 3.0 * GELU_A * x * x)


def _matmul(a, b, mode, name, *, tm=512, tn=1024, tk=1024, a_fn=None, epi=None, extra=None):
    if mode == "nn":
        (M, K), (_, N) = a.shape, b.shape
    elif mode == "nt":
        (M, K), (N, _) = a.shape, b.shape
    else:
        (K, M), (_, N) = a.shape, b.shape
    tm, tn, tk = _tile(M, tm), _tile(N, tn), _tile(K, tk)
    nk = K // tk
    if mode == "tn":
        a_spec = pl.BlockSpec((tk, tm), lambda i, j, k: (k, i))
    else:
        a_spec = pl.BlockSpec((tm, tk), lambda i, j, k: (i, k))
    if mode == "nt":
        b_spec = pl.BlockSpec((tn, tk), lambda i, j, k: (j, k))
    else:
        b_spec = pl.BlockSpec((tk, tn), lambda i, j, k: (k, j))
    o_spec = pl.BlockSpec((tm, tn), lambda i, j, k: (i, j))
    dims = {"nn": NN, "nt": NT, "tn": TN}[mode]
    has_extra = extra is not None

    def body(*refs):
        a_ref, b_ref = refs[0], refs[1]
        e_ref = refs[2] if has_extra else None
        o_ref = refs[3] if has_extra else refs[2]
        av = a_ref[...]
        if a_fn is not None:
            av = a_fn(av)
        part = _dot(av, b_ref[...], dims)

        def finish(acc):
            if epi is not None:
                acc = epi(acc, e_ref[...])
            o_ref[...] = acc

        if nk == 1:
            finish(part)
        else:
            acc_ref = refs[-1]
            k = pl.program_id(2)

            @pl.when(k == 0)
            def _():
                acc_ref[...] = part

            @pl.when(k > 0)
            def _():
                acc_ref[...] += part

            @pl.when(k == nk - 1)
            def _():
                finish(acc_ref[...])

    in_specs = [a_spec, b_spec] + ([o_spec] if has_extra else [])
    args = (a, b) + ((extra,) if has_extra else ())
    return pl.pallas_call(
        body, name=name, grid=(M // tm, N // tn, nk), in_specs=in_specs, out_specs=o_spec,
        out_shape=jax.ShapeDtypeStruct((M, N), F32),
        scratch_shapes=[pltpu.VMEM((tm, tn), F32)] if nk > 1 else [],
        compiler_params=_cparams("parallel", "parallel", "arbitrary"),
    )(*args)


def _relu2(v):
    r = jnp.maximum(v, 0.0)
    return r * r


def _epi_drelu2(acc, h):
    return acc * (2.0 * jnp.maximum(h, 0.0))


def _epi_add_residual(acc, du):
    return acc + DN_ALPHA * du


def _ln_fwd(x, y, g, b, name):
    T, D = x.shape
    tr = _tile(T, 512)

    def body(x_ref, y_ref, g_ref, b_ref, o_ref):
        u = DN_ALPHA * x_ref[...] + y_ref[...]
        mu = jnp.mean(u, axis=-1, keepdims=True)
        d = u - mu
        var = jnp.mean(d * d, axis=-1, keepdims=True)
        o_ref[...] = d * lax.rsqrt(var + LN_EPS) * g_ref[...] + b_ref[...]

    row = pl.BlockSpec((tr, D), lambda i: (i, 0))
    vec = pl.BlockSpec((1, D), lambda i: (0, 0))
    return pl.pallas_call(
        body, name=name, grid=(T // tr,), in_specs=[row, row, vec, vec], out_specs=row,
        out_shape=jax.ShapeDtypeStruct((T, D), F32), compiler_params=_cparams("parallel"),
    )(x, y, g, b)


def _ln_bwd(x, y, g, dout, name):
    T, D = x.shape
    tr = _tile(T, 512)

    def body(x_ref, y_ref, g_ref, d_ref, du_ref, dg_ref, db_ref):
        i = pl.program_id(0)
        u = DN_ALPHA * x_ref[...] + y_ref[...]
        mu = jnp.mean(u, axis=-1, keepdims=True)
        d = u - mu
        rstd = lax.rsqrt(jnp.mean(d * d, axis=-1, keepdims=True) + LN_EPS)
        xhat = d * rstd
        dout_v = d_ref[...]
        dxh = dout_v * g_ref[...]
        m1 = jnp.mean(dxh, axis=-1, keepdims=True)
        m2 = jnp.mean(dxh * xhat, axis=-1, keepdims=True)
        du_ref[...] = rstd * (dxh - m1 - xhat * m2)
        pg = jnp.sum(dout_v * xhat, axis=0, keepdims=True)
        pb = jnp.sum(dout_v, axis=0, keepdims=True)

        @pl.when(i == 0)
        def _():
            dg_ref[...] = pg
            db_ref[...] = pb

        @pl.when(i > 0)
        def _():
            dg_ref[...] += pg
            db_ref[...] += pb

    row = pl.BlockSpec((tr, D), lambda i: (i, 0))
    vec = pl.BlockSpec((1, D), lambda i: (0, 0))
    return pl.pallas_call(
        body, name=name, grid=(T // tr,), in_specs=[row, row, vec, row], out_specs=[row, vec, vec],
        out_shape=[jax.ShapeDtypeStruct((T, D), F32), jax.ShapeDtypeStruct((1, D), F32),
                   jax.ShapeDtypeStruct((1, D), F32)],
        compiler_params=_cparams("arbitrary"),
    )(x, y, g, dout)


def _loss_head(y, tgt, name):
    T, D = y.shape
    tr = _tile(T, 512)

    def body(y_ref, t_ref, dy_ref, l_ref):
        i = pl.program_id(0)
        e = y_ref[...] - t_ref[...]
        dy_ref[...] = e * (1.0 / D)
        part = jnp.sum(e * e, axis=(0, 1), keepdims=True) * (0.5 / D)

        @pl.when(i == 0)
        def _():
            l_ref[...] = part

        @pl.when(i > 0)
        def _():
            l_ref[...] += part

    row = pl.BlockSpec((tr, D), lambda i: (i, 0))
    one = pl.BlockSpec((1, 1), lambda i: (0, 0))
    return pl.pallas_call(
        body, name=name, grid=(T // tr,), in_specs=[row, row], out_specs=[row, one],
        out_shape=[jax.ShapeDtypeStruct((T, D), F32), jax.ShapeDtypeStruct((1, 1), F32)],
        compiler_params=_cparams("arbitrary"),
    )(y, tgt)


def _swap_half(x):
    n = x.shape[-1]
    lane = lax.broadcasted_iota(jnp.int32, x.shape, 1)
    first = (lane % A_HEAD_DIM) < (A_HEAD_DIM // 2)
    return jnp.where(first, pltpu.roll(x, n - A_HEAD_DIM // 2, axis=1), pltpu.roll(x, A_HEAD_DIM // 2, axis=1))


def _rope(x, cos, sin_signed):
    return x * cos + _swap_half(x) * sin_signed


def _rope_t(dy, cos, sin_signed):
    return dy * cos + _swap_half(dy * sin_signed)


def _rope_tables(T):
    half = A_HEAD_DIM // 2
    inv_freq = ROPE_THETA ** (-jnp.arange(half, dtype=F32) / half)
    ang = jnp.arange(T, dtype=F32)[:, None] * inv_freq[None, :]
    cos = jnp.tile(jnp.cos(ang), (1, 2 * A_Q_HEADS))
    sin = jnp.sin(ang)
    sin_signed = jnp.tile(jnp.concatenate([-sin, sin], axis=1), (1, A_Q_HEADS))
    return cos, sin_signed


def _band_mask(n):
    qi = lax.broadcasted_iota(jnp.int32, (WINDOW, 2 * WINDOW), 0)
    kj = lax.broadcasted_iota(jnp.int32, (WINDOW, 2 * WINDOW), 1)
    return (kj > qi) & (kj <= qi + WINDOW) & ((n > 0) | (kj >= WINDOW))


def _place(v, src_half, dst_half):
    lane = lax.broadcasted_iota(jnp.int32, v.shape, 1)
    if src_half != dst_half:
        v = pltpu.roll(v, A_HEAD_DIM, axis=1)
    keep = (lane >= A_HEAD_DIM) if dst_half else (lane < A_HEAD_DIM)
    return jnp.where(keep, v, 0.0)


def _attn_specs():
    kb, vb = HP_K // LANES, HP_V // LANES
    prev = lambda n: jnp.maximum(n - 1, 0)
    return dict(
        q=pl.BlockSpec((WINDOW, A_Q_W), lambda n: (n, 0)),
        kc=pl.BlockSpec((WINDOW, LANES), lambda n: (n, kb)),
        kp=pl.BlockSpec((WINDOW, LANES), lambda n: (prev(n), kb)),
        vc=pl.BlockSpec((WINDOW, LANES), lambda n: (n, vb)),
        vp=pl.BlockSpec((WINDOW, LANES), lambda n: (prev(n), vb)),
        tq=pl.BlockSpec((WINDOW, A_Q_W), lambda n: (n, 0)),
        tp=pl.BlockSpec((WINDOW, LANES), lambda n: (prev(n), 0)),
        sink=pl.BlockSpec((A_Q_HEADS, LANES), lambda n: (0, 0)),
        row512=pl.BlockSpec((WINDOW, A_Q_W), lambda n: (n, 0)),
        row128=pl.BlockSpec((WINDOW, LANES), lambda n: (n, 0)),
        lse=pl.BlockSpec((WINDOW, A_Q_HEADS), lambda n: (n, 0)),
    )


def _attn_fwd(proj, cos, sin_s, sinks_b, name):
    T = proj.shape[0]
    sp = _attn_specs()

    def body(q_ref, kc_ref, kp_ref, vc_ref, vp_ref, cq_ref, sq_ref, cp_ref, sp_ref, sink_ref, o_ref, l_ref):
        n = pl.program_id(0)
        cq, sq = cq_ref[...], sq_ref[...]
        q = _rope(q_ref[...], cq, sq) * (A_HEAD_DIM ** -0.5)
        kc = _rope(kc_ref[...], cq[:, :LANES], sq[:, :LANES])
        kp = _rope(kp_ref[...], cp_ref[...], sp_ref[...])
        kk = jnp.concatenate([kp, kc], axis=0)
        vv = jnp.concatenate([vp_ref[...], vc_ref[...]], axis=0)
        mask = _band_mask(n)
        lane = lax.broadcasted_iota(jnp.int32, (WINDOW, LANES), 1)
        outs = []
        lane8 = lax.broadcasted_iota(jnp.int32, (WINDOW, A_Q_HEADS), 1)
        lse = jnp.zeros((WINDOW, A_Q_HEADS), F32)
        for pb in range(A_Q_HEADS // 2):
            kvh = pb // 2
            q2 = q[:, pb * LANES:(pb + 1) * LANES]
            halves = []
            for e in range(2):
                hq = 2 * pb + e
                s = _dot(_place(q2, e, kvh), kk, NT)
                s = jnp.where(mask, s, NEG_BIG)
                sk = sink_ref[hq:hq + 1, 0:1]
                m = jnp.maximum(jnp.max(s, axis=-1, keepdims=True), sk)
                p = jnp.exp(s - m)
                den = jnp.sum(p, axis=-1, keepdims=True) + jnp.exp(sk - m)
                o = _dot(p * (1.0 / den), vv, NN)
                halves.append(_place(o, kvh, e))
                lse = jnp.where(lane8 == hq, m + jnp.log(den), lse)
            outs.append(jnp.where(lane < A_HEAD_DIM, halves[0], halves[1]))
        o_ref[...] = jnp.concatenate(outs, axis=1)
        l_ref[...] = lse

    return pl.pallas_call(
        body, name=name, grid=(T // WINDOW,),
        in_specs=[sp["q"], sp["kc"], sp["kp"], sp["vc"], sp["vp"], sp["tq"], sp["tq"], sp["tp"], sp["tp"], sp["sink"]],
        out_specs=[sp["row512"], sp["lse"]],
        out_shape=[jax.ShapeDtypeStruct((T, A_Q_W), F32), jax.ShapeDtypeStruct((T, A_Q_HEADS), F32)],
        compiler_params=_cparams("parallel"),
    )(proj, proj, proj, proj, proj, cos, sin_s, cos, sin_s, sinks_b)


def _attn_bwd(proj, cos, sin_s, sinks_b, o, lse, dmix, name):
    T = proj.shape[0]
    sp = _attn_specs()

    def body(q_ref, kc_ref, kp_ref, vc_ref, vp_ref, cq_ref, sq_ref, cp_ref, sp_ref, sink_ref, o_ref, l_ref, do_ref,
             dq_ref, dkc_ref, dkp_ref, dvc_ref, dvp_ref, dsink_ref):
        n = pl.program_id(0)
        cq, sq = cq_ref[...], sq_ref[...]
        cp, sps = cp_ref[...], sp_ref[...]
        q = _rope(q_ref[...], cq, sq) * (A_HEAD_DIM ** -0.5)
        kc = _rope(kc_ref[...], cq[:, :LANES], sq[:, :LANES])
        kp = _rope(kp_ref[...], cp, sps)
        kk = jnp.concatenate([kp, kc], axis=0)
        vv = jnp.concatenate([vp_ref[...], vc_ref[...]], axis=0)
        mask = _band_mask(n)
        lane = lax.broadcasted_iota(jnp.int32, (WINDOW, LANES), 1)
        do_all, o_all, l_all = do_ref[...], o_ref[...], l_ref[...]
        dkk = jnp.zeros((2 * WINDOW, LANES), F32)
        dvv = jnp.zeros((2 * WINDOW, LANES), F32)
        dqs = []
        lane8 = lax.broadcasted_iota(jnp.int32, (WINDOW, A_Q_HEADS), 1)
        head8 = lax.broadcasted_iota(jnp.int32, (1, A_Q_HEADS), 1)
        dsk = jnp.zeros((1, A_Q_HEADS), F32)
        for pb in range(A_Q_HEADS // 2):
            kvh = pb // 2
            q2 = q[:, pb * LANES:(pb + 1) * LANES]
            do2 = do_all[:, pb * LANES:(pb + 1) * LANES]
            prod = do2 * o_all[:, pb * LANES:(pb + 1) * LANES]
            halves = []
            for e in range(2):
                hq = 2 * pb + e
                in_half = (lane >= A_HEAD_DIM) if e else (lane < A_HEAD_DIM)
                delta = jnp.sum(jnp.where(in_half, prod, 0.0), axis=-1, keepdims=True)
                qe = _place(q2, e, kvh)
                doe = _place(do2, e, kvh)
                lh = jnp.sum(jnp.where(lane8 == hq, l_all, 0.0), axis=-1, keepdims=True)
                s = _dot(qe, kk, NT)
                p = jnp.where(mask, jnp.exp(jnp.where(mask, s, NEG_BIG) - lh), 0.0)
                dvv = dvv + _dot(p, doe, TN)
                dp = _dot(doe, vv, NT)
                ds = p * (dp - delta)
                dkk = dkk + _dot(ds, qe, TN)
                halves.append(_place(_dot(ds, kk, NN), kvh, e))
                p_sink = jnp.exp(sink_ref[hq:hq + 1, 0:1] - lh)
                dsk = jnp.where(head8 == hq, -jnp.sum(p_sink * delta, axis=(0, 1), keepdims=True), dsk)
            dqs.append(jnp.where(lane < A_HEAD_DIM, halves[0], halves[1]))
        dq = jnp.concatenate(dqs, axis=1) * (A_HEAD_DIM ** -0.5)
        dq_ref[...] = _rope_t(dq, cq, sq)
        dkp_ref[...] = _rope_t(dkk[:WINDOW], cp, sps)
        dkc_ref[...] = _rope_t(dkk[WINDOW:], cq[:, :LANES], sq[:, :LANES])
        dvp_ref[...] = dvv[:WINDOW]
        dvc_ref[...] = dvv[WINDOW:]

        @pl.when(n == 0)
        def _():
            dsink_ref[...] = dsk

        @pl.when(n > 0)
        def _():
            dsink_ref[...] += dsk

    return pl.pallas_call(
        body, name=name, grid=(T // WINDOW,),
        in_specs=[sp["q"], sp["kc"], sp["kp"], sp["vc"], sp["vp"], sp["tq"], sp["tq"], sp["tp"], sp["tp"], sp["sink"],
                  sp["row512"], sp["lse"], sp["row512"]],
        out_specs=[sp["row512"], sp["row128"], sp["row128"], sp["row128"], sp["row128"],
                   pl.BlockSpec((1, A_Q_HEADS), lambda n: (0, 0))],
        out_shape=[jax.ShapeDtypeStruct((T, A_Q_W), F32)] + [jax.ShapeDtypeStruct((T, LANES), F32)] * 4
        + [jax.ShapeDtypeStruct((1, A_Q_HEADS), F32)],
        compiler_params=_cparams("arbitrary"),
    )(proj, proj, proj, proj, proj, cos, sin_s, cos, sin_s, sinks_b, o, lse, dmix)


def _shift_down(x, prev8, k):
    if k == 0:
        return x
    row = lax.broadcasted_iota(jnp.int32, prev8.shape, 0)
    r = pltpu.roll(x, k, axis=0)
    top = jnp.where(row < k, pltpu.roll(prev8, k, axis=0), r[:SUBLANES])
    return jnp.concatenate([top, r[SUBLANES:]], axis=0)


def _shift_up(x, next8, k):
    if k == 0:
        return x
    R = x.shape[0]
    row = lax.broadcasted_iota(jnp.int32, next8.shape, 0)
    r = pltpu.roll(x, R - k, axis=0)
    bot = jnp.where(row >= SUBLANES - k, pltpu.roll(next8, SUBLANES - k, axis=0), r[R - SUBLANES:])
    return jnp.concatenate([r[:R - SUBLANES], bot], axis=0)


def _conv(x, prev8, w):
    y = x * w[CONV_K - 1:CONV_K]
    for j in range(CONV_K - 1):
        y = y + _shift_down(x, prev8, CONV_K - 1 - j) * w[j:j + 1]
    return y


def _conv_bwd(x, prev8, w, dy, next8_dy):
    dx = dy * w[CONV_K - 1:CONV_K]
    dws = []
    for j in range(CONV_K - 1):
        k = CONV_K - 1 - j
        dx = dx + _shift_up(dy, next8_dy, k) * w[j:j + 1]
        dws.append(jnp.sum(dy * _shift_down(x, prev8, k), axis=0, keepdims=True))
    dws.append(jnp.sum(dy * x, axis=0, keepdims=True))
    return dx, dws


def _dnconv_fwd(proj, conv_w, name):
    T = proj.shape[0]
    R = _tile(T, 512)
    cb0 = HP_QKVB // A_Q_W

    def body(x_ref, w_ref, o_ref, prev_ref):
        i = pl.program_id(1)

        @pl.when(i == 0)
        def _():
            prev_ref[...] = jnp.zeros_like(prev_ref)

        x = x_ref[...]
        o_ref[...] = _silu(_conv(x, prev_ref[...], w_ref[...]))
        prev_ref[...] = x[R - SUBLANES:]

    return pl.pallas_call(
        body, name=name, grid=(3, T // R),
        in_specs=[pl.BlockSpec((R, B_W), lambda j, i: (i, cb0 + j)), pl.BlockSpec((CONV_K, B_W), lambda j, i: (0, j))],
        out_specs=pl.BlockSpec((R, B_W), lambda j, i: (i, j)),
        out_shape=jax.ShapeDtypeStruct((T, 3 * B_W), F32),
        scratch_shapes=[pltpu.VMEM((SUBLANES, B_W), F32)],
        compiler_params=_cparams("parallel", "arbitrary"),
    )(proj, conv_w)


def _dnconv_bwd(proj, conv_w, dy, name):
    T = proj.shape[0]
    R = _tile(T, 512)
    nb = T // R
    cb0 = HP_QKVB // A_Q_W
    r8 = R // SUBLANES

    def body(x_ref, xp_ref, w_ref, dy_ref, dx_ref, dw_ref, next_ref):
        i = pl.program_id(1)
        blk = nb - 1 - i

        @pl.when(i == 0)
        def _():
            next_ref[...] = jnp.zeros_like(next_ref)

        x = x_ref[...]
        prev8 = jnp.where(blk > 0, xp_ref[...], 0.0)
        w = w_ref[...]
        dpre = dy_ref[...] * _dsilu(_conv(x, prev8, w))
        dx, dw = _conv_bwd(x, prev8, w, dpre, next_ref[...])
        dx_ref[...] = dx
        next_ref[...] = dpre[:SUBLANES]

        @pl.when(i == 0)
        def _():
            for j in range(CONV_K):
                dw_ref[j:j + 1, :] = dw[j]

        @pl.when(i > 0)
        def _():
            for j in range(CONV_K):
                dw_ref[j:j + 1, :] += dw[j]

    return pl.pallas_call(
        body, name=name, grid=(3, nb),
        in_specs=[pl.BlockSpec((R, B_W), lambda j, i: (nb - 1 - i, cb0 + j)),
                  pl.BlockSpec((SUBLANES, B_W), lambda j, i: (jnp.maximum((nb - 1 - i) * r8 - 1, 0), cb0 + j)),
                  pl.BlockSpec((CONV_K, B_W), lambda j, i: (0, j)),
                  pl.BlockSpec((R, B_W), lambda j, i: (nb - 1 - i, j))],
        out_specs=[pl.BlockSpec((R, B_W), lambda j, i: (nb - 1 - i, j)),
                   pl.BlockSpec((CONV_K, B_W), lambda j, i: (0, j))],
        out_shape=[jax.ShapeDtypeStruct((T, 3 * B_W), F32), jax.ShapeDtypeStruct((CONV_K, 3 * B_W), F32)],
        scratch_shapes=[pltpu.VMEM((SUBLANES, B_W), F32)],
        compiler_params=_cparams("parallel", "arbitrary"),
    )(proj, proj, conv_w, dy)


DK_SCALE = B_HEAD_DIM ** -0.5


def _tri_inv(a):
    C = a.shape[0]
    ri = lax.broadcasted_iota(jnp.int32, (C, C), 0)
    ci = lax.broadcasted_iota(jnp.int32, (C, C), 1)
    x = jnp.where(ri == ci, 1.0, 0.0) - a
    p = _dotf(a, a)
    span = 2
    while span < C:
        x = x + _dotf(x, p)
        span *= 2
        if span < C:
            p = _dotf(p, p)
    return x


def _dn_chunk(qc, kc, v, gcol, grow, bcol, s0):
    C = B_CHUNK
    ri = lax.broadcasted_iota(jnp.int32, (C, C), 0)
    ci = lax.broadcasted_iota(jnp.int32, (C, C), 1)
    incl, strict = ri >= ci, ri > ci
    rq = lax.rsqrt(jnp.sum(qc * qc, axis=-1, keepdims=True) + NORM_EPS)
    rk = lax.rsqrt(jnp.sum(kc * kc, axis=-1, keepdims=True) + NORM_EPS)
    qn = qc * rq
    q = qn * DK_SCALE
    k = kc * rk
    gc_col = jnp.sum(jnp.where(incl, grow, 0.0), axis=1, keepdims=True)
    gc_row = jnp.sum(jnp.where(ri <= ci, gcol, 0.0), axis=0, keepdims=True)
    gl = jnp.sum(gcol, axis=0, keepdims=True)
    dincl = jnp.where(incl, jnp.exp(jnp.where(incl, gc_col - gc_row, 0.0)), 0.0)
    dstrict = jnp.where(strict, dincl, 0.0)
    eg = jnp.exp(gc_col)
    ekt = jnp.exp(gl - gc_col)
    egl = jnp.exp(gl)
    kb = k * bcol
    vb = v * bcol
    kbg = kb * eg
    a = _dot(kb, k, NT) * dstrict
    tm = _tri_inv(a)
    u = _dot(tm, vb)
    w = _dot(tm, kbg)
    vn = u - _dot(w, s0)
    qk = _dot(q, k, NT) * dincl
    qg = q * eg
    kt = k * ekt
    o = _dot(qg, s0) + _dot(qk, vn)
    s1 = s0 * egl + _dot(kt, vn, TN)
    return dict(rq=rq, rk=rk, qn=qn, q=q, k=k, dincl=dincl, dstrict=dstrict, eg=eg, ekt=ekt, egl=egl, kb=kb, vb=vb,
                kbg=kbg, a=a, tm=tm, w=w, vn=vn, qk=qk, qg=qg, kt=kt, o=o, s1=s1, incl=incl, ri=ri, ci=ci)


def _dn_specs(N, rev):
    ix = (lambda n: N - 1 - n) if rev else (lambda n: n)
    wide = lambda cb: pl.BlockSpec((B_CHUNK, B_W), lambda n: (ix(n), cb))
    return dict(
        q=wide(0), k=wide(1), v=wide(2), z=wide(HP_Z // B_W), dob=wide(A_Q_W // B_W), out=wide(0),
        nw=pl.BlockSpec((1, LANES), lambda n: (0, 0)),
        col=pl.BlockSpec((B_HEADS, None, B_CHUNK, 1), lambda n: (0, ix(n), 0, 0)),
        row=pl.BlockSpec((B_HEADS, None, 1, B_CHUNK), lambda n: (0, ix(n), 0, 0)),
        state=pl.BlockSpec((B_HEADS, None, B_HEAD_DIM, B_HEAD_DIM), lambda n: (0, ix(n), 0, 0)),
    )


def _head(h):
    return slice(h * B_HEAD_DIM, (h + 1) * B_HEAD_DIM)


def _dn_fwd(qkvc, proj, norm_w, gcol, grow, bcol, name):
    T = qkvc.shape[0]
    N = T // B_CHUNK
    sp = _dn_specs(N, False)

    def body(q_ref, k_ref, v_ref, z_ref, nw_ref, gc_ref, gr_ref, bc_ref, o_ref, st_ref, s_ref):
        n = pl.program_id(0)

        @pl.when(n == 0)
        def _():
            s_ref[...] = jnp.zeros_like(s_ref)

        for h in range(B_HEADS):
            s0 = s_ref[h]
            st_ref[h] = s0
            hs = _head(h)
            f = _dn_chunk(q_ref[:, hs], k_ref[:, hs], v_ref[:, hs], gc_ref[h], gr_ref[h], bc_ref[h], s0)
            o = f["o"]
            r = lax.rsqrt(jnp.mean(o * o, axis=-1, keepdims=True) + NORM_EPS)
            o_ref[:, hs] = o * r * nw_ref[...] * _silu(z_ref[:, hs])
            s_ref[h] = f["s1"]

    return pl.pallas_call(
        body, name=name, grid=(N,),
        in_specs=[sp["q"], sp["k"], sp["v"], sp["z"], sp["nw"], sp["col"], sp["row"], sp["col"]],
        out_specs=[sp["out"], sp["state"]],
        out_shape=[jax.ShapeDtypeStruct((T, B_W), F32),
                   jax.ShapeDtypeStruct((B_HEADS, N, B_HEAD_DIM, B_HEAD_DIM), F32)],
        scratch_shapes=[pltpu.VMEM((B_HEADS, B_HEAD_DIM, B_HEAD_DIM), F32)],
        compiler_params=_cparams("arbitrary"),
    )(qkvc, qkvc, qkvc, proj, norm_w, gcol, grow, bcol)


def _dn_bwd(qkvc, proj, norm_w, gcol, grow, bcol, states, dmix, name):
    T = qkvc.shape[0]
    N = T // B_CHUNK
    sp = _dn_specs(N, True)
    C = B_CHUNK

    def body(q_ref, k_ref, v_ref, z_ref, nw_ref, gc_ref, gr_ref, bc_ref, st_ref, dob_ref,
             dq_ref, dk_ref, dv_ref, dz_ref, dg_ref, db_ref, dnw_ref, ds_ref):
        n = pl.program_id(0)

        @pl.when(n == 0)
        def _():
            ds_ref[...] = jnp.zeros_like(ds_ref)
            dnw_ref[...] = jnp.zeros_like(dnw_ref)

        for h in range(B_HEADS):
            one_head(h, q_ref, k_ref, v_ref, z_ref, nw_ref, gc_ref, gr_ref, bc_ref, st_ref, dob_ref,
                     dq_ref, dk_ref, dv_ref, dz_ref, dg_ref, db_ref, dnw_ref, ds_ref)

    def one_head(h, q_ref, k_ref, v_ref, z_ref, nw_ref, gc_ref, gr_ref, bc_ref, st_ref, dob_ref,
                 dq_ref, dk_ref, dv_ref, dz_ref, dg_ref, db_ref, dnw_ref, ds_ref):
        hs = _head(h)
        s0 = st_ref[h]
        ds1 = ds_ref[h]
        v, z, nw, bcol_v = v_ref[:, hs], z_ref[:, hs], nw_ref[...], bc_ref[h]
        f = _dn_chunk(q_ref[:, hs], k_ref[:, hs], v, gc_ref[h], gr_ref[h], bcol_v, s0)
        o, q, k, qn = f["o"], f["q"], f["k"], f["qn"]
        eg, ekt, egl = f["eg"], f["ekt"], f["egl"]
        tm, w, vn, kb, vb, kbg = f["tm"], f["w"], f["vn"], f["kb"], f["vb"], f["kbg"]
        qg, kt, qk, a = f["qg"], f["kt"], f["qk"], f["a"]
        ri, ci = f["ri"], f["ci"]

        dob_v = dob_ref[:, hs]
        r = lax.rsqrt(jnp.mean(o * o, axis=-1, keepdims=True) + NORM_EPS)
        sz = _silu(z)
        on = o * r
        dnw = jnp.sum(dob_v * sz * on, axis=0, keepdims=True)
        dz_ref[:, hs] = dob_v * on * nw * _dsilu(z)
        d_on = dob_v * sz * nw
        do = r * (d_on - on * jnp.mean(d_on * on, axis=-1, keepdims=True))

        dvn = _dot(qk, do, TN) + _dot(kt, ds1)
        dqk = _dot(do, vn, NT)
        dqg = _dot(do, s0, NT)
        ds_ref[h] = _dot(qg, do, TN) + egl * ds1 - _dot(w, dvn, TN)
        dgl = jnp.sum(s0 * ds1, axis=(0, 1), keepdims=True) * egl
        dkt = _dot(vn, ds1, NT)
        dw = -_dot(dvn, s0, NT)
        dq = dqg * eg
        dgc = jnp.sum(dqg * qg, axis=-1, keepdims=True)
        dk = dkt * ekt
        t_kt = jnp.sum(dkt * kt, axis=-1, keepdims=True)
        dgl = dgl + jnp.sum(t_kt, axis=0, keepdims=True)
        dgc = dgc - t_kt
        dqkr = dqk * f["dincl"]
        dq = dq + _dot(dqkr, k)
        dk = dk + _dot(dqkr, q, TN)
        e_qk = dqk * qk
        dgc = dgc + jnp.sum(e_qk, axis=-1, keepdims=True)
        dgc_row = -jnp.sum(e_qk, axis=0, keepdims=True)
        dtm = _dot(dvn, vb, NT) + _dot(dw, kbg, NT)
        dvb = _dot(tm, dvn, TN)
        dkbg = _dot(tm, dw, TN)
        dkb = dkbg * eg
        dgc = dgc + jnp.sum(dkbg * kbg, axis=-1, keepdims=True)
        da = -_dotf(tm, _dotf(dtm, tm, NT), TN)
        dkk = da * f["dstrict"]
        e_a = da * a
        dgc = dgc + jnp.sum(e_a, axis=-1, keepdims=True)
        dgc_row = dgc_row - jnp.sum(e_a, axis=0, keepdims=True)
        dkb = dkb + _dot(dkk, k)
        dk = dk + _dot(dkk, kb, TN)
        dk = dk + dkb * bcol_v
        db_ref[h] = jnp.sum(dkb * k, axis=-1, keepdims=True) + jnp.sum(dvb * v, axis=-1, keepdims=True)
        dv_ref[:, hs] = dvb * bcol_v
        dgc_row = dgc_row + jnp.sum(jnp.where(ri == ci, dgc, 0.0), axis=0, keepdims=True)
        dg_ref[h] = jnp.sum(jnp.where(ci >= ri, dgc_row, 0.0), axis=1, keepdims=True) + dgl
        dqs = dq * DK_SCALE
        dq_ref[:, hs] = f["rq"] * (dqs - qn * jnp.sum(dqs * qn, axis=-1, keepdims=True))
        dk_ref[:, hs] = f["rk"] * (dk - k * jnp.sum(dk * k, axis=-1, keepdims=True))
        dnw_ref[h] += dnw

    return pl.pallas_call(
        body, name=name, grid=(N,),
        in_specs=[sp["q"], sp["k"], sp["v"], sp["z"], sp["nw"], sp["col"], sp["row"], sp["col"], sp["state"], sp["dob"]],
        out_specs=[sp["out"], sp["out"], sp["out"], sp["out"], sp["col"], sp["col"],
                   pl.BlockSpec((B_HEADS, 1, LANES), lambda n: (0, 0, 0))],
        out_shape=[jax.ShapeDtypeStruct((T, B_W), F32)] * 4
        + [jax.ShapeDtypeStruct((B_HEADS, N, C, 1), F32), jax.ShapeDtypeStruct((B_HEADS, N, C, 1), F32),
           jax.ShapeDtypeStruct((B_HEADS, 1, LANES), F32)],
        scratch_shapes=[pltpu.VMEM((B_HEADS, B_HEAD_DIM, B_HEAD_DIM), F32)],
        compiler_params=_cparams("arbitrary"),
    )(qkvc, qkvc, qkvc, proj, norm_w, gcol, grow, bcol, states, dmix)


def _lru_gates(xc, wa_ref, wx_ref, ba, bx, sp):
    pre_r, pre_i = [], []
    for hb in range(LRU_BLOCKS):
        xb = xc[:, hb * LRU_BLOCK_W:(hb + 1) * LRU_BLOCK_W]
        pre_r.append(_dot(xb, wa_ref[hb]))
        pre_i.append(_dot(xb, wx_ref[hb]))
    r = _sigmoid(jnp.concatenate(pre_r, axis=1) + ba)
    i = _sigmoid(jnp.concatenate(pre_i, axis=1) + bx)
    la = -LRU_C * r * sp
    a = jnp.exp(la)
    th = jnp.tanh(la)
    s = jnp.sqrt(-2.0 * th / (1.0 - th))
    return r, i, a, s


def _scan_down(a, b):
    R = a.shape[0]
    row = lax.broadcasted_iota(jnp.int32, a.shape, 0)
    d = 1
    while d < R:
        ok = row >= d
        b = a * jnp.where(ok, pltpu.roll(b, d, axis=0), 0.0) + b
        a = a * jnp.where(ok, pltpu.roll(a, d, axis=0), 1.0)
        d *= 2
    return a, b


def _scan_up(a, b):
    R = a.shape[0]
    row = lax.broadcasted_iota(jnp.int32, a.shape, 0)
    d = 1
    while d < R:
        ok = row < R - d
        b = a * jnp.where(ok, pltpu.roll(b, R - d, axis=0), 0.0) + b
        a = a * jnp.where(ok, pltpu.roll(a, R - d, axis=0), 1.0)
        d *= 2
    return b


def _rglru_fwd(proj, conv_w, conv_b, wa, wx, ba, bx, sp, name):
    T = proj.shape[0]
    R = _tile(T, 256)
    W = D_MODEL

    def body(p_ref, cw_ref, cb_ref, wa_ref, wx_ref, ba_ref, bx_ref, sp_ref, hg_ref, h_ref, prev_ref, hc_ref):
        i = pl.program_id(0)

        @pl.when(i == 0)
        def _():
            prev_ref[...] = jnp.zeros_like(prev_ref)
            hc_ref[...] = jnp.zeros_like(hc_ref)

        xr = p_ref[:, :W]
        gate = p_ref[:, W:]
        xc = _conv(xr, prev_ref[...], cw_ref[...]) + cb_ref[...]
        prev_ref[...] = xr[R - SUBLANES:]
        r, ig, a, s = _lru_gates(xc, wa_ref, wx_ref, ba_ref[...], bx_ref[...], sp_ref[...])
        pa, hb = _scan_down(a, s * ig * xc)
        h = hb + pa * hc_ref[SUBLANES - 1:SUBLANES, :]
        h_ref[...] = h
        hg_ref[...] = h * _gelu(gate)
        hc_ref[...] = h[R - SUBLANES:]

    vec = pl.BlockSpec((1, W), lambda i: (0, 0))
    wsp = pl.BlockSpec((LRU_BLOCKS, LRU_BLOCK_W, LRU_BLOCK_W), lambda i: (0, 0, 0))
    row = pl.BlockSpec((R, W), lambda i: (i, 0))
    return pl.pallas_call(
        body, name=name, grid=(T // R,),
        in_specs=[pl.BlockSpec((R, 2 * W), lambda i: (i, 0)), pl.BlockSpec((CONV_K, W), lambda i: (0, 0)),
                  vec, wsp, wsp, vec, vec, vec],
        out_specs=[row, row],
        out_shape=[jax.ShapeDtypeStruct((T, W), F32)] * 2,
        scratch_shapes=[pltpu.VMEM((SUBLANES, W), F32), pltpu.VMEM((SUBLANES, W), F32)],
        compiler_params=_cparams("arbitrary"),
    )(proj, conv_w, conv_b, wa, wx, ba, bx, sp)


def _rglru_bwd(proj, conv_w, conv_b, wa, wx, ba, bx, sp, h, dhg, name):
    T = proj.shape[0]
    R = _tile(T, 256)
    nb = T // R
    r8 = R // SUBLANES
    W = D_MODEL

    def body(p_ref, pp_ref, cw_ref, cb_ref, wa_ref, wx_ref, ba_ref, bx_ref, sp_ref, h_ref, hp_ref, dhg_ref,
             dp_ref, dcw_ref, dcb_ref, dwa_ref, dwx_ref, dba_ref, dbx_ref, dsp_ref, lam_ref, nxt_ref):
        step = pl.program_id(0)
        blk = nb - 1 - step

        @pl.when(step == 0)
        def _():
            lam_ref[...] = jnp.zeros_like(lam_ref)
            nxt_ref[...] = jnp.zeros_like(nxt_ref)

        xr = p_ref[:, :W]
        gate = p_ref[:, W:]
        first = blk > 0
        prev8 = jnp.where(first, pp_ref[:, :W], 0.0)
        hprev8 = jnp.where(first, hp_ref[...], 0.0)
        cw = cw_ref[...]
        spv = sp_ref[...]
        xc = _conv(xr, prev8, cw) + cb_ref[...]
        r, ig, a, s = _lru_gates(xc, wa_ref, wx_ref, ba_ref[...], bx_ref[...], spv)
        hv = h_ref[...]
        dhg_v = dhg_ref[...]
        dgate = dhg_v * hv * _dgelu(gate)
        dh = dhg_v * _gelu(gate)
        row = lax.broadcasted_iota(jnp.int32, (R, W), 0)
        last = row == R - 1
        a_up = jnp.where(last, 0.0, pltpu.roll(a, R - 1, axis=0))
        lam = _scan_up(a_up, dh + jnp.where(last, lam_ref[0:1, :], 0.0))
        lam_ref[...] = (a * lam)[:SUBLANES]
        h_dn = _shift_down(hv, hprev8, 1)
        da = lam * h_dn
        bx_in = ig * xc
        dsv = lam * bx_in
        dig = lam * s * xc
        dxc = lam * s * ig
        dla = da * a - dsv * (a * a) / s
        dr = dla * (-LRU_C) * spv
        dsp = jnp.sum(dla * (-LRU_C) * r, axis=0, keepdims=True)
        dpr = dr * r * (1.0 - r)
        dpi = dig * ig * (1.0 - ig)
        dxc_parts, dwa_parts, dwx_parts = [], [], []
        for hb in range(LRU_BLOCKS):
            sl = slice(hb * LRU_BLOCK_W, (hb + 1) * LRU_BLOCK_W)
            xb, gr, gi = xc[:, sl], dpr[:, sl], dpi[:, sl]
            dxc_parts.append(_dot(gr, wa_ref[hb], NT) + _dot(gi, wx_ref[hb], NT))
            dwa_parts.append(_dot(xb, gr, TN))
            dwx_parts.append(_dot(xb, gi, TN))
        dxc = dxc + jnp.concatenate(dxc_parts, axis=1)
        dxr, dcw = _conv_bwd(xr, prev8, cw, dxc, nxt_ref[...])
        nxt_ref[...] = dxc[:SUBLANES]
        dp_ref[:, :W] = dxr
        dp_ref[:, W:] = dgate
        dcb = jnp.sum(dxc, axis=0, keepdims=True)
        dba = jnp.sum(dpr, axis=0, keepdims=True)
        dbx = jnp.sum(dpi, axis=0, keepdims=True)

        @pl.when(step == 0)
        def _():
            for j in range(CONV_K):
                dcw_ref[j:j + 1, :] = dcw[j]
            dcb_ref[...] = dcb
            dba_ref[...] = dba
            dbx_ref[...] = dbx
            dsp_ref[...] = dsp
            for hb in range(LRU_BLOCKS):
                dwa_ref[hb] = dwa_parts[hb]
                dwx_ref[hb] = dwx_parts[hb]

        @pl.when(step > 0)
        def _():
            for j in range(CONV_K):
                dcw_ref[j:j + 1, :] += dcw[j]
            dcb_ref[...] += dcb
            dba_ref[...] += dba
            dbx_ref[...] += dbx
            dsp_ref[...] += dsp
            for hb in range(LRU_BLOCKS):
                dwa_ref[hb] += dwa_parts[hb]
                dwx_ref[hb] += dwx_parts[hb]

    rv = lambda i: nb - 1 - i
    before = lambda i: jnp.maximum((nb - 1 - i) * r8 - 1, 0)
    vec = pl.BlockSpec((1, W), lambda i: (0, 0))
    cws = pl.BlockSpec((CONV_K, W), lambda i: (0, 0))
    wsp = pl.BlockSpec((LRU_BLOCKS, LRU_BLOCK_W, LRU_BLOCK_W), lambda i: (0, 0, 0))
    row = pl.BlockSpec((R, W), lambda i: (rv(i), 0))
    wshape = jax.ShapeDtypeStruct((LRU_BLOCKS, LRU_BLOCK_W, LRU_BLOCK_W), F32)
    vshape = jax.ShapeDtypeStruct((1, W), F32)
    return pl.pallas_call(
        body, name=name, grid=(nb,),
        in_specs=[pl.BlockSpec((R, 2 * W), lambda i: (rv(i), 0)), pl.BlockSpec((SUBLANES, 2 * W), lambda i: (before(i), 0)),
                  cws, vec, wsp, wsp, vec, vec, vec, row, pl.BlockSpec((SUBLANES, W), lambda i: (before(i), 0)), row],
        out_specs=[pl.BlockSpec((R, 2 * W), lambda i: (rv(i), 0)), cws, vec, wsp, wsp, vec, vec, vec],
        out_shape=[jax.ShapeDtypeStruct((T, 2 * W), F32), jax.ShapeDtypeStruct((CONV_K, W), F32), vshape,
                   wshape, wshape, vshape, vshape, vshape],
        scratch_shapes=[pltpu.VMEM((SUBLANES, W), F32), pltpu.VMEM((SUBLANES, W), F32)],
        compiler_params=_cparams("arbitrary"),
    )(proj, proj, conv_w, conv_b, wa, wx, ba, bx, sp, h, h, dhg)


MESH = pl.DeviceIdType.MESH
ANY = pl.BlockSpec(memory_space=pl.ANY)


def _position():
    x, y, c = lax.axis_index("x"), lax.axis_index("y"), lax.axis_index("c")
    other_chips = [(1 - x, y), (x, 1 - y), (1 - x, 1 - y)]
    return x, y, c, other_chips


def _all_gather_weights(shards, name):
    n = len(shards)

    def body(*refs):
        ins, outs = refs[:n], refs[n:2 * n]
        send_sems, recv_sems = refs[2 * n:]
        x, y, c, chips = _position()
        me = 2 * x + y
        sibling = (x, y, 1 - c)

        def rcopy(t, k, src, dst, to):
            return pltpu.make_async_remote_copy(src_ref=src, dst_ref=dst, send_sem=send_sems.at[t, k],
                                                recv_sem=recv_sems.at[t, k], device_id=to, device_id_type=MESH)

        started = []
        for t in range(n):
            for j, (cx, cy) in enumerate(chips):
                cp = rcopy(t, j, ins[t].at[c], outs[t].at[me, c], (cx, cy, c))
                cp.start()
                started.append(cp)
        for t in range(n):
            for j, (cx, cy) in enumerate(chips):
                blk = outs[t].at[2 * cx + cy, c]
                rcopy(t, j, blk, blk, (cx, cy, c)).wait_recv()
                cp = rcopy(t, 3 + j, blk, blk, sibling)
                cp.start()
                started.append(cp)
        for t in range(n):
            for j, (cx, cy) in enumerate(chips):
                blk = outs[t].at[2 * cx + cy, 1 - c]
                rcopy(t, 3 + j, blk, blk, sibling).wait_recv()
        for cp in started:
            cp.wait_send()

    return pl.pallas_call(
        body, name=name, in_specs=[ANY] * n, out_specs=[ANY] * n,
        out_shape=[jax.ShapeDtypeStruct((N_CHIPS,) + s.shape, s.dtype) for s in shards],
        scratch_shapes=[pltpu.SemaphoreType.DMA((n, 6)), pltpu.SemaphoreType.DMA((n, 6))],
    )(*shards)


def _rs_to_sibling(grads, name):
    n = len(grads)

    def body(*refs):
        ins, outs = refs[:n], refs[n:2 * n]
        send_sems, recv_sems = refs[2 * n:]
        x, y, c, _ = _position()
        cps = [pltpu.make_async_remote_copy(src_ref=ins[t].at[:, 1 - c], dst_ref=outs[t], send_sem=send_sems.at[t],
                                            recv_sem=recv_sems.at[t], device_id=(x, y, 1 - c), device_id_type=MESH)
               for t in range(n)]
        for cp in cps:
            cp.start()
        for cp in cps:
            cp.wait()

    return pl.pallas_call(
        body, name=name, in_specs=[ANY] * n, out_specs=[ANY] * n,
        out_shape=[jax.ShapeDtypeStruct((N_CHIPS,) + g.shape[2:], g.dtype) for g in grads],
        scratch_shapes=[pltpu.SemaphoreType.DMA((n,)), pltpu.SemaphoreType.DMA((n,))],
    )(*grads)


def _rs_across_chips(parts, name):
    n = len(parts)

    def body(*refs):
        ins, outs = refs[:n], refs[n:2 * n]
        send_sems, recv_sems = refs[2 * n:]
        x, y, c, chips = _position()
        me = 2 * x + y
        cps = []
        for t in range(n):
            for j, (cx, cy) in enumerate(chips):
                cps.append(pltpu.make_async_remote_copy(
                    src_ref=ins[t].at[2 * cx + cy], dst_ref=outs[t].at[me], send_sem=send_sems.at[t, j],
                    recv_sem=recv_sems.at[t, j], device_id=(cx, cy, c), device_id_type=MESH))
        for cp in cps:
            cp.start()
        for t in range(n):
            for j, (cx, cy) in enumerate(chips):
                blk = outs[t].at[2 * cx + cy]
                pltpu.make_async_remote_copy(src_ref=blk, dst_ref=blk, send_sem=send_sems.at[t, j],
                                             recv_sem=recv_sems.at[t, j], device_id=(cx, cy, c),
                                             device_id_type=MESH).wait_recv()
        for cp in cps:
            cp.wait_send()

    return pl.pallas_call(
        body, name=name, in_specs=[ANY] * n, out_specs=[ANY] * n,
        out_shape=[jax.ShapeDtypeStruct(p.shape, p.dtype) for p in parts],
        scratch_shapes=[pltpu.SemaphoreType.DMA((n, 3)), pltpu.SemaphoreType.DMA((n, 3))],
    )(*parts)


def _rs_join_halves(halves, name):
    n = len(halves)

    def body(*refs):
        ins, outs = refs[:n], refs[n:2 * n]
        send_sems, recv_sems = refs[2 * n:]
        x, y, c, _ = _position()
        cps = [pltpu.make_async_remote_copy(src_ref=ins[t].at[c], dst_ref=outs[t].at[c], send_sem=send_sems.at[t],
                                            recv_sem=recv_sems.at[t], device_id=(x, y, 1 - c), device_id_type=MESH)
               for t in range(n)]
        for cp in cps:
            cp.start()
        for t in range(n):
            blk = outs[t].at[1 - c]
            pltpu.make_async_remote_copy(src_ref=blk, dst_ref=blk, send_sem=send_sems.at[t], recv_sem=recv_sems.at[t],
                                         device_id=(x, y, 1 - c), device_id_type=MESH).wait_recv()
        for cp in cps:
            cp.wait_send()

    return pl.pallas_call(
        body, name=name, in_specs=[ANY] * n, out_specs=[ANY] * n,
        out_shape=[jax.ShapeDtypeStruct(h.shape, h.dtype) for h in halves],
        input_output_aliases={t: t for t in range(n)},
        scratch_shapes=[pltpu.SemaphoreType.DMA((n,)), pltpu.SemaphoreType.DMA((n,))],
    )(*halves)


def _all_gather_small(block, name):
    m_per, n = block.shape

    def body(x_ref, out_ref, send_sems, recv_sems, local_sem):
        x, y, c, chips = _position()
        me, sibling = (x, y, c), (x, y, 1 - c)

        def rows(px, py, pc):
            return out_ref.at[pl.ds((4 * px + 2 * py + pc) * m_per, m_per), :]

        def copy(k, blk, to, src=None):
            return pltpu.make_async_remote_copy(
                src_ref=rows(*blk) if src is None else src, dst_ref=rows(*blk), send_sem=send_sems.at[k],
                recv_sem=recv_sems.at[k], device_id=to, device_id_type=MESH)

        mine = pltpu.make_async_copy(x_ref, rows(*me), local_sem)
        mine.start()
        first = [copy(0, me, sibling, src=x_ref)]
        first += [copy(1 + j, me, (*chip, c), src=x_ref) for j, chip in enumerate(chips)]
        for cp in first:
            cp.start()
        passed = [copy(4 + j, (*chip, c), sibling) for j, chip in enumerate(chips)]
        for j, chip in enumerate(chips):
            copy(1 + j, (*chip, c), me).wait_recv()
            passed[j].start()
        copy(0, sibling, me).wait_recv()
        for j, chip in enumerate(chips):
            copy(4 + j, (*chip, 1 - c), me).wait_recv()
        for cp in first + passed:
            cp.wait_send()
        mine.wait()

    return pl.pallas_call(
        body, name=name, out_shape=jax.ShapeDtypeStruct((N_DEV * m_per, n), block.dtype),
        in_specs=[pl.BlockSpec(memory_space=pltpu.VMEM)], out_specs=pl.BlockSpec(memory_space=pltpu.VMEM),
        scratch_shapes=[pltpu.SemaphoreType.DMA((7,)), pltpu.SemaphoreType.DMA((7,)), pltpu.SemaphoreType.DMA],
    )(block)


def _row_tile(R, n):
    budget = 1 << 19
    if R * n <= budget or R % SUBLANES:
        return R
    t = R
    while t * n > budget and t % (2 * SUBLANES) == 0:
        t //= 2
    return t


def _pair_sum(g, recv, c_arr, name):
    _, _, R, n = g.shape
    tr = _row_tile(R, n)

    def body(c_ref, g_ref, r_ref, o_ref):
        o_ref[...] = g_ref[...] + r_ref[...]

    grid_spec = pltpu.PrefetchScalarGridSpec(
        num_scalar_prefetch=1, grid=(N_CHIPS, R // tr),
        in_specs=[pl.BlockSpec((None, None, tr, n), lambda p, i, c: (p, c[0], i, 0)),
                  pl.BlockSpec((None, tr, n), lambda p, i, c: (p, i, 0))],
        out_specs=pl.BlockSpec((None, tr, n), lambda p, i, c: (p, i, 0)))
    return pl.pallas_call(
        body, name=name, grid_spec=grid_spec, out_shape=jax.ShapeDtypeStruct(recv.shape, F32),
        compiler_params=_cparams("parallel", "parallel"),
    )(c_arr, g, recv)


def _chip_sum(recv, own, where, name):
    _, R, n = recv.shape
    tr = _row_tile(R, n)

    def body(w_ref, r0, r1, r2, r3, own_ref, o_ref):
        me = w_ref[0]
        terms = [jnp.where(me == k, own_ref[...], r[...]) for k, r in enumerate((r0, r1, r2, r3))]
        o_ref[...] = ((terms[0] + terms[1]) + terms[2]) + terms[3]

    def slot(k):
        return pl.BlockSpec((None, tr, n), lambda i, w: (w[1 + k], i, 0))

    grid_spec = pltpu.PrefetchScalarGridSpec(
        num_scalar_prefetch=1, grid=(R // tr,),
        in_specs=[slot(0), slot(1), slot(2), slot(3), pl.BlockSpec((None, tr, n), lambda i, w: (w[0], i, 0))],
        out_specs=pl.BlockSpec((None, tr, n), lambda i, w: (w[5], i, 0)))
    return pl.pallas_call(
        body, name=name, grid_spec=grid_spec, out_shape=jax.ShapeDtypeStruct((2, R, n), F32),
        compiler_params=_cparams("parallel"),
    )(where, recv, recv, recv, recv, own)


ADAM_C1 = 1.0 / (1.0 - ADAM_B1 ** ADAM_STEP)
ADAM_C2 = 1.0 / (1.0 - ADAM_B2 ** ADAM_STEP)


def _adamw_math(w, g, m, v):
    m = ADAM_B1 * m + (1.0 - ADAM_B1) * g
    v = ADAM_B2 * v + (1.0 - ADAM_B2) * (g * g)
    delta = -ADAM_LR * ((m * ADAM_C1) / (jnp.sqrt(v * ADAM_C2) + ADAM_EPS) + ADAM_WD * w)
    return delta, m, v


def _adamw(w, g, m, v, name):
    R, n = w.shape
    tr = _row_tile(R, n)

    def body(w_ref, g_ref, m_ref, v_ref, d_ref, nm_ref, nv_ref):
        d_ref[...], nm_ref[...], nv_ref[...] = _adamw_math(w_ref[...], g_ref[...], m_ref[...], v_ref[...])

    spec = pl.BlockSpec((tr, n), lambda i: (i, 0))
    return pl.pallas_call(
        body, name=name, grid=(R // tr,), in_specs=[spec] * 4, out_specs=[spec] * 3,
        out_shape=[jax.ShapeDtypeStruct((R, n), F32)] * 3, compiler_params=_cparams("parallel"),
    )(w, g, m, v)


def _adamw_small(w, gall, m, v, name):
    M, n = w.shape

    def body(w_ref, g_ref, m_ref, v_ref, gs_ref, d_ref, nm_ref, nv_ref):
        g = g_ref[0:M, :]
        for d in range(1, N_DEV):
            g = g + g_ref[d * M:(d + 1) * M, :]
        gs_ref[...] = g
        d_ref[...], nm_ref[...], nv_ref[...] = _adamw_math(w_ref[...], g, m_ref[...], v_ref[...])

    return pl.pallas_call(
        body, name=name, out_shape=[jax.ShapeDtypeStruct((M, n), F32)] * 4,
    )(w, gall, m, v)


SMALL_ROWS = 24
MID_ROWS = 4


def _pack_small(ln1_g, ln1_b, ln2_g, ln2_b, norm_w, sinks, a_log, dt_bias):
    mixed = jnp.concatenate([norm_w.reshape(-1), sinks.reshape(-1), a_log.reshape(-1), dt_bias.reshape(-1)])
    mixed = jnp.pad(mixed, (0, D_MODEL - mixed.shape[0]))[None]
    pad = jnp.zeros((SMALL_ROWS - 4 * DEPTH - 1, D_MODEL), F32)
    return jnp.concatenate([ln1_g, ln1_b, ln2_g, ln2_b, mixed, pad], axis=0)


def _unpack_small(p):
    mixed = p[4 * DEPTH]
    return (p[0:4], p[4:8], p[8:12], p[12:16], mixed[0:256].reshape(2, 128), mixed[256:272].reshape(2, 8),
            mixed[272:280].reshape(2, 4), mixed[280:288].reshape(2, 4))


def _pack_mid(conv_w, rconv_w, rconv_b, b_a, b_x, lam):
    flat = jnp.concatenate([conv_w.reshape(2, -1), rconv_w.reshape(2, -1), rconv_b, b_a, b_x, lam], axis=1)
    return jnp.pad(flat, ((0, 0), (0, MID_ROWS * D_MODEL - flat.shape[1]))).reshape(2, MID_ROWS, D_MODEL)


def _unpack_mid(p):
    lead = p.shape[:-2]
    f = p.reshape(lead + (MID_ROWS * D_MODEL,))
    return (f[..., 0:1536].reshape(lead + (4, 384)), f[..., 1536:2560].reshape(lead + (4, 256)),
            f[..., 2560:2816], f[..., 2816:3072], f[..., 3072:3328], f[..., 3328:3584])


def _cols_from_chips(g):
    p, L, R, n = g.shape
    return g.transpose(1, 2, 0, 3).reshape(L, R, p * n)


def _rows_from_chips(g):
    p, L, R, n = g.shape
    return g.transpose(1, 0, 2, 3).reshape(L, p * R, n)


def _cols_to_chips(g):
    L, R, n4 = g.shape
    return g.reshape(L, R, N_CHIPS, n4 // N_CHIPS).transpose(2, 0, 1, 3)


def _rows_to_chips(g):
    L, R4, n = g.shape
    return g.reshape(L, N_CHIPS, R4 // N_CHIPS, n).transpose(1, 0, 2, 3)


def _halves(a):
    return a.reshape(2, -1, a.shape[-1])


def _pad_hyb_cols(w):
    z = jnp.zeros(w.shape[:-1] + (HYB_PAD - HP_BG - 2 * B_HEADS,), w.dtype)
    return jnp.concatenate([w[..., 0:512], w[..., 768:2304], w[..., 2304:2816], w[..., 512:768], w[..., 2816:2824], z], axis=-1)


def _unpad_hyb_cols(w):
    return jnp.concatenate([w[..., 0:512], w[..., 2560:2816], w[..., 512:2048], w[..., 2048:2560], w[..., 2816:2824]], axis=-1)


def _hybrid_fwd(x, W, j, tables, sfx):
    cos, sin_s = tables
    T = x.shape[0]
    N = T // B_CHUNK
    proj = _matmul(x, W["hyb_w_in"][j], "nn", "hyb_in" + sfx)
    sinks_b = jnp.broadcast_to(W["hyb_sinks"][j][:, None], (A_Q_HEADS, LANES))
    o_a, lse = _attn_fwd(proj, cos, sin_s, sinks_b, "attn_fwd" + sfx)
    qkvc = _dnconv_fwd(proj, W["hyb_conv_w"][j], "dnconv_fwd" + sfx)
    bg = proj[:, HP_BG:HP_BG + 2 * B_HEADS]
    beta = jax.nn.sigmoid(bg[:, :B_HEADS])
    pre = bg[:, B_HEADS:] + W["hyb_dt_bias"][j][None]
    g = -jnp.exp(W["hyb_a_log"][j])[None] * jax.nn.softplus(pre)
    gcol = g.T.reshape(B_HEADS, N, B_CHUNK, 1)
    grow = g.T.reshape(B_HEADS, N, 1, B_CHUNK)
    bcol = beta.T.reshape(B_HEADS, N, B_CHUNK, 1)
    nw = W["hyb_norm_w"][j][None]
    o_b, states = _dn_fwd(qkvc, proj, nw, gcol, grow, bcol, "dn_fwd" + sfx)
    mix = jnp.concatenate([o_a, o_b], axis=1)
    y = _matmul(mix, W["hyb_w_out"][j], "nn", "hyb_out" + sfx)
    res = dict(proj=proj, o_a=o_a, lse=lse, qkvc=qkvc, beta=beta, pre=pre, g=g, gcol=gcol, grow=grow, bcol=bcol,
               states=states, mix=mix, sinks_b=sinks_b, nw=nw)
    return y, res


def _hybrid_bwd(x, du, W, j, res, tables, sfx):
    cos, sin_s = tables
    T = x.shape[0]
    proj = res["proj"]
    d_wout = _matmul(res["mix"], du, "tn", "hyb_out_dw" + sfx)
    dmix = _matmul(du, W["hyb_w_out"][j], "nt", "hyb_out_dx" + sfx)
    dq, dkc, dkp, dvc, dvp, dsink = _attn_bwd(proj, cos, sin_s, res["sinks_b"], res["o_a"], res["lse"], dmix,
                                               "attn_bwd" + sfx)
    zpad = jnp.zeros((WINDOW, LANES), F32)
    dk = dkc + jnp.concatenate([dkp[WINDOW:], zpad], axis=0)
    dv = dvc + jnp.concatenate([dvp[WINDOW:], zpad], axis=0)
    dqc, dkcv, dvcv, dz, dg4, dbeta4, dnw = _dn_bwd(res["qkvc"], proj, res["nw"], res["gcol"], res["grow"], res["bcol"],
                                                    res["states"], dmix, "dn_bwd" + sfx)
    dqkvc = jnp.concatenate([dqc, dkcv, dvcv], axis=1)
    dqkvb, dconv = _dnconv_bwd(proj, W["hyb_conv_w"][j], dqkvc, "dnconv_bwd" + sfx)
    dg = dg4.reshape(B_HEADS, T).T
    dbeta = dbeta4.reshape(B_HEADS, T).T
    beta = res["beta"]
    dbeta_logit = dbeta * beta * (1.0 - beta)
    da_logit = dg * (-jnp.exp(W["hyb_a_log"][j]))[None] * jax.nn.sigmoid(res["pre"])
    d_dt_bias = jnp.sum(da_logit, axis=0)
    d_a_log = jnp.sum(dg * res["g"], axis=0)
    zcols = jnp.zeros((T, HYB_PAD - HP_BG - 2 * B_HEADS), F32)
    dproj = jnp.concatenate([dq, dqkvb, dz, dk, dv, dbeta_logit, da_logit, zcols], axis=1)
    d_win = _matmul(x, dproj, "tn", "hyb_in_dw" + sfx)
    dx = _matmul(dproj, W["hyb_w_in"][j], "nt", "hyb_in_dx" + sfx, epi=_epi_add_residual, extra=du)
    grads = dict(hyb_w_in=d_win, hyb_w_out=d_wout, hyb_sinks=dsink[0], hyb_conv_w=dconv, hyb_a_log=d_a_log,
                 hyb_dt_bias=d_dt_bias, hyb_norm_w=jnp.sum(dnw[:, 0, :], axis=0))
    return dx, grads


def _rec_fwd(x, W, j, sfx):
    proj = _matmul(x, W["rec_w_in"][j], "nn", "rec_in" + sfx)
    sp = jax.nn.softplus(-W["rec_lambda"][j])[None]
    hg, h = _rglru_fwd(proj, W["rec_conv_w"][j], W["rec_conv_b"][j][None], W["rec_w_a"][j], W["rec_w_x"][j],
                       W["rec_b_a"][j][None], W["rec_b_x"][j][None], sp, "rglru_fwd" + sfx)
    y = _matmul(hg, W["rec_w_out"][j], "nn", "rec_out" + sfx)
    return y, dict(proj=proj, hg=hg, h=h, sp=sp)


def _rec_bwd(x, du, W, j, res, sfx):
    d_wout = _matmul(res["hg"], du, "tn", "rec_out_dw" + sfx)
    dhg = _matmul(du, W["rec_w_out"][j], "nt", "rec_out_dx" + sfx)
    dproj, dcw, dcb, dwa, dwx, dba, dbx, dsp = _rglru_bwd(
        res["proj"], W["rec_conv_w"][j], W["rec_conv_b"][j][None], W["rec_w_a"][j], W["rec_w_x"][j],
        W["rec_b_a"][j][None], W["rec_b_x"][j][None], res["sp"], res["h"], dhg, "rglru_bwd" + sfx)
    d_lam = dsp[0] * (-jax.nn.sigmoid(-W["rec_lambda"][j]))
    d_win = _matmul(x, dproj, "tn", "rec_in_dw" + sfx)
    dx = _matmul(dproj, W["rec_w_in"][j], "nt", "rec_in_dx" + sfx, epi=_epi_add_residual, extra=du)
    grads = dict(rec_w_in=d_win, rec_w_out=d_wout, rec_conv_w=dcw, rec_conv_b=dcb[0], rec_w_a=dwa, rec_w_x=dwx,
                 rec_b_a=dba[0], rec_b_x=dbx[0], rec_lambda=d_lam)
    return dx, grads


def _local_step(x, tgt, W):
    T = x.shape[0]
    tables = _rope_tables(T)
    acts = []
    for layer in range(DEPTH):
        j, sfx = layer // 2, ""
        if layer % 2 == 0:
            y, res = _hybrid_fwd(x, W, j, tables, sfx)
        else:
            y, res = _rec_fwd(x, W, j, sfx)
        x1 = _ln_fwd(x, y, W["ln1_g"][layer][None], W["ln1_b"][layer][None], "ln_fwd")
        h1 = _matmul(x1, W["mlp_w1"][layer], "nn", "mlp_up")
        y2 = _matmul(h1, W["mlp_w2"][layer], "nn", "mlp_down", a_fn=_relu2)
        x2 = _ln_fwd(x1, y2, W["ln2_g"][layer][None], W["ln2_b"][layer][None], "ln_fwd")
        acts.append(dict(x=x, y=y, res=res, x1=x1, h1=h1, y2=y2))
        x = x2
    dx, loss = _loss_head(x, tgt, "loss_head")
    per_layer = [None] * DEPTH
    for layer in reversed(range(DEPTH)):
        j, a = layer // 2, acts[layer]
        du2, dg2, db2 = _ln_bwd(a["x1"], a["y2"], W["ln2_g"][layer][None], dx, "ln_bwd")
        d_w2 = _matmul(a["h1"], du2, "tn", "mlp_down_dw", a_fn=_relu2)
        dh1 = _matmul(du2, W["mlp_w2"][layer], "nt", "mlp_down_dx", epi=_epi_drelu2, extra=a["h1"])
        d_w1 = _matmul(a["x1"], dh1, "tn", "mlp_up_dw")
        dx1 = _matmul(dh1, W["mlp_w1"][layer], "nt", "mlp_up_dx", epi=_epi_add_residual, extra=du2)
        du1, dg1, db1 = _ln_bwd(a["x"], a["y"], W["ln1_g"][layer][None], dx1, "ln_bwd")
        if layer % 2 == 0:
            dx, g = _hybrid_bwd(a["x"], du1, W, j, a["res"], tables, "")
        else:
            dx, g = _rec_bwd(a["x"], du1, W, j, a["res"], "")
        g.update(ln1_g=dg1[0], ln1_b=db1[0], ln2_g=dg2[0], ln2_b=db2[0], mlp_w1=d_w1, mlp_w2=d_w2)
        per_layer[layer] = g
    grads = {}
    for name in ("ln1_g", "ln1_b", "ln2_g", "ln2_b", "mlp_w1", "mlp_w2"):
        grads[name] = jnp.stack([per_layer[l][name] for l in range(DEPTH)])
    for name in per_layer[0]:
        if name not in grads:
            grads[name] = jnp.stack([per_layer[l][name] for l in (0, 2)])
    for name in per_layer[1]:
        if name not in grads:
            grads[name] = jnp.stack([per_layer[l][name] for l in (1, 3)])
    return loss, dx, grads


BIG = ("hyb_w_in", "hyb_w_out", "rec_w_in", "rec_w_out", "mlp_w1", "mlp_w2", "rec_w_a", "rec_w_x")
COL_SHARDED = ("hyb_w_in", "rec_w_in", "mlp_w1")
MID = ("hyb_conv_w", "rec_conv_w", "rec_conv_b", "rec_b_a", "rec_b_x", "rec_lambda")
SMALL = ("ln1_g", "ln1_b", "ln2_g", "ln2_b", "hyb_norm_w", "hyb_sinks", "hyb_a_log", "hyb_dt_bias")
WEIGHTS = ("hyb_w_in", "hyb_sinks", "hyb_conv_w", "hyb_a_log", "hyb_dt_bias", "hyb_norm_w", "hyb_w_out", "rec_w_in",
           "rec_conv_w", "rec_conv_b", "rec_w_a", "rec_b_a", "rec_w_x", "rec_b_x", "rec_lambda", "rec_w_out", "ln1_g",
           "ln1_b", "mlp_w1", "mlp_w2", "ln2_g", "ln2_b")


def _gather_full_weights(w):
    shards = [_halves(w[k].astype(MXU_DTYPE)) for k in BIG]
    shards.append(_pack_mid(*[w[k] for k in MID]))
    got = _all_gather_weights(shards, "all_gather_weights")
    me = 2 * lax.axis_index("x") + lax.axis_index("y")
    chip = lax.broadcasted_iota(jnp.int32, (N_CHIPS, 1, 1, 1), 0)
    got = [jnp.where(chip == me, s[None], g) for s, g in zip(shards, got)]
    W = {}
    for k, g in zip(BIG, got[:-1]):
        L = w[k].shape[0]
        if k in ("rec_w_a", "rec_w_x"):
            g5 = g.reshape((N_CHIPS,) + w[k].shape)
            W[k] = g5.transpose(1, 2, 0, 3, 4).reshape(L, LRU_BLOCKS, LRU_BLOCK_W, LRU_BLOCK_W)
        else:
            g4 = g.reshape((N_CHIPS,) + w[k].shape)
            W[k] = _cols_from_chips(g4) if k in COL_SHARDED else _rows_from_chips(g4)
    W["hyb_w_in"] = _pad_hyb_cols(W["hyb_w_in"])
    conv_w, rconv_w, rconv_b, b_a, b_x, lam = _unpack_mid(got[-1])
    W["hyb_conv_w"] = conv_w.transpose(1, 2, 0, 3).reshape(2, CONV_K, 3 * B_W)
    W["rec_conv_w"] = rconv_w.transpose(1, 2, 0, 3).reshape(2, CONV_K, D_MODEL)
    for k, v in (("rec_conv_b", rconv_b), ("rec_b_a", b_a), ("rec_b_x", b_x), ("rec_lambda", lam)):
        W[k] = v.transpose(1, 0, 2).reshape(2, D_MODEL)
    for k in SMALL:
        W[k] = w[k]
    return W


def _grads_by_chip(grads):
    out = []
    g = dict(grads)
    g["hyb_w_in"] = _unpad_hyb_cols(g["hyb_w_in"])
    for k in BIG:
        v = g[k]
        if k in ("rec_w_a", "rec_w_x"):
            L = v.shape[0]
            v = v.reshape(L, LRU_BLOCKS, N_CHIPS, LRU_BLOCK_W // N_CHIPS, LRU_BLOCK_W).transpose(2, 0, 1, 3, 4)
        else:
            v = _cols_to_chips(v) if k in COL_SHARDED else _rows_to_chips(v)
        out.append(v.reshape(N_CHIPS, 2, -1, v.shape[-1]))
    conv_w = g["hyb_conv_w"].reshape(2, CONV_K, N_CHIPS, -1).transpose(2, 0, 1, 3)
    rconv_w = g["rec_conv_w"].reshape(2, CONV_K, N_CHIPS, -1).transpose(2, 0, 1, 3)
    vecs = [g[k].reshape(2, N_CHIPS, -1).transpose(1, 0, 2) for k in ("rec_conv_b", "rec_b_a", "rec_b_x", "rec_lambda")]
    mid = jnp.stack([_pack_mid(conv_w[p], rconv_w[p], *[v[p] for v in vecs]) for p in range(N_CHIPS)])
    out.append(mid)
    return out


def kernel(x, hyb_w_in, hyb_sinks, hyb_conv_w, hyb_a_log, hyb_dt_bias, hyb_norm_w, hyb_w_out, rec_w_in, rec_conv_w, rec_conv_b, rec_w_a, rec_b_a, rec_w_x, rec_b_x, rec_lambda, rec_w_out, ln1_g, ln1_b, mlp_w1, mlp_w2, ln2_g, ln2_b, loss_target, m_hyb_w_in, m_hyb_sinks, m_hyb_conv_w, m_hyb_a_log, m_hyb_dt_bias, m_hyb_norm_w, m_hyb_w_out, m_rec_w_in, m_rec_conv_w, m_rec_conv_b, m_rec_w_a, m_rec_b_a, m_rec_w_x, m_rec_b_x, m_rec_lambda, m_rec_w_out, m_ln1_g, m_ln1_b, m_mlp_w1, m_mlp_w2, m_ln2_g, m_ln2_b, v_hyb_w_in, v_hyb_sinks, v_hyb_conv_w, v_hyb_a_log, v_hyb_dt_bias, v_hyb_norm_w, v_hyb_w_out, v_rec_w_in, v_rec_conv_w, v_rec_conv_b, v_rec_w_a, v_rec_b_a, v_rec_w_x, v_rec_b_x, v_rec_lambda, v_rec_w_out, v_ln1_g, v_ln1_b, v_mlp_w1, v_mlp_w2, v_ln2_g, v_ln2_b):
    args = locals()
    w = {k: args[k] for k in WEIGHTS}
    m = {k: args["m_" + k] for k in WEIGHTS}
    v = {k: args["v_" + k] for k in WEIGHTS}

    W = _gather_full_weights(w)
    loss, dx, grads = _local_step(x[0], loss_target[0], W)
    loss = lax.psum(loss[0, 0], ("x", "y", "c"))

    core = lax.axis_index("c").astype(jnp.int32)
    me = (2 * lax.axis_index("x") + lax.axis_index("y")).astype(jnp.int32)
    slots = jnp.arange(N_CHIPS, dtype=jnp.int32)
    where = jnp.concatenate([me[None], jnp.where(slots == me, (slots + 1) % N_CHIPS, slots), core[None]])
    by_chip = _grads_by_chip(grads)
    from_sibling = _rs_to_sibling(by_chip, "rs_to_sibling")
    pair = [_pair_sum(g, r, core[None], "pair_sum") for g, r in zip(by_chip, from_sibling)]
    from_chips = _rs_across_chips(pair, "rs_across_chips")
    half = [_chip_sum(r, p, where, "chip_sum") for r, p in zip(from_chips, pair)]
    joined = _rs_join_halves(half, "rs_join_halves")

    g_out, d_out, m_out, v_out = {}, {}, {}, {}

    def update(name, g2d):
        shape = w[name].shape
        n = g2d.shape[-1]
        d, nm, nv = _adamw(w[name].reshape(-1, n), g2d, m[name].reshape(-1, n), v[name].reshape(-1, n), "adamw")
        g_out[name], d_out[name] = g2d.reshape(shape), d.reshape(shape)
        m_out[name], v_out[name] = nm.reshape(shape), nv.reshape(shape)

    for k, g in zip(BIG, joined[:-1]):
        update(k, g.reshape(-1, g.shape[-1]))
    mid_w, mid_m, mid_v = (_pack_mid(*[t[k] for k in MID]).reshape(-1, D_MODEL) for t in (w, m, v))
    mid_g = joined[-1].reshape(-1, D_MODEL)
    mid_d, mid_nm, mid_nv = _adamw(mid_w, mid_g, mid_m, mid_v, "adamw_mid")
    for dst, packed in ((g_out, mid_g), (d_out, mid_d), (m_out, mid_nm), (v_out, mid_nv)):
        for k, val in zip(MID, _unpack_mid(packed.reshape(2, MID_ROWS, D_MODEL))):
            dst[k] = val.reshape(w[k].shape)

    small_g = _pack_small(*[grads[k] for k in SMALL])
    small_all = _all_gather_small(small_g, "all_gather_small")
    sw, sm, sv = (_pack_small(*[t[k] for k in SMALL]) for t in (w, m, v))
    sg, sd, snm, snv = _adamw_small(sw, small_all, sm, sv, "adamw_small")
    for dst, packed in ((g_out, sg), (d_out, sd), (m_out, snm), (v_out, snv)):
        for k, val in zip(SMALL, _unpack_small(packed)):
            dst[k] = val

    return (loss, dx[None], *[g_out[k] for k in WEIGHTS], *[d_out[k] for k in WEIGHTS],
            *[m_out[k] for k in WEIGHTS], *[v_out[k] for k in WEIGHTS])
```

```python
import functools

import jax
import jax.numpy as jnp
from jax import lax
from jax.experimental import pallas as pl
from jax.experimental.pallas import tpu as pltpu

F32 = jnp.float32
MXU_DTYPE = jnp.bfloat16
ACT_DTYPE = jnp.bfloat16
ICI_DTYPE = jnp.bfloat16

D_MODEL = 1024
DEPTH = 4
A_HEAD_DIM = 64
A_Q_HEADS = 8
A_KV_HEADS = 2
WINDOW = 128
ROPE_THETA = 10000.0
B_HEADS = 4
B_HEAD_DIM = 128
B_CHUNK = 64
CONV_K = 4
LRU_BLOCKS = 4
LRU_BLOCK_W = D_MODEL // LRU_BLOCKS
LRU_C = 8.0
D_FF = 4 * D_MODEL
A_Q_W = A_Q_HEADS * A_HEAD_DIM
A_KV_W = A_KV_HEADS * A_HEAD_DIM
B_W = B_HEADS * B_HEAD_DIM
HYB_PROJ = A_Q_W + 2 * A_KV_W + 4 * B_W + 2 * B_HEADS
DN_ALPHA = (2 * DEPTH) ** 0.25
LN_EPS = 1e-5
NORM_EPS = 1e-6
ADAM_LR = 0.001
ADAM_B1 = 0.9
ADAM_B2 = 0.999
ADAM_EPS = 1e-08
ADAM_WD = 0.01
ADAM_STEP = 10

HP_Q = 0
HP_QKVB = 512
HP_Z = 2048
HP_K = 2560
HP_V = 2688
HP_BG = 2816
HYB_PAD = 3072

N_CHIPS = 4
N_DEV = 8
V7X_VMEM_LIMIT = 48 * 1024 * 1024
LANES = 128
SUBLANES = 8
NEG_BIG = -1e30

NN = (((1,), (0,)), ((), ()))
NT = (((1,), (1,)), ((), ()))
TN = (((0,), (0,)), ((), ()))


def _cparams(*sem):
    return pltpu.CompilerParams(dimension_semantics=sem, vmem_limit_bytes=V7X_VMEM_LIMIT)


def _dot(a, b, dims=NN):
    return lax.dot_general(a.astype(MXU_DTYPE), b.astype(MXU_DTYPE), dims, preferred_element_type=F32)


def _split_bf16(a):
    hi = a.astype(jnp.bfloat16)
    return hi, (a - hi.astype(F32)).astype(jnp.bfloat16)


def _dotf(a, b, dims=NN):
    ah, al = _split_bf16(a)
    bh, bl = _split_bf16(b)
    dg = functools.partial(lax.dot_general, dimension_numbers=dims, preferred_element_type=F32)
    return dg(ah, bh) + (dg(ah, bl) + dg(al, bh))


def _tile(dim, pref):
    t = min(dim, pref)
    while dim % t:
        t //= 2
    return t


def _sigmoid(x):
    return 1.0 / (1.0 + jnp.exp(-x))


def _silu(x):
    return x * _sigmoid(x)


def _dsilu(x):
    s = _sigmoid(x)
    return s * (1.0 + x * (1.0 - s))


GELU_C = 0.7978845608028654
GELU_A = 0.044715


def _gelu(x):
    return 0.5 * x * (1.0 + jnp.tanh(GELU_C * (x + GELU_A * x * x * x)))


def _dgelu(x):
    t = jnp.tanh(GELU_C * (x + GELU_A * x * x * x))
    return 0.5 * (1.0 + t) + 0.5 * x * (1.0 - t * t) * GELU_C * (1.0 + 3.0 * GELU_A * x * x)


def _matmul(a, b, mode, name, *, tm=1024, tn=1024, tk=1024, a_fn=None, epi=None, extra=None, out_dtype=F32):
    if mode == "nn":
        (M, K), (_, N) = a.shape, b.shape
    elif mode == "nt":
        (M, K), (N, _) = a.shape, b.shape
    else:
        (K, M), (_, N) = a.shape, b.shape
    tm, tn, tk = _tile(M, tm), _tile(N, tn), _tile(K, tk)
    nk = K // tk
    if mode == "tn":
        a_spec = pl.BlockSpec((tk, tm), lambda i, j, k: (k, i))
    else:
        a_spec = pl.BlockSpec((tm, tk), lambda i, j, k: (i, k))
    if mode == "nt":
        b_spec = pl.BlockSpec((tn, tk), lambda i, j, k: (j, k))
    else:
        b_spec = pl.BlockSpec((tk, tn), lambda i, j, k: (k, j))
    o_spec = pl.BlockSpec((tm, tn), lambda i, j, k: (i, j))
    dims = {"nn": NN, "nt": NT, "tn": TN}[mode]
    has_extra = extra is not None

    def body(*refs):
        a_ref, b_ref = refs[0], refs[1]
        e_ref = refs[2] if has_extra else None
        o_ref = refs[3] if has_extra else refs[2]
        av = a_ref[...]
        if a_fn is not None:
            av = a_fn(av)
        part = _dot(av, b_ref[...], dims)

        def finish(acc):
            if epi is not None:
                acc = epi(acc, e_ref[...])
            o_ref[...] = acc.astype(out_dtype)

        if nk == 1:
            finish(part)
        else:
            acc_ref = refs[-1]
            k = pl.program_id(2)

            @pl.when(k == 0)
            def _():
                acc_ref[...] = part

            @pl.when(k > 0)
            def _():
                acc_ref[...] += part

            @pl.when(k == nk - 1)
            def _():
                finish(acc_ref[...])

    in_specs = [a_spec, b_spec] + ([o_spec] if has_extra else [])
    args = (a, b) + ((extra,) if has_extra else ())
    return pl.pallas_call(
        body, name=name, grid=(M // tm, N // tn, nk), in_specs=in_specs, out_specs=o_spec,
        out_shape=jax.ShapeDtypeStruct((M, N), out_dtype),
        scratch_shapes=[pltpu.VMEM((tm, tn), F32)] if nk > 1 else [],
        compiler_params=_cparams("parallel", "parallel", "arbitrary"),
    )(*args)


def _relu2(v):
    r = jnp.maximum(v, 0.0)
    return r * r


def _epi_drelu2(acc, h):
    return acc * (2.0 * jnp.maximum(h, 0.0))


def _epi_add_residual(acc, du):
    return acc + DN_ALPHA * du


def _ln_fwd(x, y, g, b, name):
    T, D = x.shape
    tr = _tile(T, 512)

    def body(x_ref, y_ref, g_ref, b_ref, o_ref):
        u = DN_ALPHA * x_ref[...] + y_ref[...]
        mu = jnp.mean(u, axis=-1, keepdims=True)
        d = u - mu
        var = jnp.mean(d * d, axis=-1, keepdims=True)
        o_ref[...] = d * lax.rsqrt(var + LN_EPS) * g_ref[...] + b_ref[...]

    row = pl.BlockSpec((tr, D), lambda i: (i, 0))
    vec = pl.BlockSpec((1, D), lambda i: (0, 0))
    return pl.pallas_call(
        body, name=name, grid=(T // tr,), in_specs=[row, row, vec, vec], out_specs=row,
        out_shape=jax.ShapeDtypeStruct((T, D), F32), compiler_params=_cparams("parallel"),
    )(x, y, g, b)


def _ln_bwd(x, y, g, dout, name):
    T, D = x.shape
    tr = _tile(T, 512)

    def body(x_ref, y_ref, g_ref, d_ref, du_ref, dg_ref, db_ref):
        i = pl.program_id(0)
        u = DN_ALPHA * x_ref[...] + y_ref[...]
        mu = jnp.mean(u, axis=-1, keepdims=True)
        d = u - mu
        rstd = lax.rsqrt(jnp.mean(d * d, axis=-1, keepdims=True) + LN_EPS)
        xhat = d * rstd
        dout_v = d_ref[...]
        dxh = dout_v * g_ref[...]
        m1 = jnp.mean(dxh, axis=-1, keepdims=True)
        m2 = jnp.mean(dxh * xhat, axis=-1, keepdims=True)
        du_ref[...] = rstd * (dxh - m1 - xhat * m2)
        pg = jnp.sum(dout_v * xhat, axis=0, keepdims=True)
        pb = jnp.sum(dout_v, axis=0, keepdims=True)

        @pl.when(i == 0)
        def _():
            dg_ref[...] = pg
            db_ref[...] = pb

        @pl.when(i > 0)
        def _():
            dg_ref[...] += pg
            db_ref[...] += pb

    row = pl.BlockSpec((tr, D), lambda i: (i, 0))
    vec = pl.BlockSpec((1, D), lambda i: (0, 0))
    return pl.pallas_call(
        body, name=name, grid=(T // tr,), in_specs=[row, row, vec, row], out_specs=[row, vec, vec],
        out_shape=[jax.ShapeDtypeStruct((T, D), F32), jax.ShapeDtypeStruct((1, D), F32),
                   jax.ShapeDtypeStruct((1, D), F32)],
        compiler_params=_cparams("arbitrary"),
    )(x, y, g, dout)


def _loss_head(y, tgt, name):
    T, D = y.shape
    tr = _tile(T, 512)

    def body(y_ref, t_ref, dy_ref, l_ref):
        i = pl.program_id(0)
        e = y_ref[...] - t_ref[...]
        dy_ref[...] = e * (1.0 / D)
        part = jnp.sum(e * e, axis=(0, 1), keepdims=True) * (0.5 / D)

        @pl.when(i == 0)
        def _():
            l_ref[...] = part

        @pl.when(i > 0)
        def _():
            l_ref[...] += part

    row = pl.BlockSpec((tr, D), lambda i: (i, 0))
    one = pl.BlockSpec((1, 1), lambda i: (0, 0))
    return pl.pallas_call(
        body, name=name, grid=(T // tr,), in_specs=[row, row], out_specs=[row, one],
        out_shape=[jax.ShapeDtypeStruct((T, D), F32), jax.ShapeDtypeStruct((1, 1), F32)],
        compiler_params=_cparams("arbitrary"),
    )(y, tgt)


def _swap_half(x):
    n = x.shape[-1]
    lane = lax.broadcasted_iota(jnp.int32, x.shape, 1)
    first = (lane % A_HEAD_DIM) < (A_HEAD_DIM // 2)
    return jnp.where(first, pltpu.roll(x, n - A_HEAD_DIM // 2, axis=1), pltpu.roll(x, A_HEAD_DIM // 2, axis=1))


def _rope(x, cos, sin_signed):
    return x * cos + _swap_half(x) * sin_signed


def _rope_t(dy, cos, sin_signed):
    return dy * cos + _swap_half(dy * sin_signed)


def _rope_tables(T):
    half = A_HEAD_DIM // 2
    inv_freq = ROPE_THETA ** (-jnp.arange(half, dtype=F32) / half)
    ang = jnp.arange(T, dtype=F32)[:, None] * inv_freq[None, :]
    cos = jnp.tile(jnp.cos(ang), (1, 2 * A_Q_HEADS))
    sin = jnp.sin(ang)
    sin_signed = jnp.tile(jnp.concatenate([-sin, sin], axis=1), (1, A_Q_HEADS))
    return cos, sin_signed


def _band_mask(n):
    qi = lax.broadcasted_iota(jnp.int32, (WINDOW, 2 * WINDOW), 0)
    kj = lax.broadcasted_iota(jnp.int32, (WINDOW, 2 * WINDOW), 1)
    return (kj > qi) & (kj <= qi + WINDOW) & ((n > 0) | (kj >= WINDOW))


def _place(v, src_half, dst_half):
    lane = lax.broadcasted_iota(jnp.int32, v.shape, 1)
    if src_half != dst_half:
        v = pltpu.roll(v, A_HEAD_DIM, axis=1)
    keep = (lane >= A_HEAD_DIM) if dst_half else (lane < A_HEAD_DIM)
    return jnp.where(keep, v, 0.0)


def _attn_specs():
    kb, vb = HP_K // LANES, HP_V // LANES
    prev = lambda n: jnp.maximum(n - 1, 0)
    return dict(
        q=pl.BlockSpec((WINDOW, A_Q_W), lambda n: (n, 0)),
        kc=pl.BlockSpec((WINDOW, LANES), lambda n: (n, kb)),
        kp=pl.BlockSpec((WINDOW, LANES), lambda n: (prev(n), kb)),
        vc=pl.BlockSpec((WINDOW, LANES), lambda n: (n, vb)),
        vp=pl.BlockSpec((WINDOW, LANES), lambda n: (prev(n), vb)),
        tq=pl.BlockSpec((WINDOW, A_Q_W), lambda n: (n, 0)),
        tp=pl.BlockSpec((WINDOW, LANES), lambda n: (prev(n), 0)),
        sink=pl.BlockSpec((A_Q_HEADS, LANES), lambda n: (0, 0)),
        row512=pl.BlockSpec((WINDOW, A_Q_W), lambda n: (n, 0)),
        row128=pl.BlockSpec((WINDOW, LANES), lambda n: (n, 0)),
        lse=pl.BlockSpec((WINDOW, A_Q_HEADS), lambda n: (n, 0)),
    )


def _attn_fwd(proj, cos, sin_s, sinks_b, name):
    T = proj.shape[0]
    sp = _attn_specs()

    def body(q_ref, kc_ref, kp_ref, vc_ref, vp_ref, cq_ref, sq_ref, cp_ref, sp_ref, sink_ref, o_ref, l_ref):
        n = pl.program_id(0)
        cq, sq = cq_ref[...], sq_ref[...]
        q = _rope(q_ref[...], cq, sq) * (A_HEAD_DIM ** -0.5)
        kc = _rope(kc_ref[...], cq[:, :LANES], sq[:, :LANES])
        kp = _rope(kp_ref[...], cp_ref[...], sp_ref[...])
        kk = jnp.concatenate([kp, kc], axis=0)
        vv = jnp.concatenate([vp_ref[...], vc_ref[...]], axis=0)
        mask = _band_mask(n)
        lane = lax.broadcasted_iota(jnp.int32, (WINDOW, LANES), 1)
        outs = []
        lane8 = lax.broadcasted_iota(jnp.int32, (WINDOW, A_Q_HEADS), 1)
        lse = jnp.zeros((WINDOW, A_Q_HEADS), F32)
        for pb in range(A_Q_HEADS // 2):
            kvh = pb // 2
            q2 = q[:, pb * LANES:(pb + 1) * LANES]
            halves = []
            for e in range(2):
                hq = 2 * pb + e
                s = _dot(_place(q2, e, kvh), kk, NT)
                s = jnp.where(mask, s, NEG_BIG)
                sk = sink_ref[hq:hq + 1, 0:1]
                m = jnp.maximum(jnp.max(s, axis=-1, keepdims=True), sk)
                p = jnp.exp(s - m)
                den = jnp.sum(p, axis=-1, keepdims=True) + jnp.exp(sk - m)
                o = _dot(p * (1.0 / den), vv, NN)
                halves.append(_place(o, kvh, e))
                lse = jnp.where(lane8 == hq, m + jnp.log(den), lse)
            outs.append(jnp.where(lane < A_HEAD_DIM, halves[0], halves[1]))
        o_ref[...] = jnp.concatenate(outs, axis=1)
        l_ref[...] = lse

    return pl.pallas_call(
        body, name=name, grid=(T // WINDOW,),
        in_specs=[sp["q"], sp["kc"], sp["kp"], sp["vc"], sp["vp"], sp["tq"], sp["tq"], sp["tp"], sp["tp"], sp["sink"]],
        out_specs=[sp["row512"], sp["lse"]],
        out_shape=[jax.ShapeDtypeStruct((T, A_Q_W), F32), jax.ShapeDtypeStruct((T, A_Q_HEADS), F32)],
        compiler_params=_cparams("parallel"),
    )(proj, proj, proj, proj, proj, cos, sin_s, cos, sin_s, sinks_b)


def _attn_bwd(proj, cos, sin_s, sinks_b, o, lse, dmix, name):
    T = proj.shape[0]
    sp = _attn_specs()

    def body(q_ref, kc_ref, kp_ref, vc_ref, vp_ref, cq_ref, sq_ref, cp_ref, sp_ref, sink_ref, o_ref, l_ref, do_ref,
             dq_ref, dkc_ref, dkp_ref, dvc_ref, dvp_ref, dsink_ref):
        n = pl.program_id(0)
        cq, sq = cq_ref[...], sq_ref[...]
        cp, sps = cp_ref[...], sp_ref[...]
        q = _rope(q_ref[...], cq, sq) * (A_HEAD_DIM ** -0.5)
        kc = _rope(kc_ref[...], cq[:, :LANES], sq[:, :LANES])
        kp = _rope(kp_ref[...], cp, sps)
        kk = jnp.concatenate([kp, kc], axis=0)
        vv = jnp.concatenate([vp_ref[...], vc_ref[...]], axis=0)
        mask = _band_mask(n)
        lane = lax.broadcasted_iota(jnp.int32, (WINDOW, LANES), 1)
        do_all, o_all, l_all = do_ref[...], o_ref[...], l_ref[...]
        dkk = jnp.zeros((2 * WINDOW, LANES), F32)
        dvv = jnp.zeros((2 * WINDOW, LANES), F32)
        dqs = []
        lane8 = lax.broadcasted_iota(jnp.int32, (WINDOW, A_Q_HEADS), 1)
        head8 = lax.broadcasted_iota(jnp.int32, (1, A_Q_HEADS), 1)
        dsk = jnp.zeros((1, A_Q_HEADS), F32)
        for pb in range(A_Q_HEADS // 2):
            kvh = pb // 2
            q2 = q[:, pb * LANES:(pb + 1) * LANES]
            do2 = do_all[:, pb * LANES:(pb + 1) * LANES]
            prod = do2 * o_all[:, pb * LANES:(pb + 1) * LANES]
            halves = []
            for e in range(2):
                hq = 2 * pb + e
                in_half = (lane >= A_HEAD_DIM) if e else (lane < A_HEAD_DIM)
                delta = jnp.sum(jnp.where(in_half, prod, 0.0), axis=-1, keepdims=True)
                qe = _place(q2, e, kvh)
                doe = _place(do2, e, kvh)
                lh = jnp.sum(jnp.where(lane8 == hq, l_all, 0.0), axis=-1, keepdims=True)
                s = _dot(qe, kk, NT)
                p = jnp.where(mask, jnp.exp(jnp.where(mask, s, NEG_BIG) - lh), 0.0)
                dvv = dvv + _dot(p, doe, TN)
                dp = _dot(doe, vv, NT)
                ds = p * (dp - delta)
                dkk = dkk + _dot(ds, qe, TN)
                halves.append(_place(_dot(ds, kk, NN), kvh, e))
                p_sink = jnp.exp(sink_ref[hq:hq + 1, 0:1] - lh)
                dsk = jnp.where(head8 == hq, -jnp.sum(p_sink * delta, axis=(0, 1), keepdims=True), dsk)
            dqs.append(jnp.where(lane < A_HEAD_DIM, halves[0], halves[1]))
        dq = jnp.concatenate(dqs, axis=1) * (A_HEAD_DIM ** -0.5)
        dq_ref[...] = _rope_t(dq, cq, sq)
        dkp_ref[...] = _rope_t(dkk[:WINDOW], cp, sps)
        dkc_ref[...] = _rope_t(dkk[WINDOW:], cq[:, :LANES], sq[:, :LANES])
        dvp_ref[...] = dvv[:WINDOW]
        dvc_ref[...] = dvv[WINDOW:]

        @pl.when(n == 0)
        def _():
            dsink_ref[...] = dsk

        @pl.when(n > 0)
        def _():
            dsink_ref[...] += dsk

    return pl.pallas_call(
        body, name=name, grid=(T // WINDOW,),
        in_specs=[sp["q"], sp["kc"], sp["kp"], sp["vc"], sp["vp"], sp["tq"], sp["tq"], sp["tp"], sp["tp"], sp["sink"],
                  sp["row512"], sp["lse"], sp["row512"]],
        out_specs=[sp["row512"], sp["row128"], sp["row128"], sp["row128"], sp["row128"],
                   pl.BlockSpec((1, A_Q_HEADS), lambda n: (0, 0))],
        out_shape=[jax.ShapeDtypeStruct((T, A_Q_W), F32)] + [jax.ShapeDtypeStruct((T, LANES), F32)] * 4
        + [jax.ShapeDtypeStruct((1, A_Q_HEADS), F32)],
        compiler_params=_cparams("arbitrary"),
    )(proj, proj, proj, proj, proj, cos, sin_s, cos, sin_s, sinks_b, o, lse, dmix)


def _shift_down(x, prev8, k):
    if k == 0:
        return x
    row = lax.broadcasted_iota(jnp.int32, prev8.shape, 0)
    r = pltpu.roll(x, k, axis=0)
    top = jnp.where(row < k, pltpu.roll(prev8, k, axis=0), r[:SUBLANES])
    return jnp.concatenate([top, r[SUBLANES:]], axis=0)


def _shift_up(x, next8, k):
    if k == 0:
        return x
    R = x.shape[0]
    row = lax.broadcasted_iota(jnp.int32, next8.shape, 0)
    r = pltpu.roll(x, R - k, axis=0)
    bot = jnp.where(row >= SUBLANES - k, pltpu.roll(next8, SUBLANES - k, axis=0), r[R - SUBLANES:])
    return jnp.concatenate([r[:R - SUBLANES], bot], axis=0)


def _conv(x, prev8, w):
    y = x * w[CONV_K - 1:CONV_K]
    for j in range(CONV_K - 1):
        y = y + _shift_down(x, prev8, CONV_K - 1 - j) * w[j:j + 1]
    return y


def _conv_bwd(x, prev8, w, dy, next8_dy):
    dx = dy * w[CONV_K - 1:CONV_K]
    dws = []
    for j in range(CONV_K - 1):
        k = CONV_K - 1 - j
        dx = dx + _shift_up(dy, next8_dy, k) * w[j:j + 1]
        dws.append(jnp.sum(dy * _shift_down(x, prev8, k), axis=0, keepdims=True))
    dws.append(jnp.sum(dy * x, axis=0, keepdims=True))
    return dx, dws


def _dnconv_fwd(proj, conv_w, name):
    T = proj.shape[0]
    R = _tile(T, 512)
    cb0 = HP_QKVB // A_Q_W

    def body(x_ref, w_ref, o_ref, prev_ref):
        i = pl.program_id(1)

        @pl.when(i == 0)
        def _():
            prev_ref[...] = jnp.zeros_like(prev_ref)

        x = x_ref[...]
        o_ref[...] = _silu(_conv(x, prev_ref[...], w_ref[...]))
        prev_ref[...] = x[R - SUBLANES:]

    return pl.pallas_call(
        body, name=name, grid=(3, T // R),
        in_specs=[pl.BlockSpec((R, B_W), lambda j, i: (i, cb0 + j)), pl.BlockSpec((CONV_K, B_W), lambda j, i: (0, j))],
        out_specs=pl.BlockSpec((R, B_W), lambda j, i: (i, j)),
        out_shape=jax.ShapeDtypeStruct((T, 3 * B_W), F32),
        scratch_shapes=[pltpu.VMEM((SUBLANES, B_W), F32)],
        compiler_params=_cparams("parallel", "arbitrary"),
    )(proj, conv_w)


def _dnconv_bwd(proj, conv_w, dy, name):
    T = proj.shape[0]
    R = _tile(T, 512)
    nb = T // R
    cb0 = HP_QKVB // A_Q_W
    r8 = R // SUBLANES

    def body(x_ref, xp_ref, w_ref, dy_ref, dx_ref, dw_ref, next_ref):
        i = pl.program_id(1)
        blk = nb - 1 - i

        @pl.when(i == 0)
        def _():
            next_ref[...] = jnp.zeros_like(next_ref)

        x = x_ref[...]
        prev8 = jnp.where(blk > 0, xp_ref[...], 0.0)
        w = w_ref[...]
        dpre = dy_ref[...] * _dsilu(_conv(x, prev8, w))
        dx, dw = _conv_bwd(x, prev8, w, dpre, next_ref[...])
        dx_ref[...] = dx
        next_ref[...] = dpre[:SUBLANES]

        @pl.when(i == 0)
        def _():
            for j in range(CONV_K):
                dw_ref[j:j + 1, :] = dw[j]

        @pl.when(i > 0)
        def _():
            for j in range(CONV_K):
                dw_ref[j:j + 1, :] += dw[j]

    return pl.pallas_call(
        body, name=name, grid=(3, nb),
        in_specs=[pl.BlockSpec((R, B_W), lambda j, i: (nb - 1 - i, cb0 + j)),
                  pl.BlockSpec((SUBLANES, B_W), lambda j, i: (jnp.maximum((nb - 1 - i) * r8 - 1, 0), cb0 + j)),
                  pl.BlockSpec((CONV_K, B_W), lambda j, i: (0, j)),
                  pl.BlockSpec((R, B_W), lambda j, i: (nb - 1 - i, j))],
        out_specs=[pl.BlockSpec((R, B_W), lambda j, i: (nb - 1 - i, j)),
                   pl.BlockSpec((CONV_K, B_W), lambda j, i: (0, j))],
        out_shape=[jax.ShapeDtypeStruct((T, 3 * B_W), F32), jax.ShapeDtypeStruct((CONV_K, 3 * B_W), F32)],
        scratch_shapes=[pltpu.VMEM((SUBLANES, B_W), F32)],
        compiler_params=_cparams("parallel", "arbitrary"),
    )(proj, proj, conv_w, dy)


DK_SCALE = B_HEAD_DIM ** -0.5


BNN = (((2,), (1,)), ((0,), (0,)))
BNT = (((2,), (2,)), ((0,), (0,)))
BTN = (((1,), (1,)), ((0,), (0,)))


def _tri_inv(a):
    C = a.shape[-1]
    ri = lax.broadcasted_iota(jnp.int32, (C, C), 0)
    ci = lax.broadcasted_iota(jnp.int32, (C, C), 1)
    x = jnp.where(ri == ci, 1.0, 0.0)[None] - a
    p = _dotf(a, a, BNN)
    span = 2
    while span < C:
        x = x + _dotf(x, p, BNN)
        span *= 2
        if span < C:
            p = _dotf(p, p, BNN)
    return x


def _dn_chunk(qc, kc, v, gcol, grow, bcol, s0):
    C = B_CHUNK
    ri = lax.broadcasted_iota(jnp.int32, (C, C), 0)
    ci = lax.broadcasted_iota(jnp.int32, (C, C), 1)
    incl, strict = (ri >= ci)[None], (ri > ci)[None]
    rq = lax.rsqrt(jnp.sum(qc * qc, axis=-1, keepdims=True) + NORM_EPS)
    rk = lax.rsqrt(jnp.sum(kc * kc, axis=-1, keepdims=True) + NORM_EPS)
    qn = qc * rq
    q = qn * DK_SCALE
    k = kc * rk
    gc_col = jnp.sum(jnp.where(incl, grow, 0.0), axis=2, keepdims=True)
    gc_row = jnp.sum(jnp.where((ri <= ci)[None], gcol, 0.0), axis=1, keepdims=True)
    gl = jnp.sum(gcol, axis=1, keepdims=True)
    dincl = jnp.where(incl, jnp.exp(jnp.where(incl, gc_col - gc_row, 0.0)), 0.0)
    dstrict = jnp.where(strict, dincl, 0.0)
    eg = jnp.exp(gc_col)
    ekt = jnp.exp(gl - gc_col)
    egl = jnp.exp(gl)
    kb = k * bcol
    vb = v * bcol
    kbg = kb * eg
    a = _dot(kb, k, BNT) * dstrict
    tm = _tri_inv(a)
    u = _dot(tm, vb, BNN)
    w = _dot(tm, kbg, BNN)
    vn = u - _dot(w, s0, BNN)
    qk = _dot(q, k, BNT) * dincl
    qg = q * eg
    kt = k * ekt
    o = _dot(qg, s0, BNN) + _dot(qk, vn, BNN)
    s1 = s0 * egl + _dot(kt, vn, BTN)
    return dict(rq=rq, rk=rk, qn=qn, q=q, k=k, dincl=dincl, dstrict=dstrict, eg=eg, ekt=ekt, egl=egl, kb=kb, vb=vb,
                kbg=kbg, a=a, tm=tm, w=w, vn=vn, qk=qk, qg=qg, kt=kt, o=o, s1=s1, ri=ri[None], ci=ci[None])


def _heads(ref):
    return jnp.stack([ref[:, h * B_HEAD_DIM:(h + 1) * B_HEAD_DIM] for h in range(B_HEADS)])


def _store_heads(ref, val):
    for h in range(B_HEADS):
        ref[:, h * B_HEAD_DIM:(h + 1) * B_HEAD_DIM] = val[h]


def _dn_specs(N, rev):
    ix = (lambda n: N - 1 - n) if rev else (lambda n: n)
    wide = lambda cb: pl.BlockSpec((B_CHUNK, B_W), lambda n: (ix(n), cb))
    return dict(
        q=wide(0), k=wide(1), v=wide(2), z=wide(HP_Z // B_W), dob=wide(A_Q_W // B_W), out=wide(0),
        nw=pl.BlockSpec((1, LANES), lambda n: (0, 0)),
        col=pl.BlockSpec((B_HEADS, None, B_CHUNK, 1), lambda n: (0, ix(n), 0, 0)),
        row=pl.BlockSpec((B_HEADS, None, 1, B_CHUNK), lambda n: (0, ix(n), 0, 0)),
        state=pl.BlockSpec((B_HEADS, None, B_HEAD_DIM, B_HEAD_DIM), lambda n: (0, ix(n), 0, 0)),
    )


def _dn_fwd(qkvc, proj, norm_w, gcol, grow, bcol, name):
    T = qkvc.shape[0]
    N = T // B_CHUNK
    sp = _dn_specs(N, False)

    def body(q_ref, k_ref, v_ref, z_ref, nw_ref, gc_ref, gr_ref, bc_ref, o_ref, st_ref, s_ref):
        n = pl.program_id(0)

        @pl.when(n == 0)
        def _():
            s_ref[...] = jnp.zeros_like(s_ref)

        s0 = s_ref[...]
        st_ref[...] = s0
        f = _dn_chunk(_heads(q_ref), _heads(k_ref), _heads(v_ref), gc_ref[...], gr_ref[...], bc_ref[...], s0)
        o = f["o"]
        r = lax.rsqrt(jnp.mean(o * o, axis=-1, keepdims=True) + NORM_EPS)
        _store_heads(o_ref, o * r * nw_ref[...][None] * _silu(_heads(z_ref)))
        s_ref[...] = f["s1"]

    return pl.pallas_call(
        body, name=name, grid=(N,),
        in_specs=[sp["q"], sp["k"], sp["v"], sp["z"], sp["nw"], sp["col"], sp["row"], sp["col"]],
        out_specs=[sp["out"], sp["state"]],
        out_shape=[jax.ShapeDtypeStruct((T, B_W), F32),
                   jax.ShapeDtypeStruct((B_HEADS, N, B_HEAD_DIM, B_HEAD_DIM), F32)],
        scratch_shapes=[pltpu.VMEM((B_HEADS, B_HEAD_DIM, B_HEAD_DIM), F32)],
        compiler_params=_cparams("arbitrary"),
    )(qkvc, qkvc, qkvc, proj, norm_w, gcol, grow, bcol)


def _dn_bwd(qkvc, proj, norm_w, gcol, grow, bcol, states, dmix, name):
    T = qkvc.shape[0]
    N = T // B_CHUNK
    sp = _dn_specs(N, True)
    C = B_CHUNK

    def body(q_ref, k_ref, v_ref, z_ref, nw_ref, gc_ref, gr_ref, bc_ref, st_ref, dob_ref,
             dq_ref, dk_ref, dv_ref, dz_ref, dg_ref, db_ref, dnw_ref, ds_ref):
        n = pl.program_id(0)

        @pl.when(n == 0)
        def _():
            ds_ref[...] = jnp.zeros_like(ds_ref)
            dnw_ref[...] = jnp.zeros_like(dnw_ref)

        s0 = st_ref[...]
        ds1 = ds_ref[...]
        v, z, nw, bcol_v = _heads(v_ref), _heads(z_ref), nw_ref[...][None], bc_ref[...]
        f = _dn_chunk(_heads(q_ref), _heads(k_ref), v, gc_ref[...], gr_ref[...], bcol_v, s0)
        o, q, k, qn = f["o"], f["q"], f["k"], f["qn"]
        eg, ekt, egl = f["eg"], f["ekt"], f["egl"]
        tm, w, vn, kb, vb, kbg = f["tm"], f["w"], f["vn"], f["kb"], f["vb"], f["kbg"]
        qg, kt, qk, a = f["qg"], f["kt"], f["qk"], f["a"]
        ri, ci = f["ri"], f["ci"]

        dob_v = _heads(dob_ref)
        r = lax.rsqrt(jnp.mean(o * o, axis=-1, keepdims=True) + NORM_EPS)
        sz = _silu(z)
        on = o * r
        dnw_ref[...] += jnp.sum(dob_v * sz * on, axis=1, keepdims=True)
        _store_heads(dz_ref, dob_v * on * nw * _dsilu(z))
        d_on = dob_v * sz * nw
        do = r * (d_on - on * jnp.mean(d_on * on, axis=-1, keepdims=True))

        dvn = _dot(qk, do, BTN) + _dot(kt, ds1, BNN)
        dqk = _dot(do, vn, BNT)
        dqg = _dot(do, s0, BNT)
        ds_ref[...] = _dot(qg, do, BTN) + egl * ds1 - _dot(w, dvn, BTN)
        dgl = jnp.sum(s0 * ds1, axis=(1, 2), keepdims=True) * egl
        dkt = _dot(vn, ds1, BNT)
        dw = -_dot(dvn, s0, BNT)
        dq = dqg * eg
        dgc = jnp.sum(dqg * qg, axis=-1, keepdims=True)
        dk = dkt * ekt
        t_kt = jnp.sum(dkt * kt, axis=-1, keepdims=True)
        dgl = dgl + jnp.sum(t_kt, axis=1, keepdims=True)
        dgc = dgc - t_kt
        dqkr = dqk * f["dincl"]
        dq = dq + _dot(dqkr, k, BNN)
        dk = dk + _dot(dqkr, q, BTN)
        e_qk = dqk * qk
        dgc = dgc + jnp.sum(e_qk, axis=-1, keepdims=True)
        dgc_row = -jnp.sum(e_qk, axis=1, keepdims=True)
        dtm = _dot(dvn, vb, BNT) + _dot(dw, kbg, BNT)
        dvb = _dot(tm, dvn, BTN)
        dkbg = _dot(tm, dw, BTN)
        dkb = dkbg * eg
        dgc = dgc + jnp.sum(dkbg * kbg, axis=-1, keepdims=True)
        da = -_dotf(tm, _dotf(dtm, tm, BNT), BTN)
        dkk = da * f["dstrict"]
        e_a = da * a
        dgc = dgc + jnp.sum(e_a, axis=-1, keepdims=True)
        dgc_row = dgc_row - jnp.sum(e_a, axis=1, keepdims=True)
        dkb = dkb + _dot(dkk, k, BNN)
        dk = dk + _dot(dkk, kb, BTN)
        dk = dk + dkb * bcol_v
        db_ref[...] = jnp.sum(dkb * k, axis=-1, keepdims=True) + jnp.sum(dvb * v, axis=-1, keepdims=True)
        _store_heads(dv_ref, dvb * bcol_v)
        dgc_row = dgc_row + jnp.sum(jnp.where(ri == ci, dgc, 0.0), axis=1, keepdims=True)
        dg_ref[...] = jnp.sum(jnp.where(ci >= ri, dgc_row, 0.0), axis=2, keepdims=True) + dgl
        dqs = dq * DK_SCALE
        _store_heads(dq_ref, f["rq"] * (dqs - qn * jnp.sum(dqs * qn, axis=-1, keepdims=True)))
        _store_heads(dk_ref, f["rk"] * (dk - k * jnp.sum(dk * k, axis=-1, keepdims=True)))

    return pl.pallas_call(
        body, name=name, grid=(N,),
        in_specs=[sp["q"], sp["k"], sp["v"], sp["z"], sp["nw"], sp["col"], sp["row"], sp["col"], sp["state"], sp["dob"]],
        out_specs=[sp["out"], sp["out"], sp["out"], sp["out"], sp["col"], sp["col"],
                   pl.BlockSpec((B_HEADS, 1, LANES), lambda n: (0, 0, 0))],
        out_shape=[jax.ShapeDtypeStruct((T, B_W), F32)] * 4
        + [jax.ShapeDtypeStruct((B_HEADS, N, C, 1), F32), jax.ShapeDtypeStruct((B_HEADS, N, C, 1), F32),
           jax.ShapeDtypeStruct((B_HEADS, 1, LANES), F32)],
        scratch_shapes=[pltpu.VMEM((B_HEADS, B_HEAD_DIM, B_HEAD_DIM), F32)],
        compiler_params=_cparams("arbitrary"),
    )(qkvc, qkvc, qkvc, proj, norm_w, gcol, grow, bcol, states, dmix)


def _lru_gates(xc, wa_ref, wx_ref, ba, bx, sp):
    pre_r, pre_i = [], []
    for hb in range(LRU_BLOCKS):
        xb = xc[:, hb * LRU_BLOCK_W:(hb + 1) * LRU_BLOCK_W]
        pre_r.append(_dot(xb, wa_ref[hb]))
        pre_i.append(_dot(xb, wx_ref[hb]))
    r = _sigmoid(jnp.concatenate(pre_r, axis=1) + ba)
    i = _sigmoid(jnp.concatenate(pre_i, axis=1) + bx)
    la = -LRU_C * r * sp
    a = jnp.exp(la)
    th = jnp.tanh(la)
    s = jnp.sqrt(-2.0 * th / (1.0 - th))
    return r, i, a, s


def _scan_down(a, b):
    R = a.shape[0]
    row = lax.broadcasted_iota(jnp.int32, a.shape, 0)
    d = 1
    while d < R:
        ok = row >= d
        b = a * jnp.where(ok, pltpu.roll(b, d, axis=0), 0.0) + b
        a = a * jnp.where(ok, pltpu.roll(a, d, axis=0), 1.0)
        d *= 2
    return a, b


def _scan_up(a, b):
    R = a.shape[0]
    row = lax.broadcasted_iota(jnp.int32, a.shape, 0)
    d = 1
    while d < R:
        ok = row < R - d
        b = a * jnp.where(ok, pltpu.roll(b, R - d, axis=0), 0.0) + b
        a = a * jnp.where(ok, pltpu.roll(a, R - d, axis=0), 1.0)
        d *= 2
    return b


def _rglru_fwd(proj, conv_w, conv_b, wa, wx, ba, bx, sp, name):
    T = proj.shape[0]
    R = _tile(T, 256)
    W = D_MODEL

    def body(p_ref, cw_ref, cb_ref, wa_ref, wx_ref, ba_ref, bx_ref, sp_ref, hg_ref, h_ref, prev_ref, hc_ref):
        i = pl.program_id(0)

        @pl.when(i == 0)
        def _():
            prev_ref[...] = jnp.zeros_like(prev_ref)
            hc_ref[...] = jnp.zeros_like(hc_ref)

        xr = p_ref[:, :W]
        gate = p_ref[:, W:]
        xc = _conv(xr, prev_ref[...], cw_ref[...]) + cb_ref[...]
        prev_ref[...] = xr[R - SUBLANES:]
        r, ig, a, s = _lru_gates(xc, wa_ref, wx_ref, ba_ref[...], bx_ref[...], sp_ref[...])
        pa, hb = _scan_down(a, s * ig * xc)
        h = hb + pa * hc_ref[SUBLANES - 1:SUBLANES, :]
        h_ref[...] = h
        hg_ref[...] = (h * _gelu(gate)).astype(ACT_DTYPE)
        hc_ref[...] = h[R - SUBLANES:]

    vec = pl.BlockSpec((1, W), lambda i: (0, 0))
    wsp = pl.BlockSpec((LRU_BLOCKS, LRU_BLOCK_W, LRU_BLOCK_W), lambda i: (0, 0, 0))
    row = pl.BlockSpec((R, W), lambda i: (i, 0))
    return pl.pallas_call(
        body, name=name, grid=(T // R,),
        in_specs=[pl.BlockSpec((R, 2 * W), lambda i: (i, 0)), pl.BlockSpec((CONV_K, W), lambda i: (0, 0)),
                  vec, wsp, wsp, vec, vec, vec],
        out_specs=[row, row],
        out_shape=[jax.ShapeDtypeStruct((T, W), ACT_DTYPE), jax.ShapeDtypeStruct((T, W), F32)],
        scratch_shapes=[pltpu.VMEM((SUBLANES, W), F32), pltpu.VMEM((SUBLANES, W), F32)],
        compiler_params=_cparams("arbitrary"),
    )(proj, conv_w, conv_b, wa, wx, ba, bx, sp)


def _rglru_bwd(proj, conv_w, conv_b, wa, wx, ba, bx, sp, h, dhg, name):
    T = proj.shape[0]
    R = _tile(T, 256)
    nb = T // R
    r8 = R // SUBLANES
    W = D_MODEL

    def body(p_ref, pp_ref, cw_ref, cb_ref, wa_ref, wx_ref, ba_ref, bx_ref, sp_ref, h_ref, hp_ref, dhg_ref,
             dp_ref, dcw_ref, dcb_ref, dwa_ref, dwx_ref, dba_ref, dbx_ref, dsp_ref, lam_ref, nxt_ref):
        step = pl.program_id(0)
        blk = nb - 1 - step

        @pl.when(step == 0)
        def _():
            lam_ref[...] = jnp.zeros_like(lam_ref)
            nxt_ref[...] = jnp.zeros_like(nxt_ref)

        xr = p_ref[:, :W]
        gate = p_ref[:, W:]
        first = blk > 0
        prev8 = jnp.where(first, pp_ref[:, :W], 0.0)
        hprev8 = jnp.where(first, hp_ref[...], 0.0)
        cw = cw_ref[...]
        spv = sp_ref[...]
        xc = _conv(xr, prev8, cw) + cb_ref[...]
        r, ig, a, s = _lru_gates(xc, wa_ref, wx_ref, ba_ref[...], bx_ref[...], spv)
        hv = h_ref[...]
        dhg_v = dhg_ref[...]
        dgate = dhg_v * hv * _dgelu(gate)
        dh = dhg_v * _gelu(gate)
        row = lax.broadcasted_iota(jnp.int32, (R, W), 0)
        last = row == R - 1
        a_up = jnp.where(last, 0.0, pltpu.roll(a, R - 1, axis=0))
        lam = _scan_up(a_up, dh + jnp.where(last, lam_ref[0:1, :], 0.0))
        lam_ref[...] = (a * lam)[:SUBLANES]
        h_dn = _shift_down(hv, hprev8, 1)
        da = lam * h_dn
        bx_in = ig * xc
        dsv = lam * bx_in
        dig = lam * s * xc
        dxc = lam * s * ig
        dla = da * a - dsv * (a * a) / s
        dr = dla * (-LRU_C) * spv
        dsp = jnp.sum(dla * (-LRU_C) * r, axis=0, keepdims=True)
        dpr = dr * r * (1.0 - r)
        dpi = dig * ig * (1.0 - ig)
        dxc_parts, dwa_parts, dwx_parts = [], [], []
        for hb in range(LRU_BLOCKS):
            sl = slice(hb * LRU_BLOCK_W, (hb + 1) * LRU_BLOCK_W)
            xb, gr, gi = xc[:, sl], dpr[:, sl], dpi[:, sl]
            dxc_parts.append(_dot(gr, wa_ref[hb], NT) + _dot(gi, wx_ref[hb], NT))
            dwa_parts.append(_dot(xb, gr, TN))
            dwx_parts.append(_dot(xb, gi, TN))
        dxc = dxc + jnp.concatenate(dxc_parts, axis=1)
        dxr, dcw = _conv_bwd(xr, prev8, cw, dxc, nxt_ref[...])
        nxt_ref[...] = dxc[:SUBLANES]
        dp_ref[:, :W] = dxr.astype(ACT_DTYPE)
        dp_ref[:, W:] = dgate.astype(ACT_DTYPE)
        dcb = jnp.sum(dxc, axis=0, keepdims=True)
        dba = jnp.sum(dpr, axis=0, keepdims=True)
        dbx = jnp.sum(dpi, axis=0, keepdims=True)

        @pl.when(step == 0)
        def _():
            for j in range(CONV_K):
                dcw_ref[j:j + 1, :] = dcw[j]
            dcb_ref[...] = dcb
            dba_ref[...] = dba
            dbx_ref[...] = dbx
            dsp_ref[...] = dsp
            for hb in range(LRU_BLOCKS):
                dwa_ref[hb] = dwa_parts[hb]
                dwx_ref[hb] = dwx_parts[hb]

        @pl.when(step > 0)
        def _():
            for j in range(CONV_K):
                dcw_ref[j:j + 1, :] += dcw[j]
            dcb_ref[...] += dcb
            dba_ref[...] += dba
            dbx_ref[...] += dbx
            dsp_ref[...] += dsp
            for hb in range(LRU_BLOCKS):
                dwa_ref[hb] += dwa_parts[hb]
                dwx_ref[hb] += dwx_parts[hb]

    rv = lambda i: nb - 1 - i
    before = lambda i: jnp.maximum((nb - 1 - i) * r8 - 1, 0)
    vec = pl.BlockSpec((1, W), lambda i: (0, 0))
    cws = pl.BlockSpec((CONV_K, W), lambda i: (0, 0))
    wsp = pl.BlockSpec((LRU_BLOCKS, LRU_BLOCK_W, LRU_BLOCK_W), lambda i: (0, 0, 0))
    row = pl.BlockSpec((R, W), lambda i: (rv(i), 0))
    wshape = jax.ShapeDtypeStruct((LRU_BLOCKS, LRU_BLOCK_W, LRU_BLOCK_W), F32)
    vshape = jax.ShapeDtypeStruct((1, W), F32)
    return pl.pallas_call(
        body, name=name, grid=(nb,),
        in_specs=[pl.BlockSpec((R, 2 * W), lambda i: (rv(i), 0)), pl.BlockSpec((SUBLANES, 2 * W), lambda i: (before(i), 0)),
                  cws, vec, wsp, wsp, vec, vec, vec, row, pl.BlockSpec((SUBLANES, W), lambda i: (before(i), 0)), row],
        out_specs=[pl.BlockSpec((R, 2 * W), lambda i: (rv(i), 0)), cws, vec, wsp, wsp, vec, vec, vec],
        out_shape=[jax.ShapeDtypeStruct((T, 2 * W), ACT_DTYPE), jax.ShapeDtypeStruct((CONV_K, W), F32), vshape,
                   wshape, wshape, vshape, vshape, vshape],
        scratch_shapes=[pltpu.VMEM((SUBLANES, W), F32), pltpu.VMEM((SUBLANES, W), F32)],
        compiler_params=_cparams("arbitrary"),
    )(proj, proj, conv_w, conv_b, wa, wx, ba, bx, sp, h, h, dhg)


MESH = pl.DeviceIdType.MESH
ANY = pl.BlockSpec(memory_space=pl.ANY)


def _position():
    x, y, c = lax.axis_index("x"), lax.axis_index("y"), lax.axis_index("c")
    other_chips = [(1 - x, y), (x, 1 - y), (1 - x, 1 - y)]
    return x, y, c, other_chips


def _all_gather_weights(shards, name):
    n = len(shards)

    def body(*refs):
        ins, outs = refs[:n], refs[n:2 * n]
        send_sems, recv_sems = refs[2 * n:]
        x, y, c, chips = _position()
        me = 2 * x + y
        sibling = (x, y, 1 - c)

        def rcopy(t, k, src, dst, to):
            return pltpu.make_async_remote_copy(src_ref=src, dst_ref=dst, send_sem=send_sems.at[t, k],
                                                recv_sem=recv_sems.at[t, k], device_id=to, device_id_type=MESH)

        started = []
        for t in range(n):
            for j, (cx, cy) in enumerate(chips):
                cp = rcopy(t, j, ins[t].at[c], outs[t].at[me, c], (cx, cy, c))
                cp.start()
                started.append(cp)
        for t in range(n):
            for j, (cx, cy) in enumerate(chips):
                blk = outs[t].at[2 * cx + cy, c]
                rcopy(t, j, blk, blk, (cx, cy, c)).wait_recv()
                cp = rcopy(t, 3 + j, blk, blk, sibling)
                cp.start()
                started.append(cp)
        for t in range(n):
            for j, (cx, cy) in enumerate(chips):
                blk = outs[t].at[2 * cx + cy, 1 - c]
                rcopy(t, 3 + j, blk, blk, sibling).wait_recv()
        for cp in started:
            cp.wait_send()

    return pl.pallas_call(
        body, name=name, in_specs=[ANY] * n, out_specs=[ANY] * n,
        out_shape=[jax.ShapeDtypeStruct((N_CHIPS,) + s.shape, s.dtype) for s in shards],
        scratch_shapes=[pltpu.SemaphoreType.DMA((n, 6)), pltpu.SemaphoreType.DMA((n, 6))],
    )(*shards)


def _rs_to_sibling(grads, name):
    n = len(grads)

    def body(*refs):
        ins, outs = refs[:n], refs[n:2 * n]
        send_sems, recv_sems = refs[2 * n:]
        x, y, c, _ = _position()
        cps = [pltpu.make_async_remote_copy(src_ref=ins[t].at[:, 1 - c], dst_ref=outs[t], send_sem=send_sems.at[t],
                                            recv_sem=recv_sems.at[t], device_id=(x, y, 1 - c), device_id_type=MESH)
               for t in range(n)]
        for cp in cps:
            cp.start()
        for cp in cps:
            cp.wait()

    return pl.pallas_call(
        body, name=name, in_specs=[ANY] * n, out_specs=[ANY] * n,
        out_shape=[jax.ShapeDtypeStruct((N_CHIPS,) + g.shape[2:], g.dtype) for g in grads],
        scratch_shapes=[pltpu.SemaphoreType.DMA((n,)), pltpu.SemaphoreType.DMA((n,))],
    )(*grads)


def _rs_across_chips(parts, name):
    n = len(parts)

    def body(*refs):
        ins, outs = refs[:n], refs[n:2 * n]
        send_sems, recv_sems = refs[2 * n:]
        x, y, c, chips = _position()
        me = 2 * x + y
        cps = []
        for t in range(n):
            for j, (cx, cy) in enumerate(chips):
                cps.append(pltpu.make_async_remote_copy(
                    src_ref=ins[t].at[2 * cx + cy], dst_ref=outs[t].at[me], send_sem=send_sems.at[t, j],
                    recv_sem=recv_sems.at[t, j], device_id=(cx, cy, c), device_id_type=MESH))
        for cp in cps:
            cp.start()
        for t in range(n):
            for j, (cx, cy) in enumerate(chips):
                blk = outs[t].at[2 * cx + cy]
                pltpu.make_async_remote_copy(src_ref=blk, dst_ref=blk, send_sem=send_sems.at[t, j],
                                             recv_sem=recv_sems.at[t, j], device_id=(cx, cy, c),
                                             device_id_type=MESH).wait_recv()
        for cp in cps:
            cp.wait_send()

    return pl.pallas_call(
        body, name=name, in_specs=[ANY] * n, out_specs=[ANY] * n,
        out_shape=[jax.ShapeDtypeStruct(p.shape, p.dtype) for p in parts],
        scratch_shapes=[pltpu.SemaphoreType.DMA((n, 3)), pltpu.SemaphoreType.DMA((n, 3))],
    )(*parts)


def _rs_join_halves(halves, name):
    n = len(halves)

    def body(*refs):
        ins, outs = refs[:n], refs[n:2 * n]
        send_sems, recv_sems = refs[2 * n:]
        x, y, c, _ = _position()
        cps = [pltpu.make_async_remote_copy(src_ref=ins[t].at[c], dst_ref=outs[t].at[c], send_sem=send_sems.at[t],
                                            recv_sem=recv_sems.at[t], device_id=(x, y, 1 - c), device_id_type=MESH)
               for t in range(n)]
        for cp in cps:
            cp.start()
        for t in range(n):
            blk = outs[t].at[1 - c]
            pltpu.make_async_remote_copy(src_ref=blk, dst_ref=blk, send_sem=send_sems.at[t], recv_sem=recv_sems.at[t],
                                         device_id=(x, y, 1 - c), device_id_type=MESH).wait_recv()
        for cp in cps:
            cp.wait_send()

    return pl.pallas_call(
        body, name=name, in_specs=[ANY] * n, out_specs=[ANY] * n,
        out_shape=[jax.ShapeDtypeStruct(h.shape, h.dtype) for h in halves],
        input_output_aliases={t: t for t in range(n)},
        scratch_shapes=[pltpu.SemaphoreType.DMA((n,)), pltpu.SemaphoreType.DMA((n,))],
    )(*halves)


def _all_gather_small(block, name):
    m_per, n = block.shape

    def body(x_ref, out_ref, send_sems, recv_sems, local_sem):
        x, y, c, chips = _position()
        me, sibling = (x, y, c), (x, y, 1 - c)

        def rows(px, py, pc):
            return out_ref.at[pl.ds((4 * px + 2 * py + pc) * m_per, m_per), :]

        def copy(k, blk, to, src=None):
            return pltpu.make_async_remote_copy(
                src_ref=rows(*blk) if src is None else src, dst_ref=rows(*blk), send_sem=send_sems.at[k],
                recv_sem=recv_sems.at[k], device_id=to, device_id_type=MESH)

        mine = pltpu.make_async_copy(x_ref, rows(*me), local_sem)
        mine.start()
        first = [copy(0, me, sibling, src=x_ref)]
        first += [copy(1 + j, me, (*chip, c), src=x_ref) for j, chip in enumerate(chips)]
        for cp in first:
            cp.start()
        passed = [copy(4 + j, (*chip, c), sibling) for j, chip in enumerate(chips)]
        for j, chip in enumerate(chips):
            copy(1 + j, (*chip, c), me).wait_recv()
            passed[j].start()
        copy(0, sibling, me).wait_recv()
        for j, chip in enumerate(chips):
            copy(4 + j, (*chip, 1 - c), me).wait_recv()
        for cp in first + passed:
            cp.wait_send()
        mine.wait()

    return pl.pallas_call(
        body, name=name, out_shape=jax.ShapeDtypeStruct((N_DEV * m_per, n), block.dtype),
        in_specs=[pl.BlockSpec(memory_space=pltpu.VMEM)], out_specs=pl.BlockSpec(memory_space=pltpu.VMEM),
        scratch_shapes=[pltpu.SemaphoreType.DMA((7,)), pltpu.SemaphoreType.DMA((7,)), pltpu.SemaphoreType.DMA],
    )(block)


def _row_tile(R, n):
    budget = 1 << 19
    if R * n <= budget or R % SUBLANES:
        return R
    t = R
    while t * n > budget and t % (2 * SUBLANES) == 0:
        t //= 2
    return t


def _pair_sum(g, recv, c_arr, name):
    _, _, R, n = g.shape
    tr = _row_tile(R, n)

    def body(c_ref, g_ref, r_ref, o_ref):
        o_ref[...] = (g_ref[...] + r_ref[...]).astype(ICI_DTYPE)

    grid_spec = pltpu.PrefetchScalarGridSpec(
        num_scalar_prefetch=1, grid=(N_CHIPS, R // tr),
        in_specs=[pl.BlockSpec((None, None, tr, n), lambda p, i, c: (p, c[0], i, 0)),
                  pl.BlockSpec((None, tr, n), lambda p, i, c: (p, i, 0))],
        out_specs=pl.BlockSpec((None, tr, n), lambda p, i, c: (p, i, 0)))
    return pl.pallas_call(
        body, name=name, grid_spec=grid_spec, out_shape=jax.ShapeDtypeStruct(recv.shape, ICI_DTYPE),
        compiler_params=_cparams("parallel", "parallel"),
    )(c_arr, g, recv)


def _chip_sum(recv, own, where, name):
    _, R, n = recv.shape
    tr = _row_tile(R, n)

    def body(w_ref, r0, r1, r2, r3, own_ref, o_ref):
        me = w_ref[0]
        terms = [jnp.where(me == k, own_ref[...], r[...]).astype(F32) for k, r in enumerate((r0, r1, r2, r3))]
        o_ref[...] = ((terms[0] + terms[1]) + terms[2]) + terms[3]

    def slot(k):
        return pl.BlockSpec((None, tr, n), lambda i, w: (w[1 + k], i, 0))

    grid_spec = pltpu.PrefetchScalarGridSpec(
        num_scalar_prefetch=1, grid=(R // tr,),
        in_specs=[slot(0), slot(1), slot(2), slot(3), pl.BlockSpec((None, tr, n), lambda i, w: (w[0], i, 0))],
        out_specs=pl.BlockSpec((None, tr, n), lambda i, w: (w[5], i, 0)))
    return pl.pallas_call(
        body, name=name, grid_spec=grid_spec, out_shape=jax.ShapeDtypeStruct((2, R, n), F32),
        compiler_params=_cparams("parallel"),
    )(where, recv, recv, recv, recv, own)


ADAM_C1 = 1.0 / (1.0 - ADAM_B1 ** ADAM_STEP)
ADAM_C2 = 1.0 / (1.0 - ADAM_B2 ** ADAM_STEP)


def _adamw_math(w, g, m, v):
    m = ADAM_B1 * m + (1.0 - ADAM_B1) * g
    v = ADAM_B2 * v + (1.0 - ADAM_B2) * (g * g)
    delta = -ADAM_LR * ((m * ADAM_C1) / (jnp.sqrt(v * ADAM_C2) + ADAM_EPS) + ADAM_WD * w)
    return delta, m, v


def _adamw(w, g, m, v, name):
    R, n = w.shape
    tr = _row_tile(R, n)

    def body(w_ref, g_ref, m_ref, v_ref, d_ref, nm_ref, nv_ref):
        d_ref[...], nm_ref[...], nv_ref[...] = _adamw_math(w_ref[...], g_ref[...], m_ref[...], v_ref[...])

    spec = pl.BlockSpec((tr, n), lambda i: (i, 0))
    return pl.pallas_call(
        body, name=name, grid=(R // tr,), in_specs=[spec] * 4, out_specs=[spec] * 3,
        out_shape=[jax.ShapeDtypeStruct((R, n), F32)] * 3, compiler_params=_cparams("parallel"),
    )(w, g, m, v)


def _adamw_small(w, gall, m, v, name):
    M, n = w.shape

    def body(w_ref, g_ref, m_ref, v_ref, gs_ref, d_ref, nm_ref, nv_ref):
        g = g_ref[0:M, :]
        for d in range(1, N_DEV):
            g = g + g_ref[d * M:(d + 1) * M, :]
        gs_ref[...] = g
        d_ref[...], nm_ref[...], nv_ref[...] = _adamw_math(w_ref[...], g, m_ref[...], v_ref[...])

    return pl.pallas_call(
        body, name=name, out_shape=[jax.ShapeDtypeStruct((M, n), F32)] * 4,
    )(w, gall, m, v)


SMALL_ROWS = 24
MID_ROWS = 4


def _pack_small(ln1_g, ln1_b, ln2_g, ln2_b, norm_w, sinks, a_log, dt_bias):
    mixed = jnp.concatenate([norm_w.reshape(-1), sinks.reshape(-1), a_log.reshape(-1), dt_bias.reshape(-1)])
    mixed = jnp.pad(mixed, (0, D_MODEL - mixed.shape[0]))[None]
    pad = jnp.zeros((SMALL_ROWS - 4 * DEPTH - 1, D_MODEL), F32)
    return jnp.concatenate([ln1_g, ln1_b, ln2_g, ln2_b, mixed, pad], axis=0)


def _unpack_small(p):
    mixed = p[4 * DEPTH]
    return (p[0:4], p[4:8], p[8:12], p[12:16], mixed[0:256].reshape(2, 128), mixed[256:272].reshape(2, 8),
            mixed[272:280].reshape(2, 4), mixed[280:288].reshape(2, 4))


def _pack_mid(conv_w, rconv_w, rconv_b, b_a, b_x, lam):
    flat = jnp.concatenate([conv_w.reshape(2, -1), rconv_w.reshape(2, -1), rconv_b, b_a, b_x, lam], axis=1)
    return jnp.pad(flat, ((0, 0), (0, MID_ROWS * D_MODEL - flat.shape[1]))).reshape(2, MID_ROWS, D_MODEL)


def _unpack_mid(p):
    lead = p.shape[:-2]
    f = p.reshape(lead + (MID_ROWS * D_MODEL,))
    return (f[..., 0:1536].reshape(lead + (4, 384)), f[..., 1536:2560].reshape(lead + (4, 256)),
            f[..., 2560:2816], f[..., 2816:3072], f[..., 3072:3328], f[..., 3328:3584])


def _cols_from_chips(g):
    p, L, R, n = g.shape
    return g.transpose(1, 2, 0, 3).reshape(L, R, p * n)


def _rows_from_chips(g):
    p, L, R, n = g.shape
    return g.transpose(1, 0, 2, 3).reshape(L, p * R, n)


def _cols_to_chips(g):
    L, R, n4 = g.shape
    return g.reshape(L, R, N_CHIPS, n4 // N_CHIPS).transpose(2, 0, 1, 3)


def _rows_to_chips(g):
    L, R4, n = g.shape
    return g.reshape(L, N_CHIPS, R4 // N_CHIPS, n).transpose(1, 0, 2, 3)


def _halves(a):
    return a.reshape(2, -1, a.shape[-1])


def _pad_hyb_cols(w):
    z = jnp.zeros(w.shape[:-1] + (HYB_PAD - HP_BG - 2 * B_HEADS,), w.dtype)
    return jnp.concatenate([w[..., 0:512], w[..., 768:2304], w[..., 2304:2816], w[..., 512:768], w[..., 2816:2824], z], axis=-1)


def _unpad_hyb_cols(w):
    return jnp.concatenate([w[..., 0:512], w[..., 2560:2816], w[..., 512:2048], w[..., 2048:2560], w[..., 2816:2824]], axis=-1)


def _hybrid_fwd(x, W, j, tables, sfx):
    cos, sin_s = tables
    T = x.shape[0]
    N = T // B_CHUNK
    proj = _matmul(x, W["hyb_w_in"][j], "nn", "hyb_in" + sfx)
    sinks_b = jnp.broadcast_to(W["hyb_sinks"][j][:, None], (A_Q_HEADS, LANES))
    o_a, lse = _attn_fwd(proj, cos, sin_s, sinks_b, "attn_fwd" + sfx)
    qkvc = _dnconv_fwd(proj, W["hyb_conv_w"][j], "dnconv_fwd" + sfx)
    bg = proj[:, HP_BG:HP_BG + 2 * B_HEADS]
    beta = jax.nn.sigmoid(bg[:, :B_HEADS])
    pre = bg[:, B_HEADS:] + W["hyb_dt_bias"][j][None]
    g = -jnp.exp(W["hyb_a_log"][j])[None] * jax.nn.softplus(pre)
    gcol = g.T.reshape(B_HEADS, N, B_CHUNK, 1)
    grow = g.T.reshape(B_HEADS, N, 1, B_CHUNK)
    bcol = beta.T.reshape(B_HEADS, N, B_CHUNK, 1)
    nw = W["hyb_norm_w"][j][None]
    o_b, states = _dn_fwd(qkvc, proj, nw, gcol, grow, bcol, "dn_fwd" + sfx)
    mix = jnp.concatenate([o_a, o_b], axis=1).astype(ACT_DTYPE)
    y = _matmul(mix, W["hyb_w_out"][j], "nn", "hyb_out" + sfx)
    res = dict(proj=proj, o_a=o_a, lse=lse, qkvc=qkvc, beta=beta, pre=pre, g=g, gcol=gcol, grow=grow, bcol=bcol,
               states=states, mix=mix, sinks_b=sinks_b, nw=nw)
    return y, res


def _hybrid_bwd(x, du, W, j, res, tables, sfx):
    cos, sin_s = tables
    T = x.shape[0]
    proj = res["proj"]
    d_wout = _matmul(res["mix"], du, "tn", "hyb_out_dw" + sfx)
    dmix = _matmul(du, W["hyb_w_out"][j], "nt", "hyb_out_dx" + sfx)
    dq, dkc, dkp, dvc, dvp, dsink = _attn_bwd(proj, cos, sin_s, res["sinks_b"], res["o_a"], res["lse"], dmix,
                                               "attn_bwd" + sfx)
    zpad = jnp.zeros((WINDOW, LANES), F32)
    dk = dkc + jnp.concatenate([dkp[WINDOW:], zpad], axis=0)
    dv = dvc + jnp.concatenate([dvp[WINDOW:], zpad], axis=0)
    dqc, dkcv, dvcv, dz, dg4, dbeta4, dnw = _dn_bwd(res["qkvc"], proj, res["nw"], res["gcol"], res["grow"], res["bcol"],
                                                    res["states"], dmix, "dn_bwd" + sfx)
    dqkvc = jnp.concatenate([dqc, dkcv, dvcv], axis=1)
    dqkvb, dconv = _dnconv_bwd(proj, W["hyb_conv_w"][j], dqkvc, "dnconv_bwd" + sfx)
    dg = dg4.reshape(B_HEADS, T).T
    dbeta = dbeta4.reshape(B_HEADS, T).T
    beta = res["beta"]
    dbeta_logit = dbeta * beta * (1.0 - beta)
    da_logit = dg * (-jnp.exp(W["hyb_a_log"][j]))[None] * jax.nn.sigmoid(res["pre"])
    d_dt_bias = jnp.sum(da_logit, axis=0)
    d_a_log = jnp.sum(dg * res["g"], axis=0)
    zcols = jnp.zeros((T, HYB_PAD - HP_BG - 2 * B_HEADS), F32)
    dproj = jnp.concatenate([dq, dqkvb, dz, dk, dv, dbeta_logit, da_logit, zcols], axis=1).astype(ACT_DTYPE)
    d_win = _matmul(x, dproj, "tn", "hyb_in_dw" + sfx)
    dx = _matmul(dproj, W["hyb_w_in"][j], "nt", "hyb_in_dx" + sfx, epi=_epi_add_residual, extra=du)
    grads = dict(hyb_w_in=d_win, hyb_w_out=d_wout, hyb_sinks=dsink[0], hyb_conv_w=dconv, hyb_a_log=d_a_log,
                 hyb_dt_bias=d_dt_bias, hyb_norm_w=jnp.sum(dnw[:, 0, :], axis=0))
    return dx, grads


def _rec_fwd(x, W, j, sfx):
    proj = _matmul(x, W["rec_w_in"][j], "nn", "rec_in" + sfx)
    sp = jax.nn.softplus(-W["rec_lambda"][j])[None]
    hg, h = _rglru_fwd(proj, W["rec_conv_w"][j], W["rec_conv_b"][j][None], W["rec_w_a"][j], W["rec_w_x"][j],
                       W["rec_b_a"][j][None], W["rec_b_x"][j][None], sp, "rglru_fwd" + sfx)
    y = _matmul(hg, W["rec_w_out"][j], "nn", "rec_out" + sfx)
    return y, dict(proj=proj, hg=hg, h=h, sp=sp)


def _rec_bwd(x, du, W, j, res, sfx):
    d_wout = _matmul(res["hg"], du, "tn", "rec_out_dw" + sfx)
    dhg = _matmul(du, W["rec_w_out"][j], "nt", "rec_out_dx" + sfx)
    dproj, dcw, dcb, dwa, dwx, dba, dbx, dsp = _rglru_bwd(
        res["proj"], W["rec_conv_w"][j], W["rec_conv_b"][j][None], W["rec_w_a"][j], W["rec_w_x"][j],
        W["rec_b_a"][j][None], W["rec_b_x"][j][None], res["sp"], res["h"], dhg, "rglru_bwd" + sfx)
    d_lam = dsp[0] * (-jax.nn.sigmoid(-W["rec_lambda"][j]))
    d_win = _matmul(x, dproj, "tn", "rec_in_dw" + sfx)
    dx = _matmul(dproj, W["rec_w_in"][j], "nt", "rec_in_dx" + sfx, epi=_epi_add_residual, extra=du)
    grads = dict(rec_w_in=d_win, rec_w_out=d_wout, rec_conv_w=dcw, rec_conv_b=dcb[0], rec_w_a=dwa, rec_w_x=dwx,
                 rec_b_a=dba[0], rec_b_x=dbx[0], rec_lambda=d_lam)
    return dx, grads


def _local_step(x, tgt, W):
    T = x.shape[0]
    tables = _rope_tables(T)
    acts = []
    for layer in range(DEPTH):
        j, sfx = layer // 2, ""
        if layer % 2 == 0:
            y, res = _hybrid_fwd(x, W, j, tables, sfx)
        else:
            y, res = _rec_fwd(x, W, j, sfx)
        x1 = _ln_fwd(x, y, W["ln1_g"][layer][None], W["ln1_b"][layer][None], "ln_fwd")
        h1 = _matmul(x1, W["mlp_w1"][layer], "nn", "mlp_up", out_dtype=ACT_DTYPE)
        y2 = _matmul(h1, W["mlp_w2"][layer], "nn", "mlp_down", a_fn=_relu2)
        x2 = _ln_fwd(x1, y2, W["ln2_g"][layer][None], W["ln2_b"][layer][None], "ln_fwd")
        acts.append(dict(x=x, y=y, res=res, x1=x1, h1=h1, y2=y2))
        x = x2
    dx, loss = _loss_head(x, tgt, "loss_head")
    per_layer = [None] * DEPTH
    for layer in reversed(range(DEPTH)):
        j, a = layer // 2, acts[layer]
        du2, dg2, db2 = _ln_bwd(a["x1"], a["y2"], W["ln2_g"][layer][None], dx, "ln_bwd")
        d_w2 = _matmul(a["h1"], du2, "tn", "mlp_down_dw", a_fn=_relu2)
        dh1 = _matmul(du2, W["mlp_w2"][layer], "nt", "mlp_down_dx", epi=_epi_drelu2, extra=a["h1"], out_dtype=ACT_DTYPE)
        d_w1 = _matmul(a["x1"], dh1, "tn", "mlp_up_dw")
        dx1 = _matmul(dh1, W["mlp_w1"][layer], "nt", "mlp_up_dx", epi=_epi_add_residual, extra=du2)
        du1, dg1, db1 = _ln_bwd(a["x"], a["y"], W["ln1_g"][layer][None], dx1, "ln_bwd")
        if layer % 2 == 0:
            dx, g = _hybrid_bwd(a["x"], du1, W, j, a["res"], tables, "")
        else:
            dx, g = _rec_bwd(a["x"], du1, W, j, a["res"], "")
        g.update(ln1_g=dg1[0], ln1_b=db1[0], ln2_g=dg2[0], ln2_b=db2[0], mlp_w1=d_w1, mlp_w2=d_w2)
        per_layer[layer] = g
    grads = {}
    for name in ("ln1_g", "ln1_b", "ln2_g", "ln2_b", "mlp_w1", "mlp_w2"):
        grads[name] = jnp.stack([per_layer[l][name] for l in range(DEPTH)])
    for name in per_layer[0]:
        if name not in grads:
            grads[name] = jnp.stack([per_layer[l][name] for l in (0, 2)])
    for name in per_layer[1]:
        if name not in grads:
            grads[name] = jnp.stack([per_layer[l][name] for l in (1, 3)])
    return loss, dx, grads


BIG = ("hyb_w_in", "hyb_w_out", "rec_w_in", "rec_w_out", "mlp_w1", "mlp_w2", "rec_w_a", "rec_w_x")
COL_SHARDED = ("hyb_w_in", "rec_w_in", "mlp_w1")
MID = ("hyb_conv_w", "rec_conv_w", "rec_conv_b", "rec_b_a", "rec_b_x", "rec_lambda")
SMALL = ("ln1_g", "ln1_b", "ln2_g", "ln2_b", "hyb_norm_w", "hyb_sinks", "hyb_a_log", "hyb_dt_bias")
WEIGHTS = ("hyb_w_in", "hyb_sinks", "hyb_conv_w", "hyb_a_log", "hyb_dt_bias", "hyb_norm_w", "hyb_w_out", "rec_w_in",
           "rec_conv_w", "rec_conv_b", "rec_w_a", "rec_b_a", "rec_w_x", "rec_b_x", "rec_lambda", "rec_w_out", "ln1_g",
           "ln1_b", "mlp_w1", "mlp_w2", "ln2_g", "ln2_b")


def _gather_full_weights(w):
    shards = [_halves(w[k].astype(MXU_DTYPE)) for k in BIG]
    shards.append(_pack_mid(*[w[k] for k in MID]))
    got = _all_gather_weights(shards, "all_gather_weights")
    me = 2 * lax.axis_index("x") + lax.axis_index("y")
    chip = lax.broadcasted_iota(jnp.int32, (N_CHIPS, 1, 1, 1), 0)
    got = [jnp.where(chip == me, s[None], g) for s, g in zip(shards, got)]
    W = {}
    for k, g in zip(BIG, got[:-1]):
        L = w[k].shape[0]
        if k in ("rec_w_a", "rec_w_x"):
            g5 = g.reshape((N_CHIPS,) + w[k].shape)
            W[k] = g5.transpose(1, 2, 0, 3, 4).reshape(L, LRU_BLOCKS, LRU_BLOCK_W, LRU_BLOCK_W)
        else:
            g4 = g.reshape((N_CHIPS,) + w[k].shape)
            W[k] = _cols_from_chips(g4) if k in COL_SHARDED else _rows_from_chips(g4)
    W["hyb_w_in"] = _pad_hyb_cols(W["hyb_w_in"])
    conv_w, rconv_w, rconv_b, b_a, b_x, lam = _unpack_mid(got[-1])
    W["hyb_conv_w"] = conv_w.transpose(1, 2, 0, 3).reshape(2, CONV_K, 3 * B_W)
    W["rec_conv_w"] = rconv_w.transpose(1, 2, 0, 3).reshape(2, CONV_K, D_MODEL)
    for k, v in (("rec_conv_b", rconv_b), ("rec_b_a", b_a), ("rec_b_x", b_x), ("rec_lambda", lam)):
        W[k] = v.transpose(1, 0, 2).reshape(2, D_MODEL)
    for k in SMALL:
        W[k] = w[k]
    return W


def _grads_by_chip(grads):
    out = []
    g = dict(grads)
    g["hyb_w_in"] = _unpad_hyb_cols(g["hyb_w_in"])
    for k in BIG:
        v = g[k]
        if k in ("rec_w_a", "rec_w_x"):
            L = v.shape[0]
            v = v.reshape(L, LRU_BLOCKS, N_CHIPS, LRU_BLOCK_W // N_CHIPS, LRU_BLOCK_W).transpose(2, 0, 1, 3, 4)
        else:
            v = _cols_to_chips(v) if k in COL_SHARDED else _rows_to_chips(v)
        out.append(v.reshape(N_CHIPS, 2, -1, v.shape[-1]))
    conv_w = g["hyb_conv_w"].reshape(2, CONV_K, N_CHIPS, -1).transpose(2, 0, 1, 3)
    rconv_w = g["rec_conv_w"].reshape(2, CONV_K, N_CHIPS, -1).transpose(2, 0, 1, 3)
    vecs = [g[k].reshape(2, N_CHIPS, -1).transpose(1, 0, 2) for k in ("rec_conv_b", "rec_b_a", "rec_b_x", "rec_lambda")]
    mid = jnp.stack([_pack_mid(conv_w[p], rconv_w[p], *[v[p] for v in vecs]) for p in range(N_CHIPS)])
    out.append(mid)
    return out


def kernel(x, hyb_w_in, hyb_sinks, hyb_conv_w, hyb_a_log, hyb_dt_bias, hyb_norm_w, hyb_w_out, rec_w_in, rec_conv_w, rec_conv_b, rec_w_a, rec_b_a, rec_w_x, rec_b_x, rec_lambda, rec_w_out, ln1_g, ln1_b, mlp_w1, mlp_w2, ln2_g, ln2_b, loss_target, m_hyb_w_in, m_hyb_sinks, m_hyb_conv_w, m_hyb_a_log, m_hyb_dt_bias, m_hyb_norm_w, m_hyb_w_out, m_rec_w_in, m_rec_conv_w, m_rec_conv_b, m_rec_w_a, m_rec_b_a, m_rec_w_x, m_rec_b_x, m_rec_lambda, m_rec_w_out, m_ln1_g, m_ln1_b, m_mlp_w1, m_mlp_w2, m_ln2_g, m_ln2_b, v_hyb_w_in, v_hyb_sinks, v_hyb_conv_w, v_hyb_a_log, v_hyb_dt_bias, v_hyb_norm_w, v_hyb_w_out, v_rec_w_in, v_rec_conv_w, v_rec_conv_b, v_rec_w_a, v_rec_b_a, v_rec_w_x, v_rec_b_x, v_rec_lambda, v_rec_w_out, v_ln1_g, v_ln1_b, v_mlp_w1, v_mlp_w2, v_ln2_g, v_ln2_b):
    args = locals()
    w = {k: args[k] for k in WEIGHTS}
    m = {k: args["m_" + k] for k in WEIGHTS}
    v = {k: args["v_" + k] for k in WEIGHTS}

    W = _gather_full_weights(w)
    loss, dx, grads = _local_step(x[0], loss_target[0], W)
    loss = lax.psum(loss[0, 0], ("x", "y", "c"))

    core = lax.axis_index("c").astype(jnp.int32)
    me = (2 * lax.axis_index("x") + lax.axis_index("y")).astype(jnp.int32)
    slots = jnp.arange(N_CHIPS, dtype=jnp.int32)
    where = jnp.concatenate([me[None], jnp.where(slots == me, (slots + 1) % N_CHIPS, slots), core[None]])
    by_chip = _grads_by_chip(grads)
    from_sibling = _rs_to_sibling(by_chip, "rs_to_sibling")
    pair = [_pair_sum(g, r, core[None], "pair_sum") for g, r in zip(by_chip, from_sibling)]
    from_chips = _rs_across_chips(pair, "rs_across_chips")
    half = [_chip_sum(r, p, where, "chip_sum") for r, p in zip(from_chips, pair)]
    joined = _rs_join_halves(half, "rs_join_halves")

    g_out, d_out, m_out, v_out = {}, {}, {}, {}

    def update(name, g2d):
        shape = w[name].shape
        n = g2d.shape[-1]
        d, nm, nv = _adamw(w[name].reshape(-1, n), g2d, m[name].reshape(-1, n), v[name].reshape(-1, n), "adamw")
        g_out[name], d_out[name] = g2d.reshape(shape), d.reshape(shape)
        m_out[name], v_out[name] = nm.reshape(shape), nv.reshape(shape)

    for k, g in zip(BIG, joined[:-1]):
        update(k, g.reshape(-1, g.shape[-1]))
    mid_w, mid_m, mid_v = (_pack_mid(*[t[k] for k in MID]).reshape(-1, D_MODEL) for t in (w, m, v))
    mid_g = joined[-1].reshape(-1, D_MODEL)
    mid_d, mid_nm, mid_nv = _adamw(mid_w, mid_g, mid_m, mid_v, "adamw_mid")
    for dst, packed in ((g_out, mid_g), (d_out, mid_d), (m_out, mid_nm), (v_out, mid_nv)):
        for k, val in zip(MID, _unpack_mid(packed.reshape(2, MID_ROWS, D_MODEL))):
            dst[k] = val.reshape(w[k].shape)

    small_g = _pack_small(*[grads[k] for k in SMALL])
    small_all = _all_gather_small(small_g, "all_gather_small")
    sw, sm, sv = (_pack_small(*[t[k] for k in SMALL]) for t in (w, m, v))
    sg, sd, snm, snv = _adamw_small(sw, small_all, sm, sv, "adamw_small")
    for dst, packed in ((g_out, sg), (d_out, sd), (m_out, snm), (v_out, snv)):
        for k, val in zip(SMALL, _unpack_small(packed)):
            dst[k] = val

    return (loss, dx[None], *[g_out[k] for k in WEIGHTS], *[d_out[k] for k in WEIGHTS],
            *[m_out[k] for k in WEIGHTS], *[v_out[k] for k in WEIGHTS])
```

```python
import functools

import jax
import jax.numpy as jnp
import numpy as np
from jax import lax
from jax.experimental import pallas as pl
from jax.experimental.pallas import tpu as pltpu

F32 = jnp.float32
MXU_DTYPE = jnp.bfloat16
ACT_DTYPE = jnp.bfloat16
ICI_DTYPE = jnp.bfloat16

D_MODEL = 1024
DEPTH = 4
A_HEAD_DIM = 64
A_Q_HEADS = 8
A_KV_HEADS = 2
WINDOW = 128
ROPE_THETA = 10000.0
B_HEADS = 4
B_HEAD_DIM = 128
B_CHUNK = 64
CONV_K = 4
LRU_BLOCKS = 4
LRU_BLOCK_W = D_MODEL // LRU_BLOCKS
LRU_C = 8.0
D_FF = 4 * D_MODEL
A_Q_W = A_Q_HEADS * A_HEAD_DIM
A_KV_W = A_KV_HEADS * A_HEAD_DIM
B_W = B_HEADS * B_HEAD_DIM
HYB_PROJ = A_Q_W + 2 * A_KV_W + 4 * B_W + 2 * B_HEADS
DN_ALPHA = (2 * DEPTH) ** 0.25
LN_EPS = 1e-5
NORM_EPS = 1e-6
ADAM_LR = 0.001
ADAM_B1 = 0.9
ADAM_B2 = 0.999
ADAM_EPS = 1e-08
ADAM_WD = 0.01
ADAM_STEP = 10

HP_Q = 0
HP_QKVB = 512
HP_Z = 2048
HP_K = 2560
HP_V = 2688
HP_BG = 2816
HYB_PAD = 3072

N_CHIPS = 4
N_DEV = 8
V7X_VMEM_LIMIT = 48 * 1024 * 1024
LANES = 128
SUBLANES = 8
NEG_BIG = -1e30

NN = (((1,), (0,)), ((), ()))
NT = (((1,), (1,)), ((), ()))
TN = (((0,), (0,)), ((), ()))


def _cparams(*sem):
    return pltpu.CompilerParams(dimension_semantics=sem, vmem_limit_bytes=V7X_VMEM_LIMIT)


def _dot(a, b, dims=NN):
    return lax.dot_general(a.astype(MXU_DTYPE), b.astype(MXU_DTYPE), dims, preferred_element_type=F32)


def _split_bf16(a):
    hi = a.astype(jnp.bfloat16)
    return hi, (a - hi.astype(F32)).astype(jnp.bfloat16)


def _dotf(a, b, dims=NN):
    ah, al = _split_bf16(a)
    bh, bl = _split_bf16(b)
    dg = functools.partial(lax.dot_general, dimension_numbers=dims, preferred_element_type=F32)
    return dg(ah, bh) + (dg(ah, bl) + dg(al, bh))


def _tile(dim, pref):
    t = min(dim, pref)
    while dim % t:
        t //= 2
    return t


def _sigmoid(x):
    return 1.0 / (1.0 + jnp.exp(-x))


def _silu(x):
    return x * _sigmoid(x)


def _dsilu(x):
    s = _sigmoid(x)
    return s * (1.0 + x * (1.0 - s))


GELU_C = 0.7978845608028654
GELU_A = 0.044715


def _gelu(x):
    return 0.5 * x * (1.0 + jnp.tanh(GELU_C * (x + GELU_A * x * x * x)))


def _dgelu(x):
    t = jnp.tanh(GELU_C * (x + GELU_A * x * x * x))
    return 0.5 * (1.0 + t) + 0.5 * x * (1.0 - t * t) * GELU_C * (1.0 + 3.0 * GELU_A * x * x)


def _matmul(a, b, mode, name, *, tm=1024, tn=1024, tk=1024, a_fn=None, epi=None, extra=None, out_dtype=F32,
            b_chips=None, out_chips=None):
    if mode == "tn":
        K, M = a.shape
    else:
        M, K = a.shape
    if b_chips is not None:
        g, b_layer = b_chips
        r, n = b.shape[2:]
        n_dim, k_dim = (r, n) if mode == "nt" else (n, r)
        N = N_CHIPS * n_dim if g == "j" else n_dim
        assert K == (N_CHIPS * k_dim if g == "k" else k_dim)
        if g == "j":
            tn = n_dim
        else:
            tk = k_dim
    elif mode == "nt":
        N = b.shape[0]
    else:
        N = b.shape[1]
    if out_chips is not None:
        og, o_layer, o_buf = out_chips
        if og == "j":
            tn = o_buf.shape[3]
        else:
            tm = o_buf.shape[2]
    tm, tn, tk = _tile(M, tm), _tile(N, tn), _tile(K, tk)
    nk = K // tk
    if mode == "tn":
        a_spec = pl.BlockSpec((tk, tm), lambda i, j, k: (k, i))
    else:
        a_spec = pl.BlockSpec((tm, tk), lambda i, j, k: (i, k))
    b_block = (tn, tk) if mode == "nt" else (tk, tn)
    if b_chips is None:
        b_spec = pl.BlockSpec(b_block, (lambda i, j, k: (j, k)) if mode == "nt" else (lambda i, j, k: (k, j)))
    elif mode == "nt":
        b_spec = pl.BlockSpec((None, None) + b_block, (lambda i, j, k: (j, b_layer, 0, k)) if g == "j"
                              else (lambda i, j, k: (k, b_layer, j, 0)))
    else:
        b_spec = pl.BlockSpec((None, None) + b_block, (lambda i, j, k: (j, b_layer, k, 0)) if g == "j"
                              else (lambda i, j, k: (k, b_layer, 0, j)))
    o_spec = pl.BlockSpec((tm, tn), lambda i, j, k: (i, j))
    e_spec = o_spec
    if out_chips is not None:
        o_spec = pl.BlockSpec((None, None, tm, tn), (lambda i, j, k: (j, o_layer, i, 0)) if og == "j"
                              else (lambda i, j, k: (i, o_layer, 0, j)))
    dims = {"nn": NN, "nt": NT, "tn": TN}[mode]
    has_extra = extra is not None
    n_in = 2 + has_extra + (out_chips is not None)

    def body(*refs):
        a_ref, b_ref = refs[0], refs[1]
        e_ref = refs[2] if has_extra else None
        o_ref = refs[n_in]
        av = a_ref[...]
        if a_fn is not None:
            av = a_fn(av)
        part = _dot(av, b_ref[...], dims)

        def finish(acc):
            if epi is not None:
                acc = epi(acc, e_ref[...])
            o_ref[...] = acc.astype(out_dtype)

        if nk == 1:
            finish(part)
        else:
            acc_ref = refs[-1]
            k = pl.program_id(2)

            @pl.when(k == 0)
            def _():
                acc_ref[...] = part

            @pl.when(k > 0)
            def _():
                acc_ref[...] += part

            @pl.when(k == nk - 1)
            def _():
                finish(acc_ref[...])

    in_specs = [a_spec, b_spec] + ([e_spec] if has_extra else [])
    args = (a, b) + ((extra,) if has_extra else ())
    out_shape = jax.ShapeDtypeStruct((M, N), out_dtype)
    aliases = {}
    if out_chips is not None:
        in_specs.append(pl.BlockSpec(memory_space=pl.ANY))
        args += (o_buf,)
        out_shape = jax.ShapeDtypeStruct(o_buf.shape, o_buf.dtype)
        aliases = {n_in - 1: 0}
    return pl.pallas_call(
        body, name=name, grid=(M // tm, N // tn, nk), in_specs=in_specs, out_specs=o_spec, out_shape=out_shape,
        input_output_aliases=aliases,
        scratch_shapes=[pltpu.VMEM((tm, tn), F32)] if nk > 1 else [],
        compiler_params=_cparams("parallel", "parallel", "arbitrary"),
    )(*args)


def _relu2(v):
    r = jnp.maximum(v, 0.0)
    return r * r


def _epi_drelu2(acc, h):
    return acc * (2.0 * jnp.maximum(h, 0.0))


def _epi_add_residual(acc, du):
    return acc + DN_ALPHA * du


def _ln_fwd(x, y, g, b, name):
    T, D = x.shape
    tr = _tile(T, 512)

    def body(x_ref, y_ref, g_ref, b_ref, o_ref, ob_ref):
        u = DN_ALPHA * x_ref[...] + y_ref[...]
        mu = jnp.mean(u, axis=-1, keepdims=True)
        d = u - mu
        var = jnp.mean(d * d, axis=-1, keepdims=True)
        o = d * lax.rsqrt(var + LN_EPS) * g_ref[...] + b_ref[...]
        o_ref[...] = o
        ob_ref[...] = o.astype(ACT_DTYPE)

    row = pl.BlockSpec((tr, D), lambda i: (i, 0))
    vec = pl.BlockSpec((1, D), lambda i: (0, 0))
    return pl.pallas_call(
        body, name=name, grid=(T // tr,), in_specs=[row, row, vec, vec], out_specs=[row, row],
        out_shape=[jax.ShapeDtypeStruct((T, D), F32), jax.ShapeDtypeStruct((T, D), ACT_DTYPE)],
        compiler_params=_cparams("parallel"),
    )(x, y, g, b)


def _ln_bwd(x, y, g, dout, name):
    T, D = x.shape
    tr = _tile(T, 512)

    def body(x_ref, y_ref, g_ref, d_ref, du_ref, dub_ref, dg_ref, db_ref):
        i = pl.program_id(0)
        u = DN_ALPHA * x_ref[...] + y_ref[...]
        mu = jnp.mean(u, axis=-1, keepdims=True)
        d = u - mu
        rstd = lax.rsqrt(jnp.mean(d * d, axis=-1, keepdims=True) + LN_EPS)
        xhat = d * rstd
        dout_v = d_ref[...]
        dxh = dout_v * g_ref[...]
        m1 = jnp.mean(dxh, axis=-1, keepdims=True)
        m2 = jnp.mean(dxh * xhat, axis=-1, keepdims=True)
        du = rstd * (dxh - m1 - xhat * m2)
        du_ref[...] = du
        dub_ref[...] = du.astype(ACT_DTYPE)
        pg = jnp.sum(dout_v * xhat, axis=0, keepdims=True)
        pb = jnp.sum(dout_v, axis=0, keepdims=True)

        @pl.when(i == 0)
        def _():
            dg_ref[...] = pg
            db_ref[...] = pb

        @pl.when(i > 0)
        def _():
            dg_ref[...] += pg
            db_ref[...] += pb

    row = pl.BlockSpec((tr, D), lambda i: (i, 0))
    vec = pl.BlockSpec((1, D), lambda i: (0, 0))
    return pl.pallas_call(
        body, name=name, grid=(T // tr,), in_specs=[row, row, vec, row], out_specs=[row, row, vec, vec],
        out_shape=[jax.ShapeDtypeStruct((T, D), F32), jax.ShapeDtypeStruct((T, D), ACT_DTYPE),
                   jax.ShapeDtypeStruct((1, D), F32), jax.ShapeDtypeStruct((1, D), F32)],
        compiler_params=_cparams("arbitrary"),
    )(x, y, g, dout)


def _loss_head(y, tgt, name):
    T, D = y.shape
    tr = _tile(T, 512)

    def body(y_ref, t_ref, dy_ref, l_ref):
        i = pl.program_id(0)
        e = y_ref[...] - t_ref[...]
        dy_ref[...] = e * (1.0 / D)
        part = jnp.sum(e * e, axis=(0, 1), keepdims=True) * (0.5 / D)

        @pl.when(i == 0)
        def _():
            l_ref[...] = part

        @pl.when(i > 0)
        def _():
            l_ref[...] += part

    row = pl.BlockSpec((tr, D), lambda i: (i, 0))
    one = pl.BlockSpec((1, 1), lambda i: (0, 0))
    return pl.pallas_call(
        body, name=name, grid=(T // tr,), in_specs=[row, row], out_specs=[row, one],
        out_shape=[jax.ShapeDtypeStruct((T, D), F32), jax.ShapeDtypeStruct((1, 1), F32)],
        compiler_params=_cparams("arbitrary"),
    )(y, tgt)


def _swap_half(x):
    n = x.shape[-1]
    lane = lax.broadcasted_iota(jnp.int32, x.shape, 1)
    first = (lane % A_HEAD_DIM) < (A_HEAD_DIM // 2)
    return jnp.where(first, pltpu.roll(x, n - A_HEAD_DIM // 2, axis=1), pltpu.roll(x, A_HEAD_DIM // 2, axis=1))


def _rope(x, cos, sin_signed):
    return x * cos + _swap_half(x) * sin_signed


def _rope_t(dy, cos, sin_signed):
    return dy * cos + _swap_half(dy * sin_signed)


def _rope_tables(T):
    half = A_HEAD_DIM // 2
    inv_freq = np.float32(ROPE_THETA) ** (-np.arange(half, dtype=np.float32) / np.float32(half))
    ang = np.arange(T, dtype=np.float32)[:, None] * inv_freq[None, :]
    cos = np.tile(np.cos(ang), (1, 4))
    sin = np.sin(ang)
    sin_signed = np.tile(np.concatenate([-sin, sin], axis=1), (1, 2))
    return jnp.asarray(cos, F32), jnp.asarray(sin_signed, F32)


def _band_mask(n):
    qi = lax.broadcasted_iota(jnp.int32, (WINDOW, 2 * WINDOW), 0)
    kj = lax.broadcasted_iota(jnp.int32, (WINDOW, 2 * WINDOW), 1)
    return (kj > qi) & (kj <= qi + WINDOW) & ((n > 0) | (kj >= WINDOW))


def _place(v, src_half, dst_half):
    lane = lax.broadcasted_iota(jnp.int32, v.shape, 1)
    if src_half != dst_half:
        v = pltpu.roll(v, A_HEAD_DIM, axis=1)
    keep = (lane >= A_HEAD_DIM) if dst_half else (lane < A_HEAD_DIM)
    return jnp.where(keep, v, 0.0)


def _attn_specs():
    kb, vb = HP_K // LANES, HP_V // LANES
    prev = lambda n: jnp.maximum(n - 1, 0)
    return dict(
        q=pl.BlockSpec((WINDOW, A_Q_W), lambda n: (n, 0)),
        kc=pl.BlockSpec((WINDOW, LANES), lambda n: (n, kb)),
        kp=pl.BlockSpec((WINDOW, LANES), lambda n: (prev(n), kb)),
        vc=pl.BlockSpec((WINDOW, LANES), lambda n: (n, vb)),
        vp=pl.BlockSpec((WINDOW, LANES), lambda n: (prev(n), vb)),
        tq=pl.BlockSpec((WINDOW, LANES), lambda n: (n, 0)),
        tp=pl.BlockSpec((WINDOW, LANES), lambda n: (prev(n), 0)),
        sink=pl.BlockSpec((A_Q_HEADS, LANES), lambda n: (0, 0)),
        row512=pl.BlockSpec((WINDOW, A_Q_W), lambda n: (n, 0)),
        row128=pl.BlockSpec((WINDOW, LANES), lambda n: (n, 0)),
        lse=pl.BlockSpec((WINDOW, A_Q_HEADS), lambda n: (n, 0)),
    )


def _attn_fwd(proj, cos, sin_s, sinks_b, name):
    T = proj.shape[0]
    sp = _attn_specs()

    def body(q_ref, kc_ref, kp_ref, vc_ref, vp_ref, cq_ref, sq_ref, cp_ref, sp_ref, sink_ref, o_ref, l_ref):
        n = pl.program_id(0)
        cq, sq = cq_ref[...], sq_ref[...]
        cq4, sq4 = jnp.tile(cq, (1, A_Q_W // LANES)), jnp.tile(sq, (1, A_Q_W // LANES))
        q = _rope(q_ref[...], cq4, sq4) * (A_HEAD_DIM ** -0.5)
        kc = _rope(kc_ref[...], cq, sq)
        kp = _rope(kp_ref[...], cp_ref[...], sp_ref[...])
        kk = jnp.concatenate([kp, kc], axis=0)
        vv = jnp.concatenate([vp_ref[...], vc_ref[...]], axis=0)
        mask = _band_mask(n)
        lane = lax.broadcasted_iota(jnp.int32, (WINDOW, LANES), 1)
        outs = []
        lane8 = lax.broadcasted_iota(jnp.int32, (WINDOW, A_Q_HEADS), 1)
        lse = jnp.zeros((WINDOW, A_Q_HEADS), F32)
        for pb in range(A_Q_HEADS // 2):
            kvh = pb // 2
            q2 = q[:, pb * LANES:(pb + 1) * LANES]
            halves = []
            for e in range(2):
                hq = 2 * pb + e
                s = _dot(_place(q2, e, kvh), kk, NT)
                s = jnp.where(mask, s, NEG_BIG)
                sk = sink_ref[hq:hq + 1, 0:1]
                m = jnp.maximum(jnp.max(s, axis=-1, keepdims=True), sk)
                p = jnp.exp(s - m)
                den = jnp.sum(p, axis=-1, keepdims=True) + jnp.exp(sk - m)
                o = _dot(p * (1.0 / den), vv, NN)
                halves.append(_place(o, kvh, e))
                lse = jnp.where(lane8 == hq, m + jnp.log(den), lse)
            outs.append(jnp.where(lane < A_HEAD_DIM, halves[0], halves[1]))
        o_ref[...] = jnp.concatenate(outs, axis=1)
        l_ref[...] = lse

    return pl.pallas_call(
        body, name=name, grid=(T // WINDOW,),
        in_specs=[sp["q"], sp["kc"], sp["kp"], sp["vc"], sp["vp"], sp["tq"], sp["tq"], sp["tp"], sp["tp"], sp["sink"]],
        out_specs=[sp["row512"], sp["lse"]],
        out_shape=[jax.ShapeDtypeStruct((T, A_Q_W), F32), jax.ShapeDtypeStruct((T, A_Q_HEADS), F32)],
        compiler_params=_cparams("parallel"),
    )(proj, proj, proj, proj, proj, cos, sin_s, cos, sin_s, sinks_b)


def _attn_bwd(proj, cos, sin_s, sinks_b, o, lse, dmix, name):
    T = proj.shape[0]
    sp = _attn_specs()

    def body(q_ref, kc_ref, kp_ref, vc_ref, vp_ref, cq_ref, sq_ref, cp_ref, sp_ref, sink_ref, o_ref, l_ref, do_ref,
             dq_ref, dkc_ref, dkp_ref, dvc_ref, dvp_ref, dsink_ref):
        n = pl.program_id(0)
        cq, sq = cq_ref[...], sq_ref[...]
        cp, sps = cp_ref[...], sp_ref[...]
        cq4, sq4 = jnp.tile(cq, (1, A_Q_W // LANES)), jnp.tile(sq, (1, A_Q_W // LANES))
        q = _rope(q_ref[...], cq4, sq4) * (A_HEAD_DIM ** -0.5)
        kc = _rope(kc_ref[...], cq, sq)
        kp = _rope(kp_ref[...], cp, sps)
        kk = jnp.concatenate([kp, kc], axis=0)
        vv = jnp.concatenate([vp_ref[...], vc_ref[...]], axis=0)
        mask = _band_mask(n)
        lane = lax.broadcasted_iota(jnp.int32, (WINDOW, LANES), 1)
        do_all, o_all, l_all = do_ref[...], o_ref[...], l_ref[...]
        dkk = jnp.zeros((2 * WINDOW, LANES), F32)
        dvv = jnp.zeros((2 * WINDOW, LANES), F32)
        dqs = []
        lane8 = lax.broadcasted_iota(jnp.int32, (WINDOW, A_Q_HEADS), 1)
        head8 = lax.broadcasted_iota(jnp.int32, (1, A_Q_HEADS), 1)
        dsk = jnp.zeros((1, A_Q_HEADS), F32)
        for pb in range(A_Q_HEADS // 2):
            kvh = pb // 2
            q2 = q[:, pb * LANES:(pb + 1) * LANES]
            do2 = do_all[:, pb * LANES:(pb + 1) * LANES]
            prod = do2 * o_all[:, pb * LANES:(pb + 1) * LANES]
            halves = []
            for e in range(2):
                hq = 2 * pb + e
                in_half = (lane >= A_HEAD_DIM) if e else (lane < A_HEAD_DIM)
                delta = jnp.sum(jnp.where(in_half, prod, 0.0), axis=-1, keepdims=True)
                qe = _place(q2, e, kvh)
                doe = _place(do2, e, kvh)
                lh = jnp.sum(jnp.where(lane8 == hq, l_all, 0.0), axis=-1, keepdims=True)
                s = _dot(qe, kk, NT)
                p = jnp.where(mask, jnp.exp(jnp.where(mask, s, NEG_BIG) - lh), 0.0)
                dvv = dvv + _dot(p, doe, TN)
                dp = _dot(doe, vv, NT)
                ds = p * (dp - delta)
                dkk = dkk + _dot(ds, qe, TN)
                halves.append(_place(_dot(ds, kk, NN), kvh, e))
                p_sink = jnp.exp(sink_ref[hq:hq + 1, 0:1] - lh)
                dsk = jnp.where(head8 == hq, -jnp.sum(p_sink * delta, axis=(0, 1), keepdims=True), dsk)
            dqs.append(jnp.where(lane < A_HEAD_DIM, halves[0], halves[1]))
        dq = jnp.concatenate(dqs, axis=1) * (A_HEAD_DIM ** -0.5)
        dq_ref[...] = _rope_t(dq, cq4, sq4)
        dkp_ref[...] = _rope_t(dkk[:WINDOW], cp, sps)
        dkc_ref[...] = _rope_t(dkk[WINDOW:], cq, sq)
        dvp_ref[...] = dvv[:WINDOW]
        dvc_ref[...] = dvv[WINDOW:]

        @pl.when(n == 0)
        def _():
            dsink_ref[...] = dsk

        @pl.when(n > 0)
        def _():
            dsink_ref[...] += dsk

    return pl.pallas_call(
        body, name=name, grid=(T // WINDOW,),
        in_specs=[sp["q"], sp["kc"], sp["kp"], sp["vc"], sp["vp"], sp["tq"], sp["tq"], sp["tp"], sp["tp"], sp["sink"],
                  sp["row512"], sp["lse"], sp["row512"]],
        out_specs=[sp["row512"], sp["row128"], sp["row128"], sp["row128"], sp["row128"],
                   pl.BlockSpec((1, A_Q_HEADS), lambda n: (0, 0))],
        out_shape=[jax.ShapeDtypeStruct((T, A_Q_W), F32)] + [jax.ShapeDtypeStruct((T, LANES), F32)] * 4
        + [jax.ShapeDtypeStruct((1, A_Q_HEADS), F32)],
        compiler_params=_cparams("arbitrary"),
    )(proj, proj, proj, proj, proj, cos, sin_s, cos, sin_s, sinks_b, o, lse, dmix)


def _shift_down(x, prev8, k):
    if k == 0:
        return x
    row = lax.broadcasted_iota(jnp.int32, prev8.shape, 0)
    r = pltpu.roll(x, k, axis=0)
    top = jnp.where(row < k, pltpu.roll(prev8, k, axis=0), r[:SUBLANES])
    return jnp.concatenate([top, r[SUBLANES:]], axis=0)


def _shift_up(x, next8, k):
    if k == 0:
        return x
    R = x.shape[0]
    row = lax.broadcasted_iota(jnp.int32, next8.shape, 0)
    r = pltpu.roll(x, R - k, axis=0)
    bot = jnp.where(row >= SUBLANES - k, pltpu.roll(next8, SUBLANES - k, axis=0), r[R - SUBLANES:])
    return jnp.concatenate([r[:R - SUBLANES], bot], axis=0)


def _conv(x, prev8, w):
    y = x * w[CONV_K - 1:CONV_K]
    for j in range(CONV_K - 1):
        y = y + _shift_down(x, prev8, CONV_K - 1 - j) * w[j:j + 1]
    return y


def _conv_bwd(x, prev8, w, dy, next8_dy):
    dx = dy * w[CONV_K - 1:CONV_K]
    dws = []
    for j in range(CONV_K - 1):
        k = CONV_K - 1 - j
        dx = dx + _shift_up(dy, next8_dy, k) * w[j:j + 1]
        dws.append(jnp.sum(dy * _shift_down(x, prev8, k), axis=0, keepdims=True))
    dws.append(jnp.sum(dy * x, axis=0, keepdims=True))
    return dx, dws


def _dnconv_fwd(proj, conv_w, name):
    T = proj.shape[0]
    R = _tile(T, 512)
    cb0 = HP_QKVB // A_Q_W

    def body(x_ref, w_ref, o_ref, prev_ref):
        i = pl.program_id(1)

        @pl.when(i == 0)
        def _():
            prev_ref[...] = jnp.zeros_like(prev_ref)

        x = x_ref[...]
        o_ref[...] = _silu(_conv(x, prev_ref[...], w_ref[...]))
        prev_ref[...] = x[R - SUBLANES:]

    return pl.pallas_call(
        body, name=name, grid=(3, T // R),
        in_specs=[pl.BlockSpec((R, B_W), lambda j, i: (i, cb0 + j)), pl.BlockSpec((CONV_K, B_W), lambda j, i: (0, j))],
        out_specs=pl.BlockSpec((R, B_W), lambda j, i: (i, j)),
        out_shape=jax.ShapeDtypeStruct((T, 3 * B_W), F32),
        scratch_shapes=[pltpu.VMEM((SUBLANES, B_W), F32)],
        compiler_params=_cparams("parallel", "arbitrary"),
    )(proj, conv_w)


def _dnconv_bwd(proj, conv_w, dy, name):
    T = proj.shape[0]
    R = _tile(T, 512)
    nb = T // R
    cb0 = HP_QKVB // A_Q_W
    r8 = R // SUBLANES

    def body(x_ref, xp_ref, w_ref, dy_ref, dx_ref, dw_ref, next_ref):
        i = pl.program_id(1)
        blk = nb - 1 - i

        @pl.when(i == 0)
        def _():
            next_ref[...] = jnp.zeros_like(next_ref)

        x = x_ref[...]
        prev8 = jnp.where(blk > 0, xp_ref[...], 0.0)
        w = w_ref[...]
        dpre = dy_ref[...] * _dsilu(_conv(x, prev8, w))
        dx, dw = _conv_bwd(x, prev8, w, dpre, next_ref[...])
        dx_ref[...] = dx
        next_ref[...] = dpre[:SUBLANES]

        @pl.when(i == 0)
        def _():
            for j in range(CONV_K):
                dw_ref[j:j + 1, :] = dw[j]

        @pl.when(i > 0)
        def _():
            for j in range(CONV_K):
                dw_ref[j:j + 1, :] += dw[j]

    return pl.pallas_call(
        body, name=name, grid=(3, nb),
        in_specs=[pl.BlockSpec((R, B_W), lambda j, i: (nb - 1 - i, cb0 + j)),
                  pl.BlockSpec((SUBLANES, B_W), lambda j, i: (jnp.maximum((nb - 1 - i) * r8 - 1, 0), cb0 + j)),
                  pl.BlockSpec((CONV_K, B_W), lambda j, i: (0, j)),
                  pl.BlockSpec((R, B_W), lambda j, i: (nb - 1 - i, j))],
        out_specs=[pl.BlockSpec((R, B_W), lambda j, i: (nb - 1 - i, j)),
                   pl.BlockSpec((CONV_K, B_W), lambda j, i: (0, j))],
        out_shape=[jax.ShapeDtypeStruct((T, 3 * B_W), F32), jax.ShapeDtypeStruct((CONV_K, 3 * B_W), F32)],
        scratch_shapes=[pltpu.VMEM((SUBLANES, B_W), F32)],
        compiler_params=_cparams("parallel", "arbitrary"),
    )(proj, proj, conv_w, dy)


DK_SCALE = B_HEAD_DIM ** -0.5


BNN = (((2,), (1,)), ((0,), (0,)))
BNT = (((2,), (2,)), ((0,), (0,)))
BTN = (((1,), (1,)), ((0,), (0,)))


def _tri_inv(a):
    C = a.shape[-1]
    ri = lax.broadcasted_iota(jnp.int32, (C, C), 0)
    ci = lax.broadcasted_iota(jnp.int32, (C, C), 1)
    x = jnp.where(ri == ci, 1.0, 0.0)[None] - a
    p = _dotf(a, a, BNN)
    span = 2
    while span < C:
        x = x + _dotf(x, p, BNN)
        span *= 2
        if span < C:
            p = _dotf(p, p, BNN)
    return x


def _dn_chunk(qc, kc, v, gcol, grow, bcol, s0):
    C = B_CHUNK
    ri = lax.broadcasted_iota(jnp.int32, (C, C), 0)
    ci = lax.broadcasted_iota(jnp.int32, (C, C), 1)
    incl, strict = (ri >= ci)[None], (ri > ci)[None]
    rq = lax.rsqrt(jnp.sum(qc * qc, axis=-1, keepdims=True) + NORM_EPS)
    rk = lax.rsqrt(jnp.sum(kc * kc, axis=-1, keepdims=True) + NORM_EPS)
    qn = qc * rq
    q = qn * DK_SCALE
    k = kc * rk
    gc_col = jnp.sum(jnp.where(incl, grow, 0.0), axis=2, keepdims=True)
    gc_row = jnp.sum(jnp.where((ri <= ci)[None], gcol, 0.0), axis=1, keepdims=True)
    gl = jnp.sum(gcol, axis=1, keepdims=True)
    dincl = jnp.where(incl, jnp.exp(jnp.where(incl, gc_col - gc_row, 0.0)), 0.0)
    dstrict = jnp.where(strict, dincl, 0.0)
    eg = jnp.exp(gc_col)
    ekt = jnp.exp(gl - gc_col)
    egl = jnp.exp(gl)
    kb = k * bcol
    vb = v * bcol
    kbg = kb * eg
    a = _dot(kb, k, BNT) * dstrict
    tm = _tri_inv(a)
    u = _dot(tm, vb, BNN)
    w = _dot(tm, kbg, BNN)
    vn = u - _dot(w, s0, BNN)
    qk = _dot(q, k, BNT) * dincl
    qg = q * eg
    kt = k * ekt
    o = _dot(qg, s0, BNN) + _dot(qk, vn, BNN)
    s1 = s0 * egl + _dot(kt, vn, BTN)
    return dict(rq=rq, rk=rk, qn=qn, q=q, k=k, dincl=dincl, dstrict=dstrict, eg=eg, ekt=ekt, egl=egl, kb=kb, vb=vb,
                kbg=kbg, a=a, tm=tm, w=w, vn=vn, qk=qk, qg=qg, kt=kt, o=o, s1=s1, ri=ri[None], ci=ci[None])


def _heads(ref):
    return jnp.stack([ref[:, h * B_HEAD_DIM:(h + 1) * B_HEAD_DIM] for h in range(B_HEADS)])


def _store_heads(ref, val):
    for h in range(B_HEADS):
        ref[:, h * B_HEAD_DIM:(h + 1) * B_HEAD_DIM] = val[h]


def _dn_specs(N, rev):
    ix = (lambda n: N - 1 - n) if rev else (lambda n: n)
    wide = lambda cb: pl.BlockSpec((B_CHUNK, B_W), lambda n: (ix(n), cb))
    return dict(
        q=wide(0), k=wide(1), v=wide(2), z=wide(HP_Z // B_W), dob=wide(A_Q_W // B_W), out=wide(0),
        nw=pl.BlockSpec((1, LANES), lambda n: (0, 0)),
        row=pl.BlockSpec((B_HEADS, None, 1, B_CHUNK), lambda n: (0, ix(n), 0, 0)),
        state=pl.BlockSpec((B_HEADS, None, B_HEAD_DIM, B_HEAD_DIM), lambda n: (0, ix(n), 0, 0)),
    )


def _to_col(row):
    C = row.shape[-1]
    eye = lax.broadcasted_iota(jnp.int32, (C, C), 0) == lax.broadcasted_iota(jnp.int32, (C, C), 1)
    return jnp.sum(jnp.where(eye[None], row, 0.0), axis=2, keepdims=True)


def _to_row(col):
    C = col.shape[1]
    eye = lax.broadcasted_iota(jnp.int32, (C, C), 0) == lax.broadcasted_iota(jnp.int32, (C, C), 1)
    return jnp.sum(jnp.where(eye[None], col, 0.0), axis=1, keepdims=True)


def _dn_fwd(qkvc, proj, norm_w, grow, brow, name):
    T = qkvc.shape[0]
    N = T // B_CHUNK
    sp = _dn_specs(N, False)

    def body(q_ref, k_ref, v_ref, z_ref, nw_ref, gr_ref, br_ref, o_ref, st_ref, s_ref):
        n = pl.program_id(0)

        @pl.when(n == 0)
        def _():
            s_ref[...] = jnp.zeros_like(s_ref)

        s0 = s_ref[...]
        st_ref[...] = s0
        grow_v = gr_ref[...]
        f = _dn_chunk(_heads(q_ref), _heads(k_ref), _heads(v_ref), _to_col(grow_v), grow_v, _to_col(br_ref[...]), s0)
        o = f["o"]
        r = lax.rsqrt(jnp.mean(o * o, axis=-1, keepdims=True) + NORM_EPS)
        _store_heads(o_ref, o * r * nw_ref[...][None] * _silu(_heads(z_ref)))
        s_ref[...] = f["s1"]

    return pl.pallas_call(
        body, name=name, grid=(N,),
        in_specs=[sp["q"], sp["k"], sp["v"], sp["z"], sp["nw"], sp["row"], sp["row"]],
        out_specs=[sp["out"], sp["state"]],
        out_shape=[jax.ShapeDtypeStruct((T, B_W), F32),
                   jax.ShapeDtypeStruct((B_HEADS, N, B_HEAD_DIM, B_HEAD_DIM), F32)],
        scratch_shapes=[pltpu.VMEM((B_HEADS, B_HEAD_DIM, B_HEAD_DIM), F32)],
        compiler_params=_cparams("arbitrary"),
    )(qkvc, qkvc, qkvc, proj, norm_w, grow, brow)


def _dn_bwd(qkvc, proj, norm_w, grow, brow, states, dmix, name):
    T = qkvc.shape[0]
    N = T // B_CHUNK
    sp = _dn_specs(N, True)
    C = B_CHUNK

    def body(q_ref, k_ref, v_ref, z_ref, nw_ref, gr_ref, br_ref, st_ref, dob_ref,
             dq_ref, dk_ref, dv_ref, dz_ref, dg_ref, db_ref, dnw_ref, ds_ref):
        n = pl.program_id(0)

        @pl.when(n == 0)
        def _():
            ds_ref[...] = jnp.zeros_like(ds_ref)
            dnw_ref[...] = jnp.zeros_like(dnw_ref)

        s0 = st_ref[...]
        ds1 = ds_ref[...]
        v, z, nw, bcol_v = _heads(v_ref), _heads(z_ref), nw_ref[...][None], _to_col(br_ref[...])
        grow_v = gr_ref[...]
        f = _dn_chunk(_heads(q_ref), _heads(k_ref), v, _to_col(grow_v), grow_v, bcol_v, s0)
        o, q, k, qn = f["o"], f["q"], f["k"], f["qn"]
        eg, ekt, egl = f["eg"], f["ekt"], f["egl"]
        tm, w, vn, kb, vb, kbg = f["tm"], f["w"], f["vn"], f["kb"], f["vb"], f["kbg"]
        qg, kt, qk, a = f["qg"], f["kt"], f["qk"], f["a"]
        ri, ci = f["ri"], f["ci"]

        dob_v = _heads(dob_ref)
        r = lax.rsqrt(jnp.mean(o * o, axis=-1, keepdims=True) + NORM_EPS)
        sz = _silu(z)
        on = o * r
        dnw_ref[...] += jnp.sum(dob_v * sz * on, axis=1, keepdims=True)
        _store_heads(dz_ref, dob_v * on * nw * _dsilu(z))
        d_on = dob_v * sz * nw
        do = r * (d_on - on * jnp.mean(d_on * on, axis=-1, keepdims=True))

        dvn = _dot(qk, do, BTN) + _dot(kt, ds1, BNN)
        dqk = _dot(do, vn, BNT)
        dqg = _dot(do, s0, BNT)
        ds_ref[...] = _dot(qg, do, BTN) + egl * ds1 - _dot(w, dvn, BTN)
        dgl = jnp.sum(s0 * ds1, axis=(1, 2), keepdims=True) * egl
        dkt = _dot(vn, ds1, BNT)
        dw = -_dot(dvn, s0, BNT)
        dq = dqg * eg
        dgc = jnp.sum(dqg * qg, axis=-1, keepdims=True)
        dk = dkt * ekt
        t_kt = jnp.sum(dkt * kt, axis=-1, keepdims=True)
        dgl = dgl + jnp.sum(t_kt, axis=1, keepdims=True)
        dgc = dgc - t_kt
        dqkr = dqk * f["dincl"]
        dq = dq + _dot(dqkr, k, BNN)
        dk = dk + _dot(dqkr, q, BTN)
        e_qk = dqk * qk
        dgc = dgc + jnp.sum(e_qk, axis=-1, keepdims=True)
        dgc_row = -jnp.sum(e_qk, axis=1, keepdims=True)
        dtm = _dot(dvn, vb, BNT) + _dot(dw, kbg, BNT)
        dvb = _dot(tm, dvn, BTN)
        dkbg = _dot(tm, dw, BTN)
        dkb = dkbg * eg
        dgc = dgc + jnp.sum(dkbg * kbg, axis=-1, keepdims=True)
        da = -_dotf(tm, _dotf(dtm, tm, BNT), BTN)
        dkk = da * f["dstrict"]
        e_a = da * a
        dgc = dgc + jnp.sum(e_a, axis=-1, keepdims=True)
        dgc_row = dgc_row - jnp.sum(e_a, axis=1, keepdims=True)
        dkb = dkb + _dot(dkk, k, BNN)
        dk = dk + _dot(dkk, kb, BTN)
        dk = dk + dkb * bcol_v
        db_ref[...] = _to_row(jnp.sum(dkb * k, axis=-1, keepdims=True) + jnp.sum(dvb * v, axis=-1, keepdims=True))
        _store_heads(dv_ref, dvb * bcol_v)
        dgc_row = dgc_row + jnp.sum(jnp.where(ri == ci, dgc, 0.0), axis=1, keepdims=True)
        dg_ref[...] = jnp.sum(jnp.where(ci <= ri, _to_col(dgc_row), 0.0), axis=1, keepdims=True) + dgl
        dqs = dq * DK_SCALE
        _store_heads(dq_ref, f["rq"] * (dqs - qn * jnp.sum(dqs * qn, axis=-1, keepdims=True)))
        _store_heads(dk_ref, f["rk"] * (dk - k * jnp.sum(dk * k, axis=-1, keepdims=True)))

    return pl.pallas_call(
        body, name=name, grid=(N,),
        in_specs=[sp["q"], sp["k"], sp["v"], sp["z"], sp["nw"], sp["row"], sp["row"], sp["state"], sp["dob"]],
        out_specs=[sp["out"], sp["out"], sp["out"], sp["out"], sp["row"], sp["row"],
                   pl.BlockSpec((B_HEADS, 1, LANES), lambda n: (0, 0, 0))],
        out_shape=[jax.ShapeDtypeStruct((T, B_W), F32)] * 4
        + [jax.ShapeDtypeStruct((B_HEADS, N, 1, C), F32), jax.ShapeDtypeStruct((B_HEADS, N, 1, C), F32),
           jax.ShapeDtypeStruct((B_HEADS, 1, LANES), F32)],
        scratch_shapes=[pltpu.VMEM((B_HEADS, B_HEAD_DIM, B_HEAD_DIM), F32)],
        compiler_params=_cparams("arbitrary"),
    )(qkvc, qkvc, qkvc, proj, norm_w, grow, brow, states, dmix)


def _lru_gates(xc, wa_ref, wx_ref, ba, bx, sp):
    pre_r, pre_i = [], []
    for hb in range(LRU_BLOCKS):
        xb = xc[:, hb * LRU_BLOCK_W:(hb + 1) * LRU_BLOCK_W]
        pre_r.append(_dot(xb, wa_ref[hb]))
        pre_i.append(_dot(xb, wx_ref[hb]))
    r = _sigmoid(jnp.concatenate(pre_r, axis=1) + ba)
    i = _sigmoid(jnp.concatenate(pre_i, axis=1) + bx)
    la = -LRU_C * r * sp
    a = jnp.exp(la)
    th = jnp.tanh(la)
    s = jnp.sqrt(-2.0 * th / (1.0 - th))
    return r, i, a, s


def _scan_down(a, b):
    R = a.shape[0]
    row = lax.broadcasted_iota(jnp.int32, a.shape, 0)
    d = 1
    while d < R:
        ok = row >= d
        b = a * jnp.where(ok, pltpu.roll(b, d, axis=0), 0.0) + b
        a = a * jnp.where(ok, pltpu.roll(a, d, axis=0), 1.0)
        d *= 2
    return a, b


def _scan_up(a, b):
    R = a.shape[0]
    row = lax.broadcasted_iota(jnp.int32, a.shape, 0)
    d = 1
    while d < R:
        ok = row < R - d
        b = a * jnp.where(ok, pltpu.roll(b, R - d, axis=0), 0.0) + b
        a = a * jnp.where(ok, pltpu.roll(a, R - d, axis=0), 1.0)
        d *= 2
    return b


def _rglru_fwd(proj, conv_w, conv_b, wa, wx, ba, bx, sp, name):
    T = proj.shape[0]
    R = _tile(T, 256)
    W = D_MODEL

    def body(p_ref, cw_ref, cb_ref, wa_ref, wx_ref, ba_ref, bx_ref, sp_ref, hg_ref, h_ref, prev_ref, hc_ref):
        i = pl.program_id(0)

        @pl.when(i == 0)
        def _():
            prev_ref[...] = jnp.zeros_like(prev_ref)
            hc_ref[...] = jnp.zeros_like(hc_ref)

        xr = p_ref[:, :W]
        gate = p_ref[:, W:]
        xc = _conv(xr, prev_ref[...], cw_ref[...]) + cb_ref[...]
        prev_ref[...] = xr[R - SUBLANES:]
        r, ig, a, s = _lru_gates(xc, wa_ref, wx_ref, ba_ref[...], bx_ref[...], sp_ref[...])
        pa, hb = _scan_down(a, s * ig * xc)
        h = hb + pa * hc_ref[SUBLANES - 1:SUBLANES, :]
        h_ref[...] = h
        hg_ref[...] = (h * _gelu(gate)).astype(ACT_DTYPE)
        hc_ref[...] = h[R - SUBLANES:]

    vec = pl.BlockSpec((1, W), lambda i: (0, 0))
    wsp = pl.BlockSpec((LRU_BLOCKS, LRU_BLOCK_W, LRU_BLOCK_W), lambda i: (0, 0, 0))
    row = pl.BlockSpec((R, W), lambda i: (i, 0))
    return pl.pallas_call(
        body, name=name, grid=(T // R,),
        in_specs=[pl.BlockSpec((R, 2 * W), lambda i: (i, 0)), pl.BlockSpec((CONV_K, W), lambda i: (0, 0)),
                  vec, wsp, wsp, vec, vec, vec],
        out_specs=[row, row],
        out_shape=[jax.ShapeDtypeStruct((T, W), ACT_DTYPE), jax.ShapeDtypeStruct((T, W), F32)],
        scratch_shapes=[pltpu.VMEM((SUBLANES, W), F32), pltpu.VMEM((SUBLANES, W), F32)],
        compiler_params=_cparams("arbitrary"),
    )(proj, conv_w, conv_b, wa, wx, ba, bx, sp)


def _rglru_bwd(proj, conv_w, conv_b, wa, wx, ba, bx, sp, h, dhg, name):
    T = proj.shape[0]
    R = _tile(T, 256)
    nb = T // R
    r8 = R // SUBLANES
    W = D_MODEL

    def body(p_ref, pp_ref, cw_ref, cb_ref, wa_ref, wx_ref, ba_ref, bx_ref, sp_ref, h_ref, hp_ref, dhg_ref,
             dp_ref, dcw_ref, dcb_ref, dwa_ref, dwx_ref, dba_ref, dbx_ref, dsp_ref, lam_ref, nxt_ref):
        step = pl.program_id(0)
        blk = nb - 1 - step

        @pl.when(step == 0)
        def _():
            lam_ref[...] = jnp.zeros_like(lam_ref)
            nxt_ref[...] = jnp.zeros_like(nxt_ref)

        xr = p_ref[:, :W]
        gate = p_ref[:, W:]
        first = blk > 0
        prev8 = jnp.where(first, pp_ref[:, :W], 0.0)
        hprev8 = jnp.where(first, hp_ref[...], 0.0)
        cw = cw_ref[...]
        spv = sp_ref[...]
        xc = _conv(xr, prev8, cw) + cb_ref[...]
        r, ig, a, s = _lru_gates(xc, wa_ref, wx_ref, ba_ref[...], bx_ref[...], spv)
        hv = h_ref[...]
        dhg_v = dhg_ref[...]
        dgate = dhg_v * hv * _dgelu(gate)
        dh = dhg_v * _gelu(gate)
        row = lax.broadcasted_iota(jnp.int32, (R, W), 0)
        last = row == R - 1
        a_up = jnp.where(last, 0.0, pltpu.roll(a, R - 1, axis=0))
        lam = _scan_up(a_up, dh + jnp.where(last, lam_ref[0:1, :], 0.0))
        lam_ref[...] = (a * lam)[:SUBLANES]
        h_dn = _shift_down(hv, hprev8, 1)
        da = lam * h_dn
        bx_in = ig * xc
        dsv = lam * bx_in
        dig = lam * s * xc
        dxc = lam * s * ig
        dla = da * a - dsv * (a * a) / s
        dr = dla * (-LRU_C) * spv
        dsp = jnp.sum(dla * (-LRU_C) * r, axis=0, keepdims=True)
        dpr = dr * r * (1.0 - r)
        dpi = dig * ig * (1.0 - ig)
        dxc_parts, dwa_parts, dwx_parts = [], [], []
        for hb in range(LRU_BLOCKS):
            sl = slice(hb * LRU_BLOCK_W, (hb + 1) * LRU_BLOCK_W)
            xb, gr, gi = xc[:, sl], dpr[:, sl], dpi[:, sl]
            dxc_parts.append(_dot(gr, wa_ref[hb], NT) + _dot(gi, wx_ref[hb], NT))
            dwa_parts.append(_dot(xb, gr, TN))
            dwx_parts.append(_dot(xb, gi, TN))
        dxc = dxc + jnp.concatenate(dxc_parts, axis=1)
        dxr, dcw = _conv_bwd(xr, prev8, cw, dxc, nxt_ref[...])
        nxt_ref[...] = dxc[:SUBLANES]
        dp_ref[:, :W] = dxr.astype(ACT_DTYPE)
        dp_ref[:, W:] = dgate.astype(ACT_DTYPE)
        dcb = jnp.sum(dxc, axis=0, keepdims=True)
        dba = jnp.sum(dpr, axis=0, keepdims=True)
        dbx = jnp.sum(dpi, axis=0, keepdims=True)

        @pl.when(step == 0)
        def _():
            for j in range(CONV_K):
                dcw_ref[j:j + 1, :] = dcw[j]
            dcb_ref[...] = dcb
            dba_ref[...] = dba
            dbx_ref[...] = dbx
            dsp_ref[...] = dsp
            for hb in range(LRU_BLOCKS):
                dwa_ref[hb] = dwa_parts[hb]
                dwx_ref[hb] = dwx_parts[hb]

        @pl.when(step > 0)
        def _():
            for j in range(CONV_K):
                dcw_ref[j:j + 1, :] += dcw[j]
            dcb_ref[...] += dcb
            dba_ref[...] += dba
            dbx_ref[...] += dbx
            dsp_ref[...] += dsp
            for hb in range(LRU_BLOCKS):
                dwa_ref[hb] += dwa_parts[hb]
                dwx_ref[hb] += dwx_parts[hb]

    rv = lambda i: nb - 1 - i
    before = lambda i: jnp.maximum((nb - 1 - i) * r8 - 1, 0)
    vec = pl.BlockSpec((1, W), lambda i: (0, 0))
    cws = pl.BlockSpec((CONV_K, W), lambda i: (0, 0))
    wsp = pl.BlockSpec((LRU_BLOCKS, LRU_BLOCK_W, LRU_BLOCK_W), lambda i: (0, 0, 0))
    row = pl.BlockSpec((R, W), lambda i: (rv(i), 0))
    wshape = jax.ShapeDtypeStruct((LRU_BLOCKS, LRU_BLOCK_W, LRU_BLOCK_W), F32)
    vshape = jax.ShapeDtypeStruct((1, W), F32)
    return pl.pallas_call(
        body, name=name, grid=(nb,),
        in_specs=[pl.BlockSpec((R, 2 * W), lambda i: (rv(i), 0)), pl.BlockSpec((SUBLANES, 2 * W), lambda i: (before(i), 0)),
                  cws, vec, wsp, wsp, vec, vec, vec, row, pl.BlockSpec((SUBLANES, W), lambda i: (before(i), 0)), row],
        out_specs=[pl.BlockSpec((R, 2 * W), lambda i: (rv(i), 0)), cws, vec, wsp, wsp, vec, vec, vec],
        out_shape=[jax.ShapeDtypeStruct((T, 2 * W), ACT_DTYPE), jax.ShapeDtypeStruct((CONV_K, W), F32), vshape,
                   wshape, wshape, vshape, vshape, vshape],
        scratch_shapes=[pltpu.VMEM((SUBLANES, W), F32), pltpu.VMEM((SUBLANES, W), F32)],
        compiler_params=_cparams("arbitrary"),
    )(proj, proj, conv_w, conv_b, wa, wx, ba, bx, sp, h, h, dhg)


MESH = pl.DeviceIdType.MESH
ANY = pl.BlockSpec(memory_space=pl.ANY)


def _position():
    x, y, c = lax.axis_index("x"), lax.axis_index("y"), lax.axis_index("c")
    other_chips = [(1 - x, y), (x, 1 - y), (1 - x, 1 - y)]
    return x, y, c, other_chips


def _all_gather_weights(shards, name):
    n = len(shards)

    def body(*refs):
        ins, outs = refs[:n], refs[n:2 * n]
        send_sems, recv_sems = refs[2 * n:]
        x, y, c, chips = _position()
        me = 2 * x + y
        sibling = (x, y, 1 - c)

        def rcopy(t, k, src, dst, to):
            return pltpu.make_async_remote_copy(src_ref=src, dst_ref=dst, send_sem=send_sems.at[t, k],
                                                recv_sem=recv_sems.at[t, k], device_id=to, device_id_type=MESH)

        started = []
        for t in range(n):
            for j, (cx, cy) in enumerate(chips):
                cp = rcopy(t, j, ins[t].at[c], outs[t].at[me, c], (cx, cy, c))
                cp.start()
                started.append(cp)
        for t in range(n):
            for j, (cx, cy) in enumerate(chips):
                blk = outs[t].at[2 * cx + cy, c]
                rcopy(t, j, blk, blk, (cx, cy, c)).wait_recv()
                cp = rcopy(t, 3 + j, blk, blk, sibling)
                cp.start()
                started.append(cp)
        for t in range(n):
            for j, (cx, cy) in enumerate(chips):
                blk = outs[t].at[2 * cx + cy, 1 - c]
                rcopy(t, 3 + j, blk, blk, sibling).wait_recv()
        for cp in started:
            cp.wait_send()

    return pl.pallas_call(
        body, name=name, in_specs=[ANY] * n, out_specs=[ANY] * n,
        out_shape=[jax.ShapeDtypeStruct((N_CHIPS,) + s.shape, s.dtype) for s in shards],
        scratch_shapes=[pltpu.SemaphoreType.DMA((n, 6)), pltpu.SemaphoreType.DMA((n, 6))],
    )(*shards)


def _rs_to_sibling(grads, name):
    n = len(grads)

    def body(*refs):
        ins, outs = refs[:n], refs[n:2 * n]
        send_sems, recv_sems = refs[2 * n:]
        x, y, c, _ = _position()
        cps = [pltpu.make_async_remote_copy(src_ref=ins[t].at[:, 1 - c], dst_ref=outs[t], send_sem=send_sems.at[t],
                                            recv_sem=recv_sems.at[t], device_id=(x, y, 1 - c), device_id_type=MESH)
               for t in range(n)]
        for cp in cps:
            cp.start()
        for cp in cps:
            cp.wait()

    return pl.pallas_call(
        body, name=name, in_specs=[ANY] * n, out_specs=[ANY] * n,
        out_shape=[jax.ShapeDtypeStruct((N_CHIPS,) + g.shape[2:], g.dtype) for g in grads],
        scratch_shapes=[pltpu.SemaphoreType.DMA((n,)), pltpu.SemaphoreType.DMA((n,))],
    )(*grads)


def _rs_across_chips(parts, name):
    n = len(parts)

    def body(*refs):
        ins, outs = refs[:n], refs[n:2 * n]
        send_sems, recv_sems = refs[2 * n:]
        x, y, c, chips = _position()
        me = 2 * x + y
        cps = []
        for t in range(n):
            for j, (cx, cy) in enumerate(chips):
                cps.append(pltpu.make_async_remote_copy(
                    src_ref=ins[t].at[2 * cx + cy], dst_ref=outs[t].at[me], send_sem=send_sems.at[t, j],
                    recv_sem=recv_sems.at[t, j], device_id=(cx, cy, c), device_id_type=MESH))
        for cp in cps:
            cp.start()
        for t in range(n):
            for j, (cx, cy) in enumerate(chips):
                blk = outs[t].at[2 * cx + cy]
                pltpu.make_async_remote_copy(src_ref=blk, dst_ref=blk, send_sem=send_sems.at[t, j],
                                             recv_sem=recv_sems.at[t, j], device_id=(cx, cy, c),
                                             device_id_type=MESH).wait_recv()
        for cp in cps:
            cp.wait_send()

    return pl.pallas_call(
        body, name=name, in_specs=[ANY] * n, out_specs=[ANY] * n,
        out_shape=[jax.ShapeDtypeStruct(p.shape, p.dtype) for p in parts],
        scratch_shapes=[pltpu.SemaphoreType.DMA((n, 3)), pltpu.SemaphoreType.DMA((n, 3))],
    )(*parts)


def _rs_join_halves(halves, name):
    n = len(halves)

    def body(*refs):
        ins, outs = refs[:n], refs[n:2 * n]
        send_sems, recv_sems = refs[2 * n:]
        x, y, c, _ = _position()
        cps = [pltpu.make_async_remote_copy(src_ref=ins[t].at[c], dst_ref=outs[t].at[c], send_sem=send_sems.at[t],
                                            recv_sem=recv_sems.at[t], device_id=(x, y, 1 - c), device_id_type=MESH)
               for t in range(n)]
        for cp in cps:
            cp.start()
        for t in range(n):
            blk = outs[t].at[1 - c]
            pltpu.make_async_remote_copy(src_ref=blk, dst_ref=blk, send_sem=send_sems.at[t], recv_sem=recv_sems.at[t],
                                         device_id=(x, y, 1 - c), device_id_type=MESH).wait_recv()
        for cp in cps:
            cp.wait_send()

    return pl.pallas_call(
        body, name=name, in_specs=[ANY] * n, out_specs=[ANY] * n,
        out_shape=[jax.ShapeDtypeStruct(h.shape, h.dtype) for h in halves],
        input_output_aliases={t: t for t in range(n)},
        scratch_shapes=[pltpu.SemaphoreType.DMA((n,)), pltpu.SemaphoreType.DMA((n,))],
    )(*halves)


def _all_gather_small(block, name):
    m_per, n = block.shape

    def body(x_ref, out_ref, send_sems, recv_sems, local_sem):
        x, y, c, chips = _position()
        me, sibling = (x, y, c), (x, y, 1 - c)

        def rows(px, py, pc):
            return out_ref.at[pl.ds((4 * px + 2 * py + pc) * m_per, m_per), :]

        def copy(k, blk, to, src=None):
            return pltpu.make_async_remote_copy(
                src_ref=rows(*blk) if src is None else src, dst_ref=rows(*blk), send_sem=send_sems.at[k],
                recv_sem=recv_sems.at[k], device_id=to, device_id_type=MESH)

        mine = pltpu.make_async_copy(x_ref, rows(*me), local_sem)
        mine.start()
        first = [copy(0, me, sibling, src=x_ref)]
        first += [copy(1 + j, me, (*chip, c), src=x_ref) for j, chip in enumerate(chips)]
        for cp in first:
            cp.start()
        passed = [copy(4 + j, (*chip, c), sibling) for j, chip in enumerate(chips)]
        for j, chip in enumerate(chips):
            copy(1 + j, (*chip, c), me).wait_recv()
            passed[j].start()
        copy(0, sibling, me).wait_recv()
        for j, chip in enumerate(chips):
            copy(4 + j, (*chip, 1 - c), me).wait_recv()
        for cp in first + passed:
            cp.wait_send()
        mine.wait()

    return pl.pallas_call(
        body, name=name, out_shape=jax.ShapeDtypeStruct((N_DEV * m_per, n), block.dtype),
        in_specs=[pl.BlockSpec(memory_space=pltpu.VMEM)], out_specs=pl.BlockSpec(memory_space=pltpu.VMEM),
        scratch_shapes=[pltpu.SemaphoreType.DMA((7,)), pltpu.SemaphoreType.DMA((7,)), pltpu.SemaphoreType.DMA],
    )(block)


def _row_tile(R, n):
    budget = 1 << 19
    if R * n <= budget or R % SUBLANES:
        return R
    t = R
    while t * n > budget and t % (2 * SUBLANES) == 0:
        t //= 2
    return t


def _pair_sum(g, recv, c_arr, name):
    _, _, R, n = g.shape
    tr = _row_tile(R, n)

    def body(c_ref, g_ref, r_ref, o_ref):
        o_ref[...] = (g_ref[...] + r_ref[...]).astype(ICI_DTYPE)

    grid_spec = pltpu.PrefetchScalarGridSpec(
        num_scalar_prefetch=1, grid=(N_CHIPS, R // tr),
        in_specs=[pl.BlockSpec((None, None, tr, n), lambda p, i, c: (p, c[0], i, 0)),
                  pl.BlockSpec((None, tr, n), lambda p, i, c: (p, i, 0))],
        out_specs=pl.BlockSpec((None, tr, n), lambda p, i, c: (p, i, 0)))
    return pl.pallas_call(
        body, name=name, grid_spec=grid_spec, out_shape=jax.ShapeDtypeStruct(recv.shape, ICI_DTYPE),
        compiler_params=_cparams("parallel", "parallel"),
    )(c_arr, g, recv)


def _chip_sum(recv, own, where, name):
    _, R, n = recv.shape
    tr = _row_tile(R, n)

    def body(w_ref, r0, r1, r2, r3, own_ref, o_ref):
        me = w_ref[0]
        terms = [jnp.where(me == k, own_ref[...], r[...]).astype(F32) for k, r in enumerate((r0, r1, r2, r3))]
        o_ref[...] = ((terms[0] + terms[1]) + terms[2]) + terms[3]

    def slot(k):
        return pl.BlockSpec((None, tr, n), lambda i, w: (w[1 + k], i, 0))

    grid_spec = pltpu.PrefetchScalarGridSpec(
        num_scalar_prefetch=1, grid=(R // tr,),
        in_specs=[slot(0), slot(1), slot(2), slot(3), pl.BlockSpec((None, tr, n), lambda i, w: (w[0], i, 0))],
        out_specs=pl.BlockSpec((None, tr, n), lambda i, w: (w[5], i, 0)))
    return pl.pallas_call(
        body, name=name, grid_spec=grid_spec, out_shape=jax.ShapeDtypeStruct((2, R, n), F32),
        compiler_params=_cparams("parallel"),
    )(where, recv, recv, recv, recv, own)


ADAM_C1 = 1.0 / (1.0 - ADAM_B1 ** ADAM_STEP)
ADAM_C2 = 1.0 / (1.0 - ADAM_B2 ** ADAM_STEP)


def _adamw_math(w, g, m, v):
    m = ADAM_B1 * m + (1.0 - ADAM_B1) * g
    v = ADAM_B2 * v + (1.0 - ADAM_B2) * (g * g)
    delta = -ADAM_LR * ((m * ADAM_C1) / (jnp.sqrt(v * ADAM_C2) + ADAM_EPS) + ADAM_WD * w)
    return delta, m, v


def _adamw(w, g, m, v, name):
    R, n = w.shape
    tr = _row_tile(R, n)

    def body(w_ref, g_ref, m_ref, v_ref, d_ref, nm_ref, nv_ref):
        d_ref[...], nm_ref[...], nv_ref[...] = _adamw_math(w_ref[...], g_ref[...], m_ref[...], v_ref[...])

    spec = pl.BlockSpec((tr, n), lambda i: (i, 0))
    return pl.pallas_call(
        body, name=name, grid=(R // tr,), in_specs=[spec] * 4, out_specs=[spec] * 3,
        out_shape=[jax.ShapeDtypeStruct((R, n), F32)] * 3, compiler_params=_cparams("parallel"),
    )(w, g, m, v)


def _adamw_small(w, gall, m, v, name):
    M, n = w.shape

    def body(w_ref, g_ref, m_ref, v_ref, gs_ref, d_ref, nm_ref, nv_ref):
        g = g_ref[0:M, :]
        for d in range(1, N_DEV):
            g = g + g_ref[d * M:(d + 1) * M, :]
        gs_ref[...] = g
        d_ref[...], nm_ref[...], nv_ref[...] = _adamw_math(w_ref[...], g, m_ref[...], v_ref[...])

    return pl.pallas_call(
        body, name=name, out_shape=[jax.ShapeDtypeStruct((M, n), F32)] * 4,
    )(w, gall, m, v)


SMALL_ROWS = 24
MID_ROWS = 4


def _pack_small(ln1_g, ln1_b, ln2_g, ln2_b, norm_w, sinks, a_log, dt_bias):
    mixed = jnp.concatenate([norm_w.reshape(-1), sinks.reshape(-1), a_log.reshape(-1), dt_bias.reshape(-1)])
    mixed = jnp.pad(mixed, (0, D_MODEL - mixed.shape[0]))[None]
    pad = jnp.zeros((SMALL_ROWS - 4 * DEPTH - 1, D_MODEL), F32)
    return jnp.concatenate([ln1_g, ln1_b, ln2_g, ln2_b, mixed, pad], axis=0)


def _unpack_small(p):
    mixed = p[4 * DEPTH]
    return (p[0:4], p[4:8], p[8:12], p[12:16], mixed[0:256].reshape(2, 128), mixed[256:272].reshape(2, 8),
            mixed[272:280].reshape(2, 4), mixed[280:288].reshape(2, 4))


def _pack_mid(conv_w, rconv_w, rconv_b, b_a, b_x, lam):
    flat = jnp.concatenate([conv_w.reshape(2, -1), rconv_w.reshape(2, -1), rconv_b, b_a, b_x, lam], axis=1)
    return jnp.pad(flat, ((0, 0), (0, MID_ROWS * D_MODEL - flat.shape[1]))).reshape(2, MID_ROWS, D_MODEL)


def _unpack_mid(p):
    lead = p.shape[:-2]
    f = p.reshape(lead + (MID_ROWS * D_MODEL,))
    return (f[..., 0:1536].reshape(lead + (4, 384)), f[..., 1536:2560].reshape(lead + (4, 256)),
            f[..., 2560:2816], f[..., 2816:3072], f[..., 3072:3328], f[..., 3328:3584])


def _cols_from_chips(g):
    p, L, R, n = g.shape
    return g.transpose(1, 2, 0, 3).reshape(L, R, p * n)


def _rows_from_chips(g):
    p, L, R, n = g.shape
    return g.transpose(1, 0, 2, 3).reshape(L, p * R, n)


def _cols_to_chips(g):
    L, R, n4 = g.shape
    return g.reshape(L, R, N_CHIPS, n4 // N_CHIPS).transpose(2, 0, 1, 3)


def _rows_to_chips(g):
    L, R4, n = g.shape
    return g.reshape(L, N_CHIPS, R4 // N_CHIPS, n).transpose(1, 0, 2, 3)


def _halves(a):
    return a.reshape(2, -1, a.shape[-1])


def _pad_hyb_cols(w):
    z = jnp.zeros(w.shape[:-1] + (HYB_PAD - HP_BG - 2 * B_HEADS,), w.dtype)
    return jnp.concatenate([w[..., 0:512], w[..., 768:2304], w[..., 2304:2816], w[..., 512:768], w[..., 2816:2824], z], axis=-1)


def _unpad_hyb_cols(w):
    return jnp.concatenate([w[..., 0:512], w[..., 2560:2816], w[..., 512:2048], w[..., 2048:2560], w[..., 2816:2824]], axis=-1)


def _hybrid_fwd(x, W, j, tables, sfx):
    cos, sin_s = tables
    T = x.shape[0]
    N = T // B_CHUNK
    proj = _matmul(x, W["hyb_w_in"][j], "nn", "hyb_in" + sfx)
    sinks_b = jnp.broadcast_to(W["hyb_sinks"][j][:, None], (A_Q_HEADS, LANES))
    o_a, lse = _attn_fwd(proj, cos, sin_s, sinks_b, "attn_fwd" + sfx)
    qkvc = _dnconv_fwd(proj, W["hyb_conv_w"][j], "dnconv_fwd" + sfx)
    bg = proj[:, HP_BG:HP_BG + 2 * B_HEADS]
    beta = jax.nn.sigmoid(bg[:, :B_HEADS])
    pre = bg[:, B_HEADS:] + W["hyb_dt_bias"][j][None]
    g = -jnp.exp(W["hyb_a_log"][j])[None] * jax.nn.softplus(pre)
    grow = g.T.reshape(B_HEADS, N, 1, B_CHUNK)
    brow = beta.T.reshape(B_HEADS, N, 1, B_CHUNK)
    nw = W["hyb_norm_w"][j][None]
    o_b, states = _dn_fwd(qkvc, proj, nw, grow, brow, "dn_fwd" + sfx)
    mix = jnp.concatenate([o_a, o_b], axis=1).astype(ACT_DTYPE)
    y = _matmul(mix, W["hyb_w_out"][j], "nn", "hyb_out" + sfx)
    res = dict(proj=proj, o_a=o_a, lse=lse, qkvc=qkvc, beta=beta, pre=pre, g=g, grow=grow, brow=brow,
               states=states, mix=mix, sinks_b=sinks_b, nw=nw)
    return y, res


def _hybrid_bwd(x, du, dub, W, j, res, tables, sfx):
    cos, sin_s = tables
    T = x.shape[0]
    proj = res["proj"]
    d_wout = _matmul(res["mix"], dub, "tn", "hyb_out_dw" + sfx)
    dmix = _matmul(dub, W["hyb_w_out"][j], "nt", "hyb_out_dx" + sfx)
    dq, dkc, dkp, dvc, dvp, dsink = _attn_bwd(proj, cos, sin_s, res["sinks_b"], res["o_a"], res["lse"], dmix,
                                               "attn_bwd" + sfx)
    zpad = jnp.zeros((WINDOW, LANES), F32)
    dk = dkc + jnp.concatenate([dkp[WINDOW:], zpad], axis=0)
    dv = dvc + jnp.concatenate([dvp[WINDOW:], zpad], axis=0)
    dqc, dkcv, dvcv, dz, dg4, dbeta4, dnw = _dn_bwd(res["qkvc"], proj, res["nw"], res["grow"], res["brow"],
                                                    res["states"], dmix, "dn_bwd" + sfx)
    dqkvc = jnp.concatenate([dqc, dkcv, dvcv], axis=1)
    dqkvb, dconv = _dnconv_bwd(proj, W["hyb_conv_w"][j], dqkvc, "dnconv_bwd" + sfx)
    dg = dg4.reshape(B_HEADS, T).T
    dbeta = dbeta4.reshape(B_HEADS, T).T
    beta = res["beta"]
    dbeta_logit = dbeta * beta * (1.0 - beta)
    da_logit = dg * (-jnp.exp(W["hyb_a_log"][j]))[None] * jax.nn.sigmoid(res["pre"])
    d_dt_bias = jnp.sum(da_logit, axis=0)
    d_a_log = jnp.sum(dg * res["g"], axis=0)
    zcols = jnp.zeros((T, HYB_PAD - HP_BG - 2 * B_HEADS), F32)
    dproj = jnp.concatenate([dq, dqkvb, dz, dk, dv, dbeta_logit, da_logit, zcols], axis=1).astype(ACT_DTYPE)
    d_win = _matmul(x, dproj, "tn", "hyb_in_dw" + sfx)
    dx = _matmul(dproj, W["hyb_w_in"][j], "nt", "hyb_in_dx" + sfx, epi=_epi_add_residual, extra=du)
    grads = dict(hyb_w_in=d_win, hyb_w_out=d_wout, hyb_sinks=dsink[0], hyb_conv_w=dconv, hyb_a_log=d_a_log,
                 hyb_dt_bias=d_dt_bias, hyb_norm_w=jnp.sum(dnw[:, 0, :], axis=0))
    return dx, grads


def _rec_fwd(x, W, j, sfx):
    proj = _matmul(x, W["rec_w_in"][j], "nn", "rec_in" + sfx)
    sp = jax.nn.softplus(-W["rec_lambda"][j])[None]
    hg, h = _rglru_fwd(proj, W["rec_conv_w"][j], W["rec_conv_b"][j][None], W["rec_w_a"][j], W["rec_w_x"][j],
                       W["rec_b_a"][j][None], W["rec_b_x"][j][None], sp, "rglru_fwd" + sfx)
    y = _matmul(hg, W["rec_w_out"][j], "nn", "rec_out" + sfx)
    return y, dict(proj=proj, hg=hg, h=h, sp=sp)


def _rec_bwd(x, du, dub, W, j, res, sfx):
    d_wout = _matmul(res["hg"], dub, "tn", "rec_out_dw" + sfx)
    dhg = _matmul(dub, W["rec_w_out"][j], "nt", "rec_out_dx" + sfx)
    dproj, dcw, dcb, dwa, dwx, dba, dbx, dsp = _rglru_bwd(
        res["proj"], W["rec_conv_w"][j], W["rec_conv_b"][j][None], W["rec_w_a"][j], W["rec_w_x"][j],
        W["rec_b_a"][j][None], W["rec_b_x"][j][None], res["sp"], res["h"], dhg, "rglru_bwd" + sfx)
    d_lam = dsp[0] * (-jax.nn.sigmoid(-W["rec_lambda"][j]))
    d_win = _matmul(x, dproj, "tn", "rec_in_dw" + sfx)
    dx = _matmul(dproj, W["rec_w_in"][j], "nt", "rec_in_dx" + sfx, epi=_epi_add_residual, extra=du)
    grads = dict(rec_w_in=d_win, rec_w_out=d_wout, rec_conv_w=dcw, rec_conv_b=dcb[0], rec_w_a=dwa, rec_w_x=dwx,
                 rec_b_a=dba[0], rec_b_x=dbx[0], rec_lambda=d_lam)
    return dx, grads


def _local_step(x, tgt, W):
    T = x.shape[0]
    tables = _rope_tables(T)
    acts = []
    xb = x.astype(ACT_DTYPE)
    for layer in range(DEPTH):
        j, sfx = layer // 2, ""
        if layer % 2 == 0:
            y, res = _hybrid_fwd(xb, W, j, tables, sfx)
        else:
            y, res = _rec_fwd(xb, W, j, sfx)
        x1, x1b = _ln_fwd(x, y, W["ln1_g"][layer][None], W["ln1_b"][layer][None], "ln_fwd")
        h1 = _matmul(x1b, W["mlp_w1"], "nn", "mlp_up", out_dtype=ACT_DTYPE, b_chips=("j", layer))
        y2 = _matmul(h1, W["mlp_w2"], "nn", "mlp_down", a_fn=_relu2, b_chips=("k", layer))
        x2, x2b = _ln_fwd(x1, y2, W["ln2_g"][layer][None], W["ln2_b"][layer][None], "ln_fwd")
        acts.append(dict(x=x, xb=xb, y=y, res=res, x1=x1, x1b=x1b, h1=h1, y2=y2))
        x, xb = x2, x2b
    dx, loss = _loss_head(x, tgt, "loss_head")
    per_layer = [None] * DEPTH
    d_w1 = lax.empty(W["mlp_w1"].shape, F32)
    d_w2 = lax.empty(W["mlp_w2"].shape, F32)
    for layer in reversed(range(DEPTH)):
        j, a = layer // 2, acts[layer]
        du2, du2b, dg2, db2 = _ln_bwd(a["x1"], a["y2"], W["ln2_g"][layer][None], dx, "ln_bwd")
        d_w2 = _matmul(a["h1"], du2b, "tn", "mlp_down_dw", a_fn=_relu2, out_chips=("i", layer, d_w2))
        dh1 = _matmul(du2b, W["mlp_w2"], "nt", "mlp_down_dx", epi=_epi_drelu2, extra=a["h1"], out_dtype=ACT_DTYPE,
                      b_chips=("j", layer))
        d_w1 = _matmul(a["x1b"], dh1, "tn", "mlp_up_dw", out_chips=("j", layer, d_w1))
        dx1 = _matmul(dh1, W["mlp_w1"], "nt", "mlp_up_dx", epi=_epi_add_residual, extra=du2, b_chips=("k", layer))
        du1, du1b, dg1, db1 = _ln_bwd(a["x"], a["y"], W["ln1_g"][layer][None], dx1, "ln_bwd")
        if layer % 2 == 0:
            dx, g = _hybrid_bwd(a["xb"], du1, du1b, W, j, a["res"], tables, "")
        else:
            dx, g = _rec_bwd(a["xb"], du1, du1b, W, j, a["res"], "")
        g.update(ln1_g=dg1[0], ln1_b=db1[0], ln2_g=dg2[0], ln2_b=db2[0])
        per_layer[layer] = g
    grads = dict(mlp_w1=d_w1, mlp_w2=d_w2)
    for name in ("ln1_g", "ln1_b", "ln2_g", "ln2_b"):
        grads[name] = jnp.stack([per_layer[l][name] for l in range(DEPTH)])
    for name in per_layer[0]:
        if name not in grads:
            grads[name] = jnp.stack([per_layer[l][name] for l in (0, 2)])
    for name in per_layer[1]:
        if name not in grads:
            grads[name] = jnp.stack([per_layer[l][name] for l in (1, 3)])
    return loss, dx, grads


BIG = ("hyb_w_in", "hyb_w_out", "rec_w_in", "rec_w_out", "mlp_w1", "mlp_w2", "rec_w_a", "rec_w_x")
COL_SHARDED = ("hyb_w_in", "rec_w_in", "mlp_w1")
CHIP_MAJOR = ("mlp_w1", "mlp_w2")
MID = ("hyb_conv_w", "rec_conv_w", "rec_conv_b", "rec_b_a", "rec_b_x", "rec_lambda")
SMALL = ("ln1_g", "ln1_b", "ln2_g", "ln2_b", "hyb_norm_w", "hyb_sinks", "hyb_a_log", "hyb_dt_bias")
WEIGHTS = ("hyb_w_in", "hyb_sinks", "hyb_conv_w", "hyb_a_log", "hyb_dt_bias", "hyb_norm_w", "hyb_w_out", "rec_w_in",
           "rec_conv_w", "rec_conv_b", "rec_w_a", "rec_b_a", "rec_w_x", "rec_b_x", "rec_lambda", "rec_w_out", "ln1_g",
           "ln1_b", "mlp_w1", "mlp_w2", "ln2_g", "ln2_b")


def _gather_full_weights(w):
    shards = [_halves(w[k].astype(MXU_DTYPE)) for k in BIG]
    shards.append(_pack_mid(*[w[k] for k in MID]))
    got = _all_gather_weights(shards, "all_gather_weights")
    me = 2 * lax.axis_index("x") + lax.axis_index("y")
    got = [lax.dynamic_update_slice(g, s[None], (me, 0, 0, 0)) for s, g in zip(shards, got)]
    W = {}
    for k, g in zip(BIG, got[:-1]):
        L = w[k].shape[0]
        if k in CHIP_MAJOR:
            W[k] = g.reshape((N_CHIPS,) + w[k].shape)
        elif k in ("rec_w_a", "rec_w_x"):
            g5 = g.reshape((N_CHIPS,) + w[k].shape)
            W[k] = g5.transpose(1, 2, 0, 3, 4).reshape(L, LRU_BLOCKS, LRU_BLOCK_W, LRU_BLOCK_W)
        else:
            g4 = g.reshape((N_CHIPS,) + w[k].shape)
            W[k] = _cols_from_chips(g4) if k in COL_SHARDED else _rows_from_chips(g4)
    W["hyb_w_in"] = _pad_hyb_cols(W["hyb_w_in"])
    conv_w, rconv_w, rconv_b, b_a, b_x, lam = _unpack_mid(got[-1])
    W["hyb_conv_w"] = conv_w.transpose(1, 2, 0, 3).reshape(2, CONV_K, 3 * B_W)
    W["rec_conv_w"] = rconv_w.transpose(1, 2, 0, 3).reshape(2, CONV_K, D_MODEL)
    for k, v in (("rec_conv_b", rconv_b), ("rec_b_a", b_a), ("rec_b_x", b_x), ("rec_lambda", lam)):
        W[k] = v.transpose(1, 0, 2).reshape(2, D_MODEL)
    for k in SMALL:
        W[k] = w[k]
    return W


def _grads_by_chip(grads):
    out = []
    g = dict(grads)
    g["hyb_w_in"] = _unpad_hyb_cols(g["hyb_w_in"])
    for k in BIG:
        v = g[k]
        if k in CHIP_MAJOR:
            pass
        elif k in ("rec_w_a", "rec_w_x"):
            L = v.shape[0]
            v = v.reshape(L, LRU_BLOCKS, N_CHIPS, LRU_BLOCK_W // N_CHIPS, LRU_BLOCK_W).transpose(2, 0, 1, 3, 4)
        else:
            v = _cols_to_chips(v) if k in COL_SHARDED else _rows_to_chips(v)
        out.append(v.reshape(N_CHIPS, 2, -1, v.shape[-1]))
    conv_w = g["hyb_conv_w"].reshape(2, CONV_K, N_CHIPS, -1).transpose(2, 0, 1, 3)
    rconv_w = g["rec_conv_w"].reshape(2, CONV_K, N_CHIPS, -1).transpose(2, 0, 1, 3)
    vecs = [g[k].reshape(2, N_CHIPS, -1).transpose(1, 0, 2) for k in ("rec_conv_b", "rec_b_a", "rec_b_x", "rec_lambda")]
    mid = jnp.stack([_pack_mid(conv_w[p], rconv_w[p], *[v[p] for v in vecs]) for p in range(N_CHIPS)])
    out.append(mid)
    return out


def kernel(x, hyb_w_in, hyb_sinks, hyb_conv_w, hyb_a_log, hyb_dt_bias, hyb_norm_w, hyb_w_out, rec_w_in, rec_conv_w, rec_conv_b, rec_w_a, rec_b_a, rec_w_x, rec_b_x, rec_lambda, rec_w_out, ln1_g, ln1_b, mlp_w1, mlp_w2, ln2_g, ln2_b, loss_target, m_hyb_w_in, m_hyb_sinks, m_hyb_conv_w, m_hyb_a_log, m_hyb_dt_bias, m_hyb_norm_w, m_hyb_w_out, m_rec_w_in, m_rec_conv_w, m_rec_conv_b, m_rec_w_a, m_rec_b_a, m_rec_w_x, m_rec_b_x, m_rec_lambda, m_rec_w_out, m_ln1_g, m_ln1_b, m_mlp_w1, m_mlp_w2, m_ln2_g, m_ln2_b, v_hyb_w_in, v_hyb_sinks, v_hyb_conv_w, v_hyb_a_log, v_hyb_dt_bias, v_hyb_norm_w, v_hyb_w_out, v_rec_w_in, v_rec_conv_w, v_rec_conv_b, v_rec_w_a, v_rec_b_a, v_rec_w_x, v_rec_b_x, v_rec_lambda, v_rec_w_out, v_ln1_g, v_ln1_b, v_mlp_w1, v_mlp_w2, v_ln2_g, v_ln2_b):
    args = locals()
    w = {k: args[k] for k in WEIGHTS}
    m = {k: args["m_" + k] for k in WEIGHTS}
    v = {k: args["v_" + k] for k in WEIGHTS}

    W = _gather_full_weights(w)
    loss, dx, grads = _local_step(x[0], loss_target[0], W)
    loss = lax.psum(loss[0, 0], ("x", "y", "c"))

    core = lax.axis_index("c").astype(jnp.int32)
    me = (2 * lax.axis_index("x") + lax.axis_index("y")).astype(jnp.int32)
    slots = jnp.arange(N_CHIPS, dtype=jnp.int32)
    where = jnp.concatenate([me[None], jnp.where(slots == me, (slots + 1) % N_CHIPS, slots), core[None]])
    by_chip = _grads_by_chip(grads)
    from_sibling = _rs_to_sibling(by_chip, "rs_to_sibling")
    pair = [_pair_sum(g, r, core[None], "pair_sum") for g, r in zip(by_chip, from_sibling)]
    from_chips = _rs_across_chips(pair, "rs_across_chips")
    half = [_chip_sum(r, p, where, "chip_sum") for r, p in zip(from_chips, pair)]
    joined = _rs_join_halves(half, "rs_join_halves")

    g_out, d_out, m_out, v_out = {}, {}, {}, {}

    def update(name, g2d):
        shape = w[name].shape
        n = g2d.shape[-1]
        d, nm, nv = _adamw(w[name].reshape(-1, n), g2d, m[name].reshape(-1, n), v[name].reshape(-1, n), "adamw")
        g_out[name], d_out[name] = g2d.reshape(shape), d.reshape(shape)
        m_out[name], v_out[name] = nm.reshape(shape), nv.reshape(shape)

    for k, g in zip(BIG, joined[:-1]):
        update(k, g.reshape(-1, g.shape[-1]))
    mid_w, mid_m, mid_v = (_pack_mid(*[t[k] for k in MID]).reshape(-1, D_MODEL) for t in (w, m, v))
    mid_g = joined[-1].reshape(-1, D_MODEL)
    mid_d, mid_nm, mid_nv = _adamw(mid_w, mid_g, mid_m, mid_v, "adamw_mid")
    for dst, packed in ((g_out, mid_g), (d_out, mid_d), (m_out, mid_nm), (v_out, mid_nv)):
        for k, val in zip(MID, _unpack_mid(packed.reshape(2, MID_ROWS, D_MODEL))):
            dst[k] = val.reshape(w[k].shape)

    small_g = _pack_small(*[grads[k] for k in SMALL])
    small_all = _all_gather_small(small_g, "all_gather_small")
    sw, sm, sv = (_pack_small(*[t[k] for k in SMALL]) for t in (w, m, v))
    sg, sd, snm, snv = _adamw_small(sw, small_all, sm, sv, "adamw_small")
    for dst, packed in ((g_out, sg), (d_out, sd), (m_out, snm), (v_out, snv)):
        for k, val in zip(SMALL, _unpack_small(packed)):
            dst[k] = val

    return (loss, dx[None], *[g_out[k] for k in WEIGHTS], *[d_out[k] for k in WEIGHTS],
            *[m_out[k] for k in WEIGHTS], *[v_out[k] for k in WEIGHTS])
```

```python
import functools

import jax
import jax.numpy as jnp
import numpy as np
from jax import lax
from jax.experimental import pallas as pl
from jax.experimental.pallas import tpu as pltpu

F32 = jnp.float32
MXU_DTYPE = jnp.bfloat16
ACT_DTYPE = jnp.bfloat16
ICI_DTYPE = jnp.bfloat16

D_MODEL = 1024
DEPTH = 4
A_HEAD_DIM = 64
A_Q_HEADS = 8
A_KV_HEADS = 2
WINDOW = 128
ROPE_THETA = 10000.0
B_HEADS = 4
B_HEAD_DIM = 128
B_CHUNK = 64
CONV_K = 4
LRU_BLOCKS = 4
LRU_BLOCK_W = D_MODEL // LRU_BLOCKS
LRU_C = 8.0
D_FF = 4 * D_MODEL
A_Q_W = A_Q_HEADS * A_HEAD_DIM
A_KV_W = A_KV_HEADS * A_HEAD_DIM
B_W = B_HEADS * B_HEAD_DIM
HYB_PROJ = A_Q_W + 2 * A_KV_W + 4 * B_W + 2 * B_HEADS
DN_ALPHA = (2 * DEPTH) ** 0.25
LN_EPS = 1e-5
NORM_EPS = 1e-6
ADAM_LR = 0.001
ADAM_B1 = 0.9
ADAM_B2 = 0.999
ADAM_EPS = 1e-08
ADAM_WD = 0.01
ADAM_STEP = 10

HP_Q = 0
HP_QKVB = 512
HP_Z = 2048
HP_K = 2560
HP_V = 2688
HP_BG = 2816
HYB_PAD = 3072

N_CHIPS = 4
N_DEV = 8
V7X_VMEM_LIMIT = 48 * 1024 * 1024
LANES = 128
SUBLANES = 8
NEG_BIG = -1e30

NN = (((1,), (0,)), ((), ()))
NT = (((1,), (1,)), ((), ()))
TN = (((0,), (0,)), ((), ()))


def _cparams(*sem):
    return pltpu.CompilerParams(dimension_semantics=sem, vmem_limit_bytes=V7X_VMEM_LIMIT)


def _dot(a, b, dims=NN):
    return lax.dot_general(a.astype(MXU_DTYPE), b.astype(MXU_DTYPE), dims, preferred_element_type=F32)


def _split_bf16(a):
    hi = a.astype(jnp.bfloat16)
    return hi, (a - hi.astype(F32)).astype(jnp.bfloat16)


def _dotf(a, b, dims=NN):
    ah, al = _split_bf16(a)
    bh, bl = _split_bf16(b)
    dg = functools.partial(lax.dot_general, dimension_numbers=dims, preferred_element_type=F32)
    return dg(ah, bh) + (dg(ah, bl) + dg(al, bh))


def _tile(dim, pref):
    t = min(dim, pref)
    while dim % t:
        t //= 2
    return t


def _sigmoid(x):
    return 1.0 / (1.0 + jnp.exp(-x))


def _silu(x):
    return x * _sigmoid(x)


def _dsilu(x):
    s = _sigmoid(x)
    return s * (1.0 + x * (1.0 - s))


GELU_C = 0.7978845608028654
GELU_A = 0.044715


def _gelu(x):
    return 0.5 * x * (1.0 + jnp.tanh(GELU_C * (x + GELU_A * x * x * x)))


def _dgelu(x):
    t = jnp.tanh(GELU_C * (x + GELU_A * x * x * x))
    return 0.5 * (1.0 + t) + 0.5 * x * (1.0 - t * t) * GELU_C * (1.0 + 3.0 * GELU_A * x * x)


def _matmul(a, b, mode, name, *, tm=1024, tn=1024, tk=1024, a_fn=None, epi=None, extra=None, out_dtype=F32,
            b_chips=None, out_chips=None):
    if mode == "tn":
        K, M = a.shape
    else:
        M, K = a.shape
    if b_chips is not None:
        g, b_layer = b_chips
        r, n = b.shape[2:]
        n_dim, k_dim = (r, n) if mode == "nt" else (n, r)
        N = N_CHIPS * n_dim if g == "j" else n_dim
        assert K == (N_CHIPS * k_dim if g == "k" else k_dim)
        if g == "j":
            tn = n_dim
        else:
            tk = k_dim
    elif mode == "nt":
        N = b.shape[0]
    else:
        N = b.shape[1]
    if out_chips is not None:
        og, o_layer, o_buf = out_chips
        if og == "j":
            tn = o_buf.shape[3]
        else:
            tm = o_buf.shape[2]
    tm, tn, tk = _tile(M, tm), _tile(N, tn), _tile(K, tk)
    nk = K // tk
    if mode == "tn":
        a_spec = pl.BlockSpec((tk, tm), lambda i, j, k: (k, i))
    else:
        a_spec = pl.BlockSpec((tm, tk), lambda i, j, k: (i, k))
    b_block = (tn, tk) if mode == "nt" else (tk, tn)
    if b_chips is None:
        b_spec = pl.BlockSpec(b_block, (lambda i, j, k: (j, k)) if mode == "nt" else (lambda i, j, k: (k, j)))
    elif mode == "nt":
        b_spec = pl.BlockSpec((None, None) + b_block, (lambda i, j, k: (j, b_layer, 0, k)) if g == "j"
                              else (lambda i, j, k: (k, b_layer, j, 0)))
    else:
        b_spec = pl.BlockSpec((None, None) + b_block, (lambda i, j, k: (j, b_layer, k, 0)) if g == "j"
                              else (lambda i, j, k: (k, b_layer, 0, j)))
    o_spec = pl.BlockSpec((tm, tn), lambda i, j, k: (i, j))
    e_spec = o_spec
    if out_chips is not None:
        o_spec = pl.BlockSpec((None, None, tm, tn), (lambda i, j, k: (j, o_layer, i, 0)) if og == "j"
                              else (lambda i, j, k: (i, o_layer, 0, j)))
    dims = {"nn": NN, "nt": NT, "tn": TN}[mode]
    has_extra = extra is not None
    n_in = 2 + has_extra + (out_chips is not None)

    def body(*refs):
        a_ref, b_ref = refs[0], refs[1]
        e_ref = refs[2] if has_extra else None
        o_ref = refs[n_in]
        av = a_ref[...]
        if a_fn is not None:
            av = a_fn(av)
        part = _dot(av, b_ref[...], dims)

        def finish(acc):
            if epi is not None:
                acc = epi(acc, e_ref[...])
            o_ref[...] = acc.astype(out_dtype)

        if nk == 1:
            finish(part)
        else:
            acc_ref = refs[-1]
            k = pl.program_id(2)

            @pl.when(k == 0)
            def _():
                acc_ref[...] = part

            @pl.when(k > 0)
            def _():
                acc_ref[...] += part

            @pl.when(k == nk - 1)
            def _():
                finish(acc_ref[...])

    in_specs = [a_spec, b_spec] + ([e_spec] if has_extra else [])
    args = (a, b) + ((extra,) if has_extra else ())
    out_shape = jax.ShapeDtypeStruct((M, N), out_dtype)
    aliases = {}
    if out_chips is not None:
        in_specs.append(pl.BlockSpec(memory_space=pl.ANY))
        args += (o_buf,)
        out_shape = jax.ShapeDtypeStruct(o_buf.shape, o_buf.dtype)
        aliases = {n_in - 1: 0}
    return pl.pallas_call(
        body, name=name, grid=(M // tm, N // tn, nk), in_specs=in_specs, out_specs=o_spec, out_shape=out_shape,
        input_output_aliases=aliases,
        scratch_shapes=[pltpu.VMEM((tm, tn), F32)] if nk > 1 else [],
        compiler_params=_cparams("parallel", "parallel", "arbitrary"),
    )(*args)


def _relu2(v):
    r = jnp.maximum(v, 0.0)
    return r * r


def _epi_drelu2(acc, h):
    return acc * (2.0 * jnp.maximum(h, 0.0))


def _epi_add_residual(acc, du):
    return acc + DN_ALPHA * du


def _ln_fwd(x, y, g, b, name):
    T, D = x.shape
    tr = _tile(T, 512)

    def body(x_ref, y_ref, g_ref, b_ref, o_ref, ob_ref):
        u = DN_ALPHA * x_ref[...] + y_ref[...]
        mu = jnp.mean(u, axis=-1, keepdims=True)
        d = u - mu
        var = jnp.mean(d * d, axis=-1, keepdims=True)
        o = d * lax.rsqrt(var + LN_EPS) * g_ref[...] + b_ref[...]
        o_ref[...] = o
        ob_ref[...] = o.astype(ACT_DTYPE)

    row = pl.BlockSpec((tr, D), lambda i: (i, 0))
    vec = pl.BlockSpec((1, D), lambda i: (0, 0))
    return pl.pallas_call(
        body, name=name, grid=(T // tr,), in_specs=[row, row, vec, vec], out_specs=[row, row],
        out_shape=[jax.ShapeDtypeStruct((T, D), F32), jax.ShapeDtypeStruct((T, D), ACT_DTYPE)],
        compiler_params=_cparams("parallel"),
    )(x, y, g, b)


def _ln_bwd(x, y, g, dout, name):
    T, D = x.shape
    tr = _tile(T, 512)

    def body(x_ref, y_ref, g_ref, d_ref, du_ref, dub_ref, dg_ref, db_ref):
        i = pl.program_id(0)
        u = DN_ALPHA * x_ref[...] + y_ref[...]
        mu = jnp.mean(u, axis=-1, keepdims=True)
        d = u - mu
        rstd = lax.rsqrt(jnp.mean(d * d, axis=-1, keepdims=True) + LN_EPS)
        xhat = d * rstd
        dout_v = d_ref[...]
        dxh = dout_v * g_ref[...]
        m1 = jnp.mean(dxh, axis=-1, keepdims=True)
        m2 = jnp.mean(dxh * xhat, axis=-1, keepdims=True)
        du = rstd * (dxh - m1 - xhat * m2)
        du_ref[...] = du
        dub_ref[...] = du.astype(ACT_DTYPE)
        pg = jnp.sum(dout_v * xhat, axis=0, keepdims=True)
        pb = jnp.sum(dout_v, axis=0, keepdims=True)

        @pl.when(i == 0)
        def _():
            dg_ref[...] = pg
            db_ref[...] = pb

        @pl.when(i > 0)
        def _():
            dg_ref[...] += pg
            db_ref[...] += pb

    row = pl.BlockSpec((tr, D), lambda i: (i, 0))
    vec = pl.BlockSpec((1, D), lambda i: (0, 0))
    return pl.pallas_call(
        body, name=name, grid=(T // tr,), in_specs=[row, row, vec, row], out_specs=[row, row, vec, vec],
        out_shape=[jax.ShapeDtypeStruct((T, D), F32), jax.ShapeDtypeStruct((T, D), ACT_DTYPE),
                   jax.ShapeDtypeStruct((1, D), F32), jax.ShapeDtypeStruct((1, D), F32)],
        compiler_params=_cparams("arbitrary"),
    )(x, y, g, dout)


def _loss_head(y, tgt, name):
    T, D = y.shape
    tr = _tile(T, 512)

    def body(y_ref, t_ref, dy_ref, l_ref):
        i = pl.program_id(0)
        e = y_ref[...] - t_ref[...]
        dy_ref[...] = e * (1.0 / D)
        part = jnp.sum(e * e, axis=(0, 1), keepdims=True) * (0.5 / D)

        @pl.when(i == 0)
        def _():
            l_ref[...] = part

        @pl.when(i > 0)
        def _():
            l_ref[...] += part

    row = pl.BlockSpec((tr, D), lambda i: (i, 0))
    one = pl.BlockSpec((1, 1), lambda i: (0, 0))
    return pl.pallas_call(
        body, name=name, grid=(T // tr,), in_specs=[row, row], out_specs=[row, one],
        out_shape=[jax.ShapeDtypeStruct((T, D), F32), jax.ShapeDtypeStruct((1, 1), F32)],
        compiler_params=_cparams("arbitrary"),
    )(y, tgt)


def _swap_half(x):
    n = x.shape[-1]
    lane = lax.broadcasted_iota(jnp.int32, x.shape, 1)
    first = (lane % A_HEAD_DIM) < (A_HEAD_DIM // 2)
    return jnp.where(first, pltpu.roll(x, n - A_HEAD_DIM // 2, axis=1), pltpu.roll(x, A_HEAD_DIM // 2, axis=1))


def _rope(x, cos, sin_signed):
    return x * cos + _swap_half(x) * sin_signed


def _rope_t(dy, cos, sin_signed):
    return dy * cos + _swap_half(dy * sin_signed)


def _rope_tables(T):
    half = A_HEAD_DIM // 2
    inv_freq = np.float32(ROPE_THETA) ** (-np.arange(half, dtype=np.float32) / np.float32(half))
    ang = np.arange(T, dtype=np.float32)[:, None] * inv_freq[None, :]
    cos = np.tile(np.cos(ang), (1, 4))
    sin = np.sin(ang)
    sin_signed = np.tile(np.concatenate([-sin, sin], axis=1), (1, 2))
    return jnp.asarray(cos, F32), jnp.asarray(sin_signed, F32)


def _band_mask(n):
    qi = lax.broadcasted_iota(jnp.int32, (WINDOW, 2 * WINDOW), 0)
    kj = lax.broadcasted_iota(jnp.int32, (WINDOW, 2 * WINDOW), 1)
    return (kj > qi) & (kj <= qi + WINDOW) & ((n > 0) | (kj >= WINDOW))


def _place(v, src_half, dst_half):
    lane = lax.broadcasted_iota(jnp.int32, v.shape, 1)
    if src_half != dst_half:
        v = pltpu.roll(v, A_HEAD_DIM, axis=1)
    keep = (lane >= A_HEAD_DIM) if dst_half else (lane < A_HEAD_DIM)
    return jnp.where(keep, v, 0.0)


def _attn_specs():
    kb, vb = HP_K // LANES, HP_V // LANES
    prev = lambda n: jnp.maximum(n - 1, 0)
    return dict(
        q=pl.BlockSpec((WINDOW, A_Q_W), lambda n: (n, 0)),
        kc=pl.BlockSpec((WINDOW, LANES), lambda n: (n, kb)),
        kp=pl.BlockSpec((WINDOW, LANES), lambda n: (prev(n), kb)),
        vc=pl.BlockSpec((WINDOW, LANES), lambda n: (n, vb)),
        vp=pl.BlockSpec((WINDOW, LANES), lambda n: (prev(n), vb)),
        tq=pl.BlockSpec((WINDOW, LANES), lambda n: (n, 0)),
        tp=pl.BlockSpec((WINDOW, LANES), lambda n: (prev(n), 0)),
        sink=pl.BlockSpec((A_Q_HEADS, LANES), lambda n: (0, 0)),
        row512=pl.BlockSpec((WINDOW, A_Q_W), lambda n: (n, 0)),
        row128=pl.BlockSpec((WINDOW, LANES), lambda n: (n, 0)),
        lse=pl.BlockSpec((WINDOW, A_Q_HEADS), lambda n: (n, 0)),
    )


def _attn_fwd(proj, cos, sin_s, sinks_b, name):
    T = proj.shape[0]
    sp = _attn_specs()

    def body(q_ref, kc_ref, kp_ref, vc_ref, vp_ref, cq_ref, sq_ref, cp_ref, sp_ref, sink_ref, o_ref, l_ref):
        n = pl.program_id(0)
        cq, sq = cq_ref[...], sq_ref[...]
        cq4, sq4 = jnp.tile(cq, (1, A_Q_W // LANES)), jnp.tile(sq, (1, A_Q_W // LANES))
        q = _rope(q_ref[...], cq4, sq4) * (A_HEAD_DIM ** -0.5)
        kc = _rope(kc_ref[...], cq, sq)
        kp = _rope(kp_ref[...], cp_ref[...], sp_ref[...])
        kk = jnp.concatenate([kp, kc], axis=0)
        vv = jnp.concatenate([vp_ref[...], vc_ref[...]], axis=0)
        mask = _band_mask(n)
        lane = lax.broadcasted_iota(jnp.int32, (WINDOW, LANES), 1)
        outs = []
        lane8 = lax.broadcasted_iota(jnp.int32, (WINDOW, A_Q_HEADS), 1)
        lse = jnp.zeros((WINDOW, A_Q_HEADS), F32)
        for pb in range(A_Q_HEADS // 2):
            kvh = pb // 2
            q2 = q[:, pb * LANES:(pb + 1) * LANES]
            halves = []
            for e in range(2):
                hq = 2 * pb + e
                s = _dot(_place(q2, e, kvh), kk, NT)
                s = jnp.where(mask, s, NEG_BIG)
                sk = sink_ref[hq:hq + 1, 0:1]
                m = jnp.maximum(jnp.max(s, axis=-1, keepdims=True), sk)
                p = jnp.exp(s - m)
                den = jnp.sum(p, axis=-1, keepdims=True) + jnp.exp(sk - m)
                o = _dot(p * (1.0 / den), vv, NN)
                halves.append(_place(o, kvh, e))
                lse = jnp.where(lane8 == hq, m + jnp.log(den), lse)
            outs.append(jnp.where(lane < A_HEAD_DIM, halves[0], halves[1]))
        o_ref[...] = jnp.concatenate(outs, axis=1)
        l_ref[...] = lse

    return pl.pallas_call(
        body, name=name, grid=(T // WINDOW,),
        in_specs=[sp["q"], sp["kc"], sp["kp"], sp["vc"], sp["vp"], sp["tq"], sp["tq"], sp["tp"], sp["tp"], sp["sink"]],
        out_specs=[sp["row512"], sp["lse"]],
        out_shape=[jax.ShapeDtypeStruct((T, A_Q_W), F32), jax.ShapeDtypeStruct((T, A_Q_HEADS), F32)],
        compiler_params=_cparams("parallel"),
    )(proj, proj, proj, proj, proj, cos, sin_s, cos, sin_s, sinks_b)


def _attn_bwd(proj, cos, sin_s, sinks_b, o, lse, dmix, name):
    T = proj.shape[0]
    sp = _attn_specs()

    def body(q_ref, kc_ref, kp_ref, vc_ref, vp_ref, cq_ref, sq_ref, cp_ref, sp_ref, sink_ref, o_ref, l_ref, do_ref,
             dq_ref, dkc_ref, dkp_ref, dvc_ref, dvp_ref, dsink_ref):
        n = pl.program_id(0)
        cq, sq = cq_ref[...], sq_ref[...]
        cp, sps = cp_ref[...], sp_ref[...]
        cq4, sq4 = jnp.tile(cq, (1, A_Q_W // LANES)), jnp.tile(sq, (1, A_Q_W // LANES))
        q = _rope(q_ref[...], cq4, sq4) * (A_HEAD_DIM ** -0.5)
        kc = _rope(kc_ref[...], cq, sq)
        kp = _rope(kp_ref[...], cp, sps)
        kk = jnp.concatenate([kp, kc], axis=0)
        vv = jnp.concatenate([vp_ref[...], vc_ref[...]], axis=0)
        mask = _band_mask(n)
        lane = lax.broadcasted_iota(jnp.int32, (WINDOW, LANES), 1)
        do_all, o_all, l_all = do_ref[...], o_ref[...], l_ref[...]
        dkk = jnp.zeros((2 * WINDOW, LANES), F32)
        dvv = jnp.zeros((2 * WINDOW, LANES), F32)
        dqs = []
        lane8 = lax.broadcasted_iota(jnp.int32, (WINDOW, A_Q_HEADS), 1)
        head8 = lax.broadcasted_iota(jnp.int32, (1, A_Q_HEADS), 1)
        dsk = jnp.zeros((1, A_Q_HEADS), F32)
        for pb in range(A_Q_HEADS // 2):
            kvh = pb // 2
            q2 = q[:, pb * LANES:(pb + 1) * LANES]
            do2 = do_all[:, pb * LANES:(pb + 1) * LANES]
            prod = do2 * o_all[:, pb * LANES:(pb + 1) * LANES]
            halves = []
            for e in range(2):
                hq = 2 * pb + e
                in_half = (lane >= A_HEAD_DIM) if e else (lane < A_HEAD_DIM)
                delta = jnp.sum(jnp.where(in_half, prod, 0.0), axis=-1, keepdims=True)
                qe = _place(q2, e, kvh)
                doe = _place(do2, e, kvh)
                lh = jnp.sum(jnp.where(lane8 == hq, l_all, 0.0), axis=-1, keepdims=True)
                s = _dot(qe, kk, NT)
                p = jnp.where(mask, jnp.exp(jnp.where(mask, s, NEG_BIG) - lh), 0.0)
                dvv = dvv + _dot(p, doe, TN)
                dp = _dot(doe, vv, NT)
                ds = p * (dp - delta)
                dkk = dkk + _dot(ds, qe, TN)
                halves.append(_place(_dot(ds, kk, NN), kvh, e))
                p_sink = jnp.exp(sink_ref[hq:hq + 1, 0:1] - lh)
                dsk = jnp.where(head8 == hq, -jnp.sum(p_sink * delta, axis=(0, 1), keepdims=True), dsk)
            dqs.append(jnp.where(lane < A_HEAD_DIM, halves[0], halves[1]))
        dq = jnp.concatenate(dqs, axis=1) * (A_HEAD_DIM ** -0.5)
        dq_ref[...] = _rope_t(dq, cq4, sq4)
        dkp_ref[...] = _rope_t(dkk[:WINDOW], cp, sps)
        dkc_ref[...] = _rope_t(dkk[WINDOW:], cq, sq)
        dvp_ref[...] = dvv[:WINDOW]
        dvc_ref[...] = dvv[WINDOW:]

        @pl.when(n == 0)
        def _():
            dsink_ref[...] = dsk

        @pl.when(n > 0)
        def _():
            dsink_ref[...] += dsk

    return pl.pallas_call(
        body, name=name, grid=(T // WINDOW,),
        in_specs=[sp["q"], sp["kc"], sp["kp"], sp["vc"], sp["vp"], sp["tq"], sp["tq"], sp["tp"], sp["tp"], sp["sink"],
                  sp["row512"], sp["lse"], sp["row512"]],
        out_specs=[sp["row512"], sp["row128"], sp["row128"], sp["row128"], sp["row128"],
                   pl.BlockSpec((1, A_Q_HEADS), lambda n: (0, 0))],
        out_shape=[jax.ShapeDtypeStruct((T, A_Q_W), F32)] + [jax.ShapeDtypeStruct((T, LANES), F32)] * 4
        + [jax.ShapeDtypeStruct((1, A_Q_HEADS), F32)],
        compiler_params=_cparams("arbitrary"),
    )(proj, proj, proj, proj, proj, cos, sin_s, cos, sin_s, sinks_b, o, lse, dmix)


def _shift_down(x, prev8, k):
    if k == 0:
        return x
    row = lax.broadcasted_iota(jnp.int32, prev8.shape, 0)
    r = pltpu.roll(x, k, axis=0)
    top = jnp.where(row < k, pltpu.roll(prev8, k, axis=0), r[:SUBLANES])
    return jnp.concatenate([top, r[SUBLANES:]], axis=0)


def _shift_up(x, next8, k):
    if k == 0:
        return x
    R = x.shape[0]
    row = lax.broadcasted_iota(jnp.int32, next8.shape, 0)
    r = pltpu.roll(x, R - k, axis=0)
    bot = jnp.where(row >= SUBLANES - k, pltpu.roll(next8, SUBLANES - k, axis=0), r[R - SUBLANES:])
    return jnp.concatenate([r[:R - SUBLANES], bot], axis=0)


def _conv(x, prev8, w):
    y = x * w[CONV_K - 1:CONV_K]
    for j in range(CONV_K - 1):
        y = y + _shift_down(x, prev8, CONV_K - 1 - j) * w[j:j + 1]
    return y


def _conv_bwd(x, prev8, w, dy, next8_dy):
    dx = dy * w[CONV_K - 1:CONV_K]
    dws = []
    for j in range(CONV_K - 1):
        k = CONV_K - 1 - j
        dx = dx + _shift_up(dy, next8_dy, k) * w[j:j + 1]
        dws.append(jnp.sum(dy * _shift_down(x, prev8, k), axis=0, keepdims=True))
    dws.append(jnp.sum(dy * x, axis=0, keepdims=True))
    return dx, dws


def _dnconv_fwd(proj, conv_w, name):
    T = proj.shape[0]
    R = _tile(T, 512)
    cb0 = HP_QKVB // A_Q_W

    def body(x_ref, w_ref, o_ref, prev_ref):
        i = pl.program_id(1)

        @pl.when(i == 0)
        def _():
            prev_ref[...] = jnp.zeros_like(prev_ref)

        x = x_ref[...]
        o_ref[...] = _silu(_conv(x, prev_ref[...], w_ref[...]))
        prev_ref[...] = x[R - SUBLANES:]

    return pl.pallas_call(
        body, name=name, grid=(3, T // R),
        in_specs=[pl.BlockSpec((R, B_W), lambda j, i: (i, cb0 + j)), pl.BlockSpec((CONV_K, B_W), lambda j, i: (0, j))],
        out_specs=pl.BlockSpec((R, B_W), lambda j, i: (i, j)),
        out_shape=jax.ShapeDtypeStruct((T, 3 * B_W), F32),
        scratch_shapes=[pltpu.VMEM((SUBLANES, B_W), F32)],
        compiler_params=_cparams("parallel", "arbitrary"),
    )(proj, conv_w)


def _dnconv_bwd(proj, conv_w, dy, name):
    T = proj.shape[0]
    R = _tile(T, 512)
    nb = T // R
    cb0 = HP_QKVB // A_Q_W
    r8 = R // SUBLANES

    def body(x_ref, xp_ref, w_ref, dy_ref, dx_ref, dw_ref, next_ref):
        i = pl.program_id(1)
        blk = nb - 1 - i

        @pl.when(i == 0)
        def _():
            next_ref[...] = jnp.zeros_like(next_ref)

        x = x_ref[...]
        prev8 = jnp.where(blk > 0, xp_ref[...], 0.0)
        w = w_ref[...]
        dpre = dy_ref[...] * _dsilu(_conv(x, prev8, w))
        dx, dw = _conv_bwd(x, prev8, w, dpre, next_ref[...])
        dx_ref[...] = dx
        next_ref[...] = dpre[:SUBLANES]

        @pl.when(i == 0)
        def _():
            for j in range(CONV_K):
                dw_ref[j:j + 1, :] = dw[j]

        @pl.when(i > 0)
        def _():
            for j in range(CONV_K):
                dw_ref[j:j + 1, :] += dw[j]

    return pl.pallas_call(
        body, name=name, grid=(3, nb),
        in_specs=[pl.BlockSpec((R, B_W), lambda j, i: (nb - 1 - i, cb0 + j)),
                  pl.BlockSpec((SUBLANES, B_W), lambda j, i: (jnp.maximum((nb - 1 - i) * r8 - 1, 0), cb0 + j)),
                  pl.BlockSpec((CONV_K, B_W), lambda j, i: (0, j)),
                  pl.BlockSpec((R, B_W), lambda j, i: (nb - 1 - i, j))],
        out_specs=[pl.BlockSpec((R, B_W), lambda j, i: (nb - 1 - i, j)),
                   pl.BlockSpec((CONV_K, B_W), lambda j, i: (0, j))],
        out_shape=[jax.ShapeDtypeStruct((T, 3 * B_W), F32), jax.ShapeDtypeStruct((CONV_K, 3 * B_W), F32)],
        scratch_shapes=[pltpu.VMEM((SUBLANES, B_W), F32)],
        compiler_params=_cparams("parallel", "arbitrary"),
    )(proj, proj, conv_w, dy)


DK_SCALE = B_HEAD_DIM ** -0.5


BNN = (((2,), (1,)), ((0,), (0,)))
BNT = (((2,), (2,)), ((0,), (0,)))
BTN = (((1,), (1,)), ((0,), (0,)))


def _tri_inv(a):
    C = a.shape[-1]
    ri = lax.broadcasted_iota(jnp.int32, (C, C), 0)
    ci = lax.broadcasted_iota(jnp.int32, (C, C), 1)
    x = jnp.where(ri == ci, 1.0, 0.0)[None] - a
    p = _dotf(a, a, BNN)
    span = 2
    while span < C:
        x = x + _dotf(x, p, BNN)
        span *= 2
        if span < C:
            p = _dotf(p, p, BNN)
    return x


def _dn_chunk(qc, kc, v, gcol, grow, bcol, s0):
    C = B_CHUNK
    ri = lax.broadcasted_iota(jnp.int32, (C, C), 0)
    ci = lax.broadcasted_iota(jnp.int32, (C, C), 1)
    incl, strict = (ri >= ci)[None], (ri > ci)[None]
    rq = lax.rsqrt(jnp.sum(qc * qc, axis=-1, keepdims=True) + NORM_EPS)
    rk = lax.rsqrt(jnp.sum(kc * kc, axis=-1, keepdims=True) + NORM_EPS)
    qn = qc * rq
    q = qn * DK_SCALE
    k = kc * rk
    gc_col = jnp.sum(jnp.where(incl, grow, 0.0), axis=2, keepdims=True)
    gc_row = jnp.sum(jnp.where((ri <= ci)[None], gcol, 0.0), axis=1, keepdims=True)
    gl = jnp.sum(gcol, axis=1, keepdims=True)
    dincl = jnp.where(incl, jnp.exp(jnp.where(incl, gc_col - gc_row, 0.0)), 0.0)
    dstrict = jnp.where(strict, dincl, 0.0)
    eg = jnp.exp(gc_col)
    ekt = jnp.exp(gl - gc_col)
    egl = jnp.exp(gl)
    kb = k * bcol
    vb = v * bcol
    kbg = kb * eg
    a = _dot(kb, k, BNT) * dstrict
    tm = _tri_inv(a)
    u = _dot(tm, vb, BNN)
    w = _dot(tm, kbg, BNN)
    vn = u - _dot(w, s0, BNN)
    qk = _dot(q, k, BNT) * dincl
    qg = q * eg
    kt = k * ekt
    o = _dot(qg, s0, BNN) + _dot(qk, vn, BNN)
    s1 = s0 * egl + _dot(kt, vn, BTN)
    return dict(rq=rq, rk=rk, qn=qn, q=q, k=k, dincl=dincl, dstrict=dstrict, eg=eg, ekt=ekt, egl=egl, kb=kb, vb=vb,
                kbg=kbg, a=a, tm=tm, w=w, vn=vn, qk=qk, qg=qg, kt=kt, o=o, s1=s1, ri=ri[None], ci=ci[None])


def _heads(ref):
    return jnp.stack([ref[:, h * B_HEAD_DIM:(h + 1) * B_HEAD_DIM] for h in range(B_HEADS)])


def _store_heads(ref, val):
    for h in range(B_HEADS):
        ref[:, h * B_HEAD_DIM:(h + 1) * B_HEAD_DIM] = val[h]


def _dn_specs(N, rev):
    ix = (lambda n: N - 1 - n) if rev else (lambda n: n)
    wide = lambda cb: pl.BlockSpec((B_CHUNK, B_W), lambda n: (ix(n), cb))
    return dict(
        q=wide(0), k=wide(1), v=wide(2), z=wide(HP_Z // B_W), dob=wide(A_Q_W // B_W), out=wide(0),
        nw=pl.BlockSpec((1, LANES), lambda n: (0, 0)),
        row=pl.BlockSpec((B_HEADS, None, 1, B_CHUNK), lambda n: (0, ix(n), 0, 0)),
        state=pl.BlockSpec((B_HEADS, None, B_HEAD_DIM, B_HEAD_DIM), lambda n: (0, ix(n), 0, 0)),
    )


def _to_col(row):
    C = row.shape[-1]
    eye = lax.broadcasted_iota(jnp.int32, (C, C), 0) == lax.broadcasted_iota(jnp.int32, (C, C), 1)
    return jnp.sum(jnp.where(eye[None], row, 0.0), axis=2, keepdims=True)


def _to_row(col):
    C = col.shape[1]
    eye = lax.broadcasted_iota(jnp.int32, (C, C), 0) == lax.broadcasted_iota(jnp.int32, (C, C), 1)
    return jnp.sum(jnp.where(eye[None], col, 0.0), axis=1, keepdims=True)


def _dn_fwd(qkvc, proj, norm_w, grow, brow, name):
    T = qkvc.shape[0]
    N = T // B_CHUNK
    sp = _dn_specs(N, False)

    def body(q_ref, k_ref, v_ref, z_ref, nw_ref, gr_ref, br_ref, o_ref, st_ref, s_ref):
        n = pl.program_id(0)

        @pl.when(n == 0)
        def _():
            s_ref[...] = jnp.zeros_like(s_ref)

        s0 = s_ref[...]
        st_ref[...] = s0
        grow_v = gr_ref[...]
        f = _dn_chunk(_heads(q_ref), _heads(k_ref), _heads(v_ref), _to_col(grow_v), grow_v, _to_col(br_ref[...]), s0)
        o = f["o"]
        r = lax.rsqrt(jnp.mean(o * o, axis=-1, keepdims=True) + NORM_EPS)
        _store_heads(o_ref, o * r * nw_ref[...][None] * _silu(_heads(z_ref)))
        s_ref[...] = f["s1"]

    return pl.pallas_call(
        body, name=name, grid=(N,),
        in_specs=[sp["q"], sp["k"], sp["v"], sp["z"], sp["nw"], sp["row"], sp["row"]],
        out_specs=[sp["out"], sp["state"]],
        out_shape=[jax.ShapeDtypeStruct((T, B_W), F32),
                   jax.ShapeDtypeStruct((B_HEADS, N, B_HEAD_DIM, B_HEAD_DIM), F32)],
        scratch_shapes=[pltpu.VMEM((B_HEADS, B_HEAD_DIM, B_HEAD_DIM), F32)],
        compiler_params=_cparams("arbitrary"),
    )(qkvc, qkvc, qkvc, proj, norm_w, grow, brow)


def _dn_bwd(qkvc, proj, norm_w, grow, brow, states, dmix, name):
    T = qkvc.shape[0]
    N = T // B_CHUNK
    sp = _dn_specs(N, True)
    C = B_CHUNK

    def body(q_ref, k_ref, v_ref, z_ref, nw_ref, gr_ref, br_ref, st_ref, dob_ref,
             dq_ref, dk_ref, dv_ref, dz_ref, dg_ref, db_ref, dnw_ref, ds_ref):
        n = pl.program_id(0)

        @pl.when(n == 0)
        def _():
            ds_ref[...] = jnp.zeros_like(ds_ref)
            dnw_ref[...] = jnp.zeros_like(dnw_ref)

        s0 = st_ref[...]
        ds1 = ds_ref[...]
        v, z, nw, bcol_v = _heads(v_ref), _heads(z_ref), nw_ref[...][None], _to_col(br_ref[...])
        grow_v = gr_ref[...]
        f = _dn_chunk(_heads(q_ref), _heads(k_ref), v, _to_col(grow_v), grow_v, bcol_v, s0)
        o, q, k, qn = f["o"], f["q"], f["k"], f["qn"]
        eg, ekt, egl = f["eg"], f["ekt"], f["egl"]
        tm, w, vn, kb, vb, kbg = f["tm"], f["w"], f["vn"], f["kb"], f["vb"], f["kbg"]
        qg, kt, qk, a = f["qg"], f["kt"], f["qk"], f["a"]
        ri, ci = f["ri"], f["ci"]

        dob_v = _heads(dob_ref)
        r = lax.rsqrt(jnp.mean(o * o, axis=-1, keepdims=True) + NORM_EPS)
        sz = _silu(z)
        on = o * r
        dnw_ref[...] += jnp.sum(dob_v * sz * on, axis=1, keepdims=True)
        _store_heads(dz_ref, dob_v * on * nw * _dsilu(z))
        d_on = dob_v * sz * nw
        do = r * (d_on - on * jnp.mean(d_on * on, axis=-1, keepdims=True))

        dvn = _dot(qk, do, BTN) + _dot(kt, ds1, BNN)
        dqk = _dot(do, vn, BNT)
        dqg = _dot(do, s0, BNT)
        ds_ref[...] = _dot(qg, do, BTN) + egl * ds1 - _dot(w, dvn, BTN)
        dgl = jnp.sum(s0 * ds1, axis=(1, 2), keepdims=True) * egl
        dkt = _dot(vn, ds1, BNT)
        dw = -_dot(dvn, s0, BNT)
        dq = dqg * eg
        dgc = jnp.sum(dqg * qg, axis=-1, keepdims=True)
        dk = dkt * ekt
        t_kt = jnp.sum(dkt * kt, axis=-1, keepdims=True)
        dgl = dgl + jnp.sum(t_kt, axis=1, keepdims=True)
        dgc = dgc - t_kt
        dqkr = dqk * f["dincl"]
        dq = dq + _dot(dqkr, k, BNN)
        dk = dk + _dot(dqkr, q, BTN)
        e_qk = dqk * qk
        dgc = dgc + jnp.sum(e_qk, axis=-1, keepdims=True)
        dgc_row = -jnp.sum(e_qk, axis=1, keepdims=True)
        dtm = _dot(dvn, vb, BNT) + _dot(dw, kbg, BNT)
        dvb = _dot(tm, dvn, BTN)
        dkbg = _dot(tm, dw, BTN)
        dkb = dkbg * eg
        dgc = dgc + jnp.sum(dkbg * kbg, axis=-1, keepdims=True)
        da = -_dotf(tm, _dotf(dtm, tm, BNT), BTN)
        dkk = da * f["dstrict"]
        e_a = da * a
        dgc = dgc + jnp.sum(e_a, axis=-1, keepdims=True)
        dgc_row = dgc_row - jnp.sum(e_a, axis=1, keepdims=True)
        dkb = dkb + _dot(dkk, k, BNN)
        dk = dk + _dot(dkk, kb, BTN)
        dk = dk + dkb * bcol_v
        db_ref[...] = _to_row(jnp.sum(dkb * k, axis=-1, keepdims=True) + jnp.sum(dvb * v, axis=-1, keepdims=True))
        _store_heads(dv_ref, dvb * bcol_v)
        dgc_row = dgc_row + jnp.sum(jnp.where(ri == ci, dgc, 0.0), axis=1, keepdims=True)
        dg_ref[...] = jnp.sum(jnp.where(ci <= ri, _to_col(dgc_row), 0.0), axis=1, keepdims=True) + dgl
        dqs = dq * DK_SCALE
        _store_heads(dq_ref, f["rq"] * (dqs - qn * jnp.sum(dqs * qn, axis=-1, keepdims=True)))
        _store_heads(dk_ref, f["rk"] * (dk - k * jnp.sum(dk * k, axis=-1, keepdims=True)))

    return pl.pallas_call(
        body, name=name, grid=(N,),
        in_specs=[sp["q"], sp["k"], sp["v"], sp["z"], sp["nw"], sp["row"], sp["row"], sp["state"], sp["dob"]],
        out_specs=[sp["out"], sp["out"], sp["out"], sp["out"], sp["row"], sp["row"],
                   pl.BlockSpec((B_HEADS, 1, LANES), lambda n: (0, 0, 0))],
        out_shape=[jax.ShapeDtypeStruct((T, B_W), F32)] * 4
        + [jax.ShapeDtypeStruct((B_HEADS, N, 1, C), F32), jax.ShapeDtypeStruct((B_HEADS, N, 1, C), F32),
           jax.ShapeDtypeStruct((B_HEADS, 1, LANES), F32)],
        scratch_shapes=[pltpu.VMEM((B_HEADS, B_HEAD_DIM, B_HEAD_DIM), F32)],
        compiler_params=_cparams("arbitrary"),
    )(qkvc, qkvc, qkvc, proj, norm_w, grow, brow, states, dmix)


def _lru_gates(xc, wa_ref, wx_ref, ba, bx, sp):
    pre_r, pre_i = [], []
    for hb in range(LRU_BLOCKS):
        xb = xc[:, hb * LRU_BLOCK_W:(hb + 1) * LRU_BLOCK_W]
        pre_r.append(_dot(xb, wa_ref[hb]))
        pre_i.append(_dot(xb, wx_ref[hb]))
    r = _sigmoid(jnp.concatenate(pre_r, axis=1) + ba)
    i = _sigmoid(jnp.concatenate(pre_i, axis=1) + bx)
    la = -LRU_C * r * sp
    a = jnp.exp(la)
    th = jnp.tanh(la)
    s = jnp.sqrt(-2.0 * th / (1.0 - th))
    return r, i, a, s


def _scan_down(a, b):
    R = a.shape[0]
    row = lax.broadcasted_iota(jnp.int32, a.shape, 0)
    d = 1
    while d < R:
        ok = row >= d
        b = a * jnp.where(ok, pltpu.roll(b, d, axis=0), 0.0) + b
        a = a * jnp.where(ok, pltpu.roll(a, d, axis=0), 1.0)
        d *= 2
    return a, b


def _scan_up(a, b):
    R = a.shape[0]
    row = lax.broadcasted_iota(jnp.int32, a.shape, 0)
    d = 1
    while d < R:
        ok = row < R - d
        b = a * jnp.where(ok, pltpu.roll(b, R - d, axis=0), 0.0) + b
        a = a * jnp.where(ok, pltpu.roll(a, R - d, axis=0), 1.0)
        d *= 2
    return b


def _rglru_fwd(proj, conv_w, conv_b, wa, wx, ba, bx, sp, name):
    T = proj.shape[0]
    R = _tile(T, 256)
    W = D_MODEL

    def body(p_ref, cw_ref, cb_ref, wa_ref, wx_ref, ba_ref, bx_ref, sp_ref, hg_ref, h_ref, prev_ref, hc_ref):
        i = pl.program_id(0)

        @pl.when(i == 0)
        def _():
            prev_ref[...] = jnp.zeros_like(prev_ref)
            hc_ref[...] = jnp.zeros_like(hc_ref)

        xr = p_ref[:, :W]
        gate = p_ref[:, W:]
        xc = _conv(xr, prev_ref[...], cw_ref[...]) + cb_ref[...]
        prev_ref[...] = xr[R - SUBLANES:]
        r, ig, a, s = _lru_gates(xc, wa_ref, wx_ref, ba_ref[...], bx_ref[...], sp_ref[...])
        pa, hb = _scan_down(a, s * ig * xc)
        h = hb + pa * hc_ref[SUBLANES - 1:SUBLANES, :]
        h_ref[...] = h
        hg_ref[...] = (h * _gelu(gate)).astype(ACT_DTYPE)
        hc_ref[...] = h[R - SUBLANES:]

    vec = pl.BlockSpec((1, W), lambda i: (0, 0))
    wsp = pl.BlockSpec((LRU_BLOCKS, LRU_BLOCK_W, LRU_BLOCK_W), lambda i: (0, 0, 0))
    row = pl.BlockSpec((R, W), lambda i: (i, 0))
    return pl.pallas_call(
        body, name=name, grid=(T // R,),
        in_specs=[pl.BlockSpec((R, 2 * W), lambda i: (i, 0)), pl.BlockSpec((CONV_K, W), lambda i: (0, 0)),
                  vec, wsp, wsp, vec, vec, vec],
        out_specs=[row, row],
        out_shape=[jax.ShapeDtypeStruct((T, W), ACT_DTYPE), jax.ShapeDtypeStruct((T, W), F32)],
        scratch_shapes=[pltpu.VMEM((SUBLANES, W), F32), pltpu.VMEM((SUBLANES, W), F32)],
        compiler_params=_cparams("arbitrary"),
    )(proj, conv_w, conv_b, wa, wx, ba, bx, sp)


def _rglru_bwd(proj, conv_w, conv_b, wa, wx, ba, bx, sp, h, dhg, name):
    T = proj.shape[0]
    R = _tile(T, 256)
    nb = T // R
    r8 = R // SUBLANES
    W = D_MODEL

    def body(p_ref, pp_ref, cw_ref, cb_ref, wa_ref, wx_ref, ba_ref, bx_ref, sp_ref, h_ref, hp_ref, dhg_ref,
             dp_ref, dcw_ref, dcb_ref, dwa_ref, dwx_ref, dba_ref, dbx_ref, dsp_ref, lam_ref, nxt_ref):
        step = pl.program_id(0)
        blk = nb - 1 - step

        @pl.when(step == 0)
        def _():
            lam_ref[...] = jnp.zeros_like(lam_ref)
            nxt_ref[...] = jnp.zeros_like(nxt_ref)

        xr = p_ref[:, :W]
        gate = p_ref[:, W:]
        first = blk > 0
        prev8 = jnp.where(first, pp_ref[:, :W], 0.0)
        hprev8 = jnp.where(first, hp_ref[...], 0.0)
        cw = cw_ref[...]
        spv = sp_ref[...]
        xc = _conv(xr, prev8, cw) + cb_ref[...]
        r, ig, a, s = _lru_gates(xc, wa_ref, wx_ref, ba_ref[...], bx_ref[...], spv)
        hv = h_ref[...]
        dhg_v = dhg_ref[...]
        dgate = dhg_v * hv * _dgelu(gate)
        dh = dhg_v * _gelu(gate)
        row = lax.broadcasted_iota(jnp.int32, (R, W), 0)
        last = row == R - 1
        a_up = jnp.where(last, 0.0, pltpu.roll(a, R - 1, axis=0))
        lam = _scan_up(a_up, dh + jnp.where(last, lam_ref[0:1, :], 0.0))
        lam_ref[...] = (a * lam)[:SUBLANES]
        h_dn = _shift_down(hv, hprev8, 1)
        da = lam * h_dn
        bx_in = ig * xc
        dsv = lam * bx_in
        dig = lam * s * xc
        dxc = lam * s * ig
        dla = da * a - dsv * (a * a) / s
        dr = dla * (-LRU_C) * spv
        dsp = jnp.sum(dla * (-LRU_C) * r, axis=0, keepdims=True)
        dpr = dr * r * (1.0 - r)
        dpi = dig * ig * (1.0 - ig)
        dxc_parts, dwa_parts, dwx_parts = [], [], []
        for hb in range(LRU_BLOCKS):
            sl = slice(hb * LRU_BLOCK_W, (hb + 1) * LRU_BLOCK_W)
            xb, gr, gi = xc[:, sl], dpr[:, sl], dpi[:, sl]
            dxc_parts.append(_dot(gr, wa_ref[hb], NT) + _dot(gi, wx_ref[hb], NT))
            dwa_parts.append(_dot(xb, gr, TN))
            dwx_parts.append(_dot(xb, gi, TN))
        dxc = dxc + jnp.concatenate(dxc_parts, axis=1)
        dxr, dcw = _conv_bwd(xr, prev8, cw, dxc, nxt_ref[...])
        nxt_ref[...] = dxc[:SUBLANES]
        dp_ref[:, :W] = dxr.astype(ACT_DTYPE)
        dp_ref[:, W:] = dgate.astype(ACT_DTYPE)
        dcb = jnp.sum(dxc, axis=0, keepdims=True)
        dba = jnp.sum(dpr, axis=0, keepdims=True)
        dbx = jnp.sum(dpi, axis=0, keepdims=True)

        @pl.when(step == 0)
        def _():
            for j in range(CONV_K):
                dcw_ref[j:j + 1, :] = dcw[j]
            dcb_ref[...] = dcb
            dba_ref[...] = dba
            dbx_ref[...] = dbx
            dsp_ref[...] = dsp
            for hb in range(LRU_BLOCKS):
                dwa_ref[hb] = dwa_parts[hb]
                dwx_ref[hb] = dwx_parts[hb]

        @pl.when(step > 0)
        def _():
            for j in range(CONV_K):
                dcw_ref[j:j + 1, :] += dcw[j]
            dcb_ref[...] += dcb
            dba_ref[...] += dba
            dbx_ref[...] += dbx
            dsp_ref[...] += dsp
            for hb in range(LRU_BLOCKS):
                dwa_ref[hb] += dwa_parts[hb]
                dwx_ref[hb] += dwx_parts[hb]

    rv = lambda i: nb - 1 - i
    before = lambda i: jnp.maximum((nb - 1 - i) * r8 - 1, 0)
    vec = pl.BlockSpec((1, W), lambda i: (0, 0))
    cws = pl.BlockSpec((CONV_K, W), lambda i: (0, 0))
    wsp = pl.BlockSpec((LRU_BLOCKS, LRU_BLOCK_W, LRU_BLOCK_W), lambda i: (0, 0, 0))
    row = pl.BlockSpec((R, W), lambda i: (rv(i), 0))
    wshape = jax.ShapeDtypeStruct((LRU_BLOCKS, LRU_BLOCK_W, LRU_BLOCK_W), F32)
    vshape = jax.ShapeDtypeStruct((1, W), F32)
    return pl.pallas_call(
        body, name=name, grid=(nb,),
        in_specs=[pl.BlockSpec((R, 2 * W), lambda i: (rv(i), 0)), pl.BlockSpec((SUBLANES, 2 * W), lambda i: (before(i), 0)),
                  cws, vec, wsp, wsp, vec, vec, vec, row, pl.BlockSpec((SUBLANES, W), lambda i: (before(i), 0)), row],
        out_specs=[pl.BlockSpec((R, 2 * W), lambda i: (rv(i), 0)), cws, vec, wsp, wsp, vec, vec, vec],
        out_shape=[jax.ShapeDtypeStruct((T, 2 * W), ACT_DTYPE), jax.ShapeDtypeStruct((CONV_K, W), F32), vshape,
                   wshape, wshape, vshape, vshape, vshape],
        scratch_shapes=[pltpu.VMEM((SUBLANES, W), F32), pltpu.VMEM((SUBLANES, W), F32)],
        compiler_params=_cparams("arbitrary"),
    )(proj, proj, conv_w, conv_b, wa, wx, ba, bx, sp, h, h, dhg)


MESH = pl.DeviceIdType.MESH
ANY = pl.BlockSpec(memory_space=pl.ANY)


def _position():
    x, y, c = lax.axis_index("x"), lax.axis_index("y"), lax.axis_index("c")
    other_chips = [(1 - x, y), (x, 1 - y), (1 - x, 1 - y)]
    return x, y, c, other_chips


def _all_gather_weights(shards, name):
    n = len(shards)

    def body(*refs):
        ins, outs = refs[:n], refs[n:2 * n]
        send_sems, recv_sems = refs[2 * n:]
        x, y, c, chips = _position()
        me = 2 * x + y
        sibling = (x, y, 1 - c)

        def rcopy(t, k, src, dst, to):
            return pltpu.make_async_remote_copy(src_ref=src, dst_ref=dst, send_sem=send_sems.at[t, k],
                                                recv_sem=recv_sems.at[t, k], device_id=to, device_id_type=MESH)

        started = []
        for t in range(n):
            for j, (cx, cy) in enumerate(chips):
                cp = rcopy(t, j, ins[t].at[c], outs[t].at[me, c], (cx, cy, c))
                cp.start()
                started.append(cp)
        for t in range(n):
            for j, (cx, cy) in enumerate(chips):
                blk = outs[t].at[2 * cx + cy, c]
                rcopy(t, j, blk, blk, (cx, cy, c)).wait_recv()
                cp = rcopy(t, 3 + j, blk, blk, sibling)
                cp.start()
                started.append(cp)
        for t in range(n):
            for j, (cx, cy) in enumerate(chips):
                blk = outs[t].at[2 * cx + cy, 1 - c]
                rcopy(t, 3 + j, blk, blk, sibling).wait_recv()
        for cp in started:
            cp.wait_send()

    return pl.pallas_call(
        body, name=name, in_specs=[ANY] * n, out_specs=[ANY] * n,
        out_shape=[jax.ShapeDtypeStruct((N_CHIPS,) + s.shape, s.dtype) for s in shards],
        scratch_shapes=[pltpu.SemaphoreType.DMA((n, 6)), pltpu.SemaphoreType.DMA((n, 6))],
    )(*shards)


HBM = pl.BlockSpec(memory_space=pltpu.HBM)
SEM = pl.BlockSpec(memory_space=pltpu.SEMAPHORE)
EFFECT = pltpu.SideEffectType.DATAFLOW_SIDE_EFFECTING


def _gather_start(own, land, after, name):
    n = len(own)
    L = own[0].shape[0]
    n_sem = n * L

    def body(*refs):
        own_refs, land_refs = refs[:n], refs[n:2 * n]
        send_sems = refs[2 * n + 1:2 * n + 1 + n_sem]
        recv_sems = refs[2 * n + 1 + n_sem:2 * n + 1 + 2 * n_sem]
        x, y, c, chips = _position()
        me = 2 * x + y
        for layer in range(L):
            for t in range(n):
                for cx, cy in chips:
                    pltpu.make_async_remote_copy(
                        src_ref=own_refs[t].at[layer], dst_ref=land_refs[t].at[me, layer],
                        send_sem=send_sems[t * L + layer], recv_sem=recv_sems[t * L + layer],
                        device_id=(cx, cy, c), device_id_type=MESH).start()

    sems = (pltpu.SemaphoreType.DMA(()),) * (2 * n_sem)
    thru = [pltpu.HBM(a.shape, a.dtype) for a in list(own) + list(land)]
    out = pl.pallas_call(
        body, name=name, out_shape=(*sems, *thru),
        in_specs=[HBM] * (2 * n) + [pl.BlockSpec(memory_space=pl.ANY)], out_specs=(SEM,) * (2 * n_sem) + (HBM,) * (2 * n),
        input_output_aliases={i: 2 * n_sem + i for i in range(2 * n)},
        compiler_params=pltpu.CompilerParams(has_side_effects=EFFECT),
    )(*[pltpu.with_memory_space_constraint(a, pltpu.HBM) for a in list(own) + list(land)], after)
    send_sems, recv_sems, thru = out[:n_sem], out[n_sem:2 * n_sem], out[2 * n_sem:]
    return ([send_sems[t * L:(t + 1) * L] for t in range(n)], [recv_sems[t * L:(t + 1) * L] for t in range(n)],
            list(thru[:n]), list(thru[n:]))


def _gather_wait(send_sems, recv_sems, own, land, layer, after, name):
    n = len(own)

    def body(*refs):
        land_refs = refs[n:2 * n]
        s_sems, r_sems = refs[2 * n:3 * n], refs[3 * n:4 * n]
        x, y, c, _ = _position()
        for t in range(n):
            three = land_refs[t].at[pl.ds(0, N_CHIPS - 1), layer]
            cp = pltpu.make_async_remote_copy(src_ref=three, dst_ref=three, send_sem=s_sems[t], recv_sem=r_sems[t],
                                              device_id=(x, y, c), device_id_type=MESH)
            cp.wait_send()
            cp.wait_recv()

    thru = [pltpu.HBM(a.shape, a.dtype) for a in list(own) + list(land)]
    out = pl.pallas_call(
        body, name=name, out_shape=tuple(thru),
        in_specs=[HBM] * (2 * n) + [SEM] * (2 * n) + [pl.BlockSpec(memory_space=pl.ANY)], out_specs=(HBM,) * (2 * n),
        input_output_aliases={i: i for i in range(2 * n)},
        compiler_params=pltpu.CompilerParams(has_side_effects=EFFECT),
    )(*own, *land, *[s[layer] for s in send_sems], *[r[layer] for r in recv_sems], after)
    return list(out[:n]), list(out[n:])


def _rs_to_sibling(grads, name):
    n = len(grads)

    def body(*refs):
        ins, outs = refs[:n], refs[n:2 * n]
        send_sems, recv_sems = refs[2 * n:]
        x, y, c, _ = _position()
        cps = [pltpu.make_async_remote_copy(src_ref=ins[t].at[:, 1 - c], dst_ref=outs[t], send_sem=send_sems.at[t],
                                            recv_sem=recv_sems.at[t], device_id=(x, y, 1 - c), device_id_type=MESH)
               for t in range(n)]
        for cp in cps:
            cp.start()
        for cp in cps:
            cp.wait()

    return pl.pallas_call(
        body, name=name, in_specs=[ANY] * n, out_specs=[ANY] * n,
        out_shape=[jax.ShapeDtypeStruct((N_CHIPS,) + g.shape[2:], g.dtype) for g in grads],
        scratch_shapes=[pltpu.SemaphoreType.DMA((n,)), pltpu.SemaphoreType.DMA((n,))],
    )(*grads)


def _rs_across_chips(parts, name):
    n = len(parts)

    def body(*refs):
        ins, outs = refs[:n], refs[n:2 * n]
        send_sems, recv_sems = refs[2 * n:]
        x, y, c, chips = _position()
        me = 2 * x + y
        cps = []
        for t in range(n):
            for j, (cx, cy) in enumerate(chips):
                cps.append(pltpu.make_async_remote_copy(
                    src_ref=ins[t].at[2 * cx + cy], dst_ref=outs[t].at[me], send_sem=send_sems.at[t, j],
                    recv_sem=recv_sems.at[t, j], device_id=(cx, cy, c), device_id_type=MESH))
        for cp in cps:
            cp.start()
        for t in range(n):
            for j, (cx, cy) in enumerate(chips):
                blk = outs[t].at[2 * cx + cy]
                pltpu.make_async_remote_copy(src_ref=blk, dst_ref=blk, send_sem=send_sems.at[t, j],
                                             recv_sem=recv_sems.at[t, j], device_id=(cx, cy, c),
                                             device_id_type=MESH).wait_recv()
        for cp in cps:
            cp.wait_send()

    return pl.pallas_call(
        body, name=name, in_specs=[ANY] * n, out_specs=[ANY] * n,
        out_shape=[jax.ShapeDtypeStruct(p.shape, p.dtype) for p in parts],
        scratch_shapes=[pltpu.SemaphoreType.DMA((n, 3)), pltpu.SemaphoreType.DMA((n, 3))],
    )(*parts)


def _rs_join_halves(halves, name):
    n = len(halves)

    def body(*refs):
        ins, outs = refs[:n], refs[n:2 * n]
        send_sems, recv_sems = refs[2 * n:]
        x, y, c, _ = _position()
        cps = [pltpu.make_async_remote_copy(src_ref=ins[t].at[c], dst_ref=outs[t].at[c], send_sem=send_sems.at[t],
                                            recv_sem=recv_sems.at[t], device_id=(x, y, 1 - c), device_id_type=MESH)
               for t in range(n)]
        for cp in cps:
            cp.start()
        for t in range(n):
            blk = outs[t].at[1 - c]
            pltpu.make_async_remote_copy(src_ref=blk, dst_ref=blk, send_sem=send_sems.at[t], recv_sem=recv_sems.at[t],
                                         device_id=(x, y, 1 - c), device_id_type=MESH).wait_recv()
        for cp in cps:
            cp.wait_send()

    return pl.pallas_call(
        body, name=name, in_specs=[ANY] * n, out_specs=[ANY] * n,
        out_shape=[jax.ShapeDtypeStruct(h.shape, h.dtype) for h in halves],
        input_output_aliases={t: t for t in range(n)},
        scratch_shapes=[pltpu.SemaphoreType.DMA((n,)), pltpu.SemaphoreType.DMA((n,))],
    )(*halves)


def _all_gather_small(block, name):
    m_per, n = block.shape

    def body(x_ref, out_ref, send_sems, recv_sems, local_sem):
        x, y, c, chips = _position()
        me, sibling = (x, y, c), (x, y, 1 - c)

        def rows(px, py, pc):
            return out_ref.at[pl.ds((4 * px + 2 * py + pc) * m_per, m_per), :]

        def copy(k, blk, to, src=None):
            return pltpu.make_async_remote_copy(
                src_ref=rows(*blk) if src is None else src, dst_ref=rows(*blk), send_sem=send_sems.at[k],
                recv_sem=recv_sems.at[k], device_id=to, device_id_type=MESH)

        mine = pltpu.make_async_copy(x_ref, rows(*me), local_sem)
        mine.start()
        first = [copy(0, me, sibling, src=x_ref)]
        first += [copy(1 + j, me, (*chip, c), src=x_ref) for j, chip in enumerate(chips)]
        for cp in first:
            cp.start()
        passed = [copy(4 + j, (*chip, c), sibling) for j, chip in enumerate(chips)]
        for j, chip in enumerate(chips):
            copy(1 + j, (*chip, c), me).wait_recv()
            passed[j].start()
        copy(0, sibling, me).wait_recv()
        for j, chip in enumerate(chips):
            copy(4 + j, (*chip, 1 - c), me).wait_recv()
        for cp in first + passed:
            cp.wait_send()
        mine.wait()

    return pl.pallas_call(
        body, name=name, out_shape=jax.ShapeDtypeStruct((N_DEV * m_per, n), block.dtype),
        in_specs=[pl.BlockSpec(memory_space=pltpu.VMEM)], out_specs=pl.BlockSpec(memory_space=pltpu.VMEM),
        scratch_shapes=[pltpu.SemaphoreType.DMA((7,)), pltpu.SemaphoreType.DMA((7,)), pltpu.SemaphoreType.DMA],
    )(block)


def _row_tile(R, n):
    budget = 1 << 19
    if R * n <= budget or R % SUBLANES:
        return R
    t = R
    while t * n > budget and t % (2 * SUBLANES) == 0:
        t //= 2
    return t


def _pair_sum(g, recv, c_arr, name):
    _, _, R, n = g.shape
    tr = _row_tile(R, n)

    def body(c_ref, g_ref, r_ref, o_ref):
        o_ref[...] = (g_ref[...] + r_ref[...]).astype(ICI_DTYPE)

    grid_spec = pltpu.PrefetchScalarGridSpec(
        num_scalar_prefetch=1, grid=(N_CHIPS, R // tr),
        in_specs=[pl.BlockSpec((None, None, tr, n), lambda p, i, c: (p, c[0], i, 0)),
                  pl.BlockSpec((None, tr, n), lambda p, i, c: (p, i, 0))],
        out_specs=pl.BlockSpec((None, tr, n), lambda p, i, c: (p, i, 0)))
    return pl.pallas_call(
        body, name=name, grid_spec=grid_spec, out_shape=jax.ShapeDtypeStruct(recv.shape, ICI_DTYPE),
        compiler_params=_cparams("parallel", "parallel"),
    )(c_arr, g, recv)


def _chip_sum(recv, own, where, name):
    _, R, n = recv.shape
    tr = _row_tile(R, n)

    def body(w_ref, r0, r1, r2, r3, own_ref, o_ref):
        me = w_ref[0]
        terms = [jnp.where(me == k, own_ref[...], r[...]).astype(F32) for k, r in enumerate((r0, r1, r2, r3))]
        o_ref[...] = ((terms[0] + terms[1]) + terms[2]) + terms[3]

    def slot(k):
        return pl.BlockSpec((None, tr, n), lambda i, w: (w[1 + k], i, 0))

    grid_spec = pltpu.PrefetchScalarGridSpec(
        num_scalar_prefetch=1, grid=(R // tr,),
        in_specs=[slot(0), slot(1), slot(2), slot(3), pl.BlockSpec((None, tr, n), lambda i, w: (w[0], i, 0))],
        out_specs=pl.BlockSpec((None, tr, n), lambda i, w: (w[5], i, 0)))
    return pl.pallas_call(
        body, name=name, grid_spec=grid_spec, out_shape=jax.ShapeDtypeStruct((2, R, n), F32),
        compiler_params=_cparams("parallel"),
    )(where, recv, recv, recv, recv, own)


ADAM_C1 = 1.0 / (1.0 - ADAM_B1 ** ADAM_STEP)
ADAM_C2 = 1.0 / (1.0 - ADAM_B2 ** ADAM_STEP)


def _adamw_math(w, g, m, v):
    m = ADAM_B1 * m + (1.0 - ADAM_B1) * g
    v = ADAM_B2 * v + (1.0 - ADAM_B2) * (g * g)
    delta = -ADAM_LR * ((m * ADAM_C1) / (jnp.sqrt(v * ADAM_C2) + ADAM_EPS) + ADAM_WD * w)
    return delta, m, v


def _adamw(w, g, m, v, name):
    R, n = w.shape
    tr = _row_tile(R, n)

    def body(w_ref, g_ref, m_ref, v_ref, d_ref, nm_ref, nv_ref):
        d_ref[...], nm_ref[...], nv_ref[...] = _adamw_math(w_ref[...], g_ref[...], m_ref[...], v_ref[...])

    spec = pl.BlockSpec((tr, n), lambda i: (i, 0))
    return pl.pallas_call(
        body, name=name, grid=(R // tr,), in_specs=[spec] * 4, out_specs=[spec] * 3,
        out_shape=[jax.ShapeDtypeStruct((R, n), F32)] * 3, compiler_params=_cparams("parallel"),
    )(w, g, m, v)


def _adamw_small(w, gall, m, v, name):
    M, n = w.shape

    def body(w_ref, g_ref, m_ref, v_ref, gs_ref, d_ref, nm_ref, nv_ref):
        g = g_ref[0:M, :]
        for d in range(1, N_DEV):
            g = g + g_ref[d * M:(d + 1) * M, :]
        gs_ref[...] = g
        d_ref[...], nm_ref[...], nv_ref[...] = _adamw_math(w_ref[...], g, m_ref[...], v_ref[...])

    return pl.pallas_call(
        body, name=name, out_shape=[jax.ShapeDtypeStruct((M, n), F32)] * 4,
    )(w, gall, m, v)


SMALL_ROWS = 24
MID_ROWS = 4


def _pack_small(ln1_g, ln1_b, ln2_g, ln2_b, norm_w, sinks, a_log, dt_bias):
    mixed = jnp.concatenate([norm_w.reshape(-1), sinks.reshape(-1), a_log.reshape(-1), dt_bias.reshape(-1)])
    mixed = jnp.pad(mixed, (0, D_MODEL - mixed.shape[0]))[None]
    pad = jnp.zeros((SMALL_ROWS - 4 * DEPTH - 1, D_MODEL), F32)
    return jnp.concatenate([ln1_g, ln1_b, ln2_g, ln2_b, mixed, pad], axis=0)


def _unpack_small(p):
    mixed = p[4 * DEPTH]
    return (p[0:4], p[4:8], p[8:12], p[12:16], mixed[0:256].reshape(2, 128), mixed[256:272].reshape(2, 8),
            mixed[272:280].reshape(2, 4), mixed[280:288].reshape(2, 4))


def _pack_mid(conv_w, rconv_w, rconv_b, b_a, b_x, lam):
    flat = jnp.concatenate([conv_w.reshape(2, -1), rconv_w.reshape(2, -1), rconv_b, b_a, b_x, lam], axis=1)
    return jnp.pad(flat, ((0, 0), (0, MID_ROWS * D_MODEL - flat.shape[1]))).reshape(2, MID_ROWS, D_MODEL)


def _unpack_mid(p):
    lead = p.shape[:-2]
    f = p.reshape(lead + (MID_ROWS * D_MODEL,))
    return (f[..., 0:1536].reshape(lead + (4, 384)), f[..., 1536:2560].reshape(lead + (4, 256)),
            f[..., 2560:2816], f[..., 2816:3072], f[..., 3072:3328], f[..., 3328:3584])


def _cols_from_chips(g):
    p, L, R, n = g.shape
    return g.transpose(1, 2, 0, 3).reshape(L, R, p * n)


def _rows_from_chips(g):
    p, L, R, n = g.shape
    return g.transpose(1, 0, 2, 3).reshape(L, p * R, n)


def _cols_to_chips(g):
    L, R, n4 = g.shape
    return g.reshape(L, R, N_CHIPS, n4 // N_CHIPS).transpose(2, 0, 1, 3)


def _rows_to_chips(g):
    L, R4, n = g.shape
    return g.reshape(L, N_CHIPS, R4 // N_CHIPS, n).transpose(1, 0, 2, 3)


def _halves(a):
    return a.reshape(2, -1, a.shape[-1])


def _pad_hyb_cols(w):
    z = jnp.zeros(w.shape[:-1] + (HYB_PAD - HP_BG - 2 * B_HEADS,), w.dtype)
    return jnp.concatenate([w[..., 0:512], w[..., 768:2304], w[..., 2304:2816], w[..., 512:768], w[..., 2816:2824], z], axis=-1)


def _unpad_hyb_cols(w):
    return jnp.concatenate([w[..., 0:512], w[..., 2560:2816], w[..., 512:2048], w[..., 2048:2560], w[..., 2816:2824]], axis=-1)


def _hybrid_fwd(x, W, j, tables, sfx):
    cos, sin_s = tables
    T = x.shape[0]
    N = T // B_CHUNK
    proj = _matmul(x, W["hyb_w_in"][j], "nn", "hyb_in" + sfx)
    sinks_b = jnp.broadcast_to(W["hyb_sinks"][j][:, None], (A_Q_HEADS, LANES))
    o_a, lse = _attn_fwd(proj, cos, sin_s, sinks_b, "attn_fwd" + sfx)
    qkvc = _dnconv_fwd(proj, W["hyb_conv_w"][j], "dnconv_fwd" + sfx)
    bg = proj[:, HP_BG:HP_BG + 2 * B_HEADS]
    beta = jax.nn.sigmoid(bg[:, :B_HEADS])
    pre = bg[:, B_HEADS:] + W["hyb_dt_bias"][j][None]
    g = -jnp.exp(W["hyb_a_log"][j])[None] * jax.nn.softplus(pre)
    grow = g.T.reshape(B_HEADS, N, 1, B_CHUNK)
    brow = beta.T.reshape(B_HEADS, N, 1, B_CHUNK)
    nw = W["hyb_norm_w"][j][None]
    o_b, states = _dn_fwd(qkvc, proj, nw, grow, brow, "dn_fwd" + sfx)
    mix = jnp.concatenate([o_a, o_b], axis=1).astype(ACT_DTYPE)
    y = _matmul(mix, W["hyb_w_out"][j], "nn", "hyb_out" + sfx)
    res = dict(proj=proj, o_a=o_a, lse=lse, qkvc=qkvc, beta=beta, pre=pre, g=g, grow=grow, brow=brow,
               states=states, mix=mix, sinks_b=sinks_b, nw=nw)
    return y, res


def _hybrid_bwd(x, du, dub, W, j, res, tables, sfx):
    cos, sin_s = tables
    T = x.shape[0]
    proj = res["proj"]
    d_wout = _matmul(res["mix"], dub, "tn", "hyb_out_dw" + sfx)
    dmix = _matmul(dub, W["hyb_w_out"][j], "nt", "hyb_out_dx" + sfx)
    dq, dkc, dkp, dvc, dvp, dsink = _attn_bwd(proj, cos, sin_s, res["sinks_b"], res["o_a"], res["lse"], dmix,
                                               "attn_bwd" + sfx)
    zpad = jnp.zeros((WINDOW, LANES), F32)
    dk = dkc + jnp.concatenate([dkp[WINDOW:], zpad], axis=0)
    dv = dvc + jnp.concatenate([dvp[WINDOW:], zpad], axis=0)
    dqc, dkcv, dvcv, dz, dg4, dbeta4, dnw = _dn_bwd(res["qkvc"], proj, res["nw"], res["grow"], res["brow"],
                                                    res["states"], dmix, "dn_bwd" + sfx)
    dqkvc = jnp.concatenate([dqc, dkcv, dvcv], axis=1)
    dqkvb, dconv = _dnconv_bwd(proj, W["hyb_conv_w"][j], dqkvc, "dnconv_bwd" + sfx)
    dg = dg4.reshape(B_HEADS, T).T
    dbeta = dbeta4.reshape(B_HEADS, T).T
    beta = res["beta"]
    dbeta_logit = dbeta * beta * (1.0 - beta)
    da_logit = dg * (-jnp.exp(W["hyb_a_log"][j]))[None] * jax.nn.sigmoid(res["pre"])
    d_dt_bias = jnp.sum(da_logit, axis=0)
    d_a_log = jnp.sum(dg * res["g"], axis=0)
    zcols = jnp.zeros((T, HYB_PAD - HP_BG - 2 * B_HEADS), F32)
    dproj = jnp.concatenate([dq, dqkvb, dz, dk, dv, dbeta_logit, da_logit, zcols], axis=1).astype(ACT_DTYPE)
    d_win = _matmul(x, dproj, "tn", "hyb_in_dw" + sfx)
    dx = _matmul(dproj, W["hyb_w_in"][j], "nt", "hyb_in_dx" + sfx, epi=_epi_add_residual, extra=du)
    grads = dict(hyb_w_in=d_win, hyb_w_out=d_wout, hyb_sinks=dsink[0], hyb_conv_w=dconv, hyb_a_log=d_a_log,
                 hyb_dt_bias=d_dt_bias, hyb_norm_w=jnp.sum(dnw[:, 0, :], axis=0))
    return dx, grads


def _rec_fwd(x, W, j, sfx):
    proj = _matmul(x, W["rec_w_in"][j], "nn", "rec_in" + sfx)
    sp = jax.nn.softplus(-W["rec_lambda"][j])[None]
    hg, h = _rglru_fwd(proj, W["rec_conv_w"][j], W["rec_conv_b"][j][None], W["rec_w_a"][j], W["rec_w_x"][j],
                       W["rec_b_a"][j][None], W["rec_b_x"][j][None], sp, "rglru_fwd" + sfx)
    y = _matmul(hg, W["rec_w_out"][j], "nn", "rec_out" + sfx)
    return y, dict(proj=proj, hg=hg, h=h, sp=sp)


def _rec_bwd(x, du, dub, W, j, res, sfx):
    d_wout = _matmul(res["hg"], dub, "tn", "rec_out_dw" + sfx)
    dhg = _matmul(dub, W["rec_w_out"][j], "nt", "rec_out_dx" + sfx)
    dproj, dcw, dcb, dwa, dwx, dba, dbx, dsp = _rglru_bwd(
        res["proj"], W["rec_conv_w"][j], W["rec_conv_b"][j][None], W["rec_w_a"][j], W["rec_w_x"][j],
        W["rec_b_a"][j][None], W["rec_b_x"][j][None], res["sp"], res["h"], dhg, "rglru_bwd" + sfx)
    d_lam = dsp[0] * (-jax.nn.sigmoid(-W["rec_lambda"][j]))
    d_win = _matmul(x, dproj, "tn", "rec_in_dw" + sfx)
    dx = _matmul(dproj, W["rec_w_in"][j], "nt", "rec_in_dx" + sfx, epi=_epi_add_residual, extra=du)
    grads = dict(rec_w_in=d_win, rec_w_out=d_wout, rec_conv_w=dcw, rec_conv_b=dcb[0], rec_w_a=dwa, rec_w_x=dwx,
                 rec_b_a=dba[0], rec_b_x=dbx[0], rec_lambda=d_lam)
    return dx, grads


def _local_step(x, tgt, W, mlp_w):
    T = x.shape[0]
    tables = _rope_tables(T)
    acts = []
    xb = x.astype(ACT_DTYPE)
    for layer in range(DEPTH):
        j, sfx = layer // 2, ""
        if layer % 2 == 0:
            y, res = _hybrid_fwd(xb, W, j, tables, sfx)
        else:
            y, res = _rec_fwd(xb, W, j, sfx)
        x1, x1b = _ln_fwd(x, y, W["ln1_g"][layer][None], W["ln1_b"][layer][None], "ln_fwd")
        w1, w2, wl = mlp_w(layer, x1b)
        h1 = _matmul(x1b, w1, "nn", "mlp_up", out_dtype=ACT_DTYPE, b_chips=("j", wl))
        y2 = _matmul(h1, w2, "nn", "mlp_down", a_fn=_relu2, b_chips=("k", wl))
        x2, x2b = _ln_fwd(x1, y2, W["ln2_g"][layer][None], W["ln2_b"][layer][None], "ln_fwd")
        acts.append(dict(x=x, xb=xb, y=y, res=res, x1=x1, x1b=x1b, h1=h1, y2=y2))
        x, xb = x2, x2b
    dx, loss = _loss_head(x, tgt, "loss_head")
    per_layer = [None] * DEPTH
    d_w1 = lax.empty((N_CHIPS, DEPTH, D_MODEL, D_FF // N_CHIPS), F32)
    d_w2 = lax.empty((N_CHIPS, DEPTH, D_FF // N_CHIPS, D_MODEL), F32)
    for layer in reversed(range(DEPTH)):
        j, a = layer // 2, acts[layer]
        du2, du2b, dg2, db2 = _ln_bwd(a["x1"], a["y2"], W["ln2_g"][layer][None], dx, "ln_bwd")
        w1, w2, wl = mlp_w(layer, du2b)
        d_w2 = _matmul(a["h1"], du2b, "tn", "mlp_down_dw", a_fn=_relu2, out_chips=("i", layer, d_w2))
        dh1 = _matmul(du2b, w2, "nt", "mlp_down_dx", epi=_epi_drelu2, extra=a["h1"], out_dtype=ACT_DTYPE,
                      b_chips=("j", wl))
        d_w1 = _matmul(a["x1b"], dh1, "tn", "mlp_up_dw", out_chips=("j", layer, d_w1))
        dx1 = _matmul(dh1, w1, "nt", "mlp_up_dx", epi=_epi_add_residual, extra=du2, b_chips=("k", wl))
        du1, du1b, dg1, db1 = _ln_bwd(a["x"], a["y"], W["ln1_g"][layer][None], dx1, "ln_bwd")
        if layer % 2 == 0:
            dx, g = _hybrid_bwd(a["xb"], du1, du1b, W, j, a["res"], tables, "")
        else:
            dx, g = _rec_bwd(a["xb"], du1, du1b, W, j, a["res"], "")
        g.update(ln1_g=dg1[0], ln1_b=db1[0], ln2_g=dg2[0], ln2_b=db2[0])
        per_layer[layer] = g
    grads = dict(mlp_w1=d_w1, mlp_w2=d_w2)
    for name in ("ln1_g", "ln1_b", "ln2_g", "ln2_b"):
        grads[name] = jnp.stack([per_layer[l][name] for l in range(DEPTH)])
    for name in per_layer[0]:
        if name not in grads:
            grads[name] = jnp.stack([per_layer[l][name] for l in (0, 2)])
    for name in per_layer[1]:
        if name not in grads:
            grads[name] = jnp.stack([per_layer[l][name] for l in (1, 3)])
    return loss, dx, grads


BIG = ("hyb_w_in", "hyb_w_out", "rec_w_in", "rec_w_out", "mlp_w1", "mlp_w2", "rec_w_a", "rec_w_x")
COL_SHARDED = ("hyb_w_in", "rec_w_in", "mlp_w1")
CHIP_MAJOR = ("mlp_w1", "mlp_w2")
MID = ("hyb_conv_w", "rec_conv_w", "rec_conv_b", "rec_b_a", "rec_b_x", "rec_lambda")
SMALL = ("ln1_g", "ln1_b", "ln2_g", "ln2_b", "hyb_norm_w", "hyb_sinks", "hyb_a_log", "hyb_dt_bias")
WEIGHTS = ("hyb_w_in", "hyb_sinks", "hyb_conv_w", "hyb_a_log", "hyb_dt_bias", "hyb_norm_w", "hyb_w_out", "rec_w_in",
           "rec_conv_w", "rec_conv_b", "rec_w_a", "rec_b_a", "rec_w_x", "rec_b_x", "rec_lambda", "rec_w_out", "ln1_g",
           "ln1_b", "mlp_w1", "mlp_w2", "ln2_g", "ln2_b")


def _gather_full_weights(w):
    wb = {k: w[k].astype(MXU_DTYPE) for k in BIG}
    shards = [_halves(wb[k][:1] if k in CHIP_MAJOR else wb[k]) for k in BIG]
    shards.append(_pack_mid(*[w[k] for k in MID]))
    got = _all_gather_weights(shards, "all_gather_weights")
    me = 2 * lax.axis_index("x") + lax.axis_index("y")
    got = [lax.dynamic_update_slice(g, s[None], (me, 0, 0, 0)) for s, g in zip(shards, got)]

    own = [wb[k][1:] for k in CHIP_MAJOR]
    land = [lax.dynamic_update_slice(lax.empty((N_CHIPS,) + o.shape, o.dtype), o[None], (me, 0, 0, 0)) for o in own]
    send_sems, recv_sems, own, land = _gather_start(own, land, got[-1], "gather_start")
    first = {k: g.reshape((N_CHIPS, 1) + w[k].shape[1:]) for k, g in zip(BIG, got[:-1]) if k in CHIP_MAJOR}
    pending = dict(own=own, land=land, waited=0)

    def mlp_w(layer, after):
        if layer == 0:
            return first["mlp_w1"], first["mlp_w2"], 0
        while pending["waited"] < layer:
            pending["own"], pending["land"] = _gather_wait(send_sems, recv_sems, pending["own"], pending["land"],
                                                           pending["waited"], after, "gather_wait_%d" % pending["waited"])
            pending["waited"] += 1
        return pending["land"][0], pending["land"][1], layer - 1

    W = {}
    for k, g in zip(BIG, got[:-1]):
        L = w[k].shape[0]
        if k in CHIP_MAJOR:
            continue
        elif k in ("rec_w_a", "rec_w_x"):
            g5 = g.reshape((N_CHIPS,) + w[k].shape)
            W[k] = g5.transpose(1, 2, 0, 3, 4).reshape(L, LRU_BLOCKS, LRU_BLOCK_W, LRU_BLOCK_W)
        else:
            g4 = g.reshape((N_CHIPS,) + w[k].shape)
            W[k] = _cols_from_chips(g4) if k in COL_SHARDED else _rows_from_chips(g4)
    W["hyb_w_in"] = _pad_hyb_cols(W["hyb_w_in"])
    conv_w, rconv_w, rconv_b, b_a, b_x, lam = _unpack_mid(got[-1])
    W["hyb_conv_w"] = conv_w.transpose(1, 2, 0, 3).reshape(2, CONV_K, 3 * B_W)
    W["rec_conv_w"] = rconv_w.transpose(1, 2, 0, 3).reshape(2, CONV_K, D_MODEL)
    for k, v in (("rec_conv_b", rconv_b), ("rec_b_a", b_a), ("rec_b_x", b_x), ("rec_lambda", lam)):
        W[k] = v.transpose(1, 0, 2).reshape(2, D_MODEL)
    for k in SMALL:
        W[k] = w[k]
    return W, mlp_w


def _grads_by_chip(grads):
    out = []
    g = dict(grads)
    g["hyb_w_in"] = _unpad_hyb_cols(g["hyb_w_in"])
    for k in BIG:
        v = g[k]
        if k in CHIP_MAJOR:
            pass
        elif k in ("rec_w_a", "rec_w_x"):
            L = v.shape[0]
            v = v.reshape(L, LRU_BLOCKS, N_CHIPS, LRU_BLOCK_W // N_CHIPS, LRU_BLOCK_W).transpose(2, 0, 1, 3, 4)
        else:
            v = _cols_to_chips(v) if k in COL_SHARDED else _rows_to_chips(v)
        out.append(v.reshape(N_CHIPS, 2, -1, v.shape[-1]))
    conv_w = g["hyb_conv_w"].reshape(2, CONV_K, N_CHIPS, -1).transpose(2, 0, 1, 3)
    rconv_w = g["rec_conv_w"].reshape(2, CONV_K, N_CHIPS, -1).transpose(2, 0, 1, 3)
    vecs = [g[k].reshape(2, N_CHIPS, -1).transpose(1, 0, 2) for k in ("rec_conv_b", "rec_b_a", "rec_b_x", "rec_lambda")]
    mid = jnp.stack([_pack_mid(conv_w[p], rconv_w[p], *[v[p] for v in vecs]) for p in range(N_CHIPS)])
    out.append(mid)
    return out


def kernel(x, hyb_w_in, hyb_sinks, hyb_conv_w, hyb_a_log, hyb_dt_bias, hyb_norm_w, hyb_w_out, rec_w_in, rec_conv_w, rec_conv_b, rec_w_a, rec_b_a, rec_w_x, rec_b_x, rec_lambda, rec_w_out, ln1_g, ln1_b, mlp_w1, mlp_w2, ln2_g, ln2_b, loss_target, m_hyb_w_in, m_hyb_sinks, m_hyb_conv_w, m_hyb_a_log, m_hyb_dt_bias, m_hyb_norm_w, m_hyb_w_out, m_rec_w_in, m_rec_conv_w, m_rec_conv_b, m_rec_w_a, m_rec_b_a, m_rec_w_x, m_rec_b_x, m_rec_lambda, m_rec_w_out, m_ln1_g, m_ln1_b, m_mlp_w1, m_mlp_w2, m_ln2_g, m_ln2_b, v_hyb_w_in, v_hyb_sinks, v_hyb_conv_w, v_hyb_a_log, v_hyb_dt_bias, v_hyb_norm_w, v_hyb_w_out, v_rec_w_in, v_rec_conv_w, v_rec_conv_b, v_rec_w_a, v_rec_b_a, v_rec_w_x, v_rec_b_x, v_rec_lambda, v_rec_w_out, v_ln1_g, v_ln1_b, v_mlp_w1, v_mlp_w2, v_ln2_g, v_ln2_b):
    args = locals()
    w = {k: args[k] for k in WEIGHTS}
    m = {k: args["m_" + k] for k in WEIGHTS}
    v = {k: args["v_" + k] for k in WEIGHTS}

    W, mlp_w = _gather_full_weights(w)
    loss, dx, grads = _local_step(x[0], loss_target[0], W, mlp_w)
    loss = lax.psum(loss[0, 0], ("x", "y", "c"))

    core = lax.axis_index("c").astype(jnp.int32)
    me = (2 * lax.axis_index("x") + lax.axis_index("y")).astype(jnp.int32)
    slots = jnp.arange(N_CHIPS, dtype=jnp.int32)
    where = jnp.concatenate([me[None], jnp.where(slots == me, (slots + 1) % N_CHIPS, slots), core[None]])
    by_chip = _grads_by_chip(grads)
    from_sibling = _rs_to_sibling(by_chip, "rs_to_sibling")
    pair = [_pair_sum(g, r, core[None], "pair_sum") for g, r in zip(by_chip, from_sibling)]
    from_chips = _rs_across_chips(pair, "rs_across_chips")
    half = [_chip_sum(r, p, where, "chip_sum") for r, p in zip(from_chips, pair)]
    joined = _rs_join_halves(half, "rs_join_halves")

    g_out, d_out, m_out, v_out = {}, {}, {}, {}

    def update(name, g2d):
        shape = w[name].shape
        n = g2d.shape[-1]
        d, nm, nv = _adamw(w[name].reshape(-1, n), g2d, m[name].reshape(-1, n), v[name].reshape(-1, n), "adamw")
        g_out[name], d_out[name] = g2d.reshape(shape), d.reshape(shape)
        m_out[name], v_out[name] = nm.reshape(shape), nv.reshape(shape)

    for k, g in zip(BIG, joined[:-1]):
        update(k, g.reshape(-1, g.shape[-1]))
    mid_w, mid_m, mid_v = (_pack_mid(*[t[k] for k in MID]).reshape(-1, D_MODEL) for t in (w, m, v))
    mid_g = joined[-1].reshape(-1, D_MODEL)
    mid_d, mid_nm, mid_nv = _adamw(mid_w, mid_g, mid_m, mid_v, "adamw_mid")
    for dst, packed in ((g_out, mid_g), (d_out, mid_d), (m_out, mid_nm), (v_out, mid_nv)):
        for k, val in zip(MID, _unpack_mid(packed.reshape(2, MID_ROWS, D_MODEL))):
            dst[k] = val.reshape(w[k].shape)

    small_g = _pack_small(*[grads[k] for k in SMALL])
    small_all = _all_gather_small(small_g, "all_gather_small")
    sw, sm, sv = (_pack_small(*[t[k] for k in SMALL]) for t in (w, m, v))
    sg, sd, snm, snv = _adamw_small(sw, small_all, sm, sv, "adamw_small")
    for dst, packed in ((g_out, sg), (d_out, sd), (m_out, snm), (v_out, snv)):
        for k, val in zip(SMALL, _unpack_small(packed)):
            dst[k] = val

    return (loss, dx[None], *[g_out[k] for k in WEIGHTS], *[d_out[k] for k in WEIGHTS],
            *[m_out[k] for k in WEIGHTS], *[v_out[k] for k in WEIGHTS])
```

```python
import functools

import jax
import jax.numpy as jnp
import numpy as np
from jax import lax
from jax.experimental import pallas as pl
from jax.experimental.pallas import tpu as pltpu

F32 = jnp.float32
MXU_DTYPE = jnp.bfloat16
ACT_DTYPE = jnp.bfloat16
ICI_DTYPE = jnp.bfloat16

D_MODEL = 1024
DEPTH = 4
A_HEAD_DIM = 64
A_Q_HEADS = 8
A_KV_HEADS = 2
WINDOW = 128
ROPE_THETA = 10000.0
B_HEADS = 4
B_HEAD_DIM = 128
B_CHUNK = 64
CONV_K = 4
LRU_BLOCKS = 4
LRU_BLOCK_W = D_MODEL // LRU_BLOCKS
LRU_C = 8.0
D_FF = 4 * D_MODEL
A_Q_W = A_Q_HEADS * A_HEAD_DIM
A_KV_W = A_KV_HEADS * A_HEAD_DIM
B_W = B_HEADS * B_HEAD_DIM
HYB_PROJ = A_Q_W + 2 * A_KV_W + 4 * B_W + 2 * B_HEADS
DN_ALPHA = (2 * DEPTH) ** 0.25
LN_EPS = 1e-5
NORM_EPS = 1e-6
ADAM_LR = 0.001
ADAM_B1 = 0.9
ADAM_B2 = 0.999
ADAM_EPS = 1e-08
ADAM_WD = 0.01
ADAM_STEP = 10

HP_Q = 0
HP_QKVB = 512
HP_Z = 2048
HP_K = 2560
HP_V = 2688
HP_BG = 2816
HYB_PAD = 3072

N_CHIPS = 4
N_DEV = 8
V7X_VMEM_LIMIT = 48 * 1024 * 1024
LANES = 128
SUBLANES = 8
NEG_BIG = -1e30

NN = (((1,), (0,)), ((), ()))
NT = (((1,), (1,)), ((), ()))
TN = (((0,), (0,)), ((), ()))


def _cparams(*sem):
    return pltpu.CompilerParams(dimension_semantics=sem, vmem_limit_bytes=V7X_VMEM_LIMIT)


def _dot(a, b, dims=NN):
    return lax.dot_general(a.astype(MXU_DTYPE), b.astype(MXU_DTYPE), dims, preferred_element_type=F32)


def _split_bf16(a):
    hi = a.astype(jnp.bfloat16)
    return hi, (a - hi.astype(F32)).astype(jnp.bfloat16)


def _dotf(a, b, dims=NN):
    ah, al = _split_bf16(a)
    bh, bl = _split_bf16(b)
    dg = functools.partial(lax.dot_general, dimension_numbers=dims, preferred_element_type=F32)
    return dg(ah, bh) + (dg(ah, bl) + dg(al, bh))


def _tile(dim, pref):
    t = min(dim, pref)
    while dim % t:
        t //= 2
    return t


def _sigmoid(x):
    return 1.0 / (1.0 + jnp.exp(-x))


def _silu(x):
    return x * _sigmoid(x)


def _dsilu(x):
    s = _sigmoid(x)
    return s * (1.0 + x * (1.0 - s))


GELU_C = 0.7978845608028654
GELU_A = 0.044715


def _gelu(x):
    return 0.5 * x * (1.0 + jnp.tanh(GELU_C * (x + GELU_A * x * x * x)))


def _dgelu(x):
    t = jnp.tanh(GELU_C * (x + GELU_A * x * x * x))
    return 0.5 * (1.0 + t) + 0.5 * x * (1.0 - t * t) * GELU_C * (1.0 + 3.0 * GELU_A * x * x)


def _matmul(a, b, mode, name, *, tm=1024, tn=1024, tk=1024, a_fn=None, epi=None, extra=None, out_dtype=F32,
            b_chips=None, out_chips=None):
    if mode == "tn":
        K, M = a.shape
    else:
        M, K = a.shape
    if b_chips is not None:
        g, b_layer = b_chips
        r, n = b.shape[2:]
        n_dim, k_dim = (r, n) if mode == "nt" else (n, r)
        N = N_CHIPS * n_dim if g == "j" else n_dim
        assert K == (N_CHIPS * k_dim if g == "k" else k_dim)
        if g == "j":
            tn = n_dim
        else:
            tk = k_dim
    elif mode == "nt":
        N = b.shape[0]
    else:
        N = b.shape[1]
    if out_chips is not None:
        og, o_layer, o_buf = out_chips
        if og == "j":
            tn = o_buf.shape[3]
        else:
            tm = o_buf.shape[2]
    tm, tn, tk = _tile(M, tm), _tile(N, tn), _tile(K, tk)
    nk = K // tk
    if mode == "tn":
        a_spec = pl.BlockSpec((tk, tm), lambda i, j, k: (k, i))
    else:
        a_spec = pl.BlockSpec((tm, tk), lambda i, j, k: (i, k))
    b_block = (tn, tk) if mode == "nt" else (tk, tn)
    if b_chips is None:
        b_spec = pl.BlockSpec(b_block, (lambda i, j, k: (j, k)) if mode == "nt" else (lambda i, j, k: (k, j)))
    elif mode == "nt":
        b_spec = pl.BlockSpec((None, None) + b_block, (lambda i, j, k: (j, b_layer, 0, k)) if g == "j"
                              else (lambda i, j, k: (k, b_layer, j, 0)))
    else:
        b_spec = pl.BlockSpec((None, None) + b_block, (lambda i, j, k: (j, b_layer, k, 0)) if g == "j"
                              else (lambda i, j, k: (k, b_layer, 0, j)))
    o_spec = pl.BlockSpec((tm, tn), lambda i, j, k: (i, j))
    e_spec = o_spec
    if out_chips is not None:
        o_spec = pl.BlockSpec((None, None, tm, tn), (lambda i, j, k: (j, o_layer, i, 0)) if og == "j"
                              else (lambda i, j, k: (i, o_layer, 0, j)))
    dims = {"nn": NN, "nt": NT, "tn": TN}[mode]
    has_extra = extra is not None
    n_in = 2 + has_extra + (out_chips is not None)

    def body(*refs):
        a_ref, b_ref = refs[0], refs[1]
        e_ref = refs[2] if has_extra else None
        o_ref = refs[n_in]
        av = a_ref[...]
        if a_fn is not None:
            av = a_fn(av)
        part = _dot(av, b_ref[...], dims)

        def finish(acc):
            if epi is not None:
                acc = epi(acc, e_ref[...])
            o_ref[...] = acc.astype(out_dtype)

        if nk == 1:
            finish(part)
        else:
            acc_ref = refs[-1]
            k = pl.program_id(2)

            @pl.when(k == 0)
            def _():
                acc_ref[...] = part

            @pl.when(k > 0)
            def _():
                acc_ref[...] += part

            @pl.when(k == nk - 1)
            def _():
                finish(acc_ref[...])

    in_specs = [a_spec, b_spec] + ([e_spec] if has_extra else [])
    args = (a, b) + ((extra,) if has_extra else ())
    out_shape = jax.ShapeDtypeStruct((M, N), out_dtype)
    aliases = {}
    if out_chips is not None:
        in_specs.append(pl.BlockSpec(memory_space=pl.ANY))
        args += (o_buf,)
        out_shape = jax.ShapeDtypeStruct(o_buf.shape, o_buf.dtype)
        aliases = {n_in - 1: 0}
    return pl.pallas_call(
        body, name=name, grid=(M // tm, N // tn, nk), in_specs=in_specs, out_specs=o_spec, out_shape=out_shape,
        input_output_aliases=aliases,
        scratch_shapes=[pltpu.VMEM((tm, tn), F32)] if nk > 1 else [],
        compiler_params=_cparams("parallel", "parallel", "arbitrary"),
    )(*args)


def _relu2(v):
    r = jnp.maximum(v, 0.0)
    return r * r


def _epi_drelu2(acc, h):
    return acc * (2.0 * jnp.maximum(h, 0.0))


def _epi_add_residual(acc, du):
    return acc + DN_ALPHA * du


def _ln_fwd(x, y, g, b, name):
    T, D = x.shape
    tr = _tile(T, 512)

    def body(x_ref, y_ref, g_ref, b_ref, o_ref, ob_ref):
        u = DN_ALPHA * x_ref[...] + y_ref[...]
        mu = jnp.mean(u, axis=-1, keepdims=True)
        d = u - mu
        var = jnp.mean(d * d, axis=-1, keepdims=True)
        o = d * lax.rsqrt(var + LN_EPS) * g_ref[...] + b_ref[...]
        o_ref[...] = o
        ob_ref[...] = o.astype(ACT_DTYPE)

    row = pl.BlockSpec((tr, D), lambda i: (i, 0))
    vec = pl.BlockSpec((1, D), lambda i: (0, 0))
    return pl.pallas_call(
        body, name=name, grid=(T // tr,), in_specs=[row, row, vec, vec], out_specs=[row, row],
        out_shape=[jax.ShapeDtypeStruct((T, D), F32), jax.ShapeDtypeStruct((T, D), ACT_DTYPE)],
        compiler_params=_cparams("parallel"),
    )(x, y, g, b)


def _ln_bwd(x, y, g, dout, name):
    T, D = x.shape
    tr = _tile(T, 512)

    def body(x_ref, y_ref, g_ref, d_ref, du_ref, dub_ref, dg_ref, db_ref):
        i = pl.program_id(0)
        u = DN_ALPHA * x_ref[...] + y_ref[...]
        mu = jnp.mean(u, axis=-1, keepdims=True)
        d = u - mu
        rstd = lax.rsqrt(jnp.mean(d * d, axis=-1, keepdims=True) + LN_EPS)
        xhat = d * rstd
        dout_v = d_ref[...]
        dxh = dout_v * g_ref[...]
        m1 = jnp.mean(dxh, axis=-1, keepdims=True)
        m2 = jnp.mean(dxh * xhat, axis=-1, keepdims=True)
        du = rstd * (dxh - m1 - xhat * m2)
        du_ref[...] = du
        dub_ref[...] = du.astype(ACT_DTYPE)
        pg = jnp.sum(dout_v * xhat, axis=0, keepdims=True)
        pb = jnp.sum(dout_v, axis=0, keepdims=True)

        @pl.when(i == 0)
        def _():
            dg_ref[...] = pg
            db_ref[...] = pb

        @pl.when(i > 0)
        def _():
            dg_ref[...] += pg
            db_ref[...] += pb

    row = pl.BlockSpec((tr, D), lambda i: (i, 0))
    vec = pl.BlockSpec((1, D), lambda i: (0, 0))
    return pl.pallas_call(
        body, name=name, grid=(T // tr,), in_specs=[row, row, vec, row], out_specs=[row, row, vec, vec],
        out_shape=[jax.ShapeDtypeStruct((T, D), F32), jax.ShapeDtypeStruct((T, D), ACT_DTYPE),
                   jax.ShapeDtypeStruct((1, D), F32), jax.ShapeDtypeStruct((1, D), F32)],
        compiler_params=_cparams("arbitrary"),
    )(x, y, g, dout)


def _loss_head(y, tgt, name):
    T, D = y.shape
    tr = _tile(T, 512)

    def body(y_ref, t_ref, dy_ref, l_ref):
        i = pl.program_id(0)
        e = y_ref[...] - t_ref[...]
        dy_ref[...] = e * (1.0 / D)
        part = jnp.sum(e * e, axis=(0, 1), keepdims=True) * (0.5 / D)

        @pl.when(i == 0)
        def _():
            l_ref[...] = part

        @pl.when(i > 0)
        def _():
            l_ref[...] += part

    row = pl.BlockSpec((tr, D), lambda i: (i, 0))
    one = pl.BlockSpec((1, 1), lambda i: (0, 0))
    return pl.pallas_call(
        body, name=name, grid=(T // tr,), in_specs=[row, row], out_specs=[row, one],
        out_shape=[jax.ShapeDtypeStruct((T, D), F32), jax.ShapeDtypeStruct((1, 1), F32)],
        compiler_params=_cparams("arbitrary"),
    )(y, tgt)


def _swap_half(x):
    n = x.shape[-1]
    lane = lax.broadcasted_iota(jnp.int32, x.shape, 1)
    first = (lane % A_HEAD_DIM) < (A_HEAD_DIM // 2)
    return jnp.where(first, pltpu.roll(x, n - A_HEAD_DIM // 2, axis=1), pltpu.roll(x, A_HEAD_DIM // 2, axis=1))


def _rope(x, cos, sin_signed):
    return x * cos + _swap_half(x) * sin_signed


def _rope_t(dy, cos, sin_signed):
    return dy * cos + _swap_half(dy * sin_signed)


def _rope_tables(T):
    half = A_HEAD_DIM // 2
    inv_freq = np.float32(ROPE_THETA) ** (-np.arange(half, dtype=np.float32) / np.float32(half))
    ang = np.arange(T, dtype=np.float32)[:, None] * inv_freq[None, :]
    cos = np.tile(np.cos(ang), (1, 4))
    sin = np.sin(ang)
    sin_signed = np.tile(np.concatenate([-sin, sin], axis=1), (1, 2))
    return jnp.asarray(cos, F32), jnp.asarray(sin_signed, F32)


def _band_mask(n):
    qi = lax.broadcasted_iota(jnp.int32, (WINDOW, 2 * WINDOW), 0)
    kj = lax.broadcasted_iota(jnp.int32, (WINDOW, 2 * WINDOW), 1)
    return (kj > qi) & (kj <= qi + WINDOW) & ((n > 0) | (kj >= WINDOW))


def _place(v, src_half, dst_half):
    lane = lax.broadcasted_iota(jnp.int32, v.shape, 1)
    if src_half != dst_half:
        v = pltpu.roll(v, A_HEAD_DIM, axis=1)
    keep = (lane >= A_HEAD_DIM) if dst_half else (lane < A_HEAD_DIM)
    return jnp.where(keep, v, 0.0)


def _attn_specs():
    kb, vb = HP_K // LANES, HP_V // LANES
    prev = lambda n: jnp.maximum(n - 1, 0)
    return dict(
        q=pl.BlockSpec((WINDOW, A_Q_W), lambda n: (n, 0)),
        kc=pl.BlockSpec((WINDOW, LANES), lambda n: (n, kb)),
        kp=pl.BlockSpec((WINDOW, LANES), lambda n: (prev(n), kb)),
        vc=pl.BlockSpec((WINDOW, LANES), lambda n: (n, vb)),
        vp=pl.BlockSpec((WINDOW, LANES), lambda n: (prev(n), vb)),
        tq=pl.BlockSpec((WINDOW, LANES), lambda n: (n, 0)),
        tp=pl.BlockSpec((WINDOW, LANES), lambda n: (prev(n), 0)),
        sink=pl.BlockSpec((A_Q_HEADS, LANES), lambda n: (0, 0)),
        row512=pl.BlockSpec((WINDOW, A_Q_W), lambda n: (n, 0)),
        row128=pl.BlockSpec((WINDOW, LANES), lambda n: (n, 0)),
        lse=pl.BlockSpec((WINDOW, A_Q_HEADS), lambda n: (n, 0)),
    )


def _attn_fwd(proj, cos, sin_s, sinks_b, name):
    T = proj.shape[0]
    sp = _attn_specs()

    def body(q_ref, kc_ref, kp_ref, vc_ref, vp_ref, cq_ref, sq_ref, cp_ref, sp_ref, sink_ref, o_ref, l_ref):
        n = pl.program_id(0)
        cq, sq = cq_ref[...], sq_ref[...]
        cq4, sq4 = jnp.tile(cq, (1, A_Q_W // LANES)), jnp.tile(sq, (1, A_Q_W // LANES))
        q = _rope(q_ref[...], cq4, sq4) * (A_HEAD_DIM ** -0.5)
        kc = _rope(kc_ref[...], cq, sq)
        kp = _rope(kp_ref[...], cp_ref[...], sp_ref[...])
        kk = jnp.concatenate([kp, kc], axis=0)
        vv = jnp.concatenate([vp_ref[...], vc_ref[...]], axis=0)
        mask = _band_mask(n)
        lane = lax.broadcasted_iota(jnp.int32, (WINDOW, LANES), 1)
        outs = []
        lane8 = lax.broadcasted_iota(jnp.int32, (WINDOW, A_Q_HEADS), 1)
        lse = jnp.zeros((WINDOW, A_Q_HEADS), F32)
        for pb in range(A_Q_HEADS // 2):
            kvh = pb // 2
            q2 = q[:, pb * LANES:(pb + 1) * LANES]
            halves = []
            for e in range(2):
                hq = 2 * pb + e
                s = _dot(_place(q2, e, kvh), kk, NT)
                s = jnp.where(mask, s, NEG_BIG)
                sk = sink_ref[hq:hq + 1, 0:1]
                m = jnp.maximum(jnp.max(s, axis=-1, keepdims=True), sk)
                p = jnp.exp(s - m)
                den = jnp.sum(p, axis=-1, keepdims=True) + jnp.exp(sk - m)
                o = _dot(p * (1.0 / den), vv, NN)
                halves.append(_place(o, kvh, e))
                lse = jnp.where(lane8 == hq, m + jnp.log(den), lse)
            outs.append(jnp.where(lane < A_HEAD_DIM, halves[0], halves[1]))
        o_ref[...] = jnp.concatenate(outs, axis=1)
        l_ref[...] = lse

    return pl.pallas_call(
        body, name=name, grid=(T // WINDOW,),
        in_specs=[sp["q"], sp["kc"], sp["kp"], sp["vc"], sp["vp"], sp["tq"], sp["tq"], sp["tp"], sp["tp"], sp["sink"]],
        out_specs=[sp["row512"], sp["lse"]],
        out_shape=[jax.ShapeDtypeStruct((T, A_Q_W), F32), jax.ShapeDtypeStruct((T, A_Q_HEADS), F32)],
        compiler_params=_cparams("parallel"),
    )(proj, proj, proj, proj, proj, cos, sin_s, cos, sin_s, sinks_b)


def _attn_bwd(proj, cos, sin_s, sinks_b, o, lse, dmix, name):
    T = proj.shape[0]
    sp = _attn_specs()

    def body(q_ref, kc_ref, kp_ref, vc_ref, vp_ref, cq_ref, sq_ref, cp_ref, sp_ref, sink_ref, o_ref, l_ref, do_ref,
             dq_ref, dkc_ref, dkp_ref, dvc_ref, dvp_ref, dsink_ref):
        n = pl.program_id(0)
        cq, sq = cq_ref[...], sq_ref[...]
        cp, sps = cp_ref[...], sp_ref[...]
        cq4, sq4 = jnp.tile(cq, (1, A_Q_W // LANES)), jnp.tile(sq, (1, A_Q_W // LANES))
        q = _rope(q_ref[...], cq4, sq4) * (A_HEAD_DIM ** -0.5)
        kc = _rope(kc_ref[...], cq, sq)
        kp = _rope(kp_ref[...], cp, sps)
        kk = jnp.concatenate([kp, kc], axis=0)
        vv = jnp.concatenate([vp_ref[...], vc_ref[...]], axis=0)
        mask = _band_mask(n)
        lane = lax.broadcasted_iota(jnp.int32, (WINDOW, LANES), 1)
        do_all, o_all, l_all = do_ref[...], o_ref[...], l_ref[...]
        dkk = jnp.zeros((2 * WINDOW, LANES), F32)
        dvv = jnp.zeros((2 * WINDOW, LANES), F32)
        dqs = []
        lane8 = lax.broadcasted_iota(jnp.int32, (WINDOW, A_Q_HEADS), 1)
        head8 = lax.broadcasted_iota(jnp.int32, (1, A_Q_HEADS), 1)
        dsk = jnp.zeros((1, A_Q_HEADS), F32)
        for pb in range(A_Q_HEADS // 2):
            kvh = pb // 2
            q2 = q[:, pb * LANES:(pb + 1) * LANES]
            do2 = do_all[:, pb * LANES:(pb + 1) * LANES]
            prod = do2 * o_all[:, pb * LANES:(pb + 1) * LANES]
            halves = []
            for e in range(2):
                hq = 2 * pb + e
                in_half = (lane >= A_HEAD_DIM) if e else (lane < A_HEAD_DIM)
                delta = jnp.sum(jnp.where(in_half, prod, 0.0), axis=-1, keepdims=True)
                qe = _place(q2, e, kvh)
                doe = _place(do2, e, kvh)
                lh = jnp.sum(jnp.where(lane8 == hq, l_all, 0.0), axis=-1, keepdims=True)
                s = _dot(qe, kk, NT)
                p = jnp.where(mask, jnp.exp(jnp.where(mask, s, NEG_BIG) - lh), 0.0)
                dvv = dvv + _dot(p, doe, TN)
                dp = _dot(doe, vv, NT)
                ds = p * (dp - delta)
                dkk = dkk + _dot(ds, qe, TN)
                halves.append(_place(_dot(ds, kk, NN), kvh, e))
                p_sink = jnp.exp(sink_ref[hq:hq + 1, 0:1] - lh)
                dsk = jnp.where(head8 == hq, -jnp.sum(p_sink * delta, axis=(0, 1), keepdims=True), dsk)
            dqs.append(jnp.where(lane < A_HEAD_DIM, halves[0], halves[1]))
        dq = jnp.concatenate(dqs, axis=1) * (A_HEAD_DIM ** -0.5)
        dq_ref[...] = _rope_t(dq, cq4, sq4)
        dkp_ref[...] = _rope_t(dkk[:WINDOW], cp, sps)
        dkc_ref[...] = _rope_t(dkk[WINDOW:], cq, sq)
        dvp_ref[...] = dvv[:WINDOW]
        dvc_ref[...] = dvv[WINDOW:]

        @pl.when(n == 0)
        def _():
            dsink_ref[...] = dsk

        @pl.when(n > 0)
        def _():
            dsink_ref[...] += dsk

    return pl.pallas_call(
        body, name=name, grid=(T // WINDOW,),
        in_specs=[sp["q"], sp["kc"], sp["kp"], sp["vc"], sp["vp"], sp["tq"], sp["tq"], sp["tp"], sp["tp"], sp["sink"],
                  sp["row512"], sp["lse"], sp["row512"]],
        out_specs=[sp["row512"], sp["row128"], sp["row128"], sp["row128"], sp["row128"],
                   pl.BlockSpec((1, A_Q_HEADS), lambda n: (0, 0))],
        out_shape=[jax.ShapeDtypeStruct((T, A_Q_W), F32)] + [jax.ShapeDtypeStruct((T, LANES), F32)] * 4
        + [jax.ShapeDtypeStruct((1, A_Q_HEADS), F32)],
        compiler_params=_cparams("arbitrary"),
    )(proj, proj, proj, proj, proj, cos, sin_s, cos, sin_s, sinks_b, o, lse, dmix)


def _shift_down(x, prev8, k):
    if k == 0:
        return x
    row = lax.broadcasted_iota(jnp.int32, prev8.shape, 0)
    r = pltpu.roll(x, k, axis=0)
    top = jnp.where(row < k, pltpu.roll(prev8, k, axis=0), r[:SUBLANES])
    return jnp.concatenate([top, r[SUBLANES:]], axis=0)


def _shift_up(x, next8, k):
    if k == 0:
        return x
    R = x.shape[0]
    row = lax.broadcasted_iota(jnp.int32, next8.shape, 0)
    r = pltpu.roll(x, R - k, axis=0)
    bot = jnp.where(row >= SUBLANES - k, pltpu.roll(next8, SUBLANES - k, axis=0), r[R - SUBLANES:])
    return jnp.concatenate([r[:R - SUBLANES], bot], axis=0)


def _conv(x, prev8, w):
    y = x * w[CONV_K - 1:CONV_K]
    for j in range(CONV_K - 1):
        y = y + _shift_down(x, prev8, CONV_K - 1 - j) * w[j:j + 1]
    return y


def _conv_bwd(x, prev8, w, dy, next8_dy):
    dx = dy * w[CONV_K - 1:CONV_K]
    dws = []
    for j in range(CONV_K - 1):
        k = CONV_K - 1 - j
        dx = dx + _shift_up(dy, next8_dy, k) * w[j:j + 1]
        dws.append(jnp.sum(dy * _shift_down(x, prev8, k), axis=0, keepdims=True))
    dws.append(jnp.sum(dy * x, axis=0, keepdims=True))
    return dx, dws


def _dnconv_fwd(proj, conv_w, name):
    T = proj.shape[0]
    R = _tile(T, 512)
    cb0 = HP_QKVB // A_Q_W

    def body(x_ref, w_ref, o_ref, prev_ref):
        i = pl.program_id(1)

        @pl.when(i == 0)
        def _():
            prev_ref[...] = jnp.zeros_like(prev_ref)

        x = x_ref[...]
        o_ref[...] = _silu(_conv(x, prev_ref[...], w_ref[...]))
        prev_ref[...] = x[R - SUBLANES:]

    return pl.pallas_call(
        body, name=name, grid=(3, T // R),
        in_specs=[pl.BlockSpec((R, B_W), lambda j, i: (i, cb0 + j)), pl.BlockSpec((CONV_K, B_W), lambda j, i: (0, j))],
        out_specs=pl.BlockSpec((R, B_W), lambda j, i: (i, j)),
        out_shape=jax.ShapeDtypeStruct((T, 3 * B_W), F32),
        scratch_shapes=[pltpu.VMEM((SUBLANES, B_W), F32)],
        compiler_params=_cparams("parallel", "arbitrary"),
    )(proj, conv_w)


def _dnconv_bwd(proj, conv_w, dy, name):
    T = proj.shape[0]
    R = _tile(T, 512)
    nb = T // R
    cb0 = HP_QKVB // A_Q_W
    r8 = R // SUBLANES

    def body(x_ref, xp_ref, w_ref, dy_ref, dx_ref, dw_ref, next_ref):
        i = pl.program_id(1)
        blk = nb - 1 - i

        @pl.when(i == 0)
        def _():
            next_ref[...] = jnp.zeros_like(next_ref)

        x = x_ref[...]
        prev8 = jnp.where(blk > 0, xp_ref[...], 0.0)
        w = w_ref[...]
        dpre = dy_ref[...] * _dsilu(_conv(x, prev8, w))
        dx, dw = _conv_bwd(x, prev8, w, dpre, next_ref[...])
        dx_ref[...] = dx
        next_ref[...] = dpre[:SUBLANES]

        @pl.when(i == 0)
        def _():
            for j in range(CONV_K):
                dw_ref[j:j + 1, :] = dw[j]

        @pl.when(i > 0)
        def _():
            for j in range(CONV_K):
                dw_ref[j:j + 1, :] += dw[j]

    return pl.pallas_call(
        body, name=name, grid=(3, nb),
        in_specs=[pl.BlockSpec((R, B_W), lambda j, i: (nb - 1 - i, cb0 + j)),
                  pl.BlockSpec((SUBLANES, B_W), lambda j, i: (jnp.maximum((nb - 1 - i) * r8 - 1, 0), cb0 + j)),
                  pl.BlockSpec((CONV_K, B_W), lambda j, i: (0, j)),
                  pl.BlockSpec((R, B_W), lambda j, i: (nb - 1 - i, j))],
        out_specs=[pl.BlockSpec((R, B_W), lambda j, i: (nb - 1 - i, j)),
                   pl.BlockSpec((CONV_K, B_W), lambda j, i: (0, j))],
        out_shape=[jax.ShapeDtypeStruct((T, 3 * B_W), F32), jax.ShapeDtypeStruct((CONV_K, 3 * B_W), F32)],
        scratch_shapes=[pltpu.VMEM((SUBLANES, B_W), F32)],
        compiler_params=_cparams("parallel", "arbitrary"),
    )(proj, proj, conv_w, dy)


DK_SCALE = B_HEAD_DIM ** -0.5


BNN = (((2,), (1,)), ((0,), (0,)))
BNT = (((2,), (2,)), ((0,), (0,)))
BTN = (((1,), (1,)), ((0,), (0,)))


def _tri_inv(a):
    C = a.shape[-1]
    ri = lax.broadcasted_iota(jnp.int32, (C, C), 0)
    ci = lax.broadcasted_iota(jnp.int32, (C, C), 1)
    x = jnp.where(ri == ci, 1.0, 0.0)[None] - a
    p = _dotf(a, a, BNN)
    span = 2
    while span < C:
        x = x + _dotf(x, p, BNN)
        span *= 2
        if span < C:
            p = _dotf(p, p, BNN)
    return x


def _dn_chunk(qc, kc, v, gcol, grow, bcol, s0):
    C = B_CHUNK
    ri = lax.broadcasted_iota(jnp.int32, (C, C), 0)
    ci = lax.broadcasted_iota(jnp.int32, (C, C), 1)
    incl, strict = (ri >= ci)[None], (ri > ci)[None]
    rq = lax.rsqrt(jnp.sum(qc * qc, axis=-1, keepdims=True) + NORM_EPS)
    rk = lax.rsqrt(jnp.sum(kc * kc, axis=-1, keepdims=True) + NORM_EPS)
    qn = qc * rq
    q = qn * DK_SCALE
    k = kc * rk
    gc_col = jnp.sum(jnp.where(incl, grow, 0.0), axis=2, keepdims=True)
    gc_row = jnp.sum(jnp.where((ri <= ci)[None], gcol, 0.0), axis=1, keepdims=True)
    gl = jnp.sum(gcol, axis=1, keepdims=True)
    dincl = jnp.where(incl, jnp.exp(jnp.where(incl, gc_col - gc_row, 0.0)), 0.0)
    dstrict = jnp.where(strict, dincl, 0.0)
    eg = jnp.exp(gc_col)
    ekt = jnp.exp(gl - gc_col)
    egl = jnp.exp(gl)
    kb = k * bcol
    vb = v * bcol
    kbg = kb * eg
    a = _dot(kb, k, BNT) * dstrict
    tm = _tri_inv(a)
    u = _dot(tm, vb, BNN)
    w = _dot(tm, kbg, BNN)
    vn = u - _dot(w, s0, BNN)
    qk = _dot(q, k, BNT) * dincl
    qg = q * eg
    kt = k * ekt
    o = _dot(qg, s0, BNN) + _dot(qk, vn, BNN)
    s1 = s0 * egl + _dot(kt, vn, BTN)
    return dict(rq=rq, rk=rk, qn=qn, q=q, k=k, dincl=dincl, dstrict=dstrict, eg=eg, ekt=ekt, egl=egl, kb=kb, vb=vb,
                kbg=kbg, a=a, tm=tm, w=w, vn=vn, qk=qk, qg=qg, kt=kt, o=o, s1=s1, ri=ri[None], ci=ci[None])


def _heads(ref):
    return jnp.stack([ref[:, h * B_HEAD_DIM:(h + 1) * B_HEAD_DIM] for h in range(B_HEADS)])


def _store_heads(ref, val):
    for h in range(B_HEADS):
        ref[:, h * B_HEAD_DIM:(h + 1) * B_HEAD_DIM] = val[h]


def _dn_specs(N, rev):
    ix = (lambda n: N - 1 - n) if rev else (lambda n: n)
    wide = lambda cb: pl.BlockSpec((B_CHUNK, B_W), lambda n: (ix(n), cb))
    return dict(
        q=wide(0), k=wide(1), v=wide(2), z=wide(HP_Z // B_W), dob=wide(A_Q_W // B_W), out=wide(0),
        nw=pl.BlockSpec((1, LANES), lambda n: (0, 0)),
        row=pl.BlockSpec((B_HEADS, None, 1, B_CHUNK), lambda n: (0, ix(n), 0, 0)),
        state=pl.BlockSpec((B_HEADS, None, B_HEAD_DIM, B_HEAD_DIM), lambda n: (0, ix(n), 0, 0)),
    )


def _to_col(row):
    C = row.shape[-1]
    eye = lax.broadcasted_iota(jnp.int32, (C, C), 0) == lax.broadcasted_iota(jnp.int32, (C, C), 1)
    return jnp.sum(jnp.where(eye[None], row, 0.0), axis=2, keepdims=True)


def _to_row(col):
    C = col.shape[1]
    eye = lax.broadcasted_iota(jnp.int32, (C, C), 0) == lax.broadcasted_iota(jnp.int32, (C, C), 1)
    return jnp.sum(jnp.where(eye[None], col, 0.0), axis=1, keepdims=True)


def _dn_fwd(qkvc, proj, norm_w, grow, brow, name):
    T = qkvc.shape[0]
    N = T // B_CHUNK
    sp = _dn_specs(N, False)

    def body(q_ref, k_ref, v_ref, z_ref, nw_ref, gr_ref, br_ref, o_ref, st_ref, s_ref):
        n = pl.program_id(0)

        @pl.when(n == 0)
        def _():
            s_ref[...] = jnp.zeros_like(s_ref)

        s0 = s_ref[...]
        st_ref[...] = s0
        grow_v = gr_ref[...]
        f = _dn_chunk(_heads(q_ref), _heads(k_ref), _heads(v_ref), _to_col(grow_v), grow_v, _to_col(br_ref[...]), s0)
        o = f["o"]
        r = lax.rsqrt(jnp.mean(o * o, axis=-1, keepdims=True) + NORM_EPS)
        _store_heads(o_ref, o * r * nw_ref[...][None] * _silu(_heads(z_ref)))
        s_ref[...] = f["s1"]

    return pl.pallas_call(
        body, name=name, grid=(N,),
        in_specs=[sp["q"], sp["k"], sp["v"], sp["z"], sp["nw"], sp["row"], sp["row"]],
        out_specs=[sp["out"], sp["state"]],
        out_shape=[jax.ShapeDtypeStruct((T, B_W), F32),
                   jax.ShapeDtypeStruct((B_HEADS, N, B_HEAD_DIM, B_HEAD_DIM), F32)],
        scratch_shapes=[pltpu.VMEM((B_HEADS, B_HEAD_DIM, B_HEAD_DIM), F32)],
        compiler_params=_cparams("arbitrary"),
    )(qkvc, qkvc, qkvc, proj, norm_w, grow, brow)


def _dn_bwd(qkvc, proj, norm_w, grow, brow, states, dmix, name):
    T = qkvc.shape[0]
    N = T // B_CHUNK
    sp = _dn_specs(N, True)
    C = B_CHUNK

    def body(q_ref, k_ref, v_ref, z_ref, nw_ref, gr_ref, br_ref, st_ref, dob_ref,
             dq_ref, dk_ref, dv_ref, dz_ref, dg_ref, db_ref, dnw_ref, ds_ref):
        n = pl.program_id(0)

        @pl.when(n == 0)
        def _():
            ds_ref[...] = jnp.zeros_like(ds_ref)
            dnw_ref[...] = jnp.zeros_like(dnw_ref)

        s0 = st_ref[...]
        ds1 = ds_ref[...]
        v, z, nw, bcol_v = _heads(v_ref), _heads(z_ref), nw_ref[...][None], _to_col(br_ref[...])
        grow_v = gr_ref[...]
        f = _dn_chunk(_heads(q_ref), _heads(k_ref), v, _to_col(grow_v), grow_v, bcol_v, s0)
        o, q, k, qn = f["o"], f["q"], f["k"], f["qn"]
        eg, ekt, egl = f["eg"], f["ekt"], f["egl"]
        tm, w, vn, kb, vb, kbg = f["tm"], f["w"], f["vn"], f["kb"], f["vb"], f["kbg"]
        qg, kt, qk, a = f["qg"], f["kt"], f["qk"], f["a"]
        ri, ci = f["ri"], f["ci"]

        dob_v = _heads(dob_ref)
        r = lax.rsqrt(jnp.mean(o * o, axis=-1, keepdims=True) + NORM_EPS)
        sz = _silu(z)
        on = o * r
        dnw_ref[...] += jnp.sum(dob_v * sz * on, axis=1, keepdims=True)
        _store_heads(dz_ref, dob_v * on * nw * _dsilu(z))
        d_on = dob_v * sz * nw
        do = r * (d_on - on * jnp.mean(d_on * on, axis=-1, keepdims=True))

        dvn = _dot(qk, do, BTN) + _dot(kt, ds1, BNN)
        dqk = _dot(do, vn, BNT)
        dqg = _dot(do, s0, BNT)
        ds_ref[...] = _dot(qg, do, BTN) + egl * ds1 - _dot(w, dvn, BTN)
        dgl = jnp.sum(s0 * ds1, axis=(1, 2), keepdims=True) * egl
        dkt = _dot(vn, ds1, BNT)
        dw = -_dot(dvn, s0, BNT)
        dq = dqg * eg
        dgc = jnp.sum(dqg * qg, axis=-1, keepdims=True)
        dk = dkt * ekt
        t_kt = jnp.sum(dkt * kt, axis=-1, keepdims=True)
        dgl = dgl + jnp.sum(t_kt, axis=1, keepdims=True)
        dgc = dgc - t_kt
        dqkr = dqk * f["dincl"]
        dq = dq + _dot(dqkr, k, BNN)
        dk = dk + _dot(dqkr, q, BTN)
        e_qk = dqk * qk
        dgc = dgc + jnp.sum(e_qk, axis=-1, keepdims=True)
        dgc_row = -jnp.sum(e_qk, axis=1, keepdims=True)
        dtm = _dot(dvn, vb, BNT) + _dot(dw, kbg, BNT)
        dvb = _dot(tm, dvn, BTN)
        dkbg = _dot(tm, dw, BTN)
        dkb = dkbg * eg
        dgc = dgc + jnp.sum(dkbg * kbg, axis=-1, keepdims=True)
        da = -_dotf(tm, _dotf(dtm, tm, BNT), BTN)
        dkk = da * f["dstrict"]
        e_a = da * a
        dgc = dgc + jnp.sum(e_a, axis=-1, keepdims=True)
        dgc_row = dgc_row - jnp.sum(e_a, axis=1, keepdims=True)
        dkb = dkb + _dot(dkk, k, BNN)
        dk = dk + _dot(dkk, kb, BTN)
        dk = dk + dkb * bcol_v
        db_ref[...] = _to_row(jnp.sum(dkb * k, axis=-1, keepdims=True) + jnp.sum(dvb * v, axis=-1, keepdims=True))
        _store_heads(dv_ref, dvb * bcol_v)
        dgc_row = dgc_row + jnp.sum(jnp.where(ri == ci, dgc, 0.0), axis=1, keepdims=True)
        dg_ref[...] = jnp.sum(jnp.where(ci <= ri, _to_col(dgc_row), 0.0), axis=1, keepdims=True) + dgl
        dqs = dq * DK_SCALE
        _store_heads(dq_ref, f["rq"] * (dqs - qn * jnp.sum(dqs * qn, axis=-1, keepdims=True)))
        _store_heads(dk_ref, f["rk"] * (dk - k * jnp.sum(dk * k, axis=-1, keepdims=True)))

    return pl.pallas_call(
        body, name=name, grid=(N,),
        in_specs=[sp["q"], sp["k"], sp["v"], sp["z"], sp["nw"], sp["row"], sp["row"], sp["state"], sp["dob"]],
        out_specs=[sp["out"], sp["out"], sp["out"], sp["out"], sp["row"], sp["row"],
                   pl.BlockSpec((B_HEADS, 1, LANES), lambda n: (0, 0, 0))],
        out_shape=[jax.ShapeDtypeStruct((T, B_W), F32)] * 4
        + [jax.ShapeDtypeStruct((B_HEADS, N, 1, C), F32), jax.ShapeDtypeStruct((B_HEADS, N, 1, C), F32),
           jax.ShapeDtypeStruct((B_HEADS, 1, LANES), F32)],
        scratch_shapes=[pltpu.VMEM((B_HEADS, B_HEAD_DIM, B_HEAD_DIM), F32)],
        compiler_params=_cparams("arbitrary"),
    )(qkvc, qkvc, qkvc, proj, norm_w, grow, brow, states, dmix)


def _lru_gates(xc, wa_ref, wx_ref, ba, bx, sp):
    pre_r, pre_i = [], []
    for hb in range(LRU_BLOCKS):
        xb = xc[:, hb * LRU_BLOCK_W:(hb + 1) * LRU_BLOCK_W]
        pre_r.append(_dot(xb, wa_ref[hb]))
        pre_i.append(_dot(xb, wx_ref[hb]))
    r = _sigmoid(jnp.concatenate(pre_r, axis=1) + ba)
    i = _sigmoid(jnp.concatenate(pre_i, axis=1) + bx)
    la = -LRU_C * r * sp
    a = jnp.exp(la)
    th = jnp.tanh(la)
    s = jnp.sqrt(-2.0 * th / (1.0 - th))
    return r, i, a, s


def _scan_down(a, b):
    R = a.shape[0]
    row = lax.broadcasted_iota(jnp.int32, a.shape, 0)
    d = 1
    while d < R:
        ok = row >= d
        b = a * jnp.where(ok, pltpu.roll(b, d, axis=0), 0.0) + b
        a = a * jnp.where(ok, pltpu.roll(a, d, axis=0), 1.0)
        d *= 2
    return a, b


def _scan_up(a, b):
    R = a.shape[0]
    row = lax.broadcasted_iota(jnp.int32, a.shape, 0)
    d = 1
    while d < R:
        ok = row < R - d
        b = a * jnp.where(ok, pltpu.roll(b, R - d, axis=0), 0.0) + b
        a = a * jnp.where(ok, pltpu.roll(a, R - d, axis=0), 1.0)
        d *= 2
    return b


def _rglru_fwd(proj, conv_w, conv_b, wa, wx, ba, bx, sp, name):
    T = proj.shape[0]
    R = _tile(T, 256)
    W = D_MODEL

    def body(p_ref, cw_ref, cb_ref, wa_ref, wx_ref, ba_ref, bx_ref, sp_ref, hg_ref, h_ref, prev_ref, hc_ref):
        i = pl.program_id(0)

        @pl.when(i == 0)
        def _():
            prev_ref[...] = jnp.zeros_like(prev_ref)
            hc_ref[...] = jnp.zeros_like(hc_ref)

        xr = p_ref[:, :W]
        gate = p_ref[:, W:]
        xc = _conv(xr, prev_ref[...], cw_ref[...]) + cb_ref[...]
        prev_ref[...] = xr[R - SUBLANES:]
        r, ig, a, s = _lru_gates(xc, wa_ref, wx_ref, ba_ref[...], bx_ref[...], sp_ref[...])
        pa, hb = _scan_down(a, s * ig * xc)
        h = hb + pa * hc_ref[SUBLANES - 1:SUBLANES, :]
        h_ref[...] = h
        hg_ref[...] = (h * _gelu(gate)).astype(ACT_DTYPE)
        hc_ref[...] = h[R - SUBLANES:]

    vec = pl.BlockSpec((1, W), lambda i: (0, 0))
    wsp = pl.BlockSpec((LRU_BLOCKS, LRU_BLOCK_W, LRU_BLOCK_W), lambda i: (0, 0, 0))
    row = pl.BlockSpec((R, W), lambda i: (i, 0))
    return pl.pallas_call(
        body, name=name, grid=(T // R,),
        in_specs=[pl.BlockSpec((R, 2 * W), lambda i: (i, 0)), pl.BlockSpec((CONV_K, W), lambda i: (0, 0)),
                  vec, wsp, wsp, vec, vec, vec],
        out_specs=[row, row],
        out_shape=[jax.ShapeDtypeStruct((T, W), ACT_DTYPE), jax.ShapeDtypeStruct((T, W), F32)],
        scratch_shapes=[pltpu.VMEM((SUBLANES, W), F32), pltpu.VMEM((SUBLANES, W), F32)],
        compiler_params=_cparams("arbitrary"),
    )(proj, conv_w, conv_b, wa, wx, ba, bx, sp)


def _rglru_bwd(proj, conv_w, conv_b, wa, wx, ba, bx, sp, h, dhg, name):
    T = proj.shape[0]
    R = _tile(T, 256)
    nb = T // R
    r8 = R // SUBLANES
    W = D_MODEL

    def body(p_ref, pp_ref, cw_ref, cb_ref, wa_ref, wx_ref, ba_ref, bx_ref, sp_ref, h_ref, hp_ref, dhg_ref,
             dp_ref, dcw_ref, dcb_ref, dwa_ref, dwx_ref, dba_ref, dbx_ref, dsp_ref, lam_ref, nxt_ref):
        step = pl.program_id(0)
        blk = nb - 1 - step

        @pl.when(step == 0)
        def _():
            lam_ref[...] = jnp.zeros_like(lam_ref)
            nxt_ref[...] = jnp.zeros_like(nxt_ref)

        xr = p_ref[:, :W]
        gate = p_ref[:, W:]
        first = blk > 0
        prev8 = jnp.where(first, pp_ref[:, :W], 0.0)
        hprev8 = jnp.where(first, hp_ref[...], 0.0)
        cw = cw_ref[...]
        spv = sp_ref[...]
        xc = _conv(xr, prev8, cw) + cb_ref[...]
        r, ig, a, s = _lru_gates(xc, wa_ref, wx_ref, ba_ref[...], bx_ref[...], spv)
        hv = h_ref[...]
        dhg_v = dhg_ref[...]
        dgate = dhg_v * hv * _dgelu(gate)
        dh = dhg_v * _gelu(gate)
        row = lax.broadcasted_iota(jnp.int32, (R, W), 0)
        last = row == R - 1
        a_up = jnp.where(last, 0.0, pltpu.roll(a, R - 1, axis=0))
        lam = _scan_up(a_up, dh + jnp.where(last, lam_ref[0:1, :], 0.0))
        lam_ref[...] = (a * lam)[:SUBLANES]
        h_dn = _shift_down(hv, hprev8, 1)
        da = lam * h_dn
        bx_in = ig * xc
        dsv = lam * bx_in
        dig = lam * s * xc
        dxc = lam * s * ig
        dla = da * a - dsv * (a * a) / s
        dr = dla * (-LRU_C) * spv
        dsp = jnp.sum(dla * (-LRU_C) * r, axis=0, keepdims=True)
        dpr = dr * r * (1.0 - r)
        dpi = dig * ig * (1.0 - ig)
        dxc_parts, dwa_parts, dwx_parts = [], [], []
        for hb in range(LRU_BLOCKS):
            sl = slice(hb * LRU_BLOCK_W, (hb + 1) * LRU_BLOCK_W)
            xb, gr, gi = xc[:, sl], dpr[:, sl], dpi[:, sl]
            dxc_parts.append(_dot(gr, wa_ref[hb], NT) + _dot(gi, wx_ref[hb], NT))
            dwa_parts.append(_dot(xb, gr, TN))
            dwx_parts.append(_dot(xb, gi, TN))
        dxc = dxc + jnp.concatenate(dxc_parts, axis=1)
        dxr, dcw = _conv_bwd(xr, prev8, cw, dxc, nxt_ref[...])
        nxt_ref[...] = dxc[:SUBLANES]
        dp_ref[:, :W] = dxr.astype(ACT_DTYPE)
        dp_ref[:, W:] = dgate.astype(ACT_DTYPE)
        dcb = jnp.sum(dxc, axis=0, keepdims=True)
        dba = jnp.sum(dpr, axis=0, keepdims=True)
        dbx = jnp.sum(dpi, axis=0, keepdims=True)

        @pl.when(step == 0)
        def _():
            for j in range(CONV_K):
                dcw_ref[j:j + 1, :] = dcw[j]
            dcb_ref[...] = dcb
            dba_ref[...] = dba
            dbx_ref[...] = dbx
            dsp_ref[...] = dsp
            for hb in range(LRU_BLOCKS):
                dwa_ref[hb] = dwa_parts[hb]
                dwx_ref[hb] = dwx_parts[hb]

        @pl.when(step > 0)
        def _():
            for j in range(CONV_K):
                dcw_ref[j:j + 1, :] += dcw[j]
            dcb_ref[...] += dcb
            dba_ref[...] += dba
            dbx_ref[...] += dbx
            dsp_ref[...] += dsp
            for hb in range(LRU_BLOCKS):
                dwa_ref[hb] += dwa_parts[hb]
                dwx_ref[hb] += dwx_parts[hb]

    rv = lambda i: nb - 1 - i
    before = lambda i: jnp.maximum((nb - 1 - i) * r8 - 1, 0)
    vec = pl.BlockSpec((1, W), lambda i: (0, 0))
    cws = pl.BlockSpec((CONV_K, W), lambda i: (0, 0))
    wsp = pl.BlockSpec((LRU_BLOCKS, LRU_BLOCK_W, LRU_BLOCK_W), lambda i: (0, 0, 0))
    row = pl.BlockSpec((R, W), lambda i: (rv(i), 0))
    wshape = jax.ShapeDtypeStruct((LRU_BLOCKS, LRU_BLOCK_W, LRU_BLOCK_W), F32)
    vshape = jax.ShapeDtypeStruct((1, W), F32)
    return pl.pallas_call(
        body, name=name, grid=(nb,),
        in_specs=[pl.BlockSpec((R, 2 * W), lambda i: (rv(i), 0)), pl.BlockSpec((SUBLANES, 2 * W), lambda i: (before(i), 0)),
                  cws, vec, wsp, wsp, vec, vec, vec, row, pl.BlockSpec((SUBLANES, W), lambda i: (before(i), 0)), row],
        out_specs=[pl.BlockSpec((R, 2 * W), lambda i: (rv(i), 0)), cws, vec, wsp, wsp, vec, vec, vec],
        out_shape=[jax.ShapeDtypeStruct((T, 2 * W), ACT_DTYPE), jax.ShapeDtypeStruct((CONV_K, W), F32), vshape,
                   wshape, wshape, vshape, vshape, vshape],
        scratch_shapes=[pltpu.VMEM((SUBLANES, W), F32), pltpu.VMEM((SUBLANES, W), F32)],
        compiler_params=_cparams("arbitrary"),
    )(proj, proj, conv_w, conv_b, wa, wx, ba, bx, sp, h, h, dhg)


MESH = pl.DeviceIdType.MESH
ANY = pl.BlockSpec(memory_space=pl.ANY)


def _position():
    x, y, c = lax.axis_index("x"), lax.axis_index("y"), lax.axis_index("c")
    other_chips = [(1 - x, y), (x, 1 - y), (1 - x, 1 - y)]
    return x, y, c, other_chips


def _all_gather_weights(shards, name):
    n = len(shards)

    def body(*refs):
        ins, outs = refs[:n], refs[n:2 * n]
        send_sems, recv_sems = refs[2 * n:]
        x, y, c, chips = _position()
        me = 2 * x + y
        sibling = (x, y, 1 - c)

        def rcopy(t, k, src, dst, to):
            return pltpu.make_async_remote_copy(src_ref=src, dst_ref=dst, send_sem=send_sems.at[t, k],
                                                recv_sem=recv_sems.at[t, k], device_id=to, device_id_type=MESH)

        started = []
        for t in range(n):
            for j, (cx, cy) in enumerate(chips):
                cp = rcopy(t, j, ins[t].at[c], outs[t].at[me, c], (cx, cy, c))
                cp.start()
                started.append(cp)
        for t in range(n):
            for j, (cx, cy) in enumerate(chips):
                blk = outs[t].at[2 * cx + cy, c]
                rcopy(t, j, blk, blk, (cx, cy, c)).wait_recv()
                cp = rcopy(t, 3 + j, blk, blk, sibling)
                cp.start()
                started.append(cp)
        for t in range(n):
            for j, (cx, cy) in enumerate(chips):
                blk = outs[t].at[2 * cx + cy, 1 - c]
                rcopy(t, 3 + j, blk, blk, sibling).wait_recv()
        for cp in started:
            cp.wait_send()

    return pl.pallas_call(
        body, name=name, in_specs=[ANY] * n, out_specs=[ANY] * n,
        out_shape=[jax.ShapeDtypeStruct((N_CHIPS,) + s.shape, s.dtype) for s in shards],
        scratch_shapes=[pltpu.SemaphoreType.DMA((n, 6)), pltpu.SemaphoreType.DMA((n, 6))],
    )(*shards)


HBM = pl.BlockSpec(memory_space=pltpu.HBM)
SEM = pl.BlockSpec(memory_space=pltpu.SEMAPHORE)
EFFECT = pltpu.SideEffectType.DATAFLOW_SIDE_EFFECTING


def _gather_start(own, land, after, name):
    n = len(own)
    L = own[0].shape[0]
    n_sem = n * L

    def body(*refs):
        own_refs, land_refs = refs[:n], refs[n:2 * n]
        send_sems = refs[2 * n + 1:2 * n + 1 + n_sem]
        recv_sems = refs[2 * n + 1 + n_sem:2 * n + 1 + 2 * n_sem]
        x, y, c, chips = _position()
        me = 2 * x + y
        for layer in range(L):
            for t in range(n):
                for cx, cy in chips:
                    pltpu.make_async_remote_copy(
                        src_ref=own_refs[t].at[layer], dst_ref=land_refs[t].at[me, layer],
                        send_sem=send_sems[t * L + layer], recv_sem=recv_sems[t * L + layer],
                        device_id=(cx, cy, c), device_id_type=MESH).start()

    sems = (pltpu.SemaphoreType.DMA(()),) * (2 * n_sem)
    thru = [pltpu.HBM(a.shape, a.dtype) for a in list(own) + list(land)]
    out = pl.pallas_call(
        body, name=name, out_shape=(*sems, *thru),
        in_specs=[HBM] * (2 * n) + [pl.BlockSpec(memory_space=pl.ANY)], out_specs=(SEM,) * (2 * n_sem) + (HBM,) * (2 * n),
        input_output_aliases={i: 2 * n_sem + i for i in range(2 * n)},
        compiler_params=pltpu.CompilerParams(has_side_effects=EFFECT),
    )(*[pltpu.with_memory_space_constraint(a, pltpu.HBM) for a in list(own) + list(land)], after)
    send_sems, recv_sems, thru = out[:n_sem], out[n_sem:2 * n_sem], out[2 * n_sem:]
    return ([send_sems[t * L:(t + 1) * L] for t in range(n)], [recv_sems[t * L:(t + 1) * L] for t in range(n)],
            list(thru[:n]), list(thru[n:]))


def _gather_wait(send_sems, recv_sems, own, land, layer, after, name):
    n = len(own)

    def body(*refs):
        land_refs = refs[n:2 * n]
        s_sems, r_sems = refs[2 * n:3 * n], refs[3 * n:4 * n]
        x, y, c, _ = _position()
        for t in range(n):
            three = land_refs[t].at[pl.ds(0, N_CHIPS - 1), layer]
            cp = pltpu.make_async_remote_copy(src_ref=three, dst_ref=three, send_sem=s_sems[t], recv_sem=r_sems[t],
                                              device_id=(x, y, c), device_id_type=MESH)
            cp.wait_send()
            cp.wait_recv()

    thru = [pltpu.HBM(a.shape, a.dtype) for a in list(own) + list(land)]
    out = pl.pallas_call(
        body, name=name, out_shape=tuple(thru),
        in_specs=[HBM] * (2 * n) + [SEM] * (2 * n) + [pl.BlockSpec(memory_space=pl.ANY)], out_specs=(HBM,) * (2 * n),
        input_output_aliases={i: i for i in range(2 * n)},
        compiler_params=pltpu.CompilerParams(has_side_effects=EFFECT),
    )(*own, *land, *[s[layer] for s in send_sems], *[r[layer] for r in recv_sems], after)
    return list(out[:n]), list(out[n:])


def _rs_to_sibling(grads, name):
    n = len(grads)

    def body(*refs):
        ins, outs = refs[:n], refs[n:2 * n]
        send_sems, recv_sems = refs[2 * n:]
        x, y, c, _ = _position()
        cps = [pltpu.make_async_remote_copy(src_ref=ins[t].at[:, 1 - c], dst_ref=outs[t], send_sem=send_sems.at[t],
                                            recv_sem=recv_sems.at[t], device_id=(x, y, 1 - c), device_id_type=MESH)
               for t in range(n)]
        for cp in cps:
            cp.start()
        for cp in cps:
            cp.wait()

    return pl.pallas_call(
        body, name=name, in_specs=[ANY] * n, out_specs=[ANY] * n,
        out_shape=[jax.ShapeDtypeStruct((N_CHIPS,) + g.shape[2:], g.dtype) for g in grads],
        scratch_shapes=[pltpu.SemaphoreType.DMA((n,)), pltpu.SemaphoreType.DMA((n,))],
    )(*grads)


def _rs_across_chips(parts, name):
    n = len(parts)

    def body(*refs):
        ins, outs = refs[:n], refs[n:2 * n]
        send_sems, recv_sems = refs[2 * n:]
        x, y, c, chips = _position()
        me = 2 * x + y
        cps = []
        for t in range(n):
            for j, (cx, cy) in enumerate(chips):
                cps.append(pltpu.make_async_remote_copy(
                    src_ref=ins[t].at[2 * cx + cy], dst_ref=outs[t].at[me], send_sem=send_sems.at[t, j],
                    recv_sem=recv_sems.at[t, j], device_id=(cx, cy, c), device_id_type=MESH))
        for cp in cps:
            cp.start()
        for t in range(n):
            for j, (cx, cy) in enumerate(chips):
                blk = outs[t].at[2 * cx + cy]
                pltpu.make_async_remote_copy(src_ref=blk, dst_ref=blk, send_sem=send_sems.at[t, j],
                                             recv_sem=recv_sems.at[t, j], device_id=(cx, cy, c),
                                             device_id_type=MESH).wait_recv()
        for cp in cps:
            cp.wait_send()

    return pl.pallas_call(
        body, name=name, in_specs=[ANY] * n, out_specs=[ANY] * n,
        out_shape=[jax.ShapeDtypeStruct(p.shape, p.dtype) for p in parts],
        scratch_shapes=[pltpu.SemaphoreType.DMA((n, 3)), pltpu.SemaphoreType.DMA((n, 3))],
    )(*parts)


def _rs_across_start(parts, recv, name):
    n = len(parts)

    def body(*refs):
        part_refs, recv_refs = refs[:n], refs[n:2 * n]
        send_sems, recv_sems = refs[2 * n:3 * n], refs[3 * n:4 * n]
        token_ref = refs[-1]
        x, y, c, chips = _position()
        me = 2 * x + y
        for t in range(n):
            for cx, cy in chips:
                pltpu.make_async_remote_copy(src_ref=part_refs[t].at[2 * cx + cy], dst_ref=recv_refs[t].at[me],
                                             send_sem=send_sems[t], recv_sem=recv_sems[t], device_id=(cx, cy, c),
                                             device_id_type=MESH).start()
        token_ref[...] = jnp.zeros_like(token_ref)

    sems = (pltpu.SemaphoreType.DMA(()),) * (2 * n)
    thru = [pltpu.HBM(a.shape, a.dtype) for a in list(parts) + list(recv)]
    out = pl.pallas_call(
        body, name=name, out_shape=(*sems, *thru, jax.ShapeDtypeStruct((SUBLANES, LANES), F32)),
        in_specs=[HBM] * (2 * n), out_specs=(SEM,) * (2 * n) + (HBM,) * (2 * n) + (pl.BlockSpec(memory_space=pltpu.VMEM),),
        input_output_aliases={i: 2 * n + i for i in range(2 * n)},
        compiler_params=pltpu.CompilerParams(has_side_effects=EFFECT),
    )(*[pltpu.with_memory_space_constraint(a, pltpu.HBM) for a in list(parts) + list(recv)])
    return out[:n], out[n:2 * n], list(out[2 * n:3 * n]), list(out[3 * n:4 * n]), out[-1]


def _rs_across_wait(send_sems, recv_sems, parts, recv, after, name):
    n = len(parts)

    def body(*refs):
        recv_refs = refs[n:2 * n]
        s_sems, r_sems = refs[2 * n:3 * n], refs[3 * n:4 * n]
        x, y, c, _ = _position()
        for t in range(n):
            three = recv_refs[t].at[pl.ds(0, N_CHIPS - 1)]
            cp = pltpu.make_async_remote_copy(src_ref=three, dst_ref=three, send_sem=s_sems[t], recv_sem=r_sems[t],
                                              device_id=(x, y, c), device_id_type=MESH)
            cp.wait_send()
            cp.wait_recv()

    thru = [pltpu.HBM(a.shape, a.dtype) for a in list(parts) + list(recv)]
    out = pl.pallas_call(
        body, name=name, out_shape=tuple(thru),
        in_specs=[HBM] * (2 * n) + [SEM] * (2 * n) + [pl.BlockSpec(memory_space=pl.ANY)], out_specs=(HBM,) * (2 * n),
        input_output_aliases={i: i for i in range(2 * n)},
        compiler_params=pltpu.CompilerParams(has_side_effects=EFFECT),
    )(*parts, *recv, *send_sems, *recv_sems, after)
    return list(out[:n]), list(out[n:])


def _rs_join_halves(halves, name):
    n = len(halves)

    def body(*refs):
        ins, outs = refs[:n], refs[n:2 * n]
        send_sems, recv_sems = refs[2 * n:]
        x, y, c, _ = _position()
        cps = [pltpu.make_async_remote_copy(src_ref=ins[t].at[c], dst_ref=outs[t].at[c], send_sem=send_sems.at[t],
                                            recv_sem=recv_sems.at[t], device_id=(x, y, 1 - c), device_id_type=MESH)
               for t in range(n)]
        for cp in cps:
            cp.start()
        for t in range(n):
            blk = outs[t].at[1 - c]
            pltpu.make_async_remote_copy(src_ref=blk, dst_ref=blk, send_sem=send_sems.at[t], recv_sem=recv_sems.at[t],
                                         device_id=(x, y, 1 - c), device_id_type=MESH).wait_recv()
        for cp in cps:
            cp.wait_send()

    return pl.pallas_call(
        body, name=name, in_specs=[ANY] * n, out_specs=[ANY] * n,
        out_shape=[jax.ShapeDtypeStruct(h.shape, h.dtype) for h in halves],
        input_output_aliases={t: t for t in range(n)},
        scratch_shapes=[pltpu.SemaphoreType.DMA((n,)), pltpu.SemaphoreType.DMA((n,))],
    )(*halves)


def _all_gather_small(block, name):
    m_per, n = block.shape

    def body(x_ref, out_ref, send_sems, recv_sems, local_sem):
        x, y, c, chips = _position()
        me, sibling = (x, y, c), (x, y, 1 - c)

        def rows(px, py, pc):
            return out_ref.at[pl.ds((4 * px + 2 * py + pc) * m_per, m_per), :]

        def copy(k, blk, to, src=None):
            return pltpu.make_async_remote_copy(
                src_ref=rows(*blk) if src is None else src, dst_ref=rows(*blk), send_sem=send_sems.at[k],
                recv_sem=recv_sems.at[k], device_id=to, device_id_type=MESH)

        mine = pltpu.make_async_copy(x_ref, rows(*me), local_sem)
        mine.start()
        first = [copy(0, me, sibling, src=x_ref)]
        first += [copy(1 + j, me, (*chip, c), src=x_ref) for j, chip in enumerate(chips)]
        for cp in first:
            cp.start()
        passed = [copy(4 + j, (*chip, c), sibling) for j, chip in enumerate(chips)]
        for j, chip in enumerate(chips):
            copy(1 + j, (*chip, c), me).wait_recv()
            passed[j].start()
        copy(0, sibling, me).wait_recv()
        for j, chip in enumerate(chips):
            copy(4 + j, (*chip, 1 - c), me).wait_recv()
        for cp in first + passed:
            cp.wait_send()
        mine.wait()

    return pl.pallas_call(
        body, name=name, out_shape=jax.ShapeDtypeStruct((N_DEV * m_per, n), block.dtype),
        in_specs=[pl.BlockSpec(memory_space=pltpu.VMEM)], out_specs=pl.BlockSpec(memory_space=pltpu.VMEM),
        scratch_shapes=[pltpu.SemaphoreType.DMA((7,)), pltpu.SemaphoreType.DMA((7,)), pltpu.SemaphoreType.DMA],
    )(block)


def _row_tile(R, n):
    budget = 1 << 19
    if R * n <= budget or R % SUBLANES:
        return R
    t = R
    while t * n > budget and t % (2 * SUBLANES) == 0:
        t //= 2
    return t


def _pair_sum(g, recv, c_arr, name):
    _, _, R, n = g.shape
    tr = _row_tile(R, n)

    def body(c_ref, g_ref, r_ref, o_ref):
        o_ref[...] = (g_ref[...] + r_ref[...]).astype(ICI_DTYPE)

    grid_spec = pltpu.PrefetchScalarGridSpec(
        num_scalar_prefetch=1, grid=(N_CHIPS, R // tr),
        in_specs=[pl.BlockSpec((None, None, tr, n), lambda p, i, c: (p, c[0], i, 0)),
                  pl.BlockSpec((None, tr, n), lambda p, i, c: (p, i, 0))],
        out_specs=pl.BlockSpec((None, tr, n), lambda p, i, c: (p, i, 0)))
    return pl.pallas_call(
        body, name=name, grid_spec=grid_spec, out_shape=jax.ShapeDtypeStruct(recv.shape, ICI_DTYPE),
        compiler_params=_cparams("parallel", "parallel"),
    )(c_arr, g, recv)


def _chip_sum(recv, own, where, name):
    _, R, n = recv.shape
    tr = _row_tile(R, n)

    def body(w_ref, r0, r1, r2, r3, own_ref, o_ref):
        me = w_ref[0]
        terms = [jnp.where(me == k, own_ref[...], r[...]).astype(F32) for k, r in enumerate((r0, r1, r2, r3))]
        o_ref[...] = ((terms[0] + terms[1]) + terms[2]) + terms[3]

    def slot(k):
        return pl.BlockSpec((None, tr, n), lambda i, w: (w[1 + k], i, 0))

    grid_spec = pltpu.PrefetchScalarGridSpec(
        num_scalar_prefetch=1, grid=(R // tr,),
        in_specs=[slot(0), slot(1), slot(2), slot(3), pl.BlockSpec((None, tr, n), lambda i, w: (w[0], i, 0))],
        out_specs=pl.BlockSpec((None, tr, n), lambda i, w: (w[5], i, 0)))
    return pl.pallas_call(
        body, name=name, grid_spec=grid_spec, out_shape=jax.ShapeDtypeStruct((2, R, n), F32),
        compiler_params=_cparams("parallel"),
    )(where, recv, recv, recv, recv, own)


ADAM_C1 = 1.0 / (1.0 - ADAM_B1 ** ADAM_STEP)
ADAM_C2 = 1.0 / (1.0 - ADAM_B2 ** ADAM_STEP)


def _adamw_math(w, g, m, v):
    m = ADAM_B1 * m + (1.0 - ADAM_B1) * g
    v = ADAM_B2 * v + (1.0 - ADAM_B2) * (g * g)
    delta = -ADAM_LR * ((m * ADAM_C1) / (jnp.sqrt(v * ADAM_C2) + ADAM_EPS) + ADAM_WD * w)
    return delta, m, v


def _adamw(w, g, m, v, name):
    R, n = w.shape
    tr = _row_tile(R, n)

    def body(w_ref, g_ref, m_ref, v_ref, d_ref, nm_ref, nv_ref):
        d_ref[...], nm_ref[...], nv_ref[...] = _adamw_math(w_ref[...], g_ref[...], m_ref[...], v_ref[...])

    spec = pl.BlockSpec((tr, n), lambda i: (i, 0))
    return pl.pallas_call(
        body, name=name, grid=(R // tr,), in_specs=[spec] * 4, out_specs=[spec] * 3,
        out_shape=[jax.ShapeDtypeStruct((R, n), F32)] * 3, compiler_params=_cparams("parallel"),
    )(w, g, m, v)


def _adamw_small(w, gall, m, v, name):
    M, n = w.shape

    def body(w_ref, g_ref, m_ref, v_ref, gs_ref, d_ref, nm_ref, nv_ref):
        g = g_ref[0:M, :]
        for d in range(1, N_DEV):
            g = g + g_ref[d * M:(d + 1) * M, :]
        gs_ref[...] = g
        d_ref[...], nm_ref[...], nv_ref[...] = _adamw_math(w_ref[...], g, m_ref[...], v_ref[...])

    return pl.pallas_call(
        body, name=name, out_shape=[jax.ShapeDtypeStruct((M, n), F32)] * 4,
    )(w, gall, m, v)


SMALL_ROWS = 24
MID_ROWS = 4


def _pack_small(ln1_g, ln1_b, ln2_g, ln2_b, norm_w, sinks, a_log, dt_bias):
    mixed = jnp.concatenate([norm_w.reshape(-1), sinks.reshape(-1), a_log.reshape(-1), dt_bias.reshape(-1)])
    mixed = jnp.pad(mixed, (0, D_MODEL - mixed.shape[0]))[None]
    pad = jnp.zeros((SMALL_ROWS - 4 * DEPTH - 1, D_MODEL), F32)
    return jnp.concatenate([ln1_g, ln1_b, ln2_g, ln2_b, mixed, pad], axis=0)


def _unpack_small(p):
    mixed = p[4 * DEPTH]
    return (p[0:4], p[4:8], p[8:12], p[12:16], mixed[0:256].reshape(2, 128), mixed[256:272].reshape(2, 8),
            mixed[272:280].reshape(2, 4), mixed[280:288].reshape(2, 4))


def _pack_mid(conv_w, rconv_w, rconv_b, b_a, b_x, lam):
    lead = conv_w.shape[0]
    flat = jnp.concatenate([conv_w.reshape(lead, -1), rconv_w.reshape(lead, -1), rconv_b, b_a, b_x, lam], axis=1)
    return jnp.pad(flat, ((0, 0), (0, MID_ROWS * D_MODEL - flat.shape[1]))).reshape(lead, MID_ROWS, D_MODEL)


def _unpack_mid(p):
    lead = p.shape[:-2]
    f = p.reshape(lead + (MID_ROWS * D_MODEL,))
    return (f[..., 0:1536].reshape(lead + (4, 384)), f[..., 1536:2560].reshape(lead + (4, 256)),
            f[..., 2560:2816], f[..., 2816:3072], f[..., 3072:3328], f[..., 3328:3584])


def _cols_from_chips(g):
    p, L, R, n = g.shape
    return g.transpose(1, 2, 0, 3).reshape(L, R, p * n)


def _rows_from_chips(g):
    p, L, R, n = g.shape
    return g.transpose(1, 0, 2, 3).reshape(L, p * R, n)


def _cols_to_chips(g):
    L, R, n4 = g.shape
    return g.reshape(L, R, N_CHIPS, n4 // N_CHIPS).transpose(2, 0, 1, 3)


def _rows_to_chips(g):
    L, R4, n = g.shape
    return g.reshape(L, N_CHIPS, R4 // N_CHIPS, n).transpose(1, 0, 2, 3)


def _halves(a):
    return a.reshape(2, -1, a.shape[-1])


def _pad_hyb_cols(w):
    z = jnp.zeros(w.shape[:-1] + (HYB_PAD - HP_BG - 2 * B_HEADS,), w.dtype)
    return jnp.concatenate([w[..., 0:512], w[..., 768:2304], w[..., 2304:2816], w[..., 512:768], w[..., 2816:2824], z], axis=-1)


def _unpad_hyb_cols(w):
    return jnp.concatenate([w[..., 0:512], w[..., 2560:2816], w[..., 512:2048], w[..., 2048:2560], w[..., 2816:2824]], axis=-1)


def _hybrid_fwd(x, W, j, tables, sfx):
    cos, sin_s = tables
    T = x.shape[0]
    N = T // B_CHUNK
    proj = _matmul(x, W["hyb_w_in"][j], "nn", "hyb_in" + sfx)
    sinks_b = jnp.broadcast_to(W["hyb_sinks"][j][:, None], (A_Q_HEADS, LANES))
    o_a, lse = _attn_fwd(proj, cos, sin_s, sinks_b, "attn_fwd" + sfx)
    qkvc = _dnconv_fwd(proj, W["hyb_conv_w"][j], "dnconv_fwd" + sfx)
    bg = proj[:, HP_BG:HP_BG + 2 * B_HEADS]
    beta = jax.nn.sigmoid(bg[:, :B_HEADS])
    pre = bg[:, B_HEADS:] + W["hyb_dt_bias"][j][None]
    g = -jnp.exp(W["hyb_a_log"][j])[None] * jax.nn.softplus(pre)
    grow = g.T.reshape(B_HEADS, N, 1, B_CHUNK)
    brow = beta.T.reshape(B_HEADS, N, 1, B_CHUNK)
    nw = W["hyb_norm_w"][j][None]
    o_b, states = _dn_fwd(qkvc, proj, nw, grow, brow, "dn_fwd" + sfx)
    mix = jnp.concatenate([o_a, o_b], axis=1).astype(ACT_DTYPE)
    y = _matmul(mix, W["hyb_w_out"][j], "nn", "hyb_out" + sfx)
    res = dict(proj=proj, o_a=o_a, lse=lse, qkvc=qkvc, beta=beta, pre=pre, g=g, grow=grow, brow=brow,
               states=states, mix=mix, sinks_b=sinks_b, nw=nw)
    return y, res


def _hybrid_bwd(x, du, dub, W, j, res, tables, sfx):
    cos, sin_s = tables
    T = x.shape[0]
    proj = res["proj"]
    d_wout = _matmul(res["mix"], dub, "tn", "hyb_out_dw" + sfx)
    dmix = _matmul(dub, W["hyb_w_out"][j], "nt", "hyb_out_dx" + sfx)
    dq, dkc, dkp, dvc, dvp, dsink = _attn_bwd(proj, cos, sin_s, res["sinks_b"], res["o_a"], res["lse"], dmix,
                                               "attn_bwd" + sfx)
    zpad = jnp.zeros((WINDOW, LANES), F32)
    dk = dkc + jnp.concatenate([dkp[WINDOW:], zpad], axis=0)
    dv = dvc + jnp.concatenate([dvp[WINDOW:], zpad], axis=0)
    dqc, dkcv, dvcv, dz, dg4, dbeta4, dnw = _dn_bwd(res["qkvc"], proj, res["nw"], res["grow"], res["brow"],
                                                    res["states"], dmix, "dn_bwd" + sfx)
    dqkvc = jnp.concatenate([dqc, dkcv, dvcv], axis=1)
    dqkvb, dconv = _dnconv_bwd(proj, W["hyb_conv_w"][j], dqkvc, "dnconv_bwd" + sfx)
    dg = dg4.reshape(B_HEADS, T).T
    dbeta = dbeta4.reshape(B_HEADS, T).T
    beta = res["beta"]
    dbeta_logit = dbeta * beta * (1.0 - beta)
    da_logit = dg * (-jnp.exp(W["hyb_a_log"][j]))[None] * jax.nn.sigmoid(res["pre"])
    d_dt_bias = jnp.sum(da_logit, axis=0)
    d_a_log = jnp.sum(dg * res["g"], axis=0)
    zcols = jnp.zeros((T, HYB_PAD - HP_BG - 2 * B_HEADS), F32)
    dproj = jnp.concatenate([dq, dqkvb, dz, dk, dv, dbeta_logit, da_logit, zcols], axis=1).astype(ACT_DTYPE)
    d_win = _matmul(x, dproj, "tn", "hyb_in_dw" + sfx)
    dx = _matmul(dproj, W["hyb_w_in"][j], "nt", "hyb_in_dx" + sfx, epi=_epi_add_residual, extra=du)
    grads = dict(hyb_w_in=d_win, hyb_w_out=d_wout, hyb_sinks=dsink[0], hyb_conv_w=dconv, hyb_a_log=d_a_log,
                 hyb_dt_bias=d_dt_bias, hyb_norm_w=jnp.sum(dnw[:, 0, :], axis=0))
    return dx, grads


def _rec_fwd(x, W, j, sfx):
    proj = _matmul(x, W["rec_w_in"][j], "nn", "rec_in" + sfx)
    sp = jax.nn.softplus(-W["rec_lambda"][j])[None]
    hg, h = _rglru_fwd(proj, W["rec_conv_w"][j], W["rec_conv_b"][j][None], W["rec_w_a"][j], W["rec_w_x"][j],
                       W["rec_b_a"][j][None], W["rec_b_x"][j][None], sp, "rglru_fwd" + sfx)
    y = _matmul(hg, W["rec_w_out"][j], "nn", "rec_out" + sfx)
    return y, dict(proj=proj, hg=hg, h=h, sp=sp)


def _rec_bwd(x, du, dub, W, j, res, sfx):
    d_wout = _matmul(res["hg"], dub, "tn", "rec_out_dw" + sfx)
    dhg = _matmul(dub, W["rec_w_out"][j], "nt", "rec_out_dx" + sfx)
    dproj, dcw, dcb, dwa, dwx, dba, dbx, dsp = _rglru_bwd(
        res["proj"], W["rec_conv_w"][j], W["rec_conv_b"][j][None], W["rec_w_a"][j], W["rec_w_x"][j],
        W["rec_b_a"][j][None], W["rec_b_x"][j][None], res["sp"], res["h"], dhg, "rglru_bwd" + sfx)
    d_lam = dsp[0] * (-jax.nn.sigmoid(-W["rec_lambda"][j]))
    d_win = _matmul(x, dproj, "tn", "rec_in_dw" + sfx)
    dx = _matmul(dproj, W["rec_w_in"][j], "nt", "rec_in_dx" + sfx, epi=_epi_add_residual, extra=du)
    grads = dict(rec_w_in=d_win, rec_w_out=d_wout, rec_conv_w=dcw, rec_conv_b=dcb[0], rec_w_a=dwa, rec_w_x=dwx,
                 rec_b_a=dba[0], rec_b_x=dbx[0], rec_lambda=d_lam)
    return dx, grads


def _local_step(x, tgt, W, mlp_w, on_group):
    T = x.shape[0]
    tables = _rope_tables(T)
    acts = []
    xb = x.astype(ACT_DTYPE)
    for layer in range(DEPTH):
        j, sfx = layer // 2, ""
        if layer % 2 == 0:
            y, res = _hybrid_fwd(xb, W, j, tables, sfx)
        else:
            y, res = _rec_fwd(xb, W, j, sfx)
        x1, x1b = _ln_fwd(x, y, W["ln1_g"][layer][None], W["ln1_b"][layer][None], "ln_fwd")
        w1, w2, wl = mlp_w(layer, x1b)
        h1 = _matmul(x1b, w1, "nn", "mlp_up", out_dtype=ACT_DTYPE, b_chips=("j", wl))
        y2 = _matmul(h1, w2, "nn", "mlp_down", a_fn=_relu2, b_chips=("k", wl))
        x2, x2b = _ln_fwd(x1, y2, W["ln2_g"][layer][None], W["ln2_b"][layer][None], "ln_fwd")
        acts.append(dict(x=x, xb=xb, y=y, res=res, x1=x1, x1b=x1b, h1=h1, y2=y2))
        x, xb = x2, x2b
    dx, loss = _loss_head(x, tgt, "loss_head")
    per_layer = [None] * DEPTH
    d_w1 = [lax.empty((N_CHIPS, 2, D_MODEL, D_FF // N_CHIPS), F32) for _ in range(DEPTH // 2)]
    d_w2 = [lax.empty((N_CHIPS, 2, D_FF // N_CHIPS, D_MODEL), F32) for _ in range(DEPTH // 2)]
    token = None
    for layer in reversed(range(DEPTH)):
        j, a = layer // 2, acts[layer]
        ln2_g = W["ln2_g"][layer][None]
        if token is not None:
            ln2_g = ln2_g + token
        du2, du2b, dg2, db2 = _ln_bwd(a["x1"], a["y2"], ln2_g, dx, "ln_bwd")
        w1, w2, wl = mlp_w(layer, du2b)
        d_w2[j] = _matmul(a["h1"], du2b, "tn", "mlp_down_dw", a_fn=_relu2, out_chips=("i", layer % 2, d_w2[j]))
        dh1 = _matmul(du2b, w2, "nt", "mlp_down_dx", epi=_epi_drelu2, extra=a["h1"], out_dtype=ACT_DTYPE,
                      b_chips=("j", wl))
        d_w1[j] = _matmul(a["x1b"], dh1, "tn", "mlp_up_dw", out_chips=("j", layer % 2, d_w1[j]))
        dx1 = _matmul(dh1, w1, "nt", "mlp_up_dx", epi=_epi_add_residual, extra=du2, b_chips=("k", wl))
        du1, du1b, dg1, db1 = _ln_bwd(a["x"], a["y"], W["ln1_g"][layer][None], dx1, "ln_bwd")
        if layer % 2 == 0:
            dx, g = _hybrid_bwd(a["xb"], du1, du1b, W, j, a["res"], tables, "")
        else:
            dx, g = _rec_bwd(a["xb"], du1, du1b, W, j, a["res"], "")
        g.update(ln1_g=dg1[0], ln1_b=db1[0], ln2_g=dg2[0], ln2_b=db2[0])
        per_layer[layer] = g
        if layer % 2 == 0:
            token = on_group(j, per_layer[layer], per_layer[layer + 1], d_w1[j], d_w2[j])
    grads = {}
    for name in ("ln1_g", "ln1_b", "ln2_g", "ln2_b"):
        grads[name] = jnp.stack([per_layer[l][name] for l in range(DEPTH)])
    for name in ("hyb_norm_w", "hyb_sinks", "hyb_a_log", "hyb_dt_bias"):
        grads[name] = jnp.stack([per_layer[l][name] for l in (0, 2)])
    return loss, dx, grads


BIG = ("hyb_w_in", "hyb_w_out", "rec_w_in", "rec_w_out", "mlp_w1", "mlp_w2", "rec_w_a", "rec_w_x")
COL_SHARDED = ("hyb_w_in", "rec_w_in", "mlp_w1")
CHIP_MAJOR = ("mlp_w1", "mlp_w2")
MID = ("hyb_conv_w", "rec_conv_w", "rec_conv_b", "rec_b_a", "rec_b_x", "rec_lambda")
SMALL = ("ln1_g", "ln1_b", "ln2_g", "ln2_b", "hyb_norm_w", "hyb_sinks", "hyb_a_log", "hyb_dt_bias")
WEIGHTS = ("hyb_w_in", "hyb_sinks", "hyb_conv_w", "hyb_a_log", "hyb_dt_bias", "hyb_norm_w", "hyb_w_out", "rec_w_in",
           "rec_conv_w", "rec_conv_b", "rec_w_a", "rec_b_a", "rec_w_x", "rec_b_x", "rec_lambda", "rec_w_out", "ln1_g",
           "ln1_b", "mlp_w1", "mlp_w2", "ln2_g", "ln2_b")


def _gather_full_weights(w):
    wb = {k: w[k].astype(MXU_DTYPE) for k in BIG}
    shards = [_halves(wb[k][:1] if k in CHIP_MAJOR else wb[k]) for k in BIG]
    shards.append(_pack_mid(*[w[k] for k in MID]))
    got = _all_gather_weights(shards, "all_gather_weights")
    me = 2 * lax.axis_index("x") + lax.axis_index("y")
    got = [lax.dynamic_update_slice(g, s[None], (me, 0, 0, 0)) for s, g in zip(shards, got)]

    own = [wb[k][1:] for k in CHIP_MAJOR]
    land = [lax.dynamic_update_slice(lax.empty((N_CHIPS,) + o.shape, o.dtype), o[None], (me, 0, 0, 0)) for o in own]
    send_sems, recv_sems, own, land = _gather_start(own, land, got[-1], "gather_start")
    first = {k: g.reshape((N_CHIPS, 1) + w[k].shape[1:]) for k, g in zip(BIG, got[:-1]) if k in CHIP_MAJOR}
    pending = dict(own=own, land=land, waited=0)

    def mlp_w(layer, after):
        if layer == 0:
            return first["mlp_w1"], first["mlp_w2"], 0
        while pending["waited"] < layer:
            pending["own"], pending["land"] = _gather_wait(send_sems, recv_sems, pending["own"], pending["land"],
                                                           pending["waited"], after, "gather_wait_%d" % pending["waited"])
            pending["waited"] += 1
        return pending["land"][0], pending["land"][1], layer - 1

    W = {}
    for k, g in zip(BIG, got[:-1]):
        L = w[k].shape[0]
        if k in CHIP_MAJOR:
            continue
        elif k in ("rec_w_a", "rec_w_x"):
            g5 = g.reshape((N_CHIPS,) + w[k].shape)
            W[k] = g5.transpose(1, 2, 0, 3, 4).reshape(L, LRU_BLOCKS, LRU_BLOCK_W, LRU_BLOCK_W)
        else:
            g4 = g.reshape((N_CHIPS,) + w[k].shape)
            W[k] = _cols_from_chips(g4) if k in COL_SHARDED else _rows_from_chips(g4)
    W["hyb_w_in"] = _pad_hyb_cols(W["hyb_w_in"])
    conv_w, rconv_w, rconv_b, b_a, b_x, lam = _unpack_mid(got[-1])
    W["hyb_conv_w"] = conv_w.transpose(1, 2, 0, 3).reshape(2, CONV_K, 3 * B_W)
    W["rec_conv_w"] = rconv_w.transpose(1, 2, 0, 3).reshape(2, CONV_K, D_MODEL)
    for k, v in (("rec_conv_b", rconv_b), ("rec_b_a", b_a), ("rec_b_x", b_x), ("rec_lambda", lam)):
        W[k] = v.transpose(1, 0, 2).reshape(2, D_MODEL)
    for k in SMALL:
        W[k] = w[k]
    return W, mlp_w


def _group_by_chip(gh, gr, d_w1, d_w2):
    g = dict(gh, **gr)
    g["hyb_w_in"] = _unpad_hyb_cols(g["hyb_w_in"])
    out = []
    for k in BIG:
        if k == "mlp_w1":
            v = d_w1
        elif k == "mlp_w2":
            v = d_w2
        elif k in ("rec_w_a", "rec_w_x"):
            v = g[k].reshape(1, LRU_BLOCKS, N_CHIPS, LRU_BLOCK_W // N_CHIPS, LRU_BLOCK_W).transpose(2, 0, 1, 3, 4)
        else:
            v = _cols_to_chips(g[k][None]) if k in COL_SHARDED else _rows_to_chips(g[k][None])
        out.append(v.reshape(N_CHIPS, 2, -1, v.shape[-1]))
    conv_w = g["hyb_conv_w"].reshape(CONV_K, N_CHIPS, -1).transpose(1, 0, 2)
    rconv_w = g["rec_conv_w"].reshape(CONV_K, N_CHIPS, -1).transpose(1, 0, 2)
    vecs = [g[k].reshape(N_CHIPS, -1) for k in ("rec_conv_b", "rec_b_a", "rec_b_x", "rec_lambda")]
    out.append(_pack_mid(conv_w, rconv_w, *vecs).reshape(N_CHIPS, 2, MID_ROWS // 2, D_MODEL))
    return out


def kernel(x, hyb_w_in, hyb_sinks, hyb_conv_w, hyb_a_log, hyb_dt_bias, hyb_norm_w, hyb_w_out, rec_w_in, rec_conv_w, rec_conv_b, rec_w_a, rec_b_a, rec_w_x, rec_b_x, rec_lambda, rec_w_out, ln1_g, ln1_b, mlp_w1, mlp_w2, ln2_g, ln2_b, loss_target, m_hyb_w_in, m_hyb_sinks, m_hyb_conv_w, m_hyb_a_log, m_hyb_dt_bias, m_hyb_norm_w, m_hyb_w_out, m_rec_w_in, m_rec_conv_w, m_rec_conv_b, m_rec_w_a, m_rec_b_a, m_rec_w_x, m_rec_b_x, m_rec_lambda, m_rec_w_out, m_ln1_g, m_ln1_b, m_mlp_w1, m_mlp_w2, m_ln2_g, m_ln2_b, v_hyb_w_in, v_hyb_sinks, v_hyb_conv_w, v_hyb_a_log, v_hyb_dt_bias, v_hyb_norm_w, v_hyb_w_out, v_rec_w_in, v_rec_conv_w, v_rec_conv_b, v_rec_w_a, v_rec_b_a, v_rec_w_x, v_rec_b_x, v_rec_lambda, v_rec_w_out, v_ln1_g, v_ln1_b, v_mlp_w1, v_mlp_w2, v_ln2_g, v_ln2_b):
    args = locals()
    w = {k: args[k] for k in WEIGHTS}
    m = {k: args["m_" + k] for k in WEIGHTS}
    v = {k: args["v_" + k] for k in WEIGHTS}

    W, mlp_w = _gather_full_weights(w)

    core = lax.axis_index("c").astype(jnp.int32)
    me = (2 * lax.axis_index("x") + lax.axis_index("y")).astype(jnp.int32)
    slots = jnp.arange(N_CHIPS, dtype=jnp.int32)
    where = jnp.concatenate([me[None], jnp.where(slots == me, (slots + 1) % N_CHIPS, slots), core[None]])
    state = {}

    def on_group(group, gh, gr, d_w1, d_w2):
        by_chip = _group_by_chip(gh, gr, d_w1, d_w2)
        from_sibling = _rs_to_sibling(by_chip, "rs_to_sibling")
        pair = [_pair_sum(g, r, core[None], "pair_sum") for g, r in zip(by_chip, from_sibling)]
        if group == 1:
            recv = [lax.empty(p.shape, p.dtype) for p in pair]
            state["upper"] = _rs_across_start(pair, recv, "rs_across_start")
            return state["upper"][4][0, 0]
        from_chips = _rs_across_chips(pair, "rs_across_chips")
        half = [_chip_sum(r, p, where, "chip_sum") for r, p in zip(from_chips, pair)]
        state["lower"] = _rs_join_halves(half, "rs_join_halves")
        return None

    loss, dx, grads = _local_step(x[0], loss_target[0], W, mlp_w, on_group)
    loss = lax.psum(loss[0, 0], ("x", "y", "c"))
    send_sems, recv_sems, pair, recv, _ = state["upper"]
    pair, from_chips = _rs_across_wait(send_sems, recv_sems, pair, recv, state["lower"][0], "rs_across_wait")
    half = [_chip_sum(r, p, where, "chip_sum") for r, p in zip(from_chips, pair)]
    upper = _rs_join_halves(half, "rs_join_halves")
    joined = [jnp.concatenate([lo.reshape(1, -1, lo.shape[-1]), up.reshape(1, -1, up.shape[-1])])
              for lo, up in zip(state["lower"], upper)]

    g_out, d_out, m_out, v_out = {}, {}, {}, {}

    def update(name, g2d):
        shape = w[name].shape
        n = g2d.shape[-1]
        d, nm, nv = _adamw(w[name].reshape(-1, n), g2d, m[name].reshape(-1, n), v[name].reshape(-1, n), "adamw")
        g_out[name], d_out[name] = g2d.reshape(shape), d.reshape(shape)
        m_out[name], v_out[name] = nm.reshape(shape), nv.reshape(shape)

    for k, g in zip(BIG, joined[:-1]):
        update(k, g.reshape(-1, g.shape[-1]))
    mid_w, mid_m, mid_v = (_pack_mid(*[t[k] for k in MID]).reshape(-1, D_MODEL) for t in (w, m, v))
    mid_g = joined[-1].reshape(-1, D_MODEL)
    mid_d, mid_nm, mid_nv = _adamw(mid_w, mid_g, mid_m, mid_v, "adamw_mid")
    for dst, packed in ((g_out, mid_g), (d_out, mid_d), (m_out, mid_nm), (v_out, mid_nv)):
        for k, val in zip(MID, _unpack_mid(packed.reshape(2, MID_ROWS, D_MODEL))):
            dst[k] = val.reshape(w[k].shape)

    small_g = _pack_small(*[grads[k] for k in SMALL])
    small_all = _all_gather_small(small_g, "all_gather_small")
    sw, sm, sv = (_pack_small(*[t[k] for k in SMALL]) for t in (w, m, v))
    sg, sd, snm, snv = _adamw_small(sw, small_all, sm, sv, "adamw_small")
    for dst, packed in ((g_out, sg), (d_out, sd), (m_out, snm), (v_out, snv)):
        for k, val in zip(SMALL, _unpack_small(packed)):
            dst[k] = val

    return (loss, dx[None], *[g_out[k] for k in WEIGHTS], *[d_out[k] for k in WEIGHTS],
            *[m_out[k] for k in WEIGHTS], *[v_out[k] for k in WEIGHTS])
```

```python
import functools

import jax
import jax.numpy as jnp
import numpy as np
from jax import lax
from jax.experimental import pallas as pl
from jax.experimental.pallas import tpu as pltpu

F32 = jnp.float32
MXU_DTYPE = jnp.bfloat16
ACT_DTYPE = jnp.bfloat16
ICI_DTYPE = jnp.bfloat16

D_MODEL = 1024
DEPTH = 4
A_HEAD_DIM = 64
A_Q_HEADS = 8
A_KV_HEADS = 2
WINDOW = 128
ROPE_THETA = 10000.0
B_HEADS = 4
B_HEAD_DIM = 128
B_CHUNK = 64
CONV_K = 4
LRU_BLOCKS = 4
LRU_BLOCK_W = D_MODEL // LRU_BLOCKS
LRU_C = 8.0
D_FF = 4 * D_MODEL
A_Q_W = A_Q_HEADS * A_HEAD_DIM
A_KV_W = A_KV_HEADS * A_HEAD_DIM
B_W = B_HEADS * B_HEAD_DIM
HYB_PROJ = A_Q_W + 2 * A_KV_W + 4 * B_W + 2 * B_HEADS
DN_ALPHA = (2 * DEPTH) ** 0.25
LN_EPS = 1e-5
NORM_EPS = 1e-6
ADAM_LR = 0.001
ADAM_B1 = 0.9
ADAM_B2 = 0.999
ADAM_EPS = 1e-08
ADAM_WD = 0.01
ADAM_STEP = 10

HP_Q = 0
HP_QKVB = 512
HP_Z = 2048
HP_K = 2560
HP_V = 2688
HP_BG = 2816
HYB_PAD = 3072

N_CHIPS = 4
N_DEV = 8
V7X_VMEM_LIMIT = 48 * 1024 * 1024
LANES = 128
SUBLANES = 8
NEG_BIG = -1e30

NN = (((1,), (0,)), ((), ()))
NT = (((1,), (1,)), ((), ()))
TN = (((0,), (0,)), ((), ()))


def _cparams(*sem):
    return pltpu.CompilerParams(dimension_semantics=sem, vmem_limit_bytes=V7X_VMEM_LIMIT)


def _dot(a, b, dims=NN):
    return lax.dot_general(a.astype(MXU_DTYPE), b.astype(MXU_DTYPE), dims, preferred_element_type=F32)


def _split_bf16(a):
    hi = a.astype(jnp.bfloat16)
    return hi, (a - hi.astype(F32)).astype(jnp.bfloat16)


def _dotf(a, b, dims=NN):
    ah, al = _split_bf16(a)
    bh, bl = _split_bf16(b)
    dg = functools.partial(lax.dot_general, dimension_numbers=dims, preferred_element_type=F32)
    return dg(ah, bh) + (dg(ah, bl) + dg(al, bh))


def _tile(dim, pref):
    t = min(dim, pref)
    while dim % t:
        t //= 2
    return t


def _sigmoid(x):
    return 1.0 / (1.0 + jnp.exp(-x))


def _silu(x):
    return x * _sigmoid(x)


def _dsilu(x):
    s = _sigmoid(x)
    return s * (1.0 + x * (1.0 - s))


GELU_C = 0.7978845608028654
GELU_A = 0.044715


def _gelu(x):
    return 0.5 * x * (1.0 + jnp.tanh(GELU_C * (x + GELU_A * x * x * x)))


def _dgelu(x):
    t = jnp.tanh(GELU_C * (x + GELU_A * x * x * x))
    return 0.5 * (1.0 + t) + 0.5 * x * (1.0 - t * t) * GELU_C * (1.0 + 3.0 * GELU_A * x * x)


def _matmul(a, b, mode, name, *, tm=1024, tn=1024, tk=1024, a_fn=None, epi=None, extra=None, out_dtype=F32,
            b_chips=None, out_chips=None):
    if mode == "tn":
        K, M = a.shape
    else:
        M, K = a.shape
    if b_chips is not None:
        g, b_layer = b_chips
        r, n = b.shape[2:]
        n_dim, k_dim = (r, n) if mode == "nt" else (n, r)
        N = N_CHIPS * n_dim if g == "j" else n_dim
        assert K == (N_CHIPS * k_dim if g == "k" else k_dim)
        if g == "j":
            tn = n_dim
        else:
            tk = k_dim
    elif mode == "nt":
        N = b.shape[0]
    else:
        N = b.shape[1]
    if out_chips is not None:
        og, o_layer, o_buf = out_chips
        if og == "j":
            tn = o_buf.shape[3]
        else:
            tm = o_buf.shape[2]
    tm, tn, tk = _tile(M, tm), _tile(N, tn), _tile(K, tk)
    nk = K // tk
    if mode == "tn":
        a_spec = pl.BlockSpec((tk, tm), lambda i, j, k: (k, i))
    else:
        a_spec = pl.BlockSpec((tm, tk), lambda i, j, k: (i, k))
    b_block = (tn, tk) if mode == "nt" else (tk, tn)
    if b_chips is None:
        b_spec = pl.BlockSpec(b_block, (lambda i, j, k: (j, k)) if mode == "nt" else (lambda i, j, k: (k, j)))
    elif mode == "nt":
        b_spec = pl.BlockSpec((None, None) + b_block, (lambda i, j, k: (j, b_layer, 0, k)) if g == "j"
                              else (lambda i, j, k: (k, b_layer, j, 0)))
    else:
        b_spec = pl.BlockSpec((None, None) + b_block, (lambda i, j, k: (j, b_layer, k, 0)) if g == "j"
                              else (lambda i, j, k: (k, b_layer, 0, j)))
    o_spec = pl.BlockSpec((tm, tn), lambda i, j, k: (i, j))
    e_spec = o_spec
    if out_chips is not None:
        o_spec = pl.BlockSpec((None, None, tm, tn), (lambda i, j, k: (j, o_layer, i, 0)) if og == "j"
                              else (lambda i, j, k: (i, o_layer, 0, j)))
    dims = {"nn": NN, "nt": NT, "tn": TN}[mode]
    has_extra = extra is not None
    n_in = 2 + has_extra + (out_chips is not None)

    def body(*refs):
        a_ref, b_ref = refs[0], refs[1]
        e_ref = refs[2] if has_extra else None
        o_ref = refs[n_in]
        av = a_ref[...]
        if a_fn is not None:
            av = a_fn(av)
        part = _dot(av, b_ref[...], dims)

        def finish(acc):
            if epi is not None:
                acc = epi(acc, e_ref[...])
            o_ref[...] = acc.astype(out_dtype)

        if nk == 1:
            finish(part)
        else:
            acc_ref = refs[-1]
            k = pl.program_id(2)

            @pl.when(k == 0)
            def _():
                acc_ref[...] = part

            @pl.when(k > 0)
            def _():
                acc_ref[...] += part

            @pl.when(k == nk - 1)
            def _():
                finish(acc_ref[...])

    in_specs = [a_spec, b_spec] + ([e_spec] if has_extra else [])
    args = (a, b) + ((extra,) if has_extra else ())
    out_shape = jax.ShapeDtypeStruct((M, N), out_dtype)
    aliases = {}
    if out_chips is not None:
        in_specs.append(pl.BlockSpec(memory_space=pl.ANY))
        args += (o_buf,)
        out_shape = jax.ShapeDtypeStruct(o_buf.shape, o_buf.dtype)
        aliases = {n_in - 1: 0}
    return pl.pallas_call(
        body, name=name, grid=(M // tm, N // tn, nk), in_specs=in_specs, out_specs=o_spec, out_shape=out_shape,
        input_output_aliases=aliases,
        scratch_shapes=[pltpu.VMEM((tm, tn), F32)] if nk > 1 else [],
        compiler_params=_cparams("parallel", "parallel", "arbitrary"),
    )(*args)


def _relu2(v):
    r = jnp.maximum(v, 0.0)
    return r * r


def _epi_drelu2(acc, h):
    return acc * (2.0 * jnp.maximum(h, 0.0))


def _epi_add_residual(acc, du):
    return acc + DN_ALPHA * du


def _ln_fwd(x, y, g, b, name):
    T, D = x.shape
    tr = _tile(T, 512)

    def body(x_ref, y_ref, g_ref, b_ref, o_ref, ob_ref):
        u = DN_ALPHA * x_ref[...] + y_ref[...]
        mu = jnp.mean(u, axis=-1, keepdims=True)
        d = u - mu
        var = jnp.mean(d * d, axis=-1, keepdims=True)
        o = d * lax.rsqrt(var + LN_EPS) * g_ref[...] + b_ref[...]
        o_ref[...] = o
        ob_ref[...] = o.astype(ACT_DTYPE)

    row = pl.BlockSpec((tr, D), lambda i: (i, 0))
    vec = pl.BlockSpec((1, D), lambda i: (0, 0))
    return pl.pallas_call(
        body, name=name, grid=(T // tr,), in_specs=[row, row, vec, vec], out_specs=[row, row],
        out_shape=[jax.ShapeDtypeStruct((T, D), F32), jax.ShapeDtypeStruct((T, D), ACT_DTYPE)],
        compiler_params=_cparams("parallel"),
    )(x, y, g, b)


def _ln_bwd(x, y, g, dout, name):
    T, D = x.shape
    tr = _tile(T, 512)

    def body(x_ref, y_ref, g_ref, d_ref, du_ref, dub_ref, dg_ref, db_ref):
        i = pl.program_id(0)
        u = DN_ALPHA * x_ref[...] + y_ref[...]
        mu = jnp.mean(u, axis=-1, keepdims=True)
        d = u - mu
        rstd = lax.rsqrt(jnp.mean(d * d, axis=-1, keepdims=True) + LN_EPS)
        xhat = d * rstd
        dout_v = d_ref[...]
        dxh = dout_v * g_ref[...]
        m1 = jnp.mean(dxh, axis=-1, keepdims=True)
        m2 = jnp.mean(dxh * xhat, axis=-1, keepdims=True)
        du = rstd * (dxh - m1 - xhat * m2)
        du_ref[...] = du
        dub_ref[...] = du.astype(ACT_DTYPE)
        pg = jnp.sum(dout_v * xhat, axis=0, keepdims=True)
        pb = jnp.sum(dout_v, axis=0, keepdims=True)

        @pl.when(i == 0)
        def _():
            dg_ref[...] = pg
            db_ref[...] = pb

        @pl.when(i > 0)
        def _():
            dg_ref[...] += pg
            db_ref[...] += pb

    row = pl.BlockSpec((tr, D), lambda i: (i, 0))
    vec = pl.BlockSpec((1, D), lambda i: (0, 0))
    return pl.pallas_call(
        body, name=name, grid=(T // tr,), in_specs=[row, row, vec, row], out_specs=[row, row, vec, vec],
        out_shape=[jax.ShapeDtypeStruct((T, D), F32), jax.ShapeDtypeStruct((T, D), ACT_DTYPE),
                   jax.ShapeDtypeStruct((1, D), F32), jax.ShapeDtypeStruct((1, D), F32)],
        compiler_params=_cparams("arbitrary"),
    )(x, y, g, dout)


def _loss_head(y, tgt, name):
    T, D = y.shape
    tr = _tile(T, 512)

    def body(y_ref, t_ref, dy_ref, l_ref):
        i = pl.program_id(0)
        e = y_ref[...] - t_ref[...]
        dy_ref[...] = e * (1.0 / D)
        part = jnp.sum(e * e, axis=(0, 1), keepdims=True) * (0.5 / D)

        @pl.when(i == 0)
        def _():
            l_ref[...] = part

        @pl.when(i > 0)
        def _():
            l_ref[...] += part

    row = pl.BlockSpec((tr, D), lambda i: (i, 0))
    one = pl.BlockSpec((1, 1), lambda i: (0, 0))
    return pl.pallas_call(
        body, name=name, grid=(T // tr,), in_specs=[row, row], out_specs=[row, one],
        out_shape=[jax.ShapeDtypeStruct((T, D), F32), jax.ShapeDtypeStruct((1, 1), F32)],
        compiler_params=_cparams("arbitrary"),
    )(y, tgt)


def _swap_half(x):
    n = x.shape[-1]
    lane = lax.broadcasted_iota(jnp.int32, x.shape, 1)
    first = (lane % A_HEAD_DIM) < (A_HEAD_DIM // 2)
    return jnp.where(first, pltpu.roll(x, n - A_HEAD_DIM // 2, axis=1), pltpu.roll(x, A_HEAD_DIM // 2, axis=1))


def _rope(x, cos, sin_signed):
    return x * cos + _swap_half(x) * sin_signed


def _rope_t(dy, cos, sin_signed):
    return dy * cos + _swap_half(dy * sin_signed)


def _rope_tables(T):
    half = A_HEAD_DIM // 2
    inv_freq = np.float32(ROPE_THETA) ** (-np.arange(half, dtype=np.float32) / np.float32(half))
    ang = np.arange(T, dtype=np.float32)[:, None] * inv_freq[None, :]
    cos = np.tile(np.cos(ang), (1, 4))
    sin = np.sin(ang)
    sin_signed = np.tile(np.concatenate([-sin, sin], axis=1), (1, 2))
    return jnp.asarray(cos, F32), jnp.asarray(sin_signed, F32)


def _band_mask(n):
    qi = lax.broadcasted_iota(jnp.int32, (WINDOW, 2 * WINDOW), 0)
    kj = lax.broadcasted_iota(jnp.int32, (WINDOW, 2 * WINDOW), 1)
    return (kj > qi) & (kj <= qi + WINDOW) & ((n > 0) | (kj >= WINDOW))


def _place(v, src_half, dst_half):
    lane = lax.broadcasted_iota(jnp.int32, v.shape, 1)
    if src_half != dst_half:
        v = pltpu.roll(v, A_HEAD_DIM, axis=1)
    keep = (lane >= A_HEAD_DIM) if dst_half else (lane < A_HEAD_DIM)
    return jnp.where(keep, v, 0.0)


def _attn_specs():
    kb, vb = HP_K // LANES, HP_V // LANES
    prev = lambda n: jnp.maximum(n - 1, 0)
    return dict(
        q=pl.BlockSpec((WINDOW, A_Q_W), lambda n: (n, 0)),
        kc=pl.BlockSpec((WINDOW, LANES), lambda n: (n, kb)),
        kp=pl.BlockSpec((WINDOW, LANES), lambda n: (prev(n), kb)),
        vc=pl.BlockSpec((WINDOW, LANES), lambda n: (n, vb)),
        vp=pl.BlockSpec((WINDOW, LANES), lambda n: (prev(n), vb)),
        tq=pl.BlockSpec((WINDOW, LANES), lambda n: (n, 0)),
        tp=pl.BlockSpec((WINDOW, LANES), lambda n: (prev(n), 0)),
        sink=pl.BlockSpec((A_Q_HEADS, LANES), lambda n: (0, 0)),
        row512=pl.BlockSpec((WINDOW, A_Q_W), lambda n: (n, 0)),
        row128=pl.BlockSpec((WINDOW, LANES), lambda n: (n, 0)),
        lse=pl.BlockSpec((WINDOW, A_Q_HEADS), lambda n: (n, 0)),
    )


def _attn_fwd(proj, cos, sin_s, sinks_b, name):
    T = proj.shape[0]
    sp = _attn_specs()

    def body(q_ref, kc_ref, kp_ref, vc_ref, vp_ref, cq_ref, sq_ref, cp_ref, sp_ref, sink_ref, o_ref, l_ref):
        n = pl.program_id(0)
        cq, sq = cq_ref[...], sq_ref[...]
        cq4, sq4 = jnp.tile(cq, (1, A_Q_W // LANES)), jnp.tile(sq, (1, A_Q_W // LANES))
        q = _rope(q_ref[...], cq4, sq4) * (A_HEAD_DIM ** -0.5)
        kc = _rope(kc_ref[...], cq, sq)
        kp = _rope(kp_ref[...], cp_ref[...], sp_ref[...])
        kk = jnp.concatenate([kp, kc], axis=0)
        vv = jnp.concatenate([vp_ref[...], vc_ref[...]], axis=0)
        mask = _band_mask(n)[None]
        lane = lax.broadcasted_iota(jnp.int32, (WINDOW, LANES), 1)
        lane8 = lax.broadcasted_iota(jnp.int32, (WINDOW, A_Q_HEADS), 1)
        qe = jnp.stack([_place(q[:, (hq // 2) * LANES:(hq // 2 + 1) * LANES], hq % 2, hq // 4) for hq in range(A_Q_HEADS)])
        kk8 = jnp.broadcast_to(kk[None], (A_Q_HEADS,) + kk.shape)
        vv8 = jnp.broadcast_to(vv[None], (A_Q_HEADS,) + vv.shape)
        sk = jnp.stack([sink_ref[hq:hq + 1, 0:1] for hq in range(A_Q_HEADS)])
        s = jnp.where(mask, _dot(qe, kk8, BNT), NEG_BIG)
        m = jnp.maximum(jnp.max(s, axis=-1, keepdims=True), sk)
        p = jnp.exp(s - m)
        den = jnp.sum(p, axis=-1, keepdims=True) + jnp.exp(sk - m)
        o = _dot(p * (1.0 / den), vv8, BNN)
        lse_h = m + jnp.log(den)
        outs = []
        lse = jnp.zeros((WINDOW, A_Q_HEADS), F32)
        for pb in range(A_Q_HEADS // 2):
            halves = [_place(o[2 * pb + e], pb // 2, e) for e in range(2)]
            outs.append(jnp.where(lane < A_HEAD_DIM, halves[0], halves[1]))
            for e in range(2):
                lse = jnp.where(lane8 == 2 * pb + e, lse_h[2 * pb + e], lse)
        o_ref[...] = jnp.concatenate(outs, axis=1)
        l_ref[...] = lse

    return pl.pallas_call(
        body, name=name, grid=(T // WINDOW,),
        in_specs=[sp["q"], sp["kc"], sp["kp"], sp["vc"], sp["vp"], sp["tq"], sp["tq"], sp["tp"], sp["tp"], sp["sink"]],
        out_specs=[sp["row512"], sp["lse"]],
        out_shape=[jax.ShapeDtypeStruct((T, A_Q_W), F32), jax.ShapeDtypeStruct((T, A_Q_HEADS), F32)],
        compiler_params=_cparams("parallel"),
    )(proj, proj, proj, proj, proj, cos, sin_s, cos, sin_s, sinks_b)


def _attn_bwd(proj, cos, sin_s, sinks_b, o, lse, dmix, name):
    T = proj.shape[0]
    sp = _attn_specs()

    def body(q_ref, kc_ref, kp_ref, vc_ref, vp_ref, cq_ref, sq_ref, cp_ref, sp_ref, sink_ref, o_ref, l_ref, do_ref,
             dq_ref, dkc_ref, dkp_ref, dvc_ref, dvp_ref, dsink_ref):
        n = pl.program_id(0)
        cq, sq = cq_ref[...], sq_ref[...]
        cp, sps = cp_ref[...], sp_ref[...]
        cq4, sq4 = jnp.tile(cq, (1, A_Q_W // LANES)), jnp.tile(sq, (1, A_Q_W // LANES))
        q = _rope(q_ref[...], cq4, sq4) * (A_HEAD_DIM ** -0.5)
        kc = _rope(kc_ref[...], cq, sq)
        kp = _rope(kp_ref[...], cp, sps)
        kk = jnp.concatenate([kp, kc], axis=0)
        vv = jnp.concatenate([vp_ref[...], vc_ref[...]], axis=0)
        mask = _band_mask(n)[None]
        lane = lax.broadcasted_iota(jnp.int32, (WINDOW, LANES), 1)
        do_all, o_all, l_all = do_ref[...], o_ref[...], l_ref[...]
        lane8 = lax.broadcasted_iota(jnp.int32, (WINDOW, A_Q_HEADS), 1)
        head8 = lax.broadcasted_iota(jnp.int32, (1, A_Q_HEADS), 1)
        prod = do_all * o_all
        qes, does, deltas, lhs = [], [], [], []
        for hq in range(A_Q_HEADS):
            pb, e, kvh = hq // 2, hq % 2, hq // 4
            blk = slice(pb * LANES, (pb + 1) * LANES)
            in_half = (lane >= A_HEAD_DIM) if e else (lane < A_HEAD_DIM)
            deltas.append(jnp.sum(jnp.where(in_half, prod[:, blk], 0.0), axis=-1, keepdims=True))
            qes.append(_place(q[:, blk], e, kvh))
            does.append(_place(do_all[:, blk], e, kvh))
            lhs.append(jnp.sum(jnp.where(lane8 == hq, l_all, 0.0), axis=-1, keepdims=True))
        qe, doe, delta, lh = jnp.stack(qes), jnp.stack(does), jnp.stack(deltas), jnp.stack(lhs)
        kk8 = jnp.broadcast_to(kk[None], (A_Q_HEADS,) + kk.shape)
        vv8 = jnp.broadcast_to(vv[None], (A_Q_HEADS,) + vv.shape)
        sk = jnp.stack([sink_ref[hq:hq + 1, 0:1] for hq in range(A_Q_HEADS)])
        s = _dot(qe, kk8, BNT)
        p = jnp.where(mask, jnp.exp(jnp.where(mask, s, NEG_BIG) - lh), 0.0)
        dvv = jnp.sum(_dot(p, doe, BTN), axis=0)
        ds = p * (_dot(doe, vv8, BNT) - delta)
        dkk = jnp.sum(_dot(ds, qe, BTN), axis=0)
        dqe = _dot(ds, kk8, BNN)
        dsink_h = -jnp.sum(jnp.exp(sk - lh) * delta, axis=(1, 2), keepdims=True)
        dqs = []
        dsk = jnp.zeros((1, A_Q_HEADS), F32)
        for pb in range(A_Q_HEADS // 2):
            halves = [_place(dqe[2 * pb + e], pb // 2, e) for e in range(2)]
            dqs.append(jnp.where(lane < A_HEAD_DIM, halves[0], halves[1]))
            for e in range(2):
                dsk = jnp.where(head8 == 2 * pb + e, dsink_h[2 * pb + e], dsk)
        dq = jnp.concatenate(dqs, axis=1) * (A_HEAD_DIM ** -0.5)
        dq_ref[...] = _rope_t(dq, cq4, sq4)
        dkp_ref[...] = _rope_t(dkk[:WINDOW], cp, sps)
        dkc_ref[...] = _rope_t(dkk[WINDOW:], cq, sq)
        dvp_ref[...] = dvv[:WINDOW]
        dvc_ref[...] = dvv[WINDOW:]

        @pl.when(n == 0)
        def _():
            dsink_ref[...] = dsk

        @pl.when(n > 0)
        def _():
            dsink_ref[...] += dsk

    return pl.pallas_call(
        body, name=name, grid=(T // WINDOW,),
        in_specs=[sp["q"], sp["kc"], sp["kp"], sp["vc"], sp["vp"], sp["tq"], sp["tq"], sp["tp"], sp["tp"], sp["sink"],
                  sp["row512"], sp["lse"], sp["row512"]],
        out_specs=[sp["row512"], sp["row128"], sp["row128"], sp["row128"], sp["row128"],
                   pl.BlockSpec((1, A_Q_HEADS), lambda n: (0, 0))],
        out_shape=[jax.ShapeDtypeStruct((T, A_Q_W), F32)] + [jax.ShapeDtypeStruct((T, LANES), F32)] * 4
        + [jax.ShapeDtypeStruct((1, A_Q_HEADS), F32)],
        compiler_params=_cparams("arbitrary"),
    )(proj, proj, proj, proj, proj, cos, sin_s, cos, sin_s, sinks_b, o, lse, dmix)


def _shift_down(x, prev8, k):
    if k == 0:
        return x
    row = lax.broadcasted_iota(jnp.int32, prev8.shape, 0)
    r = pltpu.roll(x, k, axis=0)
    top = jnp.where(row < k, pltpu.roll(prev8, k, axis=0), r[:SUBLANES])
    return jnp.concatenate([top, r[SUBLANES:]], axis=0)


def _shift_up(x, next8, k):
    if k == 0:
        return x
    R = x.shape[0]
    row = lax.broadcasted_iota(jnp.int32, next8.shape, 0)
    r = pltpu.roll(x, R - k, axis=0)
    bot = jnp.where(row >= SUBLANES - k, pltpu.roll(next8, SUBLANES - k, axis=0), r[R - SUBLANES:])
    return jnp.concatenate([r[:R - SUBLANES], bot], axis=0)


def _conv(x, prev8, w):
    y = x * w[CONV_K - 1:CONV_K]
    for j in range(CONV_K - 1):
        y = y + _shift_down(x, prev8, CONV_K - 1 - j) * w[j:j + 1]
    return y


def _conv_bwd(x, prev8, w, dy, next8_dy):
    dx = dy * w[CONV_K - 1:CONV_K]
    dws = []
    for j in range(CONV_K - 1):
        k = CONV_K - 1 - j
        dx = dx + _shift_up(dy, next8_dy, k) * w[j:j + 1]
        dws.append(jnp.sum(dy * _shift_down(x, prev8, k), axis=0, keepdims=True))
    dws.append(jnp.sum(dy * x, axis=0, keepdims=True))
    return dx, dws


def _dnconv_fwd(proj, conv_w, name):
    T = proj.shape[0]
    R = _tile(T, 512)
    cb0 = HP_QKVB // A_Q_W

    def body(x_ref, w_ref, o_ref, prev_ref):
        i = pl.program_id(1)

        @pl.when(i == 0)
        def _():
            prev_ref[...] = jnp.zeros_like(prev_ref)

        x = x_ref[...]
        o_ref[...] = _silu(_conv(x, prev_ref[...], w_ref[...]))
        prev_ref[...] = x[R - SUBLANES:]

    return pl.pallas_call(
        body, name=name, grid=(3, T // R),
        in_specs=[pl.BlockSpec((R, B_W), lambda j, i: (i, cb0 + j)), pl.BlockSpec((CONV_K, B_W), lambda j, i: (0, j))],
        out_specs=pl.BlockSpec((R, B_W), lambda j, i: (i, j)),
        out_shape=jax.ShapeDtypeStruct((T, 3 * B_W), F32),
        scratch_shapes=[pltpu.VMEM((SUBLANES, B_W), F32)],
        compiler_params=_cparams("parallel", "arbitrary"),
    )(proj, conv_w)


def _dnconv_bwd(proj, conv_w, dy, name):
    T = proj.shape[0]
    R = _tile(T, 512)
    nb = T // R
    cb0 = HP_QKVB // A_Q_W
    r8 = R // SUBLANES

    def body(x_ref, xp_ref, w_ref, dy_ref, dx_ref, dw_ref, next_ref):
        i = pl.program_id(1)
        blk = nb - 1 - i

        @pl.when(i == 0)
        def _():
            next_ref[...] = jnp.zeros_like(next_ref)

        x = x_ref[...]
        prev8 = jnp.where(blk > 0, xp_ref[...], 0.0)
        w = w_ref[...]
        dpre = dy_ref[...] * _dsilu(_conv(x, prev8, w))
        dx, dw = _conv_bwd(x, prev8, w, dpre, next_ref[...])
        dx_ref[...] = dx
        next_ref[...] = dpre[:SUBLANES]

        @pl.when(i == 0)
        def _():
            for j in range(CONV_K):
                dw_ref[j:j + 1, :] = dw[j]

        @pl.when(i > 0)
        def _():
            for j in range(CONV_K):
                dw_ref[j:j + 1, :] += dw[j]

    return pl.pallas_call(
        body, name=name, grid=(3, nb),
        in_specs=[pl.BlockSpec((R, B_W), lambda j, i: (nb - 1 - i, cb0 + j)),
                  pl.BlockSpec((SUBLANES, B_W), lambda j, i: (jnp.maximum((nb - 1 - i) * r8 - 1, 0), cb0 + j)),
                  pl.BlockSpec((CONV_K, B_W), lambda j, i: (0, j)),
                  pl.BlockSpec((R, B_W), lambda j, i: (nb - 1 - i, j))],
        out_specs=[pl.BlockSpec((R, B_W), lambda j, i: (nb - 1 - i, j)),
                   pl.BlockSpec((CONV_K, B_W), lambda j, i: (0, j))],
        out_shape=[jax.ShapeDtypeStruct((T, 3 * B_W), F32), jax.ShapeDtypeStruct((CONV_K, 3 * B_W), F32)],
        scratch_shapes=[pltpu.VMEM((SUBLANES, B_W), F32)],
        compiler_params=_cparams("parallel", "arbitrary"),
    )(proj, proj, conv_w, dy)


DK_SCALE = B_HEAD_DIM ** -0.5


BNN = (((2,), (1,)), ((0,), (0,)))
BNT = (((2,), (2,)), ((0,), (0,)))
BTN = (((1,), (1,)), ((0,), (0,)))


def _tri_inv(a):
    C = a.shape[-1]
    ri = lax.broadcasted_iota(jnp.int32, (C, C), 0)
    ci = lax.broadcasted_iota(jnp.int32, (C, C), 1)
    x = jnp.where(ri == ci, 1.0, 0.0)[None] - a
    p = _dotf(a, a, BNN)
    span = 2
    while span < C:
        x = x + _dotf(x, p, BNN)
        span *= 2
        if span < C:
            p = _dotf(p, p, BNN)
    return x


def _dn_chunk(qc, kc, v, gcol, grow, bcol, s0, tm=None):
    C = B_CHUNK
    ri = lax.broadcasted_iota(jnp.int32, (C, C), 0)
    ci = lax.broadcasted_iota(jnp.int32, (C, C), 1)
    incl, strict = (ri >= ci)[None], (ri > ci)[None]
    rq = lax.rsqrt(jnp.sum(qc * qc, axis=-1, keepdims=True) + NORM_EPS)
    rk = lax.rsqrt(jnp.sum(kc * kc, axis=-1, keepdims=True) + NORM_EPS)
    qn = qc * rq
    q = qn * DK_SCALE
    k = kc * rk
    gc_col = jnp.sum(jnp.where(incl, grow, 0.0), axis=2, keepdims=True)
    gc_row = jnp.sum(jnp.where((ri <= ci)[None], gcol, 0.0), axis=1, keepdims=True)
    gl = jnp.sum(gcol, axis=1, keepdims=True)
    dincl = jnp.where(incl, jnp.exp(jnp.where(incl, gc_col - gc_row, 0.0)), 0.0)
    dstrict = jnp.where(strict, dincl, 0.0)
    eg = jnp.exp(gc_col)
    ekt = jnp.exp(gl - gc_col)
    egl = jnp.exp(gl)
    kb = k * bcol
    vb = v * bcol
    kbg = kb * eg
    a = _dot(kb, k, BNT) * dstrict
    if tm is None:
        tm = _tri_inv(a)
    u = _dot(tm, vb, BNN)
    w = _dot(tm, kbg, BNN)
    vn = u - _dot(w, s0, BNN)
    qk = _dot(q, k, BNT) * dincl
    qg = q * eg
    kt = k * ekt
    o = _dot(qg, s0, BNN) + _dot(qk, vn, BNN)
    s1 = s0 * egl + _dot(kt, vn, BTN)
    return dict(rq=rq, rk=rk, qn=qn, q=q, k=k, dincl=dincl, dstrict=dstrict, eg=eg, ekt=ekt, egl=egl, kb=kb, vb=vb,
                kbg=kbg, a=a, tm=tm, w=w, vn=vn, qk=qk, qg=qg, kt=kt, o=o, s1=s1, ri=ri[None], ci=ci[None])


def _heads(ref):
    return jnp.stack([ref[:, h * B_HEAD_DIM:(h + 1) * B_HEAD_DIM] for h in range(B_HEADS)])


def _store_heads(ref, val):
    for h in range(B_HEADS):
        ref[:, h * B_HEAD_DIM:(h + 1) * B_HEAD_DIM] = val[h]


def _dn_specs(N, rev):
    ix = (lambda n: N - 1 - n) if rev else (lambda n: n)
    wide = lambda cb: pl.BlockSpec((B_CHUNK, B_W), lambda n: (ix(n), cb))
    return dict(
        q=wide(0), k=wide(1), v=wide(2), z=wide(HP_Z // B_W), dob=wide(A_Q_W // B_W), out=wide(0),
        nw=pl.BlockSpec((1, LANES), lambda n: (0, 0)),
        row=pl.BlockSpec((B_HEADS, None, 1, B_CHUNK), lambda n: (0, ix(n), 0, 0)),
        state=pl.BlockSpec((B_HEADS, None, B_HEAD_DIM, B_HEAD_DIM), lambda n: (0, ix(n), 0, 0)),
        inv=pl.BlockSpec((B_HEADS, None, B_CHUNK, B_CHUNK), lambda n: (0, ix(n), 0, 0)),
    )


def _to_col(row):
    C = row.shape[-1]
    eye = lax.broadcasted_iota(jnp.int32, (C, C), 0) == lax.broadcasted_iota(jnp.int32, (C, C), 1)
    return jnp.sum(jnp.where(eye[None], row, 0.0), axis=2, keepdims=True)


def _to_row(col):
    C = col.shape[1]
    eye = lax.broadcasted_iota(jnp.int32, (C, C), 0) == lax.broadcasted_iota(jnp.int32, (C, C), 1)
    return jnp.sum(jnp.where(eye[None], col, 0.0), axis=1, keepdims=True)


def _dn_fwd(qkvc, proj, norm_w, grow, brow, name):
    T = qkvc.shape[0]
    N = T // B_CHUNK
    sp = _dn_specs(N, False)

    def body(q_ref, k_ref, v_ref, z_ref, nw_ref, gr_ref, br_ref, o_ref, st_ref, tm_ref, s_ref):
        n = pl.program_id(0)

        @pl.when(n == 0)
        def _():
            s_ref[...] = jnp.zeros_like(s_ref)

        s0 = s_ref[...]
        st_ref[...] = s0
        grow_v = gr_ref[...]
        f = _dn_chunk(_heads(q_ref), _heads(k_ref), _heads(v_ref), _to_col(grow_v), grow_v, _to_col(br_ref[...]), s0)
        o = f["o"]
        r = lax.rsqrt(jnp.mean(o * o, axis=-1, keepdims=True) + NORM_EPS)
        _store_heads(o_ref, o * r * nw_ref[...][None] * _silu(_heads(z_ref)))
        s_ref[...] = f["s1"]
        tm_ref[...] = f["tm"]

    return pl.pallas_call(
        body, name=name, grid=(N,),
        in_specs=[sp["q"], sp["k"], sp["v"], sp["z"], sp["nw"], sp["row"], sp["row"]],
        out_specs=[sp["out"], sp["state"], sp["inv"]],
        out_shape=[jax.ShapeDtypeStruct((T, B_W), F32),
                   jax.ShapeDtypeStruct((B_HEADS, N, B_HEAD_DIM, B_HEAD_DIM), F32),
                   jax.ShapeDtypeStruct((B_HEADS, N, B_CHUNK, B_CHUNK), F32)],
        scratch_shapes=[pltpu.VMEM((B_HEADS, B_HEAD_DIM, B_HEAD_DIM), F32)],
        compiler_params=_cparams("arbitrary"),
    )(qkvc, qkvc, qkvc, proj, norm_w, grow, brow)


def _dn_bwd(qkvc, proj, norm_w, grow, brow, states, invs, dmix, name):
    T = qkvc.shape[0]
    N = T // B_CHUNK
    sp = _dn_specs(N, True)
    C = B_CHUNK

    def body(q_ref, k_ref, v_ref, z_ref, nw_ref, gr_ref, br_ref, st_ref, tm_ref, dob_ref,
             dq_ref, dk_ref, dv_ref, dz_ref, dg_ref, db_ref, dnw_ref, ds_ref):
        n = pl.program_id(0)

        @pl.when(n == 0)
        def _():
            ds_ref[...] = jnp.zeros_like(ds_ref)
            dnw_ref[...] = jnp.zeros_like(dnw_ref)

        s0 = st_ref[...]
        ds1 = ds_ref[...]
        v, z, nw, bcol_v = _heads(v_ref), _heads(z_ref), nw_ref[...][None], _to_col(br_ref[...])
        grow_v = gr_ref[...]
        f = _dn_chunk(_heads(q_ref), _heads(k_ref), v, _to_col(grow_v), grow_v, bcol_v, s0, tm=tm_ref[...])
        o, q, k, qn = f["o"], f["q"], f["k"], f["qn"]
        eg, ekt, egl = f["eg"], f["ekt"], f["egl"]
        tm, w, vn, kb, vb, kbg = f["tm"], f["w"], f["vn"], f["kb"], f["vb"], f["kbg"]
        qg, kt, qk, a = f["qg"], f["kt"], f["qk"], f["a"]
        ri, ci = f["ri"], f["ci"]

        dob_v = _heads(dob_ref)
        r = lax.rsqrt(jnp.mean(o * o, axis=-1, keepdims=True) + NORM_EPS)
        sz = _silu(z)
        on = o * r
        dnw_ref[...] += jnp.sum(dob_v * sz * on, axis=1, keepdims=True)
        _store_heads(dz_ref, dob_v * on * nw * _dsilu(z))
        d_on = dob_v * sz * nw
        do = r * (d_on - on * jnp.mean(d_on * on, axis=-1, keepdims=True))

        dvn = _dot(qk, do, BTN) + _dot(kt, ds1, BNN)
        dqk = _dot(do, vn, BNT)
        dqg = _dot(do, s0, BNT)
        ds_ref[...] = _dot(qg, do, BTN) + egl * ds1 - _dot(w, dvn, BTN)
        dgl = jnp.sum(s0 * ds1, axis=(1, 2), keepdims=True) * egl
        dkt = _dot(vn, ds1, BNT)
        dw = -_dot(dvn, s0, BNT)
        dq = dqg * eg
        dgc = jnp.sum(dqg * qg, axis=-1, keepdims=True)
        dk = dkt * ekt
        t_kt = jnp.sum(dkt * kt, axis=-1, keepdims=True)
        dgl = dgl + jnp.sum(t_kt, axis=1, keepdims=True)
        dgc = dgc - t_kt
        dqkr = dqk * f["dincl"]
        dq = dq + _dot(dqkr, k, BNN)
        dk = dk + _dot(dqkr, q, BTN)
        e_qk = dqk * qk
        dgc = dgc + jnp.sum(e_qk, axis=-1, keepdims=True)
        dgc_row = -jnp.sum(e_qk, axis=1, keepdims=True)
        dtm = _dot(dvn, vb, BNT) + _dot(dw, kbg, BNT)
        dvb = _dot(tm, dvn, BTN)
        dkbg = _dot(tm, dw, BTN)
        dkb = dkbg * eg
        dgc = dgc + jnp.sum(dkbg * kbg, axis=-1, keepdims=True)
        da = -_dotf(tm, _dotf(dtm, tm, BNT), BTN)
        dkk = da * f["dstrict"]
        e_a = da * a
        dgc = dgc + jnp.sum(e_a, axis=-1, keepdims=True)
        dgc_row = dgc_row - jnp.sum(e_a, axis=1, keepdims=True)
        dkb = dkb + _dot(dkk, k, BNN)
        dk = dk + _dot(dkk, kb, BTN)
        dk = dk + dkb * bcol_v
        db_ref[...] = _to_row(jnp.sum(dkb * k, axis=-1, keepdims=True) + jnp.sum(dvb * v, axis=-1, keepdims=True))
        _store_heads(dv_ref, dvb * bcol_v)
        dgc_row = dgc_row + jnp.sum(jnp.where(ri == ci, dgc, 0.0), axis=1, keepdims=True)
        dg_ref[...] = jnp.sum(jnp.where(ci <= ri, _to_col(dgc_row), 0.0), axis=1, keepdims=True) + dgl
        dqs = dq * DK_SCALE
        _store_heads(dq_ref, f["rq"] * (dqs - qn * jnp.sum(dqs * qn, axis=-1, keepdims=True)))
        _store_heads(dk_ref, f["rk"] * (dk - k * jnp.sum(dk * k, axis=-1, keepdims=True)))

    return pl.pallas_call(
        body, name=name, grid=(N,),
        in_specs=[sp["q"], sp["k"], sp["v"], sp["z"], sp["nw"], sp["row"], sp["row"], sp["state"], sp["inv"], sp["dob"]],
        out_specs=[sp["out"], sp["out"], sp["out"], sp["out"], sp["row"], sp["row"],
                   pl.BlockSpec((B_HEADS, 1, LANES), lambda n: (0, 0, 0))],
        out_shape=[jax.ShapeDtypeStruct((T, B_W), F32)] * 4
        + [jax.ShapeDtypeStruct((B_HEADS, N, 1, C), F32), jax.ShapeDtypeStruct((B_HEADS, N, 1, C), F32),
           jax.ShapeDtypeStruct((B_HEADS, 1, LANES), F32)],
        scratch_shapes=[pltpu.VMEM((B_HEADS, B_HEAD_DIM, B_HEAD_DIM), F32)],
        compiler_params=_cparams("arbitrary"),
    )(qkvc, qkvc, qkvc, proj, norm_w, grow, brow, states, invs, dmix)


def _lru_gates(xc, wa_ref, wx_ref, ba, bx, sp):
    pre_r, pre_i = [], []
    for hb in range(LRU_BLOCKS):
        xb = xc[:, hb * LRU_BLOCK_W:(hb + 1) * LRU_BLOCK_W]
        pre_r.append(_dot(xb, wa_ref[hb]))
        pre_i.append(_dot(xb, wx_ref[hb]))
    r = _sigmoid(jnp.concatenate(pre_r, axis=1) + ba)
    i = _sigmoid(jnp.concatenate(pre_i, axis=1) + bx)
    la = -LRU_C * r * sp
    a = jnp.exp(la)
    th = jnp.tanh(la)
    s = jnp.sqrt(-2.0 * th / (1.0 - th))
    return r, i, a, s


def _scan_down(a, b):
    R = a.shape[0]
    row = lax.broadcasted_iota(jnp.int32, a.shape, 0)
    d = 1
    while d < R:
        ok = row >= d
        b = a * jnp.where(ok, pltpu.roll(b, d, axis=0), 0.0) + b
        a = a * jnp.where(ok, pltpu.roll(a, d, axis=0), 1.0)
        d *= 2
    return a, b


def _scan_up(a, b):
    R = a.shape[0]
    row = lax.broadcasted_iota(jnp.int32, a.shape, 0)
    d = 1
    while d < R:
        ok = row < R - d
        b = a * jnp.where(ok, pltpu.roll(b, R - d, axis=0), 0.0) + b
        a = a * jnp.where(ok, pltpu.roll(a, R - d, axis=0), 1.0)
        d *= 2
    return b


def _rglru_fwd(proj, conv_w, conv_b, wa, wx, ba, bx, sp, name):
    T = proj.shape[0]
    R = _tile(T, 256)
    W = D_MODEL

    def body(p_ref, cw_ref, cb_ref, wa_ref, wx_ref, ba_ref, bx_ref, sp_ref, hg_ref, h_ref, prev_ref, hc_ref):
        i = pl.program_id(0)

        @pl.when(i == 0)
        def _():
            prev_ref[...] = jnp.zeros_like(prev_ref)
            hc_ref[...] = jnp.zeros_like(hc_ref)

        xr = p_ref[:, :W]
        gate = p_ref[:, W:]
        xc = _conv(xr, prev_ref[...], cw_ref[...]) + cb_ref[...]
        prev_ref[...] = xr[R - SUBLANES:]
        r, ig, a, s = _lru_gates(xc, wa_ref, wx_ref, ba_ref[...], bx_ref[...], sp_ref[...])
        pa, hb = _scan_down(a, s * ig * xc)
        h = hb + pa * hc_ref[SUBLANES - 1:SUBLANES, :]
        h_ref[...] = h
        hg_ref[...] = (h * _gelu(gate)).astype(ACT_DTYPE)
        hc_ref[...] = h[R - SUBLANES:]

    vec = pl.BlockSpec((1, W), lambda i: (0, 0))
    wsp = pl.BlockSpec((LRU_BLOCKS, LRU_BLOCK_W, LRU_BLOCK_W), lambda i: (0, 0, 0))
    row = pl.BlockSpec((R, W), lambda i: (i, 0))
    return pl.pallas_call(
        body, name=name, grid=(T // R,),
        in_specs=[pl.BlockSpec((R, 2 * W), lambda i: (i, 0)), pl.BlockSpec((CONV_K, W), lambda i: (0, 0)),
                  vec, wsp, wsp, vec, vec, vec],
        out_specs=[row, row],
        out_shape=[jax.ShapeDtypeStruct((T, W), ACT_DTYPE), jax.ShapeDtypeStruct((T, W), F32)],
        scratch_shapes=[pltpu.VMEM((SUBLANES, W), F32), pltpu.VMEM((SUBLANES, W), F32)],
        compiler_params=_cparams("arbitrary"),
    )(proj, conv_w, conv_b, wa, wx, ba, bx, sp)


def _rglru_bwd(proj, conv_w, conv_b, wa, wx, ba, bx, sp, h, dhg, name):
    T = proj.shape[0]
    R = _tile(T, 256)
    nb = T // R
    r8 = R // SUBLANES
    W = D_MODEL

    def body(p_ref, pp_ref, cw_ref, cb_ref, wa_ref, wx_ref, ba_ref, bx_ref, sp_ref, h_ref, hp_ref, dhg_ref,
             dp_ref, dcw_ref, dcb_ref, dwa_ref, dwx_ref, dba_ref, dbx_ref, dsp_ref, lam_ref, nxt_ref):
        step = pl.program_id(0)
        blk = nb - 1 - step

        @pl.when(step == 0)
        def _():
            lam_ref[...] = jnp.zeros_like(lam_ref)
            nxt_ref[...] = jnp.zeros_like(nxt_ref)

        xr = p_ref[:, :W]
        gate = p_ref[:, W:]
        first = blk > 0
        prev8 = jnp.where(first, pp_ref[:, :W], 0.0)
        hprev8 = jnp.where(first, hp_ref[...], 0.0)
        cw = cw_ref[...]
        spv = sp_ref[...]
        xc = _conv(xr, prev8, cw) + cb_ref[...]
        r, ig, a, s = _lru_gates(xc, wa_ref, wx_ref, ba_ref[...], bx_ref[...], spv)
        hv = h_ref[...]
        dhg_v = dhg_ref[...]
        dgate = dhg_v * hv * _dgelu(gate)
        dh = dhg_v * _gelu(gate)
        row = lax.broadcasted_iota(jnp.int32, (R, W), 0)
        last = row == R - 1
        a_up = jnp.where(last, 0.0, pltpu.roll(a, R - 1, axis=0))
        lam = _scan_up(a_up, dh + jnp.where(last, lam_ref[0:1, :], 0.0))
        lam_ref[...] = (a * lam)[:SUBLANES]
        h_dn = _shift_down(hv, hprev8, 1)
        da = lam * h_dn
        bx_in = ig * xc
        dsv = lam * bx_in
        dig = lam * s * xc
        dxc = lam * s * ig
        dla = da * a - dsv * (a * a) / s
        dr = dla * (-LRU_C) * spv
        dsp = jnp.sum(dla * (-LRU_C) * r, axis=0, keepdims=True)
        dpr = dr * r * (1.0 - r)
        dpi = dig * ig * (1.0 - ig)
        dxc_parts, dwa_parts, dwx_parts = [], [], []
        for hb in range(LRU_BLOCKS):
            sl = slice(hb * LRU_BLOCK_W, (hb + 1) * LRU_BLOCK_W)
            xb, gr, gi = xc[:, sl], dpr[:, sl], dpi[:, sl]
            dxc_parts.append(_dot(gr, wa_ref[hb], NT) + _dot(gi, wx_ref[hb], NT))
            dwa_parts.append(_dot(xb, gr, TN))
            dwx_parts.append(_dot(xb, gi, TN))
        dxc = dxc + jnp.concatenate(dxc_parts, axis=1)
        dxr, dcw = _conv_bwd(xr, prev8, cw, dxc, nxt_ref[...])
        nxt_ref[...] = dxc[:SUBLANES]
        dp_ref[:, :W] = dxr.astype(ACT_DTYPE)
        dp_ref[:, W:] = dgate.astype(ACT_DTYPE)
        dcb = jnp.sum(dxc, axis=0, keepdims=True)
        dba = jnp.sum(dpr, axis=0, keepdims=True)
        dbx = jnp.sum(dpi, axis=0, keepdims=True)

        @pl.when(step == 0)
        def _():
            for j in range(CONV_K):
                dcw_ref[j:j + 1, :] = dcw[j]
            dcb_ref[...] = dcb
            dba_ref[...] = dba
            dbx_ref[...] = dbx
            dsp_ref[...] = dsp
            for hb in range(LRU_BLOCKS):
                dwa_ref[hb] = dwa_parts[hb]
                dwx_ref[hb] = dwx_parts[hb]

        @pl.when(step > 0)
        def _():
            for j in range(CONV_K):
                dcw_ref[j:j + 1, :] += dcw[j]
            dcb_ref[...] += dcb
            dba_ref[...] += dba
            dbx_ref[...] += dbx
            dsp_ref[...] += dsp
            for hb in range(LRU_BLOCKS):
                dwa_ref[hb] += dwa_parts[hb]
                dwx_ref[hb] += dwx_parts[hb]

    rv = lambda i: nb - 1 - i
    before = lambda i: jnp.maximum((nb - 1 - i) * r8 - 1, 0)
    vec = pl.BlockSpec((1, W), lambda i: (0, 0))
    cws = pl.BlockSpec((CONV_K, W), lambda i: (0, 0))
    wsp = pl.BlockSpec((LRU_BLOCKS, LRU_BLOCK_W, LRU_BLOCK_W), lambda i: (0, 0, 0))
    row = pl.BlockSpec((R, W), lambda i: (rv(i), 0))
    wshape = jax.ShapeDtypeStruct((LRU_BLOCKS, LRU_BLOCK_W, LRU_BLOCK_W), F32)
    vshape = jax.ShapeDtypeStruct((1, W), F32)
    return pl.pallas_call(
        body, name=name, grid=(nb,),
        in_specs=[pl.BlockSpec((R, 2 * W), lambda i: (rv(i), 0)), pl.BlockSpec((SUBLANES, 2 * W), lambda i: (before(i), 0)),
                  cws, vec, wsp, wsp, vec, vec, vec, row, pl.BlockSpec((SUBLANES, W), lambda i: (before(i), 0)), row],
        out_specs=[pl.BlockSpec((R, 2 * W), lambda i: (rv(i), 0)), cws, vec, wsp, wsp, vec, vec, vec],
        out_shape=[jax.ShapeDtypeStruct((T, 2 * W), ACT_DTYPE), jax.ShapeDtypeStruct((CONV_K, W), F32), vshape,
                   wshape, wshape, vshape, vshape, vshape],
        scratch_shapes=[pltpu.VMEM((SUBLANES, W), F32), pltpu.VMEM((SUBLANES, W), F32)],
        compiler_params=_cparams("arbitrary"),
    )(proj, proj, conv_w, conv_b, wa, wx, ba, bx, sp, h, h, dhg)


MESH = pl.DeviceIdType.MESH
ANY = pl.BlockSpec(memory_space=pl.ANY)


def _position():
    x, y, c = lax.axis_index("x"), lax.axis_index("y"), lax.axis_index("c")
    other_chips = [(1 - x, y), (x, 1 - y), (1 - x, 1 - y)]
    return x, y, c, other_chips


def _all_gather_weights(shards, name):
    n = len(shards)

    def body(*refs):
        ins, outs = refs[:n], refs[n:2 * n]
        send_sems, recv_sems = refs[2 * n:]
        x, y, c, chips = _position()
        me = 2 * x + y
        sibling = (x, y, 1 - c)

        def rcopy(t, k, src, dst, to):
            return pltpu.make_async_remote_copy(src_ref=src, dst_ref=dst, send_sem=send_sems.at[t, k],
                                                recv_sem=recv_sems.at[t, k], device_id=to, device_id_type=MESH)

        started = []
        for t in range(n):
            for j, (cx, cy) in enumerate(chips):
                cp = rcopy(t, j, ins[t].at[c], outs[t].at[me, c], (cx, cy, c))
                cp.start()
                started.append(cp)
        for t in range(n):
            for j, (cx, cy) in enumerate(chips):
                blk = outs[t].at[2 * cx + cy, c]
                rcopy(t, j, blk, blk, (cx, cy, c)).wait_recv()
                cp = rcopy(t, 3 + j, blk, blk, sibling)
                cp.start()
                started.append(cp)
        for t in range(n):
            for j, (cx, cy) in enumerate(chips):
                blk = outs[t].at[2 * cx + cy, 1 - c]
                rcopy(t, 3 + j, blk, blk, sibling).wait_recv()
        for cp in started:
            cp.wait_send()

    return pl.pallas_call(
        body, name=name, in_specs=[ANY] * n, out_specs=[ANY] * n,
        out_shape=[jax.ShapeDtypeStruct((N_CHIPS,) + s.shape, s.dtype) for s in shards],
        scratch_shapes=[pltpu.SemaphoreType.DMA((n, 6)), pltpu.SemaphoreType.DMA((n, 6))],
    )(*shards)


HBM = pl.BlockSpec(memory_space=pltpu.HBM)
SEM = pl.BlockSpec(memory_space=pltpu.SEMAPHORE)
EFFECT = pltpu.SideEffectType.DATAFLOW_SIDE_EFFECTING


def _gather_start(own, land, after, name):
    n = len(own)
    L = own[0].shape[0]
    n_sem = n * L

    def body(*refs):
        own_refs, land_refs = refs[:n], refs[n:2 * n]
        send_sems = refs[2 * n + 1:2 * n + 1 + n_sem]
        recv_sems = refs[2 * n + 1 + n_sem:2 * n + 1 + 2 * n_sem]
        x, y, c, chips = _position()
        me = 2 * x + y
        for layer in range(L):
            for t in range(n):
                for cx, cy in chips:
                    pltpu.make_async_remote_copy(
                        src_ref=own_refs[t].at[layer], dst_ref=land_refs[t].at[me, layer],
                        send_sem=send_sems[t * L + layer], recv_sem=recv_sems[t * L + layer],
                        device_id=(cx, cy, c), device_id_type=MESH).start()

    sems = (pltpu.SemaphoreType.DMA(()),) * (2 * n_sem)
    thru = [pltpu.HBM(a.shape, a.dtype) for a in list(own) + list(land)]
    out = pl.pallas_call(
        body, name=name, out_shape=(*sems, *thru),
        in_specs=[HBM] * (2 * n) + [pl.BlockSpec(memory_space=pl.ANY)], out_specs=(SEM,) * (2 * n_sem) + (HBM,) * (2 * n),
        input_output_aliases={i: 2 * n_sem + i for i in range(2 * n)},
        compiler_params=pltpu.CompilerParams(has_side_effects=EFFECT),
    )(*[pltpu.with_memory_space_constraint(a, pltpu.HBM) for a in list(own) + list(land)], after)
    send_sems, recv_sems, thru = out[:n_sem], out[n_sem:2 * n_sem], out[2 * n_sem:]
    return ([send_sems[t * L:(t + 1) * L] for t in range(n)], [recv_sems[t * L:(t + 1) * L] for t in range(n)],
            list(thru[:n]), list(thru[n:]))


def _gather_wait(send_sems, recv_sems, own, land, layer, after, name):
    n = len(own)

    def body(*refs):
        land_refs = refs[n:2 * n]
        s_sems, r_sems = refs[2 * n:3 * n], refs[3 * n:4 * n]
        x, y, c, _ = _position()
        for t in range(n):
            three = land_refs[t].at[pl.ds(0, N_CHIPS - 1), layer]
            cp = pltpu.make_async_remote_copy(src_ref=three, dst_ref=three, send_sem=s_sems[t], recv_sem=r_sems[t],
                                              device_id=(x, y, c), device_id_type=MESH)
            cp.wait_send()
            cp.wait_recv()

    thru = [pltpu.HBM(a.shape, a.dtype) for a in list(own) + list(land)]
    out = pl.pallas_call(
        body, name=name, out_shape=tuple(thru),
        in_specs=[HBM] * (2 * n) + [SEM] * (2 * n) + [pl.BlockSpec(memory_space=pl.ANY)], out_specs=(HBM,) * (2 * n),
        input_output_aliases={i: i for i in range(2 * n)},
        compiler_params=pltpu.CompilerParams(has_side_effects=EFFECT),
    )(*own, *land, *[s[layer] for s in send_sems], *[r[layer] for r in recv_sems], after)
    return list(out[:n]), list(out[n:])


def _rs_to_sibling(grads, name):
    n = len(grads)

    def body(*refs):
        ins, outs = refs[:n], refs[n:2 * n]
        send_sems, recv_sems = refs[2 * n:]
        x, y, c, _ = _position()
        cps = [pltpu.make_async_remote_copy(src_ref=ins[t].at[:, 1 - c], dst_ref=outs[t], send_sem=send_sems.at[t],
                                            recv_sem=recv_sems.at[t], device_id=(x, y, 1 - c), device_id_type=MESH)
               for t in range(n)]
        for cp in cps:
            cp.start()
        for cp in cps:
            cp.wait()

    return pl.pallas_call(
        body, name=name, in_specs=[ANY] * n, out_specs=[ANY] * n,
        out_shape=[jax.ShapeDtypeStruct((N_CHIPS,) + g.shape[2:], g.dtype) for g in grads],
        scratch_shapes=[pltpu.SemaphoreType.DMA((n,)), pltpu.SemaphoreType.DMA((n,))],
    )(*grads)


def _rs_across_chips(parts, name):
    n = len(parts)

    def body(*refs):
        ins, outs = refs[:n], refs[n:2 * n]
        send_sems, recv_sems = refs[2 * n:]
        x, y, c, chips = _position()
        me = 2 * x + y
        cps = []
        for t in range(n):
            for j, (cx, cy) in enumerate(chips):
                cps.append(pltpu.make_async_remote_copy(
                    src_ref=ins[t].at[2 * cx + cy], dst_ref=outs[t].at[me], send_sem=send_sems.at[t, j],
                    recv_sem=recv_sems.at[t, j], device_id=(cx, cy, c), device_id_type=MESH))
        for cp in cps:
            cp.start()
        for t in range(n):
            for j, (cx, cy) in enumerate(chips):
                blk = outs[t].at[2 * cx + cy]
                pltpu.make_async_remote_copy(src_ref=blk, dst_ref=blk, send_sem=send_sems.at[t, j],
                                             recv_sem=recv_sems.at[t, j], device_id=(cx, cy, c),
                                             device_id_type=MESH).wait_recv()
        for cp in cps:
            cp.wait_send()

    return pl.pallas_call(
        body, name=name, in_specs=[ANY] * n, out_specs=[ANY] * n,
        out_shape=[jax.ShapeDtypeStruct(p.shape, p.dtype) for p in parts],
        scratch_shapes=[pltpu.SemaphoreType.DMA((n, 3)), pltpu.SemaphoreType.DMA((n, 3))],
    )(*parts)


def _rs_across_start(parts, recv, name):
    n = len(parts)

    def body(*refs):
        part_refs, recv_refs = refs[:n], refs[n:2 * n]
        send_sems, recv_sems = refs[2 * n:3 * n], refs[3 * n:4 * n]
        token_ref = refs[-1]
        x, y, c, chips = _position()
        me = 2 * x + y
        for t in range(n):
            for cx, cy in chips:
                pltpu.make_async_remote_copy(src_ref=part_refs[t].at[2 * cx + cy], dst_ref=recv_refs[t].at[me],
                                             send_sem=send_sems[t], recv_sem=recv_sems[t], device_id=(cx, cy, c),
                                             device_id_type=MESH).start()
        token_ref[...] = jnp.zeros_like(token_ref)

    sems = (pltpu.SemaphoreType.DMA(()),) * (2 * n)
    thru = [pltpu.HBM(a.shape, a.dtype) for a in list(parts) + list(recv)]
    out = pl.pallas_call(
        body, name=name, out_shape=(*sems, *thru, jax.ShapeDtypeStruct((SUBLANES, LANES), F32)),
        in_specs=[HBM] * (2 * n), out_specs=(SEM,) * (2 * n) + (HBM,) * (2 * n) + (pl.BlockSpec(memory_space=pltpu.VMEM),),
        input_output_aliases={i: 2 * n + i for i in range(2 * n)},
        compiler_params=pltpu.CompilerParams(has_side_effects=EFFECT),
    )(*[pltpu.with_memory_space_constraint(a, pltpu.HBM) for a in list(parts) + list(recv)])
    return out[:n], out[n:2 * n], list(out[2 * n:3 * n]), list(out[3 * n:4 * n]), out[-1]


def _rs_across_wait(send_sems, recv_sems, parts, recv, after, name):
    n = len(parts)

    def body(*refs):
        recv_refs = refs[n:2 * n]
        s_sems, r_sems = refs[2 * n:3 * n], refs[3 * n:4 * n]
        x, y, c, _ = _position()
        for t in range(n):
            three = recv_refs[t].at[pl.ds(0, N_CHIPS - 1)]
            cp = pltpu.make_async_remote_copy(src_ref=three, dst_ref=three, send_sem=s_sems[t], recv_sem=r_sems[t],
                                              device_id=(x, y, c), device_id_type=MESH)
            cp.wait_send()
            cp.wait_recv()

    thru = [pltpu.HBM(a.shape, a.dtype) for a in list(parts) + list(recv)]
    out = pl.pallas_call(
        body, name=name, out_shape=tuple(thru),
        in_specs=[HBM] * (2 * n) + [SEM] * (2 * n) + [pl.BlockSpec(memory_space=pl.ANY)], out_specs=(HBM,) * (2 * n),
        input_output_aliases={i: i for i in range(2 * n)},
        compiler_params=pltpu.CompilerParams(has_side_effects=EFFECT),
    )(*parts, *recv, *send_sems, *recv_sems, after)
    return list(out[:n]), list(out[n:])


def _rs_join_halves(halves, name):
    n = len(halves)

    def body(*refs):
        ins, outs = refs[:n], refs[n:2 * n]
        send_sems, recv_sems = refs[2 * n:]
        x, y, c, _ = _position()
        cps = [pltpu.make_async_remote_copy(src_ref=ins[t].at[c], dst_ref=outs[t].at[c], send_sem=send_sems.at[t],
                                            recv_sem=recv_sems.at[t], device_id=(x, y, 1 - c), device_id_type=MESH)
               for t in range(n)]
        for cp in cps:
            cp.start()
        for t in range(n):
            blk = outs[t].at[1 - c]
            pltpu.make_async_remote_copy(src_ref=blk, dst_ref=blk, send_sem=send_sems.at[t], recv_sem=recv_sems.at[t],
                                         device_id=(x, y, 1 - c), device_id_type=MESH).wait_recv()
        for cp in cps:
            cp.wait_send()

    return pl.pallas_call(
        body, name=name, in_specs=[ANY] * n, out_specs=[ANY] * n,
        out_shape=[jax.ShapeDtypeStruct(h.shape, h.dtype) for h in halves],
        input_output_aliases={t: t for t in range(n)},
        scratch_shapes=[pltpu.SemaphoreType.DMA((n,)), pltpu.SemaphoreType.DMA((n,))],
    )(*halves)


def _all_gather_small(block, name):
    m_per, n = block.shape

    def body(x_ref, out_ref, send_sems, recv_sems, local_sem):
        x, y, c, chips = _position()
        me, sibling = (x, y, c), (x, y, 1 - c)

        def rows(px, py, pc):
            return out_ref.at[pl.ds((4 * px + 2 * py + pc) * m_per, m_per), :]

        def copy(k, blk, to, src=None):
            return pltpu.make_async_remote_copy(
                src_ref=rows(*blk) if src is None else src, dst_ref=rows(*blk), send_sem=send_sems.at[k],
                recv_sem=recv_sems.at[k], device_id=to, device_id_type=MESH)

        mine = pltpu.make_async_copy(x_ref, rows(*me), local_sem)
        mine.start()
        first = [copy(0, me, sibling, src=x_ref)]
        first += [copy(1 + j, me, (*chip, c), src=x_ref) for j, chip in enumerate(chips)]
        for cp in first:
            cp.start()
        passed = [copy(4 + j, (*chip, c), sibling) for j, chip in enumerate(chips)]
        for j, chip in enumerate(chips):
            copy(1 + j, (*chip, c), me).wait_recv()
            passed[j].start()
        copy(0, sibling, me).wait_recv()
        for j, chip in enumerate(chips):
            copy(4 + j, (*chip, 1 - c), me).wait_recv()
        for cp in first + passed:
            cp.wait_send()
        mine.wait()

    return pl.pallas_call(
        body, name=name, out_shape=jax.ShapeDtypeStruct((N_DEV * m_per, n), block.dtype),
        in_specs=[pl.BlockSpec(memory_space=pltpu.VMEM)], out_specs=pl.BlockSpec(memory_space=pltpu.VMEM),
        scratch_shapes=[pltpu.SemaphoreType.DMA((7,)), pltpu.SemaphoreType.DMA((7,)), pltpu.SemaphoreType.DMA],
    )(block)


def _row_tile(R, n):
    budget = 1 << 19
    if R * n <= budget or R % SUBLANES:
        return R
    t = R
    while t * n > budget and t % (2 * SUBLANES) == 0:
        t //= 2
    return t


def _pair_sum(g, recv, c_arr, name):
    _, _, R, n = g.shape
    tr = _row_tile(R, n)

    def body(c_ref, g_ref, r_ref, o_ref):
        o_ref[...] = (g_ref[...] + r_ref[...]).astype(ICI_DTYPE)

    grid_spec = pltpu.PrefetchScalarGridSpec(
        num_scalar_prefetch=1, grid=(N_CHIPS, R // tr),
        in_specs=[pl.BlockSpec((None, None, tr, n), lambda p, i, c: (p, c[0], i, 0)),
                  pl.BlockSpec((None, tr, n), lambda p, i, c: (p, i, 0))],
        out_specs=pl.BlockSpec((None, tr, n), lambda p, i, c: (p, i, 0)))
    return pl.pallas_call(
        body, name=name, grid_spec=grid_spec, out_shape=jax.ShapeDtypeStruct(recv.shape, ICI_DTYPE),
        compiler_params=_cparams("parallel", "parallel"),
    )(c_arr, g, recv)


def _chip_sum(recv, own, where, name):
    _, R, n = recv.shape
    tr = _row_tile(R, n)

    def body(w_ref, r0, r1, r2, r3, own_ref, o_ref):
        me = w_ref[0]
        terms = [jnp.where(me == k, own_ref[...], r[...]).astype(F32) for k, r in enumerate((r0, r1, r2, r3))]
        o_ref[...] = ((terms[0] + terms[1]) + terms[2]) + terms[3]

    def slot(k):
        return pl.BlockSpec((None, tr, n), lambda i, w: (w[1 + k], i, 0))

    grid_spec = pltpu.PrefetchScalarGridSpec(
        num_scalar_prefetch=1, grid=(R // tr,),
        in_specs=[slot(0), slot(1), slot(2), slot(3), pl.BlockSpec((None, tr, n), lambda i, w: (w[0], i, 0))],
        out_specs=pl.BlockSpec((None, tr, n), lambda i, w: (w[5], i, 0)))
    return pl.pallas_call(
        body, name=name, grid_spec=grid_spec, out_shape=jax.ShapeDtypeStruct((2, R, n), F32),
        compiler_params=_cparams("parallel"),
    )(where, recv, recv, recv, recv, own)


ADAM_C1 = 1.0 / (1.0 - ADAM_B1 ** ADAM_STEP)
ADAM_C2 = 1.0 / (1.0 - ADAM_B2 ** ADAM_STEP)


def _adamw_math(w, g, m, v):
    m = ADAM_B1 * m + (1.0 - ADAM_B1) * g
    v = ADAM_B2 * v + (1.0 - ADAM_B2) * (g * g)
    delta = -ADAM_LR * ((m * ADAM_C1) / (jnp.sqrt(v * ADAM_C2) + ADAM_EPS) + ADAM_WD * w)
    return delta, m, v


def _adamw(w, g, m, v, name):
    R, n = w.shape
    tr = _row_tile(R, n)

    def body(w_ref, g_ref, m_ref, v_ref, d_ref, nm_ref, nv_ref):
        d_ref[...], nm_ref[...], nv_ref[...] = _adamw_math(w_ref[...], g_ref[...], m_ref[...], v_ref[...])

    spec = pl.BlockSpec((tr, n), lambda i: (i, 0))
    return pl.pallas_call(
        body, name=name, grid=(R // tr,), in_specs=[spec] * 4, out_specs=[spec] * 3,
        out_shape=[jax.ShapeDtypeStruct((R, n), F32)] * 3, compiler_params=_cparams("parallel"),
    )(w, g, m, v)


def _adamw_small(w, gall, m, v, name):
    M, n = w.shape

    def body(w_ref, g_ref, m_ref, v_ref, gs_ref, d_ref, nm_ref, nv_ref):
        g = g_ref[0:M, :]
        for d in range(1, N_DEV):
            g = g + g_ref[d * M:(d + 1) * M, :]
        gs_ref[...] = g
        d_ref[...], nm_ref[...], nv_ref[...] = _adamw_math(w_ref[...], g, m_ref[...], v_ref[...])

    return pl.pallas_call(
        body, name=name, out_shape=[jax.ShapeDtypeStruct((M, n), F32)] * 4,
    )(w, gall, m, v)


SMALL_ROWS = 24
MID_ROWS = 4


def _pack_small(ln1_g, ln1_b, ln2_g, ln2_b, norm_w, sinks, a_log, dt_bias):
    mixed = jnp.concatenate([norm_w.reshape(-1), sinks.reshape(-1), a_log.reshape(-1), dt_bias.reshape(-1)])
    mixed = jnp.pad(mixed, (0, D_MODEL - mixed.shape[0]))[None]
    pad = jnp.zeros((SMALL_ROWS - 4 * DEPTH - 1, D_MODEL), F32)
    return jnp.concatenate([ln1_g, ln1_b, ln2_g, ln2_b, mixed, pad], axis=0)


def _unpack_small(p):
    mixed = p[4 * DEPTH]
    return (p[0:4], p[4:8], p[8:12], p[12:16], mixed[0:256].reshape(2, 128), mixed[256:272].reshape(2, 8),
            mixed[272:280].reshape(2, 4), mixed[280:288].reshape(2, 4))


def _pack_mid(conv_w, rconv_w, rconv_b, b_a, b_x, lam):
    lead = conv_w.shape[0]
    flat = jnp.concatenate([conv_w.reshape(lead, -1), rconv_w.reshape(lead, -1), rconv_b, b_a, b_x, lam], axis=1)
    return jnp.pad(flat, ((0, 0), (0, MID_ROWS * D_MODEL - flat.shape[1]))).reshape(lead, MID_ROWS, D_MODEL)


def _unpack_mid(p):
    lead = p.shape[:-2]
    f = p.reshape(lead + (MID_ROWS * D_MODEL,))
    return (f[..., 0:1536].reshape(lead + (4, 384)), f[..., 1536:2560].reshape(lead + (4, 256)),
            f[..., 2560:2816], f[..., 2816:3072], f[..., 3072:3328], f[..., 3328:3584])


def _cols_from_chips(g):
    p, L, R, n = g.shape
    return g.transpose(1, 2, 0, 3).reshape(L, R, p * n)


def _rows_from_chips(g):
    p, L, R, n = g.shape
    return g.transpose(1, 0, 2, 3).reshape(L, p * R, n)


def _cols_to_chips(g):
    L, R, n4 = g.shape
    return g.reshape(L, R, N_CHIPS, n4 // N_CHIPS).transpose(2, 0, 1, 3)


def _rows_to_chips(g):
    L, R4, n = g.shape
    return g.reshape(L, N_CHIPS, R4 // N_CHIPS, n).transpose(1, 0, 2, 3)


def _halves(a):
    return a.reshape(2, -1, a.shape[-1])


def _pad_hyb_cols(w):
    z = jnp.zeros(w.shape[:-1] + (HYB_PAD - HP_BG - 2 * B_HEADS,), w.dtype)
    return jnp.concatenate([w[..., 0:512], w[..., 768:2304], w[..., 2304:2816], w[..., 512:768], w[..., 2816:2824], z], axis=-1)


def _unpad_hyb_cols(w):
    return jnp.concatenate([w[..., 0:512], w[..., 2560:2816], w[..., 512:2048], w[..., 2048:2560], w[..., 2816:2824]], axis=-1)


def _hybrid_fwd(x, W, j, tables, sfx):
    cos, sin_s = tables
    T = x.shape[0]
    N = T // B_CHUNK
    proj = _matmul(x, W["hyb_w_in"][j], "nn", "hyb_in" + sfx)
    sinks_b = jnp.broadcast_to(W["hyb_sinks"][j][:, None], (A_Q_HEADS, LANES))
    o_a, lse = _attn_fwd(proj, cos, sin_s, sinks_b, "attn_fwd" + sfx)
    qkvc = _dnconv_fwd(proj, W["hyb_conv_w"][j], "dnconv_fwd" + sfx)
    bg = proj[:, HP_BG:HP_BG + 2 * B_HEADS]
    beta = jax.nn.sigmoid(bg[:, :B_HEADS])
    pre = bg[:, B_HEADS:] + W["hyb_dt_bias"][j][None]
    g = -jnp.exp(W["hyb_a_log"][j])[None] * jax.nn.softplus(pre)
    grow = g.T.reshape(B_HEADS, N, 1, B_CHUNK)
    brow = beta.T.reshape(B_HEADS, N, 1, B_CHUNK)
    nw = W["hyb_norm_w"][j][None]
    o_b, states, invs = _dn_fwd(qkvc, proj, nw, grow, brow, "dn_fwd" + sfx)
    mix = jnp.concatenate([o_a, o_b], axis=1).astype(ACT_DTYPE)
    y = _matmul(mix, W["hyb_w_out"][j], "nn", "hyb_out" + sfx)
    res = dict(proj=proj, o_a=o_a, lse=lse, qkvc=qkvc, beta=beta, pre=pre, g=g, grow=grow, brow=brow,
               states=states, invs=invs, mix=mix, sinks_b=sinks_b, nw=nw)
    return y, res


def _hybrid_bwd(x, du, dub, W, j, res, tables, sfx):
    cos, sin_s = tables
    T = x.shape[0]
    proj = res["proj"]
    d_wout = _matmul(res["mix"], dub, "tn", "hyb_out_dw" + sfx)
    dmix = _matmul(dub, W["hyb_w_out"][j], "nt", "hyb_out_dx" + sfx)
    dq, dkc, dkp, dvc, dvp, dsink = _attn_bwd(proj, cos, sin_s, res["sinks_b"], res["o_a"], res["lse"], dmix,
                                               "attn_bwd" + sfx)
    zpad = jnp.zeros((WINDOW, LANES), F32)
    dk = dkc + jnp.concatenate([dkp[WINDOW:], zpad], axis=0)
    dv = dvc + jnp.concatenate([dvp[WINDOW:], zpad], axis=0)
    dqc, dkcv, dvcv, dz, dg4, dbeta4, dnw = _dn_bwd(res["qkvc"], proj, res["nw"], res["grow"], res["brow"],
                                                    res["states"], res["invs"], dmix, "dn_bwd" + sfx)
    dqkvc = jnp.concatenate([dqc, dkcv, dvcv], axis=1)
    dqkvb, dconv = _dnconv_bwd(proj, W["hyb_conv_w"][j], dqkvc, "dnconv_bwd" + sfx)
    dg = dg4.reshape(B_HEADS, T).T
    dbeta = dbeta4.reshape(B_HEADS, T).T
    beta = res["beta"]
    dbeta_logit = dbeta * beta * (1.0 - beta)
    da_logit = dg * (-jnp.exp(W["hyb_a_log"][j]))[None] * jax.nn.sigmoid(res["pre"])
    d_dt_bias = jnp.sum(da_logit, axis=0)
    d_a_log = jnp.sum(dg * res["g"], axis=0)
    zcols = jnp.zeros((T, HYB_PAD - HP_BG - 2 * B_HEADS), F32)
    dproj = jnp.concatenate([dq, dqkvb, dz, dk, dv, dbeta_logit, da_logit, zcols], axis=1).astype(ACT_DTYPE)
    d_win = _matmul(x, dproj, "tn", "hyb_in_dw" + sfx)
    dx = _matmul(dproj, W["hyb_w_in"][j], "nt", "hyb_in_dx" + sfx, epi=_epi_add_residual, extra=du)
    grads = dict(hyb_w_in=d_win, hyb_w_out=d_wout, hyb_sinks=dsink[0], hyb_conv_w=dconv, hyb_a_log=d_a_log,
                 hyb_dt_bias=d_dt_bias, hyb_norm_w=jnp.sum(dnw[:, 0, :], axis=0))
    return dx, grads


def _rec_fwd(x, W, j, sfx):
    proj = _matmul(x, W["rec_w_in"][j], "nn", "rec_in" + sfx)
    sp = jax.nn.softplus(-W["rec_lambda"][j])[None]
    hg, h = _rglru_fwd(proj, W["rec_conv_w"][j], W["rec_conv_b"][j][None], W["rec_w_a"][j], W["rec_w_x"][j],
                       W["rec_b_a"][j][None], W["rec_b_x"][j][None], sp, "rglru_fwd" + sfx)
    y = _matmul(hg, W["rec_w_out"][j], "nn", "rec_out" + sfx)
    return y, dict(proj=proj, hg=hg, h=h, sp=sp)


def _rec_bwd(x, du, dub, W, j, res, sfx):
    d_wout = _matmul(res["hg"], dub, "tn", "rec_out_dw" + sfx)
    dhg = _matmul(dub, W["rec_w_out"][j], "nt", "rec_out_dx" + sfx)
    dproj, dcw, dcb, dwa, dwx, dba, dbx, dsp = _rglru_bwd(
        res["proj"], W["rec_conv_w"][j], W["rec_conv_b"][j][None], W["rec_w_a"][j], W["rec_w_x"][j],
        W["rec_b_a"][j][None], W["rec_b_x"][j][None], res["sp"], res["h"], dhg, "rglru_bwd" + sfx)
    d_lam = dsp[0] * (-jax.nn.sigmoid(-W["rec_lambda"][j]))
    d_win = _matmul(x, dproj, "tn", "rec_in_dw" + sfx)
    dx = _matmul(dproj, W["rec_w_in"][j], "nt", "rec_in_dx" + sfx, epi=_epi_add_residual, extra=du)
    grads = dict(rec_w_in=d_win, rec_w_out=d_wout, rec_conv_w=dcw, rec_conv_b=dcb[0], rec_w_a=dwa, rec_w_x=dwx,
                 rec_b_a=dba[0], rec_b_x=dbx[0], rec_lambda=d_lam)
    return dx, grads


def _local_step(x, tgt, W, mlp_w, on_group):
    T = x.shape[0]
    tables = _rope_tables(T)
    acts = []
    xb = x.astype(ACT_DTYPE)
    for layer in range(DEPTH):
        j, sfx = layer // 2, ""
        if layer % 2 == 0:
            y, res = _hybrid_fwd(xb, W, j, tables, sfx)
        else:
            y, res = _rec_fwd(xb, W, j, sfx)
        x1, x1b = _ln_fwd(x, y, W["ln1_g"][layer][None], W["ln1_b"][layer][None], "ln_fwd")
        w1, w2, wl = mlp_w(layer, x1b)
        h1 = _matmul(x1b, w1, "nn", "mlp_up", out_dtype=ACT_DTYPE, b_chips=("j", wl))
        y2 = _matmul(h1, w2, "nn", "mlp_down", a_fn=_relu2, b_chips=("k", wl))
        x2, x2b = _ln_fwd(x1, y2, W["ln2_g"][layer][None], W["ln2_b"][layer][None], "ln_fwd")
        acts.append(dict(x=x, xb=xb, y=y, res=res, x1=x1, x1b=x1b, h1=h1, y2=y2))
        x, xb = x2, x2b
    dx, loss = _loss_head(x, tgt, "loss_head")
    per_layer = [None] * DEPTH
    d_w1 = [lax.empty((N_CHIPS, 2, D_MODEL, D_FF // N_CHIPS), F32) for _ in range(DEPTH // 2)]
    d_w2 = [lax.empty((N_CHIPS, 2, D_FF // N_CHIPS, D_MODEL), F32) for _ in range(DEPTH // 2)]
    token = None
    for layer in reversed(range(DEPTH)):
        j, a = layer // 2, acts[layer]
        ln2_g = W["ln2_g"][layer][None]
        if token is not None:
            ln2_g = ln2_g + token
        du2, du2b, dg2, db2 = _ln_bwd(a["x1"], a["y2"], ln2_g, dx, "ln_bwd")
        w1, w2, wl = mlp_w(layer, du2b)
        d_w2[j] = _matmul(a["h1"], du2b, "tn", "mlp_down_dw", a_fn=_relu2, out_chips=("i", layer % 2, d_w2[j]))
        dh1 = _matmul(du2b, w2, "nt", "mlp_down_dx", epi=_epi_drelu2, extra=a["h1"], out_dtype=ACT_DTYPE,
                      b_chips=("j", wl))
        d_w1[j] = _matmul(a["x1b"], dh1, "tn", "mlp_up_dw", out_chips=("j", layer % 2, d_w1[j]))
        dx1 = _matmul(dh1, w1, "nt", "mlp_up_dx", epi=_epi_add_residual, extra=du2, b_chips=("k", wl))
        du1, du1b, dg1, db1 = _ln_bwd(a["x"], a["y"], W["ln1_g"][layer][None], dx1, "ln_bwd")
        if layer % 2 == 0:
            dx, g = _hybrid_bwd(a["xb"], du1, du1b, W, j, a["res"], tables, "")
        else:
            dx, g = _rec_bwd(a["xb"], du1, du1b, W, j, a["res"], "")
        g.update(ln1_g=dg1[0], ln1_b=db1[0], ln2_g=dg2[0], ln2_b=db2[0])
        per_layer[layer] = g
        if layer % 2 == 0:
            token = on_group(j, per_layer[layer], per_layer[layer + 1], d_w1[j], d_w2[j])
    grads = {}
    for name in ("ln1_g", "ln1_b", "ln2_g", "ln2_b"):
        grads[name] = jnp.stack([per_layer[l][name] for l in range(DEPTH)])
    for name in ("hyb_norm_w", "hyb_sinks", "hyb_a_log", "hyb_dt_bias"):
        grads[name] = jnp.stack([per_layer[l][name] for l in (0, 2)])
    return loss, dx, grads


BIG = ("hyb_w_in", "hyb_w_out", "rec_w_in", "rec_w_out", "mlp_w1", "mlp_w2", "rec_w_a", "rec_w_x")
COL_SHARDED = ("hyb_w_in", "rec_w_in", "mlp_w1")
CHIP_MAJOR = ("mlp_w1", "mlp_w2")
MID = ("hyb_conv_w", "rec_conv_w", "rec_conv_b", "rec_b_a", "rec_b_x", "rec_lambda")
SMALL = ("ln1_g", "ln1_b", "ln2_g", "ln2_b", "hyb_norm_w", "hyb_sinks", "hyb_a_log", "hyb_dt_bias")
WEIGHTS = ("hyb_w_in", "hyb_sinks", "hyb_conv_w", "hyb_a_log", "hyb_dt_bias", "hyb_norm_w", "hyb_w_out", "rec_w_in",
           "rec_conv_w", "rec_conv_b", "rec_w_a", "rec_b_a", "rec_w_x", "rec_b_x", "rec_lambda", "rec_w_out", "ln1_g",
           "ln1_b", "mlp_w1", "mlp_w2", "ln2_g", "ln2_b")


def _gather_full_weights(w):
    wb = {k: w[k].astype(MXU_DTYPE) for k in BIG}
    shards = [_halves(wb[k][:1] if k in CHIP_MAJOR else wb[k]) for k in BIG]
    shards.append(_pack_mid(*[w[k] for k in MID]))
    got = _all_gather_weights(shards, "all_gather_weights")
    me = 2 * lax.axis_index("x") + lax.axis_index("y")
    got = [lax.dynamic_update_slice(g, s[None], (me, 0, 0, 0)) for s, g in zip(shards, got)]

    own = [wb[k][1:] for k in CHIP_MAJOR]
    land = [lax.dynamic_update_slice(lax.empty((N_CHIPS,) + o.shape, o.dtype), o[None], (me, 0, 0, 0)) for o in own]
    send_sems, recv_sems, own, land = _gather_start(own, land, got[-1], "gather_start")
    first = {k: g.reshape((N_CHIPS, 1) + w[k].shape[1:]) for k, g in zip(BIG, got[:-1]) if k in CHIP_MAJOR}
    pending = dict(own=own, land=land, waited=0)

    def mlp_w(layer, after):
        if layer == 0:
            return first["mlp_w1"], first["mlp_w2"], 0
        while pending["waited"] < layer:
            pending["own"], pending["land"] = _gather_wait(send_sems, recv_sems, pending["own"], pending["land"],
                                                           pending["waited"], after, "gather_wait_%d" % pending["waited"])
            pending["waited"] += 1
        return pending["land"][0], pending["land"][1], layer - 1

    W = {}
    for k, g in zip(BIG, got[:-1]):
        L = w[k].shape[0]
        if k in CHIP_MAJOR:
            continue
        elif k in ("rec_w_a", "rec_w_x"):
            g5 = g.reshape((N_CHIPS,) + w[k].shape)
            W[k] = g5.transpose(1, 2, 0, 3, 4).reshape(L, LRU_BLOCKS, LRU_BLOCK_W, LRU_BLOCK_W)
        else:
            g4 = g.reshape((N_CHIPS,) + w[k].shape)
            W[k] = _cols_from_chips(g4) if k in COL_SHARDED else _rows_from_chips(g4)
    W["hyb_w_in"] = _pad_hyb_cols(W["hyb_w_in"])
    conv_w, rconv_w, rconv_b, b_a, b_x, lam = _unpack_mid(got[-1])
    W["hyb_conv_w"] = conv_w.transpose(1, 2, 0, 3).reshape(2, CONV_K, 3 * B_W)
    W["rec_conv_w"] = rconv_w.transpose(1, 2, 0, 3).reshape(2, CONV_K, D_MODEL)
    for k, v in (("rec_conv_b", rconv_b), ("rec_b_a", b_a), ("rec_b_x", b_x), ("rec_lambda", lam)):
        W[k] = v.transpose(1, 0, 2).reshape(2, D_MODEL)
    for k in SMALL:
        W[k] = w[k]
    return W, mlp_w


def _group_by_chip(gh, gr, d_w1, d_w2):
    g = dict(gh, **gr)
    g["hyb_w_in"] = _unpad_hyb_cols(g["hyb_w_in"])
    out = []
    for k in BIG:
        if k == "mlp_w1":
            v = d_w1
        elif k == "mlp_w2":
            v = d_w2
        elif k in ("rec_w_a", "rec_w_x"):
            v = g[k].reshape(1, LRU_BLOCKS, N_CHIPS, LRU_BLOCK_W // N_CHIPS, LRU_BLOCK_W).transpose(2, 0, 1, 3, 4)
        else:
            v = _cols_to_chips(g[k][None]) if k in COL_SHARDED else _rows_to_chips(g[k][None])
        out.append(v.reshape(N_CHIPS, 2, -1, v.shape[-1]))
    conv_w = g["hyb_conv_w"].reshape(CONV_K, N_CHIPS, -1).transpose(1, 0, 2)
    rconv_w = g["rec_conv_w"].reshape(CONV_K, N_CHIPS, -1).transpose(1, 0, 2)
    vecs = [g[k].reshape(N_CHIPS, -1) for k in ("rec_conv_b", "rec_b_a", "rec_b_x", "rec_lambda")]
    out.append(_pack_mid(conv_w, rconv_w, *vecs).reshape(N_CHIPS, 2, MID_ROWS // 2, D_MODEL))
    return out


def kernel(x, hyb_w_in, hyb_sinks, hyb_conv_w, hyb_a_log, hyb_dt_bias, hyb_norm_w, hyb_w_out, rec_w_in, rec_conv_w, rec_conv_b, rec_w_a, rec_b_a, rec_w_x, rec_b_x, rec_lambda, rec_w_out, ln1_g, ln1_b, mlp_w1, mlp_w2, ln2_g, ln2_b, loss_target, m_hyb_w_in, m_hyb_sinks, m_hyb_conv_w, m_hyb_a_log, m_hyb_dt_bias, m_hyb_norm_w, m_hyb_w_out, m_rec_w_in, m_rec_conv_w, m_rec_conv_b, m_rec_w_a, m_rec_b_a, m_rec_w_x, m_rec_b_x, m_rec_lambda, m_rec_w_out, m_ln1_g, m_ln1_b, m_mlp_w1, m_mlp_w2, m_ln2_g, m_ln2_b, v_hyb_w_in, v_hyb_sinks, v_hyb_conv_w, v_hyb_a_log, v_hyb_dt_bias, v_hyb_norm_w, v_hyb_w_out, v_rec_w_in, v_rec_conv_w, v_rec_conv_b, v_rec_w_a, v_rec_b_a, v_rec_w_x, v_rec_b_x, v_rec_lambda, v_rec_w_out, v_ln1_g, v_ln1_b, v_mlp_w1, v_mlp_w2, v_ln2_g, v_ln2_b):
    args = locals()
    w = {k: args[k] for k in WEIGHTS}
    m = {k: args["m_" + k] for k in WEIGHTS}
    v = {k: args["v_" + k] for k in WEIGHTS}

    W, mlp_w = _gather_full_weights(w)

    core = lax.axis_index("c").astype(jnp.int32)
    me = (2 * lax.axis_index("x") + lax.axis_index("y")).astype(jnp.int32)
    slots = jnp.arange(N_CHIPS, dtype=jnp.int32)
    where = jnp.concatenate([me[None], jnp.where(slots == me, (slots + 1) % N_CHIPS, slots), core[None]])
    state = {}

    def on_group(group, gh, gr, d_w1, d_w2):
        by_chip = _group_by_chip(gh, gr, d_w1, d_w2)
        from_sibling = _rs_to_sibling(by_chip, "rs_to_sibling")
        pair = [_pair_sum(g, r, core[None], "pair_sum") for g, r in zip(by_chip, from_sibling)]
        if group == 1:
            recv = [lax.empty(p.shape, p.dtype) for p in pair]
            state["upper"] = _rs_across_start(pair, recv, "rs_across_start")
            return state["upper"][4][0, 0]
        from_chips = _rs_across_chips(pair, "rs_across_chips")
        half = [_chip_sum(r, p, where, "chip_sum") for r, p in zip(from_chips, pair)]
        state["lower"] = _rs_join_halves(half, "rs_join_halves")
        return None

    loss, dx, grads = _local_step(x[0], loss_target[0], W, mlp_w, on_group)
    loss = lax.psum(loss[0, 0], ("x", "y", "c"))
    send_sems, recv_sems, pair, recv, _ = state["upper"]
    pair, from_chips = _rs_across_wait(send_sems, recv_sems, pair, recv, state["lower"][0], "rs_across_wait")
    half = [_chip_sum(r, p, where, "chip_sum") for r, p in zip(from_chips, pair)]
    upper = _rs_join_halves(half, "rs_join_halves")
    joined = [jnp.concatenate([lo.reshape(1, -1, lo.shape[-1]), up.reshape(1, -1, up.shape[-1])])
              for lo, up in zip(state["lower"], upper)]

    g_out, d_out, m_out, v_out = {}, {}, {}, {}

    def update(name, g2d):
        shape = w[name].shape
        n = g2d.shape[-1]
        d, nm, nv = _adamw(w[name].reshape(-1, n), g2d, m[name].reshape(-1, n), v[name].reshape(-1, n), "adamw")
        g_out[name], d_out[name] = g2d.reshape(shape), d.reshape(shape)
        m_out[name], v_out[name] = nm.reshape(shape), nv.reshape(shape)

    for k, g in zip(BIG, joined[:-1]):
        update(k, g.reshape(-1, g.shape[-1]))
    mid_w, mid_m, mid_v = (_pack_mid(*[t[k] for k in MID]).reshape(-1, D_MODEL) for t in (w, m, v))
    mid_g = joined[-1].reshape(-1, D_MODEL)
    mid_d, mid_nm, mid_nv = _adamw(mid_w, mid_g, mid_m, mid_v, "adamw_mid")
    for dst, packed in ((g_out, mid_g), (d_out, mid_d), (m_out, mid_nm), (v_out, mid_nv)):
        for k, val in zip(MID, _unpack_mid(packed.reshape(2, MID_ROWS, D_MODEL))):
            dst[k] = val.reshape(w[k].shape)

    small_g = _pack_small(*[grads[k] for k in SMALL])
    small_all = _all_gather_small(small_g, "all_gather_small")
    sw, sm, sv = (_pack_small(*[t[k] for k in SMALL]) for t in (w, m, v))
    sg, sd, snm, snv = _adamw_small(sw, small_all, sm, sv, "adamw_small")
    for dst, packed in ((g_out, sg), (d_out, sd), (m_out, snm), (v_out, snv)):
        for k, val in zip(SMALL, _unpack_small(packed)):
            dst[k] = val

    return (loss, dx[None], *[g_out[k] for k in WEIGHTS], *[d_out[k] for k in WEIGHTS],
            *[m_out[k] for k in WEIGHTS], *[v_out[k] for k in WEIGHTS])
```

```python
import functools

import jax
import jax.numpy as jnp
import numpy as np
from jax import lax
from jax.experimental import pallas as pl
from jax.experimental.pallas import tpu as pltpu

F32 = jnp.float32
MXU_DTYPE = jnp.bfloat16
ACT_DTYPE = jnp.bfloat16
ICI_DTYPE = jnp.bfloat16

D_MODEL = 1024
DEPTH = 4
A_HEAD_DIM = 64
A_Q_HEADS = 8
A_KV_HEADS = 2
WINDOW = 128
ROPE_THETA = 10000.0
B_HEADS = 4
B_HEAD_DIM = 128
B_CHUNK = 64
CONV_K = 4
LRU_BLOCKS = 4
LRU_BLOCK_W = D_MODEL // LRU_BLOCKS
LRU_C = 8.0
D_FF = 4 * D_MODEL
A_Q_W = A_Q_HEADS * A_HEAD_DIM
A_KV_W = A_KV_HEADS * A_HEAD_DIM
B_W = B_HEADS * B_HEAD_DIM
HYB_PROJ = A_Q_W + 2 * A_KV_W + 4 * B_W + 2 * B_HEADS
DN_ALPHA = (2 * DEPTH) ** 0.25
LN_EPS = 1e-5
NORM_EPS = 1e-6
ADAM_LR = 0.001
ADAM_B1 = 0.9
ADAM_B2 = 0.999
ADAM_EPS = 1e-08
ADAM_WD = 0.01
ADAM_STEP = 10

HP_Q = 0
HP_QKVB = 512
HP_Z = 2048
HP_K = 2560
HP_V = 2688
HP_BG = 2816
HYB_PAD = 3072

N_CHIPS = 4
N_DEV = 8
V7X_VMEM_LIMIT = 48 * 1024 * 1024
LANES = 128
SUBLANES = 8
NEG_BIG = -1e30

NN = (((1,), (0,)), ((), ()))
NT = (((1,), (1,)), ((), ()))
TN = (((0,), (0,)), ((), ()))


def _cparams(*sem):
    return pltpu.CompilerParams(dimension_semantics=sem, vmem_limit_bytes=V7X_VMEM_LIMIT)


def _dot(a, b, dims=NN):
    return lax.dot_general(a.astype(MXU_DTYPE), b.astype(MXU_DTYPE), dims, preferred_element_type=F32)


def _split_bf16(a):
    hi = a.astype(jnp.bfloat16)
    return hi, (a - hi.astype(F32)).astype(jnp.bfloat16)


def _dotf(a, b, dims=NN):
    ah, al = _split_bf16(a)
    bh, bl = _split_bf16(b)
    dg = functools.partial(lax.dot_general, dimension_numbers=dims, preferred_element_type=F32)
    return dg(ah, bh) + (dg(ah, bl) + dg(al, bh))


def _tile(dim, pref):
    t = min(dim, pref)
    while dim % t:
        t //= 2
    return t


def _sigmoid(x):
    return 1.0 / (1.0 + jnp.exp(-x))


def _silu(x):
    return x * _sigmoid(x)


def _dsilu(x):
    s = _sigmoid(x)
    return s * (1.0 + x * (1.0 - s))


GELU_C = 0.7978845608028654
GELU_A = 0.044715


def _gelu(x):
    return 0.5 * x * (1.0 + jnp.tanh(GELU_C * (x + GELU_A * x * x * x)))


def _dgelu(x):
    t = jnp.tanh(GELU_C * (x + GELU_A * x * x * x))
    return 0.5 * (1.0 + t) + 0.5 * x * (1.0 - t * t) * GELU_C * (1.0 + 3.0 * GELU_A * x * x)


def _matmul(a, b, mode, name, *, tm=1024, tn=1024, tk=1024, a_fn=None, epi=None, extra=None, out_dtype=F32,
            b_chips=None, out_chips=None):
    if mode == "tn":
        K, M = a.shape
        tk = 2 * tk
    else:
        M, K = a.shape
    whole_k = False
    if b_chips is not None:
        g, b_layer = b_chips
        r, n = b.shape[2:]
        n_dim, k_dim = (r, n) if mode == "nt" else (n, r)
        N = N_CHIPS * n_dim if g == "j" else n_dim
        assert K == (N_CHIPS * k_dim if g == "k" else k_dim)
        if g == "j":
            tn = n_dim
        elif mode == "nn":
            whole_k, tk, tm = True, K, tm // 2
        else:
            tk = k_dim
    elif mode == "nt":
        N = b.shape[0]
        if tk < K <= 3 * tk:
            tk, tm = K, tm // 2
    else:
        N = b.shape[1]
    if out_chips is not None:
        og, o_layer, o_buf = out_chips
        if og == "j":
            tn = o_buf.shape[3]
        else:
            tm = o_buf.shape[2]
    tm, tn, tk = _tile(M, tm), _tile(N, tn), _tile(K, tk)
    nk = K // tk
    if mode == "tn":
        a_spec = pl.BlockSpec((tk, tm), lambda i, j, k: (k, i))
    else:
        a_spec = pl.BlockSpec((tm, tk), lambda i, j, k: (i, k))
    b_block = (tn, tk) if mode == "nt" else (tk, tn)
    if b_chips is None:
        b_spec = pl.BlockSpec(b_block, (lambda i, j, k: (j, k)) if mode == "nt" else (lambda i, j, k: (k, j)))
    elif mode == "nt":
        b_spec = pl.BlockSpec((None, None) + b_block, (lambda i, j, k: (j, b_layer, 0, k)) if g == "j"
                              else (lambda i, j, k: (k, b_layer, j, 0)))
    elif whole_k:
        b_spec = pl.BlockSpec((N_CHIPS, None, k_dim, tn), lambda i, j, k: (0, b_layer, 0, j))
    else:
        b_spec = pl.BlockSpec((None, None) + b_block, lambda i, j, k: (j, b_layer, k, 0))
    o_spec = pl.BlockSpec((tm, tn), lambda i, j, k: (i, j))
    e_spec = o_spec
    if out_chips is not None:
        o_spec = pl.BlockSpec((None, None, tm, tn), (lambda i, j, k: (j, o_layer, i, 0)) if og == "j"
                              else (lambda i, j, k: (i, o_layer, 0, j)))
    dims = {"nn": NN, "nt": NT, "tn": TN}[mode]
    has_extra = extra is not None
    n_in = 2 + has_extra + (out_chips is not None)

    def body(*refs):
        a_ref, b_ref = refs[0], refs[1]
        e_ref = refs[2] if has_extra else None
        o_ref = refs[n_in]
        av = a_ref[...]
        if a_fn is not None:
            av = a_fn(av)
        bv = b_ref[...]
        if whole_k:
            bv = bv.reshape(K, tn)
        part = _dot(av, bv, dims)

        def finish(acc):
            if epi is not None:
                acc = epi(acc, e_ref[...])
            o_ref[...] = acc.astype(out_dtype)

        if nk == 1:
            finish(part)
        else:
            acc_ref = refs[-1]
            k = pl.program_id(2)

            @pl.when(k == 0)
            def _():
                acc_ref[...] = part

            @pl.when(k > 0)
            def _():
                acc_ref[...] += part

            @pl.when(k == nk - 1)
            def _():
                finish(acc_ref[...])

    in_specs = [a_spec, b_spec] + ([e_spec] if has_extra else [])
    args = (a, b) + ((extra,) if has_extra else ())
    out_shape = jax.ShapeDtypeStruct((M, N), out_dtype)
    aliases = {}
    if out_chips is not None:
        in_specs.append(pl.BlockSpec(memory_space=pl.ANY))
        args += (o_buf,)
        out_shape = jax.ShapeDtypeStruct(o_buf.shape, o_buf.dtype)
        aliases = {n_in - 1: 0}
    return pl.pallas_call(
        body, name=name, grid=(M // tm, N // tn, nk), in_specs=in_specs, out_specs=o_spec, out_shape=out_shape,
        input_output_aliases=aliases,
        scratch_shapes=[pltpu.VMEM((tm, tn), F32)] if nk > 1 else [],
        compiler_params=_cparams("parallel", "parallel", "arbitrary"),
    )(*args)


def _relu2(v):
    r = jnp.maximum(v, 0.0)
    return r * r


def _epi_drelu2(acc, h):
    return acc * (2.0 * jnp.maximum(h, 0.0))


def _epi_add_residual(acc, du):
    return acc + DN_ALPHA * du


def _ln_fwd(x, y, g, b, name):
    T, D = x.shape
    tr = _tile(T, 512)

    def body(x_ref, y_ref, g_ref, b_ref, o_ref, ob_ref):
        u = DN_ALPHA * x_ref[...] + y_ref[...]
        mu = jnp.mean(u, axis=-1, keepdims=True)
        d = u - mu
        var = jnp.mean(d * d, axis=-1, keepdims=True)
        o = d * lax.rsqrt(var + LN_EPS) * g_ref[...] + b_ref[...]
        o_ref[...] = o
        ob_ref[...] = o.astype(ACT_DTYPE)

    row = pl.BlockSpec((tr, D), lambda i: (i, 0))
    vec = pl.BlockSpec((1, D), lambda i: (0, 0))
    return pl.pallas_call(
        body, name=name, grid=(T // tr,), in_specs=[row, row, vec, vec], out_specs=[row, row],
        out_shape=[jax.ShapeDtypeStruct((T, D), F32), jax.ShapeDtypeStruct((T, D), ACT_DTYPE)],
        compiler_params=_cparams("parallel"),
    )(x, y, g, b)


def _ln_bwd(x, y, g, dout, name):
    T, D = x.shape
    tr = _tile(T, 512)

    def body(x_ref, y_ref, g_ref, d_ref, du_ref, dub_ref, dg_ref, db_ref):
        i = pl.program_id(0)
        u = DN_ALPHA * x_ref[...] + y_ref[...]
        mu = jnp.mean(u, axis=-1, keepdims=True)
        d = u - mu
        rstd = lax.rsqrt(jnp.mean(d * d, axis=-1, keepdims=True) + LN_EPS)
        xhat = d * rstd
        dout_v = d_ref[...]
        dxh = dout_v * g_ref[...]
        m1 = jnp.mean(dxh, axis=-1, keepdims=True)
        m2 = jnp.mean(dxh * xhat, axis=-1, keepdims=True)
        du = rstd * (dxh - m1 - xhat * m2)
        du_ref[...] = du
        dub_ref[...] = du.astype(ACT_DTYPE)
        pg = jnp.sum(dout_v * xhat, axis=0, keepdims=True)
        pb = jnp.sum(dout_v, axis=0, keepdims=True)

        @pl.when(i == 0)
        def _():
            dg_ref[...] = pg
            db_ref[...] = pb

        @pl.when(i > 0)
        def _():
            dg_ref[...] += pg
            db_ref[...] += pb

    row = pl.BlockSpec((tr, D), lambda i: (i, 0))
    vec = pl.BlockSpec((1, D), lambda i: (0, 0))
    return pl.pallas_call(
        body, name=name, grid=(T // tr,), in_specs=[row, row, vec, row], out_specs=[row, row, vec, vec],
        out_shape=[jax.ShapeDtypeStruct((T, D), F32), jax.ShapeDtypeStruct((T, D), ACT_DTYPE),
                   jax.ShapeDtypeStruct((1, D), F32), jax.ShapeDtypeStruct((1, D), F32)],
        compiler_params=_cparams("arbitrary"),
    )(x, y, g, dout)


def _loss_head(y, tgt, name):
    T, D = y.shape
    tr = _tile(T, 512)

    def body(y_ref, t_ref, dy_ref, l_ref):
        i = pl.program_id(0)
        e = y_ref[...] - t_ref[...]
        dy_ref[...] = e * (1.0 / D)
        part = jnp.sum(e * e, axis=(0, 1), keepdims=True) * (0.5 / D)

        @pl.when(i == 0)
        def _():
            l_ref[...] = part

        @pl.when(i > 0)
        def _():
            l_ref[...] += part

    row = pl.BlockSpec((tr, D), lambda i: (i, 0))
    one = pl.BlockSpec((1, 1), lambda i: (0, 0))
    return pl.pallas_call(
        body, name=name, grid=(T // tr,), in_specs=[row, row], out_specs=[row, one],
        out_shape=[jax.ShapeDtypeStruct((T, D), F32), jax.ShapeDtypeStruct((1, 1), F32)],
        compiler_params=_cparams("arbitrary"),
    )(y, tgt)


def _swap_half(x):
    n = x.shape[-1]
    lane = lax.broadcasted_iota(jnp.int32, x.shape, 1)
    first = (lane % A_HEAD_DIM) < (A_HEAD_DIM // 2)
    return jnp.where(first, pltpu.roll(x, n - A_HEAD_DIM // 2, axis=1), pltpu.roll(x, A_HEAD_DIM // 2, axis=1))


def _rope(x, cos, sin_signed):
    return x * cos + _swap_half(x) * sin_signed


def _rope_t(dy, cos, sin_signed):
    return dy * cos + _swap_half(dy * sin_signed)


def _rope_tables(T):
    half = A_HEAD_DIM // 2
    inv_freq = np.float32(ROPE_THETA) ** (-np.arange(half, dtype=np.float32) / np.float32(half))
    ang = np.arange(T, dtype=np.float32)[:, None] * inv_freq[None, :]
    cos = np.tile(np.cos(ang), (1, 4))
    sin = np.sin(ang)
    sin_signed = np.tile(np.concatenate([-sin, sin], axis=1), (1, 2))
    return jnp.asarray(cos, F32), jnp.asarray(sin_signed, F32)


def _band_mask(n):
    qi = lax.broadcasted_iota(jnp.int32, (WINDOW, 2 * WINDOW), 0)
    kj = lax.broadcasted_iota(jnp.int32, (WINDOW, 2 * WINDOW), 1)
    return (kj > qi) & (kj <= qi + WINDOW) & ((n > 0) | (kj >= WINDOW))


def _place(v, src_half, dst_half):
    lane = lax.broadcasted_iota(jnp.int32, v.shape, 1)
    if src_half != dst_half:
        v = pltpu.roll(v, A_HEAD_DIM, axis=1)
    keep = (lane >= A_HEAD_DIM) if dst_half else (lane < A_HEAD_DIM)
    return jnp.where(keep, v, 0.0)


def _attn_specs():
    kb, vb = HP_K // LANES, HP_V // LANES
    prev = lambda n: jnp.maximum(n - 1, 0)
    return dict(
        q=pl.BlockSpec((WINDOW, A_Q_W), lambda n: (n, 0)),
        kc=pl.BlockSpec((WINDOW, LANES), lambda n: (n, kb)),
        kp=pl.BlockSpec((WINDOW, LANES), lambda n: (prev(n), kb)),
        vc=pl.BlockSpec((WINDOW, LANES), lambda n: (n, vb)),
        vp=pl.BlockSpec((WINDOW, LANES), lambda n: (prev(n), vb)),
        tq=pl.BlockSpec((WINDOW, LANES), lambda n: (n, 0)),
        tp=pl.BlockSpec((WINDOW, LANES), lambda n: (prev(n), 0)),
        sink=pl.BlockSpec((A_Q_HEADS, LANES), lambda n: (0, 0)),
        row512=pl.BlockSpec((WINDOW, A_Q_W), lambda n: (n, 0)),
        row128=pl.BlockSpec((WINDOW, LANES), lambda n: (n, 0)),
        lse=pl.BlockSpec((WINDOW, A_Q_HEADS), lambda n: (n, 0)),
    )


def _attn_fwd(proj, cos, sin_s, sinks_b, name):
    T = proj.shape[0]
    sp = _attn_specs()

    def body(q_ref, kc_ref, kp_ref, vc_ref, vp_ref, cq_ref, sq_ref, cp_ref, sp_ref, sink_ref, o_ref, l_ref):
        n = pl.program_id(0)
        cq, sq = cq_ref[...], sq_ref[...]
        cq4, sq4 = jnp.tile(cq, (1, A_Q_W // LANES)), jnp.tile(sq, (1, A_Q_W // LANES))
        q = _rope(q_ref[...], cq4, sq4) * (A_HEAD_DIM ** -0.5)
        kc = _rope(kc_ref[...], cq, sq)
        kp = _rope(kp_ref[...], cp_ref[...], sp_ref[...])
        kk = jnp.concatenate([kp, kc], axis=0)
        vv = jnp.concatenate([vp_ref[...], vc_ref[...]], axis=0)
        mask = _band_mask(n)[None]
        lane = lax.broadcasted_iota(jnp.int32, (WINDOW, LANES), 1)
        lane8 = lax.broadcasted_iota(jnp.int32, (WINDOW, A_Q_HEADS), 1)
        qe = jnp.stack([_place(q[:, (hq // 2) * LANES:(hq // 2 + 1) * LANES], hq % 2, hq // 4) for hq in range(A_Q_HEADS)])
        kk8 = jnp.broadcast_to(kk[None], (A_Q_HEADS,) + kk.shape)
        vv8 = jnp.broadcast_to(vv[None], (A_Q_HEADS,) + vv.shape)
        sk = jnp.stack([sink_ref[hq:hq + 1, 0:1] for hq in range(A_Q_HEADS)])
        s = jnp.where(mask, _dot(qe, kk8, BNT), NEG_BIG)
        m = jnp.maximum(jnp.max(s, axis=-1, keepdims=True), sk)
        p = jnp.exp(s - m)
        den = jnp.sum(p, axis=-1, keepdims=True) + jnp.exp(sk - m)
        o = _dot(p * (1.0 / den), vv8, BNN)
        lse_h = m + jnp.log(den)
        outs = []
        lse = jnp.zeros((WINDOW, A_Q_HEADS), F32)
        for pb in range(A_Q_HEADS // 2):
            halves = [_place(o[2 * pb + e], pb // 2, e) for e in range(2)]
            outs.append(jnp.where(lane < A_HEAD_DIM, halves[0], halves[1]))
            for e in range(2):
                lse = jnp.where(lane8 == 2 * pb + e, lse_h[2 * pb + e], lse)
        o_ref[...] = jnp.concatenate(outs, axis=1)
        l_ref[...] = lse

    return pl.pallas_call(
        body, name=name, grid=(T // WINDOW,),
        in_specs=[sp["q"], sp["kc"], sp["kp"], sp["vc"], sp["vp"], sp["tq"], sp["tq"], sp["tp"], sp["tp"], sp["sink"]],
        out_specs=[sp["row512"], sp["lse"]],
        out_shape=[jax.ShapeDtypeStruct((T, A_Q_W), F32), jax.ShapeDtypeStruct((T, A_Q_HEADS), F32)],
        compiler_params=_cparams("parallel"),
    )(proj, proj, proj, proj, proj, cos, sin_s, cos, sin_s, sinks_b)


def _attn_bwd(proj, cos, sin_s, sinks_b, o, lse, dmix, name):
    T = proj.shape[0]
    sp = _attn_specs()

    def body(q_ref, kc_ref, kp_ref, vc_ref, vp_ref, cq_ref, sq_ref, cp_ref, sp_ref, sink_ref, o_ref, l_ref, do_ref,
             dq_ref, dkc_ref, dkp_ref, dvc_ref, dvp_ref, dsink_ref):
        n = pl.program_id(0)
        cq, sq = cq_ref[...], sq_ref[...]
        cp, sps = cp_ref[...], sp_ref[...]
        cq4, sq4 = jnp.tile(cq, (1, A_Q_W // LANES)), jnp.tile(sq, (1, A_Q_W // LANES))
        q = _rope(q_ref[...], cq4, sq4) * (A_HEAD_DIM ** -0.5)
        kc = _rope(kc_ref[...], cq, sq)
        kp = _rope(kp_ref[...], cp, sps)
        kk = jnp.concatenate([kp, kc], axis=0)
        vv = jnp.concatenate([vp_ref[...], vc_ref[...]], axis=0)
        mask = _band_mask(n)[None]
        lane = lax.broadcasted_iota(jnp.int32, (WINDOW, LANES), 1)
        do_all, o_all, l_all = do_ref[...], o_ref[...], l_ref[...]
        lane8 = lax.broadcasted_iota(jnp.int32, (WINDOW, A_Q_HEADS), 1)
        head8 = lax.broadcasted_iota(jnp.int32, (1, A_Q_HEADS), 1)
        prod = do_all * o_all
        qes, does, deltas, lhs = [], [], [], []
        for hq in range(A_Q_HEADS):
            pb, e, kvh = hq // 2, hq % 2, hq // 4
            blk = slice(pb * LANES, (pb + 1) * LANES)
            in_half = (lane >= A_HEAD_DIM) if e else (lane < A_HEAD_DIM)
            deltas.append(jnp.sum(jnp.where(in_half, prod[:, blk], 0.0), axis=-1, keepdims=True))
            qes.append(_place(q[:, blk], e, kvh))
            does.append(_place(do_all[:, blk], e, kvh))
            lhs.append(jnp.sum(jnp.where(lane8 == hq, l_all, 0.0), axis=-1, keepdims=True))
        qe, doe, delta, lh = jnp.stack(qes), jnp.stack(does), jnp.stack(deltas), jnp.stack(lhs)
        kk8 = jnp.broadcast_to(kk[None], (A_Q_HEADS,) + kk.shape)
        vv8 = jnp.broadcast_to(vv[None], (A_Q_HEADS,) + vv.shape)
        sk = jnp.stack([sink_ref[hq:hq + 1, 0:1] for hq in range(A_Q_HEADS)])
        s = _dot(qe, kk8, BNT)
        p = jnp.where(mask, jnp.exp(jnp.where(mask, s, NEG_BIG) - lh), 0.0)
        dvv = jnp.sum(_dot(p, doe, BTN), axis=0)
        ds = p * (_dot(doe, vv8, BNT) - delta)
        dkk = jnp.sum(_dot(ds, qe, BTN), axis=0)
        dqe = _dot(ds, kk8, BNN)
        dsink_h = -jnp.sum(jnp.exp(sk - lh) * delta, axis=(1, 2), keepdims=True)
        dqs = []
        dsk = jnp.zeros((1, A_Q_HEADS), F32)
        for pb in range(A_Q_HEADS // 2):
            halves = [_place(dqe[2 * pb + e], pb // 2, e) for e in range(2)]
            dqs.append(jnp.where(lane < A_HEAD_DIM, halves[0], halves[1]))
            for e in range(2):
                dsk = jnp.where(head8 == 2 * pb + e, dsink_h[2 * pb + e], dsk)
        dq = jnp.concatenate(dqs, axis=1) * (A_HEAD_DIM ** -0.5)
        dq_ref[...] = _rope_t(dq, cq4, sq4)
        dkp_ref[...] = _rope_t(dkk[:WINDOW], cp, sps)
        dkc_ref[...] = _rope_t(dkk[WINDOW:], cq, sq)
        dvp_ref[...] = dvv[:WINDOW]
        dvc_ref[...] = dvv[WINDOW:]

        @pl.when(n == 0)
        def _():
            dsink_ref[...] = dsk

        @pl.when(n > 0)
        def _():
            dsink_ref[...] += dsk

    return pl.pallas_call(
        body, name=name, grid=(T // WINDOW,),
        in_specs=[sp["q"], sp["kc"], sp["kp"], sp["vc"], sp["vp"], sp["tq"], sp["tq"], sp["tp"], sp["tp"], sp["sink"],
                  sp["row512"], sp["lse"], sp["row512"]],
        out_specs=[sp["row512"], sp["row128"], sp["row128"], sp["row128"], sp["row128"],
                   pl.BlockSpec((1, A_Q_HEADS), lambda n: (0, 0))],
        out_shape=[jax.ShapeDtypeStruct((T, A_Q_W), F32)] + [jax.ShapeDtypeStruct((T, LANES), F32)] * 4
        + [jax.ShapeDtypeStruct((1, A_Q_HEADS), F32)],
        compiler_params=_cparams("arbitrary"),
    )(proj, proj, proj, proj, proj, cos, sin_s, cos, sin_s, sinks_b, o, lse, dmix)


def _shift_down(x, prev8, k):
    if k == 0:
        return x
    row = lax.broadcasted_iota(jnp.int32, prev8.shape, 0)
    r = pltpu.roll(x, k, axis=0)
    top = jnp.where(row < k, pltpu.roll(prev8, k, axis=0), r[:SUBLANES])
    return jnp.concatenate([top, r[SUBLANES:]], axis=0)


def _shift_up(x, next8, k):
    if k == 0:
        return x
    R = x.shape[0]
    row = lax.broadcasted_iota(jnp.int32, next8.shape, 0)
    r = pltpu.roll(x, R - k, axis=0)
    bot = jnp.where(row >= SUBLANES - k, pltpu.roll(next8, SUBLANES - k, axis=0), r[R - SUBLANES:])
    return jnp.concatenate([r[:R - SUBLANES], bot], axis=0)


def _conv(x, prev8, w):
    y = x * w[CONV_K - 1:CONV_K]
    for j in range(CONV_K - 1):
        y = y + _shift_down(x, prev8, CONV_K - 1 - j) * w[j:j + 1]
    return y


def _conv_bwd(x, prev8, w, dy, next8_dy):
    dx = dy * w[CONV_K - 1:CONV_K]
    dws = []
    for j in range(CONV_K - 1):
        k = CONV_K - 1 - j
        dx = dx + _shift_up(dy, next8_dy, k) * w[j:j + 1]
        dws.append(jnp.sum(dy * _shift_down(x, prev8, k), axis=0, keepdims=True))
    dws.append(jnp.sum(dy * x, axis=0, keepdims=True))
    return dx, dws


def _dnconv_fwd(proj, conv_w, name):
    T = proj.shape[0]
    R = _tile(T, 512)
    cb0 = HP_QKVB // A_Q_W

    def body(x_ref, w_ref, o_ref, prev_ref):
        i = pl.program_id(1)

        @pl.when(i == 0)
        def _():
            prev_ref[...] = jnp.zeros_like(prev_ref)

        x = x_ref[...]
        o_ref[...] = _silu(_conv(x, prev_ref[...], w_ref[...]))
        prev_ref[...] = x[R - SUBLANES:]

    return pl.pallas_call(
        body, name=name, grid=(3, T // R),
        in_specs=[pl.BlockSpec((R, B_W), lambda j, i: (i, cb0 + j)), pl.BlockSpec((CONV_K, B_W), lambda j, i: (0, j))],
        out_specs=pl.BlockSpec((R, B_W), lambda j, i: (i, j)),
        out_shape=jax.ShapeDtypeStruct((T, 3 * B_W), F32),
        scratch_shapes=[pltpu.VMEM((SUBLANES, B_W), F32)],
        compiler_params=_cparams("parallel", "arbitrary"),
    )(proj, conv_w)


def _dnconv_bwd(proj, conv_w, dy, name):
    T = proj.shape[0]
    R = _tile(T, 512)
    nb = T // R
    cb0 = HP_QKVB // A_Q_W
    r8 = R // SUBLANES

    def body(x_ref, xp_ref, w_ref, dy_ref, dx_ref, dw_ref, next_ref):
        i = pl.program_id(1)
        blk = nb - 1 - i

        @pl.when(i == 0)
        def _():
            next_ref[...] = jnp.zeros_like(next_ref)

        x = x_ref[...]
        prev8 = jnp.where(blk > 0, xp_ref[...], 0.0)
        w = w_ref[...]
        dpre = dy_ref[...] * _dsilu(_conv(x, prev8, w))
        dx, dw = _conv_bwd(x, prev8, w, dpre, next_ref[...])
        dx_ref[...] = dx
        next_ref[...] = dpre[:SUBLANES]

        @pl.when(i == 0)
        def _():
            for j in range(CONV_K):
                dw_ref[j:j + 1, :] = dw[j]

        @pl.when(i > 0)
        def _():
            for j in range(CONV_K):
                dw_ref[j:j + 1, :] += dw[j]

    return pl.pallas_call(
        body, name=name, grid=(3, nb),
        in_specs=[pl.BlockSpec((R, B_W), lambda j, i: (nb - 1 - i, cb0 + j)),
                  pl.BlockSpec((SUBLANES, B_W), lambda j, i: (jnp.maximum((nb - 1 - i) * r8 - 1, 0), cb0 + j)),
                  pl.BlockSpec((CONV_K, B_W), lambda j, i: (0, j)),
                  pl.BlockSpec((R, B_W), lambda j, i: (nb - 1 - i, j))],
        out_specs=[pl.BlockSpec((R, B_W), lambda j, i: (nb - 1 - i, j)),
                   pl.BlockSpec((CONV_K, B_W), lambda j, i: (0, j))],
        out_shape=[jax.ShapeDtypeStruct((T, 3 * B_W), F32), jax.ShapeDtypeStruct((CONV_K, 3 * B_W), F32)],
        scratch_shapes=[pltpu.VMEM((SUBLANES, B_W), F32)],
        compiler_params=_cparams("parallel", "arbitrary"),
    )(proj, proj, conv_w, dy)


DK_SCALE = B_HEAD_DIM ** -0.5


BNN = (((2,), (1,)), ((0,), (0,)))
BNT = (((2,), (2,)), ((0,), (0,)))
BTN = (((1,), (1,)), ((0,), (0,)))


def _tri_inv(a):
    C = a.shape[-1]
    ri = lax.broadcasted_iota(jnp.int32, (C, C), 0)
    ci = lax.broadcasted_iota(jnp.int32, (C, C), 1)
    x = jnp.where(ri == ci, 1.0, 0.0)[None] - a
    p = _dotf(a, a, BNN)
    span = 2
    while span < C:
        x = x + _dotf(x, p, BNN)
        span *= 2
        if span < C:
            p = _dotf(p, p, BNN)
    return x


def _dn_chunk(qc, kc, v, gcol, grow, bcol, s0, tm=None):
    C = B_CHUNK
    ri = lax.broadcasted_iota(jnp.int32, (C, C), 0)
    ci = lax.broadcasted_iota(jnp.int32, (C, C), 1)
    incl, strict = (ri >= ci)[None], (ri > ci)[None]
    rq = lax.rsqrt(jnp.sum(qc * qc, axis=-1, keepdims=True) + NORM_EPS)
    rk = lax.rsqrt(jnp.sum(kc * kc, axis=-1, keepdims=True) + NORM_EPS)
    qn = qc * rq
    q = qn * DK_SCALE
    k = kc * rk
    gc_col = jnp.sum(jnp.where(incl, grow, 0.0), axis=2, keepdims=True)
    gc_row = jnp.sum(jnp.where((ri <= ci)[None], gcol, 0.0), axis=1, keepdims=True)
    gl = jnp.sum(gcol, axis=1, keepdims=True)
    dincl = jnp.where(incl, jnp.exp(jnp.where(incl, gc_col - gc_row, 0.0)), 0.0)
    dstrict = jnp.where(strict, dincl, 0.0)
    eg = jnp.exp(gc_col)
    ekt = jnp.exp(gl - gc_col)
    egl = jnp.exp(gl)
    kb = k * bcol
    vb = v * bcol
    kbg = kb * eg
    a = _dot(kb, k, BNT) * dstrict
    if tm is None:
        tm = _tri_inv(a)
    u = _dot(tm, vb, BNN)
    w = _dot(tm, kbg, BNN)
    vn = u - _dot(w, s0, BNN)
    qk = _dot(q, k, BNT) * dincl
    qg = q * eg
    kt = k * ekt
    o = _dot(qg, s0, BNN) + _dot(qk, vn, BNN)
    s1 = s0 * egl + _dot(kt, vn, BTN)
    return dict(rq=rq, rk=rk, qn=qn, q=q, k=k, dincl=dincl, dstrict=dstrict, eg=eg, ekt=ekt, egl=egl, kb=kb, vb=vb,
                kbg=kbg, a=a, tm=tm, w=w, vn=vn, qk=qk, qg=qg, kt=kt, o=o, s1=s1, ri=ri[None], ci=ci[None])


def _heads(ref):
    return jnp.stack([ref[:, h * B_HEAD_DIM:(h + 1) * B_HEAD_DIM] for h in range(B_HEADS)])


def _store_heads(ref, val):
    for h in range(B_HEADS):
        ref[:, h * B_HEAD_DIM:(h + 1) * B_HEAD_DIM] = val[h]


def _dn_specs(N, rev):
    ix = (lambda n: N - 1 - n) if rev else (lambda n: n)
    wide = lambda cb: pl.BlockSpec((B_CHUNK, B_W), lambda n: (ix(n), cb))
    return dict(
        q=wide(0), k=wide(1), v=wide(2), z=wide(HP_Z // B_W), dob=wide(A_Q_W // B_W), out=wide(0),
        nw=pl.BlockSpec((1, LANES), lambda n: (0, 0)),
        row=pl.BlockSpec((B_HEADS, None, 1, B_CHUNK), lambda n: (0, ix(n), 0, 0)),
        state=pl.BlockSpec((B_HEADS, None, B_HEAD_DIM, B_HEAD_DIM), lambda n: (0, ix(n), 0, 0)),
        inv=pl.BlockSpec((B_HEADS, None, B_CHUNK, B_CHUNK), lambda n: (0, ix(n), 0, 0)),
    )


def _to_col(row):
    C = row.shape[-1]
    eye = lax.broadcasted_iota(jnp.int32, (C, C), 0) == lax.broadcasted_iota(jnp.int32, (C, C), 1)
    return jnp.sum(jnp.where(eye[None], row, 0.0), axis=2, keepdims=True)


def _to_row(col):
    C = col.shape[1]
    eye = lax.broadcasted_iota(jnp.int32, (C, C), 0) == lax.broadcasted_iota(jnp.int32, (C, C), 1)
    return jnp.sum(jnp.where(eye[None], col, 0.0), axis=1, keepdims=True)


def _dn_fwd(qkvc, proj, norm_w, grow, brow, name):
    T = qkvc.shape[0]
    N = T // B_CHUNK
    sp = _dn_specs(N, False)

    def body(q_ref, k_ref, v_ref, z_ref, nw_ref, gr_ref, br_ref, o_ref, st_ref, tm_ref, s_ref):
        n = pl.program_id(0)

        @pl.when(n == 0)
        def _():
            s_ref[...] = jnp.zeros_like(s_ref)

        s0 = s_ref[...]
        st_ref[...] = s0
        grow_v = gr_ref[...]
        f = _dn_chunk(_heads(q_ref), _heads(k_ref), _heads(v_ref), _to_col(grow_v), grow_v, _to_col(br_ref[...]), s0)
        o = f["o"]
        r = lax.rsqrt(jnp.mean(o * o, axis=-1, keepdims=True) + NORM_EPS)
        _store_heads(o_ref, o * r * nw_ref[...][None] * _silu(_heads(z_ref)))
        s_ref[...] = f["s1"]
        tm_ref[...] = f["tm"]

    return pl.pallas_call(
        body, name=name, grid=(N,),
        in_specs=[sp["q"], sp["k"], sp["v"], sp["z"], sp["nw"], sp["row"], sp["row"]],
        out_specs=[sp["out"], sp["state"], sp["inv"]],
        out_shape=[jax.ShapeDtypeStruct((T, B_W), F32),
                   jax.ShapeDtypeStruct((B_HEADS, N, B_HEAD_DIM, B_HEAD_DIM), F32),
                   jax.ShapeDtypeStruct((B_HEADS, N, B_CHUNK, B_CHUNK), F32)],
        scratch_shapes=[pltpu.VMEM((B_HEADS, B_HEAD_DIM, B_HEAD_DIM), F32)],
        compiler_params=_cparams("arbitrary"),
    )(qkvc, qkvc, qkvc, proj, norm_w, grow, brow)


def _dn_bwd(qkvc, proj, norm_w, grow, brow, states, invs, dmix, name):
    T = qkvc.shape[0]
    N = T // B_CHUNK
    sp = _dn_specs(N, True)
    C = B_CHUNK

    def body(q_ref, k_ref, v_ref, z_ref, nw_ref, gr_ref, br_ref, st_ref, tm_ref, dob_ref,
             dq_ref, dk_ref, dv_ref, dz_ref, dg_ref, db_ref, dnw_ref, ds_ref):
        n = pl.program_id(0)

        @pl.when(n == 0)
        def _():
            ds_ref[...] = jnp.zeros_like(ds_ref)
            dnw_ref[...] = jnp.zeros_like(dnw_ref)

        s0 = st_ref[...]
        ds1 = ds_ref[...]
        v, z, nw, bcol_v = _heads(v_ref), _heads(z_ref), nw_ref[...][None], _to_col(br_ref[...])
        grow_v = gr_ref[...]
        f = _dn_chunk(_heads(q_ref), _heads(k_ref), v, _to_col(grow_v), grow_v, bcol_v, s0, tm=tm_ref[...])
        o, q, k, qn = f["o"], f["q"], f["k"], f["qn"]
        eg, ekt, egl = f["eg"], f["ekt"], f["egl"]
        tm, w, vn, kb, vb, kbg = f["tm"], f["w"], f["vn"], f["kb"], f["vb"], f["kbg"]
        qg, kt, qk, a = f["qg"], f["kt"], f["qk"], f["a"]
        ri, ci = f["ri"], f["ci"]

        dob_v = _heads(dob_ref)
        r = lax.rsqrt(jnp.mean(o * o, axis=-1, keepdims=True) + NORM_EPS)
        sz = _silu(z)
        on = o * r
        dnw_ref[...] += jnp.sum(dob_v * sz * on, axis=1, keepdims=True)
        _store_heads(dz_ref, dob_v * on * nw * _dsilu(z))
        d_on = dob_v * sz * nw
        do = r * (d_on - on * jnp.mean(d_on * on, axis=-1, keepdims=True))

        dvn = _dot(qk, do, BTN) + _dot(kt, ds1, BNN)
        dqk = _dot(do, vn, BNT)
        dqg = _dot(do, s0, BNT)
        ds_ref[...] = _dot(qg, do, BTN) + egl * ds1 - _dot(w, dvn, BTN)
        dgl = jnp.sum(s0 * ds1, axis=(1, 2), keepdims=True) * egl
        dkt = _dot(vn, ds1, BNT)
        dw = -_dot(dvn, s0, BNT)
        dq = dqg * eg
        dgc = jnp.sum(dqg * qg, axis=-1, keepdims=True)
        dk = dkt * ekt
        t_kt = jnp.sum(dkt * kt, axis=-1, keepdims=True)
        dgl = dgl + jnp.sum(t_kt, axis=1, keepdims=True)
        dgc = dgc - t_kt
        dqkr = dqk * f["dincl"]
        dq = dq + _dot(dqkr, k, BNN)
        dk = dk + _dot(dqkr, q, BTN)
        e_qk = dqk * qk
        dgc = dgc + jnp.sum(e_qk, axis=-1, keepdims=True)
        dgc_row = -jnp.sum(e_qk, axis=1, keepdims=True)
        dtm = _dot(dvn, vb, BNT) + _dot(dw, kbg, BNT)
        dvb = _dot(tm, dvn, BTN)
        dkbg = _dot(tm, dw, BTN)
        dkb = dkbg * eg
        dgc = dgc + jnp.sum(dkbg * kbg, axis=-1, keepdims=True)
        da = -_dotf(tm, _dotf(dtm, tm, BNT), BTN)
        dkk = da * f["dstrict"]
        e_a = da * a
        dgc = dgc + jnp.sum(e_a, axis=-1, keepdims=True)
        dgc_row = dgc_row - jnp.sum(e_a, axis=1, keepdims=True)
        dkb = dkb + _dot(dkk, k, BNN)
        dk = dk + _dot(dkk, kb, BTN)
        dk = dk + dkb * bcol_v
        db_ref[...] = _to_row(jnp.sum(dkb * k, axis=-1, keepdims=True) + jnp.sum(dvb * v, axis=-1, keepdims=True))
        _store_heads(dv_ref, dvb * bcol_v)
        dgc_row = dgc_row + jnp.sum(jnp.where(ri == ci, dgc, 0.0), axis=1, keepdims=True)
        dg_ref[...] = jnp.sum(jnp.where(ci <= ri, _to_col(dgc_row), 0.0), axis=1, keepdims=True) + dgl
        dqs = dq * DK_SCALE
        _store_heads(dq_ref, f["rq"] * (dqs - qn * jnp.sum(dqs * qn, axis=-1, keepdims=True)))
        _store_heads(dk_ref, f["rk"] * (dk - k * jnp.sum(dk * k, axis=-1, keepdims=True)))

    return pl.pallas_call(
        body, name=name, grid=(N,),
        in_specs=[sp["q"], sp["k"], sp["v"], sp["z"], sp["nw"], sp["row"], sp["row"], sp["state"], sp["inv"], sp["dob"]],
        out_specs=[sp["out"], sp["out"], sp["out"], sp["out"], sp["row"], sp["row"],
                   pl.BlockSpec((B_HEADS, 1, LANES), lambda n: (0, 0, 0))],
        out_shape=[jax.ShapeDtypeStruct((T, B_W), F32)] * 4
        + [jax.ShapeDtypeStruct((B_HEADS, N, 1, C), F32), jax.ShapeDtypeStruct((B_HEADS, N, 1, C), F32),
           jax.ShapeDtypeStruct((B_HEADS, 1, LANES), F32)],
        scratch_shapes=[pltpu.VMEM((B_HEADS, B_HEAD_DIM, B_HEAD_DIM), F32)],
        compiler_params=_cparams("arbitrary"),
    )(qkvc, qkvc, qkvc, proj, norm_w, grow, brow, states, invs, dmix)


def _lru_gates(xc, wa_ref, wx_ref, ba, bx, sp):
    pre_r, pre_i = [], []
    for hb in range(LRU_BLOCKS):
        xb = xc[:, hb * LRU_BLOCK_W:(hb + 1) * LRU_BLOCK_W]
        pre_r.append(_dot(xb, wa_ref[hb]))
        pre_i.append(_dot(xb, wx_ref[hb]))
    r = _sigmoid(jnp.concatenate(pre_r, axis=1) + ba)
    i = _sigmoid(jnp.concatenate(pre_i, axis=1) + bx)
    la = -LRU_C * r * sp
    a = jnp.exp(la)
    th = jnp.tanh(la)
    s = jnp.sqrt(-2.0 * th / (1.0 - th))
    return r, i, a, s


def _scan_down(a, b):
    R = a.shape[0]
    row = lax.broadcasted_iota(jnp.int32, a.shape, 0)
    d = 1
    while d < R:
        ok = row >= d
        b = a * jnp.where(ok, pltpu.roll(b, d, axis=0), 0.0) + b
        a = a * jnp.where(ok, pltpu.roll(a, d, axis=0), 1.0)
        d *= 2
    return a, b


def _scan_up(a, b):
    R = a.shape[0]
    row = lax.broadcasted_iota(jnp.int32, a.shape, 0)
    d = 1
    while d < R:
        ok = row < R - d
        b = a * jnp.where(ok, pltpu.roll(b, R - d, axis=0), 0.0) + b
        a = a * jnp.where(ok, pltpu.roll(a, R - d, axis=0), 1.0)
        d *= 2
    return b


def _rglru_fwd(proj, conv_w, conv_b, wa, wx, ba, bx, sp, name):
    T = proj.shape[0]
    R = _tile(T, 256)
    W = D_MODEL

    def body(p_ref, cw_ref, cb_ref, wa_ref, wx_ref, ba_ref, bx_ref, sp_ref, hg_ref, h_ref, prev_ref, hc_ref):
        i = pl.program_id(0)

        @pl.when(i == 0)
        def _():
            prev_ref[...] = jnp.zeros_like(prev_ref)
            hc_ref[...] = jnp.zeros_like(hc_ref)

        xr = p_ref[:, :W]
        gate = p_ref[:, W:]
        xc = _conv(xr, prev_ref[...], cw_ref[...]) + cb_ref[...]
        prev_ref[...] = xr[R - SUBLANES:]
        r, ig, a, s = _lru_gates(xc, wa_ref, wx_ref, ba_ref[...], bx_ref[...], sp_ref[...])
        pa, hb = _scan_down(a, s * ig * xc)
        h = hb + pa * hc_ref[SUBLANES - 1:SUBLANES, :]
        h_ref[...] = h
        hg_ref[...] = (h * _gelu(gate)).astype(ACT_DTYPE)
        hc_ref[...] = h[R - SUBLANES:]

    vec = pl.BlockSpec((1, W), lambda i: (0, 0))
    wsp = pl.BlockSpec((LRU_BLOCKS, LRU_BLOCK_W, LRU_BLOCK_W), lambda i: (0, 0, 0))
    row = pl.BlockSpec((R, W), lambda i: (i, 0))
    return pl.pallas_call(
        body, name=name, grid=(T // R,),
        in_specs=[pl.BlockSpec((R, 2 * W), lambda i: (i, 0)), pl.BlockSpec((CONV_K, W), lambda i: (0, 0)),
                  vec, wsp, wsp, vec, vec, vec],
        out_specs=[row, row],
        out_shape=[jax.ShapeDtypeStruct((T, W), ACT_DTYPE), jax.ShapeDtypeStruct((T, W), F32)],
        scratch_shapes=[pltpu.VMEM((SUBLANES, W), F32), pltpu.VMEM((SUBLANES, W), F32)],
        compiler_params=_cparams("arbitrary"),
    )(proj, conv_w, conv_b, wa, wx, ba, bx, sp)


def _rglru_bwd(proj, conv_w, conv_b, wa, wx, ba, bx, sp, h, dhg, name):
    T = proj.shape[0]
    R = _tile(T, 256)
    nb = T // R
    r8 = R // SUBLANES
    W = D_MODEL

    def body(p_ref, pp_ref, cw_ref, cb_ref, wa_ref, wx_ref, ba_ref, bx_ref, sp_ref, h_ref, hp_ref, dhg_ref,
             dp_ref, dcw_ref, dcb_ref, dwa_ref, dwx_ref, dba_ref, dbx_ref, dsp_ref, lam_ref, nxt_ref):
        step = pl.program_id(0)
        blk = nb - 1 - step

        @pl.when(step == 0)
        def _():
            lam_ref[...] = jnp.zeros_like(lam_ref)
            nxt_ref[...] = jnp.zeros_like(nxt_ref)

        xr = p_ref[:, :W]
        gate = p_ref[:, W:]
        first = blk > 0
        prev8 = jnp.where(first, pp_ref[:, :W], 0.0)
        hprev8 = jnp.where(first, hp_ref[...], 0.0)
        cw = cw_ref[...]
        spv = sp_ref[...]
        xc = _conv(xr, prev8, cw) + cb_ref[...]
        r, ig, a, s = _lru_gates(xc, wa_ref, wx_ref, ba_ref[...], bx_ref[...], spv)
        hv = h_ref[...]
        dhg_v = dhg_ref[...]
        dgate = dhg_v * hv * _dgelu(gate)
        dh = dhg_v * _gelu(gate)
        row = lax.broadcasted_iota(jnp.int32, (R, W), 0)
        last = row == R - 1
        a_up = jnp.where(last, 0.0, pltpu.roll(a, R - 1, axis=0))
        lam = _scan_up(a_up, dh + jnp.where(last, lam_ref[0:1, :], 0.0))
        lam_ref[...] = (a * lam)[:SUBLANES]
        h_dn = _shift_down(hv, hprev8, 1)
        da = lam * h_dn
        bx_in = ig * xc
        dsv = lam * bx_in
        dig = lam * s * xc
        dxc = lam * s * ig
        dla = da * a - dsv * (a * a) / s
        dr = dla * (-LRU_C) * spv
        dsp = jnp.sum(dla * (-LRU_C) * r, axis=0, keepdims=True)
        dpr = dr * r * (1.0 - r)
        dpi = dig * ig * (1.0 - ig)
        dxc_parts, dwa_parts, dwx_parts = [], [], []
        for hb in range(LRU_BLOCKS):
            sl = slice(hb * LRU_BLOCK_W, (hb + 1) * LRU_BLOCK_W)
            xb, gr, gi = xc[:, sl], dpr[:, sl], dpi[:, sl]
            dxc_parts.append(_dot(gr, wa_ref[hb], NT) + _dot(gi, wx_ref[hb], NT))
            dwa_parts.append(_dot(xb, gr, TN))
            dwx_parts.append(_dot(xb, gi, TN))
        dxc = dxc + jnp.concatenate(dxc_parts, axis=1)
        dxr, dcw = _conv_bwd(xr, prev8, cw, dxc, nxt_ref[...])
        nxt_ref[...] = dxc[:SUBLANES]
        dp_ref[:, :W] = dxr.astype(ACT_DTYPE)
        dp_ref[:, W:] = dgate.astype(ACT_DTYPE)
        dcb = jnp.sum(dxc, axis=0, keepdims=True)
        dba = jnp.sum(dpr, axis=0, keepdims=True)
        dbx = jnp.sum(dpi, axis=0, keepdims=True)

        @pl.when(step == 0)
        def _():
            for j in range(CONV_K):
                dcw_ref[j:j + 1, :] = dcw[j]
            dcb_ref[...] = dcb
            dba_ref[...] = dba
            dbx_ref[...] = dbx
            dsp_ref[...] = dsp
            for hb in range(LRU_BLOCKS):
                dwa_ref[hb] = dwa_parts[hb]
                dwx_ref[hb] = dwx_parts[hb]

        @pl.when(step > 0)
        def _():
            for j in range(CONV_K):
                dcw_ref[j:j + 1, :] += dcw[j]
            dcb_ref[...] += dcb
            dba_ref[...] += dba
            dbx_ref[...] += dbx
            dsp_ref[...] += dsp
            for hb in range(LRU_BLOCKS):
                dwa_ref[hb] += dwa_parts[hb]
                dwx_ref[hb] += dwx_parts[hb]

    rv = lambda i: nb - 1 - i
    before = lambda i: jnp.maximum((nb - 1 - i) * r8 - 1, 0)
    vec = pl.BlockSpec((1, W), lambda i: (0, 0))
    cws = pl.BlockSpec((CONV_K, W), lambda i: (0, 0))
    wsp = pl.BlockSpec((LRU_BLOCKS, LRU_BLOCK_W, LRU_BLOCK_W), lambda i: (0, 0, 0))
    row = pl.BlockSpec((R, W), lambda i: (rv(i), 0))
    wshape = jax.ShapeDtypeStruct((LRU_BLOCKS, LRU_BLOCK_W, LRU_BLOCK_W), F32)
    vshape = jax.ShapeDtypeStruct((1, W), F32)
    return pl.pallas_call(
        body, name=name, grid=(nb,),
        in_specs=[pl.BlockSpec((R, 2 * W), lambda i: (rv(i), 0)), pl.BlockSpec((SUBLANES, 2 * W), lambda i: (before(i), 0)),
                  cws, vec, wsp, wsp, vec, vec, vec, row, pl.BlockSpec((SUBLANES, W), lambda i: (before(i), 0)), row],
        out_specs=[pl.BlockSpec((R, 2 * W), lambda i: (rv(i), 0)), cws, vec, wsp, wsp, vec, vec, vec],
        out_shape=[jax.ShapeDtypeStruct((T, 2 * W), ACT_DTYPE), jax.ShapeDtypeStruct((CONV_K, W), F32), vshape,
                   wshape, wshape, vshape, vshape, vshape],
        scratch_shapes=[pltpu.VMEM((SUBLANES, W), F32), pltpu.VMEM((SUBLANES, W), F32)],
        compiler_params=_cparams("arbitrary"),
    )(proj, proj, conv_w, conv_b, wa, wx, ba, bx, sp, h, h, dhg)


MESH = pl.DeviceIdType.MESH
ANY = pl.BlockSpec(memory_space=pl.ANY)


def _position():
    x, y, c = lax.axis_index("x"), lax.axis_index("y"), lax.axis_index("c")
    other_chips = [(1 - x, y), (x, 1 - y), (1 - x, 1 - y)]
    return x, y, c, other_chips


def _all_gather_weights(shards, name):
    n = len(shards)

    def body(*refs):
        ins, outs = refs[:n], refs[n:2 * n]
        send_sems, recv_sems = refs[2 * n:]
        x, y, c, chips = _position()
        me = 2 * x + y
        sibling = (x, y, 1 - c)

        def rcopy(t, k, src, dst, to):
            return pltpu.make_async_remote_copy(src_ref=src, dst_ref=dst, send_sem=send_sems.at[t, k],
                                                recv_sem=recv_sems.at[t, k], device_id=to, device_id_type=MESH)

        started = []
        for t in range(n):
            for j, (cx, cy) in enumerate(chips):
                cp = rcopy(t, j, ins[t].at[c], outs[t].at[me, c], (cx, cy, c))
                cp.start()
                started.append(cp)
        for t in range(n):
            for j, (cx, cy) in enumerate(chips):
                blk = outs[t].at[2 * cx + cy, c]
                rcopy(t, j, blk, blk, (cx, cy, c)).wait_recv()
                cp = rcopy(t, 3 + j, blk, blk, sibling)
                cp.start()
                started.append(cp)
        for t in range(n):
            for j, (cx, cy) in enumerate(chips):
                blk = outs[t].at[2 * cx + cy, 1 - c]
                rcopy(t, 3 + j, blk, blk, sibling).wait_recv()
        for cp in started:
            cp.wait_send()

    return pl.pallas_call(
        body, name=name, in_specs=[ANY] * n, out_specs=[ANY] * n,
        out_shape=[jax.ShapeDtypeStruct((N_CHIPS,) + s.shape, s.dtype) for s in shards],
        scratch_shapes=[pltpu.SemaphoreType.DMA((n, 6)), pltpu.SemaphoreType.DMA((n, 6))],
    )(*shards)


HBM = pl.BlockSpec(memory_space=pltpu.HBM)
SEM = pl.BlockSpec(memory_space=pltpu.SEMAPHORE)
EFFECT = pltpu.SideEffectType.DATAFLOW_SIDE_EFFECTING


def _gather_start(own, land, after, name):
    n = len(own)
    L = own[0].shape[0]
    n_sem = n * L

    def body(*refs):
        own_refs, land_refs = refs[:n], refs[n:2 * n]
        send_sems = refs[2 * n + 1:2 * n + 1 + n_sem]
        recv_sems = refs[2 * n + 1 + n_sem:2 * n + 1 + 2 * n_sem]
        x, y, c, chips = _position()
        me = 2 * x + y
        for layer in range(L):
            for t in range(n):
                for cx, cy in chips:
                    pltpu.make_async_remote_copy(
                        src_ref=own_refs[t].at[layer], dst_ref=land_refs[t].at[me, layer],
                        send_sem=send_sems[t * L + layer], recv_sem=recv_sems[t * L + layer],
                        device_id=(cx, cy, c), device_id_type=MESH).start()

    sems = (pltpu.SemaphoreType.DMA(()),) * (2 * n_sem)
    thru = [pltpu.HBM(a.shape, a.dtype) for a in list(own) + list(land)]
    out = pl.pallas_call(
        body, name=name, out_shape=(*sems, *thru),
        in_specs=[HBM] * (2 * n) + [pl.BlockSpec(memory_space=pl.ANY)], out_specs=(SEM,) * (2 * n_sem) + (HBM,) * (2 * n),
        input_output_aliases={i: 2 * n_sem + i for i in range(2 * n)},
        compiler_params=pltpu.CompilerParams(has_side_effects=EFFECT),
    )(*[pltpu.with_memory_space_constraint(a, pltpu.HBM) for a in list(own) + list(land)], after)
    send_sems, recv_sems, thru = out[:n_sem], out[n_sem:2 * n_sem], out[2 * n_sem:]
    return ([send_sems[t * L:(t + 1) * L] for t in range(n)], [recv_sems[t * L:(t + 1) * L] for t in range(n)],
            list(thru[:n]), list(thru[n:]))


def _gather_wait(send_sems, recv_sems, own, land, layer, after, name):
    n = len(own)

    def body(*refs):
        land_refs = refs[n:2 * n]
        s_sems, r_sems = refs[2 * n:3 * n], refs[3 * n:4 * n]
        x, y, c, _ = _position()
        for t in range(n):
            three = land_refs[t].at[pl.ds(0, N_CHIPS - 1), layer]
            cp = pltpu.make_async_remote_copy(src_ref=three, dst_ref=three, send_sem=s_sems[t], recv_sem=r_sems[t],
                                              device_id=(x, y, c), device_id_type=MESH)
            cp.wait_send()
            cp.wait_recv()

    thru = [pltpu.HBM(a.shape, a.dtype) for a in list(own) + list(land)]
    out = pl.pallas_call(
        body, name=name, out_shape=tuple(thru),
        in_specs=[HBM] * (2 * n) + [SEM] * (2 * n) + [pl.BlockSpec(memory_space=pl.ANY)], out_specs=(HBM,) * (2 * n),
        input_output_aliases={i: i for i in range(2 * n)},
        compiler_params=pltpu.CompilerParams(has_side_effects=EFFECT),
    )(*own, *land, *[s[layer] for s in send_sems], *[r[layer] for r in recv_sems], after)
    return list(out[:n]), list(out[n:])


def _rs_to_sibling(grads, name):
    n = len(grads)

    def body(*refs):
        ins, outs = refs[:n], refs[n:2 * n]
        send_sems, recv_sems = refs[2 * n:]
        x, y, c, _ = _position()
        cps = [pltpu.make_async_remote_copy(src_ref=ins[t].at[:, 1 - c], dst_ref=outs[t], send_sem=send_sems.at[t],
                                            recv_sem=recv_sems.at[t], device_id=(x, y, 1 - c), device_id_type=MESH)
               for t in range(n)]
        for cp in cps:
            cp.start()
        for cp in cps:
            cp.wait()

    return pl.pallas_call(
        body, name=name, in_specs=[ANY] * n, out_specs=[ANY] * n,
        out_shape=[jax.ShapeDtypeStruct((N_CHIPS,) + g.shape[2:], g.dtype) for g in grads],
        scratch_shapes=[pltpu.SemaphoreType.DMA((n,)), pltpu.SemaphoreType.DMA((n,))],
    )(*grads)


def _rs_across_start(parts, recv, name):
    n = len(parts)

    def body(*refs):
        part_refs, recv_refs = refs[:n], refs[n:2 * n]
        send_sems, recv_sems = refs[2 * n:3 * n], refs[3 * n:4 * n]
        token_ref = refs[-1]
        x, y, c, chips = _position()
        me = 2 * x + y
        for t in range(n):
            for cx, cy in chips:
                pltpu.make_async_remote_copy(src_ref=part_refs[t].at[2 * cx + cy], dst_ref=recv_refs[t].at[me],
                                             send_sem=send_sems[t], recv_sem=recv_sems[t], device_id=(cx, cy, c),
                                             device_id_type=MESH).start()
        token_ref[...] = jnp.zeros_like(token_ref)

    sems = (pltpu.SemaphoreType.DMA(()),) * (2 * n)
    thru = [pltpu.HBM(a.shape, a.dtype) for a in list(parts) + list(recv)]
    out = pl.pallas_call(
        body, name=name, out_shape=(*sems, *thru, jax.ShapeDtypeStruct((SUBLANES, LANES), F32)),
        in_specs=[HBM] * (2 * n), out_specs=(SEM,) * (2 * n) + (HBM,) * (2 * n) + (pl.BlockSpec(memory_space=pltpu.VMEM),),
        input_output_aliases={i: 2 * n + i for i in range(2 * n)},
        compiler_params=pltpu.CompilerParams(has_side_effects=EFFECT),
    )(*[pltpu.with_memory_space_constraint(a, pltpu.HBM) for a in list(parts) + list(recv)])
    return out[:n], out[n:2 * n], list(out[2 * n:3 * n]), list(out[3 * n:4 * n]), out[-1]


def _rs_across_wait(send_sems, recv_sems, parts, recv, after, name):
    n = len(parts)

    def body(*refs):
        recv_refs = refs[n:2 * n]
        s_sems, r_sems = refs[2 * n:3 * n], refs[3 * n:4 * n]
        x, y, c, _ = _position()
        for t in range(n):
            three = recv_refs[t].at[pl.ds(0, N_CHIPS - 1)]
            cp = pltpu.make_async_remote_copy(src_ref=three, dst_ref=three, send_sem=s_sems[t], recv_sem=r_sems[t],
                                              device_id=(x, y, c), device_id_type=MESH)
            cp.wait_send()
            cp.wait_recv()

    thru = [pltpu.HBM(a.shape, a.dtype) for a in list(parts) + list(recv)]
    out = pl.pallas_call(
        body, name=name, out_shape=tuple(thru),
        in_specs=[HBM] * (2 * n) + [SEM] * (2 * n) + [pl.BlockSpec(memory_space=pl.ANY)], out_specs=(HBM,) * (2 * n),
        input_output_aliases={i: i for i in range(2 * n)},
        compiler_params=pltpu.CompilerParams(has_side_effects=EFFECT),
    )(*parts, *recv, *send_sems, *recv_sems, after)
    return list(out[:n]), list(out[n:])


def _rs_join_halves(halves, name):
    n = len(halves)

    def body(*refs):
        ins, outs = refs[:n], refs[n:2 * n]
        send_sems, recv_sems = refs[2 * n:]
        x, y, c, _ = _position()
        cps = [pltpu.make_async_remote_copy(src_ref=ins[t].at[c], dst_ref=outs[t].at[c], send_sem=send_sems.at[t],
                                            recv_sem=recv_sems.at[t], device_id=(x, y, 1 - c), device_id_type=MESH)
               for t in range(n)]
        for cp in cps:
            cp.start()
        for t in range(n):
            blk = outs[t].at[1 - c]
            pltpu.make_async_remote_copy(src_ref=blk, dst_ref=blk, send_sem=send_sems.at[t], recv_sem=recv_sems.at[t],
                                         device_id=(x, y, 1 - c), device_id_type=MESH).wait_recv()
        for cp in cps:
            cp.wait_send()

    return pl.pallas_call(
        body, name=name, in_specs=[ANY] * n, out_specs=[ANY] * n,
        out_shape=[jax.ShapeDtypeStruct(h.shape, h.dtype) for h in halves],
        input_output_aliases={t: t for t in range(n)},
        scratch_shapes=[pltpu.SemaphoreType.DMA((n,)), pltpu.SemaphoreType.DMA((n,))],
    )(*halves)


def _all_gather_small(block, name):
    m_per, n = block.shape

    def body(x_ref, out_ref, send_sems, recv_sems, local_sem):
        x, y, c, chips = _position()
        me, sibling = (x, y, c), (x, y, 1 - c)

        def rows(px, py, pc):
            return out_ref.at[pl.ds((4 * px + 2 * py + pc) * m_per, m_per), :]

        def copy(k, blk, to, src=None):
            return pltpu.make_async_remote_copy(
                src_ref=rows(*blk) if src is None else src, dst_ref=rows(*blk), send_sem=send_sems.at[k],
                recv_sem=recv_sems.at[k], device_id=to, device_id_type=MESH)

        mine = pltpu.make_async_copy(x_ref, rows(*me), local_sem)
        mine.start()
        first = [copy(0, me, sibling, src=x_ref)]
        first += [copy(1 + j, me, (*chip, c), src=x_ref) for j, chip in enumerate(chips)]
        for cp in first:
            cp.start()
        passed = [copy(4 + j, (*chip, c), sibling) for j, chip in enumerate(chips)]
        for j, chip in enumerate(chips):
            copy(1 + j, (*chip, c), me).wait_recv()
            passed[j].start()
        copy(0, sibling, me).wait_recv()
        for j, chip in enumerate(chips):
            copy(4 + j, (*chip, 1 - c), me).wait_recv()
        for cp in first + passed:
            cp.wait_send()
        mine.wait()

    return pl.pallas_call(
        body, name=name, out_shape=jax.ShapeDtypeStruct((N_DEV * m_per, n), block.dtype),
        in_specs=[pl.BlockSpec(memory_space=pltpu.VMEM)], out_specs=pl.BlockSpec(memory_space=pltpu.VMEM),
        scratch_shapes=[pltpu.SemaphoreType.DMA((7,)), pltpu.SemaphoreType.DMA((7,)), pltpu.SemaphoreType.DMA],
    )(block)


def _row_tile(R, n):
    budget = 1 << 19
    if R * n <= budget or R % SUBLANES:
        return R
    t = R
    while t * n > budget and t % (2 * SUBLANES) == 0:
        t //= 2
    return t


def _pair_sum(g, recv, c_arr, name):
    _, _, R, n = g.shape
    tr = _row_tile(R, n)

    def body(c_ref, g_ref, r_ref, o_ref):
        o_ref[...] = (g_ref[...] + r_ref[...]).astype(ICI_DTYPE)

    grid_spec = pltpu.PrefetchScalarGridSpec(
        num_scalar_prefetch=1, grid=(N_CHIPS, R // tr),
        in_specs=[pl.BlockSpec((None, None, tr, n), lambda p, i, c: (p, c[0], i, 0)),
                  pl.BlockSpec((None, tr, n), lambda p, i, c: (p, i, 0))],
        out_specs=pl.BlockSpec((None, tr, n), lambda p, i, c: (p, i, 0)))
    return pl.pallas_call(
        body, name=name, grid_spec=grid_spec, out_shape=jax.ShapeDtypeStruct(recv.shape, ICI_DTYPE),
        compiler_params=_cparams("parallel", "parallel"),
    )(c_arr, g, recv)


def _chip_sum(recv, own, where, name):
    _, R, n = recv.shape
    tr = _row_tile(R, n)

    def body(w_ref, r0, r1, r2, r3, own_ref, o_ref):
        me = w_ref[0]
        terms = [jnp.where(me == k, own_ref[...], r[...]).astype(F32) for k, r in enumerate((r0, r1, r2, r3))]
        o_ref[...] = ((terms[0] + terms[1]) + terms[2]) + terms[3]

    def slot(k):
        return pl.BlockSpec((None, tr, n), lambda i, w: (w[1 + k], i, 0))

    grid_spec = pltpu.PrefetchScalarGridSpec(
        num_scalar_prefetch=1, grid=(R // tr,),
        in_specs=[slot(0), slot(1), slot(2), slot(3), pl.BlockSpec((None, tr, n), lambda i, w: (w[0], i, 0))],
        out_specs=pl.BlockSpec((None, tr, n), lambda i, w: (w[5], i, 0)))
    return pl.pallas_call(
        body, name=name, grid_spec=grid_spec, out_shape=jax.ShapeDtypeStruct((2, R, n), F32),
        compiler_params=_cparams("parallel"),
    )(where, recv, recv, recv, recv, own)


ADAM_C1 = 1.0 / (1.0 - ADAM_B1 ** ADAM_STEP)
ADAM_C2 = 1.0 / (1.0 - ADAM_B2 ** ADAM_STEP)


def _adamw_math(w, g, m, v):
    m = ADAM_B1 * m + (1.0 - ADAM_B1) * g
    v = ADAM_B2 * v + (1.0 - ADAM_B2) * (g * g)
    delta = -ADAM_LR * ((m * ADAM_C1) / (jnp.sqrt(v * ADAM_C2) + ADAM_EPS) + ADAM_WD * w)
    return delta, m, v


def _adamw(w, g, m, v, name):
    R, n = w.shape
    tr = _row_tile(R, n)

    def body(w_ref, g_ref, m_ref, v_ref, d_ref, nm_ref, nv_ref):
        d_ref[...], nm_ref[...], nv_ref[...] = _adamw_math(w_ref[...], g_ref[...], m_ref[...], v_ref[...])

    spec = pl.BlockSpec((tr, n), lambda i: (i, 0))
    return pl.pallas_call(
        body, name=name, grid=(R // tr,), in_specs=[spec] * 4, out_specs=[spec] * 3,
        out_shape=[jax.ShapeDtypeStruct((R, n), F32)] * 3, compiler_params=_cparams("parallel"),
    )(w, g, m, v)


def _adamw_half(w, g, m, v, half, prev, name):
    R, n = w.shape
    tr = _row_tile(R // 2, n)
    off = half * (R // 2 // tr)

    def body(*refs):
        w_ref, g_ref, m_ref, v_ref = refs[:4]
        go_ref, d_ref, nm_ref, nv_ref = refs[-4:]
        gv = g_ref[...]
        go_ref[...] = gv
        d_ref[...], nm_ref[...], nv_ref[...] = _adamw_math(w_ref[...], gv, m_ref[...], v_ref[...])

    rows = pl.BlockSpec((tr, n), lambda i: (i + off, 0))
    carried = [] if prev is None else list(prev)
    return pl.pallas_call(
        body, name=name, grid=(R // 2 // tr,),
        in_specs=[rows, pl.BlockSpec((tr, n), lambda i: (i, 0)), rows, rows] + [ANY] * len(carried),
        out_specs=[rows] * 4, out_shape=[jax.ShapeDtypeStruct((R, n), F32)] * 4,
        input_output_aliases={4 + i: i for i in range(len(carried))},
        compiler_params=_cparams("parallel"),
    )(w, g, m, v, *carried)


def _adamw_small(w, gall, m, v, name):
    M, n = w.shape

    def body(w_ref, g_ref, m_ref, v_ref, gs_ref, d_ref, nm_ref, nv_ref):
        g = g_ref[0:M, :]
        for d in range(1, N_DEV):
            g = g + g_ref[d * M:(d + 1) * M, :]
        gs_ref[...] = g
        d_ref[...], nm_ref[...], nv_ref[...] = _adamw_math(w_ref[...], g, m_ref[...], v_ref[...])

    return pl.pallas_call(
        body, name=name, out_shape=[jax.ShapeDtypeStruct((M, n), F32)] * 4,
    )(w, gall, m, v)


SMALL_ROWS = 24
MID_ROWS = 4


def _pack_small(ln1_g, ln1_b, ln2_g, ln2_b, norm_w, sinks, a_log, dt_bias):
    mixed = jnp.concatenate([norm_w.reshape(-1), sinks.reshape(-1), a_log.reshape(-1), dt_bias.reshape(-1)])
    mixed = jnp.pad(mixed, (0, D_MODEL - mixed.shape[0]))[None]
    pad = jnp.zeros((SMALL_ROWS - 4 * DEPTH - 1, D_MODEL), F32)
    return jnp.concatenate([ln1_g, ln1_b, ln2_g, ln2_b, mixed, pad], axis=0)


def _unpack_small(p):
    mixed = p[4 * DEPTH]
    return (p[0:4], p[4:8], p[8:12], p[12:16], mixed[0:256].reshape(2, 128), mixed[256:272].reshape(2, 8),
            mixed[272:280].reshape(2, 4), mixed[280:288].reshape(2, 4))


def _pack_mid(conv_w, rconv_w, rconv_b, b_a, b_x, lam):
    lead = conv_w.shape[0]
    flat = jnp.concatenate([conv_w.reshape(lead, -1), rconv_w.reshape(lead, -1), rconv_b, b_a, b_x, lam], axis=1)
    return jnp.pad(flat, ((0, 0), (0, MID_ROWS * D_MODEL - flat.shape[1]))).reshape(lead, MID_ROWS, D_MODEL)


def _unpack_mid(p):
    lead = p.shape[:-2]
    f = p.reshape(lead + (MID_ROWS * D_MODEL,))
    return (f[..., 0:1536].reshape(lead + (4, 384)), f[..., 1536:2560].reshape(lead + (4, 256)),
            f[..., 2560:2816], f[..., 2816:3072], f[..., 3072:3328], f[..., 3328:3584])


def _cols_from_chips(g):
    p, L, R, n = g.shape
    return g.transpose(1, 2, 0, 3).reshape(L, R, p * n)


def _rows_from_chips(g):
    p, L, R, n = g.shape
    return g.transpose(1, 0, 2, 3).reshape(L, p * R, n)


def _cols_to_chips(g):
    L, R, n4 = g.shape
    return g.reshape(L, R, N_CHIPS, n4 // N_CHIPS).transpose(2, 0, 1, 3)


def _rows_to_chips(g):
    L, R4, n = g.shape
    return g.reshape(L, N_CHIPS, R4 // N_CHIPS, n).transpose(1, 0, 2, 3)


def _halves(a):
    return a.reshape(2, -1, a.shape[-1])


def _pad_hyb_cols(w):
    z = jnp.zeros(w.shape[:-1] + (HYB_PAD - HP_BG - 2 * B_HEADS,), w.dtype)
    return jnp.concatenate([w[..., 0:512], w[..., 768:2304], w[..., 2304:2816], w[..., 512:768], w[..., 2816:2824], z], axis=-1)


def _unpad_hyb_cols(w):
    return jnp.concatenate([w[..., 0:512], w[..., 2560:2816], w[..., 512:2048], w[..., 2048:2560], w[..., 2816:2824]], axis=-1)


def _hybrid_fwd(x, W, j, tables, sfx):
    cos, sin_s = tables
    T = x.shape[0]
    N = T // B_CHUNK
    proj = _matmul(x, W["hyb_w_in"][j], "nn", "hyb_in" + sfx)
    sinks_b = jnp.broadcast_to(W["hyb_sinks"][j][:, None], (A_Q_HEADS, LANES))
    o_a, lse = _attn_fwd(proj, cos, sin_s, sinks_b, "attn_fwd" + sfx)
    qkvc = _dnconv_fwd(proj, W["hyb_conv_w"][j], "dnconv_fwd" + sfx)
    bg = proj[:, HP_BG:HP_BG + 2 * B_HEADS]
    beta = jax.nn.sigmoid(bg[:, :B_HEADS])
    pre = bg[:, B_HEADS:] + W["hyb_dt_bias"][j][None]
    g = -jnp.exp(W["hyb_a_log"][j])[None] * jax.nn.softplus(pre)
    grow = g.T.reshape(B_HEADS, N, 1, B_CHUNK)
    brow = beta.T.reshape(B_HEADS, N, 1, B_CHUNK)
    nw = W["hyb_norm_w"][j][None]
    o_b, states, invs = _dn_fwd(qkvc, proj, nw, grow, brow, "dn_fwd" + sfx)
    mix = jnp.concatenate([o_a, o_b], axis=1).astype(ACT_DTYPE)
    y = _matmul(mix, W["hyb_w_out"][j], "nn", "hyb_out" + sfx)
    res = dict(proj=proj, o_a=o_a, lse=lse, qkvc=qkvc, beta=beta, pre=pre, g=g, grow=grow, brow=brow,
               states=states, invs=invs, mix=mix, sinks_b=sinks_b, nw=nw)
    return y, res


def _hybrid_bwd(x, du, dub, W, j, res, tables, sfx):
    cos, sin_s = tables
    T = x.shape[0]
    proj = res["proj"]
    d_wout = _matmul(res["mix"], dub, "tn", "hyb_out_dw" + sfx)
    dmix = _matmul(dub, W["hyb_w_out"][j], "nt", "hyb_out_dx" + sfx)
    dq, dkc, dkp, dvc, dvp, dsink = _attn_bwd(proj, cos, sin_s, res["sinks_b"], res["o_a"], res["lse"], dmix,
                                               "attn_bwd" + sfx)
    zpad = jnp.zeros((WINDOW, LANES), F32)
    dk = dkc + jnp.concatenate([dkp[WINDOW:], zpad], axis=0)
    dv = dvc + jnp.concatenate([dvp[WINDOW:], zpad], axis=0)
    dqc, dkcv, dvcv, dz, dg4, dbeta4, dnw = _dn_bwd(res["qkvc"], proj, res["nw"], res["grow"], res["brow"],
                                                    res["states"], res["invs"], dmix, "dn_bwd" + sfx)
    dqkvc = jnp.concatenate([dqc, dkcv, dvcv], axis=1)
    dqkvb, dconv = _dnconv_bwd(proj, W["hyb_conv_w"][j], dqkvc, "dnconv_bwd" + sfx)
    dg = dg4.reshape(B_HEADS, T).T
    dbeta = dbeta4.reshape(B_HEADS, T).T
    beta = res["beta"]
    dbeta_logit = dbeta * beta * (1.0 - beta)
    da_logit = dg * (-jnp.exp(W["hyb_a_log"][j]))[None] * jax.nn.sigmoid(res["pre"])
    d_dt_bias = jnp.sum(da_logit, axis=0)
    d_a_log = jnp.sum(dg * res["g"], axis=0)
    zcols = jnp.zeros((T, HYB_PAD - HP_BG - 2 * B_HEADS), F32)
    dproj = jnp.concatenate([dq, dqkvb, dz, dk, dv, dbeta_logit, da_logit, zcols], axis=1).astype(ACT_DTYPE)
    d_win = _matmul(x, dproj, "tn", "hyb_in_dw" + sfx)
    dx = _matmul(dproj, W["hyb_w_in"][j], "nt", "hyb_in_dx" + sfx, epi=_epi_add_residual, extra=du)
    grads = dict(hyb_w_in=d_win, hyb_w_out=d_wout, hyb_sinks=dsink[0], hyb_conv_w=dconv, hyb_a_log=d_a_log,
                 hyb_dt_bias=d_dt_bias, hyb_norm_w=jnp.sum(dnw[:, 0, :], axis=0))
    return dx, grads


def _rec_fwd(x, W, j, sfx):
    proj = _matmul(x, W["rec_w_in"][j], "nn", "rec_in" + sfx)
    sp = jax.nn.softplus(-W["rec_lambda"][j])[None]
    hg, h = _rglru_fwd(proj, W["rec_conv_w"][j], W["rec_conv_b"][j][None], W["rec_w_a"][j], W["rec_w_x"][j],
                       W["rec_b_a"][j][None], W["rec_b_x"][j][None], sp, "rglru_fwd" + sfx)
    y = _matmul(hg, W["rec_w_out"][j], "nn", "rec_out" + sfx)
    return y, dict(proj=proj, hg=hg, h=h, sp=sp)


def _rec_bwd(x, du, dub, W, j, res, sfx):
    d_wout = _matmul(res["hg"], dub, "tn", "rec_out_dw" + sfx)
    dhg = _matmul(dub, W["rec_w_out"][j], "nt", "rec_out_dx" + sfx)
    dproj, dcw, dcb, dwa, dwx, dba, dbx, dsp = _rglru_bwd(
        res["proj"], W["rec_conv_w"][j], W["rec_conv_b"][j][None], W["rec_w_a"][j], W["rec_w_x"][j],
        W["rec_b_a"][j][None], W["rec_b_x"][j][None], res["sp"], res["h"], dhg, "rglru_bwd" + sfx)
    d_lam = dsp[0] * (-jax.nn.sigmoid(-W["rec_lambda"][j]))
    d_win = _matmul(x, dproj, "tn", "rec_in_dw" + sfx)
    dx = _matmul(dproj, W["rec_w_in"][j], "nt", "rec_in_dx" + sfx, epi=_epi_add_residual, extra=du)
    grads = dict(rec_w_in=d_win, rec_w_out=d_wout, rec_conv_w=dcw, rec_conv_b=dcb[0], rec_w_a=dwa, rec_w_x=dwx,
                 rec_b_a=dba[0], rec_b_x=dbx[0], rec_lambda=d_lam)
    return dx, grads


def _local_step(x, tgt, W, mlp_w, on_group):
    T = x.shape[0]
    tables = _rope_tables(T)
    acts = []
    xb = x.astype(ACT_DTYPE)
    for layer in range(DEPTH):
        j, sfx = layer // 2, ""
        if layer % 2 == 0:
            y, res = _hybrid_fwd(xb, W, j, tables, sfx)
        else:
            y, res = _rec_fwd(xb, W, j, sfx)
        x1, x1b = _ln_fwd(x, y, W["ln1_g"][layer][None], W["ln1_b"][layer][None], "ln_fwd")
        w1, w2, wl = mlp_w(layer, x1b)
        h1 = _matmul(x1b, w1, "nn", "mlp_up", out_dtype=ACT_DTYPE, b_chips=("j", wl))
        y2 = _matmul(h1, w2, "nn", "mlp_down", a_fn=_relu2, b_chips=("k", wl))
        x2, x2b = _ln_fwd(x1, y2, W["ln2_g"][layer][None], W["ln2_b"][layer][None], "ln_fwd")
        acts.append(dict(x=x, xb=xb, y=y, res=res, x1=x1, x1b=x1b, h1=h1, y2=y2))
        x, xb = x2, x2b
    dx, loss = _loss_head(x, tgt, "loss_head")
    per_layer = [None] * DEPTH
    d_w1 = [lax.empty((N_CHIPS, 2, D_MODEL, D_FF // N_CHIPS), F32) for _ in range(DEPTH // 2)]
    d_w2 = [lax.empty((N_CHIPS, 2, D_FF // N_CHIPS, D_MODEL), F32) for _ in range(DEPTH // 2)]
    token = None
    for layer in reversed(range(DEPTH)):
        j, a = layer // 2, acts[layer]
        ln2_g = W["ln2_g"][layer][None]
        if token is not None:
            ln2_g = ln2_g + token
        du2, du2b, dg2, db2 = _ln_bwd(a["x1"], a["y2"], ln2_g, dx, "ln_bwd")
        w1, w2, wl = mlp_w(layer, du2b)
        d_w2[j] = _matmul(a["h1"], du2b, "tn", "mlp_down_dw", a_fn=_relu2, out_chips=("i", layer % 2, d_w2[j]))
        dh1 = _matmul(du2b, w2, "nt", "mlp_down_dx", epi=_epi_drelu2, extra=a["h1"], out_dtype=ACT_DTYPE,
                      b_chips=("j", wl))
        d_w1[j] = _matmul(a["x1b"], dh1, "tn", "mlp_up_dw", out_chips=("j", layer % 2, d_w1[j]))
        dx1 = _matmul(dh1, w1, "nt", "mlp_up_dx", epi=_epi_add_residual, extra=du2, b_chips=("k", wl))
        du1, du1b, dg1, db1 = _ln_bwd(a["x"], a["y"], W["ln1_g"][layer][None], dx1, "ln_bwd")
        if layer % 2 == 0:
            dx, g = _hybrid_bwd(a["xb"], du1, du1b, W, j, a["res"], tables, "")
        else:
            dx, g = _rec_bwd(a["xb"], du1, du1b, W, j, a["res"], "")
        g.update(ln1_g=dg1[0], ln1_b=db1[0], ln2_g=dg2[0], ln2_b=db2[0])
        per_layer[layer] = g
        if layer % 2 == 0:
            token = on_group(j, per_layer[layer], per_layer[layer + 1], d_w1[j], d_w2[j])
    grads = {}
    for name in ("ln1_g", "ln1_b", "ln2_g", "ln2_b"):
        grads[name] = jnp.stack([per_layer[l][name] for l in range(DEPTH)])
    for name in ("hyb_norm_w", "hyb_sinks", "hyb_a_log", "hyb_dt_bias"):
        grads[name] = jnp.stack([per_layer[l][name] for l in (0, 2)])
    return loss, dx, grads


BIG = ("hyb_w_in", "hyb_w_out", "rec_w_in", "rec_w_out", "mlp_w1", "mlp_w2", "rec_w_a", "rec_w_x")
COL_SHARDED = ("hyb_w_in", "rec_w_in", "mlp_w1")
CHIP_MAJOR = ("mlp_w1", "mlp_w2")
MID = ("hyb_conv_w", "rec_conv_w", "rec_conv_b", "rec_b_a", "rec_b_x", "rec_lambda")
SMALL = ("ln1_g", "ln1_b", "ln2_g", "ln2_b", "hyb_norm_w", "hyb_sinks", "hyb_a_log", "hyb_dt_bias")
WEIGHTS = ("hyb_w_in", "hyb_sinks", "hyb_conv_w", "hyb_a_log", "hyb_dt_bias", "hyb_norm_w", "hyb_w_out", "rec_w_in",
           "rec_conv_w", "rec_conv_b", "rec_w_a", "rec_b_a", "rec_w_x", "rec_b_x", "rec_lambda", "rec_w_out", "ln1_g",
           "ln1_b", "mlp_w1", "mlp_w2", "ln2_g", "ln2_b")


def _gather_full_weights(w):
    wb = {k: w[k].astype(MXU_DTYPE) for k in BIG}
    shards = [_halves(wb[k][:1] if k in CHIP_MAJOR else wb[k]) for k in BIG]
    shards.append(_pack_mid(*[w[k] for k in MID]))
    got = _all_gather_weights(shards, "all_gather_weights")
    me = 2 * lax.axis_index("x") + lax.axis_index("y")
    got = [lax.dynamic_update_slice(g, s[None], (me, 0, 0, 0)) for s, g in zip(shards, got)]

    own = [wb[k][1:] for k in CHIP_MAJOR]
    land = [lax.dynamic_update_slice(lax.empty((N_CHIPS,) + o.shape, o.dtype), o[None], (me, 0, 0, 0)) for o in own]
    send_sems, recv_sems, own, land = _gather_start(own, land, got[-1], "gather_start")
    first = {k: g.reshape((N_CHIPS, 1) + w[k].shape[1:]) for k, g in zip(BIG, got[:-1]) if k in CHIP_MAJOR}
    pending = dict(own=own, land=land, waited=0)

    def mlp_w(layer, after):
        if layer == 0:
            return first["mlp_w1"], first["mlp_w2"], 0
        while pending["waited"] < layer:
            pending["own"], pending["land"] = _gather_wait(send_sems, recv_sems, pending["own"], pending["land"],
                                                           pending["waited"], after, "gather_wait_%d" % pending["waited"])
            pending["waited"] += 1
        return pending["land"][0], pending["land"][1], layer - 1

    W = {}
    for k, g in zip(BIG, got[:-1]):
        L = w[k].shape[0]
        if k in CHIP_MAJOR:
            continue
        elif k in ("rec_w_a", "rec_w_x"):
            g5 = g.reshape((N_CHIPS,) + w[k].shape)
            W[k] = g5.transpose(1, 2, 0, 3, 4).reshape(L, LRU_BLOCKS, LRU_BLOCK_W, LRU_BLOCK_W)
        else:
            g4 = g.reshape((N_CHIPS,) + w[k].shape)
            W[k] = _cols_from_chips(g4) if k in COL_SHARDED else _rows_from_chips(g4)
    W["hyb_w_in"] = _pad_hyb_cols(W["hyb_w_in"])
    conv_w, rconv_w, rconv_b, b_a, b_x, lam = _unpack_mid(got[-1])
    W["hyb_conv_w"] = conv_w.transpose(1, 2, 0, 3).reshape(2, CONV_K, 3 * B_W)
    W["rec_conv_w"] = rconv_w.transpose(1, 2, 0, 3).reshape(2, CONV_K, D_MODEL)
    for k, v in (("rec_conv_b", rconv_b), ("rec_b_a", b_a), ("rec_b_x", b_x), ("rec_lambda", lam)):
        W[k] = v.transpose(1, 0, 2).reshape(2, D_MODEL)
    for k in SMALL:
        W[k] = w[k]
    return W, mlp_w


def _group_by_chip(gh, gr, d_w1, d_w2):
    g = dict(gh, **gr)
    g["hyb_w_in"] = _unpad_hyb_cols(g["hyb_w_in"])
    out = []
    for k in BIG:
        if k == "mlp_w1":
            v = d_w1
        elif k == "mlp_w2":
            v = d_w2
        elif k in ("rec_w_a", "rec_w_x"):
            v = g[k].reshape(1, LRU_BLOCKS, N_CHIPS, LRU_BLOCK_W // N_CHIPS, LRU_BLOCK_W).transpose(2, 0, 1, 3, 4)
        else:
            v = _cols_to_chips(g[k][None]) if k in COL_SHARDED else _rows_to_chips(g[k][None])
        out.append(v.reshape(N_CHIPS, 2, -1, v.shape[-1]))
    conv_w = g["hyb_conv_w"].reshape(CONV_K, N_CHIPS, -1).transpose(1, 0, 2)
    rconv_w = g["rec_conv_w"].reshape(CONV_K, N_CHIPS, -1).transpose(1, 0, 2)
    vecs = [g[k].reshape(N_CHIPS, -1) for k in ("rec_conv_b", "rec_b_a", "rec_b_x", "rec_lambda")]
    out.append(_pack_mid(conv_w, rconv_w, *vecs).reshape(N_CHIPS, 2, MID_ROWS // 2, D_MODEL))
    return out


def kernel(x, hyb_w_in, hyb_sinks, hyb_conv_w, hyb_a_log, hyb_dt_bias, hyb_norm_w, hyb_w_out, rec_w_in, rec_conv_w, rec_conv_b, rec_w_a, rec_b_a, rec_w_x, rec_b_x, rec_lambda, rec_w_out, ln1_g, ln1_b, mlp_w1, mlp_w2, ln2_g, ln2_b, loss_target, m_hyb_w_in, m_hyb_sinks, m_hyb_conv_w, m_hyb_a_log, m_hyb_dt_bias, m_hyb_norm_w, m_hyb_w_out, m_rec_w_in, m_rec_conv_w, m_rec_conv_b, m_rec_w_a, m_rec_b_a, m_rec_w_x, m_rec_b_x, m_rec_lambda, m_rec_w_out, m_ln1_g, m_ln1_b, m_mlp_w1, m_mlp_w2, m_ln2_g, m_ln2_b, v_hyb_w_in, v_hyb_sinks, v_hyb_conv_w, v_hyb_a_log, v_hyb_dt_bias, v_hyb_norm_w, v_hyb_w_out, v_rec_w_in, v_rec_conv_w, v_rec_conv_b, v_rec_w_a, v_rec_b_a, v_rec_w_x, v_rec_b_x, v_rec_lambda, v_rec_w_out, v_ln1_g, v_ln1_b, v_mlp_w1, v_mlp_w2, v_ln2_g, v_ln2_b):
    args = locals()
    w = {k: args[k] for k in WEIGHTS}
    m = {k: args["m_" + k] for k in WEIGHTS}
    v = {k: args["v_" + k] for k in WEIGHTS}

    W, mlp_w = _gather_full_weights(w)

    core = lax.axis_index("c").astype(jnp.int32)
    me = (2 * lax.axis_index("x") + lax.axis_index("y")).astype(jnp.int32)
    slots = jnp.arange(N_CHIPS, dtype=jnp.int32)
    where = jnp.concatenate([me[None], jnp.where(slots == me, (slots + 1) % N_CHIPS, slots), core[None]])
    state = {}

    def on_group(group, gh, gr, d_w1, d_w2):
        by_chip = _group_by_chip(gh, gr, d_w1, d_w2)
        from_sibling = _rs_to_sibling(by_chip, "rs_to_sibling")
        pair = [_pair_sum(g, r, core[None], "pair_sum") for g, r in zip(by_chip, from_sibling)]
        recv = [lax.empty(p.shape, p.dtype) for p in pair]
        state[group] = _rs_across_start(pair, recv, "rs_across_start_%d" % group)
        return state[group][4][0, 0]

    def finish_group(group, after, prev):
        send_sems, recv_sems, pair, recv, _ = state[group]
        pair, from_chips = _rs_across_wait(send_sems, recv_sems, pair, recv, after, "rs_across_wait_%d" % group)
        half = [_chip_sum(r, p, where, "chip_sum") for r, p in zip(from_chips, pair)]
        joined = _rs_join_halves(half, "rs_join_halves")
        outs = {}
        for k, g in zip(BIG, joined[:-1]):
            n = g.shape[-1]
            outs[k] = _adamw_half(w[k].reshape(-1, n), g.reshape(-1, n), m[k].reshape(-1, n), v[k].reshape(-1, n),
                                  group, None if prev is None else prev[k], "adamw")
        return outs, joined[-1]

    loss, dx, grads = _local_step(x[0], loss_target[0], W, mlp_w, on_group)
    loss = lax.psum(loss[0, 0], ("x", "y", "c"))
    g_out, d_out, m_out, v_out = {}, {}, {}, {}

    upper, mid_upper = finish_group(1, state[0][4], None)
    small_g = _pack_small(*[grads[k] for k in SMALL])
    small_all = _all_gather_small(small_g, "all_gather_small")
    sw, sm, sv = (_pack_small(*[t[k] for k in SMALL]) for t in (w, m, v))
    sg, sd, snm, snv = _adamw_small(sw, small_all, sm, sv, "adamw_small")
    for dst, packed in ((g_out, sg), (d_out, sd), (m_out, snm), (v_out, snv)):
        for k, val in zip(SMALL, _unpack_small(packed)):
            dst[k] = val

    both, mid_lower = finish_group(0, sg, upper)
    for k in BIG:
        shape = w[k].shape
        g_out[k], d_out[k], m_out[k], v_out[k] = (t.reshape(shape) for t in both[k])
    mid_w, mid_m, mid_v = (_pack_mid(*[t[k] for k in MID]).reshape(-1, D_MODEL) for t in (w, m, v))
    mid_g = jnp.concatenate([mid_lower.reshape(-1, D_MODEL), mid_upper.reshape(-1, D_MODEL)])
    mid_d, mid_nm, mid_nv = _adamw(mid_w, mid_g, mid_m, mid_v, "adamw_mid")
    for dst, packed in ((g_out, mid_g), (d_out, mid_d), (m_out, mid_nm), (v_out, mid_nv)):
        for k, val in zip(MID, _unpack_mid(packed.reshape(2, MID_ROWS, D_MODEL))):
            dst[k] = val.reshape(w[k].shape)

    return (loss, dx[None], *[g_out[k] for k in WEIGHTS], *[d_out[k] for k in WEIGHTS],
            *[m_out[k] for k in WEIGHTS], *[v_out[k] for k in WEIGHTS])
```

```python
import functools

import jax
import jax.numpy as jnp
import numpy as np
from jax import lax
from jax.experimental import pallas as pl
from jax.experimental.pallas import tpu as pltpu

F32 = jnp.float32
MXU_DTYPE = jnp.bfloat16
ACT_DTYPE = jnp.bfloat16
ICI_DTYPE = jnp.bfloat16

D_MODEL = 1024
DEPTH = 4
A_HEAD_DIM = 64
A_Q_HEADS = 8
A_KV_HEADS = 2
WINDOW = 128
ROPE_THETA = 10000.0
B_HEADS = 4
B_HEAD_DIM = 128
B_CHUNK = 64
CONV_K = 4
LRU_BLOCKS = 4
LRU_BLOCK_W = D_MODEL // LRU_BLOCKS
LRU_C = 8.0
D_FF = 4 * D_MODEL
A_Q_W = A_Q_HEADS * A_HEAD_DIM
A_KV_W = A_KV_HEADS * A_HEAD_DIM
B_W = B_HEADS * B_HEAD_DIM
HYB_PROJ = A_Q_W + 2 * A_KV_W + 4 * B_W + 2 * B_HEADS
DN_ALPHA = (2 * DEPTH) ** 0.25
LN_EPS = 1e-5
NORM_EPS = 1e-6
ADAM_LR = 0.001
ADAM_B1 = 0.9
ADAM_B2 = 0.999
ADAM_EPS = 1e-08
ADAM_WD = 0.01
ADAM_STEP = 10

HP_Q = 0
HP_QKVB = 512
HP_Z = 2048
HP_K = 2560
HP_V = 2688
HP_BG = 2816
HYB_PAD = 3072

N_CHIPS = 4
N_DEV = 8
V7X_VMEM_LIMIT = 48 * 1024 * 1024
LANES = 128
SUBLANES = 8
NEG_BIG = -1e30

NN = (((1,), (0,)), ((), ()))
NT = (((1,), (1,)), ((), ()))
TN = (((0,), (0,)), ((), ()))


def _cparams(*sem):
    return pltpu.CompilerParams(dimension_semantics=sem, vmem_limit_bytes=V7X_VMEM_LIMIT)


def _dot(a, b, dims=NN):
    return lax.dot_general(a.astype(MXU_DTYPE), b.astype(MXU_DTYPE), dims, preferred_element_type=F32)


def _split_bf16(a):
    hi = a.astype(jnp.bfloat16)
    return hi, (a - hi.astype(F32)).astype(jnp.bfloat16)


def _dotf(a, b, dims=NN):
    ah, al = _split_bf16(a)
    bh, bl = _split_bf16(b)
    dg = functools.partial(lax.dot_general, dimension_numbers=dims, preferred_element_type=F32)
    return dg(ah, bh) + (dg(ah, bl) + dg(al, bh))


def _tile(dim, pref):
    t = min(dim, pref)
    while dim % t:
        t //= 2
    return t


def _sigmoid(x):
    return 1.0 / (1.0 + jnp.exp(-x))


def _silu(x):
    return x * _sigmoid(x)


def _dsilu(x):
    s = _sigmoid(x)
    return s * (1.0 + x * (1.0 - s))


GELU_C = 0.7978845608028654
GELU_A = 0.044715


def _gelu(x):
    return 0.5 * x * (1.0 + jnp.tanh(GELU_C * (x + GELU_A * x * x * x)))


def _dgelu(x):
    t = jnp.tanh(GELU_C * (x + GELU_A * x * x * x))
    return 0.5 * (1.0 + t) + 0.5 * x * (1.0 - t * t) * GELU_C * (1.0 + 3.0 * GELU_A * x * x)


def _matmul(a, b, mode, name, *, tm=1024, tn=1024, tk=1024, a_fn=None, epi=None, extra=None, out_dtype=F32,
            b_chips=None, out_chips=None):
    if mode == "tn":
        K, M = a.shape
        tk = 2 * tk
    else:
        M, K = a.shape
    whole_k = False
    if b_chips is not None:
        g, b_layer = b_chips
        r, n = b.shape[2:]
        n_dim, k_dim = (r, n) if mode == "nt" else (n, r)
        N = N_CHIPS * n_dim if g == "j" else n_dim
        assert K == (N_CHIPS * k_dim if g == "k" else k_dim)
        if g == "j":
            tn = n_dim
        elif mode == "nn":
            whole_k, tk, tm = True, K, tm // 2
        else:
            tk = k_dim
    elif mode == "nt":
        N = b.shape[0]
        if tk < K <= 3 * tk:
            tk, tm = K, tm // 2
    else:
        N = b.shape[1]
    if out_chips is not None:
        og, o_layer, o_buf = out_chips
        if og == "j":
            tn = o_buf.shape[3]
        else:
            tm = o_buf.shape[2]
    tm, tn, tk = _tile(M, tm), _tile(N, tn), _tile(K, tk)
    nk = K // tk
    if mode == "tn":
        a_spec = pl.BlockSpec((tk, tm), lambda i, j, k: (k, i))
    else:
        a_spec = pl.BlockSpec((tm, tk), lambda i, j, k: (i, k))
    b_block = (tn, tk) if mode == "nt" else (tk, tn)
    if b_chips is None:
        b_spec = pl.BlockSpec(b_block, (lambda i, j, k: (j, k)) if mode == "nt" else (lambda i, j, k: (k, j)))
    elif mode == "nt":
        b_spec = pl.BlockSpec((None, None) + b_block, (lambda i, j, k: (j, b_layer, 0, k)) if g == "j"
                              else (lambda i, j, k: (k, b_layer, j, 0)))
    elif whole_k:
        b_spec = pl.BlockSpec((N_CHIPS, None, k_dim, tn), lambda i, j, k: (0, b_layer, 0, j))
    else:
        b_spec = pl.BlockSpec((None, None) + b_block, lambda i, j, k: (j, b_layer, k, 0))
    o_spec = pl.BlockSpec((tm, tn), lambda i, j, k: (i, j))
    e_spec = o_spec
    if out_chips is not None:
        o_spec = pl.BlockSpec((None, None, tm, tn), (lambda i, j, k: (j, o_layer, i, 0)) if og == "j"
                              else (lambda i, j, k: (i, o_layer, 0, j)))
    dims = {"nn": NN, "nt": NT, "tn": TN}[mode]
    has_extra = extra is not None
    n_in = 2 + has_extra + (out_chips is not None)

    def body(*refs):
        a_ref, b_ref = refs[0], refs[1]
        e_ref = refs[2] if has_extra else None
        o_ref = refs[n_in]
        av = a_ref[...]
        if a_fn is not None:
            av = a_fn(av)
        bv = b_ref[...]
        if whole_k:
            bv = bv.reshape(K, tn)
        part = _dot(av, bv, dims)

        def finish(acc):
            if epi is not None:
                acc = epi(acc, e_ref[...])
            o_ref[...] = acc.astype(out_dtype)

        if nk == 1:
            finish(part)
        else:
            acc_ref = refs[-1]
            k = pl.program_id(2)

            @pl.when(k == 0)
            def _():
                acc_ref[...] = part

            @pl.when(k > 0)
            def _():
                acc_ref[...] += part

            @pl.when(k == nk - 1)
            def _():
                finish(acc_ref[...])

    in_specs = [a_spec, b_spec] + ([e_spec] if has_extra else [])
    args = (a, b) + ((extra,) if has_extra else ())
    out_shape = jax.ShapeDtypeStruct((M, N), out_dtype)
    aliases = {}
    if out_chips is not None:
        in_specs.append(pl.BlockSpec(memory_space=pl.ANY))
        args += (o_buf,)
        out_shape = jax.ShapeDtypeStruct(o_buf.shape, o_buf.dtype)
        aliases = {n_in - 1: 0}
    return pl.pallas_call(
        body, name=name, grid=(M // tm, N // tn, nk), in_specs=in_specs, out_specs=o_spec, out_shape=out_shape,
        input_output_aliases=aliases,
        scratch_shapes=[pltpu.VMEM((tm, tn), F32)] if nk > 1 else [],
        compiler_params=_cparams("parallel", "parallel", "arbitrary"),
    )(*args)


def _relu2(v):
    r = jnp.maximum(v, 0.0)
    return r * r


def _epi_drelu2(acc, h):
    return acc * (2.0 * jnp.maximum(h, 0.0))


def _epi_add_residual(acc, du):
    return acc + DN_ALPHA * du


def _ln_fwd(x, y, g, b, name):
    T, D = x.shape
    tr = _tile(T, 512)

    def body(x_ref, y_ref, g_ref, b_ref, o_ref, ob_ref):
        u = DN_ALPHA * x_ref[...] + y_ref[...]
        mu = jnp.mean(u, axis=-1, keepdims=True)
        d = u - mu
        var = jnp.mean(d * d, axis=-1, keepdims=True)
        o = d * lax.rsqrt(var + LN_EPS) * g_ref[...] + b_ref[...]
        o_ref[...] = o
        ob_ref[...] = o.astype(ACT_DTYPE)

    row = pl.BlockSpec((tr, D), lambda i: (i, 0))
    vec = pl.BlockSpec((1, D), lambda i: (0, 0))
    return pl.pallas_call(
        body, name=name, grid=(T // tr,), in_specs=[row, row, vec, vec], out_specs=[row, row],
        out_shape=[jax.ShapeDtypeStruct((T, D), F32), jax.ShapeDtypeStruct((T, D), ACT_DTYPE)],
        compiler_params=_cparams("parallel"),
    )(x, y, g, b)


def _ln_bwd(x, y, g, dout, name):
    T, D = x.shape
    tr = _tile(T, 512)

    def body(x_ref, y_ref, g_ref, d_ref, du_ref, dub_ref, dg_ref, db_ref):
        i = pl.program_id(0)
        u = DN_ALPHA * x_ref[...] + y_ref[...]
        mu = jnp.mean(u, axis=-1, keepdims=True)
        d = u - mu
        rstd = lax.rsqrt(jnp.mean(d * d, axis=-1, keepdims=True) + LN_EPS)
        xhat = d * rstd
        dout_v = d_ref[...]
        dxh = dout_v * g_ref[...]
        m1 = jnp.mean(dxh, axis=-1, keepdims=True)
        m2 = jnp.mean(dxh * xhat, axis=-1, keepdims=True)
        du = rstd * (dxh - m1 - xhat * m2)
        du_ref[...] = du
        dub_ref[...] = du.astype(ACT_DTYPE)
        pg = jnp.sum(dout_v * xhat, axis=0, keepdims=True)
        pb = jnp.sum(dout_v, axis=0, keepdims=True)

        @pl.when(i == 0)
        def _():
            dg_ref[...] = pg
            db_ref[...] = pb

        @pl.when(i > 0)
        def _():
            dg_ref[...] += pg
            db_ref[...] += pb

    row = pl.BlockSpec((tr, D), lambda i: (i, 0))
    vec = pl.BlockSpec((1, D), lambda i: (0, 0))
    return pl.pallas_call(
        body, name=name, grid=(T // tr,), in_specs=[row, row, vec, row], out_specs=[row, row, vec, vec],
        out_shape=[jax.ShapeDtypeStruct((T, D), F32), jax.ShapeDtypeStruct((T, D), ACT_DTYPE),
                   jax.ShapeDtypeStruct((1, D), F32), jax.ShapeDtypeStruct((1, D), F32)],
        compiler_params=_cparams("arbitrary"),
    )(x, y, g, dout)


def _loss_head(y, tgt, name):
    T, D = y.shape
    tr = _tile(T, 512)

    def body(y_ref, t_ref, dy_ref, l_ref):
        i = pl.program_id(0)
        e = y_ref[...] - t_ref[...]
        dy_ref[...] = e * (1.0 / D)
        part = jnp.sum(e * e, axis=(0, 1), keepdims=True) * (0.5 / D)

        @pl.when(i == 0)
        def _():
            l_ref[...] = part

        @pl.when(i > 0)
        def _():
            l_ref[...] += part

    row = pl.BlockSpec((tr, D), lambda i: (i, 0))
    one = pl.BlockSpec((1, 1), lambda i: (0, 0))
    return pl.pallas_call(
        body, name=name, grid=(T // tr,), in_specs=[row, row], out_specs=[row, one],
        out_shape=[jax.ShapeDtypeStruct((T, D), F32), jax.ShapeDtypeStruct((1, 1), F32)],
        compiler_params=_cparams("arbitrary"),
    )(y, tgt)


def _swap_half(x):
    n = x.shape[-1]
    lane = lax.broadcasted_iota(jnp.int32, x.shape, 1)
    first = (lane % A_HEAD_DIM) < (A_HEAD_DIM // 2)
    return jnp.where(first, pltpu.roll(x, n - A_HEAD_DIM // 2, axis=1), pltpu.roll(x, A_HEAD_DIM // 2, axis=1))


def _rope(x, cos, sin_signed):
    return x * cos + _swap_half(x) * sin_signed


def _rope_t(dy, cos, sin_signed):
    return dy * cos + _swap_half(dy * sin_signed)


def _rope_tables(T):
    half = A_HEAD_DIM // 2
    inv_freq = np.float32(ROPE_THETA) ** (-np.arange(half, dtype=np.float32) / np.float32(half))
    ang = np.arange(T, dtype=np.float32)[:, None] * inv_freq[None, :]
    cos = np.tile(np.cos(ang), (1, 4))
    sin = np.sin(ang)
    sin_signed = np.tile(np.concatenate([-sin, sin], axis=1), (1, 2))
    return jnp.asarray(cos, F32), jnp.asarray(sin_signed, F32)


def _band_mask(n):
    qi = lax.broadcasted_iota(jnp.int32, (WINDOW, 2 * WINDOW), 0)
    kj = lax.broadcasted_iota(jnp.int32, (WINDOW, 2 * WINDOW), 1)
    return (kj > qi) & (kj <= qi + WINDOW) & ((n > 0) | (kj >= WINDOW))


def _place(v, src_half, dst_half):
    lane = lax.broadcasted_iota(jnp.int32, v.shape, 1)
    if src_half != dst_half:
        v = pltpu.roll(v, A_HEAD_DIM, axis=1)
    keep = (lane >= A_HEAD_DIM) if dst_half else (lane < A_HEAD_DIM)
    return jnp.where(keep, v, 0.0)


def _attn_specs():
    kb, vb = HP_K // LANES, HP_V // LANES
    prev = lambda n: jnp.maximum(n - 1, 0)
    return dict(
        q=pl.BlockSpec((WINDOW, A_Q_W), lambda n: (n, 0)),
        kc=pl.BlockSpec((WINDOW, LANES), lambda n: (n, kb)),
        kp=pl.BlockSpec((WINDOW, LANES), lambda n: (prev(n), kb)),
        vc=pl.BlockSpec((WINDOW, LANES), lambda n: (n, vb)),
        vp=pl.BlockSpec((WINDOW, LANES), lambda n: (prev(n), vb)),
        tq=pl.BlockSpec((WINDOW, LANES), lambda n: (n, 0)),
        tp=pl.BlockSpec((WINDOW, LANES), lambda n: (prev(n), 0)),
        sink=pl.BlockSpec((A_Q_HEADS, LANES), lambda n: (0, 0)),
        row512=pl.BlockSpec((WINDOW, A_Q_W), lambda n: (n, 0)),
        row128=pl.BlockSpec((WINDOW, LANES), lambda n: (n, 0)),
        lse=pl.BlockSpec((WINDOW, A_Q_HEADS), lambda n: (n, 0)),
    )


def _attn_fwd(proj, cos, sin_s, sinks_b, name):
    T = proj.shape[0]
    sp = _attn_specs()

    def body(q_ref, kc_ref, kp_ref, vc_ref, vp_ref, cq_ref, sq_ref, cp_ref, sp_ref, sink_ref, o_ref, l_ref):
        n = pl.program_id(0)
        cq, sq = cq_ref[...], sq_ref[...]
        cq4, sq4 = jnp.tile(cq, (1, A_Q_W // LANES)), jnp.tile(sq, (1, A_Q_W // LANES))
        q = _rope(q_ref[...], cq4, sq4) * (A_HEAD_DIM ** -0.5)
        kc = _rope(kc_ref[...], cq, sq)
        kp = _rope(kp_ref[...], cp_ref[...], sp_ref[...])
        kk = jnp.concatenate([kp, kc], axis=0)
        vv = jnp.concatenate([vp_ref[...], vc_ref[...]], axis=0)
        mask = _band_mask(n)[None]
        lane = lax.broadcasted_iota(jnp.int32, (WINDOW, LANES), 1)
        lane8 = lax.broadcasted_iota(jnp.int32, (WINDOW, A_Q_HEADS), 1)
        qe = jnp.stack([_place(q[:, (hq // 2) * LANES:(hq // 2 + 1) * LANES], hq % 2, hq // 4) for hq in range(A_Q_HEADS)])
        kk8 = jnp.broadcast_to(kk[None], (A_Q_HEADS,) + kk.shape)
        vv8 = jnp.broadcast_to(vv[None], (A_Q_HEADS,) + vv.shape)
        sk = jnp.stack([sink_ref[hq:hq + 1, 0:1] for hq in range(A_Q_HEADS)])
        s = jnp.where(mask, _dot(qe, kk8, BNT), NEG_BIG)
        m = jnp.maximum(jnp.max(s, axis=-1, keepdims=True), sk)
        p = jnp.exp(s - m)
        den = jnp.sum(p, axis=-1, keepdims=True) + jnp.exp(sk - m)
        o = _dot(p * (1.0 / den), vv8, BNN)
        lse_h = m + jnp.log(den)
        outs = []
        lse = jnp.zeros((WINDOW, A_Q_HEADS), F32)
        for pb in range(A_Q_HEADS // 2):
            halves = [_place(o[2 * pb + e], pb // 2, e) for e in range(2)]
            outs.append(jnp.where(lane < A_HEAD_DIM, halves[0], halves[1]))
            for e in range(2):
                lse = jnp.where(lane8 == 2 * pb + e, lse_h[2 * pb + e], lse)
        o_ref[...] = jnp.concatenate(outs, axis=1)
        l_ref[...] = lse

    return pl.pallas_call(
        body, name=name, grid=(T // WINDOW,),
        in_specs=[sp["q"], sp["kc"], sp["kp"], sp["vc"], sp["vp"], sp["tq"], sp["tq"], sp["tp"], sp["tp"], sp["sink"]],
        out_specs=[sp["row512"], sp["lse"]],
        out_shape=[jax.ShapeDtypeStruct((T, A_Q_W), F32), jax.ShapeDtypeStruct((T, A_Q_HEADS), F32)],
        compiler_params=_cparams("parallel"),
    )(proj, proj, proj, proj, proj, cos, sin_s, cos, sin_s, sinks_b)


def _attn_bwd(proj, cos, sin_s, sinks_b, o, lse, dmix, name):
    T = proj.shape[0]
    sp = _attn_specs()

    def body(q_ref, kc_ref, kp_ref, vc_ref, vp_ref, cq_ref, sq_ref, cp_ref, sp_ref, sink_ref, o_ref, l_ref, do_ref,
             dq_ref, dkc_ref, dkp_ref, dvc_ref, dvp_ref, dsink_ref):
        n = pl.program_id(0)
        cq, sq = cq_ref[...], sq_ref[...]
        cp, sps = cp_ref[...], sp_ref[...]
        cq4, sq4 = jnp.tile(cq, (1, A_Q_W // LANES)), jnp.tile(sq, (1, A_Q_W // LANES))
        q = _rope(q_ref[...], cq4, sq4) * (A_HEAD_DIM ** -0.5)
        kc = _rope(kc_ref[...], cq, sq)
        kp = _rope(kp_ref[...], cp, sps)
        kk = jnp.concatenate([kp, kc], axis=0)
        vv = jnp.concatenate([vp_ref[...], vc_ref[...]], axis=0)
        mask = _band_mask(n)[None]
        lane = lax.broadcasted_iota(jnp.int32, (WINDOW, LANES), 1)
        do_all, o_all, l_all = do_ref[...], o_ref[...], l_ref[...]
        lane8 = lax.broadcasted_iota(jnp.int32, (WINDOW, A_Q_HEADS), 1)
        head8 = lax.broadcasted_iota(jnp.int32, (1, A_Q_HEADS), 1)
        prod = do_all * o_all
        qes, does, deltas, lhs = [], [], [], []
        for hq in range(A_Q_HEADS):
            pb, e, kvh = hq // 2, hq % 2, hq // 4
            blk = slice(pb * LANES, (pb + 1) * LANES)
            in_half = (lane >= A_HEAD_DIM) if e else (lane < A_HEAD_DIM)
            deltas.append(jnp.sum(jnp.where(in_half, prod[:, blk], 0.0), axis=-1, keepdims=True))
            qes.append(_place(q[:, blk], e, kvh))
            does.append(_place(do_all[:, blk], e, kvh))
            lhs.append(jnp.sum(jnp.where(lane8 == hq, l_all, 0.0), axis=-1, keepdims=True))
        qe, doe, delta, lh = jnp.stack(qes), jnp.stack(does), jnp.stack(deltas), jnp.stack(lhs)
        kk8 = jnp.broadcast_to(kk[None], (A_Q_HEADS,) + kk.shape)
        vv8 = jnp.broadcast_to(vv[None], (A_Q_HEADS,) + vv.shape)
        sk = jnp.stack([sink_ref[hq:hq + 1, 0:1] for hq in range(A_Q_HEADS)])
        s = _dot(qe, kk8, BNT)
        p = jnp.where(mask, jnp.exp(jnp.where(mask, s, NEG_BIG) - lh), 0.0)
        dvv = jnp.sum(_dot(p, doe, BTN), axis=0)
        ds = p * (_dot(doe, vv8, BNT) - delta)
        dkk = jnp.sum(_dot(ds, qe, BTN), axis=0)
        dqe = _dot(ds, kk8, BNN)
        dsink_h = -jnp.sum(jnp.exp(sk - lh) * delta, axis=(1, 2), keepdims=True)
        dqs = []
        dsk = jnp.zeros((1, A_Q_HEADS), F32)
        for pb in range(A_Q_HEADS // 2):
            halves = [_place(dqe[2 * pb + e], pb // 2, e) for e in range(2)]
            dqs.append(jnp.where(lane < A_HEAD_DIM, halves[0], halves[1]))
            for e in range(2):
                dsk = jnp.where(head8 == 2 * pb + e, dsink_h[2 * pb + e], dsk)
        dq = jnp.concatenate(dqs, axis=1) * (A_HEAD_DIM ** -0.5)
        dq_ref[...] = _rope_t(dq, cq4, sq4)
        dkp_ref[...] = _rope_t(dkk[:WINDOW], cp, sps)
        dkc_ref[...] = _rope_t(dkk[WINDOW:], cq, sq)
        dvp_ref[...] = dvv[:WINDOW]
        dvc_ref[...] = dvv[WINDOW:]

        @pl.when(n == 0)
        def _():
            dsink_ref[...] = dsk

        @pl.when(n > 0)
        def _():
            dsink_ref[...] += dsk

    return pl.pallas_call(
        body, name=name, grid=(T // WINDOW,),
        in_specs=[sp["q"], sp["kc"], sp["kp"], sp["vc"], sp["vp"], sp["tq"], sp["tq"], sp["tp"], sp["tp"], sp["sink"],
                  sp["row512"], sp["lse"], sp["row512"]],
        out_specs=[sp["row512"], sp["row128"], sp["row128"], sp["row128"], sp["row128"],
                   pl.BlockSpec((1, A_Q_HEADS), lambda n: (0, 0))],
        out_shape=[jax.ShapeDtypeStruct((T, A_Q_W), F32)] + [jax.ShapeDtypeStruct((T, LANES), F32)] * 4
        + [jax.ShapeDtypeStruct((1, A_Q_HEADS), F32)],
        compiler_params=_cparams("arbitrary"),
    )(proj, proj, proj, proj, proj, cos, sin_s, cos, sin_s, sinks_b, o, lse, dmix)


def _shift_down(x, prev8, k):
    if k == 0:
        return x
    row = lax.broadcasted_iota(jnp.int32, prev8.shape, 0)
    r = pltpu.roll(x, k, axis=0)
    top = jnp.where(row < k, pltpu.roll(prev8, k, axis=0), r[:SUBLANES])
    return jnp.concatenate([top, r[SUBLANES:]], axis=0)


def _shift_up(x, next8, k):
    if k == 0:
        return x
    R = x.shape[0]
    row = lax.broadcasted_iota(jnp.int32, next8.shape, 0)
    r = pltpu.roll(x, R - k, axis=0)
    bot = jnp.where(row >= SUBLANES - k, pltpu.roll(next8, SUBLANES - k, axis=0), r[R - SUBLANES:])
    return jnp.concatenate([r[:R - SUBLANES], bot], axis=0)


def _conv(x, prev8, w):
    y = x * w[CONV_K - 1:CONV_K]
    for j in range(CONV_K - 1):
        y = y + _shift_down(x, prev8, CONV_K - 1 - j) * w[j:j + 1]
    return y


def _conv_bwd(x, prev8, w, dy, next8_dy):
    dx = dy * w[CONV_K - 1:CONV_K]
    dws = []
    for j in range(CONV_K - 1):
        k = CONV_K - 1 - j
        dx = dx + _shift_up(dy, next8_dy, k) * w[j:j + 1]
        dws.append(jnp.sum(dy * _shift_down(x, prev8, k), axis=0, keepdims=True))
    dws.append(jnp.sum(dy * x, axis=0, keepdims=True))
    return dx, dws


def _dnconv_fwd(proj, conv_w, name):
    T = proj.shape[0]
    R = _tile(T, 512)
    cb0 = HP_QKVB // A_Q_W

    def body(x_ref, w_ref, o_ref, prev_ref):
        i = pl.program_id(1)

        @pl.when(i == 0)
        def _():
            prev_ref[...] = jnp.zeros_like(prev_ref)

        x = x_ref[...]
        o_ref[...] = _silu(_conv(x, prev_ref[...], w_ref[...]))
        prev_ref[...] = x[R - SUBLANES:]

    return pl.pallas_call(
        body, name=name, grid=(3, T // R),
        in_specs=[pl.BlockSpec((R, B_W), lambda j, i: (i, cb0 + j)), pl.BlockSpec((CONV_K, B_W), lambda j, i: (0, j))],
        out_specs=pl.BlockSpec((R, B_W), lambda j, i: (i, j)),
        out_shape=jax.ShapeDtypeStruct((T, 3 * B_W), F32),
        scratch_shapes=[pltpu.VMEM((SUBLANES, B_W), F32)],
        compiler_params=_cparams("parallel", "arbitrary"),
    )(proj, conv_w)


def _dnconv_bwd(proj, conv_w, dy, name):
    T = proj.shape[0]
    R = _tile(T, 512)
    nb = T // R
    cb0 = HP_QKVB // A_Q_W
    r8 = R // SUBLANES

    def body(x_ref, xp_ref, w_ref, dy_ref, dx_ref, dw_ref, next_ref):
        i = pl.program_id(1)
        blk = nb - 1 - i

        @pl.when(i == 0)
        def _():
            next_ref[...] = jnp.zeros_like(next_ref)

        x = x_ref[...]
        prev8 = jnp.where(blk > 0, xp_ref[...], 0.0)
        w = w_ref[...]
        dpre = dy_ref[...] * _dsilu(_conv(x, prev8, w))
        dx, dw = _conv_bwd(x, prev8, w, dpre, next_ref[...])
        dx_ref[...] = dx
        next_ref[...] = dpre[:SUBLANES]

        @pl.when(i == 0)
        def _():
            for j in range(CONV_K):
                dw_ref[j:j + 1, :] = dw[j]

        @pl.when(i > 0)
        def _():
            for j in range(CONV_K):
                dw_ref[j:j + 1, :] += dw[j]

    return pl.pallas_call(
        body, name=name, grid=(3, nb),
        in_specs=[pl.BlockSpec((R, B_W), lambda j, i: (nb - 1 - i, cb0 + j)),
                  pl.BlockSpec((SUBLANES, B_W), lambda j, i: (jnp.maximum((nb - 1 - i) * r8 - 1, 0), cb0 + j)),
                  pl.BlockSpec((CONV_K, B_W), lambda j, i: (0, j)),
                  pl.BlockSpec((R, B_W), lambda j, i: (nb - 1 - i, j))],
        out_specs=[pl.BlockSpec((R, B_W), lambda j, i: (nb - 1 - i, j)),
                   pl.BlockSpec((CONV_K, B_W), lambda j, i: (0, j))],
        out_shape=[jax.ShapeDtypeStruct((T, 3 * B_W), F32), jax.ShapeDtypeStruct((CONV_K, 3 * B_W), F32)],
        scratch_shapes=[pltpu.VMEM((SUBLANES, B_W), F32)],
        compiler_params=_cparams("parallel", "arbitrary"),
    )(proj, proj, conv_w, dy)


DK_SCALE = B_HEAD_DIM ** -0.5


BNN = (((2,), (1,)), ((0,), (0,)))
BNT = (((2,), (2,)), ((0,), (0,)))
BTN = (((1,), (1,)), ((0,), (0,)))


def _tri_inv(a):
    C = a.shape[-1]
    ri = lax.broadcasted_iota(jnp.int32, (C, C), 0)
    ci = lax.broadcasted_iota(jnp.int32, (C, C), 1)
    x = jnp.where(ri == ci, 1.0, 0.0)[None] - a
    p = _dotf(a, a, BNN)
    span = 2
    while span < C:
        x = x + _dotf(x, p, BNN)
        span *= 2
        if span < C:
            p = _dotf(p, p, BNN)
    return x


def _dn_chunk(qc, kc, v, gcol, grow, bcol, s0, tm=None):
    C = B_CHUNK
    ri = lax.broadcasted_iota(jnp.int32, (C, C), 0)
    ci = lax.broadcasted_iota(jnp.int32, (C, C), 1)
    incl, strict = (ri >= ci)[None], (ri > ci)[None]
    rq = lax.rsqrt(jnp.sum(qc * qc, axis=-1, keepdims=True) + NORM_EPS)
    rk = lax.rsqrt(jnp.sum(kc * kc, axis=-1, keepdims=True) + NORM_EPS)
    qn = qc * rq
    q = qn * DK_SCALE
    k = kc * rk
    gc_col = jnp.sum(jnp.where(incl, grow, 0.0), axis=2, keepdims=True)
    gc_row = jnp.sum(jnp.where((ri <= ci)[None], gcol, 0.0), axis=1, keepdims=True)
    gl = jnp.sum(gcol, axis=1, keepdims=True)
    dincl = jnp.where(incl, jnp.exp(jnp.where(incl, gc_col - gc_row, 0.0)), 0.0)
    dstrict = jnp.where(strict, dincl, 0.0)
    eg = jnp.exp(gc_col)
    ekt = jnp.exp(gl - gc_col)
    egl = jnp.exp(gl)
    kb = k * bcol
    vb = v * bcol
    kbg = kb * eg
    a = _dot(kb, k, BNT) * dstrict
    if tm is None:
        tm = _tri_inv(a)
    u = _dot(tm, vb, BNN)
    w = _dot(tm, kbg, BNN)
    vn = u - _dot(w, s0, BNN)
    qk = _dot(q, k, BNT) * dincl
    qg = q * eg
    kt = k * ekt
    o = _dot(qg, s0, BNN) + _dot(qk, vn, BNN)
    s1 = s0 * egl + _dot(kt, vn, BTN)
    return dict(rq=rq, rk=rk, qn=qn, q=q, k=k, dincl=dincl, dstrict=dstrict, eg=eg, ekt=ekt, egl=egl, kb=kb, vb=vb,
                kbg=kbg, a=a, tm=tm, w=w, vn=vn, qk=qk, qg=qg, kt=kt, o=o, s1=s1, ri=ri[None], ci=ci[None])


def _heads(ref):
    return jnp.stack([ref[:, h * B_HEAD_DIM:(h + 1) * B_HEAD_DIM] for h in range(B_HEADS)])


def _store_heads(ref, val):
    for h in range(B_HEADS):
        ref[:, h * B_HEAD_DIM:(h + 1) * B_HEAD_DIM] = val[h]


def _dn_specs(N, rev):
    ix = (lambda n: N - 1 - n) if rev else (lambda n: n)
    wide = lambda cb: pl.BlockSpec((B_CHUNK, B_W), lambda n: (ix(n), cb))
    return dict(
        q=wide(0), k=wide(1), v=wide(2), z=wide(HP_Z // B_W), dob=wide(A_Q_W // B_W), out=wide(0),
        nw=pl.BlockSpec((1, LANES), lambda n: (0, 0)),
        row=pl.BlockSpec((B_HEADS, None, 1, B_CHUNK), lambda n: (0, ix(n), 0, 0)),
        state=pl.BlockSpec((B_HEADS, None, B_HEAD_DIM, B_HEAD_DIM), lambda n: (0, ix(n), 0, 0)),
        inv=pl.BlockSpec((B_HEADS, None, B_CHUNK, B_CHUNK), lambda n: (0, ix(n), 0, 0)),
    )


def _to_col(row):
    C = row.shape[-1]
    eye = lax.broadcasted_iota(jnp.int32, (C, C), 0) == lax.broadcasted_iota(jnp.int32, (C, C), 1)
    return jnp.sum(jnp.where(eye[None], row, 0.0), axis=2, keepdims=True)


def _to_row(col):
    C = col.shape[1]
    eye = lax.broadcasted_iota(jnp.int32, (C, C), 0) == lax.broadcasted_iota(jnp.int32, (C, C), 1)
    return jnp.sum(jnp.where(eye[None], col, 0.0), axis=1, keepdims=True)


def _dn_fwd(qkvc, proj, norm_w, grow, brow, name):
    T = qkvc.shape[0]
    N = T // B_CHUNK
    sp = _dn_specs(N, False)

    def body(q_ref, k_ref, v_ref, z_ref, nw_ref, gr_ref, br_ref, o_ref, st_ref, tm_ref, s_ref):
        n = pl.program_id(0)

        @pl.when(n == 0)
        def _():
            s_ref[...] = jnp.zeros_like(s_ref)

        s0 = s_ref[...]
        st_ref[...] = s0
        grow_v = gr_ref[...]
        f = _dn_chunk(_heads(q_ref), _heads(k_ref), _heads(v_ref), _to_col(grow_v), grow_v, _to_col(br_ref[...]), s0)
        o = f["o"]
        r = lax.rsqrt(jnp.mean(o * o, axis=-1, keepdims=True) + NORM_EPS)
        _store_heads(o_ref, o * r * nw_ref[...][None] * _silu(_heads(z_ref)))
        s_ref[...] = f["s1"]
        tm_ref[...] = f["tm"]

    return pl.pallas_call(
        body, name=name, grid=(N,),
        in_specs=[sp["q"], sp["k"], sp["v"], sp["z"], sp["nw"], sp["row"], sp["row"]],
        out_specs=[sp["out"], sp["state"], sp["inv"]],
        out_shape=[jax.ShapeDtypeStruct((T, B_W), F32),
                   jax.ShapeDtypeStruct((B_HEADS, N, B_HEAD_DIM, B_HEAD_DIM), F32),
                   jax.ShapeDtypeStruct((B_HEADS, N, B_CHUNK, B_CHUNK), F32)],
        scratch_shapes=[pltpu.VMEM((B_HEADS, B_HEAD_DIM, B_HEAD_DIM), F32)],
        compiler_params=_cparams("arbitrary"),
    )(qkvc, qkvc, qkvc, proj, norm_w, grow, brow)


def _dn_bwd(qkvc, proj, norm_w, grow, brow, states, invs, dmix, name):
    T = qkvc.shape[0]
    N = T // B_CHUNK
    sp = _dn_specs(N, True)
    C = B_CHUNK

    def body(q_ref, k_ref, v_ref, z_ref, nw_ref, gr_ref, br_ref, st_ref, tm_ref, dob_ref,
             dq_ref, dk_ref, dv_ref, dz_ref, dg_ref, db_ref, dnw_ref, ds_ref):
        n = pl.program_id(0)

        @pl.when(n == 0)
        def _():
            ds_ref[...] = jnp.zeros_like(ds_ref)
            dnw_ref[...] = jnp.zeros_like(dnw_ref)

        s0 = st_ref[...]
        ds1 = ds_ref[...]
        v, z, nw, bcol_v = _heads(v_ref), _heads(z_ref), nw_ref[...][None], _to_col(br_ref[...])
        grow_v = gr_ref[...]
        f = _dn_chunk(_heads(q_ref), _heads(k_ref), v, _to_col(grow_v), grow_v, bcol_v, s0, tm=tm_ref[...])
        o, q, k, qn = f["o"], f["q"], f["k"], f["qn"]
        eg, ekt, egl = f["eg"], f["ekt"], f["egl"]
        tm, w, vn, kb, vb, kbg = f["tm"], f["w"], f["vn"], f["kb"], f["vb"], f["kbg"]
        qg, kt, qk, a = f["qg"], f["kt"], f["qk"], f["a"]
        ri, ci = f["ri"], f["ci"]

        dob_v = _heads(dob_ref)
        r = lax.rsqrt(jnp.mean(o * o, axis=-1, keepdims=True) + NORM_EPS)
        sz = _silu(z)
        on = o * r
        dnw_ref[...] += jnp.sum(dob_v * sz * on, axis=1, keepdims=True)
        _store_heads(dz_ref, dob_v * on * nw * _dsilu(z))
        d_on = dob_v * sz * nw
        do = r * (d_on - on * jnp.mean(d_on * on, axis=-1, keepdims=True))

        dvn = _dot(qk, do, BTN) + _dot(kt, ds1, BNN)
        dqk = _dot(do, vn, BNT)
        dqg = _dot(do, s0, BNT)
        ds_ref[...] = _dot(qg, do, BTN) + egl * ds1 - _dot(w, dvn, BTN)
        dgl = jnp.sum(s0 * ds1, axis=(1, 2), keepdims=True) * egl
        dkt = _dot(vn, ds1, BNT)
        dw = -_dot(dvn, s0, BNT)
        dq = dqg * eg
        dgc = jnp.sum(dqg * qg, axis=-1, keepdims=True)
        dk = dkt * ekt
        t_kt = jnp.sum(dkt * kt, axis=-1, keepdims=True)
        dgl = dgl + jnp.sum(t_kt, axis=1, keepdims=True)
        dgc = dgc - t_kt
        dqkr = dqk * f["dincl"]
        dq = dq + _dot(dqkr, k, BNN)
        dk = dk + _dot(dqkr, q, BTN)
        e_qk = dqk * qk
        dgc = dgc + jnp.sum(e_qk, axis=-1, keepdims=True)
        dgc_row = -jnp.sum(e_qk, axis=1, keepdims=True)
        dtm = _dot(dvn, vb, BNT) + _dot(dw, kbg, BNT)
        dvb = _dot(tm, dvn, BTN)
        dkbg = _dot(tm, dw, BTN)
        dkb = dkbg * eg
        dgc = dgc + jnp.sum(dkbg * kbg, axis=-1, keepdims=True)
        da = -_dotf(tm, _dotf(dtm, tm, BNT), BTN)
        dkk = da * f["dstrict"]
        e_a = da * a
        dgc = dgc + jnp.sum(e_a, axis=-1, keepdims=True)
        dgc_row = dgc_row - jnp.sum(e_a, axis=1, keepdims=True)
        dkb = dkb + _dot(dkk, k, BNN)
        dk = dk + _dot(dkk, kb, BTN)
        dk = dk + dkb * bcol_v
        db_ref[...] = _to_row(jnp.sum(dkb * k, axis=-1, keepdims=True) + jnp.sum(dvb * v, axis=-1, keepdims=True))
        _store_heads(dv_ref, dvb * bcol_v)
        dgc_row = dgc_row + jnp.sum(jnp.where(ri == ci, dgc, 0.0), axis=1, keepdims=True)
        dg_ref[...] = jnp.sum(jnp.where(ci <= ri, _to_col(dgc_row), 0.0), axis=1, keepdims=True) + dgl
        dqs = dq * DK_SCALE
        _store_heads(dq_ref, f["rq"] * (dqs - qn * jnp.sum(dqs * qn, axis=-1, keepdims=True)))
        _store_heads(dk_ref, f["rk"] * (dk - k * jnp.sum(dk * k, axis=-1, keepdims=True)))

    return pl.pallas_call(
        body, name=name, grid=(N,),
        in_specs=[sp["q"], sp["k"], sp["v"], sp["z"], sp["nw"], sp["row"], sp["row"], sp["state"], sp["inv"], sp["dob"]],
        out_specs=[sp["out"], sp["out"], sp["out"], sp["out"], sp["row"], sp["row"],
                   pl.BlockSpec((B_HEADS, 1, LANES), lambda n: (0, 0, 0))],
        out_shape=[jax.ShapeDtypeStruct((T, B_W), F32)] * 4
        + [jax.ShapeDtypeStruct((B_HEADS, N, 1, C), F32), jax.ShapeDtypeStruct((B_HEADS, N, 1, C), F32),
           jax.ShapeDtypeStruct((B_HEADS, 1, LANES), F32)],
        scratch_shapes=[pltpu.VMEM((B_HEADS, B_HEAD_DIM, B_HEAD_DIM), F32)],
        compiler_params=_cparams("arbitrary"),
    )(qkvc, qkvc, qkvc, proj, norm_w, grow, brow, states, invs, dmix)


def _lru_gates(xc, wa_ref, wx_ref, ba, bx, sp):
    pre_r, pre_i = [], []
    for hb in range(LRU_BLOCKS):
        xb = xc[:, hb * LRU_BLOCK_W:(hb + 1) * LRU_BLOCK_W]
        pre_r.append(_dot(xb, wa_ref[hb]))
        pre_i.append(_dot(xb, wx_ref[hb]))
    r = _sigmoid(jnp.concatenate(pre_r, axis=1) + ba)
    i = _sigmoid(jnp.concatenate(pre_i, axis=1) + bx)
    la = -LRU_C * r * sp
    a = jnp.exp(la)
    th = jnp.tanh(la)
    s = jnp.sqrt(-2.0 * th / (1.0 - th))
    return r, i, a, s


def _scan_down(a, b):
    R = a.shape[0]
    row = lax.broadcasted_iota(jnp.int32, a.shape, 0)
    d = 1
    while d < R:
        ok = row >= d
        b = a * jnp.where(ok, pltpu.roll(b, d, axis=0), 0.0) + b
        a = a * jnp.where(ok, pltpu.roll(a, d, axis=0), 1.0)
        d *= 2
    return a, b


def _scan_up(a, b):
    R = a.shape[0]
    row = lax.broadcasted_iota(jnp.int32, a.shape, 0)
    d = 1
    while d < R:
        ok = row < R - d
        b = a * jnp.where(ok, pltpu.roll(b, R - d, axis=0), 0.0) + b
        a = a * jnp.where(ok, pltpu.roll(a, R - d, axis=0), 1.0)
        d *= 2
    return b


def _rglru_fwd(proj, conv_w, conv_b, wa, wx, ba, bx, sp, name):
    T = proj.shape[0]
    R = _tile(T, 256)
    W = D_MODEL

    def body(p_ref, cw_ref, cb_ref, wa_ref, wx_ref, ba_ref, bx_ref, sp_ref, hg_ref, h_ref, prev_ref, hc_ref):
        i = pl.program_id(0)

        @pl.when(i == 0)
        def _():
            prev_ref[...] = jnp.zeros_like(prev_ref)
            hc_ref[...] = jnp.zeros_like(hc_ref)

        xr = p_ref[:, :W]
        gate = p_ref[:, W:]
        xc = _conv(xr, prev_ref[...], cw_ref[...]) + cb_ref[...]
        prev_ref[...] = xr[R - SUBLANES:]
        r, ig, a, s = _lru_gates(xc, wa_ref, wx_ref, ba_ref[...], bx_ref[...], sp_ref[...])
        pa, hb = _scan_down(a, s * ig * xc)
        h = hb + pa * hc_ref[SUBLANES - 1:SUBLANES, :]
        h_ref[...] = h
        hg_ref[...] = (h * _gelu(gate)).astype(ACT_DTYPE)
        hc_ref[...] = h[R - SUBLANES:]

    vec = pl.BlockSpec((1, W), lambda i: (0, 0))
    wsp = pl.BlockSpec((LRU_BLOCKS, LRU_BLOCK_W, LRU_BLOCK_W), lambda i: (0, 0, 0))
    row = pl.BlockSpec((R, W), lambda i: (i, 0))
    return pl.pallas_call(
        body, name=name, grid=(T // R,),
        in_specs=[pl.BlockSpec((R, 2 * W), lambda i: (i, 0)), pl.BlockSpec((CONV_K, W), lambda i: (0, 0)),
                  vec, wsp, wsp, vec, vec, vec],
        out_specs=[row, row],
        out_shape=[jax.ShapeDtypeStruct((T, W), ACT_DTYPE), jax.ShapeDtypeStruct((T, W), F32)],
        scratch_shapes=[pltpu.VMEM((SUBLANES, W), F32), pltpu.VMEM((SUBLANES, W), F32)],
        compiler_params=_cparams("arbitrary"),
    )(proj, conv_w, conv_b, wa, wx, ba, bx, sp)


def _rglru_bwd(proj, conv_w, conv_b, wa, wx, ba, bx, sp, h, dhg, name):
    T = proj.shape[0]
    R = _tile(T, 256)
    nb = T // R
    r8 = R // SUBLANES
    W = D_MODEL

    def body(p_ref, pp_ref, cw_ref, cb_ref, wa_ref, wx_ref, ba_ref, bx_ref, sp_ref, h_ref, hp_ref, dhg_ref,
             dp_ref, dcw_ref, dcb_ref, dwa_ref, dwx_ref, dba_ref, dbx_ref, dsp_ref, lam_ref, nxt_ref):
        step = pl.program_id(0)
        blk = nb - 1 - step

        @pl.when(step == 0)
        def _():
            lam_ref[...] = jnp.zeros_like(lam_ref)
            nxt_ref[...] = jnp.zeros_like(nxt_ref)

        xr = p_ref[:, :W]
        gate = p_ref[:, W:]
        first = blk > 0
        prev8 = jnp.where(first, pp_ref[:, :W], 0.0)
        hprev8 = jnp.where(first, hp_ref[...], 0.0)
        cw = cw_ref[...]
        spv = sp_ref[...]
        xc = _conv(xr, prev8, cw) + cb_ref[...]
        r, ig, a, s = _lru_gates(xc, wa_ref, wx_ref, ba_ref[...], bx_ref[...], spv)
        hv = h_ref[...]
        dhg_v = dhg_ref[...]
        dgate = dhg_v * hv * _dgelu(gate)
        dh = dhg_v * _gelu(gate)
        row = lax.broadcasted_iota(jnp.int32, (R, W), 0)
        last = row == R - 1
        a_up = jnp.where(last, 0.0, pltpu.roll(a, R - 1, axis=0))
        lam = _scan_up(a_up, dh + jnp.where(last, lam_ref[0:1, :], 0.0))
        lam_ref[...] = (a * lam)[:SUBLANES]
        h_dn = _shift_down(hv, hprev8, 1)
        da = lam * h_dn
        bx_in = ig * xc
        dsv = lam * bx_in
        dig = lam * s * xc
        dxc = lam * s * ig
        dla = da * a - dsv * (a * a) / s
        dr = dla * (-LRU_C) * spv
        dsp = jnp.sum(dla * (-LRU_C) * r, axis=0, keepdims=True)
        dpr = dr * r * (1.0 - r)
        dpi = dig * ig * (1.0 - ig)
        dxc_parts, dwa_parts, dwx_parts = [], [], []
        for hb in range(LRU_BLOCKS):
            sl = slice(hb * LRU_BLOCK_W, (hb + 1) * LRU_BLOCK_W)
            xb, gr, gi = xc[:, sl], dpr[:, sl], dpi[:, sl]
            dxc_parts.append(_dot(gr, wa_ref[hb], NT) + _dot(gi, wx_ref[hb], NT))
            dwa_parts.append(_dot(xb, gr, TN))
            dwx_parts.append(_dot(xb, gi, TN))
        dxc = dxc + jnp.concatenate(dxc_parts, axis=1)
        dxr, dcw = _conv_bwd(xr, prev8, cw, dxc, nxt_ref[...])
        nxt_ref[...] = dxc[:SUBLANES]
        dp_ref[:, :W] = dxr.astype(ACT_DTYPE)
        dp_ref[:, W:] = dgate.astype(ACT_DTYPE)
        dcb = jnp.sum(dxc, axis=0, keepdims=True)
        dba = jnp.sum(dpr, axis=0, keepdims=True)
        dbx = jnp.sum(dpi, axis=0, keepdims=True)

        @pl.when(step == 0)
        def _():
            for j in range(CONV_K):
                dcw_ref[j:j + 1, :] = dcw[j]
            dcb_ref[...] = dcb
            dba_ref[...] = dba
            dbx_ref[...] = dbx
            dsp_ref[...] = dsp
            for hb in range(LRU_BLOCKS):
                dwa_ref[hb] = dwa_parts[hb]
                dwx_ref[hb] = dwx_parts[hb]

        @pl.when(step > 0)
        def _():
            for j in range(CONV_K):
                dcw_ref[j:j + 1, :] += dcw[j]
            dcb_ref[...] += dcb
            dba_ref[...] += dba
            dbx_ref[...] += dbx
            dsp_ref[...] += dsp
            for hb in range(LRU_BLOCKS):
                dwa_ref[hb] += dwa_parts[hb]
                dwx_ref[hb] += dwx_parts[hb]

    rv = lambda i: nb - 1 - i
    before = lambda i: jnp.maximum((nb - 1 - i) * r8 - 1, 0)
    vec = pl.BlockSpec((1, W), lambda i: (0, 0))
    cws = pl.BlockSpec((CONV_K, W), lambda i: (0, 0))
    wsp = pl.BlockSpec((LRU_BLOCKS, LRU_BLOCK_W, LRU_BLOCK_W), lambda i: (0, 0, 0))
    row = pl.BlockSpec((R, W), lambda i: (rv(i), 0))
    wshape = jax.ShapeDtypeStruct((LRU_BLOCKS, LRU_BLOCK_W, LRU_BLOCK_W), F32)
    vshape = jax.ShapeDtypeStruct((1, W), F32)
    return pl.pallas_call(
        body, name=name, grid=(nb,),
        in_specs=[pl.BlockSpec((R, 2 * W), lambda i: (rv(i), 0)), pl.BlockSpec((SUBLANES, 2 * W), lambda i: (before(i), 0)),
                  cws, vec, wsp, wsp, vec, vec, vec, row, pl.BlockSpec((SUBLANES, W), lambda i: (before(i), 0)), row],
        out_specs=[pl.BlockSpec((R, 2 * W), lambda i: (rv(i), 0)), cws, vec, wsp, wsp, vec, vec, vec],
        out_shape=[jax.ShapeDtypeStruct((T, 2 * W), ACT_DTYPE), jax.ShapeDtypeStruct((CONV_K, W), F32), vshape,
                   wshape, wshape, vshape, vshape, vshape],
        scratch_shapes=[pltpu.VMEM((SUBLANES, W), F32), pltpu.VMEM((SUBLANES, W), F32)],
        compiler_params=_cparams("arbitrary"),
    )(proj, proj, conv_w, conv_b, wa, wx, ba, bx, sp, h, h, dhg)


MESH = pl.DeviceIdType.MESH
ANY = pl.BlockSpec(memory_space=pl.ANY)


def _position():
    x, y, c = lax.axis_index("x"), lax.axis_index("y"), lax.axis_index("c")
    other_chips = [(1 - x, y), (x, 1 - y), (1 - x, 1 - y)]
    return x, y, c, other_chips


def _all_gather_weights(shards, name):
    n = len(shards)

    def body(*refs):
        ins, outs = refs[:n], refs[n:2 * n]
        send_sems, recv_sems = refs[2 * n:]
        x, y, c, chips = _position()
        me = 2 * x + y
        sibling = (x, y, 1 - c)

        def rcopy(t, k, src, dst, to):
            return pltpu.make_async_remote_copy(src_ref=src, dst_ref=dst, send_sem=send_sems.at[t, k],
                                                recv_sem=recv_sems.at[t, k], device_id=to, device_id_type=MESH)

        started = []
        for t in range(n):
            for j, (cx, cy) in enumerate(chips):
                cp = rcopy(t, j, ins[t].at[c], outs[t].at[me, c], (cx, cy, c))
                cp.start()
                started.append(cp)
        for t in range(n):
            for j, (cx, cy) in enumerate(chips):
                blk = outs[t].at[2 * cx + cy, c]
                rcopy(t, j, blk, blk, (cx, cy, c)).wait_recv()
                cp = rcopy(t, 3 + j, blk, blk, sibling)
                cp.start()
                started.append(cp)
        for t in range(n):
            for j, (cx, cy) in enumerate(chips):
                blk = outs[t].at[2 * cx + cy, 1 - c]
                rcopy(t, 3 + j, blk, blk, sibling).wait_recv()
        for cp in started:
            cp.wait_send()

    return pl.pallas_call(
        body, name=name, in_specs=[ANY] * n, out_specs=[ANY] * n,
        out_shape=[jax.ShapeDtypeStruct((N_CHIPS,) + s.shape, s.dtype) for s in shards],
        scratch_shapes=[pltpu.SemaphoreType.DMA((n, 6)), pltpu.SemaphoreType.DMA((n, 6))],
    )(*shards)


HBM = pl.BlockSpec(memory_space=pltpu.HBM)
SEM = pl.BlockSpec(memory_space=pltpu.SEMAPHORE)
EFFECT = pltpu.SideEffectType.DATAFLOW_SIDE_EFFECTING


def _gather_start(own, land, after, name):
    n = len(own)

    def body(*refs):
        own_refs, land_refs = refs[:n], refs[n:2 * n]
        send_sems = refs[2 * n + 1:3 * n + 1]
        recv_sems = refs[3 * n + 1:4 * n + 1]
        x, y, c, chips = _position()
        me = 2 * x + y
        for t in range(n):
            for cx, cy in chips:
                pltpu.make_async_remote_copy(
                    src_ref=own_refs[t], dst_ref=land_refs[t].at[me], send_sem=send_sems[t], recv_sem=recv_sems[t],
                    device_id=(cx, cy, c), device_id_type=MESH).start()

    sems = (pltpu.SemaphoreType.DMA(()),) * (2 * n)
    thru = [pltpu.HBM(a.shape, a.dtype) for a in list(own) + list(land)]
    out = pl.pallas_call(
        body, name=name, out_shape=(*sems, *thru),
        in_specs=[HBM] * (2 * n) + [pl.BlockSpec(memory_space=pl.ANY)], out_specs=(SEM,) * (2 * n) + (HBM,) * (2 * n),
        input_output_aliases={i: 2 * n + i for i in range(2 * n)},
        compiler_params=pltpu.CompilerParams(has_side_effects=EFFECT),
    )(*[pltpu.with_memory_space_constraint(a, pltpu.HBM) for a in list(own) + list(land)], after)
    return list(out[:n]), list(out[n:2 * n]), list(out[2 * n:3 * n]), list(out[3 * n:])


def _gather_wait(send_sems, recv_sems, own, land, after, name):
    n = len(own)

    def body(*refs):
        land_refs = refs[n:2 * n]
        s_sems, r_sems = refs[2 * n:3 * n], refs[3 * n:4 * n]
        x, y, c, _ = _position()
        for t in range(n):
            three = land_refs[t].at[pl.ds(0, N_CHIPS - 1)]
            cp = pltpu.make_async_remote_copy(src_ref=three, dst_ref=three, send_sem=s_sems[t], recv_sem=r_sems[t],
                                              device_id=(x, y, c), device_id_type=MESH)
            cp.wait_send()
            cp.wait_recv()

    thru = [pltpu.HBM(a.shape, a.dtype) for a in list(own) + list(land)]
    out = pl.pallas_call(
        body, name=name, out_shape=tuple(thru),
        in_specs=[HBM] * (2 * n) + [SEM] * (2 * n) + [pl.BlockSpec(memory_space=pl.ANY)], out_specs=(HBM,) * (2 * n),
        input_output_aliases={i: i for i in range(2 * n)},
        compiler_params=pltpu.CompilerParams(has_side_effects=EFFECT),
    )(*own, *land, *send_sems, *recv_sems, after)
    return list(out[n:])


def _rs_to_sibling(grads, name):
    n = len(grads)

    def body(*refs):
        ins, outs = refs[:n], refs[n:2 * n]
        send_sems, recv_sems = refs[2 * n:]
        x, y, c, _ = _position()
        cps = [pltpu.make_async_remote_copy(src_ref=ins[t].at[:, 1 - c], dst_ref=outs[t], send_sem=send_sems.at[t],
                                            recv_sem=recv_sems.at[t], device_id=(x, y, 1 - c), device_id_type=MESH)
               for t in range(n)]
        for cp in cps:
            cp.start()
        for cp in cps:
            cp.wait()

    return pl.pallas_call(
        body, name=name, in_specs=[ANY] * n, out_specs=[ANY] * n,
        out_shape=[jax.ShapeDtypeStruct((N_CHIPS,) + g.shape[2:], g.dtype) for g in grads],
        scratch_shapes=[pltpu.SemaphoreType.DMA((n,)), pltpu.SemaphoreType.DMA((n,))],
    )(*grads)


def _rs_across_start(parts, recv, name):
    n = len(parts)

    def body(*refs):
        part_refs, recv_refs = refs[:n], refs[n:2 * n]
        send_sems, recv_sems = refs[2 * n:3 * n], refs[3 * n:4 * n]
        token_ref = refs[-1]
        x, y, c, chips = _position()
        me = 2 * x + y
        for t in range(n):
            for cx, cy in chips:
                pltpu.make_async_remote_copy(src_ref=part_refs[t].at[2 * cx + cy], dst_ref=recv_refs[t].at[me],
                                             send_sem=send_sems[t], recv_sem=recv_sems[t], device_id=(cx, cy, c),
                                             device_id_type=MESH).start()
        token_ref[...] = jnp.zeros_like(token_ref)

    sems = (pltpu.SemaphoreType.DMA(()),) * (2 * n)
    thru = [pltpu.HBM(a.shape, a.dtype) for a in list(parts) + list(recv)]
    out = pl.pallas_call(
        body, name=name, out_shape=(*sems, *thru, jax.ShapeDtypeStruct((SUBLANES, LANES), F32)),
        in_specs=[HBM] * (2 * n), out_specs=(SEM,) * (2 * n) + (HBM,) * (2 * n) + (pl.BlockSpec(memory_space=pltpu.VMEM),),
        input_output_aliases={i: 2 * n + i for i in range(2 * n)},
        compiler_params=pltpu.CompilerParams(has_side_effects=EFFECT),
    )(*[pltpu.with_memory_space_constraint(a, pltpu.HBM) for a in list(parts) + list(recv)])
    return out[:n], out[n:2 * n], list(out[2 * n:3 * n]), list(out[3 * n:4 * n]), out[-1]


def _rs_across_wait(send_sems, recv_sems, parts, recv, after, name):
    n = len(parts)

    def body(*refs):
        recv_refs = refs[n:2 * n]
        s_sems, r_sems = refs[2 * n:3 * n], refs[3 * n:4 * n]
        x, y, c, _ = _position()
        for t in range(n):
            three = recv_refs[t].at[pl.ds(0, N_CHIPS - 1)]
            cp = pltpu.make_async_remote_copy(src_ref=three, dst_ref=three, send_sem=s_sems[t], recv_sem=r_sems[t],
                                              device_id=(x, y, c), device_id_type=MESH)
            cp.wait_send()
            cp.wait_recv()

    thru = [pltpu.HBM(a.shape, a.dtype) for a in list(parts) + list(recv)]
    out = pl.pallas_call(
        body, name=name, out_shape=tuple(thru),
        in_specs=[HBM] * (2 * n) + [SEM] * (2 * n) + [pl.BlockSpec(memory_space=pl.ANY)], out_specs=(HBM,) * (2 * n),
        input_output_aliases={i: i for i in range(2 * n)},
        compiler_params=pltpu.CompilerParams(has_side_effects=EFFECT),
    )(*parts, *recv, *send_sems, *recv_sems, after)
    return list(out[:n]), list(out[n:])


def _rs_join_halves(halves, name):
    n = len(halves)

    def body(*refs):
        ins, outs = refs[:n], refs[n:2 * n]
        send_sems, recv_sems = refs[2 * n:]
        x, y, c, _ = _position()
        cps = [pltpu.make_async_remote_copy(src_ref=ins[t].at[c], dst_ref=outs[t].at[c], send_sem=send_sems.at[t],
                                            recv_sem=recv_sems.at[t], device_id=(x, y, 1 - c), device_id_type=MESH)
               for t in range(n)]
        for cp in cps:
            cp.start()
        for t in range(n):
            blk = outs[t].at[1 - c]
            pltpu.make_async_remote_copy(src_ref=blk, dst_ref=blk, send_sem=send_sems.at[t], recv_sem=recv_sems.at[t],
                                         device_id=(x, y, 1 - c), device_id_type=MESH).wait_recv()
        for cp in cps:
            cp.wait_send()

    return pl.pallas_call(
        body, name=name, in_specs=[ANY] * n, out_specs=[ANY] * n,
        out_shape=[jax.ShapeDtypeStruct(h.shape, h.dtype) for h in halves],
        input_output_aliases={t: t for t in range(n)},
        scratch_shapes=[pltpu.SemaphoreType.DMA((n,)), pltpu.SemaphoreType.DMA((n,))],
    )(*halves)


def _all_gather_small(block, name):
    m_per, n = block.shape

    def body(x_ref, out_ref, send_sems, recv_sems, local_sem):
        x, y, c, chips = _position()
        me, sibling = (x, y, c), (x, y, 1 - c)

        def rows(px, py, pc):
            return out_ref.at[pl.ds((4 * px + 2 * py + pc) * m_per, m_per), :]

        def copy(k, blk, to, src=None):
            return pltpu.make_async_remote_copy(
                src_ref=rows(*blk) if src is None else src, dst_ref=rows(*blk), send_sem=send_sems.at[k],
                recv_sem=recv_sems.at[k], device_id=to, device_id_type=MESH)

        mine = pltpu.make_async_copy(x_ref, rows(*me), local_sem)
        mine.start()
        first = [copy(0, me, sibling, src=x_ref)]
        first += [copy(1 + j, me, (*chip, c), src=x_ref) for j, chip in enumerate(chips)]
        for cp in first:
            cp.start()
        passed = [copy(4 + j, (*chip, c), sibling) for j, chip in enumerate(chips)]
        for j, chip in enumerate(chips):
            copy(1 + j, (*chip, c), me).wait_recv()
            passed[j].start()
        copy(0, sibling, me).wait_recv()
        for j, chip in enumerate(chips):
            copy(4 + j, (*chip, 1 - c), me).wait_recv()
        for cp in first + passed:
            cp.wait_send()
        mine.wait()

    return pl.pallas_call(
        body, name=name, out_shape=jax.ShapeDtypeStruct((N_DEV * m_per, n), block.dtype),
        in_specs=[pl.BlockSpec(memory_space=pltpu.VMEM)], out_specs=pl.BlockSpec(memory_space=pltpu.VMEM),
        scratch_shapes=[pltpu.SemaphoreType.DMA((7,)), pltpu.SemaphoreType.DMA((7,)), pltpu.SemaphoreType.DMA],
    )(block)


def _row_tile(R, n):
    budget = 1 << 19
    if R * n <= budget or R % SUBLANES:
        return R
    t = R
    while t * n > budget and t % (2 * SUBLANES) == 0:
        t //= 2
    return t


def _pair_sum(g, recv, c_arr, name):
    _, _, R, n = g.shape
    tr = _row_tile(R, n)

    def body(c_ref, g_ref, r_ref, o_ref):
        o_ref[...] = (g_ref[...] + r_ref[...]).astype(ICI_DTYPE)

    grid_spec = pltpu.PrefetchScalarGridSpec(
        num_scalar_prefetch=1, grid=(N_CHIPS, R // tr),
        in_specs=[pl.BlockSpec((None, None, tr, n), lambda p, i, c: (p, c[0], i, 0)),
                  pl.BlockSpec((None, tr, n), lambda p, i, c: (p, i, 0))],
        out_specs=pl.BlockSpec((None, tr, n), lambda p, i, c: (p, i, 0)))
    return pl.pallas_call(
        body, name=name, grid_spec=grid_spec, out_shape=jax.ShapeDtypeStruct(recv.shape, ICI_DTYPE),
        compiler_params=_cparams("parallel", "parallel"),
    )(c_arr, g, recv)


def _chip_sum(recv, own, where, name):
    _, R, n = recv.shape
    tr = _row_tile(R, n)

    def body(w_ref, r0, r1, r2, r3, own_ref, o_ref):
        me = w_ref[0]
        terms = [jnp.where(me == k, own_ref[...], r[...]).astype(F32) for k, r in enumerate((r0, r1, r2, r3))]
        o_ref[...] = ((terms[0] + terms[1]) + terms[2]) + terms[3]

    def slot(k):
        return pl.BlockSpec((None, tr, n), lambda i, w: (w[1 + k], i, 0))

    grid_spec = pltpu.PrefetchScalarGridSpec(
        num_scalar_prefetch=1, grid=(R // tr,),
        in_specs=[slot(0), slot(1), slot(2), slot(3), pl.BlockSpec((None, tr, n), lambda i, w: (w[0], i, 0))],
        out_specs=pl.BlockSpec((None, tr, n), lambda i, w: (w[5], i, 0)))
    return pl.pallas_call(
        body, name=name, grid_spec=grid_spec, out_shape=jax.ShapeDtypeStruct((2, R, n), F32),
        compiler_params=_cparams("parallel"),
    )(where, recv, recv, recv, recv, own)


ADAM_C1 = 1.0 / (1.0 - ADAM_B1 ** ADAM_STEP)
ADAM_C2 = 1.0 / (1.0 - ADAM_B2 ** ADAM_STEP)


def _adamw_math(w, g, m, v):
    m = ADAM_B1 * m + (1.0 - ADAM_B1) * g
    v = ADAM_B2 * v + (1.0 - ADAM_B2) * (g * g)
    delta = -ADAM_LR * ((m * ADAM_C1) / (jnp.sqrt(v * ADAM_C2) + ADAM_EPS) + ADAM_WD * w)
    return delta, m, v


def _adamw(w, g, m, v, name):
    R, n = w.shape
    tr = _row_tile(R, n)

    def body(w_ref, g_ref, m_ref, v_ref, d_ref, nm_ref, nv_ref):
        d_ref[...], nm_ref[...], nv_ref[...] = _adamw_math(w_ref[...], g_ref[...], m_ref[...], v_ref[...])

    spec = pl.BlockSpec((tr, n), lambda i: (i, 0))
    return pl.pallas_call(
        body, name=name, grid=(R // tr,), in_specs=[spec] * 4, out_specs=[spec] * 3,
        out_shape=[jax.ShapeDtypeStruct((R, n), F32)] * 3, compiler_params=_cparams("parallel"),
    )(w, g, m, v)


def _adamw_half(w, g, m, v, half, prev, name):
    R, n = w.shape
    tr = _row_tile(R // 2, n)
    off = half * (R // 2 // tr)

    def body(*refs):
        w_ref, g_ref, m_ref, v_ref = refs[:4]
        go_ref, d_ref, nm_ref, nv_ref = refs[-4:]
        gv = g_ref[...]
        go_ref[...] = gv
        d_ref[...], nm_ref[...], nv_ref[...] = _adamw_math(w_ref[...], gv, m_ref[...], v_ref[...])

    rows = pl.BlockSpec((tr, n), lambda i: (i + off, 0))
    carried = [] if prev is None else list(prev)
    return pl.pallas_call(
        body, name=name, grid=(R // 2 // tr,),
        in_specs=[rows, pl.BlockSpec((tr, n), lambda i: (i, 0)), rows, rows] + [ANY] * len(carried),
        out_specs=[rows] * 4, out_shape=[jax.ShapeDtypeStruct((R, n), F32)] * 4,
        input_output_aliases={4 + i: i for i in range(len(carried))},
        compiler_params=_cparams("parallel"),
    )(w, g, m, v, *carried)


def _adamw_small(w, gall, m, v, name):
    M, n = w.shape

    def body(w_ref, g_ref, m_ref, v_ref, gs_ref, d_ref, nm_ref, nv_ref):
        g = g_ref[0:M, :]
        for d in range(1, N_DEV):
            g = g + g_ref[d * M:(d + 1) * M, :]
        gs_ref[...] = g
        d_ref[...], nm_ref[...], nv_ref[...] = _adamw_math(w_ref[...], g, m_ref[...], v_ref[...])

    return pl.pallas_call(
        body, name=name, out_shape=[jax.ShapeDtypeStruct((M, n), F32)] * 4,
    )(w, gall, m, v)


SMALL_ROWS = 24
MID_ROWS = 4


def _pack_small(ln1_g, ln1_b, ln2_g, ln2_b, norm_w, sinks, a_log, dt_bias):
    mixed = jnp.concatenate([norm_w.reshape(-1), sinks.reshape(-1), a_log.reshape(-1), dt_bias.reshape(-1)])
    mixed = jnp.pad(mixed, (0, D_MODEL - mixed.shape[0]))[None]
    pad = jnp.zeros((SMALL_ROWS - 4 * DEPTH - 1, D_MODEL), F32)
    return jnp.concatenate([ln1_g, ln1_b, ln2_g, ln2_b, mixed, pad], axis=0)


def _unpack_small(p):
    mixed = p[4 * DEPTH]
    return (p[0:4], p[4:8], p[8:12], p[12:16], mixed[0:256].reshape(2, 128), mixed[256:272].reshape(2, 8),
            mixed[272:280].reshape(2, 4), mixed[280:288].reshape(2, 4))


def _pack_mid(conv_w, rconv_w, rconv_b, b_a, b_x, lam):
    lead = conv_w.shape[0]
    flat = jnp.concatenate([conv_w.reshape(lead, -1), rconv_w.reshape(lead, -1), rconv_b, b_a, b_x, lam], axis=1)
    return jnp.pad(flat, ((0, 0), (0, MID_ROWS * D_MODEL - flat.shape[1]))).reshape(lead, MID_ROWS, D_MODEL)


def _unpack_mid(p):
    lead = p.shape[:-2]
    f = p.reshape(lead + (MID_ROWS * D_MODEL,))
    return (f[..., 0:1536].reshape(lead + (4, 384)), f[..., 1536:2560].reshape(lead + (4, 256)),
            f[..., 2560:2816], f[..., 2816:3072], f[..., 3072:3328], f[..., 3328:3584])


def _cols_from_chips(g):
    p, L, R, n = g.shape
    return g.transpose(1, 2, 0, 3).reshape(L, R, p * n)


def _rows_from_chips(g):
    p, L, R, n = g.shape
    return g.transpose(1, 0, 2, 3).reshape(L, p * R, n)


def _cols_to_chips(g):
    L, R, n4 = g.shape
    return g.reshape(L, R, N_CHIPS, n4 // N_CHIPS).transpose(2, 0, 1, 3)


def _rows_to_chips(g):
    L, R4, n = g.shape
    return g.reshape(L, N_CHIPS, R4 // N_CHIPS, n).transpose(1, 0, 2, 3)


def _halves(a):
    return a.reshape(2, -1, a.shape[-1])


def _pad_hyb_cols(w):
    z = jnp.zeros(w.shape[:-1] + (HYB_PAD - HP_BG - 2 * B_HEADS,), w.dtype)
    return jnp.concatenate([w[..., 0:512], w[..., 768:2304], w[..., 2304:2816], w[..., 512:768], w[..., 2816:2824], z], axis=-1)


def _unpad_hyb_cols(w):
    return jnp.concatenate([w[..., 0:512], w[..., 2560:2816], w[..., 512:2048], w[..., 2048:2560], w[..., 2816:2824]], axis=-1)


def _hybrid_fwd(x, W, j, tables, sfx):
    cos, sin_s = tables
    T = x.shape[0]
    N = T // B_CHUNK
    proj = _matmul(x, W["hyb_w_in"][j], "nn", "hyb_in" + sfx)
    sinks_b = jnp.broadcast_to(W["hyb_sinks"][j][:, None], (A_Q_HEADS, LANES))
    o_a, lse = _attn_fwd(proj, cos, sin_s, sinks_b, "attn_fwd" + sfx)
    qkvc = _dnconv_fwd(proj, W["hyb_conv_w"][j], "dnconv_fwd" + sfx)
    bg = proj[:, HP_BG:HP_BG + 2 * B_HEADS]
    beta = jax.nn.sigmoid(bg[:, :B_HEADS])
    pre = bg[:, B_HEADS:] + W["hyb_dt_bias"][j][None]
    g = -jnp.exp(W["hyb_a_log"][j])[None] * jax.nn.softplus(pre)
    grow = g.T.reshape(B_HEADS, N, 1, B_CHUNK)
    brow = beta.T.reshape(B_HEADS, N, 1, B_CHUNK)
    nw = W["hyb_norm_w"][j][None]
    o_b, states, invs = _dn_fwd(qkvc, proj, nw, grow, brow, "dn_fwd" + sfx)
    mix = jnp.concatenate([o_a, o_b], axis=1).astype(ACT_DTYPE)
    y = _matmul(mix, W["hyb_w_out"][j], "nn", "hyb_out" + sfx)
    res = dict(proj=proj, o_a=o_a, lse=lse, qkvc=qkvc, beta=beta, pre=pre, g=g, grow=grow, brow=brow,
               states=states, invs=invs, mix=mix, sinks_b=sinks_b, nw=nw)
    return y, res


def _hybrid_bwd(x, du, dub, W, j, res, tables, sfx):
    cos, sin_s = tables
    T = x.shape[0]
    proj = res["proj"]
    d_wout = _matmul(res["mix"], dub, "tn", "hyb_out_dw" + sfx)
    dmix = _matmul(dub, W["hyb_w_out"][j], "nt", "hyb_out_dx" + sfx)
    dq, dkc, dkp, dvc, dvp, dsink = _attn_bwd(proj, cos, sin_s, res["sinks_b"], res["o_a"], res["lse"], dmix,
                                               "attn_bwd" + sfx)
    zpad = jnp.zeros((WINDOW, LANES), F32)
    dk = dkc + jnp.concatenate([dkp[WINDOW:], zpad], axis=0)
    dv = dvc + jnp.concatenate([dvp[WINDOW:], zpad], axis=0)
    dqc, dkcv, dvcv, dz, dg4, dbeta4, dnw = _dn_bwd(res["qkvc"], proj, res["nw"], res["grow"], res["brow"],
                                                    res["states"], res["invs"], dmix, "dn_bwd" + sfx)
    dqkvc = jnp.concatenate([dqc, dkcv, dvcv], axis=1)
    dqkvb, dconv = _dnconv_bwd(proj, W["hyb_conv_w"][j], dqkvc, "dnconv_bwd" + sfx)
    dg = dg4.reshape(B_HEADS, T).T
    dbeta = dbeta4.reshape(B_HEADS, T).T
    beta = res["beta"]
    dbeta_logit = dbeta * beta * (1.0 - beta)
    da_logit = dg * (-jnp.exp(W["hyb_a_log"][j]))[None] * jax.nn.sigmoid(res["pre"])
    d_dt_bias = jnp.sum(da_logit, axis=0)
    d_a_log = jnp.sum(dg * res["g"], axis=0)
    zcols = jnp.zeros((T, HYB_PAD - HP_BG - 2 * B_HEADS), F32)
    dproj = jnp.concatenate([dq, dqkvb, dz, dk, dv, dbeta_logit, da_logit, zcols], axis=1).astype(ACT_DTYPE)
    d_win = _matmul(x, dproj, "tn", "hyb_in_dw" + sfx)
    dx = _matmul(dproj, W["hyb_w_in"][j], "nt", "hyb_in_dx" + sfx, epi=_epi_add_residual, extra=du)
    grads = dict(hyb_w_in=d_win, hyb_w_out=d_wout, hyb_sinks=dsink[0], hyb_conv_w=dconv, hyb_a_log=d_a_log,
                 hyb_dt_bias=d_dt_bias, hyb_norm_w=jnp.sum(dnw[:, 0, :], axis=0))
    return dx, grads


def _rec_fwd(x, W, j, sfx):
    proj = _matmul(x, W["rec_w_in"][j], "nn", "rec_in" + sfx)
    sp = jax.nn.softplus(-W["rec_lambda"][j])[None]
    hg, h = _rglru_fwd(proj, W["rec_conv_w"][j], W["rec_conv_b"][j][None], W["rec_w_a"][j], W["rec_w_x"][j],
                       W["rec_b_a"][j][None], W["rec_b_x"][j][None], sp, "rglru_fwd" + sfx)
    y = _matmul(hg, W["rec_w_out"][j], "nn", "rec_out" + sfx)
    return y, dict(proj=proj, hg=hg, h=h, sp=sp)


def _rec_bwd(x, du, dub, W, j, res, sfx):
    d_wout = _matmul(res["hg"], dub, "tn", "rec_out_dw" + sfx)
    dhg = _matmul(dub, W["rec_w_out"][j], "nt", "rec_out_dx" + sfx)
    dproj, dcw, dcb, dwa, dwx, dba, dbx, dsp = _rglru_bwd(
        res["proj"], W["rec_conv_w"][j], W["rec_conv_b"][j][None], W["rec_w_a"][j], W["rec_w_x"][j],
        W["rec_b_a"][j][None], W["rec_b_x"][j][None], res["sp"], res["h"], dhg, "rglru_bwd" + sfx)
    d_lam = dsp[0] * (-jax.nn.sigmoid(-W["rec_lambda"][j]))
    d_win = _matmul(x, dproj, "tn", "rec_in_dw" + sfx)
    dx = _matmul(dproj, W["rec_w_in"][j], "nt", "rec_in_dx" + sfx, epi=_epi_add_residual, extra=du)
    grads = dict(rec_w_in=d_win, rec_w_out=d_wout, rec_conv_w=dcw, rec_conv_b=dcb[0], rec_w_a=dwa, rec_w_x=dwx,
                 rec_b_a=dba[0], rec_b_x=dbx[0], rec_lambda=d_lam)
    return dx, grads


def _local_step(x, tgt, W, mlp_w, mixer_ready, on_group):
    T = x.shape[0]
    tables = _rope_tables(T)
    acts = []
    xb = x.astype(ACT_DTYPE)
    for layer in range(DEPTH):
        j, sfx = layer // 2, ""
        mixer_ready(layer, xb)
        if layer % 2 == 0:
            y, res = _hybrid_fwd(xb, W, j, tables, sfx)
        else:
            y, res = _rec_fwd(xb, W, j, sfx)
        x1, x1b = _ln_fwd(x, y, W["ln1_g"][layer][None], W["ln1_b"][layer][None], "ln_fwd")
        w1, w2, wl = mlp_w(layer, x1b)
        h1 = _matmul(x1b, w1, "nn", "mlp_up", out_dtype=ACT_DTYPE, b_chips=("j", wl))
        y2 = _matmul(h1, w2, "nn", "mlp_down", a_fn=_relu2, b_chips=("k", wl))
        x2, x2b = _ln_fwd(x1, y2, W["ln2_g"][layer][None], W["ln2_b"][layer][None], "ln_fwd")
        acts.append(dict(x=x, xb=xb, y=y, res=res, x1=x1, x1b=x1b, h1=h1, y2=y2))
        x, xb = x2, x2b
    dx, loss = _loss_head(x, tgt, "loss_head")
    per_layer = [None] * DEPTH
    d_w1 = [lax.empty((N_CHIPS, 2, D_MODEL, D_FF // N_CHIPS), F32) for _ in range(DEPTH // 2)]
    d_w2 = [lax.empty((N_CHIPS, 2, D_FF // N_CHIPS, D_MODEL), F32) for _ in range(DEPTH // 2)]
    token = None
    for layer in reversed(range(DEPTH)):
        j, a = layer // 2, acts[layer]
        ln2_g = W["ln2_g"][layer][None]
        if token is not None:
            ln2_g = ln2_g + token
        du2, du2b, dg2, db2 = _ln_bwd(a["x1"], a["y2"], ln2_g, dx, "ln_bwd")
        w1, w2, wl = mlp_w(layer, du2b)
        d_w2[j] = _matmul(a["h1"], du2b, "tn", "mlp_down_dw", a_fn=_relu2, out_chips=("i", layer % 2, d_w2[j]))
        dh1 = _matmul(du2b, w2, "nt", "mlp_down_dx", epi=_epi_drelu2, extra=a["h1"], out_dtype=ACT_DTYPE,
                      b_chips=("j", wl))
        d_w1[j] = _matmul(a["x1b"], dh1, "tn", "mlp_up_dw", out_chips=("j", layer % 2, d_w1[j]))
        dx1 = _matmul(dh1, w1, "nt", "mlp_up_dx", epi=_epi_add_residual, extra=du2, b_chips=("k", wl))
        du1, du1b, dg1, db1 = _ln_bwd(a["x"], a["y"], W["ln1_g"][layer][None], dx1, "ln_bwd")
        if layer % 2 == 0:
            dx, g = _hybrid_bwd(a["xb"], du1, du1b, W, j, a["res"], tables, "")
        else:
            dx, g = _rec_bwd(a["xb"], du1, du1b, W, j, a["res"], "")
        g.update(ln1_g=dg1[0], ln1_b=db1[0], ln2_g=dg2[0], ln2_b=db2[0])
        per_layer[layer] = g
        if layer % 2 == 0:
            token = on_group(j, per_layer[layer], per_layer[layer + 1], d_w1[j], d_w2[j])
    grads = {}
    for name in ("ln1_g", "ln1_b", "ln2_g", "ln2_b"):
        grads[name] = jnp.stack([per_layer[l][name] for l in range(DEPTH)])
    for name in ("hyb_norm_w", "hyb_sinks", "hyb_a_log", "hyb_dt_bias"):
        grads[name] = jnp.stack([per_layer[l][name] for l in (0, 2)])
    return loss, dx, grads


BIG = ("hyb_w_in", "hyb_w_out", "rec_w_in", "rec_w_out", "mlp_w1", "mlp_w2", "rec_w_a", "rec_w_x")
COL_SHARDED = ("hyb_w_in", "rec_w_in", "mlp_w1")
CHIP_MAJOR = ("mlp_w1", "mlp_w2")
MID = ("hyb_conv_w", "rec_conv_w", "rec_conv_b", "rec_b_a", "rec_b_x", "rec_lambda")
SMALL = ("ln1_g", "ln1_b", "ln2_g", "ln2_b", "hyb_norm_w", "hyb_sinks", "hyb_a_log", "hyb_dt_bias")
WEIGHTS = ("hyb_w_in", "hyb_sinks", "hyb_conv_w", "hyb_a_log", "hyb_dt_bias", "hyb_norm_w", "hyb_w_out", "rec_w_in",
           "rec_conv_w", "rec_conv_b", "rec_w_a", "rec_b_a", "rec_w_x", "rec_b_x", "rec_lambda", "rec_w_out", "ln1_g",
           "ln1_b", "mlp_w1", "mlp_w2", "ln2_g", "ln2_b")


def _gather_full_weights(w):
    wb = {k: w[k].astype(MXU_DTYPE) for k in BIG}
    now = ("hyb_w_in", "hyb_w_out", "mlp_w1", "mlp_w2")
    shards = [_halves(wb[k][:1]) for k in now]
    shards.append(_pack_mid(*[w[k] for k in MID]))
    got = _all_gather_weights(shards, "all_gather_weights")
    me = 2 * lax.axis_index("x") + lax.axis_index("y")
    got = [lax.dynamic_update_slice(g, s[None], (me, 0, 0, 0)) for s, g in zip(shards, got)]

    def full(k, g):
        g = g.reshape((N_CHIPS,) + w[k].shape[1:])
        if k in CHIP_MAJOR:
            return g[:, None]
        if k in ("rec_w_a", "rec_w_x"):
            return g.transpose(1, 0, 2, 3).reshape(LRU_BLOCKS, LRU_BLOCK_W, LRU_BLOCK_W)
        f = _cols_from_chips(g[:, None])[0] if k in COL_SHARDED else _rows_from_chips(g[:, None])[0]
        return _pad_hyb_cols(f) if k == "hyb_w_in" else f

    rec = ("rec_w_in", "rec_w_out", "rec_w_a", "rec_w_x")
    groups = [(rec, 0), (CHIP_MAJOR, 1), (("hyb_w_in", "hyb_w_out"), 1), (CHIP_MAJOR, 2), (rec, 1), (CHIP_MAJOR, 3)]
    own = [wb[k][j] for names, j in groups for k in names]
    land = [lax.dynamic_update_slice(lax.empty((N_CHIPS,) + o.shape, o.dtype), o[None], (me,) + (0,) * o.ndim)
            for o in own]
    send_sems, recv_sems, own, land = _gather_start(own, land, got[-1], "gather_start")
    W = {k: [None] * w[k].shape[0] for k in BIG}
    for k, g in zip(now, got[:-1]):
        W[k][0] = full(k, g)
    arrived = [0]

    def ensure(upto, after):
        while arrived[0] <= upto:
            gi = arrived[0]
            names, j = groups[gi]
            lo = sum(len(nm) for nm, _ in groups[:gi])
            sl = slice(lo, lo + len(names))
            got_g = _gather_wait(send_sems[sl], recv_sems[sl], own[sl], land[sl], after, "gather_wait_%d" % gi)
            for k, g in zip(names, got_g):
                W[k][j] = full(k, g)
            arrived[0] += 1

    def mixer_ready(layer, after):
        if layer:
            ensure({1: 0, 2: 2, 3: 4}[layer], after)

    def mlp_w(layer, after):
        if layer:
            ensure({1: 1, 2: 3, 3: 5}[layer], after)
        return W["mlp_w1"][layer], W["mlp_w2"][layer], 0

    conv_w, rconv_w, rconv_b, b_a, b_x, lam = _unpack_mid(got[-1])
    W["hyb_conv_w"] = conv_w.transpose(1, 2, 0, 3).reshape(2, CONV_K, 3 * B_W)
    W["rec_conv_w"] = rconv_w.transpose(1, 2, 0, 3).reshape(2, CONV_K, D_MODEL)
    for k, v in (("rec_conv_b", rconv_b), ("rec_b_a", b_a), ("rec_b_x", b_x), ("rec_lambda", lam)):
        W[k] = v.transpose(1, 0, 2).reshape(2, D_MODEL)
    for k in SMALL:
        W[k] = w[k]
    return W, mlp_w, mixer_ready


def _group_by_chip(gh, gr, d_w1, d_w2):
    g = dict(gh, **gr)
    g["hyb_w_in"] = _unpad_hyb_cols(g["hyb_w_in"])
    out = []
    for k in BIG:
        if k == "mlp_w1":
            v = d_w1
        elif k == "mlp_w2":
            v = d_w2
        elif k in ("rec_w_a", "rec_w_x"):
            v = g[k].reshape(1, LRU_BLOCKS, N_CHIPS, LRU_BLOCK_W // N_CHIPS, LRU_BLOCK_W).transpose(2, 0, 1, 3, 4)
        else:
            v = _cols_to_chips(g[k][None]) if k in COL_SHARDED else _rows_to_chips(g[k][None])
        out.append(v.reshape(N_CHIPS, 2, -1, v.shape[-1]))
    conv_w = g["hyb_conv_w"].reshape(CONV_K, N_CHIPS, -1).transpose(1, 0, 2)
    rconv_w = g["rec_conv_w"].reshape(CONV_K, N_CHIPS, -1).transpose(1, 0, 2)
    vecs = [g[k].reshape(N_CHIPS, -1) for k in ("rec_conv_b", "rec_b_a", "rec_b_x", "rec_lambda")]
    out.append(_pack_mid(conv_w, rconv_w, *vecs).reshape(N_CHIPS, 2, MID_ROWS // 2, D_MODEL))
    return out


def kernel(x, hyb_w_in, hyb_sinks, hyb_conv_w, hyb_a_log, hyb_dt_bias, hyb_norm_w, hyb_w_out, rec_w_in, rec_conv_w, rec_conv_b, rec_w_a, rec_b_a, rec_w_x, rec_b_x, rec_lambda, rec_w_out, ln1_g, ln1_b, mlp_w1, mlp_w2, ln2_g, ln2_b, loss_target, m_hyb_w_in, m_hyb_sinks, m_hyb_conv_w, m_hyb_a_log, m_hyb_dt_bias, m_hyb_norm_w, m_hyb_w_out, m_rec_w_in, m_rec_conv_w, m_rec_conv_b, m_rec_w_a, m_rec_b_a, m_rec_w_x, m_rec_b_x, m_rec_lambda, m_rec_w_out, m_ln1_g, m_ln1_b, m_mlp_w1, m_mlp_w2, m_ln2_g, m_ln2_b, v_hyb_w_in, v_hyb_sinks, v_hyb_conv_w, v_hyb_a_log, v_hyb_dt_bias, v_hyb_norm_w, v_hyb_w_out, v_rec_w_in, v_rec_conv_w, v_rec_conv_b, v_rec_w_a, v_rec_b_a, v_rec_w_x, v_rec_b_x, v_rec_lambda, v_rec_w_out, v_ln1_g, v_ln1_b, v_mlp_w1, v_mlp_w2, v_ln2_g, v_ln2_b):
    args = locals()
    w = {k: args[k] for k in WEIGHTS}
    m = {k: args["m_" + k] for k in WEIGHTS}
    v = {k: args["v_" + k] for k in WEIGHTS}

    W, mlp_w, mixer_ready = _gather_full_weights(w)

    core = lax.axis_index("c").astype(jnp.int32)
    me = (2 * lax.axis_index("x") + lax.axis_index("y")).astype(jnp.int32)
    slots = jnp.arange(N_CHIPS, dtype=jnp.int32)
    where = jnp.concatenate([me[None], jnp.where(slots == me, (slots + 1) % N_CHIPS, slots), core[None]])
    state = {}

    def on_group(group, gh, gr, d_w1, d_w2):
        by_chip = _group_by_chip(gh, gr, d_w1, d_w2)
        from_sibling = _rs_to_sibling(by_chip, "rs_to_sibling")
        pair = [_pair_sum(g, r, core[None], "pair_sum") for g, r in zip(by_chip, from_sibling)]
        recv = [lax.empty(p.shape, p.dtype) for p in pair]
        state[group] = _rs_across_start(pair, recv, "rs_across_start_%d" % group)
        return state[group][4][0, 0]

    def finish_group(group, after, prev):
        send_sems, recv_sems, pair, recv, _ = state[group]
        pair, from_chips = _rs_across_wait(send_sems, recv_sems, pair, recv, after, "rs_across_wait_%d" % group)
        half = [_chip_sum(r, p, where, "chip_sum") for r, p in zip(from_chips, pair)]
        joined = _rs_join_halves(half, "rs_join_halves")
        outs = {}
        for k, g in zip(BIG, joined[:-1]):
            n = g.shape[-1]
            outs[k] = _adamw_half(w[k].reshape(-1, n), g.reshape(-1, n), m[k].reshape(-1, n), v[k].reshape(-1, n),
                                  group, None if prev is None else prev[k], "adamw")
        return outs, joined[-1]

    loss, dx, grads = _local_step(x[0], loss_target[0], W, mlp_w, mixer_ready, on_group)
    loss = lax.psum(loss[0, 0], ("x", "y", "c"))
    g_out, d_out, m_out, v_out = {}, {}, {}, {}

    upper, mid_upper = finish_group(1, state[0][4], None)
    small_g = _pack_small(*[grads[k] for k in SMALL])
    small_all = _all_gather_small(small_g, "all_gather_small")
    sw, sm, sv = (_pack_small(*[t[k] for k in SMALL]) for t in (w, m, v))
    sg, sd, snm, snv = _adamw_small(sw, small_all, sm, sv, "adamw_small")
    for dst, packed in ((g_out, sg), (d_out, sd), (m_out, snm), (v_out, snv)):
        for k, val in zip(SMALL, _unpack_small(packed)):
            dst[k] = val

    done = sum([upper[k][1][0, 0] for k in BIG], sg[0, 0]).reshape(1, 1)
    both, mid_lower = finish_group(0, done, upper)
    for k in BIG:
        shape = w[k].shape
        g_out[k], d_out[k], m_out[k], v_out[k] = (t.reshape(shape) for t in both[k])
    mid_w, mid_m, mid_v = (_pack_mid(*[t[k] for k in MID]).reshape(-1, D_MODEL) for t in (w, m, v))
    mid_g = jnp.concatenate([mid_lower.reshape(-1, D_MODEL), mid_upper.reshape(-1, D_MODEL)])
    mid_d, mid_nm, mid_nv = _adamw(mid_w, mid_g, mid_m, mid_v, "adamw_mid")
    for dst, packed in ((g_out, mid_g), (d_out, mid_d), (m_out, mid_nm), (v_out, mid_nv)):
        for k, val in zip(MID, _unpack_mid(packed.reshape(2, MID_ROWS, D_MODEL))):
            dst[k] = val.reshape(w[k].shape)

    return (loss, dx[None], *[g_out[k] for k in WEIGHTS], *[d_out[k] for k in WEIGHTS],
            *[m_out[k] for k in WEIGHTS], *[v_out[k] for k in WEIGHTS])
```

```python
import functools

import jax
import jax.numpy as jnp
import numpy as np
from jax import lax
from jax.experimental import pallas as pl
from jax.experimental.pallas import tpu as pltpu

F32 = jnp.float32
MXU_DTYPE = jnp.bfloat16
ACT_DTYPE = jnp.bfloat16
ICI_DTYPE = jnp.bfloat16

D_MODEL = 1024
DEPTH = 4
A_HEAD_DIM = 64
A_Q_HEADS = 8
A_KV_HEADS = 2
WINDOW = 128
ROPE_THETA = 10000.0
B_HEADS = 4
B_HEAD_DIM = 128
B_CHUNK = 64
CONV_K = 4
LRU_BLOCKS = 4
LRU_BLOCK_W = D_MODEL // LRU_BLOCKS
LRU_C = 8.0
D_FF = 4 * D_MODEL
A_Q_W = A_Q_HEADS * A_HEAD_DIM
A_KV_W = A_KV_HEADS * A_HEAD_DIM
B_W = B_HEADS * B_HEAD_DIM
HYB_PROJ = A_Q_W + 2 * A_KV_W + 4 * B_W + 2 * B_HEADS
DN_ALPHA = (2 * DEPTH) ** 0.25
LN_EPS = 1e-5
NORM_EPS = 1e-6
ADAM_LR = 0.001
ADAM_B1 = 0.9
ADAM_B2 = 0.999
ADAM_EPS = 1e-08
ADAM_WD = 0.01
ADAM_STEP = 10

HP_Q = 0
HP_QKVB = 512
HP_Z = 2048
HP_K = 2560
HP_V = 2688
HP_BG = 2816
HYB_PAD = 3072

N_CHIPS = 4
N_DEV = 8
V7X_VMEM_LIMIT = 48 * 1024 * 1024
LANES = 128
SUBLANES = 8
NEG_BIG = -1e30

NN = (((1,), (0,)), ((), ()))
NT = (((1,), (1,)), ((), ()))
TN = (((0,), (0,)), ((), ()))


def _cparams(*sem):
    return pltpu.CompilerParams(dimension_semantics=sem, vmem_limit_bytes=V7X_VMEM_LIMIT)


def _dot(a, b, dims=NN):
    return lax.dot_general(a.astype(MXU_DTYPE), b.astype(MXU_DTYPE), dims, preferred_element_type=F32)


def _split_bf16(a):
    hi = a.astype(jnp.bfloat16)
    return hi, (a - hi.astype(F32)).astype(jnp.bfloat16)


def _dotf(a, b, dims=NN):
    ah, al = _split_bf16(a)
    bh, bl = _split_bf16(b)
    dg = functools.partial(lax.dot_general, dimension_numbers=dims, preferred_element_type=F32)
    return dg(ah, bh) + (dg(ah, bl) + dg(al, bh))


def _tile(dim, pref):
    t = min(dim, pref)
    while dim % t:
        t //= 2
    return t


def _sigmoid(x):
    return 1.0 / (1.0 + jnp.exp(-x))


def _silu(x):
    return x * _sigmoid(x)


def _dsilu(x):
    s = _sigmoid(x)
    return s * (1.0 + x * (1.0 - s))


GELU_C = 0.7978845608028654
GELU_A = 0.044715


def _gelu(x):
    return 0.5 * x * (1.0 + jnp.tanh(GELU_C * (x + GELU_A * x * x * x)))


def _dgelu(x):
    t = jnp.tanh(GELU_C * (x + GELU_A * x * x * x))
    return 0.5 * (1.0 + t) + 0.5 * x * (1.0 - t * t) * GELU_C * (1.0 + 3.0 * GELU_A * x * x)


def _matmul(a, b, mode, name, *, tm=1024, tn=1024, tk=1024, a_fn=None, epi=None, extra=None, out_dtype=F32,
            b_chips=None, out_chips=None):
    if mode == "tn":
        K, M = a.shape
        tk = 2 * tk
    else:
        M, K = a.shape
    whole_k = False
    if b_chips is not None:
        g, b_layer = b_chips
        r, n = b.shape[2:]
        n_dim, k_dim = (r, n) if mode == "nt" else (n, r)
        N = N_CHIPS * n_dim if g == "j" else n_dim
        assert K == (N_CHIPS * k_dim if g == "k" else k_dim)
        if g == "j":
            tn = n_dim
        else:
            whole_k, tk, tm = True, K, tm // 2
    elif mode == "nt":
        N = b.shape[0]
        if tk < K <= 3 * tk:
            tk, tm = K, tm // 2
    else:
        N = b.shape[1]
    if out_chips is not None:
        og, o_layer, o_buf = out_chips
        if og == "j":
            tn = o_buf.shape[3]
        else:
            tm = o_buf.shape[2]
    tm, tn, tk = _tile(M, tm), _tile(N, tn), _tile(K, tk)
    nk = K // tk
    if mode == "tn":
        a_spec = pl.BlockSpec((tk, tm), lambda i, j, k: (k, i))
    else:
        a_spec = pl.BlockSpec((tm, tk), lambda i, j, k: (i, k))
    b_block = (tn, tk) if mode == "nt" else (tk, tn)
    if b_chips is None:
        b_spec = pl.BlockSpec(b_block, (lambda i, j, k: (j, k)) if mode == "nt" else (lambda i, j, k: (k, j)))
    elif mode == "nt" and whole_k:
        b_spec = pl.BlockSpec((N_CHIPS, None, tn, k_dim), lambda i, j, k: (0, b_layer, j, 0))
    elif mode == "nt":
        b_spec = pl.BlockSpec((None, None) + b_block, lambda i, j, k: (j, b_layer, 0, k))
    elif whole_k:
        b_spec = pl.BlockSpec((N_CHIPS, None, k_dim, tn), lambda i, j, k: (0, b_layer, 0, j))
    else:
        b_spec = pl.BlockSpec((None, None) + b_block, lambda i, j, k: (j, b_layer, k, 0))
    o_spec = pl.BlockSpec((tm, tn), lambda i, j, k: (i, j))
    e_spec = o_spec
    if out_chips is not None:
        o_spec = pl.BlockSpec((None, None, tm, tn), (lambda i, j, k: (j, o_layer, i, 0)) if og == "j"
                              else (lambda i, j, k: (i, o_layer, 0, j)))
    dims = {"nn": NN, "nt": NT, "tn": TN}[mode]
    has_extra = extra is not None
    n_in = 2 + has_extra + (out_chips is not None)

    def body(*refs):
        a_ref, b_ref = refs[0], refs[1]
        e_ref = refs[2] if has_extra else None
        o_ref = refs[n_in]
        av = a_ref[...]
        if a_fn is not None:
            av = a_fn(av)
        if whole_k and mode == "nt":
            part = _dot(av[:, :k_dim], b_ref[0], dims)
            for chip in range(1, N_CHIPS):
                part = part + _dot(av[:, chip * k_dim:(chip + 1) * k_dim], b_ref[chip], dims)
        else:
            bv = b_ref[...]
            if whole_k:
                bv = bv.reshape(K, tn)
            part = _dot(av, bv, dims)

        def finish(acc):
            if epi is not None:
                acc = epi(acc, e_ref[...])
            o_ref[...] = acc.astype(out_dtype)

        if nk == 1:
            finish(part)
        else:
            acc_ref = refs[-1]
            k = pl.program_id(2)

            @pl.when(k == 0)
            def _():
                acc_ref[...] = part

            @pl.when(k > 0)
            def _():
                acc_ref[...] += part

            @pl.when(k == nk - 1)
            def _():
                finish(acc_ref[...])

    in_specs = [a_spec, b_spec] + ([e_spec] if has_extra else [])
    args = (a, b) + ((extra,) if has_extra else ())
    out_shape = jax.ShapeDtypeStruct((M, N), out_dtype)
    aliases = {}
    if out_chips is not None:
        in_specs.append(pl.BlockSpec(memory_space=pl.ANY))
        args += (o_buf,)
        out_shape = jax.ShapeDtypeStruct(o_buf.shape, o_buf.dtype)
        aliases = {n_in - 1: 0}
    return pl.pallas_call(
        body, name=name, grid=(M // tm, N // tn, nk), in_specs=in_specs, out_specs=o_spec, out_shape=out_shape,
        input_output_aliases=aliases,
        scratch_shapes=[pltpu.VMEM((tm, tn), F32)] if nk > 1 else [],
        compiler_params=_cparams("parallel", "parallel", "arbitrary"),
    )(*args)


def _relu2(v):
    r = jnp.maximum(v, 0.0)
    return r * r


def _epi_drelu2(acc, h):
    return acc * (2.0 * jnp.maximum(h, 0.0))


def _epi_add_residual(acc, du):
    return acc + DN_ALPHA * du


def _ln_fwd(x, y, g, b, name):
    T, D = x.shape
    tr = _tile(T, 512)

    def body(x_ref, y_ref, g_ref, b_ref, o_ref, ob_ref):
        u = DN_ALPHA * x_ref[...] + y_ref[...]
        mu = jnp.mean(u, axis=-1, keepdims=True)
        d = u - mu
        var = jnp.mean(d * d, axis=-1, keepdims=True)
        o = d * lax.rsqrt(var + LN_EPS) * g_ref[...] + b_ref[...]
        o_ref[...] = o
        ob_ref[...] = o.astype(ACT_DTYPE)

    row = pl.BlockSpec((tr, D), lambda i: (i, 0))
    vec = pl.BlockSpec((1, D), lambda i: (0, 0))
    return pl.pallas_call(
        body, name=name, grid=(T // tr,), in_specs=[row, row, vec, vec], out_specs=[row, row],
        out_shape=[jax.ShapeDtypeStruct((T, D), F32), jax.ShapeDtypeStruct((T, D), ACT_DTYPE)],
        compiler_params=_cparams("parallel"),
    )(x, y, g, b)


def _ln_bwd(x, y, g, dout, name):
    T, D = x.shape
    tr = _tile(T, 512)

    def body(x_ref, y_ref, g_ref, d_ref, du_ref, dub_ref, dg_ref, db_ref):
        i = pl.program_id(0)
        u = DN_ALPHA * x_ref[...] + y_ref[...]
        mu = jnp.mean(u, axis=-1, keepdims=True)
        d = u - mu
        rstd = lax.rsqrt(jnp.mean(d * d, axis=-1, keepdims=True) + LN_EPS)
        xhat = d * rstd
        dout_v = d_ref[...]
        dxh = dout_v * g_ref[...]
        m1 = jnp.mean(dxh, axis=-1, keepdims=True)
        m2 = jnp.mean(dxh * xhat, axis=-1, keepdims=True)
        du = rstd * (dxh - m1 - xhat * m2)
        du_ref[...] = du
        dub_ref[...] = du.astype(ACT_DTYPE)
        pg = jnp.sum(dout_v * xhat, axis=0, keepdims=True)
        pb = jnp.sum(dout_v, axis=0, keepdims=True)

        @pl.when(i == 0)
        def _():
            dg_ref[...] = pg
            db_ref[...] = pb

        @pl.when(i > 0)
        def _():
            dg_ref[...] += pg
            db_ref[...] += pb

    row = pl.BlockSpec((tr, D), lambda i: (i, 0))
    vec = pl.BlockSpec((1, D), lambda i: (0, 0))
    return pl.pallas_call(
        body, name=name, grid=(T // tr,), in_specs=[row, row, vec, row], out_specs=[row, row, vec, vec],
        out_shape=[jax.ShapeDtypeStruct((T, D), F32), jax.ShapeDtypeStruct((T, D), ACT_DTYPE),
                   jax.ShapeDtypeStruct((1, D), F32), jax.ShapeDtypeStruct((1, D), F32)],
        compiler_params=_cparams("arbitrary"),
    )(x, y, g, dout)


def _loss_head(y, tgt, name):
    T, D = y.shape
    tr = _tile(T, 512)

    def body(y_ref, t_ref, dy_ref, l_ref):
        i = pl.program_id(0)
        e = y_ref[...] - t_ref[...]
        dy_ref[...] = e * (1.0 / D)
        part = jnp.sum(e * e, axis=(0, 1), keepdims=True) * (0.5 / D)

        @pl.when(i == 0)
        def _():
            l_ref[...] = part

        @pl.when(i > 0)
        def _():
            l_ref[...] += part

    row = pl.BlockSpec((tr, D), lambda i: (i, 0))
    one = pl.BlockSpec((1, 1), lambda i: (0, 0))
    return pl.pallas_call(
        body, name=name, grid=(T // tr,), in_specs=[row, row], out_specs=[row, one],
        out_shape=[jax.ShapeDtypeStruct((T, D), F32), jax.ShapeDtypeStruct((1, 1), F32)],
        compiler_params=_cparams("arbitrary"),
    )(y, tgt)


def _swap_half(x):
    n = x.shape[-1]
    lane = lax.broadcasted_iota(jnp.int32, x.shape, 1)
    first = (lane % A_HEAD_DIM) < (A_HEAD_DIM // 2)
    return jnp.where(first, pltpu.roll(x, n - A_HEAD_DIM // 2, axis=1), pltpu.roll(x, A_HEAD_DIM // 2, axis=1))


def _rope(x, cos, sin_signed):
    return x * cos + _swap_half(x) * sin_signed


def _rope_t(dy, cos, sin_signed):
    return dy * cos + _swap_half(dy * sin_signed)


def _rope_tables(T):
    half = A_HEAD_DIM // 2
    inv_freq = np.float32(ROPE_THETA) ** (-np.arange(half, dtype=np.float32) / np.float32(half))
    ang = np.arange(T, dtype=np.float32)[:, None] * inv_freq[None, :]
    cos = np.tile(np.cos(ang), (1, 4))
    sin = np.sin(ang)
    sin_signed = np.tile(np.concatenate([-sin, sin], axis=1), (1, 2))
    return jnp.asarray(cos, F32), jnp.asarray(sin_signed, F32)


def _band_mask(n):
    qi = lax.broadcasted_iota(jnp.int32, (WINDOW, 2 * WINDOW), 0)
    kj = lax.broadcasted_iota(jnp.int32, (WINDOW, 2 * WINDOW), 1)
    return (kj > qi) & (kj <= qi + WINDOW) & ((n > 0) | (kj >= WINDOW))


def _place(v, src_half, dst_half):
    lane = lax.broadcasted_iota(jnp.int32, v.shape, 1)
    if src_half != dst_half:
        v = pltpu.roll(v, A_HEAD_DIM, axis=1)
    keep = (lane >= A_HEAD_DIM) if dst_half else (lane < A_HEAD_DIM)
    return jnp.where(keep, v, 0.0)


def _attn_specs():
    kb, vb = HP_K // LANES, HP_V // LANES
    prev = lambda n: jnp.maximum(n - 1, 0)
    return dict(
        q=pl.BlockSpec((WINDOW, A_Q_W), lambda n: (n, 0)),
        kc=pl.BlockSpec((WINDOW, LANES), lambda n: (n, kb)),
        kp=pl.BlockSpec((WINDOW, LANES), lambda n: (prev(n), kb)),
        vc=pl.BlockSpec((WINDOW, LANES), lambda n: (n, vb)),
        vp=pl.BlockSpec((WINDOW, LANES), lambda n: (prev(n), vb)),
        tq=pl.BlockSpec((WINDOW, LANES), lambda n: (n, 0)),
        tp=pl.BlockSpec((WINDOW, LANES), lambda n: (prev(n), 0)),
        sink=pl.BlockSpec((A_Q_HEADS, LANES), lambda n: (0, 0)),
        row512=pl.BlockSpec((WINDOW, A_Q_W), lambda n: (n, 0)),
        row128=pl.BlockSpec((WINDOW, LANES), lambda n: (n, 0)),
        lse=pl.BlockSpec((WINDOW, A_Q_HEADS), lambda n: (n, 0)),
    )


def _attn_fwd(proj, cos, sin_s, sinks_b, name):
    T = proj.shape[0]
    sp = _attn_specs()

    def body(q_ref, kc_ref, kp_ref, vc_ref, vp_ref, cq_ref, sq_ref, cp_ref, sp_ref, sink_ref, o_ref, l_ref):
        n = pl.program_id(0)
        cq, sq = cq_ref[...], sq_ref[...]
        cq4, sq4 = jnp.tile(cq, (1, A_Q_W // LANES)), jnp.tile(sq, (1, A_Q_W // LANES))
        q = _rope(q_ref[...], cq4, sq4) * (A_HEAD_DIM ** -0.5)
        kc = _rope(kc_ref[...], cq, sq)
        kp = _rope(kp_ref[...], cp_ref[...], sp_ref[...])
        kk = jnp.concatenate([kp, kc], axis=0)
        vv = jnp.concatenate([vp_ref[...], vc_ref[...]], axis=0)
        mask = _band_mask(n)[None]
        lane = lax.broadcasted_iota(jnp.int32, (WINDOW, LANES), 1)
        lane8 = lax.broadcasted_iota(jnp.int32, (WINDOW, A_Q_HEADS), 1)
        qe = jnp.stack([_place(q[:, (hq // 2) * LANES:(hq // 2 + 1) * LANES], hq % 2, hq // 4) for hq in range(A_Q_HEADS)])
        kk8 = jnp.broadcast_to(kk[None], (A_Q_HEADS,) + kk.shape)
        vv8 = jnp.broadcast_to(vv[None], (A_Q_HEADS,) + vv.shape)
        sk = jnp.stack([sink_ref[hq:hq + 1, 0:1] for hq in range(A_Q_HEADS)])
        s = jnp.where(mask, _dot(qe, kk8, BNT), NEG_BIG)
        m = jnp.maximum(jnp.max(s, axis=-1, keepdims=True), sk)
        p = jnp.exp(s - m)
        den = jnp.sum(p, axis=-1, keepdims=True) + jnp.exp(sk - m)
        o = _dot(p * (1.0 / den), vv8, BNN)
        lse_h = m + jnp.log(den)
        outs = []
        lse = jnp.zeros((WINDOW, A_Q_HEADS), F32)
        for pb in range(A_Q_HEADS // 2):
            halves = [_place(o[2 * pb + e], pb // 2, e) for e in range(2)]
            outs.append(jnp.where(lane < A_HEAD_DIM, halves[0], halves[1]))
            for e in range(2):
                lse = jnp.where(lane8 == 2 * pb + e, lse_h[2 * pb + e], lse)
        o_ref[...] = jnp.concatenate(outs, axis=1)
        l_ref[...] = lse

    return pl.pallas_call(
        body, name=name, grid=(T // WINDOW,),
        in_specs=[sp["q"], sp["kc"], sp["kp"], sp["vc"], sp["vp"], sp["tq"], sp["tq"], sp["tp"], sp["tp"], sp["sink"]],
        out_specs=[sp["row512"], sp["lse"]],
        out_shape=[jax.ShapeDtypeStruct((T, A_Q_W), F32), jax.ShapeDtypeStruct((T, A_Q_HEADS), F32)],
        compiler_params=_cparams("parallel"),
    )(proj, proj, proj, proj, proj, cos, sin_s, cos, sin_s, sinks_b)


def _attn_bwd(proj, cos, sin_s, sinks_b, o, lse, dmix, name):
    T = proj.shape[0]
    sp = _attn_specs()

    def body(q_ref, kc_ref, kp_ref, vc_ref, vp_ref, cq_ref, sq_ref, cp_ref, sp_ref, sink_ref, o_ref, l_ref, do_ref,
             dq_ref, dkc_ref, dkp_ref, dvc_ref, dvp_ref, dsink_ref):
        n = pl.program_id(0)
        cq, sq = cq_ref[...], sq_ref[...]
        cp, sps = cp_ref[...], sp_ref[...]
        cq4, sq4 = jnp.tile(cq, (1, A_Q_W // LANES)), jnp.tile(sq, (1, A_Q_W // LANES))
        q = _rope(q_ref[...], cq4, sq4) * (A_HEAD_DIM ** -0.5)
        kc = _rope(kc_ref[...], cq, sq)
        kp = _rope(kp_ref[...], cp, sps)
        kk = jnp.concatenate([kp, kc], axis=0)
        vv = jnp.concatenate([vp_ref[...], vc_ref[...]], axis=0)
        mask = _band_mask(n)[None]
        lane = lax.broadcasted_iota(jnp.int32, (WINDOW, LANES), 1)
        do_all, o_all, l_all = do_ref[...], o_ref[...], l_ref[...]
        lane8 = lax.broadcasted_iota(jnp.int32, (WINDOW, A_Q_HEADS), 1)
        head8 = lax.broadcasted_iota(jnp.int32, (1, A_Q_HEADS), 1)
        prod = do_all * o_all
        qes, does, deltas, lhs = [], [], [], []
        for hq in range(A_Q_HEADS):
            pb, e, kvh = hq // 2, hq % 2, hq // 4
            blk = slice(pb * LANES, (pb + 1) * LANES)
            in_half = (lane >= A_HEAD_DIM) if e else (lane < A_HEAD_DIM)
            deltas.append(jnp.sum(jnp.where(in_half, prod[:, blk], 0.0), axis=-1, keepdims=True))
            qes.append(_place(q[:, blk], e, kvh))
            does.append(_place(do_all[:, blk], e, kvh))
            lhs.append(jnp.sum(jnp.where(lane8 == hq, l_all, 0.0), axis=-1, keepdims=True))
        qe, doe, delta, lh = jnp.stack(qes), jnp.stack(does), jnp.stack(deltas), jnp.stack(lhs)
        kk8 = jnp.broadcast_to(kk[None], (A_Q_HEADS,) + kk.shape)
        vv8 = jnp.broadcast_to(vv[None], (A_Q_HEADS,) + vv.shape)
        sk = jnp.stack([sink_ref[hq:hq + 1, 0:1] for hq in range(A_Q_HEADS)])
        s = _dot(qe, kk8, BNT)
        p = jnp.where(mask, jnp.exp(jnp.where(mask, s, NEG_BIG) - lh), 0.0)
        dvv = jnp.sum(_dot(p, doe, BTN), axis=0)
        ds = p * (_dot(doe, vv8, BNT) - delta)
        dkk = jnp.sum(_dot(ds, qe, BTN), axis=0)
        dqe = _dot(ds, kk8, BNN)
        dsink_h = -jnp.sum(jnp.exp(sk - lh) * delta, axis=(1, 2), keepdims=True)
        dqs = []
        dsk = jnp.zeros((1, A_Q_HEADS), F32)
        for pb in range(A_Q_HEADS // 2):
            halves = [_place(dqe[2 * pb + e], pb // 2, e) for e in range(2)]
            dqs.append(jnp.where(lane < A_HEAD_DIM, halves[0], halves[1]))
            for e in range(2):
                dsk = jnp.where(head8 == 2 * pb + e, dsink_h[2 * pb + e], dsk)
        dq = jnp.concatenate(dqs, axis=1) * (A_HEAD_DIM ** -0.5)
        dq_ref[...] = _rope_t(dq, cq4, sq4).astype(ACT_DTYPE)
        dkp_ref[...] = _rope_t(dkk[:WINDOW], cp, sps)
        dkc_ref[...] = _rope_t(dkk[WINDOW:], cq, sq)
        dvp_ref[...] = dvv[:WINDOW]
        dvc_ref[...] = dvv[WINDOW:]

        @pl.when(n == 0)
        def _():
            dsink_ref[...] = dsk

        @pl.when(n > 0)
        def _():
            dsink_ref[...] += dsk

    return pl.pallas_call(
        body, name=name, grid=(T // WINDOW,),
        in_specs=[sp["q"], sp["kc"], sp["kp"], sp["vc"], sp["vp"], sp["tq"], sp["tq"], sp["tp"], sp["tp"], sp["sink"],
                  sp["row512"], sp["lse"], sp["row512"]],
        out_specs=[sp["row512"], sp["row128"], sp["row128"], sp["row128"], sp["row128"],
                   pl.BlockSpec((1, A_Q_HEADS), lambda n: (0, 0))],
        out_shape=[jax.ShapeDtypeStruct((T, HYB_PAD), ACT_DTYPE)] + [jax.ShapeDtypeStruct((T, LANES), F32)] * 4
        + [jax.ShapeDtypeStruct((1, A_Q_HEADS), F32)],
        compiler_params=_cparams("arbitrary"),
    )(proj, proj, proj, proj, proj, cos, sin_s, cos, sin_s, sinks_b, o, lse, dmix)


def _shift_down(x, prev8, k):
    if k == 0:
        return x
    row = lax.broadcasted_iota(jnp.int32, prev8.shape, 0)
    r = pltpu.roll(x, k, axis=0)
    top = jnp.where(row < k, pltpu.roll(prev8, k, axis=0), r[:SUBLANES])
    return jnp.concatenate([top, r[SUBLANES:]], axis=0)


def _shift_up(x, next8, k):
    if k == 0:
        return x
    R = x.shape[0]
    row = lax.broadcasted_iota(jnp.int32, next8.shape, 0)
    r = pltpu.roll(x, R - k, axis=0)
    bot = jnp.where(row >= SUBLANES - k, pltpu.roll(next8, SUBLANES - k, axis=0), r[R - SUBLANES:])
    return jnp.concatenate([r[:R - SUBLANES], bot], axis=0)


def _conv(x, prev8, w):
    y = x * w[CONV_K - 1:CONV_K]
    for j in range(CONV_K - 1):
        y = y + _shift_down(x, prev8, CONV_K - 1 - j) * w[j:j + 1]
    return y


def _conv_bwd(x, prev8, w, dy, next8_dy):
    dx = dy * w[CONV_K - 1:CONV_K]
    dws = []
    for j in range(CONV_K - 1):
        k = CONV_K - 1 - j
        dx = dx + _shift_up(dy, next8_dy, k) * w[j:j + 1]
        dws.append(jnp.sum(dy * _shift_down(x, prev8, k), axis=0, keepdims=True))
    dws.append(jnp.sum(dy * x, axis=0, keepdims=True))
    return dx, dws


def _dnconv_fwd(proj, conv_w, name):
    T = proj.shape[0]
    R = _tile(T, 512)
    cb0 = HP_QKVB // A_Q_W

    def body(x_ref, w_ref, o_ref, prev_ref):
        i = pl.program_id(1)

        @pl.when(i == 0)
        def _():
            prev_ref[...] = jnp.zeros_like(prev_ref)

        x = x_ref[...]
        o_ref[...] = _silu(_conv(x, prev_ref[...], w_ref[...]))
        prev_ref[...] = x[R - SUBLANES:]

    return pl.pallas_call(
        body, name=name, grid=(3, T // R),
        in_specs=[pl.BlockSpec((R, B_W), lambda j, i: (i, cb0 + j)), pl.BlockSpec((CONV_K, B_W), lambda j, i: (0, j))],
        out_specs=pl.BlockSpec((R, B_W), lambda j, i: (i, j)),
        out_shape=jax.ShapeDtypeStruct((T, 3 * B_W), F32),
        scratch_shapes=[pltpu.VMEM((SUBLANES, B_W), F32)],
        compiler_params=_cparams("parallel", "arbitrary"),
    )(proj, conv_w)


def _dnconv_bwd(proj, conv_w, dy, dproj, name):
    T = proj.shape[0]
    R = _tile(T, 512)
    nb = T // R
    cb0 = HP_QKVB // A_Q_W
    r8 = R // SUBLANES

    def body(x_ref, xp_ref, w_ref, dy_ref, dproj_in, dx_ref, dw_ref, next_ref):
        i = pl.program_id(1)
        blk = nb - 1 - i

        @pl.when(i == 0)
        def _():
            next_ref[...] = jnp.zeros_like(next_ref)

        x = x_ref[...]
        prev8 = jnp.where(blk > 0, xp_ref[...], 0.0)
        w = w_ref[...]
        dpre = dy_ref[...] * _dsilu(_conv(x, prev8, w))
        dx, dw = _conv_bwd(x, prev8, w, dpre, next_ref[...])
        dx_ref[...] = dx.astype(ACT_DTYPE)
        next_ref[...] = dpre[:SUBLANES]

        @pl.when(i == 0)
        def _():
            for j in range(CONV_K):
                dw_ref[j:j + 1, :] = dw[j]

        @pl.when(i > 0)
        def _():
            for j in range(CONV_K):
                dw_ref[j:j + 1, :] += dw[j]

    return pl.pallas_call(
        body, name=name, grid=(3, nb),
        in_specs=[pl.BlockSpec((R, B_W), lambda j, i: (nb - 1 - i, cb0 + j)),
                  pl.BlockSpec((SUBLANES, B_W), lambda j, i: (jnp.maximum((nb - 1 - i) * r8 - 1, 0), cb0 + j)),
                  pl.BlockSpec((CONV_K, B_W), lambda j, i: (0, j)),
                  pl.BlockSpec((R, B_W), lambda j, i: (nb - 1 - i, j)), ANY],
        out_specs=[pl.BlockSpec((R, B_W), lambda j, i: (nb - 1 - i, cb0 + j)),
                   pl.BlockSpec((CONV_K, B_W), lambda j, i: (0, j))],
        out_shape=[jax.ShapeDtypeStruct(dproj.shape, dproj.dtype), jax.ShapeDtypeStruct((CONV_K, 3 * B_W), F32)],
        input_output_aliases={4: 0},
        scratch_shapes=[pltpu.VMEM((SUBLANES, B_W), F32)],
        compiler_params=_cparams("parallel", "arbitrary"),
    )(proj, proj, conv_w, dy, dproj)


DK_SCALE = B_HEAD_DIM ** -0.5


BNN = (((2,), (1,)), ((0,), (0,)))
BNT = (((2,), (2,)), ((0,), (0,)))
BTN = (((1,), (1,)), ((0,), (0,)))


def _tri_inv(a):
    C = a.shape[-1]
    ri = lax.broadcasted_iota(jnp.int32, (C, C), 0)
    ci = lax.broadcasted_iota(jnp.int32, (C, C), 1)
    x = jnp.where(ri == ci, 1.0, 0.0)[None] - a
    p = _dotf(a, a, BNN)
    span = 2
    while span < C:
        x = x + _dotf(x, p, BNN)
        span *= 2
        if span < C:
            p = _dotf(p, p, BNN)
    return x


def _dn_chunk(qc, kc, v, gcol, grow, bcol, s0, tm=None):
    C = B_CHUNK
    ri = lax.broadcasted_iota(jnp.int32, (C, C), 0)
    ci = lax.broadcasted_iota(jnp.int32, (C, C), 1)
    incl, strict = (ri >= ci)[None], (ri > ci)[None]
    rq = lax.rsqrt(jnp.sum(qc * qc, axis=-1, keepdims=True) + NORM_EPS)
    rk = lax.rsqrt(jnp.sum(kc * kc, axis=-1, keepdims=True) + NORM_EPS)
    qn = qc * rq
    q = qn * DK_SCALE
    k = kc * rk
    gc_col = jnp.sum(jnp.where(incl, grow, 0.0), axis=2, keepdims=True)
    gc_row = jnp.sum(jnp.where((ri <= ci)[None], gcol, 0.0), axis=1, keepdims=True)
    gl = jnp.sum(gcol, axis=1, keepdims=True)
    dincl = jnp.where(incl, jnp.exp(jnp.where(incl, gc_col - gc_row, 0.0)), 0.0)
    dstrict = jnp.where(strict, dincl, 0.0)
    eg = jnp.exp(gc_col)
    ekt = jnp.exp(gl - gc_col)
    egl = jnp.exp(gl)
    kb = k * bcol
    vb = v * bcol
    kbg = kb * eg
    a = _dot(kb, k, BNT) * dstrict
    if tm is None:
        tm = _tri_inv(a)
    u = _dot(tm, vb, BNN)
    w = _dot(tm, kbg, BNN)
    vn = u - _dot(w, s0, BNN)
    qk = _dot(q, k, BNT) * dincl
    qg = q * eg
    kt = k * ekt
    o = _dot(qg, s0, BNN) + _dot(qk, vn, BNN)
    s1 = s0 * egl + _dot(kt, vn, BTN)
    return dict(rq=rq, rk=rk, qn=qn, q=q, k=k, dincl=dincl, dstrict=dstrict, eg=eg, ekt=ekt, egl=egl, kb=kb, vb=vb,
                kbg=kbg, a=a, tm=tm, w=w, vn=vn, qk=qk, qg=qg, kt=kt, o=o, s1=s1, ri=ri[None], ci=ci[None])


def _heads(ref):
    return jnp.stack([ref[:, h * B_HEAD_DIM:(h + 1) * B_HEAD_DIM] for h in range(B_HEADS)])


def _store_heads(ref, val):
    for h in range(B_HEADS):
        ref[:, h * B_HEAD_DIM:(h + 1) * B_HEAD_DIM] = val[h]


def _dn_specs(N, rev):
    ix = (lambda n: N - 1 - n) if rev else (lambda n: n)
    wide = lambda cb: pl.BlockSpec((B_CHUNK, B_W), lambda n: (ix(n), cb))
    return dict(
        q=wide(0), k=wide(1), v=wide(2), z=wide(HP_Z // B_W), dob=wide(A_Q_W // B_W), out=wide(0),
        nw=pl.BlockSpec((1, LANES), lambda n: (0, 0)),
        row=pl.BlockSpec((B_HEADS, None, 1, B_CHUNK), lambda n: (0, ix(n), 0, 0)),
        state=pl.BlockSpec((B_HEADS, None, B_HEAD_DIM, B_HEAD_DIM), lambda n: (0, ix(n), 0, 0)),
        inv=pl.BlockSpec((B_HEADS, None, B_CHUNK, B_CHUNK), lambda n: (0, ix(n), 0, 0)),
    )


def _to_col(row):
    C = row.shape[-1]
    eye = lax.broadcasted_iota(jnp.int32, (C, C), 0) == lax.broadcasted_iota(jnp.int32, (C, C), 1)
    return jnp.sum(jnp.where(eye[None], row, 0.0), axis=2, keepdims=True)


def _to_row(col):
    C = col.shape[1]
    eye = lax.broadcasted_iota(jnp.int32, (C, C), 0) == lax.broadcasted_iota(jnp.int32, (C, C), 1)
    return jnp.sum(jnp.where(eye[None], col, 0.0), axis=1, keepdims=True)


def _dn_fwd(qkvc, proj, norm_w, grow, brow, name):
    T = qkvc.shape[0]
    N = T // B_CHUNK
    sp = _dn_specs(N, False)

    def body(q_ref, k_ref, v_ref, z_ref, nw_ref, gr_ref, br_ref, o_ref, st_ref, tm_ref, s_ref):
        n = pl.program_id(0)

        @pl.when(n == 0)
        def _():
            s_ref[...] = jnp.zeros_like(s_ref)

        s0 = s_ref[...]
        st_ref[...] = s0
        grow_v = gr_ref[...]
        f = _dn_chunk(_heads(q_ref), _heads(k_ref), _heads(v_ref), _to_col(grow_v), grow_v, _to_col(br_ref[...]), s0)
        o = f["o"]
        r = lax.rsqrt(jnp.mean(o * o, axis=-1, keepdims=True) + NORM_EPS)
        _store_heads(o_ref, o * r * nw_ref[...][None] * _silu(_heads(z_ref)))
        s_ref[...] = f["s1"]
        tm_ref[...] = f["tm"]

    return pl.pallas_call(
        body, name=name, grid=(N,),
        in_specs=[sp["q"], sp["k"], sp["v"], sp["z"], sp["nw"], sp["row"], sp["row"]],
        out_specs=[sp["out"], sp["state"], sp["inv"]],
        out_shape=[jax.ShapeDtypeStruct((T, B_W), F32),
                   jax.ShapeDtypeStruct((B_HEADS, N, B_HEAD_DIM, B_HEAD_DIM), F32),
                   jax.ShapeDtypeStruct((B_HEADS, N, B_CHUNK, B_CHUNK), F32)],
        scratch_shapes=[pltpu.VMEM((B_HEADS, B_HEAD_DIM, B_HEAD_DIM), F32)],
        compiler_params=_cparams("arbitrary"),
    )(qkvc, qkvc, qkvc, proj, norm_w, grow, brow)


def _dn_bwd(qkvc, proj, norm_w, grow, brow, states, invs, dmix, dproj, name):
    T = qkvc.shape[0]
    N = T // B_CHUNK
    sp = _dn_specs(N, True)
    C = B_CHUNK

    def body(q_ref, k_ref, v_ref, z_ref, nw_ref, gr_ref, br_ref, st_ref, tm_ref, dob_ref, dproj_in,
             dqkv_ref, dz_ref, dg_ref, db_ref, dnw_ref, ds_ref):
        n = pl.program_id(0)
        dq_ref = dqkv_ref.at[:, 0:B_W]
        dk_ref = dqkv_ref.at[:, B_W:2 * B_W]
        dv_ref = dqkv_ref.at[:, 2 * B_W:3 * B_W]

        @pl.when(n == 0)
        def _():
            ds_ref[...] = jnp.zeros_like(ds_ref)
            dnw_ref[...] = jnp.zeros_like(dnw_ref)

        s0 = st_ref[...]
        ds1 = ds_ref[...]
        v, z, nw, bcol_v = _heads(v_ref), _heads(z_ref), nw_ref[...][None], _to_col(br_ref[...])
        grow_v = gr_ref[...]
        f = _dn_chunk(_heads(q_ref), _heads(k_ref), v, _to_col(grow_v), grow_v, bcol_v, s0, tm=tm_ref[...])
        o, q, k, qn = f["o"], f["q"], f["k"], f["qn"]
        eg, ekt, egl = f["eg"], f["ekt"], f["egl"]
        tm, w, vn, kb, vb, kbg = f["tm"], f["w"], f["vn"], f["kb"], f["vb"], f["kbg"]
        qg, kt, qk, a = f["qg"], f["kt"], f["qk"], f["a"]
        ri, ci = f["ri"], f["ci"]

        dob_v = _heads(dob_ref)
        r = lax.rsqrt(jnp.mean(o * o, axis=-1, keepdims=True) + NORM_EPS)
        sz = _silu(z)
        on = o * r
        dnw_ref[...] += jnp.sum(dob_v * sz * on, axis=1, keepdims=True)
        _store_heads(dz_ref, (dob_v * on * nw * _dsilu(z)).astype(ACT_DTYPE))
        d_on = dob_v * sz * nw
        do = r * (d_on - on * jnp.mean(d_on * on, axis=-1, keepdims=True))

        dvn = _dot(qk, do, BTN) + _dot(kt, ds1, BNN)
        dqk = _dot(do, vn, BNT)
        dqg = _dot(do, s0, BNT)
        ds_ref[...] = _dot(qg, do, BTN) + egl * ds1 - _dot(w, dvn, BTN)
        dgl = jnp.sum(s0 * ds1, axis=(1, 2), keepdims=True) * egl
        dkt = _dot(vn, ds1, BNT)
        dw = -_dot(dvn, s0, BNT)
        dq = dqg * eg
        dgc = jnp.sum(dqg * qg, axis=-1, keepdims=True)
        dk = dkt * ekt
        t_kt = jnp.sum(dkt * kt, axis=-1, keepdims=True)
        dgl = dgl + jnp.sum(t_kt, axis=1, keepdims=True)
        dgc = dgc - t_kt
        dqkr = dqk * f["dincl"]
        dq = dq + _dot(dqkr, k, BNN)
        dk = dk + _dot(dqkr, q, BTN)
        e_qk = dqk * qk
        dgc = dgc + jnp.sum(e_qk, axis=-1, keepdims=True)
        dgc_row = -jnp.sum(e_qk, axis=1, keepdims=True)
        dtm = _dot(dvn, vb, BNT) + _dot(dw, kbg, BNT)
        dvb = _dot(tm, dvn, BTN)
        dkbg = _dot(tm, dw, BTN)
        dkb = dkbg * eg
        dgc = dgc + jnp.sum(dkbg * kbg, axis=-1, keepdims=True)
        da = -_dotf(tm, _dotf(dtm, tm, BNT), BTN)
        dkk = da * f["dstrict"]
        e_a = da * a
        dgc = dgc + jnp.sum(e_a, axis=-1, keepdims=True)
        dgc_row = dgc_row - jnp.sum(e_a, axis=1, keepdims=True)
        dkb = dkb + _dot(dkk, k, BNN)
        dk = dk + _dot(dkk, kb, BTN)
        dk = dk + dkb * bcol_v
        db_ref[...] = _to_row(jnp.sum(dkb * k, axis=-1, keepdims=True) + jnp.sum(dvb * v, axis=-1, keepdims=True))
        _store_heads(dv_ref, dvb * bcol_v)
        dgc_row = dgc_row + jnp.sum(jnp.where(ri == ci, dgc, 0.0), axis=1, keepdims=True)
        dg_ref[...] = jnp.sum(jnp.where(ci <= ri, _to_col(dgc_row), 0.0), axis=1, keepdims=True) + dgl
        dqs = dq * DK_SCALE
        _store_heads(dq_ref, f["rq"] * (dqs - qn * jnp.sum(dqs * qn, axis=-1, keepdims=True)))
        _store_heads(dk_ref, f["rk"] * (dk - k * jnp.sum(dk * k, axis=-1, keepdims=True)))

    return pl.pallas_call(
        body, name=name, grid=(N,),
        in_specs=[sp["q"], sp["k"], sp["v"], sp["z"], sp["nw"], sp["row"], sp["row"], sp["state"], sp["inv"], sp["dob"],
                  ANY],
        out_specs=[pl.BlockSpec((C, 3 * B_W), lambda n: (N - 1 - n, 0)), sp["z"], sp["row"], sp["row"],
                   pl.BlockSpec((B_HEADS, 1, LANES), lambda n: (0, 0, 0))],
        out_shape=[jax.ShapeDtypeStruct((T, 3 * B_W), F32), jax.ShapeDtypeStruct(dproj.shape, dproj.dtype),
                   jax.ShapeDtypeStruct((B_HEADS, N, 1, C), F32), jax.ShapeDtypeStruct((B_HEADS, N, 1, C), F32),
                   jax.ShapeDtypeStruct((B_HEADS, 1, LANES), F32)],
        input_output_aliases={10: 1},
        scratch_shapes=[pltpu.VMEM((B_HEADS, B_HEAD_DIM, B_HEAD_DIM), F32)],
        compiler_params=_cparams("arbitrary"),
    )(qkvc, qkvc, qkvc, proj, norm_w, grow, brow, states, invs, dmix, dproj)


def _lru_gates(xc, wa_ref, wx_ref, ba, bx, sp):
    pre_r, pre_i = [], []
    for hb in range(LRU_BLOCKS):
        xb = xc[:, hb * LRU_BLOCK_W:(hb + 1) * LRU_BLOCK_W]
        pre_r.append(_dot(xb, wa_ref[hb]))
        pre_i.append(_dot(xb, wx_ref[hb]))
    r = _sigmoid(jnp.concatenate(pre_r, axis=1) + ba)
    i = _sigmoid(jnp.concatenate(pre_i, axis=1) + bx)
    la = -LRU_C * r * sp
    a = jnp.exp(la)
    th = jnp.tanh(la)
    s = jnp.sqrt(-2.0 * th / (1.0 - th))
    return r, i, a, s


def _scan_down(a, b):
    R = a.shape[0]
    row = lax.broadcasted_iota(jnp.int32, a.shape, 0)
    d = 1
    while d < R:
        ok = row >= d
        b = a * jnp.where(ok, pltpu.roll(b, d, axis=0), 0.0) + b
        a = a * jnp.where(ok, pltpu.roll(a, d, axis=0), 1.0)
        d *= 2
    return a, b


def _scan_up(a, b):
    R = a.shape[0]
    row = lax.broadcasted_iota(jnp.int32, a.shape, 0)
    d = 1
    while d < R:
        ok = row < R - d
        b = a * jnp.where(ok, pltpu.roll(b, R - d, axis=0), 0.0) + b
        a = a * jnp.where(ok, pltpu.roll(a, R - d, axis=0), 1.0)
        d *= 2
    return b


def _rglru_fwd(proj, conv_w, conv_b, wa, wx, ba, bx, sp, name):
    T = proj.shape[0]
    R = _tile(T, 256)
    W = D_MODEL

    def body(p_ref, cw_ref, cb_ref, wa_ref, wx_ref, ba_ref, bx_ref, sp_ref, hg_ref, h_ref, prev_ref, hc_ref):
        i = pl.program_id(0)

        @pl.when(i == 0)
        def _():
            prev_ref[...] = jnp.zeros_like(prev_ref)
            hc_ref[...] = jnp.zeros_like(hc_ref)

        xr = p_ref[:, :W]
        gate = p_ref[:, W:]
        xc = _conv(xr, prev_ref[...], cw_ref[...]) + cb_ref[...]
        prev_ref[...] = xr[R - SUBLANES:]
        r, ig, a, s = _lru_gates(xc, wa_ref, wx_ref, ba_ref[...], bx_ref[...], sp_ref[...])
        pa, hb = _scan_down(a, s * ig * xc)
        h = hb + pa * hc_ref[SUBLANES - 1:SUBLANES, :]
        h_ref[...] = h
        hg_ref[...] = (h * _gelu(gate)).astype(ACT_DTYPE)
        hc_ref[...] = h[R - SUBLANES:]

    vec = pl.BlockSpec((1, W), lambda i: (0, 0))
    wsp = pl.BlockSpec((LRU_BLOCKS, LRU_BLOCK_W, LRU_BLOCK_W), lambda i: (0, 0, 0))
    row = pl.BlockSpec((R, W), lambda i: (i, 0))
    return pl.pallas_call(
        body, name=name, grid=(T // R,),
        in_specs=[pl.BlockSpec((R, 2 * W), lambda i: (i, 0)), pl.BlockSpec((CONV_K, W), lambda i: (0, 0)),
                  vec, wsp, wsp, vec, vec, vec],
        out_specs=[row, row],
        out_shape=[jax.ShapeDtypeStruct((T, W), ACT_DTYPE), jax.ShapeDtypeStruct((T, W), F32)],
        scratch_shapes=[pltpu.VMEM((SUBLANES, W), F32), pltpu.VMEM((SUBLANES, W), F32)],
        compiler_params=_cparams("arbitrary"),
    )(proj, conv_w, conv_b, wa, wx, ba, bx, sp)


def _rglru_bwd(proj, conv_w, conv_b, wa, wx, ba, bx, sp, h, dhg, name):
    T = proj.shape[0]
    R = _tile(T, 256)
    nb = T // R
    r8 = R // SUBLANES
    W = D_MODEL

    def body(p_ref, pp_ref, cw_ref, cb_ref, wa_ref, wx_ref, ba_ref, bx_ref, sp_ref, h_ref, hp_ref, dhg_ref,
             dp_ref, dcw_ref, dcb_ref, dwa_ref, dwx_ref, dba_ref, dbx_ref, dsp_ref, lam_ref, nxt_ref):
        step = pl.program_id(0)
        blk = nb - 1 - step

        @pl.when(step == 0)
        def _():
            lam_ref[...] = jnp.zeros_like(lam_ref)
            nxt_ref[...] = jnp.zeros_like(nxt_ref)

        xr = p_ref[:, :W]
        gate = p_ref[:, W:]
        first = blk > 0
        prev8 = jnp.where(first, pp_ref[:, :W], 0.0)
        hprev8 = jnp.where(first, hp_ref[...], 0.0)
        cw = cw_ref[...]
        spv = sp_ref[...]
        xc = _conv(xr, prev8, cw) + cb_ref[...]
        r, ig, a, s = _lru_gates(xc, wa_ref, wx_ref, ba_ref[...], bx_ref[...], spv)
        hv = h_ref[...]
        dhg_v = dhg_ref[...]
        dgate = dhg_v * hv * _dgelu(gate)
        dh = dhg_v * _gelu(gate)
        row = lax.broadcasted_iota(jnp.int32, (R, W), 0)
        last = row == R - 1
        a_up = jnp.where(last, 0.0, pltpu.roll(a, R - 1, axis=0))
        lam = _scan_up(a_up, dh + jnp.where(last, lam_ref[0:1, :], 0.0))
        lam_ref[...] = (a * lam)[:SUBLANES]
        h_dn = _shift_down(hv, hprev8, 1)
        da = lam * h_dn
        bx_in = ig * xc
        dsv = lam * bx_in
        dig = lam * s * xc
        dxc = lam * s * ig
        dla = da * a - dsv * (a * a) / s
        dr = dla * (-LRU_C) * spv
        dsp = jnp.sum(dla * (-LRU_C) * r, axis=0, keepdims=True)
        dpr = dr * r * (1.0 - r)
        dpi = dig * ig * (1.0 - ig)
        dxc_parts, dwa_parts, dwx_parts = [], [], []
        for hb in range(LRU_BLOCKS):
            sl = slice(hb * LRU_BLOCK_W, (hb + 1) * LRU_BLOCK_W)
            xb, gr, gi = xc[:, sl], dpr[:, sl], dpi[:, sl]
            dxc_parts.append(_dot(gr, wa_ref[hb], NT) + _dot(gi, wx_ref[hb], NT))
            dwa_parts.append(_dot(xb, gr, TN))
            dwx_parts.append(_dot(xb, gi, TN))
        dxc = dxc + jnp.concatenate(dxc_parts, axis=1)
        dxr, dcw = _conv_bwd(xr, prev8, cw, dxc, nxt_ref[...])
        nxt_ref[...] = dxc[:SUBLANES]
        dp_ref[:, :W] = dxr.astype(ACT_DTYPE)
        dp_ref[:, W:] = dgate.astype(ACT_DTYPE)
        dcb = jnp.sum(dxc, axis=0, keepdims=True)
        dba = jnp.sum(dpr, axis=0, keepdims=True)
        dbx = jnp.sum(dpi, axis=0, keepdims=True)

        @pl.when(step == 0)
        def _():
            for j in range(CONV_K):
                dcw_ref[j:j + 1, :] = dcw[j]
            dcb_ref[...] = dcb
            dba_ref[...] = dba
            dbx_ref[...] = dbx
            dsp_ref[...] = dsp
            for hb in range(LRU_BLOCKS):
                dwa_ref[hb] = dwa_parts[hb]
                dwx_ref[hb] = dwx_parts[hb]

        @pl.when(step > 0)
        def _():
            for j in range(CONV_K):
                dcw_ref[j:j + 1, :] += dcw[j]
            dcb_ref[...] += dcb
            dba_ref[...] += dba
            dbx_ref[...] += dbx
            dsp_ref[...] += dsp
            for hb in range(LRU_BLOCKS):
                dwa_ref[hb] += dwa_parts[hb]
                dwx_ref[hb] += dwx_parts[hb]

    rv = lambda i: nb - 1 - i
    before = lambda i: jnp.maximum((nb - 1 - i) * r8 - 1, 0)
    vec = pl.BlockSpec((1, W), lambda i: (0, 0))
    cws = pl.BlockSpec((CONV_K, W), lambda i: (0, 0))
    wsp = pl.BlockSpec((LRU_BLOCKS, LRU_BLOCK_W, LRU_BLOCK_W), lambda i: (0, 0, 0))
    row = pl.BlockSpec((R, W), lambda i: (rv(i), 0))
    wshape = jax.ShapeDtypeStruct((LRU_BLOCKS, LRU_BLOCK_W, LRU_BLOCK_W), F32)
    vshape = jax.ShapeDtypeStruct((1, W), F32)
    return pl.pallas_call(
        body, name=name, grid=(nb,),
        in_specs=[pl.BlockSpec((R, 2 * W), lambda i: (rv(i), 0)), pl.BlockSpec((SUBLANES, 2 * W), lambda i: (before(i), 0)),
                  cws, vec, wsp, wsp, vec, vec, vec, row, pl.BlockSpec((SUBLANES, W), lambda i: (before(i), 0)), row],
        out_specs=[pl.BlockSpec((R, 2 * W), lambda i: (rv(i), 0)), cws, vec, wsp, wsp, vec, vec, vec],
        out_shape=[jax.ShapeDtypeStruct((T, 2 * W), ACT_DTYPE), jax.ShapeDtypeStruct((CONV_K, W), F32), vshape,
                   wshape, wshape, vshape, vshape, vshape],
        scratch_shapes=[pltpu.VMEM((SUBLANES, W), F32), pltpu.VMEM((SUBLANES, W), F32)],
        compiler_params=_cparams("arbitrary"),
    )(proj, proj, conv_w, conv_b, wa, wx, ba, bx, sp, h, h, dhg)


MESH = pl.DeviceIdType.MESH
ANY = pl.BlockSpec(memory_space=pl.ANY)


def _position():
    x, y, c = lax.axis_index("x"), lax.axis_index("y"), lax.axis_index("c")
    other_chips = [(1 - x, y), (x, 1 - y), (1 - x, 1 - y)]
    return x, y, c, other_chips


def _all_gather_weights(shards, name):
    n = len(shards)

    def body(*refs):
        ins, outs = refs[:n], refs[n:2 * n]
        send_sems, recv_sems = refs[2 * n:]
        x, y, c, chips = _position()
        me = 2 * x + y
        sibling = (x, y, 1 - c)

        def rcopy(t, k, src, dst, to):
            return pltpu.make_async_remote_copy(src_ref=src, dst_ref=dst, send_sem=send_sems.at[t, k],
                                                recv_sem=recv_sems.at[t, k], device_id=to, device_id_type=MESH)

        started = []
        for t in range(n):
            for j, (cx, cy) in enumerate(chips):
                cp = rcopy(t, j, ins[t].at[c], outs[t].at[me, c], (cx, cy, c))
                cp.start()
                started.append(cp)
        for t in range(n):
            for j, (cx, cy) in enumerate(chips):
                blk = outs[t].at[2 * cx + cy, c]
                rcopy(t, j, blk, blk, (cx, cy, c)).wait_recv()
                cp = rcopy(t, 3 + j, blk, blk, sibling)
                cp.start()
                started.append(cp)
        for t in range(n):
            for j, (cx, cy) in enumerate(chips):
                blk = outs[t].at[2 * cx + cy, 1 - c]
                rcopy(t, 3 + j, blk, blk, sibling).wait_recv()
        for cp in started:
            cp.wait_send()

    return pl.pallas_call(
        body, name=name, in_specs=[ANY] * n, out_specs=[ANY] * n,
        out_shape=[jax.ShapeDtypeStruct((N_CHIPS,) + s.shape, s.dtype) for s in shards],
        scratch_shapes=[pltpu.SemaphoreType.DMA((n, 6)), pltpu.SemaphoreType.DMA((n, 6))],
    )(*shards)


HBM = pl.BlockSpec(memory_space=pltpu.HBM)
SEM = pl.BlockSpec(memory_space=pltpu.SEMAPHORE)
EFFECT = pltpu.SideEffectType.DATAFLOW_SIDE_EFFECTING


def _gather_start(own, land, after, name):
    n = len(own)

    def body(*refs):
        own_refs, land_refs = refs[:n], refs[n:2 * n]
        send_sems = refs[2 * n + 1:3 * n + 1]
        recv_sems = refs[3 * n + 1:4 * n + 1]
        x, y, c, chips = _position()
        me = 2 * x + y
        for t in range(n):
            for cx, cy in chips:
                pltpu.make_async_remote_copy(
                    src_ref=own_refs[t], dst_ref=land_refs[t].at[me], send_sem=send_sems[t], recv_sem=recv_sems[t],
                    device_id=(cx, cy, c), device_id_type=MESH).start()

    sems = (pltpu.SemaphoreType.DMA(()),) * (2 * n)
    thru = [pltpu.HBM(a.shape, a.dtype) for a in list(own) + list(land)]
    out = pl.pallas_call(
        body, name=name, out_shape=(*sems, *thru),
        in_specs=[HBM] * (2 * n) + [pl.BlockSpec(memory_space=pl.ANY)], out_specs=(SEM,) * (2 * n) + (HBM,) * (2 * n),
        input_output_aliases={i: 2 * n + i for i in range(2 * n)},
        compiler_params=pltpu.CompilerParams(has_side_effects=EFFECT),
    )(*[pltpu.with_memory_space_constraint(a, pltpu.HBM) for a in list(own) + list(land)], after)
    return list(out[:n]), list(out[n:2 * n]), list(out[2 * n:3 * n]), list(out[3 * n:])


def _gather_wait(send_sems, recv_sems, own, land, after, name):
    n = len(own)

    def body(*refs):
        land_refs = refs[n:2 * n]
        s_sems, r_sems = refs[2 * n:3 * n], refs[3 * n:4 * n]
        x, y, c, _ = _position()
        for t in range(n):
            three = land_refs[t].at[pl.ds(0, N_CHIPS - 1)]
            cp = pltpu.make_async_remote_copy(src_ref=three, dst_ref=three, send_sem=s_sems[t], recv_sem=r_sems[t],
                                              device_id=(x, y, c), device_id_type=MESH)
            cp.wait_send()
            cp.wait_recv()

    thru = [pltpu.HBM(a.shape, a.dtype) for a in list(own) + list(land)]
    out = pl.pallas_call(
        body, name=name, out_shape=tuple(thru),
        in_specs=[HBM] * (2 * n) + [SEM] * (2 * n) + [pl.BlockSpec(memory_space=pl.ANY)], out_specs=(HBM,) * (2 * n),
        input_output_aliases={i: i for i in range(2 * n)},
        compiler_params=pltpu.CompilerParams(has_side_effects=EFFECT),
    )(*own, *land, *send_sems, *recv_sems, after)
    return list(out[n:])


def _rs_to_sibling(grads, name):
    n = len(grads)

    def body(*refs):
        ins, outs = refs[:n], refs[n:2 * n]
        send_sems, recv_sems = refs[2 * n:]
        x, y, c, _ = _position()
        cps = [pltpu.make_async_remote_copy(src_ref=ins[t].at[:, 1 - c], dst_ref=outs[t], send_sem=send_sems.at[t],
                                            recv_sem=recv_sems.at[t], device_id=(x, y, 1 - c), device_id_type=MESH)
               for t in range(n)]
        for cp in cps:
            cp.start()
        for cp in cps:
            cp.wait()

    return pl.pallas_call(
        body, name=name, in_specs=[ANY] * n, out_specs=[ANY] * n,
        out_shape=[jax.ShapeDtypeStruct((N_CHIPS,) + g.shape[2:], g.dtype) for g in grads],
        scratch_shapes=[pltpu.SemaphoreType.DMA((n,)), pltpu.SemaphoreType.DMA((n,))],
    )(*grads)


def _rs_across_start(parts, recv, name):
    n = len(parts)

    def body(*refs):
        part_refs, recv_refs = refs[:n], refs[n:2 * n]
        send_sems, recv_sems = refs[2 * n:3 * n], refs[3 * n:4 * n]
        token_ref = refs[-1]
        x, y, c, chips = _position()
        me = 2 * x + y
        for t in range(n):
            for cx, cy in chips:
                pltpu.make_async_remote_copy(src_ref=part_refs[t].at[2 * cx + cy], dst_ref=recv_refs[t].at[me],
                                             send_sem=send_sems[t], recv_sem=recv_sems[t], device_id=(cx, cy, c),
                                             device_id_type=MESH).start()
        token_ref[...] = jnp.zeros_like(token_ref)

    sems = (pltpu.SemaphoreType.DMA(()),) * (2 * n)
    thru = [pltpu.HBM(a.shape, a.dtype) for a in list(parts) + list(recv)]
    out = pl.pallas_call(
        body, name=name, out_shape=(*sems, *thru, jax.ShapeDtypeStruct((SUBLANES, LANES), F32)),
        in_specs=[HBM] * (2 * n), out_specs=(SEM,) * (2 * n) + (HBM,) * (2 * n) + (pl.BlockSpec(memory_space=pltpu.VMEM),),
        input_output_aliases={i: 2 * n + i for i in range(2 * n)},
        compiler_params=pltpu.CompilerParams(has_side_effects=EFFECT),
    )(*[pltpu.with_memory_space_constraint(a, pltpu.HBM) for a in list(parts) + list(recv)])
    return out[:n], out[n:2 * n], list(out[2 * n:3 * n]), list(out[3 * n:4 * n]), out[-1]


def _rs_across_wait(send_sems, recv_sems, parts, recv, after, name):
    n = len(parts)

    def body(*refs):
        recv_refs = refs[n:2 * n]
        s_sems, r_sems = refs[2 * n:3 * n], refs[3 * n:4 * n]
        x, y, c, _ = _position()
        for t in range(n):
            three = recv_refs[t].at[pl.ds(0, N_CHIPS - 1)]
            cp = pltpu.make_async_remote_copy(src_ref=three, dst_ref=three, send_sem=s_sems[t], recv_sem=r_sems[t],
                                              device_id=(x, y, c), device_id_type=MESH)
            cp.wait_send()
            cp.wait_recv()

    thru = [pltpu.HBM(a.shape, a.dtype) for a in list(parts) + list(recv)]
    out = pl.pallas_call(
        body, name=name, out_shape=tuple(thru),
        in_specs=[HBM] * (2 * n) + [SEM] * (2 * n) + [pl.BlockSpec(memory_space=pl.ANY)], out_specs=(HBM,) * (2 * n),
        input_output_aliases={i: i for i in range(2 * n)},
        compiler_params=pltpu.CompilerParams(has_side_effects=EFFECT),
    )(*parts, *recv, *send_sems, *recv_sems, after)
    return list(out[:n]), list(out[n:])


def _rs_join_halves(halves, name):
    n = len(halves)

    def body(*refs):
        ins, outs = refs[:n], refs[n:2 * n]
        send_sems, recv_sems = refs[2 * n:]
        x, y, c, _ = _position()
        cps = [pltpu.make_async_remote_copy(src_ref=ins[t].at[c], dst_ref=outs[t].at[c], send_sem=send_sems.at[t],
                                            recv_sem=recv_sems.at[t], device_id=(x, y, 1 - c), device_id_type=MESH)
               for t in range(n)]
        for cp in cps:
            cp.start()
        for t in range(n):
            blk = outs[t].at[1 - c]
            pltpu.make_async_remote_copy(src_ref=blk, dst_ref=blk, send_sem=send_sems.at[t], recv_sem=recv_sems.at[t],
                                         device_id=(x, y, 1 - c), device_id_type=MESH).wait_recv()
        for cp in cps:
            cp.wait_send()

    return pl.pallas_call(
        body, name=name, in_specs=[ANY] * n, out_specs=[ANY] * n,
        out_shape=[jax.ShapeDtypeStruct(h.shape, h.dtype) for h in halves],
        input_output_aliases={t: t for t in range(n)},
        scratch_shapes=[pltpu.SemaphoreType.DMA((n,)), pltpu.SemaphoreType.DMA((n,))],
    )(*halves)


def _all_gather_small(block, name):
    m_per, n = block.shape

    def body(x_ref, out_ref, send_sems, recv_sems, local_sem):
        x, y, c, chips = _position()
        me, sibling = (x, y, c), (x, y, 1 - c)

        def rows(px, py, pc):
            return out_ref.at[pl.ds((4 * px + 2 * py + pc) * m_per, m_per), :]

        def copy(k, blk, to, src=None):
            return pltpu.make_async_remote_copy(
                src_ref=rows(*blk) if src is None else src, dst_ref=rows(*blk), send_sem=send_sems.at[k],
                recv_sem=recv_sems.at[k], device_id=to, device_id_type=MESH)

        mine = pltpu.make_async_copy(x_ref, rows(*me), local_sem)
        mine.start()
        first = [copy(0, me, sibling, src=x_ref)]
        first += [copy(1 + j, me, (*chip, c), src=x_ref) for j, chip in enumerate(chips)]
        for cp in first:
            cp.start()
        passed = [copy(4 + j, (*chip, c), sibling) for j, chip in enumerate(chips)]
        for j, chip in enumerate(chips):
            copy(1 + j, (*chip, c), me).wait_recv()
            passed[j].start()
        copy(0, sibling, me).wait_recv()
        for j, chip in enumerate(chips):
            copy(4 + j, (*chip, 1 - c), me).wait_recv()
        for cp in first + passed:
            cp.wait_send()
        mine.wait()

    return pl.pallas_call(
        body, name=name, out_shape=jax.ShapeDtypeStruct((N_DEV * m_per, n), block.dtype),
        in_specs=[pl.BlockSpec(memory_space=pltpu.VMEM)], out_specs=pl.BlockSpec(memory_space=pltpu.VMEM),
        scratch_shapes=[pltpu.SemaphoreType.DMA((7,)), pltpu.SemaphoreType.DMA((7,)), pltpu.SemaphoreType.DMA],
    )(block)


def _row_tile(R, n):
    budget = 1 << 19
    if R * n <= budget or R % SUBLANES:
        return R
    t = R
    while t * n > budget and t % (2 * SUBLANES) == 0:
        t //= 2
    return t


def _pair_sum(g, recv, c_arr, name):
    _, _, R, n = g.shape
    tr = _row_tile(R, n)

    def body(c_ref, g_ref, r_ref, o_ref):
        o_ref[...] = (g_ref[...] + r_ref[...]).astype(ICI_DTYPE)

    grid_spec = pltpu.PrefetchScalarGridSpec(
        num_scalar_prefetch=1, grid=(N_CHIPS, R // tr),
        in_specs=[pl.BlockSpec((None, None, tr, n), lambda p, i, c: (p, c[0], i, 0)),
                  pl.BlockSpec((None, tr, n), lambda p, i, c: (p, i, 0))],
        out_specs=pl.BlockSpec((None, tr, n), lambda p, i, c: (p, i, 0)))
    return pl.pallas_call(
        body, name=name, grid_spec=grid_spec, out_shape=jax.ShapeDtypeStruct(recv.shape, ICI_DTYPE),
        compiler_params=_cparams("parallel", "parallel"),
    )(c_arr, g, recv)


def _chip_sum(recv, own, where, name):
    _, R, n = recv.shape
    tr = _row_tile(R, n)

    def body(w_ref, r0, r1, r2, r3, own_ref, o_ref):
        me = w_ref[0]
        terms = [jnp.where(me == k, own_ref[...], r[...]).astype(F32) for k, r in enumerate((r0, r1, r2, r3))]
        o_ref[...] = ((terms[0] + terms[1]) + terms[2]) + terms[3]

    def slot(k):
        return pl.BlockSpec((None, tr, n), lambda i, w: (w[1 + k], i, 0))

    grid_spec = pltpu.PrefetchScalarGridSpec(
        num_scalar_prefetch=1, grid=(R // tr,),
        in_specs=[slot(0), slot(1), slot(2), slot(3), pl.BlockSpec((None, tr, n), lambda i, w: (w[0], i, 0))],
        out_specs=pl.BlockSpec((None, tr, n), lambda i, w: (w[5], i, 0)))
    return pl.pallas_call(
        body, name=name, grid_spec=grid_spec, out_shape=jax.ShapeDtypeStruct((2, R, n), F32),
        compiler_params=_cparams("parallel"),
    )(where, recv, recv, recv, recv, own)


ADAM_C1 = 1.0 / (1.0 - ADAM_B1 ** ADAM_STEP)
ADAM_C2 = 1.0 / (1.0 - ADAM_B2 ** ADAM_STEP)


def _adamw_math(w, g, m, v):
    m = ADAM_B1 * m + (1.0 - ADAM_B1) * g
    v = ADAM_B2 * v + (1.0 - ADAM_B2) * (g * g)
    delta = -ADAM_LR * ((m * ADAM_C1) / (jnp.sqrt(v * ADAM_C2) + ADAM_EPS) + ADAM_WD * w)
    return delta, m, v


def _adamw(w, g, m, v, name):
    R, n = w.shape
    tr = _row_tile(R, n)

    def body(w_ref, g_ref, m_ref, v_ref, d_ref, nm_ref, nv_ref):
        d_ref[...], nm_ref[...], nv_ref[...] = _adamw_math(w_ref[...], g_ref[...], m_ref[...], v_ref[...])

    spec = pl.BlockSpec((tr, n), lambda i: (i, 0))
    return pl.pallas_call(
        body, name=name, grid=(R // tr,), in_specs=[spec] * 4, out_specs=[spec] * 3,
        out_shape=[jax.ShapeDtypeStruct((R, n), F32)] * 3, compiler_params=_cparams("parallel"),
    )(w, g, m, v)


def _adamw_half(w, g, m, v, half, prev, name):
    R, n = w.shape
    tr = _row_tile(R // 2, n)
    off = half * (R // 2 // tr)

    def body(*refs):
        w_ref, g_ref, m_ref, v_ref = refs[:4]
        go_ref, d_ref, nm_ref, nv_ref = refs[-4:]
        gv = g_ref[...]
        go_ref[...] = gv
        d_ref[...], nm_ref[...], nv_ref[...] = _adamw_math(w_ref[...], gv, m_ref[...], v_ref[...])

    rows = pl.BlockSpec((tr, n), lambda i: (i + off, 0))
    carried = [] if prev is None else list(prev)
    return pl.pallas_call(
        body, name=name, grid=(R // 2 // tr,),
        in_specs=[rows, pl.BlockSpec((tr, n), lambda i: (i, 0)), rows, rows] + [ANY] * len(carried),
        out_specs=[rows] * 4, out_shape=[jax.ShapeDtypeStruct((R, n), F32)] * 4,
        input_output_aliases={4 + i: i for i in range(len(carried))},
        compiler_params=_cparams("parallel"),
    )(w, g, m, v, *carried)


def _adamw_small(w, gall, m, v, name):
    M, n = w.shape

    def body(w_ref, g_ref, m_ref, v_ref, gs_ref, d_ref, nm_ref, nv_ref):
        g = g_ref[0:M, :]
        for d in range(1, N_DEV):
            g = g + g_ref[d * M:(d + 1) * M, :]
        gs_ref[...] = g
        d_ref[...], nm_ref[...], nv_ref[...] = _adamw_math(w_ref[...], g, m_ref[...], v_ref[...])

    return pl.pallas_call(
        body, name=name, out_shape=[jax.ShapeDtypeStruct((M, n), F32)] * 4,
    )(w, gall, m, v)


SMALL_ROWS = 24
MID_ROWS = 4


def _pack_small(ln1_g, ln1_b, ln2_g, ln2_b, norm_w, sinks, a_log, dt_bias):
    mixed = jnp.concatenate([norm_w.reshape(-1), sinks.reshape(-1), a_log.reshape(-1), dt_bias.reshape(-1)])
    mixed = jnp.pad(mixed, (0, D_MODEL - mixed.shape[0]))[None]
    pad = jnp.zeros((SMALL_ROWS - 4 * DEPTH - 1, D_MODEL), F32)
    return jnp.concatenate([ln1_g, ln1_b, ln2_g, ln2_b, mixed, pad], axis=0)


def _unpack_small(p):
    mixed = p[4 * DEPTH]
    return (p[0:4], p[4:8], p[8:12], p[12:16], mixed[0:256].reshape(2, 128), mixed[256:272].reshape(2, 8),
            mixed[272:280].reshape(2, 4), mixed[280:288].reshape(2, 4))


def _pack_mid(conv_w, rconv_w, rconv_b, b_a, b_x, lam):
    lead = conv_w.shape[0]
    flat = jnp.concatenate([conv_w.reshape(lead, -1), rconv_w.reshape(lead, -1), rconv_b, b_a, b_x, lam], axis=1)
    return jnp.pad(flat, ((0, 0), (0, MID_ROWS * D_MODEL - flat.shape[1]))).reshape(lead, MID_ROWS, D_MODEL)


def _unpack_mid(p):
    lead = p.shape[:-2]
    f = p.reshape(lead + (MID_ROWS * D_MODEL,))
    return (f[..., 0:1536].reshape(lead + (4, 384)), f[..., 1536:2560].reshape(lead + (4, 256)),
            f[..., 2560:2816], f[..., 2816:3072], f[..., 3072:3328], f[..., 3328:3584])


def _cols_from_chips(g):
    p, L, R, n = g.shape
    return g.transpose(1, 2, 0, 3).reshape(L, R, p * n)


def _rows_from_chips(g):
    p, L, R, n = g.shape
    return g.transpose(1, 0, 2, 3).reshape(L, p * R, n)


def _cols_to_chips(g):
    L, R, n4 = g.shape
    return g.reshape(L, R, N_CHIPS, n4 // N_CHIPS).transpose(2, 0, 1, 3)


def _rows_to_chips(g):
    L, R4, n = g.shape
    return g.reshape(L, N_CHIPS, R4 // N_CHIPS, n).transpose(1, 0, 2, 3)


def _halves(a):
    return a.reshape(2, -1, a.shape[-1])


def _pad_hyb_cols(w):
    z = jnp.zeros(w.shape[:-1] + (HYB_PAD - HP_BG - 2 * B_HEADS,), w.dtype)
    return jnp.concatenate([w[..., 0:512], w[..., 768:2304], w[..., 2304:2816], w[..., 512:768], w[..., 2816:2824], z], axis=-1)


def _unpad_hyb_cols(w):
    return jnp.concatenate([w[..., 0:512], w[..., 2560:2816], w[..., 512:2048], w[..., 2048:2560], w[..., 2816:2824]], axis=-1)


def _hybrid_fwd(x, W, j, tables, sfx):
    cos, sin_s = tables
    T = x.shape[0]
    N = T // B_CHUNK
    proj = _matmul(x, W["hyb_w_in"][j], "nn", "hyb_in" + sfx)
    sinks_b = jnp.broadcast_to(W["hyb_sinks"][j][:, None], (A_Q_HEADS, LANES))
    o_a, lse = _attn_fwd(proj, cos, sin_s, sinks_b, "attn_fwd" + sfx)
    qkvc = _dnconv_fwd(proj, W["hyb_conv_w"][j], "dnconv_fwd" + sfx)
    bg = proj[:, HP_BG:HP_BG + 2 * B_HEADS]
    beta = jax.nn.sigmoid(bg[:, :B_HEADS])
    pre = bg[:, B_HEADS:] + W["hyb_dt_bias"][j][None]
    g = -jnp.exp(W["hyb_a_log"][j])[None] * jax.nn.softplus(pre)
    grow = g.T.reshape(B_HEADS, N, 1, B_CHUNK)
    brow = beta.T.reshape(B_HEADS, N, 1, B_CHUNK)
    nw = W["hyb_norm_w"][j][None]
    o_b, states, invs = _dn_fwd(qkvc, proj, nw, grow, brow, "dn_fwd" + sfx)
    mix = jnp.concatenate([o_a, o_b], axis=1).astype(ACT_DTYPE)
    y = _matmul(mix, W["hyb_w_out"][j], "nn", "hyb_out" + sfx)
    res = dict(proj=proj, o_a=o_a, lse=lse, qkvc=qkvc, beta=beta, pre=pre, g=g, grow=grow, brow=brow,
               states=states, invs=invs, mix=mix, sinks_b=sinks_b, nw=nw)
    return y, res


def _hybrid_bwd(x, du, dub, W, j, res, tables, sfx):
    cos, sin_s = tables
    T = x.shape[0]
    proj = res["proj"]
    d_wout = _matmul(res["mix"], dub, "tn", "hyb_out_dw" + sfx)
    dmix = _matmul(dub, W["hyb_w_out"][j], "nt", "hyb_out_dx" + sfx)
    dproj, dkc, dkp, dvc, dvp, dsink = _attn_bwd(proj, cos, sin_s, res["sinks_b"], res["o_a"], res["lse"], dmix,
                                                  "attn_bwd" + sfx)
    zpad = jnp.zeros((WINDOW, LANES), F32)
    dk = dkc + jnp.concatenate([dkp[WINDOW:], zpad], axis=0)
    dv = dvc + jnp.concatenate([dvp[WINDOW:], zpad], axis=0)
    dqkvc, dproj, dg4, dbeta4, dnw = _dn_bwd(res["qkvc"], proj, res["nw"], res["grow"], res["brow"], res["states"],
                                             res["invs"], dmix, dproj, "dn_bwd" + sfx)
    dproj, dconv = _dnconv_bwd(proj, W["hyb_conv_w"][j], dqkvc, dproj, "dnconv_bwd" + sfx)
    dg = dg4.reshape(B_HEADS, T).T
    dbeta = dbeta4.reshape(B_HEADS, T).T
    beta = res["beta"]
    dbeta_logit = dbeta * beta * (1.0 - beta)
    da_logit = dg * (-jnp.exp(W["hyb_a_log"][j]))[None] * jax.nn.sigmoid(res["pre"])
    d_dt_bias = jnp.sum(da_logit, axis=0)
    d_a_log = jnp.sum(dg * res["g"], axis=0)
    zcols = jnp.zeros((T, HYB_PAD - HP_BG - 2 * B_HEADS), F32)
    tail = jnp.concatenate([dk, dv, dbeta_logit, da_logit, zcols], axis=1).astype(ACT_DTYPE)
    dproj = lax.dynamic_update_slice(dproj, tail, (0, HP_K))
    d_win = _matmul(x, dproj, "tn", "hyb_in_dw" + sfx)
    dx = _matmul(dproj, W["hyb_w_in"][j], "nt", "hyb_in_dx" + sfx, epi=_epi_add_residual, extra=du)
    grads = dict(hyb_w_in=d_win, hyb_w_out=d_wout, hyb_sinks=dsink[0], hyb_conv_w=dconv, hyb_a_log=d_a_log,
                 hyb_dt_bias=d_dt_bias, hyb_norm_w=jnp.sum(dnw[:, 0, :], axis=0))
    return dx, grads


def _rec_fwd(x, W, j, sfx):
    proj = _matmul(x, W["rec_w_in"][j], "nn", "rec_in" + sfx)
    sp = jax.nn.softplus(-W["rec_lambda"][j])[None]
    hg, h = _rglru_fwd(proj, W["rec_conv_w"][j], W["rec_conv_b"][j][None], W["rec_w_a"][j], W["rec_w_x"][j],
                       W["rec_b_a"][j][None], W["rec_b_x"][j][None], sp, "rglru_fwd" + sfx)
    y = _matmul(hg, W["rec_w_out"][j], "nn", "rec_out" + sfx)
    return y, dict(proj=proj, hg=hg, h=h, sp=sp)


def _rec_bwd(x, du, dub, W, j, res, sfx):
    d_wout = _matmul(res["hg"], dub, "tn", "rec_out_dw" + sfx)
    dhg = _matmul(dub, W["rec_w_out"][j], "nt", "rec_out_dx" + sfx)
    dproj, dcw, dcb, dwa, dwx, dba, dbx, dsp = _rglru_bwd(
        res["proj"], W["rec_conv_w"][j], W["rec_conv_b"][j][None], W["rec_w_a"][j], W["rec_w_x"][j],
        W["rec_b_a"][j][None], W["rec_b_x"][j][None], res["sp"], res["h"], dhg, "rglru_bwd" + sfx)
    d_lam = dsp[0] * (-jax.nn.sigmoid(-W["rec_lambda"][j]))
    d_win = _matmul(x, dproj, "tn", "rec_in_dw" + sfx)
    dx = _matmul(dproj, W["rec_w_in"][j], "nt", "rec_in_dx" + sfx, epi=_epi_add_residual, extra=du)
    grads = dict(rec_w_in=d_win, rec_w_out=d_wout, rec_conv_w=dcw, rec_conv_b=dcb[0], rec_w_a=dwa, rec_w_x=dwx,
                 rec_b_a=dba[0], rec_b_x=dbx[0], rec_lambda=d_lam)
    return dx, grads


def _local_step(x, tgt, W, mlp_w, mixer_ready, on_group):
    T = x.shape[0]
    tables = _rope_tables(T)
    acts = []
    xb = x.astype(ACT_DTYPE)
    for layer in range(DEPTH):
        j, sfx = layer // 2, ""
        mixer_ready(layer, xb)
        if layer % 2 == 0:
            y, res = _hybrid_fwd(xb, W, j, tables, sfx)
        else:
            y, res = _rec_fwd(xb, W, j, sfx)
        x1, x1b = _ln_fwd(x, y, W["ln1_g"][layer][None], W["ln1_b"][layer][None], "ln_fwd")
        w1, w2, wl = mlp_w(layer, x1b)
        h1 = _matmul(x1b, w1, "nn", "mlp_up", out_dtype=ACT_DTYPE, b_chips=("j", wl))
        y2 = _matmul(h1, w2, "nn", "mlp_down", a_fn=_relu2, b_chips=("k", wl))
        x2, x2b = _ln_fwd(x1, y2, W["ln2_g"][layer][None], W["ln2_b"][layer][None], "ln_fwd")
        acts.append(dict(x=x, xb=xb, y=y, res=res, x1=x1, x1b=x1b, h1=h1, y2=y2))
        x, xb = x2, x2b
    dx, loss = _loss_head(x, tgt, "loss_head")
    per_layer = [None] * DEPTH
    d_w1 = [lax.empty((N_CHIPS, 2, D_MODEL, D_FF // N_CHIPS), F32) for _ in range(DEPTH // 2)]
    d_w2 = [lax.empty((N_CHIPS, 2, D_FF // N_CHIPS, D_MODEL), F32) for _ in range(DEPTH // 2)]
    token = None
    for layer in reversed(range(DEPTH)):
        j, a = layer // 2, acts[layer]
        ln2_g = W["ln2_g"][layer][None]
        if token is not None:
            ln2_g = ln2_g + token
        du2, du2b, dg2, db2 = _ln_bwd(a["x1"], a["y2"], ln2_g, dx, "ln_bwd")
        w1, w2, wl = mlp_w(layer, du2b)
        d_w2[j] = _matmul(a["h1"], du2b, "tn", "mlp_down_dw", a_fn=_relu2, out_chips=("i", layer % 2, d_w2[j]))
        dh1 = _matmul(du2b, w2, "nt", "mlp_down_dx", epi=_epi_drelu2, extra=a["h1"], out_dtype=ACT_DTYPE,
                      b_chips=("j", wl))
        d_w1[j] = _matmul(a["x1b"], dh1, "tn", "mlp_up_dw", out_chips=("j", layer % 2, d_w1[j]))
        dx1 = _matmul(dh1, w1, "nt", "mlp_up_dx", epi=_epi_add_residual, extra=du2, b_chips=("k", wl))
        du1, du1b, dg1, db1 = _ln_bwd(a["x"], a["y"], W["ln1_g"][layer][None], dx1, "ln_bwd")
        if layer % 2 == 0:
            dx, g = _hybrid_bwd(a["xb"], du1, du1b, W, j, a["res"], tables, "")
        else:
            dx, g = _rec_bwd(a["xb"], du1, du1b, W, j, a["res"], "")
        g.update(ln1_g=dg1[0], ln1_b=db1[0], ln2_g=dg2[0], ln2_b=db2[0])
        per_layer[layer] = g
        if layer % 2 == 0:
            token = on_group(j, per_layer[layer], per_layer[layer + 1], d_w1[j], d_w2[j])
    grads = {}
    for name in ("ln1_g", "ln1_b", "ln2_g", "ln2_b"):
        grads[name] = jnp.stack([per_layer[l][name] for l in range(DEPTH)])
    for name in ("hyb_norm_w", "hyb_sinks", "hyb_a_log", "hyb_dt_bias"):
        grads[name] = jnp.stack([per_layer[l][name] for l in (0, 2)])
    return loss, dx, grads


BIG = ("hyb_w_in", "hyb_w_out", "rec_w_in", "rec_w_out", "mlp_w1", "mlp_w2", "rec_w_a", "rec_w_x")
COL_SHARDED = ("hyb_w_in", "rec_w_in", "mlp_w1")
CHIP_MAJOR = ("mlp_w1", "mlp_w2")
MID = ("hyb_conv_w", "rec_conv_w", "rec_conv_b", "rec_b_a", "rec_b_x", "rec_lambda")
SMALL = ("ln1_g", "ln1_b", "ln2_g", "ln2_b", "hyb_norm_w", "hyb_sinks", "hyb_a_log", "hyb_dt_bias")
WEIGHTS = ("hyb_w_in", "hyb_sinks", "hyb_conv_w", "hyb_a_log", "hyb_dt_bias", "hyb_norm_w", "hyb_w_out", "rec_w_in",
           "rec_conv_w", "rec_conv_b", "rec_w_a", "rec_b_a", "rec_w_x", "rec_b_x", "rec_lambda", "rec_w_out", "ln1_g",
           "ln1_b", "mlp_w1", "mlp_w2", "ln2_g", "ln2_b")


def _gather_full_weights(w):
    wb = {k: w[k].astype(MXU_DTYPE) for k in BIG}
    now = ("hyb_w_in", "hyb_w_out", "mlp_w1", "mlp_w2")
    shards = [_halves(wb[k][:1]) for k in now]
    shards.append(_pack_mid(*[w[k] for k in MID]))
    got = _all_gather_weights(shards, "all_gather_weights")
    me = 2 * lax.axis_index("x") + lax.axis_index("y")
    got = [lax.dynamic_update_slice(g, s[None], (me, 0, 0, 0)) for s, g in zip(shards, got)]

    def full(k, g):
        g = g.reshape((N_CHIPS,) + w[k].shape[1:])
        if k in CHIP_MAJOR:
            return g[:, None]
        if k in ("rec_w_a", "rec_w_x"):
            return g.transpose(1, 0, 2, 3).reshape(LRU_BLOCKS, LRU_BLOCK_W, LRU_BLOCK_W)
        f = _cols_from_chips(g[:, None])[0] if k in COL_SHARDED else _rows_from_chips(g[:, None])[0]
        return _pad_hyb_cols(f) if k == "hyb_w_in" else f

    rec = ("rec_w_in", "rec_w_out", "rec_w_a", "rec_w_x")
    groups = [(rec, 0), (CHIP_MAJOR, 1), (("hyb_w_in", "hyb_w_out"), 1), (CHIP_MAJOR, 2), (rec, 1), (CHIP_MAJOR, 3)]
    own = [wb[k][j] for names, j in groups for k in names]
    land = [lax.dynamic_update_slice(lax.empty((N_CHIPS,) + o.shape, o.dtype), o[None], (me,) + (0,) * o.ndim)
            for o in own]
    send_sems, recv_sems, own, land = _gather_start(own, land, got[-1], "gather_start")
    W = {k: [None] * w[k].shape[0] for k in BIG}
    for k, g in zip(now, got[:-1]):
        W[k][0] = full(k, g)
    arrived = [0]

    def ensure(upto, after):
        while arrived[0] <= upto:
            gi = arrived[0]
            names, j = groups[gi]
            lo = sum(len(nm) for nm, _ in groups[:gi])
            sl = slice(lo, lo + len(names))
            got_g = _gather_wait(send_sems[sl], recv_sems[sl], own[sl], land[sl], after, "gather_wait_%d" % gi)
            for k, g in zip(names, got_g):
                W[k][j] = full(k, g)
            arrived[0] += 1

    def mixer_ready(layer, after):
        if layer:
            ensure({1: 0, 2: 2, 3: 4}[layer], after)

    def mlp_w(layer, after):
        if layer:
            ensure({1: 1, 2: 3, 3: 5}[layer], after)
        return W["mlp_w1"][layer], W["mlp_w2"][layer], 0

    conv_w, rconv_w, rconv_b, b_a, b_x, lam = _unpack_mid(got[-1])
    W["hyb_conv_w"] = conv_w.transpose(1, 2, 0, 3).reshape(2, CONV_K, 3 * B_W)
    W["rec_conv_w"] = rconv_w.transpose(1, 2, 0, 3).reshape(2, CONV_K, D_MODEL)
    for k, v in (("rec_conv_b", rconv_b), ("rec_b_a", b_a), ("rec_b_x", b_x), ("rec_lambda", lam)):
        W[k] = v.transpose(1, 0, 2).reshape(2, D_MODEL)
    for k in SMALL:
        W[k] = w[k]
    return W, mlp_w, mixer_ready


def _group_by_chip(gh, gr, d_w1, d_w2):
    g = dict(gh, **gr)
    g["hyb_w_in"] = _unpad_hyb_cols(g["hyb_w_in"])
    out = []
    for k in BIG:
        if k == "mlp_w1":
            v = d_w1
        elif k == "mlp_w2":
            v = d_w2
        elif k in ("rec_w_a", "rec_w_x"):
            v = g[k].reshape(1, LRU_BLOCKS, N_CHIPS, LRU_BLOCK_W // N_CHIPS, LRU_BLOCK_W).transpose(2, 0, 1, 3, 4)
        else:
            v = _cols_to_chips(g[k][None]) if k in COL_SHARDED else _rows_to_chips(g[k][None])
        out.append(v.reshape(N_CHIPS, 2, -1, v.shape[-1]))
    conv_w = g["hyb_conv_w"].reshape(CONV_K, N_CHIPS, -1).transpose(1, 0, 2)
    rconv_w = g["rec_conv_w"].reshape(CONV_K, N_CHIPS, -1).transpose(1, 0, 2)
    vecs = [g[k].reshape(N_CHIPS, -1) for k in ("rec_conv_b", "rec_b_a", "rec_b_x", "rec_lambda")]
    out.append(_pack_mid(conv_w, rconv_w, *vecs).reshape(N_CHIPS, 2, MID_ROWS // 2, D_MODEL))
    return out


def kernel(x, hyb_w_in, hyb_sinks, hyb_conv_w, hyb_a_log, hyb_dt_bias, hyb_norm_w, hyb_w_out, rec_w_in, rec_conv_w, rec_conv_b, rec_w_a, rec_b_a, rec_w_x, rec_b_x, rec_lambda, rec_w_out, ln1_g, ln1_b, mlp_w1, mlp_w2, ln2_g, ln2_b, loss_target, m_hyb_w_in, m_hyb_sinks, m_hyb_conv_w, m_hyb_a_log, m_hyb_dt_bias, m_hyb_norm_w, m_hyb_w_out, m_rec_w_in, m_rec_conv_w, m_rec_conv_b, m_rec_w_a, m_rec_b_a, m_rec_w_x, m_rec_b_x, m_rec_lambda, m_rec_w_out, m_ln1_g, m_ln1_b, m_mlp_w1, m_mlp_w2, m_ln2_g, m_ln2_b, v_hyb_w_in, v_hyb_sinks, v_hyb_conv_w, v_hyb_a_log, v_hyb_dt_bias, v_hyb_norm_w, v_hyb_w_out, v_rec_w_in, v_rec_conv_w, v_rec_conv_b, v_rec_w_a, v_rec_b_a, v_rec_w_x, v_rec_b_x, v_rec_lambda, v_rec_w_out, v_ln1_g, v_ln1_b, v_mlp_w1, v_mlp_w2, v_ln2_g, v_ln2_b):
    args = locals()
    w = {k: args[k] for k in WEIGHTS}
    m = {k: args["m_" + k] for k in WEIGHTS}
    v = {k: args["v_" + k] for k in WEIGHTS}

    W, mlp_w, mixer_ready = _gather_full_weights(w)

    core = lax.axis_index("c").astype(jnp.int32)
    me = (2 * lax.axis_index("x") + lax.axis_index("y")).astype(jnp.int32)
    slots = jnp.arange(N_CHIPS, dtype=jnp.int32)
    where = jnp.concatenate([me[None], jnp.where(slots == me, (slots + 1) % N_CHIPS, slots), core[None]])
    state = {}

    def on_group(group, gh, gr, d_w1, d_w2):
        by_chip = _group_by_chip(gh, gr, d_w1, d_w2)
        from_sibling = _rs_to_sibling(by_chip, "rs_to_sibling")
        pair = [_pair_sum(g, r, core[None], "pair_sum") for g, r in zip(by_chip, from_sibling)]
        recv = [lax.empty(p.shape, p.dtype) for p in pair]
        state[group] = _rs_across_start(pair, recv, "rs_across_start_%d" % group)
        return state[group][4][0, 0]

    def finish_group(group, after, prev):
        send_sems, recv_sems, pair, recv, _ = state[group]
        pair, from_chips = _rs_across_wait(send_sems, recv_sems, pair, recv, after, "rs_across_wait_%d" % group)
        half = [_chip_sum(r, p, where, "chip_sum") for r, p in zip(from_chips, pair)]
        joined = _rs_join_halves(half, "rs_join_halves")
        outs = {}
        for k, g in zip(BIG, joined[:-1]):
            n = g.shape[-1]
            outs[k] = _adamw_half(w[k].reshape(-1, n), g.reshape(-1, n), m[k].reshape(-1, n), v[k].reshape(-1, n),
                                  group, None if prev is None else prev[k], "adamw")
        return outs, joined[-1]

    loss, dx, grads = _local_step(x[0], loss_target[0], W, mlp_w, mixer_ready, on_group)
    loss = lax.psum(loss[0, 0], ("x", "y", "c"))
    g_out, d_out, m_out, v_out = {}, {}, {}, {}

    upper, mid_upper = finish_group(1, state[0][4], None)
    small_g = _pack_small(*[grads[k] for k in SMALL])
    small_all = _all_gather_small(small_g, "all_gather_small")
    sw, sm, sv = (_pack_small(*[t[k] for k in SMALL]) for t in (w, m, v))
    sg, sd, snm, snv = _adamw_small(sw, small_all, sm, sv, "adamw_small")
    for dst, packed in ((g_out, sg), (d_out, sd), (m_out, snm), (v_out, snv)):
        for k, val in zip(SMALL, _unpack_small(packed)):
            dst[k] = val

    done = sum([upper[k][1][0, 0] for k in BIG], sg[0, 0]).reshape(1, 1)
    both, mid_lower = finish_group(0, done, upper)
    for k in BIG:
        shape = w[k].shape
        g_out[k], d_out[k], m_out[k], v_out[k] = (t.reshape(shape) for t in both[k])
    mid_w, mid_m, mid_v = (_pack_mid(*[t[k] for k in MID]).reshape(-1, D_MODEL) for t in (w, m, v))
    mid_g = jnp.concatenate([mid_lower.reshape(-1, D_MODEL), mid_upper.reshape(-1, D_MODEL)])
    mid_d, mid_nm, mid_nv = _adamw(mid_w, mid_g, mid_m, mid_v, "adamw_mid")
    for dst, packed in ((g_out, mid_g), (d_out, mid_d), (m_out, mid_nm), (v_out, mid_nv)):
        for k, val in zip(MID, _unpack_mid(packed.reshape(2, MID_ROWS, D_MODEL))):
            dst[k] = val.reshape(w[k].shape)

    return (loss, dx[None], *[g_out[k] for k in WEIGHTS], *[d_out[k] for k in WEIGHTS],
            *[m_out[k] for k in WEIGHTS], *[v_out[k] for k in WEIGHTS])
```

```python
import functools

import jax
import jax.numpy as jnp
import numpy as np
from jax import lax
from jax.experimental import pallas as pl
from jax.experimental.pallas import tpu as pltpu

F32 = jnp.float32
MXU_DTYPE = jnp.bfloat16
ACT_DTYPE = jnp.bfloat16
ICI_DTYPE = jnp.bfloat16

D_MODEL = 1024
DEPTH = 4
A_HEAD_DIM = 64
A_Q_HEADS = 8
A_KV_HEADS = 2
WINDOW = 128
ROPE_THETA = 10000.0
B_HEADS = 4
B_HEAD_DIM = 128
B_CHUNK = 64
CONV_K = 4
LRU_BLOCKS = 4
LRU_BLOCK_W = D_MODEL // LRU_BLOCKS
LRU_C = 8.0
D_FF = 4 * D_MODEL
A_Q_W = A_Q_HEADS * A_HEAD_DIM
A_KV_W = A_KV_HEADS * A_HEAD_DIM
B_W = B_HEADS * B_HEAD_DIM
HYB_PROJ = A_Q_W + 2 * A_KV_W + 4 * B_W + 2 * B_HEADS
DN_ALPHA = (2 * DEPTH) ** 0.25
LN_EPS = 1e-5
NORM_EPS = 1e-6
ADAM_LR = 0.001
ADAM_B1 = 0.9
ADAM_B2 = 0.999
ADAM_EPS = 1e-08
ADAM_WD = 0.01
ADAM_STEP = 10

HP_Q = 0
HP_QKVB = 512
HP_Z = 2048
HP_K = 2560
HP_V = 2688
HP_BG = 2816
HYB_PAD = 3072

N_CHIPS = 4
N_DEV = 8
V7X_VMEM_LIMIT = 48 * 1024 * 1024
LANES = 128
SUBLANES = 8
NEG_BIG = -1e30

NN = (((1,), (0,)), ((), ()))
NT = (((1,), (1,)), ((), ()))
TN = (((0,), (0,)), ((), ()))


def _cparams(*sem):
    return pltpu.CompilerParams(dimension_semantics=sem, vmem_limit_bytes=V7X_VMEM_LIMIT)


def _dot(a, b, dims=NN):
    return lax.dot_general(a.astype(MXU_DTYPE), b.astype(MXU_DTYPE), dims, preferred_element_type=F32)


def _split_bf16(a):
    hi = a.astype(jnp.bfloat16)
    return hi, (a - hi.astype(F32)).astype(jnp.bfloat16)


def _dotf(a, b, dims=NN):
    ah, al = _split_bf16(a)
    bh, bl = _split_bf16(b)
    dg = functools.partial(lax.dot_general, dimension_numbers=dims, preferred_element_type=F32)
    return dg(ah, bh) + (dg(ah, bl) + dg(al, bh))


def _tile(dim, pref):
    t = min(dim, pref)
    while dim % t:
        t //= 2
    return t


def _sigmoid(x):
    return 1.0 / (1.0 + jnp.exp(-x))


def _silu(x):
    return x * _sigmoid(x)


def _dsilu(x):
    s = _sigmoid(x)
    return s * (1.0 + x * (1.0 - s))


GELU_C = 0.7978845608028654
GELU_A = 0.044715


def _gelu(x):
    return 0.5 * x * (1.0 + jnp.tanh(GELU_C * (x + GELU_A * x * x * x)))


def _dgelu(x):
    t = jnp.tanh(GELU_C * (x + GELU_A * x * x * x))
    return 0.5 * (1.0 + t) + 0.5 * x * (1.0 - t * t) * GELU_C * (1.0 + 3.0 * GELU_A * x * x)


def _matmul(a, b, mode, name, *, tm=1024, tn=1024, tk=1024, a_fn=None, epi=None, extra=None, out_dtype=F32,
            b_chips=None, out_chips=None):
    chunk = tk
    if mode == "tn":
        K, M = a.shape
        if K <= 4 * tk:
            tk, tm = K, tm // 2
    else:
        M, K = a.shape
    whole_k = False
    if b_chips is not None:
        g, b_layer = b_chips
        r, n = b.shape[2:]
        n_dim, k_dim = (r, n) if mode == "nt" else (n, r)
        N = N_CHIPS * n_dim if g == "j" else n_dim
        assert K == (N_CHIPS * k_dim if g == "k" else k_dim)
        if g == "j":
            tn = n_dim
        else:
            whole_k, tk, tm = True, K, tm // 2
    elif mode == "nt":
        N = b.shape[0]
        if tk < K <= 3 * tk:
            tk, tm = K, tm // 2
    else:
        N = b.shape[1]
    if out_chips is not None:
        og, o_layer, o_buf = out_chips
        if og == "j":
            tn = o_buf.shape[3]
        else:
            tm = min(tm, o_buf.shape[2])
    tm, tn, tk = _tile(M, tm), _tile(N, tn), _tile(K, tk)
    nk = K // tk
    if mode == "tn":
        a_spec = pl.BlockSpec((tk, tm), lambda i, j, k: (k, i))
    else:
        a_spec = pl.BlockSpec((tm, tk), lambda i, j, k: (i, k))
    b_block = (tn, tk) if mode == "nt" else (tk, tn)
    if b_chips is None:
        b_spec = pl.BlockSpec(b_block, (lambda i, j, k: (j, k)) if mode == "nt" else (lambda i, j, k: (k, j)))
    elif mode == "nt" and whole_k:
        b_spec = pl.BlockSpec((N_CHIPS, None, tn, k_dim), lambda i, j, k: (0, b_layer, j, 0))
    elif mode == "nt":
        b_spec = pl.BlockSpec((None, None) + b_block, lambda i, j, k: (j, b_layer, 0, k))
    elif whole_k:
        b_spec = pl.BlockSpec((N_CHIPS, None, k_dim, tn), lambda i, j, k: (0, b_layer, 0, j))
    else:
        b_spec = pl.BlockSpec((None, None) + b_block, lambda i, j, k: (j, b_layer, k, 0))
    o_spec = pl.BlockSpec((tm, tn), lambda i, j, k: (i, j))
    e_spec = o_spec
    if out_chips is not None:
        per = o_buf.shape[2] // tm
        o_spec = pl.BlockSpec((None, None, tm, tn), (lambda i, j, k: (j, o_layer, i, 0)) if og == "j"
                              else (lambda i, j, k: (i // per, o_layer, i % per, j)))
    dims = {"nn": NN, "nt": NT, "tn": TN}[mode]
    has_extra = extra is not None
    n_in = 2 + has_extra + (out_chips is not None)

    def body(*refs):
        a_ref, b_ref = refs[0], refs[1]
        e_ref = refs[2] if has_extra else None
        o_ref = refs[n_in]
        av = a_ref[...]
        if a_fn is not None:
            av = a_fn(av)
        if whole_k and mode == "nt":
            part = _dot(av[:, :k_dim], b_ref[0], dims)
            for chip in range(1, N_CHIPS):
                part = part + _dot(av[:, chip * k_dim:(chip + 1) * k_dim], b_ref[chip], dims)
        elif mode == "tn" and tk > chunk:
            part = _dot(av[:chunk], b_ref[0:chunk, :], dims)
            for c0 in range(chunk, tk, chunk):
                part = part + _dot(av[c0:c0 + chunk], b_ref[c0:c0 + chunk, :], dims)
        else:
            bv = b_ref[...]
            if whole_k:
                bv = bv.reshape(K, tn)
            part = _dot(av, bv, dims)

        def finish(acc):
            if epi is not None:
                acc = epi(acc, e_ref[...])
            o_ref[...] = acc.astype(out_dtype)

        if nk == 1:
            finish(part)
        else:
            acc_ref = refs[-1]
            k = pl.program_id(2)

            @pl.when(k == 0)
            def _():
                acc_ref[...] = part

            @pl.when(k > 0)
            def _():
                acc_ref[...] += part

            @pl.when(k == nk - 1)
            def _():
                finish(acc_ref[...])

    in_specs = [a_spec, b_spec] + ([e_spec] if has_extra else [])
    args = (a, b) + ((extra,) if has_extra else ())
    out_shape = jax.ShapeDtypeStruct((M, N), out_dtype)
    aliases = {}
    if out_chips is not None:
        in_specs.append(pl.BlockSpec(memory_space=pl.ANY))
        args += (o_buf,)
        out_shape = jax.ShapeDtypeStruct(o_buf.shape, o_buf.dtype)
        aliases = {n_in - 1: 0}
    return pl.pallas_call(
        body, name=name, grid=(M // tm, N // tn, nk), in_specs=in_specs, out_specs=o_spec, out_shape=out_shape,
        input_output_aliases=aliases,
        scratch_shapes=[pltpu.VMEM((tm, tn), F32)] if nk > 1 else [],
        compiler_params=_cparams("parallel", "parallel", "arbitrary"),
    )(*args)


def _relu2(v):
    r = jnp.maximum(v, 0.0)
    return r * r


def _epi_drelu2(acc, h):
    return acc * (2.0 * jnp.maximum(h, 0.0))


def _epi_add_residual(acc, du):
    return acc + DN_ALPHA * du


def _ln_fwd(x, y, g, b, name):
    T, D = x.shape
    tr = _tile(T, 512)

    def body(x_ref, y_ref, g_ref, b_ref, o_ref, ob_ref):
        u = DN_ALPHA * x_ref[...] + y_ref[...]
        mu = jnp.mean(u, axis=-1, keepdims=True)
        d = u - mu
        var = jnp.mean(d * d, axis=-1, keepdims=True)
        o = d * lax.rsqrt(var + LN_EPS) * g_ref[...] + b_ref[...]
        o_ref[...] = o
        ob_ref[...] = o.astype(ACT_DTYPE)

    row = pl.BlockSpec((tr, D), lambda i: (i, 0))
    vec = pl.BlockSpec((1, D), lambda i: (0, 0))
    return pl.pallas_call(
        body, name=name, grid=(T // tr,), in_specs=[row, row, vec, vec], out_specs=[row, row],
        out_shape=[jax.ShapeDtypeStruct((T, D), F32), jax.ShapeDtypeStruct((T, D), ACT_DTYPE)],
        compiler_params=_cparams("parallel"),
    )(x, y, g, b)


def _ln_bwd(x, y, g, dout, name):
    T, D = x.shape
    tr = _tile(T, 512)

    def body(x_ref, y_ref, g_ref, d_ref, du_ref, dub_ref, dg_ref, db_ref):
        i = pl.program_id(0)
        u = DN_ALPHA * x_ref[...] + y_ref[...]
        mu = jnp.mean(u, axis=-1, keepdims=True)
        d = u - mu
        rstd = lax.rsqrt(jnp.mean(d * d, axis=-1, keepdims=True) + LN_EPS)
        xhat = d * rstd
        dout_v = d_ref[...]
        dxh = dout_v * g_ref[...]
        m1 = jnp.mean(dxh, axis=-1, keepdims=True)
        m2 = jnp.mean(dxh * xhat, axis=-1, keepdims=True)
        du = rstd * (dxh - m1 - xhat * m2)
        du_ref[...] = du
        dub_ref[...] = du.astype(ACT_DTYPE)
        pg = jnp.sum(dout_v * xhat, axis=0, keepdims=True)
        pb = jnp.sum(dout_v, axis=0, keepdims=True)

        @pl.when(i == 0)
        def _():
            dg_ref[...] = pg
            db_ref[...] = pb

        @pl.when(i > 0)
        def _():
            dg_ref[...] += pg
            db_ref[...] += pb

    row = pl.BlockSpec((tr, D), lambda i: (i, 0))
    vec = pl.BlockSpec((1, D), lambda i: (0, 0))
    return pl.pallas_call(
        body, name=name, grid=(T // tr,), in_specs=[row, row, vec, row], out_specs=[row, row, vec, vec],
        out_shape=[jax.ShapeDtypeStruct((T, D), F32), jax.ShapeDtypeStruct((T, D), ACT_DTYPE),
                   jax.ShapeDtypeStruct((1, D), F32), jax.ShapeDtypeStruct((1, D), F32)],
        compiler_params=_cparams("arbitrary"),
    )(x, y, g, dout)


def _loss_head(y, tgt, name):
    T, D = y.shape
    tr = _tile(T, 512)

    def body(y_ref, t_ref, dy_ref, l_ref):
        i = pl.program_id(0)
        e = y_ref[...] - t_ref[...]
        dy_ref[...] = e * (1.0 / D)
        part = jnp.sum(e * e, axis=(0, 1), keepdims=True) * (0.5 / D)

        @pl.when(i == 0)
        def _():
            l_ref[...] = part

        @pl.when(i > 0)
        def _():
            l_ref[...] += part

    row = pl.BlockSpec((tr, D), lambda i: (i, 0))
    one = pl.BlockSpec((1, 1), lambda i: (0, 0))
    return pl.pallas_call(
        body, name=name, grid=(T // tr,), in_specs=[row, row], out_specs=[row, one],
        out_shape=[jax.ShapeDtypeStruct((T, D), F32), jax.ShapeDtypeStruct((1, 1), F32)],
        compiler_params=_cparams("arbitrary"),
    )(y, tgt)


def _swap_half(x):
    n = x.shape[-1]
    lane = lax.broadcasted_iota(jnp.int32, x.shape, 1)
    first = (lane % A_HEAD_DIM) < (A_HEAD_DIM // 2)
    return jnp.where(first, pltpu.roll(x, n - A_HEAD_DIM // 2, axis=1), pltpu.roll(x, A_HEAD_DIM // 2, axis=1))


def _rope(x, cos, sin_signed):
    return x * cos + _swap_half(x) * sin_signed


def _rope_t(dy, cos, sin_signed):
    return dy * cos + _swap_half(dy * sin_signed)


def _rope_tables(T):
    half = A_HEAD_DIM // 2
    inv_freq = np.float32(ROPE_THETA) ** (-np.arange(half, dtype=np.float32) / np.float32(half))
    ang = np.arange(T, dtype=np.float32)[:, None] * inv_freq[None, :]
    cos = np.tile(np.cos(ang), (1, 4))
    sin = np.sin(ang)
    sin_signed = np.tile(np.concatenate([-sin, sin], axis=1), (1, 2))
    return jnp.asarray(cos, F32), jnp.asarray(sin_signed, F32)


def _band_mask(n):
    qi = lax.broadcasted_iota(jnp.int32, (WINDOW, 2 * WINDOW), 0)
    kj = lax.broadcasted_iota(jnp.int32, (WINDOW, 2 * WINDOW), 1)
    return (kj > qi) & (kj <= qi + WINDOW) & ((n > 0) | (kj >= WINDOW))


def _place(v, src_half, dst_half):
    lane = lax.broadcasted_iota(jnp.int32, v.shape, 1)
    if src_half != dst_half:
        v = pltpu.roll(v, A_HEAD_DIM, axis=1)
    keep = (lane >= A_HEAD_DIM) if dst_half else (lane < A_HEAD_DIM)
    return jnp.where(keep, v, 0.0)


def _attn_specs():
    kb, vb = HP_K // LANES, HP_V // LANES
    prev = lambda n: jnp.maximum(n - 1, 0)
    return dict(
        q=pl.BlockSpec((WINDOW, A_Q_W), lambda n: (n, 0)),
        kc=pl.BlockSpec((WINDOW, LANES), lambda n: (n, kb)),
        kp=pl.BlockSpec((WINDOW, LANES), lambda n: (prev(n), kb)),
        vc=pl.BlockSpec((WINDOW, LANES), lambda n: (n, vb)),
        vp=pl.BlockSpec((WINDOW, LANES), lambda n: (prev(n), vb)),
        tq=pl.BlockSpec((WINDOW, LANES), lambda n: (n, 0)),
        tp=pl.BlockSpec((WINDOW, LANES), lambda n: (prev(n), 0)),
        sink=pl.BlockSpec((A_Q_HEADS, LANES), lambda n: (0, 0)),
        row512=pl.BlockSpec((WINDOW, A_Q_W), lambda n: (n, 0)),
        row128=pl.BlockSpec((WINDOW, LANES), lambda n: (n, 0)),
        lse=pl.BlockSpec((WINDOW, A_Q_HEADS), lambda n: (n, 0)),
    )


def _attn_fwd(proj, cos, sin_s, sinks_b, name):
    T = proj.shape[0]
    sp = _attn_specs()

    def body(q_ref, kc_ref, kp_ref, vc_ref, vp_ref, cq_ref, sq_ref, cp_ref, sp_ref, sink_ref, o_ref, l_ref):
        n = pl.program_id(0)
        cq, sq = cq_ref[...], sq_ref[...]
        cq4, sq4 = jnp.tile(cq, (1, A_Q_W // LANES)), jnp.tile(sq, (1, A_Q_W // LANES))
        q = _rope(q_ref[...], cq4, sq4) * (A_HEAD_DIM ** -0.5)
        kc = _rope(kc_ref[...], cq, sq)
        kp = _rope(kp_ref[...], cp_ref[...], sp_ref[...])
        kk = jnp.concatenate([kp, kc], axis=0)
        vv = jnp.concatenate([vp_ref[...], vc_ref[...]], axis=0)
        mask = _band_mask(n)[None]
        lane = lax.broadcasted_iota(jnp.int32, (WINDOW, LANES), 1)
        lane8 = lax.broadcasted_iota(jnp.int32, (WINDOW, A_Q_HEADS), 1)
        qe = jnp.stack([_place(q[:, (hq // 2) * LANES:(hq // 2 + 1) * LANES], hq % 2, hq // 4) for hq in range(A_Q_HEADS)])
        kk8 = jnp.broadcast_to(kk[None], (A_Q_HEADS,) + kk.shape)
        vv8 = jnp.broadcast_to(vv[None], (A_Q_HEADS,) + vv.shape)
        sk = jnp.stack([sink_ref[hq:hq + 1, 0:1] for hq in range(A_Q_HEADS)])
        s = jnp.where(mask, _dot(qe, kk8, BNT), NEG_BIG)
        m = jnp.maximum(jnp.max(s, axis=-1, keepdims=True), sk)
        p = jnp.exp(s - m)
        den = jnp.sum(p, axis=-1, keepdims=True) + jnp.exp(sk - m)
        o = _dot(p * (1.0 / den), vv8, BNN)
        lse_h = m + jnp.log(den)
        outs = []
        lse = jnp.zeros((WINDOW, A_Q_HEADS), F32)
        for pb in range(A_Q_HEADS // 2):
            halves = [_place(o[2 * pb + e], pb // 2, e) for e in range(2)]
            outs.append(jnp.where(lane < A_HEAD_DIM, halves[0], halves[1]))
            for e in range(2):
                lse = jnp.where(lane8 == 2 * pb + e, lse_h[2 * pb + e], lse)
        o_ref[...] = jnp.concatenate(outs, axis=1)
        l_ref[...] = lse

    return pl.pallas_call(
        body, name=name, grid=(T // WINDOW,),
        in_specs=[sp["q"], sp["kc"], sp["kp"], sp["vc"], sp["vp"], sp["tq"], sp["tq"], sp["tp"], sp["tp"], sp["sink"]],
        out_specs=[sp["row512"], sp["lse"]],
        out_shape=[jax.ShapeDtypeStruct((T, A_Q_W), F32), jax.ShapeDtypeStruct((T, A_Q_HEADS), F32)],
        compiler_params=_cparams("parallel"),
    )(proj, proj, proj, proj, proj, cos, sin_s, cos, sin_s, sinks_b)


def _attn_bwd(proj, cos, sin_s, sinks_b, o, lse, dmix, name):
    T = proj.shape[0]
    sp = _attn_specs()

    def body(q_ref, kc_ref, kp_ref, vc_ref, vp_ref, cq_ref, sq_ref, cp_ref, sp_ref, sink_ref, o_ref, l_ref, do_ref,
             dq_ref, dkc_ref, dkp_ref, dvc_ref, dvp_ref, dsink_ref):
        n = pl.program_id(0)
        cq, sq = cq_ref[...], sq_ref[...]
        cp, sps = cp_ref[...], sp_ref[...]
        cq4, sq4 = jnp.tile(cq, (1, A_Q_W // LANES)), jnp.tile(sq, (1, A_Q_W // LANES))
        q = _rope(q_ref[...], cq4, sq4) * (A_HEAD_DIM ** -0.5)
        kc = _rope(kc_ref[...], cq, sq)
        kp = _rope(kp_ref[...], cp, sps)
        kk = jnp.concatenate([kp, kc], axis=0)
        vv = jnp.concatenate([vp_ref[...], vc_ref[...]], axis=0)
        mask = _band_mask(n)[None]
        lane = lax.broadcasted_iota(jnp.int32, (WINDOW, LANES), 1)
        do_all, o_all, l_all = do_ref[...], o_ref[...], l_ref[...]
        lane8 = lax.broadcasted_iota(jnp.int32, (WINDOW, A_Q_HEADS), 1)
        head8 = lax.broadcasted_iota(jnp.int32, (1, A_Q_HEADS), 1)
        prod = do_all * o_all
        qes, does, deltas, lhs = [], [], [], []
        for hq in range(A_Q_HEADS):
            pb, e, kvh = hq // 2, hq % 2, hq // 4
            blk = slice(pb * LANES, (pb + 1) * LANES)
            in_half = (lane >= A_HEAD_DIM) if e else (lane < A_HEAD_DIM)
            deltas.append(jnp.sum(jnp.where(in_half, prod[:, blk], 0.0), axis=-1, keepdims=True))
            qes.append(_place(q[:, blk], e, kvh))
            does.append(_place(do_all[:, blk], e, kvh))
            lhs.append(jnp.sum(jnp.where(lane8 == hq, l_all, 0.0), axis=-1, keepdims=True))
        qe, doe, delta, lh = jnp.stack(qes), jnp.stack(does), jnp.stack(deltas), jnp.stack(lhs)
        kk8 = jnp.broadcast_to(kk[None], (A_Q_HEADS,) + kk.shape)
        vv8 = jnp.broadcast_to(vv[None], (A_Q_HEADS,) + vv.shape)
        sk = jnp.stack([sink_ref[hq:hq + 1, 0:1] for hq in range(A_Q_HEADS)])
        s = _dot(qe, kk8, BNT)
        p = jnp.where(mask, jnp.exp(jnp.where(mask, s, NEG_BIG) - lh), 0.0)
        dvv = jnp.sum(_dot(p, doe, BTN), axis=0)
        ds = p * (_dot(doe, vv8, BNT) - delta)
        dkk = jnp.sum(_dot(ds, qe, BTN), axis=0)
        dqe = _dot(ds, kk8, BNN)
        dsink_h = -jnp.sum(jnp.exp(sk - lh) * delta, axis=(1, 2), keepdims=True)
        dqs = []
        dsk = jnp.zeros((1, A_Q_HEADS), F32)
        for pb in range(A_Q_HEADS // 2):
            halves = [_place(dqe[2 * pb + e], pb // 2, e) for e in range(2)]
            dqs.append(jnp.where(lane < A_HEAD_DIM, halves[0], halves[1]))
            for e in range(2):
                dsk = jnp.where(head8 == 2 * pb + e, dsink_h[2 * pb + e], dsk)
        dq = jnp.concatenate(dqs, axis=1) * (A_HEAD_DIM ** -0.5)
        dq_ref[...] = _rope_t(dq, cq4, sq4).astype(ACT_DTYPE)
        dkp_ref[...] = _rope_t(dkk[:WINDOW], cp, sps)
        dkc_ref[...] = _rope_t(dkk[WINDOW:], cq, sq)
        dvp_ref[...] = dvv[:WINDOW]
        dvc_ref[...] = dvv[WINDOW:]

        @pl.when(n == 0)
        def _():
            dsink_ref[...] = dsk

        @pl.when(n > 0)
        def _():
            dsink_ref[...] += dsk

    return pl.pallas_call(
        body, name=name, grid=(T // WINDOW,),
        in_specs=[sp["q"], sp["kc"], sp["kp"], sp["vc"], sp["vp"], sp["tq"], sp["tq"], sp["tp"], sp["tp"], sp["sink"],
                  sp["row512"], sp["lse"], sp["row512"]],
        out_specs=[sp["row512"], sp["row128"], sp["row128"], sp["row128"], sp["row128"],
                   pl.BlockSpec((1, A_Q_HEADS), lambda n: (0, 0))],
        out_shape=[jax.ShapeDtypeStruct((T, HYB_PAD), ACT_DTYPE)] + [jax.ShapeDtypeStruct((T, LANES), F32)] * 4
        + [jax.ShapeDtypeStruct((1, A_Q_HEADS), F32)],
        compiler_params=_cparams("arbitrary"),
    )(proj, proj, proj, proj, proj, cos, sin_s, cos, sin_s, sinks_b, o, lse, dmix)


def _shift_down(x, prev8, k):
    if k == 0:
        return x
    row = lax.broadcasted_iota(jnp.int32, prev8.shape, 0)
    r = pltpu.roll(x, k, axis=0)
    top = jnp.where(row < k, pltpu.roll(prev8, k, axis=0), r[:SUBLANES])
    return jnp.concatenate([top, r[SUBLANES:]], axis=0)


def _shift_up(x, next8, k):
    if k == 0:
        return x
    R = x.shape[0]
    row = lax.broadcasted_iota(jnp.int32, next8.shape, 0)
    r = pltpu.roll(x, R - k, axis=0)
    bot = jnp.where(row >= SUBLANES - k, pltpu.roll(next8, SUBLANES - k, axis=0), r[R - SUBLANES:])
    return jnp.concatenate([r[:R - SUBLANES], bot], axis=0)


def _conv(x, prev8, w):
    y = x * w[CONV_K - 1:CONV_K]
    for j in range(CONV_K - 1):
        y = y + _shift_down(x, prev8, CONV_K - 1 - j) * w[j:j + 1]
    return y


def _conv_bwd(x, prev8, w, dy, next8_dy):
    dx = dy * w[CONV_K - 1:CONV_K]
    dws = []
    for j in range(CONV_K - 1):
        k = CONV_K - 1 - j
        dx = dx + _shift_up(dy, next8_dy, k) * w[j:j + 1]
        dws.append(jnp.sum(dy * _shift_down(x, prev8, k), axis=0, keepdims=True))
    dws.append(jnp.sum(dy * x, axis=0, keepdims=True))
    return dx, dws


def _dnconv_fwd(proj, conv_w, name):
    T = proj.shape[0]
    R = _tile(T, 512)
    cb0 = HP_QKVB // A_Q_W

    def body(x_ref, w_ref, o_ref, prev_ref):
        i = pl.program_id(1)

        @pl.when(i == 0)
        def _():
            prev_ref[...] = jnp.zeros_like(prev_ref)

        x = x_ref[...]
        o_ref[...] = _silu(_conv(x, prev_ref[...], w_ref[...]))
        prev_ref[...] = x[R - SUBLANES:]

    return pl.pallas_call(
        body, name=name, grid=(3, T // R),
        in_specs=[pl.BlockSpec((R, B_W), lambda j, i: (i, cb0 + j)), pl.BlockSpec((CONV_K, B_W), lambda j, i: (0, j))],
        out_specs=pl.BlockSpec((R, B_W), lambda j, i: (i, j)),
        out_shape=jax.ShapeDtypeStruct((T, 3 * B_W), F32),
        scratch_shapes=[pltpu.VMEM((SUBLANES, B_W), F32)],
        compiler_params=_cparams("parallel", "arbitrary"),
    )(proj, conv_w)


def _dnconv_bwd(proj, conv_w, dy, dproj, name):
    T = proj.shape[0]
    R = _tile(T, 512)
    nb = T // R
    cb0 = HP_QKVB // A_Q_W
    r8 = R // SUBLANES

    def body(x_ref, xp_ref, w_ref, dy_ref, dproj_in, dx_ref, dw_ref, next_ref):
        i = pl.program_id(1)
        blk = nb - 1 - i

        @pl.when(i == 0)
        def _():
            next_ref[...] = jnp.zeros_like(next_ref)

        x = x_ref[...]
        prev8 = jnp.where(blk > 0, xp_ref[...], 0.0)
        w = w_ref[...]
        dpre = dy_ref[...] * _dsilu(_conv(x, prev8, w))
        dx, dw = _conv_bwd(x, prev8, w, dpre, next_ref[...])
        dx_ref[...] = dx.astype(ACT_DTYPE)
        next_ref[...] = dpre[:SUBLANES]

        @pl.when(i == 0)
        def _():
            for j in range(CONV_K):
                dw_ref[j:j + 1, :] = dw[j]

        @pl.when(i > 0)
        def _():
            for j in range(CONV_K):
                dw_ref[j:j + 1, :] += dw[j]

    return pl.pallas_call(
        body, name=name, grid=(3, nb),
        in_specs=[pl.BlockSpec((R, B_W), lambda j, i: (nb - 1 - i, cb0 + j)),
                  pl.BlockSpec((SUBLANES, B_W), lambda j, i: (jnp.maximum((nb - 1 - i) * r8 - 1, 0), cb0 + j)),
                  pl.BlockSpec((CONV_K, B_W), lambda j, i: (0, j)),
                  pl.BlockSpec((R, B_W), lambda j, i: (nb - 1 - i, j)), ANY],
        out_specs=[pl.BlockSpec((R, B_W), lambda j, i: (nb - 1 - i, cb0 + j)),
                   pl.BlockSpec((CONV_K, B_W), lambda j, i: (0, j))],
        out_shape=[jax.ShapeDtypeStruct(dproj.shape, dproj.dtype), jax.ShapeDtypeStruct((CONV_K, 3 * B_W), F32)],
        input_output_aliases={4: 0},
        scratch_shapes=[pltpu.VMEM((SUBLANES, B_W), F32)],
        compiler_params=_cparams("parallel", "arbitrary"),
    )(proj, proj, conv_w, dy, dproj)


DK_SCALE = B_HEAD_DIM ** -0.5


BNN = (((2,), (1,)), ((0,), (0,)))
BNT = (((2,), (2,)), ((0,), (0,)))
BTN = (((1,), (1,)), ((0,), (0,)))


def _tri_inv(a):
    C = a.shape[-1]
    ri = lax.broadcasted_iota(jnp.int32, (C, C), 0)
    ci = lax.broadcasted_iota(jnp.int32, (C, C), 1)
    x = jnp.where(ri == ci, 1.0, 0.0)[None] - a
    p = _dotf(a, a, BNN)
    span = 2
    while span < C:
        x = x + _dotf(x, p, BNN)
        span *= 2
        if span < C:
            p = _dotf(p, p, BNN)
    return x


def _dn_chunk(qc, kc, v, gcol, grow, bcol, s0, tm=None):
    C = B_CHUNK
    ri = lax.broadcasted_iota(jnp.int32, (C, C), 0)
    ci = lax.broadcasted_iota(jnp.int32, (C, C), 1)
    incl, strict = (ri >= ci)[None], (ri > ci)[None]
    rq = lax.rsqrt(jnp.sum(qc * qc, axis=-1, keepdims=True) + NORM_EPS)
    rk = lax.rsqrt(jnp.sum(kc * kc, axis=-1, keepdims=True) + NORM_EPS)
    qn = qc * rq
    q = qn * DK_SCALE
    k = kc * rk
    gc_col = jnp.sum(jnp.where(incl, grow, 0.0), axis=2, keepdims=True)
    gc_row = jnp.sum(jnp.where((ri <= ci)[None], gcol, 0.0), axis=1, keepdims=True)
    gl = jnp.sum(gcol, axis=1, keepdims=True)
    dincl = jnp.where(incl, jnp.exp(jnp.where(incl, gc_col - gc_row, 0.0)), 0.0)
    dstrict = jnp.where(strict, dincl, 0.0)
    eg = jnp.exp(gc_col)
    ekt = jnp.exp(gl - gc_col)
    egl = jnp.exp(gl)
    kb = k * bcol
    vb = v * bcol
    kbg = kb * eg
    a = _dot(kb, k, BNT) * dstrict
    if tm is None:
        tm = _tri_inv(a)
    u = _dot(tm, vb, BNN)
    w = _dot(tm, kbg, BNN)
    vn = u - _dot(w, s0, BNN)
    qk = _dot(q, k, BNT) * dincl
    qg = q * eg
    kt = k * ekt
    o = _dot(qg, s0, BNN) + _dot(qk, vn, BNN)
    s1 = s0 * egl + _dot(kt, vn, BTN)
    return dict(rq=rq, rk=rk, qn=qn, q=q, k=k, dincl=dincl, dstrict=dstrict, eg=eg, ekt=ekt, egl=egl, kb=kb, vb=vb,
                kbg=kbg, a=a, tm=tm, w=w, vn=vn, qk=qk, qg=qg, kt=kt, o=o, s1=s1, ri=ri[None], ci=ci[None])


def _heads(ref):
    return jnp.stack([ref[:, h * B_HEAD_DIM:(h + 1) * B_HEAD_DIM] for h in range(B_HEADS)])


def _store_heads(ref, val):
    for h in range(B_HEADS):
        ref[:, h * B_HEAD_DIM:(h + 1) * B_HEAD_DIM] = val[h]


def _dn_specs(N, rev):
    ix = (lambda n: N - 1 - n) if rev else (lambda n: n)
    wide = lambda cb: pl.BlockSpec((B_CHUNK, B_W), lambda n: (ix(n), cb))
    return dict(
        q=wide(0), k=wide(1), v=wide(2), z=wide(HP_Z // B_W), dob=wide(A_Q_W // B_W), out=wide(0),
        nw=pl.BlockSpec((1, LANES), lambda n: (0, 0)),
        row=pl.BlockSpec((B_HEADS, None, 1, B_CHUNK), lambda n: (0, ix(n), 0, 0)),
        state=pl.BlockSpec((B_HEADS, None, B_HEAD_DIM, B_HEAD_DIM), lambda n: (0, ix(n), 0, 0)),
        inv=pl.BlockSpec((B_HEADS, None, B_CHUNK, B_CHUNK), lambda n: (0, ix(n), 0, 0)),
    )


def _to_col(row):
    C = row.shape[-1]
    eye = lax.broadcasted_iota(jnp.int32, (C, C), 0) == lax.broadcasted_iota(jnp.int32, (C, C), 1)
    return jnp.sum(jnp.where(eye[None], row, 0.0), axis=2, keepdims=True)


def _to_row(col):
    C = col.shape[1]
    eye = lax.broadcasted_iota(jnp.int32, (C, C), 0) == lax.broadcasted_iota(jnp.int32, (C, C), 1)
    return jnp.sum(jnp.where(eye[None], col, 0.0), axis=1, keepdims=True)


def _dn_fwd(qkvc, proj, norm_w, grow, brow, name):
    T = qkvc.shape[0]
    N = T // B_CHUNK
    sp = _dn_specs(N, False)

    def body(q_ref, k_ref, v_ref, z_ref, nw_ref, gr_ref, br_ref, o_ref, st_ref, tm_ref, s_ref):
        n = pl.program_id(0)

        @pl.when(n == 0)
        def _():
            s_ref[...] = jnp.zeros_like(s_ref)

        s0 = s_ref[...]
        st_ref[...] = s0
        grow_v = gr_ref[...]
        f = _dn_chunk(_heads(q_ref), _heads(k_ref), _heads(v_ref), _to_col(grow_v), grow_v, _to_col(br_ref[...]), s0)
        o = f["o"]
        r = lax.rsqrt(jnp.mean(o * o, axis=-1, keepdims=True) + NORM_EPS)
        _store_heads(o_ref, o * r * nw_ref[...][None] * _silu(_heads(z_ref)))
        s_ref[...] = f["s1"]
        tm_ref[...] = f["tm"]

    return pl.pallas_call(
        body, name=name, grid=(N,),
        in_specs=[sp["q"], sp["k"], sp["v"], sp["z"], sp["nw"], sp["row"], sp["row"]],
        out_specs=[sp["out"], sp["state"], sp["inv"]],
        out_shape=[jax.ShapeDtypeStruct((T, B_W), F32),
                   jax.ShapeDtypeStruct((B_HEADS, N, B_HEAD_DIM, B_HEAD_DIM), F32),
                   jax.ShapeDtypeStruct((B_HEADS, N, B_CHUNK, B_CHUNK), F32)],
        scratch_shapes=[pltpu.VMEM((B_HEADS, B_HEAD_DIM, B_HEAD_DIM), F32)],
        compiler_params=_cparams("arbitrary"),
    )(qkvc, qkvc, qkvc, proj, norm_w, grow, brow)


def _dn_bwd(qkvc, proj, norm_w, grow, brow, states, invs, dmix, dproj, name):
    T = qkvc.shape[0]
    N = T // B_CHUNK
    sp = _dn_specs(N, True)
    C = B_CHUNK

    def body(q_ref, k_ref, v_ref, z_ref, nw_ref, gr_ref, br_ref, st_ref, tm_ref, dob_ref, dproj_in,
             dqkv_ref, dz_ref, dg_ref, db_ref, dnw_ref, ds_ref):
        n = pl.program_id(0)
        dq_ref = dqkv_ref.at[:, 0:B_W]
        dk_ref = dqkv_ref.at[:, B_W:2 * B_W]
        dv_ref = dqkv_ref.at[:, 2 * B_W:3 * B_W]

        @pl.when(n == 0)
        def _():
            ds_ref[...] = jnp.zeros_like(ds_ref)
            dnw_ref[...] = jnp.zeros_like(dnw_ref)

        s0 = st_ref[...]
        ds1 = ds_ref[...]
        v, z, nw, bcol_v = _heads(v_ref), _heads(z_ref), nw_ref[...][None], _to_col(br_ref[...])
        grow_v = gr_ref[...]
        f = _dn_chunk(_heads(q_ref), _heads(k_ref), v, _to_col(grow_v), grow_v, bcol_v, s0, tm=tm_ref[...])
        o, q, k, qn = f["o"], f["q"], f["k"], f["qn"]
        eg, ekt, egl = f["eg"], f["ekt"], f["egl"]
        tm, w, vn, kb, vb, kbg = f["tm"], f["w"], f["vn"], f["kb"], f["vb"], f["kbg"]
        qg, kt, qk, a = f["qg"], f["kt"], f["qk"], f["a"]
        ri, ci = f["ri"], f["ci"]

        dob_v = _heads(dob_ref)
        r = lax.rsqrt(jnp.mean(o * o, axis=-1, keepdims=True) + NORM_EPS)
        sz = _silu(z)
        on = o * r
        dnw_ref[...] += jnp.sum(dob_v * sz * on, axis=1, keepdims=True)
        _store_heads(dz_ref, (dob_v * on * nw * _dsilu(z)).astype(ACT_DTYPE))
        d_on = dob_v * sz * nw
        do = r * (d_on - on * jnp.mean(d_on * on, axis=-1, keepdims=True))

        dvn = _dot(qk, do, BTN) + _dot(kt, ds1, BNN)
        dqk = _dot(do, vn, BNT)
        dqg = _dot(do, s0, BNT)
        ds_ref[...] = _dot(qg, do, BTN) + egl * ds1 - _dot(w, dvn, BTN)
        dgl = jnp.sum(s0 * ds1, axis=(1, 2), keepdims=True) * egl
        dkt = _dot(vn, ds1, BNT)
        dw = -_dot(dvn, s0, BNT)
        dq = dqg * eg
        dgc = jnp.sum(dqg * qg, axis=-1, keepdims=True)
        dk = dkt * ekt
        t_kt = jnp.sum(dkt * kt, axis=-1, keepdims=True)
        dgl = dgl + jnp.sum(t_kt, axis=1, keepdims=True)
        dgc = dgc - t_kt
        dqkr = dqk * f["dincl"]
        dq = dq + _dot(dqkr, k, BNN)
        dk = dk + _dot(dqkr, q, BTN)
        e_qk = dqk * qk
        dgc = dgc + jnp.sum(e_qk, axis=-1, keepdims=True)
        dgc_row = -jnp.sum(e_qk, axis=1, keepdims=True)
        dtm = _dot(dvn, vb, BNT) + _dot(dw, kbg, BNT)
        dvb = _dot(tm, dvn, BTN)
        dkbg = _dot(tm, dw, BTN)
        dkb = dkbg * eg
        dgc = dgc + jnp.sum(dkbg * kbg, axis=-1, keepdims=True)
        da = -_dotf(tm, _dotf(dtm, tm, BNT), BTN)
        dkk = da * f["dstrict"]
        e_a = da * a
        dgc = dgc + jnp.sum(e_a, axis=-1, keepdims=True)
        dgc_row = dgc_row - jnp.sum(e_a, axis=1, keepdims=True)
        dkb = dkb + _dot(dkk, k, BNN)
        dk = dk + _dot(dkk, kb, BTN)
        dk = dk + dkb * bcol_v
        db_ref[...] = _to_row(jnp.sum(dkb * k, axis=-1, keepdims=True) + jnp.sum(dvb * v, axis=-1, keepdims=True))
        _store_heads(dv_ref, dvb * bcol_v)
        dgc_row = dgc_row + jnp.sum(jnp.where(ri == ci, dgc, 0.0), axis=1, keepdims=True)
        dg_ref[...] = jnp.sum(jnp.where(ci <= ri, _to_col(dgc_row), 0.0), axis=1, keepdims=True) + dgl
        dqs = dq * DK_SCALE
        _store_heads(dq_ref, f["rq"] * (dqs - qn * jnp.sum(dqs * qn, axis=-1, keepdims=True)))
        _store_heads(dk_ref, f["rk"] * (dk - k * jnp.sum(dk * k, axis=-1, keepdims=True)))

    return pl.pallas_call(
        body, name=name, grid=(N,),
        in_specs=[sp["q"], sp["k"], sp["v"], sp["z"], sp["nw"], sp["row"], sp["row"], sp["state"], sp["inv"], sp["dob"],
                  ANY],
        out_specs=[pl.BlockSpec((C, 3 * B_W), lambda n: (N - 1 - n, 0)), sp["z"], sp["row"], sp["row"],
                   pl.BlockSpec((B_HEADS, 1, LANES), lambda n: (0, 0, 0))],
        out_shape=[jax.ShapeDtypeStruct((T, 3 * B_W), F32), jax.ShapeDtypeStruct(dproj.shape, dproj.dtype),
                   jax.ShapeDtypeStruct((B_HEADS, N, 1, C), F32), jax.ShapeDtypeStruct((B_HEADS, N, 1, C), F32),
                   jax.ShapeDtypeStruct((B_HEADS, 1, LANES), F32)],
        input_output_aliases={10: 1},
        scratch_shapes=[pltpu.VMEM((B_HEADS, B_HEAD_DIM, B_HEAD_DIM), F32)],
        compiler_params=_cparams("arbitrary"),
    )(qkvc, qkvc, qkvc, proj, norm_w, grow, brow, states, invs, dmix, dproj)


def _lru_gates(xc, wa_ref, wx_ref, ba, bx, sp):
    pre_r, pre_i = [], []
    for hb in range(LRU_BLOCKS):
        xb = xc[:, hb * LRU_BLOCK_W:(hb + 1) * LRU_BLOCK_W]
        pre_r.append(_dot(xb, wa_ref[hb]))
        pre_i.append(_dot(xb, wx_ref[hb]))
    r = _sigmoid(jnp.concatenate(pre_r, axis=1) + ba)
    i = _sigmoid(jnp.concatenate(pre_i, axis=1) + bx)
    la = -LRU_C * r * sp
    a = jnp.exp(la)
    th = jnp.tanh(la)
    s = jnp.sqrt(-2.0 * th / (1.0 - th))
    return r, i, a, s


def _scan_down(a, b):
    R = a.shape[0]
    row = lax.broadcasted_iota(jnp.int32, a.shape, 0)
    d = 1
    while d < R:
        ok = row >= d
        b = a * jnp.where(ok, pltpu.roll(b, d, axis=0), 0.0) + b
        a = a * jnp.where(ok, pltpu.roll(a, d, axis=0), 1.0)
        d *= 2
    return a, b


def _scan_up(a, b):
    R = a.shape[0]
    row = lax.broadcasted_iota(jnp.int32, a.shape, 0)
    d = 1
    while d < R:
        ok = row < R - d
        b = a * jnp.where(ok, pltpu.roll(b, R - d, axis=0), 0.0) + b
        a = a * jnp.where(ok, pltpu.roll(a, R - d, axis=0), 1.0)
        d *= 2
    return b


def _rglru_fwd(proj, conv_w, conv_b, wa, wx, ba, bx, sp, name):
    T = proj.shape[0]
    R = _tile(T, 256)
    W = D_MODEL

    def body(p_ref, cw_ref, cb_ref, wa_ref, wx_ref, ba_ref, bx_ref, sp_ref, hg_ref, h_ref, prev_ref, hc_ref):
        i = pl.program_id(0)

        @pl.when(i == 0)
        def _():
            prev_ref[...] = jnp.zeros_like(prev_ref)
            hc_ref[...] = jnp.zeros_like(hc_ref)

        xr = p_ref[:, :W]
        gate = p_ref[:, W:]
        xc = _conv(xr, prev_ref[...], cw_ref[...]) + cb_ref[...]
        prev_ref[...] = xr[R - SUBLANES:]
        r, ig, a, s = _lru_gates(xc, wa_ref, wx_ref, ba_ref[...], bx_ref[...], sp_ref[...])
        pa, hb = _scan_down(a, s * ig * xc)
        h = hb + pa * hc_ref[SUBLANES - 1:SUBLANES, :]
        h_ref[...] = h
        hg_ref[...] = (h * _gelu(gate)).astype(ACT_DTYPE)
        hc_ref[...] = h[R - SUBLANES:]

    vec = pl.BlockSpec((1, W), lambda i: (0, 0))
    wsp = pl.BlockSpec((LRU_BLOCKS, LRU_BLOCK_W, LRU_BLOCK_W), lambda i: (0, 0, 0))
    row = pl.BlockSpec((R, W), lambda i: (i, 0))
    return pl.pallas_call(
        body, name=name, grid=(T // R,),
        in_specs=[pl.BlockSpec((R, 2 * W), lambda i: (i, 0)), pl.BlockSpec((CONV_K, W), lambda i: (0, 0)),
                  vec, wsp, wsp, vec, vec, vec],
        out_specs=[row, row],
        out_shape=[jax.ShapeDtypeStruct((T, W), ACT_DTYPE), jax.ShapeDtypeStruct((T, W), F32)],
        scratch_shapes=[pltpu.VMEM((SUBLANES, W), F32), pltpu.VMEM((SUBLANES, W), F32)],
        compiler_params=_cparams("arbitrary"),
    )(proj, conv_w, conv_b, wa, wx, ba, bx, sp)


def _rglru_bwd(proj, conv_w, conv_b, wa, wx, ba, bx, sp, h, dhg, name):
    T = proj.shape[0]
    R = _tile(T, 256)
    nb = T // R
    r8 = R // SUBLANES
    W = D_MODEL

    def body(p_ref, pp_ref, cw_ref, cb_ref, wa_ref, wx_ref, ba_ref, bx_ref, sp_ref, h_ref, hp_ref, dhg_ref,
             dp_ref, dcw_ref, dcb_ref, dwa_ref, dwx_ref, dba_ref, dbx_ref, dsp_ref, lam_ref, nxt_ref):
        step = pl.program_id(0)
        blk = nb - 1 - step

        @pl.when(step == 0)
        def _():
            lam_ref[...] = jnp.zeros_like(lam_ref)
            nxt_ref[...] = jnp.zeros_like(nxt_ref)

        xr = p_ref[:, :W]
        gate = p_ref[:, W:]
        first = blk > 0
        prev8 = jnp.where(first, pp_ref[:, :W], 0.0)
        hprev8 = jnp.where(first, hp_ref[...], 0.0)
        cw = cw_ref[...]
        spv = sp_ref[...]
        xc = _conv(xr, prev8, cw) + cb_ref[...]
        r, ig, a, s = _lru_gates(xc, wa_ref, wx_ref, ba_ref[...], bx_ref[...], spv)
        hv = h_ref[...]
        dhg_v = dhg_ref[...]
        dgate = dhg_v * hv * _dgelu(gate)
        dh = dhg_v * _gelu(gate)
        row = lax.broadcasted_iota(jnp.int32, (R, W), 0)
        last = row == R - 1
        a_up = jnp.where(last, 0.0, pltpu.roll(a, R - 1, axis=0))
        lam = _scan_up(a_up, dh + jnp.where(last, lam_ref[0:1, :], 0.0))
        lam_ref[...] = (a * lam)[:SUBLANES]
        h_dn = _shift_down(hv, hprev8, 1)
        da = lam * h_dn
        bx_in = ig * xc
        dsv = lam * bx_in
        dig = lam * s * xc
        dxc = lam * s * ig
        dla = da * a - dsv * (a * a) / s
        dr = dla * (-LRU_C) * spv
        dsp = jnp.sum(dla * (-LRU_C) * r, axis=0, keepdims=True)
        dpr = dr * r * (1.0 - r)
        dpi = dig * ig * (1.0 - ig)
        dxc_parts, dwa_parts, dwx_parts = [], [], []
        for hb in range(LRU_BLOCKS):
            sl = slice(hb * LRU_BLOCK_W, (hb + 1) * LRU_BLOCK_W)
            xb, gr, gi = xc[:, sl], dpr[:, sl], dpi[:, sl]
            dxc_parts.append(_dot(gr, wa_ref[hb], NT) + _dot(gi, wx_ref[hb], NT))
            dwa_parts.append(_dot(xb, gr, TN))
            dwx_parts.append(_dot(xb, gi, TN))
        dxc = dxc + jnp.concatenate(dxc_parts, axis=1)
        dxr, dcw = _conv_bwd(xr, prev8, cw, dxc, nxt_ref[...])
        nxt_ref[...] = dxc[:SUBLANES]
        dp_ref[:, :W] = dxr.astype(ACT_DTYPE)
        dp_ref[:, W:] = dgate.astype(ACT_DTYPE)
        dcb = jnp.sum(dxc, axis=0, keepdims=True)
        dba = jnp.sum(dpr, axis=0, keepdims=True)
        dbx = jnp.sum(dpi, axis=0, keepdims=True)

        @pl.when(step == 0)
        def _():
            for j in range(CONV_K):
                dcw_ref[j:j + 1, :] = dcw[j]
            dcb_ref[...] = dcb
            dba_ref[...] = dba
            dbx_ref[...] = dbx
            dsp_ref[...] = dsp
            for hb in range(LRU_BLOCKS):
                dwa_ref[hb] = dwa_parts[hb]
                dwx_ref[hb] = dwx_parts[hb]

        @pl.when(step > 0)
        def _():
            for j in range(CONV_K):
                dcw_ref[j:j + 1, :] += dcw[j]
            dcb_ref[...] += dcb
            dba_ref[...] += dba
            dbx_ref[...] += dbx
            dsp_ref[...] += dsp
            for hb in range(LRU_BLOCKS):
                dwa_ref[hb] += dwa_parts[hb]
                dwx_ref[hb] += dwx_parts[hb]

    rv = lambda i: nb - 1 - i
    before = lambda i: jnp.maximum((nb - 1 - i) * r8 - 1, 0)
    vec = pl.BlockSpec((1, W), lambda i: (0, 0))
    cws = pl.BlockSpec((CONV_K, W), lambda i: (0, 0))
    wsp = pl.BlockSpec((LRU_BLOCKS, LRU_BLOCK_W, LRU_BLOCK_W), lambda i: (0, 0, 0))
    row = pl.BlockSpec((R, W), lambda i: (rv(i), 0))
    wshape = jax.ShapeDtypeStruct((LRU_BLOCKS, LRU_BLOCK_W, LRU_BLOCK_W), F32)
    vshape = jax.ShapeDtypeStruct((1, W), F32)
    return pl.pallas_call(
        body, name=name, grid=(nb,),
        in_specs=[pl.BlockSpec((R, 2 * W), lambda i: (rv(i), 0)), pl.BlockSpec((SUBLANES, 2 * W), lambda i: (before(i), 0)),
                  cws, vec, wsp, wsp, vec, vec, vec, row, pl.BlockSpec((SUBLANES, W), lambda i: (before(i), 0)), row],
        out_specs=[pl.BlockSpec((R, 2 * W), lambda i: (rv(i), 0)), cws, vec, wsp, wsp, vec, vec, vec],
        out_shape=[jax.ShapeDtypeStruct((T, 2 * W), ACT_DTYPE), jax.ShapeDtypeStruct((CONV_K, W), F32), vshape,
                   wshape, wshape, vshape, vshape, vshape],
        scratch_shapes=[pltpu.VMEM((SUBLANES, W), F32), pltpu.VMEM((SUBLANES, W), F32)],
        compiler_params=_cparams("arbitrary"),
    )(proj, proj, conv_w, conv_b, wa, wx, ba, bx, sp, h, h, dhg)


MESH = pl.DeviceIdType.MESH
ANY = pl.BlockSpec(memory_space=pl.ANY)


def _position():
    x, y, c = lax.axis_index("x"), lax.axis_index("y"), lax.axis_index("c")
    other_chips = [(1 - x, y), (x, 1 - y), (1 - x, 1 - y)]
    return x, y, c, other_chips


def _all_gather_weights(shards, name):
    n = len(shards)

    def body(*refs):
        ins, outs = refs[:n], refs[n:2 * n]
        send_sems, recv_sems = refs[2 * n:]
        x, y, c, chips = _position()
        me = 2 * x + y
        sibling = (x, y, 1 - c)

        def rcopy(t, k, src, dst, to):
            return pltpu.make_async_remote_copy(src_ref=src, dst_ref=dst, send_sem=send_sems.at[t, k],
                                                recv_sem=recv_sems.at[t, k], device_id=to, device_id_type=MESH)

        started = []
        for t in range(n):
            for j, (cx, cy) in enumerate(chips):
                cp = rcopy(t, j, ins[t].at[c], outs[t].at[me, c], (cx, cy, c))
                cp.start()
                started.append(cp)
        for t in range(n):
            for j, (cx, cy) in enumerate(chips):
                blk = outs[t].at[2 * cx + cy, c]
                rcopy(t, j, blk, blk, (cx, cy, c)).wait_recv()
                cp = rcopy(t, 3 + j, blk, blk, sibling)
                cp.start()
                started.append(cp)
        for t in range(n):
            for j, (cx, cy) in enumerate(chips):
                blk = outs[t].at[2 * cx + cy, 1 - c]
                rcopy(t, 3 + j, blk, blk, sibling).wait_recv()
        for cp in started:
            cp.wait_send()

    return pl.pallas_call(
        body, name=name, in_specs=[ANY] * n, out_specs=[ANY] * n,
        out_shape=[jax.ShapeDtypeStruct((N_CHIPS,) + s.shape, s.dtype) for s in shards],
        scratch_shapes=[pltpu.SemaphoreType.DMA((n, 6)), pltpu.SemaphoreType.DMA((n, 6))],
    )(*shards)


HBM = pl.BlockSpec(memory_space=pltpu.HBM)
SEM = pl.BlockSpec(memory_space=pltpu.SEMAPHORE)
EFFECT = pltpu.SideEffectType.DATAFLOW_SIDE_EFFECTING


def _gather_start(own, land, after, name):
    n = len(own)

    def body(*refs):
        own_refs, land_refs = refs[:n], refs[n:2 * n]
        send_sems = refs[2 * n + 1:3 * n + 1]
        recv_sems = refs[3 * n + 1:4 * n + 1]
        x, y, c, chips = _position()
        me = 2 * x + y
        for t in range(n):
            for cx, cy in chips:
                pltpu.make_async_remote_copy(
                    src_ref=own_refs[t], dst_ref=land_refs[t].at[me], send_sem=send_sems[t], recv_sem=recv_sems[t],
                    device_id=(cx, cy, c), device_id_type=MESH).start()

    sems = (pltpu.SemaphoreType.DMA(()),) * (2 * n)
    thru = [pltpu.HBM(a.shape, a.dtype) for a in list(own) + list(land)]
    out = pl.pallas_call(
        body, name=name, out_shape=(*sems, *thru),
        in_specs=[HBM] * (2 * n) + [pl.BlockSpec(memory_space=pl.ANY)], out_specs=(SEM,) * (2 * n) + (HBM,) * (2 * n),
        input_output_aliases={i: 2 * n + i for i in range(2 * n)},
        compiler_params=pltpu.CompilerParams(has_side_effects=EFFECT),
    )(*[pltpu.with_memory_space_constraint(a, pltpu.HBM) for a in list(own) + list(land)], after)
    return list(out[:n]), list(out[n:2 * n]), list(out[2 * n:3 * n]), list(out[3 * n:])


def _gather_wait(send_sems, recv_sems, own, land, after, name):
    n = len(own)

    def body(*refs):
        land_refs = refs[n:2 * n]
        s_sems, r_sems = refs[2 * n:3 * n], refs[3 * n:4 * n]
        x, y, c, _ = _position()
        for t in range(n):
            three = land_refs[t].at[pl.ds(0, N_CHIPS - 1)]
            cp = pltpu.make_async_remote_copy(src_ref=three, dst_ref=three, send_sem=s_sems[t], recv_sem=r_sems[t],
                                              device_id=(x, y, c), device_id_type=MESH)
            cp.wait_send()
            cp.wait_recv()

    thru = [pltpu.HBM(a.shape, a.dtype) for a in list(own) + list(land)]
    out = pl.pallas_call(
        body, name=name, out_shape=tuple(thru),
        in_specs=[HBM] * (2 * n) + [SEM] * (2 * n) + [pl.BlockSpec(memory_space=pl.ANY)], out_specs=(HBM,) * (2 * n),
        input_output_aliases={i: i for i in range(2 * n)},
        compiler_params=pltpu.CompilerParams(has_side_effects=EFFECT),
    )(*own, *land, *send_sems, *recv_sems, after)
    return list(out[n:])


def _rs_to_sibling(grads, name):
    n = len(grads)

    def body(*refs):
        ins, outs = refs[:n], refs[n:2 * n]
        send_sems, recv_sems = refs[2 * n:]
        x, y, c, _ = _position()
        cps = [pltpu.make_async_remote_copy(src_ref=ins[t].at[:, 1 - c], dst_ref=outs[t], send_sem=send_sems.at[t],
                                            recv_sem=recv_sems.at[t], device_id=(x, y, 1 - c), device_id_type=MESH)
               for t in range(n)]
        for cp in cps:
            cp.start()
        for cp in cps:
            cp.wait()

    return pl.pallas_call(
        body, name=name, in_specs=[ANY] * n, out_specs=[ANY] * n,
        out_shape=[jax.ShapeDtypeStruct((N_CHIPS,) + g.shape[2:], g.dtype) for g in grads],
        scratch_shapes=[pltpu.SemaphoreType.DMA((n,)), pltpu.SemaphoreType.DMA((n,))],
    )(*grads)


def _rs_across_start(parts, recv, name):
    n = len(parts)

    def body(*refs):
        part_refs, recv_refs = refs[:n], refs[n:2 * n]
        send_sems, recv_sems = refs[2 * n:3 * n], refs[3 * n:4 * n]
        token_ref = refs[-1]
        x, y, c, chips = _position()
        me = 2 * x + y
        for t in range(n):
            for cx, cy in chips:
                pltpu.make_async_remote_copy(src_ref=part_refs[t].at[2 * cx + cy], dst_ref=recv_refs[t].at[me],
                                             send_sem=send_sems[t], recv_sem=recv_sems[t], device_id=(cx, cy, c),
                                             device_id_type=MESH).start()
        token_ref[...] = jnp.zeros_like(token_ref)

    sems = (pltpu.SemaphoreType.DMA(()),) * (2 * n)
    thru = [pltpu.HBM(a.shape, a.dtype) for a in list(parts) + list(recv)]
    out = pl.pallas_call(
        body, name=name, out_shape=(*sems, *thru, jax.ShapeDtypeStruct((SUBLANES, LANES), F32)),
        in_specs=[HBM] * (2 * n), out_specs=(SEM,) * (2 * n) + (HBM,) * (2 * n) + (pl.BlockSpec(memory_space=pltpu.VMEM),),
        input_output_aliases={i: 2 * n + i for i in range(2 * n)},
        compiler_params=pltpu.CompilerParams(has_side_effects=EFFECT),
    )(*[pltpu.with_memory_space_constraint(a, pltpu.HBM) for a in list(parts) + list(recv)])
    return out[:n], out[n:2 * n], list(out[2 * n:3 * n]), list(out[3 * n:4 * n]), out[-1]


def _rs_across_wait(send_sems, recv_sems, parts, recv, after, name):
    n = len(parts)

    def body(*refs):
        recv_refs = refs[n:2 * n]
        s_sems, r_sems = refs[2 * n:3 * n], refs[3 * n:4 * n]
        x, y, c, _ = _position()
        for t in range(n):
            three = recv_refs[t].at[pl.ds(0, N_CHIPS - 1)]
            cp = pltpu.make_async_remote_copy(src_ref=three, dst_ref=three, send_sem=s_sems[t], recv_sem=r_sems[t],
                                              device_id=(x, y, c), device_id_type=MESH)
            cp.wait_send()
            cp.wait_recv()

    thru = [pltpu.HBM(a.shape, a.dtype) for a in list(parts) + list(recv)]
    out = pl.pallas_call(
        body, name=name, out_shape=tuple(thru),
        in_specs=[HBM] * (2 * n) + [SEM] * (2 * n) + [pl.BlockSpec(memory_space=pl.ANY)], out_specs=(HBM,) * (2 * n),
        input_output_aliases={i: i for i in range(2 * n)},
        compiler_params=pltpu.CompilerParams(has_side_effects=EFFECT),
    )(*parts, *recv, *send_sems, *recv_sems, after)
    return list(out[:n]), list(out[n:])


def _rs_join_halves(halves, name):
    n = len(halves)

    def body(*refs):
        ins, outs = refs[:n], refs[n:2 * n]
        send_sems, recv_sems = refs[2 * n:]
        x, y, c, _ = _position()
        cps = [pltpu.make_async_remote_copy(src_ref=ins[t].at[c], dst_ref=outs[t].at[c], send_sem=send_sems.at[t],
                                            recv_sem=recv_sems.at[t], device_id=(x, y, 1 - c), device_id_type=MESH)
               for t in range(n)]
        for cp in cps:
            cp.start()
        for t in range(n):
            blk = outs[t].at[1 - c]
            pltpu.make_async_remote_copy(src_ref=blk, dst_ref=blk, send_sem=send_sems.at[t], recv_sem=recv_sems.at[t],
                                         device_id=(x, y, 1 - c), device_id_type=MESH).wait_recv()
        for cp in cps:
            cp.wait_send()

    return pl.pallas_call(
        body, name=name, in_specs=[ANY] * n, out_specs=[ANY] * n,
        out_shape=[jax.ShapeDtypeStruct(h.shape, h.dtype) for h in halves],
        input_output_aliases={t: t for t in range(n)},
        scratch_shapes=[pltpu.SemaphoreType.DMA((n,)), pltpu.SemaphoreType.DMA((n,))],
    )(*halves)


def _all_gather_small(block, name):
    m_per, n = block.shape

    def body(x_ref, out_ref, send_sems, recv_sems, local_sem):
        x, y, c, chips = _position()
        me, sibling = (x, y, c), (x, y, 1 - c)

        def rows(px, py, pc):
            return out_ref.at[pl.ds((4 * px + 2 * py + pc) * m_per, m_per), :]

        def copy(k, blk, to, src=None):
            return pltpu.make_async_remote_copy(
                src_ref=rows(*blk) if src is None else src, dst_ref=rows(*blk), send_sem=send_sems.at[k],
                recv_sem=recv_sems.at[k], device_id=to, device_id_type=MESH)

        mine = pltpu.make_async_copy(x_ref, rows(*me), local_sem)
        mine.start()
        first = [copy(0, me, sibling, src=x_ref)]
        first += [copy(1 + j, me, (*chip, c), src=x_ref) for j, chip in enumerate(chips)]
        for cp in first:
            cp.start()
        passed = [copy(4 + j, (*chip, c), sibling) for j, chip in enumerate(chips)]
        for j, chip in enumerate(chips):
            copy(1 + j, (*chip, c), me).wait_recv()
            passed[j].start()
        copy(0, sibling, me).wait_recv()
        for j, chip in enumerate(chips):
            copy(4 + j, (*chip, 1 - c), me).wait_recv()
        for cp in first + passed:
            cp.wait_send()
        mine.wait()

    return pl.pallas_call(
        body, name=name, out_shape=jax.ShapeDtypeStruct((N_DEV * m_per, n), block.dtype),
        in_specs=[pl.BlockSpec(memory_space=pltpu.VMEM)], out_specs=pl.BlockSpec(memory_space=pltpu.VMEM),
        scratch_shapes=[pltpu.SemaphoreType.DMA((7,)), pltpu.SemaphoreType.DMA((7,)), pltpu.SemaphoreType.DMA],
    )(block)


def _row_tile(R, n):
    budget = 1 << 19
    if R * n <= budget or R % SUBLANES:
        return R
    t = R
    while t * n > budget and t % (2 * SUBLANES) == 0:
        t //= 2
    return t


def _pair_sum(g, recv, c_arr, name):
    _, _, R, n = g.shape
    tr = _row_tile(R, n)

    def body(c_ref, g_ref, r_ref, o_ref):
        o_ref[...] = (g_ref[...] + r_ref[...]).astype(ICI_DTYPE)

    grid_spec = pltpu.PrefetchScalarGridSpec(
        num_scalar_prefetch=1, grid=(N_CHIPS, R // tr),
        in_specs=[pl.BlockSpec((None, None, tr, n), lambda p, i, c: (p, c[0], i, 0)),
                  pl.BlockSpec((None, tr, n), lambda p, i, c: (p, i, 0))],
        out_specs=pl.BlockSpec((None, tr, n), lambda p, i, c: (p, i, 0)))
    return pl.pallas_call(
        body, name=name, grid_spec=grid_spec, out_shape=jax.ShapeDtypeStruct(recv.shape, ICI_DTYPE),
        compiler_params=_cparams("parallel", "parallel"),
    )(c_arr, g, recv)


def _chip_sum(recv, own, where, name):
    _, R, n = recv.shape
    tr = _row_tile(R, n)

    def body(w_ref, r0, r1, r2, r3, own_ref, o_ref):
        me = w_ref[0]
        terms = [jnp.where(me == k, own_ref[...], r[...]).astype(F32) for k, r in enumerate((r0, r1, r2, r3))]
        o_ref[...] = ((terms[0] + terms[1]) + terms[2]) + terms[3]

    def slot(k):
        return pl.BlockSpec((None, tr, n), lambda i, w: (w[1 + k], i, 0))

    grid_spec = pltpu.PrefetchScalarGridSpec(
        num_scalar_prefetch=1, grid=(R // tr,),
        in_specs=[slot(0), slot(1), slot(2), slot(3), pl.BlockSpec((None, tr, n), lambda i, w: (w[0], i, 0))],
        out_specs=pl.BlockSpec((None, tr, n), lambda i, w: (w[5], i, 0)))
    return pl.pallas_call(
        body, name=name, grid_spec=grid_spec, out_shape=jax.ShapeDtypeStruct((2, R, n), F32),
        compiler_params=_cparams("parallel"),
    )(where, recv, recv, recv, recv, own)


ADAM_C1 = 1.0 / (1.0 - ADAM_B1 ** ADAM_STEP)
ADAM_C2 = 1.0 / (1.0 - ADAM_B2 ** ADAM_STEP)


def _adamw_math(w, g, m, v):
    m = ADAM_B1 * m + (1.0 - ADAM_B1) * g
    v = ADAM_B2 * v + (1.0 - ADAM_B2) * (g * g)
    delta = -ADAM_LR * ((m * ADAM_C1) / (jnp.sqrt(v * ADAM_C2) + ADAM_EPS) + ADAM_WD * w)
    return delta, m, v


def _adamw(w, g, m, v, name):
    R, n = w.shape
    tr = _row_tile(R, n)

    def body(w_ref, g_ref, m_ref, v_ref, d_ref, nm_ref, nv_ref):
        d_ref[...], nm_ref[...], nv_ref[...] = _adamw_math(w_ref[...], g_ref[...], m_ref[...], v_ref[...])

    spec = pl.BlockSpec((tr, n), lambda i: (i, 0))
    return pl.pallas_call(
        body, name=name, grid=(R // tr,), in_specs=[spec] * 4, out_specs=[spec] * 3,
        out_shape=[jax.ShapeDtypeStruct((R, n), F32)] * 3, compiler_params=_cparams("parallel"),
    )(w, g, m, v)


def _adamw_half(w, g, m, v, half, prev, name):
    R, n = w.shape
    tr = _row_tile(R // 2, n)
    off = half * (R // 2 // tr)

    def body(*refs):
        w_ref, g_ref, m_ref, v_ref = refs[:4]
        go_ref, d_ref, nm_ref, nv_ref = refs[-4:]
        gv = g_ref[...]
        go_ref[...] = gv
        d_ref[...], nm_ref[...], nv_ref[...] = _adamw_math(w_ref[...], gv, m_ref[...], v_ref[...])

    rows = pl.BlockSpec((tr, n), lambda i: (i + off, 0))
    carried = [] if prev is None else list(prev)
    return pl.pallas_call(
        body, name=name, grid=(R // 2 // tr,),
        in_specs=[rows, pl.BlockSpec((tr, n), lambda i: (i, 0)), rows, rows] + [ANY] * len(carried),
        out_specs=[rows] * 4, out_shape=[jax.ShapeDtypeStruct((R, n), F32)] * 4,
        input_output_aliases={4 + i: i for i in range(len(carried))},
        compiler_params=_cparams("parallel"),
    )(w, g, m, v, *carried)


def _adamw_small(w, gall, m, v, name):
    M, n = w.shape

    def body(w_ref, g_ref, m_ref, v_ref, gs_ref, d_ref, nm_ref, nv_ref):
        g = g_ref[0:M, :]
        for d in range(1, N_DEV):
            g = g + g_ref[d * M:(d + 1) * M, :]
        gs_ref[...] = g
        d_ref[...], nm_ref[...], nv_ref[...] = _adamw_math(w_ref[...], g, m_ref[...], v_ref[...])

    return pl.pallas_call(
        body, name=name, out_shape=[jax.ShapeDtypeStruct((M, n), F32)] * 4,
    )(w, gall, m, v)


SMALL_ROWS = 24
MID_ROWS = 4


def _pack_small(ln1_g, ln1_b, ln2_g, ln2_b, norm_w, sinks, a_log, dt_bias):
    mixed = jnp.concatenate([norm_w.reshape(-1), sinks.reshape(-1), a_log.reshape(-1), dt_bias.reshape(-1)])
    mixed = jnp.pad(mixed, (0, D_MODEL - mixed.shape[0]))[None]
    pad = jnp.zeros((SMALL_ROWS - 4 * DEPTH - 1, D_MODEL), F32)
    return jnp.concatenate([ln1_g, ln1_b, ln2_g, ln2_b, mixed, pad], axis=0)


def _unpack_small(p):
    mixed = p[4 * DEPTH]
    return (p[0:4], p[4:8], p[8:12], p[12:16], mixed[0:256].reshape(2, 128), mixed[256:272].reshape(2, 8),
            mixed[272:280].reshape(2, 4), mixed[280:288].reshape(2, 4))


def _pack_mid(conv_w, rconv_w, rconv_b, b_a, b_x, lam):
    lead = conv_w.shape[0]
    flat = jnp.concatenate([conv_w.reshape(lead, -1), rconv_w.reshape(lead, -1), rconv_b, b_a, b_x, lam], axis=1)
    return jnp.pad(flat, ((0, 0), (0, MID_ROWS * D_MODEL - flat.shape[1]))).reshape(lead, MID_ROWS, D_MODEL)


def _unpack_mid(p):
    lead = p.shape[:-2]
    f = p.reshape(lead + (MID_ROWS * D_MODEL,))
    return (f[..., 0:1536].reshape(lead + (4, 384)), f[..., 1536:2560].reshape(lead + (4, 256)),
            f[..., 2560:2816], f[..., 2816:3072], f[..., 3072:3328], f[..., 3328:3584])


def _cols_from_chips(g):
    p, L, R, n = g.shape
    return g.transpose(1, 2, 0, 3).reshape(L, R, p * n)


def _rows_from_chips(g):
    p, L, R, n = g.shape
    return g.transpose(1, 0, 2, 3).reshape(L, p * R, n)


def _cols_to_chips(g):
    L, R, n4 = g.shape
    return g.reshape(L, R, N_CHIPS, n4 // N_CHIPS).transpose(2, 0, 1, 3)


def _rows_to_chips(g):
    L, R4, n = g.shape
    return g.reshape(L, N_CHIPS, R4 // N_CHIPS, n).transpose(1, 0, 2, 3)


def _halves(a):
    return a.reshape(2, -1, a.shape[-1])


def _pad_hyb_cols(w):
    z = jnp.zeros(w.shape[:-1] + (HYB_PAD - HP_BG - 2 * B_HEADS,), w.dtype)
    return jnp.concatenate([w[..., 0:512], w[..., 768:2304], w[..., 2304:2816], w[..., 512:768], w[..., 2816:2824], z], axis=-1)


def _unpad_hyb_cols(w):
    return jnp.concatenate([w[..., 0:512], w[..., 2560:2816], w[..., 512:2048], w[..., 2048:2560], w[..., 2816:2824]], axis=-1)


def _hybrid_fwd(x, W, j, tables, sfx, out_ready):
    cos, sin_s = tables
    T = x.shape[0]
    N = T // B_CHUNK
    proj = _matmul(x, W["hyb_w_in"][j], "nn", "hyb_in" + sfx)
    sinks_b = jnp.broadcast_to(W["hyb_sinks"][j][:, None], (A_Q_HEADS, LANES))
    o_a, lse = _attn_fwd(proj, cos, sin_s, sinks_b, "attn_fwd" + sfx)
    qkvc = _dnconv_fwd(proj, W["hyb_conv_w"][j], "dnconv_fwd" + sfx)
    bg = proj[:, HP_BG:HP_BG + 2 * B_HEADS]
    beta = jax.nn.sigmoid(bg[:, :B_HEADS])
    pre = bg[:, B_HEADS:] + W["hyb_dt_bias"][j][None]
    g = -jnp.exp(W["hyb_a_log"][j])[None] * jax.nn.softplus(pre)
    grow = g.T.reshape(B_HEADS, N, 1, B_CHUNK)
    brow = beta.T.reshape(B_HEADS, N, 1, B_CHUNK)
    nw = W["hyb_norm_w"][j][None]
    o_b, states, invs = _dn_fwd(qkvc, proj, nw, grow, brow, "dn_fwd" + sfx)
    mix = jnp.concatenate([o_a, o_b], axis=1).astype(ACT_DTYPE)
    out_ready(mix)
    y = _matmul(mix, W["hyb_w_out"][j], "nn", "hyb_out" + sfx)
    res = dict(proj=proj, o_a=o_a, lse=lse, qkvc=qkvc, beta=beta, pre=pre, g=g, grow=grow, brow=brow,
               states=states, invs=invs, mix=mix, sinks_b=sinks_b, nw=nw)
    return y, res


def _hybrid_bwd(x, du, dub, W, j, res, tables, sfx):
    cos, sin_s = tables
    T = x.shape[0]
    proj = res["proj"]
    d_wout = _matmul(res["mix"], dub, "tn", "hyb_out_dw" + sfx)
    dmix = _matmul(dub, W["hyb_w_out"][j], "nt", "hyb_out_dx" + sfx)
    dproj, dkc, dkp, dvc, dvp, dsink = _attn_bwd(proj, cos, sin_s, res["sinks_b"], res["o_a"], res["lse"], dmix,
                                                  "attn_bwd" + sfx)
    zpad = jnp.zeros((WINDOW, LANES), F32)
    dk = dkc + jnp.concatenate([dkp[WINDOW:], zpad], axis=0)
    dv = dvc + jnp.concatenate([dvp[WINDOW:], zpad], axis=0)
    dqkvc, dproj, dg4, dbeta4, dnw = _dn_bwd(res["qkvc"], proj, res["nw"], res["grow"], res["brow"], res["states"],
                                             res["invs"], dmix, dproj, "dn_bwd" + sfx)
    dproj, dconv = _dnconv_bwd(proj, W["hyb_conv_w"][j], dqkvc, dproj, "dnconv_bwd" + sfx)
    dg = dg4.reshape(B_HEADS, T).T
    dbeta = dbeta4.reshape(B_HEADS, T).T
    beta = res["beta"]
    dbeta_logit = dbeta * beta * (1.0 - beta)
    da_logit = dg * (-jnp.exp(W["hyb_a_log"][j]))[None] * jax.nn.sigmoid(res["pre"])
    d_dt_bias = jnp.sum(da_logit, axis=0)
    d_a_log = jnp.sum(dg * res["g"], axis=0)
    zcols = jnp.zeros((T, HYB_PAD - HP_BG - 2 * B_HEADS), F32)
    tail = jnp.concatenate([dk, dv, dbeta_logit, da_logit, zcols], axis=1).astype(ACT_DTYPE)
    dproj = lax.dynamic_update_slice(dproj, tail, (0, HP_K))
    d_win = _matmul(x, dproj, "tn", "hyb_in_dw" + sfx)
    dx = _matmul(dproj, W["hyb_w_in"][j], "nt", "hyb_in_dx" + sfx, epi=_epi_add_residual, extra=du)
    grads = dict(hyb_w_in=d_win, hyb_w_out=d_wout, hyb_sinks=dsink[0], hyb_conv_w=dconv, hyb_a_log=d_a_log,
                 hyb_dt_bias=d_dt_bias, hyb_norm_w=jnp.sum(dnw[:, 0, :], axis=0))
    return dx, grads


def _rec_fwd(x, W, j, sfx):
    proj = _matmul(x, W["rec_w_in"][j], "nn", "rec_in" + sfx)
    sp = jax.nn.softplus(-W["rec_lambda"][j])[None]
    hg, h = _rglru_fwd(proj, W["rec_conv_w"][j], W["rec_conv_b"][j][None], W["rec_w_a"][j], W["rec_w_x"][j],
                       W["rec_b_a"][j][None], W["rec_b_x"][j][None], sp, "rglru_fwd" + sfx)
    y = _matmul(hg, W["rec_w_out"][j], "nn", "rec_out" + sfx)
    return y, dict(proj=proj, hg=hg, h=h, sp=sp)


def _rec_bwd(x, du, dub, W, j, res, sfx):
    d_wout = _matmul(res["hg"], dub, "tn", "rec_out_dw" + sfx)
    dhg = _matmul(dub, W["rec_w_out"][j], "nt", "rec_out_dx" + sfx)
    dproj, dcw, dcb, dwa, dwx, dba, dbx, dsp = _rglru_bwd(
        res["proj"], W["rec_conv_w"][j], W["rec_conv_b"][j][None], W["rec_w_a"][j], W["rec_w_x"][j],
        W["rec_b_a"][j][None], W["rec_b_x"][j][None], res["sp"], res["h"], dhg, "rglru_bwd" + sfx)
    d_lam = dsp[0] * (-jax.nn.sigmoid(-W["rec_lambda"][j]))
    d_win = _matmul(x, dproj, "tn", "rec_in_dw" + sfx)
    dx = _matmul(dproj, W["rec_w_in"][j], "nt", "rec_in_dx" + sfx, epi=_epi_add_residual, extra=du)
    grads = dict(rec_w_in=d_win, rec_w_out=d_wout, rec_conv_w=dcw, rec_conv_b=dcb[0], rec_w_a=dwa, rec_w_x=dwx,
                 rec_b_a=dba[0], rec_b_x=dbx[0], rec_lambda=d_lam)
    return dx, grads


def _local_step(x, tgt, W, mlp_w, mixer_ready, on_group):
    T = x.shape[0]
    tables = _rope_tables(T)
    acts = []
    xb = x.astype(ACT_DTYPE)
    for layer in range(DEPTH):
        j, sfx = layer // 2, ""
        mixer_ready(layer, xb)
        if layer % 2 == 0:
            y, res = _hybrid_fwd(xb, W, j, tables, sfx, functools.partial(mixer_ready, layer, out_projection=True))
        else:
            y, res = _rec_fwd(xb, W, j, sfx)
        x1, x1b = _ln_fwd(x, y, W["ln1_g"][layer][None], W["ln1_b"][layer][None], "ln_fwd")
        w1, w2, wl = mlp_w(layer, x1b)
        h1 = _matmul(x1b, w1, "nn", "mlp_up", out_dtype=ACT_DTYPE, b_chips=("j", wl))
        y2 = _matmul(h1, w2, "nn", "mlp_down", a_fn=_relu2, b_chips=("k", wl))
        x2, x2b = _ln_fwd(x1, y2, W["ln2_g"][layer][None], W["ln2_b"][layer][None], "ln_fwd")
        acts.append(dict(x=x, xb=xb, y=y, res=res, x1=x1, x1b=x1b, h1=h1, y2=y2))
        x, xb = x2, x2b
    dx, loss = _loss_head(x, tgt, "loss_head")
    per_layer = [None] * DEPTH
    d_w1 = [lax.empty((N_CHIPS, 2, D_MODEL, D_FF // N_CHIPS), F32) for _ in range(DEPTH // 2)]
    d_w2 = [lax.empty((N_CHIPS, 2, D_FF // N_CHIPS, D_MODEL), F32) for _ in range(DEPTH // 2)]
    token = None
    for layer in reversed(range(DEPTH)):
        j, a = layer // 2, acts[layer]
        ln2_g = W["ln2_g"][layer][None]
        if token is not None:
            ln2_g = ln2_g + token
        du2, du2b, dg2, db2 = _ln_bwd(a["x1"], a["y2"], ln2_g, dx, "ln_bwd")
        w1, w2, wl = mlp_w(layer, du2b)
        d_w2[j] = _matmul(a["h1"], du2b, "tn", "mlp_down_dw", a_fn=_relu2, out_chips=("i", layer % 2, d_w2[j]))
        dh1 = _matmul(du2b, w2, "nt", "mlp_down_dx", epi=_epi_drelu2, extra=a["h1"], out_dtype=ACT_DTYPE,
                      b_chips=("j", wl))
        d_w1[j] = _matmul(a["x1b"], dh1, "tn", "mlp_up_dw", out_chips=("j", layer % 2, d_w1[j]))
        dx1 = _matmul(dh1, w1, "nt", "mlp_up_dx", epi=_epi_add_residual, extra=du2, b_chips=("k", wl))
        du1, du1b, dg1, db1 = _ln_bwd(a["x"], a["y"], W["ln1_g"][layer][None], dx1, "ln_bwd")
        if layer % 2 == 0:
            dx, g = _hybrid_bwd(a["xb"], du1, du1b, W, j, a["res"], tables, "")
        else:
            dx, g = _rec_bwd(a["xb"], du1, du1b, W, j, a["res"], "")
        g.update(ln1_g=dg1[0], ln1_b=db1[0], ln2_g=dg2[0], ln2_b=db2[0])
        per_layer[layer] = g
        if layer % 2 == 0:
            token = on_group(j, per_layer[layer], per_layer[layer + 1], d_w1[j], d_w2[j])
    grads = {}
    for name in ("ln1_g", "ln1_b", "ln2_g", "ln2_b"):
        grads[name] = jnp.stack([per_layer[l][name] for l in range(DEPTH)])
    for name in ("hyb_norm_w", "hyb_sinks", "hyb_a_log", "hyb_dt_bias"):
        grads[name] = jnp.stack([per_layer[l][name] for l in (0, 2)])
    return loss, dx, grads


BIG = ("hyb_w_in", "hyb_w_out", "rec_w_in", "rec_w_out", "mlp_w1", "mlp_w2", "rec_w_a", "rec_w_x")
COL_SHARDED = ("hyb_w_in", "rec_w_in", "mlp_w1")
CHIP_MAJOR = ("mlp_w1", "mlp_w2")
MID = ("hyb_conv_w", "rec_conv_w", "rec_conv_b", "rec_b_a", "rec_b_x", "rec_lambda")
SMALL = ("ln1_g", "ln1_b", "ln2_g", "ln2_b", "hyb_norm_w", "hyb_sinks", "hyb_a_log", "hyb_dt_bias")
WEIGHTS = ("hyb_w_in", "hyb_sinks", "hyb_conv_w", "hyb_a_log", "hyb_dt_bias", "hyb_norm_w", "hyb_w_out", "rec_w_in",
           "rec_conv_w", "rec_conv_b", "rec_w_a", "rec_b_a", "rec_w_x", "rec_b_x", "rec_lambda", "rec_w_out", "ln1_g",
           "ln1_b", "mlp_w1", "mlp_w2", "ln2_g", "ln2_b")


def _gather_full_weights(w):
    wb = {k: w[k].astype(MXU_DTYPE) for k in BIG}
    now = ("hyb_w_in",)
    shards = [_halves(wb[k][:1]) for k in now]
    shards.append(_pack_mid(*[w[k] for k in MID]))
    got = _all_gather_weights(shards, "all_gather_weights")
    me = 2 * lax.axis_index("x") + lax.axis_index("y")
    got = [lax.dynamic_update_slice(g, s[None], (me, 0, 0, 0)) for s, g in zip(shards, got)]

    def full(k, g):
        g = g.reshape((N_CHIPS,) + w[k].shape[1:])
        if k in CHIP_MAJOR:
            return g[:, None]
        if k in ("rec_w_a", "rec_w_x"):
            return g.transpose(1, 0, 2, 3).reshape(LRU_BLOCKS, LRU_BLOCK_W, LRU_BLOCK_W)
        f = _cols_from_chips(g[:, None])[0] if k in COL_SHARDED else _rows_from_chips(g[:, None])[0]
        return _pad_hyb_cols(f) if k == "hyb_w_in" else f

    rec = ("rec_w_in", "rec_w_out", "rec_w_a", "rec_w_x")
    groups = [(("hyb_w_out",), 0), (CHIP_MAJOR, 0), (rec, 0), (CHIP_MAJOR, 1), (("hyb_w_in", "hyb_w_out"), 1),
              (CHIP_MAJOR, 2), (rec, 1), (CHIP_MAJOR, 3)]
    own = [wb[k][j] for names, j in groups for k in names]
    land = [lax.dynamic_update_slice(lax.empty((N_CHIPS,) + o.shape, o.dtype), o[None], (me,) + (0,) * o.ndim)
            for o in own]
    send_sems, recv_sems, own, land = _gather_start(own, land, got[-1], "gather_start")
    W = {k: [None] * w[k].shape[0] for k in BIG}
    for k, g in zip(now, got[:-1]):
        W[k][0] = full(k, g)
    arrived = [0]

    def ensure(upto, after):
        while arrived[0] <= upto:
            gi = arrived[0]
            names, j = groups[gi]
            lo = sum(len(nm) for nm, _ in groups[:gi])
            sl = slice(lo, lo + len(names))
            got_g = _gather_wait(send_sems[sl], recv_sems[sl], own[sl], land[sl], after, "gather_wait_%d" % gi)
            for k, g in zip(names, got_g):
                W[k][j] = full(k, g)
            arrived[0] += 1

    def mixer_ready(layer, after, out_projection=False):
        if layer:
            ensure({1: 2, 2: 4, 3: 6}[layer], after)
        elif out_projection:
            ensure(0, after)

    def mlp_w(layer, after):
        ensure({0: 1, 1: 3, 2: 5, 3: 7}[layer], after)
        return W["mlp_w1"][layer], W["mlp_w2"][layer], 0

    conv_w, rconv_w, rconv_b, b_a, b_x, lam = _unpack_mid(got[-1])
    W["hyb_conv_w"] = conv_w.transpose(1, 2, 0, 3).reshape(2, CONV_K, 3 * B_W)
    W["rec_conv_w"] = rconv_w.transpose(1, 2, 0, 3).reshape(2, CONV_K, D_MODEL)
    for k, v in (("rec_conv_b", rconv_b), ("rec_b_a", b_a), ("rec_b_x", b_x), ("rec_lambda", lam)):
        W[k] = v.transpose(1, 0, 2).reshape(2, D_MODEL)
    for k in SMALL:
        W[k] = w[k]
    return W, mlp_w, mixer_ready


def _group_by_chip(gh, gr, d_w1, d_w2):
    g = dict(gh, **gr)
    g["hyb_w_in"] = _unpad_hyb_cols(g["hyb_w_in"])
    out = []
    for k in BIG:
        if k == "mlp_w1":
            v = d_w1
        elif k == "mlp_w2":
            v = d_w2
        elif k in ("rec_w_a", "rec_w_x"):
            v = g[k].reshape(1, LRU_BLOCKS, N_CHIPS, LRU_BLOCK_W // N_CHIPS, LRU_BLOCK_W).transpose(2, 0, 1, 3, 4)
        else:
            v = _cols_to_chips(g[k][None]) if k in COL_SHARDED else _rows_to_chips(g[k][None])
        out.append(v.reshape(N_CHIPS, 2, -1, v.shape[-1]))
    conv_w = g["hyb_conv_w"].reshape(CONV_K, N_CHIPS, -1).transpose(1, 0, 2)
    rconv_w = g["rec_conv_w"].reshape(CONV_K, N_CHIPS, -1).transpose(1, 0, 2)
    vecs = [g[k].reshape(N_CHIPS, -1) for k in ("rec_conv_b", "rec_b_a", "rec_b_x", "rec_lambda")]
    out.append(_pack_mid(conv_w, rconv_w, *vecs).reshape(N_CHIPS, 2, MID_ROWS // 2, D_MODEL))
    return out


def kernel(x, hyb_w_in, hyb_sinks, hyb_conv_w, hyb_a_log, hyb_dt_bias, hyb_norm_w, hyb_w_out, rec_w_in, rec_conv_w, rec_conv_b, rec_w_a, rec_b_a, rec_w_x, rec_b_x, rec_lambda, rec_w_out, ln1_g, ln1_b, mlp_w1, mlp_w2, ln2_g, ln2_b, loss_target, m_hyb_w_in, m_hyb_sinks, m_hyb_conv_w, m_hyb_a_log, m_hyb_dt_bias, m_hyb_norm_w, m_hyb_w_out, m_rec_w_in, m_rec_conv_w, m_rec_conv_b, m_rec_w_a, m_rec_b_a, m_rec_w_x, m_rec_b_x, m_rec_lambda, m_rec_w_out, m_ln1_g, m_ln1_b, m_mlp_w1, m_mlp_w2, m_ln2_g, m_ln2_b, v_hyb_w_in, v_hyb_sinks, v_hyb_conv_w, v_hyb_a_log, v_hyb_dt_bias, v_hyb_norm_w, v_hyb_w_out, v_rec_w_in, v_rec_conv_w, v_rec_conv_b, v_rec_w_a, v_rec_b_a, v_rec_w_x, v_rec_b_x, v_rec_lambda, v_rec_w_out, v_ln1_g, v_ln1_b, v_mlp_w1, v_mlp_w2, v_ln2_g, v_ln2_b):
    args = locals()
    w = {k: args[k] for k in WEIGHTS}
    m = {k: args["m_" + k] for k in WEIGHTS}
    v = {k: args["v_" + k] for k in WEIGHTS}

    W, mlp_w, mixer_ready = _gather_full_weights(w)

    core = lax.axis_index("c").astype(jnp.int32)
    me = (2 * lax.axis_index("x") + lax.axis_index("y")).astype(jnp.int32)
    slots = jnp.arange(N_CHIPS, dtype=jnp.int32)
    where = jnp.concatenate([me[None], jnp.where(slots == me, (slots + 1) % N_CHIPS, slots), core[None]])
    state = {}

    def on_group(group, gh, gr, d_w1, d_w2):
        by_chip = _group_by_chip(gh, gr, d_w1, d_w2)
        from_sibling = _rs_to_sibling(by_chip, "rs_to_sibling")
        pair = [_pair_sum(g, r, core[None], "pair_sum") for g, r in zip(by_chip, from_sibling)]
        recv = [lax.empty(p.shape, p.dtype) for p in pair]
        state[group] = _rs_across_start(pair, recv, "rs_across_start_%d" % group)
        return state[group][4][0, 0]

    def finish_group(group, after, prev):
        send_sems, recv_sems, pair, recv, _ = state[group]
        pair, from_chips = _rs_across_wait(send_sems, recv_sems, pair, recv, after, "rs_across_wait_%d" % group)
        half = [_chip_sum(r, p, where, "chip_sum") for r, p in zip(from_chips, pair)]
        joined = _rs_join_halves(half, "rs_join_halves")
        outs = {}
        for k, g in zip(BIG, joined[:-1]):
            n = g.shape[-1]
            outs[k] = _adamw_half(w[k].reshape(-1, n), g.reshape(-1, n), m[k].reshape(-1, n), v[k].reshape(-1, n),
                                  group, None if prev is None else prev[k], "adamw")
        return outs, joined[-1]

    loss, dx, grads = _local_step(x[0], loss_target[0], W, mlp_w, mixer_ready, on_group)
    loss = lax.psum(loss[0, 0], ("x", "y", "c"))
    g_out, d_out, m_out, v_out = {}, {}, {}, {}

    upper, mid_upper = finish_group(1, state[0][4], None)
    small_g = _pack_small(*[grads[k] for k in SMALL])
    small_all = _all_gather_small(small_g, "all_gather_small")
    sw, sm, sv = (_pack_small(*[t[k] for k in SMALL]) for t in (w, m, v))
    sg, sd, snm, snv = _adamw_small(sw, small_all, sm, sv, "adamw_small")
    for dst, packed in ((g_out, sg), (d_out, sd), (m_out, snm), (v_out, snv)):
        for k, val in zip(SMALL, _unpack_small(packed)):
            dst[k] = val

    done = sum([upper[k][1][0, 0] for k in BIG], sg[0, 0]).reshape(1, 1)
    both, mid_lower = finish_group(0, done, upper)
    for k in BIG:
        shape = w[k].shape
        g_out[k], d_out[k], m_out[k], v_out[k] = (t.reshape(shape) for t in both[k])
    mid_w, mid_m, mid_v = (_pack_mid(*[t[k] for k in MID]).reshape(-1, D_MODEL) for t in (w, m, v))
    mid_g = jnp.concatenate([mid_lower.reshape(-1, D_MODEL), mid_upper.reshape(-1, D_MODEL)])
    mid_d, mid_nm, mid_nv = _adamw(mid_w, mid_g, mid_m, mid_v, "adamw_mid")
    for dst, packed in ((g_out, mid_g), (d_out, mid_d), (m_out, mid_nm), (v_out, mid_nv)):
        for k, val in zip(MID, _unpack_mid(packed.reshape(2, MID_ROWS, D_MODEL))):
            dst[k] = val.reshape(w[k].shape)

    return (loss, dx[None], *[g_out[k] for k in WEIGHTS], *[d_out[k] for k in WEIGHTS],
            *[m_out[k] for k in WEIGHTS], *[v_out[k] for k in WEIGHTS])
```

```python
import functools

import jax
import jax.numpy as jnp
import numpy as np
from jax import lax
from jax.experimental import pallas as pl
from jax.experimental.pallas import tpu as pltpu

F32 = jnp.float32
MXU_DTYPE = jnp.bfloat16
ACT_DTYPE = jnp.bfloat16
ICI_DTYPE = jnp.bfloat16

D_MODEL = 1024
DEPTH = 4
A_HEAD_DIM = 64
A_Q_HEADS = 8
A_KV_HEADS = 2
WINDOW = 128
ROPE_THETA = 10000.0
B_HEADS = 4
B_HEAD_DIM = 128
B_CHUNK = 64
CONV_K = 4
LRU_BLOCKS = 4
LRU_BLOCK_W = D_MODEL // LRU_BLOCKS
LRU_C = 8.0
D_FF = 4 * D_MODEL
A_Q_W = A_Q_HEADS * A_HEAD_DIM
A_KV_W = A_KV_HEADS * A_HEAD_DIM
B_W = B_HEADS * B_HEAD_DIM
HYB_PROJ = A_Q_W + 2 * A_KV_W + 4 * B_W + 2 * B_HEADS
DN_ALPHA = (2 * DEPTH) ** 0.25
LN_EPS = 1e-5
NORM_EPS = 1e-6
ADAM_LR = 0.001
ADAM_B1 = 0.9
ADAM_B2 = 0.999
ADAM_EPS = 1e-08
ADAM_WD = 0.01
ADAM_STEP = 10

HP_Q = 0
HP_QKVB = 512
HP_Z = 2048
HP_K = 2560
HP_V = 2688
HP_BG = 2816
HYB_PAD = 3072

N_CHIPS = 4
N_DEV = 8
V7X_VMEM_LIMIT = 48 * 1024 * 1024
LANES = 128
SUBLANES = 8
NEG_BIG = -1e30

NN = (((1,), (0,)), ((), ()))
NT = (((1,), (1,)), ((), ()))
TN = (((0,), (0,)), ((), ()))


def _cparams(*sem):
    return pltpu.CompilerParams(dimension_semantics=sem, vmem_limit_bytes=V7X_VMEM_LIMIT)


def _dot(a, b, dims=NN):
    return lax.dot_general(a.astype(MXU_DTYPE), b.astype(MXU_DTYPE), dims, preferred_element_type=F32)


def _split_bf16(a):
    hi = a.astype(jnp.bfloat16)
    return hi, (a - hi.astype(F32)).astype(jnp.bfloat16)


def _dotf(a, b, dims=NN):
    ah, al = _split_bf16(a)
    bh, bl = _split_bf16(b)
    dg = functools.partial(lax.dot_general, dimension_numbers=dims, preferred_element_type=F32)
    return dg(ah, bh) + (dg(ah, bl) + dg(al, bh))


def _tile(dim, pref):
    t = min(dim, pref)
    while dim % t:
        t //= 2
    return t


def _sigmoid(x):
    return 1.0 / (1.0 + jnp.exp(-x))


def _silu(x):
    return x * _sigmoid(x)


def _dsilu(x):
    s = _sigmoid(x)
    return s * (1.0 + x * (1.0 - s))


GELU_C = 0.7978845608028654
GELU_A = 0.044715


def _gelu(x):
    return 0.5 * x * (1.0 + jnp.tanh(GELU_C * (x + GELU_A * x * x * x)))


def _dgelu(x):
    t = jnp.tanh(GELU_C * (x + GELU_A * x * x * x))
    return 0.5 * (1.0 + t) + 0.5 * x * (1.0 - t * t) * GELU_C * (1.0 + 3.0 * GELU_A * x * x)


def _matmul(a, b, mode, name, *, tm=1024, tn=1024, tk=1024, a_fn=None, epi=None, extra=None, out_dtype=F32,
            b_chips=None, out_chips=None):
    chunk = tk
    if mode == "tn":
        K, M = a.shape
        if K <= 4 * tk:
            tk, tm = K, tm // 2
    else:
        M, K = a.shape
    whole_k = False
    if b_chips is not None:
        g, b_layer = b_chips
        r, n = b.shape[2:]
        n_dim, k_dim = (r, n) if mode == "nt" else (n, r)
        N = N_CHIPS * n_dim if g == "j" else n_dim
        assert K == (N_CHIPS * k_dim if g == "k" else k_dim)
        if g == "j":
            tn = n_dim
        else:
            whole_k, tk, tm = True, K, tm // 2
    elif mode == "nt":
        N = b.shape[0]
        if tk < K <= 3 * tk:
            tk, tm = K, tm // 2
    else:
        N = b.shape[1]
    if out_chips is not None:
        og, o_layer, o_buf = out_chips
        if og == "j":
            tn = o_buf.shape[3]
        else:
            tm = min(tm, o_buf.shape[2])
    tm, tn, tk = _tile(M, tm), _tile(N, tn), _tile(K, tk)
    nk = K // tk
    if mode == "tn":
        a_spec = pl.BlockSpec((tk, tm), lambda i, j, k: (k, i))
    else:
        a_spec = pl.BlockSpec((tm, tk), lambda i, j, k: (i, k))
    b_block = (tn, tk) if mode == "nt" else (tk, tn)
    if b_chips is None:
        b_spec = pl.BlockSpec(b_block, (lambda i, j, k: (j, k)) if mode == "nt" else (lambda i, j, k: (k, j)))
    elif mode == "nt" and whole_k:
        b_spec = pl.BlockSpec((N_CHIPS, None, tn, k_dim), lambda i, j, k: (0, b_layer, j, 0))
    elif mode == "nt":
        b_spec = pl.BlockSpec((None, None) + b_block, lambda i, j, k: (j, b_layer, 0, k))
    elif whole_k:
        b_spec = pl.BlockSpec((N_CHIPS, None, k_dim, tn), lambda i, j, k: (0, b_layer, 0, j),
                              pipeline_mode=pl.Buffered(1) if N == tn else None)
    else:
        b_spec = pl.BlockSpec((None, None) + b_block, lambda i, j, k: (j, b_layer, k, 0))
    o_spec = pl.BlockSpec((tm, tn), lambda i, j, k: (i, j))
    e_spec = o_spec
    if out_chips is not None:
        per = o_buf.shape[2] // tm
        o_spec = pl.BlockSpec((None, None, tm, tn), (lambda i, j, k: (j, o_layer, i, 0)) if og == "j"
                              else (lambda i, j, k: (i // per, o_layer, i % per, j)))
    dims = {"nn": NN, "nt": NT, "tn": TN}[mode]
    extras = () if extra is None else (extra if isinstance(extra, tuple) else (extra,))
    out_dtypes = out_dtype if isinstance(out_dtype, tuple) else (out_dtype,)
    n_in = 2 + len(extras) + (out_chips is not None)

    def body(*refs):
        a_ref, b_ref = refs[0], refs[1]
        e_refs = refs[2:2 + len(extras)]
        o_refs = refs[n_in:n_in + len(out_dtypes)]
        av = a_ref[...]
        if a_fn is not None:
            av = a_fn(av)
        if whole_k and mode == "nt":
            part = _dot(av[:, :k_dim], b_ref[0], dims)
            for chip in range(1, N_CHIPS):
                part = part + _dot(av[:, chip * k_dim:(chip + 1) * k_dim], b_ref[chip], dims)
        elif mode == "tn" and tk > chunk:
            part = _dot(av[:chunk], b_ref[0:chunk, :], dims)
            for c0 in range(chunk, tk, chunk):
                part = part + _dot(av[c0:c0 + chunk], b_ref[c0:c0 + chunk, :], dims)
        else:
            bv = b_ref[...]
            if whole_k:
                bv = bv.reshape(K, tn)
            part = _dot(av, bv, dims)

        def finish(acc):
            if epi is not None:
                acc = epi(acc, *[e[...] for e in e_refs])
            for o_ref, val, dt in zip(o_refs, acc if isinstance(acc, tuple) else (acc,), out_dtypes):
                o_ref[...] = val.astype(dt)

        if nk == 1:
            finish(part)
        else:
            acc_ref = refs[-1]
            k = pl.program_id(2)

            @pl.when(k == 0)
            def _():
                acc_ref[...] = part

            @pl.when(k > 0)
            def _():
                acc_ref[...] += part

            @pl.when(k == nk - 1)
            def _():
                finish(acc_ref[...])

    row_spec = pl.BlockSpec((1, tn), lambda i, j, k: (0, j))
    in_specs = [a_spec, b_spec] + [row_spec if e.shape[0] == 1 else e_spec for e in extras]
    args = (a, b) + extras
    out_shape = [jax.ShapeDtypeStruct((M, N), dt) for dt in out_dtypes]
    out_specs = [o_spec] * len(out_dtypes)
    aliases = {}
    if out_chips is not None:
        in_specs.append(pl.BlockSpec(memory_space=pl.ANY))
        args += (o_buf,)
        out_shape = [jax.ShapeDtypeStruct(o_buf.shape, o_buf.dtype)]
        aliases = {n_in - 1: 0}
    if not isinstance(out_dtype, tuple):
        out_shape, out_specs = out_shape[0], out_specs[0]
    return pl.pallas_call(
        body, name=name, grid=(M // tm, N // tn, nk), in_specs=in_specs, out_specs=out_specs, out_shape=out_shape,
        input_output_aliases=aliases,
        scratch_shapes=[pltpu.VMEM((tm, tn), F32)] if nk > 1 else [],
        compiler_params=_cparams("parallel", "parallel", "arbitrary"),
    )(*args)


def _relu2(v):
    r = jnp.maximum(v, 0.0)
    return r * r


def _epi_drelu2(acc, h):
    return acc * (2.0 * jnp.maximum(h, 0.0))


def _epi_add_residual(acc, du):
    return acc + DN_ALPHA * du


def _epi_residual_ln(acc, x, g, b):
    u = DN_ALPHA * x + acc
    mu = jnp.mean(u, axis=-1, keepdims=True)
    d = u - mu
    var = jnp.mean(d * d, axis=-1, keepdims=True)
    o = d * lax.rsqrt(var + LN_EPS) * g + b
    return o, o, u


def _ln_out():
    return (F32, ACT_DTYPE, F32)


def _ln_bwd(u, g, dout, name):
    T, D = u.shape
    tr = _tile(T, 512)

    def body(u_ref, g_ref, d_ref, du_ref, dub_ref, dg_ref, db_ref):
        i = pl.program_id(0)
        u = u_ref[...]
        mu = jnp.mean(u, axis=-1, keepdims=True)
        d = u - mu
        rstd = lax.rsqrt(jnp.mean(d * d, axis=-1, keepdims=True) + LN_EPS)
        xhat = d * rstd
        dout_v = d_ref[...]
        dxh = dout_v * g_ref[...]
        m1 = jnp.mean(dxh, axis=-1, keepdims=True)
        m2 = jnp.mean(dxh * xhat, axis=-1, keepdims=True)
        du = rstd * (dxh - m1 - xhat * m2)
        du_ref[...] = du
        dub_ref[...] = du.astype(ACT_DTYPE)
        pg = jnp.sum(dout_v * xhat, axis=0, keepdims=True)
        pb = jnp.sum(dout_v, axis=0, keepdims=True)

        @pl.when(i == 0)
        def _():
            dg_ref[...] = pg
            db_ref[...] = pb

        @pl.when(i > 0)
        def _():
            dg_ref[...] += pg
            db_ref[...] += pb

    row = pl.BlockSpec((tr, D), lambda i: (i, 0))
    vec = pl.BlockSpec((1, D), lambda i: (0, 0))
    return pl.pallas_call(
        body, name=name, grid=(T // tr,), in_specs=[row, vec, row], out_specs=[row, row, vec, vec],
        out_shape=[jax.ShapeDtypeStruct((T, D), F32), jax.ShapeDtypeStruct((T, D), ACT_DTYPE),
                   jax.ShapeDtypeStruct((1, D), F32), jax.ShapeDtypeStruct((1, D), F32)],
        compiler_params=_cparams("arbitrary"),
    )(u, g, dout)


def _loss_head(y, tgt, name):
    T, D = y.shape
    tr = _tile(T, 512)

    def body(y_ref, t_ref, dy_ref, l_ref):
        i = pl.program_id(0)
        e = y_ref[...] - t_ref[...]
        dy_ref[...] = e * (1.0 / D)
        part = jnp.sum(e * e, axis=(0, 1), keepdims=True) * (0.5 / D)

        @pl.when(i == 0)
        def _():
            l_ref[...] = part

        @pl.when(i > 0)
        def _():
            l_ref[...] += part

    row = pl.BlockSpec((tr, D), lambda i: (i, 0))
    one = pl.BlockSpec((1, 1), lambda i: (0, 0))
    return pl.pallas_call(
        body, name=name, grid=(T // tr,), in_specs=[row, row], out_specs=[row, one],
        out_shape=[jax.ShapeDtypeStruct((T, D), F32), jax.ShapeDtypeStruct((1, 1), F32)],
        compiler_params=_cparams("arbitrary"),
    )(y, tgt)


def _swap_half(x):
    n = x.shape[-1]
    lane = lax.broadcasted_iota(jnp.int32, x.shape, 1)
    first = (lane % A_HEAD_DIM) < (A_HEAD_DIM // 2)
    return jnp.where(first, pltpu.roll(x, n - A_HEAD_DIM // 2, axis=1), pltpu.roll(x, A_HEAD_DIM // 2, axis=1))


def _rope(x, cos, sin_signed):
    return x * cos + _swap_half(x) * sin_signed


def _rope_t(dy, cos, sin_signed):
    return dy * cos + _swap_half(dy * sin_signed)


def _rope_tables(T):
    half = A_HEAD_DIM // 2
    inv_freq = np.float32(ROPE_THETA) ** (-np.arange(half, dtype=np.float32) / np.float32(half))
    ang = np.arange(T, dtype=np.float32)[:, None] * inv_freq[None, :]
    cos = np.tile(np.cos(ang), (1, 4))
    sin = np.sin(ang)
    sin_signed = np.tile(np.concatenate([-sin, sin], axis=1), (1, 2))
    return jnp.asarray(cos, F32), jnp.asarray(sin_signed, F32)


def _band_mask(n):
    qi = lax.broadcasted_iota(jnp.int32, (WINDOW, 2 * WINDOW), 0)
    kj = lax.broadcasted_iota(jnp.int32, (WINDOW, 2 * WINDOW), 1)
    return (kj > qi) & (kj <= qi + WINDOW) & ((n > 0) | (kj >= WINDOW))


def _place(v, src_half, dst_half):
    lane = lax.broadcasted_iota(jnp.int32, v.shape, 1)
    if src_half != dst_half:
        v = pltpu.roll(v, A_HEAD_DIM, axis=1)
    keep = (lane >= A_HEAD_DIM) if dst_half else (lane < A_HEAD_DIM)
    return jnp.where(keep, v, 0.0)


def _attn_specs():
    kb, vb = HP_K // LANES, HP_V // LANES
    prev = lambda n: jnp.maximum(n - 1, 0)
    return dict(
        q=pl.BlockSpec((WINDOW, A_Q_W), lambda n: (n, 0)),
        kc=pl.BlockSpec((WINDOW, LANES), lambda n: (n, kb)),
        kp=pl.BlockSpec((WINDOW, LANES), lambda n: (prev(n), kb)),
        vc=pl.BlockSpec((WINDOW, LANES), lambda n: (n, vb)),
        vp=pl.BlockSpec((WINDOW, LANES), lambda n: (prev(n), vb)),
        tq=pl.BlockSpec((WINDOW, LANES), lambda n: (n, 0)),
        tp=pl.BlockSpec((WINDOW, LANES), lambda n: (prev(n), 0)),
        sink=pl.BlockSpec((A_Q_HEADS, LANES), lambda n: (0, 0)),
        row512=pl.BlockSpec((WINDOW, A_Q_W), lambda n: (n, 0)),
        row128=pl.BlockSpec((WINDOW, LANES), lambda n: (n, 0)),
        lse=pl.BlockSpec((WINDOW, A_Q_HEADS), lambda n: (n, 0)),
    )


def _attn_fwd(proj, cos, sin_s, sinks_b, name):
    T = proj.shape[0]
    sp = _attn_specs()

    def body(q_ref, kc_ref, kp_ref, vc_ref, vp_ref, cq_ref, sq_ref, cp_ref, sp_ref, sink_ref, o_ref, l_ref):
        n = pl.program_id(0)
        cq, sq = cq_ref[...], sq_ref[...]
        cq4, sq4 = jnp.tile(cq, (1, A_Q_W // LANES)), jnp.tile(sq, (1, A_Q_W // LANES))
        q = _rope(q_ref[...], cq4, sq4) * (A_HEAD_DIM ** -0.5)
        kc = _rope(kc_ref[...], cq, sq)
        kp = _rope(kp_ref[...], cp_ref[...], sp_ref[...])
        kk = jnp.concatenate([kp, kc], axis=0)
        vv = jnp.concatenate([vp_ref[...], vc_ref[...]], axis=0)
        mask = _band_mask(n)[None]
        lane = lax.broadcasted_iota(jnp.int32, (WINDOW, LANES), 1)
        lane8 = lax.broadcasted_iota(jnp.int32, (WINDOW, A_Q_HEADS), 1)
        qe = jnp.stack([_place(q[:, (hq // 2) * LANES:(hq // 2 + 1) * LANES], hq % 2, hq // 4) for hq in range(A_Q_HEADS)])
        kk8 = jnp.broadcast_to(kk[None], (A_Q_HEADS,) + kk.shape)
        vv8 = jnp.broadcast_to(vv[None], (A_Q_HEADS,) + vv.shape)
        sk = jnp.stack([sink_ref[hq:hq + 1, 0:1] for hq in range(A_Q_HEADS)])
        s = jnp.where(mask, _dot(qe, kk8, BNT), NEG_BIG)
        m = jnp.maximum(jnp.max(s, axis=-1, keepdims=True), sk)
        p = jnp.exp(s - m)
        den = jnp.sum(p, axis=-1, keepdims=True) + jnp.exp(sk - m)
        o = _dot(p * (1.0 / den), vv8, BNN)
        lse_h = m + jnp.log(den)
        outs = []
        lse = jnp.zeros((WINDOW, A_Q_HEADS), F32)
        for pb in range(A_Q_HEADS // 2):
            halves = [_place(o[2 * pb + e], pb // 2, e) for e in range(2)]
            outs.append(jnp.where(lane < A_HEAD_DIM, halves[0], halves[1]))
            for e in range(2):
                lse = jnp.where(lane8 == 2 * pb + e, lse_h[2 * pb + e], lse)
        o_ref[...] = jnp.concatenate(outs, axis=1)
        l_ref[...] = lse

    return pl.pallas_call(
        body, name=name, grid=(T // WINDOW,),
        in_specs=[sp["q"], sp["kc"], sp["kp"], sp["vc"], sp["vp"], sp["tq"], sp["tq"], sp["tp"], sp["tp"], sp["sink"]],
        out_specs=[sp["row512"], sp["lse"]],
        out_shape=[jax.ShapeDtypeStruct((T, A_Q_W), F32), jax.ShapeDtypeStruct((T, A_Q_HEADS), F32)],
        compiler_params=_cparams("parallel"),
    )(proj, proj, proj, proj, proj, cos, sin_s, cos, sin_s, sinks_b)


def _attn_bwd(proj, cos, sin_s, sinks_b, o, lse, dmix, name):
    T = proj.shape[0]
    sp = _attn_specs()

    def body(q_ref, kc_ref, kp_ref, vc_ref, vp_ref, cq_ref, sq_ref, cp_ref, sp_ref, sink_ref, o_ref, l_ref, do_ref,
             dq_ref, dkc_ref, dkp_ref, dvc_ref, dvp_ref, dsink_ref):
        n = pl.program_id(0)
        cq, sq = cq_ref[...], sq_ref[...]
        cp, sps = cp_ref[...], sp_ref[...]
        cq4, sq4 = jnp.tile(cq, (1, A_Q_W // LANES)), jnp.tile(sq, (1, A_Q_W // LANES))
        q = _rope(q_ref[...], cq4, sq4) * (A_HEAD_DIM ** -0.5)
        kc = _rope(kc_ref[...], cq, sq)
        kp = _rope(kp_ref[...], cp, sps)
        kk = jnp.concatenate([kp, kc], axis=0)
        vv = jnp.concatenate([vp_ref[...], vc_ref[...]], axis=0)
        mask = _band_mask(n)[None]
        lane = lax.broadcasted_iota(jnp.int32, (WINDOW, LANES), 1)
        do_all, o_all, l_all = do_ref[...], o_ref[...], l_ref[...]
        lane8 = lax.broadcasted_iota(jnp.int32, (WINDOW, A_Q_HEADS), 1)
        head8 = lax.broadcasted_iota(jnp.int32, (1, A_Q_HEADS), 1)
        prod = do_all * o_all
        qes, does, deltas, lhs = [], [], [], []
        for hq in range(A_Q_HEADS):
            pb, e, kvh = hq // 2, hq % 2, hq // 4
            blk = slice(pb * LANES, (pb + 1) * LANES)
            in_half = (lane >= A_HEAD_DIM) if e else (lane < A_HEAD_DIM)
            deltas.append(jnp.sum(jnp.where(in_half, prod[:, blk], 0.0), axis=-1, keepdims=True))
            qes.append(_place(q[:, blk], e, kvh))
            does.append(_place(do_all[:, blk], e, kvh))
            lhs.append(jnp.sum(jnp.where(lane8 == hq, l_all, 0.0), axis=-1, keepdims=True))
        qe, doe, delta, lh = jnp.stack(qes), jnp.stack(does), jnp.stack(deltas), jnp.stack(lhs)
        kk8 = jnp.broadcast_to(kk[None], (A_Q_HEADS,) + kk.shape)
        vv8 = jnp.broadcast_to(vv[None], (A_Q_HEADS,) + vv.shape)
        sk = jnp.stack([sink_ref[hq:hq + 1, 0:1] for hq in range(A_Q_HEADS)])
        s = _dot(qe, kk8, BNT)
        p = jnp.where(mask, jnp.exp(jnp.where(mask, s, NEG_BIG) - lh), 0.0)
        dvv = jnp.sum(_dot(p, doe, BTN), axis=0)
        ds = p * (_dot(doe, vv8, BNT) - delta)
        dkk = jnp.sum(_dot(ds, qe, BTN), axis=0)
        dqe = _dot(ds, kk8, BNN)
        dsink_h = -jnp.sum(jnp.exp(sk - lh) * delta, axis=(1, 2), keepdims=True)
        dqs = []
        dsk = jnp.zeros((1, A_Q_HEADS), F32)
        for pb in range(A_Q_HEADS // 2):
            halves = [_place(dqe[2 * pb + e], pb // 2, e) for e in range(2)]
            dqs.append(jnp.where(lane < A_HEAD_DIM, halves[0], halves[1]))
            for e in range(2):
                dsk = jnp.where(head8 == 2 * pb + e, dsink_h[2 * pb + e], dsk)
        dq = jnp.concatenate(dqs, axis=1) * (A_HEAD_DIM ** -0.5)
        dq_ref[...] = _rope_t(dq, cq4, sq4).astype(ACT_DTYPE)
        dkp_ref[...] = _rope_t(dkk[:WINDOW], cp, sps)
        dkc_ref[...] = _rope_t(dkk[WINDOW:], cq, sq)
        dvp_ref[...] = dvv[:WINDOW]
        dvc_ref[...] = dvv[WINDOW:]

        @pl.when(n == 0)
        def _():
            dsink_ref[...] = dsk

        @pl.when(n > 0)
        def _():
            dsink_ref[...] += dsk

    return pl.pallas_call(
        body, name=name, grid=(T // WINDOW,),
        in_specs=[sp["q"], sp["kc"], sp["kp"], sp["vc"], sp["vp"], sp["tq"], sp["tq"], sp["tp"], sp["tp"], sp["sink"],
                  sp["row512"], sp["lse"], sp["row512"]],
        out_specs=[sp["row512"], sp["row128"], sp["row128"], sp["row128"], sp["row128"],
                   pl.BlockSpec((1, A_Q_HEADS), lambda n: (0, 0))],
        out_shape=[jax.ShapeDtypeStruct((T, HYB_PAD), ACT_DTYPE)] + [jax.ShapeDtypeStruct((T, LANES), F32)] * 4
        + [jax.ShapeDtypeStruct((1, A_Q_HEADS), F32)],
        compiler_params=_cparams("arbitrary"),
    )(proj, proj, proj, proj, proj, cos, sin_s, cos, sin_s, sinks_b, o, lse, dmix)


def _shift_down(x, prev8, k):
    if k == 0:
        return x
    row = lax.broadcasted_iota(jnp.int32, prev8.shape, 0)
    r = pltpu.roll(x, k, axis=0)
    top = jnp.where(row < k, pltpu.roll(prev8, k, axis=0), r[:SUBLANES])
    return jnp.concatenate([top, r[SUBLANES:]], axis=0)


def _shift_up(x, next8, k):
    if k == 0:
        return x
    R = x.shape[0]
    row = lax.broadcasted_iota(jnp.int32, next8.shape, 0)
    r = pltpu.roll(x, R - k, axis=0)
    bot = jnp.where(row >= SUBLANES - k, pltpu.roll(next8, SUBLANES - k, axis=0), r[R - SUBLANES:])
    return jnp.concatenate([r[:R - SUBLANES], bot], axis=0)


def _conv(x, prev8, w):
    y = x * w[CONV_K - 1:CONV_K]
    for j in range(CONV_K - 1):
        y = y + _shift_down(x, prev8, CONV_K - 1 - j) * w[j:j + 1]
    return y


def _conv_bwd(x, prev8, w, dy, next8_dy):
    dx = dy * w[CONV_K - 1:CONV_K]
    dws = []
    for j in range(CONV_K - 1):
        k = CONV_K - 1 - j
        dx = dx + _shift_up(dy, next8_dy, k) * w[j:j + 1]
        dws.append(jnp.sum(dy * _shift_down(x, prev8, k), axis=0, keepdims=True))
    dws.append(jnp.sum(dy * x, axis=0, keepdims=True))
    return dx, dws


def _dnconv_fwd(proj, conv_w, name):
    T = proj.shape[0]
    R = _tile(T, 512)
    cb0 = HP_QKVB // A_Q_W

    def body(x_ref, w_ref, o_ref, prev_ref):
        i = pl.program_id(1)

        @pl.when(i == 0)
        def _():
            prev_ref[...] = jnp.zeros_like(prev_ref)

        x = x_ref[...]
        o_ref[...] = _silu(_conv(x, prev_ref[...], w_ref[...]))
        prev_ref[...] = x[R - SUBLANES:]

    return pl.pallas_call(
        body, name=name, grid=(3, T // R),
        in_specs=[pl.BlockSpec((R, B_W), lambda j, i: (i, cb0 + j)), pl.BlockSpec((CONV_K, B_W), lambda j, i: (0, j))],
        out_specs=pl.BlockSpec((R, B_W), lambda j, i: (i, j)),
        out_shape=jax.ShapeDtypeStruct((T, 3 * B_W), F32),
        scratch_shapes=[pltpu.VMEM((SUBLANES, B_W), F32)],
        compiler_params=_cparams("parallel", "arbitrary"),
    )(proj, conv_w)


def _dnconv_bwd(proj, conv_w, dy, dproj, name):
    T = proj.shape[0]
    R = _tile(T, 512)
    nb = T // R
    cb0 = HP_QKVB // A_Q_W
    r8 = R // SUBLANES

    def body(x_ref, xp_ref, w_ref, dy_ref, dproj_in, dx_ref, dw_ref, next_ref):
        i = pl.program_id(1)
        blk = nb - 1 - i

        @pl.when(i == 0)
        def _():
            next_ref[...] = jnp.zeros_like(next_ref)

        x = x_ref[...]
        prev8 = jnp.where(blk > 0, xp_ref[...], 0.0)
        w = w_ref[...]
        dpre = dy_ref[...] * _dsilu(_conv(x, prev8, w))
        dx, dw = _conv_bwd(x, prev8, w, dpre, next_ref[...])
        dx_ref[...] = dx.astype(ACT_DTYPE)
        next_ref[...] = dpre[:SUBLANES]

        @pl.when(i == 0)
        def _():
            for j in range(CONV_K):
                dw_ref[j:j + 1, :] = dw[j]

        @pl.when(i > 0)
        def _():
            for j in range(CONV_K):
                dw_ref[j:j + 1, :] += dw[j]

    return pl.pallas_call(
        body, name=name, grid=(3, nb),
        in_specs=[pl.BlockSpec((R, B_W), lambda j, i: (nb - 1 - i, cb0 + j)),
                  pl.BlockSpec((SUBLANES, B_W), lambda j, i: (jnp.maximum((nb - 1 - i) * r8 - 1, 0), cb0 + j)),
                  pl.BlockSpec((CONV_K, B_W), lambda j, i: (0, j)),
                  pl.BlockSpec((R, B_W), lambda j, i: (nb - 1 - i, j)), ANY],
        out_specs=[pl.BlockSpec((R, B_W), lambda j, i: (nb - 1 - i, cb0 + j)),
                   pl.BlockSpec((CONV_K, B_W), lambda j, i: (0, j))],
        out_shape=[jax.ShapeDtypeStruct(dproj.shape, dproj.dtype), jax.ShapeDtypeStruct((CONV_K, 3 * B_W), F32)],
        input_output_aliases={4: 0},
        scratch_shapes=[pltpu.VMEM((SUBLANES, B_W), F32)],
        compiler_params=_cparams("parallel", "arbitrary"),
    )(proj, proj, conv_w, dy, dproj)


DK_SCALE = B_HEAD_DIM ** -0.5


BNN = (((2,), (1,)), ((0,), (0,)))
BNT = (((2,), (2,)), ((0,), (0,)))
BTN = (((1,), (1,)), ((0,), (0,)))


def _tri_inv(a):
    C = a.shape[-1]
    ri = lax.broadcasted_iota(jnp.int32, (C, C), 0)
    ci = lax.broadcasted_iota(jnp.int32, (C, C), 1)
    x = jnp.where(ri == ci, 1.0, 0.0)[None] - a
    p = _dotf(a, a, BNN)
    span = 2
    while span < C:
        x = x + _dotf(x, p, BNN)
        span *= 2
        if span < C:
            p = _dotf(p, p, BNN)
    return x


def _dn_chunk(qc, kc, v, gcol, grow, bcol, s0, tm=None):
    C = B_CHUNK
    ri = lax.broadcasted_iota(jnp.int32, (C, C), 0)
    ci = lax.broadcasted_iota(jnp.int32, (C, C), 1)
    incl, strict = (ri >= ci)[None], (ri > ci)[None]
    rq = lax.rsqrt(jnp.sum(qc * qc, axis=-1, keepdims=True) + NORM_EPS)
    rk = lax.rsqrt(jnp.sum(kc * kc, axis=-1, keepdims=True) + NORM_EPS)
    qn = qc * rq
    q = qn * DK_SCALE
    k = kc * rk
    gc_col = jnp.sum(jnp.where(incl, grow, 0.0), axis=2, keepdims=True)
    gc_row = jnp.sum(jnp.where((ri <= ci)[None], gcol, 0.0), axis=1, keepdims=True)
    gl = jnp.sum(gcol, axis=1, keepdims=True)
    dincl = jnp.where(incl, jnp.exp(jnp.where(incl, gc_col - gc_row, 0.0)), 0.0)
    dstrict = jnp.where(strict, dincl, 0.0)
    eg = jnp.exp(gc_col)
    ekt = jnp.exp(gl - gc_col)
    egl = jnp.exp(gl)
    kb = k * bcol
    vb = v * bcol
    kbg = kb * eg
    a = _dot(kb, k, BNT) * dstrict
    if tm is None:
        tm = _tri_inv(a)
    u = _dot(tm, vb, BNN)
    w = _dot(tm, kbg, BNN)
    vn = u - _dot(w, s0, BNN)
    qk = _dot(q, k, BNT) * dincl
    qg = q * eg
    kt = k * ekt
    o = _dot(qg, s0, BNN) + _dot(qk, vn, BNN)
    s1 = s0 * egl + _dot(kt, vn, BTN)
    return dict(rq=rq, rk=rk, qn=qn, q=q, k=k, dincl=dincl, dstrict=dstrict, eg=eg, ekt=ekt, egl=egl, kb=kb, vb=vb,
                kbg=kbg, a=a, tm=tm, w=w, vn=vn, qk=qk, qg=qg, kt=kt, o=o, s1=s1, ri=ri[None], ci=ci[None])


def _heads(ref):
    return jnp.stack([ref[:, h * B_HEAD_DIM:(h + 1) * B_HEAD_DIM] for h in range(B_HEADS)])


def _store_heads(ref, val):
    for h in range(B_HEADS):
        ref[:, h * B_HEAD_DIM:(h + 1) * B_HEAD_DIM] = val[h]


def _dn_specs(N, rev):
    ix = (lambda n: N - 1 - n) if rev else (lambda n: n)
    wide = lambda cb: pl.BlockSpec((B_CHUNK, B_W), lambda n: (ix(n), cb))
    return dict(
        q=wide(0), k=wide(1), v=wide(2), z=wide(HP_Z // B_W), dob=wide(A_Q_W // B_W), out=wide(0),
        nw=pl.BlockSpec((1, LANES), lambda n: (0, 0)),
        row=pl.BlockSpec((B_HEADS, None, 1, B_CHUNK), lambda n: (0, ix(n), 0, 0)),
        state=pl.BlockSpec((B_HEADS, None, B_HEAD_DIM, B_HEAD_DIM), lambda n: (0, ix(n), 0, 0)),
        inv=pl.BlockSpec((B_HEADS, None, B_CHUNK, B_CHUNK), lambda n: (0, ix(n), 0, 0)),
    )


def _to_col(row):
    C = row.shape[-1]
    eye = lax.broadcasted_iota(jnp.int32, (C, C), 0) == lax.broadcasted_iota(jnp.int32, (C, C), 1)
    return jnp.sum(jnp.where(eye[None], row, 0.0), axis=2, keepdims=True)


def _to_row(col):
    C = col.shape[1]
    eye = lax.broadcasted_iota(jnp.int32, (C, C), 0) == lax.broadcasted_iota(jnp.int32, (C, C), 1)
    return jnp.sum(jnp.where(eye[None], col, 0.0), axis=1, keepdims=True)


def _dn_fwd(qkvc, proj, norm_w, grow, brow, name):
    T = qkvc.shape[0]
    N = T // B_CHUNK
    sp = _dn_specs(N, False)

    def body(q_ref, k_ref, v_ref, z_ref, nw_ref, gr_ref, br_ref, o_ref, st_ref, tm_ref, s_ref):
        n = pl.program_id(0)

        @pl.when(n == 0)
        def _():
            s_ref[...] = jnp.zeros_like(s_ref)

        s0 = s_ref[...]
        st_ref[...] = s0
        grow_v = gr_ref[...]
        f = _dn_chunk(_heads(q_ref), _heads(k_ref), _heads(v_ref), _to_col(grow_v), grow_v, _to_col(br_ref[...]), s0)
        o = f["o"]
        r = lax.rsqrt(jnp.mean(o * o, axis=-1, keepdims=True) + NORM_EPS)
        _store_heads(o_ref, o * r * nw_ref[...][None] * _silu(_heads(z_ref)))
        s_ref[...] = f["s1"]
        tm_ref[...] = f["tm"]

    return pl.pallas_call(
        body, name=name, grid=(N,),
        in_specs=[sp["q"], sp["k"], sp["v"], sp["z"], sp["nw"], sp["row"], sp["row"]],
        out_specs=[sp["out"], sp["state"], sp["inv"]],
        out_shape=[jax.ShapeDtypeStruct((T, B_W), F32),
                   jax.ShapeDtypeStruct((B_HEADS, N, B_HEAD_DIM, B_HEAD_DIM), F32),
                   jax.ShapeDtypeStruct((B_HEADS, N, B_CHUNK, B_CHUNK), F32)],
        scratch_shapes=[pltpu.VMEM((B_HEADS, B_HEAD_DIM, B_HEAD_DIM), F32)],
        compiler_params=_cparams("arbitrary"),
    )(qkvc, qkvc, qkvc, proj, norm_w, grow, brow)


def _dn_bwd(qkvc, proj, norm_w, grow, brow, states, invs, dmix, dproj, name):
    T = qkvc.shape[0]
    N = T // B_CHUNK
    sp = _dn_specs(N, True)
    C = B_CHUNK

    def body(q_ref, k_ref, v_ref, z_ref, nw_ref, gr_ref, br_ref, st_ref, tm_ref, dob_ref, dproj_in,
             dqkv_ref, dz_ref, dg_ref, db_ref, dnw_ref, ds_ref):
        n = pl.program_id(0)
        dq_ref = dqkv_ref.at[:, 0:B_W]
        dk_ref = dqkv_ref.at[:, B_W:2 * B_W]
        dv_ref = dqkv_ref.at[:, 2 * B_W:3 * B_W]

        @pl.when(n == 0)
        def _():
            ds_ref[...] = jnp.zeros_like(ds_ref)
            dnw_ref[...] = jnp.zeros_like(dnw_ref)

        s0 = st_ref[...]
        ds1 = ds_ref[...]
        v, z, nw, bcol_v = _heads(v_ref), _heads(z_ref), nw_ref[...][None], _to_col(br_ref[...])
        grow_v = gr_ref[...]
        f = _dn_chunk(_heads(q_ref), _heads(k_ref), v, _to_col(grow_v), grow_v, bcol_v, s0, tm=tm_ref[...])
        o, q, k, qn = f["o"], f["q"], f["k"], f["qn"]
        eg, ekt, egl = f["eg"], f["ekt"], f["egl"]
        tm, w, vn, kb, vb, kbg = f["tm"], f["w"], f["vn"], f["kb"], f["vb"], f["kbg"]
        qg, kt, qk, a = f["qg"], f["kt"], f["qk"], f["a"]
        ri, ci = f["ri"], f["ci"]

        dob_v = _heads(dob_ref)
        r = lax.rsqrt(jnp.mean(o * o, axis=-1, keepdims=True) + NORM_EPS)
        sz = _silu(z)
        on = o * r
        dnw_ref[...] += jnp.sum(dob_v * sz * on, axis=1, keepdims=True)
        _store_heads(dz_ref, (dob_v * on * nw * _dsilu(z)).astype(ACT_DTYPE))
        d_on = dob_v * sz * nw
        do = r * (d_on - on * jnp.mean(d_on * on, axis=-1, keepdims=True))

        dvn = _dot(qk, do, BTN) + _dot(kt, ds1, BNN)
        dqk = _dot(do, vn, BNT)
        dqg = _dot(do, s0, BNT)
        ds_ref[...] = _dot(qg, do, BTN) + egl * ds1 - _dot(w, dvn, BTN)
        dgl = jnp.sum(s0 * ds1, axis=(1, 2), keepdims=True) * egl
        dkt = _dot(vn, ds1, BNT)
        dw = -_dot(dvn, s0, BNT)
        dq = dqg * eg
        dgc = jnp.sum(dqg * qg, axis=-1, keepdims=True)
        dk = dkt * ekt
        t_kt = jnp.sum(dkt * kt, axis=-1, keepdims=True)
        dgl = dgl + jnp.sum(t_kt, axis=1, keepdims=True)
        dgc = dgc - t_kt
        dqkr = dqk * f["dincl"]
        dq = dq + _dot(dqkr, k, BNN)
        dk = dk + _dot(dqkr, q, BTN)
        e_qk = dqk * qk
        dgc = dgc + jnp.sum(e_qk, axis=-1, keepdims=True)
        dgc_row = -jnp.sum(e_qk, axis=1, keepdims=True)
        dtm = _dot(dvn, vb, BNT) + _dot(dw, kbg, BNT)
        dvb = _dot(tm, dvn, BTN)
        dkbg = _dot(tm, dw, BTN)
        dkb = dkbg * eg
        dgc = dgc + jnp.sum(dkbg * kbg, axis=-1, keepdims=True)
        da = -_dotf(tm, _dotf(dtm, tm, BNT), BTN)
        dkk = da * f["dstrict"]
        e_a = da * a
        dgc = dgc + jnp.sum(e_a, axis=-1, keepdims=True)
        dgc_row = dgc_row - jnp.sum(e_a, axis=1, keepdims=True)
        dkb = dkb + _dot(dkk, k, BNN)
        dk = dk + _dot(dkk, kb, BTN)
        dk = dk + dkb * bcol_v
        db_ref[...] = _to_row(jnp.sum(dkb * k, axis=-1, keepdims=True) + jnp.sum(dvb * v, axis=-1, keepdims=True))
        _store_heads(dv_ref, dvb * bcol_v)
        dgc_row = dgc_row + jnp.sum(jnp.where(ri == ci, dgc, 0.0), axis=1, keepdims=True)
        dg_ref[...] = jnp.sum(jnp.where(ci <= ri, _to_col(dgc_row), 0.0), axis=1, keepdims=True) + dgl
        dqs = dq * DK_SCALE
        _store_heads(dq_ref, f["rq"] * (dqs - qn * jnp.sum(dqs * qn, axis=-1, keepdims=True)))
        _store_heads(dk_ref, f["rk"] * (dk - k * jnp.sum(dk * k, axis=-1, keepdims=True)))

    return pl.pallas_call(
        body, name=name, grid=(N,),
        in_specs=[sp["q"], sp["k"], sp["v"], sp["z"], sp["nw"], sp["row"], sp["row"], sp["state"], sp["inv"], sp["dob"],
                  ANY],
        out_specs=[pl.BlockSpec((C, 3 * B_W), lambda n: (N - 1 - n, 0)), sp["z"], sp["row"], sp["row"],
                   pl.BlockSpec((B_HEADS, 1, LANES), lambda n: (0, 0, 0))],
        out_shape=[jax.ShapeDtypeStruct((T, 3 * B_W), F32), jax.ShapeDtypeStruct(dproj.shape, dproj.dtype),
                   jax.ShapeDtypeStruct((B_HEADS, N, 1, C), F32), jax.ShapeDtypeStruct((B_HEADS, N, 1, C), F32),
                   jax.ShapeDtypeStruct((B_HEADS, 1, LANES), F32)],
        input_output_aliases={10: 1},
        scratch_shapes=[pltpu.VMEM((B_HEADS, B_HEAD_DIM, B_HEAD_DIM), F32)],
        compiler_params=_cparams("arbitrary"),
    )(qkvc, qkvc, qkvc, proj, norm_w, grow, brow, states, invs, dmix, dproj)


def _lru_gates(xc, wa_ref, wx_ref, ba, bx, sp):
    pre_r, pre_i = [], []
    for hb in range(LRU_BLOCKS):
        xb = xc[:, hb * LRU_BLOCK_W:(hb + 1) * LRU_BLOCK_W]
        pre_r.append(_dot(xb, wa_ref[hb]))
        pre_i.append(_dot(xb, wx_ref[hb]))
    r = _sigmoid(jnp.concatenate(pre_r, axis=1) + ba)
    i = _sigmoid(jnp.concatenate(pre_i, axis=1) + bx)
    la = -LRU_C * r * sp
    a = jnp.exp(la)
    th = jnp.tanh(la)
    s = jnp.sqrt(-2.0 * th / (1.0 - th))
    return r, i, a, s


def _scan_down(a, b):
    R = a.shape[0]
    row = lax.broadcasted_iota(jnp.int32, a.shape, 0)
    d = 1
    while d < R:
        ok = row >= d
        b = a * jnp.where(ok, pltpu.roll(b, d, axis=0), 0.0) + b
        a = a * jnp.where(ok, pltpu.roll(a, d, axis=0), 1.0)
        d *= 2
    return a, b


def _scan_up(a, b):
    R = a.shape[0]
    row = lax.broadcasted_iota(jnp.int32, a.shape, 0)
    d = 1
    while d < R:
        ok = row < R - d
        b = a * jnp.where(ok, pltpu.roll(b, R - d, axis=0), 0.0) + b
        a = a * jnp.where(ok, pltpu.roll(a, R - d, axis=0), 1.0)
        d *= 2
    return b


def _rglru_fwd(proj, conv_w, conv_b, wa, wx, ba, bx, sp, name):
    T = proj.shape[0]
    R = _tile(T, 256)
    W = D_MODEL

    def body(p_ref, cw_ref, cb_ref, wa_ref, wx_ref, ba_ref, bx_ref, sp_ref, hg_ref, h_ref, prev_ref, hc_ref):
        i = pl.program_id(0)

        @pl.when(i == 0)
        def _():
            prev_ref[...] = jnp.zeros_like(prev_ref)
            hc_ref[...] = jnp.zeros_like(hc_ref)

        xr = p_ref[:, :W]
        gate = p_ref[:, W:]
        xc = _conv(xr, prev_ref[...], cw_ref[...]) + cb_ref[...]
        prev_ref[...] = xr[R - SUBLANES:]
        r, ig, a, s = _lru_gates(xc, wa_ref, wx_ref, ba_ref[...], bx_ref[...], sp_ref[...])
        pa, hb = _scan_down(a, s * ig * xc)
        h = hb + pa * hc_ref[SUBLANES - 1:SUBLANES, :]
        h_ref[...] = h
        hg_ref[...] = (h * _gelu(gate)).astype(ACT_DTYPE)
        hc_ref[...] = h[R - SUBLANES:]

    vec = pl.BlockSpec((1, W), lambda i: (0, 0))
    wsp = pl.BlockSpec((LRU_BLOCKS, LRU_BLOCK_W, LRU_BLOCK_W), lambda i: (0, 0, 0))
    row = pl.BlockSpec((R, W), lambda i: (i, 0))
    return pl.pallas_call(
        body, name=name, grid=(T // R,),
        in_specs=[pl.BlockSpec((R, 2 * W), lambda i: (i, 0)), pl.BlockSpec((CONV_K, W), lambda i: (0, 0)),
                  vec, wsp, wsp, vec, vec, vec],
        out_specs=[row, row],
        out_shape=[jax.ShapeDtypeStruct((T, W), ACT_DTYPE), jax.ShapeDtypeStruct((T, W), F32)],
        scratch_shapes=[pltpu.VMEM((SUBLANES, W), F32), pltpu.VMEM((SUBLANES, W), F32)],
        compiler_params=_cparams("arbitrary"),
    )(proj, conv_w, conv_b, wa, wx, ba, bx, sp)


def _rglru_bwd(proj, conv_w, conv_b, wa, wx, ba, bx, sp, h, dhg, name):
    T = proj.shape[0]
    R = _tile(T, 256)
    nb = T // R
    r8 = R // SUBLANES
    W = D_MODEL

    def body(p_ref, pp_ref, cw_ref, cb_ref, wa_ref, wx_ref, ba_ref, bx_ref, sp_ref, h_ref, hp_ref, dhg_ref,
             dp_ref, dcw_ref, dcb_ref, dwa_ref, dwx_ref, dba_ref, dbx_ref, dsp_ref, lam_ref, nxt_ref):
        step = pl.program_id(0)
        blk = nb - 1 - step

        @pl.when(step == 0)
        def _():
            lam_ref[...] = jnp.zeros_like(lam_ref)
            nxt_ref[...] = jnp.zeros_like(nxt_ref)

        xr = p_ref[:, :W]
        gate = p_ref[:, W:]
        first = blk > 0
        prev8 = jnp.where(first, pp_ref[:, :W], 0.0)
        hprev8 = jnp.where(first, hp_ref[...], 0.0)
        cw = cw_ref[...]
        spv = sp_ref[...]
        xc = _conv(xr, prev8, cw) + cb_ref[...]
        r, ig, a, s = _lru_gates(xc, wa_ref, wx_ref, ba_ref[...], bx_ref[...], spv)
        hv = h_ref[...]
        dhg_v = dhg_ref[...]
        dgate = dhg_v * hv * _dgelu(gate)
        dh = dhg_v * _gelu(gate)
        row = lax.broadcasted_iota(jnp.int32, (R, W), 0)
        last = row == R - 1
        a_up = jnp.where(last, 0.0, pltpu.roll(a, R - 1, axis=0))
        lam = _scan_up(a_up, dh + jnp.where(last, lam_ref[0:1, :], 0.0))
        lam_ref[...] = (a * lam)[:SUBLANES]
        h_dn = _shift_down(hv, hprev8, 1)
        da = lam * h_dn
        bx_in = ig * xc
        dsv = lam * bx_in
        dig = lam * s * xc
        dxc = lam * s * ig
        dla = da * a - dsv * (a * a) / s
        dr = dla * (-LRU_C) * spv
        dsp = jnp.sum(dla * (-LRU_C) * r, axis=0, keepdims=True)
        dpr = dr * r * (1.0 - r)
        dpi = dig * ig * (1.0 - ig)
        dxc_parts, dwa_parts, dwx_parts = [], [], []
        for hb in range(LRU_BLOCKS):
            sl = slice(hb * LRU_BLOCK_W, (hb + 1) * LRU_BLOCK_W)
            xb, gr, gi = xc[:, sl], dpr[:, sl], dpi[:, sl]
            dxc_parts.append(_dot(gr, wa_ref[hb], NT) + _dot(gi, wx_ref[hb], NT))
            dwa_parts.append(_dot(xb, gr, TN))
            dwx_parts.append(_dot(xb, gi, TN))
        dxc = dxc + jnp.concatenate(dxc_parts, axis=1)
        dxr, dcw = _conv_bwd(xr, prev8, cw, dxc, nxt_ref[...])
        nxt_ref[...] = dxc[:SUBLANES]
        dp_ref[:, :W] = dxr.astype(ACT_DTYPE)
        dp_ref[:, W:] = dgate.astype(ACT_DTYPE)
        dcb = jnp.sum(dxc, axis=0, keepdims=True)
        dba = jnp.sum(dpr, axis=0, keepdims=True)
        dbx = jnp.sum(dpi, axis=0, keepdims=True)

        @pl.when(step == 0)
        def _():
            for j in range(CONV_K):
                dcw_ref[j:j + 1, :] = dcw[j]
            dcb_ref[...] = dcb
            dba_ref[...] = dba
            dbx_ref[...] = dbx
            dsp_ref[...] = dsp
            for hb in range(LRU_BLOCKS):
                dwa_ref[hb] = dwa_parts[hb]
                dwx_ref[hb] = dwx_parts[hb]

        @pl.when(step > 0)
        def _():
            for j in range(CONV_K):
                dcw_ref[j:j + 1, :] += dcw[j]
            dcb_ref[...] += dcb
            dba_ref[...] += dba
            dbx_ref[...] += dbx
            dsp_ref[...] += dsp
            for hb in range(LRU_BLOCKS):
                dwa_ref[hb] += dwa_parts[hb]
                dwx_ref[hb] += dwx_parts[hb]

    rv = lambda i: nb - 1 - i
    before = lambda i: jnp.maximum((nb - 1 - i) * r8 - 1, 0)
    vec = pl.BlockSpec((1, W), lambda i: (0, 0))
    cws = pl.BlockSpec((CONV_K, W), lambda i: (0, 0))
    wsp = pl.BlockSpec((LRU_BLOCKS, LRU_BLOCK_W, LRU_BLOCK_W), lambda i: (0, 0, 0))
    row = pl.BlockSpec((R, W), lambda i: (rv(i), 0))
    wshape = jax.ShapeDtypeStruct((LRU_BLOCKS, LRU_BLOCK_W, LRU_BLOCK_W), F32)
    vshape = jax.ShapeDtypeStruct((1, W), F32)
    return pl.pallas_call(
        body, name=name, grid=(nb,),
        in_specs=[pl.BlockSpec((R, 2 * W), lambda i: (rv(i), 0)), pl.BlockSpec((SUBLANES, 2 * W), lambda i: (before(i), 0)),
                  cws, vec, wsp, wsp, vec, vec, vec, row, pl.BlockSpec((SUBLANES, W), lambda i: (before(i), 0)), row],
        out_specs=[pl.BlockSpec((R, 2 * W), lambda i: (rv(i), 0)), cws, vec, wsp, wsp, vec, vec, vec],
        out_shape=[jax.ShapeDtypeStruct((T, 2 * W), ACT_DTYPE), jax.ShapeDtypeStruct((CONV_K, W), F32), vshape,
                   wshape, wshape, vshape, vshape, vshape],
        scratch_shapes=[pltpu.VMEM((SUBLANES, W), F32), pltpu.VMEM((SUBLANES, W), F32)],
        compiler_params=_cparams("arbitrary"),
    )(proj, proj, conv_w, conv_b, wa, wx, ba, bx, sp, h, h, dhg)


MESH = pl.DeviceIdType.MESH
ANY = pl.BlockSpec(memory_space=pl.ANY)


def _position():
    x, y, c = lax.axis_index("x"), lax.axis_index("y"), lax.axis_index("c")
    other_chips = [(1 - x, y), (x, 1 - y), (1 - x, 1 - y)]
    return x, y, c, other_chips


def _all_gather_weights(shards, name):
    n = len(shards)

    def body(*refs):
        ins, outs = refs[:n], refs[n:2 * n]
        send_sems, recv_sems = refs[2 * n:]
        x, y, c, chips = _position()
        me = 2 * x + y
        sibling = (x, y, 1 - c)

        def rcopy(t, k, src, dst, to):
            return pltpu.make_async_remote_copy(src_ref=src, dst_ref=dst, send_sem=send_sems.at[t, k],
                                                recv_sem=recv_sems.at[t, k], device_id=to, device_id_type=MESH)

        started = []
        for t in range(n):
            for j, (cx, cy) in enumerate(chips):
                cp = rcopy(t, j, ins[t].at[c], outs[t].at[me, c], (cx, cy, c))
                cp.start()
                started.append(cp)
        for t in range(n):
            for j, (cx, cy) in enumerate(chips):
                blk = outs[t].at[2 * cx + cy, c]
                rcopy(t, j, blk, blk, (cx, cy, c)).wait_recv()
                cp = rcopy(t, 3 + j, blk, blk, sibling)
                cp.start()
                started.append(cp)
        for t in range(n):
            for j, (cx, cy) in enumerate(chips):
                blk = outs[t].at[2 * cx + cy, 1 - c]
                rcopy(t, 3 + j, blk, blk, sibling).wait_recv()
        for cp in started:
            cp.wait_send()

    return pl.pallas_call(
        body, name=name, in_specs=[ANY] * n, out_specs=[ANY] * n,
        out_shape=[jax.ShapeDtypeStruct((N_CHIPS,) + s.shape, s.dtype) for s in shards],
        scratch_shapes=[pltpu.SemaphoreType.DMA((n, 6)), pltpu.SemaphoreType.DMA((n, 6))],
    )(*shards)


HBM = pl.BlockSpec(memory_space=pltpu.HBM)
SEM = pl.BlockSpec(memory_space=pltpu.SEMAPHORE)
EFFECT = pltpu.SideEffectType.DATAFLOW_SIDE_EFFECTING


def _gather_start(own, land, after, name):
    n = len(own)

    def body(*refs):
        own_refs, land_refs = refs[:n], refs[n:2 * n]
        send_sems = refs[2 * n + 1:3 * n + 1]
        recv_sems = refs[3 * n + 1:4 * n + 1]
        x, y, c, chips = _position()
        me = 2 * x + y
        for t in range(n):
            for cx, cy in chips:
                pltpu.make_async_remote_copy(
                    src_ref=own_refs[t], dst_ref=land_refs[t].at[me], send_sem=send_sems[t], recv_sem=recv_sems[t],
                    device_id=(cx, cy, c), device_id_type=MESH).start()

    sems = (pltpu.SemaphoreType.DMA(()),) * (2 * n)
    thru = [pltpu.HBM(a.shape, a.dtype) for a in list(own) + list(land)]
    out = pl.pallas_call(
        body, name=name, out_shape=(*sems, *thru),
        in_specs=[HBM] * (2 * n) + [pl.BlockSpec(memory_space=pl.ANY)], out_specs=(SEM,) * (2 * n) + (HBM,) * (2 * n),
        input_output_aliases={i: 2 * n + i for i in range(2 * n)},
        compiler_params=pltpu.CompilerParams(has_side_effects=EFFECT),
    )(*[pltpu.with_memory_space_constraint(a, pltpu.HBM) for a in list(own) + list(land)], after)
    return list(out[:n]), list(out[n:2 * n]), list(out[2 * n:3 * n]), list(out[3 * n:])


def _gather_wait(send_sems, recv_sems, own, land, after, name):
    n = len(own)

    def body(*refs):
        land_refs = refs[n:2 * n]
        s_sems, r_sems = refs[2 * n:3 * n], refs[3 * n:4 * n]
        x, y, c, _ = _position()
        for t in range(n):
            three = land_refs[t].at[pl.ds(0, N_CHIPS - 1)]
            cp = pltpu.make_async_remote_copy(src_ref=three, dst_ref=three, send_sem=s_sems[t], recv_sem=r_sems[t],
                                              device_id=(x, y, c), device_id_type=MESH)
            cp.wait_send()
            cp.wait_recv()

    thru = [pltpu.HBM(a.shape, a.dtype) for a in list(own) + list(land)]
    out = pl.pallas_call(
        body, name=name, out_shape=tuple(thru),
        in_specs=[HBM] * (2 * n) + [SEM] * (2 * n) + [pl.BlockSpec(memory_space=pl.ANY)], out_specs=(HBM,) * (2 * n),
        input_output_aliases={i: i for i in range(2 * n)},
        compiler_params=pltpu.CompilerParams(has_side_effects=EFFECT),
    )(*own, *land, *send_sems, *recv_sems, after)
    return list(out[n:])


def _rs_to_sibling(grads, name):
    n = len(grads)

    def body(*refs):
        ins, outs = refs[:n], refs[n:2 * n]
        send_sems, recv_sems = refs[2 * n:]
        x, y, c, _ = _position()
        cps = [pltpu.make_async_remote_copy(src_ref=ins[t].at[:, 1 - c], dst_ref=outs[t], send_sem=send_sems.at[t],
                                            recv_sem=recv_sems.at[t], device_id=(x, y, 1 - c), device_id_type=MESH)
               for t in range(n)]
        for cp in cps:
            cp.start()
        for cp in cps:
            cp.wait()

    return pl.pallas_call(
        body, name=name, in_specs=[ANY] * n, out_specs=[ANY] * n,
        out_shape=[jax.ShapeDtypeStruct((N_CHIPS,) + g.shape[2:], g.dtype) for g in grads],
        scratch_shapes=[pltpu.SemaphoreType.DMA((n,)), pltpu.SemaphoreType.DMA((n,))],
    )(*grads)


def _rs_across_start(parts, recv, name):
    n = len(parts)

    def body(*refs):
        part_refs, recv_refs = refs[:n], refs[n:2 * n]
        send_sems, recv_sems = refs[2 * n:3 * n], refs[3 * n:4 * n]
        token_ref = refs[-1]
        x, y, c, chips = _position()
        me = 2 * x + y
        for t in range(n):
            for cx, cy in chips:
                pltpu.make_async_remote_copy(src_ref=part_refs[t].at[2 * cx + cy], dst_ref=recv_refs[t].at[me],
                                             send_sem=send_sems[t], recv_sem=recv_sems[t], device_id=(cx, cy, c),
                                             device_id_type=MESH).start()
        token_ref[...] = jnp.zeros_like(token_ref)

    sems = (pltpu.SemaphoreType.DMA(()),) * (2 * n)
    thru = [pltpu.HBM(a.shape, a.dtype) for a in list(parts) + list(recv)]
    out = pl.pallas_call(
        body, name=name, out_shape=(*sems, *thru, jax.ShapeDtypeStruct((SUBLANES, LANES), F32)),
        in_specs=[HBM] * (2 * n), out_specs=(SEM,) * (2 * n) + (HBM,) * (2 * n) + (pl.BlockSpec(memory_space=pltpu.VMEM),),
        input_output_aliases={i: 2 * n + i for i in range(2 * n)},
        compiler_params=pltpu.CompilerParams(has_side_effects=EFFECT),
    )(*[pltpu.with_memory_space_constraint(a, pltpu.HBM) for a in list(parts) + list(recv)])
    return out[:n], out[n:2 * n], list(out[2 * n:3 * n]), list(out[3 * n:4 * n]), out[-1]


def _rs_across_wait(send_sems, recv_sems, parts, recv, after, name):
    n = len(parts)

    def body(*refs):
        recv_refs = refs[n:2 * n]
        s_sems, r_sems = refs[2 * n:3 * n], refs[3 * n:4 * n]
        x, y, c, _ = _position()
        for t in range(n):
            three = recv_refs[t].at[pl.ds(0, N_CHIPS - 1)]
            cp = pltpu.make_async_remote_copy(src_ref=three, dst_ref=three, send_sem=s_sems[t], recv_sem=r_sems[t],
                                              device_id=(x, y, c), device_id_type=MESH)
            cp.wait_send()
            cp.wait_recv()

    thru = [pltpu.HBM(a.shape, a.dtype) for a in list(parts) + list(recv)]
    out = pl.pallas_call(
        body, name=name, out_shape=tuple(thru),
        in_specs=[HBM] * (2 * n) + [SEM] * (2 * n) + [pl.BlockSpec(memory_space=pl.ANY)], out_specs=(HBM,) * (2 * n),
        input_output_aliases={i: i for i in range(2 * n)},
        compiler_params=pltpu.CompilerParams(has_side_effects=EFFECT),
    )(*parts, *recv, *send_sems, *recv_sems, after)
    return list(out[:n]), list(out[n:])


def _rs_join_halves(halves, name):
    n = len(halves)

    def body(*refs):
        ins, outs = refs[:n], refs[n:2 * n]
        send_sems, recv_sems = refs[2 * n:]
        x, y, c, _ = _position()
        cps = [pltpu.make_async_remote_copy(src_ref=ins[t].at[c], dst_ref=outs[t].at[c], send_sem=send_sems.at[t],
                                            recv_sem=recv_sems.at[t], device_id=(x, y, 1 - c), device_id_type=MESH)
               for t in range(n)]
        for cp in cps:
            cp.start()
        for t in range(n):
            blk = outs[t].at[1 - c]
            pltpu.make_async_remote_copy(src_ref=blk, dst_ref=blk, send_sem=send_sems.at[t], recv_sem=recv_sems.at[t],
                                         device_id=(x, y, 1 - c), device_id_type=MESH).wait_recv()
        for cp in cps:
            cp.wait_send()

    return pl.pallas_call(
        body, name=name, in_specs=[ANY] * n, out_specs=[ANY] * n,
        out_shape=[jax.ShapeDtypeStruct(h.shape, h.dtype) for h in halves],
        input_output_aliases={t: t for t in range(n)},
        scratch_shapes=[pltpu.SemaphoreType.DMA((n,)), pltpu.SemaphoreType.DMA((n,))],
    )(*halves)


def _all_gather_small(block, name):
    m_per, n = block.shape

    def body(x_ref, out_ref, send_sems, recv_sems, local_sem):
        x, y, c, chips = _position()
        me, sibling = (x, y, c), (x, y, 1 - c)

        def rows(px, py, pc):
            return out_ref.at[pl.ds((4 * px + 2 * py + pc) * m_per, m_per), :]

        def copy(k, blk, to, src=None):
            return pltpu.make_async_remote_copy(
                src_ref=rows(*blk) if src is None else src, dst_ref=rows(*blk), send_sem=send_sems.at[k],
                recv_sem=recv_sems.at[k], device_id=to, device_id_type=MESH)

        mine = pltpu.make_async_copy(x_ref, rows(*me), local_sem)
        mine.start()
        first = [copy(0, me, sibling, src=x_ref)]
        first += [copy(1 + j, me, (*chip, c), src=x_ref) for j, chip in enumerate(chips)]
        for cp in first:
            cp.start()
        passed = [copy(4 + j, (*chip, c), sibling) for j, chip in enumerate(chips)]
        for j, chip in enumerate(chips):
            copy(1 + j, (*chip, c), me).wait_recv()
            passed[j].start()
        copy(0, sibling, me).wait_recv()
        for j, chip in enumerate(chips):
            copy(4 + j, (*chip, 1 - c), me).wait_recv()
        for cp in first + passed:
            cp.wait_send()
        mine.wait()

    return pl.pallas_call(
        body, name=name, out_shape=jax.ShapeDtypeStruct((N_DEV * m_per, n), block.dtype),
        in_specs=[pl.BlockSpec(memory_space=pltpu.VMEM)], out_specs=pl.BlockSpec(memory_space=pltpu.VMEM),
        scratch_shapes=[pltpu.SemaphoreType.DMA((7,)), pltpu.SemaphoreType.DMA((7,)), pltpu.SemaphoreType.DMA],
    )(block)


def _row_tile(R, n):
    budget = 1 << 19
    if R * n <= budget or R % SUBLANES:
        return R
    t = R
    while t * n > budget and t % (2 * SUBLANES) == 0:
        t //= 2
    return t


def _pair_sum(g, recv, c_arr, name):
    _, _, R, n = g.shape
    tr = _row_tile(R, n)

    def body(c_ref, g_ref, r_ref, o_ref):
        o_ref[...] = (g_ref[...] + r_ref[...]).astype(ICI_DTYPE)

    grid_spec = pltpu.PrefetchScalarGridSpec(
        num_scalar_prefetch=1, grid=(N_CHIPS, R // tr),
        in_specs=[pl.BlockSpec((None, None, tr, n), lambda p, i, c: (p, c[0], i, 0)),
                  pl.BlockSpec((None, tr, n), lambda p, i, c: (p, i, 0))],
        out_specs=pl.BlockSpec((None, tr, n), lambda p, i, c: (p, i, 0)))
    return pl.pallas_call(
        body, name=name, grid_spec=grid_spec, out_shape=jax.ShapeDtypeStruct(recv.shape, ICI_DTYPE),
        compiler_params=_cparams("parallel", "parallel"),
    )(c_arr, g, recv)


def _chip_sum(recv, own, where, name):
    _, R, n = recv.shape
    tr = _row_tile(R, n)

    def body(w_ref, r0, r1, r2, r3, own_ref, o_ref):
        me = w_ref[0]
        terms = [jnp.where(me == k, own_ref[...], r[...]).astype(F32) for k, r in enumerate((r0, r1, r2, r3))]
        o_ref[...] = ((terms[0] + terms[1]) + terms[2]) + terms[3]

    def slot(k):
        return pl.BlockSpec((None, tr, n), lambda i, w: (w[1 + k], i, 0))

    grid_spec = pltpu.PrefetchScalarGridSpec(
        num_scalar_prefetch=1, grid=(R // tr,),
        in_specs=[slot(0), slot(1), slot(2), slot(3), pl.BlockSpec((None, tr, n), lambda i, w: (w[0], i, 0))],
        out_specs=pl.BlockSpec((None, tr, n), lambda i, w: (w[5], i, 0)))
    return pl.pallas_call(
        body, name=name, grid_spec=grid_spec, out_shape=jax.ShapeDtypeStruct((2, R, n), F32),
        compiler_params=_cparams("parallel"),
    )(where, recv, recv, recv, recv, own)


ADAM_C1 = 1.0 / (1.0 - ADAM_B1 ** ADAM_STEP)
ADAM_C2 = 1.0 / (1.0 - ADAM_B2 ** ADAM_STEP)


def _adamw_math(w, g, m, v):
    m = ADAM_B1 * m + (1.0 - ADAM_B1) * g
    v = ADAM_B2 * v + (1.0 - ADAM_B2) * (g * g)
    delta = -ADAM_LR * ((m * ADAM_C1) / (jnp.sqrt(v * ADAM_C2) + ADAM_EPS) + ADAM_WD * w)
    return delta, m, v


def _adamw(w, g, m, v, name):
    R, n = w.shape
    tr = _row_tile(R, n)

    def body(w_ref, g_ref, m_ref, v_ref, d_ref, nm_ref, nv_ref):
        d_ref[...], nm_ref[...], nv_ref[...] = _adamw_math(w_ref[...], g_ref[...], m_ref[...], v_ref[...])

    spec = pl.BlockSpec((tr, n), lambda i: (i, 0))
    return pl.pallas_call(
        body, name=name, grid=(R // tr,), in_specs=[spec] * 4, out_specs=[spec] * 3,
        out_shape=[jax.ShapeDtypeStruct((R, n), F32)] * 3, compiler_params=_cparams("parallel"),
    )(w, g, m, v)


def _adamw_half(w, g, m, v, half, prev, name):
    R, n = w.shape
    tr = _row_tile(R // 2, n)
    off = half * (R // 2 // tr)

    def body(*refs):
        w_ref, g_ref, m_ref, v_ref = refs[:4]
        go_ref, d_ref, nm_ref, nv_ref = refs[-4:]
        gv = g_ref[...]
        go_ref[...] = gv
        d_ref[...], nm_ref[...], nv_ref[...] = _adamw_math(w_ref[...], gv, m_ref[...], v_ref[...])

    rows = pl.BlockSpec((tr, n), lambda i: (i + off, 0))
    carried = [] if prev is None else list(prev)
    return pl.pallas_call(
        body, name=name, grid=(R // 2 // tr,),
        in_specs=[rows, pl.BlockSpec((tr, n), lambda i: (i, 0)), rows, rows] + [ANY] * len(carried),
        out_specs=[rows] * 4, out_shape=[jax.ShapeDtypeStruct((R, n), F32)] * 4,
        input_output_aliases={4 + i: i for i in range(len(carried))},
        compiler_params=_cparams("parallel"),
    )(w, g, m, v, *carried)


def _adamw_small(w, gall, m, v, name):
    M, n = w.shape

    def body(w_ref, g_ref, m_ref, v_ref, gs_ref, d_ref, nm_ref, nv_ref):
        g = g_ref[0:M, :]
        for d in range(1, N_DEV):
            g = g + g_ref[d * M:(d + 1) * M, :]
        gs_ref[...] = g
        d_ref[...], nm_ref[...], nv_ref[...] = _adamw_math(w_ref[...], g, m_ref[...], v_ref[...])

    return pl.pallas_call(
        body, name=name, out_shape=[jax.ShapeDtypeStruct((M, n), F32)] * 4,
    )(w, gall, m, v)


SMALL_ROWS = 24
MID_ROWS = 4


def _pack_small(ln1_g, ln1_b, ln2_g, ln2_b, norm_w, sinks, a_log, dt_bias):
    mixed = jnp.concatenate([norm_w.reshape(-1), sinks.reshape(-1), a_log.reshape(-1), dt_bias.reshape(-1)])
    mixed = jnp.pad(mixed, (0, D_MODEL - mixed.shape[0]))[None]
    pad = jnp.zeros((SMALL_ROWS - 4 * DEPTH - 1, D_MODEL), F32)
    return jnp.concatenate([ln1_g, ln1_b, ln2_g, ln2_b, mixed, pad], axis=0)


def _unpack_small(p):
    mixed = p[4 * DEPTH]
    return (p[0:4], p[4:8], p[8:12], p[12:16], mixed[0:256].reshape(2, 128), mixed[256:272].reshape(2, 8),
            mixed[272:280].reshape(2, 4), mixed[280:288].reshape(2, 4))


def _pack_mid(conv_w, rconv_w, rconv_b, b_a, b_x, lam):
    lead = conv_w.shape[0]
    flat = jnp.concatenate([conv_w.reshape(lead, -1), rconv_w.reshape(lead, -1), rconv_b, b_a, b_x, lam], axis=1)
    return jnp.pad(flat, ((0, 0), (0, MID_ROWS * D_MODEL - flat.shape[1]))).reshape(lead, MID_ROWS, D_MODEL)


def _unpack_mid(p):
    lead = p.shape[:-2]
    f = p.reshape(lead + (MID_ROWS * D_MODEL,))
    return (f[..., 0:1536].reshape(lead + (4, 384)), f[..., 1536:2560].reshape(lead + (4, 256)),
            f[..., 2560:2816], f[..., 2816:3072], f[..., 3072:3328], f[..., 3328:3584])


def _cols_from_chips(g):
    p, L, R, n = g.shape
    return g.transpose(1, 2, 0, 3).reshape(L, R, p * n)


def _rows_from_chips(g):
    p, L, R, n = g.shape
    return g.transpose(1, 0, 2, 3).reshape(L, p * R, n)


def _cols_to_chips(g):
    L, R, n4 = g.shape
    return g.reshape(L, R, N_CHIPS, n4 // N_CHIPS).transpose(2, 0, 1, 3)


def _rows_to_chips(g):
    L, R4, n = g.shape
    return g.reshape(L, N_CHIPS, R4 // N_CHIPS, n).transpose(1, 0, 2, 3)


def _halves(a):
    return a.reshape(2, -1, a.shape[-1])


def _pad_hyb_cols(w):
    z = jnp.zeros(w.shape[:-1] + (HYB_PAD - HP_BG - 2 * B_HEADS,), w.dtype)
    return jnp.concatenate([w[..., 0:512], w[..., 768:2304], w[..., 2304:2816], w[..., 512:768], w[..., 2816:2824], z], axis=-1)


def _unpad_hyb_cols(w):
    return jnp.concatenate([w[..., 0:512], w[..., 2560:2816], w[..., 512:2048], w[..., 2048:2560], w[..., 2816:2824]], axis=-1)


def _hybrid_fwd(x, W, j, tables, sfx, out_ready, ln):
    cos, sin_s = tables
    T = x.shape[0]
    N = T // B_CHUNK
    proj = _matmul(x, W["hyb_w_in"][j], "nn", "hyb_in" + sfx)
    sinks_b = jnp.broadcast_to(W["hyb_sinks"][j][:, None], (A_Q_HEADS, LANES))
    o_a, lse = _attn_fwd(proj, cos, sin_s, sinks_b, "attn_fwd" + sfx)
    qkvc = _dnconv_fwd(proj, W["hyb_conv_w"][j], "dnconv_fwd" + sfx)
    bg = proj[:, HP_BG:HP_BG + 2 * B_HEADS]
    beta = jax.nn.sigmoid(bg[:, :B_HEADS])
    pre = bg[:, B_HEADS:] + W["hyb_dt_bias"][j][None]
    g = -jnp.exp(W["hyb_a_log"][j])[None] * jax.nn.softplus(pre)
    grow = g.T.reshape(B_HEADS, N, 1, B_CHUNK)
    brow = beta.T.reshape(B_HEADS, N, 1, B_CHUNK)
    nw = W["hyb_norm_w"][j][None]
    o_b, states, invs = _dn_fwd(qkvc, proj, nw, grow, brow, "dn_fwd" + sfx)
    mix = jnp.concatenate([o_a, o_b], axis=1).astype(ACT_DTYPE)
    out_ready(mix)
    out = _matmul(mix, W["hyb_w_out"][j], "nn", "hyb_out" + sfx, epi=_epi_residual_ln, extra=ln, out_dtype=_ln_out(),
                  tm=512)
    res = dict(proj=proj, o_a=o_a, lse=lse, qkvc=qkvc, beta=beta, pre=pre, g=g, grow=grow, brow=brow,
               states=states, invs=invs, mix=mix, sinks_b=sinks_b, nw=nw)
    return out, res


def _hybrid_bwd(x, du, dub, W, j, res, tables, sfx):
    cos, sin_s = tables
    T = x.shape[0]
    proj = res["proj"]
    d_wout = _matmul(res["mix"], dub, "tn", "hyb_out_dw" + sfx)
    dmix = _matmul(dub, W["hyb_w_out"][j], "nt", "hyb_out_dx" + sfx)
    dproj, dkc, dkp, dvc, dvp, dsink = _attn_bwd(proj, cos, sin_s, res["sinks_b"], res["o_a"], res["lse"], dmix,
                                                  "attn_bwd" + sfx)
    zpad = jnp.zeros((WINDOW, LANES), F32)
    dk = dkc + jnp.concatenate([dkp[WINDOW:], zpad], axis=0)
    dv = dvc + jnp.concatenate([dvp[WINDOW:], zpad], axis=0)
    dqkvc, dproj, dg4, dbeta4, dnw = _dn_bwd(res["qkvc"], proj, res["nw"], res["grow"], res["brow"], res["states"],
                                             res["invs"], dmix, dproj, "dn_bwd" + sfx)
    dproj, dconv = _dnconv_bwd(proj, W["hyb_conv_w"][j], dqkvc, dproj, "dnconv_bwd" + sfx)
    dg = dg4.reshape(B_HEADS, T).T
    dbeta = dbeta4.reshape(B_HEADS, T).T
    beta = res["beta"]
    dbeta_logit = dbeta * beta * (1.0 - beta)
    da_logit = dg * (-jnp.exp(W["hyb_a_log"][j]))[None] * jax.nn.sigmoid(res["pre"])
    d_dt_bias = jnp.sum(da_logit, axis=0)
    d_a_log = jnp.sum(dg * res["g"], axis=0)
    zcols = jnp.zeros((T, HYB_PAD - HP_BG - 2 * B_HEADS), F32)
    tail = jnp.concatenate([dk, dv, dbeta_logit, da_logit, zcols], axis=1).astype(ACT_DTYPE)
    dproj = lax.dynamic_update_slice(dproj, tail, (0, HP_K))
    d_win = _matmul(x, dproj, "tn", "hyb_in_dw" + sfx)
    dx = _matmul(dproj, W["hyb_w_in"][j], "nt", "hyb_in_dx" + sfx, epi=_epi_add_residual, extra=du)
    grads = dict(hyb_w_in=d_win, hyb_w_out=d_wout, hyb_sinks=dsink[0], hyb_conv_w=dconv, hyb_a_log=d_a_log,
                 hyb_dt_bias=d_dt_bias, hyb_norm_w=jnp.sum(dnw[:, 0, :], axis=0))
    return dx, grads


def _rec_fwd(x, W, j, sfx, ln):
    proj = _matmul(x, W["rec_w_in"][j], "nn", "rec_in" + sfx)
    sp = jax.nn.softplus(-W["rec_lambda"][j])[None]
    hg, h = _rglru_fwd(proj, W["rec_conv_w"][j], W["rec_conv_b"][j][None], W["rec_w_a"][j], W["rec_w_x"][j],
                       W["rec_b_a"][j][None], W["rec_b_x"][j][None], sp, "rglru_fwd" + sfx)
    out = _matmul(hg, W["rec_w_out"][j], "nn", "rec_out" + sfx, epi=_epi_residual_ln, extra=ln, out_dtype=_ln_out(),
                  tm=512)
    return out, dict(proj=proj, hg=hg, h=h, sp=sp)


def _rec_bwd(x, du, dub, W, j, res, sfx):
    d_wout = _matmul(res["hg"], dub, "tn", "rec_out_dw" + sfx)
    dhg = _matmul(dub, W["rec_w_out"][j], "nt", "rec_out_dx" + sfx)
    dproj, dcw, dcb, dwa, dwx, dba, dbx, dsp = _rglru_bwd(
        res["proj"], W["rec_conv_w"][j], W["rec_conv_b"][j][None], W["rec_w_a"][j], W["rec_w_x"][j],
        W["rec_b_a"][j][None], W["rec_b_x"][j][None], res["sp"], res["h"], dhg, "rglru_bwd" + sfx)
    d_lam = dsp[0] * (-jax.nn.sigmoid(-W["rec_lambda"][j]))
    d_win = _matmul(x, dproj, "tn", "rec_in_dw" + sfx)
    dx = _matmul(dproj, W["rec_w_in"][j], "nt", "rec_in_dx" + sfx, epi=_epi_add_residual, extra=du)
    grads = dict(rec_w_in=d_win, rec_w_out=d_wout, rec_conv_w=dcw, rec_conv_b=dcb[0], rec_w_a=dwa, rec_w_x=dwx,
                 rec_b_a=dba[0], rec_b_x=dbx[0], rec_lambda=d_lam)
    return dx, grads


def _local_step(x, tgt, W, mlp_w, mixer_ready, on_group):
    T = x.shape[0]
    tables = _rope_tables(T)
    acts = []
    xb = x.astype(ACT_DTYPE)
    for layer in range(DEPTH):
        j, sfx = layer // 2, ""
        mixer_ready(layer, xb)
        ln1 = (x, W["ln1_g"][layer][None], W["ln1_b"][layer][None])
        if layer % 2 == 0:
            (x1, x1b, u1), res = _hybrid_fwd(xb, W, j, tables, sfx,
                                             functools.partial(mixer_ready, layer, out_projection=True), ln1)
        else:
            (x1, x1b, u1), res = _rec_fwd(xb, W, j, sfx, ln1)
        w1, w2, wl = mlp_w(layer, x1b)
        h1 = _matmul(x1b, w1, "nn", "mlp_up", out_dtype=ACT_DTYPE, b_chips=("j", wl))
        x2, x2b, u2 = _matmul(h1, w2, "nn", "mlp_down", a_fn=_relu2, b_chips=("k", wl), epi=_epi_residual_ln,
                              extra=(x1, W["ln2_g"][layer][None], W["ln2_b"][layer][None]), out_dtype=_ln_out())
        acts.append(dict(xb=xb, res=res, u1=u1, x1b=x1b, h1=h1, u2=u2))
        x, xb = x2, x2b
    dx, loss = _loss_head(x, tgt, "loss_head")
    per_layer = [None] * DEPTH
    d_w1 = [lax.empty((N_CHIPS, 2, D_MODEL, D_FF // N_CHIPS), F32) for _ in range(DEPTH // 2)]
    d_w2 = [lax.empty((N_CHIPS, 2, D_FF // N_CHIPS, D_MODEL), F32) for _ in range(DEPTH // 2)]
    token = None
    for layer in reversed(range(DEPTH)):
        j, a = layer // 2, acts[layer]
        ln2_g = W["ln2_g"][layer][None]
        if token is not None:
            ln2_g = ln2_g + token
        du2, du2b, dg2, db2 = _ln_bwd(a["u2"], ln2_g, dx, "ln_bwd")
        w1, w2, wl = mlp_w(layer, du2b)
        d_w2[j] = _matmul(a["h1"], du2b, "tn", "mlp_down_dw", a_fn=_relu2, out_chips=("i", layer % 2, d_w2[j]))
        dh1 = _matmul(du2b, w2, "nt", "mlp_down_dx", epi=_epi_drelu2, extra=a["h1"], out_dtype=ACT_DTYPE,
                      b_chips=("j", wl))
        d_w1[j] = _matmul(a["x1b"], dh1, "tn", "mlp_up_dw", out_chips=("j", layer % 2, d_w1[j]))
        dx1 = _matmul(dh1, w1, "nt", "mlp_up_dx", epi=_epi_add_residual, extra=du2, b_chips=("k", wl))
        du1, du1b, dg1, db1 = _ln_bwd(a["u1"], W["ln1_g"][layer][None], dx1, "ln_bwd")
        if layer % 2 == 0:
            dx, g = _hybrid_bwd(a["xb"], du1, du1b, W, j, a["res"], tables, "")
        else:
            dx, g = _rec_bwd(a["xb"], du1, du1b, W, j, a["res"], "")
        g.update(ln1_g=dg1[0], ln1_b=db1[0], ln2_g=dg2[0], ln2_b=db2[0])
        per_layer[layer] = g
        if layer % 2 == 0:
            token = on_group(j, per_layer[layer], per_layer[layer + 1], d_w1[j], d_w2[j])
    grads = {}
    for name in ("ln1_g", "ln1_b", "ln2_g", "ln2_b"):
        grads[name] = jnp.stack([per_layer[l][name] for l in range(DEPTH)])
    for name in ("hyb_norm_w", "hyb_sinks", "hyb_a_log", "hyb_dt_bias"):
        grads[name] = jnp.stack([per_layer[l][name] for l in (0, 2)])
    return loss, dx, grads


BIG = ("hyb_w_in", "hyb_w_out", "rec_w_in", "rec_w_out", "mlp_w1", "mlp_w2", "rec_w_a", "rec_w_x")
COL_SHARDED = ("hyb_w_in", "rec_w_in", "mlp_w1")
CHIP_MAJOR = ("mlp_w1", "mlp_w2")
MID = ("hyb_conv_w", "rec_conv_w", "rec_conv_b", "rec_b_a", "rec_b_x", "rec_lambda")
SMALL = ("ln1_g", "ln1_b", "ln2_g", "ln2_b", "hyb_norm_w", "hyb_sinks", "hyb_a_log", "hyb_dt_bias")
WEIGHTS = ("hyb_w_in", "hyb_sinks", "hyb_conv_w", "hyb_a_log", "hyb_dt_bias", "hyb_norm_w", "hyb_w_out", "rec_w_in",
           "rec_conv_w", "rec_conv_b", "rec_w_a", "rec_b_a", "rec_w_x", "rec_b_x", "rec_lambda", "rec_w_out", "ln1_g",
           "ln1_b", "mlp_w1", "mlp_w2", "ln2_g", "ln2_b")


def _gather_full_weights(w):
    wb = {k: w[k].astype(MXU_DTYPE) for k in BIG}
    now = ("hyb_w_in",)
    shards = [_halves(wb[k][:1]) for k in now]
    shards.append(_pack_mid(*[w[k] for k in MID]))
    got = _all_gather_weights(shards, "all_gather_weights")
    me = 2 * lax.axis_index("x") + lax.axis_index("y")
    got = [lax.dynamic_update_slice(g, s[None], (me, 0, 0, 0)) for s, g in zip(shards, got)]

    def full(k, g):
        g = g.reshape((N_CHIPS,) + w[k].shape[1:])
        if k in CHIP_MAJOR:
            return g[:, None]
        if k in ("rec_w_a", "rec_w_x"):
            return g.transpose(1, 0, 2, 3).reshape(LRU_BLOCKS, LRU_BLOCK_W, LRU_BLOCK_W)
        f = _cols_from_chips(g[:, None])[0] if k in COL_SHARDED else _rows_from_chips(g[:, None])[0]
        return _pad_hyb_cols(f) if k == "hyb_w_in" else f

    rec = ("rec_w_in", "rec_w_out", "rec_w_a", "rec_w_x")
    groups = [(("hyb_w_out",), 0), (CHIP_MAJOR, 0), (rec, 0), (CHIP_MAJOR, 1), (("hyb_w_in", "hyb_w_out"), 1),
              (CHIP_MAJOR, 2), (rec, 1), (CHIP_MAJOR, 3)]
    own = [wb[k][j] for names, j in groups for k in names]
    land = [lax.dynamic_update_slice(lax.empty((N_CHIPS,) + o.shape, o.dtype), o[None], (me,) + (0,) * o.ndim)
            for o in own]
    send_sems, recv_sems, own, land = _gather_start(own, land, got[-1], "gather_start")
    W = {k: [None] * w[k].shape[0] for k in BIG}
    for k, g in zip(now, got[:-1]):
        W[k][0] = full(k, g)
    arrived = [0]

    def ensure(upto, after):
        while arrived[0] <= upto:
            gi = arrived[0]
            names, j = groups[gi]
            lo = sum(len(nm) for nm, _ in groups[:gi])
            sl = slice(lo, lo + len(names))
            got_g = _gather_wait(send_sems[sl], recv_sems[sl], own[sl], land[sl], after, "gather_wait_%d" % gi)
            for k, g in zip(names, got_g):
                W[k][j] = full(k, g)
            arrived[0] += 1

    def mixer_ready(layer, after, out_projection=False):
        if layer:
            ensure({1: 2, 2: 4, 3: 6}[layer], after)
        elif out_projection:
            ensure(0, after)

    def mlp_w(layer, after):
        ensure({0: 1, 1: 3, 2: 5, 3: 7}[layer], after)
        return W["mlp_w1"][layer], W["mlp_w2"][layer], 0

    conv_w, rconv_w, rconv_b, b_a, b_x, lam = _unpack_mid(got[-1])
    W["hyb_conv_w"] = conv_w.transpose(1, 2, 0, 3).reshape(2, CONV_K, 3 * B_W)
    W["rec_conv_w"] = rconv_w.transpose(1, 2, 0, 3).reshape(2, CONV_K, D_MODEL)
    for k, v in (("rec_conv_b", rconv_b), ("rec_b_a", b_a), ("rec_b_x", b_x), ("rec_lambda", lam)):
        W[k] = v.transpose(1, 0, 2).reshape(2, D_MODEL)
    for k in SMALL:
        W[k] = w[k]
    return W, mlp_w, mixer_ready


def _group_by_chip(gh, gr, d_w1, d_w2):
    g = dict(gh, **gr)
    g["hyb_w_in"] = _unpad_hyb_cols(g["hyb_w_in"])
    out = []
    for k in BIG:
        if k == "mlp_w1":
            v = d_w1
        elif k == "mlp_w2":
            v = d_w2
        elif k in ("rec_w_a", "rec_w_x"):
            v = g[k].reshape(1, LRU_BLOCKS, N_CHIPS, LRU_BLOCK_W // N_CHIPS, LRU_BLOCK_W).transpose(2, 0, 1, 3, 4)
        else:
            v = _cols_to_chips(g[k][None]) if k in COL_SHARDED else _rows_to_chips(g[k][None])
        out.append(v.reshape(N_CHIPS, 2, -1, v.shape[-1]))
    conv_w = g["hyb_conv_w"].reshape(CONV_K, N_CHIPS, -1).transpose(1, 0, 2)
    rconv_w = g["rec_conv_w"].reshape(CONV_K, N_CHIPS, -1).transpose(1, 0, 2)
    vecs = [g[k].reshape(N_CHIPS, -1) for k in ("rec_conv_b", "rec_b_a", "rec_b_x", "rec_lambda")]
    out.append(_pack_mid(conv_w, rconv_w, *vecs).reshape(N_CHIPS, 2, MID_ROWS // 2, D_MODEL))
    return out


def kernel(x, hyb_w_in, hyb_sinks, hyb_conv_w, hyb_a_log, hyb_dt_bias, hyb_norm_w, hyb_w_out, rec_w_in, rec_conv_w, rec_conv_b, rec_w_a, rec_b_a, rec_w_x, rec_b_x, rec_lambda, rec_w_out, ln1_g, ln1_b, mlp_w1, mlp_w2, ln2_g, ln2_b, loss_target, m_hyb_w_in, m_hyb_sinks, m_hyb_conv_w, m_hyb_a_log, m_hyb_dt_bias, m_hyb_norm_w, m_hyb_w_out, m_rec_w_in, m_rec_conv_w, m_rec_conv_b, m_rec_w_a, m_rec_b_a, m_rec_w_x, m_rec_b_x, m_rec_lambda, m_rec_w_out, m_ln1_g, m_ln1_b, m_mlp_w1, m_mlp_w2, m_ln2_g, m_ln2_b, v_hyb_w_in, v_hyb_sinks, v_hyb_conv_w, v_hyb_a_log, v_hyb_dt_bias, v_hyb_norm_w, v_hyb_w_out, v_rec_w_in, v_rec_conv_w, v_rec_conv_b, v_rec_w_a, v_rec_b_a, v_rec_w_x, v_rec_b_x, v_rec_lambda, v_rec_w_out, v_ln1_g, v_ln1_b, v_mlp_w1, v_mlp_w2, v_ln2_g, v_ln2_b):
    args = locals()
    w = {k: args[k] for k in WEIGHTS}
    m = {k: args["m_" + k] for k in WEIGHTS}
    v = {k: args["v_" + k] for k in WEIGHTS}

    W, mlp_w, mixer_ready = _gather_full_weights(w)

    core = lax.axis_index("c").astype(jnp.int32)
    me = (2 * lax.axis_index("x") + lax.axis_index("y")).astype(jnp.int32)
    slots = jnp.arange(N_CHIPS, dtype=jnp.int32)
    where = jnp.concatenate([me[None], jnp.where(slots == me, (slots + 1) % N_CHIPS, slots), core[None]])
    state = {}

    def on_group(group, gh, gr, d_w1, d_w2):
        by_chip = _group_by_chip(gh, gr, d_w1, d_w2)
        from_sibling = _rs_to_sibling(by_chip, "rs_to_sibling")
        pair = [_pair_sum(g, r, core[None], "pair_sum") for g, r in zip(by_chip, from_sibling)]
        recv = [lax.empty(p.shape, p.dtype) for p in pair]
        state[group] = _rs_across_start(pair, recv, "rs_across_start_%d" % group)
        return state[group][4][0, 0]

    def finish_group(group, after, prev):
        send_sems, recv_sems, pair, recv, _ = state[group]
        pair, from_chips = _rs_across_wait(send_sems, recv_sems, pair, recv, after, "rs_across_wait_%d" % group)
        half = [_chip_sum(r, p, where, "chip_sum") for r, p in zip(from_chips, pair)]
        joined = _rs_join_halves(half, "rs_join_halves")
        outs = {}
        for k, g in zip(BIG, joined[:-1]):
            n = g.shape[-1]
            outs[k] = _adamw_half(w[k].reshape(-1, n), g.reshape(-1, n), m[k].reshape(-1, n), v[k].reshape(-1, n),
                                  group, None if prev is None else prev[k], "adamw")
        return outs, joined[-1]

    loss, dx, grads = _local_step(x[0], loss_target[0], W, mlp_w, mixer_ready, on_group)
    loss = lax.psum(loss[0, 0], ("x", "y", "c"))
    g_out, d_out, m_out, v_out = {}, {}, {}, {}

    upper, mid_upper = finish_group(1, state[0][4], None)
    small_g = _pack_small(*[grads[k] for k in SMALL])
    small_all = _all_gather_small(small_g, "all_gather_small")
    sw, sm, sv = (_pack_small(*[t[k] for k in SMALL]) for t in (w, m, v))
    sg, sd, snm, snv = _adamw_small(sw, small_all, sm, sv, "adamw_small")
    for dst, packed in ((g_out, sg), (d_out, sd), (m_out, snm), (v_out, snv)):
        for k, val in zip(SMALL, _unpack_small(packed)):
            dst[k] = val

    done = sum([upper[k][1][0, 0] for k in BIG], sg[0, 0]).reshape(1, 1)
    both, mid_lower = finish_group(0, done, upper)
    for k in BIG:
        shape = w[k].shape
        g_out[k], d_out[k], m_out[k], v_out[k] = (t.reshape(shape) for t in both[k])
    mid_w, mid_m, mid_v = (_pack_mid(*[t[k] for k in MID]).reshape(-1, D_MODEL) for t in (w, m, v))
    mid_g = jnp.concatenate([mid_lower.reshape(-1, D_MODEL), mid_upper.reshape(-1, D_MODEL)])
    mid_d, mid_nm, mid_nv = _adamw(mid_w, mid_g, mid_m, mid_v, "adamw_mid")
    for dst, packed in ((g_out, mid_g), (d_out, mid_d), (m_out, mid_nm), (v_out, mid_nv)):
        for k, val in zip(MID, _unpack_mid(packed.reshape(2, MID_ROWS, D_MODEL))):
            dst[k] = val.reshape(w[k].shape)

    return (loss, dx[None], *[g_out[k] for k in WEIGHTS], *[d_out[k] for k in WEIGHTS],
            *[m_out[k] for k in WEIGHTS], *[v_out[k] for k in WEIGHTS])
```

```python
import functools

import jax
import jax.numpy as jnp
import numpy as np
from jax import lax
from jax.experimental import pallas as pl
from jax.experimental.pallas import tpu as pltpu

F32 = jnp.float32
MXU_DTYPE = jnp.bfloat16
ACT_DTYPE = jnp.bfloat16
ICI_DTYPE = jnp.bfloat16

D_MODEL = 1024
DEPTH = 4
A_HEAD_DIM = 64
A_Q_HEADS = 8
A_KV_HEADS = 2
WINDOW = 128
ROPE_THETA = 10000.0
B_HEADS = 4
B_HEAD_DIM = 128
B_CHUNK = 64
CONV_K = 4
LRU_BLOCKS = 4
LRU_BLOCK_W = D_MODEL // LRU_BLOCKS
LRU_C = 8.0
D_FF = 4 * D_MODEL
A_Q_W = A_Q_HEADS * A_HEAD_DIM
A_KV_W = A_KV_HEADS * A_HEAD_DIM
B_W = B_HEADS * B_HEAD_DIM
HYB_PROJ = A_Q_W + 2 * A_KV_W + 4 * B_W + 2 * B_HEADS
DN_ALPHA = (2 * DEPTH) ** 0.25
LN_EPS = 1e-5
NORM_EPS = 1e-6
ADAM_LR = 0.001
ADAM_B1 = 0.9
ADAM_B2 = 0.999
ADAM_EPS = 1e-08
ADAM_WD = 0.01
ADAM_STEP = 10

HP_Q = 0
HP_QKVB = 512
HP_Z = 2048
HP_K = 2560
HP_V = 2688
HP_BG = 2816
HYB_PAD = 3072

N_CHIPS = 4
N_DEV = 8
V7X_VMEM_LIMIT = 48 * 1024 * 1024
LANES = 128
SUBLANES = 8
NEG_BIG = -1e30

NN = (((1,), (0,)), ((), ()))
NT = (((1,), (1,)), ((), ()))
TN = (((0,), (0,)), ((), ()))


def _cparams(*sem):
    return pltpu.CompilerParams(dimension_semantics=sem, vmem_limit_bytes=V7X_VMEM_LIMIT)


def _dot(a, b, dims=NN):
    return lax.dot_general(a.astype(MXU_DTYPE), b.astype(MXU_DTYPE), dims, preferred_element_type=F32)


def _split_bf16(a):
    hi = a.astype(jnp.bfloat16)
    return hi, (a - hi.astype(F32)).astype(jnp.bfloat16)


def _dotf(a, b, dims=NN):
    ah, al = _split_bf16(a)
    bh, bl = _split_bf16(b)
    dg = functools.partial(lax.dot_general, dimension_numbers=dims, preferred_element_type=F32)
    return dg(ah, bh) + (dg(ah, bl) + dg(al, bh))


def _tile(dim, pref):
    t = min(dim, pref)
    while dim % t:
        t //= 2
    return t


def _sigmoid(x):
    return 1.0 / (1.0 + jnp.exp(-x))


def _silu(x):
    return x * _sigmoid(x)


def _dsilu(x):
    s = _sigmoid(x)
    return s * (1.0 + x * (1.0 - s))


GELU_C = 0.7978845608028654
GELU_A = 0.044715


def _gelu(x):
    return 0.5 * x * (1.0 + jnp.tanh(GELU_C * (x + GELU_A * x * x * x)))


def _dgelu(x):
    t = jnp.tanh(GELU_C * (x + GELU_A * x * x * x))
    return 0.5 * (1.0 + t) + 0.5 * x * (1.0 - t * t) * GELU_C * (1.0 + 3.0 * GELU_A * x * x)


def _matmul(a, b, mode, name, *, tm=1024, tn=1024, tk=1024, a_fn=None, epi=None, extra=None, out_dtype=F32,
            b_chips=None, out_chips=None):
    chunk = tk
    if mode == "tn":
        K, M = a.shape
        if K <= 4 * tk:
            tk, tm = K, tm // 2
    else:
        M, K = a.shape
    whole_k = False
    if b_chips is not None:
        g, b_layer = b_chips
        r, n = b.shape[2:]
        n_dim, k_dim = (r, n) if mode == "nt" else (n, r)
        N = N_CHIPS * n_dim if g == "j" else n_dim
        assert K == (N_CHIPS * k_dim if g == "k" else k_dim)
        if g == "j":
            tn = n_dim
        else:
            whole_k, tk, tm = True, K, tm // 2
    elif mode == "nt":
        N = b.shape[0]
        if tk < K <= 3 * tk:
            tk, tm = K, tm // 2
    else:
        N = b.shape[1]
    if out_chips is not None:
        og, o_layer, o_buf = out_chips
        if og == "j":
            tn = o_buf.shape[3]
        else:
            tm = min(tm, o_buf.shape[2])
    tm, tn, tk = _tile(M, tm), _tile(N, tn), _tile(K, tk)
    nk = K // tk
    if mode == "tn":
        a_spec = pl.BlockSpec((tk, tm), lambda i, j, k: (k, i))
    else:
        a_spec = pl.BlockSpec((tm, tk), lambda i, j, k: (i, k))
    b_block = (tn, tk) if mode == "nt" else (tk, tn)
    if b_chips is None:
        b_spec = pl.BlockSpec(b_block, (lambda i, j, k: (j, k)) if mode == "nt" else (lambda i, j, k: (k, j)))
    elif mode == "nt" and whole_k:
        b_spec = pl.BlockSpec((N_CHIPS, None, tn, k_dim), lambda i, j, k: (0, b_layer, j, 0))
    elif mode == "nt":
        b_spec = pl.BlockSpec((None, None) + b_block, lambda i, j, k: (j, b_layer, 0, k))
    elif whole_k:
        b_spec = pl.BlockSpec((N_CHIPS, None, k_dim, tn), lambda i, j, k: (0, b_layer, 0, j),
                              pipeline_mode=pl.Buffered(1) if N == tn else None)
    else:
        b_spec = pl.BlockSpec((None, None) + b_block, lambda i, j, k: (j, b_layer, k, 0))
    o_spec = pl.BlockSpec((tm, tn), lambda i, j, k: (i, j))
    e_spec = o_spec
    if out_chips is not None:
        per = o_buf.shape[2] // tm
        o_spec = pl.BlockSpec((None, None, tm, tn), (lambda i, j, k: (j, o_layer, i, 0)) if og == "j"
                              else (lambda i, j, k: (i // per, o_layer, i % per, j)))
    dims = {"nn": NN, "nt": NT, "tn": TN}[mode]
    extras = () if extra is None else (extra if isinstance(extra, tuple) else (extra,))
    out_dtypes = out_dtype if isinstance(out_dtype, tuple) else (out_dtype,)
    n_in = 2 + len(extras) + (out_chips is not None)

    def body(*refs):
        a_ref, b_ref = refs[0], refs[1]
        e_refs = refs[2:2 + len(extras)]
        o_refs = refs[n_in:n_in + len(out_dtypes)]
        av = a_ref[...]
        if a_fn is not None:
            av = a_fn(av)
        if whole_k and mode == "nt":
            part = _dot(av[:, :k_dim], b_ref[0], dims)
            for chip in range(1, N_CHIPS):
                part = part + _dot(av[:, chip * k_dim:(chip + 1) * k_dim], b_ref[chip], dims)
        elif mode == "tn" and tk > chunk:
            part = _dot(av[:chunk], b_ref[0:chunk, :], dims)
            for c0 in range(chunk, tk, chunk):
                part = part + _dot(av[c0:c0 + chunk], b_ref[c0:c0 + chunk, :], dims)
        else:
            bv = b_ref[...]
            if whole_k:
                bv = bv.reshape(K, tn)
            part = _dot(av, bv, dims)

        def finish(acc):
            if epi is not None:
                acc = epi(acc, *[e[...] for e in e_refs])
            for o_ref, val, dt in zip(o_refs, acc if isinstance(acc, tuple) else (acc,), out_dtypes):
                o_ref[...] = val.astype(dt)

        if nk == 1:
            finish(part)
        else:
            acc_ref = refs[-1]
            k = pl.program_id(2)

            @pl.when(k == 0)
            def _():
                acc_ref[...] = part

            @pl.when(k > 0)
            def _():
                acc_ref[...] += part

            @pl.when(k == nk - 1)
            def _():
                finish(acc_ref[...])

    row_spec = pl.BlockSpec((1, tn), lambda i, j, k: (0, j))
    in_specs = [a_spec, b_spec] + [row_spec if e.shape[0] == 1 else e_spec for e in extras]
    args = (a, b) + extras
    out_shape = [jax.ShapeDtypeStruct((M, N), dt) for dt in out_dtypes]
    out_specs = [o_spec] * len(out_dtypes)
    aliases = {}
    if out_chips is not None:
        in_specs.append(pl.BlockSpec(memory_space=pl.ANY))
        args += (o_buf,)
        out_shape = [jax.ShapeDtypeStruct(o_buf.shape, o_buf.dtype)]
        aliases = {n_in - 1: 0}
    if not isinstance(out_dtype, tuple):
        out_shape, out_specs = out_shape[0], out_specs[0]
    return pl.pallas_call(
        body, name=name, grid=(M // tm, N // tn, nk), in_specs=in_specs, out_specs=out_specs, out_shape=out_shape,
        input_output_aliases=aliases,
        scratch_shapes=[pltpu.VMEM((tm, tn), F32)] if nk > 1 else [],
        compiler_params=_cparams("parallel", "parallel", "arbitrary"),
    )(*args)


def _relu2(v):
    r = jnp.maximum(v, 0.0)
    return r * r


def _epi_drelu2(acc, h):
    return acc * (2.0 * jnp.maximum(h, 0.0))


def _epi_add_residual(acc, du):
    return acc + DN_ALPHA * du


def _epi_residual_ln(acc, x, g, b):
    u = DN_ALPHA * x + acc
    mu = jnp.mean(u, axis=-1, keepdims=True)
    d = u - mu
    var = jnp.mean(d * d, axis=-1, keepdims=True)
    o = d * lax.rsqrt(var + LN_EPS) * g + b
    return o, o, u


def _ln_out():
    return (F32, ACT_DTYPE, F32)


def _ln_bwd(u, g, dout, name):
    T, D = u.shape
    tr = _tile(T, 512)

    def body(u_ref, g_ref, d_ref, du_ref, dub_ref, dg_ref, db_ref):
        i = pl.program_id(0)
        u = u_ref[...]
        mu = jnp.mean(u, axis=-1, keepdims=True)
        d = u - mu
        rstd = lax.rsqrt(jnp.mean(d * d, axis=-1, keepdims=True) + LN_EPS)
        xhat = d * rstd
        dout_v = d_ref[...]
        dxh = dout_v * g_ref[...]
        m1 = jnp.mean(dxh, axis=-1, keepdims=True)
        m2 = jnp.mean(dxh * xhat, axis=-1, keepdims=True)
        du = rstd * (dxh - m1 - xhat * m2)
        du_ref[...] = du
        dub_ref[...] = du.astype(ACT_DTYPE)
        pg = jnp.sum(dout_v * xhat, axis=0, keepdims=True)
        pb = jnp.sum(dout_v, axis=0, keepdims=True)

        @pl.when(i == 0)
        def _():
            dg_ref[...] = pg
            db_ref[...] = pb

        @pl.when(i > 0)
        def _():
            dg_ref[...] += pg
            db_ref[...] += pb

    row = pl.BlockSpec((tr, D), lambda i: (i, 0))
    vec = pl.BlockSpec((1, D), lambda i: (0, 0))
    return pl.pallas_call(
        body, name=name, grid=(T // tr,), in_specs=[row, vec, row], out_specs=[row, row, vec, vec],
        out_shape=[jax.ShapeDtypeStruct((T, D), F32), jax.ShapeDtypeStruct((T, D), ACT_DTYPE),
                   jax.ShapeDtypeStruct((1, D), F32), jax.ShapeDtypeStruct((1, D), F32)],
        compiler_params=_cparams("arbitrary"),
    )(u, g, dout)


def _loss_head(y, tgt, name):
    T, D = y.shape
    tr = _tile(T, 512)

    def body(y_ref, t_ref, dy_ref, l_ref):
        i = pl.program_id(0)
        e = y_ref[...] - t_ref[...]
        dy_ref[...] = e * (1.0 / D)
        part = jnp.sum(e * e, axis=(0, 1), keepdims=True) * (0.5 / D)

        @pl.when(i == 0)
        def _():
            l_ref[...] = part

        @pl.when(i > 0)
        def _():
            l_ref[...] += part

    row = pl.BlockSpec((tr, D), lambda i: (i, 0))
    one = pl.BlockSpec((1, 1), lambda i: (0, 0))
    return pl.pallas_call(
        body, name=name, grid=(T // tr,), in_specs=[row, row], out_specs=[row, one],
        out_shape=[jax.ShapeDtypeStruct((T, D), F32), jax.ShapeDtypeStruct((1, 1), F32)],
        compiler_params=_cparams("arbitrary"),
    )(y, tgt)


def _swap_half(x):
    n = x.shape[-1]
    lane = lax.broadcasted_iota(jnp.int32, x.shape, 1)
    first = (lane % A_HEAD_DIM) < (A_HEAD_DIM // 2)
    return jnp.where(first, pltpu.roll(x, n - A_HEAD_DIM // 2, axis=1), pltpu.roll(x, A_HEAD_DIM // 2, axis=1))


def _rope(x, cos, sin_signed):
    return x * cos + _swap_half(x) * sin_signed


def _rope_t(dy, cos, sin_signed):
    return dy * cos + _swap_half(dy * sin_signed)


def _rope_tables(T):
    half = A_HEAD_DIM // 2
    inv_freq = np.float32(ROPE_THETA) ** (-np.arange(half, dtype=np.float32) / np.float32(half))
    ang = np.arange(T, dtype=np.float32)[:, None] * inv_freq[None, :]
    cos = np.tile(np.cos(ang), (1, 4))
    sin = np.sin(ang)
    sin_signed = np.tile(np.concatenate([-sin, sin], axis=1), (1, 2))
    return jnp.asarray(cos, F32), jnp.asarray(sin_signed, F32)


def _band_mask(n):
    qi = lax.broadcasted_iota(jnp.int32, (WINDOW, 2 * WINDOW), 0)
    kj = lax.broadcasted_iota(jnp.int32, (WINDOW, 2 * WINDOW), 1)
    return (kj > qi) & (kj <= qi + WINDOW) & ((n > 0) | (kj >= WINDOW))


def _place(v, src_half, dst_half):
    lane = lax.broadcasted_iota(jnp.int32, v.shape, 1)
    if src_half != dst_half:
        v = pltpu.roll(v, A_HEAD_DIM, axis=1)
    keep = (lane >= A_HEAD_DIM) if dst_half else (lane < A_HEAD_DIM)
    return jnp.where(keep, v, 0.0)


def _attn_specs():
    kb, vb = HP_K // LANES, HP_V // LANES
    prev = lambda n: jnp.maximum(n - 1, 0)
    return dict(
        q=pl.BlockSpec((WINDOW, A_Q_W), lambda n: (n, 0)),
        kc=pl.BlockSpec((WINDOW, LANES), lambda n: (n, kb)),
        kp=pl.BlockSpec((WINDOW, LANES), lambda n: (prev(n), kb)),
        vc=pl.BlockSpec((WINDOW, LANES), lambda n: (n, vb)),
        vp=pl.BlockSpec((WINDOW, LANES), lambda n: (prev(n), vb)),
        tq=pl.BlockSpec((WINDOW, LANES), lambda n: (n, 0)),
        tp=pl.BlockSpec((WINDOW, LANES), lambda n: (prev(n), 0)),
        sink=pl.BlockSpec((A_Q_HEADS, LANES), lambda n: (0, 0)),
        row512=pl.BlockSpec((WINDOW, A_Q_W), lambda n: (n, 0)),
        row128=pl.BlockSpec((WINDOW, LANES), lambda n: (n, 0)),
        lse=pl.BlockSpec((WINDOW, A_Q_HEADS), lambda n: (n, 0)),
    )


def _attn_fwd(proj, cos, sin_s, sinks_b, name):
    T = proj.shape[0]
    sp = _attn_specs()

    def body(q_ref, kc_ref, kp_ref, vc_ref, vp_ref, cq_ref, sq_ref, cp_ref, sp_ref, sink_ref, o_ref, l_ref):
        n = pl.program_id(0)
        cq, sq = cq_ref[...], sq_ref[...]
        cq4, sq4 = jnp.tile(cq, (1, A_Q_W // LANES)), jnp.tile(sq, (1, A_Q_W // LANES))
        q = _rope(q_ref[...], cq4, sq4) * (A_HEAD_DIM ** -0.5)
        kc = _rope(kc_ref[...], cq, sq)
        kp = _rope(kp_ref[...], cp_ref[...], sp_ref[...])
        kk = jnp.concatenate([kp, kc], axis=0)
        vv = jnp.concatenate([vp_ref[...], vc_ref[...]], axis=0)
        mask = _band_mask(n)[None]
        lane = lax.broadcasted_iota(jnp.int32, (WINDOW, LANES), 1)
        lane8 = lax.broadcasted_iota(jnp.int32, (WINDOW, A_Q_HEADS), 1)
        qe = jnp.stack([_place(q[:, (hq // 2) * LANES:(hq // 2 + 1) * LANES], hq % 2, hq // 4) for hq in range(A_Q_HEADS)])
        kk8 = jnp.broadcast_to(kk[None], (A_Q_HEADS,) + kk.shape)
        vv8 = jnp.broadcast_to(vv[None], (A_Q_HEADS,) + vv.shape)
        sk = jnp.stack([sink_ref[hq:hq + 1, 0:1] for hq in range(A_Q_HEADS)])
        s = jnp.where(mask, _dot(qe, kk8, BNT), NEG_BIG)
        m = jnp.maximum(jnp.max(s, axis=-1, keepdims=True), sk)
        p = jnp.exp(s - m)
        den = jnp.sum(p, axis=-1, keepdims=True) + jnp.exp(sk - m)
        o = _dot(p * (1.0 / den), vv8, BNN)
        lse_h = m + jnp.log(den)
        outs = []
        lse = jnp.zeros((WINDOW, A_Q_HEADS), F32)
        for pb in range(A_Q_HEADS // 2):
            halves = [_place(o[2 * pb + e], pb // 2, e) for e in range(2)]
            outs.append(jnp.where(lane < A_HEAD_DIM, halves[0], halves[1]))
            for e in range(2):
                lse = jnp.where(lane8 == 2 * pb + e, lse_h[2 * pb + e], lse)
        o_ref[...] = jnp.concatenate(outs, axis=1)
        l_ref[...] = lse

    return pl.pallas_call(
        body, name=name, grid=(T // WINDOW,),
        in_specs=[sp["q"], sp["kc"], sp["kp"], sp["vc"], sp["vp"], sp["tq"], sp["tq"], sp["tp"], sp["tp"], sp["sink"]],
        out_specs=[sp["row512"], sp["lse"]],
        out_shape=[jax.ShapeDtypeStruct((T, A_Q_W), F32), jax.ShapeDtypeStruct((T, A_Q_HEADS), F32)],
        compiler_params=_cparams("parallel"),
    )(proj, proj, proj, proj, proj, cos, sin_s, cos, sin_s, sinks_b)


def _attn_bwd(proj, cos, sin_s, sinks_b, o, lse, dmix, name):
    T = proj.shape[0]
    sp = _attn_specs()

    def body(q_ref, kc_ref, kp_ref, vc_ref, vp_ref, cq_ref, sq_ref, cp_ref, sp_ref, sink_ref, o_ref, l_ref, do_ref,
             dq_ref, dkc_ref, dkp_ref, dvc_ref, dvp_ref, dsink_ref):
        n = pl.program_id(0)
        cq, sq = cq_ref[...], sq_ref[...]
        cp, sps = cp_ref[...], sp_ref[...]
        cq4, sq4 = jnp.tile(cq, (1, A_Q_W // LANES)), jnp.tile(sq, (1, A_Q_W // LANES))
        q = _rope(q_ref[...], cq4, sq4) * (A_HEAD_DIM ** -0.5)
        kc = _rope(kc_ref[...], cq, sq)
        kp = _rope(kp_ref[...], cp, sps)
        kk = jnp.concatenate([kp, kc], axis=0)
        vv = jnp.concatenate([vp_ref[...], vc_ref[...]], axis=0)
        mask = _band_mask(n)[None]
        lane = lax.broadcasted_iota(jnp.int32, (WINDOW, LANES), 1)
        do_all, o_all, l_all = do_ref[...], o_ref[...], l_ref[...]
        lane8 = lax.broadcasted_iota(jnp.int32, (WINDOW, A_Q_HEADS), 1)
        head8 = lax.broadcasted_iota(jnp.int32, (1, A_Q_HEADS), 1)
        prod = do_all * o_all
        qes, does, deltas, lhs = [], [], [], []
        for hq in range(A_Q_HEADS):
            pb, e, kvh = hq // 2, hq % 2, hq // 4
            blk = slice(pb * LANES, (pb + 1) * LANES)
            in_half = (lane >= A_HEAD_DIM) if e else (lane < A_HEAD_DIM)
            deltas.append(jnp.sum(jnp.where(in_half, prod[:, blk], 0.0), axis=-1, keepdims=True))
            qes.append(_place(q[:, blk], e, kvh))
            does.append(_place(do_all[:, blk], e, kvh))
            lhs.append(jnp.sum(jnp.where(lane8 == hq, l_all, 0.0), axis=-1, keepdims=True))
        qe, doe, delta, lh = jnp.stack(qes), jnp.stack(does), jnp.stack(deltas), jnp.stack(lhs)
        kk8 = jnp.broadcast_to(kk[None], (A_Q_HEADS,) + kk.shape)
        vv8 = jnp.broadcast_to(vv[None], (A_Q_HEADS,) + vv.shape)
        sk = jnp.stack([sink_ref[hq:hq + 1, 0:1] for hq in range(A_Q_HEADS)])
        s = _dot(qe, kk8, BNT)
        p = jnp.where(mask, jnp.exp(jnp.where(mask, s, NEG_BIG) - lh), 0.0)
        dvv = jnp.sum(_dot(p, doe, BTN), axis=0)
        ds = p * (_dot(doe, vv8, BNT) - delta)
        dkk = jnp.sum(_dot(ds, qe, BTN), axis=0)
        dqe = _dot(ds, kk8, BNN)
        dsink_h = -jnp.sum(jnp.exp(sk - lh) * delta, axis=(1, 2), keepdims=True)
        dqs = []
        dsk = jnp.zeros((1, A_Q_HEADS), F32)
        for pb in range(A_Q_HEADS // 2):
            halves = [_place(dqe[2 * pb + e], pb // 2, e) for e in range(2)]
            dqs.append(jnp.where(lane < A_HEAD_DIM, halves[0], halves[1]))
            for e in range(2):
                dsk = jnp.where(head8 == 2 * pb + e, dsink_h[2 * pb + e], dsk)
        dq = jnp.concatenate(dqs, axis=1) * (A_HEAD_DIM ** -0.5)
        dq_ref[...] = _rope_t(dq, cq4, sq4).astype(ACT_DTYPE)
        dkp_ref[...] = _rope_t(dkk[:WINDOW], cp, sps)
        dkc_ref[...] = _rope_t(dkk[WINDOW:], cq, sq)
        dvp_ref[...] = dvv[:WINDOW]
        dvc_ref[...] = dvv[WINDOW:]

        @pl.when(n == 0)
        def _():
            dsink_ref[...] = dsk

        @pl.when(n > 0)
        def _():
            dsink_ref[...] += dsk

    return pl.pallas_call(
        body, name=name, grid=(T // WINDOW,),
        in_specs=[sp["q"], sp["kc"], sp["kp"], sp["vc"], sp["vp"], sp["tq"], sp["tq"], sp["tp"], sp["tp"], sp["sink"],
                  sp["row512"], sp["lse"], sp["row512"]],
        out_specs=[sp["row512"], sp["row128"], sp["row128"], sp["row128"], sp["row128"],
                   pl.BlockSpec((1, A_Q_HEADS), lambda n: (0, 0))],
        out_shape=[jax.ShapeDtypeStruct((T, HYB_PAD), ACT_DTYPE)] + [jax.ShapeDtypeStruct((T, LANES), F32)] * 4
        + [jax.ShapeDtypeStruct((1, A_Q_HEADS), F32)],
        compiler_params=_cparams("arbitrary"),
    )(proj, proj, proj, proj, proj, cos, sin_s, cos, sin_s, sinks_b, o, lse, dmix)


def _shift_down(x, prev8, k):
    if k == 0:
        return x
    row = lax.broadcasted_iota(jnp.int32, prev8.shape, 0)
    r = pltpu.roll(x, k, axis=0)
    top = jnp.where(row < k, pltpu.roll(prev8, k, axis=0), r[:SUBLANES])
    return jnp.concatenate([top, r[SUBLANES:]], axis=0)


def _shift_up(x, next8, k):
    if k == 0:
        return x
    R = x.shape[0]
    row = lax.broadcasted_iota(jnp.int32, next8.shape, 0)
    r = pltpu.roll(x, R - k, axis=0)
    bot = jnp.where(row >= SUBLANES - k, pltpu.roll(next8, SUBLANES - k, axis=0), r[R - SUBLANES:])
    return jnp.concatenate([r[:R - SUBLANES], bot], axis=0)


def _conv(x, prev8, w):
    y = x * w[CONV_K - 1:CONV_K]
    for j in range(CONV_K - 1):
        y = y + _shift_down(x, prev8, CONV_K - 1 - j) * w[j:j + 1]
    return y


def _conv_bwd(x, prev8, w, dy, next8_dy):
    dx = dy * w[CONV_K - 1:CONV_K]
    dws = []
    for j in range(CONV_K - 1):
        k = CONV_K - 1 - j
        dx = dx + _shift_up(dy, next8_dy, k) * w[j:j + 1]
        dws.append(jnp.sum(dy * _shift_down(x, prev8, k), axis=0, keepdims=True))
    dws.append(jnp.sum(dy * x, axis=0, keepdims=True))
    return dx, dws


def _dnconv_fwd(proj, conv_w, name):
    T = proj.shape[0]
    R = _tile(T, 512)
    cb0 = HP_QKVB // A_Q_W

    def body(x_ref, w_ref, o_ref, prev_ref):
        i = pl.program_id(1)

        @pl.when(i == 0)
        def _():
            prev_ref[...] = jnp.zeros_like(prev_ref)

        x = x_ref[...]
        o_ref[...] = _silu(_conv(x, prev_ref[...], w_ref[...]))
        prev_ref[...] = x[R - SUBLANES:]

    return pl.pallas_call(
        body, name=name, grid=(3, T // R),
        in_specs=[pl.BlockSpec((R, B_W), lambda j, i: (i, cb0 + j)), pl.BlockSpec((CONV_K, B_W), lambda j, i: (0, j))],
        out_specs=pl.BlockSpec((R, B_W), lambda j, i: (i, j)),
        out_shape=jax.ShapeDtypeStruct((T, 3 * B_W), F32),
        scratch_shapes=[pltpu.VMEM((SUBLANES, B_W), F32)],
        compiler_params=_cparams("parallel", "arbitrary"),
    )(proj, conv_w)


def _dnconv_bwd(proj, conv_w, dy, dproj, name):
    T = proj.shape[0]
    R = _tile(T, 512)
    nb = T // R
    cb0 = HP_QKVB // A_Q_W
    r8 = R // SUBLANES

    def body(x_ref, xp_ref, w_ref, dy_ref, dproj_in, dx_ref, dw_ref, next_ref):
        i = pl.program_id(1)
        blk = nb - 1 - i

        @pl.when(i == 0)
        def _():
            next_ref[...] = jnp.zeros_like(next_ref)

        x = x_ref[...]
        prev8 = jnp.where(blk > 0, xp_ref[...], 0.0)
        w = w_ref[...]
        dpre = dy_ref[...] * _dsilu(_conv(x, prev8, w))
        dx, dw = _conv_bwd(x, prev8, w, dpre, next_ref[...])
        dx_ref[...] = dx.astype(ACT_DTYPE)
        next_ref[...] = dpre[:SUBLANES]

        @pl.when(i == 0)
        def _():
            for j in range(CONV_K):
                dw_ref[j:j + 1, :] = dw[j]

        @pl.when(i > 0)
        def _():
            for j in range(CONV_K):
                dw_ref[j:j + 1, :] += dw[j]

    return pl.pallas_call(
        body, name=name, grid=(3, nb),
        in_specs=[pl.BlockSpec((R, B_W), lambda j, i: (nb - 1 - i, cb0 + j)),
                  pl.BlockSpec((SUBLANES, B_W), lambda j, i: (jnp.maximum((nb - 1 - i) * r8 - 1, 0), cb0 + j)),
                  pl.BlockSpec((CONV_K, B_W), lambda j, i: (0, j)),
                  pl.BlockSpec((R, B_W), lambda j, i: (nb - 1 - i, j)), ANY],
        out_specs=[pl.BlockSpec((R, B_W), lambda j, i: (nb - 1 - i, cb0 + j)),
                   pl.BlockSpec((CONV_K, B_W), lambda j, i: (0, j))],
        out_shape=[jax.ShapeDtypeStruct(dproj.shape, dproj.dtype), jax.ShapeDtypeStruct((CONV_K, 3 * B_W), F32)],
        input_output_aliases={4: 0},
        scratch_shapes=[pltpu.VMEM((SUBLANES, B_W), F32)],
        compiler_params=_cparams("parallel", "arbitrary"),
    )(proj, proj, conv_w, dy, dproj)


DK_SCALE = B_HEAD_DIM ** -0.5


BNN = (((2,), (1,)), ((0,), (0,)))
BNT = (((2,), (2,)), ((0,), (0,)))
BTN = (((1,), (1,)), ((0,), (0,)))


def _tri_inv(a):
    C = a.shape[-1]
    ri = lax.broadcasted_iota(jnp.int32, (C, C), 0)
    ci = lax.broadcasted_iota(jnp.int32, (C, C), 1)
    x = jnp.where(ri == ci, 1.0, 0.0)[None] - a
    p = _dotf(a, a, BNN)
    span = 2
    while span < C:
        dot = _dotf if span <= 4 else _dot
        x = x + dot(x, p, BNN)
        span *= 2
        if span < C:
            p = dot(p, p, BNN)
    return x


def _dn_chunk(qc, kc, v, gcol, grow, bcol, s0, tm=None):
    C = B_CHUNK
    ri = lax.broadcasted_iota(jnp.int32, (C, C), 0)
    ci = lax.broadcasted_iota(jnp.int32, (C, C), 1)
    incl, strict = (ri >= ci)[None], (ri > ci)[None]
    rq = lax.rsqrt(jnp.sum(qc * qc, axis=-1, keepdims=True) + NORM_EPS)
    rk = lax.rsqrt(jnp.sum(kc * kc, axis=-1, keepdims=True) + NORM_EPS)
    qn = qc * rq
    q = qn * DK_SCALE
    k = kc * rk
    gc_col = jnp.sum(jnp.where(incl, grow, 0.0), axis=2, keepdims=True)
    gc_row = jnp.sum(jnp.where((ri <= ci)[None], gcol, 0.0), axis=1, keepdims=True)
    gl = jnp.sum(gcol, axis=1, keepdims=True)
    dincl = jnp.where(incl, jnp.exp(jnp.where(incl, gc_col - gc_row, 0.0)), 0.0)
    dstrict = jnp.where(strict, dincl, 0.0)
    eg = jnp.exp(gc_col)
    ekt = jnp.exp(gl - gc_col)
    egl = jnp.exp(gl)
    kb = k * bcol
    vb = v * bcol
    kbg = kb * eg
    a = _dot(kb, k, BNT) * dstrict
    if tm is None:
        tm = _tri_inv(a)
    u = _dot(tm, vb, BNN)
    w = _dot(tm, kbg, BNN)
    vn = u - _dot(w, s0, BNN)
    qk = _dot(q, k, BNT) * dincl
    qg = q * eg
    kt = k * ekt
    o = _dot(qg, s0, BNN) + _dot(qk, vn, BNN)
    s1 = s0 * egl + _dot(kt, vn, BTN)
    return dict(rq=rq, rk=rk, qn=qn, q=q, k=k, dincl=dincl, dstrict=dstrict, eg=eg, ekt=ekt, egl=egl, kb=kb, vb=vb,
                kbg=kbg, a=a, tm=tm, w=w, vn=vn, qk=qk, qg=qg, kt=kt, o=o, s1=s1, ri=ri[None], ci=ci[None])


def _heads(ref):
    return jnp.stack([ref[:, h * B_HEAD_DIM:(h + 1) * B_HEAD_DIM] for h in range(B_HEADS)])


def _store_heads(ref, val):
    for h in range(B_HEADS):
        ref[:, h * B_HEAD_DIM:(h + 1) * B_HEAD_DIM] = val[h]


def _dn_specs(N, rev):
    ix = (lambda n: N - 1 - n) if rev else (lambda n: n)
    wide = lambda cb: pl.BlockSpec((B_CHUNK, B_W), lambda n: (ix(n), cb))
    return dict(
        q=wide(0), k=wide(1), v=wide(2), z=wide(HP_Z // B_W), dob=wide(A_Q_W // B_W), out=wide(0),
        nw=pl.BlockSpec((1, LANES), lambda n: (0, 0)),
        row=pl.BlockSpec((B_HEADS, None, 1, B_CHUNK), lambda n: (0, ix(n), 0, 0)),
        state=pl.BlockSpec((B_HEADS, None, B_HEAD_DIM, B_HEAD_DIM), lambda n: (0, ix(n), 0, 0)),
        inv=pl.BlockSpec((B_HEADS, None, B_CHUNK, B_CHUNK), lambda n: (0, ix(n), 0, 0)),
    )


def _to_col(row):
    C = row.shape[-1]
    eye = lax.broadcasted_iota(jnp.int32, (C, C), 0) == lax.broadcasted_iota(jnp.int32, (C, C), 1)
    return jnp.sum(jnp.where(eye[None], row, 0.0), axis=2, keepdims=True)


def _to_row(col):
    C = col.shape[1]
    eye = lax.broadcasted_iota(jnp.int32, (C, C), 0) == lax.broadcasted_iota(jnp.int32, (C, C), 1)
    return jnp.sum(jnp.where(eye[None], col, 0.0), axis=1, keepdims=True)


def _dn_fwd(qkvc, proj, norm_w, grow, brow, name):
    T = qkvc.shape[0]
    N = T // B_CHUNK
    sp = _dn_specs(N, False)

    def body(q_ref, k_ref, v_ref, z_ref, nw_ref, gr_ref, br_ref, o_ref, st_ref, tm_ref, s_ref):
        n = pl.program_id(0)

        @pl.when(n == 0)
        def _():
            s_ref[...] = jnp.zeros_like(s_ref)

        s0 = s_ref[...]
        st_ref[...] = s0
        grow_v = gr_ref[...]
        f = _dn_chunk(_heads(q_ref), _heads(k_ref), _heads(v_ref), _to_col(grow_v), grow_v, _to_col(br_ref[...]), s0)
        o = f["o"]
        r = lax.rsqrt(jnp.mean(o * o, axis=-1, keepdims=True) + NORM_EPS)
        _store_heads(o_ref, o * r * nw_ref[...][None] * _silu(_heads(z_ref)))
        s_ref[...] = f["s1"]
        tm_ref[...] = f["tm"]

    return pl.pallas_call(
        body, name=name, grid=(N,),
        in_specs=[sp["q"], sp["k"], sp["v"], sp["z"], sp["nw"], sp["row"], sp["row"]],
        out_specs=[sp["out"], sp["state"], sp["inv"]],
        out_shape=[jax.ShapeDtypeStruct((T, B_W), F32),
                   jax.ShapeDtypeStruct((B_HEADS, N, B_HEAD_DIM, B_HEAD_DIM), F32),
                   jax.ShapeDtypeStruct((B_HEADS, N, B_CHUNK, B_CHUNK), F32)],
        scratch_shapes=[pltpu.VMEM((B_HEADS, B_HEAD_DIM, B_HEAD_DIM), F32)],
        compiler_params=_cparams("arbitrary"),
    )(qkvc, qkvc, qkvc, proj, norm_w, grow, brow)


def _dn_bwd(qkvc, proj, norm_w, grow, brow, states, invs, dmix, dproj, name):
    T = qkvc.shape[0]
    N = T // B_CHUNK
    sp = _dn_specs(N, True)
    C = B_CHUNK

    def body(q_ref, k_ref, v_ref, z_ref, nw_ref, gr_ref, br_ref, st_ref, tm_ref, dob_ref, dproj_in,
             dqkv_ref, dz_ref, dg_ref, db_ref, dnw_ref, ds_ref):
        n = pl.program_id(0)
        dq_ref = dqkv_ref.at[:, 0:B_W]
        dk_ref = dqkv_ref.at[:, B_W:2 * B_W]
        dv_ref = dqkv_ref.at[:, 2 * B_W:3 * B_W]

        @pl.when(n == 0)
        def _():
            ds_ref[...] = jnp.zeros_like(ds_ref)
            dnw_ref[...] = jnp.zeros_like(dnw_ref)

        s0 = st_ref[...]
        ds1 = ds_ref[...]
        v, z, nw, bcol_v = _heads(v_ref), _heads(z_ref), nw_ref[...][None], _to_col(br_ref[...])
        grow_v = gr_ref[...]
        f = _dn_chunk(_heads(q_ref), _heads(k_ref), v, _to_col(grow_v), grow_v, bcol_v, s0, tm=tm_ref[...])
        o, q, k, qn = f["o"], f["q"], f["k"], f["qn"]
        eg, ekt, egl = f["eg"], f["ekt"], f["egl"]
        tm, w, vn, kb, vb, kbg = f["tm"], f["w"], f["vn"], f["kb"], f["vb"], f["kbg"]
        qg, kt, qk, a = f["qg"], f["kt"], f["qk"], f["a"]
        ri, ci = f["ri"], f["ci"]

        dob_v = _heads(dob_ref)
        r = lax.rsqrt(jnp.mean(o * o, axis=-1, keepdims=True) + NORM_EPS)
        sz = _silu(z)
        on = o * r
        dnw_ref[...] += jnp.sum(dob_v * sz * on, axis=1, keepdims=True)
        _store_heads(dz_ref, (dob_v * on * nw * _dsilu(z)).astype(ACT_DTYPE))
        d_on = dob_v * sz * nw
        do = r * (d_on - on * jnp.mean(d_on * on, axis=-1, keepdims=True))

        dvn = _dot(qk, do, BTN) + _dot(kt, ds1, BNN)
        dqk = _dot(do, vn, BNT)
        dqg = _dot(do, s0, BNT)
        ds_ref[...] = _dot(qg, do, BTN) + egl * ds1 - _dot(w, dvn, BTN)
        dgl = jnp.sum(s0 * ds1, axis=(1, 2), keepdims=True) * egl
        dkt = _dot(vn, ds1, BNT)
        dw = -_dot(dvn, s0, BNT)
        dq = dqg * eg
        dgc = jnp.sum(dqg * qg, axis=-1, keepdims=True)
        dk = dkt * ekt
        t_kt = jnp.sum(dkt * kt, axis=-1, keepdims=True)
        dgl = dgl + jnp.sum(t_kt, axis=1, keepdims=True)
        dgc = dgc - t_kt
        dqkr = dqk * f["dincl"]
        dq = dq + _dot(dqkr, k, BNN)
        dk = dk + _dot(dqkr, q, BTN)
        e_qk = dqk * qk
        dgc = dgc + jnp.sum(e_qk, axis=-1, keepdims=True)
        dgc_row = -jnp.sum(e_qk, axis=1, keepdims=True)
        dtm = _dot(dvn, vb, BNT) + _dot(dw, kbg, BNT)
        dvb = _dot(tm, dvn, BTN)
        dkbg = _dot(tm, dw, BTN)
        dkb = dkbg * eg
        dgc = dgc + jnp.sum(dkbg * kbg, axis=-1, keepdims=True)
        da = -_dotf(tm, _dotf(dtm, tm, BNT), BTN)
        dkk = da * f["dstrict"]
        e_a = da * a
        dgc = dgc + jnp.sum(e_a, axis=-1, keepdims=True)
        dgc_row = dgc_row - jnp.sum(e_a, axis=1, keepdims=True)
        dkb = dkb + _dot(dkk, k, BNN)
        dk = dk + _dot(dkk, kb, BTN)
        dk = dk + dkb * bcol_v
        db_ref[...] = _to_row(jnp.sum(dkb * k, axis=-1, keepdims=True) + jnp.sum(dvb * v, axis=-1, keepdims=True))
        _store_heads(dv_ref, dvb * bcol_v)
        dgc_row = dgc_row + jnp.sum(jnp.where(ri == ci, dgc, 0.0), axis=1, keepdims=True)
        dg_ref[...] = jnp.sum(jnp.where(ci <= ri, _to_col(dgc_row), 0.0), axis=1, keepdims=True) + dgl
        dqs = dq * DK_SCALE
        _store_heads(dq_ref, f["rq"] * (dqs - qn * jnp.sum(dqs * qn, axis=-1, keepdims=True)))
        _store_heads(dk_ref, f["rk"] * (dk - k * jnp.sum(dk * k, axis=-1, keepdims=True)))

    return pl.pallas_call(
        body, name=name, grid=(N,),
        in_specs=[sp["q"], sp["k"], sp["v"], sp["z"], sp["nw"], sp["row"], sp["row"], sp["state"], sp["inv"], sp["dob"],
                  ANY],
        out_specs=[pl.BlockSpec((C, 3 * B_W), lambda n: (N - 1 - n, 0)), sp["z"], sp["row"], sp["row"],
                   pl.BlockSpec((B_HEADS, 1, LANES), lambda n: (0, 0, 0))],
        out_shape=[jax.ShapeDtypeStruct((T, 3 * B_W), F32), jax.ShapeDtypeStruct(dproj.shape, dproj.dtype),
                   jax.ShapeDtypeStruct((B_HEADS, N, 1, C), F32), jax.ShapeDtypeStruct((B_HEADS, N, 1, C), F32),
                   jax.ShapeDtypeStruct((B_HEADS, 1, LANES), F32)],
        input_output_aliases={10: 1},
        scratch_shapes=[pltpu.VMEM((B_HEADS, B_HEAD_DIM, B_HEAD_DIM), F32)],
        compiler_params=_cparams("arbitrary"),
    )(qkvc, qkvc, qkvc, proj, norm_w, grow, brow, states, invs, dmix, dproj)


def _lru_gates(xc, wa_ref, wx_ref, ba, bx, sp):
    pre_r, pre_i = [], []
    for hb in range(LRU_BLOCKS):
        xb = xc[:, hb * LRU_BLOCK_W:(hb + 1) * LRU_BLOCK_W]
        pre_r.append(_dot(xb, wa_ref[hb]))
        pre_i.append(_dot(xb, wx_ref[hb]))
    r = _sigmoid(jnp.concatenate(pre_r, axis=1) + ba)
    i = _sigmoid(jnp.concatenate(pre_i, axis=1) + bx)
    la = -LRU_C * r * sp
    a = jnp.exp(la)
    th = jnp.tanh(la)
    s = jnp.sqrt(-2.0 * th / (1.0 - th))
    return r, i, a, s


def _scan_down(a, b, h_in):
    R = a.shape[0]
    sub = lax.broadcasted_iota(jnp.int32, a.shape, 0) % SUBLANES
    d = 1
    while d < SUBLANES:
        ok = sub >= d
        b = a * jnp.where(ok, pltpu.roll(b, d, axis=0), 0.0) + b
        a = a * jnp.where(ok, pltpu.roll(a, d, axis=0), 1.0)
        d *= 2
    out, last = [], h_in
    for r0 in range(0, R, SUBLANES):
        h = b[r0:r0 + SUBLANES] + a[r0:r0 + SUBLANES] * last
        out.append(h)
        last = h[SUBLANES - 1:SUBLANES]
    return jnp.concatenate(out, axis=0)


def _scan_up(a, b, l_in):
    R = a.shape[0]
    sub = lax.broadcasted_iota(jnp.int32, a.shape, 0) % SUBLANES
    d = 1
    while d < SUBLANES:
        ok = sub < SUBLANES - d
        b = a * jnp.where(ok, pltpu.roll(b, R - d, axis=0), 0.0) + b
        a = a * jnp.where(ok, pltpu.roll(a, R - d, axis=0), 1.0)
        d *= 2
    out, nxt = [], l_in
    for r0 in range(R - SUBLANES, -1, -SUBLANES):
        lam = b[r0:r0 + SUBLANES] + a[r0:r0 + SUBLANES] * nxt
        out.append(lam)
        nxt = lam[0:1]
    return jnp.concatenate(out[::-1], axis=0)


def _rglru_fwd(proj, conv_w, conv_b, wa, wx, ba, bx, sp, name):
    T = proj.shape[0]
    R = _tile(T, 256)
    W = D_MODEL

    def body(p_ref, cw_ref, cb_ref, wa_ref, wx_ref, ba_ref, bx_ref, sp_ref, hg_ref, h_ref, prev_ref, hc_ref):
        i = pl.program_id(0)

        @pl.when(i == 0)
        def _():
            prev_ref[...] = jnp.zeros_like(prev_ref)
            hc_ref[...] = jnp.zeros_like(hc_ref)

        xr = p_ref[:, :W]
        gate = p_ref[:, W:]
        xc = _conv(xr, prev_ref[...], cw_ref[...]) + cb_ref[...]
        prev_ref[...] = xr[R - SUBLANES:]
        r, ig, a, s = _lru_gates(xc, wa_ref, wx_ref, ba_ref[...], bx_ref[...], sp_ref[...])
        h = _scan_down(a, s * ig * xc, hc_ref[SUBLANES - 1:SUBLANES, :])
        h_ref[...] = h
        hg_ref[...] = (h * _gelu(gate)).astype(ACT_DTYPE)
        hc_ref[...] = h[R - SUBLANES:]

    vec = pl.BlockSpec((1, W), lambda i: (0, 0))
    wsp = pl.BlockSpec((LRU_BLOCKS, LRU_BLOCK_W, LRU_BLOCK_W), lambda i: (0, 0, 0))
    row = pl.BlockSpec((R, W), lambda i: (i, 0))
    return pl.pallas_call(
        body, name=name, grid=(T // R,),
        in_specs=[pl.BlockSpec((R, 2 * W), lambda i: (i, 0)), pl.BlockSpec((CONV_K, W), lambda i: (0, 0)),
                  vec, wsp, wsp, vec, vec, vec],
        out_specs=[row, row],
        out_shape=[jax.ShapeDtypeStruct((T, W), ACT_DTYPE), jax.ShapeDtypeStruct((T, W), F32)],
        scratch_shapes=[pltpu.VMEM((SUBLANES, W), F32), pltpu.VMEM((SUBLANES, W), F32)],
        compiler_params=_cparams("arbitrary"),
    )(proj, conv_w, conv_b, wa, wx, ba, bx, sp)


def _rglru_bwd(proj, conv_w, conv_b, wa, wx, ba, bx, sp, h, dhg, name):
    T = proj.shape[0]
    R = _tile(T, 256)
    nb = T // R
    r8 = R // SUBLANES
    W = D_MODEL

    def body(p_ref, pp_ref, cw_ref, cb_ref, wa_ref, wx_ref, ba_ref, bx_ref, sp_ref, h_ref, hp_ref, dhg_ref,
             dp_ref, dcw_ref, dcb_ref, dwa_ref, dwx_ref, dba_ref, dbx_ref, dsp_ref, lam_ref, nxt_ref):
        step = pl.program_id(0)
        blk = nb - 1 - step

        @pl.when(step == 0)
        def _():
            lam_ref[...] = jnp.zeros_like(lam_ref)
            nxt_ref[...] = jnp.zeros_like(nxt_ref)

        xr = p_ref[:, :W]
        gate = p_ref[:, W:]
        first = blk > 0
        prev8 = jnp.where(first, pp_ref[:, :W], 0.0)
        hprev8 = jnp.where(first, hp_ref[...], 0.0)
        cw = cw_ref[...]
        spv = sp_ref[...]
        xc = _conv(xr, prev8, cw) + cb_ref[...]
        r, ig, a, s = _lru_gates(xc, wa_ref, wx_ref, ba_ref[...], bx_ref[...], spv)
        hv = h_ref[...]
        dhg_v = dhg_ref[...]
        dgate = dhg_v * hv * _dgelu(gate)
        dh = dhg_v * _gelu(gate)
        row = lax.broadcasted_iota(jnp.int32, (R, W), 0)
        last = row == R - 1
        a_up = jnp.where(last, 0.0, pltpu.roll(a, R - 1, axis=0))
        lam = _scan_up(a_up, dh + jnp.where(last, lam_ref[0:1, :], 0.0), jnp.zeros((1, W), F32))
        lam_ref[...] = (a * lam)[:SUBLANES]
        h_dn = _shift_down(hv, hprev8, 1)
        da = lam * h_dn
        bx_in = ig * xc
        dsv = lam * bx_in
        dig = lam * s * xc
        dxc = lam * s * ig
        dla = da * a - dsv * (a * a) / s
        dr = dla * (-LRU_C) * spv
        dsp = jnp.sum(dla * (-LRU_C) * r, axis=0, keepdims=True)
        dpr = dr * r * (1.0 - r)
        dpi = dig * ig * (1.0 - ig)
        dxc_parts, dwa_parts, dwx_parts = [], [], []
        for hb in range(LRU_BLOCKS):
            sl = slice(hb * LRU_BLOCK_W, (hb + 1) * LRU_BLOCK_W)
            xb, gr, gi = xc[:, sl], dpr[:, sl], dpi[:, sl]
            dxc_parts.append(_dot(gr, wa_ref[hb], NT) + _dot(gi, wx_ref[hb], NT))
            dwa_parts.append(_dot(xb, gr, TN))
            dwx_parts.append(_dot(xb, gi, TN))
        dxc = dxc + jnp.concatenate(dxc_parts, axis=1)
        dxr, dcw = _conv_bwd(xr, prev8, cw, dxc, nxt_ref[...])
        nxt_ref[...] = dxc[:SUBLANES]
        dp_ref[:, :W] = dxr.astype(ACT_DTYPE)
        dp_ref[:, W:] = dgate.astype(ACT_DTYPE)
        dcb = jnp.sum(dxc, axis=0, keepdims=True)
        dba = jnp.sum(dpr, axis=0, keepdims=True)
        dbx = jnp.sum(dpi, axis=0, keepdims=True)

        @pl.when(step == 0)
        def _():
            for j in range(CONV_K):
                dcw_ref[j:j + 1, :] = dcw[j]
            dcb_ref[...] = dcb
            dba_ref[...] = dba
            dbx_ref[...] = dbx
            dsp_ref[...] = dsp
            for hb in range(LRU_BLOCKS):
                dwa_ref[hb] = dwa_parts[hb]
                dwx_ref[hb] = dwx_parts[hb]

        @pl.when(step > 0)
        def _():
            for j in range(CONV_K):
                dcw_ref[j:j + 1, :] += dcw[j]
            dcb_ref[...] += dcb
            dba_ref[...] += dba
            dbx_ref[...] += dbx
            dsp_ref[...] += dsp
            for hb in range(LRU_BLOCKS):
                dwa_ref[hb] += dwa_parts[hb]
                dwx_ref[hb] += dwx_parts[hb]

    rv = lambda i: nb - 1 - i
    before = lambda i: jnp.maximum((nb - 1 - i) * r8 - 1, 0)
    vec = pl.BlockSpec((1, W), lambda i: (0, 0))
    cws = pl.BlockSpec((CONV_K, W), lambda i: (0, 0))
    wsp = pl.BlockSpec((LRU_BLOCKS, LRU_BLOCK_W, LRU_BLOCK_W), lambda i: (0, 0, 0))
    row = pl.BlockSpec((R, W), lambda i: (rv(i), 0))
    wshape = jax.ShapeDtypeStruct((LRU_BLOCKS, LRU_BLOCK_W, LRU_BLOCK_W), F32)
    vshape = jax.ShapeDtypeStruct((1, W), F32)
    return pl.pallas_call(
        body, name=name, grid=(nb,),
        in_specs=[pl.BlockSpec((R, 2 * W), lambda i: (rv(i), 0)), pl.BlockSpec((SUBLANES, 2 * W), lambda i: (before(i), 0)),
                  cws, vec, wsp, wsp, vec, vec, vec, row, pl.BlockSpec((SUBLANES, W), lambda i: (before(i), 0)), row],
        out_specs=[pl.BlockSpec((R, 2 * W), lambda i: (rv(i), 0)), cws, vec, wsp, wsp, vec, vec, vec],
        out_shape=[jax.ShapeDtypeStruct((T, 2 * W), ACT_DTYPE), jax.ShapeDtypeStruct((CONV_K, W), F32), vshape,
                   wshape, wshape, vshape, vshape, vshape],
        scratch_shapes=[pltpu.VMEM((SUBLANES, W), F32), pltpu.VMEM((SUBLANES, W), F32)],
        compiler_params=_cparams("arbitrary"),
    )(proj, proj, conv_w, conv_b, wa, wx, ba, bx, sp, h, h, dhg)


MESH = pl.DeviceIdType.MESH
ANY = pl.BlockSpec(memory_space=pl.ANY)


def _position():
    x, y, c = lax.axis_index("x"), lax.axis_index("y"), lax.axis_index("c")
    other_chips = [(1 - x, y), (x, 1 - y), (1 - x, 1 - y)]
    return x, y, c, other_chips


def _all_gather_weights(shards, name):
    n = len(shards)

    def body(*refs):
        ins, outs = refs[:n], refs[n:2 * n]
        send_sems, recv_sems = refs[2 * n:]
        x, y, c, chips = _position()
        me = 2 * x + y
        sibling = (x, y, 1 - c)

        def rcopy(t, k, src, dst, to):
            return pltpu.make_async_remote_copy(src_ref=src, dst_ref=dst, send_sem=send_sems.at[t, k],
                                                recv_sem=recv_sems.at[t, k], device_id=to, device_id_type=MESH)

        started = []
        for t in range(n):
            for j, (cx, cy) in enumerate(chips):
                cp = rcopy(t, j, ins[t].at[c], outs[t].at[me, c], (cx, cy, c))
                cp.start()
                started.append(cp)
        for t in range(n):
            for j, (cx, cy) in enumerate(chips):
                blk = outs[t].at[2 * cx + cy, c]
                rcopy(t, j, blk, blk, (cx, cy, c)).wait_recv()
                cp = rcopy(t, 3 + j, blk, blk, sibling)
                cp.start()
                started.append(cp)
        for t in range(n):
            for j, (cx, cy) in enumerate(chips):
                blk = outs[t].at[2 * cx + cy, 1 - c]
                rcopy(t, 3 + j, blk, blk, sibling).wait_recv()
        for cp in started:
            cp.wait_send()

    return pl.pallas_call(
        body, name=name, in_specs=[ANY] * n, out_specs=[ANY] * n,
        out_shape=[jax.ShapeDtypeStruct((N_CHIPS,) + s.shape, s.dtype) for s in shards],
        scratch_shapes=[pltpu.SemaphoreType.DMA((n, 6)), pltpu.SemaphoreType.DMA((n, 6))],
    )(*shards)


HBM = pl.BlockSpec(memory_space=pltpu.HBM)
SEM = pl.BlockSpec(memory_space=pltpu.SEMAPHORE)
EFFECT = pltpu.SideEffectType.DATAFLOW_SIDE_EFFECTING


def _gather_start(own, land, after, name):
    n = len(own)

    def body(*refs):
        own_refs, land_refs = refs[:n], refs[n:2 * n]
        send_sems = refs[2 * n + 1:3 * n + 1]
        recv_sems = refs[3 * n + 1:4 * n + 1]
        x, y, c, chips = _position()
        me = 2 * x + y
        for t in range(n):
            for cx, cy in chips:
                pltpu.make_async_remote_copy(
                    src_ref=own_refs[t], dst_ref=land_refs[t].at[me], send_sem=send_sems[t], recv_sem=recv_sems[t],
                    device_id=(cx, cy, c), device_id_type=MESH).start()

    sems = (pltpu.SemaphoreType.DMA(()),) * (2 * n)
    thru = [pltpu.HBM(a.shape, a.dtype) for a in list(own) + list(land)]
    out = pl.pallas_call(
        body, name=name, out_shape=(*sems, *thru),
        in_specs=[HBM] * (2 * n) + [pl.BlockSpec(memory_space=pl.ANY)], out_specs=(SEM,) * (2 * n) + (HBM,) * (2 * n),
        input_output_aliases={i: 2 * n + i for i in range(2 * n)},
        compiler_params=pltpu.CompilerParams(has_side_effects=EFFECT),
    )(*[pltpu.with_memory_space_constraint(a, pltpu.HBM) for a in list(own) + list(land)], after)
    return list(out[:n]), list(out[n:2 * n]), list(out[2 * n:3 * n]), list(out[3 * n:])


def _gather_wait(send_sems, recv_sems, own, land, after, name):
    n = len(own)

    def body(*refs):
        land_refs = refs[n:2 * n]
        s_sems, r_sems = refs[2 * n:3 * n], refs[3 * n:4 * n]
        x, y, c, _ = _position()
        for t in range(n):
            three = land_refs[t].at[pl.ds(0, N_CHIPS - 1)]
            cp = pltpu.make_async_remote_copy(src_ref=three, dst_ref=three, send_sem=s_sems[t], recv_sem=r_sems[t],
                                              device_id=(x, y, c), device_id_type=MESH)
            cp.wait_send()
            cp.wait_recv()

    thru = [pltpu.HBM(a.shape, a.dtype) for a in list(own) + list(land)]
    out = pl.pallas_call(
        body, name=name, out_shape=tuple(thru),
        in_specs=[HBM] * (2 * n) + [SEM] * (2 * n) + [pl.BlockSpec(memory_space=pl.ANY)], out_specs=(HBM,) * (2 * n),
        input_output_aliases={i: i for i in range(2 * n)},
        compiler_params=pltpu.CompilerParams(has_side_effects=EFFECT),
    )(*own, *land, *send_sems, *recv_sems, after)
    return list(out[n:])


def _rs_to_sibling(grads, name):
    n = len(grads)

    def body(*refs):
        ins, outs = refs[:n], refs[n:2 * n]
        send_sems, recv_sems = refs[2 * n:]
        x, y, c, _ = _position()
        cps = [pltpu.make_async_remote_copy(src_ref=ins[t].at[:, 1 - c], dst_ref=outs[t], send_sem=send_sems.at[t],
                                            recv_sem=recv_sems.at[t], device_id=(x, y, 1 - c), device_id_type=MESH)
               for t in range(n)]
        for cp in cps:
            cp.start()
        for cp in cps:
            cp.wait()

    return pl.pallas_call(
        body, name=name, in_specs=[ANY] * n, out_specs=[ANY] * n,
        out_shape=[jax.ShapeDtypeStruct((N_CHIPS,) + g.shape[2:], g.dtype) for g in grads],
        scratch_shapes=[pltpu.SemaphoreType.DMA((n,)), pltpu.SemaphoreType.DMA((n,))],
    )(*grads)


def _rs_across_start(parts, recv, name):
    n = len(parts)

    def body(*refs):
        part_refs, recv_refs = refs[:n], refs[n:2 * n]
        send_sems, recv_sems = refs[2 * n:3 * n], refs[3 * n:4 * n]
        token_ref = refs[-1]
        x, y, c, chips = _position()
        me = 2 * x + y
        for t in range(n):
            for cx, cy in chips:
                pltpu.make_async_remote_copy(src_ref=part_refs[t].at[2 * cx + cy], dst_ref=recv_refs[t].at[me],
                                             send_sem=send_sems[t], recv_sem=recv_sems[t], device_id=(cx, cy, c),
                                             device_id_type=MESH).start()
        token_ref[...] = jnp.zeros_like(token_ref)

    sems = (pltpu.SemaphoreType.DMA(()),) * (2 * n)
    thru = [pltpu.HBM(a.shape, a.dtype) for a in list(parts) + list(recv)]
    out = pl.pallas_call(
        body, name=name, out_shape=(*sems, *thru, jax.ShapeDtypeStruct((SUBLANES, LANES), F32)),
        in_specs=[HBM] * (2 * n), out_specs=(SEM,) * (2 * n) + (HBM,) * (2 * n) + (pl.BlockSpec(memory_space=pltpu.VMEM),),
        input_output_aliases={i: 2 * n + i for i in range(2 * n)},
        compiler_params=pltpu.CompilerParams(has_side_effects=EFFECT),
    )(*[pltpu.with_memory_space_constraint(a, pltpu.HBM) for a in list(parts) + list(recv)])
    return out[:n], out[n:2 * n], list(out[2 * n:3 * n]), list(out[3 * n:4 * n]), out[-1]


def _rs_across_wait(send_sems, recv_sems, parts, recv, after, name):
    n = len(parts)

    def body(*refs):
        recv_refs = refs[n:2 * n]
        s_sems, r_sems = refs[2 * n:3 * n], refs[3 * n:4 * n]
        x, y, c, _ = _position()
        for t in range(n):
            three = recv_refs[t].at[pl.ds(0, N_CHIPS - 1)]
            cp = pltpu.make_async_remote_copy(src_ref=three, dst_ref=three, send_sem=s_sems[t], recv_sem=r_sems[t],
                                              device_id=(x, y, c), device_id_type=MESH)
            cp.wait_send()
            cp.wait_recv()

    thru = [pltpu.HBM(a.shape, a.dtype) for a in list(parts) + list(recv)]
    out = pl.pallas_call(
        body, name=name, out_shape=tuple(thru),
        in_specs=[HBM] * (2 * n) + [SEM] * (2 * n) + [pl.BlockSpec(memory_space=pl.ANY)], out_specs=(HBM,) * (2 * n),
        input_output_aliases={i: i for i in range(2 * n)},
        compiler_params=pltpu.CompilerParams(has_side_effects=EFFECT),
    )(*parts, *recv, *send_sems, *recv_sems, after)
    return list(out[:n]), list(out[n:])


def _rs_join_halves(halves, name):
    n = len(halves)

    def body(*refs):
        ins, outs = refs[:n], refs[n:2 * n]
        send_sems, recv_sems = refs[2 * n:]
        x, y, c, _ = _position()
        cps = [pltpu.make_async_remote_copy(src_ref=ins[t].at[c], dst_ref=outs[t].at[c], send_sem=send_sems.at[t],
                                            recv_sem=recv_sems.at[t], device_id=(x, y, 1 - c), device_id_type=MESH)
               for t in range(n)]
        for cp in cps:
            cp.start()
        for t in range(n):
            blk = outs[t].at[1 - c]
            pltpu.make_async_remote_copy(src_ref=blk, dst_ref=blk, send_sem=send_sems.at[t], recv_sem=recv_sems.at[t],
                                         device_id=(x, y, 1 - c), device_id_type=MESH).wait_recv()
        for cp in cps:
            cp.wait_send()

    return pl.pallas_call(
        body, name=name, in_specs=[ANY] * n, out_specs=[ANY] * n,
        out_shape=[jax.ShapeDtypeStruct(h.shape, h.dtype) for h in halves],
        input_output_aliases={t: t for t in range(n)},
        scratch_shapes=[pltpu.SemaphoreType.DMA((n,)), pltpu.SemaphoreType.DMA((n,))],
    )(*halves)


def _all_gather_small(block, name):
    m_per, n = block.shape

    def body(x_ref, out_ref, send_sems, recv_sems, local_sem):
        x, y, c, chips = _position()
        me, sibling = (x, y, c), (x, y, 1 - c)

        def rows(px, py, pc):
            return out_ref.at[pl.ds((4 * px + 2 * py + pc) * m_per, m_per), :]

        def copy(k, blk, to, src=None):
            return pltpu.make_async_remote_copy(
                src_ref=rows(*blk) if src is None else src, dst_ref=rows(*blk), send_sem=send_sems.at[k],
                recv_sem=recv_sems.at[k], device_id=to, device_id_type=MESH)

        mine = pltpu.make_async_copy(x_ref, rows(*me), local_sem)
        mine.start()
        first = [copy(0, me, sibling, src=x_ref)]
        first += [copy(1 + j, me, (*chip, c), src=x_ref) for j, chip in enumerate(chips)]
        for cp in first:
            cp.start()
        passed = [copy(4 + j, (*chip, c), sibling) for j, chip in enumerate(chips)]
        for j, chip in enumerate(chips):
            copy(1 + j, (*chip, c), me).wait_recv()
            passed[j].start()
        copy(0, sibling, me).wait_recv()
        for j, chip in enumerate(chips):
            copy(4 + j, (*chip, 1 - c), me).wait_recv()
        for cp in first + passed:
            cp.wait_send()
        mine.wait()

    return pl.pallas_call(
        body, name=name, out_shape=jax.ShapeDtypeStruct((N_DEV * m_per, n), block.dtype),
        in_specs=[pl.BlockSpec(memory_space=pltpu.VMEM)], out_specs=pl.BlockSpec(memory_space=pltpu.VMEM),
        scratch_shapes=[pltpu.SemaphoreType.DMA((7,)), pltpu.SemaphoreType.DMA((7,)), pltpu.SemaphoreType.DMA],
    )(block)


def _row_tile(R, n):
    budget = 1 << 19
    if R * n <= budget or R % SUBLANES:
        return R
    t = R
    while t * n > budget and t % (2 * SUBLANES) == 0:
        t //= 2
    return t


def _pair_sum(g, recv, c_arr, name):
    _, _, R, n = g.shape
    tr = _row_tile(R, n)

    def body(c_ref, g_ref, r_ref, o_ref):
        o_ref[...] = (g_ref[...] + r_ref[...]).astype(ICI_DTYPE)

    grid_spec = pltpu.PrefetchScalarGridSpec(
        num_scalar_prefetch=1, grid=(N_CHIPS, R // tr),
        in_specs=[pl.BlockSpec((None, None, tr, n), lambda p, i, c: (p, c[0], i, 0)),
                  pl.BlockSpec((None, tr, n), lambda p, i, c: (p, i, 0))],
        out_specs=pl.BlockSpec((None, tr, n), lambda p, i, c: (p, i, 0)))
    return pl.pallas_call(
        body, name=name, grid_spec=grid_spec, out_shape=jax.ShapeDtypeStruct(recv.shape, ICI_DTYPE),
        compiler_params=_cparams("parallel", "parallel"),
    )(c_arr, g, recv)


def _chip_sum(recv, own, where, name):
    _, R, n = recv.shape
    tr = _row_tile(R, n)

    def body(w_ref, r0, r1, r2, r3, own_ref, o_ref):
        me = w_ref[0]
        terms = [jnp.where(me == k, own_ref[...], r[...]).astype(F32) for k, r in enumerate((r0, r1, r2, r3))]
        o_ref[...] = ((terms[0] + terms[1]) + terms[2]) + terms[3]

    def slot(k):
        return pl.BlockSpec((None, tr, n), lambda i, w: (w[1 + k], i, 0))

    grid_spec = pltpu.PrefetchScalarGridSpec(
        num_scalar_prefetch=1, grid=(R // tr,),
        in_specs=[slot(0), slot(1), slot(2), slot(3), pl.BlockSpec((None, tr, n), lambda i, w: (w[0], i, 0))],
        out_specs=pl.BlockSpec((None, tr, n), lambda i, w: (w[5], i, 0)))
    return pl.pallas_call(
        body, name=name, grid_spec=grid_spec, out_shape=jax.ShapeDtypeStruct((2, R, n), F32),
        compiler_params=_cparams("parallel"),
    )(where, recv, recv, recv, recv, own)


ADAM_C1 = 1.0 / (1.0 - ADAM_B1 ** ADAM_STEP)
ADAM_C2 = 1.0 / (1.0 - ADAM_B2 ** ADAM_STEP)


def _adamw_math(w, g, m, v):
    m = ADAM_B1 * m + (1.0 - ADAM_B1) * g
    v = ADAM_B2 * v + (1.0 - ADAM_B2) * (g * g)
    delta = -ADAM_LR * ((m * ADAM_C1) / (jnp.sqrt(v * ADAM_C2) + ADAM_EPS) + ADAM_WD * w)
    return delta, m, v


def _adamw(w, g, m, v, name):
    R, n = w.shape
    tr = _row_tile(R, n)

    def body(w_ref, g_ref, m_ref, v_ref, d_ref, nm_ref, nv_ref):
        d_ref[...], nm_ref[...], nv_ref[...] = _adamw_math(w_ref[...], g_ref[...], m_ref[...], v_ref[...])

    spec = pl.BlockSpec((tr, n), lambda i: (i, 0))
    return pl.pallas_call(
        body, name=name, grid=(R // tr,), in_specs=[spec] * 4, out_specs=[spec] * 3,
        out_shape=[jax.ShapeDtypeStruct((R, n), F32)] * 3, compiler_params=_cparams("parallel"),
    )(w, g, m, v)


def _adamw_half(w, g, m, v, half, prev, name):
    R, n = w.shape
    tr = _row_tile(R // 2, n)
    off = half * (R // 2 // tr)

    def body(*refs):
        w_ref, g_ref, m_ref, v_ref = refs[:4]
        go_ref, d_ref, nm_ref, nv_ref = refs[-4:]
        gv = g_ref[...]
        go_ref[...] = gv
        d_ref[...], nm_ref[...], nv_ref[...] = _adamw_math(w_ref[...], gv, m_ref[...], v_ref[...])

    rows = pl.BlockSpec((tr, n), lambda i: (i + off, 0))
    carried = [] if prev is None else list(prev)
    return pl.pallas_call(
        body, name=name, grid=(R // 2 // tr,),
        in_specs=[rows, pl.BlockSpec((tr, n), lambda i: (i, 0)), rows, rows] + [ANY] * len(carried),
        out_specs=[rows] * 4, out_shape=[jax.ShapeDtypeStruct((R, n), F32)] * 4,
        input_output_aliases={4 + i: i for i in range(len(carried))},
        compiler_params=_cparams("parallel"),
    )(w, g, m, v, *carried)


def _adamw_small(w, gall, m, v, name):
    M, n = w.shape

    def body(w_ref, g_ref, m_ref, v_ref, gs_ref, d_ref, nm_ref, nv_ref):
        g = g_ref[0:M, :]
        for d in range(1, N_DEV):
            g = g + g_ref[d * M:(d + 1) * M, :]
        gs_ref[...] = g
        d_ref[...], nm_ref[...], nv_ref[...] = _adamw_math(w_ref[...], g, m_ref[...], v_ref[...])

    return pl.pallas_call(
        body, name=name, out_shape=[jax.ShapeDtypeStruct((M, n), F32)] * 4,
    )(w, gall, m, v)


SMALL_ROWS = 24
MID_ROWS = 4


def _pack_small(ln1_g, ln1_b, ln2_g, ln2_b, norm_w, sinks, a_log, dt_bias):
    mixed = jnp.concatenate([norm_w.reshape(-1), sinks.reshape(-1), a_log.reshape(-1), dt_bias.reshape(-1)])
    mixed = jnp.pad(mixed, (0, D_MODEL - mixed.shape[0]))[None]
    pad = jnp.zeros((SMALL_ROWS - 4 * DEPTH - 1, D_MODEL), F32)
    return jnp.concatenate([ln1_g, ln1_b, ln2_g, ln2_b, mixed, pad], axis=0)


def _unpack_small(p):
    mixed = p[4 * DEPTH]
    return (p[0:4], p[4:8], p[8:12], p[12:16], mixed[0:256].reshape(2, 128), mixed[256:272].reshape(2, 8),
            mixed[272:280].reshape(2, 4), mixed[280:288].reshape(2, 4))


def _pack_mid(conv_w, rconv_w, rconv_b, b_a, b_x, lam):
    lead = conv_w.shape[0]
    flat = jnp.concatenate([conv_w.reshape(lead, -1), rconv_w.reshape(lead, -1), rconv_b, b_a, b_x, lam], axis=1)
    return jnp.pad(flat, ((0, 0), (0, MID_ROWS * D_MODEL - flat.shape[1]))).reshape(lead, MID_ROWS, D_MODEL)


def _unpack_mid(p):
    lead = p.shape[:-2]
    f = p.reshape(lead + (MID_ROWS * D_MODEL,))
    return (f[..., 0:1536].reshape(lead + (4, 384)), f[..., 1536:2560].reshape(lead + (4, 256)),
            f[..., 2560:2816], f[..., 2816:3072], f[..., 3072:3328], f[..., 3328:3584])


def _cols_from_chips(g):
    p, L, R, n = g.shape
    return g.transpose(1, 2, 0, 3).reshape(L, R, p * n)


def _rows_from_chips(g):
    p, L, R, n = g.shape
    return g.transpose(1, 0, 2, 3).reshape(L, p * R, n)


def _cols_to_chips(g):
    L, R, n4 = g.shape
    return g.reshape(L, R, N_CHIPS, n4 // N_CHIPS).transpose(2, 0, 1, 3)


def _rows_to_chips(g):
    L, R4, n = g.shape
    return g.reshape(L, N_CHIPS, R4 // N_CHIPS, n).transpose(1, 0, 2, 3)


def _halves(a):
    return a.reshape(2, -1, a.shape[-1])


def _pad_hyb_cols(w):
    z = jnp.zeros(w.shape[:-1] + (HYB_PAD - HP_BG - 2 * B_HEADS,), w.dtype)
    return jnp.concatenate([w[..., 0:512], w[..., 768:2304], w[..., 2304:2816], w[..., 512:768], w[..., 2816:2824], z], axis=-1)


def _unpad_hyb_cols(w):
    return jnp.concatenate([w[..., 0:512], w[..., 2560:2816], w[..., 512:2048], w[..., 2048:2560], w[..., 2816:2824]], axis=-1)


def _hybrid_fwd(x, W, j, tables, sfx, out_ready, ln):
    cos, sin_s = tables
    T = x.shape[0]
    N = T // B_CHUNK
    proj = _matmul(x, W["hyb_w_in"][j], "nn", "hyb_in" + sfx)
    sinks_b = jnp.broadcast_to(W["hyb_sinks"][j][:, None], (A_Q_HEADS, LANES))
    o_a, lse = _attn_fwd(proj, cos, sin_s, sinks_b, "attn_fwd" + sfx)
    qkvc = _dnconv_fwd(proj, W["hyb_conv_w"][j], "dnconv_fwd" + sfx)
    bg = proj[:, HP_BG:HP_BG + 2 * B_HEADS]
    beta = jax.nn.sigmoid(bg[:, :B_HEADS])
    pre = bg[:, B_HEADS:] + W["hyb_dt_bias"][j][None]
    g = -jnp.exp(W["hyb_a_log"][j])[None] * jax.nn.softplus(pre)
    grow = g.T.reshape(B_HEADS, N, 1, B_CHUNK)
    brow = beta.T.reshape(B_HEADS, N, 1, B_CHUNK)
    nw = W["hyb_norm_w"][j][None]
    o_b, states, invs = _dn_fwd(qkvc, proj, nw, grow, brow, "dn_fwd" + sfx)
    mix = jnp.concatenate([o_a, o_b], axis=1).astype(ACT_DTYPE)
    out_ready(mix)
    out = _matmul(mix, W["hyb_w_out"][j], "nn", "hyb_out" + sfx, epi=_epi_residual_ln, extra=ln, out_dtype=_ln_out(),
                  tm=512)
    res = dict(proj=proj, o_a=o_a, lse=lse, qkvc=qkvc, beta=beta, pre=pre, g=g, grow=grow, brow=brow,
               states=states, invs=invs, mix=mix, sinks_b=sinks_b, nw=nw)
    return out, res


def _hybrid_bwd(x, du, dub, W, j, res, tables, sfx):
    cos, sin_s = tables
    T = x.shape[0]
    proj = res["proj"]
    d_wout = _matmul(res["mix"], dub, "tn", "hyb_out_dw" + sfx)
    dmix = _matmul(dub, W["hyb_w_out"][j], "nt", "hyb_out_dx" + sfx)
    dproj, dkc, dkp, dvc, dvp, dsink = _attn_bwd(proj, cos, sin_s, res["sinks_b"], res["o_a"], res["lse"], dmix,
                                                  "attn_bwd" + sfx)
    zpad = jnp.zeros((WINDOW, LANES), F32)
    dk = dkc + jnp.concatenate([dkp[WINDOW:], zpad], axis=0)
    dv = dvc + jnp.concatenate([dvp[WINDOW:], zpad], axis=0)
    dqkvc, dproj, dg4, dbeta4, dnw = _dn_bwd(res["qkvc"], proj, res["nw"], res["grow"], res["brow"], res["states"],
                                             res["invs"], dmix, dproj, "dn_bwd" + sfx)
    dproj, dconv = _dnconv_bwd(proj, W["hyb_conv_w"][j], dqkvc, dproj, "dnconv_bwd" + sfx)
    dg = dg4.reshape(B_HEADS, T).T
    dbeta = dbeta4.reshape(B_HEADS, T).T
    beta = res["beta"]
    dbeta_logit = dbeta * beta * (1.0 - beta)
    da_logit = dg * (-jnp.exp(W["hyb_a_log"][j]))[None] * jax.nn.sigmoid(res["pre"])
    d_dt_bias = jnp.sum(da_logit, axis=0)
    d_a_log = jnp.sum(dg * res["g"], axis=0)
    zcols = jnp.zeros((T, HYB_PAD - HP_BG - 2 * B_HEADS), F32)
    tail = jnp.concatenate([dk, dv, dbeta_logit, da_logit, zcols], axis=1).astype(ACT_DTYPE)
    dproj = lax.dynamic_update_slice(dproj, tail, (0, HP_K))
    d_win = _matmul(x, dproj, "tn", "hyb_in_dw" + sfx)
    dx = _matmul(dproj, W["hyb_w_in"][j], "nt", "hyb_in_dx" + sfx, epi=_epi_add_residual, extra=du)
    grads = dict(hyb_w_in=d_win, hyb_w_out=d_wout, hyb_sinks=dsink[0], hyb_conv_w=dconv, hyb_a_log=d_a_log,
                 hyb_dt_bias=d_dt_bias, hyb_norm_w=jnp.sum(dnw[:, 0, :], axis=0))
    return dx, grads


def _rec_fwd(x, W, j, sfx, ln):
    proj = _matmul(x, W["rec_w_in"][j], "nn", "rec_in" + sfx)
    sp = jax.nn.softplus(-W["rec_lambda"][j])[None]
    hg, h = _rglru_fwd(proj, W["rec_conv_w"][j], W["rec_conv_b"][j][None], W["rec_w_a"][j], W["rec_w_x"][j],
                       W["rec_b_a"][j][None], W["rec_b_x"][j][None], sp, "rglru_fwd" + sfx)
    out = _matmul(hg, W["rec_w_out"][j], "nn", "rec_out" + sfx, epi=_epi_residual_ln, extra=ln, out_dtype=_ln_out(),
                  tm=512)
    return out, dict(proj=proj, hg=hg, h=h, sp=sp)


def _rec_bwd(x, du, dub, W, j, res, sfx):
    d_wout = _matmul(res["hg"], dub, "tn", "rec_out_dw" + sfx)
    dhg = _matmul(dub, W["rec_w_out"][j], "nt", "rec_out_dx" + sfx)
    dproj, dcw, dcb, dwa, dwx, dba, dbx, dsp = _rglru_bwd(
        res["proj"], W["rec_conv_w"][j], W["rec_conv_b"][j][None], W["rec_w_a"][j], W["rec_w_x"][j],
        W["rec_b_a"][j][None], W["rec_b_x"][j][None], res["sp"], res["h"], dhg, "rglru_bwd" + sfx)
    d_lam = dsp[0] * (-jax.nn.sigmoid(-W["rec_lambda"][j]))
    d_win = _matmul(x, dproj, "tn", "rec_in_dw" + sfx)
    dx = _matmul(dproj, W["rec_w_in"][j], "nt", "rec_in_dx" + sfx, epi=_epi_add_residual, extra=du)
    grads = dict(rec_w_in=d_win, rec_w_out=d_wout, rec_conv_w=dcw, rec_conv_b=dcb[0], rec_w_a=dwa, rec_w_x=dwx,
                 rec_b_a=dba[0], rec_b_x=dbx[0], rec_lambda=d_lam)
    return dx, grads


def _local_step(x, tgt, W, mlp_w, mixer_ready, on_group):
    T = x.shape[0]
    tables = _rope_tables(T)
    acts = []
    xb = x.astype(ACT_DTYPE)
    for layer in range(DEPTH):
        j, sfx = layer // 2, ""
        mixer_ready(layer, xb)
        ln1 = (x, W["ln1_g"][layer][None], W["ln1_b"][layer][None])
        if layer % 2 == 0:
            (x1, x1b, u1), res = _hybrid_fwd(xb, W, j, tables, sfx,
                                             functools.partial(mixer_ready, layer, out_projection=True), ln1)
        else:
            (x1, x1b, u1), res = _rec_fwd(xb, W, j, sfx, ln1)
        w1, w2, wl = mlp_w(layer, x1b)
        h1 = _matmul(x1b, w1, "nn", "mlp_up", out_dtype=ACT_DTYPE, b_chips=("j", wl))
        x2, x2b, u2 = _matmul(h1, w2, "nn", "mlp_down", a_fn=_relu2, b_chips=("k", wl), epi=_epi_residual_ln,
                              extra=(x1, W["ln2_g"][layer][None], W["ln2_b"][layer][None]), out_dtype=_ln_out())
        acts.append(dict(xb=xb, res=res, u1=u1, x1b=x1b, h1=h1, u2=u2))
        x, xb = x2, x2b
    dx, loss = _loss_head(x, tgt, "loss_head")
    per_layer = [None] * DEPTH
    d_w1 = [lax.empty((N_CHIPS, 2, D_MODEL, D_FF // N_CHIPS), F32) for _ in range(DEPTH // 2)]
    d_w2 = [lax.empty((N_CHIPS, 2, D_FF // N_CHIPS, D_MODEL), F32) for _ in range(DEPTH // 2)]
    token = None
    for layer in reversed(range(DEPTH)):
        j, a = layer // 2, acts[layer]
        ln2_g = W["ln2_g"][layer][None]
        if token is not None:
            ln2_g = ln2_g + token
        du2, du2b, dg2, db2 = _ln_bwd(a["u2"], ln2_g, dx, "ln_bwd")
        w1, w2, wl = mlp_w(layer, du2b)
        d_w2[j] = _matmul(a["h1"], du2b, "tn", "mlp_down_dw", a_fn=_relu2, out_chips=("i", layer % 2, d_w2[j]))
        dh1 = _matmul(du2b, w2, "nt", "mlp_down_dx", epi=_epi_drelu2, extra=a["h1"], out_dtype=ACT_DTYPE,
                      b_chips=("j", wl))
        d_w1[j] = _matmul(a["x1b"], dh1, "tn", "mlp_up_dw", out_chips=("j", layer % 2, d_w1[j]))
        dx1 = _matmul(dh1, w1, "nt", "mlp_up_dx", epi=_epi_add_residual, extra=du2, b_chips=("k", wl))
        du1, du1b, dg1, db1 = _ln_bwd(a["u1"], W["ln1_g"][layer][None], dx1, "ln_bwd")
        if layer % 2 == 0:
            dx, g = _hybrid_bwd(a["xb"], du1, du1b, W, j, a["res"], tables, "")
        else:
            dx, g = _rec_bwd(a["xb"], du1, du1b, W, j, a["res"], "")
        g.update(ln1_g=dg1[0], ln1_b=db1[0], ln2_g=dg2[0], ln2_b=db2[0])
        per_layer[layer] = g
        if layer % 2 == 0:
            token = on_group(j, per_layer[layer], per_layer[layer + 1], d_w1[j], d_w2[j])
    grads = {}
    for name in ("ln1_g", "ln1_b", "ln2_g", "ln2_b"):
        grads[name] = jnp.stack([per_layer[l][name] for l in range(DEPTH)])
    for name in ("hyb_norm_w", "hyb_sinks", "hyb_a_log", "hyb_dt_bias"):
        grads[name] = jnp.stack([per_layer[l][name] for l in (0, 2)])
    return loss, dx, grads


BIG = ("hyb_w_in", "hyb_w_out", "rec_w_in", "rec_w_out", "mlp_w1", "mlp_w2", "rec_w_a", "rec_w_x")
COL_SHARDED = ("hyb_w_in", "rec_w_in", "mlp_w1")
CHIP_MAJOR = ("mlp_w1", "mlp_w2")
MID = ("hyb_conv_w", "rec_conv_w", "rec_conv_b", "rec_b_a", "rec_b_x", "rec_lambda")
SMALL = ("ln1_g", "ln1_b", "ln2_g", "ln2_b", "hyb_norm_w", "hyb_sinks", "hyb_a_log", "hyb_dt_bias")
WEIGHTS = ("hyb_w_in", "hyb_sinks", "hyb_conv_w", "hyb_a_log", "hyb_dt_bias", "hyb_norm_w", "hyb_w_out", "rec_w_in",
           "rec_conv_w", "rec_conv_b", "rec_w_a", "rec_b_a", "rec_w_x", "rec_b_x", "rec_lambda", "rec_w_out", "ln1_g",
           "ln1_b", "mlp_w1", "mlp_w2", "ln2_g", "ln2_b")


def _gather_full_weights(w):
    wb = {k: w[k].astype(MXU_DTYPE) for k in BIG}
    now = ("hyb_w_in",)
    shards = [_halves(wb[k][:1]) for k in now]
    shards.append(_pack_mid(*[w[k] for k in MID]))
    got = _all_gather_weights(shards, "all_gather_weights")
    me = 2 * lax.axis_index("x") + lax.axis_index("y")
    got = [lax.dynamic_update_slice(g, s[None], (me, 0, 0, 0)) for s, g in zip(shards, got)]

    def full(k, g):
        g = g.reshape((N_CHIPS,) + w[k].shape[1:])
        if k in CHIP_MAJOR:
            return g[:, None]
        if k in ("rec_w_a", "rec_w_x"):
            return g.transpose(1, 0, 2, 3).reshape(LRU_BLOCKS, LRU_BLOCK_W, LRU_BLOCK_W)
        f = _cols_from_chips(g[:, None])[0] if k in COL_SHARDED else _rows_from_chips(g[:, None])[0]
        return _pad_hyb_cols(f) if k == "hyb_w_in" else f

    rec = ("rec_w_in", "rec_w_out", "rec_w_a", "rec_w_x")
    groups = [(("hyb_w_out",), 0), (CHIP_MAJOR, 0), (rec, 0), (CHIP_MAJOR, 1), (("hyb_w_in", "hyb_w_out"), 1),
              (CHIP_MAJOR, 2), (rec, 1), (CHIP_MAJOR, 3)]
    own = [wb[k][j] for names, j in groups for k in names]
    land = [lax.dynamic_update_slice(lax.empty((N_CHIPS,) + o.shape, o.dtype), o[None], (me,) + (0,) * o.ndim)
            for o in own]
    send_sems, recv_sems, own, land = _gather_start(own, land, got[-1], "gather_start")
    W = {k: [None] * w[k].shape[0] for k in BIG}
    for k, g in zip(now, got[:-1]):
        W[k][0] = full(k, g)
    arrived = [0]

    def ensure(upto, after):
        while arrived[0] <= upto:
            gi = arrived[0]
            names, j = groups[gi]
            lo = sum(len(nm) for nm, _ in groups[:gi])
            sl = slice(lo, lo + len(names))
            got_g = _gather_wait(send_sems[sl], recv_sems[sl], own[sl], land[sl], after, "gather_wait_%d" % gi)
            for k, g in zip(names, got_g):
                W[k][j] = full(k, g)
            arrived[0] += 1

    def mixer_ready(layer, after, out_projection=False):
        if layer:
            ensure({1: 2, 2: 4, 3: 6}[layer], after)
        elif out_projection:
            ensure(0, after)

    def mlp_w(layer, after):
        ensure({0: 1, 1: 3, 2: 5, 3: 7}[layer], after)
        return W["mlp_w1"][layer], W["mlp_w2"][layer], 0

    conv_w, rconv_w, rconv_b, b_a, b_x, lam = _unpack_mid(got[-1])
    W["hyb_conv_w"] = conv_w.transpose(1, 2, 0, 3).reshape(2, CONV_K, 3 * B_W)
    W["rec_conv_w"] = rconv_w.transpose(1, 2, 0, 3).reshape(2, CONV_K, D_MODEL)
    for k, v in (("rec_conv_b", rconv_b), ("rec_b_a", b_a), ("rec_b_x", b_x), ("rec_lambda", lam)):
        W[k] = v.transpose(1, 0, 2).reshape(2, D_MODEL)
    for k in SMALL:
        W[k] = w[k]
    return W, mlp_w, mixer_ready


def _group_by_chip(gh, gr, d_w1, d_w2):
    g = dict(gh, **gr)
    g["hyb_w_in"] = _unpad_hyb_cols(g["hyb_w_in"])
    out = []
    for k in BIG:
        if k == "mlp_w1":
            v = d_w1
        elif k == "mlp_w2":
            v = d_w2
        elif k in ("rec_w_a", "rec_w_x"):
            v = g[k].reshape(1, LRU_BLOCKS, N_CHIPS, LRU_BLOCK_W // N_CHIPS, LRU_BLOCK_W).transpose(2, 0, 1, 3, 4)
        else:
            v = _cols_to_chips(g[k][None]) if k in COL_SHARDED else _rows_to_chips(g[k][None])
        out.append(v.reshape(N_CHIPS, 2, -1, v.shape[-1]))
    conv_w = g["hyb_conv_w"].reshape(CONV_K, N_CHIPS, -1).transpose(1, 0, 2)
    rconv_w = g["rec_conv_w"].reshape(CONV_K, N_CHIPS, -1).transpose(1, 0, 2)
    vecs = [g[k].reshape(N_CHIPS, -1) for k in ("rec_conv_b", "rec_b_a", "rec_b_x", "rec_lambda")]
    out.append(_pack_mid(conv_w, rconv_w, *vecs).reshape(N_CHIPS, 2, MID_ROWS // 2, D_MODEL))
    return out


def kernel(x, hyb_w_in, hyb_sinks, hyb_conv_w, hyb_a_log, hyb_dt_bias, hyb_norm_w, hyb_w_out, rec_w_in, rec_conv_w, rec_conv_b, rec_w_a, rec_b_a, rec_w_x, rec_b_x, rec_lambda, rec_w_out, ln1_g, ln1_b, mlp_w1, mlp_w2, ln2_g, ln2_b, loss_target, m_hyb_w_in, m_hyb_sinks, m_hyb_conv_w, m_hyb_a_log, m_hyb_dt_bias, m_hyb_norm_w, m_hyb_w_out, m_rec_w_in, m_rec_conv_w, m_rec_conv_b, m_rec_w_a, m_rec_b_a, m_rec_w_x, m_rec_b_x, m_rec_lambda, m_rec_w_out, m_ln1_g, m_ln1_b, m_mlp_w1, m_mlp_w2, m_ln2_g, m_ln2_b, v_hyb_w_in, v_hyb_sinks, v_hyb_conv_w, v_hyb_a_log, v_hyb_dt_bias, v_hyb_norm_w, v_hyb_w_out, v_rec_w_in, v_rec_conv_w, v_rec_conv_b, v_rec_w_a, v_rec_b_a, v_rec_w_x, v_rec_b_x, v_rec_lambda, v_rec_w_out, v_ln1_g, v_ln1_b, v_mlp_w1, v_mlp_w2, v_ln2_g, v_ln2_b):
    args = locals()
    w = {k: args[k] for k in WEIGHTS}
    m = {k: args["m_" + k] for k in WEIGHTS}
    v = {k: args["v_" + k] for k in WEIGHTS}

    W, mlp_w, mixer_ready = _gather_full_weights(w)

    core = lax.axis_index("c").astype(jnp.int32)
    me = (2 * lax.axis_index("x") + lax.axis_index("y")).astype(jnp.int32)
    slots = jnp.arange(N_CHIPS, dtype=jnp.int32)
    where = jnp.concatenate([me[None], jnp.where(slots == me, (slots + 1) % N_CHIPS, slots), core[None]])
    state = {}

    def on_group(group, gh, gr, d_w1, d_w2):
        by_chip = _group_by_chip(gh, gr, d_w1, d_w2)
        from_sibling = _rs_to_sibling(by_chip, "rs_to_sibling")
        pair = [_pair_sum(g, r, core[None], "pair_sum") for g, r in zip(by_chip, from_sibling)]
        recv = [lax.empty(p.shape, p.dtype) for p in pair]
        state[group] = _rs_across_start(pair, recv, "rs_across_start_%d" % group)
        return state[group][4][0, 0]

    def finish_group(group, after, prev):
        send_sems, recv_sems, pair, recv, _ = state[group]
        pair, from_chips = _rs_across_wait(send_sems, recv_sems, pair, recv, after, "rs_across_wait_%d" % group)
        half = [_chip_sum(r, p, where, "chip_sum") for r, p in zip(from_chips, pair)]
        joined = _rs_join_halves(half, "rs_join_halves")
        outs = {}
        for k, g in zip(BIG, joined[:-1]):
            n = g.shape[-1]
            outs[k] = _adamw_half(w[k].reshape(-1, n), g.reshape(-1, n), m[k].reshape(-1, n), v[k].reshape(-1, n),
                                  group, None if prev is None else prev[k], "adamw")
        return outs, joined[-1]

    loss, dx, grads = _local_step(x[0], loss_target[0], W, mlp_w, mixer_ready, on_group)
    loss = lax.psum(loss[0, 0], ("x", "y", "c"))
    g_out, d_out, m_out, v_out = {}, {}, {}, {}

    upper, mid_upper = finish_group(1, state[0][4], None)
    small_g = _pack_small(*[grads[k] for k in SMALL])
    small_all = _all_gather_small(small_g, "all_gather_small")
    sw, sm, sv = (_pack_small(*[t[k] for k in SMALL]) for t in (w, m, v))
    sg, sd, snm, snv = _adamw_small(sw, small_all, sm, sv, "adamw_small")
    for dst, packed in ((g_out, sg), (d_out, sd), (m_out, snm), (v_out, snv)):
        for k, val in zip(SMALL, _unpack_small(packed)):
            dst[k] = val

    done = sum([upper[k][1][0, 0] for k in BIG], sg[0, 0]).reshape(1, 1)
    both, mid_lower = finish_group(0, done, upper)
    for k in BIG:
        shape = w[k].shape
        g_out[k], d_out[k], m_out[k], v_out[k] = (t.reshape(shape) for t in both[k])
    mid_w, mid_m, mid_v = (_pack_mid(*[t[k] for k in MID]).reshape(-1, D_MODEL) for t in (w, m, v))
    mid_g = jnp.concatenate([mid_lower.reshape(-1, D_MODEL), mid_upper.reshape(-1, D_MODEL)])
    mid_d, mid_nm, mid_nv = _adamw(mid_w, mid_g, mid_m, mid_v, "adamw_mid")
    for dst, packed in ((g_out, mid_g), (d_out, mid_d), (m_out, mid_nm), (v_out, mid_nv)):
        for k, val in zip(MID, _unpack_mid(packed.reshape(2, MID_ROWS, D_MODEL))):
            dst[k] = val.reshape(w[k].shape)

    return (loss, dx[None], *[g_out[k] for k in WEIGHTS], *[d_out[k] for k in WEIGHTS],
            *[m_out[k] for k in WEIGHTS], *[v_out[k] for k in WEIGHTS])
```

```python
import functools

import jax
import jax.numpy as jnp
import numpy as np
from jax import lax
from jax.experimental import pallas as pl
from jax.experimental.pallas import tpu as pltpu

F32 = jnp.float32
MXU_DTYPE = jnp.bfloat16
ACT_DTYPE = jnp.bfloat16
ICI_DTYPE = jnp.bfloat16

D_MODEL = 1024
DEPTH = 4
A_HEAD_DIM = 64
A_Q_HEADS = 8
A_KV_HEADS = 2
WINDOW = 128
ROPE_THETA = 10000.0
B_HEADS = 4
B_HEAD_DIM = 128
B_CHUNK = 64
CONV_K = 4
LRU_BLOCKS = 4
LRU_BLOCK_W = D_MODEL // LRU_BLOCKS
LRU_C = 8.0
D_FF = 4 * D_MODEL
A_Q_W = A_Q_HEADS * A_HEAD_DIM
A_KV_W = A_KV_HEADS * A_HEAD_DIM
B_W = B_HEADS * B_HEAD_DIM
HYB_PROJ = A_Q_W + 2 * A_KV_W + 4 * B_W + 2 * B_HEADS
DN_ALPHA = (2 * DEPTH) ** 0.25
LN_EPS = 1e-5
NORM_EPS = 1e-6
ADAM_LR = 0.001
ADAM_B1 = 0.9
ADAM_B2 = 0.999
ADAM_EPS = 1e-08
ADAM_WD = 0.01
ADAM_STEP = 10

HP_Q = 0
HP_QKVB = 512
HP_Z = 2048
HP_K = 2560
HP_V = 2688
HP_BG = 2816
HYB_PAD = 3072

N_CHIPS = 4
N_DEV = 8
V7X_VMEM_LIMIT = 48 * 1024 * 1024
LANES = 128
SUBLANES = 8
NEG_BIG = -1e30

NN = (((1,), (0,)), ((), ()))
NT = (((1,), (1,)), ((), ()))
TN = (((0,), (0,)), ((), ()))


def _cparams(*sem):
    return pltpu.CompilerParams(dimension_semantics=sem, vmem_limit_bytes=V7X_VMEM_LIMIT)


def _dot(a, b, dims=NN):
    return lax.dot_general(a.astype(MXU_DTYPE), b.astype(MXU_DTYPE), dims, preferred_element_type=F32)


def _split_bf16(a):
    hi = a.astype(jnp.bfloat16)
    return hi, (a - hi.astype(F32)).astype(jnp.bfloat16)


def _dotf(a, b, dims=NN):
    ah, al = _split_bf16(a)
    bh, bl = _split_bf16(b)
    dg = functools.partial(lax.dot_general, dimension_numbers=dims, preferred_element_type=F32)
    return dg(ah, bh) + (dg(ah, bl) + dg(al, bh))


def _tile(dim, pref):
    t = min(dim, pref)
    while dim % t:
        t //= 2
    return t


def _sigmoid(x):
    return 1.0 / (1.0 + jnp.exp(-x))


def _silu(x):
    return x * _sigmoid(x)


def _dsilu(x):
    s = _sigmoid(x)
    return s * (1.0 + x * (1.0 - s))


GELU_C = 0.7978845608028654
GELU_A = 0.044715


def _gelu(x):
    return 0.5 * x * (1.0 + jnp.tanh(GELU_C * (x + GELU_A * x * x * x)))


def _dgelu(x):
    t = jnp.tanh(GELU_C * (x + GELU_A * x * x * x))
    return 0.5 * (1.0 + t) + 0.5 * x * (1.0 - t * t) * GELU_C * (1.0 + 3.0 * GELU_A * x * x)


def _matmul(a, b, mode, name, *, tm=1024, tn=1024, tk=1024, a_fn=None, epi=None, extra=None, out_dtype=F32,
            b_chips=None, out_chips=None):
    chunk = tk
    if mode == "tn":
        K, M = a.shape
        if K <= 4 * tk:
            tk, tm = K, tm // 2
    else:
        M, K = a.shape
    whole_k = False
    if b_chips is not None:
        g, b_layer = b_chips
        r, n = b.shape[2:]
        n_dim, k_dim = (r, n) if mode == "nt" else (n, r)
        N = N_CHIPS * n_dim if g == "j" else n_dim
        assert K == (N_CHIPS * k_dim if g == "k" else k_dim)
        if g == "j":
            tn = n_dim
        else:
            whole_k, tk, tm = True, K, tm // 2
    elif mode == "nt":
        N = b.shape[0]
        if tk < K <= 3 * tk:
            tk, tm = K, tm // 2
    else:
        N = b.shape[1]
    if out_chips is not None:
        og, o_layer, o_buf = out_chips
        if og == "j":
            tn = o_buf.shape[3]
        else:
            tm = min(tm, o_buf.shape[2])
    tm, tn, tk = _tile(M, tm), _tile(N, tn), _tile(K, tk)
    nk = K // tk
    if mode == "tn":
        a_spec = pl.BlockSpec((tk, tm), lambda i, j, k: (k, i))
    else:
        a_spec = pl.BlockSpec((tm, tk), lambda i, j, k: (i, k))
    b_block = (tn, tk) if mode == "nt" else (tk, tn)
    if b_chips is None:
        b_spec = pl.BlockSpec(b_block, (lambda i, j, k: (j, k)) if mode == "nt" else (lambda i, j, k: (k, j)))
    elif mode == "nt" and whole_k:
        b_spec = pl.BlockSpec((N_CHIPS, None, tn, k_dim), lambda i, j, k: (0, b_layer, j, 0))
    elif mode == "nt":
        b_spec = pl.BlockSpec((None, None) + b_block, lambda i, j, k: (j, b_layer, 0, k))
    elif whole_k:
        b_spec = pl.BlockSpec((N_CHIPS, None, k_dim, tn), lambda i, j, k: (0, b_layer, 0, j),
                              pipeline_mode=pl.Buffered(1) if N == tn else None)
    else:
        b_spec = pl.BlockSpec((None, None) + b_block, lambda i, j, k: (j, b_layer, k, 0))
    o_spec = pl.BlockSpec((tm, tn), lambda i, j, k: (i, j))
    e_spec = o_spec
    if out_chips is not None:
        per = o_buf.shape[2] // tm
        o_spec = pl.BlockSpec((None, None, tm, tn), (lambda i, j, k: (j, o_layer, i, 0)) if og == "j"
                              else (lambda i, j, k: (i // per, o_layer, i % per, j)))
    dims = {"nn": NN, "nt": NT, "tn": TN}[mode]
    extras = () if extra is None else (extra if isinstance(extra, tuple) else (extra,))
    out_dtypes = out_dtype if isinstance(out_dtype, tuple) else (out_dtype,)
    n_in = 2 + len(extras) + (out_chips is not None)

    def body(*refs):
        a_ref, b_ref = refs[0], refs[1]
        e_refs = refs[2:2 + len(extras)]
        o_refs = refs[n_in:n_in + len(out_dtypes)]
        av = a_ref[...]
        if a_fn is not None:
            av = a_fn(av)
        if whole_k and mode == "nt":
            part = _dot(av[:, :k_dim], b_ref[0], dims)
            for chip in range(1, N_CHIPS):
                part = part + _dot(av[:, chip * k_dim:(chip + 1) * k_dim], b_ref[chip], dims)
        elif mode == "tn" and tk > chunk:
            part = _dot(av[:chunk], b_ref[0:chunk, :], dims)
            for c0 in range(chunk, tk, chunk):
                part = part + _dot(av[c0:c0 + chunk], b_ref[c0:c0 + chunk, :], dims)
        else:
            bv = b_ref[...]
            if whole_k:
                bv = bv.reshape(K, tn)
            part = _dot(av, bv, dims)

        def finish(acc):
            if epi is not None:
                acc = epi(acc, *[e[...] for e in e_refs])
            for o_ref, val, dt in zip(o_refs, acc if isinstance(acc, tuple) else (acc,), out_dtypes):
                o_ref[...] = val.astype(dt)

        if nk == 1:
            finish(part)
        else:
            acc_ref = refs[-1]
            k = pl.program_id(2)

            @pl.when(k == 0)
            def _():
                acc_ref[...] = part

            @pl.when(k > 0)
            def _():
                acc_ref[...] += part

            @pl.when(k == nk - 1)
            def _():
                finish(acc_ref[...])

    row_spec = pl.BlockSpec((1, tn), lambda i, j, k: (0, j))
    in_specs = [a_spec, b_spec] + [row_spec if e.shape[0] == 1 else e_spec for e in extras]
    args = (a, b) + extras
    out_shape = [jax.ShapeDtypeStruct((M, N), dt) for dt in out_dtypes]
    out_specs = [o_spec] * len(out_dtypes)
    aliases = {}
    if out_chips is not None:
        in_specs.append(pl.BlockSpec(memory_space=pl.ANY))
        args += (o_buf,)
        out_shape = [jax.ShapeDtypeStruct(o_buf.shape, o_buf.dtype)]
        aliases = {n_in - 1: 0}
    if not isinstance(out_dtype, tuple):
        out_shape, out_specs = out_shape[0], out_specs[0]
    return pl.pallas_call(
        body, name=name, grid=(M // tm, N // tn, nk), in_specs=in_specs, out_specs=out_specs, out_shape=out_shape,
        input_output_aliases=aliases,
        scratch_shapes=[pltpu.VMEM((tm, tn), F32)] if nk > 1 else [],
        compiler_params=_cparams("parallel", "parallel", "arbitrary"),
    )(*args)


def _relu2(v):
    r = jnp.maximum(v, 0.0)
    return r * r


def _epi_drelu2(acc, h):
    return acc * (2.0 * jnp.maximum(h, 0.0))


def _epi_add_residual(acc, du):
    return acc + DN_ALPHA * du


def _epi_residual_ln(acc, x, g, b):
    u = DN_ALPHA * x + acc
    mu = jnp.mean(u, axis=-1, keepdims=True)
    d = u - mu
    var = jnp.mean(d * d, axis=-1, keepdims=True)
    o = d * lax.rsqrt(var + LN_EPS) * g + b
    return o, o, u


def _ln_out():
    return (F32, ACT_DTYPE, F32)


def _ln_bwd(u, g, dout, name):
    T, D = u.shape
    tr = _tile(T, 512)

    def body(u_ref, g_ref, d_ref, du_ref, dub_ref, dg_ref, db_ref):
        i = pl.program_id(0)
        u = u_ref[...]
        mu = jnp.mean(u, axis=-1, keepdims=True)
        d = u - mu
        rstd = lax.rsqrt(jnp.mean(d * d, axis=-1, keepdims=True) + LN_EPS)
        xhat = d * rstd
        dout_v = d_ref[...]
        dxh = dout_v * g_ref[...]
        m1 = jnp.mean(dxh, axis=-1, keepdims=True)
        m2 = jnp.mean(dxh * xhat, axis=-1, keepdims=True)
        du = rstd * (dxh - m1 - xhat * m2)
        du_ref[...] = du
        dub_ref[...] = du.astype(ACT_DTYPE)
        pg = jnp.sum(dout_v * xhat, axis=0, keepdims=True)
        pb = jnp.sum(dout_v, axis=0, keepdims=True)

        @pl.when(i == 0)
        def _():
            dg_ref[...] = pg
            db_ref[...] = pb

        @pl.when(i > 0)
        def _():
            dg_ref[...] += pg
            db_ref[...] += pb

    row = pl.BlockSpec((tr, D), lambda i: (i, 0))
    vec = pl.BlockSpec((1, D), lambda i: (0, 0))
    return pl.pallas_call(
        body, name=name, grid=(T // tr,), in_specs=[row, vec, row], out_specs=[row, row, vec, vec],
        out_shape=[jax.ShapeDtypeStruct((T, D), F32), jax.ShapeDtypeStruct((T, D), ACT_DTYPE),
                   jax.ShapeDtypeStruct((1, D), F32), jax.ShapeDtypeStruct((1, D), F32)],
        compiler_params=_cparams("arbitrary"),
    )(u, g, dout)


def _loss_head(y, tgt, name):
    T, D = y.shape
    tr = _tile(T, 512)

    def body(y_ref, t_ref, dy_ref, l_ref):
        i = pl.program_id(0)
        e = y_ref[...] - t_ref[...]
        dy_ref[...] = e * (1.0 / D)
        part = jnp.sum(e * e, axis=(0, 1), keepdims=True) * (0.5 / D)

        @pl.when(i == 0)
        def _():
            l_ref[...] = part

        @pl.when(i > 0)
        def _():
            l_ref[...] += part

    row = pl.BlockSpec((tr, D), lambda i: (i, 0))
    one = pl.BlockSpec((1, 1), lambda i: (0, 0))
    return pl.pallas_call(
        body, name=name, grid=(T // tr,), in_specs=[row, row], out_specs=[row, one],
        out_shape=[jax.ShapeDtypeStruct((T, D), F32), jax.ShapeDtypeStruct((1, 1), F32)],
        compiler_params=_cparams("arbitrary"),
    )(y, tgt)


def _swap_half(x):
    n = x.shape[-1]
    lane = lax.broadcasted_iota(jnp.int32, x.shape, 1)
    first = (lane % A_HEAD_DIM) < (A_HEAD_DIM // 2)
    return jnp.where(first, pltpu.roll(x, n - A_HEAD_DIM // 2, axis=1), pltpu.roll(x, A_HEAD_DIM // 2, axis=1))


def _rope(x, cos, sin_signed):
    return x * cos + _swap_half(x) * sin_signed


def _rope_t(dy, cos, sin_signed):
    return dy * cos + _swap_half(dy * sin_signed)


def _rope_tables(T):
    half = A_HEAD_DIM // 2
    inv_freq = np.float32(ROPE_THETA) ** (-np.arange(half, dtype=np.float32) / np.float32(half))
    ang = np.arange(T, dtype=np.float32)[:, None] * inv_freq[None, :]
    cos = np.tile(np.cos(ang), (1, 4))
    sin = np.sin(ang)
    sin_signed = np.tile(np.concatenate([-sin, sin], axis=1), (1, 2))
    return jnp.asarray(cos, F32), jnp.asarray(sin_signed, F32)


def _band_mask(n):
    qi = lax.broadcasted_iota(jnp.int32, (WINDOW, 2 * WINDOW), 0)
    kj = lax.broadcasted_iota(jnp.int32, (WINDOW, 2 * WINDOW), 1)
    return (kj > qi) & (kj <= qi + WINDOW) & ((n > 0) | (kj >= WINDOW))


def _place(v, src_half, dst_half):
    lane = lax.broadcasted_iota(jnp.int32, v.shape, 1)
    if src_half != dst_half:
        v = pltpu.roll(v, A_HEAD_DIM, axis=1)
    keep = (lane >= A_HEAD_DIM) if dst_half else (lane < A_HEAD_DIM)
    return jnp.where(keep, v, 0.0)


def _attn_specs():
    kb, vb = HP_K // LANES, HP_V // LANES
    prev = lambda n: jnp.maximum(n - 1, 0)
    return dict(
        q=pl.BlockSpec((WINDOW, A_Q_W), lambda n: (n, 0)),
        kc=pl.BlockSpec((WINDOW, LANES), lambda n: (n, kb)),
        kp=pl.BlockSpec((WINDOW, LANES), lambda n: (prev(n), kb)),
        vc=pl.BlockSpec((WINDOW, LANES), lambda n: (n, vb)),
        vp=pl.BlockSpec((WINDOW, LANES), lambda n: (prev(n), vb)),
        tq=pl.BlockSpec((WINDOW, LANES), lambda n: (n, 0)),
        tp=pl.BlockSpec((WINDOW, LANES), lambda n: (prev(n), 0)),
        sink=pl.BlockSpec((A_Q_HEADS, LANES), lambda n: (0, 0)),
        row512=pl.BlockSpec((WINDOW, A_Q_W), lambda n: (n, 0)),
        row128=pl.BlockSpec((WINDOW, LANES), lambda n: (n, 0)),
        lse=pl.BlockSpec((WINDOW, A_Q_HEADS), lambda n: (n, 0)),
    )


def _attn_fwd(proj, cos, sin_s, sinks_b, name):
    T = proj.shape[0]
    sp = _attn_specs()

    def body(q_ref, kc_ref, kp_ref, vc_ref, vp_ref, cq_ref, sq_ref, cp_ref, sp_ref, sink_ref, o_ref, l_ref):
        n = pl.program_id(0)
        cq, sq = cq_ref[...], sq_ref[...]
        cq4, sq4 = jnp.tile(cq, (1, A_Q_W // LANES)), jnp.tile(sq, (1, A_Q_W // LANES))
        q = _rope(q_ref[...], cq4, sq4) * (A_HEAD_DIM ** -0.5)
        kc = _rope(kc_ref[...], cq, sq)
        kp = _rope(kp_ref[...], cp_ref[...], sp_ref[...])
        kk = jnp.concatenate([kp, kc], axis=0)
        vv = jnp.concatenate([vp_ref[...], vc_ref[...]], axis=0)
        mask = _band_mask(n)[None]
        lane = lax.broadcasted_iota(jnp.int32, (WINDOW, LANES), 1)
        lane8 = lax.broadcasted_iota(jnp.int32, (WINDOW, A_Q_HEADS), 1)
        qe = jnp.stack([_place(q[:, (hq // 2) * LANES:(hq // 2 + 1) * LANES], hq % 2, hq // 4) for hq in range(A_Q_HEADS)])
        kk8 = jnp.broadcast_to(kk[None], (A_Q_HEADS,) + kk.shape)
        vv8 = jnp.broadcast_to(vv[None], (A_Q_HEADS,) + vv.shape)
        sk = jnp.stack([sink_ref[hq:hq + 1, 0:1] for hq in range(A_Q_HEADS)])
        s = jnp.where(mask, _dot(qe, kk8, BNT), NEG_BIG)
        m = jnp.maximum(jnp.max(s, axis=-1, keepdims=True), sk)
        p = jnp.exp(s - m)
        den = jnp.sum(p, axis=-1, keepdims=True) + jnp.exp(sk - m)
        o = _dot(p * (1.0 / den), vv8, BNN)
        lse_h = m + jnp.log(den)
        outs = []
        lse = jnp.zeros((WINDOW, A_Q_HEADS), F32)
        for pb in range(A_Q_HEADS // 2):
            halves = [_place(o[2 * pb + e], pb // 2, e) for e in range(2)]
            outs.append(jnp.where(lane < A_HEAD_DIM, halves[0], halves[1]))
            for e in range(2):
                lse = jnp.where(lane8 == 2 * pb + e, lse_h[2 * pb + e], lse)
        o_ref[...] = jnp.concatenate(outs, axis=1)
        l_ref[...] = lse

    return pl.pallas_call(
        body, name=name, grid=(T // WINDOW,),
        in_specs=[sp["q"], sp["kc"], sp["kp"], sp["vc"], sp["vp"], sp["tq"], sp["tq"], sp["tp"], sp["tp"], sp["sink"]],
        out_specs=[sp["row512"], sp["lse"]],
        out_shape=[jax.ShapeDtypeStruct((T, A_Q_W), F32), jax.ShapeDtypeStruct((T, A_Q_HEADS), F32)],
        compiler_params=_cparams("parallel"),
    )(proj, proj, proj, proj, proj, cos, sin_s, cos, sin_s, sinks_b)


def _attn_bwd(proj, cos, sin_s, sinks_b, o, lse, dmix, name):
    T = proj.shape[0]
    sp = _attn_specs()

    def body(q_ref, kc_ref, kp_ref, vc_ref, vp_ref, cq_ref, sq_ref, cp_ref, sp_ref, sink_ref, o_ref, l_ref, do_ref,
             dq_ref, dkc_ref, dkp_ref, dvc_ref, dvp_ref, dsink_ref):
        n = pl.program_id(0)
        cq, sq = cq_ref[...], sq_ref[...]
        cp, sps = cp_ref[...], sp_ref[...]
        cq4, sq4 = jnp.tile(cq, (1, A_Q_W // LANES)), jnp.tile(sq, (1, A_Q_W // LANES))
        q = _rope(q_ref[...], cq4, sq4) * (A_HEAD_DIM ** -0.5)
        kc = _rope(kc_ref[...], cq, sq)
        kp = _rope(kp_ref[...], cp, sps)
        kk = jnp.concatenate([kp, kc], axis=0)
        vv = jnp.concatenate([vp_ref[...], vc_ref[...]], axis=0)
        mask = _band_mask(n)[None]
        lane = lax.broadcasted_iota(jnp.int32, (WINDOW, LANES), 1)
        do_all, o_all, l_all = do_ref[...], o_ref[...], l_ref[...]
        lane8 = lax.broadcasted_iota(jnp.int32, (WINDOW, A_Q_HEADS), 1)
        head8 = lax.broadcasted_iota(jnp.int32, (1, A_Q_HEADS), 1)
        prod = do_all * o_all
        qes, does, deltas, lhs = [], [], [], []
        for hq in range(A_Q_HEADS):
            pb, e, kvh = hq // 2, hq % 2, hq // 4
            blk = slice(pb * LANES, (pb + 1) * LANES)
            in_half = (lane >= A_HEAD_DIM) if e else (lane < A_HEAD_DIM)
            deltas.append(jnp.sum(jnp.where(in_half, prod[:, blk], 0.0), axis=-1, keepdims=True))
            qes.append(_place(q[:, blk], e, kvh))
            does.append(_place(do_all[:, blk], e, kvh))
            lhs.append(jnp.sum(jnp.where(lane8 == hq, l_all, 0.0), axis=-1, keepdims=True))
        qe, doe, delta, lh = jnp.stack(qes), jnp.stack(does), jnp.stack(deltas), jnp.stack(lhs)
        kk8 = jnp.broadcast_to(kk[None], (A_Q_HEADS,) + kk.shape)
        vv8 = jnp.broadcast_to(vv[None], (A_Q_HEADS,) + vv.shape)
        sk = jnp.stack([sink_ref[hq:hq + 1, 0:1] for hq in range(A_Q_HEADS)])
        s = _dot(qe, kk8, BNT)
        p = jnp.where(mask, jnp.exp(jnp.where(mask, s, NEG_BIG) - lh), 0.0)
        dvv = jnp.sum(_dot(p, doe, BTN), axis=0)
        ds = p * (_dot(doe, vv8, BNT) - delta)
        dkk = jnp.sum(_dot(ds, qe, BTN), axis=0)
        dqe = _dot(ds, kk8, BNN)
        dsink_h = -jnp.sum(jnp.exp(sk - lh) * delta, axis=(1, 2), keepdims=True)
        dqs = []
        dsk = jnp.zeros((1, A_Q_HEADS), F32)
        for pb in range(A_Q_HEADS // 2):
            halves = [_place(dqe[2 * pb + e], pb // 2, e) for e in range(2)]
            dqs.append(jnp.where(lane < A_HEAD_DIM, halves[0], halves[1]))
            for e in range(2):
                dsk = jnp.where(head8 == 2 * pb + e, dsink_h[2 * pb + e], dsk)
        dq = jnp.concatenate(dqs, axis=1) * (A_HEAD_DIM ** -0.5)
        dq_ref[...] = _rope_t(dq, cq4, sq4).astype(ACT_DTYPE)
        dkp_ref[...] = _rope_t(dkk[:WINDOW], cp, sps)
        dkc_ref[...] = _rope_t(dkk[WINDOW:], cq, sq)
        dvp_ref[...] = dvv[:WINDOW]
        dvc_ref[...] = dvv[WINDOW:]

        @pl.when(n == 0)
        def _():
            dsink_ref[...] = dsk

        @pl.when(n > 0)
        def _():
            dsink_ref[...] += dsk

    return pl.pallas_call(
        body, name=name, grid=(T // WINDOW,),
        in_specs=[sp["q"], sp["kc"], sp["kp"], sp["vc"], sp["vp"], sp["tq"], sp["tq"], sp["tp"], sp["tp"], sp["sink"],
                  sp["row512"], sp["lse"], sp["row512"]],
        out_specs=[sp["row512"], sp["row128"], sp["row128"], sp["row128"], sp["row128"],
                   pl.BlockSpec((1, A_Q_HEADS), lambda n: (0, 0))],
        out_shape=[jax.ShapeDtypeStruct((T, HYB_PAD), ACT_DTYPE)] + [jax.ShapeDtypeStruct((T, LANES), F32)] * 4
        + [jax.ShapeDtypeStruct((1, A_Q_HEADS), F32)],
        compiler_params=_cparams("arbitrary"),
    )(proj, proj, proj, proj, proj, cos, sin_s, cos, sin_s, sinks_b, o, lse, dmix)


def _shift_down(x, prev8, k):
    if k == 0:
        return x
    row = lax.broadcasted_iota(jnp.int32, prev8.shape, 0)
    r = pltpu.roll(x, k, axis=0)
    top = jnp.where(row < k, pltpu.roll(prev8, k, axis=0), r[:SUBLANES])
    return jnp.concatenate([top, r[SUBLANES:]], axis=0)


def _shift_up(x, next8, k):
    if k == 0:
        return x
    R = x.shape[0]
    row = lax.broadcasted_iota(jnp.int32, next8.shape, 0)
    r = pltpu.roll(x, R - k, axis=0)
    bot = jnp.where(row >= SUBLANES - k, pltpu.roll(next8, SUBLANES - k, axis=0), r[R - SUBLANES:])
    return jnp.concatenate([r[:R - SUBLANES], bot], axis=0)


def _conv(x, prev8, w):
    y = x * w[CONV_K - 1:CONV_K]
    for j in range(CONV_K - 1):
        y = y + _shift_down(x, prev8, CONV_K - 1 - j) * w[j:j + 1]
    return y


def _conv_bwd(x, prev8, w, dy, next8_dy):
    dx = dy * w[CONV_K - 1:CONV_K]
    dws = []
    for j in range(CONV_K - 1):
        k = CONV_K - 1 - j
        dx = dx + _shift_up(dy, next8_dy, k) * w[j:j + 1]
        dws.append(jnp.sum(dy * _shift_down(x, prev8, k), axis=0, keepdims=True))
    dws.append(jnp.sum(dy * x, axis=0, keepdims=True))
    return dx, dws


def _dnconv_fwd(proj, conv_w, name):
    T = proj.shape[0]
    R = _tile(T, 512)
    cb0 = HP_QKVB // A_Q_W

    def body(x_ref, w_ref, o_ref, prev_ref):
        i = pl.program_id(1)

        @pl.when(i == 0)
        def _():
            prev_ref[...] = jnp.zeros_like(prev_ref)

        x = x_ref[...]
        o_ref[...] = _silu(_conv(x, prev_ref[...], w_ref[...]))
        prev_ref[...] = x[R - SUBLANES:]

    return pl.pallas_call(
        body, name=name, grid=(3, T // R),
        in_specs=[pl.BlockSpec((R, B_W), lambda j, i: (i, cb0 + j)), pl.BlockSpec((CONV_K, B_W), lambda j, i: (0, j))],
        out_specs=pl.BlockSpec((R, B_W), lambda j, i: (i, j)),
        out_shape=jax.ShapeDtypeStruct((T, 3 * B_W), F32),
        scratch_shapes=[pltpu.VMEM((SUBLANES, B_W), F32)],
        compiler_params=_cparams("parallel", "arbitrary"),
    )(proj, conv_w)


def _dnconv_bwd(proj, conv_w, dy, dproj, name):
    T = proj.shape[0]
    R = _tile(T, 512)
    nb = T // R
    cb0 = HP_QKVB // A_Q_W
    r8 = R // SUBLANES

    def body(x_ref, xp_ref, w_ref, dy_ref, dproj_in, dx_ref, dw_ref, next_ref):
        i = pl.program_id(1)
        blk = nb - 1 - i

        @pl.when(i == 0)
        def _():
            next_ref[...] = jnp.zeros_like(next_ref)

        x = x_ref[...]
        prev8 = jnp.where(blk > 0, xp_ref[...], 0.0)
        w = w_ref[...]
        dpre = dy_ref[...] * _dsilu(_conv(x, prev8, w))
        dx, dw = _conv_bwd(x, prev8, w, dpre, next_ref[...])
        dx_ref[...] = dx.astype(ACT_DTYPE)
        next_ref[...] = dpre[:SUBLANES]

        @pl.when(i == 0)
        def _():
            for j in range(CONV_K):
                dw_ref[j:j + 1, :] = dw[j]

        @pl.when(i > 0)
        def _():
            for j in range(CONV_K):
                dw_ref[j:j + 1, :] += dw[j]

    return pl.pallas_call(
        body, name=name, grid=(3, nb),
        in_specs=[pl.BlockSpec((R, B_W), lambda j, i: (nb - 1 - i, cb0 + j)),
                  pl.BlockSpec((SUBLANES, B_W), lambda j, i: (jnp.maximum((nb - 1 - i) * r8 - 1, 0), cb0 + j)),
                  pl.BlockSpec((CONV_K, B_W), lambda j, i: (0, j)),
                  pl.BlockSpec((R, B_W), lambda j, i: (nb - 1 - i, j)), ANY],
        out_specs=[pl.BlockSpec((R, B_W), lambda j, i: (nb - 1 - i, cb0 + j)),
                   pl.BlockSpec((CONV_K, B_W), lambda j, i: (0, j))],
        out_shape=[jax.ShapeDtypeStruct(dproj.shape, dproj.dtype), jax.ShapeDtypeStruct((CONV_K, 3 * B_W), F32)],
        input_output_aliases={4: 0},
        scratch_shapes=[pltpu.VMEM((SUBLANES, B_W), F32)],
        compiler_params=_cparams("parallel", "arbitrary"),
    )(proj, proj, conv_w, dy, dproj)


DK_SCALE = B_HEAD_DIM ** -0.5


BNN = (((2,), (1,)), ((0,), (0,)))
BNT = (((2,), (2,)), ((0,), (0,)))
BTN = (((1,), (1,)), ((0,), (0,)))


def _tri_inv(a):
    C = a.shape[-1]
    ri = lax.broadcasted_iota(jnp.int32, (C, C), 0)
    ci = lax.broadcasted_iota(jnp.int32, (C, C), 1)
    x = jnp.where(ri == ci, 1.0, 0.0)[None] - a
    p = _dotf(a, a, BNN)
    span = 2
    while span < C:
        dot = _dotf if span <= 4 else _dot
        x = x + dot(x, p, BNN)
        span *= 2
        if span < C:
            p = dot(p, p, BNN)
    return x


def _dn_chunk(qc, kc, v, gcol, grow, bcol, s0, tm=None):
    C = B_CHUNK
    ri = lax.broadcasted_iota(jnp.int32, (C, C), 0)
    ci = lax.broadcasted_iota(jnp.int32, (C, C), 1)
    incl, strict = (ri >= ci)[None], (ri > ci)[None]
    rq = lax.rsqrt(jnp.sum(qc * qc, axis=-1, keepdims=True) + NORM_EPS)
    rk = lax.rsqrt(jnp.sum(kc * kc, axis=-1, keepdims=True) + NORM_EPS)
    qn = qc * rq
    q = qn * DK_SCALE
    k = kc * rk
    gc_col = jnp.sum(jnp.where(incl, grow, 0.0), axis=2, keepdims=True)
    gc_row = jnp.sum(jnp.where((ri <= ci)[None], gcol, 0.0), axis=1, keepdims=True)
    gl = jnp.sum(gcol, axis=1, keepdims=True)
    dincl = jnp.where(incl, jnp.exp(jnp.where(incl, gc_col - gc_row, 0.0)), 0.0)
    dstrict = jnp.where(strict, dincl, 0.0)
    eg = jnp.exp(gc_col)
    ekt = jnp.exp(gl - gc_col)
    egl = jnp.exp(gl)
    kb = k * bcol
    vb = v * bcol
    kbg = kb * eg
    a = _dot(kb, k, BNT) * dstrict
    if tm is None:
        tm = _tri_inv(a)
    u = _dot(tm, vb, BNN)
    w = _dot(tm, kbg, BNN)
    vn = u - _dot(w, s0, BNN)
    qk = _dot(q, k, BNT) * dincl
    qg = q * eg
    kt = k * ekt
    o = _dot(qg, s0, BNN) + _dot(qk, vn, BNN)
    s1 = s0 * egl + _dot(kt, vn, BTN)
    return dict(rq=rq, rk=rk, qn=qn, q=q, k=k, dincl=dincl, dstrict=dstrict, eg=eg, ekt=ekt, egl=egl, kb=kb, vb=vb,
                kbg=kbg, a=a, tm=tm, w=w, vn=vn, qk=qk, qg=qg, kt=kt, o=o, s1=s1, ri=ri[None], ci=ci[None])


def _heads(ref):
    return jnp.stack([ref[:, h * B_HEAD_DIM:(h + 1) * B_HEAD_DIM] for h in range(B_HEADS)])


def _store_heads(ref, val):
    for h in range(B_HEADS):
        ref[:, h * B_HEAD_DIM:(h + 1) * B_HEAD_DIM] = val[h]


def _dn_specs(N, rev):
    ix = (lambda n: N - 1 - n) if rev else (lambda n: n)
    wide = lambda cb: pl.BlockSpec((B_CHUNK, B_W), lambda n: (ix(n), cb))
    return dict(
        q=wide(0), k=wide(1), v=wide(2), z=wide(HP_Z // B_W), dob=wide(A_Q_W // B_W), out=wide(0),
        nw=pl.BlockSpec((1, LANES), lambda n: (0, 0)),
        row=pl.BlockSpec((B_HEADS, None, 1, B_CHUNK), lambda n: (0, ix(n), 0, 0)),
        state=pl.BlockSpec((B_HEADS, None, B_HEAD_DIM, B_HEAD_DIM), lambda n: (0, ix(n), 0, 0)),
        inv=pl.BlockSpec((B_HEADS, None, B_CHUNK, B_CHUNK), lambda n: (0, ix(n), 0, 0)),
    )


def _to_col(row):
    C = row.shape[-1]
    eye = lax.broadcasted_iota(jnp.int32, (C, C), 0) == lax.broadcasted_iota(jnp.int32, (C, C), 1)
    return jnp.sum(jnp.where(eye[None], row, 0.0), axis=2, keepdims=True)


def _to_row(col):
    C = col.shape[1]
    eye = lax.broadcasted_iota(jnp.int32, (C, C), 0) == lax.broadcasted_iota(jnp.int32, (C, C), 1)
    return jnp.sum(jnp.where(eye[None], col, 0.0), axis=1, keepdims=True)


def _dn_fwd(qkvc, proj, norm_w, grow, brow, name):
    T = qkvc.shape[0]
    N = T // B_CHUNK
    sp = _dn_specs(N, False)

    def body(q_ref, k_ref, v_ref, z_ref, nw_ref, gr_ref, br_ref, o_ref, st_ref, tm_ref, s_ref):
        n = pl.program_id(0)

        @pl.when(n == 0)
        def _():
            s_ref[...] = jnp.zeros_like(s_ref)

        s0 = s_ref[...]
        st_ref[...] = s0
        grow_v = gr_ref[...]
        f = _dn_chunk(_heads(q_ref), _heads(k_ref), _heads(v_ref), _to_col(grow_v), grow_v, _to_col(br_ref[...]), s0)
        o = f["o"]
        r = lax.rsqrt(jnp.mean(o * o, axis=-1, keepdims=True) + NORM_EPS)
        _store_heads(o_ref, o * r * nw_ref[...][None] * _silu(_heads(z_ref)))
        s_ref[...] = f["s1"]
        tm_ref[...] = f["tm"]

    return pl.pallas_call(
        body, name=name, grid=(N,),
        in_specs=[sp["q"], sp["k"], sp["v"], sp["z"], sp["nw"], sp["row"], sp["row"]],
        out_specs=[sp["out"], sp["state"], sp["inv"]],
        out_shape=[jax.ShapeDtypeStruct((T, B_W), F32),
                   jax.ShapeDtypeStruct((B_HEADS, N, B_HEAD_DIM, B_HEAD_DIM), F32),
                   jax.ShapeDtypeStruct((B_HEADS, N, B_CHUNK, B_CHUNK), F32)],
        scratch_shapes=[pltpu.VMEM((B_HEADS, B_HEAD_DIM, B_HEAD_DIM), F32)],
        compiler_params=_cparams("arbitrary"),
    )(qkvc, qkvc, qkvc, proj, norm_w, grow, brow)


def _dn_bwd(qkvc, proj, norm_w, grow, brow, states, invs, dmix, dproj, name):
    T = qkvc.shape[0]
    N = T // B_CHUNK
    sp = _dn_specs(N, True)
    C = B_CHUNK

    def body(q_ref, k_ref, v_ref, z_ref, nw_ref, gr_ref, br_ref, st_ref, tm_ref, dob_ref, dproj_in,
             dqkv_ref, dz_ref, dg_ref, db_ref, dnw_ref, ds_ref):
        n = pl.program_id(0)
        dq_ref = dqkv_ref.at[:, 0:B_W]
        dk_ref = dqkv_ref.at[:, B_W:2 * B_W]
        dv_ref = dqkv_ref.at[:, 2 * B_W:3 * B_W]

        @pl.when(n == 0)
        def _():
            ds_ref[...] = jnp.zeros_like(ds_ref)
            dnw_ref[...] = jnp.zeros_like(dnw_ref)

        s0 = st_ref[...]
        ds1 = ds_ref[...]
        v, z, nw, bcol_v = _heads(v_ref), _heads(z_ref), nw_ref[...][None], _to_col(br_ref[...])
        grow_v = gr_ref[...]
        f = _dn_chunk(_heads(q_ref), _heads(k_ref), v, _to_col(grow_v), grow_v, bcol_v, s0, tm=tm_ref[...])
        o, q, k, qn = f["o"], f["q"], f["k"], f["qn"]
        eg, ekt, egl = f["eg"], f["ekt"], f["egl"]
        tm, w, vn, kb, vb, kbg = f["tm"], f["w"], f["vn"], f["kb"], f["vb"], f["kbg"]
        qg, kt, qk, a = f["qg"], f["kt"], f["qk"], f["a"]
        ri, ci = f["ri"], f["ci"]

        dob_v = _heads(dob_ref)
        r = lax.rsqrt(jnp.mean(o * o, axis=-1, keepdims=True) + NORM_EPS)
        sz = _silu(z)
        on = o * r
        dnw_ref[...] += jnp.sum(dob_v * sz * on, axis=1, keepdims=True)
        _store_heads(dz_ref, (dob_v * on * nw * _dsilu(z)).astype(ACT_DTYPE))
        d_on = dob_v * sz * nw
        do = r * (d_on - on * jnp.mean(d_on * on, axis=-1, keepdims=True))

        dvn = _dot(qk, do, BTN) + _dot(kt, ds1, BNN)
        dqk = _dot(do, vn, BNT)
        dqg = _dot(do, s0, BNT)
        ds_ref[...] = _dot(qg, do, BTN) + egl * ds1 - _dot(w, dvn, BTN)
        dgl = jnp.sum(s0 * ds1, axis=(1, 2), keepdims=True) * egl
        dkt = _dot(vn, ds1, BNT)
        dw = -_dot(dvn, s0, BNT)
        dq = dqg * eg
        dgc = jnp.sum(dqg * qg, axis=-1, keepdims=True)
        dk = dkt * ekt
        t_kt = jnp.sum(dkt * kt, axis=-1, keepdims=True)
        dgl = dgl + jnp.sum(t_kt, axis=1, keepdims=True)
        dgc = dgc - t_kt
        dqkr = dqk * f["dincl"]
        dq = dq + _dot(dqkr, k, BNN)
        dk = dk + _dot(dqkr, q, BTN)
        e_qk = dqk * qk
        dgc = dgc + jnp.sum(e_qk, axis=-1, keepdims=True)
        dgc_row = -jnp.sum(e_qk, axis=1, keepdims=True)
        dtm = _dot(dvn, vb, BNT) + _dot(dw, kbg, BNT)
        dvb = _dot(tm, dvn, BTN)
        dkbg = _dot(tm, dw, BTN)
        dkb = dkbg * eg
        dgc = dgc + jnp.sum(dkbg * kbg, axis=-1, keepdims=True)
        da = -_dotf(tm, _dotf(dtm, tm, BNT), BTN)
        dkk = da * f["dstrict"]
        e_a = da * a
        dgc = dgc + jnp.sum(e_a, axis=-1, keepdims=True)
        dgc_row = dgc_row - jnp.sum(e_a, axis=1, keepdims=True)
        dkb = dkb + _dot(dkk, k, BNN)
        dk = dk + _dot(dkk, kb, BTN)
        dk = dk + dkb * bcol_v
        db_ref[...] = _to_row(jnp.sum(dkb * k, axis=-1, keepdims=True) + jnp.sum(dvb * v, axis=-1, keepdims=True))
        _store_heads(dv_ref, dvb * bcol_v)
        dgc_row = dgc_row + jnp.sum(jnp.where(ri == ci, dgc, 0.0), axis=1, keepdims=True)
        dg_ref[...] = jnp.sum(jnp.where(ci <= ri, _to_col(dgc_row), 0.0), axis=1, keepdims=True) + dgl
        dqs = dq * DK_SCALE
        _store_heads(dq_ref, f["rq"] * (dqs - qn * jnp.sum(dqs * qn, axis=-1, keepdims=True)))
        _store_heads(dk_ref, f["rk"] * (dk - k * jnp.sum(dk * k, axis=-1, keepdims=True)))

    return pl.pallas_call(
        body, name=name, grid=(N,),
        in_specs=[sp["q"], sp["k"], sp["v"], sp["z"], sp["nw"], sp["row"], sp["row"], sp["state"], sp["inv"], sp["dob"],
                  ANY],
        out_specs=[pl.BlockSpec((C, 3 * B_W), lambda n: (N - 1 - n, 0)), sp["z"], sp["row"], sp["row"],
                   pl.BlockSpec((B_HEADS, 1, LANES), lambda n: (0, 0, 0))],
        out_shape=[jax.ShapeDtypeStruct((T, 3 * B_W), F32), jax.ShapeDtypeStruct(dproj.shape, dproj.dtype),
                   jax.ShapeDtypeStruct((B_HEADS, N, 1, C), F32), jax.ShapeDtypeStruct((B_HEADS, N, 1, C), F32),
                   jax.ShapeDtypeStruct((B_HEADS, 1, LANES), F32)],
        input_output_aliases={10: 1},
        scratch_shapes=[pltpu.VMEM((B_HEADS, B_HEAD_DIM, B_HEAD_DIM), F32)],
        compiler_params=_cparams("arbitrary"),
    )(qkvc, qkvc, qkvc, proj, norm_w, grow, brow, states, invs, dmix, dproj)


def _lru_gates(xc, wa_ref, wx_ref, ba, bx, sp):
    pre_r, pre_i = [], []
    for hb in range(LRU_BLOCKS):
        xb = xc[:, hb * LRU_BLOCK_W:(hb + 1) * LRU_BLOCK_W]
        pre_r.append(_dot(xb, wa_ref[hb]))
        pre_i.append(_dot(xb, wx_ref[hb]))
    r = _sigmoid(jnp.concatenate(pre_r, axis=1) + ba)
    i = _sigmoid(jnp.concatenate(pre_i, axis=1) + bx)
    la = -LRU_C * r * sp
    a = jnp.exp(la)
    th = jnp.tanh(la)
    s = jnp.sqrt(-2.0 * th / (1.0 - th))
    return r, i, a, s


def _scan_down(a, b, h_in):
    R = a.shape[0]
    sub = lax.broadcasted_iota(jnp.int32, a.shape, 0) % SUBLANES
    d = 1
    while d < SUBLANES:
        ok = sub >= d
        b = a * jnp.where(ok, pltpu.roll(b, d, axis=0), 0.0) + b
        a = a * jnp.where(ok, pltpu.roll(a, d, axis=0), 1.0)
        d *= 2
    out, last = [], h_in
    for r0 in range(0, R, SUBLANES):
        h = b[r0:r0 + SUBLANES] + a[r0:r0 + SUBLANES] * last
        out.append(h)
        last = h[SUBLANES - 1:SUBLANES]
    return jnp.concatenate(out, axis=0)


def _scan_up(a, b, l_in):
    R = a.shape[0]
    sub = lax.broadcasted_iota(jnp.int32, a.shape, 0) % SUBLANES
    d = 1
    while d < SUBLANES:
        ok = sub < SUBLANES - d
        b = a * jnp.where(ok, pltpu.roll(b, R - d, axis=0), 0.0) + b
        a = a * jnp.where(ok, pltpu.roll(a, R - d, axis=0), 1.0)
        d *= 2
    out, nxt = [], l_in
    for r0 in range(R - SUBLANES, -1, -SUBLANES):
        lam = b[r0:r0 + SUBLANES] + a[r0:r0 + SUBLANES] * nxt
        out.append(lam)
        nxt = lam[0:1]
    return jnp.concatenate(out[::-1], axis=0)


def _rglru_fwd(proj, conv_w, conv_b, wa, wx, ba, bx, sp, name):
    T = proj.shape[0]
    R = _tile(T, 256)
    W = D_MODEL

    def body(p_ref, cw_ref, cb_ref, wa_ref, wx_ref, ba_ref, bx_ref, sp_ref, hg_ref, h_ref, prev_ref, hc_ref):
        i = pl.program_id(0)

        @pl.when(i == 0)
        def _():
            prev_ref[...] = jnp.zeros_like(prev_ref)
            hc_ref[...] = jnp.zeros_like(hc_ref)

        xr = p_ref[:, :W]
        gate = p_ref[:, W:]
        xc = _conv(xr, prev_ref[...], cw_ref[...]) + cb_ref[...]
        prev_ref[...] = xr[R - SUBLANES:]
        r, ig, a, s = _lru_gates(xc, wa_ref, wx_ref, ba_ref[...], bx_ref[...], sp_ref[...])
        h = _scan_down(a, s * ig * xc, hc_ref[SUBLANES - 1:SUBLANES, :])
        h_ref[...] = h
        hg_ref[...] = (h * _gelu(gate)).astype(ACT_DTYPE)
        hc_ref[...] = h[R - SUBLANES:]

    vec = pl.BlockSpec((1, W), lambda i: (0, 0))
    wsp = pl.BlockSpec((LRU_BLOCKS, LRU_BLOCK_W, LRU_BLOCK_W), lambda i: (0, 0, 0))
    row = pl.BlockSpec((R, W), lambda i: (i, 0))
    return pl.pallas_call(
        body, name=name, grid=(T // R,),
        in_specs=[pl.BlockSpec((R, 2 * W), lambda i: (i, 0)), pl.BlockSpec((CONV_K, W), lambda i: (0, 0)),
                  vec, wsp, wsp, vec, vec, vec],
        out_specs=[row, row],
        out_shape=[jax.ShapeDtypeStruct((T, W), ACT_DTYPE), jax.ShapeDtypeStruct((T, W), F32)],
        scratch_shapes=[pltpu.VMEM((SUBLANES, W), F32), pltpu.VMEM((SUBLANES, W), F32)],
        compiler_params=_cparams("arbitrary"),
    )(proj, conv_w, conv_b, wa, wx, ba, bx, sp)


def _rglru_bwd(proj, conv_w, conv_b, wa, wx, ba, bx, sp, h, dhg, name):
    T = proj.shape[0]
    R = _tile(T, 256)
    nb = T // R
    r8 = R // SUBLANES
    W = D_MODEL

    def body(p_ref, pp_ref, cw_ref, cb_ref, wa_ref, wx_ref, ba_ref, bx_ref, sp_ref, h_ref, hp_ref, dhg_ref,
             dp_ref, dcw_ref, dcb_ref, dwa_ref, dwx_ref, dba_ref, dbx_ref, dsp_ref, lam_ref, nxt_ref):
        step = pl.program_id(0)
        blk = nb - 1 - step

        @pl.when(step == 0)
        def _():
            lam_ref[...] = jnp.zeros_like(lam_ref)
            nxt_ref[...] = jnp.zeros_like(nxt_ref)

        xr = p_ref[:, :W]
        gate = p_ref[:, W:]
        first = blk > 0
        prev8 = jnp.where(first, pp_ref[:, :W], 0.0)
        hprev8 = jnp.where(first, hp_ref[...], 0.0)
        cw = cw_ref[...]
        spv = sp_ref[...]
        xc = _conv(xr, prev8, cw) + cb_ref[...]
        r, ig, a, s = _lru_gates(xc, wa_ref, wx_ref, ba_ref[...], bx_ref[...], spv)
        hv = h_ref[...]
        dhg_v = dhg_ref[...]
        dgate = dhg_v * hv * _dgelu(gate)
        dh = dhg_v * _gelu(gate)
        row = lax.broadcasted_iota(jnp.int32, (R, W), 0)
        last = row == R - 1
        a_up = jnp.where(last, 0.0, pltpu.roll(a, R - 1, axis=0))
        lam = _scan_up(a_up, dh + jnp.where(last, lam_ref[0:1, :], 0.0), jnp.zeros((1, W), F32))
        lam_ref[...] = (a * lam)[:SUBLANES]
        h_dn = _shift_down(hv, hprev8, 1)
        da = lam * h_dn
        bx_in = ig * xc
        dsv = lam * bx_in
        dig = lam * s * xc
        dxc = lam * s * ig
        dla = da * a - dsv * (a * a) / s
        dr = dla * (-LRU_C) * spv
        dsp = jnp.sum(dla * (-LRU_C) * r, axis=0, keepdims=True)
        dpr = dr * r * (1.0 - r)
        dpi = dig * ig * (1.0 - ig)
        dxc_parts, dwa_parts, dwx_parts = [], [], []
        for hb in range(LRU_BLOCKS):
            sl = slice(hb * LRU_BLOCK_W, (hb + 1) * LRU_BLOCK_W)
            xb, gr, gi = xc[:, sl], dpr[:, sl], dpi[:, sl]
            dxc_parts.append(_dot(gr, wa_ref[hb], NT) + _dot(gi, wx_ref[hb], NT))
            dwa_parts.append(_dot(xb, gr, TN))
            dwx_parts.append(_dot(xb, gi, TN))
        dxc = dxc + jnp.concatenate(dxc_parts, axis=1)
        dxr, dcw = _conv_bwd(xr, prev8, cw, dxc, nxt_ref[...])
        nxt_ref[...] = dxc[:SUBLANES]
        dp_ref[:, :W] = dxr.astype(ACT_DTYPE)
        dp_ref[:, W:] = dgate.astype(ACT_DTYPE)
        dcb = jnp.sum(dxc, axis=0, keepdims=True)
        dba = jnp.sum(dpr, axis=0, keepdims=True)
        dbx = jnp.sum(dpi, axis=0, keepdims=True)

        @pl.when(step == 0)
        def _():
            for j in range(CONV_K):
                dcw_ref[j:j + 1, :] = dcw[j]
            dcb_ref[...] = dcb
            dba_ref[...] = dba
            dbx_ref[...] = dbx
            dsp_ref[...] = dsp
            for hb in range(LRU_BLOCKS):
                dwa_ref[hb] = dwa_parts[hb]
                dwx_ref[hb] = dwx_parts[hb]

        @pl.when(step > 0)
        def _():
            for j in range(CONV_K):
                dcw_ref[j:j + 1, :] += dcw[j]
            dcb_ref[...] += dcb
            dba_ref[...] += dba
            dbx_ref[...] += dbx
            dsp_ref[...] += dsp
            for hb in range(LRU_BLOCKS):
                dwa_ref[hb] += dwa_parts[hb]
                dwx_ref[hb] += dwx_parts[hb]

    rv = lambda i: nb - 1 - i
    before = lambda i: jnp.maximum((nb - 1 - i) * r8 - 1, 0)
    vec = pl.BlockSpec((1, W), lambda i: (0, 0))
    cws = pl.BlockSpec((CONV_K, W), lambda i: (0, 0))
    wsp = pl.BlockSpec((LRU_BLOCKS, LRU_BLOCK_W, LRU_BLOCK_W), lambda i: (0, 0, 0))
    row = pl.BlockSpec((R, W), lambda i: (rv(i), 0))
    wshape = jax.ShapeDtypeStruct((LRU_BLOCKS, LRU_BLOCK_W, LRU_BLOCK_W), F32)
    vshape = jax.ShapeDtypeStruct((1, W), F32)
    return pl.pallas_call(
        body, name=name, grid=(nb,),
        in_specs=[pl.BlockSpec((R, 2 * W), lambda i: (rv(i), 0)), pl.BlockSpec((SUBLANES, 2 * W), lambda i: (before(i), 0)),
                  cws, vec, wsp, wsp, vec, vec, vec, row, pl.BlockSpec((SUBLANES, W), lambda i: (before(i), 0)), row],
        out_specs=[pl.BlockSpec((R, 2 * W), lambda i: (rv(i), 0)), cws, vec, wsp, wsp, vec, vec, vec],
        out_shape=[jax.ShapeDtypeStruct((T, 2 * W), ACT_DTYPE), jax.ShapeDtypeStruct((CONV_K, W), F32), vshape,
                   wshape, wshape, vshape, vshape, vshape],
        scratch_shapes=[pltpu.VMEM((SUBLANES, W), F32), pltpu.VMEM((SUBLANES, W), F32)],
        compiler_params=_cparams("arbitrary"),
    )(proj, proj, conv_w, conv_b, wa, wx, ba, bx, sp, h, h, dhg)


MESH = pl.DeviceIdType.MESH
ANY = pl.BlockSpec(memory_space=pl.ANY)


def _position():
    x, y, c = lax.axis_index("x"), lax.axis_index("y"), lax.axis_index("c")
    other_chips = [(1 - x, y), (x, 1 - y), (1 - x, 1 - y)]
    return x, y, c, other_chips


def _all_gather_weights(shards, name):
    n = len(shards)

    def body(*refs):
        ins, outs = refs[:n], refs[n:2 * n]
        send_sems, recv_sems = refs[2 * n:]
        x, y, c, chips = _position()
        me = 2 * x + y
        sibling = (x, y, 1 - c)

        def rcopy(t, k, src, dst, to):
            return pltpu.make_async_remote_copy(src_ref=src, dst_ref=dst, send_sem=send_sems.at[t, k],
                                                recv_sem=recv_sems.at[t, k], device_id=to, device_id_type=MESH)

        started = []
        for t in range(n):
            for j, (cx, cy) in enumerate(chips):
                cp = rcopy(t, j, ins[t].at[c], outs[t].at[me, c], (cx, cy, c))
                cp.start()
                started.append(cp)
        for t in range(n):
            for j, (cx, cy) in enumerate(chips):
                blk = outs[t].at[2 * cx + cy, c]
                rcopy(t, j, blk, blk, (cx, cy, c)).wait_recv()
                cp = rcopy(t, 3 + j, blk, blk, sibling)
                cp.start()
                started.append(cp)
        for t in range(n):
            for j, (cx, cy) in enumerate(chips):
                blk = outs[t].at[2 * cx + cy, 1 - c]
                rcopy(t, 3 + j, blk, blk, sibling).wait_recv()
        for cp in started:
            cp.wait_send()

    return pl.pallas_call(
        body, name=name, in_specs=[ANY] * n, out_specs=[ANY] * n,
        out_shape=[jax.ShapeDtypeStruct((N_CHIPS,) + s.shape, s.dtype) for s in shards],
        scratch_shapes=[pltpu.SemaphoreType.DMA((n, 6)), pltpu.SemaphoreType.DMA((n, 6))],
    )(*shards)


HBM = pl.BlockSpec(memory_space=pltpu.HBM)
SEM = pl.BlockSpec(memory_space=pltpu.SEMAPHORE)
EFFECT = pltpu.SideEffectType.DATAFLOW_SIDE_EFFECTING


def _gather_start(own, land, after, name):
    n = len(own)

    def body(*refs):
        own_refs, land_refs = refs[:n], refs[n:2 * n]
        send_sems = refs[2 * n + 1:3 * n + 1]
        recv_sems = refs[3 * n + 1:4 * n + 1]
        x, y, c, chips = _position()
        me = 2 * x + y
        for t in range(n):
            for cx, cy in chips:
                pltpu.make_async_remote_copy(
                    src_ref=own_refs[t], dst_ref=land_refs[t].at[me], send_sem=send_sems[t], recv_sem=recv_sems[t],
                    device_id=(cx, cy, c), device_id_type=MESH).start()

    sems = (pltpu.SemaphoreType.DMA(()),) * (2 * n)
    thru = [pltpu.HBM(a.shape, a.dtype) for a in list(own) + list(land)]
    out = pl.pallas_call(
        body, name=name, out_shape=(*sems, *thru),
        in_specs=[HBM] * (2 * n) + [pl.BlockSpec(memory_space=pl.ANY)], out_specs=(SEM,) * (2 * n) + (HBM,) * (2 * n),
        input_output_aliases={i: 2 * n + i for i in range(2 * n)},
        compiler_params=pltpu.CompilerParams(has_side_effects=EFFECT),
    )(*[pltpu.with_memory_space_constraint(a, pltpu.HBM) for a in list(own) + list(land)], after)
    return list(out[:n]), list(out[n:2 * n]), list(out[2 * n:3 * n]), list(out[3 * n:])


def _gather_wait(send_sems, recv_sems, own, land, after, name):
    n = len(own)

    def body(*refs):
        land_refs = refs[n:2 * n]
        s_sems, r_sems = refs[2 * n:3 * n], refs[3 * n:4 * n]
        x, y, c, _ = _position()
        for t in range(n):
            three = land_refs[t].at[pl.ds(0, N_CHIPS - 1)]
            cp = pltpu.make_async_remote_copy(src_ref=three, dst_ref=three, send_sem=s_sems[t], recv_sem=r_sems[t],
                                              device_id=(x, y, c), device_id_type=MESH)
            cp.wait_send()
            cp.wait_recv()

    thru = [pltpu.HBM(a.shape, a.dtype) for a in list(own) + list(land)]
    out = pl.pallas_call(
        body, name=name, out_shape=tuple(thru),
        in_specs=[HBM] * (2 * n) + [SEM] * (2 * n) + [pl.BlockSpec(memory_space=pl.ANY)], out_specs=(HBM,) * (2 * n),
        input_output_aliases={i: i for i in range(2 * n)},
        compiler_params=pltpu.CompilerParams(has_side_effects=EFFECT),
    )(*own, *land, *send_sems, *recv_sems, after)
    return list(out[n:])


def _rs_to_sibling(grads, name):
    n = len(grads)

    def body(*refs):
        ins, outs = refs[:n], refs[n:2 * n]
        send_sems, recv_sems = refs[2 * n:]
        x, y, c, _ = _position()
        cps = [pltpu.make_async_remote_copy(src_ref=ins[t].at[:, 1 - c], dst_ref=outs[t], send_sem=send_sems.at[t],
                                            recv_sem=recv_sems.at[t], device_id=(x, y, 1 - c), device_id_type=MESH)
               for t in range(n)]
        for cp in cps:
            cp.start()
        for cp in cps:
            cp.wait()

    return pl.pallas_call(
        body, name=name, in_specs=[ANY] * n, out_specs=[ANY] * n,
        out_shape=[jax.ShapeDtypeStruct((N_CHIPS,) + g.shape[2:], g.dtype) for g in grads],
        scratch_shapes=[pltpu.SemaphoreType.DMA((n,)), pltpu.SemaphoreType.DMA((n,))],
    )(*grads)


def _rs_sibling_start(grads, recv, name):
    n = len(grads)

    def body(*refs):
        grad_refs, recv_refs = refs[:n], refs[n:2 * n]
        send_sems, recv_sems = refs[2 * n:3 * n], refs[3 * n:4 * n]
        token_ref = refs[-1]
        x, y, c, _ = _position()
        for t in range(n):
            pltpu.make_async_remote_copy(src_ref=grad_refs[t].at[:, 1 - c], dst_ref=recv_refs[t], send_sem=send_sems[t],
                                         recv_sem=recv_sems[t], device_id=(x, y, 1 - c), device_id_type=MESH).start()
        token_ref[...] = jnp.zeros_like(token_ref)

    sems = (pltpu.SemaphoreType.DMA(()),) * (2 * n)
    thru = [pltpu.HBM(a.shape, a.dtype) for a in list(grads) + list(recv)]
    out = pl.pallas_call(
        body, name=name, out_shape=(*sems, *thru, jax.ShapeDtypeStruct((SUBLANES, LANES), F32)),
        in_specs=[HBM] * (2 * n), out_specs=(SEM,) * (2 * n) + (HBM,) * (2 * n) + (pl.BlockSpec(memory_space=pltpu.VMEM),),
        input_output_aliases={i: 2 * n + i for i in range(2 * n)},
        compiler_params=pltpu.CompilerParams(has_side_effects=EFFECT),
    )(*[pltpu.with_memory_space_constraint(a, pltpu.HBM) for a in list(grads) + list(recv)])
    return out[:n], out[n:2 * n], list(out[2 * n:3 * n]), list(out[3 * n:4 * n]), out[-1]


def _rs_sibling_wait(send_sems, recv_sems, grads, recv, after, name):
    n = len(grads)

    def body(*refs):
        recv_refs = refs[n:2 * n]
        s_sems, r_sems = refs[2 * n:3 * n], refs[3 * n:4 * n]
        x, y, c, _ = _position()
        for t in range(n):
            cp = pltpu.make_async_remote_copy(src_ref=recv_refs[t], dst_ref=recv_refs[t], send_sem=s_sems[t],
                                              recv_sem=r_sems[t], device_id=(x, y, c), device_id_type=MESH)
            cp.wait_send()
            cp.wait_recv()

    thru = [pltpu.HBM(a.shape, a.dtype) for a in list(grads) + list(recv)]
    out = pl.pallas_call(
        body, name=name, out_shape=tuple(thru),
        in_specs=[HBM] * (2 * n) + [SEM] * (2 * n) + [pl.BlockSpec(memory_space=pl.ANY)], out_specs=(HBM,) * (2 * n),
        input_output_aliases={i: i for i in range(2 * n)},
        compiler_params=pltpu.CompilerParams(has_side_effects=EFFECT),
    )(*grads, *recv, *send_sems, *recv_sems, after)
    return list(out[:n]), list(out[n:])


def _rs_across_start(parts, recv, name):
    n = len(parts)

    def body(*refs):
        part_refs, recv_refs = refs[:n], refs[n:2 * n]
        send_sems, recv_sems = refs[2 * n:3 * n], refs[3 * n:4 * n]
        token_ref = refs[-1]
        x, y, c, chips = _position()
        me = 2 * x + y
        for t in range(n):
            for cx, cy in chips:
                pltpu.make_async_remote_copy(src_ref=part_refs[t].at[2 * cx + cy], dst_ref=recv_refs[t].at[me],
                                             send_sem=send_sems[t], recv_sem=recv_sems[t], device_id=(cx, cy, c),
                                             device_id_type=MESH).start()
        token_ref[...] = jnp.zeros_like(token_ref)

    sems = (pltpu.SemaphoreType.DMA(()),) * (2 * n)
    thru = [pltpu.HBM(a.shape, a.dtype) for a in list(parts) + list(recv)]
    out = pl.pallas_call(
        body, name=name, out_shape=(*sems, *thru, jax.ShapeDtypeStruct((SUBLANES, LANES), F32)),
        in_specs=[HBM] * (2 * n), out_specs=(SEM,) * (2 * n) + (HBM,) * (2 * n) + (pl.BlockSpec(memory_space=pltpu.VMEM),),
        input_output_aliases={i: 2 * n + i for i in range(2 * n)},
        compiler_params=pltpu.CompilerParams(has_side_effects=EFFECT),
    )(*[pltpu.with_memory_space_constraint(a, pltpu.HBM) for a in list(parts) + list(recv)])
    return out[:n], out[n:2 * n], list(out[2 * n:3 * n]), list(out[3 * n:4 * n]), out[-1]


def _rs_across_wait(send_sems, recv_sems, parts, recv, after, name):
    n = len(parts)

    def body(*refs):
        recv_refs = refs[n:2 * n]
        s_sems, r_sems = refs[2 * n:3 * n], refs[3 * n:4 * n]
        x, y, c, _ = _position()
        for t in range(n):
            three = recv_refs[t].at[pl.ds(0, N_CHIPS - 1)]
            cp = pltpu.make_async_remote_copy(src_ref=three, dst_ref=three, send_sem=s_sems[t], recv_sem=r_sems[t],
                                              device_id=(x, y, c), device_id_type=MESH)
            cp.wait_send()
            cp.wait_recv()

    thru = [pltpu.HBM(a.shape, a.dtype) for a in list(parts) + list(recv)]
    out = pl.pallas_call(
        body, name=name, out_shape=tuple(thru),
        in_specs=[HBM] * (2 * n) + [SEM] * (2 * n) + [pl.BlockSpec(memory_space=pl.ANY)], out_specs=(HBM,) * (2 * n),
        input_output_aliases={i: i for i in range(2 * n)},
        compiler_params=pltpu.CompilerParams(has_side_effects=EFFECT),
    )(*parts, *recv, *send_sems, *recv_sems, after)
    return list(out[:n]), list(out[n:])


def _rs_join_halves(halves, name):
    n = len(halves)

    def body(*refs):
        ins, outs = refs[:n], refs[n:2 * n]
        send_sems, recv_sems = refs[2 * n:]
        x, y, c, _ = _position()
        cps = [pltpu.make_async_remote_copy(src_ref=ins[t].at[c], dst_ref=outs[t].at[c], send_sem=send_sems.at[t],
                                            recv_sem=recv_sems.at[t], device_id=(x, y, 1 - c), device_id_type=MESH)
               for t in range(n)]
        for cp in cps:
            cp.start()
        for t in range(n):
            blk = outs[t].at[1 - c]
            pltpu.make_async_remote_copy(src_ref=blk, dst_ref=blk, send_sem=send_sems.at[t], recv_sem=recv_sems.at[t],
                                         device_id=(x, y, 1 - c), device_id_type=MESH).wait_recv()
        for cp in cps:
            cp.wait_send()

    return pl.pallas_call(
        body, name=name, in_specs=[ANY] * n, out_specs=[ANY] * n,
        out_shape=[jax.ShapeDtypeStruct(h.shape, h.dtype) for h in halves],
        input_output_aliases={t: t for t in range(n)},
        scratch_shapes=[pltpu.SemaphoreType.DMA((n,)), pltpu.SemaphoreType.DMA((n,))],
    )(*halves)


def _all_gather_small(block, name):
    m_per, n = block.shape

    def body(x_ref, out_ref, send_sems, recv_sems, local_sem):
        x, y, c, chips = _position()
        me, sibling = (x, y, c), (x, y, 1 - c)

        def rows(px, py, pc):
            return out_ref.at[pl.ds((4 * px + 2 * py + pc) * m_per, m_per), :]

        def copy(k, blk, to, src=None):
            return pltpu.make_async_remote_copy(
                src_ref=rows(*blk) if src is None else src, dst_ref=rows(*blk), send_sem=send_sems.at[k],
                recv_sem=recv_sems.at[k], device_id=to, device_id_type=MESH)

        mine = pltpu.make_async_copy(x_ref, rows(*me), local_sem)
        mine.start()
        first = [copy(0, me, sibling, src=x_ref)]
        first += [copy(1 + j, me, (*chip, c), src=x_ref) for j, chip in enumerate(chips)]
        for cp in first:
            cp.start()
        passed = [copy(4 + j, (*chip, c), sibling) for j, chip in enumerate(chips)]
        for j, chip in enumerate(chips):
            copy(1 + j, (*chip, c), me).wait_recv()
            passed[j].start()
        copy(0, sibling, me).wait_recv()
        for j, chip in enumerate(chips):
            copy(4 + j, (*chip, 1 - c), me).wait_recv()
        for cp in first + passed:
            cp.wait_send()
        mine.wait()

    return pl.pallas_call(
        body, name=name, out_shape=jax.ShapeDtypeStruct((N_DEV * m_per, n), block.dtype),
        in_specs=[pl.BlockSpec(memory_space=pltpu.VMEM)], out_specs=pl.BlockSpec(memory_space=pltpu.VMEM),
        scratch_shapes=[pltpu.SemaphoreType.DMA((7,)), pltpu.SemaphoreType.DMA((7,)), pltpu.SemaphoreType.DMA],
    )(block)


def _row_tile(R, n):
    budget = 1 << 19
    if R * n <= budget or R % SUBLANES:
        return R
    t = R
    while t * n > budget and t % (2 * SUBLANES) == 0:
        t //= 2
    return t


def _pair_sum(g, recv, c_arr, name):
    _, _, R, n = g.shape
    tr = _row_tile(R, n)

    def body(c_ref, g_ref, r_ref, o_ref):
        o_ref[...] = (g_ref[...] + r_ref[...]).astype(ICI_DTYPE)

    grid_spec = pltpu.PrefetchScalarGridSpec(
        num_scalar_prefetch=1, grid=(N_CHIPS, R // tr),
        in_specs=[pl.BlockSpec((None, None, tr, n), lambda p, i, c: (p, c[0], i, 0)),
                  pl.BlockSpec((None, tr, n), lambda p, i, c: (p, i, 0))],
        out_specs=pl.BlockSpec((None, tr, n), lambda p, i, c: (p, i, 0)))
    return pl.pallas_call(
        body, name=name, grid_spec=grid_spec, out_shape=jax.ShapeDtypeStruct(recv.shape, ICI_DTYPE),
        compiler_params=_cparams("parallel", "parallel"),
    )(c_arr, g, recv)


def _chip_sum(recv, own, where, name):
    _, R, n = recv.shape
    tr = _row_tile(R, n)

    def body(w_ref, r0, r1, r2, r3, own_ref, o_ref):
        me = w_ref[0]
        terms = [jnp.where(me == k, own_ref[...], r[...]).astype(F32) for k, r in enumerate((r0, r1, r2, r3))]
        o_ref[...] = ((terms[0] + terms[1]) + terms[2]) + terms[3]

    def slot(k):
        return pl.BlockSpec((None, tr, n), lambda i, w: (w[1 + k], i, 0))

    grid_spec = pltpu.PrefetchScalarGridSpec(
        num_scalar_prefetch=1, grid=(R // tr,),
        in_specs=[slot(0), slot(1), slot(2), slot(3), pl.BlockSpec((None, tr, n), lambda i, w: (w[0], i, 0))],
        out_specs=pl.BlockSpec((None, tr, n), lambda i, w: (w[5], i, 0)))
    return pl.pallas_call(
        body, name=name, grid_spec=grid_spec, out_shape=jax.ShapeDtypeStruct((2, R, n), F32),
        compiler_params=_cparams("parallel"),
    )(where, recv, recv, recv, recv, own)


ADAM_C1 = 1.0 / (1.0 - ADAM_B1 ** ADAM_STEP)
ADAM_C2 = 1.0 / (1.0 - ADAM_B2 ** ADAM_STEP)


def _adamw_math(w, g, m, v):
    m = ADAM_B1 * m + (1.0 - ADAM_B1) * g
    v = ADAM_B2 * v + (1.0 - ADAM_B2) * (g * g)
    delta = -ADAM_LR * ((m * ADAM_C1) / (jnp.sqrt(v * ADAM_C2) + ADAM_EPS) + ADAM_WD * w)
    return delta, m, v


def _adamw(w, g, m, v, name):
    R, n = w.shape
    tr = _row_tile(R, n)

    def body(w_ref, g_ref, m_ref, v_ref, d_ref, nm_ref, nv_ref):
        d_ref[...], nm_ref[...], nv_ref[...] = _adamw_math(w_ref[...], g_ref[...], m_ref[...], v_ref[...])

    spec = pl.BlockSpec((tr, n), lambda i: (i, 0))
    return pl.pallas_call(
        body, name=name, grid=(R // tr,), in_specs=[spec] * 4, out_specs=[spec] * 3,
        out_shape=[jax.ShapeDtypeStruct((R, n), F32)] * 3, compiler_params=_cparams("parallel"),
    )(w, g, m, v)


def _adamw_half(w, g, m, v, half, prev, name):
    R, n = w.shape
    tr = _row_tile(R // 2, n)
    off = half * (R // 2 // tr)

    def body(*refs):
        w_ref, g_ref, m_ref, v_ref = refs[:4]
        go_ref, d_ref, nm_ref, nv_ref = refs[-4:]
        gv = g_ref[...]
        go_ref[...] = gv
        d_ref[...], nm_ref[...], nv_ref[...] = _adamw_math(w_ref[...], gv, m_ref[...], v_ref[...])

    rows = pl.BlockSpec((tr, n), lambda i: (i + off, 0))
    carried = [] if prev is None else list(prev)
    return pl.pallas_call(
        body, name=name, grid=(R // 2 // tr,),
        in_specs=[rows, pl.BlockSpec((tr, n), lambda i: (i, 0)), rows, rows] + [ANY] * len(carried),
        out_specs=[rows] * 4, out_shape=[jax.ShapeDtypeStruct((R, n), F32)] * 4,
        input_output_aliases={4 + i: i for i in range(len(carried))},
        compiler_params=_cparams("parallel"),
    )(w, g, m, v, *carried)


def _adamw_small(w, gall, m, v, name):
    M, n = w.shape

    def body(w_ref, g_ref, m_ref, v_ref, gs_ref, d_ref, nm_ref, nv_ref):
        g = g_ref[0:M, :]
        for d in range(1, N_DEV):
            g = g + g_ref[d * M:(d + 1) * M, :]
        gs_ref[...] = g
        d_ref[...], nm_ref[...], nv_ref[...] = _adamw_math(w_ref[...], g, m_ref[...], v_ref[...])

    return pl.pallas_call(
        body, name=name, out_shape=[jax.ShapeDtypeStruct((M, n), F32)] * 4,
    )(w, gall, m, v)


SMALL_ROWS = 24
MID_ROWS = 4


def _pack_small(ln1_g, ln1_b, ln2_g, ln2_b, norm_w, sinks, a_log, dt_bias):
    mixed = jnp.concatenate([norm_w.reshape(-1), sinks.reshape(-1), a_log.reshape(-1), dt_bias.reshape(-1)])
    mixed = jnp.pad(mixed, (0, D_MODEL - mixed.shape[0]))[None]
    pad = jnp.zeros((SMALL_ROWS - 4 * DEPTH - 1, D_MODEL), F32)
    return jnp.concatenate([ln1_g, ln1_b, ln2_g, ln2_b, mixed, pad], axis=0)


def _unpack_small(p):
    mixed = p[4 * DEPTH]
    return (p[0:4], p[4:8], p[8:12], p[12:16], mixed[0:256].reshape(2, 128), mixed[256:272].reshape(2, 8),
            mixed[272:280].reshape(2, 4), mixed[280:288].reshape(2, 4))


def _pack_mid(conv_w, rconv_w, rconv_b, b_a, b_x, lam):
    lead = conv_w.shape[0]
    flat = jnp.concatenate([conv_w.reshape(lead, -1), rconv_w.reshape(lead, -1), rconv_b, b_a, b_x, lam], axis=1)
    return jnp.pad(flat, ((0, 0), (0, MID_ROWS * D_MODEL - flat.shape[1]))).reshape(lead, MID_ROWS, D_MODEL)


def _unpack_mid(p):
    lead = p.shape[:-2]
    f = p.reshape(lead + (MID_ROWS * D_MODEL,))
    return (f[..., 0:1536].reshape(lead + (4, 384)), f[..., 1536:2560].reshape(lead + (4, 256)),
            f[..., 2560:2816], f[..., 2816:3072], f[..., 3072:3328], f[..., 3328:3584])


def _cols_from_chips(g):
    p, L, R, n = g.shape
    return g.transpose(1, 2, 0, 3).reshape(L, R, p * n)


def _rows_from_chips(g):
    p, L, R, n = g.shape
    return g.transpose(1, 0, 2, 3).reshape(L, p * R, n)


def _cols_to_chips(g):
    L, R, n4 = g.shape
    return g.reshape(L, R, N_CHIPS, n4 // N_CHIPS).transpose(2, 0, 1, 3)


def _rows_to_chips(g):
    L, R4, n = g.shape
    return g.reshape(L, N_CHIPS, R4 // N_CHIPS, n).transpose(1, 0, 2, 3)


def _halves(a):
    return a.reshape(2, -1, a.shape[-1])


def _pad_hyb_cols(w):
    z = jnp.zeros(w.shape[:-1] + (HYB_PAD - HP_BG - 2 * B_HEADS,), w.dtype)
    return jnp.concatenate([w[..., 0:512], w[..., 768:2304], w[..., 2304:2816], w[..., 512:768], w[..., 2816:2824], z], axis=-1)


def _unpad_hyb_cols(w):
    return jnp.concatenate([w[..., 0:512], w[..., 2560:2816], w[..., 512:2048], w[..., 2048:2560], w[..., 2816:2824]], axis=-1)


def _hybrid_fwd(x, W, j, tables, sfx, out_ready, ln):
    cos, sin_s = tables
    T = x.shape[0]
    N = T // B_CHUNK
    proj = _matmul(x, W["hyb_w_in"][j], "nn", "hyb_in" + sfx)
    sinks_b = jnp.broadcast_to(W["hyb_sinks"][j][:, None], (A_Q_HEADS, LANES))
    o_a, lse = _attn_fwd(proj, cos, sin_s, sinks_b, "attn_fwd" + sfx)
    qkvc = _dnconv_fwd(proj, W["hyb_conv_w"][j], "dnconv_fwd" + sfx)
    bg = proj[:, HP_BG:HP_BG + 2 * B_HEADS]
    beta = jax.nn.sigmoid(bg[:, :B_HEADS])
    pre = bg[:, B_HEADS:] + W["hyb_dt_bias"][j][None]
    g = -jnp.exp(W["hyb_a_log"][j])[None] * jax.nn.softplus(pre)
    grow = g.T.reshape(B_HEADS, N, 1, B_CHUNK)
    brow = beta.T.reshape(B_HEADS, N, 1, B_CHUNK)
    nw = W["hyb_norm_w"][j][None]
    o_b, states, invs = _dn_fwd(qkvc, proj, nw, grow, brow, "dn_fwd" + sfx)
    mix = jnp.concatenate([o_a, o_b], axis=1).astype(ACT_DTYPE)
    out_ready(mix)
    out = _matmul(mix, W["hyb_w_out"][j], "nn", "hyb_out" + sfx, epi=_epi_residual_ln, extra=ln, out_dtype=_ln_out(),
                  tm=512)
    res = dict(proj=proj, o_a=o_a, lse=lse, qkvc=qkvc, beta=beta, pre=pre, g=g, grow=grow, brow=brow,
               states=states, invs=invs, mix=mix, sinks_b=sinks_b, nw=nw)
    return out, res


def _hybrid_bwd(x, du, dub, W, j, res, tables, sfx):
    cos, sin_s = tables
    T = x.shape[0]
    proj = res["proj"]
    d_wout = _matmul(res["mix"], dub, "tn", "hyb_out_dw" + sfx)
    dmix = _matmul(dub, W["hyb_w_out"][j], "nt", "hyb_out_dx" + sfx)
    dproj, dkc, dkp, dvc, dvp, dsink = _attn_bwd(proj, cos, sin_s, res["sinks_b"], res["o_a"], res["lse"], dmix,
                                                  "attn_bwd" + sfx)
    zpad = jnp.zeros((WINDOW, LANES), F32)
    dk = dkc + jnp.concatenate([dkp[WINDOW:], zpad], axis=0)
    dv = dvc + jnp.concatenate([dvp[WINDOW:], zpad], axis=0)
    dqkvc, dproj, dg4, dbeta4, dnw = _dn_bwd(res["qkvc"], proj, res["nw"], res["grow"], res["brow"], res["states"],
                                             res["invs"], dmix, dproj, "dn_bwd" + sfx)
    dproj, dconv = _dnconv_bwd(proj, W["hyb_conv_w"][j], dqkvc, dproj, "dnconv_bwd" + sfx)
    dg = dg4.reshape(B_HEADS, T).T
    dbeta = dbeta4.reshape(B_HEADS, T).T
    beta = res["beta"]
    dbeta_logit = dbeta * beta * (1.0 - beta)
    da_logit = dg * (-jnp.exp(W["hyb_a_log"][j]))[None] * jax.nn.sigmoid(res["pre"])
    d_dt_bias = jnp.sum(da_logit, axis=0)
    d_a_log = jnp.sum(dg * res["g"], axis=0)
    zcols = jnp.zeros((T, HYB_PAD - HP_BG - 2 * B_HEADS), F32)
    tail = jnp.concatenate([dk, dv, dbeta_logit, da_logit, zcols], axis=1).astype(ACT_DTYPE)
    dproj = lax.dynamic_update_slice(dproj, tail, (0, HP_K))
    d_win = _matmul(x, dproj, "tn", "hyb_in_dw" + sfx)
    dx = _matmul(dproj, W["hyb_w_in"][j], "nt", "hyb_in_dx" + sfx, epi=_epi_add_residual, extra=du)
    grads = dict(hyb_w_in=d_win, hyb_w_out=d_wout, hyb_sinks=dsink[0], hyb_conv_w=dconv, hyb_a_log=d_a_log,
                 hyb_dt_bias=d_dt_bias, hyb_norm_w=jnp.sum(dnw[:, 0, :], axis=0))
    return dx, grads


def _rec_fwd(x, W, j, sfx, ln):
    proj = _matmul(x, W["rec_w_in"][j], "nn", "rec_in" + sfx)
    sp = jax.nn.softplus(-W["rec_lambda"][j])[None]
    hg, h = _rglru_fwd(proj, W["rec_conv_w"][j], W["rec_conv_b"][j][None], W["rec_w_a"][j], W["rec_w_x"][j],
                       W["rec_b_a"][j][None], W["rec_b_x"][j][None], sp, "rglru_fwd" + sfx)
    out = _matmul(hg, W["rec_w_out"][j], "nn", "rec_out" + sfx, epi=_epi_residual_ln, extra=ln, out_dtype=_ln_out(),
                  tm=512)
    return out, dict(proj=proj, hg=hg, h=h, sp=sp)


def _rec_bwd(x, du, dub, W, j, res, sfx):
    d_wout = _matmul(res["hg"], dub, "tn", "rec_out_dw" + sfx)
    dhg = _matmul(dub, W["rec_w_out"][j], "nt", "rec_out_dx" + sfx)
    dproj, dcw, dcb, dwa, dwx, dba, dbx, dsp = _rglru_bwd(
        res["proj"], W["rec_conv_w"][j], W["rec_conv_b"][j][None], W["rec_w_a"][j], W["rec_w_x"][j],
        W["rec_b_a"][j][None], W["rec_b_x"][j][None], res["sp"], res["h"], dhg, "rglru_bwd" + sfx)
    d_lam = dsp[0] * (-jax.nn.sigmoid(-W["rec_lambda"][j]))
    d_win = _matmul(x, dproj, "tn", "rec_in_dw" + sfx)
    dx = _matmul(dproj, W["rec_w_in"][j], "nt", "rec_in_dx" + sfx, epi=_epi_add_residual, extra=du)
    grads = dict(rec_w_in=d_win, rec_w_out=d_wout, rec_conv_w=dcw, rec_conv_b=dcb[0], rec_w_a=dwa, rec_w_x=dwx,
                 rec_b_a=dba[0], rec_b_x=dbx[0], rec_lambda=d_lam)
    return dx, grads


def _local_step(x, tgt, W, mlp_w, mixer_ready, on_group):
    T = x.shape[0]
    tables = _rope_tables(T)
    acts = []
    xb = x.astype(ACT_DTYPE)
    for layer in range(DEPTH):
        j, sfx = layer // 2, ""
        mixer_ready(layer, xb)
        ln1 = (x, W["ln1_g"][layer][None], W["ln1_b"][layer][None])
        if layer % 2 == 0:
            (x1, x1b, u1), res = _hybrid_fwd(xb, W, j, tables, sfx,
                                             functools.partial(mixer_ready, layer, out_projection=True), ln1)
        else:
            (x1, x1b, u1), res = _rec_fwd(xb, W, j, sfx, ln1)
        w1, w2, wl = mlp_w(layer, x1b)
        h1 = _matmul(x1b, w1, "nn", "mlp_up", out_dtype=ACT_DTYPE, b_chips=("j", wl))
        x2, x2b, u2 = _matmul(h1, w2, "nn", "mlp_down", a_fn=_relu2, b_chips=("k", wl), epi=_epi_residual_ln,
                              extra=(x1, W["ln2_g"][layer][None], W["ln2_b"][layer][None]), out_dtype=_ln_out())
        acts.append(dict(xb=xb, res=res, u1=u1, x1b=x1b, h1=h1, u2=u2))
        x, xb = x2, x2b
    dx, loss = _loss_head(x, tgt, "loss_head")
    per_layer = [None] * DEPTH
    d_w1 = [lax.empty((N_CHIPS, 2, D_MODEL, D_FF // N_CHIPS), F32) for _ in range(DEPTH // 2)]
    d_w2 = [lax.empty((N_CHIPS, 2, D_FF // N_CHIPS, D_MODEL), F32) for _ in range(DEPTH // 2)]
    token = None
    for layer in reversed(range(DEPTH)):
        j, a = layer // 2, acts[layer]
        ln2_g = W["ln2_g"][layer][None]
        if token is not None:
            ln2_g = ln2_g + token
        du2, du2b, dg2, db2 = _ln_bwd(a["u2"], ln2_g, dx, "ln_bwd")
        w1, w2, wl = mlp_w(layer, du2b)
        d_w2[j] = _matmul(a["h1"], du2b, "tn", "mlp_down_dw", a_fn=_relu2, out_chips=("i", layer % 2, d_w2[j]))
        dh1 = _matmul(du2b, w2, "nt", "mlp_down_dx", epi=_epi_drelu2, extra=a["h1"], out_dtype=ACT_DTYPE,
                      b_chips=("j", wl))
        d_w1[j] = _matmul(a["x1b"], dh1, "tn", "mlp_up_dw", out_chips=("j", layer % 2, d_w1[j]))
        dx1 = _matmul(dh1, w1, "nt", "mlp_up_dx", epi=_epi_add_residual, extra=du2, b_chips=("k", wl))
        ln1_g = W["ln1_g"][layer][None]
        token = on_group(None, layer, dx1, None, None) if layer == 1 else None
        if token is not None:
            ln1_g = ln1_g + token
        du1, du1b, dg1, db1 = _ln_bwd(a["u1"], ln1_g, dx1, "ln_bwd")
        if layer % 2 == 0:
            dx, g = _hybrid_bwd(a["xb"], du1, du1b, W, j, a["res"], tables, "")
        else:
            dx, g = _rec_bwd(a["xb"], du1, du1b, W, j, a["res"], "")
        g.update(ln1_g=dg1[0], ln1_b=db1[0], ln2_g=dg2[0], ln2_b=db2[0])
        per_layer[layer] = g
        if layer % 2 == 0:
            token = on_group(j, per_layer[layer], per_layer[layer + 1], d_w1[j], d_w2[j])
    grads = {}
    for name in ("ln1_g", "ln1_b", "ln2_g", "ln2_b"):
        grads[name] = jnp.stack([per_layer[l][name] for l in range(DEPTH)])
    for name in ("hyb_norm_w", "hyb_sinks", "hyb_a_log", "hyb_dt_bias"):
        grads[name] = jnp.stack([per_layer[l][name] for l in (0, 2)])
    return loss, dx, grads


BIG = ("hyb_w_in", "hyb_w_out", "rec_w_in", "rec_w_out", "mlp_w1", "mlp_w2", "rec_w_a", "rec_w_x")
COL_SHARDED = ("hyb_w_in", "rec_w_in", "mlp_w1")
CHIP_MAJOR = ("mlp_w1", "mlp_w2")
MID = ("hyb_conv_w", "rec_conv_w", "rec_conv_b", "rec_b_a", "rec_b_x", "rec_lambda")
SMALL = ("ln1_g", "ln1_b", "ln2_g", "ln2_b", "hyb_norm_w", "hyb_sinks", "hyb_a_log", "hyb_dt_bias")
WEIGHTS = ("hyb_w_in", "hyb_sinks", "hyb_conv_w", "hyb_a_log", "hyb_dt_bias", "hyb_norm_w", "hyb_w_out", "rec_w_in",
           "rec_conv_w", "rec_conv_b", "rec_w_a", "rec_b_a", "rec_w_x", "rec_b_x", "rec_lambda", "rec_w_out", "ln1_g",
           "ln1_b", "mlp_w1", "mlp_w2", "ln2_g", "ln2_b")


def _gather_full_weights(w):
    wb = {k: w[k].astype(MXU_DTYPE) for k in BIG}
    now = ("hyb_w_in",)
    shards = [_halves(wb[k][:1]) for k in now]
    shards.append(_pack_mid(*[w[k] for k in MID]))
    got = _all_gather_weights(shards, "all_gather_weights")
    me = 2 * lax.axis_index("x") + lax.axis_index("y")
    got = [lax.dynamic_update_slice(g, s[None], (me, 0, 0, 0)) for s, g in zip(shards, got)]

    def full(k, g):
        g = g.reshape((N_CHIPS,) + w[k].shape[1:])
        if k in CHIP_MAJOR:
            return g[:, None]
        if k in ("rec_w_a", "rec_w_x"):
            return g.transpose(1, 0, 2, 3).reshape(LRU_BLOCKS, LRU_BLOCK_W, LRU_BLOCK_W)
        f = _cols_from_chips(g[:, None])[0] if k in COL_SHARDED else _rows_from_chips(g[:, None])[0]
        return _pad_hyb_cols(f) if k == "hyb_w_in" else f

    rec = ("rec_w_in", "rec_w_out", "rec_w_a", "rec_w_x")
    groups = [(("hyb_w_out",), 0), (CHIP_MAJOR, 0), (rec, 0), (CHIP_MAJOR, 1), (("hyb_w_in", "hyb_w_out"), 1),
              (CHIP_MAJOR, 2), (rec, 1), (CHIP_MAJOR, 3)]
    own = [wb[k][j] for names, j in groups for k in names]
    land = [lax.dynamic_update_slice(lax.empty((N_CHIPS,) + o.shape, o.dtype), o[None], (me,) + (0,) * o.ndim)
            for o in own]
    send_sems, recv_sems, own, land = _gather_start(own, land, got[-1], "gather_start")
    W = {k: [None] * w[k].shape[0] for k in BIG}
    for k, g in zip(now, got[:-1]):
        W[k][0] = full(k, g)
    arrived = [0]

    def ensure(upto, after):
        while arrived[0] <= upto:
            gi = arrived[0]
            names, j = groups[gi]
            lo = sum(len(nm) for nm, _ in groups[:gi])
            sl = slice(lo, lo + len(names))
            got_g = _gather_wait(send_sems[sl], recv_sems[sl], own[sl], land[sl], after, "gather_wait_%d" % gi)
            for k, g in zip(names, got_g):
                W[k][j] = full(k, g)
            arrived[0] += 1

    def mixer_ready(layer, after, out_projection=False):
        if layer:
            ensure({1: 2, 2: 4, 3: 6}[layer], after)
        elif out_projection:
            ensure(0, after)

    def mlp_w(layer, after):
        ensure({0: 1, 1: 3, 2: 5, 3: 7}[layer], after)
        return W["mlp_w1"][layer], W["mlp_w2"][layer], 0

    conv_w, rconv_w, rconv_b, b_a, b_x, lam = _unpack_mid(got[-1])
    W["hyb_conv_w"] = conv_w.transpose(1, 2, 0, 3).reshape(2, CONV_K, 3 * B_W)
    W["rec_conv_w"] = rconv_w.transpose(1, 2, 0, 3).reshape(2, CONV_K, D_MODEL)
    for k, v in (("rec_conv_b", rconv_b), ("rec_b_a", b_a), ("rec_b_x", b_x), ("rec_lambda", lam)):
        W[k] = v.transpose(1, 0, 2).reshape(2, D_MODEL)
    for k in SMALL:
        W[k] = w[k]
    return W, mlp_w, mixer_ready


def _group_by_chip(gh, gr, d_w1, d_w2):
    g = dict(gh, **gr)
    g["hyb_w_in"] = _unpad_hyb_cols(g["hyb_w_in"])
    out = []
    for k in BIG:
        if k == "mlp_w1":
            v = d_w1
        elif k == "mlp_w2":
            v = d_w2
        elif k in ("rec_w_a", "rec_w_x"):
            v = g[k].reshape(1, LRU_BLOCKS, N_CHIPS, LRU_BLOCK_W // N_CHIPS, LRU_BLOCK_W).transpose(2, 0, 1, 3, 4)
        else:
            v = _cols_to_chips(g[k][None]) if k in COL_SHARDED else _rows_to_chips(g[k][None])
        out.append(v.reshape(N_CHIPS, 2, -1, v.shape[-1]))
    conv_w = g["hyb_conv_w"].reshape(CONV_K, N_CHIPS, -1).transpose(1, 0, 2)
    rconv_w = g["rec_conv_w"].reshape(CONV_K, N_CHIPS, -1).transpose(1, 0, 2)
    vecs = [g[k].reshape(N_CHIPS, -1) for k in ("rec_conv_b", "rec_b_a", "rec_b_x", "rec_lambda")]
    out.append(_pack_mid(conv_w, rconv_w, *vecs).reshape(N_CHIPS, 2, MID_ROWS // 2, D_MODEL))
    return out


def kernel(x, hyb_w_in, hyb_sinks, hyb_conv_w, hyb_a_log, hyb_dt_bias, hyb_norm_w, hyb_w_out, rec_w_in, rec_conv_w, rec_conv_b, rec_w_a, rec_b_a, rec_w_x, rec_b_x, rec_lambda, rec_w_out, ln1_g, ln1_b, mlp_w1, mlp_w2, ln2_g, ln2_b, loss_target, m_hyb_w_in, m_hyb_sinks, m_hyb_conv_w, m_hyb_a_log, m_hyb_dt_bias, m_hyb_norm_w, m_hyb_w_out, m_rec_w_in, m_rec_conv_w, m_rec_conv_b, m_rec_w_a, m_rec_b_a, m_rec_w_x, m_rec_b_x, m_rec_lambda, m_rec_w_out, m_ln1_g, m_ln1_b, m_mlp_w1, m_mlp_w2, m_ln2_g, m_ln2_b, v_hyb_w_in, v_hyb_sinks, v_hyb_conv_w, v_hyb_a_log, v_hyb_dt_bias, v_hyb_norm_w, v_hyb_w_out, v_rec_w_in, v_rec_conv_w, v_rec_conv_b, v_rec_w_a, v_rec_b_a, v_rec_w_x, v_rec_b_x, v_rec_lambda, v_rec_w_out, v_ln1_g, v_ln1_b, v_mlp_w1, v_mlp_w2, v_ln2_g, v_ln2_b):
    args = locals()
    w = {k: args[k] for k in WEIGHTS}
    m = {k: args["m_" + k] for k in WEIGHTS}
    v = {k: args["v_" + k] for k in WEIGHTS}

    W, mlp_w, mixer_ready = _gather_full_weights(w)

    core = lax.axis_index("c").astype(jnp.int32)
    me = (2 * lax.axis_index("x") + lax.axis_index("y")).astype(jnp.int32)
    slots = jnp.arange(N_CHIPS, dtype=jnp.int32)
    where = jnp.concatenate([me[None], jnp.where(slots == me, (slots + 1) % N_CHIPS, slots), core[None]])
    state = {}

    def on_group(group, gh, gr, d_w1, d_w2):
        if group is None:
            send_sems, recv_sems, by_chip, recv, _ = state["sibling"]
            by_chip, from_sibling = _rs_sibling_wait(send_sems, recv_sems, by_chip, recv, gr, "rs_sibling_wait")
            group = 1
        else:
            by_chip = _group_by_chip(gh, gr, d_w1, d_w2)
            if group == 1:
                recv = [lax.empty((N_CHIPS,) + g.shape[2:], g.dtype) for g in by_chip]
                state["sibling"] = _rs_sibling_start(by_chip, recv, "rs_sibling_start")
                return state["sibling"][4][0, 0]
            from_sibling = _rs_to_sibling(by_chip, "rs_to_sibling")
        pair = [_pair_sum(g, r, core[None], "pair_sum") for g, r in zip(by_chip, from_sibling)]
        recv = [lax.empty(p.shape, p.dtype) for p in pair]
        state[group] = _rs_across_start(pair, recv, "rs_across_start_%d" % group)
        return state[group][4][0, 0]

    def finish_group(group, after, prev):
        send_sems, recv_sems, pair, recv, _ = state[group]
        pair, from_chips = _rs_across_wait(send_sems, recv_sems, pair, recv, after, "rs_across_wait_%d" % group)
        half = [_chip_sum(r, p, where, "chip_sum") for r, p in zip(from_chips, pair)]
        joined = _rs_join_halves(half, "rs_join_halves")
        outs = {}
        for k, g in zip(BIG, joined[:-1]):
            n = g.shape[-1]
            outs[k] = _adamw_half(w[k].reshape(-1, n), g.reshape(-1, n), m[k].reshape(-1, n), v[k].reshape(-1, n),
                                  group, None if prev is None else prev[k], "adamw")
        return outs, joined[-1]

    loss, dx, grads = _local_step(x[0], loss_target[0], W, mlp_w, mixer_ready, on_group)
    loss = lax.psum(loss[0, 0], ("x", "y", "c"))
    g_out, d_out, m_out, v_out = {}, {}, {}, {}

    upper, mid_upper = finish_group(1, state[0][4], None)
    small_g = _pack_small(*[grads[k] for k in SMALL])
    small_all = _all_gather_small(small_g, "all_gather_small")
    sw, sm, sv = (_pack_small(*[t[k] for k in SMALL]) for t in (w, m, v))
    sg, sd, snm, snv = _adamw_small(sw, small_all, sm, sv, "adamw_small")
    for dst, packed in ((g_out, sg), (d_out, sd), (m_out, snm), (v_out, snv)):
        for k, val in zip(SMALL, _unpack_small(packed)):
            dst[k] = val

    done = sum([upper[k][1][0, 0] for k in BIG], sg[0, 0]).reshape(1, 1)
    both, mid_lower = finish_group(0, done, upper)
    for k in BIG:
        shape = w[k].shape
        g_out[k], d_out[k], m_out[k], v_out[k] = (t.reshape(shape) for t in both[k])
    mid_w, mid_m, mid_v = (_pack_mid(*[t[k] for k in MID]).reshape(-1, D_MODEL) for t in (w, m, v))
    mid_g = jnp.concatenate([mid_lower.reshape(-1, D_MODEL), mid_upper.reshape(-1, D_MODEL)])
    mid_d, mid_nm, mid_nv = _adamw(mid_w, mid_g, mid_m, mid_v, "adamw_mid")
    for dst, packed in ((g_out, mid_g), (d_out, mid_d), (m_out, mid_nm), (v_out, mid_nv)):
        for k, val in zip(MID, _unpack_mid(packed.reshape(2, MID_ROWS, D_MODEL))):
            dst[k] = val.reshape(w[k].shape)

    return (loss, dx[None], *[g_out[k] for k in WEIGHTS], *[d_out[k] for k in WEIGHTS],
            *[m_out[k] for k in WEIGHTS], *[v_out[k] for k in WEIGHTS])
```

```python
import functools

import jax
import jax.numpy as jnp
import numpy as np
from jax import lax
from jax.experimental import pallas as pl
from jax.experimental.pallas import tpu as pltpu

F32 = jnp.float32
MXU_DTYPE = jnp.bfloat16
ACT_DTYPE = jnp.bfloat16
ICI_DTYPE = jnp.bfloat16

D_MODEL = 1024
DEPTH = 4
A_HEAD_DIM = 64
A_Q_HEADS = 8
A_KV_HEADS = 2
WINDOW = 128
ROPE_THETA = 10000.0
B_HEADS = 4
B_HEAD_DIM = 128
B_CHUNK = 64
CONV_K = 4
LRU_BLOCKS = 4
LRU_BLOCK_W = D_MODEL // LRU_BLOCKS
LRU_C = 8.0
D_FF = 4 * D_MODEL
A_Q_W = A_Q_HEADS * A_HEAD_DIM
A_KV_W = A_KV_HEADS * A_HEAD_DIM
B_W = B_HEADS * B_HEAD_DIM
HYB_PROJ = A_Q_W + 2 * A_KV_W + 4 * B_W + 2 * B_HEADS
DN_ALPHA = (2 * DEPTH) ** 0.25
LN_EPS = 1e-5
NORM_EPS = 1e-6
ADAM_LR = 0.001
ADAM_B1 = 0.9
ADAM_B2 = 0.999
ADAM_EPS = 1e-08
ADAM_WD = 0.01
ADAM_STEP = 10

HP_Q = 0
HP_QKVB = 512
HP_Z = 2048
HP_K = 2560
HP_V = 2688
HP_BG = 2816
HYB_PAD = 3072

N_CHIPS = 4
N_DEV = 8
V7X_VMEM_LIMIT = 48 * 1024 * 1024
LANES = 128
SUBLANES = 8
NEG_BIG = -1e30

NN = (((1,), (0,)), ((), ()))
NT = (((1,), (1,)), ((), ()))
TN = (((0,), (0,)), ((), ()))


def _cparams(*sem):
    return pltpu.CompilerParams(dimension_semantics=sem, vmem_limit_bytes=V7X_VMEM_LIMIT)


def _dot(a, b, dims=NN):
    return lax.dot_general(a.astype(MXU_DTYPE), b.astype(MXU_DTYPE), dims, preferred_element_type=F32)


def _split_bf16(a):
    hi = a.astype(jnp.bfloat16)
    return hi, (a - hi.astype(F32)).astype(jnp.bfloat16)


def _dotf(a, b, dims=NN):
    ah, al = _split_bf16(a)
    bh, bl = _split_bf16(b)
    dg = functools.partial(lax.dot_general, dimension_numbers=dims, preferred_element_type=F32)
    return dg(ah, bh) + (dg(ah, bl) + dg(al, bh))


def _tile(dim, pref):
    t = min(dim, pref)
    while dim % t:
        t //= 2
    return t


def _sigmoid(x):
    return 1.0 / (1.0 + jnp.exp(-x))


def _silu(x):
    return x * _sigmoid(x)


def _dsilu(x):
    s = _sigmoid(x)
    return s * (1.0 + x * (1.0 - s))


GELU_C = 0.7978845608028654
GELU_A = 0.044715


def _gelu(x):
    return 0.5 * x * (1.0 + jnp.tanh(GELU_C * (x + GELU_A * x * x * x)))


def _dgelu(x):
    t = jnp.tanh(GELU_C * (x + GELU_A * x * x * x))
    return 0.5 * (1.0 + t) + 0.5 * x * (1.0 - t * t) * GELU_C * (1.0 + 3.0 * GELU_A * x * x)


def _matmul(a, b, mode, name, *, tm=1024, tn=1024, tk=1024, a_fn=None, epi=None, extra=None, out_dtype=F32,
            b_chips=None, out_chips=None):
    chunk = tk
    if mode == "tn":
        K, M = a.shape
        if K <= 4 * tk:
            tk, tm = K, tm // 2
    else:
        M, K = a.shape
    whole_k = False
    if b_chips is not None:
        g, b_layer = b_chips
        r, n = b.shape[2:]
        n_dim, k_dim = (r, n) if mode == "nt" else (n, r)
        N = N_CHIPS * n_dim if g == "j" else n_dim
        assert K == (N_CHIPS * k_dim if g == "k" else k_dim)
        if g == "j":
            tn = n_dim
        else:
            whole_k, tk, tm = True, K, tm // 2
    elif mode == "nt":
        N = b.shape[0]
        if tk < K <= 3 * tk:
            tk, tm = K, tm // 2
    else:
        N = b.shape[1]
    if out_chips is not None:
        og, o_layer, o_buf = out_chips
        if og == "j":
            tn = o_buf.shape[3]
        else:
            tm = min(tm, o_buf.shape[2])
    tm, tn, tk = _tile(M, tm), _tile(N, tn), _tile(K, tk)
    nk = K // tk
    if mode == "tn":
        a_spec = pl.BlockSpec((tk, tm), lambda i, j, k: (k, i))
    else:
        a_spec = pl.BlockSpec((tm, tk), lambda i, j, k: (i, k))
    b_block = (tn, tk) if mode == "nt" else (tk, tn)
    if b_chips is None:
        b_spec = pl.BlockSpec(b_block, (lambda i, j, k: (j, k)) if mode == "nt" else (lambda i, j, k: (k, j)))
    elif mode == "nt" and whole_k:
        b_spec = pl.BlockSpec((N_CHIPS, None, tn, k_dim), lambda i, j, k: (0, b_layer, j, 0))
    elif mode == "nt":
        b_spec = pl.BlockSpec((None, None) + b_block, lambda i, j, k: (j, b_layer, 0, k))
    elif whole_k:
        b_spec = pl.BlockSpec((N_CHIPS, None, k_dim, tn), lambda i, j, k: (0, b_layer, 0, j),
                              pipeline_mode=pl.Buffered(1) if N == tn else None)
    else:
        b_spec = pl.BlockSpec((None, None) + b_block, lambda i, j, k: (j, b_layer, k, 0))
    o_spec = pl.BlockSpec((tm, tn), lambda i, j, k: (i, j))
    e_spec = o_spec
    if out_chips is not None:
        per = o_buf.shape[2] // tm
        o_spec = pl.BlockSpec((None, None, tm, tn), (lambda i, j, k: (j, o_layer, i, 0)) if og == "j"
                              else (lambda i, j, k: (i // per, o_layer, i % per, j)))
    dims = {"nn": NN, "nt": NT, "tn": TN}[mode]
    extras = () if extra is None else (extra if isinstance(extra, tuple) else (extra,))
    out_dtypes = out_dtype if isinstance(out_dtype, tuple) else (out_dtype,)
    n_in = 2 + len(extras) + (out_chips is not None)

    def body(*refs):
        a_ref, b_ref = refs[0], refs[1]
        e_refs = refs[2:2 + len(extras)]
        o_refs = refs[n_in:n_in + len(out_dtypes)]
        av = a_ref[...]
        if a_fn is not None:
            av = a_fn(av)
        if whole_k and mode == "nt":
            part = _dot(av[:, :k_dim], b_ref[0], dims)
            for chip in range(1, N_CHIPS):
                part = part + _dot(av[:, chip * k_dim:(chip + 1) * k_dim], b_ref[chip], dims)
        elif mode == "tn" and tk > chunk:
            part = _dot(av[:chunk], b_ref[0:chunk, :], dims)
            for c0 in range(chunk, tk, chunk):
                part = part + _dot(av[c0:c0 + chunk], b_ref[c0:c0 + chunk, :], dims)
        else:
            bv = b_ref[...]
            if whole_k:
                bv = bv.reshape(K, tn)
            part = _dot(av, bv, dims)

        def finish(acc):
            if epi is not None:
                acc = epi(acc, *[e[...] for e in e_refs])
            for o_ref, val, dt in zip(o_refs, acc if isinstance(acc, tuple) else (acc,), out_dtypes):
                o_ref[...] = val.astype(dt)

        if nk == 1:
            finish(part)
        else:
            acc_ref = refs[-1]
            k = pl.program_id(2)

            @pl.when(k == 0)
            def _():
                acc_ref[...] = part

            @pl.when(k > 0)
            def _():
                acc_ref[...] += part

            @pl.when(k == nk - 1)
            def _():
                finish(acc_ref[...])

    row_spec = pl.BlockSpec((1, tn), lambda i, j, k: (0, j))
    in_specs = [a_spec, b_spec] + [row_spec if e.shape[0] == 1 else e_spec for e in extras]
    args = (a, b) + extras
    out_shape = [jax.ShapeDtypeStruct((M, N), dt) for dt in out_dtypes]
    out_specs = [o_spec] * len(out_dtypes)
    aliases = {}
    if out_chips is not None:
        in_specs.append(pl.BlockSpec(memory_space=pl.ANY))
        args += (o_buf,)
        out_shape = [jax.ShapeDtypeStruct(o_buf.shape, o_buf.dtype)]
        aliases = {n_in - 1: 0}
    if not isinstance(out_dtype, tuple):
        out_shape, out_specs = out_shape[0], out_specs[0]
    return pl.pallas_call(
        body, name=name, grid=(M // tm, N // tn, nk), in_specs=in_specs, out_specs=out_specs, out_shape=out_shape,
        input_output_aliases=aliases,
        scratch_shapes=[pltpu.VMEM((tm, tn), F32)] if nk > 1 else [],
        compiler_params=_cparams("parallel", "parallel", "arbitrary"),
    )(*args)


def _relu2(v):
    r = jnp.maximum(v, 0.0)
    return r * r


def _epi_drelu2(acc, h):
    return acc * (2.0 * jnp.maximum(h, 0.0))


def _epi_add_residual(acc, du):
    return acc + DN_ALPHA * du


def _epi_residual_ln(acc, x, g, b):
    u = DN_ALPHA * x + acc
    mu = jnp.mean(u, axis=-1, keepdims=True)
    d = u - mu
    var = jnp.mean(d * d, axis=-1, keepdims=True)
    o = d * lax.rsqrt(var + LN_EPS) * g + b
    return o, o, u


def _ln_out():
    return (F32, ACT_DTYPE, F32)


def _ln_bwd(u, g, dout, name):
    T, D = u.shape
    tr = _tile(T, 512)

    def body(u_ref, g_ref, d_ref, du_ref, dub_ref, dg_ref, db_ref):
        i = pl.program_id(0)
        u = u_ref[...]
        mu = jnp.mean(u, axis=-1, keepdims=True)
        d = u - mu
        rstd = lax.rsqrt(jnp.mean(d * d, axis=-1, keepdims=True) + LN_EPS)
        xhat = d * rstd
        dout_v = d_ref[...]
        dxh = dout_v * g_ref[...]
        m1 = jnp.mean(dxh, axis=-1, keepdims=True)
        m2 = jnp.mean(dxh * xhat, axis=-1, keepdims=True)
        du = rstd * (dxh - m1 - xhat * m2)
        du_ref[...] = du
        dub_ref[...] = du.astype(ACT_DTYPE)
        pg = jnp.sum(dout_v * xhat, axis=0, keepdims=True)
        pb = jnp.sum(dout_v, axis=0, keepdims=True)

        @pl.when(i == 0)
        def _():
            dg_ref[...] = pg
            db_ref[...] = pb

        @pl.when(i > 0)
        def _():
            dg_ref[...] += pg
            db_ref[...] += pb

    row = pl.BlockSpec((tr, D), lambda i: (i, 0))
    vec = pl.BlockSpec((1, D), lambda i: (0, 0))
    return pl.pallas_call(
        body, name=name, grid=(T // tr,), in_specs=[row, vec, row], out_specs=[row, row, vec, vec],
        out_shape=[jax.ShapeDtypeStruct((T, D), F32), jax.ShapeDtypeStruct((T, D), ACT_DTYPE),
                   jax.ShapeDtypeStruct((1, D), F32), jax.ShapeDtypeStruct((1, D), F32)],
        compiler_params=_cparams("arbitrary"),
    )(u, g, dout)


def _loss_head(y, tgt, name):
    T, D = y.shape
    tr = _tile(T, 512)

    def body(y_ref, t_ref, dy_ref, l_ref):
        i = pl.program_id(0)
        e = y_ref[...] - t_ref[...]
        dy_ref[...] = e * (1.0 / D)
        part = jnp.sum(e * e, axis=(0, 1), keepdims=True) * (0.5 / D)

        @pl.when(i == 0)
        def _():
            l_ref[...] = part

        @pl.when(i > 0)
        def _():
            l_ref[...] += part

    row = pl.BlockSpec((tr, D), lambda i: (i, 0))
    one = pl.BlockSpec((1, 1), lambda i: (0, 0))
    return pl.pallas_call(
        body, name=name, grid=(T // tr,), in_specs=[row, row], out_specs=[row, one],
        out_shape=[jax.ShapeDtypeStruct((T, D), F32), jax.ShapeDtypeStruct((1, 1), F32)],
        compiler_params=_cparams("arbitrary"),
    )(y, tgt)


def _swap_half(x):
    n = x.shape[-1]
    lane = lax.broadcasted_iota(jnp.int32, x.shape, 1)
    first = (lane % A_HEAD_DIM) < (A_HEAD_DIM // 2)
    return jnp.where(first, pltpu.roll(x, n - A_HEAD_DIM // 2, axis=1), pltpu.roll(x, A_HEAD_DIM // 2, axis=1))


def _rope(x, cos, sin_signed):
    return x * cos + _swap_half(x) * sin_signed


def _rope_t(dy, cos, sin_signed):
    return dy * cos + _swap_half(dy * sin_signed)


def _rope_tables(T):
    half = A_HEAD_DIM // 2
    inv_freq = np.float32(ROPE_THETA) ** (-np.arange(half, dtype=np.float32) / np.float32(half))
    ang = np.arange(T, dtype=np.float32)[:, None] * inv_freq[None, :]
    cos = np.tile(np.cos(ang), (1, 4))
    sin = np.sin(ang)
    sin_signed = np.tile(np.concatenate([-sin, sin], axis=1), (1, 2))
    return jnp.asarray(cos, F32), jnp.asarray(sin_signed, F32)


def _band_mask(n):
    qi = lax.broadcasted_iota(jnp.int32, (WINDOW, 2 * WINDOW), 0)
    kj = lax.broadcasted_iota(jnp.int32, (WINDOW, 2 * WINDOW), 1)
    return (kj > qi) & (kj <= qi + WINDOW) & ((n > 0) | (kj >= WINDOW))


def _place(v, src_half, dst_half):
    lane = lax.broadcasted_iota(jnp.int32, v.shape, 1)
    if src_half != dst_half:
        v = pltpu.roll(v, A_HEAD_DIM, axis=1)
    keep = (lane >= A_HEAD_DIM) if dst_half else (lane < A_HEAD_DIM)
    return jnp.where(keep, v, 0.0)


def _attn_specs():
    kb, vb = HP_K // LANES, HP_V // LANES
    prev = lambda n: jnp.maximum(n - 1, 0)
    return dict(
        q=pl.BlockSpec((WINDOW, A_Q_W), lambda n: (n, 0)),
        kc=pl.BlockSpec((WINDOW, LANES), lambda n: (n, kb)),
        kp=pl.BlockSpec((WINDOW, LANES), lambda n: (prev(n), kb)),
        vc=pl.BlockSpec((WINDOW, LANES), lambda n: (n, vb)),
        vp=pl.BlockSpec((WINDOW, LANES), lambda n: (prev(n), vb)),
        tq=pl.BlockSpec((WINDOW, LANES), lambda n: (n, 0)),
        tp=pl.BlockSpec((WINDOW, LANES), lambda n: (prev(n), 0)),
        sink=pl.BlockSpec((A_Q_HEADS, LANES), lambda n: (0, 0)),
        row512=pl.BlockSpec((WINDOW, A_Q_W), lambda n: (n, 0)),
        row128=pl.BlockSpec((WINDOW, LANES), lambda n: (n, 0)),
        lse=pl.BlockSpec((WINDOW, A_Q_HEADS), lambda n: (n, 0)),
    )


def _attn_fwd(proj, cos, sin_s, sinks_b, name):
    T = proj.shape[0]
    sp = _attn_specs()

    def body(q_ref, kc_ref, kp_ref, vc_ref, vp_ref, cq_ref, sq_ref, cp_ref, sp_ref, sink_ref, o_ref, l_ref):
        n = pl.program_id(0)
        cq, sq = cq_ref[...], sq_ref[...]
        cq4, sq4 = jnp.tile(cq, (1, A_Q_W // LANES)), jnp.tile(sq, (1, A_Q_W // LANES))
        q = _rope(q_ref[...], cq4, sq4) * (A_HEAD_DIM ** -0.5)
        kc = _rope(kc_ref[...], cq, sq)
        kp = _rope(kp_ref[...], cp_ref[...], sp_ref[...])
        kk = jnp.concatenate([kp, kc], axis=0)
        vv = jnp.concatenate([vp_ref[...], vc_ref[...]], axis=0)
        mask = _band_mask(n)[None]
        lane = lax.broadcasted_iota(jnp.int32, (WINDOW, LANES), 1)
        lane8 = lax.broadcasted_iota(jnp.int32, (WINDOW, A_Q_HEADS), 1)
        qe = jnp.stack([_place(q[:, (hq // 2) * LANES:(hq // 2 + 1) * LANES], hq % 2, hq // 4) for hq in range(A_Q_HEADS)])
        kk8 = jnp.broadcast_to(kk[None], (A_Q_HEADS,) + kk.shape)
        vv8 = jnp.broadcast_to(vv[None], (A_Q_HEADS,) + vv.shape)
        sk = jnp.stack([sink_ref[hq:hq + 1, 0:1] for hq in range(A_Q_HEADS)])
        s = jnp.where(mask, _dot(qe, kk8, BNT), NEG_BIG)
        m = jnp.maximum(jnp.max(s, axis=-1, keepdims=True), sk)
        p = jnp.exp(s - m)
        den = jnp.sum(p, axis=-1, keepdims=True) + jnp.exp(sk - m)
        o = _dot(p * (1.0 / den), vv8, BNN)
        lse_h = m + jnp.log(den)
        outs = []
        lse = jnp.zeros((WINDOW, A_Q_HEADS), F32)
        for pb in range(A_Q_HEADS // 2):
            halves = [_place(o[2 * pb + e], pb // 2, e) for e in range(2)]
            outs.append(jnp.where(lane < A_HEAD_DIM, halves[0], halves[1]))
            for e in range(2):
                lse = jnp.where(lane8 == 2 * pb + e, lse_h[2 * pb + e], lse)
        o_ref[...] = jnp.concatenate(outs, axis=1)
        l_ref[...] = lse

    return pl.pallas_call(
        body, name=name, grid=(T // WINDOW,),
        in_specs=[sp["q"], sp["kc"], sp["kp"], sp["vc"], sp["vp"], sp["tq"], sp["tq"], sp["tp"], sp["tp"], sp["sink"]],
        out_specs=[sp["row512"], sp["lse"]],
        out_shape=[jax.ShapeDtypeStruct((T, A_Q_W), F32), jax.ShapeDtypeStruct((T, A_Q_HEADS), F32)],
        compiler_params=_cparams("parallel"),
    )(proj, proj, proj, proj, proj, cos, sin_s, cos, sin_s, sinks_b)


def _attn_bwd(proj, cos, sin_s, sinks_b, o, lse, dmix, name):
    T = proj.shape[0]
    sp = _attn_specs()

    def body(q_ref, kc_ref, kp_ref, vc_ref, vp_ref, cq_ref, sq_ref, cp_ref, sp_ref, sink_ref, o_ref, l_ref, do_ref,
             dq_ref, dkc_ref, dkp_ref, dvc_ref, dvp_ref, dsink_ref):
        n = pl.program_id(0)
        cq, sq = cq_ref[...], sq_ref[...]
        cp, sps = cp_ref[...], sp_ref[...]
        cq4, sq4 = jnp.tile(cq, (1, A_Q_W // LANES)), jnp.tile(sq, (1, A_Q_W // LANES))
        q = _rope(q_ref[...], cq4, sq4) * (A_HEAD_DIM ** -0.5)
        kc = _rope(kc_ref[...], cq, sq)
        kp = _rope(kp_ref[...], cp, sps)
        kk = jnp.concatenate([kp, kc], axis=0)
        vv = jnp.concatenate([vp_ref[...], vc_ref[...]], axis=0)
        mask = _band_mask(n)[None]
        lane = lax.broadcasted_iota(jnp.int32, (WINDOW, LANES), 1)
        do_all, o_all, l_all = do_ref[...], o_ref[...], l_ref[...]
        lane8 = lax.broadcasted_iota(jnp.int32, (WINDOW, A_Q_HEADS), 1)
        head8 = lax.broadcasted_iota(jnp.int32, (1, A_Q_HEADS), 1)
        prod = do_all * o_all
        qes, does, deltas, lhs = [], [], [], []
        for hq in range(A_Q_HEADS):
            pb, e, kvh = hq // 2, hq % 2, hq // 4
            blk = slice(pb * LANES, (pb + 1) * LANES)
            in_half = (lane >= A_HEAD_DIM) if e else (lane < A_HEAD_DIM)
            deltas.append(jnp.sum(jnp.where(in_half, prod[:, blk], 0.0), axis=-1, keepdims=True))
            qes.append(_place(q[:, blk], e, kvh))
            does.append(_place(do_all[:, blk], e, kvh))
            lhs.append(jnp.sum(jnp.where(lane8 == hq, l_all, 0.0), axis=-1, keepdims=True))
        qe, doe, delta, lh = jnp.stack(qes), jnp.stack(does), jnp.stack(deltas), jnp.stack(lhs)
        kk8 = jnp.broadcast_to(kk[None], (A_Q_HEADS,) + kk.shape)
        vv8 = jnp.broadcast_to(vv[None], (A_Q_HEADS,) + vv.shape)
        sk = jnp.stack([sink_ref[hq:hq + 1, 0:1] for hq in range(A_Q_HEADS)])
        s = _dot(qe, kk8, BNT)
        p = jnp.where(mask, jnp.exp(jnp.where(mask, s, NEG_BIG) - lh), 0.0)
        dvv = jnp.sum(_dot(p, doe, BTN), axis=0)
        ds = p * (_dot(doe, vv8, BNT) - delta)
        dkk = jnp.sum(_dot(ds, qe, BTN), axis=0)
        dqe = _dot(ds, kk8, BNN)
        dsink_h = -jnp.sum(jnp.exp(sk - lh) * delta, axis=(1, 2), keepdims=True)
        dqs = []
        dsk = jnp.zeros((1, A_Q_HEADS), F32)
        for pb in range(A_Q_HEADS // 2):
            halves = [_place(dqe[2 * pb + e], pb // 2, e) for e in range(2)]
            dqs.append(jnp.where(lane < A_HEAD_DIM, halves[0], halves[1]))
            for e in range(2):
                dsk = jnp.where(head8 == 2 * pb + e, dsink_h[2 * pb + e], dsk)
        dq = jnp.concatenate(dqs, axis=1) * (A_HEAD_DIM ** -0.5)
        dq_ref[...] = _rope_t(dq, cq4, sq4).astype(ACT_DTYPE)
        dkp_ref[...] = _rope_t(dkk[:WINDOW], cp, sps)
        dkc_ref[...] = _rope_t(dkk[WINDOW:], cq, sq)
        dvp_ref[...] = dvv[:WINDOW]
        dvc_ref[...] = dvv[WINDOW:]

        @pl.when(n == 0)
        def _():
            dsink_ref[...] = dsk

        @pl.when(n > 0)
        def _():
            dsink_ref[...] += dsk

    return pl.pallas_call(
        body, name=name, grid=(T // WINDOW,),
        in_specs=[sp["q"], sp["kc"], sp["kp"], sp["vc"], sp["vp"], sp["tq"], sp["tq"], sp["tp"], sp["tp"], sp["sink"],
                  sp["row512"], sp["lse"], sp["row512"]],
        out_specs=[sp["row512"], sp["row128"], sp["row128"], sp["row128"], sp["row128"],
                   pl.BlockSpec((1, A_Q_HEADS), lambda n: (0, 0))],
        out_shape=[jax.ShapeDtypeStruct((T, HYB_PAD), ACT_DTYPE)] + [jax.ShapeDtypeStruct((T, LANES), F32)] * 4
        + [jax.ShapeDtypeStruct((1, A_Q_HEADS), F32)],
        compiler_params=_cparams("arbitrary"),
    )(proj, proj, proj, proj, proj, cos, sin_s, cos, sin_s, sinks_b, o, lse, dmix)


def _shift_down(x, prev8, k):
    if k == 0:
        return x
    R, W = x.shape
    r = pltpu.roll(jnp.concatenate([prev8, x], axis=0).reshape(R // SUBLANES + 1, SUBLANES, W), k, axis=1)
    sub = lax.broadcasted_iota(jnp.int32, (R // SUBLANES, SUBLANES, W), 1)
    return jnp.where(sub < k, r[:-1], r[1:]).reshape(R, W)


def _shift_up(x, next8, k):
    if k == 0:
        return x
    R, W = x.shape
    r = pltpu.roll(jnp.concatenate([x, next8], axis=0).reshape(R // SUBLANES + 1, SUBLANES, W), SUBLANES - k, axis=1)
    sub = lax.broadcasted_iota(jnp.int32, (R // SUBLANES, SUBLANES, W), 1)
    return jnp.where(sub >= SUBLANES - k, r[1:], r[:-1]).reshape(R, W)


def _conv(x, prev8, w):
    w0, w1, w2, w3 = (w[j:j + 1] for j in range(CONV_K))
    x1 = _shift_down(x, prev8, 1)
    v = x * w1 + x1 * w0
    v_prev8 = prev8 * w1 + pltpu.roll(prev8, 1, axis=0) * w0
    return x * w3 + x1 * w2 + _shift_down(v, v_prev8, 2)


def _conv_bwd(x, w, dy, next8_dy):
    dx = dy * w[CONV_K - 1:CONV_K]
    dws = []
    for j in range(CONV_K - 1):
        up = _shift_up(dy, next8_dy, CONV_K - 1 - j)
        dx = dx + up * w[j:j + 1]
        dws.append(jnp.sum(up * x, axis=0, keepdims=True))
    dws.append(jnp.sum(dy * x, axis=0, keepdims=True))
    return dx, dws


def _dnconv_fwd(proj, conv_w, name):
    T = proj.shape[0]
    R = _tile(T, 512)
    cb0 = HP_QKVB // A_Q_W

    def body(x_ref, w_ref, o_ref, prev_ref):
        i = pl.program_id(1)

        @pl.when(i == 0)
        def _():
            prev_ref[...] = jnp.zeros_like(prev_ref)

        x = x_ref[...]
        o_ref[...] = _silu(_conv(x, prev_ref[...], w_ref[...]))
        prev_ref[...] = x[R - SUBLANES:]

    return pl.pallas_call(
        body, name=name, grid=(3, T // R),
        in_specs=[pl.BlockSpec((R, B_W), lambda j, i: (i, cb0 + j)), pl.BlockSpec((CONV_K, B_W), lambda j, i: (0, j))],
        out_specs=pl.BlockSpec((R, B_W), lambda j, i: (i, j)),
        out_shape=jax.ShapeDtypeStruct((T, 3 * B_W), F32),
        scratch_shapes=[pltpu.VMEM((SUBLANES, B_W), F32)],
        compiler_params=_cparams("parallel", "arbitrary"),
    )(proj, conv_w)


def _dnconv_bwd(proj, conv_w, dy, dproj, name):
    T = proj.shape[0]
    R = _tile(T, 512)
    nb = T // R
    cb0 = HP_QKVB // A_Q_W
    r8 = R // SUBLANES

    def body(x_ref, xp_ref, w_ref, dy_ref, dproj_in, dx_ref, dw_ref, next_ref):
        i = pl.program_id(1)
        blk = nb - 1 - i

        @pl.when(i == 0)
        def _():
            next_ref[...] = jnp.zeros_like(next_ref)

        x = x_ref[...]
        prev8 = jnp.where(blk > 0, xp_ref[...], 0.0)
        w = w_ref[...]
        dpre = dy_ref[...] * _dsilu(_conv(x, prev8, w))
        dx, dw = _conv_bwd(x, w, dpre, next_ref[...])
        dx_ref[...] = dx.astype(ACT_DTYPE)
        next_ref[...] = dpre[:SUBLANES]

        @pl.when(i == 0)
        def _():
            for j in range(CONV_K):
                dw_ref[j:j + 1, :] = dw[j]

        @pl.when(i > 0)
        def _():
            for j in range(CONV_K):
                dw_ref[j:j + 1, :] += dw[j]

    return pl.pallas_call(
        body, name=name, grid=(3, nb),
        in_specs=[pl.BlockSpec((R, B_W), lambda j, i: (nb - 1 - i, cb0 + j)),
                  pl.BlockSpec((SUBLANES, B_W), lambda j, i: (jnp.maximum((nb - 1 - i) * r8 - 1, 0), cb0 + j)),
                  pl.BlockSpec((CONV_K, B_W), lambda j, i: (0, j)),
                  pl.BlockSpec((R, B_W), lambda j, i: (nb - 1 - i, j)), ANY],
        out_specs=[pl.BlockSpec((R, B_W), lambda j, i: (nb - 1 - i, cb0 + j)),
                   pl.BlockSpec((CONV_K, B_W), lambda j, i: (0, j))],
        out_shape=[jax.ShapeDtypeStruct(dproj.shape, dproj.dtype), jax.ShapeDtypeStruct((CONV_K, 3 * B_W), F32)],
        input_output_aliases={4: 0},
        scratch_shapes=[pltpu.VMEM((SUBLANES, B_W), F32)],
        compiler_params=_cparams("parallel", "arbitrary"),
    )(proj, proj, conv_w, dy, dproj)


DK_SCALE = B_HEAD_DIM ** -0.5


BNN = (((2,), (1,)), ((0,), (0,)))
BNT = (((2,), (2,)), ((0,), (0,)))
BTN = (((1,), (1,)), ((0,), (0,)))


def _tri_inv(a):
    C = a.shape[-1]
    ri = lax.broadcasted_iota(jnp.int32, (C, C), 0)
    ci = lax.broadcasted_iota(jnp.int32, (C, C), 1)
    x = jnp.where(ri == ci, 1.0, 0.0)[None] - a
    p = _dotf(a, a, BNN)
    span = 2
    while span < C:
        dot = _dotf if span <= 4 else _dot
        x = x + dot(x, p, BNN)
        span *= 2
        if span < C:
            p = dot(p, p, BNN)
    return x


def _dn_chunk(qc, kc, v, gcol, grow, bcol, s0, tm=None):
    C = B_CHUNK
    ri = lax.broadcasted_iota(jnp.int32, (C, C), 0)
    ci = lax.broadcasted_iota(jnp.int32, (C, C), 1)
    incl, strict = (ri >= ci)[None], (ri > ci)[None]
    rq = lax.rsqrt(jnp.sum(qc * qc, axis=-1, keepdims=True) + NORM_EPS)
    rk = lax.rsqrt(jnp.sum(kc * kc, axis=-1, keepdims=True) + NORM_EPS)
    qn = qc * rq
    q = qn * DK_SCALE
    k = kc * rk
    gc_col = jnp.sum(jnp.where(incl, grow, 0.0), axis=2, keepdims=True)
    gc_row = jnp.sum(jnp.where((ri <= ci)[None], gcol, 0.0), axis=1, keepdims=True)
    gl = jnp.sum(gcol, axis=1, keepdims=True)
    dincl = jnp.where(incl, jnp.exp(jnp.where(incl, gc_col - gc_row, 0.0)), 0.0)
    dstrict = jnp.where(strict, dincl, 0.0)
    eg = jnp.exp(gc_col)
    ekt = jnp.exp(gl - gc_col)
    egl = jnp.exp(gl)
    kb = k * bcol
    vb = v * bcol
    kbg = kb * eg
    a = _dot(kb, k, BNT) * dstrict
    if tm is None:
        tm = _tri_inv(a)
    u = _dot(tm, vb, BNN)
    w = _dot(tm, kbg, BNN)
    vn = u - _dot(w, s0, BNN)
    qk = _dot(q, k, BNT) * dincl
    qg = q * eg
    kt = k * ekt
    o = _dot(qg, s0, BNN) + _dot(qk, vn, BNN)
    s1 = s0 * egl + _dot(kt, vn, BTN)
    return dict(rq=rq, rk=rk, qn=qn, q=q, k=k, dincl=dincl, dstrict=dstrict, eg=eg, ekt=ekt, egl=egl, kb=kb, vb=vb,
                kbg=kbg, a=a, tm=tm, w=w, vn=vn, qk=qk, qg=qg, kt=kt, o=o, s1=s1, ri=ri[None], ci=ci[None])


def _heads(ref):
    return jnp.stack([ref[:, h * B_HEAD_DIM:(h + 1) * B_HEAD_DIM] for h in range(B_HEADS)])


def _store_heads(ref, val):
    for h in range(B_HEADS):
        ref[:, h * B_HEAD_DIM:(h + 1) * B_HEAD_DIM] = val[h]


def _dn_specs(N, rev):
    ix = (lambda n: N - 1 - n) if rev else (lambda n: n)
    wide = lambda cb: pl.BlockSpec((B_CHUNK, B_W), lambda n: (ix(n), cb))
    return dict(
        q=wide(0), k=wide(1), v=wide(2), z=wide(HP_Z // B_W), dob=wide(A_Q_W // B_W), out=wide(0),
        nw=pl.BlockSpec((1, LANES), lambda n: (0, 0)),
        row=pl.BlockSpec((B_HEADS, None, 1, B_CHUNK), lambda n: (0, ix(n), 0, 0)),
        state=pl.BlockSpec((B_HEADS, None, B_HEAD_DIM, B_HEAD_DIM), lambda n: (0, ix(n), 0, 0)),
        inv=pl.BlockSpec((B_HEADS, None, B_CHUNK, B_CHUNK), lambda n: (0, ix(n), 0, 0)),
    )


def _to_col(row):
    C = row.shape[-1]
    eye = lax.broadcasted_iota(jnp.int32, (C, C), 0) == lax.broadcasted_iota(jnp.int32, (C, C), 1)
    return jnp.sum(jnp.where(eye[None], row, 0.0), axis=2, keepdims=True)


def _to_row(col):
    C = col.shape[1]
    eye = lax.broadcasted_iota(jnp.int32, (C, C), 0) == lax.broadcasted_iota(jnp.int32, (C, C), 1)
    return jnp.sum(jnp.where(eye[None], col, 0.0), axis=1, keepdims=True)


def _dn_fwd(qkvc, proj, norm_w, grow, brow, name):
    T = qkvc.shape[0]
    N = T // B_CHUNK
    sp = _dn_specs(N, False)

    def body(q_ref, k_ref, v_ref, z_ref, nw_ref, gr_ref, br_ref, o_ref, st_ref, tm_ref, s_ref):
        n = pl.program_id(0)

        @pl.when(n == 0)
        def _():
            s_ref[...] = jnp.zeros_like(s_ref)

        s0 = s_ref[...]
        st_ref[...] = s0
        grow_v = gr_ref[...]
        f = _dn_chunk(_heads(q_ref), _heads(k_ref), _heads(v_ref), _to_col(grow_v), grow_v, _to_col(br_ref[...]), s0)
        o = f["o"]
        r = lax.rsqrt(jnp.mean(o * o, axis=-1, keepdims=True) + NORM_EPS)
        _store_heads(o_ref, o * r * nw_ref[...][None] * _silu(_heads(z_ref)))
        s_ref[...] = f["s1"]
        tm_ref[...] = f["tm"]

    return pl.pallas_call(
        body, name=name, grid=(N,),
        in_specs=[sp["q"], sp["k"], sp["v"], sp["z"], sp["nw"], sp["row"], sp["row"]],
        out_specs=[sp["out"], sp["state"], sp["inv"]],
        out_shape=[jax.ShapeDtypeStruct((T, B_W), F32),
                   jax.ShapeDtypeStruct((B_HEADS, N, B_HEAD_DIM, B_HEAD_DIM), F32),
                   jax.ShapeDtypeStruct((B_HEADS, N, B_CHUNK, B_CHUNK), F32)],
        scratch_shapes=[pltpu.VMEM((B_HEADS, B_HEAD_DIM, B_HEAD_DIM), F32)],
        compiler_params=_cparams("arbitrary"),
    )(qkvc, qkvc, qkvc, proj, norm_w, grow, brow)


def _dn_bwd(qkvc, proj, norm_w, grow, brow, states, invs, dmix, dproj, name):
    T = qkvc.shape[0]
    N = T // B_CHUNK
    sp = _dn_specs(N, True)
    C = B_CHUNK

    def body(q_ref, k_ref, v_ref, z_ref, nw_ref, gr_ref, br_ref, st_ref, tm_ref, dob_ref, dproj_in,
             dqkv_ref, dz_ref, dg_ref, db_ref, dnw_ref, ds_ref):
        n = pl.program_id(0)
        dq_ref = dqkv_ref.at[:, 0:B_W]
        dk_ref = dqkv_ref.at[:, B_W:2 * B_W]
        dv_ref = dqkv_ref.at[:, 2 * B_W:3 * B_W]

        @pl.when(n == 0)
        def _():
            ds_ref[...] = jnp.zeros_like(ds_ref)
            dnw_ref[...] = jnp.zeros_like(dnw_ref)

        s0 = st_ref[...]
        ds1 = ds_ref[...]
        v, z, nw, bcol_v = _heads(v_ref), _heads(z_ref), nw_ref[...][None], _to_col(br_ref[...])
        grow_v = gr_ref[...]
        f = _dn_chunk(_heads(q_ref), _heads(k_ref), v, _to_col(grow_v), grow_v, bcol_v, s0, tm=tm_ref[...])
        o, q, k, qn = f["o"], f["q"], f["k"], f["qn"]
        eg, ekt, egl = f["eg"], f["ekt"], f["egl"]
        tm, w, vn, kb, vb, kbg = f["tm"], f["w"], f["vn"], f["kb"], f["vb"], f["kbg"]
        qg, kt, qk, a = f["qg"], f["kt"], f["qk"], f["a"]
        ri, ci = f["ri"], f["ci"]

        dob_v = _heads(dob_ref)
        r = lax.rsqrt(jnp.mean(o * o, axis=-1, keepdims=True) + NORM_EPS)
        sz = _silu(z)
        on = o * r
        dnw_ref[...] += jnp.sum(dob_v * sz * on, axis=1, keepdims=True)
        _store_heads(dz_ref, (dob_v * on * nw * _dsilu(z)).astype(ACT_DTYPE))
        d_on = dob_v * sz * nw
        do = r * (d_on - on * jnp.mean(d_on * on, axis=-1, keepdims=True))

        dvn = _dot(qk, do, BTN) + _dot(kt, ds1, BNN)
        dqk = _dot(do, vn, BNT)
        dqg = _dot(do, s0, BNT)
        ds_ref[...] = _dot(qg, do, BTN) + egl * ds1 - _dot(w, dvn, BTN)
        dgl = jnp.sum(s0 * ds1, axis=(1, 2), keepdims=True) * egl
        dkt = _dot(vn, ds1, BNT)
        dw = -_dot(dvn, s0, BNT)
        dq = dqg * eg
        dgc = jnp.sum(dqg * qg, axis=-1, keepdims=True)
        dk = dkt * ekt
        t_kt = jnp.sum(dkt * kt, axis=-1, keepdims=True)
        dgl = dgl + jnp.sum(t_kt, axis=1, keepdims=True)
        dgc = dgc - t_kt
        dqkr = dqk * f["dincl"]
        dq = dq + _dot(dqkr, k, BNN)
        dk = dk + _dot(dqkr, q, BTN)
        e_qk = dqk * qk
        dgc = dgc + jnp.sum(e_qk, axis=-1, keepdims=True)
        dgc_row = -jnp.sum(e_qk, axis=1, keepdims=True)
        dtm = _dot(dvn, vb, BNT) + _dot(dw, kbg, BNT)
        dvb = _dot(tm, dvn, BTN)
        dkbg = _dot(tm, dw, BTN)
        dkb = dkbg * eg
        dgc = dgc + jnp.sum(dkbg * kbg, axis=-1, keepdims=True)
        da = -_dotf(tm, _dotf(dtm, tm, BNT), BTN)
        dkk = da * f["dstrict"]
        e_a = da * a
        dgc = dgc + jnp.sum(e_a, axis=-1, keepdims=True)
        dgc_row = dgc_row - jnp.sum(e_a, axis=1, keepdims=True)
        dkb = dkb + _dot(dkk, k, BNN)
        dk = dk + _dot(dkk, kb, BTN)
        dk = dk + dkb * bcol_v
        db_ref[...] = _to_row(jnp.sum(dkb * k, axis=-1, keepdims=True) + jnp.sum(dvb * v, axis=-1, keepdims=True))
        _store_heads(dv_ref, dvb * bcol_v)
        dgc_row = dgc_row + jnp.sum(jnp.where(ri == ci, dgc, 0.0), axis=1, keepdims=True)
        dg_ref[...] = jnp.sum(jnp.where(ci <= ri, _to_col(dgc_row), 0.0), axis=1, keepdims=True) + dgl
        dqs = dq * DK_SCALE
        _store_heads(dq_ref, f["rq"] * (dqs - qn * jnp.sum(dqs * qn, axis=-1, keepdims=True)))
        _store_heads(dk_ref, f["rk"] * (dk - k * jnp.sum(dk * k, axis=-1, keepdims=True)))

    return pl.pallas_call(
        body, name=name, grid=(N,),
        in_specs=[sp["q"], sp["k"], sp["v"], sp["z"], sp["nw"], sp["row"], sp["row"], sp["state"], sp["inv"], sp["dob"],
                  ANY],
        out_specs=[pl.BlockSpec((C, 3 * B_W), lambda n: (N - 1 - n, 0)), sp["z"], sp["row"], sp["row"],
                   pl.BlockSpec((B_HEADS, 1, LANES), lambda n: (0, 0, 0))],
        out_shape=[jax.ShapeDtypeStruct((T, 3 * B_W), F32), jax.ShapeDtypeStruct(dproj.shape, dproj.dtype),
                   jax.ShapeDtypeStruct((B_HEADS, N, 1, C), F32), jax.ShapeDtypeStruct((B_HEADS, N, 1, C), F32),
                   jax.ShapeDtypeStruct((B_HEADS, 1, LANES), F32)],
        input_output_aliases={10: 1},
        scratch_shapes=[pltpu.VMEM((B_HEADS, B_HEAD_DIM, B_HEAD_DIM), F32)],
        compiler_params=_cparams("arbitrary"),
    )(qkvc, qkvc, qkvc, proj, norm_w, grow, brow, states, invs, dmix, dproj)


def _lru_gates(xc, wa_ref, wx_ref, ba, bx, sp):
    pre_r, pre_i = [], []
    for hb in range(LRU_BLOCKS):
        xb = xc[:, hb * LRU_BLOCK_W:(hb + 1) * LRU_BLOCK_W]
        pre_r.append(_dot(xb, wa_ref[hb]))
        pre_i.append(_dot(xb, wx_ref[hb]))
    r = _sigmoid(jnp.concatenate(pre_r, axis=1) + ba)
    i = _sigmoid(jnp.concatenate(pre_i, axis=1) + bx)
    la = -LRU_C * r * sp
    a = jnp.exp(la)
    th = jnp.tanh(la)
    s = jnp.sqrt(-2.0 * th / (1.0 - th))
    return r, i, a, s


def _scan_down(a, b, h_in):
    R, W = a.shape
    a = a.reshape(R // SUBLANES, SUBLANES, W)
    b = b.reshape(R // SUBLANES, SUBLANES, W)
    sub = lax.broadcasted_iota(jnp.int32, a.shape, 1)
    d = 1
    while d < SUBLANES:
        ok = sub >= d
        b = a * jnp.where(ok, pltpu.roll(b, d, axis=1), 0.0) + b
        a = a * jnp.where(ok, pltpu.roll(a, d, axis=1), 1.0)
        d *= 2
    out, last = [], h_in
    for g in range(R // SUBLANES):
        h = b[g] + a[g] * last
        out.append(h)
        last = h[SUBLANES - 1:SUBLANES]
    return jnp.concatenate(out, axis=0)


def _scan_up(a, b, l_in):
    R, W = a.shape
    a = a.reshape(R // SUBLANES, SUBLANES, W)
    b = b.reshape(R // SUBLANES, SUBLANES, W)
    sub = lax.broadcasted_iota(jnp.int32, a.shape, 1)
    d = 1
    while d < SUBLANES:
        ok = sub < SUBLANES - d
        b = a * jnp.where(ok, pltpu.roll(b, SUBLANES - d, axis=1), 0.0) + b
        a = a * jnp.where(ok, pltpu.roll(a, SUBLANES - d, axis=1), 1.0)
        d *= 2
    out, nxt = [], l_in
    for g in range(R // SUBLANES - 1, -1, -1):
        lam = b[g] + a[g] * nxt
        out.append(lam)
        nxt = lam[0:1]
    return jnp.concatenate(out[::-1], axis=0)


def _rglru_fwd(proj, conv_w, conv_b, wa, wx, ba, bx, sp, name):
    T = proj.shape[0]
    R = _tile(T, 256)
    W = D_MODEL

    def body(p_ref, cw_ref, cb_ref, wa_ref, wx_ref, ba_ref, bx_ref, sp_ref, hg_ref, h_ref, prev_ref, hc_ref):
        i = pl.program_id(0)

        @pl.when(i == 0)
        def _():
            prev_ref[...] = jnp.zeros_like(prev_ref)
            hc_ref[...] = jnp.zeros_like(hc_ref)

        xr = p_ref[:, :W]
        gate = p_ref[:, W:]
        xc = _conv(xr, prev_ref[...], cw_ref[...]) + cb_ref[...]
        prev_ref[...] = xr[R - SUBLANES:]
        r, ig, a, s = _lru_gates(xc, wa_ref, wx_ref, ba_ref[...], bx_ref[...], sp_ref[...])
        h = _scan_down(a, s * ig * xc, hc_ref[SUBLANES - 1:SUBLANES, :])
        h_ref[...] = h
        hg_ref[...] = (h * _gelu(gate)).astype(ACT_DTYPE)
        hc_ref[...] = h[R - SUBLANES:]

    vec = pl.BlockSpec((1, W), lambda i: (0, 0))
    wsp = pl.BlockSpec((LRU_BLOCKS, LRU_BLOCK_W, LRU_BLOCK_W), lambda i: (0, 0, 0))
    row = pl.BlockSpec((R, W), lambda i: (i, 0))
    return pl.pallas_call(
        body, name=name, grid=(T // R,),
        in_specs=[pl.BlockSpec((R, 2 * W), lambda i: (i, 0)), pl.BlockSpec((CONV_K, W), lambda i: (0, 0)),
                  vec, wsp, wsp, vec, vec, vec],
        out_specs=[row, row],
        out_shape=[jax.ShapeDtypeStruct((T, W), ACT_DTYPE), jax.ShapeDtypeStruct((T, W), F32)],
        scratch_shapes=[pltpu.VMEM((SUBLANES, W), F32), pltpu.VMEM((SUBLANES, W), F32)],
        compiler_params=_cparams("arbitrary"),
    )(proj, conv_w, conv_b, wa, wx, ba, bx, sp)


def _rglru_bwd(proj, conv_w, conv_b, wa, wx, ba, bx, sp, h, dhg, name):
    T = proj.shape[0]
    R = _tile(T, 256)
    nb = T // R
    r8 = R // SUBLANES
    W = D_MODEL

    def body(p_ref, pp_ref, cw_ref, cb_ref, wa_ref, wx_ref, ba_ref, bx_ref, sp_ref, h_ref, hp_ref, dhg_ref,
             dp_ref, dcw_ref, dcb_ref, dwa_ref, dwx_ref, dba_ref, dbx_ref, dsp_ref, lam_ref, nxt_ref):
        step = pl.program_id(0)
        blk = nb - 1 - step

        @pl.when(step == 0)
        def _():
            lam_ref[...] = jnp.zeros_like(lam_ref)
            nxt_ref[...] = jnp.zeros_like(nxt_ref)

        xr = p_ref[:, :W]
        gate = p_ref[:, W:]
        first = blk > 0
        prev8 = jnp.where(first, pp_ref[:, :W], 0.0)
        hprev8 = jnp.where(first, hp_ref[...], 0.0)
        cw = cw_ref[...]
        spv = sp_ref[...]
        xc = _conv(xr, prev8, cw) + cb_ref[...]
        r, ig, a, s = _lru_gates(xc, wa_ref, wx_ref, ba_ref[...], bx_ref[...], spv)
        hv = h_ref[...]
        dhg_v = dhg_ref[...]
        dgate = dhg_v * hv * _dgelu(gate)
        dh = dhg_v * _gelu(gate)
        row = lax.broadcasted_iota(jnp.int32, (R, W), 0)
        last = row == R - 1
        a_up = jnp.where(last, 0.0, pltpu.roll(a, R - 1, axis=0))
        lam = _scan_up(a_up, dh + jnp.where(last, lam_ref[0:1, :], 0.0), jnp.zeros((1, W), F32))
        lam_ref[...] = (a * lam)[:SUBLANES]
        h_dn = _shift_down(hv, hprev8, 1)
        da = lam * h_dn
        bx_in = ig * xc
        dsv = lam * bx_in
        dig = lam * s * xc
        dxc = lam * s * ig
        dla = da * a - dsv * (a * a) / s
        dr = dla * (-LRU_C) * spv
        dsp = jnp.sum(dla * (-LRU_C) * r, axis=0, keepdims=True)
        dpr = dr * r * (1.0 - r)
        dpi = dig * ig * (1.0 - ig)
        dxc_parts, dwa_parts, dwx_parts = [], [], []
        for hb in range(LRU_BLOCKS):
            sl = slice(hb * LRU_BLOCK_W, (hb + 1) * LRU_BLOCK_W)
            xb, gr, gi = xc[:, sl], dpr[:, sl], dpi[:, sl]
            dxc_parts.append(_dot(gr, wa_ref[hb], NT) + _dot(gi, wx_ref[hb], NT))
            dwa_parts.append(_dot(xb, gr, TN))
            dwx_parts.append(_dot(xb, gi, TN))
        dxc = dxc + jnp.concatenate(dxc_parts, axis=1)
        dxr, dcw = _conv_bwd(xr, cw, dxc, nxt_ref[...])
        nxt_ref[...] = dxc[:SUBLANES]
        dp_ref[:, :W] = dxr.astype(ACT_DTYPE)
        dp_ref[:, W:] = dgate.astype(ACT_DTYPE)
        dcb = jnp.sum(dxc, axis=0, keepdims=True)
        dba = jnp.sum(dpr, axis=0, keepdims=True)
        dbx = jnp.sum(dpi, axis=0, keepdims=True)

        @pl.when(step == 0)
        def _():
            for j in range(CONV_K):
                dcw_ref[j:j + 1, :] = dcw[j]
            dcb_ref[...] = dcb
            dba_ref[...] = dba
            dbx_ref[...] = dbx
            dsp_ref[...] = dsp
            for hb in range(LRU_BLOCKS):
                dwa_ref[hb] = dwa_parts[hb]
                dwx_ref[hb] = dwx_parts[hb]

        @pl.when(step > 0)
        def _():
            for j in range(CONV_K):
                dcw_ref[j:j + 1, :] += dcw[j]
            dcb_ref[...] += dcb
            dba_ref[...] += dba
            dbx_ref[...] += dbx
            dsp_ref[...] += dsp
            for hb in range(LRU_BLOCKS):
                dwa_ref[hb] += dwa_parts[hb]
                dwx_ref[hb] += dwx_parts[hb]

    rv = lambda i: nb - 1 - i
    before = lambda i: jnp.maximum((nb - 1 - i) * r8 - 1, 0)
    vec = pl.BlockSpec((1, W), lambda i: (0, 0))
    cws = pl.BlockSpec((CONV_K, W), lambda i: (0, 0))
    wsp = pl.BlockSpec((LRU_BLOCKS, LRU_BLOCK_W, LRU_BLOCK_W), lambda i: (0, 0, 0))
    row = pl.BlockSpec((R, W), lambda i: (rv(i), 0))
    wshape = jax.ShapeDtypeStruct((LRU_BLOCKS, LRU_BLOCK_W, LRU_BLOCK_W), F32)
    vshape = jax.ShapeDtypeStruct((1, W), F32)
    return pl.pallas_call(
        body, name=name, grid=(nb,),
        in_specs=[pl.BlockSpec((R, 2 * W), lambda i: (rv(i), 0)), pl.BlockSpec((SUBLANES, 2 * W), lambda i: (before(i), 0)),
                  cws, vec, wsp, wsp, vec, vec, vec, row, pl.BlockSpec((SUBLANES, W), lambda i: (before(i), 0)), row],
        out_specs=[pl.BlockSpec((R, 2 * W), lambda i: (rv(i), 0)), cws, vec, wsp, wsp, vec, vec, vec],
        out_shape=[jax.ShapeDtypeStruct((T, 2 * W), ACT_DTYPE), jax.ShapeDtypeStruct((CONV_K, W), F32), vshape,
                   wshape, wshape, vshape, vshape, vshape],
        scratch_shapes=[pltpu.VMEM((SUBLANES, W), F32), pltpu.VMEM((SUBLANES, W), F32)],
        compiler_params=_cparams("arbitrary"),
    )(proj, proj, conv_w, conv_b, wa, wx, ba, bx, sp, h, h, dhg)


MESH = pl.DeviceIdType.MESH
ANY = pl.BlockSpec(memory_space=pl.ANY)


def _position():
    x, y, c = lax.axis_index("x"), lax.axis_index("y"), lax.axis_index("c")
    other_chips = [(1 - x, y), (x, 1 - y), (1 - x, 1 - y)]
    return x, y, c, other_chips


def _all_gather_weights(shards, name):
    n = len(shards)

    def body(*refs):
        ins, outs = refs[:n], refs[n:2 * n]
        send_sems, recv_sems = refs[2 * n:]
        x, y, c, chips = _position()
        me = 2 * x + y
        sibling = (x, y, 1 - c)

        def rcopy(t, k, src, dst, to):
            return pltpu.make_async_remote_copy(src_ref=src, dst_ref=dst, send_sem=send_sems.at[t, k],
                                                recv_sem=recv_sems.at[t, k], device_id=to, device_id_type=MESH)

        started = []
        for t in range(n):
            for j, (cx, cy) in enumerate(chips):
                cp = rcopy(t, j, ins[t].at[c], outs[t].at[me, c], (cx, cy, c))
                cp.start()
                started.append(cp)
        for t in range(n):
            for j, (cx, cy) in enumerate(chips):
                blk = outs[t].at[2 * cx + cy, c]
                rcopy(t, j, blk, blk, (cx, cy, c)).wait_recv()
                cp = rcopy(t, 3 + j, blk, blk, sibling)
                cp.start()
                started.append(cp)
        for t in range(n):
            for j, (cx, cy) in enumerate(chips):
                blk = outs[t].at[2 * cx + cy, 1 - c]
                rcopy(t, 3 + j, blk, blk, sibling).wait_recv()
        for cp in started:
            cp.wait_send()

    return pl.pallas_call(
        body, name=name, in_specs=[ANY] * n, out_specs=[ANY] * n,
        out_shape=[jax.ShapeDtypeStruct((N_CHIPS,) + s.shape, s.dtype) for s in shards],
        scratch_shapes=[pltpu.SemaphoreType.DMA((n, 6)), pltpu.SemaphoreType.DMA((n, 6))],
    )(*shards)


HBM = pl.BlockSpec(memory_space=pltpu.HBM)
SEM = pl.BlockSpec(memory_space=pltpu.SEMAPHORE)
EFFECT = pltpu.SideEffectType.DATAFLOW_SIDE_EFFECTING


def _gather_start(own, land, after, name):
    n = len(own)

    def body(*refs):
        own_refs, land_refs = refs[:n], refs[n:2 * n]
        send_sems = refs[2 * n + 1:3 * n + 1]
        recv_sems = refs[3 * n + 1:4 * n + 1]
        x, y, c, chips = _position()
        me = 2 * x + y
        for t in range(n):
            for cx, cy in chips:
                pltpu.make_async_remote_copy(
                    src_ref=own_refs[t], dst_ref=land_refs[t].at[me], send_sem=send_sems[t], recv_sem=recv_sems[t],
                    device_id=(cx, cy, c), device_id_type=MESH).start()

    sems = (pltpu.SemaphoreType.DMA(()),) * (2 * n)
    thru = [pltpu.HBM(a.shape, a.dtype) for a in list(own) + list(land)]
    out = pl.pallas_call(
        body, name=name, out_shape=(*sems, *thru),
        in_specs=[HBM] * (2 * n) + [pl.BlockSpec(memory_space=pl.ANY)], out_specs=(SEM,) * (2 * n) + (HBM,) * (2 * n),
        input_output_aliases={i: 2 * n + i for i in range(2 * n)},
        compiler_params=pltpu.CompilerParams(has_side_effects=EFFECT),
    )(*[pltpu.with_memory_space_constraint(a, pltpu.HBM) for a in list(own) + list(land)], after)
    return list(out[:n]), list(out[n:2 * n]), list(out[2 * n:3 * n]), list(out[3 * n:])


def _gather_wait(send_sems, recv_sems, own, land, after, name):
    n = len(own)

    def body(*refs):
        land_refs = refs[n:2 * n]
        s_sems, r_sems = refs[2 * n:3 * n], refs[3 * n:4 * n]
        x, y, c, _ = _position()
        for t in range(n):
            three = land_refs[t].at[pl.ds(0, N_CHIPS - 1)]
            cp = pltpu.make_async_remote_copy(src_ref=three, dst_ref=three, send_sem=s_sems[t], recv_sem=r_sems[t],
                                              device_id=(x, y, c), device_id_type=MESH)
            cp.wait_send()
            cp.wait_recv()

    thru = [pltpu.HBM(a.shape, a.dtype) for a in list(own) + list(land)]
    out = pl.pallas_call(
        body, name=name, out_shape=tuple(thru),
        in_specs=[HBM] * (2 * n) + [SEM] * (2 * n) + [pl.BlockSpec(memory_space=pl.ANY)], out_specs=(HBM,) * (2 * n),
        input_output_aliases={i: i for i in range(2 * n)},
        compiler_params=pltpu.CompilerParams(has_side_effects=EFFECT),
    )(*own, *land, *send_sems, *recv_sems, after)
    return list(out[n:])


def _rs_to_sibling(grads, name):
    n = len(grads)

    def body(*refs):
        ins, outs = refs[:n], refs[n:2 * n]
        send_sems, recv_sems = refs[2 * n:]
        x, y, c, _ = _position()
        cps = [pltpu.make_async_remote_copy(src_ref=ins[t].at[:, 1 - c], dst_ref=outs[t], send_sem=send_sems.at[t],
                                            recv_sem=recv_sems.at[t], device_id=(x, y, 1 - c), device_id_type=MESH)
               for t in range(n)]
        for cp in cps:
            cp.start()
        for cp in cps:
            cp.wait()

    return pl.pallas_call(
        body, name=name, in_specs=[ANY] * n, out_specs=[ANY] * n,
        out_shape=[jax.ShapeDtypeStruct((N_CHIPS,) + g.shape[2:], g.dtype) for g in grads],
        scratch_shapes=[pltpu.SemaphoreType.DMA((n,)), pltpu.SemaphoreType.DMA((n,))],
    )(*grads)


def _rs_sibling_start(grads, recv, name):
    n = len(grads)

    def body(*refs):
        grad_refs, recv_refs = refs[:n], refs[n:2 * n]
        send_sems, recv_sems = refs[2 * n:3 * n], refs[3 * n:4 * n]
        token_ref = refs[-1]
        x, y, c, _ = _position()
        for t in range(n):
            pltpu.make_async_remote_copy(src_ref=grad_refs[t].at[:, 1 - c], dst_ref=recv_refs[t], send_sem=send_sems[t],
                                         recv_sem=recv_sems[t], device_id=(x, y, 1 - c), device_id_type=MESH).start()
        token_ref[...] = jnp.zeros_like(token_ref)

    sems = (pltpu.SemaphoreType.DMA(()),) * (2 * n)
    thru = [pltpu.HBM(a.shape, a.dtype) for a in list(grads) + list(recv)]
    out = pl.pallas_call(
        body, name=name, out_shape=(*sems, *thru, jax.ShapeDtypeStruct((SUBLANES, LANES), F32)),
        in_specs=[HBM] * (2 * n), out_specs=(SEM,) * (2 * n) + (HBM,) * (2 * n) + (pl.BlockSpec(memory_space=pltpu.VMEM),),
        input_output_aliases={i: 2 * n + i for i in range(2 * n)},
        compiler_params=pltpu.CompilerParams(has_side_effects=EFFECT),
    )(*[pltpu.with_memory_space_constraint(a, pltpu.HBM) for a in list(grads) + list(recv)])
    return out[:n], out[n:2 * n], list(out[2 * n:3 * n]), list(out[3 * n:4 * n]), out[-1]


def _rs_sibling_wait(send_sems, recv_sems, grads, recv, after, name):
    n = len(grads)

    def body(*refs):
        recv_refs = refs[n:2 * n]
        s_sems, r_sems = refs[2 * n:3 * n], refs[3 * n:4 * n]
        x, y, c, _ = _position()
        for t in range(n):
            cp = pltpu.make_async_remote_copy(src_ref=recv_refs[t], dst_ref=recv_refs[t], send_sem=s_sems[t],
                                              recv_sem=r_sems[t], device_id=(x, y, c), device_id_type=MESH)
            cp.wait_send()
            cp.wait_recv()

    thru = [pltpu.HBM(a.shape, a.dtype) for a in list(grads) + list(recv)]
    out = pl.pallas_call(
        body, name=name, out_shape=tuple(thru),
        in_specs=[HBM] * (2 * n) + [SEM] * (2 * n) + [pl.BlockSpec(memory_space=pl.ANY)], out_specs=(HBM,) * (2 * n),
        input_output_aliases={i: i for i in range(2 * n)},
        compiler_params=pltpu.CompilerParams(has_side_effects=EFFECT),
    )(*grads, *recv, *send_sems, *recv_sems, after)
    return list(out[:n]), list(out[n:])


def _rs_across_start(parts, recv, name):
    n = len(parts)

    def body(*refs):
        part_refs, recv_refs = refs[:n], refs[n:2 * n]
        send_sems, recv_sems = refs[2 * n:3 * n], refs[3 * n:4 * n]
        token_ref = refs[-1]
        x, y, c, chips = _position()
        me = 2 * x + y
        for t in range(n):
            for cx, cy in chips:
                pltpu.make_async_remote_copy(src_ref=part_refs[t].at[2 * cx + cy], dst_ref=recv_refs[t].at[me],
                                             send_sem=send_sems[t], recv_sem=recv_sems[t], device_id=(cx, cy, c),
                                             device_id_type=MESH).start()
        token_ref[...] = jnp.zeros_like(token_ref)

    sems = (pltpu.SemaphoreType.DMA(()),) * (2 * n)
    thru = [pltpu.HBM(a.shape, a.dtype) for a in list(parts) + list(recv)]
    out = pl.pallas_call(
        body, name=name, out_shape=(*sems, *thru, jax.ShapeDtypeStruct((SUBLANES, LANES), F32)),
        in_specs=[HBM] * (2 * n), out_specs=(SEM,) * (2 * n) + (HBM,) * (2 * n) + (pl.BlockSpec(memory_space=pltpu.VMEM),),
        input_output_aliases={i: 2 * n + i for i in range(2 * n)},
        compiler_params=pltpu.CompilerParams(has_side_effects=EFFECT),
    )(*[pltpu.with_memory_space_constraint(a, pltpu.HBM) for a in list(parts) + list(recv)])
    return out[:n], out[n:2 * n], list(out[2 * n:3 * n]), list(out[3 * n:4 * n]), out[-1]


def _rs_across_wait(send_sems, recv_sems, parts, recv, after, name):
    n = len(parts)

    def body(*refs):
        recv_refs = refs[n:2 * n]
        s_sems, r_sems = refs[2 * n:3 * n], refs[3 * n:4 * n]
        x, y, c, _ = _position()
        for t in range(n):
            three = recv_refs[t].at[pl.ds(0, N_CHIPS - 1)]
            cp = pltpu.make_async_remote_copy(src_ref=three, dst_ref=three, send_sem=s_sems[t], recv_sem=r_sems[t],
                                              device_id=(x, y, c), device_id_type=MESH)
            cp.wait_send()
            cp.wait_recv()

    thru = [pltpu.HBM(a.shape, a.dtype) for a in list(parts) + list(recv)]
    out = pl.pallas_call(
        body, name=name, out_shape=tuple(thru),
        in_specs=[HBM] * (2 * n) + [SEM] * (2 * n) + [pl.BlockSpec(memory_space=pl.ANY)], out_specs=(HBM,) * (2 * n),
        input_output_aliases={i: i for i in range(2 * n)},
        compiler_params=pltpu.CompilerParams(has_side_effects=EFFECT),
    )(*parts, *recv, *send_sems, *recv_sems, after)
    return list(out[:n]), list(out[n:])


def _rs_join_halves(halves, name):
    n = len(halves)

    def body(*refs):
        ins, outs = refs[:n], refs[n:2 * n]
        send_sems, recv_sems = refs[2 * n:]
        x, y, c, _ = _position()
        cps = [pltpu.make_async_remote_copy(src_ref=ins[t].at[c], dst_ref=outs[t].at[c], send_sem=send_sems.at[t],
                                            recv_sem=recv_sems.at[t], device_id=(x, y, 1 - c), device_id_type=MESH)
               for t in range(n)]
        for cp in cps:
            cp.start()
        for t in range(n):
            blk = outs[t].at[1 - c]
            pltpu.make_async_remote_copy(src_ref=blk, dst_ref=blk, send_sem=send_sems.at[t], recv_sem=recv_sems.at[t],
                                         device_id=(x, y, 1 - c), device_id_type=MESH).wait_recv()
        for cp in cps:
            cp.wait_send()

    return pl.pallas_call(
        body, name=name, in_specs=[ANY] * n, out_specs=[ANY] * n,
        out_shape=[jax.ShapeDtypeStruct(h.shape, h.dtype) for h in halves],
        input_output_aliases={t: t for t in range(n)},
        scratch_shapes=[pltpu.SemaphoreType.DMA((n,)), pltpu.SemaphoreType.DMA((n,))],
    )(*halves)


def _all_gather_small(block, name):
    m_per, n = block.shape

    def body(x_ref, out_ref, send_sems, recv_sems, local_sem):
        x, y, c, chips = _position()
        me, sibling = (x, y, c), (x, y, 1 - c)

        def rows(px, py, pc):
            return out_ref.at[pl.ds((4 * px + 2 * py + pc) * m_per, m_per), :]

        def copy(k, blk, to, src=None):
            return pltpu.make_async_remote_copy(
                src_ref=rows(*blk) if src is None else src, dst_ref=rows(*blk), send_sem=send_sems.at[k],
                recv_sem=recv_sems.at[k], device_id=to, device_id_type=MESH)

        mine = pltpu.make_async_copy(x_ref, rows(*me), local_sem)
        mine.start()
        first = [copy(0, me, sibling, src=x_ref)]
        first += [copy(1 + j, me, (*chip, c), src=x_ref) for j, chip in enumerate(chips)]
        for cp in first:
            cp.start()
        passed = [copy(4 + j, (*chip, c), sibling) for j, chip in enumerate(chips)]
        for j, chip in enumerate(chips):
            copy(1 + j, (*chip, c), me).wait_recv()
            passed[j].start()
        copy(0, sibling, me).wait_recv()
        for j, chip in enumerate(chips):
            copy(4 + j, (*chip, 1 - c), me).wait_recv()
        for cp in first + passed:
            cp.wait_send()
        mine.wait()

    return pl.pallas_call(
        body, name=name, out_shape=jax.ShapeDtypeStruct((N_DEV * m_per, n), block.dtype),
        in_specs=[pl.BlockSpec(memory_space=pltpu.VMEM)], out_specs=pl.BlockSpec(memory_space=pltpu.VMEM),
        scratch_shapes=[pltpu.SemaphoreType.DMA((7,)), pltpu.SemaphoreType.DMA((7,)), pltpu.SemaphoreType.DMA],
    )(block)


def _row_tile(R, n):
    budget = 1 << 19
    if R * n <= budget or R % SUBLANES:
        return R
    t = R
    while t * n > budget and t % (2 * SUBLANES) == 0:
        t //= 2
    return t


def _pair_sum(g, recv, c_arr, name):
    _, _, R, n = g.shape
    tr = _row_tile(R, n)

    def body(c_ref, g_ref, r_ref, o_ref):
        o_ref[...] = (g_ref[...] + r_ref[...]).astype(ICI_DTYPE)

    grid_spec = pltpu.PrefetchScalarGridSpec(
        num_scalar_prefetch=1, grid=(N_CHIPS, R // tr),
        in_specs=[pl.BlockSpec((None, None, tr, n), lambda p, i, c: (p, c[0], i, 0)),
                  pl.BlockSpec((None, tr, n), lambda p, i, c: (p, i, 0))],
        out_specs=pl.BlockSpec((None, tr, n), lambda p, i, c: (p, i, 0)))
    return pl.pallas_call(
        body, name=name, grid_spec=grid_spec, out_shape=jax.ShapeDtypeStruct(recv.shape, ICI_DTYPE),
        compiler_params=_cparams("parallel", "parallel"),
    )(c_arr, g, recv)


def _chip_sum(recv, own, where, name):
    _, R, n = recv.shape
    tr = _row_tile(R, n)

    def body(w_ref, r0, r1, r2, r3, own_ref, o_ref):
        me = w_ref[0]
        terms = [jnp.where(me == k, own_ref[...], r[...]).astype(F32) for k, r in enumerate((r0, r1, r2, r3))]
        o_ref[...] = ((terms[0] + terms[1]) + terms[2]) + terms[3]

    def slot(k):
        return pl.BlockSpec((None, tr, n), lambda i, w: (w[1 + k], i, 0))

    grid_spec = pltpu.PrefetchScalarGridSpec(
        num_scalar_prefetch=1, grid=(R // tr,),
        in_specs=[slot(0), slot(1), slot(2), slot(3), pl.BlockSpec((None, tr, n), lambda i, w: (w[0], i, 0))],
        out_specs=pl.BlockSpec((None, tr, n), lambda i, w: (w[5], i, 0)))
    return pl.pallas_call(
        body, name=name, grid_spec=grid_spec, out_shape=jax.ShapeDtypeStruct((2, R, n), F32),
        compiler_params=_cparams("parallel"),
    )(where, recv, recv, recv, recv, own)


ADAM_C1 = 1.0 / (1.0 - ADAM_B1 ** ADAM_STEP)
ADAM_C2 = 1.0 / (1.0 - ADAM_B2 ** ADAM_STEP)


def _adamw_math(w, g, m, v):
    m = ADAM_B1 * m + (1.0 - ADAM_B1) * g
    v = ADAM_B2 * v + (1.0 - ADAM_B2) * (g * g)
    delta = -ADAM_LR * ((m * ADAM_C1) / (jnp.sqrt(v * ADAM_C2) + ADAM_EPS) + ADAM_WD * w)
    return delta, m, v


def _adamw(w, g, m, v, name):
    R, n = w.shape
    tr = _row_tile(R, n)

    def body(w_ref, g_ref, m_ref, v_ref, d_ref, nm_ref, nv_ref):
        d_ref[...], nm_ref[...], nv_ref[...] = _adamw_math(w_ref[...], g_ref[...], m_ref[...], v_ref[...])

    spec = pl.BlockSpec((tr, n), lambda i: (i, 0))
    return pl.pallas_call(
        body, name=name, grid=(R // tr,), in_specs=[spec] * 4, out_specs=[spec] * 3,
        out_shape=[jax.ShapeDtypeStruct((R, n), F32)] * 3, compiler_params=_cparams("parallel"),
    )(w, g, m, v)


def _adamw_half(w, g, m, v, half, prev, name):
    R, n = w.shape
    tr = _row_tile(R // 2, n)
    off = half * (R // 2 // tr)

    def body(*refs):
        w_ref, g_ref, m_ref, v_ref = refs[:4]
        go_ref, d_ref, nm_ref, nv_ref = refs[-4:]
        gv = g_ref[...]
        go_ref[...] = gv
        d_ref[...], nm_ref[...], nv_ref[...] = _adamw_math(w_ref[...], gv, m_ref[...], v_ref[...])

    rows = pl.BlockSpec((tr, n), lambda i: (i + off, 0))
    carried = [] if prev is None else list(prev)
    return pl.pallas_call(
        body, name=name, grid=(R // 2 // tr,),
        in_specs=[rows, pl.BlockSpec((tr, n), lambda i: (i, 0)), rows, rows] + [ANY] * len(carried),
        out_specs=[rows] * 4, out_shape=[jax.ShapeDtypeStruct((R, n), F32)] * 4,
        input_output_aliases={4 + i: i for i in range(len(carried))},
        compiler_params=_cparams("parallel"),
    )(w, g, m, v, *carried)


def _adamw_small(w, gall, m, v, name):
    M, n = w.shape

    def body(w_ref, g_ref, m_ref, v_ref, gs_ref, d_ref, nm_ref, nv_ref):
        g = g_ref[0:M, :]
        for d in range(1, N_DEV):
            g = g + g_ref[d * M:(d + 1) * M, :]
        gs_ref[...] = g
        d_ref[...], nm_ref[...], nv_ref[...] = _adamw_math(w_ref[...], g, m_ref[...], v_ref[...])

    return pl.pallas_call(
        body, name=name, out_shape=[jax.ShapeDtypeStruct((M, n), F32)] * 4,
    )(w, gall, m, v)


SMALL_ROWS = 24
MID_ROWS = 4


def _pack_small(ln1_g, ln1_b, ln2_g, ln2_b, norm_w, sinks, a_log, dt_bias):
    mixed = jnp.concatenate([norm_w.reshape(-1), sinks.reshape(-1), a_log.reshape(-1), dt_bias.reshape(-1)])
    mixed = jnp.pad(mixed, (0, D_MODEL - mixed.shape[0]))[None]
    pad = jnp.zeros((SMALL_ROWS - 4 * DEPTH - 1, D_MODEL), F32)
    return jnp.concatenate([ln1_g, ln1_b, ln2_g, ln2_b, mixed, pad], axis=0)


def _unpack_small(p):
    mixed = p[4 * DEPTH]
    return (p[0:4], p[4:8], p[8:12], p[12:16], mixed[0:256].reshape(2, 128), mixed[256:272].reshape(2, 8),
            mixed[272:280].reshape(2, 4), mixed[280:288].reshape(2, 4))


def _pack_mid(conv_w, rconv_w, rconv_b, b_a, b_x, lam):
    lead = conv_w.shape[0]
    flat = jnp.concatenate([conv_w.reshape(lead, -1), rconv_w.reshape(lead, -1), rconv_b, b_a, b_x, lam], axis=1)
    return jnp.pad(flat, ((0, 0), (0, MID_ROWS * D_MODEL - flat.shape[1]))).reshape(lead, MID_ROWS, D_MODEL)


def _unpack_mid(p):
    lead = p.shape[:-2]
    f = p.reshape(lead + (MID_ROWS * D_MODEL,))
    return (f[..., 0:1536].reshape(lead + (4, 384)), f[..., 1536:2560].reshape(lead + (4, 256)),
            f[..., 2560:2816], f[..., 2816:3072], f[..., 3072:3328], f[..., 3328:3584])


def _cols_from_chips(g):
    p, L, R, n = g.shape
    return g.transpose(1, 2, 0, 3).reshape(L, R, p * n)


def _rows_from_chips(g):
    p, L, R, n = g.shape
    return g.transpose(1, 0, 2, 3).reshape(L, p * R, n)


def _cols_to_chips(g):
    L, R, n4 = g.shape
    return g.reshape(L, R, N_CHIPS, n4 // N_CHIPS).transpose(2, 0, 1, 3)


def _rows_to_chips(g):
    L, R4, n = g.shape
    return g.reshape(L, N_CHIPS, R4 // N_CHIPS, n).transpose(1, 0, 2, 3)


def _halves(a):
    return a.reshape(2, -1, a.shape[-1])


def _pad_hyb_cols(w):
    z = jnp.zeros(w.shape[:-1] + (HYB_PAD - HP_BG - 2 * B_HEADS,), w.dtype)
    return jnp.concatenate([w[..., 0:512], w[..., 768:2304], w[..., 2304:2816], w[..., 512:768], w[..., 2816:2824], z], axis=-1)


def _unpad_hyb_cols(w):
    return jnp.concatenate([w[..., 0:512], w[..., 2560:2816], w[..., 512:2048], w[..., 2048:2560], w[..., 2816:2824]], axis=-1)


def _hybrid_fwd(x, W, j, tables, sfx, out_ready, ln):
    cos, sin_s = tables
    T = x.shape[0]
    N = T // B_CHUNK
    proj = _matmul(x, W["hyb_w_in"][j], "nn", "hyb_in" + sfx)
    sinks_b = jnp.broadcast_to(W["hyb_sinks"][j][:, None], (A_Q_HEADS, LANES))
    o_a, lse = _attn_fwd(proj, cos, sin_s, sinks_b, "attn_fwd" + sfx)
    qkvc = _dnconv_fwd(proj, W["hyb_conv_w"][j], "dnconv_fwd" + sfx)
    bg = proj[:, HP_BG:HP_BG + 2 * B_HEADS]
    beta = jax.nn.sigmoid(bg[:, :B_HEADS])
    pre = bg[:, B_HEADS:] + W["hyb_dt_bias"][j][None]
    g = -jnp.exp(W["hyb_a_log"][j])[None] * jax.nn.softplus(pre)
    grow = g.T.reshape(B_HEADS, N, 1, B_CHUNK)
    brow = beta.T.reshape(B_HEADS, N, 1, B_CHUNK)
    nw = W["hyb_norm_w"][j][None]
    o_b, states, invs = _dn_fwd(qkvc, proj, nw, grow, brow, "dn_fwd" + sfx)
    mix = jnp.concatenate([o_a, o_b], axis=1).astype(ACT_DTYPE)
    out_ready(mix)
    out = _matmul(mix, W["hyb_w_out"][j], "nn", "hyb_out" + sfx, epi=_epi_residual_ln, extra=ln, out_dtype=_ln_out(),
                  tm=512)
    res = dict(proj=proj, o_a=o_a, lse=lse, qkvc=qkvc, beta=beta, pre=pre, g=g, grow=grow, brow=brow,
               states=states, invs=invs, mix=mix, sinks_b=sinks_b, nw=nw)
    return out, res


def _hybrid_bwd(x, du, dub, W, j, res, tables, sfx):
    cos, sin_s = tables
    T = x.shape[0]
    proj = res["proj"]
    d_wout = _matmul(res["mix"], dub, "tn", "hyb_out_dw" + sfx)
    dmix = _matmul(dub, W["hyb_w_out"][j], "nt", "hyb_out_dx" + sfx)
    dproj, dkc, dkp, dvc, dvp, dsink = _attn_bwd(proj, cos, sin_s, res["sinks_b"], res["o_a"], res["lse"], dmix,
                                                  "attn_bwd" + sfx)
    zpad = jnp.zeros((WINDOW, LANES), F32)
    dk = dkc + jnp.concatenate([dkp[WINDOW:], zpad], axis=0)
    dv = dvc + jnp.concatenate([dvp[WINDOW:], zpad], axis=0)
    dqkvc, dproj, dg4, dbeta4, dnw = _dn_bwd(res["qkvc"], proj, res["nw"], res["grow"], res["brow"], res["states"],
                                             res["invs"], dmix, dproj, "dn_bwd" + sfx)
    dproj, dconv = _dnconv_bwd(proj, W["hyb_conv_w"][j], dqkvc, dproj, "dnconv_bwd" + sfx)
    dg = dg4.reshape(B_HEADS, T).T
    dbeta = dbeta4.reshape(B_HEADS, T).T
    beta = res["beta"]
    dbeta_logit = dbeta * beta * (1.0 - beta)
    da_logit = dg * (-jnp.exp(W["hyb_a_log"][j]))[None] * jax.nn.sigmoid(res["pre"])
    d_dt_bias = jnp.sum(da_logit, axis=0)
    d_a_log = jnp.sum(dg * res["g"], axis=0)
    zcols = jnp.zeros((T, HYB_PAD - HP_BG - 2 * B_HEADS), F32)
    tail = jnp.concatenate([dk, dv, dbeta_logit, da_logit, zcols], axis=1).astype(ACT_DTYPE)
    dproj = lax.dynamic_update_slice(dproj, tail, (0, HP_K))
    d_win = _matmul(x, dproj, "tn", "hyb_in_dw" + sfx)
    dx = _matmul(dproj, W["hyb_w_in"][j], "nt", "hyb_in_dx" + sfx, epi=_epi_add_residual, extra=du)
    grads = dict(hyb_w_in=d_win, hyb_w_out=d_wout, hyb_sinks=dsink[0], hyb_conv_w=dconv, hyb_a_log=d_a_log,
                 hyb_dt_bias=d_dt_bias, hyb_norm_w=jnp.sum(dnw[:, 0, :], axis=0))
    return dx, grads


def _rec_fwd(x, W, j, sfx, ln):
    proj = _matmul(x, W["rec_w_in"][j], "nn", "rec_in" + sfx)
    sp = jax.nn.softplus(-W["rec_lambda"][j])[None]
    hg, h = _rglru_fwd(proj, W["rec_conv_w"][j], W["rec_conv_b"][j][None], W["rec_w_a"][j], W["rec_w_x"][j],
                       W["rec_b_a"][j][None], W["rec_b_x"][j][None], sp, "rglru_fwd" + sfx)
    out = _matmul(hg, W["rec_w_out"][j], "nn", "rec_out" + sfx, epi=_epi_residual_ln, extra=ln, out_dtype=_ln_out(),
                  tm=512)
    return out, dict(proj=proj, hg=hg, h=h, sp=sp)


def _rec_bwd(x, du, dub, W, j, res, sfx):
    d_wout = _matmul(res["hg"], dub, "tn", "rec_out_dw" + sfx)
    dhg = _matmul(dub, W["rec_w_out"][j], "nt", "rec_out_dx" + sfx)
    dproj, dcw, dcb, dwa, dwx, dba, dbx, dsp = _rglru_bwd(
        res["proj"], W["rec_conv_w"][j], W["rec_conv_b"][j][None], W["rec_w_a"][j], W["rec_w_x"][j],
        W["rec_b_a"][j][None], W["rec_b_x"][j][None], res["sp"], res["h"], dhg, "rglru_bwd" + sfx)
    d_lam = dsp[0] * (-jax.nn.sigmoid(-W["rec_lambda"][j]))
    d_win = _matmul(x, dproj, "tn", "rec_in_dw" + sfx)
    dx = _matmul(dproj, W["rec_w_in"][j], "nt", "rec_in_dx" + sfx, epi=_epi_add_residual, extra=du)
    grads = dict(rec_w_in=d_win, rec_w_out=d_wout, rec_conv_w=dcw, rec_conv_b=dcb[0], rec_w_a=dwa, rec_w_x=dwx,
                 rec_b_a=dba[0], rec_b_x=dbx[0], rec_lambda=d_lam)
    return dx, grads


def _local_step(x, tgt, W, mlp_w, mixer_ready, on_group):
    T = x.shape[0]
    tables = _rope_tables(T)
    acts = []
    xb = x.astype(ACT_DTYPE)
    for layer in range(DEPTH):
        j, sfx = layer // 2, ""
        mixer_ready(layer, xb)
        ln1 = (x, W["ln1_g"][layer][None], W["ln1_b"][layer][None])
        if layer % 2 == 0:
            (x1, x1b, u1), res = _hybrid_fwd(xb, W, j, tables, sfx,
                                             functools.partial(mixer_ready, layer, out_projection=True), ln1)
        else:
            (x1, x1b, u1), res = _rec_fwd(xb, W, j, sfx, ln1)
        w1, w2, wl = mlp_w(layer, x1b)
        h1 = _matmul(x1b, w1, "nn", "mlp_up", out_dtype=ACT_DTYPE, b_chips=("j", wl))
        x2, x2b, u2 = _matmul(h1, w2, "nn", "mlp_down", a_fn=_relu2, b_chips=("k", wl), epi=_epi_residual_ln,
                              extra=(x1, W["ln2_g"][layer][None], W["ln2_b"][layer][None]), out_dtype=_ln_out())
        acts.append(dict(xb=xb, res=res, u1=u1, x1b=x1b, h1=h1, u2=u2))
        x, xb = x2, x2b
    dx, loss = _loss_head(x, tgt, "loss_head")
    per_layer = [None] * DEPTH
    d_w1 = [lax.empty((N_CHIPS, 2, D_MODEL, D_FF // N_CHIPS), F32) for _ in range(DEPTH // 2)]
    d_w2 = [lax.empty((N_CHIPS, 2, D_FF // N_CHIPS, D_MODEL), F32) for _ in range(DEPTH // 2)]
    token = None
    for layer in reversed(range(DEPTH)):
        j, a = layer // 2, acts[layer]
        ln2_g = W["ln2_g"][layer][None]
        if token is not None:
            ln2_g = ln2_g + token
        du2, du2b, dg2, db2 = _ln_bwd(a["u2"], ln2_g, dx, "ln_bwd")
        w1, w2, wl = mlp_w(layer, du2b)
        d_w2[j] = _matmul(a["h1"], du2b, "tn", "mlp_down_dw", a_fn=_relu2, out_chips=("i", layer % 2, d_w2[j]))
        dh1 = _matmul(du2b, w2, "nt", "mlp_down_dx", epi=_epi_drelu2, extra=a["h1"], out_dtype=ACT_DTYPE,
                      b_chips=("j", wl))
        d_w1[j] = _matmul(a["x1b"], dh1, "tn", "mlp_up_dw", out_chips=("j", layer % 2, d_w1[j]))
        dx1 = _matmul(dh1, w1, "nt", "mlp_up_dx", epi=_epi_add_residual, extra=du2, b_chips=("k", wl))
        ln1_g = W["ln1_g"][layer][None]
        token = on_group(None, layer, dx1, None, None) if layer == 1 else None
        if token is not None:
            ln1_g = ln1_g + token
        du1, du1b, dg1, db1 = _ln_bwd(a["u1"], ln1_g, dx1, "ln_bwd")
        if layer % 2 == 0:
            dx, g = _hybrid_bwd(a["xb"], du1, du1b, W, j, a["res"], tables, "")
        else:
            dx, g = _rec_bwd(a["xb"], du1, du1b, W, j, a["res"], "")
        g.update(ln1_g=dg1[0], ln1_b=db1[0], ln2_g=dg2[0], ln2_b=db2[0])
        per_layer[layer] = g
        if layer % 2 == 0:
            token = on_group(j, per_layer[layer], per_layer[layer + 1], d_w1[j], d_w2[j])
    grads = {}
    for name in ("ln1_g", "ln1_b", "ln2_g", "ln2_b"):
        grads[name] = jnp.stack([per_layer[l][name] for l in range(DEPTH)])
    for name in ("hyb_norm_w", "hyb_sinks", "hyb_a_log", "hyb_dt_bias"):
        grads[name] = jnp.stack([per_layer[l][name] for l in (0, 2)])
    return loss, dx, grads


BIG = ("hyb_w_in", "hyb_w_out", "rec_w_in", "rec_w_out", "mlp_w1", "mlp_w2", "rec_w_a", "rec_w_x")
COL_SHARDED = ("hyb_w_in", "rec_w_in", "mlp_w1")
CHIP_MAJOR = ("mlp_w1", "mlp_w2")
MID = ("hyb_conv_w", "rec_conv_w", "rec_conv_b", "rec_b_a", "rec_b_x", "rec_lambda")
SMALL = ("ln1_g", "ln1_b", "ln2_g", "ln2_b", "hyb_norm_w", "hyb_sinks", "hyb_a_log", "hyb_dt_bias")
WEIGHTS = ("hyb_w_in", "hyb_sinks", "hyb_conv_w", "hyb_a_log", "hyb_dt_bias", "hyb_norm_w", "hyb_w_out", "rec_w_in",
           "rec_conv_w", "rec_conv_b", "rec_w_a", "rec_b_a", "rec_w_x", "rec_b_x", "rec_lambda", "rec_w_out", "ln1_g",
           "ln1_b", "mlp_w1", "mlp_w2", "ln2_g", "ln2_b")


def _gather_full_weights(w):
    wb = {k: w[k].astype(MXU_DTYPE) for k in BIG}
    now = ("hyb_w_in",)
    shards = [_halves(wb[k][:1]) for k in now]
    shards.append(_pack_mid(*[w[k] for k in MID]))
    got = _all_gather_weights(shards, "all_gather_weights")
    me = 2 * lax.axis_index("x") + lax.axis_index("y")
    got = [lax.dynamic_update_slice(g, s[None], (me, 0, 0, 0)) for s, g in zip(shards, got)]

    def full(k, g):
        g = g.reshape((N_CHIPS,) + w[k].shape[1:])
        if k in CHIP_MAJOR:
            return g[:, None]
        if k in ("rec_w_a", "rec_w_x"):
            return g.transpose(1, 0, 2, 3).reshape(LRU_BLOCKS, LRU_BLOCK_W, LRU_BLOCK_W)
        f = _cols_from_chips(g[:, None])[0] if k in COL_SHARDED else _rows_from_chips(g[:, None])[0]
        return _pad_hyb_cols(f) if k == "hyb_w_in" else f

    rec = ("rec_w_in", "rec_w_out", "rec_w_a", "rec_w_x")
    groups = [(("hyb_w_out",), 0), (CHIP_MAJOR, 0), (rec, 0), (CHIP_MAJOR, 1), (("hyb_w_in", "hyb_w_out"), 1),
              (CHIP_MAJOR, 2), (rec, 1), (CHIP_MAJOR, 3)]
    own = [wb[k][j] for names, j in groups for k in names]
    land = [lax.dynamic_update_slice(lax.empty((N_CHIPS,) + o.shape, o.dtype), o[None], (me,) + (0,) * o.ndim)
            for o in own]
    send_sems, recv_sems, own, land = _gather_start(own, land, got[-1], "gather_start")
    W = {k: [None] * w[k].shape[0] for k in BIG}
    for k, g in zip(now, got[:-1]):
        W[k][0] = full(k, g)
    arrived = [0]

    def ensure(upto, after):
        while arrived[0] <= upto:
            gi = arrived[0]
            names, j = groups[gi]
            lo = sum(len(nm) for nm, _ in groups[:gi])
            sl = slice(lo, lo + len(names))
            got_g = _gather_wait(send_sems[sl], recv_sems[sl], own[sl], land[sl], after, "gather_wait_%d" % gi)
            for k, g in zip(names, got_g):
                W[k][j] = full(k, g)
            arrived[0] += 1

    def mixer_ready(layer, after, out_projection=False):
        if layer:
            ensure({1: 2, 2: 4, 3: 6}[layer], after)
        elif out_projection:
            ensure(0, after)

    def mlp_w(layer, after):
        ensure({0: 1, 1: 3, 2: 5, 3: 7}[layer], after)
        return W["mlp_w1"][layer], W["mlp_w2"][layer], 0

    conv_w, rconv_w, rconv_b, b_a, b_x, lam = _unpack_mid(got[-1])
    W["hyb_conv_w"] = conv_w.transpose(1, 2, 0, 3).reshape(2, CONV_K, 3 * B_W)
    W["rec_conv_w"] = rconv_w.transpose(1, 2, 0, 3).reshape(2, CONV_K, D_MODEL)
    for k, v in (("rec_conv_b", rconv_b), ("rec_b_a", b_a), ("rec_b_x", b_x), ("rec_lambda", lam)):
        W[k] = v.transpose(1, 0, 2).reshape(2, D_MODEL)
    for k in SMALL:
        W[k] = w[k]
    return W, mlp_w, mixer_ready


def _group_by_chip(gh, gr, d_w1, d_w2):
    g = dict(gh, **gr)
    g["hyb_w_in"] = _unpad_hyb_cols(g["hyb_w_in"])
    out = []
    for k in BIG:
        if k == "mlp_w1":
            v = d_w1
        elif k == "mlp_w2":
            v = d_w2
        elif k in ("rec_w_a", "rec_w_x"):
            v = g[k].reshape(1, LRU_BLOCKS, N_CHIPS, LRU_BLOCK_W // N_CHIPS, LRU_BLOCK_W).transpose(2, 0, 1, 3, 4)
        else:
            v = _cols_to_chips(g[k][None]) if k in COL_SHARDED else _rows_to_chips(g[k][None])
        out.append(v.reshape(N_CHIPS, 2, -1, v.shape[-1]))
    conv_w = g["hyb_conv_w"].reshape(CONV_K, N_CHIPS, -1).transpose(1, 0, 2)
    rconv_w = g["rec_conv_w"].reshape(CONV_K, N_CHIPS, -1).transpose(1, 0, 2)
    vecs = [g[k].reshape(N_CHIPS, -1) for k in ("rec_conv_b", "rec_b_a", "rec_b_x", "rec_lambda")]
    out.append(_pack_mid(conv_w, rconv_w, *vecs).reshape(N_CHIPS, 2, MID_ROWS // 2, D_MODEL))
    return out


def kernel(x, hyb_w_in, hyb_sinks, hyb_conv_w, hyb_a_log, hyb_dt_bias, hyb_norm_w, hyb_w_out, rec_w_in, rec_conv_w, rec_conv_b, rec_w_a, rec_b_a, rec_w_x, rec_b_x, rec_lambda, rec_w_out, ln1_g, ln1_b, mlp_w1, mlp_w2, ln2_g, ln2_b, loss_target, m_hyb_w_in, m_hyb_sinks, m_hyb_conv_w, m_hyb_a_log, m_hyb_dt_bias, m_hyb_norm_w, m_hyb_w_out, m_rec_w_in, m_rec_conv_w, m_rec_conv_b, m_rec_w_a, m_rec_b_a, m_rec_w_x, m_rec_b_x, m_rec_lambda, m_rec_w_out, m_ln1_g, m_ln1_b, m_mlp_w1, m_mlp_w2, m_ln2_g, m_ln2_b, v_hyb_w_in, v_hyb_sinks, v_hyb_conv_w, v_hyb_a_log, v_hyb_dt_bias, v_hyb_norm_w, v_hyb_w_out, v_rec_w_in, v_rec_conv_w, v_rec_conv_b, v_rec_w_a, v_rec_b_a, v_rec_w_x, v_rec_b_x, v_rec_lambda, v_rec_w_out, v_ln1_g, v_ln1_b, v_mlp_w1, v_mlp_w2, v_ln2_g, v_ln2_b):
    args = locals()
    w = {k: args[k] for k in WEIGHTS}
    m = {k: args["m_" + k] for k in WEIGHTS}
    v = {k: args["v_" + k] for k in WEIGHTS}

    W, mlp_w, mixer_ready = _gather_full_weights(w)

    core = lax.axis_index("c").astype(jnp.int32)
    me = (2 * lax.axis_index("x") + lax.axis_index("y")).astype(jnp.int32)
    slots = jnp.arange(N_CHIPS, dtype=jnp.int32)
    where = jnp.concatenate([me[None], jnp.where(slots == me, (slots + 1) % N_CHIPS, slots), core[None]])
    state = {}

    def on_group(group, gh, gr, d_w1, d_w2):
        if group is None:
            send_sems, recv_sems, by_chip, recv, _ = state["sibling"]
            by_chip, from_sibling = _rs_sibling_wait(send_sems, recv_sems, by_chip, recv, gr, "rs_sibling_wait")
            group = 1
        else:
            by_chip = _group_by_chip(gh, gr, d_w1, d_w2)
            if group == 1:
                recv = [lax.empty((N_CHIPS,) + g.shape[2:], g.dtype) for g in by_chip]
                state["sibling"] = _rs_sibling_start(by_chip, recv, "rs_sibling_start")
                return state["sibling"][4][0, 0]
            from_sibling = _rs_to_sibling(by_chip, "rs_to_sibling")
        pair = [_pair_sum(g, r, core[None], "pair_sum") for g, r in zip(by_chip, from_sibling)]
        recv = [lax.empty(p.shape, p.dtype) for p in pair]
        state[group] = _rs_across_start(pair, recv, "rs_across_start_%d" % group)
        return state[group][4][0, 0]

    def finish_group(group, after, prev):
        send_sems, recv_sems, pair, recv, _ = state[group]
        pair, from_chips = _rs_across_wait(send_sems, recv_sems, pair, recv, after, "rs_across_wait_%d" % group)
        half = [_chip_sum(r, p, where, "chip_sum") for r, p in zip(from_chips, pair)]
        joined = _rs_join_halves(half, "rs_join_halves")
        outs = {}
        for k, g in zip(BIG, joined[:-1]):
            n = g.shape[-1]
            outs[k] = _adamw_half(w[k].reshape(-1, n), g.reshape(-1, n), m[k].reshape(-1, n), v[k].reshape(-1, n),
                                  group, None if prev is None else prev[k], "adamw")
        return outs, joined[-1]

    loss, dx, grads = _local_step(x[0], loss_target[0], W, mlp_w, mixer_ready, on_group)
    loss = lax.psum(loss[0, 0], ("x", "y", "c"))
    g_out, d_out, m_out, v_out = {}, {}, {}, {}

    upper, mid_upper = finish_group(1, state[0][4], None)
    small_g = _pack_small(*[grads[k] for k in SMALL])
    small_all = _all_gather_small(small_g, "all_gather_small")
    sw, sm, sv = (_pack_small(*[t[k] for k in SMALL]) for t in (w, m, v))
    sg, sd, snm, snv = _adamw_small(sw, small_all, sm, sv, "adamw_small")
    for dst, packed in ((g_out, sg), (d_out, sd), (m_out, snm), (v_out, snv)):
        for k, val in zip(SMALL, _unpack_small(packed)):
            dst[k] = val

    done = sum([upper[k][1][0, 0] for k in BIG], sg[0, 0]).reshape(1, 1)
    both, mid_lower = finish_group(0, done, upper)
    for k in BIG:
        shape = w[k].shape
        g_out[k], d_out[k], m_out[k], v_out[k] = (t.reshape(shape) for t in both[k])
    mid_w, mid_m, mid_v = (_pack_mid(*[t[k] for k in MID]).reshape(-1, D_MODEL) for t in (w, m, v))
    mid_g = jnp.concatenate([mid_lower.reshape(-1, D_MODEL), mid_upper.reshape(-1, D_MODEL)])
    mid_d, mid_nm, mid_nv = _adamw(mid_w, mid_g, mid_m, mid_v, "adamw_mid")
    for dst, packed in ((g_out, mid_g), (d_out, mid_d), (m_out, mid_nm), (v_out, mid_nv)):
        for k, val in zip(MID, _unpack_mid(packed.reshape(2, MID_ROWS, D_MODEL))):
            dst[k] = val.reshape(w[k].shape)

    return (loss, dx[None], *[g_out[k] for k in WEIGHTS], *[d_out[k] for k in WEIGHTS],
            *[m_out[k] for k in WEIGHTS], *[v_out[k] for k in WEIGHTS])
```

```python
import functools
import math

import jax
import jax.numpy as jnp
import numpy as np
from jax import lax
from jax.experimental import pallas as pl
from jax.experimental.pallas import tpu as pltpu

F32 = jnp.float32
MXU_DTYPE = jnp.bfloat16
ACT_DTYPE = jnp.bfloat16
ICI_DTYPE = jnp.bfloat16

D_MODEL = 1024
DEPTH = 4
A_HEAD_DIM = 64
A_Q_HEADS = 8
A_KV_HEADS = 2
WINDOW = 128
ROPE_THETA = 10000.0
B_HEADS = 4
B_HEAD_DIM = 128
B_CHUNK = 64
CONV_K = 4
LRU_BLOCKS = 4
LRU_BLOCK_W = D_MODEL // LRU_BLOCKS
LRU_C = 8.0
D_FF = 4 * D_MODEL
A_Q_W = A_Q_HEADS * A_HEAD_DIM
A_KV_W = A_KV_HEADS * A_HEAD_DIM
B_W = B_HEADS * B_HEAD_DIM
HYB_PROJ = A_Q_W + 2 * A_KV_W + 4 * B_W + 2 * B_HEADS
DN_ALPHA = (2 * DEPTH) ** 0.25
LN_EPS = 1e-5
NORM_EPS = 1e-6
ADAM_LR = 0.001
ADAM_B1 = 0.9
ADAM_B2 = 0.999
ADAM_EPS = 1e-08
ADAM_WD = 0.01
ADAM_STEP = 10

HP_Q = 0
HP_QKVB = 512
HP_Z = 2048
HP_K = 2560
HP_V = 2688
HP_BG = 2816
HYB_PAD = 3072

N_CHIPS = 4
N_DEV = 8
V7X_VMEM_LIMIT = 48 * 1024 * 1024
LANES = 128
SUBLANES = 8
NEG_BIG = -1e30

NN = (((1,), (0,)), ((), ()))
NT = (((1,), (1,)), ((), ()))
TN = (((0,), (0,)), ((), ()))


def _cparams(*sem):
    return pltpu.CompilerParams(dimension_semantics=sem, vmem_limit_bytes=V7X_VMEM_LIMIT)


def _dot(a, b, dims=NN):
    return lax.dot_general(a.astype(MXU_DTYPE), b.astype(MXU_DTYPE), dims, preferred_element_type=F32)


def _split_bf16(a):
    hi = a.astype(jnp.bfloat16)
    return hi, (a - hi.astype(F32)).astype(jnp.bfloat16)


def _dotf(a, b, dims=NN):
    ah, al = _split_bf16(a)
    bh, bl = _split_bf16(b)
    dg = functools.partial(lax.dot_general, dimension_numbers=dims, preferred_element_type=F32)
    return dg(ah, bh) + (dg(ah, bl) + dg(al, bh))


def _tile(dim, pref):
    t = min(dim, pref)
    while dim % t:
        t //= 2
    return t


def _sigmoid(x):
    return 1.0 / (1.0 + jnp.exp(-x))


def _silu(x):
    return x * _sigmoid(x)


def _dsilu(x):
    s = _sigmoid(x)
    return s * (1.0 + x * (1.0 - s))


GELU_C = 0.7978845608028654
GELU_A = 0.044715


def _gelu(x):
    return 0.5 * x * (1.0 + jnp.tanh(GELU_C * (x + GELU_A * x * x * x)))


def _dgelu(x):
    t = jnp.tanh(GELU_C * (x + GELU_A * x * x * x))
    return 0.5 * (1.0 + t) + 0.5 * x * (1.0 - t * t) * GELU_C * (1.0 + 3.0 * GELU_A * x * x)


def _matmul(a, b, mode, name, *, tm=1024, tn=1024, tk=1024, a_fn=None, epi=None, extra=None, out_dtype=F32,
            b_chips=None, out_chips=None):
    chunk = tk
    if mode == "tn":
        K, M = a.shape
        if K <= 4 * tk:
            tk, tm = K, tm // 2
    else:
        M, K = a.shape
    whole_k = False
    if b_chips is not None:
        g, b_layer = b_chips
        r, n = b.shape[2:]
        n_dim, k_dim = (r, n) if mode == "nt" else (n, r)
        N = N_CHIPS * n_dim if g == "j" else n_dim
        assert K == (N_CHIPS * k_dim if g == "k" else k_dim)
        if g == "j":
            tn = n_dim
        else:
            whole_k, tk, tm = True, K, tm // 2
    elif mode == "nt":
        N = b.shape[0]
        if tk < K <= 3 * tk:
            tk, tm = K, tm // 2
    else:
        N = b.shape[1]
    if out_chips is not None:
        og, o_layer, o_buf = out_chips
        if og == "j":
            tn = o_buf.shape[3]
        else:
            tm = min(tm, o_buf.shape[2])
    tm, tn, tk = _tile(M, tm), _tile(N, tn), _tile(K, tk)
    nk = K // tk
    if mode == "tn":
        a_spec = pl.BlockSpec((tk, tm), lambda i, j, k: (k, i))
    else:
        a_spec = pl.BlockSpec((tm, tk), lambda i, j, k: (i, k))
    b_block = (tn, tk) if mode == "nt" else (tk, tn)
    if b_chips is None:
        b_spec = pl.BlockSpec(b_block, (lambda i, j, k: (j, k)) if mode == "nt" else (lambda i, j, k: (k, j)))
    elif mode == "nt" and whole_k:
        b_spec = pl.BlockSpec((N_CHIPS, None, tn, k_dim), lambda i, j, k: (0, b_layer, j, 0))
    elif mode == "nt":
        b_spec = pl.BlockSpec((None, None) + b_block, lambda i, j, k: (j, b_layer, 0, k))
    elif whole_k:
        b_spec = pl.BlockSpec((N_CHIPS, None, k_dim, tn), lambda i, j, k: (0, b_layer, 0, j),
                              pipeline_mode=pl.Buffered(1) if N == tn else None)
    else:
        b_spec = pl.BlockSpec((None, None) + b_block, lambda i, j, k: (j, b_layer, k, 0))
    o_spec = pl.BlockSpec((tm, tn), lambda i, j, k: (i, j))
    e_spec = o_spec
    if out_chips is not None:
        per = o_buf.shape[2] // tm
        o_spec = pl.BlockSpec((None, None, tm, tn), (lambda i, j, k: (j, o_layer, i, 0)) if og == "j"
                              else (lambda i, j, k: (i // per, o_layer, i % per, j)))
    dims = {"nn": NN, "nt": NT, "tn": TN}[mode]
    extras = () if extra is None else (extra if isinstance(extra, tuple) else (extra,))
    out_dtypes = out_dtype if isinstance(out_dtype, tuple) else (out_dtype,)
    n_in = 2 + len(extras) + (out_chips is not None)

    def body(*refs):
        a_ref, b_ref = refs[0], refs[1]
        e_refs = refs[2:2 + len(extras)]
        o_refs = refs[n_in:n_in + len(out_dtypes)]
        av = a_ref[...]
        if a_fn is not None:
            av = a_fn(av)
        if whole_k and mode == "nt":
            part = _dot(av[:, :k_dim], b_ref[0], dims)
            for chip in range(1, N_CHIPS):
                part = part + _dot(av[:, chip * k_dim:(chip + 1) * k_dim], b_ref[chip], dims)
        elif mode == "tn" and tk > chunk:
            part = _dot(av[:chunk], b_ref[0:chunk, :], dims)
            for c0 in range(chunk, tk, chunk):
                part = part + _dot(av[c0:c0 + chunk], b_ref[c0:c0 + chunk, :], dims)
        else:
            bv = b_ref[...]
            if whole_k:
                bv = bv.reshape(K, tn)
            part = _dot(av, bv, dims)

        def finish(acc):
            if epi is not None:
                acc = epi(acc, *[e[...] for e in e_refs])
            for o_ref, val, dt in zip(o_refs, acc if isinstance(acc, tuple) else (acc,), out_dtypes):
                o_ref[...] = val.astype(dt)

        if nk == 1:
            finish(part)
        else:
            acc_ref = refs[-1]
            k = pl.program_id(2)

            @pl.when(k == 0)
            def _():
                acc_ref[...] = part

            @pl.when(k > 0)
            def _():
                acc_ref[...] += part

            @pl.when(k == nk - 1)
            def _():
                finish(acc_ref[...])

    row_spec = pl.BlockSpec((1, tn), lambda i, j, k: (0, j))
    in_specs = [a_spec, b_spec] + [row_spec if e.shape[0] == 1 else e_spec for e in extras]
    args = (a, b) + extras
    out_shape = [jax.ShapeDtypeStruct((M, N), dt) for dt in out_dtypes]
    out_specs = [o_spec] * len(out_dtypes)
    aliases = {}
    if out_chips is not None:
        in_specs.append(pl.BlockSpec(memory_space=pl.ANY))
        args += (o_buf,)
        out_shape = [jax.ShapeDtypeStruct(o_buf.shape, o_buf.dtype)]
        aliases = {n_in - 1: 0}
    if not isinstance(out_dtype, tuple):
        out_shape, out_specs = out_shape[0], out_specs[0]
    return pl.pallas_call(
        body, name=name, grid=(M // tm, N // tn, nk), in_specs=in_specs, out_specs=out_specs, out_shape=out_shape,
        input_output_aliases=aliases,
        scratch_shapes=[pltpu.VMEM((tm, tn), F32)] if nk > 1 else [],
        compiler_params=_cparams("parallel", "parallel", "arbitrary"),
    )(*args)


def _relu2(v):
    r = jnp.maximum(v, 0.0)
    return r * r


def _epi_drelu2(acc, h):
    return acc * (2.0 * jnp.maximum(h, 0.0))


def _epi_add_residual(acc, du):
    return acc + DN_ALPHA * du


def _epi_residual_ln(acc, x, g, b):
    u = DN_ALPHA * x + acc
    mu = jnp.mean(u, axis=-1, keepdims=True)
    d = u - mu
    var = jnp.mean(d * d, axis=-1, keepdims=True)
    o = d * lax.rsqrt(var + LN_EPS) * g + b
    return o, o, u


def _ln_out():
    return (F32, ACT_DTYPE, F32)


def _ln_bwd(u, g, dout, name):
    T, D = u.shape
    tr = _tile(T, 512)

    def body(u_ref, g_ref, d_ref, du_ref, dub_ref, dg_ref, db_ref):
        i = pl.program_id(0)
        u = u_ref[...]
        mu = jnp.mean(u, axis=-1, keepdims=True)
        d = u - mu
        rstd = lax.rsqrt(jnp.mean(d * d, axis=-1, keepdims=True) + LN_EPS)
        xhat = d * rstd
        dout_v = d_ref[...]
        dxh = dout_v * g_ref[...]
        m1 = jnp.mean(dxh, axis=-1, keepdims=True)
        m2 = jnp.mean(dxh * xhat, axis=-1, keepdims=True)
        du = rstd * (dxh - m1 - xhat * m2)
        du_ref[...] = du
        dub_ref[...] = du.astype(ACT_DTYPE)
        pg = jnp.sum(dout_v * xhat, axis=0, keepdims=True)
        pb = jnp.sum(dout_v, axis=0, keepdims=True)

        @pl.when(i == 0)
        def _():
            dg_ref[...] = pg
            db_ref[...] = pb

        @pl.when(i > 0)
        def _():
            dg_ref[...] += pg
            db_ref[...] += pb

    row = pl.BlockSpec((tr, D), lambda i: (i, 0))
    vec = pl.BlockSpec((1, D), lambda i: (0, 0))
    return pl.pallas_call(
        body, name=name, grid=(T // tr,), in_specs=[row, vec, row], out_specs=[row, row, vec, vec],
        out_shape=[jax.ShapeDtypeStruct((T, D), F32), jax.ShapeDtypeStruct((T, D), ACT_DTYPE),
                   jax.ShapeDtypeStruct((1, D), F32), jax.ShapeDtypeStruct((1, D), F32)],
        compiler_params=_cparams("arbitrary"),
    )(u, g, dout)


def _loss_head(y, tgt, name):
    T, D = y.shape
    tr = _tile(T, 512)

    def body(y_ref, t_ref, dy_ref, l_ref):
        i = pl.program_id(0)
        e = y_ref[...] - t_ref[...]
        dy_ref[...] = e * (1.0 / D)
        part = jnp.sum(e * e, axis=(0, 1), keepdims=True) * (0.5 / D)

        @pl.when(i == 0)
        def _():
            l_ref[...] = part

        @pl.when(i > 0)
        def _():
            l_ref[...] += part

    row = pl.BlockSpec((tr, D), lambda i: (i, 0))
    one = pl.BlockSpec((1, 1), lambda i: (0, 0))
    return pl.pallas_call(
        body, name=name, grid=(T // tr,), in_specs=[row, row], out_specs=[row, one],
        out_shape=[jax.ShapeDtypeStruct((T, D), F32), jax.ShapeDtypeStruct((1, 1), F32)],
        compiler_params=_cparams("arbitrary"),
    )(y, tgt)


def _swap_half(x):
    n = x.shape[-1]
    lane = lax.broadcasted_iota(jnp.int32, x.shape, 1)
    first = (lane % A_HEAD_DIM) < (A_HEAD_DIM // 2)
    return jnp.where(first, pltpu.roll(x, n - A_HEAD_DIM // 2, axis=1), pltpu.roll(x, A_HEAD_DIM // 2, axis=1))


def _rope(x, cos, sin_signed):
    return x * cos + _swap_half(x) * sin_signed


def _rope_t(dy, cos, sin_signed):
    return dy * cos + _swap_half(dy * sin_signed)


def _rope_tables(T):
    half = A_HEAD_DIM // 2
    inv_freq = np.float32(ROPE_THETA) ** (-np.arange(half, dtype=np.float32) / np.float32(half))
    ang = np.arange(T, dtype=np.float32)[:, None] * inv_freq[None, :]
    cos = np.tile(np.cos(ang), (1, 4))
    sin = np.sin(ang)
    sin_signed = np.tile(np.concatenate([-sin, sin], axis=1), (1, 2))
    return jnp.asarray(cos, F32), jnp.asarray(sin_signed, F32)


def _band_mask(n):
    qi = lax.broadcasted_iota(jnp.int32, (WINDOW, 2 * WINDOW), 0)
    kj = lax.broadcasted_iota(jnp.int32, (WINDOW, 2 * WINDOW), 1)
    return (kj > qi) & (kj <= qi + WINDOW) & ((n > 0) | (kj >= WINDOW))


def _place(v, src_half, dst_half):
    lane = lax.broadcasted_iota(jnp.int32, v.shape, 1)
    if src_half != dst_half:
        v = pltpu.roll(v, A_HEAD_DIM, axis=1)
    keep = (lane >= A_HEAD_DIM) if dst_half else (lane < A_HEAD_DIM)
    return jnp.where(keep, v, 0.0)


def _attn_specs():
    kb, vb = HP_K // LANES, HP_V // LANES
    prev = lambda n: jnp.maximum(n - 1, 0)
    return dict(
        q=pl.BlockSpec((WINDOW, A_Q_W), lambda n: (n, 0)),
        kc=pl.BlockSpec((WINDOW, LANES), lambda n: (n, kb)),
        kp=pl.BlockSpec((WINDOW, LANES), lambda n: (prev(n), kb)),
        vc=pl.BlockSpec((WINDOW, LANES), lambda n: (n, vb)),
        vp=pl.BlockSpec((WINDOW, LANES), lambda n: (prev(n), vb)),
        tq=pl.BlockSpec((WINDOW, LANES), lambda n: (n, 0)),
        tp=pl.BlockSpec((WINDOW, LANES), lambda n: (prev(n), 0)),
        sink=pl.BlockSpec((A_Q_HEADS, LANES), lambda n: (0, 0)),
        row512=pl.BlockSpec((WINDOW, A_Q_W), lambda n: (n, 0)),
        row128=pl.BlockSpec((WINDOW, LANES), lambda n: (n, 0)),
        lse=pl.BlockSpec((WINDOW, A_Q_HEADS), lambda n: (n, 0)),
    )


def _attn_fwd(proj, cos, sin_s, sinks_b, name):
    T = proj.shape[0]
    sp = _attn_specs()

    def body(q_ref, kc_ref, kp_ref, vc_ref, vp_ref, cq_ref, sq_ref, cp_ref, sp_ref, sink_ref, o_ref, l_ref):
        n = pl.program_id(0)
        cq, sq = cq_ref[...], sq_ref[...]
        cq4, sq4 = jnp.tile(cq, (1, A_Q_W // LANES)), jnp.tile(sq, (1, A_Q_W // LANES))
        q = _rope(q_ref[...], cq4, sq4) * (A_HEAD_DIM ** -0.5)
        kc = _rope(kc_ref[...], cq, sq)
        kp = _rope(kp_ref[...], cp_ref[...], sp_ref[...])
        kk = jnp.concatenate([kp, kc], axis=0)
        vv = jnp.concatenate([vp_ref[...], vc_ref[...]], axis=0)
        mask = _band_mask(n)[None]
        lane = lax.broadcasted_iota(jnp.int32, (WINDOW, LANES), 1)
        lane8 = lax.broadcasted_iota(jnp.int32, (WINDOW, A_Q_HEADS), 1)
        qe = jnp.stack([_place(q[:, (hq // 2) * LANES:(hq // 2 + 1) * LANES], hq % 2, hq // 4) for hq in range(A_Q_HEADS)])
        kk8 = jnp.broadcast_to(kk[None], (A_Q_HEADS,) + kk.shape)
        vv8 = jnp.broadcast_to(vv[None], (A_Q_HEADS,) + vv.shape)
        sk = jnp.stack([sink_ref[hq:hq + 1, 0:1] for hq in range(A_Q_HEADS)])
        s = jnp.where(mask, _dot(qe, kk8, BNT), NEG_BIG)
        m = jnp.maximum(jnp.max(s, axis=-1, keepdims=True), sk)
        p = jnp.exp(s - m)
        den = jnp.sum(p, axis=-1, keepdims=True) + jnp.exp(sk - m)
        o = _dot(p * (1.0 / den), vv8, BNN)
        lse_h = m + jnp.log(den)
        outs = []
        lse = jnp.zeros((WINDOW, A_Q_HEADS), F32)
        for pb in range(A_Q_HEADS // 2):
            halves = [_place(o[2 * pb + e], pb // 2, e) for e in range(2)]
            outs.append(jnp.where(lane < A_HEAD_DIM, halves[0], halves[1]))
            for e in range(2):
                lse = jnp.where(lane8 == 2 * pb + e, lse_h[2 * pb + e], lse)
        o_ref[...] = jnp.concatenate(outs, axis=1)
        l_ref[...] = lse

    return pl.pallas_call(
        body, name=name, grid=(T // WINDOW,),
        in_specs=[sp["q"], sp["kc"], sp["kp"], sp["vc"], sp["vp"], sp["tq"], sp["tq"], sp["tp"], sp["tp"], sp["sink"]],
        out_specs=[sp["row512"], sp["lse"]],
        out_shape=[jax.ShapeDtypeStruct((T, A_Q_W), F32), jax.ShapeDtypeStruct((T, A_Q_HEADS), F32)],
        compiler_params=_cparams("parallel"),
    )(proj, proj, proj, proj, proj, cos, sin_s, cos, sin_s, sinks_b)


def _attn_bwd(proj, cos, sin_s, sinks_b, o, lse, dmix, name):
    T = proj.shape[0]
    sp = _attn_specs()

    def body(q_ref, kc_ref, kp_ref, vc_ref, vp_ref, cq_ref, sq_ref, cp_ref, sp_ref, sink_ref, o_ref, l_ref, do_ref,
             dq_ref, dkc_ref, dkp_ref, dvc_ref, dvp_ref, dsink_ref):
        n = pl.program_id(0)
        cq, sq = cq_ref[...], sq_ref[...]
        cp, sps = cp_ref[...], sp_ref[...]
        cq4, sq4 = jnp.tile(cq, (1, A_Q_W // LANES)), jnp.tile(sq, (1, A_Q_W // LANES))
        q = _rope(q_ref[...], cq4, sq4) * (A_HEAD_DIM ** -0.5)
        kc = _rope(kc_ref[...], cq, sq)
        kp = _rope(kp_ref[...], cp, sps)
        kk = jnp.concatenate([kp, kc], axis=0)
        vv = jnp.concatenate([vp_ref[...], vc_ref[...]], axis=0)
        mask = _band_mask(n)[None]
        lane = lax.broadcasted_iota(jnp.int32, (WINDOW, LANES), 1)
        do_all, o_all, l_all = do_ref[...], o_ref[...], l_ref[...]
        lane8 = lax.broadcasted_iota(jnp.int32, (WINDOW, A_Q_HEADS), 1)
        head8 = lax.broadcasted_iota(jnp.int32, (1, A_Q_HEADS), 1)
        prod = do_all * o_all
        qes, does, deltas, lhs = [], [], [], []
        for hq in range(A_Q_HEADS):
            pb, e, kvh = hq // 2, hq % 2, hq // 4
            blk = slice(pb * LANES, (pb + 1) * LANES)
            in_half = (lane >= A_HEAD_DIM) if e else (lane < A_HEAD_DIM)
            deltas.append(jnp.sum(jnp.where(in_half, prod[:, blk], 0.0), axis=-1, keepdims=True))
            qes.append(_place(q[:, blk], e, kvh))
            does.append(_place(do_all[:, blk], e, kvh))
            lhs.append(jnp.sum(jnp.where(lane8 == hq, l_all, 0.0), axis=-1, keepdims=True))
        qe, doe, delta, lh = jnp.stack(qes), jnp.stack(does), jnp.stack(deltas), jnp.stack(lhs)
        kk8 = jnp.broadcast_to(kk[None], (A_Q_HEADS,) + kk.shape)
        vv8 = jnp.broadcast_to(vv[None], (A_Q_HEADS,) + vv.shape)
        sk = jnp.stack([sink_ref[hq:hq + 1, 0:1] for hq in range(A_Q_HEADS)])
        s = _dot(qe, kk8, BNT)
        p = jnp.where(mask, jnp.exp(jnp.where(mask, s, NEG_BIG) - lh), 0.0)
        dvv = jnp.sum(_dot(p, doe, BTN), axis=0)
        ds = p * (_dot(doe, vv8, BNT) - delta)
        dkk = jnp.sum(_dot(ds, qe, BTN), axis=0)
        dqe = _dot(ds, kk8, BNN)
        dsink_h = -jnp.sum(jnp.exp(sk - lh) * delta, axis=(1, 2), keepdims=True)
        dqs = []
        dsk = jnp.zeros((1, A_Q_HEADS), F32)
        for pb in range(A_Q_HEADS // 2):
            halves = [_place(dqe[2 * pb + e], pb // 2, e) for e in range(2)]
            dqs.append(jnp.where(lane < A_HEAD_DIM, halves[0], halves[1]))
            for e in range(2):
                dsk = jnp.where(head8 == 2 * pb + e, dsink_h[2 * pb + e], dsk)
        dq = jnp.concatenate(dqs, axis=1) * (A_HEAD_DIM ** -0.5)
        dq_ref[...] = _rope_t(dq, cq4, sq4).astype(ACT_DTYPE)
        dkp_ref[...] = _rope_t(dkk[:WINDOW], cp, sps)
        dkc_ref[...] = _rope_t(dkk[WINDOW:], cq, sq)
        dvp_ref[...] = dvv[:WINDOW]
        dvc_ref[...] = dvv[WINDOW:]

        @pl.when(n == 0)
        def _():
            dsink_ref[...] = dsk

        @pl.when(n > 0)
        def _():
            dsink_ref[...] += dsk

    return pl.pallas_call(
        body, name=name, grid=(T // WINDOW,),
        in_specs=[sp["q"], sp["kc"], sp["kp"], sp["vc"], sp["vp"], sp["tq"], sp["tq"], sp["tp"], sp["tp"], sp["sink"],
                  sp["row512"], sp["lse"], sp["row512"]],
        out_specs=[sp["row512"], sp["row128"], sp["row128"], sp["row128"], sp["row128"],
                   pl.BlockSpec((1, A_Q_HEADS), lambda n: (0, 0))],
        out_shape=[jax.ShapeDtypeStruct((T, HYB_PAD), ACT_DTYPE)] + [jax.ShapeDtypeStruct((T, LANES), F32)] * 4
        + [jax.ShapeDtypeStruct((1, A_Q_HEADS), F32)],
        compiler_params=_cparams("arbitrary"),
    )(proj, proj, proj, proj, proj, cos, sin_s, cos, sin_s, sinks_b, o, lse, dmix)


def _shift_down(x, prev8, k):
    if k == 0:
        return x
    R, W = x.shape
    r = pltpu.roll(jnp.concatenate([prev8, x], axis=0).reshape(R // SUBLANES + 1, SUBLANES, W), k, axis=1)
    sub = lax.broadcasted_iota(jnp.int32, (R // SUBLANES, SUBLANES, W), 1)
    return jnp.where(sub < k, r[:-1], r[1:]).reshape(R, W)


def _shift_up(x, next8, k):
    if k == 0:
        return x
    R, W = x.shape
    r = pltpu.roll(jnp.concatenate([x, next8], axis=0).reshape(R // SUBLANES + 1, SUBLANES, W), SUBLANES - k, axis=1)
    sub = lax.broadcasted_iota(jnp.int32, (R // SUBLANES, SUBLANES, W), 1)
    return jnp.where(sub >= SUBLANES - k, r[1:], r[:-1]).reshape(R, W)


def _conv(x, prev8, w):
    w0, w1, w2, w3 = (w[j:j + 1] for j in range(CONV_K))
    x1 = _shift_down(x, prev8, 1)
    v = x * w1 + x1 * w0
    v_prev8 = prev8 * w1 + pltpu.roll(prev8, 1, axis=0) * w0
    return x * w3 + x1 * w2 + _shift_down(v, v_prev8, 2)


def _conv_bwd(x, w, dy, next8_dy):
    dx = dy * w[CONV_K - 1:CONV_K]
    dws = []
    for j in range(CONV_K - 1):
        up = _shift_up(dy, next8_dy, CONV_K - 1 - j)
        dx = dx + up * w[j:j + 1]
        dws.append(jnp.sum(up * x, axis=0, keepdims=True))
    dws.append(jnp.sum(dy * x, axis=0, keepdims=True))
    return dx, dws


def _dnconv_fwd(proj, conv_w, name):
    T = proj.shape[0]
    R = _tile(T, 512)
    cb0 = HP_QKVB // A_Q_W

    def body(x_ref, w_ref, o_ref, prev_ref):
        i = pl.program_id(1)

        @pl.when(i == 0)
        def _():
            prev_ref[...] = jnp.zeros_like(prev_ref)

        x = x_ref[...]
        o_ref[...] = _silu(_conv(x, prev_ref[...], w_ref[...]))
        prev_ref[...] = x[R - SUBLANES:]

    return pl.pallas_call(
        body, name=name, grid=(3, T // R),
        in_specs=[pl.BlockSpec((R, B_W), lambda j, i: (i, cb0 + j)), pl.BlockSpec((CONV_K, B_W), lambda j, i: (0, j))],
        out_specs=pl.BlockSpec((R, B_W), lambda j, i: (i, j)),
        out_shape=jax.ShapeDtypeStruct((T, 3 * B_W), F32),
        scratch_shapes=[pltpu.VMEM((SUBLANES, B_W), F32)],
        compiler_params=_cparams("parallel", "arbitrary"),
    )(proj, conv_w)


def _dnconv_bwd(proj, conv_w, dy, dproj, name):
    T = proj.shape[0]
    R = _tile(T, 512)
    nb = T // R
    cb0 = HP_QKVB // A_Q_W
    r8 = R // SUBLANES

    def body(x_ref, xp_ref, w_ref, dy_ref, dproj_in, dx_ref, dw_ref, next_ref):
        i = pl.program_id(1)
        blk = nb - 1 - i

        @pl.when(i == 0)
        def _():
            next_ref[...] = jnp.zeros_like(next_ref)

        x = x_ref[...]
        prev8 = jnp.where(blk > 0, xp_ref[...], 0.0)
        w = w_ref[...]
        dpre = dy_ref[...] * _dsilu(_conv(x, prev8, w))
        dx, dw = _conv_bwd(x, w, dpre, next_ref[...])
        dx_ref[...] = dx.astype(ACT_DTYPE)
        next_ref[...] = dpre[:SUBLANES]

        @pl.when(i == 0)
        def _():
            for j in range(CONV_K):
                dw_ref[j:j + 1, :] = dw[j]

        @pl.when(i > 0)
        def _():
            for j in range(CONV_K):
                dw_ref[j:j + 1, :] += dw[j]

    return pl.pallas_call(
        body, name=name, grid=(3, nb),
        in_specs=[pl.BlockSpec((R, B_W), lambda j, i: (nb - 1 - i, cb0 + j)),
                  pl.BlockSpec((SUBLANES, B_W), lambda j, i: (jnp.maximum((nb - 1 - i) * r8 - 1, 0), cb0 + j)),
                  pl.BlockSpec((CONV_K, B_W), lambda j, i: (0, j)),
                  pl.BlockSpec((R, B_W), lambda j, i: (nb - 1 - i, j)), ANY],
        out_specs=[pl.BlockSpec((R, B_W), lambda j, i: (nb - 1 - i, cb0 + j)),
                   pl.BlockSpec((CONV_K, B_W), lambda j, i: (0, j))],
        out_shape=[jax.ShapeDtypeStruct(dproj.shape, dproj.dtype), jax.ShapeDtypeStruct((CONV_K, 3 * B_W), F32)],
        input_output_aliases={4: 0},
        scratch_shapes=[pltpu.VMEM((SUBLANES, B_W), F32)],
        compiler_params=_cparams("parallel", "arbitrary"),
    )(proj, proj, conv_w, dy, dproj)


DK_SCALE = B_HEAD_DIM ** -0.5


BNN = (((2,), (1,)), ((0,), (0,)))
BNT = (((2,), (2,)), ((0,), (0,)))
BTN = (((1,), (1,)), ((0,), (0,)))


def _tri_inv(a):
    C = a.shape[-1]
    ri = lax.broadcasted_iota(jnp.int32, (C, C), 0)
    ci = lax.broadcasted_iota(jnp.int32, (C, C), 1)
    x = jnp.where(ri == ci, 1.0, 0.0)[None] - a
    p = _dotf(a, a, BNN)
    span = 2
    while span < C:
        dot = _dotf if span <= 4 else _dot
        x = x + dot(x, p, BNN)
        span *= 2
        if span < C:
            p = dot(p, p, BNN)
    return x


def _dn_chunk(qc, kc, v, gcol, grow, bcol, s0, tm=None):
    C = B_CHUNK
    ri = lax.broadcasted_iota(jnp.int32, (C, C), 0)
    ci = lax.broadcasted_iota(jnp.int32, (C, C), 1)
    incl, strict = (ri >= ci)[None], (ri > ci)[None]
    rq = lax.rsqrt(jnp.sum(qc * qc, axis=-1, keepdims=True) + NORM_EPS)
    rk = lax.rsqrt(jnp.sum(kc * kc, axis=-1, keepdims=True) + NORM_EPS)
    qn = qc * rq
    q = qn * DK_SCALE
    k = kc * rk
    gc_col = jnp.sum(jnp.where(incl, grow, 0.0), axis=2, keepdims=True)
    gc_row = jnp.sum(jnp.where((ri <= ci)[None], gcol, 0.0), axis=1, keepdims=True)
    gl = jnp.sum(gcol, axis=1, keepdims=True)
    dincl = jnp.where(incl, jnp.exp(jnp.where(incl, gc_col - gc_row, 0.0)), 0.0)
    dstrict = jnp.where(strict, dincl, 0.0)
    eg = jnp.exp(gc_col)
    ekt = jnp.exp(gl - gc_col)
    egl = jnp.exp(gl)
    kb = k * bcol
    vb = v * bcol
    kbg = kb * eg
    a = _dot(kb, k, BNT) * dstrict
    if tm is None:
        tm = _tri_inv(a)
    u = _dot(tm, vb, BNN)
    w = _dot(tm, kbg, BNN)
    vn = u - _dot(w, s0, BNN)
    qk = _dot(q, k, BNT) * dincl
    qg = q * eg
    kt = k * ekt
    o = _dot(qg, s0, BNN) + _dot(qk, vn, BNN)
    s1 = s0 * egl + _dot(kt, vn, BTN)
    return dict(rq=rq, rk=rk, qn=qn, q=q, k=k, dincl=dincl, dstrict=dstrict, eg=eg, ekt=ekt, egl=egl, kb=kb, vb=vb,
                kbg=kbg, a=a, tm=tm, w=w, vn=vn, qk=qk, qg=qg, kt=kt, o=o, s1=s1, ri=ri[None], ci=ci[None])


def _heads(ref):
    return jnp.stack([ref[:, h * B_HEAD_DIM:(h + 1) * B_HEAD_DIM] for h in range(B_HEADS)])


def _store_heads(ref, val):
    for h in range(B_HEADS):
        ref[:, h * B_HEAD_DIM:(h + 1) * B_HEAD_DIM] = val[h]


def _dn_specs(N, rev):
    ix = (lambda n: N - 1 - n) if rev else (lambda n: n)
    wide = lambda cb: pl.BlockSpec((B_CHUNK, B_W), lambda n: (ix(n), cb))
    return dict(
        q=wide(0), k=wide(1), v=wide(2), z=wide(HP_Z // B_W), dob=wide(A_Q_W // B_W), out=wide(0),
        nw=pl.BlockSpec((1, LANES), lambda n: (0, 0)),
        row=pl.BlockSpec((B_HEADS, None, 1, B_CHUNK), lambda n: (0, ix(n), 0, 0)),
        state=pl.BlockSpec((B_HEADS, None, B_HEAD_DIM, B_HEAD_DIM), lambda n: (0, ix(n), 0, 0)),
        inv=pl.BlockSpec((B_HEADS, None, B_CHUNK, B_CHUNK), lambda n: (0, ix(n), 0, 0)),
    )


def _to_col(row):
    C = row.shape[-1]
    eye = lax.broadcasted_iota(jnp.int32, (C, C), 0) == lax.broadcasted_iota(jnp.int32, (C, C), 1)
    return jnp.sum(jnp.where(eye[None], row, 0.0), axis=2, keepdims=True)


def _to_row(col):
    C = col.shape[1]
    eye = lax.broadcasted_iota(jnp.int32, (C, C), 0) == lax.broadcasted_iota(jnp.int32, (C, C), 1)
    return jnp.sum(jnp.where(eye[None], col, 0.0), axis=1, keepdims=True)


def _dn_fwd(qkvc, proj, norm_w, grow, brow, name):
    T = qkvc.shape[0]
    N = T // B_CHUNK
    sp = _dn_specs(N, False)

    def body(q_ref, k_ref, v_ref, z_ref, nw_ref, gr_ref, br_ref, o_ref, st_ref, tm_ref, s_ref):
        n = pl.program_id(0)

        @pl.when(n == 0)
        def _():
            s_ref[...] = jnp.zeros_like(s_ref)

        s0 = s_ref[...]
        st_ref[...] = s0
        grow_v = gr_ref[...]
        f = _dn_chunk(_heads(q_ref), _heads(k_ref), _heads(v_ref), _to_col(grow_v), grow_v, _to_col(br_ref[...]), s0)
        o = f["o"]
        r = lax.rsqrt(jnp.mean(o * o, axis=-1, keepdims=True) + NORM_EPS)
        _store_heads(o_ref, o * r * nw_ref[...][None] * _silu(_heads(z_ref)))
        s_ref[...] = f["s1"]
        tm_ref[...] = f["tm"]

    return pl.pallas_call(
        body, name=name, grid=(N,),
        in_specs=[sp["q"], sp["k"], sp["v"], sp["z"], sp["nw"], sp["row"], sp["row"]],
        out_specs=[sp["out"], sp["state"], sp["inv"]],
        out_shape=[jax.ShapeDtypeStruct((T, B_W), F32),
                   jax.ShapeDtypeStruct((B_HEADS, N, B_HEAD_DIM, B_HEAD_DIM), F32),
                   jax.ShapeDtypeStruct((B_HEADS, N, B_CHUNK, B_CHUNK), F32)],
        scratch_shapes=[pltpu.VMEM((B_HEADS, B_HEAD_DIM, B_HEAD_DIM), F32)],
        compiler_params=_cparams("arbitrary"),
    )(qkvc, qkvc, qkvc, proj, norm_w, grow, brow)


def _dn_bwd(qkvc, proj, norm_w, grow, brow, states, invs, dmix, dproj, name):
    T = qkvc.shape[0]
    N = T // B_CHUNK
    sp = _dn_specs(N, True)
    C = B_CHUNK

    def body(q_ref, k_ref, v_ref, z_ref, nw_ref, gr_ref, br_ref, st_ref, tm_ref, dob_ref, dproj_in,
             dqkv_ref, dz_ref, dg_ref, db_ref, dnw_ref, ds_ref):
        n = pl.program_id(0)
        dq_ref = dqkv_ref.at[:, 0:B_W]
        dk_ref = dqkv_ref.at[:, B_W:2 * B_W]
        dv_ref = dqkv_ref.at[:, 2 * B_W:3 * B_W]

        @pl.when(n == 0)
        def _():
            ds_ref[...] = jnp.zeros_like(ds_ref)
            dnw_ref[...] = jnp.zeros_like(dnw_ref)

        s0 = st_ref[...]
        ds1 = ds_ref[...]
        v, z, nw, bcol_v = _heads(v_ref), _heads(z_ref), nw_ref[...][None], _to_col(br_ref[...])
        grow_v = gr_ref[...]
        f = _dn_chunk(_heads(q_ref), _heads(k_ref), v, _to_col(grow_v), grow_v, bcol_v, s0, tm=tm_ref[...])
        o, q, k, qn = f["o"], f["q"], f["k"], f["qn"]
        eg, ekt, egl = f["eg"], f["ekt"], f["egl"]
        tm, w, vn, kb, vb, kbg = f["tm"], f["w"], f["vn"], f["kb"], f["vb"], f["kbg"]
        qg, kt, qk, a = f["qg"], f["kt"], f["qk"], f["a"]
        ri, ci = f["ri"], f["ci"]

        dob_v = _heads(dob_ref)
        r = lax.rsqrt(jnp.mean(o * o, axis=-1, keepdims=True) + NORM_EPS)
        sz = _silu(z)
        on = o * r
        dnw_ref[...] += jnp.sum(dob_v * sz * on, axis=1, keepdims=True)
        _store_heads(dz_ref, (dob_v * on * nw * _dsilu(z)).astype(ACT_DTYPE))
        d_on = dob_v * sz * nw
        do = r * (d_on - on * jnp.mean(d_on * on, axis=-1, keepdims=True))

        dvn = _dot(qk, do, BTN) + _dot(kt, ds1, BNN)
        dqk = _dot(do, vn, BNT)
        dqg = _dot(do, s0, BNT)
        ds_ref[...] = _dot(qg, do, BTN) + egl * ds1 - _dot(w, dvn, BTN)
        dgl = jnp.sum(s0 * ds1, axis=(1, 2), keepdims=True) * egl
        dkt = _dot(vn, ds1, BNT)
        dw = -_dot(dvn, s0, BNT)
        dq = dqg * eg
        dgc = jnp.sum(dqg * qg, axis=-1, keepdims=True)
        dk = dkt * ekt
        t_kt = jnp.sum(dkt * kt, axis=-1, keepdims=True)
        dgl = dgl + jnp.sum(t_kt, axis=1, keepdims=True)
        dgc = dgc - t_kt
        dqkr = dqk * f["dincl"]
        dq = dq + _dot(dqkr, k, BNN)
        dk = dk + _dot(dqkr, q, BTN)
        e_qk = dqk * qk
        dgc = dgc + jnp.sum(e_qk, axis=-1, keepdims=True)
        dgc_row = -jnp.sum(e_qk, axis=1, keepdims=True)
        dtm = _dot(dvn, vb, BNT) + _dot(dw, kbg, BNT)
        dvb = _dot(tm, dvn, BTN)
        dkbg = _dot(tm, dw, BTN)
        dkb = dkbg * eg
        dgc = dgc + jnp.sum(dkbg * kbg, axis=-1, keepdims=True)
        da = -_dotf(tm, _dotf(dtm, tm, BNT), BTN)
        dkk = da * f["dstrict"]
        e_a = da * a
        dgc = dgc + jnp.sum(e_a, axis=-1, keepdims=True)
        dgc_row = dgc_row - jnp.sum(e_a, axis=1, keepdims=True)
        dkb = dkb + _dot(dkk, k, BNN)
        dk = dk + _dot(dkk, kb, BTN)
        dk = dk + dkb * bcol_v
        db_ref[...] = _to_row(jnp.sum(dkb * k, axis=-1, keepdims=True) + jnp.sum(dvb * v, axis=-1, keepdims=True))
        _store_heads(dv_ref, dvb * bcol_v)
        dgc_row = dgc_row + jnp.sum(jnp.where(ri == ci, dgc, 0.0), axis=1, keepdims=True)
        dg_ref[...] = jnp.sum(jnp.where(ci <= ri, _to_col(dgc_row), 0.0), axis=1, keepdims=True) + dgl
        dqs = dq * DK_SCALE
        _store_heads(dq_ref, f["rq"] * (dqs - qn * jnp.sum(dqs * qn, axis=-1, keepdims=True)))
        _store_heads(dk_ref, f["rk"] * (dk - k * jnp.sum(dk * k, axis=-1, keepdims=True)))

    return pl.pallas_call(
        body, name=name, grid=(N,),
        in_specs=[sp["q"], sp["k"], sp["v"], sp["z"], sp["nw"], sp["row"], sp["row"], sp["state"], sp["inv"], sp["dob"],
                  ANY],
        out_specs=[pl.BlockSpec((C, 3 * B_W), lambda n: (N - 1 - n, 0)), sp["z"], sp["row"], sp["row"],
                   pl.BlockSpec((B_HEADS, 1, LANES), lambda n: (0, 0, 0))],
        out_shape=[jax.ShapeDtypeStruct((T, 3 * B_W), F32), jax.ShapeDtypeStruct(dproj.shape, dproj.dtype),
                   jax.ShapeDtypeStruct((B_HEADS, N, 1, C), F32), jax.ShapeDtypeStruct((B_HEADS, N, 1, C), F32),
                   jax.ShapeDtypeStruct((B_HEADS, 1, LANES), F32)],
        input_output_aliases={10: 1},
        scratch_shapes=[pltpu.VMEM((B_HEADS, B_HEAD_DIM, B_HEAD_DIM), F32)],
        compiler_params=_cparams("arbitrary"),
    )(qkvc, qkvc, qkvc, proj, norm_w, grow, brow, states, invs, dmix, dproj)


def _lru_gates(xc, wa_ref, wx_ref, ba, bx, sp):
    pre_r, pre_i = [], []
    for hb in range(LRU_BLOCKS):
        xb = xc[:, hb * LRU_BLOCK_W:(hb + 1) * LRU_BLOCK_W]
        pre_r.append(_dot(xb, wa_ref[hb]))
        pre_i.append(_dot(xb, wx_ref[hb]))
    r = _sigmoid(jnp.concatenate(pre_r, axis=1) + ba)
    i = _sigmoid(jnp.concatenate(pre_i, axis=1) + bx)
    la = -LRU_C * r * sp
    a = jnp.exp(la)
    th = jnp.tanh(la)
    s = jnp.sqrt(-2.0 * th / (1.0 - th))
    return r, i, a, s


def _scan_down(a, b, h_in):
    R, W = a.shape
    a = a.reshape(R // SUBLANES, SUBLANES, W)
    b = b.reshape(R // SUBLANES, SUBLANES, W)
    sub = lax.broadcasted_iota(jnp.int32, a.shape, 1)
    d = 1
    while d < SUBLANES:
        ok = sub >= d
        b = a * jnp.where(ok, pltpu.roll(b, d, axis=1), 0.0) + b
        a = a * jnp.where(ok, pltpu.roll(a, d, axis=1), 1.0)
        d *= 2
    out, last = [], h_in
    for g in range(R // SUBLANES):
        h = b[g] + a[g] * last
        out.append(h)
        last = h[SUBLANES - 1:SUBLANES]
    return jnp.concatenate(out, axis=0)


def _scan_up(a, b, l_in):
    R, W = a.shape
    a = a.reshape(R // SUBLANES, SUBLANES, W)
    b = b.reshape(R // SUBLANES, SUBLANES, W)
    sub = lax.broadcasted_iota(jnp.int32, a.shape, 1)
    d = 1
    while d < SUBLANES:
        ok = sub < SUBLANES - d
        b = a * jnp.where(ok, pltpu.roll(b, SUBLANES - d, axis=1), 0.0) + b
        a = a * jnp.where(ok, pltpu.roll(a, SUBLANES - d, axis=1), 1.0)
        d *= 2
    out, nxt = [], l_in
    for g in range(R // SUBLANES - 1, -1, -1):
        lam = b[g] + a[g] * nxt
        out.append(lam)
        nxt = lam[0:1]
    return jnp.concatenate(out[::-1], axis=0)


def _rglru_fwd(proj, conv_w, conv_b, wa, wx, ba, bx, sp, name):
    T = proj.shape[0]
    R = _tile(T, 256)
    W = D_MODEL

    def body(p_ref, cw_ref, cb_ref, wa_ref, wx_ref, ba_ref, bx_ref, sp_ref, hg_ref, h_ref, prev_ref, hc_ref):
        i = pl.program_id(0)

        @pl.when(i == 0)
        def _():
            prev_ref[...] = jnp.zeros_like(prev_ref)
            hc_ref[...] = jnp.zeros_like(hc_ref)

        xr = p_ref[:, :W]
        gate = p_ref[:, W:]
        xc = _conv(xr, prev_ref[...], cw_ref[...]) + cb_ref[...]
        prev_ref[...] = xr[R - SUBLANES:]
        r, ig, a, s = _lru_gates(xc, wa_ref, wx_ref, ba_ref[...], bx_ref[...], sp_ref[...])
        h = _scan_down(a, s * ig * xc, hc_ref[SUBLANES - 1:SUBLANES, :])
        h_ref[...] = h
        hg_ref[...] = (h * _gelu(gate)).astype(ACT_DTYPE)
        hc_ref[...] = h[R - SUBLANES:]

    vec = pl.BlockSpec((1, W), lambda i: (0, 0))
    wsp = pl.BlockSpec((LRU_BLOCKS, LRU_BLOCK_W, LRU_BLOCK_W), lambda i: (0, 0, 0))
    row = pl.BlockSpec((R, W), lambda i: (i, 0))
    return pl.pallas_call(
        body, name=name, grid=(T // R,),
        in_specs=[pl.BlockSpec((R, 2 * W), lambda i: (i, 0)), pl.BlockSpec((CONV_K, W), lambda i: (0, 0)),
                  vec, wsp, wsp, vec, vec, vec],
        out_specs=[row, row],
        out_shape=[jax.ShapeDtypeStruct((T, W), ACT_DTYPE), jax.ShapeDtypeStruct((T, W), F32)],
        scratch_shapes=[pltpu.VMEM((SUBLANES, W), F32), pltpu.VMEM((SUBLANES, W), F32)],
        compiler_params=_cparams("arbitrary"),
    )(proj, conv_w, conv_b, wa, wx, ba, bx, sp)


def _rglru_bwd(proj, conv_w, conv_b, wa, wx, ba, bx, sp, h, dhg, name):
    T = proj.shape[0]
    R = _tile(T, 256)
    nb = T // R
    r8 = R // SUBLANES
    W = D_MODEL

    def body(p_ref, pp_ref, cw_ref, cb_ref, wa_ref, wx_ref, ba_ref, bx_ref, sp_ref, h_ref, hp_ref, dhg_ref,
             dp_ref, dcw_ref, dcb_ref, dwa_ref, dwx_ref, dba_ref, dbx_ref, dsp_ref, lam_ref, nxt_ref):
        step = pl.program_id(0)
        blk = nb - 1 - step

        @pl.when(step == 0)
        def _():
            lam_ref[...] = jnp.zeros_like(lam_ref)
            nxt_ref[...] = jnp.zeros_like(nxt_ref)

        xr = p_ref[:, :W]
        gate = p_ref[:, W:]
        first = blk > 0
        prev8 = jnp.where(first, pp_ref[:, :W], 0.0)
        hprev8 = jnp.where(first, hp_ref[...], 0.0)
        cw = cw_ref[...]
        spv = sp_ref[...]
        xc = _conv(xr, prev8, cw) + cb_ref[...]
        r, ig, a, s = _lru_gates(xc, wa_ref, wx_ref, ba_ref[...], bx_ref[...], spv)
        hv = h_ref[...]
        dhg_v = dhg_ref[...]
        dgate = dhg_v * hv * _dgelu(gate)
        dh = dhg_v * _gelu(gate)
        row = lax.broadcasted_iota(jnp.int32, (R, W), 0)
        last = row == R - 1
        a_up = jnp.where(last, 0.0, pltpu.roll(a, R - 1, axis=0))
        lam = _scan_up(a_up, dh + jnp.where(last, lam_ref[0:1, :], 0.0), jnp.zeros((1, W), F32))
        lam_ref[...] = (a * lam)[:SUBLANES]
        h_dn = _shift_down(hv, hprev8, 1)
        da = lam * h_dn
        bx_in = ig * xc
        dsv = lam * bx_in
        dig = lam * s * xc
        dxc = lam * s * ig
        dla = da * a - dsv * (a * a) / s
        dr = dla * (-LRU_C) * spv
        dsp = jnp.sum(dla * (-LRU_C) * r, axis=0, keepdims=True)
        dpr = dr * r * (1.0 - r)
        dpi = dig * ig * (1.0 - ig)
        dxc_parts, dwa_parts, dwx_parts = [], [], []
        for hb in range(LRU_BLOCKS):
            sl = slice(hb * LRU_BLOCK_W, (hb + 1) * LRU_BLOCK_W)
            xb, gr, gi = xc[:, sl], dpr[:, sl], dpi[:, sl]
            dxc_parts.append(_dot(gr, wa_ref[hb], NT) + _dot(gi, wx_ref[hb], NT))
            dwa_parts.append(_dot(xb, gr, TN))
            dwx_parts.append(_dot(xb, gi, TN))
        dxc = dxc + jnp.concatenate(dxc_parts, axis=1)
        dxr, dcw = _conv_bwd(xr, cw, dxc, nxt_ref[...])
        nxt_ref[...] = dxc[:SUBLANES]
        dp_ref[:, :W] = dxr.astype(ACT_DTYPE)
        dp_ref[:, W:] = dgate.astype(ACT_DTYPE)
        dcb = jnp.sum(dxc, axis=0, keepdims=True)
        dba = jnp.sum(dpr, axis=0, keepdims=True)
        dbx = jnp.sum(dpi, axis=0, keepdims=True)

        @pl.when(step == 0)
        def _():
            for j in range(CONV_K):
                dcw_ref[j:j + 1, :] = dcw[j]
            dcb_ref[...] = dcb
            dba_ref[...] = dba
            dbx_ref[...] = dbx
            dsp_ref[...] = dsp
            for hb in range(LRU_BLOCKS):
                dwa_ref[hb] = dwa_parts[hb]
                dwx_ref[hb] = dwx_parts[hb]

        @pl.when(step > 0)
        def _():
            for j in range(CONV_K):
                dcw_ref[j:j + 1, :] += dcw[j]
            dcb_ref[...] += dcb
            dba_ref[...] += dba
            dbx_ref[...] += dbx
            dsp_ref[...] += dsp
            for hb in range(LRU_BLOCKS):
                dwa_ref[hb] += dwa_parts[hb]
                dwx_ref[hb] += dwx_parts[hb]

    rv = lambda i: nb - 1 - i
    before = lambda i: jnp.maximum((nb - 1 - i) * r8 - 1, 0)
    vec = pl.BlockSpec((1, W), lambda i: (0, 0))
    cws = pl.BlockSpec((CONV_K, W), lambda i: (0, 0))
    wsp = pl.BlockSpec((LRU_BLOCKS, LRU_BLOCK_W, LRU_BLOCK_W), lambda i: (0, 0, 0))
    row = pl.BlockSpec((R, W), lambda i: (rv(i), 0))
    wshape = jax.ShapeDtypeStruct((LRU_BLOCKS, LRU_BLOCK_W, LRU_BLOCK_W), F32)
    vshape = jax.ShapeDtypeStruct((1, W), F32)
    return pl.pallas_call(
        body, name=name, grid=(nb,),
        in_specs=[pl.BlockSpec((R, 2 * W), lambda i: (rv(i), 0)), pl.BlockSpec((SUBLANES, 2 * W), lambda i: (before(i), 0)),
                  cws, vec, wsp, wsp, vec, vec, vec, row, pl.BlockSpec((SUBLANES, W), lambda i: (before(i), 0)), row],
        out_specs=[pl.BlockSpec((R, 2 * W), lambda i: (rv(i), 0)), cws, vec, wsp, wsp, vec, vec, vec],
        out_shape=[jax.ShapeDtypeStruct((T, 2 * W), ACT_DTYPE), jax.ShapeDtypeStruct((CONV_K, W), F32), vshape,
                   wshape, wshape, vshape, vshape, vshape],
        scratch_shapes=[pltpu.VMEM((SUBLANES, W), F32), pltpu.VMEM((SUBLANES, W), F32)],
        compiler_params=_cparams("arbitrary"),
    )(proj, proj, conv_w, conv_b, wa, wx, ba, bx, sp, h, h, dhg)


MESH = pl.DeviceIdType.MESH
ANY = pl.BlockSpec(memory_space=pl.ANY)


def _position():
    x, y, c = lax.axis_index("x"), lax.axis_index("y"), lax.axis_index("c")
    other_chips = [(1 - x, y), (x, 1 - y), (1 - x, 1 - y)]
    return x, y, c, other_chips


def _all_gather_weights(shards, name):
    n = len(shards)

    def body(*refs):
        ins, outs = refs[:n], refs[n:2 * n]
        send_sems, recv_sems = refs[2 * n:]
        x, y, c, chips = _position()
        me = 2 * x + y
        sibling = (x, y, 1 - c)

        def rcopy(t, k, src, dst, to):
            return pltpu.make_async_remote_copy(src_ref=src, dst_ref=dst, send_sem=send_sems.at[t, k],
                                                recv_sem=recv_sems.at[t, k], device_id=to, device_id_type=MESH)

        started = []
        for t in range(n):
            for j, (cx, cy) in enumerate(chips):
                cp = rcopy(t, j, ins[t].at[c], outs[t].at[me, c], (cx, cy, c))
                cp.start()
                started.append(cp)
        for t in range(n):
            for j, (cx, cy) in enumerate(chips):
                blk = outs[t].at[2 * cx + cy, c]
                rcopy(t, j, blk, blk, (cx, cy, c)).wait_recv()
                cp = rcopy(t, 3 + j, blk, blk, sibling)
                cp.start()
                started.append(cp)
        for t in range(n):
            for j, (cx, cy) in enumerate(chips):
                blk = outs[t].at[2 * cx + cy, 1 - c]
                rcopy(t, 3 + j, blk, blk, sibling).wait_recv()
        for cp in started:
            cp.wait_send()

    return pl.pallas_call(
        body, name=name, in_specs=[ANY] * n, out_specs=[ANY] * n,
        out_shape=[jax.ShapeDtypeStruct((N_CHIPS,) + s.shape, s.dtype) for s in shards],
        scratch_shapes=[pltpu.SemaphoreType.DMA((n, 6)), pltpu.SemaphoreType.DMA((n, 6))],
    )(*shards)


HBM = pl.BlockSpec(memory_space=pltpu.HBM)
SEM = pl.BlockSpec(memory_space=pltpu.SEMAPHORE)
EFFECT = pltpu.SideEffectType.DATAFLOW_SIDE_EFFECTING


def _gather_start(own, land, after, name):
    n = len(own)

    def body(*refs):
        own_refs, land_refs = refs[:n], refs[n:2 * n]
        send_sems = refs[2 * n + 1:3 * n + 1]
        recv_sems = refs[3 * n + 1:4 * n + 1]
        x, y, c, chips = _position()
        me = 2 * x + y
        for t in range(n):
            for cx, cy in chips:
                pltpu.make_async_remote_copy(
                    src_ref=own_refs[t], dst_ref=land_refs[t].at[me], send_sem=send_sems[t], recv_sem=recv_sems[t],
                    device_id=(cx, cy, c), device_id_type=MESH).start()

    sems = (pltpu.SemaphoreType.DMA(()),) * (2 * n)
    thru = [pltpu.HBM(a.shape, a.dtype) for a in list(own) + list(land)]
    out = pl.pallas_call(
        body, name=name, out_shape=(*sems, *thru),
        in_specs=[HBM] * (2 * n) + [pl.BlockSpec(memory_space=pl.ANY)], out_specs=(SEM,) * (2 * n) + (HBM,) * (2 * n),
        input_output_aliases={i: 2 * n + i for i in range(2 * n)},
        compiler_params=pltpu.CompilerParams(has_side_effects=EFFECT),
    )(*[pltpu.with_memory_space_constraint(a, pltpu.HBM) for a in list(own) + list(land)], after)
    return list(out[:n]), list(out[n:2 * n]), list(out[2 * n:3 * n]), list(out[3 * n:])


def _gather_wait(send_sems, recv_sems, own, land, after, name):
    n = len(own)

    def body(*refs):
        land_refs = refs[n:2 * n]
        s_sems, r_sems = refs[2 * n:3 * n], refs[3 * n:4 * n]
        x, y, c, _ = _position()
        for t in range(n):
            three = land_refs[t].at[pl.ds(0, N_CHIPS - 1)]
            cp = pltpu.make_async_remote_copy(src_ref=three, dst_ref=three, send_sem=s_sems[t], recv_sem=r_sems[t],
                                              device_id=(x, y, c), device_id_type=MESH)
            cp.wait_send()
            cp.wait_recv()

    thru = [pltpu.HBM(a.shape, a.dtype) for a in list(own) + list(land)]
    out = pl.pallas_call(
        body, name=name, out_shape=tuple(thru),
        in_specs=[HBM] * (2 * n) + [SEM] * (2 * n) + [pl.BlockSpec(memory_space=pl.ANY)], out_specs=(HBM,) * (2 * n),
        input_output_aliases={i: i for i in range(2 * n)},
        compiler_params=pltpu.CompilerParams(has_side_effects=EFFECT),
    )(*own, *land, *send_sems, *recv_sems, after)
    return list(out[n:])


def _rs_to_sibling(grads, name):
    n = len(grads)

    def body(*refs):
        ins, outs = refs[:n], refs[n:2 * n]
        send_sems, recv_sems = refs[2 * n:]
        x, y, c, _ = _position()
        cps = [pltpu.make_async_remote_copy(src_ref=ins[t].at[:, 1 - c], dst_ref=outs[t], send_sem=send_sems.at[t],
                                            recv_sem=recv_sems.at[t], device_id=(x, y, 1 - c), device_id_type=MESH)
               for t in range(n)]
        for cp in cps:
            cp.start()
        for cp in cps:
            cp.wait()

    return pl.pallas_call(
        body, name=name, in_specs=[ANY] * n, out_specs=[ANY] * n,
        out_shape=[jax.ShapeDtypeStruct((N_CHIPS,) + g.shape[2:], g.dtype) for g in grads],
        scratch_shapes=[pltpu.SemaphoreType.DMA((n,)), pltpu.SemaphoreType.DMA((n,))],
    )(*grads)


def _rs_sibling_start(grads, recv, name):
    n = len(grads)

    def body(*refs):
        grad_refs, recv_refs = refs[:n], refs[n:2 * n]
        send_sems, recv_sems = refs[2 * n:3 * n], refs[3 * n:4 * n]
        token_ref = refs[-1]
        x, y, c, _ = _position()
        for t in range(n):
            pltpu.make_async_remote_copy(src_ref=grad_refs[t].at[:, 1 - c], dst_ref=recv_refs[t], send_sem=send_sems[t],
                                         recv_sem=recv_sems[t], device_id=(x, y, 1 - c), device_id_type=MESH).start()
        token_ref[...] = jnp.zeros_like(token_ref)

    sems = (pltpu.SemaphoreType.DMA(()),) * (2 * n)
    thru = [pltpu.HBM(a.shape, a.dtype) for a in list(grads) + list(recv)]
    out = pl.pallas_call(
        body, name=name, out_shape=(*sems, *thru, jax.ShapeDtypeStruct((SUBLANES, LANES), F32)),
        in_specs=[HBM] * (2 * n), out_specs=(SEM,) * (2 * n) + (HBM,) * (2 * n) + (pl.BlockSpec(memory_space=pltpu.VMEM),),
        input_output_aliases={i: 2 * n + i for i in range(2 * n)},
        compiler_params=pltpu.CompilerParams(has_side_effects=EFFECT),
    )(*[pltpu.with_memory_space_constraint(a, pltpu.HBM) for a in list(grads) + list(recv)])
    return out[:n], out[n:2 * n], list(out[2 * n:3 * n]), list(out[3 * n:4 * n]), out[-1]


def _rs_sibling_wait(send_sems, recv_sems, grads, recv, after, name):
    n = len(grads)

    def body(*refs):
        recv_refs = refs[n:2 * n]
        s_sems, r_sems = refs[2 * n:3 * n], refs[3 * n:4 * n]
        x, y, c, _ = _position()
        for t in range(n):
            cp = pltpu.make_async_remote_copy(src_ref=recv_refs[t], dst_ref=recv_refs[t], send_sem=s_sems[t],
                                              recv_sem=r_sems[t], device_id=(x, y, c), device_id_type=MESH)
            cp.wait_send()
            cp.wait_recv()

    thru = [pltpu.HBM(a.shape, a.dtype) for a in list(grads) + list(recv)]
    out = pl.pallas_call(
        body, name=name, out_shape=tuple(thru),
        in_specs=[HBM] * (2 * n) + [SEM] * (2 * n) + [pl.BlockSpec(memory_space=pl.ANY)], out_specs=(HBM,) * (2 * n),
        input_output_aliases={i: i for i in range(2 * n)},
        compiler_params=pltpu.CompilerParams(has_side_effects=EFFECT),
    )(*grads, *recv, *send_sems, *recv_sems, after)
    return list(out[:n]), list(out[n:])


def _rs_across_start(parts, recv, name):
    n = len(parts)

    def body(*refs):
        part_refs, recv_refs = refs[:n], refs[n:2 * n]
        send_sems, recv_sems = refs[2 * n:3 * n], refs[3 * n:4 * n]
        token_ref = refs[-1]
        x, y, c, chips = _position()
        me = 2 * x + y
        for t in range(n):
            for cx, cy in chips:
                pltpu.make_async_remote_copy(src_ref=part_refs[t].at[2 * cx + cy], dst_ref=recv_refs[t].at[me],
                                             send_sem=send_sems[t], recv_sem=recv_sems[t], device_id=(cx, cy, c),
                                             device_id_type=MESH).start()
        token_ref[...] = jnp.zeros_like(token_ref)

    sems = (pltpu.SemaphoreType.DMA(()),) * (2 * n)
    thru = [pltpu.HBM(a.shape, a.dtype) for a in list(parts) + list(recv)]
    out = pl.pallas_call(
        body, name=name, out_shape=(*sems, *thru, jax.ShapeDtypeStruct((SUBLANES, LANES), F32)),
        in_specs=[HBM] * (2 * n), out_specs=(SEM,) * (2 * n) + (HBM,) * (2 * n) + (pl.BlockSpec(memory_space=pltpu.VMEM),),
        input_output_aliases={i: 2 * n + i for i in range(2 * n)},
        compiler_params=pltpu.CompilerParams(has_side_effects=EFFECT),
    )(*[pltpu.with_memory_space_constraint(a, pltpu.HBM) for a in list(parts) + list(recv)])
    return out[:n], out[n:2 * n], list(out[2 * n:3 * n]), list(out[3 * n:4 * n]), out[-1]


def _rs_across_wait(send_sems, recv_sems, parts, recv, after, name):
    n = len(parts)

    def body(*refs):
        recv_refs = refs[n:2 * n]
        s_sems, r_sems = refs[2 * n:3 * n], refs[3 * n:4 * n]
        x, y, c, _ = _position()
        for t in range(n):
            three = recv_refs[t].at[pl.ds(0, N_CHIPS - 1)]
            cp = pltpu.make_async_remote_copy(src_ref=three, dst_ref=three, send_sem=s_sems[t], recv_sem=r_sems[t],
                                              device_id=(x, y, c), device_id_type=MESH)
            cp.wait_send()
            cp.wait_recv()

    thru = [pltpu.HBM(a.shape, a.dtype) for a in list(parts) + list(recv)]
    out = pl.pallas_call(
        body, name=name, out_shape=tuple(thru),
        in_specs=[HBM] * (2 * n) + [SEM] * (2 * n) + [pl.BlockSpec(memory_space=pl.ANY)], out_specs=(HBM,) * (2 * n),
        input_output_aliases={i: i for i in range(2 * n)},
        compiler_params=pltpu.CompilerParams(has_side_effects=EFFECT),
    )(*parts, *recv, *send_sems, *recv_sems, after)
    return list(out[:n]), list(out[n:])


def _rs_join_halves(halves, name):
    n = len(halves)

    def body(*refs):
        ins, outs = refs[:n], refs[n:2 * n]
        send_sems, recv_sems = refs[2 * n:]
        x, y, c, _ = _position()
        cps = [pltpu.make_async_remote_copy(src_ref=ins[t].at[c], dst_ref=outs[t].at[c], send_sem=send_sems.at[t],
                                            recv_sem=recv_sems.at[t], device_id=(x, y, 1 - c), device_id_type=MESH)
               for t in range(n)]
        for cp in cps:
            cp.start()
        for t in range(n):
            blk = outs[t].at[1 - c]
            pltpu.make_async_remote_copy(src_ref=blk, dst_ref=blk, send_sem=send_sems.at[t], recv_sem=recv_sems.at[t],
                                         device_id=(x, y, 1 - c), device_id_type=MESH).wait_recv()
        for cp in cps:
            cp.wait_send()

    return pl.pallas_call(
        body, name=name, in_specs=[ANY] * n, out_specs=[ANY] * n,
        out_shape=[jax.ShapeDtypeStruct(h.shape, h.dtype) for h in halves],
        input_output_aliases={t: t for t in range(n)},
        scratch_shapes=[pltpu.SemaphoreType.DMA((n,)), pltpu.SemaphoreType.DMA((n,))],
    )(*halves)


def _all_gather_small(block, name):
    m_per, n = block.shape

    def body(x_ref, out_ref, send_sems, recv_sems, local_sem):
        x, y, c, chips = _position()
        me, sibling = (x, y, c), (x, y, 1 - c)

        def rows(px, py, pc):
            return out_ref.at[pl.ds((4 * px + 2 * py + pc) * m_per, m_per), :]

        def copy(k, blk, to, src=None):
            return pltpu.make_async_remote_copy(
                src_ref=rows(*blk) if src is None else src, dst_ref=rows(*blk), send_sem=send_sems.at[k],
                recv_sem=recv_sems.at[k], device_id=to, device_id_type=MESH)

        mine = pltpu.make_async_copy(x_ref, rows(*me), local_sem)
        mine.start()
        first = [copy(0, me, sibling, src=x_ref)]
        first += [copy(1 + j, me, (*chip, c), src=x_ref) for j, chip in enumerate(chips)]
        for cp in first:
            cp.start()
        passed = [copy(4 + j, (*chip, c), sibling) for j, chip in enumerate(chips)]
        for j, chip in enumerate(chips):
            copy(1 + j, (*chip, c), me).wait_recv()
            passed[j].start()
        copy(0, sibling, me).wait_recv()
        for j, chip in enumerate(chips):
            copy(4 + j, (*chip, 1 - c), me).wait_recv()
        for cp in first + passed:
            cp.wait_send()
        mine.wait()

    return pl.pallas_call(
        body, name=name, out_shape=jax.ShapeDtypeStruct((N_DEV * m_per, n), block.dtype),
        in_specs=[pl.BlockSpec(memory_space=pltpu.VMEM)], out_specs=pl.BlockSpec(memory_space=pltpu.VMEM),
        scratch_shapes=[pltpu.SemaphoreType.DMA((7,)), pltpu.SemaphoreType.DMA((7,)), pltpu.SemaphoreType.DMA],
    )(block)


def _row_tile(R, n):
    budget = 1 << 19
    if R * n <= budget or R % SUBLANES:
        return R
    t = R
    while t * n > budget and t % (2 * SUBLANES) == 0:
        t //= 2
    return t


def _pair_sum(g, recv, c_arr, name):
    _, _, R, n = g.shape
    tr = _row_tile(R, n)

    def body(c_ref, g_ref, r_ref, o_ref):
        o_ref[...] = (g_ref[...] + r_ref[...]).astype(ICI_DTYPE)

    grid_spec = pltpu.PrefetchScalarGridSpec(
        num_scalar_prefetch=1, grid=(N_CHIPS, R // tr),
        in_specs=[pl.BlockSpec((None, None, tr, n), lambda p, i, c: (p, c[0], i, 0)),
                  pl.BlockSpec((None, tr, n), lambda p, i, c: (p, i, 0))],
        out_specs=pl.BlockSpec((None, tr, n), lambda p, i, c: (p, i, 0)))
    return pl.pallas_call(
        body, name=name, grid_spec=grid_spec, out_shape=jax.ShapeDtypeStruct(recv.shape, ICI_DTYPE),
        compiler_params=_cparams("parallel", "parallel"),
    )(c_arr, g, recv)


def _chip_sum(recv, own, where, name):
    _, R, n = recv.shape
    tr = _row_tile(R, n)

    def body(w_ref, r0, r1, r2, r3, own_ref, o_ref):
        me = w_ref[0]
        terms = [jnp.where(me == k, own_ref[...], r[...]).astype(F32) for k, r in enumerate((r0, r1, r2, r3))]
        o_ref[...] = ((terms[0] + terms[1]) + terms[2]) + terms[3]

    def slot(k):
        return pl.BlockSpec((None, tr, n), lambda i, w: (w[1 + k], i, 0))

    grid_spec = pltpu.PrefetchScalarGridSpec(
        num_scalar_prefetch=1, grid=(R // tr,),
        in_specs=[slot(0), slot(1), slot(2), slot(3), pl.BlockSpec((None, tr, n), lambda i, w: (w[0], i, 0))],
        out_specs=pl.BlockSpec((None, tr, n), lambda i, w: (w[5], i, 0)))
    return pl.pallas_call(
        body, name=name, grid_spec=grid_spec, out_shape=jax.ShapeDtypeStruct((2, R, n), F32),
        compiler_params=_cparams("parallel"),
    )(where, recv, recv, recv, recv, own)


ADAM_C1 = 1.0 / (1.0 - ADAM_B1 ** ADAM_STEP)
ADAM_C2 = 1.0 / (1.0 - ADAM_B2 ** ADAM_STEP)


def _adamw_math(w, g, m, v):
    m = ADAM_B1 * m + (1.0 - ADAM_B1) * g
    v = ADAM_B2 * v + (1.0 - ADAM_B2) * (g * g)
    delta = -ADAM_LR * ((m * ADAM_C1) / (jnp.sqrt(v * ADAM_C2) + ADAM_EPS) + ADAM_WD * w)
    return delta, m, v


def _adamw(w, g, m, v, name):
    R, n = w.shape
    tr = _row_tile(R, n)

    def body(w_ref, g_ref, m_ref, v_ref, d_ref, nm_ref, nv_ref):
        d_ref[...], nm_ref[...], nv_ref[...] = _adamw_math(w_ref[...], g_ref[...], m_ref[...], v_ref[...])

    spec = pl.BlockSpec((tr, n), lambda i: (i, 0))
    return pl.pallas_call(
        body, name=name, grid=(R // tr,), in_specs=[spec] * 4, out_specs=[spec] * 3,
        out_shape=[jax.ShapeDtypeStruct((R, n), F32)] * 3, compiler_params=_cparams("parallel"),
    )(w, g, m, v)


def _adamw_rows(w, g, m, v, row0, prev, name):
    R, n = w.shape
    rows = g.shape[0]
    tr = _row_tile(math.gcd(row0, rows), n)
    off = row0 // tr

    def body(*refs):
        w_ref, g_ref, m_ref, v_ref = refs[:4]
        go_ref, d_ref, nm_ref, nv_ref = refs[-4:]
        gv = g_ref[...]
        go_ref[...] = gv
        d_ref[...], nm_ref[...], nv_ref[...] = _adamw_math(w_ref[...], gv, m_ref[...], v_ref[...])

    at = pl.BlockSpec((tr, n), lambda i: (i + off, 0))
    carried = [] if prev is None else list(prev)
    return pl.pallas_call(
        body, name=name, grid=(rows // tr,),
        in_specs=[at, pl.BlockSpec((tr, n), lambda i: (i, 0)), at, at] + [ANY] * len(carried),
        out_specs=[at] * 4, out_shape=[jax.ShapeDtypeStruct((R, n), F32)] * 4,
        input_output_aliases={4 + i: i for i in range(len(carried))},
        compiler_params=_cparams("parallel"),
    )(w, g, m, v, *carried)


def _adamw_small(w, gall, m, v, name):
    M, n = w.shape

    def body(w_ref, g_ref, m_ref, v_ref, gs_ref, d_ref, nm_ref, nv_ref):
        g = g_ref[0:M, :]
        for d in range(1, N_DEV):
            g = g + g_ref[d * M:(d + 1) * M, :]
        gs_ref[...] = g
        d_ref[...], nm_ref[...], nv_ref[...] = _adamw_math(w_ref[...], g, m_ref[...], v_ref[...])

    return pl.pallas_call(
        body, name=name, out_shape=[jax.ShapeDtypeStruct((M, n), F32)] * 4,
    )(w, gall, m, v)


SMALL_ROWS = 24
MID_ROWS = 4


def _pack_small(ln1_g, ln1_b, ln2_g, ln2_b, norm_w, sinks, a_log, dt_bias):
    mixed = jnp.concatenate([norm_w.reshape(-1), sinks.reshape(-1), a_log.reshape(-1), dt_bias.reshape(-1)])
    mixed = jnp.pad(mixed, (0, D_MODEL - mixed.shape[0]))[None]
    pad = jnp.zeros((SMALL_ROWS - 4 * DEPTH - 1, D_MODEL), F32)
    return jnp.concatenate([ln1_g, ln1_b, ln2_g, ln2_b, mixed, pad], axis=0)


def _unpack_small(p):
    mixed = p[4 * DEPTH]
    return (p[0:4], p[4:8], p[8:12], p[12:16], mixed[0:256].reshape(2, 128), mixed[256:272].reshape(2, 8),
            mixed[272:280].reshape(2, 4), mixed[280:288].reshape(2, 4))


def _pack_mid(conv_w, rconv_w, rconv_b, b_a, b_x, lam):
    lead = conv_w.shape[0]
    flat = jnp.concatenate([conv_w.reshape(lead, -1), rconv_w.reshape(lead, -1), rconv_b, b_a, b_x, lam], axis=1)
    return jnp.pad(flat, ((0, 0), (0, MID_ROWS * D_MODEL - flat.shape[1]))).reshape(lead, MID_ROWS, D_MODEL)


def _unpack_mid(p):
    lead = p.shape[:-2]
    f = p.reshape(lead + (MID_ROWS * D_MODEL,))
    return (f[..., 0:1536].reshape(lead + (4, 384)), f[..., 1536:2560].reshape(lead + (4, 256)),
            f[..., 2560:2816], f[..., 2816:3072], f[..., 3072:3328], f[..., 3328:3584])


def _cols_from_chips(g):
    p, L, R, n = g.shape
    return g.transpose(1, 2, 0, 3).reshape(L, R, p * n)


def _rows_from_chips(g):
    p, L, R, n = g.shape
    return g.transpose(1, 0, 2, 3).reshape(L, p * R, n)


def _cols_to_chips(g):
    L, R, n4 = g.shape
    return g.reshape(L, R, N_CHIPS, n4 // N_CHIPS).transpose(2, 0, 1, 3)


def _rows_to_chips(g):
    L, R4, n = g.shape
    return g.reshape(L, N_CHIPS, R4 // N_CHIPS, n).transpose(1, 0, 2, 3)


def _halves(a):
    return a.reshape(2, -1, a.shape[-1])


def _pad_hyb_cols(w):
    z = jnp.zeros(w.shape[:-1] + (HYB_PAD - HP_BG - 2 * B_HEADS,), w.dtype)
    return jnp.concatenate([w[..., 0:512], w[..., 768:2304], w[..., 2304:2816], w[..., 512:768], w[..., 2816:2824], z], axis=-1)


def _unpad_hyb_cols(w):
    return jnp.concatenate([w[..., 0:512], w[..., 2560:2816], w[..., 512:2048], w[..., 2048:2560], w[..., 2816:2824]], axis=-1)


def _hybrid_fwd(x, W, j, tables, sfx, out_ready, ln):
    cos, sin_s = tables
    T = x.shape[0]
    N = T // B_CHUNK
    proj = _matmul(x, W["hyb_w_in"][j], "nn", "hyb_in" + sfx)
    sinks_b = jnp.broadcast_to(W["hyb_sinks"][j][:, None], (A_Q_HEADS, LANES))
    o_a, lse = _attn_fwd(proj, cos, sin_s, sinks_b, "attn_fwd" + sfx)
    qkvc = _dnconv_fwd(proj, W["hyb_conv_w"][j], "dnconv_fwd" + sfx)
    bg = proj[:, HP_BG:HP_BG + 2 * B_HEADS]
    beta = jax.nn.sigmoid(bg[:, :B_HEADS])
    pre = bg[:, B_HEADS:] + W["hyb_dt_bias"][j][None]
    g = -jnp.exp(W["hyb_a_log"][j])[None] * jax.nn.softplus(pre)
    grow = g.T.reshape(B_HEADS, N, 1, B_CHUNK)
    brow = beta.T.reshape(B_HEADS, N, 1, B_CHUNK)
    nw = W["hyb_norm_w"][j][None]
    o_b, states, invs = _dn_fwd(qkvc, proj, nw, grow, brow, "dn_fwd" + sfx)
    mix = jnp.concatenate([o_a, o_b], axis=1).astype(ACT_DTYPE)
    out_ready(mix)
    out = _matmul(mix, W["hyb_w_out"][j], "nn", "hyb_out" + sfx, epi=_epi_residual_ln, extra=ln, out_dtype=_ln_out(),
                  tm=512)
    res = dict(proj=proj, o_a=o_a, lse=lse, qkvc=qkvc, beta=beta, pre=pre, g=g, grow=grow, brow=brow,
               states=states, invs=invs, mix=mix, sinks_b=sinks_b, nw=nw)
    return out, res


def _hybrid_bwd(x, du, dub, W, j, res, tables, sfx):
    cos, sin_s = tables
    T = x.shape[0]
    proj = res["proj"]
    d_wout = _matmul(res["mix"], dub, "tn", "hyb_out_dw" + sfx)
    dmix = _matmul(dub, W["hyb_w_out"][j], "nt", "hyb_out_dx" + sfx)
    dproj, dkc, dkp, dvc, dvp, dsink = _attn_bwd(proj, cos, sin_s, res["sinks_b"], res["o_a"], res["lse"], dmix,
                                                  "attn_bwd" + sfx)
    zpad = jnp.zeros((WINDOW, LANES), F32)
    dk = dkc + jnp.concatenate([dkp[WINDOW:], zpad], axis=0)
    dv = dvc + jnp.concatenate([dvp[WINDOW:], zpad], axis=0)
    dqkvc, dproj, dg4, dbeta4, dnw = _dn_bwd(res["qkvc"], proj, res["nw"], res["grow"], res["brow"], res["states"],
                                             res["invs"], dmix, dproj, "dn_bwd" + sfx)
    dproj, dconv = _dnconv_bwd(proj, W["hyb_conv_w"][j], dqkvc, dproj, "dnconv_bwd" + sfx)
    dg = dg4.reshape(B_HEADS, T).T
    dbeta = dbeta4.reshape(B_HEADS, T).T
    beta = res["beta"]
    dbeta_logit = dbeta * beta * (1.0 - beta)
    da_logit = dg * (-jnp.exp(W["hyb_a_log"][j]))[None] * jax.nn.sigmoid(res["pre"])
    d_dt_bias = jnp.sum(da_logit, axis=0)
    d_a_log = jnp.sum(dg * res["g"], axis=0)
    zcols = jnp.zeros((T, HYB_PAD - HP_BG - 2 * B_HEADS), F32)
    tail = jnp.concatenate([dk, dv, dbeta_logit, da_logit, zcols], axis=1).astype(ACT_DTYPE)
    dproj = lax.dynamic_update_slice(dproj, tail, (0, HP_K))
    d_win = _matmul(x, dproj, "tn", "hyb_in_dw" + sfx)
    dx = _matmul(dproj, W["hyb_w_in"][j], "nt", "hyb_in_dx" + sfx, epi=_epi_add_residual, extra=du)
    grads = dict(hyb_w_in=d_win, hyb_w_out=d_wout, hyb_sinks=dsink[0], hyb_conv_w=dconv, hyb_a_log=d_a_log,
                 hyb_dt_bias=d_dt_bias, hyb_norm_w=jnp.sum(dnw[:, 0, :], axis=0))
    return dx, grads


def _rec_fwd(x, W, j, sfx, ln):
    proj = _matmul(x, W["rec_w_in"][j], "nn", "rec_in" + sfx)
    sp = jax.nn.softplus(-W["rec_lambda"][j])[None]
    hg, h = _rglru_fwd(proj, W["rec_conv_w"][j], W["rec_conv_b"][j][None], W["rec_w_a"][j], W["rec_w_x"][j],
                       W["rec_b_a"][j][None], W["rec_b_x"][j][None], sp, "rglru_fwd" + sfx)
    out = _matmul(hg, W["rec_w_out"][j], "nn", "rec_out" + sfx, epi=_epi_residual_ln, extra=ln, out_dtype=_ln_out(),
                  tm=512)
    return out, dict(proj=proj, hg=hg, h=h, sp=sp)


def _rec_bwd(x, du, dub, W, j, res, sfx):
    d_wout = _matmul(res["hg"], dub, "tn", "rec_out_dw" + sfx)
    dhg = _matmul(dub, W["rec_w_out"][j], "nt", "rec_out_dx" + sfx)
    dproj, dcw, dcb, dwa, dwx, dba, dbx, dsp = _rglru_bwd(
        res["proj"], W["rec_conv_w"][j], W["rec_conv_b"][j][None], W["rec_w_a"][j], W["rec_w_x"][j],
        W["rec_b_a"][j][None], W["rec_b_x"][j][None], res["sp"], res["h"], dhg, "rglru_bwd" + sfx)
    d_lam = dsp[0] * (-jax.nn.sigmoid(-W["rec_lambda"][j]))
    d_win = _matmul(x, dproj, "tn", "rec_in_dw" + sfx)
    dx = _matmul(dproj, W["rec_w_in"][j], "nt", "rec_in_dx" + sfx, epi=_epi_add_residual, extra=du)
    grads = dict(rec_w_in=d_win, rec_w_out=d_wout, rec_conv_w=dcw, rec_conv_b=dcb[0], rec_w_a=dwa, rec_w_x=dwx,
                 rec_b_a=dba[0], rec_b_x=dbx[0], rec_lambda=d_lam)
    return dx, grads


def _local_step(x, tgt, W, mlp_w, mixer_ready, on_group):
    T = x.shape[0]
    tables = _rope_tables(T)
    acts = []
    xb = x.astype(ACT_DTYPE)
    for layer in range(DEPTH):
        j, sfx = layer // 2, ""
        mixer_ready(layer, xb)
        ln1 = (x, W["ln1_g"][layer][None], W["ln1_b"][layer][None])
        if layer % 2 == 0:
            (x1, x1b, u1), res = _hybrid_fwd(xb, W, j, tables, sfx,
                                             functools.partial(mixer_ready, layer, out_projection=True), ln1)
        else:
            (x1, x1b, u1), res = _rec_fwd(xb, W, j, sfx, ln1)
        w1, w2, wl = mlp_w(layer, x1b)
        h1 = _matmul(x1b, w1, "nn", "mlp_up", out_dtype=ACT_DTYPE, b_chips=("j", wl))
        x2, x2b, u2 = _matmul(h1, w2, "nn", "mlp_down", a_fn=_relu2, b_chips=("k", wl), epi=_epi_residual_ln,
                              extra=(x1, W["ln2_g"][layer][None], W["ln2_b"][layer][None]), out_dtype=_ln_out())
        acts.append(dict(xb=xb, res=res, u1=u1, x1b=x1b, h1=h1, u2=u2))
        x, xb = x2, x2b
    dx, loss = _loss_head(x, tgt, "loss_head")
    per_layer = [None] * DEPTH
    d_w1 = dict(lower=lax.empty((N_CHIPS, 1, D_MODEL, D_FF // N_CHIPS), F32),
                upper=lax.empty((N_CHIPS, DEPTH - 1, D_MODEL, D_FF // N_CHIPS), F32))
    d_w2 = dict(lower=lax.empty((N_CHIPS, 1, D_FF // N_CHIPS, D_MODEL), F32),
                upper=lax.empty((N_CHIPS, DEPTH - 1, D_FF // N_CHIPS, D_MODEL), F32))
    token = None
    for layer in reversed(range(DEPTH)):
        j, a = layer // 2, acts[layer]
        grp, li = ("upper", layer - 1) if layer else ("lower", 0)
        ln2_g = W["ln2_g"][layer][None]
        if token is not None:
            ln2_g = ln2_g + token
        du2, du2b, dg2, db2 = _ln_bwd(a["u2"], ln2_g, dx, "ln_bwd")
        w1, w2, wl = mlp_w(layer, du2b)
        d_w2[grp] = _matmul(a["h1"], du2b, "tn", "mlp_down_dw", a_fn=_relu2, out_chips=("i", li, d_w2[grp]))
        dh1 = _matmul(du2b, w2, "nt", "mlp_down_dx", epi=_epi_drelu2, extra=a["h1"], out_dtype=ACT_DTYPE,
                      b_chips=("j", wl))
        d_w1[grp] = _matmul(a["x1b"], dh1, "tn", "mlp_up_dw", out_chips=("j", li, d_w1[grp]))
        dx1 = _matmul(dh1, w1, "nt", "mlp_up_dx", epi=_epi_add_residual, extra=du2, b_chips=("k", wl))
        ln1_g = W["ln1_g"][layer][None]
        token = on_group("middle", dx1, None, None, None) if layer == 0 else None
        if token is not None:
            ln1_g = ln1_g + token
        du1, du1b, dg1, db1 = _ln_bwd(a["u1"], ln1_g, dx1, "ln_bwd")
        if layer % 2 == 0:
            dx, g = _hybrid_bwd(a["xb"], du1, du1b, W, j, a["res"], tables, "")
        else:
            dx, g = _rec_bwd(a["xb"], du1, du1b, W, j, a["res"], "")
        g.update(ln1_g=dg1[0], ln1_b=db1[0], ln2_g=dg2[0], ln2_b=db2[0])
        per_layer[layer] = g
        if layer == 1:
            token = on_group("upper", [per_layer[2]], [per_layer[1], per_layer[3]], d_w1["upper"], d_w2["upper"])
        elif layer == 0:
            on_group("lower", [per_layer[0]], [], d_w1["lower"], d_w2["lower"])
    grads = {}
    for name in ("ln1_g", "ln1_b", "ln2_g", "ln2_b"):
        grads[name] = jnp.stack([per_layer[l][name] for l in range(DEPTH)])
    for name in ("hyb_norm_w", "hyb_sinks", "hyb_a_log", "hyb_dt_bias"):
        grads[name] = jnp.stack([per_layer[l][name] for l in (0, 2)])
    return loss, dx, grads


BIG = ("hyb_w_in", "hyb_w_out", "rec_w_in", "rec_w_out", "mlp_w1", "mlp_w2", "rec_w_a", "rec_w_x")
COL_SHARDED = ("hyb_w_in", "rec_w_in", "mlp_w1")
CHIP_MAJOR = ("mlp_w1", "mlp_w2")
MID = ("hyb_conv_w", "rec_conv_w", "rec_conv_b", "rec_b_a", "rec_b_x", "rec_lambda")
SMALL = ("ln1_g", "ln1_b", "ln2_g", "ln2_b", "hyb_norm_w", "hyb_sinks", "hyb_a_log", "hyb_dt_bias")
WEIGHTS = ("hyb_w_in", "hyb_sinks", "hyb_conv_w", "hyb_a_log", "hyb_dt_bias", "hyb_norm_w", "hyb_w_out", "rec_w_in",
           "rec_conv_w", "rec_conv_b", "rec_w_a", "rec_b_a", "rec_w_x", "rec_b_x", "rec_lambda", "rec_w_out", "ln1_g",
           "ln1_b", "mlp_w1", "mlp_w2", "ln2_g", "ln2_b")


def _gather_full_weights(w):
    wb = {k: w[k].astype(MXU_DTYPE) for k in BIG}
    now = ("hyb_w_in",)
    shards = [_halves(wb[k][:1]) for k in now]
    shards.append(_pack_mid(*[w[k] for k in MID]))
    got = _all_gather_weights(shards, "all_gather_weights")
    me = 2 * lax.axis_index("x") + lax.axis_index("y")
    got = [lax.dynamic_update_slice(g, s[None], (me, 0, 0, 0)) for s, g in zip(shards, got)]

    def full(k, g):
        g = g.reshape((N_CHIPS,) + w[k].shape[1:])
        if k in CHIP_MAJOR:
            return g[:, None]
        if k in ("rec_w_a", "rec_w_x"):
            return g.transpose(1, 0, 2, 3).reshape(LRU_BLOCKS, LRU_BLOCK_W, LRU_BLOCK_W)
        f = _cols_from_chips(g[:, None])[0] if k in COL_SHARDED else _rows_from_chips(g[:, None])[0]
        return _pad_hyb_cols(f) if k == "hyb_w_in" else f

    rec = ("rec_w_in", "rec_w_out", "rec_w_a", "rec_w_x")
    groups = [(("hyb_w_out",), 0), (CHIP_MAJOR, 0), (rec, 0), (CHIP_MAJOR, 1), (("hyb_w_in", "hyb_w_out"), 1),
              (CHIP_MAJOR, 2), (rec, 1), (CHIP_MAJOR, 3)]
    own = [wb[k][j] for names, j in groups for k in names]
    land = [lax.dynamic_update_slice(lax.empty((N_CHIPS,) + o.shape, o.dtype), o[None], (me,) + (0,) * o.ndim)
            for o in own]
    send_sems, recv_sems, own, land = _gather_start(own, land, got[-1], "gather_start")
    W = {k: [None] * w[k].shape[0] for k in BIG}
    for k, g in zip(now, got[:-1]):
        W[k][0] = full(k, g)
    arrived = [0]

    def ensure(upto, after):
        while arrived[0] <= upto:
            gi = arrived[0]
            names, j = groups[gi]
            lo = sum(len(nm) for nm, _ in groups[:gi])
            sl = slice(lo, lo + len(names))
            got_g = _gather_wait(send_sems[sl], recv_sems[sl], own[sl], land[sl], after, "gather_wait_%d" % gi)
            for k, g in zip(names, got_g):
                W[k][j] = full(k, g)
            arrived[0] += 1

    def mixer_ready(layer, after, out_projection=False):
        if layer:
            ensure({1: 2, 2: 4, 3: 6}[layer], after)
        elif out_projection:
            ensure(0, after)

    def mlp_w(layer, after):
        ensure({0: 1, 1: 3, 2: 5, 3: 7}[layer], after)
        return W["mlp_w1"][layer], W["mlp_w2"][layer], 0

    conv_w, rconv_w, rconv_b, b_a, b_x, lam = _unpack_mid(got[-1])
    W["hyb_conv_w"] = conv_w.transpose(1, 2, 0, 3).reshape(2, CONV_K, 3 * B_W)
    W["rec_conv_w"] = rconv_w.transpose(1, 2, 0, 3).reshape(2, CONV_K, D_MODEL)
    for k, v in (("rec_conv_b", rconv_b), ("rec_b_a", b_a), ("rec_b_x", b_x), ("rec_lambda", lam)):
        W[k] = v.transpose(1, 0, 2).reshape(2, D_MODEL)
    for k in SMALL:
        W[k] = w[k]
    return W, mlp_w, mixer_ready


REC_VECTORS = ("rec_conv_b", "rec_b_a", "rec_b_x", "rec_lambda")


def _group_by_chip(hyb, rec, d_w1, d_w2):
    t = dict(mlp_w1=d_w1, mlp_w2=d_w2)
    if hyb:
        t["hyb_w_in"] = _cols_to_chips(jnp.stack([_unpad_hyb_cols(g["hyb_w_in"]) for g in hyb]))
        t["hyb_w_out"] = _rows_to_chips(jnp.stack([g["hyb_w_out"] for g in hyb]))
    if rec:
        t["rec_w_in"] = _cols_to_chips(jnp.stack([g["rec_w_in"] for g in rec]))
        t["rec_w_out"] = _rows_to_chips(jnp.stack([g["rec_w_out"] for g in rec]))
        for k in ("rec_w_a", "rec_w_x"):
            v = jnp.stack([g[k] for g in rec])
            t[k] = v.reshape(len(rec), LRU_BLOCKS, N_CHIPS, LRU_BLOCK_W // N_CHIPS, LRU_BLOCK_W).transpose(2, 0, 1, 3, 4)
    flat = [g["hyb_conv_w"].reshape(CONV_K, N_CHIPS, -1).transpose(1, 0, 2).reshape(N_CHIPS, -1) for g in hyb]
    for g in rec:
        flat.append(g["rec_conv_w"].reshape(CONV_K, N_CHIPS, -1).transpose(1, 0, 2).reshape(N_CHIPS, -1))
        flat += [g[k].reshape(N_CHIPS, -1) for k in REC_VECTORS]
    flat = jnp.concatenate(flat, axis=1)
    flat = jnp.pad(flat, ((0, 0), (0, -flat.shape[1] % (2 * D_MODEL))))
    names = [k for k in BIG if k in t]
    out = [t[k].reshape(N_CHIPS, 2, -1, t[k].shape[-1]) for k in names]
    return names + ["small"], out + [flat.reshape(N_CHIPS, 2, -1, D_MODEL)]


def kernel(x, hyb_w_in, hyb_sinks, hyb_conv_w, hyb_a_log, hyb_dt_bias, hyb_norm_w, hyb_w_out, rec_w_in, rec_conv_w, rec_conv_b, rec_w_a, rec_b_a, rec_w_x, rec_b_x, rec_lambda, rec_w_out, ln1_g, ln1_b, mlp_w1, mlp_w2, ln2_g, ln2_b, loss_target, m_hyb_w_in, m_hyb_sinks, m_hyb_conv_w, m_hyb_a_log, m_hyb_dt_bias, m_hyb_norm_w, m_hyb_w_out, m_rec_w_in, m_rec_conv_w, m_rec_conv_b, m_rec_w_a, m_rec_b_a, m_rec_w_x, m_rec_b_x, m_rec_lambda, m_rec_w_out, m_ln1_g, m_ln1_b, m_mlp_w1, m_mlp_w2, m_ln2_g, m_ln2_b, v_hyb_w_in, v_hyb_sinks, v_hyb_conv_w, v_hyb_a_log, v_hyb_dt_bias, v_hyb_norm_w, v_hyb_w_out, v_rec_w_in, v_rec_conv_w, v_rec_conv_b, v_rec_w_a, v_rec_b_a, v_rec_w_x, v_rec_b_x, v_rec_lambda, v_rec_w_out, v_ln1_g, v_ln1_b, v_mlp_w1, v_mlp_w2, v_ln2_g, v_ln2_b):
    args = locals()
    w = {k: args[k] for k in WEIGHTS}
    m = {k: args["m_" + k] for k in WEIGHTS}
    v = {k: args["v_" + k] for k in WEIGHTS}

    W, mlp_w, mixer_ready = _gather_full_weights(w)

    core = lax.axis_index("c").astype(jnp.int32)
    me = (2 * lax.axis_index("x") + lax.axis_index("y")).astype(jnp.int32)
    slots = jnp.arange(N_CHIPS, dtype=jnp.int32)
    where = jnp.concatenate([me[None], jnp.where(slots == me, (slots + 1) % N_CHIPS, slots), core[None]])
    state = {}

    def on_group(group, hyb, rec, d_w1, d_w2):
        if group == "middle":
            group = "upper"
            send_sems, recv_sems, by_chip, recv, _ = state["sibling"]
            by_chip, from_sibling = _rs_sibling_wait(send_sems, recv_sems, by_chip, recv, hyb, "rs_sibling_wait")
        else:
            state[group + " names"], by_chip = _group_by_chip(hyb, rec, d_w1, d_w2)
            if group == "upper":
                recv = [lax.empty((N_CHIPS,) + g.shape[2:], g.dtype) for g in by_chip]
                state["sibling"] = _rs_sibling_start(by_chip, recv, "rs_sibling_start")
                return state["sibling"][4][0, 0]
            from_sibling = _rs_to_sibling(by_chip, "rs_to_sibling")
        pair = [_pair_sum(g, r, core[None], "pair_sum") for g, r in zip(by_chip, from_sibling)]
        recv = [lax.empty(p.shape, p.dtype) for p in pair]
        state[group] = _rs_across_start(pair, recv, "rs_across_start_" + group)
        return state[group][4][0, 0]

    def finish_group(group, after, prev):
        send_sems, recv_sems, pair, recv, _ = state[group]
        pair, from_chips = _rs_across_wait(send_sems, recv_sems, pair, recv, after, "rs_across_wait_" + group)
        half = [_chip_sum(r, p, where, "chip_sum") for r, p in zip(from_chips, pair)]
        joined = _rs_join_halves(half, "rs_join_halves")
        outs = dict(prev or {})
        for k, g in zip(state[group + " names"][:-1], joined[:-1]):
            n = g.shape[-1]
            w2, g2 = w[k].reshape(-1, n), g.reshape(-1, n)
            row0 = w2.shape[0] - g2.shape[0] if group == "upper" else 0
            outs[k] = _adamw_rows(w2, g2, m[k].reshape(-1, n), v[k].reshape(-1, n), row0, outs.get(k), "adamw")
        return outs, joined[-1].reshape(-1)

    loss, dx, grads = _local_step(x[0], loss_target[0], W, mlp_w, mixer_ready, on_group)
    loss = lax.psum(loss[0, 0], ("x", "y", "c"))
    g_out, d_out, m_out, v_out = {}, {}, {}, {}

    upper, small_upper = finish_group("upper", state["lower"][4], None)
    small_g = _pack_small(*[grads[k] for k in SMALL])
    small_all = _all_gather_small(small_g, "all_gather_small")
    sw, sm, sv = (_pack_small(*[t[k] for k in SMALL]) for t in (w, m, v))
    sg, sd, snm, snv = _adamw_small(sw, small_all, sm, sv, "adamw_small")
    for dst, packed in ((g_out, sg), (d_out, sd), (m_out, snm), (v_out, snv)):
        for k, val in zip(SMALL, _unpack_small(packed)):
            dst[k] = val

    done = sum([upper[k][1][0, 0] for k in BIG], sg[0, 0]).reshape(1, 1)
    both, small_lower = finish_group("lower", done, upper)
    for k in BIG:
        shape = w[k].shape
        g_out[k], d_out[k], m_out[k], v_out[k] = (t.reshape(shape) for t in both[k])
    n_conv, n_rconv, n_vec = (w[k][0].size for k in ("hyb_conv_w", "rec_conv_w", "rec_conv_b"))
    n_rec = n_rconv + len(REC_VECTORS) * n_vec
    conv_g = jnp.stack([small_lower[:n_conv], small_upper[:n_conv]])
    rec_g = jnp.stack([small_upper[n_conv + i * n_rec:n_conv + (i + 1) * n_rec] for i in range(2)])
    vec_g = [rec_g[:, n_rconv + i * n_vec:n_rconv + (i + 1) * n_vec] for i in range(len(REC_VECTORS))]
    mid_g = _pack_mid(conv_g, rec_g[:, :n_rconv], *vec_g).reshape(-1, D_MODEL)
    mid_w, mid_m, mid_v = (_pack_mid(*[t[k] for k in MID]).reshape(-1, D_MODEL) for t in (w, m, v))
    mid_d, mid_nm, mid_nv = _adamw(mid_w, mid_g, mid_m, mid_v, "adamw_mid")
    for dst, packed in ((g_out, mid_g), (d_out, mid_d), (m_out, mid_nm), (v_out, mid_nv)):
        for k, val in zip(MID, _unpack_mid(packed.reshape(2, MID_ROWS, D_MODEL))):
            dst[k] = val.reshape(w[k].shape)

    return (loss, dx[None], *[g_out[k] for k in WEIGHTS], *[d_out[k] for k in WEIGHTS],
            *[m_out[k] for k in WEIGHTS], *[v_out[k] for k in WEIGHTS])
```

```python
import functools
import math

import jax
import jax.numpy as jnp
import numpy as np
from jax import lax
from jax.experimental import pallas as pl
from jax.experimental.pallas import tpu as pltpu

F32 = jnp.float32
MXU_DTYPE = jnp.bfloat16
ACT_DTYPE = jnp.bfloat16
ICI_DTYPE = jnp.bfloat16

D_MODEL = 1024
DEPTH = 4
A_HEAD_DIM = 64
A_Q_HEADS = 8
A_KV_HEADS = 2
WINDOW = 128
ROPE_THETA = 10000.0
B_HEADS = 4
B_HEAD_DIM = 128
B_CHUNK = 64
CONV_K = 4
LRU_BLOCKS = 4
LRU_BLOCK_W = D_MODEL // LRU_BLOCKS
LRU_C = 8.0
D_FF = 4 * D_MODEL
A_Q_W = A_Q_HEADS * A_HEAD_DIM
A_KV_W = A_KV_HEADS * A_HEAD_DIM
B_W = B_HEADS * B_HEAD_DIM
HYB_PROJ = A_Q_W + 2 * A_KV_W + 4 * B_W + 2 * B_HEADS
DN_ALPHA = (2 * DEPTH) ** 0.25
LN_EPS = 1e-5
NORM_EPS = 1e-6
ADAM_LR = 0.001
ADAM_B1 = 0.9
ADAM_B2 = 0.999
ADAM_EPS = 1e-08
ADAM_WD = 0.01
ADAM_STEP = 10

HP_Q = 0
HP_QKVB = 512
HP_Z = 2048
HP_K = 2560
HP_V = 2688
HP_BG = 2816
HYB_PAD = 3072

N_CHIPS = 4
N_DEV = 8
V7X_VMEM_LIMIT = 48 * 1024 * 1024
LANES = 128
SUBLANES = 8
NEG_BIG = -1e30

NN = (((1,), (0,)), ((), ()))
NT = (((1,), (1,)), ((), ()))
TN = (((0,), (0,)), ((), ()))


def _cparams(*sem):
    return pltpu.CompilerParams(dimension_semantics=sem, vmem_limit_bytes=V7X_VMEM_LIMIT)


def _dot(a, b, dims=NN):
    return lax.dot_general(a.astype(MXU_DTYPE), b.astype(MXU_DTYPE), dims, preferred_element_type=F32)


def _split_bf16(a):
    hi = a.astype(jnp.bfloat16)
    return hi, (a - hi.astype(F32)).astype(jnp.bfloat16)


def _dotf(a, b, dims=NN):
    ah, al = _split_bf16(a)
    bh, bl = _split_bf16(b)
    dg = functools.partial(lax.dot_general, dimension_numbers=dims, preferred_element_type=F32)
    return dg(ah, bh) + (dg(ah, bl) + dg(al, bh))


def _tile(dim, pref):
    t = min(dim, pref)
    while dim % t:
        t //= 2
    return t


def _sigmoid(x):
    return 1.0 / (1.0 + jnp.exp(-x))


def _silu(x):
    return x * _sigmoid(x)


def _dsilu(x):
    s = _sigmoid(x)
    return s * (1.0 + x * (1.0 - s))


GELU_C = 0.7978845608028654
GELU_A = 0.044715


def _gelu(x):
    return 0.5 * x * (1.0 + jnp.tanh(GELU_C * (x + GELU_A * x * x * x)))


def _dgelu(x):
    t = jnp.tanh(GELU_C * (x + GELU_A * x * x * x))
    return 0.5 * (1.0 + t) + 0.5 * x * (1.0 - t * t) * GELU_C * (1.0 + 3.0 * GELU_A * x * x)


def _matmul(a, b, mode, name, *, tm=1024, tn=1024, tk=1024, a_fn=None, epi=None, extra=None, out_dtype=F32,
            b_chips=None, out_chips=None):
    chunk = tk
    if mode == "tn":
        K, M = a.shape
        if K <= 4 * tk:
            tk, tm = K, tm // 2
    else:
        M, K = a.shape
    whole_k = False
    if b_chips is not None:
        g, b_layer = b_chips
        r, n = b.shape[2:]
        n_dim, k_dim = (r, n) if mode == "nt" else (n, r)
        N = N_CHIPS * n_dim if g == "j" else n_dim
        assert K == (N_CHIPS * k_dim if g == "k" else k_dim)
        if g == "j":
            tn = n_dim
        else:
            whole_k, tk, tm = True, K, tm // 2
    elif mode == "nt":
        N = b.shape[0]
        if tk < K <= 3 * tk:
            tk, tm = K, tm // 2
    else:
        N = b.shape[1]
    if out_chips is not None:
        og, o_layer, o_buf = out_chips
        if og == "j":
            tn = o_buf.shape[3]
        else:
            tm = min(tm, o_buf.shape[2])
    tm, tn, tk = _tile(M, tm), _tile(N, tn), _tile(K, tk)
    nk = K // tk
    if mode == "tn":
        a_spec = pl.BlockSpec((tk, tm), lambda i, j, k: (k, i))
    else:
        a_spec = pl.BlockSpec((tm, tk), lambda i, j, k: (i, k))
    b_block = (tn, tk) if mode == "nt" else (tk, tn)
    if b_chips is None:
        b_spec = pl.BlockSpec(b_block, (lambda i, j, k: (j, k)) if mode == "nt" else (lambda i, j, k: (k, j)))
    elif mode == "nt" and whole_k:
        b_spec = pl.BlockSpec((N_CHIPS, None, tn, k_dim), lambda i, j, k: (0, b_layer, j, 0))
    elif mode == "nt":
        b_spec = pl.BlockSpec((None, None) + b_block, lambda i, j, k: (j, b_layer, 0, k))
    elif whole_k:
        b_spec = pl.BlockSpec((N_CHIPS, None, k_dim, tn), lambda i, j, k: (0, b_layer, 0, j),
                              pipeline_mode=pl.Buffered(1) if N == tn else None)
    else:
        b_spec = pl.BlockSpec((None, None) + b_block, lambda i, j, k: (j, b_layer, k, 0))
    o_spec = pl.BlockSpec((tm, tn), lambda i, j, k: (i, j))
    e_spec = o_spec
    if out_chips is not None:
        per = o_buf.shape[2] // tm
        o_spec = pl.BlockSpec((None, None, tm, tn), (lambda i, j, k: (j, o_layer, i, 0)) if og == "j"
                              else (lambda i, j, k: (i // per, o_layer, i % per, j)))
    dims = {"nn": NN, "nt": NT, "tn": TN}[mode]
    extras = () if extra is None else (extra if isinstance(extra, tuple) else (extra,))
    out_dtypes = out_dtype if isinstance(out_dtype, tuple) else (out_dtype,)
    n_in = 2 + len(extras) + (out_chips is not None)

    def body(*refs):
        a_ref, b_ref = refs[0], refs[1]
        e_refs = refs[2:2 + len(extras)]
        o_refs = refs[n_in:n_in + len(out_dtypes)]
        av = a_ref[...]
        if a_fn is not None:
            av = a_fn(av)
        if whole_k and mode == "nt":
            part = _dot(av[:, :k_dim], b_ref[0], dims)
            for chip in range(1, N_CHIPS):
                part = part + _dot(av[:, chip * k_dim:(chip + 1) * k_dim], b_ref[chip], dims)
        elif mode == "tn" and tk > chunk:
            part = _dot(av[:chunk], b_ref[0:chunk, :], dims)
            for c0 in range(chunk, tk, chunk):
                part = part + _dot(av[c0:c0 + chunk], b_ref[c0:c0 + chunk, :], dims)
        else:
            bv = b_ref[...]
            if whole_k:
                bv = bv.reshape(K, tn)
            part = _dot(av, bv, dims)

        def finish(acc):
            if epi is not None:
                acc = epi(acc, *[e[...] for e in e_refs])
            for o_ref, val, dt in zip(o_refs, acc if isinstance(acc, tuple) else (acc,), out_dtypes):
                o_ref[...] = val.astype(dt)

        if nk == 1:
            finish(part)
        else:
            acc_ref = refs[-1]
            k = pl.program_id(2)

            @pl.when(k == 0)
            def _():
                acc_ref[...] = part

            @pl.when(k > 0)
            def _():
                acc_ref[...] += part

            @pl.when(k == nk - 1)
            def _():
                finish(acc_ref[...])

    row_spec = pl.BlockSpec((1, tn), lambda i, j, k: (0, j))
    in_specs = [a_spec, b_spec] + [row_spec if e.shape[0] == 1 else e_spec for e in extras]
    args = (a, b) + extras
    out_shape = [jax.ShapeDtypeStruct((M, N), dt) for dt in out_dtypes]
    out_specs = [o_spec] * len(out_dtypes)
    aliases = {}
    if out_chips is not None:
        in_specs.append(pl.BlockSpec(memory_space=pl.ANY))
        args += (o_buf,)
        out_shape = [jax.ShapeDtypeStruct(o_buf.shape, o_buf.dtype)]
        aliases = {n_in - 1: 0}
    if not isinstance(out_dtype, tuple):
        out_shape, out_specs = out_shape[0], out_specs[0]
    return pl.pallas_call(
        body, name=name, grid=(M // tm, N // tn, nk), in_specs=in_specs, out_specs=out_specs, out_shape=out_shape,
        input_output_aliases=aliases,
        scratch_shapes=[pltpu.VMEM((tm, tn), F32)] if nk > 1 else [],
        compiler_params=_cparams("parallel", "parallel", "arbitrary"),
    )(*args)


def _relu2(v):
    r = jnp.maximum(v, 0.0)
    return r * r


def _epi_drelu2(acc, h):
    return acc * (2.0 * jnp.maximum(h, 0.0))


def _epi_add_residual(acc, du):
    return acc + DN_ALPHA * du


def _epi_residual_ln(acc, x, g, b):
    u = DN_ALPHA * x + acc
    mu = jnp.mean(u, axis=-1, keepdims=True)
    d = u - mu
    var = jnp.mean(d * d, axis=-1, keepdims=True)
    return d * lax.rsqrt(var + LN_EPS) * g + b, u


def _epi_residual_ln_last(acc, x, g, b):
    o, u = _epi_residual_ln(acc, x, g, b)
    return o, o, u


def _ln_out(last=False):
    return (F32, ACT_DTYPE, F32) if last else (ACT_DTYPE, F32)


def _ln_bwd(u, g, dout, name):
    T, D = u.shape
    tr = _tile(T, 512)

    def body(u_ref, g_ref, d_ref, dub_ref, dg_ref, db_ref):
        i = pl.program_id(0)
        u = u_ref[...]
        mu = jnp.mean(u, axis=-1, keepdims=True)
        d = u - mu
        rstd = lax.rsqrt(jnp.mean(d * d, axis=-1, keepdims=True) + LN_EPS)
        xhat = d * rstd
        dout_v = d_ref[...]
        dxh = dout_v * g_ref[...]
        m1 = jnp.mean(dxh, axis=-1, keepdims=True)
        m2 = jnp.mean(dxh * xhat, axis=-1, keepdims=True)
        du = rstd * (dxh - m1 - xhat * m2)
        dub_ref[...] = du.astype(ACT_DTYPE)
        pg = jnp.sum(dout_v * xhat, axis=0, keepdims=True)
        pb = jnp.sum(dout_v, axis=0, keepdims=True)

        @pl.when(i == 0)
        def _():
            dg_ref[...] = pg
            db_ref[...] = pb

        @pl.when(i > 0)
        def _():
            dg_ref[...] += pg
            db_ref[...] += pb

    row = pl.BlockSpec((tr, D), lambda i: (i, 0))
    vec = pl.BlockSpec((1, D), lambda i: (0, 0))
    return pl.pallas_call(
        body, name=name, grid=(T // tr,), in_specs=[row, vec, row], out_specs=[row, vec, vec],
        out_shape=[jax.ShapeDtypeStruct((T, D), ACT_DTYPE),
                   jax.ShapeDtypeStruct((1, D), F32), jax.ShapeDtypeStruct((1, D), F32)],
        compiler_params=_cparams("arbitrary"),
    )(u, g, dout)


def _loss_head(y, tgt, name):
    T, D = y.shape
    tr = _tile(T, 512)

    def body(y_ref, t_ref, dy_ref, l_ref):
        i = pl.program_id(0)
        e = y_ref[...] - t_ref[...]
        dy_ref[...] = e * (1.0 / D)
        part = jnp.sum(e * e, axis=(0, 1), keepdims=True) * (0.5 / D)

        @pl.when(i == 0)
        def _():
            l_ref[...] = part

        @pl.when(i > 0)
        def _():
            l_ref[...] += part

    row = pl.BlockSpec((tr, D), lambda i: (i, 0))
    one = pl.BlockSpec((1, 1), lambda i: (0, 0))
    return pl.pallas_call(
        body, name=name, grid=(T // tr,), in_specs=[row, row], out_specs=[row, one],
        out_shape=[jax.ShapeDtypeStruct((T, D), F32), jax.ShapeDtypeStruct((1, 1), F32)],
        compiler_params=_cparams("arbitrary"),
    )(y, tgt)


def _swap_half(x):
    n = x.shape[-1]
    lane = lax.broadcasted_iota(jnp.int32, x.shape, 1)
    first = (lane % A_HEAD_DIM) < (A_HEAD_DIM // 2)
    return jnp.where(first, pltpu.roll(x, n - A_HEAD_DIM // 2, axis=1), pltpu.roll(x, A_HEAD_DIM // 2, axis=1))


def _rope(x, cos, sin_signed):
    return x * cos + _swap_half(x) * sin_signed


def _rope_t(dy, cos, sin_signed):
    return dy * cos + _swap_half(dy * sin_signed)


def _rope_tables(T):
    half = A_HEAD_DIM // 2
    inv_freq = np.float32(ROPE_THETA) ** (-np.arange(half, dtype=np.float32) / np.float32(half))
    ang = np.arange(T, dtype=np.float32)[:, None] * inv_freq[None, :]
    cos = np.tile(np.cos(ang), (1, 4))
    sin = np.sin(ang)
    sin_signed = np.tile(np.concatenate([-sin, sin], axis=1), (1, 2))
    return jnp.asarray(cos, F32), jnp.asarray(sin_signed, F32)


def _band_mask(n):
    qi = lax.broadcasted_iota(jnp.int32, (WINDOW, 2 * WINDOW), 0)
    kj = lax.broadcasted_iota(jnp.int32, (WINDOW, 2 * WINDOW), 1)
    return (kj > qi) & (kj <= qi + WINDOW) & ((n > 0) | (kj >= WINDOW))


def _place(v, src_half, dst_half):
    lane = lax.broadcasted_iota(jnp.int32, v.shape, 1)
    if src_half != dst_half:
        v = pltpu.roll(v, A_HEAD_DIM, axis=1)
    keep = (lane >= A_HEAD_DIM) if dst_half else (lane < A_HEAD_DIM)
    return jnp.where(keep, v, 0.0)


def _attn_specs():
    kb, vb = HP_K // LANES, HP_V // LANES
    prev = lambda n: jnp.maximum(n - 1, 0)
    return dict(
        q=pl.BlockSpec((WINDOW, A_Q_W), lambda n: (n, 0)),
        kc=pl.BlockSpec((WINDOW, LANES), lambda n: (n, kb)),
        kp=pl.BlockSpec((WINDOW, LANES), lambda n: (prev(n), kb)),
        vc=pl.BlockSpec((WINDOW, LANES), lambda n: (n, vb)),
        vp=pl.BlockSpec((WINDOW, LANES), lambda n: (prev(n), vb)),
        tq=pl.BlockSpec((WINDOW, LANES), lambda n: (n, 0)),
        tp=pl.BlockSpec((WINDOW, LANES), lambda n: (prev(n), 0)),
        sink=pl.BlockSpec((A_Q_HEADS, LANES), lambda n: (0, 0)),
        row512=pl.BlockSpec((WINDOW, A_Q_W), lambda n: (n, 0)),
        row128=pl.BlockSpec((WINDOW, LANES), lambda n: (n, 0)),
        lse=pl.BlockSpec((WINDOW, A_Q_HEADS), lambda n: (n, 0)),
    )


def _attn_fwd(proj, cos, sin_s, sinks_b, name):
    T = proj.shape[0]
    sp = _attn_specs()

    def body(q_ref, kc_ref, kp_ref, vc_ref, vp_ref, cq_ref, sq_ref, cp_ref, sp_ref, sink_ref, o_ref, l_ref):
        n = pl.program_id(0)
        cq, sq = cq_ref[...], sq_ref[...]
        cq4, sq4 = jnp.tile(cq, (1, A_Q_W // LANES)), jnp.tile(sq, (1, A_Q_W // LANES))
        q = _rope(q_ref[...], cq4, sq4) * (A_HEAD_DIM ** -0.5)
        kc = _rope(kc_ref[...], cq, sq)
        kp = _rope(kp_ref[...], cp_ref[...], sp_ref[...])
        kk = jnp.concatenate([kp, kc], axis=0)
        vv = jnp.concatenate([vp_ref[...], vc_ref[...]], axis=0)
        mask = _band_mask(n)[None]
        lane = lax.broadcasted_iota(jnp.int32, (WINDOW, LANES), 1)
        lane8 = lax.broadcasted_iota(jnp.int32, (WINDOW, A_Q_HEADS), 1)
        qe = jnp.stack([_place(q[:, (hq // 2) * LANES:(hq // 2 + 1) * LANES], hq % 2, hq // 4) for hq in range(A_Q_HEADS)])
        kk8 = jnp.broadcast_to(kk[None], (A_Q_HEADS,) + kk.shape)
        vv8 = jnp.broadcast_to(vv[None], (A_Q_HEADS,) + vv.shape)
        sk = jnp.stack([sink_ref[hq:hq + 1, 0:1] for hq in range(A_Q_HEADS)])
        s = jnp.where(mask, _dot(qe, kk8, BNT), NEG_BIG)
        m = jnp.maximum(jnp.max(s, axis=-1, keepdims=True), sk)
        p = jnp.exp(s - m)
        den = jnp.sum(p, axis=-1, keepdims=True) + jnp.exp(sk - m)
        o = _dot(p * (1.0 / den), vv8, BNN)
        lse_h = m + jnp.log(den)
        outs = []
        lse = jnp.zeros((WINDOW, A_Q_HEADS), F32)
        for pb in range(A_Q_HEADS // 2):
            halves = [_place(o[2 * pb + e], pb // 2, e) for e in range(2)]
            outs.append(jnp.where(lane < A_HEAD_DIM, halves[0], halves[1]))
            for e in range(2):
                lse = jnp.where(lane8 == 2 * pb + e, lse_h[2 * pb + e], lse)
        o_ref[...] = jnp.concatenate(outs, axis=1)
        l_ref[...] = lse

    return pl.pallas_call(
        body, name=name, grid=(T // WINDOW,),
        in_specs=[sp["q"], sp["kc"], sp["kp"], sp["vc"], sp["vp"], sp["tq"], sp["tq"], sp["tp"], sp["tp"], sp["sink"]],
        out_specs=[sp["row512"], sp["lse"]],
        out_shape=[jax.ShapeDtypeStruct((T, A_Q_W), F32), jax.ShapeDtypeStruct((T, A_Q_HEADS), F32)],
        compiler_params=_cparams("parallel"),
    )(proj, proj, proj, proj, proj, cos, sin_s, cos, sin_s, sinks_b)


def _attn_bwd(proj, cos, sin_s, sinks_b, o, lse, dmix, name):
    T = proj.shape[0]
    sp = _attn_specs()

    def body(q_ref, kc_ref, kp_ref, vc_ref, vp_ref, cq_ref, sq_ref, cp_ref, sp_ref, sink_ref, o_ref, l_ref, do_ref,
             dq_ref, dkc_ref, dkp_ref, dvc_ref, dvp_ref, dsink_ref):
        n = pl.program_id(0)
        cq, sq = cq_ref[...], sq_ref[...]
        cp, sps = cp_ref[...], sp_ref[...]
        cq4, sq4 = jnp.tile(cq, (1, A_Q_W // LANES)), jnp.tile(sq, (1, A_Q_W // LANES))
        q = _rope(q_ref[...], cq4, sq4) * (A_HEAD_DIM ** -0.5)
        kc = _rope(kc_ref[...], cq, sq)
        kp = _rope(kp_ref[...], cp, sps)
        kk = jnp.concatenate([kp, kc], axis=0)
        vv = jnp.concatenate([vp_ref[...], vc_ref[...]], axis=0)
        mask = _band_mask(n)[None]
        lane = lax.broadcasted_iota(jnp.int32, (WINDOW, LANES), 1)
        do_all, o_all, l_all = do_ref[...], o_ref[...], l_ref[...]
        lane8 = lax.broadcasted_iota(jnp.int32, (WINDOW, A_Q_HEADS), 1)
        head8 = lax.broadcasted_iota(jnp.int32, (1, A_Q_HEADS), 1)
        prod = do_all * o_all
        qes, does, deltas, lhs = [], [], [], []
        for hq in range(A_Q_HEADS):
            pb, e, kvh = hq // 2, hq % 2, hq // 4
            blk = slice(pb * LANES, (pb + 1) * LANES)
            in_half = (lane >= A_HEAD_DIM) if e else (lane < A_HEAD_DIM)
            deltas.append(jnp.sum(jnp.where(in_half, prod[:, blk], 0.0), axis=-1, keepdims=True))
            qes.append(_place(q[:, blk], e, kvh))
            does.append(_place(do_all[:, blk], e, kvh))
            lhs.append(jnp.sum(jnp.where(lane8 == hq, l_all, 0.0), axis=-1, keepdims=True))
        qe, doe, delta, lh = jnp.stack(qes), jnp.stack(does), jnp.stack(deltas), jnp.stack(lhs)
        kk8 = jnp.broadcast_to(kk[None], (A_Q_HEADS,) + kk.shape)
        vv8 = jnp.broadcast_to(vv[None], (A_Q_HEADS,) + vv.shape)
        sk = jnp.stack([sink_ref[hq:hq + 1, 0:1] for hq in range(A_Q_HEADS)])
        s = _dot(qe, kk8, BNT)
        p = jnp.where(mask, jnp.exp(jnp.where(mask, s, NEG_BIG) - lh), 0.0)
        dvv = jnp.sum(_dot(p, doe, BTN), axis=0)
        ds = p * (_dot(doe, vv8, BNT) - delta)
        dkk = jnp.sum(_dot(ds, qe, BTN), axis=0)
        dqe = _dot(ds, kk8, BNN)
        dsink_h = -jnp.sum(jnp.exp(sk - lh) * delta, axis=(1, 2), keepdims=True)
        dqs = []
        dsk = jnp.zeros((1, A_Q_HEADS), F32)
        for pb in range(A_Q_HEADS // 2):
            halves = [_place(dqe[2 * pb + e], pb // 2, e) for e in range(2)]
            dqs.append(jnp.where(lane < A_HEAD_DIM, halves[0], halves[1]))
            for e in range(2):
                dsk = jnp.where(head8 == 2 * pb + e, dsink_h[2 * pb + e], dsk)
        dq = jnp.concatenate(dqs, axis=1) * (A_HEAD_DIM ** -0.5)
        dq_ref[...] = _rope_t(dq, cq4, sq4).astype(ACT_DTYPE)
        dkp_ref[...] = _rope_t(dkk[:WINDOW], cp, sps)
        dkc_ref[...] = _rope_t(dkk[WINDOW:], cq, sq)
        dvp_ref[...] = dvv[:WINDOW]
        dvc_ref[...] = dvv[WINDOW:]

        @pl.when(n == 0)
        def _():
            dsink_ref[...] = dsk

        @pl.when(n > 0)
        def _():
            dsink_ref[...] += dsk

    return pl.pallas_call(
        body, name=name, grid=(T // WINDOW,),
        in_specs=[sp["q"], sp["kc"], sp["kp"], sp["vc"], sp["vp"], sp["tq"], sp["tq"], sp["tp"], sp["tp"], sp["sink"],
                  sp["row512"], sp["lse"], sp["row512"]],
        out_specs=[sp["row512"], sp["row128"], sp["row128"], sp["row128"], sp["row128"],
                   pl.BlockSpec((1, A_Q_HEADS), lambda n: (0, 0))],
        out_shape=[jax.ShapeDtypeStruct((T, HYB_PAD), ACT_DTYPE)] + [jax.ShapeDtypeStruct((T, LANES), F32)] * 4
        + [jax.ShapeDtypeStruct((1, A_Q_HEADS), F32)],
        compiler_params=_cparams("arbitrary"),
    )(proj, proj, proj, proj, proj, cos, sin_s, cos, sin_s, sinks_b, o, lse, dmix)


def _shift_down(x, prev8, k):
    if k == 0:
        return x
    R, W = x.shape
    r = pltpu.roll(jnp.concatenate([prev8, x], axis=0).reshape(R // SUBLANES + 1, SUBLANES, W), k, axis=1)
    sub = lax.broadcasted_iota(jnp.int32, (R // SUBLANES, SUBLANES, W), 1)
    return jnp.where(sub < k, r[:-1], r[1:]).reshape(R, W)


def _shift_up(x, next8, k):
    if k == 0:
        return x
    R, W = x.shape
    r = pltpu.roll(jnp.concatenate([x, next8], axis=0).reshape(R // SUBLANES + 1, SUBLANES, W), SUBLANES - k, axis=1)
    sub = lax.broadcasted_iota(jnp.int32, (R // SUBLANES, SUBLANES, W), 1)
    return jnp.where(sub >= SUBLANES - k, r[1:], r[:-1]).reshape(R, W)


def _conv(x, prev8, w):
    w0, w1, w2, w3 = (w[j:j + 1] for j in range(CONV_K))
    x1 = _shift_down(x, prev8, 1)
    v = x * w1 + x1 * w0
    v_prev8 = prev8 * w1 + pltpu.roll(prev8, 1, axis=0) * w0
    return x * w3 + x1 * w2 + _shift_down(v, v_prev8, 2)


def _conv_bwd(x, w, dy, next8_dy):
    dx = dy * w[CONV_K - 1:CONV_K]
    dws = []
    for j in range(CONV_K - 1):
        up = _shift_up(dy, next8_dy, CONV_K - 1 - j)
        dx = dx + up * w[j:j + 1]
        dws.append(jnp.sum(up * x, axis=0, keepdims=True))
    dws.append(jnp.sum(dy * x, axis=0, keepdims=True))
    return dx, dws


def _dnconv_fwd(proj, conv_w, name):
    T = proj.shape[0]
    R = _tile(T, 512)
    cb0 = HP_QKVB // A_Q_W

    def body(x_ref, w_ref, o_ref, prev_ref):
        i = pl.program_id(1)

        @pl.when(i == 0)
        def _():
            prev_ref[...] = jnp.zeros_like(prev_ref)

        x = x_ref[...]
        o_ref[...] = _silu(_conv(x, prev_ref[...], w_ref[...]))
        prev_ref[...] = x[R - SUBLANES:]

    return pl.pallas_call(
        body, name=name, grid=(3, T // R),
        in_specs=[pl.BlockSpec((R, B_W), lambda j, i: (i, cb0 + j)), pl.BlockSpec((CONV_K, B_W), lambda j, i: (0, j))],
        out_specs=pl.BlockSpec((R, B_W), lambda j, i: (i, j)),
        out_shape=jax.ShapeDtypeStruct((T, 3 * B_W), F32),
        scratch_shapes=[pltpu.VMEM((SUBLANES, B_W), F32)],
        compiler_params=_cparams("parallel", "arbitrary"),
    )(proj, conv_w)


def _dnconv_bwd(proj, conv_w, dy, dproj, name):
    T = proj.shape[0]
    R = _tile(T, 512)
    nb = T // R
    cb0 = HP_QKVB // A_Q_W
    r8 = R // SUBLANES

    def body(x_ref, xp_ref, w_ref, dy_ref, dproj_in, dx_ref, dw_ref, next_ref):
        i = pl.program_id(1)
        blk = nb - 1 - i

        @pl.when(i == 0)
        def _():
            next_ref[...] = jnp.zeros_like(next_ref)

        x = x_ref[...]
        prev8 = jnp.where(blk > 0, xp_ref[...], 0.0)
        w = w_ref[...]
        dpre = dy_ref[...] * _dsilu(_conv(x, prev8, w))
        dx, dw = _conv_bwd(x, w, dpre, next_ref[...])
        dx_ref[...] = dx.astype(ACT_DTYPE)
        next_ref[...] = dpre[:SUBLANES]

        @pl.when(i == 0)
        def _():
            for j in range(CONV_K):
                dw_ref[j:j + 1, :] = dw[j]

        @pl.when(i > 0)
        def _():
            for j in range(CONV_K):
                dw_ref[j:j + 1, :] += dw[j]

    return pl.pallas_call(
        body, name=name, grid=(3, nb),
        in_specs=[pl.BlockSpec((R, B_W), lambda j, i: (nb - 1 - i, cb0 + j)),
                  pl.BlockSpec((SUBLANES, B_W), lambda j, i: (jnp.maximum((nb - 1 - i) * r8 - 1, 0), cb0 + j)),
                  pl.BlockSpec((CONV_K, B_W), lambda j, i: (0, j)),
                  pl.BlockSpec((R, B_W), lambda j, i: (nb - 1 - i, j)), ANY],
        out_specs=[pl.BlockSpec((R, B_W), lambda j, i: (nb - 1 - i, cb0 + j)),
                   pl.BlockSpec((CONV_K, B_W), lambda j, i: (0, j))],
        out_shape=[jax.ShapeDtypeStruct(dproj.shape, dproj.dtype), jax.ShapeDtypeStruct((CONV_K, 3 * B_W), F32)],
        input_output_aliases={4: 0},
        scratch_shapes=[pltpu.VMEM((SUBLANES, B_W), F32)],
        compiler_params=_cparams("parallel", "arbitrary"),
    )(proj, proj, conv_w, dy, dproj)


DK_SCALE = B_HEAD_DIM ** -0.5


BNN = (((2,), (1,)), ((0,), (0,)))
BNT = (((2,), (2,)), ((0,), (0,)))
BTN = (((1,), (1,)), ((0,), (0,)))


def _tri_inv(a):
    C = a.shape[-1]
    ri = lax.broadcasted_iota(jnp.int32, (C, C), 0)
    ci = lax.broadcasted_iota(jnp.int32, (C, C), 1)
    x = jnp.where(ri == ci, 1.0, 0.0)[None] - a
    p = _dotf(a, a, BNN)
    span = 2
    while span < C:
        dot = _dotf if span <= 4 else _dot
        x = x + dot(x, p, BNN)
        span *= 2
        if span < C:
            p = dot(p, p, BNN)
    return x


def _dn_chunk(qc, kc, v, gcol, grow, bcol, s0, tm=None):
    C = B_CHUNK
    ri = lax.broadcasted_iota(jnp.int32, (C, C), 0)
    ci = lax.broadcasted_iota(jnp.int32, (C, C), 1)
    incl, strict = (ri >= ci)[None], (ri > ci)[None]
    rq = lax.rsqrt(jnp.sum(qc * qc, axis=-1, keepdims=True) + NORM_EPS)
    rk = lax.rsqrt(jnp.sum(kc * kc, axis=-1, keepdims=True) + NORM_EPS)
    qn = qc * rq
    q = qn * DK_SCALE
    k = kc * rk
    gc_col = jnp.sum(jnp.where(incl, grow, 0.0), axis=2, keepdims=True)
    gc_row = jnp.sum(jnp.where((ri <= ci)[None], gcol, 0.0), axis=1, keepdims=True)
    gl = jnp.sum(gcol, axis=1, keepdims=True)
    dincl = jnp.where(incl, jnp.exp(jnp.where(incl, gc_col - gc_row, 0.0)), 0.0)
    dstrict = jnp.where(strict, dincl, 0.0)
    eg = jnp.exp(gc_col)
    ekt = jnp.exp(gl - gc_col)
    egl = jnp.exp(gl)
    kb = k * bcol
    vb = v * bcol
    kbg = kb * eg
    a = _dot(kb, k, BNT) * dstrict
    if tm is None:
        tm = _tri_inv(a)
    u = _dot(tm, vb, BNN)
    w = _dot(tm, kbg, BNN)
    vn = u - _dot(w, s0, BNN)
    qk = _dot(q, k, BNT) * dincl
    qg = q * eg
    kt = k * ekt
    o = _dot(qg, s0, BNN) + _dot(qk, vn, BNN)
    s1 = s0 * egl + _dot(kt, vn, BTN)
    return dict(rq=rq, rk=rk, qn=qn, q=q, k=k, dincl=dincl, dstrict=dstrict, eg=eg, ekt=ekt, egl=egl, kb=kb, vb=vb,
                kbg=kbg, a=a, tm=tm, w=w, vn=vn, qk=qk, qg=qg, kt=kt, o=o, s1=s1, ri=ri[None], ci=ci[None])


def _heads(ref):
    return jnp.stack([ref[:, h * B_HEAD_DIM:(h + 1) * B_HEAD_DIM] for h in range(B_HEADS)])


def _store_heads(ref, val):
    for h in range(B_HEADS):
        ref[:, h * B_HEAD_DIM:(h + 1) * B_HEAD_DIM] = val[h]


def _dn_specs(N, rev):
    ix = (lambda n: N - 1 - n) if rev else (lambda n: n)
    wide = lambda cb: pl.BlockSpec((B_CHUNK, B_W), lambda n: (ix(n), cb))
    return dict(
        q=wide(0), k=wide(1), v=wide(2), z=wide(HP_Z // B_W), dob=wide(A_Q_W // B_W), out=wide(0),
        nw=pl.BlockSpec((1, LANES), lambda n: (0, 0)),
        row=pl.BlockSpec((B_HEADS, None, 1, B_CHUNK), lambda n: (0, ix(n), 0, 0)),
        state=pl.BlockSpec((B_HEADS, None, B_HEAD_DIM, B_HEAD_DIM), lambda n: (0, ix(n), 0, 0)),
        inv=pl.BlockSpec((B_HEADS, None, B_CHUNK, B_CHUNK), lambda n: (0, ix(n), 0, 0)),
    )


def _to_col(row):
    C = row.shape[-1]
    eye = lax.broadcasted_iota(jnp.int32, (C, C), 0) == lax.broadcasted_iota(jnp.int32, (C, C), 1)
    return jnp.sum(jnp.where(eye[None], row, 0.0), axis=2, keepdims=True)


def _to_row(col):
    C = col.shape[1]
    eye = lax.broadcasted_iota(jnp.int32, (C, C), 0) == lax.broadcasted_iota(jnp.int32, (C, C), 1)
    return jnp.sum(jnp.where(eye[None], col, 0.0), axis=1, keepdims=True)


def _dn_fwd(qkvc, proj, norm_w, grow, brow, name):
    T = qkvc.shape[0]
    N = T // B_CHUNK
    sp = _dn_specs(N, False)

    def body(q_ref, k_ref, v_ref, z_ref, nw_ref, gr_ref, br_ref, o_ref, st_ref, tm_ref, s_ref):
        n = pl.program_id(0)

        @pl.when(n == 0)
        def _():
            s_ref[...] = jnp.zeros_like(s_ref)

        s0 = s_ref[...]
        st_ref[...] = s0
        grow_v = gr_ref[...]
        f = _dn_chunk(_heads(q_ref), _heads(k_ref), _heads(v_ref), _to_col(grow_v), grow_v, _to_col(br_ref[...]), s0)
        o = f["o"]
        r = lax.rsqrt(jnp.mean(o * o, axis=-1, keepdims=True) + NORM_EPS)
        _store_heads(o_ref, o * r * nw_ref[...][None] * _silu(_heads(z_ref)))
        s_ref[...] = f["s1"]
        tm_ref[...] = f["tm"]

    return pl.pallas_call(
        body, name=name, grid=(N,),
        in_specs=[sp["q"], sp["k"], sp["v"], sp["z"], sp["nw"], sp["row"], sp["row"]],
        out_specs=[sp["out"], sp["state"], sp["inv"]],
        out_shape=[jax.ShapeDtypeStruct((T, B_W), F32),
                   jax.ShapeDtypeStruct((B_HEADS, N, B_HEAD_DIM, B_HEAD_DIM), F32),
                   jax.ShapeDtypeStruct((B_HEADS, N, B_CHUNK, B_CHUNK), F32)],
        scratch_shapes=[pltpu.VMEM((B_HEADS, B_HEAD_DIM, B_HEAD_DIM), F32)],
        compiler_params=_cparams("arbitrary"),
    )(qkvc, qkvc, qkvc, proj, norm_w, grow, brow)


def _dn_bwd(qkvc, proj, norm_w, grow, brow, states, invs, dmix, dproj, name):
    T = qkvc.shape[0]
    N = T // B_CHUNK
    sp = _dn_specs(N, True)
    C = B_CHUNK

    def body(q_ref, k_ref, v_ref, z_ref, nw_ref, gr_ref, br_ref, st_ref, tm_ref, dob_ref, dproj_in,
             dqkv_ref, dz_ref, dg_ref, db_ref, dnw_ref, ds_ref):
        n = pl.program_id(0)
        dq_ref = dqkv_ref.at[:, 0:B_W]
        dk_ref = dqkv_ref.at[:, B_W:2 * B_W]
        dv_ref = dqkv_ref.at[:, 2 * B_W:3 * B_W]

        @pl.when(n == 0)
        def _():
            ds_ref[...] = jnp.zeros_like(ds_ref)
            dnw_ref[...] = jnp.zeros_like(dnw_ref)

        s0 = st_ref[...]
        ds1 = ds_ref[...]
        v, z, nw, bcol_v = _heads(v_ref), _heads(z_ref), nw_ref[...][None], _to_col(br_ref[...])
        grow_v = gr_ref[...]
        f = _dn_chunk(_heads(q_ref), _heads(k_ref), v, _to_col(grow_v), grow_v, bcol_v, s0, tm=tm_ref[...])
        o, q, k, qn = f["o"], f["q"], f["k"], f["qn"]
        eg, ekt, egl = f["eg"], f["ekt"], f["egl"]
        tm, w, vn, kb, vb, kbg = f["tm"], f["w"], f["vn"], f["kb"], f["vb"], f["kbg"]
        qg, kt, qk, a = f["qg"], f["kt"], f["qk"], f["a"]
        ri, ci = f["ri"], f["ci"]

        dob_v = _heads(dob_ref)
        r = lax.rsqrt(jnp.mean(o * o, axis=-1, keepdims=True) + NORM_EPS)
        sz = _silu(z)
        on = o * r
        dnw_ref[...] += jnp.sum(dob_v * sz * on, axis=1, keepdims=True)
        _store_heads(dz_ref, (dob_v * on * nw * _dsilu(z)).astype(ACT_DTYPE))
        d_on = dob_v * sz * nw
        do = r * (d_on - on * jnp.mean(d_on * on, axis=-1, keepdims=True))

        dvn = _dot(qk, do, BTN) + _dot(kt, ds1, BNN)
        dqk = _dot(do, vn, BNT)
        dqg = _dot(do, s0, BNT)
        ds_ref[...] = _dot(qg, do, BTN) + egl * ds1 - _dot(w, dvn, BTN)
        dgl = jnp.sum(s0 * ds1, axis=(1, 2), keepdims=True) * egl
        dkt = _dot(vn, ds1, BNT)
        dw = -_dot(dvn, s0, BNT)
        dq = dqg * eg
        dgc = jnp.sum(dqg * qg, axis=-1, keepdims=True)
        dk = dkt * ekt
        t_kt = jnp.sum(dkt * kt, axis=-1, keepdims=True)
        dgl = dgl + jnp.sum(t_kt, axis=1, keepdims=True)
        dgc = dgc - t_kt
        dqkr = dqk * f["dincl"]
        dq = dq + _dot(dqkr, k, BNN)
        dk = dk + _dot(dqkr, q, BTN)
        e_qk = dqk * qk
        dgc = dgc + jnp.sum(e_qk, axis=-1, keepdims=True)
        dgc_row = -jnp.sum(e_qk, axis=1, keepdims=True)
        dtm = _dot(dvn, vb, BNT) + _dot(dw, kbg, BNT)
        dvb = _dot(tm, dvn, BTN)
        dkbg = _dot(tm, dw, BTN)
        dkb = dkbg * eg
        dgc = dgc + jnp.sum(dkbg * kbg, axis=-1, keepdims=True)
        da = -_dotf(tm, _dotf(dtm, tm, BNT), BTN)
        dkk = da * f["dstrict"]
        e_a = da * a
        dgc = dgc + jnp.sum(e_a, axis=-1, keepdims=True)
        dgc_row = dgc_row - jnp.sum(e_a, axis=1, keepdims=True)
        dkb = dkb + _dot(dkk, k, BNN)
        dk = dk + _dot(dkk, kb, BTN)
        dk = dk + dkb * bcol_v
        db_ref[...] = _to_row(jnp.sum(dkb * k, axis=-1, keepdims=True) + jnp.sum(dvb * v, axis=-1, keepdims=True))
        _store_heads(dv_ref, dvb * bcol_v)
        dgc_row = dgc_row + jnp.sum(jnp.where(ri == ci, dgc, 0.0), axis=1, keepdims=True)
        dg_ref[...] = jnp.sum(jnp.where(ci <= ri, _to_col(dgc_row), 0.0), axis=1, keepdims=True) + dgl
        dqs = dq * DK_SCALE
        _store_heads(dq_ref, f["rq"] * (dqs - qn * jnp.sum(dqs * qn, axis=-1, keepdims=True)))
        _store_heads(dk_ref, f["rk"] * (dk - k * jnp.sum(dk * k, axis=-1, keepdims=True)))

    return pl.pallas_call(
        body, name=name, grid=(N,),
        in_specs=[sp["q"], sp["k"], sp["v"], sp["z"], sp["nw"], sp["row"], sp["row"], sp["state"], sp["inv"], sp["dob"],
                  ANY],
        out_specs=[pl.BlockSpec((C, 3 * B_W), lambda n: (N - 1 - n, 0)), sp["z"], sp["row"], sp["row"],
                   pl.BlockSpec((B_HEADS, 1, LANES), lambda n: (0, 0, 0))],
        out_shape=[jax.ShapeDtypeStruct((T, 3 * B_W), F32), jax.ShapeDtypeStruct(dproj.shape, dproj.dtype),
                   jax.ShapeDtypeStruct((B_HEADS, N, 1, C), F32), jax.ShapeDtypeStruct((B_HEADS, N, 1, C), F32),
                   jax.ShapeDtypeStruct((B_HEADS, 1, LANES), F32)],
        input_output_aliases={10: 1},
        scratch_shapes=[pltpu.VMEM((B_HEADS, B_HEAD_DIM, B_HEAD_DIM), F32)],
        compiler_params=_cparams("arbitrary"),
    )(qkvc, qkvc, qkvc, proj, norm_w, grow, brow, states, invs, dmix, dproj)


def _lru_gates(xc, wa_ref, wx_ref, ba, bx, sp):
    pre_r, pre_i = [], []
    for hb in range(LRU_BLOCKS):
        xb = xc[:, hb * LRU_BLOCK_W:(hb + 1) * LRU_BLOCK_W]
        pre_r.append(_dot(xb, wa_ref[hb]))
        pre_i.append(_dot(xb, wx_ref[hb]))
    r = _sigmoid(jnp.concatenate(pre_r, axis=1) + ba)
    i = _sigmoid(jnp.concatenate(pre_i, axis=1) + bx)
    la = -LRU_C * r * sp
    a = jnp.exp(la)
    th = jnp.tanh(la)
    s = jnp.sqrt(-2.0 * th / (1.0 - th))
    return r, i, a, s


def _scan_down(a, b, h_in):
    R, W = a.shape
    a = a.reshape(R // SUBLANES, SUBLANES, W)
    b = b.reshape(R // SUBLANES, SUBLANES, W)
    sub = lax.broadcasted_iota(jnp.int32, a.shape, 1)
    d = 1
    while d < SUBLANES:
        ok = sub >= d
        b = a * jnp.where(ok, pltpu.roll(b, d, axis=1), 0.0) + b
        a = a * jnp.where(ok, pltpu.roll(a, d, axis=1), 1.0)
        d *= 2
    out, last = [], h_in
    for g in range(R // SUBLANES):
        h = b[g] + a[g] * last
        out.append(h)
        last = h[SUBLANES - 1:SUBLANES]
    return jnp.concatenate(out, axis=0)


def _scan_up(a, b, l_in):
    R, W = a.shape
    a = a.reshape(R // SUBLANES, SUBLANES, W)
    b = b.reshape(R // SUBLANES, SUBLANES, W)
    sub = lax.broadcasted_iota(jnp.int32, a.shape, 1)
    d = 1
    while d < SUBLANES:
        ok = sub < SUBLANES - d
        b = a * jnp.where(ok, pltpu.roll(b, SUBLANES - d, axis=1), 0.0) + b
        a = a * jnp.where(ok, pltpu.roll(a, SUBLANES - d, axis=1), 1.0)
        d *= 2
    out, nxt = [], l_in
    for g in range(R // SUBLANES - 1, -1, -1):
        lam = b[g] + a[g] * nxt
        out.append(lam)
        nxt = lam[0:1]
    return jnp.concatenate(out[::-1], axis=0)


def _rglru_fwd(proj, conv_w, conv_b, wa, wx, ba, bx, sp, name):
    T = proj.shape[0]
    R = _tile(T, 256)
    W = D_MODEL

    def body(p_ref, cw_ref, cb_ref, wa_ref, wx_ref, ba_ref, bx_ref, sp_ref, hg_ref, h_ref, prev_ref, hc_ref):
        i = pl.program_id(0)

        @pl.when(i == 0)
        def _():
            prev_ref[...] = jnp.zeros_like(prev_ref)
            hc_ref[...] = jnp.zeros_like(hc_ref)

        xr = p_ref[:, :W]
        gate = p_ref[:, W:]
        xc = _conv(xr, prev_ref[...], cw_ref[...]) + cb_ref[...]
        prev_ref[...] = xr[R - SUBLANES:]
        r, ig, a, s = _lru_gates(xc, wa_ref, wx_ref, ba_ref[...], bx_ref[...], sp_ref[...])
        h = _scan_down(a, s * ig * xc, hc_ref[SUBLANES - 1:SUBLANES, :])
        h_ref[...] = h
        hg_ref[...] = (h * _gelu(gate)).astype(ACT_DTYPE)
        hc_ref[...] = h[R - SUBLANES:]

    vec = pl.BlockSpec((1, W), lambda i: (0, 0))
    wsp = pl.BlockSpec((LRU_BLOCKS, LRU_BLOCK_W, LRU_BLOCK_W), lambda i: (0, 0, 0))
    row = pl.BlockSpec((R, W), lambda i: (i, 0))
    return pl.pallas_call(
        body, name=name, grid=(T // R,),
        in_specs=[pl.BlockSpec((R, 2 * W), lambda i: (i, 0)), pl.BlockSpec((CONV_K, W), lambda i: (0, 0)),
                  vec, wsp, wsp, vec, vec, vec],
        out_specs=[row, row],
        out_shape=[jax.ShapeDtypeStruct((T, W), ACT_DTYPE), jax.ShapeDtypeStruct((T, W), F32)],
        scratch_shapes=[pltpu.VMEM((SUBLANES, W), F32), pltpu.VMEM((SUBLANES, W), F32)],
        compiler_params=_cparams("arbitrary"),
    )(proj, conv_w, conv_b, wa, wx, ba, bx, sp)


def _rglru_bwd(proj, conv_w, conv_b, wa, wx, ba, bx, sp, h, dhg, name):
    T = proj.shape[0]
    R = _tile(T, 256)
    nb = T // R
    r8 = R // SUBLANES
    W = D_MODEL

    def body(p_ref, pp_ref, cw_ref, cb_ref, wa_ref, wx_ref, ba_ref, bx_ref, sp_ref, h_ref, hp_ref, dhg_ref,
             dp_ref, dcw_ref, dcb_ref, dwa_ref, dwx_ref, dba_ref, dbx_ref, dsp_ref, lam_ref, nxt_ref):
        step = pl.program_id(0)
        blk = nb - 1 - step

        @pl.when(step == 0)
        def _():
            lam_ref[...] = jnp.zeros_like(lam_ref)
            nxt_ref[...] = jnp.zeros_like(nxt_ref)

        xr = p_ref[:, :W]
        gate = p_ref[:, W:]
        first = blk > 0
        prev8 = jnp.where(first, pp_ref[:, :W], 0.0)
        hprev8 = jnp.where(first, hp_ref[...], 0.0)
        cw = cw_ref[...]
        spv = sp_ref[...]
        xc = _conv(xr, prev8, cw) + cb_ref[...]
        r, ig, a, s = _lru_gates(xc, wa_ref, wx_ref, ba_ref[...], bx_ref[...], spv)
        hv = h_ref[...]
        dhg_v = dhg_ref[...]
        dgate = dhg_v * hv * _dgelu(gate)
        dh = dhg_v * _gelu(gate)
        row = lax.broadcasted_iota(jnp.int32, (R, W), 0)
        last = row == R - 1
        a_up = jnp.where(last, 0.0, pltpu.roll(a, R - 1, axis=0))
        lam = _scan_up(a_up, dh + jnp.where(last, lam_ref[0:1, :], 0.0), jnp.zeros((1, W), F32))
        lam_ref[...] = (a * lam)[:SUBLANES]
        h_dn = _shift_down(hv, hprev8, 1)
        da = lam * h_dn
        bx_in = ig * xc
        dsv = lam * bx_in
        dig = lam * s * xc
        dxc = lam * s * ig
        dla = da * a - dsv * (a * a) / s
        dr = dla * (-LRU_C) * spv
        dsp = jnp.sum(dla * (-LRU_C) * r, axis=0, keepdims=True)
        dpr = dr * r * (1.0 - r)
        dpi = dig * ig * (1.0 - ig)
        dxc_parts, dwa_parts, dwx_parts = [], [], []
        for hb in range(LRU_BLOCKS):
            sl = slice(hb * LRU_BLOCK_W, (hb + 1) * LRU_BLOCK_W)
            xb, gr, gi = xc[:, sl], dpr[:, sl], dpi[:, sl]
            dxc_parts.append(_dot(gr, wa_ref[hb], NT) + _dot(gi, wx_ref[hb], NT))
            dwa_parts.append(_dot(xb, gr, TN))
            dwx_parts.append(_dot(xb, gi, TN))
        dxc = dxc + jnp.concatenate(dxc_parts, axis=1)
        dxr, dcw = _conv_bwd(xr, cw, dxc, nxt_ref[...])
        nxt_ref[...] = dxc[:SUBLANES]
        dp_ref[:, :W] = dxr.astype(ACT_DTYPE)
        dp_ref[:, W:] = dgate.astype(ACT_DTYPE)
        dcb = jnp.sum(dxc, axis=0, keepdims=True)
        dba = jnp.sum(dpr, axis=0, keepdims=True)
        dbx = jnp.sum(dpi, axis=0, keepdims=True)

        @pl.when(step == 0)
        def _():
            for j in range(CONV_K):
                dcw_ref[j:j + 1, :] = dcw[j]
            dcb_ref[...] = dcb
            dba_ref[...] = dba
            dbx_ref[...] = dbx
            dsp_ref[...] = dsp
            for hb in range(LRU_BLOCKS):
                dwa_ref[hb] = dwa_parts[hb]
                dwx_ref[hb] = dwx_parts[hb]

        @pl.when(step > 0)
        def _():
            for j in range(CONV_K):
                dcw_ref[j:j + 1, :] += dcw[j]
            dcb_ref[...] += dcb
            dba_ref[...] += dba
            dbx_ref[...] += dbx
            dsp_ref[...] += dsp
            for hb in range(LRU_BLOCKS):
                dwa_ref[hb] += dwa_parts[hb]
                dwx_ref[hb] += dwx_parts[hb]

    rv = lambda i: nb - 1 - i
    before = lambda i: jnp.maximum((nb - 1 - i) * r8 - 1, 0)
    vec = pl.BlockSpec((1, W), lambda i: (0, 0))
    cws = pl.BlockSpec((CONV_K, W), lambda i: (0, 0))
    wsp = pl.BlockSpec((LRU_BLOCKS, LRU_BLOCK_W, LRU_BLOCK_W), lambda i: (0, 0, 0))
    row = pl.BlockSpec((R, W), lambda i: (rv(i), 0))
    wshape = jax.ShapeDtypeStruct((LRU_BLOCKS, LRU_BLOCK_W, LRU_BLOCK_W), F32)
    vshape = jax.ShapeDtypeStruct((1, W), F32)
    return pl.pallas_call(
        body, name=name, grid=(nb,),
        in_specs=[pl.BlockSpec((R, 2 * W), lambda i: (rv(i), 0)), pl.BlockSpec((SUBLANES, 2 * W), lambda i: (before(i), 0)),
                  cws, vec, wsp, wsp, vec, vec, vec, row, pl.BlockSpec((SUBLANES, W), lambda i: (before(i), 0)), row],
        out_specs=[pl.BlockSpec((R, 2 * W), lambda i: (rv(i), 0)), cws, vec, wsp, wsp, vec, vec, vec],
        out_shape=[jax.ShapeDtypeStruct((T, 2 * W), ACT_DTYPE), jax.ShapeDtypeStruct((CONV_K, W), F32), vshape,
                   wshape, wshape, vshape, vshape, vshape],
        scratch_shapes=[pltpu.VMEM((SUBLANES, W), F32), pltpu.VMEM((SUBLANES, W), F32)],
        compiler_params=_cparams("arbitrary"),
    )(proj, proj, conv_w, conv_b, wa, wx, ba, bx, sp, h, h, dhg)


MESH = pl.DeviceIdType.MESH
ANY = pl.BlockSpec(memory_space=pl.ANY)


def _position():
    x, y, c = lax.axis_index("x"), lax.axis_index("y"), lax.axis_index("c")
    other_chips = [(1 - x, y), (x, 1 - y), (1 - x, 1 - y)]
    return x, y, c, other_chips


def _all_gather_weights(shards, name):
    n = len(shards)

    def body(*refs):
        ins, outs = refs[:n], refs[n:2 * n]
        send_sems, recv_sems = refs[2 * n:]
        x, y, c, chips = _position()
        me = 2 * x + y
        sibling = (x, y, 1 - c)

        def rcopy(t, k, src, dst, to):
            return pltpu.make_async_remote_copy(src_ref=src, dst_ref=dst, send_sem=send_sems.at[t, k],
                                                recv_sem=recv_sems.at[t, k], device_id=to, device_id_type=MESH)

        started = []
        for t in range(n):
            for j, (cx, cy) in enumerate(chips):
                cp = rcopy(t, j, ins[t].at[c], outs[t].at[me, c], (cx, cy, c))
                cp.start()
                started.append(cp)
        for t in range(n):
            for j, (cx, cy) in enumerate(chips):
                blk = outs[t].at[2 * cx + cy, c]
                rcopy(t, j, blk, blk, (cx, cy, c)).wait_recv()
                cp = rcopy(t, 3 + j, blk, blk, sibling)
                cp.start()
                started.append(cp)
        for t in range(n):
            for j, (cx, cy) in enumerate(chips):
                blk = outs[t].at[2 * cx + cy, 1 - c]
                rcopy(t, 3 + j, blk, blk, sibling).wait_recv()
        for cp in started:
            cp.wait_send()

    return pl.pallas_call(
        body, name=name, in_specs=[ANY] * n, out_specs=[ANY] * n,
        out_shape=[jax.ShapeDtypeStruct((N_CHIPS,) + s.shape, s.dtype) for s in shards],
        scratch_shapes=[pltpu.SemaphoreType.DMA((n, 6)), pltpu.SemaphoreType.DMA((n, 6))],
    )(*shards)


HBM = pl.BlockSpec(memory_space=pltpu.HBM)
SEM = pl.BlockSpec(memory_space=pltpu.SEMAPHORE)
EFFECT = pltpu.SideEffectType.DATAFLOW_SIDE_EFFECTING


def _gather_start(own, land, after, name):
    n = len(own)

    def body(*refs):
        own_refs, land_refs = refs[:n], refs[n:2 * n]
        send_sems = refs[2 * n + 1:3 * n + 1]
        recv_sems = refs[3 * n + 1:4 * n + 1]
        x, y, c, chips = _position()
        me = 2 * x + y
        for t in range(n):
            for cx, cy in chips:
                pltpu.make_async_remote_copy(
                    src_ref=own_refs[t], dst_ref=land_refs[t].at[me], send_sem=send_sems[t], recv_sem=recv_sems[t],
                    device_id=(cx, cy, c), device_id_type=MESH).start()

    sems = (pltpu.SemaphoreType.DMA(()),) * (2 * n)
    thru = [pltpu.HBM(a.shape, a.dtype) for a in list(own) + list(land)]
    out = pl.pallas_call(
        body, name=name, out_shape=(*sems, *thru),
        in_specs=[HBM] * (2 * n) + [pl.BlockSpec(memory_space=pl.ANY)], out_specs=(SEM,) * (2 * n) + (HBM,) * (2 * n),
        input_output_aliases={i: 2 * n + i for i in range(2 * n)},
        compiler_params=pltpu.CompilerParams(has_side_effects=EFFECT),
    )(*[pltpu.with_memory_space_constraint(a, pltpu.HBM) for a in list(own) + list(land)], after)
    return list(out[:n]), list(out[n:2 * n]), list(out[2 * n:3 * n]), list(out[3 * n:])


def _gather_wait(send_sems, recv_sems, own, land, after, name):
    n = len(own)

    def body(*refs):
        land_refs = refs[n:2 * n]
        s_sems, r_sems = refs[2 * n:3 * n], refs[3 * n:4 * n]
        x, y, c, _ = _position()
        for t in range(n):
            three = land_refs[t].at[pl.ds(0, N_CHIPS - 1)]
            cp = pltpu.make_async_remote_copy(src_ref=three, dst_ref=three, send_sem=s_sems[t], recv_sem=r_sems[t],
                                              device_id=(x, y, c), device_id_type=MESH)
            cp.wait_send()
            cp.wait_recv()

    thru = [pltpu.HBM(a.shape, a.dtype) for a in list(own) + list(land)]
    out = pl.pallas_call(
        body, name=name, out_shape=tuple(thru),
        in_specs=[HBM] * (2 * n) + [SEM] * (2 * n) + [pl.BlockSpec(memory_space=pl.ANY)], out_specs=(HBM,) * (2 * n),
        input_output_aliases={i: i for i in range(2 * n)},
        compiler_params=pltpu.CompilerParams(has_side_effects=EFFECT),
    )(*own, *land, *send_sems, *recv_sems, after)
    return list(out[n:])


def _rs_to_sibling(grads, name):
    n = len(grads)

    def body(*refs):
        ins, outs = refs[:n], refs[n:2 * n]
        send_sems, recv_sems = refs[2 * n:]
        x, y, c, _ = _position()
        cps = [pltpu.make_async_remote_copy(src_ref=ins[t].at[:, 1 - c], dst_ref=outs[t], send_sem=send_sems.at[t],
                                            recv_sem=recv_sems.at[t], device_id=(x, y, 1 - c), device_id_type=MESH)
               for t in range(n)]
        for cp in cps:
            cp.start()
        for cp in cps:
            cp.wait()

    return pl.pallas_call(
        body, name=name, in_specs=[ANY] * n, out_specs=[ANY] * n,
        out_shape=[jax.ShapeDtypeStruct((N_CHIPS,) + g.shape[2:], g.dtype) for g in grads],
        scratch_shapes=[pltpu.SemaphoreType.DMA((n,)), pltpu.SemaphoreType.DMA((n,))],
    )(*grads)


def _rs_sibling_start(grads, recv, name):
    n = len(grads)

    def body(*refs):
        grad_refs, recv_refs = refs[:n], refs[n:2 * n]
        send_sems, recv_sems = refs[2 * n:3 * n], refs[3 * n:4 * n]
        token_ref = refs[-1]
        x, y, c, _ = _position()
        for t in range(n):
            pltpu.make_async_remote_copy(src_ref=grad_refs[t].at[:, 1 - c], dst_ref=recv_refs[t], send_sem=send_sems[t],
                                         recv_sem=recv_sems[t], device_id=(x, y, 1 - c), device_id_type=MESH).start()
        token_ref[...] = jnp.zeros_like(token_ref)

    sems = (pltpu.SemaphoreType.DMA(()),) * (2 * n)
    thru = [pltpu.HBM(a.shape, a.dtype) for a in list(grads) + list(recv)]
    out = pl.pallas_call(
        body, name=name, out_shape=(*sems, *thru, jax.ShapeDtypeStruct((SUBLANES, LANES), F32)),
        in_specs=[HBM] * (2 * n), out_specs=(SEM,) * (2 * n) + (HBM,) * (2 * n) + (pl.BlockSpec(memory_space=pltpu.VMEM),),
        input_output_aliases={i: 2 * n + i for i in range(2 * n)},
        compiler_params=pltpu.CompilerParams(has_side_effects=EFFECT),
    )(*[pltpu.with_memory_space_constraint(a, pltpu.HBM) for a in list(grads) + list(recv)])
    return out[:n], out[n:2 * n], list(out[2 * n:3 * n]), list(out[3 * n:4 * n]), out[-1]


def _rs_sibling_wait(send_sems, recv_sems, grads, recv, after, name):
    n = len(grads)

    def body(*refs):
        recv_refs = refs[n:2 * n]
        s_sems, r_sems = refs[2 * n:3 * n], refs[3 * n:4 * n]
        x, y, c, _ = _position()
        for t in range(n):
            cp = pltpu.make_async_remote_copy(src_ref=recv_refs[t], dst_ref=recv_refs[t], send_sem=s_sems[t],
                                              recv_sem=r_sems[t], device_id=(x, y, c), device_id_type=MESH)
            cp.wait_send()
            cp.wait_recv()

    thru = [pltpu.HBM(a.shape, a.dtype) for a in list(grads) + list(recv)]
    out = pl.pallas_call(
        body, name=name, out_shape=tuple(thru),
        in_specs=[HBM] * (2 * n) + [SEM] * (2 * n) + [pl.BlockSpec(memory_space=pl.ANY)], out_specs=(HBM,) * (2 * n),
        input_output_aliases={i: i for i in range(2 * n)},
        compiler_params=pltpu.CompilerParams(has_side_effects=EFFECT),
    )(*grads, *recv, *send_sems, *recv_sems, after)
    return list(out[:n]), list(out[n:])


def _rs_across_start(parts, recv, name):
    n = len(parts)

    def body(*refs):
        part_refs, recv_refs = refs[:n], refs[n:2 * n]
        send_sems, recv_sems = refs[2 * n:3 * n], refs[3 * n:4 * n]
        token_ref = refs[-1]
        x, y, c, chips = _position()
        me = 2 * x + y
        for t in range(n):
            for cx, cy in chips:
                pltpu.make_async_remote_copy(src_ref=part_refs[t].at[2 * cx + cy], dst_ref=recv_refs[t].at[me],
                                             send_sem=send_sems[t], recv_sem=recv_sems[t], device_id=(cx, cy, c),
                                             device_id_type=MESH).start()
        token_ref[...] = jnp.zeros_like(token_ref)

    sems = (pltpu.SemaphoreType.DMA(()),) * (2 * n)
    thru = [pltpu.HBM(a.shape, a.dtype) for a in list(parts) + list(recv)]
    out = pl.pallas_call(
        body, name=name, out_shape=(*sems, *thru, jax.ShapeDtypeStruct((SUBLANES, LANES), F32)),
        in_specs=[HBM] * (2 * n), out_specs=(SEM,) * (2 * n) + (HBM,) * (2 * n) + (pl.BlockSpec(memory_space=pltpu.VMEM),),
        input_output_aliases={i: 2 * n + i for i in range(2 * n)},
        compiler_params=pltpu.CompilerParams(has_side_effects=EFFECT),
    )(*[pltpu.with_memory_space_constraint(a, pltpu.HBM) for a in list(parts) + list(recv)])
    return out[:n], out[n:2 * n], list(out[2 * n:3 * n]), list(out[3 * n:4 * n]), out[-1]


def _rs_across_wait(send_sems, recv_sems, parts, recv, after, name):
    n = len(parts)

    def body(*refs):
        recv_refs = refs[n:2 * n]
        s_sems, r_sems = refs[2 * n:3 * n], refs[3 * n:4 * n]
        x, y, c, _ = _position()
        for t in range(n):
            three = recv_refs[t].at[pl.ds(0, N_CHIPS - 1)]
            cp = pltpu.make_async_remote_copy(src_ref=three, dst_ref=three, send_sem=s_sems[t], recv_sem=r_sems[t],
                                              device_id=(x, y, c), device_id_type=MESH)
            cp.wait_send()
            cp.wait_recv()

    thru = [pltpu.HBM(a.shape, a.dtype) for a in list(parts) + list(recv)]
    out = pl.pallas_call(
        body, name=name, out_shape=tuple(thru),
        in_specs=[HBM] * (2 * n) + [SEM] * (2 * n) + [pl.BlockSpec(memory_space=pl.ANY)], out_specs=(HBM,) * (2 * n),
        input_output_aliases={i: i for i in range(2 * n)},
        compiler_params=pltpu.CompilerParams(has_side_effects=EFFECT),
    )(*parts, *recv, *send_sems, *recv_sems, after)
    return list(out[:n]), list(out[n:])


def _rs_join_halves(halves, name):
    n = len(halves)

    def body(*refs):
        ins, outs = refs[:n], refs[n:2 * n]
        send_sems, recv_sems = refs[2 * n:]
        x, y, c, _ = _position()
        cps = [pltpu.make_async_remote_copy(src_ref=ins[t].at[c], dst_ref=outs[t].at[c], send_sem=send_sems.at[t],
                                            recv_sem=recv_sems.at[t], device_id=(x, y, 1 - c), device_id_type=MESH)
               for t in range(n)]
        for cp in cps:
            cp.start()
        for t in range(n):
            blk = outs[t].at[1 - c]
            pltpu.make_async_remote_copy(src_ref=blk, dst_ref=blk, send_sem=send_sems.at[t], recv_sem=recv_sems.at[t],
                                         device_id=(x, y, 1 - c), device_id_type=MESH).wait_recv()
        for cp in cps:
            cp.wait_send()

    return pl.pallas_call(
        body, name=name, in_specs=[ANY] * n, out_specs=[ANY] * n,
        out_shape=[jax.ShapeDtypeStruct(h.shape, h.dtype) for h in halves],
        input_output_aliases={t: t for t in range(n)},
        scratch_shapes=[pltpu.SemaphoreType.DMA((n,)), pltpu.SemaphoreType.DMA((n,))],
    )(*halves)


def _all_gather_small(block, name):
    m_per, n = block.shape

    def body(x_ref, out_ref, send_sems, recv_sems, local_sem):
        x, y, c, chips = _position()
        me, sibling = (x, y, c), (x, y, 1 - c)

        def rows(px, py, pc):
            return out_ref.at[pl.ds((4 * px + 2 * py + pc) * m_per, m_per), :]

        def copy(k, blk, to, src=None):
            return pltpu.make_async_remote_copy(
                src_ref=rows(*blk) if src is None else src, dst_ref=rows(*blk), send_sem=send_sems.at[k],
                recv_sem=recv_sems.at[k], device_id=to, device_id_type=MESH)

        mine = pltpu.make_async_copy(x_ref, rows(*me), local_sem)
        mine.start()
        first = [copy(0, me, sibling, src=x_ref)]
        first += [copy(1 + j, me, (*chip, c), src=x_ref) for j, chip in enumerate(chips)]
        for cp in first:
            cp.start()
        passed = [copy(4 + j, (*chip, c), sibling) for j, chip in enumerate(chips)]
        for j, chip in enumerate(chips):
            copy(1 + j, (*chip, c), me).wait_recv()
            passed[j].start()
        copy(0, sibling, me).wait_recv()
        for j, chip in enumerate(chips):
            copy(4 + j, (*chip, 1 - c), me).wait_recv()
        for cp in first + passed:
            cp.wait_send()
        mine.wait()

    return pl.pallas_call(
        body, name=name, out_shape=jax.ShapeDtypeStruct((N_DEV * m_per, n), block.dtype),
        in_specs=[pl.BlockSpec(memory_space=pltpu.VMEM)], out_specs=pl.BlockSpec(memory_space=pltpu.VMEM),
        scratch_shapes=[pltpu.SemaphoreType.DMA((7,)), pltpu.SemaphoreType.DMA((7,)), pltpu.SemaphoreType.DMA],
    )(block)


def _row_tile(R, n):
    budget = 1 << 19
    if R * n <= budget or R % SUBLANES:
        return R
    t = R
    while t * n > budget and t % (2 * SUBLANES) == 0:
        t //= 2
    return t


def _pair_sum(g, recv, c_arr, name):
    _, _, R, n = g.shape
    tr = _row_tile(R, n)

    def body(c_ref, g_ref, r_ref, o_ref):
        o_ref[...] = (g_ref[...] + r_ref[...]).astype(ICI_DTYPE)

    grid_spec = pltpu.PrefetchScalarGridSpec(
        num_scalar_prefetch=1, grid=(N_CHIPS, R // tr),
        in_specs=[pl.BlockSpec((None, None, tr, n), lambda p, i, c: (p, c[0], i, 0)),
                  pl.BlockSpec((None, tr, n), lambda p, i, c: (p, i, 0))],
        out_specs=pl.BlockSpec((None, tr, n), lambda p, i, c: (p, i, 0)))
    return pl.pallas_call(
        body, name=name, grid_spec=grid_spec, out_shape=jax.ShapeDtypeStruct(recv.shape, ICI_DTYPE),
        compiler_params=_cparams("parallel", "parallel"),
    )(c_arr, g, recv)


def _chip_sum(recv, own, where, name):
    _, R, n = recv.shape
    tr = _row_tile(R, n)

    def body(w_ref, r0, r1, r2, r3, own_ref, o_ref):
        me = w_ref[0]
        terms = [jnp.where(me == k, own_ref[...], r[...]).astype(F32) for k, r in enumerate((r0, r1, r2, r3))]
        o_ref[...] = ((terms[0] + terms[1]) + terms[2]) + terms[3]

    def slot(k):
        return pl.BlockSpec((None, tr, n), lambda i, w: (w[1 + k], i, 0))

    grid_spec = pltpu.PrefetchScalarGridSpec(
        num_scalar_prefetch=1, grid=(R // tr,),
        in_specs=[slot(0), slot(1), slot(2), slot(3), pl.BlockSpec((None, tr, n), lambda i, w: (w[0], i, 0))],
        out_specs=pl.BlockSpec((None, tr, n), lambda i, w: (w[5], i, 0)))
    return pl.pallas_call(
        body, name=name, grid_spec=grid_spec, out_shape=jax.ShapeDtypeStruct((2, R, n), F32),
        compiler_params=_cparams("parallel"),
    )(where, recv, recv, recv, recv, own)


ADAM_C1 = 1.0 / (1.0 - ADAM_B1 ** ADAM_STEP)
ADAM_C2 = 1.0 / (1.0 - ADAM_B2 ** ADAM_STEP)


def _adamw_math(w, g, m, v):
    m = ADAM_B1 * m + (1.0 - ADAM_B1) * g
    v = ADAM_B2 * v + (1.0 - ADAM_B2) * (g * g)
    delta = -ADAM_LR * ((m * ADAM_C1) / (jnp.sqrt(v * ADAM_C2) + ADAM_EPS) + ADAM_WD * w)
    return delta, m, v


def _adamw(w, g, m, v, name):
    R, n = w.shape
    tr = _row_tile(R, n)

    def body(w_ref, g_ref, m_ref, v_ref, d_ref, nm_ref, nv_ref):
        d_ref[...], nm_ref[...], nv_ref[...] = _adamw_math(w_ref[...], g_ref[...], m_ref[...], v_ref[...])

    spec = pl.BlockSpec((tr, n), lambda i: (i, 0))
    return pl.pallas_call(
        body, name=name, grid=(R // tr,), in_specs=[spec] * 4, out_specs=[spec] * 3,
        out_shape=[jax.ShapeDtypeStruct((R, n), F32)] * 3, compiler_params=_cparams("parallel"),
    )(w, g, m, v)


def _adamw_rows(w, g, m, v, row0, prev, name):
    R, n = w.shape
    rows = g.shape[0]
    tr = _row_tile(math.gcd(row0, rows), n)
    off = row0 // tr

    def body(*refs):
        w_ref, g_ref, m_ref, v_ref = refs[:4]
        go_ref, d_ref, nm_ref, nv_ref = refs[-4:]
        gv = g_ref[...]
        go_ref[...] = gv
        d_ref[...], nm_ref[...], nv_ref[...] = _adamw_math(w_ref[...], gv, m_ref[...], v_ref[...])

    at = pl.BlockSpec((tr, n), lambda i: (i + off, 0))
    carried = [] if prev is None else list(prev)
    return pl.pallas_call(
        body, name=name, grid=(rows // tr,),
        in_specs=[at, pl.BlockSpec((tr, n), lambda i: (i, 0)), at, at] + [ANY] * len(carried),
        out_specs=[at] * 4, out_shape=[jax.ShapeDtypeStruct((R, n), F32)] * 4,
        input_output_aliases={4 + i: i for i in range(len(carried))},
        compiler_params=_cparams("parallel"),
    )(w, g, m, v, *carried)


def _adamw_small(w, gall, m, v, name):
    M, n = w.shape

    def body(w_ref, g_ref, m_ref, v_ref, gs_ref, d_ref, nm_ref, nv_ref):
        g = g_ref[0:M, :]
        for d in range(1, N_DEV):
            g = g + g_ref[d * M:(d + 1) * M, :]
        gs_ref[...] = g
        d_ref[...], nm_ref[...], nv_ref[...] = _adamw_math(w_ref[...], g, m_ref[...], v_ref[...])

    return pl.pallas_call(
        body, name=name, out_shape=[jax.ShapeDtypeStruct((M, n), F32)] * 4,
    )(w, gall, m, v)


SMALL_ROWS = 24
MID_ROWS = 4


def _pack_small(ln1_g, ln1_b, ln2_g, ln2_b, norm_w, sinks, a_log, dt_bias):
    mixed = jnp.concatenate([norm_w.reshape(-1), sinks.reshape(-1), a_log.reshape(-1), dt_bias.reshape(-1)])
    mixed = jnp.pad(mixed, (0, D_MODEL - mixed.shape[0]))[None]
    pad = jnp.zeros((SMALL_ROWS - 4 * DEPTH - 1, D_MODEL), F32)
    return jnp.concatenate([ln1_g, ln1_b, ln2_g, ln2_b, mixed, pad], axis=0)


def _unpack_small(p):
    mixed = p[4 * DEPTH]
    return (p[0:4], p[4:8], p[8:12], p[12:16], mixed[0:256].reshape(2, 128), mixed[256:272].reshape(2, 8),
            mixed[272:280].reshape(2, 4), mixed[280:288].reshape(2, 4))


def _pack_mid(conv_w, rconv_w, rconv_b, b_a, b_x, lam):
    lead = conv_w.shape[0]
    flat = jnp.concatenate([conv_w.reshape(lead, -1), rconv_w.reshape(lead, -1), rconv_b, b_a, b_x, lam], axis=1)
    return jnp.pad(flat, ((0, 0), (0, MID_ROWS * D_MODEL - flat.shape[1]))).reshape(lead, MID_ROWS, D_MODEL)


def _unpack_mid(p):
    lead = p.shape[:-2]
    f = p.reshape(lead + (MID_ROWS * D_MODEL,))
    return (f[..., 0:1536].reshape(lead + (4, 384)), f[..., 1536:2560].reshape(lead + (4, 256)),
            f[..., 2560:2816], f[..., 2816:3072], f[..., 3072:3328], f[..., 3328:3584])


def _cols_from_chips(g):
    p, L, R, n = g.shape
    return g.transpose(1, 2, 0, 3).reshape(L, R, p * n)


def _rows_from_chips(g):
    p, L, R, n = g.shape
    return g.transpose(1, 0, 2, 3).reshape(L, p * R, n)


def _cols_to_chips(g):
    L, R, n4 = g.shape
    return g.reshape(L, R, N_CHIPS, n4 // N_CHIPS).transpose(2, 0, 1, 3)


def _rows_to_chips(g):
    L, R4, n = g.shape
    return g.reshape(L, N_CHIPS, R4 // N_CHIPS, n).transpose(1, 0, 2, 3)


def _halves(a):
    return a.reshape(2, -1, a.shape[-1])


def _pad_hyb_cols(w):
    z = jnp.zeros(w.shape[:-1] + (HYB_PAD - HP_BG - 2 * B_HEADS,), w.dtype)
    return jnp.concatenate([w[..., 0:512], w[..., 768:2304], w[..., 2304:2816], w[..., 512:768], w[..., 2816:2824], z], axis=-1)


def _unpad_hyb_cols(w):
    return jnp.concatenate([w[..., 0:512], w[..., 2560:2816], w[..., 512:2048], w[..., 2048:2560], w[..., 2816:2824]], axis=-1)


def _hybrid_fwd(x, W, j, tables, sfx, out_ready, ln):
    cos, sin_s = tables
    T = x.shape[0]
    N = T // B_CHUNK
    proj = _matmul(x, W["hyb_w_in"][j], "nn", "hyb_in" + sfx)
    sinks_b = jnp.broadcast_to(W["hyb_sinks"][j][:, None], (A_Q_HEADS, LANES))
    o_a, lse = _attn_fwd(proj, cos, sin_s, sinks_b, "attn_fwd" + sfx)
    qkvc = _dnconv_fwd(proj, W["hyb_conv_w"][j], "dnconv_fwd" + sfx)
    bg = proj[:, HP_BG:HP_BG + 2 * B_HEADS]
    beta = jax.nn.sigmoid(bg[:, :B_HEADS])
    pre = bg[:, B_HEADS:] + W["hyb_dt_bias"][j][None]
    g = -jnp.exp(W["hyb_a_log"][j])[None] * jax.nn.softplus(pre)
    grow = g.T.reshape(B_HEADS, N, 1, B_CHUNK)
    brow = beta.T.reshape(B_HEADS, N, 1, B_CHUNK)
    nw = W["hyb_norm_w"][j][None]
    o_b, states, invs = _dn_fwd(qkvc, proj, nw, grow, brow, "dn_fwd" + sfx)
    mix = jnp.concatenate([o_a, o_b], axis=1).astype(ACT_DTYPE)
    out_ready(mix)
    out = _matmul(mix, W["hyb_w_out"][j], "nn", "hyb_out" + sfx, epi=_epi_residual_ln, extra=ln, out_dtype=_ln_out(),
                  tm=512)
    res = dict(proj=proj, o_a=o_a, lse=lse, qkvc=qkvc, beta=beta, pre=pre, g=g, grow=grow, brow=brow,
               states=states, invs=invs, mix=mix, sinks_b=sinks_b, nw=nw)
    return out, res


def _hybrid_bwd(x, du, dub, W, j, res, tables, sfx):
    cos, sin_s = tables
    T = x.shape[0]
    proj = res["proj"]
    d_wout = _matmul(res["mix"], dub, "tn", "hyb_out_dw" + sfx)
    dmix = _matmul(dub, W["hyb_w_out"][j], "nt", "hyb_out_dx" + sfx)
    dproj, dkc, dkp, dvc, dvp, dsink = _attn_bwd(proj, cos, sin_s, res["sinks_b"], res["o_a"], res["lse"], dmix,
                                                  "attn_bwd" + sfx)
    zpad = jnp.zeros((WINDOW, LANES), F32)
    dk = dkc + jnp.concatenate([dkp[WINDOW:], zpad], axis=0)
    dv = dvc + jnp.concatenate([dvp[WINDOW:], zpad], axis=0)
    dqkvc, dproj, dg4, dbeta4, dnw = _dn_bwd(res["qkvc"], proj, res["nw"], res["grow"], res["brow"], res["states"],
                                             res["invs"], dmix, dproj, "dn_bwd" + sfx)
    dproj, dconv = _dnconv_bwd(proj, W["hyb_conv_w"][j], dqkvc, dproj, "dnconv_bwd" + sfx)
    dg = dg4.reshape(B_HEADS, T).T
    dbeta = dbeta4.reshape(B_HEADS, T).T
    beta = res["beta"]
    dbeta_logit = dbeta * beta * (1.0 - beta)
    da_logit = dg * (-jnp.exp(W["hyb_a_log"][j]))[None] * jax.nn.sigmoid(res["pre"])
    d_dt_bias = jnp.sum(da_logit, axis=0)
    d_a_log = jnp.sum(dg * res["g"], axis=0)
    zcols = jnp.zeros((T, HYB_PAD - HP_BG - 2 * B_HEADS), F32)
    tail = jnp.concatenate([dk, dv, dbeta_logit, da_logit, zcols], axis=1).astype(ACT_DTYPE)
    dproj = lax.dynamic_update_slice(dproj, tail, (0, HP_K))
    d_win = _matmul(x, dproj, "tn", "hyb_in_dw" + sfx)
    dx = _matmul(dproj, W["hyb_w_in"][j], "nt", "hyb_in_dx" + sfx, epi=_epi_add_residual, extra=du)
    grads = dict(hyb_w_in=d_win, hyb_w_out=d_wout, hyb_sinks=dsink[0], hyb_conv_w=dconv, hyb_a_log=d_a_log,
                 hyb_dt_bias=d_dt_bias, hyb_norm_w=jnp.sum(dnw[:, 0, :], axis=0))
    return dx, grads


def _rec_fwd(x, W, j, sfx, ln):
    proj = _matmul(x, W["rec_w_in"][j], "nn", "rec_in" + sfx)
    sp = jax.nn.softplus(-W["rec_lambda"][j])[None]
    hg, h = _rglru_fwd(proj, W["rec_conv_w"][j], W["rec_conv_b"][j][None], W["rec_w_a"][j], W["rec_w_x"][j],
                       W["rec_b_a"][j][None], W["rec_b_x"][j][None], sp, "rglru_fwd" + sfx)
    out = _matmul(hg, W["rec_w_out"][j], "nn", "rec_out" + sfx, epi=_epi_residual_ln, extra=ln, out_dtype=_ln_out(),
                  tm=512)
    return out, dict(proj=proj, hg=hg, h=h, sp=sp)


def _rec_bwd(x, du, dub, W, j, res, sfx):
    d_wout = _matmul(res["hg"], dub, "tn", "rec_out_dw" + sfx)
    dhg = _matmul(dub, W["rec_w_out"][j], "nt", "rec_out_dx" + sfx)
    dproj, dcw, dcb, dwa, dwx, dba, dbx, dsp = _rglru_bwd(
        res["proj"], W["rec_conv_w"][j], W["rec_conv_b"][j][None], W["rec_w_a"][j], W["rec_w_x"][j],
        W["rec_b_a"][j][None], W["rec_b_x"][j][None], res["sp"], res["h"], dhg, "rglru_bwd" + sfx)
    d_lam = dsp[0] * (-jax.nn.sigmoid(-W["rec_lambda"][j]))
    d_win = _matmul(x, dproj, "tn", "rec_in_dw" + sfx)
    dx = _matmul(dproj, W["rec_w_in"][j], "nt", "rec_in_dx" + sfx, epi=_epi_add_residual, extra=du)
    grads = dict(rec_w_in=d_win, rec_w_out=d_wout, rec_conv_w=dcw, rec_conv_b=dcb[0], rec_w_a=dwa, rec_w_x=dwx,
                 rec_b_a=dba[0], rec_b_x=dbx[0], rec_lambda=d_lam)
    return dx, grads


def _local_step(x, tgt, W, mlp_w, mixer_ready, on_group):
    T = x.shape[0]
    tables = _rope_tables(T)
    acts = []
    xb = x.astype(ACT_DTYPE)
    for layer in range(DEPTH):
        j, sfx = layer // 2, ""
        last = layer == DEPTH - 1
        mixer_ready(layer, xb)
        ln1 = (xb if layer else x, W["ln1_g"][layer][None], W["ln1_b"][layer][None])
        if layer % 2 == 0:
            (x1b, u1), res = _hybrid_fwd(xb, W, j, tables, sfx,
                                         functools.partial(mixer_ready, layer, out_projection=True), ln1)
        else:
            (x1b, u1), res = _rec_fwd(xb, W, j, sfx, ln1)
        w1, w2, wl = mlp_w(layer, x1b)
        h1 = _matmul(x1b, w1, "nn", "mlp_up", out_dtype=ACT_DTYPE, b_chips=("j", wl))
        out = _matmul(h1, w2, "nn", "mlp_down", a_fn=_relu2, b_chips=("k", wl),
                      epi=_epi_residual_ln_last if last else _epi_residual_ln,
                      extra=(x1b, W["ln2_g"][layer][None], W["ln2_b"][layer][None]), out_dtype=_ln_out(last))
        acts.append(dict(xb=xb, res=res, u1=u1, x1b=x1b, h1=h1, u2=out[-1]))
        xb = out[-2]
    dx, loss = _loss_head(out[0], tgt, "loss_head")
    per_layer = [None] * DEPTH
    d_w1 = dict(lower=lax.empty((N_CHIPS, 1, D_MODEL, D_FF // N_CHIPS), F32),
                upper=lax.empty((N_CHIPS, DEPTH - 1, D_MODEL, D_FF // N_CHIPS), F32))
    d_w2 = dict(lower=lax.empty((N_CHIPS, 1, D_FF // N_CHIPS, D_MODEL), F32),
                upper=lax.empty((N_CHIPS, DEPTH - 1, D_FF // N_CHIPS, D_MODEL), F32))
    token = None
    for layer in reversed(range(DEPTH)):
        j, a = layer // 2, acts[layer]
        grp, li = ("upper", layer - 1) if layer else ("lower", 0)
        ln2_g = W["ln2_g"][layer][None]
        if token is not None:
            ln2_g = ln2_g + token
        du2b, dg2, db2 = _ln_bwd(a["u2"], ln2_g, dx, "ln_bwd")
        w1, w2, wl = mlp_w(layer, du2b)
        d_w2[grp] = _matmul(a["h1"], du2b, "tn", "mlp_down_dw", a_fn=_relu2, out_chips=("i", li, d_w2[grp]))
        dh1 = _matmul(du2b, w2, "nt", "mlp_down_dx", epi=_epi_drelu2, extra=a["h1"], out_dtype=ACT_DTYPE,
                      b_chips=("j", wl))
        d_w1[grp] = _matmul(a["x1b"], dh1, "tn", "mlp_up_dw", out_chips=("j", li, d_w1[grp]))
        dx1 = _matmul(dh1, w1, "nt", "mlp_up_dx", epi=_epi_add_residual, extra=du2b, b_chips=("k", wl))
        ln1_g = W["ln1_g"][layer][None]
        token = on_group("middle", dx1, None, None, None) if layer == 0 else None
        if token is not None:
            ln1_g = ln1_g + token
        du1b, dg1, db1 = _ln_bwd(a["u1"], ln1_g, dx1, "ln_bwd")
        if layer % 2 == 0:
            dx, g = _hybrid_bwd(a["xb"], du1b, du1b, W, j, a["res"], tables, "")
        else:
            dx, g = _rec_bwd(a["xb"], du1b, du1b, W, j, a["res"], "")
        g.update(ln1_g=dg1[0], ln1_b=db1[0], ln2_g=dg2[0], ln2_b=db2[0])
        per_layer[layer] = g
        if layer == 1:
            token = on_group("upper", [per_layer[2]], [per_layer[1], per_layer[3]], d_w1["upper"], d_w2["upper"])
        elif layer == 0:
            on_group("lower", [per_layer[0]], [], d_w1["lower"], d_w2["lower"])
    grads = {}
    for name in ("ln1_g", "ln1_b", "ln2_g", "ln2_b"):
        grads[name] = jnp.stack([per_layer[l][name] for l in range(DEPTH)])
    for name in ("hyb_norm_w", "hyb_sinks", "hyb_a_log", "hyb_dt_bias"):
        grads[name] = jnp.stack([per_layer[l][name] for l in (0, 2)])
    return loss, dx, grads


BIG = ("hyb_w_in", "hyb_w_out", "rec_w_in", "rec_w_out", "mlp_w1", "mlp_w2", "rec_w_a", "rec_w_x")
COL_SHARDED = ("hyb_w_in", "rec_w_in", "mlp_w1")
CHIP_MAJOR = ("mlp_w1", "mlp_w2")
MID = ("hyb_conv_w", "rec_conv_w", "rec_conv_b", "rec_b_a", "rec_b_x", "rec_lambda")
SMALL = ("ln1_g", "ln1_b", "ln2_g", "ln2_b", "hyb_norm_w", "hyb_sinks", "hyb_a_log", "hyb_dt_bias")
WEIGHTS = ("hyb_w_in", "hyb_sinks", "hyb_conv_w", "hyb_a_log", "hyb_dt_bias", "hyb_norm_w", "hyb_w_out", "rec_w_in",
           "rec_conv_w", "rec_conv_b", "rec_w_a", "rec_b_a", "rec_w_x", "rec_b_x", "rec_lambda", "rec_w_out", "ln1_g",
           "ln1_b", "mlp_w1", "mlp_w2", "ln2_g", "ln2_b")


def _gather_full_weights(w):
    wb = {k: w[k].astype(MXU_DTYPE) for k in BIG}
    now = ("hyb_w_in",)
    shards = [_halves(wb[k][:1]) for k in now]
    shards.append(_pack_mid(*[w[k] for k in MID]))
    got = _all_gather_weights(shards, "all_gather_weights")
    me = 2 * lax.axis_index("x") + lax.axis_index("y")
    got = [lax.dynamic_update_slice(g, s[None], (me, 0, 0, 0)) for s, g in zip(shards, got)]

    def full(k, g):
        g = g.reshape((N_CHIPS,) + w[k].shape[1:])
        if k in CHIP_MAJOR:
            return g[:, None]
        if k in ("rec_w_a", "rec_w_x"):
            return g.transpose(1, 0, 2, 3).reshape(LRU_BLOCKS, LRU_BLOCK_W, LRU_BLOCK_W)
        f = _cols_from_chips(g[:, None])[0] if k in COL_SHARDED else _rows_from_chips(g[:, None])[0]
        return _pad_hyb_cols(f) if k == "hyb_w_in" else f

    rec = ("rec_w_in", "rec_w_out", "rec_w_a", "rec_w_x")
    groups = [(("hyb_w_out",), 0), (CHIP_MAJOR, 0), (rec, 0), (CHIP_MAJOR, 1), (("hyb_w_in", "hyb_w_out"), 1),
              (CHIP_MAJOR, 2), (rec, 1), (CHIP_MAJOR, 3)]
    own = [wb[k][j] for names, j in groups for k in names]
    land = [lax.dynamic_update_slice(lax.empty((N_CHIPS,) + o.shape, o.dtype), o[None], (me,) + (0,) * o.ndim)
            for o in own]
    send_sems, recv_sems, own, land = _gather_start(own, land, got[-1], "gather_start")
    W = {k: [None] * w[k].shape[0] for k in BIG}
    for k, g in zip(now, got[:-1]):
        W[k][0] = full(k, g)
    arrived = [0]

    def ensure(upto, after):
        while arrived[0] <= upto:
            gi = arrived[0]
            names, j = groups[gi]
            lo = sum(len(nm) for nm, _ in groups[:gi])
            sl = slice(lo, lo + len(names))
            got_g = _gather_wait(send_sems[sl], recv_sems[sl], own[sl], land[sl], after, "gather_wait_%d" % gi)
            for k, g in zip(names, got_g):
                W[k][j] = full(k, g)
            arrived[0] += 1

    def mixer_ready(layer, after, out_projection=False):
        if layer:
            ensure({1: 2, 2: 4, 3: 6}[layer], after)
        elif out_projection:
            ensure(0, after)

    def mlp_w(layer, after):
        ensure({0: 1, 1: 3, 2: 5, 3: 7}[layer], after)
        return W["mlp_w1"][layer], W["mlp_w2"][layer], 0

    conv_w, rconv_w, rconv_b, b_a, b_x, lam = _unpack_mid(got[-1])
    W["hyb_conv_w"] = conv_w.transpose(1, 2, 0, 3).reshape(2, CONV_K, 3 * B_W)
    W["rec_conv_w"] = rconv_w.transpose(1, 2, 0, 3).reshape(2, CONV_K, D_MODEL)
    for k, v in (("rec_conv_b", rconv_b), ("rec_b_a", b_a), ("rec_b_x", b_x), ("rec_lambda", lam)):
        W[k] = v.transpose(1, 0, 2).reshape(2, D_MODEL)
    for k in SMALL:
        W[k] = w[k]
    return W, mlp_w, mixer_ready


REC_VECTORS = ("rec_conv_b", "rec_b_a", "rec_b_x", "rec_lambda")


def _group_by_chip(hyb, rec, d_w1, d_w2):
    t = dict(mlp_w1=d_w1, mlp_w2=d_w2)
    if hyb:
        t["hyb_w_in"] = _cols_to_chips(jnp.stack([_unpad_hyb_cols(g["hyb_w_in"]) for g in hyb]))
        t["hyb_w_out"] = _rows_to_chips(jnp.stack([g["hyb_w_out"] for g in hyb]))
    if rec:
        t["rec_w_in"] = _cols_to_chips(jnp.stack([g["rec_w_in"] for g in rec]))
        t["rec_w_out"] = _rows_to_chips(jnp.stack([g["rec_w_out"] for g in rec]))
        for k in ("rec_w_a", "rec_w_x"):
            v = jnp.stack([g[k] for g in rec])
            t[k] = v.reshape(len(rec), LRU_BLOCKS, N_CHIPS, LRU_BLOCK_W // N_CHIPS, LRU_BLOCK_W).transpose(2, 0, 1, 3, 4)
    flat = [g["hyb_conv_w"].reshape(CONV_K, N_CHIPS, -1).transpose(1, 0, 2).reshape(N_CHIPS, -1) for g in hyb]
    for g in rec:
        flat.append(g["rec_conv_w"].reshape(CONV_K, N_CHIPS, -1).transpose(1, 0, 2).reshape(N_CHIPS, -1))
        flat += [g[k].reshape(N_CHIPS, -1) for k in REC_VECTORS]
    flat = jnp.concatenate(flat, axis=1)
    flat = jnp.pad(flat, ((0, 0), (0, -flat.shape[1] % (2 * D_MODEL))))
    names = [k for k in BIG if k in t]
    out = [t[k].reshape(N_CHIPS, 2, -1, t[k].shape[-1]) for k in names]
    return names + ["small"], out + [flat.reshape(N_CHIPS, 2, -1, D_MODEL)]


def kernel(x, hyb_w_in, hyb_sinks, hyb_conv_w, hyb_a_log, hyb_dt_bias, hyb_norm_w, hyb_w_out, rec_w_in, rec_conv_w, rec_conv_b, rec_w_a, rec_b_a, rec_w_x, rec_b_x, rec_lambda, rec_w_out, ln1_g, ln1_b, mlp_w1, mlp_w2, ln2_g, ln2_b, loss_target, m_hyb_w_in, m_hyb_sinks, m_hyb_conv_w, m_hyb_a_log, m_hyb_dt_bias, m_hyb_norm_w, m_hyb_w_out, m_rec_w_in, m_rec_conv_w, m_rec_conv_b, m_rec_w_a, m_rec_b_a, m_rec_w_x, m_rec_b_x, m_rec_lambda, m_rec_w_out, m_ln1_g, m_ln1_b, m_mlp_w1, m_mlp_w2, m_ln2_g, m_ln2_b, v_hyb_w_in, v_hyb_sinks, v_hyb_conv_w, v_hyb_a_log, v_hyb_dt_bias, v_hyb_norm_w, v_hyb_w_out, v_rec_w_in, v_rec_conv_w, v_rec_conv_b, v_rec_w_a, v_rec_b_a, v_rec_w_x, v_rec_b_x, v_rec_lambda, v_rec_w_out, v_ln1_g, v_ln1_b, v_mlp_w1, v_mlp_w2, v_ln2_g, v_ln2_b):
    args = locals()
    w = {k: args[k] for k in WEIGHTS}
    m = {k: args["m_" + k] for k in WEIGHTS}
    v = {k: args["v_" + k] for k in WEIGHTS}

    W, mlp_w, mixer_ready = _gather_full_weights(w)

    core = lax.axis_index("c").astype(jnp.int32)
    me = (2 * lax.axis_index("x") + lax.axis_index("y")).astype(jnp.int32)
    slots = jnp.arange(N_CHIPS, dtype=jnp.int32)
    where = jnp.concatenate([me[None], jnp.where(slots == me, (slots + 1) % N_CHIPS, slots), core[None]])
    state = {}

    def on_group(group, hyb, rec, d_w1, d_w2):
        if group == "middle":
            group = "upper"
            send_sems, recv_sems, by_chip, recv, _ = state["sibling"]
            by_chip, from_sibling = _rs_sibling_wait(send_sems, recv_sems, by_chip, recv, hyb, "rs_sibling_wait")
        else:
            state[group + " names"], by_chip = _group_by_chip(hyb, rec, d_w1, d_w2)
            if group == "upper":
                recv = [lax.empty((N_CHIPS,) + g.shape[2:], g.dtype) for g in by_chip]
                state["sibling"] = _rs_sibling_start(by_chip, recv, "rs_sibling_start")
                return state["sibling"][4][0, 0]
            from_sibling = _rs_to_sibling(by_chip, "rs_to_sibling")
        pair = [_pair_sum(g, r, core[None], "pair_sum") for g, r in zip(by_chip, from_sibling)]
        recv = [lax.empty(p.shape, p.dtype) for p in pair]
        state[group] = _rs_across_start(pair, recv, "rs_across_start_" + group)
        return state[group][4][0, 0]

    def finish_group(group, after, prev):
        send_sems, recv_sems, pair, recv, _ = state[group]
        pair, from_chips = _rs_across_wait(send_sems, recv_sems, pair, recv, after, "rs_across_wait_" + group)
        half = [_chip_sum(r, p, where, "chip_sum") for r, p in zip(from_chips, pair)]
        joined = _rs_join_halves(half, "rs_join_halves")
        outs = dict(prev or {})
        for k, g in zip(state[group + " names"][:-1], joined[:-1]):
            n = g.shape[-1]
            w2, g2 = w[k].reshape(-1, n), g.reshape(-1, n)
            row0 = w2.shape[0] - g2.shape[0] if group == "upper" else 0
            outs[k] = _adamw_rows(w2, g2, m[k].reshape(-1, n), v[k].reshape(-1, n), row0, outs.get(k), "adamw")
        return outs, joined[-1].reshape(-1)

    loss, dx, grads = _local_step(x[0], loss_target[0], W, mlp_w, mixer_ready, on_group)
    loss = lax.psum(loss[0, 0], ("x", "y", "c"))
    g_out, d_out, m_out, v_out = {}, {}, {}, {}

    upper, small_upper = finish_group("upper", state["lower"][4], None)
    small_g = _pack_small(*[grads[k] for k in SMALL])
    small_all = _all_gather_small(small_g, "all_gather_small")
    sw, sm, sv = (_pack_small(*[t[k] for k in SMALL]) for t in (w, m, v))
    sg, sd, snm, snv = _adamw_small(sw, small_all, sm, sv, "adamw_small")
    for dst, packed in ((g_out, sg), (d_out, sd), (m_out, snm), (v_out, snv)):
        for k, val in zip(SMALL, _unpack_small(packed)):
            dst[k] = val

    done = sum([upper[k][1][0, 0] for k in BIG], sg[0, 0]).reshape(1, 1)
    both, small_lower = finish_group("lower", done, upper)
    for k in BIG:
        shape = w[k].shape
        g_out[k], d_out[k], m_out[k], v_out[k] = (t.reshape(shape) for t in both[k])
    n_conv, n_rconv, n_vec = (w[k][0].size for k in ("hyb_conv_w", "rec_conv_w", "rec_conv_b"))
    n_rec = n_rconv + len(REC_VECTORS) * n_vec
    conv_g = jnp.stack([small_lower[:n_conv], small_upper[:n_conv]])
    rec_g = jnp.stack([small_upper[n_conv + i * n_rec:n_conv + (i + 1) * n_rec] for i in range(2)])
    vec_g = [rec_g[:, n_rconv + i * n_vec:n_rconv + (i + 1) * n_vec] for i in range(len(REC_VECTORS))]
    mid_g = _pack_mid(conv_g, rec_g[:, :n_rconv], *vec_g).reshape(-1, D_MODEL)
    mid_w, mid_m, mid_v = (_pack_mid(*[t[k] for k in MID]).reshape(-1, D_MODEL) for t in (w, m, v))
    mid_d, mid_nm, mid_nv = _adamw(mid_w, mid_g, mid_m, mid_v, "adamw_mid")
    for dst, packed in ((g_out, mid_g), (d_out, mid_d), (m_out, mid_nm), (v_out, mid_nv)):
        for k, val in zip(MID, _unpack_mid(packed.reshape(2, MID_ROWS, D_MODEL))):
            dst[k] = val.reshape(w[k].shape)

    return (loss, dx[None], *[g_out[k] for k in WEIGHTS], *[d_out[k] for k in WEIGHTS],
            *[m_out[k] for k in WEIGHTS], *[v_out[k] for k in WEIGHTS])
```

```python
import functools
import math

import jax
import jax.numpy as jnp
import numpy as np
from jax import lax
from jax.experimental import pallas as pl
from jax.experimental.pallas import tpu as pltpu

F32 = jnp.float32
MXU_DTYPE = jnp.bfloat16
ACT_DTYPE = jnp.bfloat16
ICI_DTYPE = jnp.bfloat16

D_MODEL = 1024
DEPTH = 4
A_HEAD_DIM = 64
A_Q_HEADS = 8
A_KV_HEADS = 2
WINDOW = 128
ROPE_THETA = 10000.0
B_HEADS = 4
B_HEAD_DIM = 128
B_CHUNK = 64
CONV_K = 4
LRU_BLOCKS = 4
LRU_BLOCK_W = D_MODEL // LRU_BLOCKS
LRU_C = 8.0
D_FF = 4 * D_MODEL
A_Q_W = A_Q_HEADS * A_HEAD_DIM
A_KV_W = A_KV_HEADS * A_HEAD_DIM
B_W = B_HEADS * B_HEAD_DIM
HYB_PROJ = A_Q_W + 2 * A_KV_W + 4 * B_W + 2 * B_HEADS
DN_ALPHA = (2 * DEPTH) ** 0.25
LN_EPS = 1e-5
NORM_EPS = 1e-6
ADAM_LR = 0.001
ADAM_B1 = 0.9
ADAM_B2 = 0.999
ADAM_EPS = 1e-08
ADAM_WD = 0.01
ADAM_STEP = 10

HP_Q = 0
HP_QKVB = 512
HP_Z = 2048
HP_K = 2560
HP_V = 2688
HP_BG = 2816
HYB_PAD = 3072

N_CHIPS = 4
N_DEV = 8
V7X_VMEM_LIMIT = 48 * 1024 * 1024
LANES = 128
SUBLANES = 8
NEG_BIG = -1e30

NN = (((1,), (0,)), ((), ()))
NT = (((1,), (1,)), ((), ()))
TN = (((0,), (0,)), ((), ()))


def _cparams(*sem):
    return pltpu.CompilerParams(dimension_semantics=sem, vmem_limit_bytes=V7X_VMEM_LIMIT)


def _dot(a, b, dims=NN):
    return lax.dot_general(a.astype(MXU_DTYPE), b.astype(MXU_DTYPE), dims, preferred_element_type=F32)


def _split_bf16(a):
    hi = a.astype(jnp.bfloat16)
    return hi, (a - hi.astype(F32)).astype(jnp.bfloat16)


def _dotf(a, b, dims=NN):
    ah, al = _split_bf16(a)
    bh, bl = _split_bf16(b)
    dg = functools.partial(lax.dot_general, dimension_numbers=dims, preferred_element_type=F32)
    return dg(ah, bh) + (dg(ah, bl) + dg(al, bh))


def _tile(dim, pref):
    t = min(dim, pref)
    while dim % t:
        t //= 2
    return t


def _sigmoid(x):
    return 1.0 / (1.0 + jnp.exp(-x))


def _silu(x):
    return x * _sigmoid(x)


def _dsilu(x):
    s = _sigmoid(x)
    return s * (1.0 + x * (1.0 - s))


GELU_C = 0.7978845608028654
GELU_A = 0.044715


def _gelu(x):
    return 0.5 * x * (1.0 + jnp.tanh(GELU_C * (x + GELU_A * x * x * x)))


def _dgelu(x):
    t = jnp.tanh(GELU_C * (x + GELU_A * x * x * x))
    return 0.5 * (1.0 + t) + 0.5 * x * (1.0 - t * t) * GELU_C * (1.0 + 3.0 * GELU_A * x * x)


def _matmul(a, b, mode, name, *, tm=1024, tn=1024, tk=1024, a_fn=None, epi=None, extra=None, out_dtype=F32,
            b_chips=None, out_chips=None):
    chunk = tk
    if mode == "tn":
        K, M = a.shape
        if K <= 4 * tk:
            tk, tm = K, tm // 2
    else:
        M, K = a.shape
    whole_k = False
    if b_chips is not None:
        g, b_layer = b_chips
        r, n = b.shape[2:]
        n_dim, k_dim = (r, n) if mode == "nt" else (n, r)
        N = N_CHIPS * n_dim if g == "j" else n_dim
        assert K == (N_CHIPS * k_dim if g == "k" else k_dim)
        if g == "j":
            tn = n_dim
        else:
            whole_k, tk, tm = True, K, tm // 2
    elif mode == "nt":
        N = b.shape[0]
        if tk < K <= 3 * tk:
            tk, tm = K, tm // 2
    else:
        N = b.shape[1]
    if out_chips is not None:
        og, o_layer, o_buf = out_chips
        if og == "j":
            tn = o_buf.shape[3]
        else:
            tm = min(tm, o_buf.shape[2])
    tm, tn, tk = _tile(M, tm), _tile(N, tn), _tile(K, tk)
    nk = K // tk
    if mode == "tn":
        a_spec = pl.BlockSpec((tk, tm), lambda i, j, k: (k, i))
    else:
        a_spec = pl.BlockSpec((tm, tk), lambda i, j, k: (i, k))
    b_block = (tn, tk) if mode == "nt" else (tk, tn)
    if b_chips is None:
        b_spec = pl.BlockSpec(b_block, (lambda i, j, k: (j, k)) if mode == "nt" else (lambda i, j, k: (k, j)))
    elif mode == "nt" and whole_k:
        b_spec = pl.BlockSpec((N_CHIPS, None, tn, k_dim), lambda i, j, k: (0, b_layer, j, 0))
    elif mode == "nt":
        b_spec = pl.BlockSpec((None, None) + b_block, lambda i, j, k: (j, b_layer, 0, k))
    elif whole_k:
        b_spec = pl.BlockSpec((N_CHIPS, None, k_dim, tn), lambda i, j, k: (0, b_layer, 0, j),
                              pipeline_mode=pl.Buffered(1) if N == tn else None)
    else:
        b_spec = pl.BlockSpec((None, None) + b_block, lambda i, j, k: (j, b_layer, k, 0))
    o_spec = pl.BlockSpec((tm, tn), lambda i, j, k: (i, j))
    e_spec = o_spec
    if out_chips is not None:
        per = o_buf.shape[2] // tm
        o_spec = pl.BlockSpec((None, None, tm, tn), (lambda i, j, k: (j, o_layer, i, 0)) if og == "j"
                              else (lambda i, j, k: (i // per, o_layer, i % per, j)))
    dims = {"nn": NN, "nt": NT, "tn": TN}[mode]
    extras = () if extra is None else (extra if isinstance(extra, tuple) else (extra,))
    out_dtypes = out_dtype if isinstance(out_dtype, tuple) else (out_dtype,)
    n_in = 2 + len(extras) + (out_chips is not None)

    def body(*refs):
        a_ref, b_ref = refs[0], refs[1]
        e_refs = refs[2:2 + len(extras)]
        o_refs = refs[n_in:n_in + len(out_dtypes)]
        av = a_ref[...]
        if a_fn is not None:
            av = a_fn(av)
        if whole_k and mode == "nt":
            part = _dot(av[:, :k_dim], b_ref[0], dims)
            for chip in range(1, N_CHIPS):
                part = part + _dot(av[:, chip * k_dim:(chip + 1) * k_dim], b_ref[chip], dims)
        elif mode == "tn" and tk > chunk:
            part = _dot(av[:chunk], b_ref[0:chunk, :], dims)
            for c0 in range(chunk, tk, chunk):
                part = part + _dot(av[c0:c0 + chunk], b_ref[c0:c0 + chunk, :], dims)
        else:
            bv = b_ref[...]
            if whole_k:
                bv = bv.reshape(K, tn)
            part = _dot(av, bv, dims)

        def finish(acc):
            if epi is not None:
                acc = epi(acc, *[e[...] for e in e_refs])
            for o_ref, val, dt in zip(o_refs, acc if isinstance(acc, tuple) else (acc,), out_dtypes):
                o_ref[...] = val.astype(dt)

        if nk == 1:
            finish(part)
        else:
            acc_ref = refs[-1]
            k = pl.program_id(2)

            @pl.when(k == 0)
            def _():
                acc_ref[...] = part

            @pl.when(k > 0)
            def _():
                acc_ref[...] += part

            @pl.when(k == nk - 1)
            def _():
                finish(acc_ref[...])

    row_spec = pl.BlockSpec((1, tn), lambda i, j, k: (0, j))
    in_specs = [a_spec, b_spec] + [row_spec if e.shape[0] == 1 else e_spec for e in extras]
    args = (a, b) + extras
    out_shape = [jax.ShapeDtypeStruct((M, N), dt) for dt in out_dtypes]
    out_specs = [o_spec] * len(out_dtypes)
    aliases = {}
    if out_chips is not None:
        in_specs.append(pl.BlockSpec(memory_space=pl.ANY))
        args += (o_buf,)
        out_shape = [jax.ShapeDtypeStruct(o_buf.shape, o_buf.dtype)]
        aliases = {n_in - 1: 0}
    if not isinstance(out_dtype, tuple):
        out_shape, out_specs = out_shape[0], out_specs[0]
    return pl.pallas_call(
        body, name=name, grid=(M // tm, N // tn, nk), in_specs=in_specs, out_specs=out_specs, out_shape=out_shape,
        input_output_aliases=aliases,
        scratch_shapes=[pltpu.VMEM((tm, tn), F32)] if nk > 1 else [],
        compiler_params=_cparams("parallel", "parallel", "arbitrary"),
    )(*args)


def _relu2(v):
    r = jnp.maximum(v, 0.0)
    return r * r


def _epi_drelu2(acc, h):
    return acc * (2.0 * jnp.maximum(h, 0.0))


def _epi_add_residual(acc, du):
    return acc + DN_ALPHA * du


def _epi_residual_ln(acc, x, g, b):
    u = DN_ALPHA * x + acc
    mu = jnp.mean(u, axis=-1, keepdims=True)
    d = u - mu
    var = jnp.mean(d * d, axis=-1, keepdims=True)
    return d * lax.rsqrt(var + LN_EPS) * g + b, u


def _epi_residual_ln_last(acc, x, g, b):
    o, u = _epi_residual_ln(acc, x, g, b)
    return o, o, u


def _ln_out(last=False):
    return (F32, ACT_DTYPE, F32) if last else (ACT_DTYPE, F32)


def _ln_bwd(u, g, dout, name):
    T, D = u.shape
    tr = _tile(T, 512)

    def body(u_ref, g_ref, d_ref, dub_ref, dg_ref, db_ref):
        i = pl.program_id(0)
        u = u_ref[...]
        mu = jnp.mean(u, axis=-1, keepdims=True)
        d = u - mu
        rstd = lax.rsqrt(jnp.mean(d * d, axis=-1, keepdims=True) + LN_EPS)
        xhat = d * rstd
        dout_v = d_ref[...]
        dxh = dout_v * g_ref[...]
        m1 = jnp.mean(dxh, axis=-1, keepdims=True)
        m2 = jnp.mean(dxh * xhat, axis=-1, keepdims=True)
        du = rstd * (dxh - m1 - xhat * m2)
        dub_ref[...] = du.astype(ACT_DTYPE)
        pg = jnp.sum(dout_v * xhat, axis=0, keepdims=True)
        pb = jnp.sum(dout_v, axis=0, keepdims=True)

        @pl.when(i == 0)
        def _():
            dg_ref[...] = pg
            db_ref[...] = pb

        @pl.when(i > 0)
        def _():
            dg_ref[...] += pg
            db_ref[...] += pb

    row = pl.BlockSpec((tr, D), lambda i: (i, 0))
    vec = pl.BlockSpec((1, D), lambda i: (0, 0))
    return pl.pallas_call(
        body, name=name, grid=(T // tr,), in_specs=[row, vec, row], out_specs=[row, vec, vec],
        out_shape=[jax.ShapeDtypeStruct((T, D), ACT_DTYPE),
                   jax.ShapeDtypeStruct((1, D), F32), jax.ShapeDtypeStruct((1, D), F32)],
        compiler_params=_cparams("arbitrary"),
    )(u, g, dout)


def _loss_head(y, tgt, name):
    T, D = y.shape
    tr = _tile(T, 512)

    def body(y_ref, t_ref, dy_ref, l_ref):
        i = pl.program_id(0)
        e = y_ref[...] - t_ref[...]
        dy_ref[...] = e * (1.0 / D)
        part = jnp.sum(e * e, axis=(0, 1), keepdims=True) * (0.5 / D)

        @pl.when(i == 0)
        def _():
            l_ref[...] = part

        @pl.when(i > 0)
        def _():
            l_ref[...] += part

    row = pl.BlockSpec((tr, D), lambda i: (i, 0))
    one = pl.BlockSpec((1, 1), lambda i: (0, 0))
    return pl.pallas_call(
        body, name=name, grid=(T // tr,), in_specs=[row, row], out_specs=[row, one],
        out_shape=[jax.ShapeDtypeStruct((T, D), F32), jax.ShapeDtypeStruct((1, 1), F32)],
        compiler_params=_cparams("arbitrary"),
    )(y, tgt)


def _swap_half(x):
    n = x.shape[-1]
    lane = lax.broadcasted_iota(jnp.int32, x.shape, 1)
    first = (lane % A_HEAD_DIM) < (A_HEAD_DIM // 2)
    return jnp.where(first, pltpu.roll(x, n - A_HEAD_DIM // 2, axis=1), pltpu.roll(x, A_HEAD_DIM // 2, axis=1))


def _rope(x, cos, sin_signed):
    return x * cos + _swap_half(x) * sin_signed


def _rope_t(dy, cos, sin_signed):
    return dy * cos + _swap_half(dy * sin_signed)


def _rope_tables(T):
    half = A_HEAD_DIM // 2
    inv_freq = np.float32(ROPE_THETA) ** (-np.arange(half, dtype=np.float32) / np.float32(half))
    ang = np.arange(T, dtype=np.float32)[:, None] * inv_freq[None, :]
    cos = np.tile(np.cos(ang), (1, 4))
    sin = np.sin(ang)
    sin_signed = np.tile(np.concatenate([-sin, sin], axis=1), (1, 2))
    return jnp.asarray(cos, F32), jnp.asarray(sin_signed, F32)


def _band_mask(n):
    qi = lax.broadcasted_iota(jnp.int32, (WINDOW, 2 * WINDOW), 0)
    kj = lax.broadcasted_iota(jnp.int32, (WINDOW, 2 * WINDOW), 1)
    return (kj > qi) & (kj <= qi + WINDOW) & ((n > 0) | (kj >= WINDOW))


def _place(v, src_half, dst_half):
    lane = lax.broadcasted_iota(jnp.int32, v.shape, 1)
    if src_half != dst_half:
        v = pltpu.roll(v, A_HEAD_DIM, axis=1)
    keep = (lane >= A_HEAD_DIM) if dst_half else (lane < A_HEAD_DIM)
    return jnp.where(keep, v, 0.0)


def _attn_specs():
    kb, vb = HP_K // LANES, HP_V // LANES
    prev = lambda n: jnp.maximum(n - 1, 0)
    return dict(
        q=pl.BlockSpec((WINDOW, A_Q_W), lambda n: (n, 0)),
        kc=pl.BlockSpec((WINDOW, LANES), lambda n: (n, kb)),
        kp=pl.BlockSpec((WINDOW, LANES), lambda n: (prev(n), kb)),
        vc=pl.BlockSpec((WINDOW, LANES), lambda n: (n, vb)),
        vp=pl.BlockSpec((WINDOW, LANES), lambda n: (prev(n), vb)),
        tq=pl.BlockSpec((WINDOW, LANES), lambda n: (n, 0)),
        tp=pl.BlockSpec((WINDOW, LANES), lambda n: (prev(n), 0)),
        sink=pl.BlockSpec((A_Q_HEADS, LANES), lambda n: (0, 0)),
        row512=pl.BlockSpec((WINDOW, A_Q_W), lambda n: (n, 0)),
        row128=pl.BlockSpec((WINDOW, LANES), lambda n: (n, 0)),
        lse=pl.BlockSpec((WINDOW, A_Q_HEADS), lambda n: (n, 0)),
    )


def _attn_fwd(proj, cos, sin_s, sinks_b, name):
    T = proj.shape[0]
    sp = _attn_specs()

    def body(q_ref, kc_ref, kp_ref, vc_ref, vp_ref, cq_ref, sq_ref, cp_ref, sp_ref, sink_ref, o_ref, l_ref):
        n = pl.program_id(0)
        cq, sq = cq_ref[...], sq_ref[...]
        cq4, sq4 = jnp.tile(cq, (1, A_Q_W // LANES)), jnp.tile(sq, (1, A_Q_W // LANES))
        q = _rope(q_ref[...], cq4, sq4) * (A_HEAD_DIM ** -0.5)
        kc = _rope(kc_ref[...], cq, sq)
        kp = _rope(kp_ref[...], cp_ref[...], sp_ref[...])
        kk = jnp.concatenate([kp, kc], axis=0)
        vv = jnp.concatenate([vp_ref[...], vc_ref[...]], axis=0)
        mask = _band_mask(n)[None]
        lane = lax.broadcasted_iota(jnp.int32, (WINDOW, LANES), 1)
        lane8 = lax.broadcasted_iota(jnp.int32, (WINDOW, A_Q_HEADS), 1)
        qe = jnp.stack([_place(q[:, (hq // 2) * LANES:(hq // 2 + 1) * LANES], hq % 2, hq // 4) for hq in range(A_Q_HEADS)])
        kk8 = jnp.broadcast_to(kk[None], (A_Q_HEADS,) + kk.shape)
        vv8 = jnp.broadcast_to(vv[None], (A_Q_HEADS,) + vv.shape)
        sk = jnp.stack([sink_ref[hq:hq + 1, 0:1] for hq in range(A_Q_HEADS)])
        s = jnp.where(mask, _dot(qe, kk8, BNT), NEG_BIG)
        m = jnp.maximum(jnp.max(s, axis=-1, keepdims=True), sk)
        p = jnp.exp(s - m)
        den = jnp.sum(p, axis=-1, keepdims=True) + jnp.exp(sk - m)
        o = _dot(p * (1.0 / den), vv8, BNN)
        lse_h = m + jnp.log(den)
        outs = []
        lse = jnp.zeros((WINDOW, A_Q_HEADS), F32)
        for pb in range(A_Q_HEADS // 2):
            halves = [_place(o[2 * pb + e], pb // 2, e) for e in range(2)]
            outs.append(jnp.where(lane < A_HEAD_DIM, halves[0], halves[1]))
            for e in range(2):
                lse = jnp.where(lane8 == 2 * pb + e, lse_h[2 * pb + e], lse)
        o_ref[...] = jnp.concatenate(outs, axis=1)
        l_ref[...] = lse

    return pl.pallas_call(
        body, name=name, grid=(T // WINDOW,),
        in_specs=[sp["q"], sp["kc"], sp["kp"], sp["vc"], sp["vp"], sp["tq"], sp["tq"], sp["tp"], sp["tp"], sp["sink"]],
        out_specs=[sp["row512"], sp["lse"]],
        out_shape=[jax.ShapeDtypeStruct((T, A_Q_W), F32), jax.ShapeDtypeStruct((T, A_Q_HEADS), F32)],
        compiler_params=_cparams("parallel"),
    )(proj, proj, proj, proj, proj, cos, sin_s, cos, sin_s, sinks_b)


def _attn_bwd(proj, cos, sin_s, sinks_b, o, lse, dmix, name):
    T = proj.shape[0]
    sp = _attn_specs()

    def body(q_ref, kc_ref, kp_ref, vc_ref, vp_ref, cq_ref, sq_ref, cp_ref, sp_ref, sink_ref, o_ref, l_ref, do_ref,
             dq_ref, dkc_ref, dkp_ref, dvc_ref, dvp_ref, dsink_ref):
        n = pl.program_id(0)
        cq, sq = cq_ref[...], sq_ref[...]
        cp, sps = cp_ref[...], sp_ref[...]
        cq4, sq4 = jnp.tile(cq, (1, A_Q_W // LANES)), jnp.tile(sq, (1, A_Q_W // LANES))
        q = _rope(q_ref[...], cq4, sq4) * (A_HEAD_DIM ** -0.5)
        kc = _rope(kc_ref[...], cq, sq)
        kp = _rope(kp_ref[...], cp, sps)
        kk = jnp.concatenate([kp, kc], axis=0)
        vv = jnp.concatenate([vp_ref[...], vc_ref[...]], axis=0)
        mask = _band_mask(n)[None]
        lane = lax.broadcasted_iota(jnp.int32, (WINDOW, LANES), 1)
        do_all, o_all, l_all = do_ref[...], o_ref[...], l_ref[...]
        lane8 = lax.broadcasted_iota(jnp.int32, (WINDOW, A_Q_HEADS), 1)
        head8 = lax.broadcasted_iota(jnp.int32, (1, A_Q_HEADS), 1)
        prod = do_all * o_all
        qes, does, deltas, lhs = [], [], [], []
        for hq in range(A_Q_HEADS):
            pb, e, kvh = hq // 2, hq % 2, hq // 4
            blk = slice(pb * LANES, (pb + 1) * LANES)
            in_half = (lane >= A_HEAD_DIM) if e else (lane < A_HEAD_DIM)
            deltas.append(jnp.sum(jnp.where(in_half, prod[:, blk], 0.0), axis=-1, keepdims=True))
            qes.append(_place(q[:, blk], e, kvh))
            does.append(_place(do_all[:, blk], e, kvh))
            lhs.append(jnp.sum(jnp.where(lane8 == hq, l_all, 0.0), axis=-1, keepdims=True))
        qe, doe, delta, lh = jnp.stack(qes), jnp.stack(does), jnp.stack(deltas), jnp.stack(lhs)
        kk8 = jnp.broadcast_to(kk[None], (A_Q_HEADS,) + kk.shape)
        vv8 = jnp.broadcast_to(vv[None], (A_Q_HEADS,) + vv.shape)
        sk = jnp.stack([sink_ref[hq:hq + 1, 0:1] for hq in range(A_Q_HEADS)])
        s = _dot(qe, kk8, BNT)
        p = jnp.where(mask, jnp.exp(jnp.where(mask, s, NEG_BIG) - lh), 0.0)
        dvv = jnp.sum(_dot(p, doe, BTN), axis=0)
        ds = p * (_dot(doe, vv8, BNT) - delta)
        dkk = jnp.sum(_dot(ds, qe, BTN), axis=0)
        dqe = _dot(ds, kk8, BNN)
        dsink_h = -jnp.sum(jnp.exp(sk - lh) * delta, axis=(1, 2), keepdims=True)
        dqs = []
        dsk = jnp.zeros((1, A_Q_HEADS), F32)
        for pb in range(A_Q_HEADS // 2):
            halves = [_place(dqe[2 * pb + e], pb // 2, e) for e in range(2)]
            dqs.append(jnp.where(lane < A_HEAD_DIM, halves[0], halves[1]))
            for e in range(2):
                dsk = jnp.where(head8 == 2 * pb + e, dsink_h[2 * pb + e], dsk)
        dq = jnp.concatenate(dqs, axis=1) * (A_HEAD_DIM ** -0.5)
        dq_ref[...] = _rope_t(dq, cq4, sq4).astype(ACT_DTYPE)
        dkp_ref[...] = _rope_t(dkk[:WINDOW], cp, sps)
        dkc_ref[...] = _rope_t(dkk[WINDOW:], cq, sq)
        dvp_ref[...] = dvv[:WINDOW]
        dvc_ref[...] = dvv[WINDOW:]

        @pl.when(n == 0)
        def _():
            dsink_ref[...] = dsk

        @pl.when(n > 0)
        def _():
            dsink_ref[...] += dsk

    return pl.pallas_call(
        body, name=name, grid=(T // WINDOW,),
        in_specs=[sp["q"], sp["kc"], sp["kp"], sp["vc"], sp["vp"], sp["tq"], sp["tq"], sp["tp"], sp["tp"], sp["sink"],
                  sp["row512"], sp["lse"], sp["row512"]],
        out_specs=[sp["row512"], sp["row128"], sp["row128"], sp["row128"], sp["row128"],
                   pl.BlockSpec((1, A_Q_HEADS), lambda n: (0, 0))],
        out_shape=[jax.ShapeDtypeStruct((T, HYB_PAD), ACT_DTYPE)] + [jax.ShapeDtypeStruct((T, LANES), F32)] * 4
        + [jax.ShapeDtypeStruct((1, A_Q_HEADS), F32)],
        compiler_params=_cparams("arbitrary"),
    )(proj, proj, proj, proj, proj, cos, sin_s, cos, sin_s, sinks_b, o, lse, dmix)


def _shift_down(x, prev8, k):
    if k == 0:
        return x
    R, W = x.shape
    r = pltpu.roll(jnp.concatenate([prev8, x], axis=0).reshape(R // SUBLANES + 1, SUBLANES, W), k, axis=1)
    sub = lax.broadcasted_iota(jnp.int32, (R // SUBLANES, SUBLANES, W), 1)
    return jnp.where(sub < k, r[:-1], r[1:]).reshape(R, W)


def _shift_up(x, next8, k):
    if k == 0:
        return x
    R, W = x.shape
    r = pltpu.roll(jnp.concatenate([x, next8], axis=0).reshape(R // SUBLANES + 1, SUBLANES, W), SUBLANES - k, axis=1)
    sub = lax.broadcasted_iota(jnp.int32, (R // SUBLANES, SUBLANES, W), 1)
    return jnp.where(sub >= SUBLANES - k, r[1:], r[:-1]).reshape(R, W)


def _conv(x, prev8, w):
    w0, w1, w2, w3 = (w[j:j + 1] for j in range(CONV_K))
    x1 = _shift_down(x, prev8, 1)
    v = x * w1 + x1 * w0
    v_prev8 = prev8 * w1 + pltpu.roll(prev8, 1, axis=0) * w0
    return x * w3 + x1 * w2 + _shift_down(v, v_prev8, 2)


def _conv_bwd(x, w, dy, next8_dy):
    dx = dy * w[CONV_K - 1:CONV_K]
    dws = []
    for j in range(CONV_K - 1):
        up = _shift_up(dy, next8_dy, CONV_K - 1 - j)
        dx = dx + up * w[j:j + 1]
        dws.append(jnp.sum(up * x, axis=0, keepdims=True))
    dws.append(jnp.sum(dy * x, axis=0, keepdims=True))
    return dx, dws


def _dnconv_bwd(proj, conv_w, dy, dproj, name):
    T = proj.shape[0]
    R = _tile(T, 512)
    nb = T // R
    cb0 = HP_QKVB // A_Q_W
    r8 = R // SUBLANES

    def body(x_ref, xp_ref, w_ref, dy_ref, dproj_in, dx_ref, dw_ref, next_ref):
        i = pl.program_id(1)
        blk = nb - 1 - i

        @pl.when(i == 0)
        def _():
            next_ref[...] = jnp.zeros_like(next_ref)

        x = x_ref[...]
        prev8 = jnp.where(blk > 0, xp_ref[...], 0.0)
        w = w_ref[...]
        dpre = dy_ref[...] * _dsilu(_conv(x, prev8, w))
        dx, dw = _conv_bwd(x, w, dpre, next_ref[...])
        dx_ref[...] = dx.astype(ACT_DTYPE)
        next_ref[...] = dpre[:SUBLANES]

        @pl.when(i == 0)
        def _():
            for j in range(CONV_K):
                dw_ref[j:j + 1, :] = dw[j]

        @pl.when(i > 0)
        def _():
            for j in range(CONV_K):
                dw_ref[j:j + 1, :] += dw[j]

    return pl.pallas_call(
        body, name=name, grid=(3, nb),
        in_specs=[pl.BlockSpec((R, B_W), lambda j, i: (nb - 1 - i, cb0 + j)),
                  pl.BlockSpec((SUBLANES, B_W), lambda j, i: (jnp.maximum((nb - 1 - i) * r8 - 1, 0), cb0 + j)),
                  pl.BlockSpec((CONV_K, B_W), lambda j, i: (0, j)),
                  pl.BlockSpec((R, B_W), lambda j, i: (nb - 1 - i, j)), ANY],
        out_specs=[pl.BlockSpec((R, B_W), lambda j, i: (nb - 1 - i, cb0 + j)),
                   pl.BlockSpec((CONV_K, B_W), lambda j, i: (0, j))],
        out_shape=[jax.ShapeDtypeStruct(dproj.shape, dproj.dtype), jax.ShapeDtypeStruct((CONV_K, 3 * B_W), F32)],
        input_output_aliases={4: 0},
        scratch_shapes=[pltpu.VMEM((SUBLANES, B_W), F32)],
        compiler_params=_cparams("parallel", "arbitrary"),
    )(proj, proj, conv_w, dy, dproj)


DK_SCALE = B_HEAD_DIM ** -0.5


BNN = (((2,), (1,)), ((0,), (0,)))
BNT = (((2,), (2,)), ((0,), (0,)))
BTN = (((1,), (1,)), ((0,), (0,)))


def _tri_inv(a):
    C = a.shape[-1]
    ri = lax.broadcasted_iota(jnp.int32, (C, C), 0)
    ci = lax.broadcasted_iota(jnp.int32, (C, C), 1)
    x = jnp.where(ri == ci, 1.0, 0.0)[None] - a
    p = _dotf(a, a, BNN)
    span = 2
    while span < C:
        dot = _dotf if span <= 4 else _dot
        x = x + dot(x, p, BNN)
        span *= 2
        if span < C:
            p = dot(p, p, BNN)
    return x


def _dn_chunk(qc, kc, v, gcol, grow, bcol, s0, tm=None):
    C = B_CHUNK
    ri = lax.broadcasted_iota(jnp.int32, (C, C), 0)
    ci = lax.broadcasted_iota(jnp.int32, (C, C), 1)
    incl, strict = (ri >= ci)[None], (ri > ci)[None]
    rq = lax.rsqrt(jnp.sum(qc * qc, axis=-1, keepdims=True) + NORM_EPS)
    rk = lax.rsqrt(jnp.sum(kc * kc, axis=-1, keepdims=True) + NORM_EPS)
    qn = qc * rq
    q = qn * DK_SCALE
    k = kc * rk
    gc_col = jnp.sum(jnp.where(incl, grow, 0.0), axis=2, keepdims=True)
    gc_row = jnp.sum(jnp.where((ri <= ci)[None], gcol, 0.0), axis=1, keepdims=True)
    gl = jnp.sum(gcol, axis=1, keepdims=True)
    dincl = jnp.where(incl, jnp.exp(jnp.where(incl, gc_col - gc_row, 0.0)), 0.0)
    dstrict = jnp.where(strict, dincl, 0.0)
    eg = jnp.exp(gc_col)
    ekt = jnp.exp(gl - gc_col)
    egl = jnp.exp(gl)
    kb = k * bcol
    vb = v * bcol
    kbg = kb * eg
    a = _dot(kb, k, BNT) * dstrict
    if tm is None:
        tm = _tri_inv(a)
    u = _dot(tm, vb, BNN)
    w = _dot(tm, kbg, BNN)
    vn = u - _dot(w, s0, BNN)
    qk = _dot(q, k, BNT) * dincl
    qg = q * eg
    kt = k * ekt
    o = _dot(qg, s0, BNN) + _dot(qk, vn, BNN)
    s1 = s0 * egl + _dot(kt, vn, BTN)
    return dict(rq=rq, rk=rk, qn=qn, q=q, k=k, dincl=dincl, dstrict=dstrict, eg=eg, ekt=ekt, egl=egl, kb=kb, vb=vb,
                kbg=kbg, a=a, tm=tm, w=w, vn=vn, qk=qk, qg=qg, kt=kt, o=o, s1=s1, ri=ri[None], ci=ci[None])


def _heads(ref):
    return jnp.stack([ref[:, h * B_HEAD_DIM:(h + 1) * B_HEAD_DIM] for h in range(B_HEADS)])


def _store_heads(ref, val):
    for h in range(B_HEADS):
        ref[:, h * B_HEAD_DIM:(h + 1) * B_HEAD_DIM] = val[h]


def _dn_specs(N, rev):
    ix = (lambda n: N - 1 - n) if rev else (lambda n: n)
    wide = lambda cb: pl.BlockSpec((B_CHUNK, B_W), lambda n: (ix(n), cb))
    return dict(
        q=wide(0), k=wide(1), v=wide(2), z=wide(HP_Z // B_W), dob=wide(A_Q_W // B_W), out=wide(0),
        nw=pl.BlockSpec((1, LANES), lambda n: (0, 0)),
        row=pl.BlockSpec((B_HEADS, None, 1, B_CHUNK), lambda n: (0, ix(n), 0, 0)),
        state=pl.BlockSpec((B_HEADS, None, B_HEAD_DIM, B_HEAD_DIM), lambda n: (0, ix(n), 0, 0)),
        inv=pl.BlockSpec((B_HEADS, None, B_CHUNK, B_CHUNK), lambda n: (0, ix(n), 0, 0)),
    )


def _to_col(row):
    C = row.shape[-1]
    eye = lax.broadcasted_iota(jnp.int32, (C, C), 0) == lax.broadcasted_iota(jnp.int32, (C, C), 1)
    return jnp.sum(jnp.where(eye[None], row, 0.0), axis=2, keepdims=True)


def _to_row(col):
    C = col.shape[1]
    eye = lax.broadcasted_iota(jnp.int32, (C, C), 0) == lax.broadcasted_iota(jnp.int32, (C, C), 1)
    return jnp.sum(jnp.where(eye[None], col, 0.0), axis=1, keepdims=True)


def _dn_fwd(proj, conv_w, norm_w, grow, brow, name):
    T = proj.shape[0]
    N = T // B_CHUNK
    sp = _dn_specs(N, False)
    cb0 = HP_QKVB // B_W

    def body(xq_ref, xk_ref, xv_ref, cw_ref, z_ref, nw_ref, gr_ref, br_ref, o_ref, st_ref, tm_ref, c_ref,
             s_ref, prev_ref):
        n = pl.program_id(0)

        @pl.when(n == 0)
        def _():
            s_ref[...] = jnp.zeros_like(s_ref)
            prev_ref[...] = jnp.zeros_like(prev_ref)

        qkv = []
        for part, x_ref in enumerate((xq_ref, xk_ref, xv_ref)):
            cols = slice(part * B_W, (part + 1) * B_W)
            x = x_ref[...]
            xc = _silu(_conv(x, prev_ref[:, cols], cw_ref[:, cols]))
            prev_ref[:, cols] = x[B_CHUNK - SUBLANES:]
            c_ref[:, cols] = xc
            qkv.append(jnp.stack([xc[:, h * B_HEAD_DIM:(h + 1) * B_HEAD_DIM] for h in range(B_HEADS)]))
        s0 = s_ref[...]
        st_ref[...] = s0
        grow_v = gr_ref[...]
        f = _dn_chunk(qkv[0], qkv[1], qkv[2], _to_col(grow_v), grow_v, _to_col(br_ref[...]), s0)
        o = f["o"]
        r = lax.rsqrt(jnp.mean(o * o, axis=-1, keepdims=True) + NORM_EPS)
        _store_heads(o_ref, o * r * nw_ref[...][None] * _silu(_heads(z_ref)))
        s_ref[...] = f["s1"]
        tm_ref[...] = f["tm"]

    wide = lambda cb: pl.BlockSpec((B_CHUNK, B_W), lambda n: (n, cb))
    return pl.pallas_call(
        body, name=name, grid=(N,),
        in_specs=[wide(cb0), wide(cb0 + 1), wide(cb0 + 2), pl.BlockSpec((CONV_K, 3 * B_W), lambda n: (0, 0)),
                  sp["z"], sp["nw"], sp["row"], sp["row"]],
        out_specs=[sp["out"], sp["state"], sp["inv"], pl.BlockSpec((B_CHUNK, 3 * B_W), lambda n: (n, 0))],
        out_shape=[jax.ShapeDtypeStruct((T, B_W), F32),
                   jax.ShapeDtypeStruct((B_HEADS, N, B_HEAD_DIM, B_HEAD_DIM), F32),
                   jax.ShapeDtypeStruct((B_HEADS, N, B_CHUNK, B_CHUNK), F32),
                   jax.ShapeDtypeStruct((T, 3 * B_W), F32)],
        scratch_shapes=[pltpu.VMEM((B_HEADS, B_HEAD_DIM, B_HEAD_DIM), F32), pltpu.VMEM((SUBLANES, 3 * B_W), F32)],
        compiler_params=_cparams("arbitrary"),
    )(proj, proj, proj, conv_w, proj, norm_w, grow, brow)


def _dn_bwd(qkvc, proj, norm_w, grow, brow, states, invs, dmix, dproj, name):
    T = qkvc.shape[0]
    N = T // B_CHUNK
    sp = _dn_specs(N, True)
    C = B_CHUNK

    def body(q_ref, k_ref, v_ref, z_ref, nw_ref, gr_ref, br_ref, st_ref, tm_ref, dob_ref, dproj_in,
             dqkv_ref, dz_ref, dg_ref, db_ref, dnw_ref, ds_ref):
        n = pl.program_id(0)
        dq_ref = dqkv_ref.at[:, 0:B_W]
        dk_ref = dqkv_ref.at[:, B_W:2 * B_W]
        dv_ref = dqkv_ref.at[:, 2 * B_W:3 * B_W]

        @pl.when(n == 0)
        def _():
            ds_ref[...] = jnp.zeros_like(ds_ref)
            dnw_ref[...] = jnp.zeros_like(dnw_ref)

        s0 = st_ref[...]
        ds1 = ds_ref[...]
        v, z, nw, bcol_v = _heads(v_ref), _heads(z_ref), nw_ref[...][None], _to_col(br_ref[...])
        grow_v = gr_ref[...]
        f = _dn_chunk(_heads(q_ref), _heads(k_ref), v, _to_col(grow_v), grow_v, bcol_v, s0, tm=tm_ref[...])
        o, q, k, qn = f["o"], f["q"], f["k"], f["qn"]
        eg, ekt, egl = f["eg"], f["ekt"], f["egl"]
        tm, w, vn, kb, vb, kbg = f["tm"], f["w"], f["vn"], f["kb"], f["vb"], f["kbg"]
        qg, kt, qk, a = f["qg"], f["kt"], f["qk"], f["a"]
        ri, ci = f["ri"], f["ci"]

        dob_v = _heads(dob_ref)
        r = lax.rsqrt(jnp.mean(o * o, axis=-1, keepdims=True) + NORM_EPS)
        sz = _silu(z)
        on = o * r
        dnw_ref[...] += jnp.sum(dob_v * sz * on, axis=1, keepdims=True)
        _store_heads(dz_ref, (dob_v * on * nw * _dsilu(z)).astype(ACT_DTYPE))
        d_on = dob_v * sz * nw
        do = r * (d_on - on * jnp.mean(d_on * on, axis=-1, keepdims=True))

        dvn = _dot(qk, do, BTN) + _dot(kt, ds1, BNN)
        dqk = _dot(do, vn, BNT)
        dqg = _dot(do, s0, BNT)
        ds_ref[...] = _dot(qg, do, BTN) + egl * ds1 - _dot(w, dvn, BTN)
        dgl = jnp.sum(s0 * ds1, axis=(1, 2), keepdims=True) * egl
        dkt = _dot(vn, ds1, BNT)
        dw = -_dot(dvn, s0, BNT)
        dq = dqg * eg
        dgc = jnp.sum(dqg * qg, axis=-1, keepdims=True)
        dk = dkt * ekt
        t_kt = jnp.sum(dkt * kt, axis=-1, keepdims=True)
        dgl = dgl + jnp.sum(t_kt, axis=1, keepdims=True)
        dgc = dgc - t_kt
        dqkr = dqk * f["dincl"]
        dq = dq + _dot(dqkr, k, BNN)
        dk = dk + _dot(dqkr, q, BTN)
        e_qk = dqk * qk
        dgc = dgc + jnp.sum(e_qk, axis=-1, keepdims=True)
        dgc_row = -jnp.sum(e_qk, axis=1, keepdims=True)
        dtm = _dot(dvn, vb, BNT) + _dot(dw, kbg, BNT)
        dvb = _dot(tm, dvn, BTN)
        dkbg = _dot(tm, dw, BTN)
        dkb = dkbg * eg
        dgc = dgc + jnp.sum(dkbg * kbg, axis=-1, keepdims=True)
        da = -_dotf(tm, _dotf(dtm, tm, BNT), BTN)
        dkk = da * f["dstrict"]
        e_a = da * a
        dgc = dgc + jnp.sum(e_a, axis=-1, keepdims=True)
        dgc_row = dgc_row - jnp.sum(e_a, axis=1, keepdims=True)
        dkb = dkb + _dot(dkk, k, BNN)
        dk = dk + _dot(dkk, kb, BTN)
        dk = dk + dkb * bcol_v
        db_ref[...] = _to_row(jnp.sum(dkb * k, axis=-1, keepdims=True) + jnp.sum(dvb * v, axis=-1, keepdims=True))
        _store_heads(dv_ref, dvb * bcol_v)
        dgc_row = dgc_row + jnp.sum(jnp.where(ri == ci, dgc, 0.0), axis=1, keepdims=True)
        dg_ref[...] = jnp.sum(jnp.where(ci <= ri, _to_col(dgc_row), 0.0), axis=1, keepdims=True) + dgl
        dqs = dq * DK_SCALE
        _store_heads(dq_ref, f["rq"] * (dqs - qn * jnp.sum(dqs * qn, axis=-1, keepdims=True)))
        _store_heads(dk_ref, f["rk"] * (dk - k * jnp.sum(dk * k, axis=-1, keepdims=True)))

    return pl.pallas_call(
        body, name=name, grid=(N,),
        in_specs=[sp["q"], sp["k"], sp["v"], sp["z"], sp["nw"], sp["row"], sp["row"], sp["state"], sp["inv"], sp["dob"],
                  ANY],
        out_specs=[pl.BlockSpec((C, 3 * B_W), lambda n: (N - 1 - n, 0)), sp["z"], sp["row"], sp["row"],
                   pl.BlockSpec((B_HEADS, 1, LANES), lambda n: (0, 0, 0))],
        out_shape=[jax.ShapeDtypeStruct((T, 3 * B_W), F32), jax.ShapeDtypeStruct(dproj.shape, dproj.dtype),
                   jax.ShapeDtypeStruct((B_HEADS, N, 1, C), F32), jax.ShapeDtypeStruct((B_HEADS, N, 1, C), F32),
                   jax.ShapeDtypeStruct((B_HEADS, 1, LANES), F32)],
        input_output_aliases={10: 1},
        scratch_shapes=[pltpu.VMEM((B_HEADS, B_HEAD_DIM, B_HEAD_DIM), F32)],
        compiler_params=_cparams("arbitrary"),
    )(qkvc, qkvc, qkvc, proj, norm_w, grow, brow, states, invs, dmix, dproj)


def _lru_gates(xc, wa_ref, wx_ref, ba, bx, sp):
    pre_r, pre_i = [], []
    for hb in range(LRU_BLOCKS):
        xb = xc[:, hb * LRU_BLOCK_W:(hb + 1) * LRU_BLOCK_W]
        pre_r.append(_dot(xb, wa_ref[hb]))
        pre_i.append(_dot(xb, wx_ref[hb]))
    r = _sigmoid(jnp.concatenate(pre_r, axis=1) + ba)
    i = _sigmoid(jnp.concatenate(pre_i, axis=1) + bx)
    la = -LRU_C * r * sp
    a = jnp.exp(la)
    th = jnp.tanh(la)
    s = jnp.sqrt(-2.0 * th / (1.0 - th))
    return r, i, a, s


def _scan_down(a, b, h_in):
    R, W = a.shape
    a = a.reshape(R // SUBLANES, SUBLANES, W)
    b = b.reshape(R // SUBLANES, SUBLANES, W)
    sub = lax.broadcasted_iota(jnp.int32, a.shape, 1)
    d = 1
    while d < SUBLANES:
        ok = sub >= d
        b = a * jnp.where(ok, pltpu.roll(b, d, axis=1), 0.0) + b
        a = a * jnp.where(ok, pltpu.roll(a, d, axis=1), 1.0)
        d *= 2
    out, last = [], h_in
    for g in range(R // SUBLANES):
        h = b[g] + a[g] * last
        out.append(h)
        last = h[SUBLANES - 1:SUBLANES]
    return jnp.concatenate(out, axis=0)


def _scan_up(a, b, l_in):
    R, W = a.shape
    a = a.reshape(R // SUBLANES, SUBLANES, W)
    b = b.reshape(R // SUBLANES, SUBLANES, W)
    sub = lax.broadcasted_iota(jnp.int32, a.shape, 1)
    d = 1
    while d < SUBLANES:
        ok = sub < SUBLANES - d
        b = a * jnp.where(ok, pltpu.roll(b, SUBLANES - d, axis=1), 0.0) + b
        a = a * jnp.where(ok, pltpu.roll(a, SUBLANES - d, axis=1), 1.0)
        d *= 2
    out, nxt = [], l_in
    for g in range(R // SUBLANES - 1, -1, -1):
        lam = b[g] + a[g] * nxt
        out.append(lam)
        nxt = lam[0:1]
    return jnp.concatenate(out[::-1], axis=0)


def _rglru_fwd(proj, conv_w, conv_b, wa, wx, ba, bx, sp, name):
    T = proj.shape[0]
    R = _tile(T, 256)
    W = D_MODEL

    def body(p_ref, cw_ref, cb_ref, wa_ref, wx_ref, ba_ref, bx_ref, sp_ref, hg_ref, h_ref, prev_ref, hc_ref):
        i = pl.program_id(0)

        @pl.when(i == 0)
        def _():
            prev_ref[...] = jnp.zeros_like(prev_ref)
            hc_ref[...] = jnp.zeros_like(hc_ref)

        xr = p_ref[:, :W]
        gate = p_ref[:, W:]
        xc = _conv(xr, prev_ref[...], cw_ref[...]) + cb_ref[...]
        prev_ref[...] = xr[R - SUBLANES:]
        r, ig, a, s = _lru_gates(xc, wa_ref, wx_ref, ba_ref[...], bx_ref[...], sp_ref[...])
        h = _scan_down(a, s * ig * xc, hc_ref[SUBLANES - 1:SUBLANES, :])
        h_ref[...] = h
        hg_ref[...] = (h * _gelu(gate)).astype(ACT_DTYPE)
        hc_ref[...] = h[R - SUBLANES:]

    vec = pl.BlockSpec((1, W), lambda i: (0, 0))
    wsp = pl.BlockSpec((LRU_BLOCKS, LRU_BLOCK_W, LRU_BLOCK_W), lambda i: (0, 0, 0))
    row = pl.BlockSpec((R, W), lambda i: (i, 0))
    return pl.pallas_call(
        body, name=name, grid=(T // R,),
        in_specs=[pl.BlockSpec((R, 2 * W), lambda i: (i, 0)), pl.BlockSpec((CONV_K, W), lambda i: (0, 0)),
                  vec, wsp, wsp, vec, vec, vec],
        out_specs=[row, row],
        out_shape=[jax.ShapeDtypeStruct((T, W), ACT_DTYPE), jax.ShapeDtypeStruct((T, W), F32)],
        scratch_shapes=[pltpu.VMEM((SUBLANES, W), F32), pltpu.VMEM((SUBLANES, W), F32)],
        compiler_params=_cparams("arbitrary"),
    )(proj, conv_w, conv_b, wa, wx, ba, bx, sp)


def _rglru_bwd(proj, conv_w, conv_b, wa, wx, ba, bx, sp, h, dhg, name):
    T = proj.shape[0]
    R = _tile(T, 256)
    nb = T // R
    r8 = R // SUBLANES
    W = D_MODEL

    def body(p_ref, pp_ref, cw_ref, cb_ref, wa_ref, wx_ref, ba_ref, bx_ref, sp_ref, h_ref, hp_ref, dhg_ref,
             dp_ref, dcw_ref, dcb_ref, dwa_ref, dwx_ref, dba_ref, dbx_ref, dsp_ref, lam_ref, nxt_ref):
        step = pl.program_id(0)
        blk = nb - 1 - step

        @pl.when(step == 0)
        def _():
            lam_ref[...] = jnp.zeros_like(lam_ref)
            nxt_ref[...] = jnp.zeros_like(nxt_ref)

        xr = p_ref[:, :W]
        gate = p_ref[:, W:]
        first = blk > 0
        prev8 = jnp.where(first, pp_ref[:, :W], 0.0)
        hprev8 = jnp.where(first, hp_ref[...], 0.0)
        cw = cw_ref[...]
        spv = sp_ref[...]
        xc = _conv(xr, prev8, cw) + cb_ref[...]
        r, ig, a, s = _lru_gates(xc, wa_ref, wx_ref, ba_ref[...], bx_ref[...], spv)
        hv = h_ref[...]
        dhg_v = dhg_ref[...]
        dgate = dhg_v * hv * _dgelu(gate)
        dh = dhg_v * _gelu(gate)
        row = lax.broadcasted_iota(jnp.int32, (R, W), 0)
        last = row == R - 1
        a_up = jnp.where(last, 0.0, pltpu.roll(a, R - 1, axis=0))
        lam = _scan_up(a_up, dh + jnp.where(last, lam_ref[0:1, :], 0.0), jnp.zeros((1, W), F32))
        lam_ref[...] = (a * lam)[:SUBLANES]
        h_dn = _shift_down(hv, hprev8, 1)
        da = lam * h_dn
        bx_in = ig * xc
        dsv = lam * bx_in
        dig = lam * s * xc
        dxc = lam * s * ig
        dla = da * a - dsv * (a * a) / s
        dr = dla * (-LRU_C) * spv
        dsp = jnp.sum(dla * (-LRU_C) * r, axis=0, keepdims=True)
        dpr = dr * r * (1.0 - r)
        dpi = dig * ig * (1.0 - ig)
        dxc_parts, dwa_parts, dwx_parts = [], [], []
        for hb in range(LRU_BLOCKS):
            sl = slice(hb * LRU_BLOCK_W, (hb + 1) * LRU_BLOCK_W)
            xb, gr, gi = xc[:, sl], dpr[:, sl], dpi[:, sl]
            dxc_parts.append(_dot(gr, wa_ref[hb], NT) + _dot(gi, wx_ref[hb], NT))
            dwa_parts.append(_dot(xb, gr, TN))
            dwx_parts.append(_dot(xb, gi, TN))
        dxc = dxc + jnp.concatenate(dxc_parts, axis=1)
        dxr, dcw = _conv_bwd(xr, cw, dxc, nxt_ref[...])
        nxt_ref[...] = dxc[:SUBLANES]
        dp_ref[:, :W] = dxr.astype(ACT_DTYPE)
        dp_ref[:, W:] = dgate.astype(ACT_DTYPE)
        dcb = jnp.sum(dxc, axis=0, keepdims=True)
        dba = jnp.sum(dpr, axis=0, keepdims=True)
        dbx = jnp.sum(dpi, axis=0, keepdims=True)

        @pl.when(step == 0)
        def _():
            for j in range(CONV_K):
                dcw_ref[j:j + 1, :] = dcw[j]
            dcb_ref[...] = dcb
            dba_ref[...] = dba
            dbx_ref[...] = dbx
            dsp_ref[...] = dsp
            for hb in range(LRU_BLOCKS):
                dwa_ref[hb] = dwa_parts[hb]
                dwx_ref[hb] = dwx_parts[hb]

        @pl.when(step > 0)
        def _():
            for j in range(CONV_K):
                dcw_ref[j:j + 1, :] += dcw[j]
            dcb_ref[...] += dcb
            dba_ref[...] += dba
            dbx_ref[...] += dbx
            dsp_ref[...] += dsp
            for hb in range(LRU_BLOCKS):
                dwa_ref[hb] += dwa_parts[hb]
                dwx_ref[hb] += dwx_parts[hb]

    rv = lambda i: nb - 1 - i
    before = lambda i: jnp.maximum((nb - 1 - i) * r8 - 1, 0)
    vec = pl.BlockSpec((1, W), lambda i: (0, 0))
    cws = pl.BlockSpec((CONV_K, W), lambda i: (0, 0))
    wsp = pl.BlockSpec((LRU_BLOCKS, LRU_BLOCK_W, LRU_BLOCK_W), lambda i: (0, 0, 0))
    row = pl.BlockSpec((R, W), lambda i: (rv(i), 0))
    wshape = jax.ShapeDtypeStruct((LRU_BLOCKS, LRU_BLOCK_W, LRU_BLOCK_W), F32)
    vshape = jax.ShapeDtypeStruct((1, W), F32)
    return pl.pallas_call(
        body, name=name, grid=(nb,),
        in_specs=[pl.BlockSpec((R, 2 * W), lambda i: (rv(i), 0)), pl.BlockSpec((SUBLANES, 2 * W), lambda i: (before(i), 0)),
                  cws, vec, wsp, wsp, vec, vec, vec, row, pl.BlockSpec((SUBLANES, W), lambda i: (before(i), 0)), row],
        out_specs=[pl.BlockSpec((R, 2 * W), lambda i: (rv(i), 0)), cws, vec, wsp, wsp, vec, vec, vec],
        out_shape=[jax.ShapeDtypeStruct((T, 2 * W), ACT_DTYPE), jax.ShapeDtypeStruct((CONV_K, W), F32), vshape,
                   wshape, wshape, vshape, vshape, vshape],
        scratch_shapes=[pltpu.VMEM((SUBLANES, W), F32), pltpu.VMEM((SUBLANES, W), F32)],
        compiler_params=_cparams("arbitrary"),
    )(proj, proj, conv_w, conv_b, wa, wx, ba, bx, sp, h, h, dhg)


MESH = pl.DeviceIdType.MESH
ANY = pl.BlockSpec(memory_space=pl.ANY)


def _position():
    x, y, c = lax.axis_index("x"), lax.axis_index("y"), lax.axis_index("c")
    other_chips = [(1 - x, y), (x, 1 - y), (1 - x, 1 - y)]
    return x, y, c, other_chips


def _all_gather_weights(shards, name):
    n = len(shards)

    def body(*refs):
        ins, outs = refs[:n], refs[n:2 * n]
        send_sems, recv_sems = refs[2 * n:]
        x, y, c, chips = _position()
        me = 2 * x + y
        sibling = (x, y, 1 - c)

        def rcopy(t, k, src, dst, to):
            return pltpu.make_async_remote_copy(src_ref=src, dst_ref=dst, send_sem=send_sems.at[t, k],
                                                recv_sem=recv_sems.at[t, k], device_id=to, device_id_type=MESH)

        started = []
        for t in range(n):
            for j, (cx, cy) in enumerate(chips):
                cp = rcopy(t, j, ins[t].at[c], outs[t].at[me, c], (cx, cy, c))
                cp.start()
                started.append(cp)
        for t in range(n):
            for j, (cx, cy) in enumerate(chips):
                blk = outs[t].at[2 * cx + cy, c]
                rcopy(t, j, blk, blk, (cx, cy, c)).wait_recv()
                cp = rcopy(t, 3 + j, blk, blk, sibling)
                cp.start()
                started.append(cp)
        for t in range(n):
            for j, (cx, cy) in enumerate(chips):
                blk = outs[t].at[2 * cx + cy, 1 - c]
                rcopy(t, 3 + j, blk, blk, sibling).wait_recv()
        for cp in started:
            cp.wait_send()

    return pl.pallas_call(
        body, name=name, in_specs=[ANY] * n, out_specs=[ANY] * n,
        out_shape=[jax.ShapeDtypeStruct((N_CHIPS,) + s.shape, s.dtype) for s in shards],
        scratch_shapes=[pltpu.SemaphoreType.DMA((n, 6)), pltpu.SemaphoreType.DMA((n, 6))],
    )(*shards)


HBM = pl.BlockSpec(memory_space=pltpu.HBM)
SEM = pl.BlockSpec(memory_space=pltpu.SEMAPHORE)
EFFECT = pltpu.SideEffectType.DATAFLOW_SIDE_EFFECTING


def _gather_start(own, land, after, name):
    n = len(own)

    def body(*refs):
        own_refs, land_refs = refs[:n], refs[n:2 * n]
        send_sems = refs[2 * n + 1:3 * n + 1]
        recv_sems = refs[3 * n + 1:4 * n + 1]
        x, y, c, chips = _position()
        me = 2 * x + y
        for t in range(n):
            for cx, cy in chips:
                pltpu.make_async_remote_copy(
                    src_ref=own_refs[t], dst_ref=land_refs[t].at[me], send_sem=send_sems[t], recv_sem=recv_sems[t],
                    device_id=(cx, cy, c), device_id_type=MESH).start()

    sems = (pltpu.SemaphoreType.DMA(()),) * (2 * n)
    thru = [pltpu.HBM(a.shape, a.dtype) for a in list(own) + list(land)]
    out = pl.pallas_call(
        body, name=name, out_shape=(*sems, *thru),
        in_specs=[HBM] * (2 * n) + [pl.BlockSpec(memory_space=pl.ANY)], out_specs=(SEM,) * (2 * n) + (HBM,) * (2 * n),
        input_output_aliases={i: 2 * n + i for i in range(2 * n)},
        compiler_params=pltpu.CompilerParams(has_side_effects=EFFECT),
    )(*[pltpu.with_memory_space_constraint(a, pltpu.HBM) for a in list(own) + list(land)], after)
    return list(out[:n]), list(out[n:2 * n]), list(out[2 * n:3 * n]), list(out[3 * n:])


def _gather_wait(send_sems, recv_sems, own, land, after, name):
    n = len(own)

    def body(*refs):
        land_refs = refs[n:2 * n]
        s_sems, r_sems = refs[2 * n:3 * n], refs[3 * n:4 * n]
        x, y, c, _ = _position()
        for t in range(n):
            three = land_refs[t].at[pl.ds(0, N_CHIPS - 1)]
            cp = pltpu.make_async_remote_copy(src_ref=three, dst_ref=three, send_sem=s_sems[t], recv_sem=r_sems[t],
                                              device_id=(x, y, c), device_id_type=MESH)
            cp.wait_send()
            cp.wait_recv()

    thru = [pltpu.HBM(a.shape, a.dtype) for a in list(own) + list(land)]
    out = pl.pallas_call(
        body, name=name, out_shape=tuple(thru),
        in_specs=[HBM] * (2 * n) + [SEM] * (2 * n) + [pl.BlockSpec(memory_space=pl.ANY)], out_specs=(HBM,) * (2 * n),
        input_output_aliases={i: i for i in range(2 * n)},
        compiler_params=pltpu.CompilerParams(has_side_effects=EFFECT),
    )(*own, *land, *send_sems, *recv_sems, after)
    return list(out[n:])


def _rs_to_sibling(grads, name):
    n = len(grads)

    def body(*refs):
        ins, outs = refs[:n], refs[n:2 * n]
        send_sems, recv_sems = refs[2 * n:]
        x, y, c, _ = _position()
        cps = [pltpu.make_async_remote_copy(src_ref=ins[t].at[:, 1 - c], dst_ref=outs[t], send_sem=send_sems.at[t],
                                            recv_sem=recv_sems.at[t], device_id=(x, y, 1 - c), device_id_type=MESH)
               for t in range(n)]
        for cp in cps:
            cp.start()
        for cp in cps:
            cp.wait()

    return pl.pallas_call(
        body, name=name, in_specs=[ANY] * n, out_specs=[ANY] * n,
        out_shape=[jax.ShapeDtypeStruct((N_CHIPS,) + g.shape[2:], g.dtype) for g in grads],
        scratch_shapes=[pltpu.SemaphoreType.DMA((n,)), pltpu.SemaphoreType.DMA((n,))],
    )(*grads)


def _rs_sibling_start(grads, recv, name):
    n = len(grads)

    def body(*refs):
        grad_refs, recv_refs = refs[:n], refs[n:2 * n]
        send_sems, recv_sems = refs[2 * n:3 * n], refs[3 * n:4 * n]
        token_ref = refs[-1]
        x, y, c, _ = _position()
        for t in range(n):
            pltpu.make_async_remote_copy(src_ref=grad_refs[t].at[:, 1 - c], dst_ref=recv_refs[t], send_sem=send_sems[t],
                                         recv_sem=recv_sems[t], device_id=(x, y, 1 - c), device_id_type=MESH).start()
        token_ref[...] = jnp.zeros_like(token_ref)

    sems = (pltpu.SemaphoreType.DMA(()),) * (2 * n)
    thru = [pltpu.HBM(a.shape, a.dtype) for a in list(grads) + list(recv)]
    out = pl.pallas_call(
        body, name=name, out_shape=(*sems, *thru, jax.ShapeDtypeStruct((SUBLANES, LANES), F32)),
        in_specs=[HBM] * (2 * n), out_specs=(SEM,) * (2 * n) + (HBM,) * (2 * n) + (pl.BlockSpec(memory_space=pltpu.VMEM),),
        input_output_aliases={i: 2 * n + i for i in range(2 * n)},
        compiler_params=pltpu.CompilerParams(has_side_effects=EFFECT),
    )(*[pltpu.with_memory_space_constraint(a, pltpu.HBM) for a in list(grads) + list(recv)])
    return out[:n], out[n:2 * n], list(out[2 * n:3 * n]), list(out[3 * n:4 * n]), out[-1]


def _rs_sibling_wait(send_sems, recv_sems, grads, recv, after, name):
    n = len(grads)

    def body(*refs):
        recv_refs = refs[n:2 * n]
        s_sems, r_sems = refs[2 * n:3 * n], refs[3 * n:4 * n]
        x, y, c, _ = _position()
        for t in range(n):
            cp = pltpu.make_async_remote_copy(src_ref=recv_refs[t], dst_ref=recv_refs[t], send_sem=s_sems[t],
                                              recv_sem=r_sems[t], device_id=(x, y, c), device_id_type=MESH)
            cp.wait_send()
            cp.wait_recv()

    thru = [pltpu.HBM(a.shape, a.dtype) for a in list(grads) + list(recv)]
    out = pl.pallas_call(
        body, name=name, out_shape=tuple(thru),
        in_specs=[HBM] * (2 * n) + [SEM] * (2 * n) + [pl.BlockSpec(memory_space=pl.ANY)], out_specs=(HBM,) * (2 * n),
        input_output_aliases={i: i for i in range(2 * n)},
        compiler_params=pltpu.CompilerParams(has_side_effects=EFFECT),
    )(*grads, *recv, *send_sems, *recv_sems, after)
    return list(out[:n]), list(out[n:])


def _rs_across_start(parts, recv, name):
    n = len(parts)

    def body(*refs):
        part_refs, recv_refs = refs[:n], refs[n:2 * n]
        send_sems, recv_sems = refs[2 * n:3 * n], refs[3 * n:4 * n]
        token_ref = refs[-1]
        x, y, c, chips = _position()
        me = 2 * x + y
        for t in range(n):
            for cx, cy in chips:
                pltpu.make_async_remote_copy(src_ref=part_refs[t].at[2 * cx + cy], dst_ref=recv_refs[t].at[me],
                                             send_sem=send_sems[t], recv_sem=recv_sems[t], device_id=(cx, cy, c),
                                             device_id_type=MESH).start()
        token_ref[...] = jnp.zeros_like(token_ref)

    sems = (pltpu.SemaphoreType.DMA(()),) * (2 * n)
    thru = [pltpu.HBM(a.shape, a.dtype) for a in list(parts) + list(recv)]
    out = pl.pallas_call(
        body, name=name, out_shape=(*sems, *thru, jax.ShapeDtypeStruct((SUBLANES, LANES), F32)),
        in_specs=[HBM] * (2 * n), out_specs=(SEM,) * (2 * n) + (HBM,) * (2 * n) + (pl.BlockSpec(memory_space=pltpu.VMEM),),
        input_output_aliases={i: 2 * n + i for i in range(2 * n)},
        compiler_params=pltpu.CompilerParams(has_side_effects=EFFECT),
    )(*[pltpu.with_memory_space_constraint(a, pltpu.HBM) for a in list(parts) + list(recv)])
    return out[:n], out[n:2 * n], list(out[2 * n:3 * n]), list(out[3 * n:4 * n]), out[-1]


def _rs_across_wait(send_sems, recv_sems, parts, recv, after, name):
    n = len(parts)

    def body(*refs):
        recv_refs = refs[n:2 * n]
        s_sems, r_sems = refs[2 * n:3 * n], refs[3 * n:4 * n]
        x, y, c, _ = _position()
        for t in range(n):
            three = recv_refs[t].at[pl.ds(0, N_CHIPS - 1)]
            cp = pltpu.make_async_remote_copy(src_ref=three, dst_ref=three, send_sem=s_sems[t], recv_sem=r_sems[t],
                                              device_id=(x, y, c), device_id_type=MESH)
            cp.wait_send()
            cp.wait_recv()

    thru = [pltpu.HBM(a.shape, a.dtype) for a in list(parts) + list(recv)]
    out = pl.pallas_call(
        body, name=name, out_shape=tuple(thru),
        in_specs=[HBM] * (2 * n) + [SEM] * (2 * n) + [pl.BlockSpec(memory_space=pl.ANY)], out_specs=(HBM,) * (2 * n),
        input_output_aliases={i: i for i in range(2 * n)},
        compiler_params=pltpu.CompilerParams(has_side_effects=EFFECT),
    )(*parts, *recv, *send_sems, *recv_sems, after)
    return list(out[:n]), list(out[n:])


def _rs_join_halves(halves, name):
    n = len(halves)

    def body(*refs):
        ins, outs = refs[:n], refs[n:2 * n]
        send_sems, recv_sems = refs[2 * n:]
        x, y, c, _ = _position()
        cps = [pltpu.make_async_remote_copy(src_ref=ins[t].at[c], dst_ref=outs[t].at[c], send_sem=send_sems.at[t],
                                            recv_sem=recv_sems.at[t], device_id=(x, y, 1 - c), device_id_type=MESH)
               for t in range(n)]
        for cp in cps:
            cp.start()
        for t in range(n):
            blk = outs[t].at[1 - c]
            pltpu.make_async_remote_copy(src_ref=blk, dst_ref=blk, send_sem=send_sems.at[t], recv_sem=recv_sems.at[t],
                                         device_id=(x, y, 1 - c), device_id_type=MESH).wait_recv()
        for cp in cps:
            cp.wait_send()

    return pl.pallas_call(
        body, name=name, in_specs=[ANY] * n, out_specs=[ANY] * n,
        out_shape=[jax.ShapeDtypeStruct(h.shape, h.dtype) for h in halves],
        input_output_aliases={t: t for t in range(n)},
        scratch_shapes=[pltpu.SemaphoreType.DMA((n,)), pltpu.SemaphoreType.DMA((n,))],
    )(*halves)


def _all_gather_small(block, name):
    m_per, n = block.shape

    def body(x_ref, out_ref, send_sems, recv_sems, local_sem):
        x, y, c, chips = _position()
        me, sibling = (x, y, c), (x, y, 1 - c)

        def rows(px, py, pc):
            return out_ref.at[pl.ds((4 * px + 2 * py + pc) * m_per, m_per), :]

        def copy(k, blk, to, src=None):
            return pltpu.make_async_remote_copy(
                src_ref=rows(*blk) if src is None else src, dst_ref=rows(*blk), send_sem=send_sems.at[k],
                recv_sem=recv_sems.at[k], device_id=to, device_id_type=MESH)

        mine = pltpu.make_async_copy(x_ref, rows(*me), local_sem)
        mine.start()
        first = [copy(0, me, sibling, src=x_ref)]
        first += [copy(1 + j, me, (*chip, c), src=x_ref) for j, chip in enumerate(chips)]
        for cp in first:
            cp.start()
        passed = [copy(4 + j, (*chip, c), sibling) for j, chip in enumerate(chips)]
        for j, chip in enumerate(chips):
            copy(1 + j, (*chip, c), me).wait_recv()
            passed[j].start()
        copy(0, sibling, me).wait_recv()
        for j, chip in enumerate(chips):
            copy(4 + j, (*chip, 1 - c), me).wait_recv()
        for cp in first + passed:
            cp.wait_send()
        mine.wait()

    return pl.pallas_call(
        body, name=name, out_shape=jax.ShapeDtypeStruct((N_DEV * m_per, n), block.dtype),
        in_specs=[pl.BlockSpec(memory_space=pltpu.VMEM)], out_specs=pl.BlockSpec(memory_space=pltpu.VMEM),
        scratch_shapes=[pltpu.SemaphoreType.DMA((7,)), pltpu.SemaphoreType.DMA((7,)), pltpu.SemaphoreType.DMA],
    )(block)


def _row_tile(R, n):
    budget = 1 << 19
    if R * n <= budget or R % SUBLANES:
        return R
    t = R
    while t * n > budget and t % (2 * SUBLANES) == 0:
        t //= 2
    return t


def _pair_sum(g, recv, c_arr, name):
    _, _, R, n = g.shape
    tr = _row_tile(R, n)

    def body(c_ref, g_ref, r_ref, o_ref):
        o_ref[...] = (g_ref[...] + r_ref[...]).astype(ICI_DTYPE)

    grid_spec = pltpu.PrefetchScalarGridSpec(
        num_scalar_prefetch=1, grid=(N_CHIPS, R // tr),
        in_specs=[pl.BlockSpec((None, None, tr, n), lambda p, i, c: (p, c[0], i, 0)),
                  pl.BlockSpec((None, tr, n), lambda p, i, c: (p, i, 0))],
        out_specs=pl.BlockSpec((None, tr, n), lambda p, i, c: (p, i, 0)))
    return pl.pallas_call(
        body, name=name, grid_spec=grid_spec, out_shape=jax.ShapeDtypeStruct(recv.shape, ICI_DTYPE),
        compiler_params=_cparams("parallel", "parallel"),
    )(c_arr, g, recv)


def _chip_sum(recv, own, where, name):
    _, R, n = recv.shape
    tr = _row_tile(R, n)

    def body(w_ref, r0, r1, r2, r3, own_ref, o_ref):
        me = w_ref[0]
        terms = [jnp.where(me == k, own_ref[...], r[...]).astype(F32) for k, r in enumerate((r0, r1, r2, r3))]
        o_ref[...] = ((terms[0] + terms[1]) + terms[2]) + terms[3]

    def slot(k):
        return pl.BlockSpec((None, tr, n), lambda i, w: (w[1 + k], i, 0))

    grid_spec = pltpu.PrefetchScalarGridSpec(
        num_scalar_prefetch=1, grid=(R // tr,),
        in_specs=[slot(0), slot(1), slot(2), slot(3), pl.BlockSpec((None, tr, n), lambda i, w: (w[0], i, 0))],
        out_specs=pl.BlockSpec((None, tr, n), lambda i, w: (w[5], i, 0)))
    return pl.pallas_call(
        body, name=name, grid_spec=grid_spec, out_shape=jax.ShapeDtypeStruct((2, R, n), F32),
        compiler_params=_cparams("parallel"),
    )(where, recv, recv, recv, recv, own)


ADAM_C1 = 1.0 / (1.0 - ADAM_B1 ** ADAM_STEP)
ADAM_C2 = 1.0 / (1.0 - ADAM_B2 ** ADAM_STEP)


def _adamw_math(w, g, m, v):
    m = ADAM_B1 * m + (1.0 - ADAM_B1) * g
    v = ADAM_B2 * v + (1.0 - ADAM_B2) * (g * g)
    delta = -ADAM_LR * ((m * ADAM_C1) / (jnp.sqrt(v * ADAM_C2) + ADAM_EPS) + ADAM_WD * w)
    return delta, m, v


def _adamw(w, g, m, v, name):
    R, n = w.shape
    tr = _row_tile(R, n)

    def body(w_ref, g_ref, m_ref, v_ref, d_ref, nm_ref, nv_ref):
        d_ref[...], nm_ref[...], nv_ref[...] = _adamw_math(w_ref[...], g_ref[...], m_ref[...], v_ref[...])

    spec = pl.BlockSpec((tr, n), lambda i: (i, 0))
    return pl.pallas_call(
        body, name=name, grid=(R // tr,), in_specs=[spec] * 4, out_specs=[spec] * 3,
        out_shape=[jax.ShapeDtypeStruct((R, n), F32)] * 3, compiler_params=_cparams("parallel"),
    )(w, g, m, v)


def _adamw_rows(w, g, m, v, row0, prev, name):
    R, n = w.shape
    rows = g.shape[0]
    tr = _row_tile(math.gcd(row0, rows), n)
    off = row0 // tr

    def body(*refs):
        w_ref, g_ref, m_ref, v_ref = refs[:4]
        go_ref, d_ref, nm_ref, nv_ref = refs[-4:]
        gv = g_ref[...]
        go_ref[...] = gv
        d_ref[...], nm_ref[...], nv_ref[...] = _adamw_math(w_ref[...], gv, m_ref[...], v_ref[...])

    at = pl.BlockSpec((tr, n), lambda i: (i + off, 0))
    carried = [] if prev is None else list(prev)
    return pl.pallas_call(
        body, name=name, grid=(rows // tr,),
        in_specs=[at, pl.BlockSpec((tr, n), lambda i: (i, 0)), at, at] + [ANY] * len(carried),
        out_specs=[at] * 4, out_shape=[jax.ShapeDtypeStruct((R, n), F32)] * 4,
        input_output_aliases={4 + i: i for i in range(len(carried))},
        compiler_params=_cparams("parallel"),
    )(w, g, m, v, *carried)


def _adamw_small(w, gall, m, v, name):
    M, n = w.shape

    def body(w_ref, g_ref, m_ref, v_ref, gs_ref, d_ref, nm_ref, nv_ref):
        g = g_ref[0:M, :]
        for d in range(1, N_DEV):
            g = g + g_ref[d * M:(d + 1) * M, :]
        gs_ref[...] = g
        d_ref[...], nm_ref[...], nv_ref[...] = _adamw_math(w_ref[...], g, m_ref[...], v_ref[...])

    return pl.pallas_call(
        body, name=name, out_shape=[jax.ShapeDtypeStruct((M, n), F32)] * 4,
    )(w, gall, m, v)


SMALL_ROWS = 24
MID_ROWS = 4


def _pack_small(ln1_g, ln1_b, ln2_g, ln2_b, norm_w, sinks, a_log, dt_bias):
    mixed = jnp.concatenate([norm_w.reshape(-1), sinks.reshape(-1), a_log.reshape(-1), dt_bias.reshape(-1)])
    mixed = jnp.pad(mixed, (0, D_MODEL - mixed.shape[0]))[None]
    pad = jnp.zeros((SMALL_ROWS - 4 * DEPTH - 1, D_MODEL), F32)
    return jnp.concatenate([ln1_g, ln1_b, ln2_g, ln2_b, mixed, pad], axis=0)


def _unpack_small(p):
    mixed = p[4 * DEPTH]
    return (p[0:4], p[4:8], p[8:12], p[12:16], mixed[0:256].reshape(2, 128), mixed[256:272].reshape(2, 8),
            mixed[272:280].reshape(2, 4), mixed[280:288].reshape(2, 4))


def _pack_mid(conv_w, rconv_w, rconv_b, b_a, b_x, lam):
    lead = conv_w.shape[0]
    flat = jnp.concatenate([conv_w.reshape(lead, -1), rconv_w.reshape(lead, -1), rconv_b, b_a, b_x, lam], axis=1)
    return jnp.pad(flat, ((0, 0), (0, MID_ROWS * D_MODEL - flat.shape[1]))).reshape(lead, MID_ROWS, D_MODEL)


def _unpack_mid(p):
    lead = p.shape[:-2]
    f = p.reshape(lead + (MID_ROWS * D_MODEL,))
    return (f[..., 0:1536].reshape(lead + (4, 384)), f[..., 1536:2560].reshape(lead + (4, 256)),
            f[..., 2560:2816], f[..., 2816:3072], f[..., 3072:3328], f[..., 3328:3584])


def _cols_from_chips(g):
    p, L, R, n = g.shape
    return g.transpose(1, 2, 0, 3).reshape(L, R, p * n)


def _rows_from_chips(g):
    p, L, R, n = g.shape
    return g.transpose(1, 0, 2, 3).reshape(L, p * R, n)


def _cols_to_chips(g):
    L, R, n4 = g.shape
    return g.reshape(L, R, N_CHIPS, n4 // N_CHIPS).transpose(2, 0, 1, 3)


def _rows_to_chips(g):
    L, R4, n = g.shape
    return g.reshape(L, N_CHIPS, R4 // N_CHIPS, n).transpose(1, 0, 2, 3)


def _halves(a):
    return a.reshape(2, -1, a.shape[-1])


def _pad_hyb_cols(w):
    z = jnp.zeros(w.shape[:-1] + (HYB_PAD - HP_BG - 2 * B_HEADS,), w.dtype)
    return jnp.concatenate([w[..., 0:512], w[..., 768:2304], w[..., 2304:2816], w[..., 512:768], w[..., 2816:2824], z], axis=-1)


def _unpad_hyb_cols(w):
    return jnp.concatenate([w[..., 0:512], w[..., 2560:2816], w[..., 512:2048], w[..., 2048:2560], w[..., 2816:2824]], axis=-1)


def _hybrid_fwd(x, W, j, tables, sfx, out_ready, ln):
    cos, sin_s = tables
    T = x.shape[0]
    N = T // B_CHUNK
    proj = _matmul(x, W["hyb_w_in"][j], "nn", "hyb_in" + sfx)
    sinks_b = jnp.broadcast_to(W["hyb_sinks"][j][:, None], (A_Q_HEADS, LANES))
    o_a, lse = _attn_fwd(proj, cos, sin_s, sinks_b, "attn_fwd" + sfx)
    bg = proj[:, HP_BG:HP_BG + 2 * B_HEADS]
    beta = jax.nn.sigmoid(bg[:, :B_HEADS])
    pre = bg[:, B_HEADS:] + W["hyb_dt_bias"][j][None]
    g = -jnp.exp(W["hyb_a_log"][j])[None] * jax.nn.softplus(pre)
    grow = g.T.reshape(B_HEADS, N, 1, B_CHUNK)
    brow = beta.T.reshape(B_HEADS, N, 1, B_CHUNK)
    nw = W["hyb_norm_w"][j][None]
    o_b, states, invs, qkvc = _dn_fwd(proj, W["hyb_conv_w"][j], nw, grow, brow, "dn_fwd" + sfx)
    mix = jnp.concatenate([o_a, o_b], axis=1).astype(ACT_DTYPE)
    out_ready(mix)
    out = _matmul(mix, W["hyb_w_out"][j], "nn", "hyb_out" + sfx, epi=_epi_residual_ln, extra=ln, out_dtype=_ln_out(),
                  tm=512)
    res = dict(proj=proj, o_a=o_a, lse=lse, qkvc=qkvc, beta=beta, pre=pre, g=g, grow=grow, brow=brow,
               states=states, invs=invs, mix=mix, sinks_b=sinks_b, nw=nw)
    return out, res


def _hybrid_bwd(x, du, W, j, res, tables, sfx):
    cos, sin_s = tables
    T = x.shape[0]
    proj = res["proj"]
    d_wout = _matmul(res["mix"], du, "tn", "hyb_out_dw" + sfx)
    dmix = _matmul(du, W["hyb_w_out"][j], "nt", "hyb_out_dx" + sfx)
    dproj, dkc, dkp, dvc, dvp, dsink = _attn_bwd(proj, cos, sin_s, res["sinks_b"], res["o_a"], res["lse"], dmix,
                                                  "attn_bwd" + sfx)
    zpad = jnp.zeros((WINDOW, LANES), F32)
    dk = dkc + jnp.concatenate([dkp[WINDOW:], zpad], axis=0)
    dv = dvc + jnp.concatenate([dvp[WINDOW:], zpad], axis=0)
    dqkvc, dproj, dg4, dbeta4, dnw = _dn_bwd(res["qkvc"], proj, res["nw"], res["grow"], res["brow"], res["states"],
                                             res["invs"], dmix, dproj, "dn_bwd" + sfx)
    dproj, dconv = _dnconv_bwd(proj, W["hyb_conv_w"][j], dqkvc, dproj, "dnconv_bwd" + sfx)
    dg = dg4.reshape(B_HEADS, T).T
    dbeta = dbeta4.reshape(B_HEADS, T).T
    beta = res["beta"]
    dbeta_logit = dbeta * beta * (1.0 - beta)
    da_logit = dg * (-jnp.exp(W["hyb_a_log"][j]))[None] * jax.nn.sigmoid(res["pre"])
    d_dt_bias = jnp.sum(da_logit, axis=0)
    d_a_log = jnp.sum(dg * res["g"], axis=0)
    zcols = jnp.zeros((T, HYB_PAD - HP_BG - 2 * B_HEADS), F32)
    tail = jnp.concatenate([dk, dv, dbeta_logit, da_logit, zcols], axis=1).astype(ACT_DTYPE)
    dproj = lax.dynamic_update_slice(dproj, tail, (0, HP_K))
    d_win = _matmul(x, dproj, "tn", "hyb_in_dw" + sfx)
    dx = _matmul(dproj, W["hyb_w_in"][j], "nt", "hyb_in_dx" + sfx, epi=_epi_add_residual, extra=du)
    grads = dict(hyb_w_in=d_win, hyb_w_out=d_wout, hyb_sinks=dsink[0], hyb_conv_w=dconv, hyb_a_log=d_a_log,
                 hyb_dt_bias=d_dt_bias, hyb_norm_w=jnp.sum(dnw[:, 0, :], axis=0))
    return dx, grads


def _rec_fwd(x, W, j, sfx, ln):
    proj = _matmul(x, W["rec_w_in"][j], "nn", "rec_in" + sfx)
    sp = jax.nn.softplus(-W["rec_lambda"][j])[None]
    hg, h = _rglru_fwd(proj, W["rec_conv_w"][j], W["rec_conv_b"][j][None], W["rec_w_a"][j], W["rec_w_x"][j],
                       W["rec_b_a"][j][None], W["rec_b_x"][j][None], sp, "rglru_fwd" + sfx)
    out = _matmul(hg, W["rec_w_out"][j], "nn", "rec_out" + sfx, epi=_epi_residual_ln, extra=ln, out_dtype=_ln_out(),
                  tm=512)
    return out, dict(proj=proj, hg=hg, h=h, sp=sp)


def _rec_bwd(x, du, W, j, res, sfx):
    d_wout = _matmul(res["hg"], du, "tn", "rec_out_dw" + sfx)
    dhg = _matmul(du, W["rec_w_out"][j], "nt", "rec_out_dx" + sfx)
    dproj, dcw, dcb, dwa, dwx, dba, dbx, dsp = _rglru_bwd(
        res["proj"], W["rec_conv_w"][j], W["rec_conv_b"][j][None], W["rec_w_a"][j], W["rec_w_x"][j],
        W["rec_b_a"][j][None], W["rec_b_x"][j][None], res["sp"], res["h"], dhg, "rglru_bwd" + sfx)
    d_lam = dsp[0] * (-jax.nn.sigmoid(-W["rec_lambda"][j]))
    d_win = _matmul(x, dproj, "tn", "rec_in_dw" + sfx)
    dx = _matmul(dproj, W["rec_w_in"][j], "nt", "rec_in_dx" + sfx, epi=_epi_add_residual, extra=du)
    grads = dict(rec_w_in=d_win, rec_w_out=d_wout, rec_conv_w=dcw, rec_conv_b=dcb[0], rec_w_a=dwa, rec_w_x=dwx,
                 rec_b_a=dba[0], rec_b_x=dbx[0], rec_lambda=d_lam)
    return dx, grads


def _local_step(x, tgt, W, mlp_w, mixer_ready, on_group):
    T = x.shape[0]
    tables = _rope_tables(T)
    acts = []
    xb = x.astype(ACT_DTYPE)
    for layer in range(DEPTH):
        j, sfx = layer // 2, ""
        last = layer == DEPTH - 1
        mixer_ready(layer, xb)
        ln1 = (xb if layer else x, W["ln1_g"][layer][None], W["ln1_b"][layer][None])
        if layer % 2 == 0:
            (x1b, u1), res = _hybrid_fwd(xb, W, j, tables, sfx,
                                         functools.partial(mixer_ready, layer, out_projection=True), ln1)
        else:
            (x1b, u1), res = _rec_fwd(xb, W, j, sfx, ln1)
        w1, w2, wl = mlp_w(layer, x1b)
        h1 = _matmul(x1b, w1, "nn", "mlp_up", out_dtype=ACT_DTYPE, b_chips=("j", wl))
        out = _matmul(h1, w2, "nn", "mlp_down", a_fn=_relu2, b_chips=("k", wl),
                      epi=_epi_residual_ln_last if last else _epi_residual_ln,
                      extra=(x1b, W["ln2_g"][layer][None], W["ln2_b"][layer][None]), out_dtype=_ln_out(last))
        acts.append(dict(xb=xb, res=res, u1=u1, x1b=x1b, h1=h1, u2=out[-1]))
        xb = out[-2]
    dx, loss = _loss_head(out[0], tgt, "loss_head")
    per_layer = [None] * DEPTH
    d_w1 = dict(lower=lax.empty((N_CHIPS, 1, D_MODEL, D_FF // N_CHIPS), F32),
                upper=lax.empty((N_CHIPS, DEPTH - 1, D_MODEL, D_FF // N_CHIPS), F32))
    d_w2 = dict(lower=lax.empty((N_CHIPS, 1, D_FF // N_CHIPS, D_MODEL), F32),
                upper=lax.empty((N_CHIPS, DEPTH - 1, D_FF // N_CHIPS, D_MODEL), F32))
    token = None
    for layer in reversed(range(DEPTH)):
        j, a = layer // 2, acts[layer]
        grp, li = ("upper", layer - 1) if layer else ("lower", 0)
        ln2_g = W["ln2_g"][layer][None]
        if token is not None:
            ln2_g = ln2_g + token
        du2, dg2, db2 = _ln_bwd(a["u2"], ln2_g, dx, "ln_bwd")
        w1, w2, wl = mlp_w(layer, du2)
        d_w2[grp] = _matmul(a["h1"], du2, "tn", "mlp_down_dw", a_fn=_relu2, out_chips=("i", li, d_w2[grp]))
        dh1 = _matmul(du2, w2, "nt", "mlp_down_dx", epi=_epi_drelu2, extra=a["h1"], out_dtype=ACT_DTYPE,
                      b_chips=("j", wl))
        d_w1[grp] = _matmul(a["x1b"], dh1, "tn", "mlp_up_dw", out_chips=("j", li, d_w1[grp]))
        dx1 = _matmul(dh1, w1, "nt", "mlp_up_dx", epi=_epi_add_residual, extra=du2, b_chips=("k", wl))
        ln1_g = W["ln1_g"][layer][None]
        token = on_group("middle", dx1, None, None, None) if layer == 0 else None
        if token is not None:
            ln1_g = ln1_g + token
        du1, dg1, db1 = _ln_bwd(a["u1"], ln1_g, dx1, "ln_bwd")
        if layer % 2 == 0:
            dx, g = _hybrid_bwd(a["xb"], du1, W, j, a["res"], tables, "")
        else:
            dx, g = _rec_bwd(a["xb"], du1, W, j, a["res"], "")
        g.update(ln1_g=dg1[0], ln1_b=db1[0], ln2_g=dg2[0], ln2_b=db2[0])
        per_layer[layer] = g
        if layer == 1:
            token = on_group("upper", [per_layer[2]], [per_layer[1], per_layer[3]], d_w1["upper"], d_w2["upper"])
        elif layer == 0:
            on_group("lower", [per_layer[0]], [], d_w1["lower"], d_w2["lower"])
    grads = {}
    for name in ("ln1_g", "ln1_b", "ln2_g", "ln2_b"):
        grads[name] = jnp.stack([per_layer[l][name] for l in range(DEPTH)])
    for name in ("hyb_norm_w", "hyb_sinks", "hyb_a_log", "hyb_dt_bias"):
        grads[name] = jnp.stack([per_layer[l][name] for l in (0, 2)])
    return loss, dx, grads


BIG = ("hyb_w_in", "hyb_w_out", "rec_w_in", "rec_w_out", "mlp_w1", "mlp_w2", "rec_w_a", "rec_w_x")
COL_SHARDED = ("hyb_w_in", "rec_w_in", "mlp_w1")
CHIP_MAJOR = ("mlp_w1", "mlp_w2")
MID = ("hyb_conv_w", "rec_conv_w", "rec_conv_b", "rec_b_a", "rec_b_x", "rec_lambda")
SMALL = ("ln1_g", "ln1_b", "ln2_g", "ln2_b", "hyb_norm_w", "hyb_sinks", "hyb_a_log", "hyb_dt_bias")
WEIGHTS = ("hyb_w_in", "hyb_sinks", "hyb_conv_w", "hyb_a_log", "hyb_dt_bias", "hyb_norm_w", "hyb_w_out", "rec_w_in",
           "rec_conv_w", "rec_conv_b", "rec_w_a", "rec_b_a", "rec_w_x", "rec_b_x", "rec_lambda", "rec_w_out", "ln1_g",
           "ln1_b", "mlp_w1", "mlp_w2", "ln2_g", "ln2_b")


def _gather_full_weights(w):
    wb = {k: w[k].astype(MXU_DTYPE) for k in BIG}
    now = ("hyb_w_in",)
    shards = [_halves(wb[k][:1]) for k in now]
    shards.append(_pack_mid(*[w[k] for k in MID]))
    got = _all_gather_weights(shards, "all_gather_weights")
    me = 2 * lax.axis_index("x") + lax.axis_index("y")
    got = [lax.dynamic_update_slice(g, s[None], (me, 0, 0, 0)) for s, g in zip(shards, got)]

    def full(k, g):
        g = g.reshape((N_CHIPS,) + w[k].shape[1:])
        if k in CHIP_MAJOR:
            return g[:, None]
        if k in ("rec_w_a", "rec_w_x"):
            return g.transpose(1, 0, 2, 3).reshape(LRU_BLOCKS, LRU_BLOCK_W, LRU_BLOCK_W)
        f = _cols_from_chips(g[:, None])[0] if k in COL_SHARDED else _rows_from_chips(g[:, None])[0]
        return _pad_hyb_cols(f) if k == "hyb_w_in" else f

    rec = ("rec_w_in", "rec_w_out", "rec_w_a", "rec_w_x")
    groups = [(("hyb_w_out",), 0), (CHIP_MAJOR, 0), (rec, 0), (CHIP_MAJOR, 1), (("hyb_w_in", "hyb_w_out"), 1),
              (CHIP_MAJOR, 2), (rec, 1), (CHIP_MAJOR, 3)]
    own = [wb[k][j] for names, j in groups for k in names]
    land = [lax.dynamic_update_slice(lax.empty((N_CHIPS,) + o.shape, o.dtype), o[None], (me,) + (0,) * o.ndim)
            for o in own]
    send_sems, recv_sems, own, land = _gather_start(own, land, got[-1], "gather_start")
    W = {k: [None] * w[k].shape[0] for k in BIG}
    for k, g in zip(now, got[:-1]):
        W[k][0] = full(k, g)
    arrived = [0]

    def ensure(upto, after):
        while arrived[0] <= upto:
            gi = arrived[0]
            names, j = groups[gi]
            lo = sum(len(nm) for nm, _ in groups[:gi])
            sl = slice(lo, lo + len(names))
            got_g = _gather_wait(send_sems[sl], recv_sems[sl], own[sl], land[sl], after, "gather_wait_%d" % gi)
            for k, g in zip(names, got_g):
                W[k][j] = full(k, g)
            arrived[0] += 1

    def mixer_ready(layer, after, out_projection=False):
        if layer:
            ensure({1: 2, 2: 4, 3: 6}[layer], after)
        elif out_projection:
            ensure(0, after)

    def mlp_w(layer, after):
        ensure({0: 1, 1: 3, 2: 5, 3: 7}[layer], after)
        return W["mlp_w1"][layer], W["mlp_w2"][layer], 0

    conv_w, rconv_w, rconv_b, b_a, b_x, lam = _unpack_mid(got[-1])
    W["hyb_conv_w"] = conv_w.transpose(1, 2, 0, 3).reshape(2, CONV_K, 3 * B_W)
    W["rec_conv_w"] = rconv_w.transpose(1, 2, 0, 3).reshape(2, CONV_K, D_MODEL)
    for k, v in (("rec_conv_b", rconv_b), ("rec_b_a", b_a), ("rec_b_x", b_x), ("rec_lambda", lam)):
        W[k] = v.transpose(1, 0, 2).reshape(2, D_MODEL)
    for k in SMALL:
        W[k] = w[k]
    return W, mlp_w, mixer_ready


REC_VECTORS = ("rec_conv_b", "rec_b_a", "rec_b_x", "rec_lambda")


def _group_by_chip(hyb, rec, d_w1, d_w2):
    t = dict(mlp_w1=d_w1, mlp_w2=d_w2)
    if hyb:
        t["hyb_w_in"] = _cols_to_chips(jnp.stack([_unpad_hyb_cols(g["hyb_w_in"]) for g in hyb]))
        t["hyb_w_out"] = _rows_to_chips(jnp.stack([g["hyb_w_out"] for g in hyb]))
    if rec:
        t["rec_w_in"] = _cols_to_chips(jnp.stack([g["rec_w_in"] for g in rec]))
        t["rec_w_out"] = _rows_to_chips(jnp.stack([g["rec_w_out"] for g in rec]))
        for k in ("rec_w_a", "rec_w_x"):
            v = jnp.stack([g[k] for g in rec])
            t[k] = v.reshape(len(rec), LRU_BLOCKS, N_CHIPS, LRU_BLOCK_W // N_CHIPS, LRU_BLOCK_W).transpose(2, 0, 1, 3, 4)
    flat = [g["hyb_conv_w"].reshape(CONV_K, N_CHIPS, -1).transpose(1, 0, 2).reshape(N_CHIPS, -1) for g in hyb]
    for g in rec:
        flat.append(g["rec_conv_w"].reshape(CONV_K, N_CHIPS, -1).transpose(1, 0, 2).reshape(N_CHIPS, -1))
        flat += [g[k].reshape(N_CHIPS, -1) for k in REC_VECTORS]
    flat = jnp.concatenate(flat, axis=1)
    flat = jnp.pad(flat, ((0, 0), (0, -flat.shape[1] % (2 * D_MODEL))))
    names = [k for k in BIG if k in t]
    out = [t[k].reshape(N_CHIPS, 2, -1, t[k].shape[-1]) for k in names]
    return names + ["small"], out + [flat.reshape(N_CHIPS, 2, -1, D_MODEL)]


def kernel(x, hyb_w_in, hyb_sinks, hyb_conv_w, hyb_a_log, hyb_dt_bias, hyb_norm_w, hyb_w_out, rec_w_in, rec_conv_w, rec_conv_b, rec_w_a, rec_b_a, rec_w_x, rec_b_x, rec_lambda, rec_w_out, ln1_g, ln1_b, mlp_w1, mlp_w2, ln2_g, ln2_b, loss_target, m_hyb_w_in, m_hyb_sinks, m_hyb_conv_w, m_hyb_a_log, m_hyb_dt_bias, m_hyb_norm_w, m_hyb_w_out, m_rec_w_in, m_rec_conv_w, m_rec_conv_b, m_rec_w_a, m_rec_b_a, m_rec_w_x, m_rec_b_x, m_rec_lambda, m_rec_w_out, m_ln1_g, m_ln1_b, m_mlp_w1, m_mlp_w2, m_ln2_g, m_ln2_b, v_hyb_w_in, v_hyb_sinks, v_hyb_conv_w, v_hyb_a_log, v_hyb_dt_bias, v_hyb_norm_w, v_hyb_w_out, v_rec_w_in, v_rec_conv_w, v_rec_conv_b, v_rec_w_a, v_rec_b_a, v_rec_w_x, v_rec_b_x, v_rec_lambda, v_rec_w_out, v_ln1_g, v_ln1_b, v_mlp_w1, v_mlp_w2, v_ln2_g, v_ln2_b):
    args = locals()
    w = {k: args[k] for k in WEIGHTS}
    m = {k: args["m_" + k] for k in WEIGHTS}
    v = {k: args["v_" + k] for k in WEIGHTS}

    W, mlp_w, mixer_ready = _gather_full_weights(w)

    core = lax.axis_index("c").astype(jnp.int32)
    me = (2 * lax.axis_index("x") + lax.axis_index("y")).astype(jnp.int32)
    slots = jnp.arange(N_CHIPS, dtype=jnp.int32)
    where = jnp.concatenate([me[None], jnp.where(slots == me, (slots + 1) % N_CHIPS, slots), core[None]])
    state = {}

    def on_group(group, hyb, rec, d_w1, d_w2):
        if group == "middle":
            group = "upper"
            send_sems, recv_sems, by_chip, recv, _ = state["sibling"]
            by_chip, from_sibling = _rs_sibling_wait(send_sems, recv_sems, by_chip, recv, hyb, "rs_sibling_wait")
        else:
            state[group + " names"], by_chip = _group_by_chip(hyb, rec, d_w1, d_w2)
            if group == "upper":
                recv = [lax.empty((N_CHIPS,) + g.shape[2:], g.dtype) for g in by_chip]
                state["sibling"] = _rs_sibling_start(by_chip, recv, "rs_sibling_start")
                return state["sibling"][4][0, 0]
            from_sibling = _rs_to_sibling(by_chip, "rs_to_sibling")
        pair = [_pair_sum(g, r, core[None], "pair_sum") for g, r in zip(by_chip, from_sibling)]
        recv = [lax.empty(p.shape, p.dtype) for p in pair]
        state[group] = _rs_across_start(pair, recv, "rs_across_start_" + group)
        return state[group][4][0, 0]

    def finish_group(group, after, prev):
        send_sems, recv_sems, pair, recv, _ = state[group]
        pair, from_chips = _rs_across_wait(send_sems, recv_sems, pair, recv, after, "rs_across_wait_" + group)
        half = [_chip_sum(r, p, where, "chip_sum") for r, p in zip(from_chips, pair)]
        joined = _rs_join_halves(half, "rs_join_halves")
        outs = dict(prev or {})
        for k, g in zip(state[group + " names"][:-1], joined[:-1]):
            n = g.shape[-1]
            w2, g2 = w[k].reshape(-1, n), g.reshape(-1, n)
            row0 = w2.shape[0] - g2.shape[0] if group == "upper" else 0
            outs[k] = _adamw_rows(w2, g2, m[k].reshape(-1, n), v[k].reshape(-1, n), row0, outs.get(k), "adamw")
        return outs, joined[-1].reshape(-1)

    loss, dx, grads = _local_step(x[0], loss_target[0], W, mlp_w, mixer_ready, on_group)
    loss = lax.psum(loss[0, 0], ("x", "y", "c"))
    g_out, d_out, m_out, v_out = {}, {}, {}, {}

    upper, small_upper = finish_group("upper", state["lower"][4], None)
    small_g = _pack_small(*[grads[k] for k in SMALL])
    small_all = _all_gather_small(small_g, "all_gather_small")
    sw, sm, sv = (_pack_small(*[t[k] for k in SMALL]) for t in (w, m, v))
    sg, sd, snm, snv = _adamw_small(sw, small_all, sm, sv, "adamw_small")
    for dst, packed in ((g_out, sg), (d_out, sd), (m_out, snm), (v_out, snv)):
        for k, val in zip(SMALL, _unpack_small(packed)):
            dst[k] = val

    done = sum([upper[k][1][0, 0] for k in BIG], sg[0, 0]).reshape(1, 1)
    both, small_lower = finish_group("lower", done, upper)
    for k in BIG:
        shape = w[k].shape
        g_out[k], d_out[k], m_out[k], v_out[k] = (t.reshape(shape) for t in both[k])
    n_conv, n_rconv, n_vec = (w[k][0].size for k in ("hyb_conv_w", "rec_conv_w", "rec_conv_b"))
    n_rec = n_rconv + len(REC_VECTORS) * n_vec
    conv_g = jnp.stack([small_lower[:n_conv], small_upper[:n_conv]])
    rec_g = jnp.stack([small_upper[n_conv + i * n_rec:n_conv + (i + 1) * n_rec] for i in range(2)])
    vec_g = [rec_g[:, n_rconv + i * n_vec:n_rconv + (i + 1) * n_vec] for i in range(len(REC_VECTORS))]
    mid_g = _pack_mid(conv_g, rec_g[:, :n_rconv], *vec_g).reshape(-1, D_MODEL)
    mid_w, mid_m, mid_v = (_pack_mid(*[t[k] for k in MID]).reshape(-1, D_MODEL) for t in (w, m, v))
    mid_d, mid_nm, mid_nv = _adamw(mid_w, mid_g, mid_m, mid_v, "adamw_mid")
    for dst, packed in ((g_out, mid_g), (d_out, mid_d), (m_out, mid_nm), (v_out, mid_nv)):
        for k, val in zip(MID, _unpack_mid(packed.reshape(2, MID_ROWS, D_MODEL))):
            dst[k] = val.reshape(w[k].shape)

    return (loss, dx[None], *[g_out[k] for k in WEIGHTS], *[d_out[k] for k in WEIGHTS],
            *[m_out[k] for k in WEIGHTS], *[v_out[k] for k in WEIGHTS])
```

```python
import functools
import math

import jax
import jax.numpy as jnp
import numpy as np
from jax import lax
from jax.experimental import pallas as pl
from jax.experimental.pallas import tpu as pltpu

F32 = jnp.float32
MXU_DTYPE = jnp.bfloat16
ACT_DTYPE = jnp.bfloat16
ICI_DTYPE = jnp.bfloat16

D_MODEL = 1024
DEPTH = 4
A_HEAD_DIM = 64
A_Q_HEADS = 8
A_KV_HEADS = 2
WINDOW = 128
ROPE_THETA = 10000.0
B_HEADS = 4
B_HEAD_DIM = 128
B_CHUNK = 64
CONV_K = 4
LRU_BLOCKS = 4
LRU_BLOCK_W = D_MODEL // LRU_BLOCKS
LRU_C = 8.0
D_FF = 4 * D_MODEL
A_Q_W = A_Q_HEADS * A_HEAD_DIM
A_KV_W = A_KV_HEADS * A_HEAD_DIM
B_W = B_HEADS * B_HEAD_DIM
HYB_PROJ = A_Q_W + 2 * A_KV_W + 4 * B_W + 2 * B_HEADS
DN_ALPHA = (2 * DEPTH) ** 0.25
LN_EPS = 1e-5
NORM_EPS = 1e-6
ADAM_LR = 0.001
ADAM_B1 = 0.9
ADAM_B2 = 0.999
ADAM_EPS = 1e-08
ADAM_WD = 0.01
ADAM_STEP = 10

HP_Q = 0
HP_QKVB = 512
HP_Z = 2048
HP_K = 2560
HP_V = 2688
HP_BG = 2816
HYB_PAD = 3072

N_CHIPS = 4
N_DEV = 8
V7X_VMEM_LIMIT = 48 * 1024 * 1024
LANES = 128
SUBLANES = 8
NEG_BIG = -1e30

NN = (((1,), (0,)), ((), ()))
NT = (((1,), (1,)), ((), ()))
TN = (((0,), (0,)), ((), ()))


def _cparams(*sem):
    return pltpu.CompilerParams(dimension_semantics=sem, vmem_limit_bytes=V7X_VMEM_LIMIT)


def _dot(a, b, dims=NN):
    return lax.dot_general(a.astype(MXU_DTYPE), b.astype(MXU_DTYPE), dims, preferred_element_type=F32)


def _split_bf16(a):
    hi = a.astype(jnp.bfloat16)
    return hi, (a - hi.astype(F32)).astype(jnp.bfloat16)


def _dotf(a, b, dims=NN):
    ah, al = _split_bf16(a)
    bh, bl = _split_bf16(b)
    dg = functools.partial(lax.dot_general, dimension_numbers=dims, preferred_element_type=F32)
    return dg(ah, bh) + (dg(ah, bl) + dg(al, bh))


def _tile(dim, pref):
    t = min(dim, pref)
    while dim % t:
        t //= 2
    return t


def _sigmoid(x):
    return 1.0 / (1.0 + jnp.exp(-x))


def _silu(x):
    return x * _sigmoid(x)


def _dsilu(x):
    s = _sigmoid(x)
    return s * (1.0 + x * (1.0 - s))


GELU_C = 0.7978845608028654
GELU_A = 0.044715


def _gelu(x):
    return 0.5 * x * (1.0 + jnp.tanh(GELU_C * (x + GELU_A * x * x * x)))


def _dgelu(x):
    t = jnp.tanh(GELU_C * (x + GELU_A * x * x * x))
    return 0.5 * (1.0 + t) + 0.5 * x * (1.0 - t * t) * GELU_C * (1.0 + 3.0 * GELU_A * x * x)


def _matmul(a, b, mode, name, *, tm=1024, tn=1024, tk=1024, a_fn=None, epi=None, extra=None, out_dtype=F32,
            b_chips=None, out_chips=None):
    chunk = tk
    if mode == "tn":
        K, M = a.shape
        if K <= 4 * tk:
            tk, tm = K, tm // 2
    else:
        M, K = a.shape
    whole_k = False
    if b_chips is not None:
        g, b_layer = b_chips
        r, n = b.shape[2:]
        n_dim, k_dim = (r, n) if mode == "nt" else (n, r)
        N = N_CHIPS * n_dim if g == "j" else n_dim
        assert K == (N_CHIPS * k_dim if g == "k" else k_dim)
        if g == "j":
            tn = n_dim
        else:
            whole_k, tk, tm = True, K, tm // 2
    elif mode == "nt":
        N = b.shape[0]
        if tk < K <= 3 * tk:
            tk, tm = K, tm // 2
    else:
        N = b.shape[1]
    if out_chips is not None:
        og, o_layer, o_buf = out_chips
        if og == "j":
            tn = o_buf.shape[3]
        else:
            tm = min(tm, o_buf.shape[2])
    tm, tn, tk = _tile(M, tm), _tile(N, tn), _tile(K, tk)
    nk = K // tk
    if mode == "tn":
        a_spec = pl.BlockSpec((tk, tm), lambda i, j, k: (k, i))
    else:
        a_spec = pl.BlockSpec((tm, tk), lambda i, j, k: (i, k))
    b_block = (tn, tk) if mode == "nt" else (tk, tn)
    if b_chips is None:
        b_spec = pl.BlockSpec(b_block, (lambda i, j, k: (j, k)) if mode == "nt" else (lambda i, j, k: (k, j)))
    elif mode == "nt" and whole_k:
        b_spec = pl.BlockSpec((N_CHIPS, None, tn, k_dim), lambda i, j, k: (0, b_layer, j, 0))
    elif mode == "nt":
        b_spec = pl.BlockSpec((None, None) + b_block, lambda i, j, k: (j, b_layer, 0, k))
    elif whole_k:
        b_spec = pl.BlockSpec((N_CHIPS, None, k_dim, tn), lambda i, j, k: (0, b_layer, 0, j),
                              pipeline_mode=pl.Buffered(1) if N == tn else None)
    else:
        b_spec = pl.BlockSpec((None, None) + b_block, lambda i, j, k: (j, b_layer, k, 0))
    o_spec = pl.BlockSpec((tm, tn), lambda i, j, k: (i, j))
    e_spec = o_spec
    if out_chips is not None:
        per = o_buf.shape[2] // tm
        o_spec = pl.BlockSpec((None, None, tm, tn), (lambda i, j, k: (j, o_layer, i, 0)) if og == "j"
                              else (lambda i, j, k: (i // per, o_layer, i % per, j)))
    dims = {"nn": NN, "nt": NT, "tn": TN}[mode]
    extras = () if extra is None else (extra if isinstance(extra, tuple) else (extra,))
    out_dtypes = out_dtype if isinstance(out_dtype, tuple) else (out_dtype,)
    n_in = 2 + len(extras) + (out_chips is not None)

    def body(*refs):
        a_ref, b_ref = refs[0], refs[1]
        e_refs = refs[2:2 + len(extras)]
        o_refs = refs[n_in:n_in + len(out_dtypes)]
        av = a_ref[...]
        if a_fn is not None:
            av = a_fn(av)
        if whole_k and mode == "nt":
            part = _dot(av[:, :k_dim], b_ref[0], dims)
            for chip in range(1, N_CHIPS):
                part = part + _dot(av[:, chip * k_dim:(chip + 1) * k_dim], b_ref[chip], dims)
        elif mode == "tn" and tk > chunk:
            part = _dot(av[:chunk], b_ref[0:chunk, :], dims)
            for c0 in range(chunk, tk, chunk):
                part = part + _dot(av[c0:c0 + chunk], b_ref[c0:c0 + chunk, :], dims)
        else:
            bv = b_ref[...]
            if whole_k:
                bv = bv.reshape(K, tn)
            part = _dot(av, bv, dims)

        def finish(acc):
            if epi is not None:
                acc = epi(acc, *[e[...] for e in e_refs])
            for o_ref, val, dt in zip(o_refs, acc if isinstance(acc, tuple) else (acc,), out_dtypes):
                o_ref[...] = val.astype(dt)

        if nk == 1:
            finish(part)
        else:
            acc_ref = refs[-1]
            k = pl.program_id(2)

            @pl.when(k == 0)
            def _():
                acc_ref[...] = part

            @pl.when(k > 0)
            def _():
                acc_ref[...] += part

            @pl.when(k == nk - 1)
            def _():
                finish(acc_ref[...])

    row_spec = pl.BlockSpec((1, tn), lambda i, j, k: (0, j))
    in_specs = [a_spec, b_spec] + [row_spec if e.shape[0] == 1 else e_spec for e in extras]
    args = (a, b) + extras
    out_shape = [jax.ShapeDtypeStruct((M, N), dt) for dt in out_dtypes]
    out_specs = [o_spec] * len(out_dtypes)
    aliases = {}
    if out_chips is not None:
        in_specs.append(pl.BlockSpec(memory_space=pl.ANY))
        args += (o_buf,)
        out_shape = [jax.ShapeDtypeStruct(o_buf.shape, o_buf.dtype)]
        aliases = {n_in - 1: 0}
    if not isinstance(out_dtype, tuple):
        out_shape, out_specs = out_shape[0], out_specs[0]
    return pl.pallas_call(
        body, name=name, grid=(M // tm, N // tn, nk), in_specs=in_specs, out_specs=out_specs, out_shape=out_shape,
        input_output_aliases=aliases,
        scratch_shapes=[pltpu.VMEM((tm, tn), F32)] if nk > 1 else [],
        compiler_params=_cparams("parallel", "parallel", "arbitrary"),
    )(*args)


def _relu2(v):
    r = jnp.maximum(v, 0.0)
    return r * r


def _epi_drelu2(acc, h):
    return acc * (2.0 * jnp.maximum(h, 0.0))


def _epi_add_residual(acc, du):
    return acc + DN_ALPHA * du


def _epi_residual_ln(acc, x, g, b):
    u = DN_ALPHA * x + acc
    mu = jnp.mean(u, axis=-1, keepdims=True)
    d = u - mu
    var = jnp.mean(d * d, axis=-1, keepdims=True)
    return d * lax.rsqrt(var + LN_EPS) * g + b, u


def _epi_residual_ln_last(acc, x, g, b):
    o, u = _epi_residual_ln(acc, x, g, b)
    return o, o, u


def _ln_out(last=False):
    return (F32, ACT_DTYPE, F32) if last else (ACT_DTYPE, F32)


def _ln_bwd(u, g, dout, name):
    T, D = u.shape
    tr = _tile(T, 512)

    def body(u_ref, g_ref, d_ref, dub_ref, dg_ref, db_ref):
        i = pl.program_id(0)
        u = u_ref[...]
        mu = jnp.mean(u, axis=-1, keepdims=True)
        d = u - mu
        rstd = lax.rsqrt(jnp.mean(d * d, axis=-1, keepdims=True) + LN_EPS)
        xhat = d * rstd
        dout_v = d_ref[...]
        dxh = dout_v * g_ref[...]
        m1 = jnp.mean(dxh, axis=-1, keepdims=True)
        m2 = jnp.mean(dxh * xhat, axis=-1, keepdims=True)
        du = rstd * (dxh - m1 - xhat * m2)
        dub_ref[...] = du.astype(ACT_DTYPE)
        pg = jnp.sum(dout_v * xhat, axis=0, keepdims=True)
        pb = jnp.sum(dout_v, axis=0, keepdims=True)

        @pl.when(i == 0)
        def _():
            dg_ref[...] = pg
            db_ref[...] = pb

        @pl.when(i > 0)
        def _():
            dg_ref[...] += pg
            db_ref[...] += pb

    row = pl.BlockSpec((tr, D), lambda i: (i, 0))
    vec = pl.BlockSpec((1, D), lambda i: (0, 0))
    return pl.pallas_call(
        body, name=name, grid=(T // tr,), in_specs=[row, vec, row], out_specs=[row, vec, vec],
        out_shape=[jax.ShapeDtypeStruct((T, D), ACT_DTYPE),
                   jax.ShapeDtypeStruct((1, D), F32), jax.ShapeDtypeStruct((1, D), F32)],
        compiler_params=_cparams("arbitrary"),
    )(u, g, dout)


def _loss_head(y, tgt, name):
    T, D = y.shape
    tr = _tile(T, 512)

    def body(y_ref, t_ref, dy_ref, l_ref):
        i = pl.program_id(0)
        e = y_ref[...] - t_ref[...]
        dy_ref[...] = e * (1.0 / D)
        part = jnp.sum(e * e, axis=(0, 1), keepdims=True) * (0.5 / D)

        @pl.when(i == 0)
        def _():
            l_ref[...] = part

        @pl.when(i > 0)
        def _():
            l_ref[...] += part

    row = pl.BlockSpec((tr, D), lambda i: (i, 0))
    one = pl.BlockSpec((1, 1), lambda i: (0, 0))
    return pl.pallas_call(
        body, name=name, grid=(T // tr,), in_specs=[row, row], out_specs=[row, one],
        out_shape=[jax.ShapeDtypeStruct((T, D), F32), jax.ShapeDtypeStruct((1, 1), F32)],
        compiler_params=_cparams("arbitrary"),
    )(y, tgt)


def _swap_half(x):
    n = x.shape[-1]
    lane = lax.broadcasted_iota(jnp.int32, x.shape, 1)
    first = (lane % A_HEAD_DIM) < (A_HEAD_DIM // 2)
    return jnp.where(first, pltpu.roll(x, n - A_HEAD_DIM // 2, axis=1), pltpu.roll(x, A_HEAD_DIM // 2, axis=1))


def _rope(x, cos, sin_signed):
    return x * cos + _swap_half(x) * sin_signed


def _rope_t(dy, cos, sin_signed):
    return dy * cos + _swap_half(dy * sin_signed)


def _rope_tables(T):
    half = A_HEAD_DIM // 2
    inv_freq = np.float32(ROPE_THETA) ** (-np.arange(half, dtype=np.float32) / np.float32(half))
    ang = np.arange(T, dtype=np.float32)[:, None] * inv_freq[None, :]
    cos = np.tile(np.cos(ang), (1, 4))
    sin = np.sin(ang)
    sin_signed = np.tile(np.concatenate([-sin, sin], axis=1), (1, 2))
    return jnp.asarray(cos, F32), jnp.asarray(sin_signed, F32)


def _band_mask(n):
    qi = lax.broadcasted_iota(jnp.int32, (WINDOW, 2 * WINDOW), 0)
    kj = lax.broadcasted_iota(jnp.int32, (WINDOW, 2 * WINDOW), 1)
    return (kj > qi) & (kj <= qi + WINDOW) & ((n > 0) | (kj >= WINDOW))


def _place(v, src_half, dst_half):
    lane = lax.broadcasted_iota(jnp.int32, v.shape, 1)
    if src_half != dst_half:
        v = pltpu.roll(v, A_HEAD_DIM, axis=1)
    keep = (lane >= A_HEAD_DIM) if dst_half else (lane < A_HEAD_DIM)
    return jnp.where(keep, v, 0.0)


def _attn_specs():
    kb, vb = HP_K // LANES, HP_V // LANES
    prev = lambda n: jnp.maximum(n - 1, 0)
    return dict(
        q=pl.BlockSpec((WINDOW, A_Q_W), lambda n: (n, 0)),
        kc=pl.BlockSpec((WINDOW, LANES), lambda n: (n, kb)),
        kp=pl.BlockSpec((WINDOW, LANES), lambda n: (prev(n), kb)),
        vc=pl.BlockSpec((WINDOW, LANES), lambda n: (n, vb)),
        vp=pl.BlockSpec((WINDOW, LANES), lambda n: (prev(n), vb)),
        tq=pl.BlockSpec((WINDOW, LANES), lambda n: (n, 0)),
        tp=pl.BlockSpec((WINDOW, LANES), lambda n: (prev(n), 0)),
        sink=pl.BlockSpec((A_Q_HEADS, LANES), lambda n: (0, 0)),
        row512=pl.BlockSpec((WINDOW, A_Q_W), lambda n: (n, 0)),
        row128=pl.BlockSpec((WINDOW, LANES), lambda n: (n, 0)),
        lse=pl.BlockSpec((WINDOW, A_Q_HEADS), lambda n: (n, 0)),
    )


def _attn_fwd(proj, cos, sin_s, sinks_b, name):
    T = proj.shape[0]
    sp = _attn_specs()

    def body(q_ref, kc_ref, kp_ref, vc_ref, vp_ref, cq_ref, sq_ref, cp_ref, sp_ref, sink_ref, o_ref, l_ref):
        n = pl.program_id(0)
        cq, sq = cq_ref[...], sq_ref[...]
        cq4, sq4 = jnp.tile(cq, (1, A_Q_W // LANES)), jnp.tile(sq, (1, A_Q_W // LANES))
        q = _rope(q_ref[...], cq4, sq4) * (A_HEAD_DIM ** -0.5)
        kc = _rope(kc_ref[...], cq, sq)
        kp = _rope(kp_ref[...], cp_ref[...], sp_ref[...])
        kk = jnp.concatenate([kp, kc], axis=0)
        vv = jnp.concatenate([vp_ref[...], vc_ref[...]], axis=0)
        mask = _band_mask(n)[None]
        lane = lax.broadcasted_iota(jnp.int32, (WINDOW, LANES), 1)
        lane8 = lax.broadcasted_iota(jnp.int32, (WINDOW, A_Q_HEADS), 1)
        qe = jnp.stack([_place(q[:, (hq // 2) * LANES:(hq // 2 + 1) * LANES], hq % 2, hq // 4) for hq in range(A_Q_HEADS)])
        kk8 = jnp.broadcast_to(kk[None], (A_Q_HEADS,) + kk.shape)
        vv8 = jnp.broadcast_to(vv[None], (A_Q_HEADS,) + vv.shape)
        sk = jnp.stack([sink_ref[hq:hq + 1, 0:1] for hq in range(A_Q_HEADS)])
        s = jnp.where(mask, _dot(qe, kk8, BNT), NEG_BIG)
        m = jnp.maximum(jnp.max(s, axis=-1, keepdims=True), sk)
        p = jnp.exp(s - m)
        den = jnp.sum(p, axis=-1, keepdims=True) + jnp.exp(sk - m)
        o = _dot(p * (1.0 / den), vv8, BNN)
        lse_h = m + jnp.log(den)
        outs = []
        lse = jnp.zeros((WINDOW, A_Q_HEADS), F32)
        for pb in range(A_Q_HEADS // 2):
            halves = [_place(o[2 * pb + e], pb // 2, e) for e in range(2)]
            outs.append(jnp.where(lane < A_HEAD_DIM, halves[0], halves[1]))
            for e in range(2):
                lse = jnp.where(lane8 == 2 * pb + e, lse_h[2 * pb + e], lse)
        o_ref[...] = jnp.concatenate(outs, axis=1)
        l_ref[...] = lse

    return pl.pallas_call(
        body, name=name, grid=(T // WINDOW,),
        in_specs=[sp["q"], sp["kc"], sp["kp"], sp["vc"], sp["vp"], sp["tq"], sp["tq"], sp["tp"], sp["tp"], sp["sink"]],
        out_specs=[sp["row512"], sp["lse"]],
        out_shape=[jax.ShapeDtypeStruct((T, A_Q_W), F32), jax.ShapeDtypeStruct((T, A_Q_HEADS), F32)],
        compiler_params=_cparams("parallel"),
    )(proj, proj, proj, proj, proj, cos, sin_s, cos, sin_s, sinks_b)


def _attn_bwd(proj, cos, sin_s, sinks_b, o, lse, dmix, name):
    T = proj.shape[0]
    sp = _attn_specs()

    def body(q_ref, kc_ref, kp_ref, vc_ref, vp_ref, cq_ref, sq_ref, cp_ref, sp_ref, sink_ref, o_ref, l_ref, do_ref,
             dq_ref, dkc_ref, dkp_ref, dvc_ref, dvp_ref, dsink_ref):
        n = pl.program_id(0)
        cq, sq = cq_ref[...], sq_ref[...]
        cp, sps = cp_ref[...], sp_ref[...]
        cq4, sq4 = jnp.tile(cq, (1, A_Q_W // LANES)), jnp.tile(sq, (1, A_Q_W // LANES))
        q = _rope(q_ref[...], cq4, sq4) * (A_HEAD_DIM ** -0.5)
        kc = _rope(kc_ref[...], cq, sq)
        kp = _rope(kp_ref[...], cp, sps)
        kk = jnp.concatenate([kp, kc], axis=0)
        vv = jnp.concatenate([vp_ref[...], vc_ref[...]], axis=0)
        mask = _band_mask(n)[None]
        lane = lax.broadcasted_iota(jnp.int32, (WINDOW, LANES), 1)
        do_all, o_all, l_all = do_ref[...], o_ref[...], l_ref[...]
        lane8 = lax.broadcasted_iota(jnp.int32, (WINDOW, A_Q_HEADS), 1)
        head8 = lax.broadcasted_iota(jnp.int32, (1, A_Q_HEADS), 1)
        prod = do_all * o_all
        qes, does, deltas, lhs = [], [], [], []
        for hq in range(A_Q_HEADS):
            pb, e, kvh = hq // 2, hq % 2, hq // 4
            blk = slice(pb * LANES, (pb + 1) * LANES)
            in_half = (lane >= A_HEAD_DIM) if e else (lane < A_HEAD_DIM)
            deltas.append(jnp.sum(jnp.where(in_half, prod[:, blk], 0.0), axis=-1, keepdims=True))
            qes.append(_place(q[:, blk], e, kvh))
            does.append(_place(do_all[:, blk], e, kvh))
            lhs.append(jnp.sum(jnp.where(lane8 == hq, l_all, 0.0), axis=-1, keepdims=True))
        qe, doe, delta, lh = jnp.stack(qes), jnp.stack(does), jnp.stack(deltas), jnp.stack(lhs)
        kk8 = jnp.broadcast_to(kk[None], (A_Q_HEADS,) + kk.shape)
        vv8 = jnp.broadcast_to(vv[None], (A_Q_HEADS,) + vv.shape)
        sk = jnp.stack([sink_ref[hq:hq + 1, 0:1] for hq in range(A_Q_HEADS)])
        s = _dot(qe, kk8, BNT)
        p = jnp.where(mask, jnp.exp(jnp.where(mask, s, NEG_BIG) - lh), 0.0)
        dvv = jnp.sum(_dot(p, doe, BTN), axis=0)
        ds = p * (_dot(doe, vv8, BNT) - delta)
        dkk = jnp.sum(_dot(ds, qe, BTN), axis=0)
        dqe = _dot(ds, kk8, BNN)
        dsink_h = -jnp.sum(jnp.exp(sk - lh) * delta, axis=(1, 2), keepdims=True)
        dqs = []
        dsk = jnp.zeros((1, A_Q_HEADS), F32)
        for pb in range(A_Q_HEADS // 2):
            halves = [_place(dqe[2 * pb + e], pb // 2, e) for e in range(2)]
            dqs.append(jnp.where(lane < A_HEAD_DIM, halves[0], halves[1]))
            for e in range(2):
                dsk = jnp.where(head8 == 2 * pb + e, dsink_h[2 * pb + e], dsk)
        dq = jnp.concatenate(dqs, axis=1) * (A_HEAD_DIM ** -0.5)
        dq_ref[...] = _rope_t(dq, cq4, sq4).astype(ACT_DTYPE)
        dkp_ref[...] = _rope_t(dkk[:WINDOW], cp, sps)
        dkc_ref[...] = _rope_t(dkk[WINDOW:], cq, sq)
        dvp_ref[...] = dvv[:WINDOW]
        dvc_ref[...] = dvv[WINDOW:]

        @pl.when(n == 0)
        def _():
            dsink_ref[...] = dsk

        @pl.when(n > 0)
        def _():
            dsink_ref[...] += dsk

    return pl.pallas_call(
        body, name=name, grid=(T // WINDOW,),
        in_specs=[sp["q"], sp["kc"], sp["kp"], sp["vc"], sp["vp"], sp["tq"], sp["tq"], sp["tp"], sp["tp"], sp["sink"],
                  sp["row512"], sp["lse"], sp["row512"]],
        out_specs=[sp["row512"], sp["row128"], sp["row128"], sp["row128"], sp["row128"],
                   pl.BlockSpec((1, A_Q_HEADS), lambda n: (0, 0))],
        out_shape=[jax.ShapeDtypeStruct((T, HYB_PAD), ACT_DTYPE)] + [jax.ShapeDtypeStruct((T, LANES), F32)] * 4
        + [jax.ShapeDtypeStruct((1, A_Q_HEADS), F32)],
        compiler_params=_cparams("arbitrary"),
    )(proj, proj, proj, proj, proj, cos, sin_s, cos, sin_s, sinks_b, o, lse, dmix)


def _shift_down(x, prev8, k):
    if k == 0:
        return x
    R, W = x.shape
    r = pltpu.roll(jnp.concatenate([prev8, x], axis=0).reshape(R // SUBLANES + 1, SUBLANES, W), k, axis=1)
    sub = lax.broadcasted_iota(jnp.int32, (R // SUBLANES, SUBLANES, W), 1)
    return jnp.where(sub < k, r[:-1], r[1:]).reshape(R, W)


def _shift_up(x, next8, k):
    if k == 0:
        return x
    R, W = x.shape
    r = pltpu.roll(jnp.concatenate([x, next8], axis=0).reshape(R // SUBLANES + 1, SUBLANES, W), SUBLANES - k, axis=1)
    sub = lax.broadcasted_iota(jnp.int32, (R // SUBLANES, SUBLANES, W), 1)
    return jnp.where(sub >= SUBLANES - k, r[1:], r[:-1]).reshape(R, W)


def _conv(x, prev8, w):
    w0, w1, w2, w3 = (w[j:j + 1] for j in range(CONV_K))
    x1 = _shift_down(x, prev8, 1)
    v = x * w1 + x1 * w0
    v_prev8 = prev8 * w1 + pltpu.roll(prev8, 1, axis=0) * w0
    return x * w3 + x1 * w2 + _shift_down(v, v_prev8, 2)


def _conv_bwd(x, w, dy, next8_dy):
    dx = dy * w[CONV_K - 1:CONV_K]
    dws = []
    for j in range(CONV_K - 1):
        up = _shift_up(dy, next8_dy, CONV_K - 1 - j)
        dx = dx + up * w[j:j + 1]
        dws.append(jnp.sum(up * x, axis=0, keepdims=True))
    dws.append(jnp.sum(dy * x, axis=0, keepdims=True))
    return dx, dws


DK_SCALE = B_HEAD_DIM ** -0.5


BNN = (((2,), (1,)), ((0,), (0,)))
BNT = (((2,), (2,)), ((0,), (0,)))
BTN = (((1,), (1,)), ((0,), (0,)))


def _tri_inv(a):
    C = a.shape[-1]
    ri = lax.broadcasted_iota(jnp.int32, (C, C), 0)
    ci = lax.broadcasted_iota(jnp.int32, (C, C), 1)
    x = jnp.where(ri == ci, 1.0, 0.0)[None] - a
    p = _dotf(a, a, BNN)
    span = 2
    while span < C:
        dot = _dotf if span <= 4 else _dot
        x = x + dot(x, p, BNN)
        span *= 2
        if span < C:
            p = dot(p, p, BNN)
    return x


def _dn_chunk(qc, kc, v, gcol, grow, bcol, s0, tm=None):
    C = B_CHUNK
    ri = lax.broadcasted_iota(jnp.int32, (C, C), 0)
    ci = lax.broadcasted_iota(jnp.int32, (C, C), 1)
    incl, strict = (ri >= ci)[None], (ri > ci)[None]
    rq = lax.rsqrt(jnp.sum(qc * qc, axis=-1, keepdims=True) + NORM_EPS)
    rk = lax.rsqrt(jnp.sum(kc * kc, axis=-1, keepdims=True) + NORM_EPS)
    qn = qc * rq
    q = qn * DK_SCALE
    k = kc * rk
    gc_col = jnp.sum(jnp.where(incl, grow, 0.0), axis=2, keepdims=True)
    gc_row = jnp.sum(jnp.where((ri <= ci)[None], gcol, 0.0), axis=1, keepdims=True)
    gl = jnp.sum(gcol, axis=1, keepdims=True)
    dincl = jnp.where(incl, jnp.exp(jnp.where(incl, gc_col - gc_row, 0.0)), 0.0)
    dstrict = jnp.where(strict, dincl, 0.0)
    eg = jnp.exp(gc_col)
    ekt = jnp.exp(gl - gc_col)
    egl = jnp.exp(gl)
    kb = k * bcol
    vb = v * bcol
    kbg = kb * eg
    a = _dot(kb, k, BNT) * dstrict
    if tm is None:
        tm = _tri_inv(a)
    u = _dot(tm, vb, BNN)
    w = _dot(tm, kbg, BNN)
    vn = u - _dot(w, s0, BNN)
    qk = _dot(q, k, BNT) * dincl
    qg = q * eg
    kt = k * ekt
    o = _dot(qg, s0, BNN) + _dot(qk, vn, BNN)
    s1 = s0 * egl + _dot(kt, vn, BTN)
    return dict(rq=rq, rk=rk, qn=qn, q=q, k=k, dincl=dincl, dstrict=dstrict, eg=eg, ekt=ekt, egl=egl, kb=kb, vb=vb,
                kbg=kbg, a=a, tm=tm, w=w, vn=vn, qk=qk, qg=qg, kt=kt, o=o, s1=s1, ri=ri[None], ci=ci[None])


def _heads(ref):
    return jnp.stack([ref[:, h * B_HEAD_DIM:(h + 1) * B_HEAD_DIM] for h in range(B_HEADS)])


def _store_heads(ref, val):
    for h in range(B_HEADS):
        ref[:, h * B_HEAD_DIM:(h + 1) * B_HEAD_DIM] = val[h]


def _dn_specs(N, rev):
    ix = (lambda n: N - 1 - n) if rev else (lambda n: n)
    wide = lambda cb: pl.BlockSpec((B_CHUNK, B_W), lambda n: (ix(n), cb))
    return dict(
        q=wide(0), k=wide(1), v=wide(2), z=wide(HP_Z // B_W), dob=wide(A_Q_W // B_W), out=wide(0),
        nw=pl.BlockSpec((1, LANES), lambda n: (0, 0)),
        row=pl.BlockSpec((B_HEADS, None, 1, B_CHUNK), lambda n: (0, ix(n), 0, 0)),
        state=pl.BlockSpec((B_HEADS, None, B_HEAD_DIM, B_HEAD_DIM), lambda n: (0, ix(n), 0, 0)),
        inv=pl.BlockSpec((B_HEADS, None, B_CHUNK, B_CHUNK), lambda n: (0, ix(n), 0, 0)),
    )


def _to_col(row):
    C = row.shape[-1]
    eye = lax.broadcasted_iota(jnp.int32, (C, C), 0) == lax.broadcasted_iota(jnp.int32, (C, C), 1)
    return jnp.sum(jnp.where(eye[None], row, 0.0), axis=2, keepdims=True)


def _to_row(col):
    C = col.shape[1]
    eye = lax.broadcasted_iota(jnp.int32, (C, C), 0) == lax.broadcasted_iota(jnp.int32, (C, C), 1)
    return jnp.sum(jnp.where(eye[None], col, 0.0), axis=1, keepdims=True)


def _dn_fwd(proj, conv_w, norm_w, grow, brow, name):
    T = proj.shape[0]
    N = T // B_CHUNK
    sp = _dn_specs(N, False)
    cb0 = HP_QKVB // B_W

    def body(xq_ref, xk_ref, xv_ref, cw_ref, z_ref, nw_ref, gr_ref, br_ref, o_ref, st_ref, tm_ref, c_ref,
             s_ref, prev_ref):
        n = pl.program_id(0)

        @pl.when(n == 0)
        def _():
            s_ref[...] = jnp.zeros_like(s_ref)
            prev_ref[...] = jnp.zeros_like(prev_ref)

        qkv = []
        for part, x_ref in enumerate((xq_ref, xk_ref, xv_ref)):
            cols = slice(part * B_W, (part + 1) * B_W)
            x = x_ref[...]
            xc = _silu(_conv(x, prev_ref[:, cols], cw_ref[:, cols]))
            prev_ref[:, cols] = x[B_CHUNK - SUBLANES:]
            c_ref[:, cols] = xc
            qkv.append(jnp.stack([xc[:, h * B_HEAD_DIM:(h + 1) * B_HEAD_DIM] for h in range(B_HEADS)]))
        s0 = s_ref[...]
        st_ref[...] = s0
        grow_v = gr_ref[...]
        f = _dn_chunk(qkv[0], qkv[1], qkv[2], _to_col(grow_v), grow_v, _to_col(br_ref[...]), s0)
        o = f["o"]
        r = lax.rsqrt(jnp.mean(o * o, axis=-1, keepdims=True) + NORM_EPS)
        _store_heads(o_ref, o * r * nw_ref[...][None] * _silu(_heads(z_ref)))
        s_ref[...] = f["s1"]
        tm_ref[...] = f["tm"]

    wide = lambda cb: pl.BlockSpec((B_CHUNK, B_W), lambda n: (n, cb))
    return pl.pallas_call(
        body, name=name, grid=(N,),
        in_specs=[wide(cb0), wide(cb0 + 1), wide(cb0 + 2), pl.BlockSpec((CONV_K, 3 * B_W), lambda n: (0, 0)),
                  sp["z"], sp["nw"], sp["row"], sp["row"]],
        out_specs=[sp["out"], sp["state"], sp["inv"], pl.BlockSpec((B_CHUNK, 3 * B_W), lambda n: (n, 0))],
        out_shape=[jax.ShapeDtypeStruct((T, B_W), F32),
                   jax.ShapeDtypeStruct((B_HEADS, N, B_HEAD_DIM, B_HEAD_DIM), F32),
                   jax.ShapeDtypeStruct((B_HEADS, N, B_CHUNK, B_CHUNK), F32),
                   jax.ShapeDtypeStruct((T, 3 * B_W), F32)],
        scratch_shapes=[pltpu.VMEM((B_HEADS, B_HEAD_DIM, B_HEAD_DIM), F32), pltpu.VMEM((SUBLANES, 3 * B_W), F32)],
        compiler_params=_cparams("arbitrary"),
    )(proj, proj, proj, conv_w, proj, norm_w, grow, brow)


def _dn_bwd(qkvc, proj, conv_w, norm_w, grow, brow, states, invs, dmix, dproj, name):
    T = qkvc.shape[0]
    N = T // B_CHUNK
    sp = _dn_specs(N, True)
    C = B_CHUNK
    cb0 = HP_QKVB // B_W

    def body(q_ref, k_ref, v_ref, z_ref, nw_ref, gr_ref, br_ref, st_ref, tm_ref, dob_ref, dproj_in,
             xq_ref, xk_ref, xv_ref, pq_ref, pk_ref, pv_ref, cw_ref,
             dp_ref, dg_ref, db_ref, dnw_ref, dcw_ref, ds_ref, nxt_ref):
        n = pl.program_id(0)
        dp_ref[...] = dproj_in[...]
        dz_ref = dp_ref.at[:, HP_Z:HP_Z + B_W]
        d_qkv = {}

        @pl.when(n == 0)
        def _():
            ds_ref[...] = jnp.zeros_like(ds_ref)
            dnw_ref[...] = jnp.zeros_like(dnw_ref)
            dcw_ref[...] = jnp.zeros_like(dcw_ref)
            nxt_ref[...] = jnp.zeros_like(nxt_ref)

        s0 = st_ref[...]
        ds1 = ds_ref[...]
        v, z, nw, bcol_v = _heads(v_ref), _heads(z_ref), nw_ref[...][None], _to_col(br_ref[...])
        grow_v = gr_ref[...]
        f = _dn_chunk(_heads(q_ref), _heads(k_ref), v, _to_col(grow_v), grow_v, bcol_v, s0, tm=tm_ref[...])
        o, q, k, qn = f["o"], f["q"], f["k"], f["qn"]
        eg, ekt, egl = f["eg"], f["ekt"], f["egl"]
        tm, w, vn, kb, vb, kbg = f["tm"], f["w"], f["vn"], f["kb"], f["vb"], f["kbg"]
        qg, kt, qk, a = f["qg"], f["kt"], f["qk"], f["a"]
        ri, ci = f["ri"], f["ci"]

        dob_v = _heads(dob_ref)
        r = lax.rsqrt(jnp.mean(o * o, axis=-1, keepdims=True) + NORM_EPS)
        sz = _silu(z)
        on = o * r
        dnw_ref[...] += jnp.sum(dob_v * sz * on, axis=1, keepdims=True)
        _store_heads(dz_ref, (dob_v * on * nw * _dsilu(z)).astype(ACT_DTYPE))
        d_on = dob_v * sz * nw
        do = r * (d_on - on * jnp.mean(d_on * on, axis=-1, keepdims=True))

        dvn = _dot(qk, do, BTN) + _dot(kt, ds1, BNN)
        dqk = _dot(do, vn, BNT)
        dqg = _dot(do, s0, BNT)
        ds_ref[...] = _dot(qg, do, BTN) + egl * ds1 - _dot(w, dvn, BTN)
        dgl = jnp.sum(s0 * ds1, axis=(1, 2), keepdims=True) * egl
        dkt = _dot(vn, ds1, BNT)
        dw = -_dot(dvn, s0, BNT)
        dq = dqg * eg
        dgc = jnp.sum(dqg * qg, axis=-1, keepdims=True)
        dk = dkt * ekt
        t_kt = jnp.sum(dkt * kt, axis=-1, keepdims=True)
        dgl = dgl + jnp.sum(t_kt, axis=1, keepdims=True)
        dgc = dgc - t_kt
        dqkr = dqk * f["dincl"]
        dq = dq + _dot(dqkr, k, BNN)
        dk = dk + _dot(dqkr, q, BTN)
        e_qk = dqk * qk
        dgc = dgc + jnp.sum(e_qk, axis=-1, keepdims=True)
        dgc_row = -jnp.sum(e_qk, axis=1, keepdims=True)
        dtm = _dot(dvn, vb, BNT) + _dot(dw, kbg, BNT)
        dvb = _dot(tm, dvn, BTN)
        dkbg = _dot(tm, dw, BTN)
        dkb = dkbg * eg
        dgc = dgc + jnp.sum(dkbg * kbg, axis=-1, keepdims=True)
        da = -_dotf(tm, _dotf(dtm, tm, BNT), BTN)
        dkk = da * f["dstrict"]
        e_a = da * a
        dgc = dgc + jnp.sum(e_a, axis=-1, keepdims=True)
        dgc_row = dgc_row - jnp.sum(e_a, axis=1, keepdims=True)
        dkb = dkb + _dot(dkk, k, BNN)
        dk = dk + _dot(dkk, kb, BTN)
        dk = dk + dkb * bcol_v
        db_ref[...] = _to_row(jnp.sum(dkb * k, axis=-1, keepdims=True) + jnp.sum(dvb * v, axis=-1, keepdims=True))
        d_qkv[2] = dvb * bcol_v
        dgc_row = dgc_row + jnp.sum(jnp.where(ri == ci, dgc, 0.0), axis=1, keepdims=True)
        dg_ref[...] = jnp.sum(jnp.where(ci <= ri, _to_col(dgc_row), 0.0), axis=1, keepdims=True) + dgl
        dqs = dq * DK_SCALE
        d_qkv[0] = f["rq"] * (dqs - qn * jnp.sum(dqs * qn, axis=-1, keepdims=True))
        d_qkv[1] = f["rk"] * (dk - k * jnp.sum(dk * k, axis=-1, keepdims=True))
        first = n < N - 1
        for part, (x_ref, p_ref) in enumerate(((xq_ref, pq_ref), (xk_ref, pk_ref), (xv_ref, pv_ref))):
            cols = slice(part * B_W, (part + 1) * B_W)
            dy = jnp.concatenate([d_qkv[part][h] for h in range(B_HEADS)], axis=1)
            x, w = x_ref[...], cw_ref[:, cols]
            dpre = dy * _dsilu(_conv(x, jnp.where(first, p_ref[...], 0.0), w))
            dx, dws = _conv_bwd(x, w, dpre, nxt_ref[:, cols])
            nxt_ref[:, cols] = dpre[:SUBLANES]
            dp_ref[:, HP_QKVB + part * B_W:HP_QKVB + (part + 1) * B_W] = dx.astype(ACT_DTYPE)
            for j in range(CONV_K):
                dcw_ref[j:j + 1, cols] += dws[j]

    rev = lambda n: N - 1 - n
    part_x = lambda p: pl.BlockSpec((C, B_W), lambda n: (rev(n), cb0 + p))
    part_prev = lambda p: pl.BlockSpec((SUBLANES, B_W), lambda n: (jnp.maximum(rev(n) * (C // SUBLANES) - 1, 0), cb0 + p))
    whole_row = pl.BlockSpec((C, HYB_PAD), lambda n: (rev(n), 0))
    return pl.pallas_call(
        body, name=name, grid=(N,),
        in_specs=[sp["q"], sp["k"], sp["v"], sp["z"], sp["nw"], sp["row"], sp["row"], sp["state"], sp["inv"], sp["dob"],
                  whole_row, part_x(0), part_x(1), part_x(2), part_prev(0), part_prev(1), part_prev(2),
                  pl.BlockSpec((CONV_K, 3 * B_W), lambda n: (0, 0))],
        out_specs=[whole_row, sp["row"], sp["row"], pl.BlockSpec((B_HEADS, 1, LANES), lambda n: (0, 0, 0)),
                   pl.BlockSpec((CONV_K, 3 * B_W), lambda n: (0, 0))],
        out_shape=[jax.ShapeDtypeStruct(dproj.shape, dproj.dtype),
                   jax.ShapeDtypeStruct((B_HEADS, N, 1, C), F32), jax.ShapeDtypeStruct((B_HEADS, N, 1, C), F32),
                   jax.ShapeDtypeStruct((B_HEADS, 1, LANES), F32), jax.ShapeDtypeStruct((CONV_K, 3 * B_W), F32)],
        input_output_aliases={10: 0},
        scratch_shapes=[pltpu.VMEM((B_HEADS, B_HEAD_DIM, B_HEAD_DIM), F32), pltpu.VMEM((SUBLANES, 3 * B_W), F32)],
        compiler_params=_cparams("arbitrary"),
    )(qkvc, qkvc, qkvc, proj, norm_w, grow, brow, states, invs, dmix, dproj, proj, proj, proj, proj, proj, proj, conv_w)


def _lru_gates(xc, wa_ref, wx_ref, ba, bx, sp):
    pre_r, pre_i = [], []
    for hb in range(LRU_BLOCKS):
        xb = xc[:, hb * LRU_BLOCK_W:(hb + 1) * LRU_BLOCK_W]
        pre_r.append(_dot(xb, wa_ref[hb]))
        pre_i.append(_dot(xb, wx_ref[hb]))
    r = _sigmoid(jnp.concatenate(pre_r, axis=1) + ba)
    i = _sigmoid(jnp.concatenate(pre_i, axis=1) + bx)
    la = -LRU_C * r * sp
    a = jnp.exp(la)
    th = jnp.tanh(la)
    s = jnp.sqrt(-2.0 * th / (1.0 - th))
    return r, i, a, s


def _scan_down(a, b, h_in):
    R, W = a.shape
    a = a.reshape(R // SUBLANES, SUBLANES, W)
    b = b.reshape(R // SUBLANES, SUBLANES, W)
    sub = lax.broadcasted_iota(jnp.int32, a.shape, 1)
    d = 1
    while d < SUBLANES:
        ok = sub >= d
        b = a * jnp.where(ok, pltpu.roll(b, d, axis=1), 0.0) + b
        a = a * jnp.where(ok, pltpu.roll(a, d, axis=1), 1.0)
        d *= 2
    out, last = [], h_in
    for g in range(R // SUBLANES):
        h = b[g] + a[g] * last
        out.append(h)
        last = h[SUBLANES - 1:SUBLANES]
    return jnp.concatenate(out, axis=0)


def _scan_up(a, b, l_in):
    R, W = a.shape
    a = a.reshape(R // SUBLANES, SUBLANES, W)
    b = b.reshape(R // SUBLANES, SUBLANES, W)
    sub = lax.broadcasted_iota(jnp.int32, a.shape, 1)
    d = 1
    while d < SUBLANES:
        ok = sub < SUBLANES - d
        b = a * jnp.where(ok, pltpu.roll(b, SUBLANES - d, axis=1), 0.0) + b
        a = a * jnp.where(ok, pltpu.roll(a, SUBLANES - d, axis=1), 1.0)
        d *= 2
    out, nxt = [], l_in
    for g in range(R // SUBLANES - 1, -1, -1):
        lam = b[g] + a[g] * nxt
        out.append(lam)
        nxt = lam[0:1]
    return jnp.concatenate(out[::-1], axis=0)


def _rglru_fwd(proj, conv_w, conv_b, wa, wx, ba, bx, sp, name):
    T = proj.shape[0]
    R = _tile(T, 256)
    W = D_MODEL

    def body(p_ref, cw_ref, cb_ref, wa_ref, wx_ref, ba_ref, bx_ref, sp_ref, hg_ref, h_ref, prev_ref, hc_ref):
        i = pl.program_id(0)

        @pl.when(i == 0)
        def _():
            prev_ref[...] = jnp.zeros_like(prev_ref)
            hc_ref[...] = jnp.zeros_like(hc_ref)

        xr = p_ref[:, :W]
        gate = p_ref[:, W:]
        xc = _conv(xr, prev_ref[...], cw_ref[...]) + cb_ref[...]
        prev_ref[...] = xr[R - SUBLANES:]
        r, ig, a, s = _lru_gates(xc, wa_ref, wx_ref, ba_ref[...], bx_ref[...], sp_ref[...])
        h = _scan_down(a, s * ig * xc, hc_ref[SUBLANES - 1:SUBLANES, :])
        h_ref[...] = h
        hg_ref[...] = (h * _gelu(gate)).astype(ACT_DTYPE)
        hc_ref[...] = h[R - SUBLANES:]

    vec = pl.BlockSpec((1, W), lambda i: (0, 0))
    wsp = pl.BlockSpec((LRU_BLOCKS, LRU_BLOCK_W, LRU_BLOCK_W), lambda i: (0, 0, 0))
    row = pl.BlockSpec((R, W), lambda i: (i, 0))
    return pl.pallas_call(
        body, name=name, grid=(T // R,),
        in_specs=[pl.BlockSpec((R, 2 * W), lambda i: (i, 0)), pl.BlockSpec((CONV_K, W), lambda i: (0, 0)),
                  vec, wsp, wsp, vec, vec, vec],
        out_specs=[row, row],
        out_shape=[jax.ShapeDtypeStruct((T, W), ACT_DTYPE), jax.ShapeDtypeStruct((T, W), F32)],
        scratch_shapes=[pltpu.VMEM((SUBLANES, W), F32), pltpu.VMEM((SUBLANES, W), F32)],
        compiler_params=_cparams("arbitrary"),
    )(proj, conv_w, conv_b, wa, wx, ba, bx, sp)


def _rglru_bwd(proj, conv_w, conv_b, wa, wx, ba, bx, sp, h, dhg, name):
    T = proj.shape[0]
    R = _tile(T, 256)
    nb = T // R
    r8 = R // SUBLANES
    W = D_MODEL

    def body(p_ref, pp_ref, cw_ref, cb_ref, wa_ref, wx_ref, ba_ref, bx_ref, sp_ref, h_ref, hp_ref, dhg_ref,
             dp_ref, dcw_ref, dcb_ref, dwa_ref, dwx_ref, dba_ref, dbx_ref, dsp_ref, lam_ref, nxt_ref):
        step = pl.program_id(0)
        blk = nb - 1 - step

        @pl.when(step == 0)
        def _():
            lam_ref[...] = jnp.zeros_like(lam_ref)
            nxt_ref[...] = jnp.zeros_like(nxt_ref)

        xr = p_ref[:, :W]
        gate = p_ref[:, W:]
        first = blk > 0
        prev8 = jnp.where(first, pp_ref[:, :W], 0.0)
        hprev8 = jnp.where(first, hp_ref[...], 0.0)
        cw = cw_ref[...]
        spv = sp_ref[...]
        xc = _conv(xr, prev8, cw) + cb_ref[...]
        r, ig, a, s = _lru_gates(xc, wa_ref, wx_ref, ba_ref[...], bx_ref[...], spv)
        hv = h_ref[...]
        dhg_v = dhg_ref[...]
        dgate = dhg_v * hv * _dgelu(gate)
        dh = dhg_v * _gelu(gate)
        row = lax.broadcasted_iota(jnp.int32, (R, W), 0)
        last = row == R - 1
        a_up = jnp.where(last, 0.0, pltpu.roll(a, R - 1, axis=0))
        lam = _scan_up(a_up, dh + jnp.where(last, lam_ref[0:1, :], 0.0), jnp.zeros((1, W), F32))
        lam_ref[...] = (a * lam)[:SUBLANES]
        h_dn = _shift_down(hv, hprev8, 1)
        da = lam * h_dn
        bx_in = ig * xc
        dsv = lam * bx_in
        dig = lam * s * xc
        dxc = lam * s * ig
        dla = da * a - dsv * (a * a) / s
        dr = dla * (-LRU_C) * spv
        dsp = jnp.sum(dla * (-LRU_C) * r, axis=0, keepdims=True)
        dpr = dr * r * (1.0 - r)
        dpi = dig * ig * (1.0 - ig)
        dxc_parts, dwa_parts, dwx_parts = [], [], []
        for hb in range(LRU_BLOCKS):
            sl = slice(hb * LRU_BLOCK_W, (hb + 1) * LRU_BLOCK_W)
            xb, gr, gi = xc[:, sl], dpr[:, sl], dpi[:, sl]
            dxc_parts.append(_dot(gr, wa_ref[hb], NT) + _dot(gi, wx_ref[hb], NT))
            dwa_parts.append(_dot(xb, gr, TN))
            dwx_parts.append(_dot(xb, gi, TN))
        dxc = dxc + jnp.concatenate(dxc_parts, axis=1)
        dxr, dcw = _conv_bwd(xr, cw, dxc, nxt_ref[...])
        nxt_ref[...] = dxc[:SUBLANES]
        dp_ref[:, :W] = dxr.astype(ACT_DTYPE)
        dp_ref[:, W:] = dgate.astype(ACT_DTYPE)
        dcb = jnp.sum(dxc, axis=0, keepdims=True)
        dba = jnp.sum(dpr, axis=0, keepdims=True)
        dbx = jnp.sum(dpi, axis=0, keepdims=True)

        @pl.when(step == 0)
        def _():
            for j in range(CONV_K):
                dcw_ref[j:j + 1, :] = dcw[j]
            dcb_ref[...] = dcb
            dba_ref[...] = dba
            dbx_ref[...] = dbx
            dsp_ref[...] = dsp
            for hb in range(LRU_BLOCKS):
                dwa_ref[hb] = dwa_parts[hb]
                dwx_ref[hb] = dwx_parts[hb]

        @pl.when(step > 0)
        def _():
            for j in range(CONV_K):
                dcw_ref[j:j + 1, :] += dcw[j]
            dcb_ref[...] += dcb
            dba_ref[...] += dba
            dbx_ref[...] += dbx
            dsp_ref[...] += dsp
            for hb in range(LRU_BLOCKS):
                dwa_ref[hb] += dwa_parts[hb]
                dwx_ref[hb] += dwx_parts[hb]

    rv = lambda i: nb - 1 - i
    before = lambda i: jnp.maximum((nb - 1 - i) * r8 - 1, 0)
    vec = pl.BlockSpec((1, W), lambda i: (0, 0))
    cws = pl.BlockSpec((CONV_K, W), lambda i: (0, 0))
    wsp = pl.BlockSpec((LRU_BLOCKS, LRU_BLOCK_W, LRU_BLOCK_W), lambda i: (0, 0, 0))
    row = pl.BlockSpec((R, W), lambda i: (rv(i), 0))
    wshape = jax.ShapeDtypeStruct((LRU_BLOCKS, LRU_BLOCK_W, LRU_BLOCK_W), F32)
    vshape = jax.ShapeDtypeStruct((1, W), F32)
    return pl.pallas_call(
        body, name=name, grid=(nb,),
        in_specs=[pl.BlockSpec((R, 2 * W), lambda i: (rv(i), 0)), pl.BlockSpec((SUBLANES, 2 * W), lambda i: (before(i), 0)),
                  cws, vec, wsp, wsp, vec, vec, vec, row, pl.BlockSpec((SUBLANES, W), lambda i: (before(i), 0)), row],
        out_specs=[pl.BlockSpec((R, 2 * W), lambda i: (rv(i), 0)), cws, vec, wsp, wsp, vec, vec, vec],
        out_shape=[jax.ShapeDtypeStruct((T, 2 * W), ACT_DTYPE), jax.ShapeDtypeStruct((CONV_K, W), F32), vshape,
                   wshape, wshape, vshape, vshape, vshape],
        scratch_shapes=[pltpu.VMEM((SUBLANES, W), F32), pltpu.VMEM((SUBLANES, W), F32)],
        compiler_params=_cparams("arbitrary"),
    )(proj, proj, conv_w, conv_b, wa, wx, ba, bx, sp, h, h, dhg)


MESH = pl.DeviceIdType.MESH
ANY = pl.BlockSpec(memory_space=pl.ANY)


def _position():
    x, y, c = lax.axis_index("x"), lax.axis_index("y"), lax.axis_index("c")
    other_chips = [(1 - x, y), (x, 1 - y), (1 - x, 1 - y)]
    return x, y, c, other_chips


def _all_gather_weights(shards, name):
    n = len(shards)

    def body(*refs):
        ins, outs = refs[:n], refs[n:2 * n]
        send_sems, recv_sems = refs[2 * n:]
        x, y, c, chips = _position()
        me = 2 * x + y
        sibling = (x, y, 1 - c)

        def rcopy(t, k, src, dst, to):
            return pltpu.make_async_remote_copy(src_ref=src, dst_ref=dst, send_sem=send_sems.at[t, k],
                                                recv_sem=recv_sems.at[t, k], device_id=to, device_id_type=MESH)

        started = []
        for t in range(n):
            for j, (cx, cy) in enumerate(chips):
                cp = rcopy(t, j, ins[t].at[c], outs[t].at[me, c], (cx, cy, c))
                cp.start()
                started.append(cp)
        for t in range(n):
            for j, (cx, cy) in enumerate(chips):
                blk = outs[t].at[2 * cx + cy, c]
                rcopy(t, j, blk, blk, (cx, cy, c)).wait_recv()
                cp = rcopy(t, 3 + j, blk, blk, sibling)
                cp.start()
                started.append(cp)
        for t in range(n):
            for j, (cx, cy) in enumerate(chips):
                blk = outs[t].at[2 * cx + cy, 1 - c]
                rcopy(t, 3 + j, blk, blk, sibling).wait_recv()
        for cp in started:
            cp.wait_send()

    return pl.pallas_call(
        body, name=name, in_specs=[ANY] * n, out_specs=[ANY] * n,
        out_shape=[jax.ShapeDtypeStruct((N_CHIPS,) + s.shape, s.dtype) for s in shards],
        scratch_shapes=[pltpu.SemaphoreType.DMA((n, 6)), pltpu.SemaphoreType.DMA((n, 6))],
    )(*shards)


HBM = pl.BlockSpec(memory_space=pltpu.HBM)
SEM = pl.BlockSpec(memory_space=pltpu.SEMAPHORE)
EFFECT = pltpu.SideEffectType.DATAFLOW_SIDE_EFFECTING


def _gather_start(own, land, after, name):
    n = len(own)

    def body(*refs):
        own_refs, land_refs = refs[:n], refs[n:2 * n]
        send_sems = refs[2 * n + 1:3 * n + 1]
        recv_sems = refs[3 * n + 1:4 * n + 1]
        x, y, c, chips = _position()
        me = 2 * x + y
        for t in range(n):
            for cx, cy in chips:
                pltpu.make_async_remote_copy(
                    src_ref=own_refs[t], dst_ref=land_refs[t].at[me], send_sem=send_sems[t], recv_sem=recv_sems[t],
                    device_id=(cx, cy, c), device_id_type=MESH).start()

    sems = (pltpu.SemaphoreType.DMA(()),) * (2 * n)
    thru = [pltpu.HBM(a.shape, a.dtype) for a in list(own) + list(land)]
    out = pl.pallas_call(
        body, name=name, out_shape=(*sems, *thru),
        in_specs=[HBM] * (2 * n) + [pl.BlockSpec(memory_space=pl.ANY)], out_specs=(SEM,) * (2 * n) + (HBM,) * (2 * n),
        input_output_aliases={i: 2 * n + i for i in range(2 * n)},
        compiler_params=pltpu.CompilerParams(has_side_effects=EFFECT),
    )(*[pltpu.with_memory_space_constraint(a, pltpu.HBM) for a in list(own) + list(land)], after)
    return list(out[:n]), list(out[n:2 * n]), list(out[2 * n:3 * n]), list(out[3 * n:])


def _gather_wait(send_sems, recv_sems, own, land, after, name):
    n = len(own)

    def body(*refs):
        land_refs = refs[n:2 * n]
        s_sems, r_sems = refs[2 * n:3 * n], refs[3 * n:4 * n]
        x, y, c, _ = _position()
        for t in range(n):
            three = land_refs[t].at[pl.ds(0, N_CHIPS - 1)]
            cp = pltpu.make_async_remote_copy(src_ref=three, dst_ref=three, send_sem=s_sems[t], recv_sem=r_sems[t],
                                              device_id=(x, y, c), device_id_type=MESH)
            cp.wait_send()
            cp.wait_recv()

    thru = [pltpu.HBM(a.shape, a.dtype) for a in list(own) + list(land)]
    out = pl.pallas_call(
        body, name=name, out_shape=tuple(thru),
        in_specs=[HBM] * (2 * n) + [SEM] * (2 * n) + [pl.BlockSpec(memory_space=pl.ANY)], out_specs=(HBM,) * (2 * n),
        input_output_aliases={i: i for i in range(2 * n)},
        compiler_params=pltpu.CompilerParams(has_side_effects=EFFECT),
    )(*own, *land, *send_sems, *recv_sems, after)
    return list(out[n:])


def _rs_to_sibling(grads, name):
    n = len(grads)

    def body(*refs):
        ins, outs = refs[:n], refs[n:2 * n]
        send_sems, recv_sems = refs[2 * n:]
        x, y, c, _ = _position()
        cps = [pltpu.make_async_remote_copy(src_ref=ins[t].at[:, 1 - c], dst_ref=outs[t], send_sem=send_sems.at[t],
                                            recv_sem=recv_sems.at[t], device_id=(x, y, 1 - c), device_id_type=MESH)
               for t in range(n)]
        for cp in cps:
            cp.start()
        for cp in cps:
            cp.wait()

    return pl.pallas_call(
        body, name=name, in_specs=[ANY] * n, out_specs=[ANY] * n,
        out_shape=[jax.ShapeDtypeStruct((N_CHIPS,) + g.shape[2:], g.dtype) for g in grads],
        scratch_shapes=[pltpu.SemaphoreType.DMA((n,)), pltpu.SemaphoreType.DMA((n,))],
    )(*grads)


def _rs_sibling_start(grads, recv, name):
    n = len(grads)

    def body(*refs):
        grad_refs, recv_refs = refs[:n], refs[n:2 * n]
        send_sems, recv_sems = refs[2 * n:3 * n], refs[3 * n:4 * n]
        token_ref = refs[-1]
        x, y, c, _ = _position()
        for t in range(n):
            pltpu.make_async_remote_copy(src_ref=grad_refs[t].at[:, 1 - c], dst_ref=recv_refs[t], send_sem=send_sems[t],
                                         recv_sem=recv_sems[t], device_id=(x, y, 1 - c), device_id_type=MESH).start()
        token_ref[...] = jnp.zeros_like(token_ref)

    sems = (pltpu.SemaphoreType.DMA(()),) * (2 * n)
    thru = [pltpu.HBM(a.shape, a.dtype) for a in list(grads) + list(recv)]
    out = pl.pallas_call(
        body, name=name, out_shape=(*sems, *thru, jax.ShapeDtypeStruct((SUBLANES, LANES), F32)),
        in_specs=[HBM] * (2 * n), out_specs=(SEM,) * (2 * n) + (HBM,) * (2 * n) + (pl.BlockSpec(memory_space=pltpu.VMEM),),
        input_output_aliases={i: 2 * n + i for i in range(2 * n)},
        compiler_params=pltpu.CompilerParams(has_side_effects=EFFECT),
    )(*[pltpu.with_memory_space_constraint(a, pltpu.HBM) for a in list(grads) + list(recv)])
    return out[:n], out[n:2 * n], list(out[2 * n:3 * n]), list(out[3 * n:4 * n]), out[-1]


def _rs_sibling_wait(send_sems, recv_sems, grads, recv, after, name):
    n = len(grads)

    def body(*refs):
        recv_refs = refs[n:2 * n]
        s_sems, r_sems = refs[2 * n:3 * n], refs[3 * n:4 * n]
        x, y, c, _ = _position()
        for t in range(n):
            cp = pltpu.make_async_remote_copy(src_ref=recv_refs[t], dst_ref=recv_refs[t], send_sem=s_sems[t],
                                              recv_sem=r_sems[t], device_id=(x, y, c), device_id_type=MESH)
            cp.wait_send()
            cp.wait_recv()

    thru = [pltpu.HBM(a.shape, a.dtype) for a in list(grads) + list(recv)]
    out = pl.pallas_call(
        body, name=name, out_shape=tuple(thru),
        in_specs=[HBM] * (2 * n) + [SEM] * (2 * n) + [pl.BlockSpec(memory_space=pl.ANY)], out_specs=(HBM,) * (2 * n),
        input_output_aliases={i: i for i in range(2 * n)},
        compiler_params=pltpu.CompilerParams(has_side_effects=EFFECT),
    )(*grads, *recv, *send_sems, *recv_sems, after)
    return list(out[:n]), list(out[n:])


def _rs_across_start(parts, recv, name):
    n = len(parts)

    def body(*refs):
        part_refs, recv_refs = refs[:n], refs[n:2 * n]
        send_sems, recv_sems = refs[2 * n:3 * n], refs[3 * n:4 * n]
        token_ref = refs[-1]
        x, y, c, chips = _position()
        me = 2 * x + y
        for t in range(n):
            for cx, cy in chips:
                pltpu.make_async_remote_copy(src_ref=part_refs[t].at[2 * cx + cy], dst_ref=recv_refs[t].at[me],
                                             send_sem=send_sems[t], recv_sem=recv_sems[t], device_id=(cx, cy, c),
                                             device_id_type=MESH).start()
        token_ref[...] = jnp.zeros_like(token_ref)

    sems = (pltpu.SemaphoreType.DMA(()),) * (2 * n)
    thru = [pltpu.HBM(a.shape, a.dtype) for a in list(parts) + list(recv)]
    out = pl.pallas_call(
        body, name=name, out_shape=(*sems, *thru, jax.ShapeDtypeStruct((SUBLANES, LANES), F32)),
        in_specs=[HBM] * (2 * n), out_specs=(SEM,) * (2 * n) + (HBM,) * (2 * n) + (pl.BlockSpec(memory_space=pltpu.VMEM),),
        input_output_aliases={i: 2 * n + i for i in range(2 * n)},
        compiler_params=pltpu.CompilerParams(has_side_effects=EFFECT),
    )(*[pltpu.with_memory_space_constraint(a, pltpu.HBM) for a in list(parts) + list(recv)])
    return out[:n], out[n:2 * n], list(out[2 * n:3 * n]), list(out[3 * n:4 * n]), out[-1]


def _rs_across_wait(send_sems, recv_sems, parts, recv, after, name):
    n = len(parts)

    def body(*refs):
        recv_refs = refs[n:2 * n]
        s_sems, r_sems = refs[2 * n:3 * n], refs[3 * n:4 * n]
        x, y, c, _ = _position()
        for t in range(n):
            three = recv_refs[t].at[pl.ds(0, N_CHIPS - 1)]
            cp = pltpu.make_async_remote_copy(src_ref=three, dst_ref=three, send_sem=s_sems[t], recv_sem=r_sems[t],
                                              device_id=(x, y, c), device_id_type=MESH)
            cp.wait_send()
            cp.wait_recv()

    thru = [pltpu.HBM(a.shape, a.dtype) for a in list(parts) + list(recv)]
    out = pl.pallas_call(
        body, name=name, out_shape=tuple(thru),
        in_specs=[HBM] * (2 * n) + [SEM] * (2 * n) + [pl.BlockSpec(memory_space=pl.ANY)], out_specs=(HBM,) * (2 * n),
        input_output_aliases={i: i for i in range(2 * n)},
        compiler_params=pltpu.CompilerParams(has_side_effects=EFFECT),
    )(*parts, *recv, *send_sems, *recv_sems, after)
    return list(out[:n]), list(out[n:])


def _rs_join_halves(halves, name):
    n = len(halves)

    def body(*refs):
        ins, outs = refs[:n], refs[n:2 * n]
        send_sems, recv_sems = refs[2 * n:]
        x, y, c, _ = _position()
        cps = [pltpu.make_async_remote_copy(src_ref=ins[t].at[c], dst_ref=outs[t].at[c], send_sem=send_sems.at[t],
                                            recv_sem=recv_sems.at[t], device_id=(x, y, 1 - c), device_id_type=MESH)
               for t in range(n)]
        for cp in cps:
            cp.start()
        for t in range(n):
            blk = outs[t].at[1 - c]
            pltpu.make_async_remote_copy(src_ref=blk, dst_ref=blk, send_sem=send_sems.at[t], recv_sem=recv_sems.at[t],
                                         device_id=(x, y, 1 - c), device_id_type=MESH).wait_recv()
        for cp in cps:
            cp.wait_send()

    return pl.pallas_call(
        body, name=name, in_specs=[ANY] * n, out_specs=[ANY] * n,
        out_shape=[jax.ShapeDtypeStruct(h.shape, h.dtype) for h in halves],
        input_output_aliases={t: t for t in range(n)},
        scratch_shapes=[pltpu.SemaphoreType.DMA((n,)), pltpu.SemaphoreType.DMA((n,))],
    )(*halves)


def _all_gather_small(block, name):
    m_per, n = block.shape

    def body(x_ref, out_ref, send_sems, recv_sems, local_sem):
        x, y, c, chips = _position()
        me, sibling = (x, y, c), (x, y, 1 - c)

        def rows(px, py, pc):
            return out_ref.at[pl.ds((4 * px + 2 * py + pc) * m_per, m_per), :]

        def copy(k, blk, to, src=None):
            return pltpu.make_async_remote_copy(
                src_ref=rows(*blk) if src is None else src, dst_ref=rows(*blk), send_sem=send_sems.at[k],
                recv_sem=recv_sems.at[k], device_id=to, device_id_type=MESH)

        mine = pltpu.make_async_copy(x_ref, rows(*me), local_sem)
        mine.start()
        first = [copy(0, me, sibling, src=x_ref)]
        first += [copy(1 + j, me, (*chip, c), src=x_ref) for j, chip in enumerate(chips)]
        for cp in first:
            cp.start()
        passed = [copy(4 + j, (*chip, c), sibling) for j, chip in enumerate(chips)]
        for j, chip in enumerate(chips):
            copy(1 + j, (*chip, c), me).wait_recv()
            passed[j].start()
        copy(0, sibling, me).wait_recv()
        for j, chip in enumerate(chips):
            copy(4 + j, (*chip, 1 - c), me).wait_recv()
        for cp in first + passed:
            cp.wait_send()
        mine.wait()

    return pl.pallas_call(
        body, name=name, out_shape=jax.ShapeDtypeStruct((N_DEV * m_per, n), block.dtype),
        in_specs=[pl.BlockSpec(memory_space=pltpu.VMEM)], out_specs=pl.BlockSpec(memory_space=pltpu.VMEM),
        scratch_shapes=[pltpu.SemaphoreType.DMA((7,)), pltpu.SemaphoreType.DMA((7,)), pltpu.SemaphoreType.DMA],
    )(block)


def _row_tile(R, n):
    budget = 1 << 19
    if R * n <= budget or R % SUBLANES:
        return R
    t = R
    while t * n > budget and t % (2 * SUBLANES) == 0:
        t //= 2
    return t


def _pair_sum(g, recv, c_arr, name):
    _, _, R, n = g.shape
    tr = _row_tile(R, n)

    def body(c_ref, g_ref, r_ref, o_ref):
        o_ref[...] = (g_ref[...] + r_ref[...]).astype(ICI_DTYPE)

    grid_spec = pltpu.PrefetchScalarGridSpec(
        num_scalar_prefetch=1, grid=(N_CHIPS, R // tr),
        in_specs=[pl.BlockSpec((None, None, tr, n), lambda p, i, c: (p, c[0], i, 0)),
                  pl.BlockSpec((None, tr, n), lambda p, i, c: (p, i, 0))],
        out_specs=pl.BlockSpec((None, tr, n), lambda p, i, c: (p, i, 0)))
    return pl.pallas_call(
        body, name=name, grid_spec=grid_spec, out_shape=jax.ShapeDtypeStruct(recv.shape, ICI_DTYPE),
        compiler_params=_cparams("parallel", "parallel"),
    )(c_arr, g, recv)


def _chip_sum(recv, own, where, name):
    _, R, n = recv.shape
    tr = _row_tile(R, n)

    def body(w_ref, r0, r1, r2, r3, own_ref, o_ref):
        me = w_ref[0]
        terms = [jnp.where(me == k, own_ref[...], r[...]).astype(F32) for k, r in enumerate((r0, r1, r2, r3))]
        o_ref[...] = ((terms[0] + terms[1]) + terms[2]) + terms[3]

    def slot(k):
        return pl.BlockSpec((None, tr, n), lambda i, w: (w[1 + k], i, 0))

    grid_spec = pltpu.PrefetchScalarGridSpec(
        num_scalar_prefetch=1, grid=(R // tr,),
        in_specs=[slot(0), slot(1), slot(2), slot(3), pl.BlockSpec((None, tr, n), lambda i, w: (w[0], i, 0))],
        out_specs=pl.BlockSpec((None, tr, n), lambda i, w: (w[5], i, 0)))
    return pl.pallas_call(
        body, name=name, grid_spec=grid_spec, out_shape=jax.ShapeDtypeStruct((2, R, n), F32),
        compiler_params=_cparams("parallel"),
    )(where, recv, recv, recv, recv, own)


ADAM_C1 = 1.0 / (1.0 - ADAM_B1 ** ADAM_STEP)
ADAM_C2 = 1.0 / (1.0 - ADAM_B2 ** ADAM_STEP)


def _adamw_math(w, g, m, v):
    m = ADAM_B1 * m + (1.0 - ADAM_B1) * g
    v = ADAM_B2 * v + (1.0 - ADAM_B2) * (g * g)
    delta = -ADAM_LR * ((m * ADAM_C1) / (jnp.sqrt(v * ADAM_C2) + ADAM_EPS) + ADAM_WD * w)
    return delta, m, v


def _adamw(w, g, m, v, name):
    R, n = w.shape
    tr = _row_tile(R, n)

    def body(w_ref, g_ref, m_ref, v_ref, d_ref, nm_ref, nv_ref):
        d_ref[...], nm_ref[...], nv_ref[...] = _adamw_math(w_ref[...], g_ref[...], m_ref[...], v_ref[...])

    spec = pl.BlockSpec((tr, n), lambda i: (i, 0))
    return pl.pallas_call(
        body, name=name, grid=(R // tr,), in_specs=[spec] * 4, out_specs=[spec] * 3,
        out_shape=[jax.ShapeDtypeStruct((R, n), F32)] * 3, compiler_params=_cparams("parallel"),
    )(w, g, m, v)


def _adamw_rows(w, g, m, v, row0, prev, name):
    R, n = w.shape
    rows = g.shape[0]
    tr = _row_tile(math.gcd(row0, rows), n)
    off = row0 // tr

    def body(*refs):
        w_ref, g_ref, m_ref, v_ref = refs[:4]
        go_ref, d_ref, nm_ref, nv_ref = refs[-4:]
        gv = g_ref[...]
        go_ref[...] = gv
        d_ref[...], nm_ref[...], nv_ref[...] = _adamw_math(w_ref[...], gv, m_ref[...], v_ref[...])

    at = pl.BlockSpec((tr, n), lambda i: (i + off, 0))
    carried = [] if prev is None else list(prev)
    return pl.pallas_call(
        body, name=name, grid=(rows // tr,),
        in_specs=[at, pl.BlockSpec((tr, n), lambda i: (i, 0)), at, at] + [ANY] * len(carried),
        out_specs=[at] * 4, out_shape=[jax.ShapeDtypeStruct((R, n), F32)] * 4,
        input_output_aliases={4 + i: i for i in range(len(carried))},
        compiler_params=_cparams("parallel"),
    )(w, g, m, v, *carried)


def _adamw_small(w, gall, m, v, name):
    M, n = w.shape

    def body(w_ref, g_ref, m_ref, v_ref, gs_ref, d_ref, nm_ref, nv_ref):
        g = g_ref[0:M, :]
        for d in range(1, N_DEV):
            g = g + g_ref[d * M:(d + 1) * M, :]
        gs_ref[...] = g
        d_ref[...], nm_ref[...], nv_ref[...] = _adamw_math(w_ref[...], g, m_ref[...], v_ref[...])

    return pl.pallas_call(
        body, name=name, out_shape=[jax.ShapeDtypeStruct((M, n), F32)] * 4,
    )(w, gall, m, v)


SMALL_ROWS = 24
MID_ROWS = 4


def _pack_small(ln1_g, ln1_b, ln2_g, ln2_b, norm_w, sinks, a_log, dt_bias):
    mixed = jnp.concatenate([norm_w.reshape(-1), sinks.reshape(-1), a_log.reshape(-1), dt_bias.reshape(-1)])
    mixed = jnp.pad(mixed, (0, D_MODEL - mixed.shape[0]))[None]
    pad = jnp.zeros((SMALL_ROWS - 4 * DEPTH - 1, D_MODEL), F32)
    return jnp.concatenate([ln1_g, ln1_b, ln2_g, ln2_b, mixed, pad], axis=0)


def _unpack_small(p):
    mixed = p[4 * DEPTH]
    return (p[0:4], p[4:8], p[8:12], p[12:16], mixed[0:256].reshape(2, 128), mixed[256:272].reshape(2, 8),
            mixed[272:280].reshape(2, 4), mixed[280:288].reshape(2, 4))


def _pack_mid(conv_w, rconv_w, rconv_b, b_a, b_x, lam):
    lead = conv_w.shape[0]
    flat = jnp.concatenate([conv_w.reshape(lead, -1), rconv_w.reshape(lead, -1), rconv_b, b_a, b_x, lam], axis=1)
    return jnp.pad(flat, ((0, 0), (0, MID_ROWS * D_MODEL - flat.shape[1]))).reshape(lead, MID_ROWS, D_MODEL)


def _unpack_mid(p):
    lead = p.shape[:-2]
    f = p.reshape(lead + (MID_ROWS * D_MODEL,))
    return (f[..., 0:1536].reshape(lead + (4, 384)), f[..., 1536:2560].reshape(lead + (4, 256)),
            f[..., 2560:2816], f[..., 2816:3072], f[..., 3072:3328], f[..., 3328:3584])


def _cols_from_chips(g):
    p, L, R, n = g.shape
    return g.transpose(1, 2, 0, 3).reshape(L, R, p * n)


def _rows_from_chips(g):
    p, L, R, n = g.shape
    return g.transpose(1, 0, 2, 3).reshape(L, p * R, n)


def _cols_to_chips(g):
    L, R, n4 = g.shape
    return g.reshape(L, R, N_CHIPS, n4 // N_CHIPS).transpose(2, 0, 1, 3)


def _rows_to_chips(g):
    L, R4, n = g.shape
    return g.reshape(L, N_CHIPS, R4 // N_CHIPS, n).transpose(1, 0, 2, 3)


def _halves(a):
    return a.reshape(2, -1, a.shape[-1])


def _pad_hyb_cols(w):
    z = jnp.zeros(w.shape[:-1] + (HYB_PAD - HP_BG - 2 * B_HEADS,), w.dtype)
    return jnp.concatenate([w[..., 0:512], w[..., 768:2304], w[..., 2304:2816], w[..., 512:768], w[..., 2816:2824], z], axis=-1)


def _unpad_hyb_cols(w):
    return jnp.concatenate([w[..., 0:512], w[..., 2560:2816], w[..., 512:2048], w[..., 2048:2560], w[..., 2816:2824]], axis=-1)


def _hybrid_fwd(x, W, j, tables, sfx, out_ready, ln):
    cos, sin_s = tables
    T = x.shape[0]
    N = T // B_CHUNK
    proj = _matmul(x, W["hyb_w_in"][j], "nn", "hyb_in" + sfx)
    sinks_b = jnp.broadcast_to(W["hyb_sinks"][j][:, None], (A_Q_HEADS, LANES))
    o_a, lse = _attn_fwd(proj, cos, sin_s, sinks_b, "attn_fwd" + sfx)
    bg = proj[:, HP_BG:HP_BG + 2 * B_HEADS]
    beta = jax.nn.sigmoid(bg[:, :B_HEADS])
    pre = bg[:, B_HEADS:] + W["hyb_dt_bias"][j][None]
    g = -jnp.exp(W["hyb_a_log"][j])[None] * jax.nn.softplus(pre)
    grow = g.T.reshape(B_HEADS, N, 1, B_CHUNK)
    brow = beta.T.reshape(B_HEADS, N, 1, B_CHUNK)
    nw = W["hyb_norm_w"][j][None]
    o_b, states, invs, qkvc = _dn_fwd(proj, W["hyb_conv_w"][j], nw, grow, brow, "dn_fwd" + sfx)
    mix = jnp.concatenate([o_a, o_b], axis=1).astype(ACT_DTYPE)
    out_ready(mix)
    out = _matmul(mix, W["hyb_w_out"][j], "nn", "hyb_out" + sfx, epi=_epi_residual_ln, extra=ln, out_dtype=_ln_out(),
                  tm=512)
    res = dict(proj=proj, o_a=o_a, lse=lse, qkvc=qkvc, beta=beta, pre=pre, g=g, grow=grow, brow=brow,
               states=states, invs=invs, mix=mix, sinks_b=sinks_b, nw=nw)
    return out, res


def _hybrid_bwd(x, du, W, j, res, tables, sfx):
    cos, sin_s = tables
    T = x.shape[0]
    proj = res["proj"]
    d_wout = _matmul(res["mix"], du, "tn", "hyb_out_dw" + sfx)
    dmix = _matmul(du, W["hyb_w_out"][j], "nt", "hyb_out_dx" + sfx)
    dproj, dkc, dkp, dvc, dvp, dsink = _attn_bwd(proj, cos, sin_s, res["sinks_b"], res["o_a"], res["lse"], dmix,
                                                  "attn_bwd" + sfx)
    zpad = jnp.zeros((WINDOW, LANES), F32)
    dk = dkc + jnp.concatenate([dkp[WINDOW:], zpad], axis=0)
    dv = dvc + jnp.concatenate([dvp[WINDOW:], zpad], axis=0)
    dproj, dg4, dbeta4, dnw, dconv = _dn_bwd(res["qkvc"], proj, W["hyb_conv_w"][j], res["nw"], res["grow"], res["brow"],
                                             res["states"], res["invs"], dmix, dproj, "dn_bwd" + sfx)
    dg = dg4.reshape(B_HEADS, T).T
    dbeta = dbeta4.reshape(B_HEADS, T).T
    beta = res["beta"]
    dbeta_logit = dbeta * beta * (1.0 - beta)
    da_logit = dg * (-jnp.exp(W["hyb_a_log"][j]))[None] * jax.nn.sigmoid(res["pre"])
    d_dt_bias = jnp.sum(da_logit, axis=0)
    d_a_log = jnp.sum(dg * res["g"], axis=0)
    zcols = jnp.zeros((T, HYB_PAD - HP_BG - 2 * B_HEADS), F32)
    tail = jnp.concatenate([dk, dv, dbeta_logit, da_logit, zcols], axis=1).astype(ACT_DTYPE)
    dproj = lax.dynamic_update_slice(dproj, tail, (0, HP_K))
    d_win = _matmul(x, dproj, "tn", "hyb_in_dw" + sfx)
    dx = _matmul(dproj, W["hyb_w_in"][j], "nt", "hyb_in_dx" + sfx, epi=_epi_add_residual, extra=du)
    grads = dict(hyb_w_in=d_win, hyb_w_out=d_wout, hyb_sinks=dsink[0], hyb_conv_w=dconv, hyb_a_log=d_a_log,
                 hyb_dt_bias=d_dt_bias, hyb_norm_w=jnp.sum(dnw[:, 0, :], axis=0))
    return dx, grads


def _rec_fwd(x, W, j, sfx, ln):
    proj = _matmul(x, W["rec_w_in"][j], "nn", "rec_in" + sfx)
    sp = jax.nn.softplus(-W["rec_lambda"][j])[None]
    hg, h = _rglru_fwd(proj, W["rec_conv_w"][j], W["rec_conv_b"][j][None], W["rec_w_a"][j], W["rec_w_x"][j],
                       W["rec_b_a"][j][None], W["rec_b_x"][j][None], sp, "rglru_fwd" + sfx)
    out = _matmul(hg, W["rec_w_out"][j], "nn", "rec_out" + sfx, epi=_epi_residual_ln, extra=ln, out_dtype=_ln_out(),
                  tm=512)
    return out, dict(proj=proj, hg=hg, h=h, sp=sp)


def _rec_bwd(x, du, W, j, res, sfx):
    d_wout = _matmul(res["hg"], du, "tn", "rec_out_dw" + sfx)
    dhg = _matmul(du, W["rec_w_out"][j], "nt", "rec_out_dx" + sfx)
    dproj, dcw, dcb, dwa, dwx, dba, dbx, dsp = _rglru_bwd(
        res["proj"], W["rec_conv_w"][j], W["rec_conv_b"][j][None], W["rec_w_a"][j], W["rec_w_x"][j],
        W["rec_b_a"][j][None], W["rec_b_x"][j][None], res["sp"], res["h"], dhg, "rglru_bwd" + sfx)
    d_lam = dsp[0] * (-jax.nn.sigmoid(-W["rec_lambda"][j]))
    d_win = _matmul(x, dproj, "tn", "rec_in_dw" + sfx)
    dx = _matmul(dproj, W["rec_w_in"][j], "nt", "rec_in_dx" + sfx, epi=_epi_add_residual, extra=du)
    grads = dict(rec_w_in=d_win, rec_w_out=d_wout, rec_conv_w=dcw, rec_conv_b=dcb[0], rec_w_a=dwa, rec_w_x=dwx,
                 rec_b_a=dba[0], rec_b_x=dbx[0], rec_lambda=d_lam)
    return dx, grads


def _local_step(x, tgt, W, mlp_w, mixer_ready, on_group):
    T = x.shape[0]
    tables = _rope_tables(T)
    acts = []
    xb = x.astype(ACT_DTYPE)
    for layer in range(DEPTH):
        j, sfx = layer // 2, ""
        last = layer == DEPTH - 1
        mixer_ready(layer, xb)
        ln1 = (xb if layer else x, W["ln1_g"][layer][None], W["ln1_b"][layer][None])
        if layer % 2 == 0:
            (x1b, u1), res = _hybrid_fwd(xb, W, j, tables, sfx,
                                         functools.partial(mixer_ready, layer, out_projection=True), ln1)
        else:
            (x1b, u1), res = _rec_fwd(xb, W, j, sfx, ln1)
        w1, w2, wl = mlp_w(layer, x1b)
        h1 = _matmul(x1b, w1, "nn", "mlp_up", out_dtype=ACT_DTYPE, b_chips=("j", wl))
        out = _matmul(h1, w2, "nn", "mlp_down", a_fn=_relu2, b_chips=("k", wl),
                      epi=_epi_residual_ln_last if last else _epi_residual_ln,
                      extra=(x1b, W["ln2_g"][layer][None], W["ln2_b"][layer][None]), out_dtype=_ln_out(last))
        acts.append(dict(xb=xb, res=res, u1=u1, x1b=x1b, h1=h1, u2=out[-1]))
        xb = out[-2]
    dx, loss = _loss_head(out[0], tgt, "loss_head")
    per_layer = [None] * DEPTH
    d_w1 = dict(lower=lax.empty((N_CHIPS, 1, D_MODEL, D_FF // N_CHIPS), F32),
                upper=lax.empty((N_CHIPS, DEPTH - 1, D_MODEL, D_FF // N_CHIPS), F32))
    d_w2 = dict(lower=lax.empty((N_CHIPS, 1, D_FF // N_CHIPS, D_MODEL), F32),
                upper=lax.empty((N_CHIPS, DEPTH - 1, D_FF // N_CHIPS, D_MODEL), F32))
    token = None
    for layer in reversed(range(DEPTH)):
        j, a = layer // 2, acts[layer]
        grp, li = ("upper", layer - 1) if layer else ("lower", 0)
        ln2_g = W["ln2_g"][layer][None]
        if token is not None:
            ln2_g = ln2_g + token
        du2, dg2, db2 = _ln_bwd(a["u2"], ln2_g, dx, "ln_bwd")
        w1, w2, wl = mlp_w(layer, du2)
        d_w2[grp] = _matmul(a["h1"], du2, "tn", "mlp_down_dw", a_fn=_relu2, out_chips=("i", li, d_w2[grp]))
        dh1 = _matmul(du2, w2, "nt", "mlp_down_dx", epi=_epi_drelu2, extra=a["h1"], out_dtype=ACT_DTYPE,
                      b_chips=("j", wl))
        d_w1[grp] = _matmul(a["x1b"], dh1, "tn", "mlp_up_dw", out_chips=("j", li, d_w1[grp]))
        dx1 = _matmul(dh1, w1, "nt", "mlp_up_dx", epi=_epi_add_residual, extra=du2, b_chips=("k", wl))
        ln1_g = W["ln1_g"][layer][None]
        token = on_group("middle", dx1, None, None, None) if layer == 0 else None
        if token is not None:
            ln1_g = ln1_g + token
        du1, dg1, db1 = _ln_bwd(a["u1"], ln1_g, dx1, "ln_bwd")
        if layer % 2 == 0:
            dx, g = _hybrid_bwd(a["xb"], du1, W, j, a["res"], tables, "")
        else:
            dx, g = _rec_bwd(a["xb"], du1, W, j, a["res"], "")
        g.update(ln1_g=dg1[0], ln1_b=db1[0], ln2_g=dg2[0], ln2_b=db2[0])
        per_layer[layer] = g
        if layer == 1:
            token = on_group("upper", [per_layer[2]], [per_layer[1], per_layer[3]], d_w1["upper"], d_w2["upper"])
        elif layer == 0:
            on_group("lower", [per_layer[0]], [], d_w1["lower"], d_w2["lower"])
    grads = {}
    for name in ("ln1_g", "ln1_b", "ln2_g", "ln2_b"):
        grads[name] = jnp.stack([per_layer[l][name] for l in range(DEPTH)])
    for name in ("hyb_norm_w", "hyb_sinks", "hyb_a_log", "hyb_dt_bias"):
        grads[name] = jnp.stack([per_layer[l][name] for l in (0, 2)])
    return loss, dx, grads


BIG = ("hyb_w_in", "hyb_w_out", "rec_w_in", "rec_w_out", "mlp_w1", "mlp_w2", "rec_w_a", "rec_w_x")
COL_SHARDED = ("hyb_w_in", "rec_w_in", "mlp_w1")
CHIP_MAJOR = ("mlp_w1", "mlp_w2")
MID = ("hyb_conv_w", "rec_conv_w", "rec_conv_b", "rec_b_a", "rec_b_x", "rec_lambda")
SMALL = ("ln1_g", "ln1_b", "ln2_g", "ln2_b", "hyb_norm_w", "hyb_sinks", "hyb_a_log", "hyb_dt_bias")
WEIGHTS = ("hyb_w_in", "hyb_sinks", "hyb_conv_w", "hyb_a_log", "hyb_dt_bias", "hyb_norm_w", "hyb_w_out", "rec_w_in",
           "rec_conv_w", "rec_conv_b", "rec_w_a", "rec_b_a", "rec_w_x", "rec_b_x", "rec_lambda", "rec_w_out", "ln1_g",
           "ln1_b", "mlp_w1", "mlp_w2", "ln2_g", "ln2_b")


def _gather_full_weights(w):
    wb = {k: w[k].astype(MXU_DTYPE) for k in BIG}
    now = ("hyb_w_in",)
    shards = [_halves(wb[k][:1]) for k in now]
    shards.append(_pack_mid(*[w[k] for k in MID]))
    got = _all_gather_weights(shards, "all_gather_weights")
    me = 2 * lax.axis_index("x") + lax.axis_index("y")
    got = [lax.dynamic_update_slice(g, s[None], (me, 0, 0, 0)) for s, g in zip(shards, got)]

    def full(k, g):
        g = g.reshape((N_CHIPS,) + w[k].shape[1:])
        if k in CHIP_MAJOR:
            return g[:, None]
        if k in ("rec_w_a", "rec_w_x"):
            return g.transpose(1, 0, 2, 3).reshape(LRU_BLOCKS, LRU_BLOCK_W, LRU_BLOCK_W)
        f = _cols_from_chips(g[:, None])[0] if k in COL_SHARDED else _rows_from_chips(g[:, None])[0]
        return _pad_hyb_cols(f) if k == "hyb_w_in" else f

    rec = ("rec_w_in", "rec_w_out", "rec_w_a", "rec_w_x")
    groups = [(("hyb_w_out",), 0), (CHIP_MAJOR, 0), (rec, 0), (CHIP_MAJOR, 1), (("hyb_w_in", "hyb_w_out"), 1),
              (CHIP_MAJOR, 2), (rec, 1), (CHIP_MAJOR, 3)]
    own = [wb[k][j] for names, j in groups for k in names]
    land = [lax.dynamic_update_slice(lax.empty((N_CHIPS,) + o.shape, o.dtype), o[None], (me,) + (0,) * o.ndim)
            for o in own]
    send_sems, recv_sems, own, land = _gather_start(own, land, got[-1], "gather_start")
    W = {k: [None] * w[k].shape[0] for k in BIG}
    for k, g in zip(now, got[:-1]):
        W[k][0] = full(k, g)
    arrived = [0]

    def ensure(upto, after):
        while arrived[0] <= upto:
            gi = arrived[0]
            names, j = groups[gi]
            lo = sum(len(nm) for nm, _ in groups[:gi])
            sl = slice(lo, lo + len(names))
            got_g = _gather_wait(send_sems[sl], recv_sems[sl], own[sl], land[sl], after, "gather_wait_%d" % gi)
            for k, g in zip(names, got_g):
                W[k][j] = full(k, g)
            arrived[0] += 1

    def mixer_ready(layer, after, out_projection=False):
        if layer:
            ensure({1: 2, 2: 4, 3: 6}[layer], after)
        elif out_projection:
            ensure(0, after)

    def mlp_w(layer, after):
        ensure({0: 1, 1: 3, 2: 5, 3: 7}[layer], after)
        return W["mlp_w1"][layer], W["mlp_w2"][layer], 0

    conv_w, rconv_w, rconv_b, b_a, b_x, lam = _unpack_mid(got[-1])
    W["hyb_conv_w"] = conv_w.transpose(1, 2, 0, 3).reshape(2, CONV_K, 3 * B_W)
    W["rec_conv_w"] = rconv_w.transpose(1, 2, 0, 3).reshape(2, CONV_K, D_MODEL)
    for k, v in (("rec_conv_b", rconv_b), ("rec_b_a", b_a), ("rec_b_x", b_x), ("rec_lambda", lam)):
        W[k] = v.transpose(1, 0, 2).reshape(2, D_MODEL)
    for k in SMALL:
        W[k] = w[k]
    return W, mlp_w, mixer_ready


REC_VECTORS = ("rec_conv_b", "rec_b_a", "rec_b_x", "rec_lambda")


def _group_by_chip(hyb, rec, d_w1, d_w2):
    t = dict(mlp_w1=d_w1, mlp_w2=d_w2)
    if hyb:
        t["hyb_w_in"] = _cols_to_chips(jnp.stack([_unpad_hyb_cols(g["hyb_w_in"]) for g in hyb]))
        t["hyb_w_out"] = _rows_to_chips(jnp.stack([g["hyb_w_out"] for g in hyb]))
    if rec:
        t["rec_w_in"] = _cols_to_chips(jnp.stack([g["rec_w_in"] for g in rec]))
        t["rec_w_out"] = _rows_to_chips(jnp.stack([g["rec_w_out"] for g in rec]))
        for k in ("rec_w_a", "rec_w_x"):
            v = jnp.stack([g[k] for g in rec])
            t[k] = v.reshape(len(rec), LRU_BLOCKS, N_CHIPS, LRU_BLOCK_W // N_CHIPS, LRU_BLOCK_W).transpose(2, 0, 1, 3, 4)
    flat = [g["hyb_conv_w"].reshape(CONV_K, N_CHIPS, -1).transpose(1, 0, 2).reshape(N_CHIPS, -1) for g in hyb]
    for g in rec:
        flat.append(g["rec_conv_w"].reshape(CONV_K, N_CHIPS, -1).transpose(1, 0, 2).reshape(N_CHIPS, -1))
        flat += [g[k].reshape(N_CHIPS, -1) for k in REC_VECTORS]
    flat = jnp.concatenate(flat, axis=1)
    flat = jnp.pad(flat, ((0, 0), (0, -flat.shape[1] % (2 * D_MODEL))))
    names = [k for k in BIG if k in t]
    out = [t[k].reshape(N_CHIPS, 2, -1, t[k].shape[-1]) for k in names]
    return names + ["small"], out + [flat.reshape(N_CHIPS, 2, -1, D_MODEL)]


def kernel(x, hyb_w_in, hyb_sinks, hyb_conv_w, hyb_a_log, hyb_dt_bias, hyb_norm_w, hyb_w_out, rec_w_in, rec_conv_w, rec_conv_b, rec_w_a, rec_b_a, rec_w_x, rec_b_x, rec_lambda, rec_w_out, ln1_g, ln1_b, mlp_w1, mlp_w2, ln2_g, ln2_b, loss_target, m_hyb_w_in, m_hyb_sinks, m_hyb_conv_w, m_hyb_a_log, m_hyb_dt_bias, m_hyb_norm_w, m_hyb_w_out, m_rec_w_in, m_rec_conv_w, m_rec_conv_b, m_rec_w_a, m_rec_b_a, m_rec_w_x, m_rec_b_x, m_rec_lambda, m_rec_w_out, m_ln1_g, m_ln1_b, m_mlp_w1, m_mlp_w2, m_ln2_g, m_ln2_b, v_hyb_w_in, v_hyb_sinks, v_hyb_conv_w, v_hyb_a_log, v_hyb_dt_bias, v_hyb_norm_w, v_hyb_w_out, v_rec_w_in, v_rec_conv_w, v_rec_conv_b, v_rec_w_a, v_rec_b_a, v_rec_w_x, v_rec_b_x, v_rec_lambda, v_rec_w_out, v_ln1_g, v_ln1_b, v_mlp_w1, v_mlp_w2, v_ln2_g, v_ln2_b):
    args = locals()
    w = {k: args[k] for k in WEIGHTS}
    m = {k: args["m_" + k] for k in WEIGHTS}
    v = {k: args["v_" + k] for k in WEIGHTS}

    W, mlp_w, mixer_ready = _gather_full_weights(w)

    core = lax.axis_index("c").astype(jnp.int32)
    me = (2 * lax.axis_index("x") + lax.axis_index("y")).astype(jnp.int32)
    slots = jnp.arange(N_CHIPS, dtype=jnp.int32)
    where = jnp.concatenate([me[None], jnp.where(slots == me, (slots + 1) % N_CHIPS, slots), core[None]])
    state = {}

    def on_group(group, hyb, rec, d_w1, d_w2):
        if group == "middle":
            group = "upper"
            send_sems, recv_sems, by_chip, recv, _ = state["sibling"]
            by_chip, from_sibling = _rs_sibling_wait(send_sems, recv_sems, by_chip, recv, hyb, "rs_sibling_wait")
        else:
            state[group + " names"], by_chip = _group_by_chip(hyb, rec, d_w1, d_w2)
            if group == "upper":
                recv = [lax.empty((N_CHIPS,) + g.shape[2:], g.dtype) for g in by_chip]
                state["sibling"] = _rs_sibling_start(by_chip, recv, "rs_sibling_start")
                return state["sibling"][4][0, 0]
            from_sibling = _rs_to_sibling(by_chip, "rs_to_sibling")
        pair = [_pair_sum(g, r, core[None], "pair_sum") for g, r in zip(by_chip, from_sibling)]
        recv = [lax.empty(p.shape, p.dtype) for p in pair]
        state[group] = _rs_across_start(pair, recv, "rs_across_start_" + group)
        return state[group][4][0, 0]

    def finish_group(group, after, prev):
        send_sems, recv_sems, pair, recv, _ = state[group]
        pair, from_chips = _rs_across_wait(send_sems, recv_sems, pair, recv, after, "rs_across_wait_" + group)
        half = [_chip_sum(r, p, where, "chip_sum") for r, p in zip(from_chips, pair)]
        joined = _rs_join_halves(half, "rs_join_halves")
        outs = dict(prev or {})
        for k, g in zip(state[group + " names"][:-1], joined[:-1]):
            n = g.shape[-1]
            w2, g2 = w[k].reshape(-1, n), g.reshape(-1, n)
            row0 = w2.shape[0] - g2.shape[0] if group == "upper" else 0
            outs[k] = _adamw_rows(w2, g2, m[k].reshape(-1, n), v[k].reshape(-1, n), row0, outs.get(k), "adamw")
        return outs, joined[-1].reshape(-1)

    loss, dx, grads = _local_step(x[0], loss_target[0], W, mlp_w, mixer_ready, on_group)
    loss = lax.psum(loss[0, 0], ("x", "y", "c"))
    g_out, d_out, m_out, v_out = {}, {}, {}, {}

    upper, small_upper = finish_group("upper", state["lower"][4], None)
    small_g = _pack_small(*[grads[k] for k in SMALL])
    small_all = _all_gather_small(small_g, "all_gather_small")
    sw, sm, sv = (_pack_small(*[t[k] for k in SMALL]) for t in (w, m, v))
    sg, sd, snm, snv = _adamw_small(sw, small_all, sm, sv, "adamw_small")
    for dst, packed in ((g_out, sg), (d_out, sd), (m_out, snm), (v_out, snv)):
        for k, val in zip(SMALL, _unpack_small(packed)):
            dst[k] = val

    done = sum([upper[k][1][0, 0] for k in BIG], sg[0, 0]).reshape(1, 1)
    both, small_lower = finish_group("lower", done, upper)
    for k in BIG:
        shape = w[k].shape
        g_out[k], d_out[k], m_out[k], v_out[k] = (t.reshape(shape) for t in both[k])
    n_conv, n_rconv, n_vec = (w[k][0].size for k in ("hyb_conv_w", "rec_conv_w", "rec_conv_b"))
    n_rec = n_rconv + len(REC_VECTORS) * n_vec
    conv_g = jnp.stack([small_lower[:n_conv], small_upper[:n_conv]])
    rec_g = jnp.stack([small_upper[n_conv + i * n_rec:n_conv + (i + 1) * n_rec] for i in range(2)])
    vec_g = [rec_g[:, n_rconv + i * n_vec:n_rconv + (i + 1) * n_vec] for i in range(len(REC_VECTORS))]
    mid_g = _pack_mid(conv_g, rec_g[:, :n_rconv], *vec_g).reshape(-1, D_MODEL)
    mid_w, mid_m, mid_v = (_pack_mid(*[t[k] for k in MID]).reshape(-1, D_MODEL) for t in (w, m, v))
    mid_d, mid_nm, mid_nv = _adamw(mid_w, mid_g, mid_m, mid_v, "adamw_mid")
    for dst, packed in ((g_out, mid_g), (d_out, mid_d), (m_out, mid_nm), (v_out, mid_nv)):
        for k, val in zip(MID, _unpack_mid(packed.reshape(2, MID_ROWS, D_MODEL))):
            dst[k] = val.reshape(w[k].shape)

    return (loss, dx[None], *[g_out[k] for k in WEIGHTS], *[d_out[k] for k in WEIGHTS],
            *[m_out[k] for k in WEIGHTS], *[v_out[k] for k in WEIGHTS])
```

```python
import functools
import math

import jax
import jax.numpy as jnp
import numpy as np
from jax import lax
from jax.experimental import pallas as pl
from jax.experimental.pallas import tpu as pltpu

F32 = jnp.float32
MXU_DTYPE = jnp.bfloat16
ACT_DTYPE = jnp.bfloat16
ICI_DTYPE = jnp.bfloat16

D_MODEL = 1024
DEPTH = 4
A_HEAD_DIM = 64
A_Q_HEADS = 8
A_KV_HEADS = 2
WINDOW = 128
ROPE_THETA = 10000.0
B_HEADS = 4
B_HEAD_DIM = 128
B_CHUNK = 64
CONV_K = 4
LRU_BLOCKS = 4
LRU_BLOCK_W = D_MODEL // LRU_BLOCKS
LRU_C = 8.0
D_FF = 4 * D_MODEL
A_Q_W = A_Q_HEADS * A_HEAD_DIM
A_KV_W = A_KV_HEADS * A_HEAD_DIM
B_W = B_HEADS * B_HEAD_DIM
HYB_PROJ = A_Q_W + 2 * A_KV_W + 4 * B_W + 2 * B_HEADS
DN_ALPHA = (2 * DEPTH) ** 0.25
LN_EPS = 1e-5
NORM_EPS = 1e-6
ADAM_LR = 0.001
ADAM_B1 = 0.9
ADAM_B2 = 0.999
ADAM_EPS = 1e-08
ADAM_WD = 0.01
ADAM_STEP = 10

HP_Q = 0
HP_QKVB = 512
HP_Z = 2048
HP_K = 2560
HP_V = 2688
HP_BG = 2816
HYB_PAD = 3072

N_CHIPS = 4
N_DEV = 8
V7X_VMEM_LIMIT = 48 * 1024 * 1024
LANES = 128
SUBLANES = 8
NEG_BIG = -1e30

NN = (((1,), (0,)), ((), ()))
NT = (((1,), (1,)), ((), ()))
TN = (((0,), (0,)), ((), ()))


def _cparams(*sem):
    return pltpu.CompilerParams(dimension_semantics=sem, vmem_limit_bytes=V7X_VMEM_LIMIT)


def _dot(a, b, dims=NN):
    return lax.dot_general(a.astype(MXU_DTYPE), b.astype(MXU_DTYPE), dims, preferred_element_type=F32)


def _split_bf16(a):
    hi = a.astype(jnp.bfloat16)
    return hi, (a - hi.astype(F32)).astype(jnp.bfloat16)


def _dotf(a, b, dims=NN):
    ah, al = _split_bf16(a)
    bh, bl = _split_bf16(b)
    dg = functools.partial(lax.dot_general, dimension_numbers=dims, preferred_element_type=F32)
    return dg(ah, bh) + (dg(ah, bl) + dg(al, bh))


def _tile(dim, pref):
    t = min(dim, pref)
    while dim % t:
        t //= 2
    return t


def _sigmoid(x):
    return 1.0 / (1.0 + jnp.exp(-x))


def _silu(x):
    return x * _sigmoid(x)


def _dsilu(x):
    s = _sigmoid(x)
    return s * (1.0 + x * (1.0 - s))


GELU_C = 0.7978845608028654
GELU_A = 0.044715


def _gelu(x):
    return 0.5 * x * (1.0 + jnp.tanh(GELU_C * (x + GELU_A * x * x * x)))


def _dgelu(x):
    t = jnp.tanh(GELU_C * (x + GELU_A * x * x * x))
    return 0.5 * (1.0 + t) + 0.5 * x * (1.0 - t * t) * GELU_C * (1.0 + 3.0 * GELU_A * x * x)


def _matmul(a, b, mode, name, *, tm=1024, tn=1024, tk=1024, a_fn=None, epi=None, extra=None, out_dtype=F32,
            b_chips=None, out_chips=None):
    chunk = tk
    if mode == "tn":
        K, M = a.shape
        if K <= 4 * tk:
            tk, tm = K, tm // 2
    else:
        M, K = a.shape
    whole_k = False
    if b_chips is not None:
        g, b_layer = b_chips
        r, n = b.shape[2:]
        n_dim, k_dim = (r, n) if mode == "nt" else (n, r)
        N = N_CHIPS * n_dim if g == "j" else n_dim
        assert K == (N_CHIPS * k_dim if g == "k" else k_dim)
        if g == "j":
            tn = n_dim
        else:
            whole_k, tk, tm = True, K, tm // 2
    elif mode == "nt":
        N = b.shape[0]
        if tk < K <= 3 * tk:
            tk, tm = K, tm // 2
    else:
        N = b.shape[1]
    if out_chips is not None:
        og, o_layer, o_buf = out_chips
        if og == "j":
            tn = o_buf.shape[3]
        else:
            tm = min(tm, o_buf.shape[2])
    tm, tn, tk = _tile(M, tm), _tile(N, tn), _tile(K, tk)
    nk = K // tk
    if mode == "tn":
        a_spec = pl.BlockSpec((tk, tm), lambda i, j, k: (k, i))
    else:
        a_spec = pl.BlockSpec((tm, tk), lambda i, j, k: (i, k))
    b_block = (tn, tk) if mode == "nt" else (tk, tn)
    if b_chips is None:
        b_spec = pl.BlockSpec(b_block, (lambda i, j, k: (j, k)) if mode == "nt" else (lambda i, j, k: (k, j)))
    elif mode == "nt" and whole_k:
        b_spec = pl.BlockSpec((N_CHIPS, None, tn, k_dim), lambda i, j, k: (0, b_layer, j, 0))
    elif mode == "nt":
        b_spec = pl.BlockSpec((None, None) + b_block, lambda i, j, k: (j, b_layer, 0, k))
    elif whole_k:
        b_spec = pl.BlockSpec((N_CHIPS, None, k_dim, tn), lambda i, j, k: (0, b_layer, 0, j),
                              pipeline_mode=pl.Buffered(1) if N == tn else None)
    else:
        b_spec = pl.BlockSpec((None, None) + b_block, lambda i, j, k: (j, b_layer, k, 0))
    o_spec = pl.BlockSpec((tm, tn), lambda i, j, k: (i, j))
    e_spec = o_spec
    if out_chips is not None:
        per = o_buf.shape[2] // tm
        o_spec = pl.BlockSpec((None, None, tm, tn), (lambda i, j, k: (j, o_layer, i, 0)) if og == "j"
                              else (lambda i, j, k: (i // per, o_layer, i % per, j)))
    dims = {"nn": NN, "nt": NT, "tn": TN}[mode]
    extras = () if extra is None else (extra if isinstance(extra, tuple) else (extra,))
    out_dtypes = out_dtype if isinstance(out_dtype, tuple) else (out_dtype,)
    n_in = 2 + len(extras) + (out_chips is not None)

    def body(*refs):
        a_ref, b_ref = refs[0], refs[1]
        e_refs = refs[2:2 + len(extras)]
        o_refs = refs[n_in:n_in + len(out_dtypes)]
        av = a_ref[...]
        if a_fn is not None:
            av = a_fn(av)
        if whole_k and mode == "nt":
            part = _dot(av[:, :k_dim], b_ref[0], dims)
            for chip in range(1, N_CHIPS):
                part = part + _dot(av[:, chip * k_dim:(chip + 1) * k_dim], b_ref[chip], dims)
        elif mode == "tn" and tk > chunk:
            part = _dot(av[:chunk], b_ref[0:chunk, :], dims)
            for c0 in range(chunk, tk, chunk):
                part = part + _dot(av[c0:c0 + chunk], b_ref[c0:c0 + chunk, :], dims)
        else:
            bv = b_ref[...]
            if whole_k:
                bv = bv.reshape(K, tn)
            part = _dot(av, bv, dims)

        def finish(acc):
            if epi is not None:
                acc = epi(acc, *[e[...] for e in e_refs])
            for o_ref, val, dt in zip(o_refs, acc if isinstance(acc, tuple) else (acc,), out_dtypes):
                o_ref[...] = val.astype(dt)

        if nk == 1:
            finish(part)
        else:
            acc_ref = refs[-1]
            k = pl.program_id(2)

            @pl.when(k == 0)
            def _():
                acc_ref[...] = part

            @pl.when(k > 0)
            def _():
                acc_ref[...] += part

            @pl.when(k == nk - 1)
            def _():
                finish(acc_ref[...])

    row_spec = pl.BlockSpec((1, tn), lambda i, j, k: (0, j))
    in_specs = [a_spec, b_spec] + [row_spec if e.shape[0] == 1 else e_spec for e in extras]
    args = (a, b) + extras
    out_shape = [jax.ShapeDtypeStruct((M, N), dt) for dt in out_dtypes]
    out_specs = [o_spec] * len(out_dtypes)
    aliases = {}
    if out_chips is not None:
        in_specs.append(pl.BlockSpec(memory_space=pl.ANY))
        args += (o_buf,)
        out_shape = [jax.ShapeDtypeStruct(o_buf.shape, o_buf.dtype)]
        aliases = {n_in - 1: 0}
    if not isinstance(out_dtype, tuple):
        out_shape, out_specs = out_shape[0], out_specs[0]
    return pl.pallas_call(
        body, name=name, grid=(M // tm, N // tn, nk), in_specs=in_specs, out_specs=out_specs, out_shape=out_shape,
        input_output_aliases=aliases,
        scratch_shapes=[pltpu.VMEM((tm, tn), F32)] if nk > 1 else [],
        compiler_params=_cparams("parallel", "parallel", "arbitrary"),
    )(*args)


def _relu2(v):
    r = jnp.maximum(v, 0.0)
    return r * r


def _epi_drelu2(acc, h):
    return acc * (2.0 * jnp.maximum(h, 0.0))


def _epi_add_residual(acc, du):
    return acc + DN_ALPHA * du


def _epi_residual_ln(acc, x, g, b):
    u = DN_ALPHA * x + acc
    mu = jnp.mean(u, axis=-1, keepdims=True)
    d = u - mu
    var = jnp.mean(d * d, axis=-1, keepdims=True)
    return d * lax.rsqrt(var + LN_EPS) * g + b, u


def _epi_residual_ln_last(acc, x, g, b):
    o, u = _epi_residual_ln(acc, x, g, b)
    return o, o, u


def _ln_out(last=False):
    return (F32, ACT_DTYPE, F32) if last else (ACT_DTYPE, F32)


def _ln_bwd(u, g, dout, name):
    T, D = u.shape
    tr = _tile(T, 512)

    def body(u_ref, g_ref, d_ref, dub_ref, dg_ref, db_ref):
        i = pl.program_id(0)
        u = u_ref[...]
        mu = jnp.mean(u, axis=-1, keepdims=True)
        d = u - mu
        rstd = lax.rsqrt(jnp.mean(d * d, axis=-1, keepdims=True) + LN_EPS)
        xhat = d * rstd
        dout_v = d_ref[...].astype(F32)
        dxh = dout_v * g_ref[...]
        m1 = jnp.mean(dxh, axis=-1, keepdims=True)
        m2 = jnp.mean(dxh * xhat, axis=-1, keepdims=True)
        du = rstd * (dxh - m1 - xhat * m2)
        dub_ref[...] = du.astype(ACT_DTYPE)
        pg = jnp.sum(dout_v * xhat, axis=0, keepdims=True)
        pb = jnp.sum(dout_v, axis=0, keepdims=True)

        @pl.when(i == 0)
        def _():
            dg_ref[...] = pg
            db_ref[...] = pb

        @pl.when(i > 0)
        def _():
            dg_ref[...] += pg
            db_ref[...] += pb

    row = pl.BlockSpec((tr, D), lambda i: (i, 0))
    vec = pl.BlockSpec((1, D), lambda i: (0, 0))
    return pl.pallas_call(
        body, name=name, grid=(T // tr,), in_specs=[row, vec, row], out_specs=[row, vec, vec],
        out_shape=[jax.ShapeDtypeStruct((T, D), ACT_DTYPE),
                   jax.ShapeDtypeStruct((1, D), F32), jax.ShapeDtypeStruct((1, D), F32)],
        compiler_params=_cparams("arbitrary"),
    )(u, g, dout)


def _loss_head(y, tgt, name):
    T, D = y.shape
    tr = _tile(T, 512)

    def body(y_ref, t_ref, dy_ref, l_ref):
        i = pl.program_id(0)
        e = y_ref[...] - t_ref[...]
        dy_ref[...] = e * (1.0 / D)
        part = jnp.sum(e * e, axis=(0, 1), keepdims=True) * (0.5 / D)

        @pl.when(i == 0)
        def _():
            l_ref[...] = part

        @pl.when(i > 0)
        def _():
            l_ref[...] += part

    row = pl.BlockSpec((tr, D), lambda i: (i, 0))
    one = pl.BlockSpec((1, 1), lambda i: (0, 0))
    return pl.pallas_call(
        body, name=name, grid=(T // tr,), in_specs=[row, row], out_specs=[row, one],
        out_shape=[jax.ShapeDtypeStruct((T, D), F32), jax.ShapeDtypeStruct((1, 1), F32)],
        compiler_params=_cparams("arbitrary"),
    )(y, tgt)


def _swap_half(x):
    n = x.shape[-1]
    lane = lax.broadcasted_iota(jnp.int32, x.shape, 1)
    first = (lane % A_HEAD_DIM) < (A_HEAD_DIM // 2)
    return jnp.where(first, pltpu.roll(x, n - A_HEAD_DIM // 2, axis=1), pltpu.roll(x, A_HEAD_DIM // 2, axis=1))


def _rope(x, cos, sin_signed):
    return x * cos + _swap_half(x) * sin_signed


def _rope_t(dy, cos, sin_signed):
    return dy * cos + _swap_half(dy * sin_signed)


def _rope_tables(T):
    half = A_HEAD_DIM // 2
    inv_freq = np.float32(ROPE_THETA) ** (-np.arange(half, dtype=np.float32) / np.float32(half))
    ang = np.arange(T, dtype=np.float32)[:, None] * inv_freq[None, :]
    cos = np.tile(np.cos(ang), (1, 4))
    sin = np.sin(ang)
    sin_signed = np.tile(np.concatenate([-sin, sin], axis=1), (1, 2))
    return jnp.asarray(cos, F32), jnp.asarray(sin_signed, F32)


def _band_mask(n):
    qi = lax.broadcasted_iota(jnp.int32, (WINDOW, 2 * WINDOW), 0)
    kj = lax.broadcasted_iota(jnp.int32, (WINDOW, 2 * WINDOW), 1)
    return (kj > qi) & (kj <= qi + WINDOW) & ((n > 0) | (kj >= WINDOW))


def _place(v, src_half, dst_half):
    lane = lax.broadcasted_iota(jnp.int32, v.shape, 1)
    if src_half != dst_half:
        v = pltpu.roll(v, A_HEAD_DIM, axis=1)
    keep = (lane >= A_HEAD_DIM) if dst_half else (lane < A_HEAD_DIM)
    return jnp.where(keep, v, 0.0)


def _attn_specs():
    kb, vb = HP_K // LANES, HP_V // LANES
    prev = lambda n: jnp.maximum(n - 1, 0)
    return dict(
        q=pl.BlockSpec((WINDOW, A_Q_W), lambda n: (n, 0)),
        kc=pl.BlockSpec((WINDOW, LANES), lambda n: (n, kb)),
        kp=pl.BlockSpec((WINDOW, LANES), lambda n: (prev(n), kb)),
        vc=pl.BlockSpec((WINDOW, LANES), lambda n: (n, vb)),
        vp=pl.BlockSpec((WINDOW, LANES), lambda n: (prev(n), vb)),
        tq=pl.BlockSpec((WINDOW, LANES), lambda n: (n, 0)),
        tp=pl.BlockSpec((WINDOW, LANES), lambda n: (prev(n), 0)),
        sink=pl.BlockSpec((A_Q_HEADS, LANES), lambda n: (0, 0)),
        row512=pl.BlockSpec((WINDOW, A_Q_W), lambda n: (n, 0)),
        row128=pl.BlockSpec((WINDOW, LANES), lambda n: (n, 0)),
        lse=pl.BlockSpec((WINDOW, A_Q_HEADS), lambda n: (n, 0)),
    )


def _attn_fwd(proj, cos, sin_s, sinks_b, name):
    T = proj.shape[0]
    sp = _attn_specs()

    def body(q_ref, kc_ref, kp_ref, vc_ref, vp_ref, cq_ref, sq_ref, cp_ref, sp_ref, sink_ref, o_ref, l_ref):
        n = pl.program_id(0)
        cq, sq = cq_ref[...], sq_ref[...]
        cq4, sq4 = jnp.tile(cq, (1, A_Q_W // LANES)), jnp.tile(sq, (1, A_Q_W // LANES))
        q = _rope(q_ref[...], cq4, sq4) * (A_HEAD_DIM ** -0.5)
        kc = _rope(kc_ref[...], cq, sq)
        kp = _rope(kp_ref[...], cp_ref[...], sp_ref[...])
        kk = jnp.concatenate([kp, kc], axis=0)
        vv = jnp.concatenate([vp_ref[...], vc_ref[...]], axis=0)
        mask = _band_mask(n)[None]
        lane = lax.broadcasted_iota(jnp.int32, (WINDOW, LANES), 1)
        lane8 = lax.broadcasted_iota(jnp.int32, (WINDOW, A_Q_HEADS), 1)
        qe = jnp.stack([_place(q[:, (hq // 2) * LANES:(hq // 2 + 1) * LANES], hq % 2, hq // 4) for hq in range(A_Q_HEADS)])
        kk8 = jnp.broadcast_to(kk[None], (A_Q_HEADS,) + kk.shape)
        vv8 = jnp.broadcast_to(vv[None], (A_Q_HEADS,) + vv.shape)
        sk = jnp.stack([sink_ref[hq:hq + 1, 0:1] for hq in range(A_Q_HEADS)])
        s = jnp.where(mask, _dot(qe, kk8, BNT), NEG_BIG)
        m = jnp.maximum(jnp.max(s, axis=-1, keepdims=True), sk)
        p = jnp.exp(s - m)
        den = jnp.sum(p, axis=-1, keepdims=True) + jnp.exp(sk - m)
        o = _dot(p * (1.0 / den), vv8, BNN)
        lse_h = m + jnp.log(den)
        outs = []
        lse = jnp.zeros((WINDOW, A_Q_HEADS), F32)
        for pb in range(A_Q_HEADS // 2):
            halves = [_place(o[2 * pb + e], pb // 2, e) for e in range(2)]
            outs.append(jnp.where(lane < A_HEAD_DIM, halves[0], halves[1]))
            for e in range(2):
                lse = jnp.where(lane8 == 2 * pb + e, lse_h[2 * pb + e], lse)
        o_ref[...] = jnp.concatenate(outs, axis=1)
        l_ref[...] = lse

    return pl.pallas_call(
        body, name=name, grid=(T // WINDOW,),
        in_specs=[sp["q"], sp["kc"], sp["kp"], sp["vc"], sp["vp"], sp["tq"], sp["tq"], sp["tp"], sp["tp"], sp["sink"]],
        out_specs=[sp["row512"], sp["lse"]],
        out_shape=[jax.ShapeDtypeStruct((T, A_Q_W), F32), jax.ShapeDtypeStruct((T, A_Q_HEADS), F32)],
        compiler_params=_cparams("parallel"),
    )(proj, proj, proj, proj, proj, cos, sin_s, cos, sin_s, sinks_b)


def _attn_bwd(proj, cos, sin_s, sinks_b, o, lse, dmix, name):
    T = proj.shape[0]
    sp = _attn_specs()

    def body(q_ref, kc_ref, kp_ref, vc_ref, vp_ref, cq_ref, sq_ref, cp_ref, sp_ref, sink_ref, o_ref, l_ref, do_ref,
             dq_ref, dkc_ref, dkp_ref, dvc_ref, dvp_ref, dsink_ref):
        n = pl.program_id(0)
        cq, sq = cq_ref[...], sq_ref[...]
        cp, sps = cp_ref[...], sp_ref[...]
        cq4, sq4 = jnp.tile(cq, (1, A_Q_W // LANES)), jnp.tile(sq, (1, A_Q_W // LANES))
        q = _rope(q_ref[...], cq4, sq4) * (A_HEAD_DIM ** -0.5)
        kc = _rope(kc_ref[...], cq, sq)
        kp = _rope(kp_ref[...], cp, sps)
        kk = jnp.concatenate([kp, kc], axis=0)
        vv = jnp.concatenate([vp_ref[...], vc_ref[...]], axis=0)
        mask = _band_mask(n)[None]
        lane = lax.broadcasted_iota(jnp.int32, (WINDOW, LANES), 1)
        do_all, o_all, l_all = do_ref[...], o_ref[...], l_ref[...]
        lane8 = lax.broadcasted_iota(jnp.int32, (WINDOW, A_Q_HEADS), 1)
        head8 = lax.broadcasted_iota(jnp.int32, (1, A_Q_HEADS), 1)
        prod = do_all * o_all
        qes, does, deltas, lhs = [], [], [], []
        for hq in range(A_Q_HEADS):
            pb, e, kvh = hq // 2, hq % 2, hq // 4
            blk = slice(pb * LANES, (pb + 1) * LANES)
            in_half = (lane >= A_HEAD_DIM) if e else (lane < A_HEAD_DIM)
            deltas.append(jnp.sum(jnp.where(in_half, prod[:, blk], 0.0), axis=-1, keepdims=True))
            qes.append(_place(q[:, blk], e, kvh))
            does.append(_place(do_all[:, blk], e, kvh))
            lhs.append(jnp.sum(jnp.where(lane8 == hq, l_all, 0.0), axis=-1, keepdims=True))
        qe, doe, delta, lh = jnp.stack(qes), jnp.stack(does), jnp.stack(deltas), jnp.stack(lhs)
        kk8 = jnp.broadcast_to(kk[None], (A_Q_HEADS,) + kk.shape)
        vv8 = jnp.broadcast_to(vv[None], (A_Q_HEADS,) + vv.shape)
        sk = jnp.stack([sink_ref[hq:hq + 1, 0:1] for hq in range(A_Q_HEADS)])
        s = _dot(qe, kk8, BNT)
        p = jnp.where(mask, jnp.exp(jnp.where(mask, s, NEG_BIG) - lh), 0.0)
        dvv = jnp.sum(_dot(p, doe, BTN), axis=0)
        ds = p * (_dot(doe, vv8, BNT) - delta)
        dkk = jnp.sum(_dot(ds, qe, BTN), axis=0)
        dqe = _dot(ds, kk8, BNN)
        dsink_h = -jnp.sum(jnp.exp(sk - lh) * delta, axis=(1, 2), keepdims=True)
        dqs = []
        dsk = jnp.zeros((1, A_Q_HEADS), F32)
        for pb in range(A_Q_HEADS // 2):
            halves = [_place(dqe[2 * pb + e], pb // 2, e) for e in range(2)]
            dqs.append(jnp.where(lane < A_HEAD_DIM, halves[0], halves[1]))
            for e in range(2):
                dsk = jnp.where(head8 == 2 * pb + e, dsink_h[2 * pb + e], dsk)
        dq = jnp.concatenate(dqs, axis=1) * (A_HEAD_DIM ** -0.5)
        dq_ref[...] = _rope_t(dq, cq4, sq4).astype(ACT_DTYPE)
        dkp_ref[...] = _rope_t(dkk[:WINDOW], cp, sps)
        dkc_ref[...] = _rope_t(dkk[WINDOW:], cq, sq)
        dvp_ref[...] = dvv[:WINDOW]
        dvc_ref[...] = dvv[WINDOW:]

        @pl.when(n == 0)
        def _():
            dsink_ref[...] = dsk

        @pl.when(n > 0)
        def _():
            dsink_ref[...] += dsk

    return pl.pallas_call(
        body, name=name, grid=(T // WINDOW,),
        in_specs=[sp["q"], sp["kc"], sp["kp"], sp["vc"], sp["vp"], sp["tq"], sp["tq"], sp["tp"], sp["tp"], sp["sink"],
                  sp["row512"], sp["lse"], sp["row512"]],
        out_specs=[sp["row512"], sp["row128"], sp["row128"], sp["row128"], sp["row128"],
                   pl.BlockSpec((1, A_Q_HEADS), lambda n: (0, 0))],
        out_shape=[jax.ShapeDtypeStruct((T, HYB_PAD), ACT_DTYPE)] + [jax.ShapeDtypeStruct((T, LANES), F32)] * 4
        + [jax.ShapeDtypeStruct((1, A_Q_HEADS), F32)],
        compiler_params=_cparams("arbitrary"),
    )(proj, proj, proj, proj, proj, cos, sin_s, cos, sin_s, sinks_b, o, lse, dmix)


def _shift_down(x, prev8, k):
    if k == 0:
        return x
    R, W = x.shape
    r = pltpu.roll(jnp.concatenate([prev8, x], axis=0).reshape(R // SUBLANES + 1, SUBLANES, W), k, axis=1)
    sub = lax.broadcasted_iota(jnp.int32, (R // SUBLANES, SUBLANES, W), 1)
    return jnp.where(sub < k, r[:-1], r[1:]).reshape(R, W)


def _shift_up(x, next8, k):
    if k == 0:
        return x
    R, W = x.shape
    r = pltpu.roll(jnp.concatenate([x, next8], axis=0).reshape(R // SUBLANES + 1, SUBLANES, W), SUBLANES - k, axis=1)
    sub = lax.broadcasted_iota(jnp.int32, (R // SUBLANES, SUBLANES, W), 1)
    return jnp.where(sub >= SUBLANES - k, r[1:], r[:-1]).reshape(R, W)


def _conv(x, prev8, w):
    w0, w1, w2, w3 = (w[j:j + 1] for j in range(CONV_K))
    x1 = _shift_down(x, prev8, 1)
    v = x * w1 + x1 * w0
    v_prev8 = prev8 * w1 + pltpu.roll(prev8, 1, axis=0) * w0
    return x * w3 + x1 * w2 + _shift_down(v, v_prev8, 2)


def _conv_bwd(x, w, dy, next8_dy):
    dx = dy * w[CONV_K - 1:CONV_K]
    dws = []
    for j in range(CONV_K - 1):
        up = _shift_up(dy, next8_dy, CONV_K - 1 - j)
        dx = dx + up * w[j:j + 1]
        dws.append(jnp.sum(up * x, axis=0, keepdims=True))
    dws.append(jnp.sum(dy * x, axis=0, keepdims=True))
    return dx, dws


DK_SCALE = B_HEAD_DIM ** -0.5


BNN = (((2,), (1,)), ((0,), (0,)))
BNT = (((2,), (2,)), ((0,), (0,)))
BTN = (((1,), (1,)), ((0,), (0,)))


def _tri_inv(a):
    C = a.shape[-1]
    ri = lax.broadcasted_iota(jnp.int32, (C, C), 0)
    ci = lax.broadcasted_iota(jnp.int32, (C, C), 1)
    x = jnp.where(ri == ci, 1.0, 0.0)[None] - a
    p = _dotf(a, a, BNN)
    span = 2
    while span < C:
        dot = _dotf if span <= 4 else _dot
        x = x + dot(x, p, BNN)
        span *= 2
        if span < C:
            p = dot(p, p, BNN)
    return x


def _dn_chunk(qc, kc, v, gcol, grow, bcol, s0, tm=None):
    C = B_CHUNK
    ri = lax.broadcasted_iota(jnp.int32, (C, C), 0)
    ci = lax.broadcasted_iota(jnp.int32, (C, C), 1)
    incl, strict = (ri >= ci)[None], (ri > ci)[None]
    rq = lax.rsqrt(jnp.sum(qc * qc, axis=-1, keepdims=True) + NORM_EPS)
    rk = lax.rsqrt(jnp.sum(kc * kc, axis=-1, keepdims=True) + NORM_EPS)
    qn = qc * rq
    q = qn * DK_SCALE
    k = kc * rk
    gc_col = jnp.sum(jnp.where(incl, grow, 0.0), axis=2, keepdims=True)
    gc_row = jnp.sum(jnp.where((ri <= ci)[None], gcol, 0.0), axis=1, keepdims=True)
    gl = jnp.sum(gcol, axis=1, keepdims=True)
    dincl = jnp.where(incl, jnp.exp(jnp.where(incl, gc_col - gc_row, 0.0)), 0.0)
    dstrict = jnp.where(strict, dincl, 0.0)
    eg = jnp.exp(gc_col)
    ekt = jnp.exp(gl - gc_col)
    egl = jnp.exp(gl)
    kb = k * bcol
    vb = v * bcol
    kbg = kb * eg
    a = _dot(kb, k, BNT) * dstrict
    if tm is None:
        tm = _tri_inv(a)
    u = _dot(tm, vb, BNN)
    w = _dot(tm, kbg, BNN)
    vn = u - _dot(w, s0, BNN)
    qk = _dot(q, k, BNT) * dincl
    qg = q * eg
    kt = k * ekt
    o = _dot(qg, s0, BNN) + _dot(qk, vn, BNN)
    s1 = s0 * egl + _dot(kt, vn, BTN)
    return dict(rq=rq, rk=rk, qn=qn, q=q, k=k, dincl=dincl, dstrict=dstrict, eg=eg, ekt=ekt, egl=egl, kb=kb, vb=vb,
                kbg=kbg, a=a, tm=tm, w=w, vn=vn, qk=qk, qg=qg, kt=kt, o=o, s1=s1, ri=ri[None], ci=ci[None])


def _heads(ref):
    return jnp.stack([ref[:, h * B_HEAD_DIM:(h + 1) * B_HEAD_DIM] for h in range(B_HEADS)])


def _store_heads(ref, val):
    for h in range(B_HEADS):
        ref[:, h * B_HEAD_DIM:(h + 1) * B_HEAD_DIM] = val[h]


def _dn_specs(N, rev):
    ix = (lambda n: N - 1 - n) if rev else (lambda n: n)
    wide = lambda cb: pl.BlockSpec((B_CHUNK, B_W), lambda n: (ix(n), cb))
    return dict(
        q=wide(0), k=wide(1), v=wide(2), z=wide(HP_Z // B_W), dob=wide(A_Q_W // B_W), out=wide(0),
        nw=pl.BlockSpec((1, LANES), lambda n: (0, 0)),
        row=pl.BlockSpec((B_HEADS, None, 1, B_CHUNK), lambda n: (0, ix(n), 0, 0)),
        state=pl.BlockSpec((B_HEADS, None, B_HEAD_DIM, B_HEAD_DIM), lambda n: (0, ix(n), 0, 0)),
        inv=pl.BlockSpec((B_HEADS, None, B_CHUNK, B_CHUNK), lambda n: (0, ix(n), 0, 0)),
    )


def _to_col(row):
    C = row.shape[-1]
    eye = lax.broadcasted_iota(jnp.int32, (C, C), 0) == lax.broadcasted_iota(jnp.int32, (C, C), 1)
    return jnp.sum(jnp.where(eye[None], row, 0.0), axis=2, keepdims=True)


def _to_row(col):
    C = col.shape[1]
    eye = lax.broadcasted_iota(jnp.int32, (C, C), 0) == lax.broadcasted_iota(jnp.int32, (C, C), 1)
    return jnp.sum(jnp.where(eye[None], col, 0.0), axis=1, keepdims=True)


def _dn_fwd(proj, conv_w, norm_w, grow, brow, name):
    T = proj.shape[0]
    N = T // B_CHUNK
    sp = _dn_specs(N, False)
    cb0 = HP_QKVB // B_W

    def body(xq_ref, xk_ref, xv_ref, cw_ref, z_ref, nw_ref, gr_ref, br_ref, o_ref, st_ref, tm_ref, c_ref,
             s_ref, prev_ref):
        n = pl.program_id(0)

        @pl.when(n == 0)
        def _():
            s_ref[...] = jnp.zeros_like(s_ref)
            prev_ref[...] = jnp.zeros_like(prev_ref)

        qkv = []
        for part, x_ref in enumerate((xq_ref, xk_ref, xv_ref)):
            cols = slice(part * B_W, (part + 1) * B_W)
            x = x_ref[...]
            xc = _silu(_conv(x, prev_ref[:, cols], cw_ref[:, cols]))
            prev_ref[:, cols] = x[B_CHUNK - SUBLANES:]
            c_ref[:, cols] = xc
            qkv.append(jnp.stack([xc[:, h * B_HEAD_DIM:(h + 1) * B_HEAD_DIM] for h in range(B_HEADS)]))
        s0 = s_ref[...]
        st_ref[...] = s0
        grow_v = gr_ref[...]
        f = _dn_chunk(qkv[0], qkv[1], qkv[2], _to_col(grow_v), grow_v, _to_col(br_ref[...]), s0)
        o = f["o"]
        r = lax.rsqrt(jnp.mean(o * o, axis=-1, keepdims=True) + NORM_EPS)
        _store_heads(o_ref, o * r * nw_ref[...][None] * _silu(_heads(z_ref)))
        s_ref[...] = f["s1"]
        tm_ref[...] = f["tm"]

    wide = lambda cb: pl.BlockSpec((B_CHUNK, B_W), lambda n: (n, cb))
    return pl.pallas_call(
        body, name=name, grid=(N,),
        in_specs=[wide(cb0), wide(cb0 + 1), wide(cb0 + 2), pl.BlockSpec((CONV_K, 3 * B_W), lambda n: (0, 0)),
                  sp["z"], sp["nw"], sp["row"], sp["row"]],
        out_specs=[sp["out"], sp["state"], sp["inv"], pl.BlockSpec((B_CHUNK, 3 * B_W), lambda n: (n, 0))],
        out_shape=[jax.ShapeDtypeStruct((T, B_W), F32),
                   jax.ShapeDtypeStruct((B_HEADS, N, B_HEAD_DIM, B_HEAD_DIM), F32),
                   jax.ShapeDtypeStruct((B_HEADS, N, B_CHUNK, B_CHUNK), F32),
                   jax.ShapeDtypeStruct((T, 3 * B_W), F32)],
        scratch_shapes=[pltpu.VMEM((B_HEADS, B_HEAD_DIM, B_HEAD_DIM), F32), pltpu.VMEM((SUBLANES, 3 * B_W), F32)],
        compiler_params=_cparams("arbitrary"),
    )(proj, proj, proj, conv_w, proj, norm_w, grow, brow)


def _dn_bwd(qkvc, proj, conv_w, norm_w, grow, brow, states, invs, dmix, dproj, name):
    T = qkvc.shape[0]
    N = T // B_CHUNK
    sp = _dn_specs(N, True)
    C = B_CHUNK
    cb0 = HP_QKVB // B_W

    def body(q_ref, k_ref, v_ref, z_ref, nw_ref, gr_ref, br_ref, st_ref, tm_ref, dob_ref, dproj_in,
             xq_ref, xk_ref, xv_ref, pq_ref, pk_ref, pv_ref, cw_ref,
             dp_ref, dg_ref, db_ref, dnw_ref, dcw_ref, ds_ref, nxt_ref):
        n = pl.program_id(0)
        dp_ref[...] = dproj_in[...]
        dz_ref = dp_ref.at[:, HP_Z:HP_Z + B_W]
        d_qkv = {}

        @pl.when(n == 0)
        def _():
            ds_ref[...] = jnp.zeros_like(ds_ref)
            dnw_ref[...] = jnp.zeros_like(dnw_ref)
            dcw_ref[...] = jnp.zeros_like(dcw_ref)
            nxt_ref[...] = jnp.zeros_like(nxt_ref)

        s0 = st_ref[...]
        ds1 = ds_ref[...]
        v, z, nw, bcol_v = _heads(v_ref), _heads(z_ref), nw_ref[...][None], _to_col(br_ref[...])
        grow_v = gr_ref[...]
        f = _dn_chunk(_heads(q_ref), _heads(k_ref), v, _to_col(grow_v), grow_v, bcol_v, s0, tm=tm_ref[...])
        o, q, k, qn = f["o"], f["q"], f["k"], f["qn"]
        eg, ekt, egl = f["eg"], f["ekt"], f["egl"]
        tm, w, vn, kb, vb, kbg = f["tm"], f["w"], f["vn"], f["kb"], f["vb"], f["kbg"]
        qg, kt, qk, a = f["qg"], f["kt"], f["qk"], f["a"]
        ri, ci = f["ri"], f["ci"]

        dob_v = _heads(dob_ref)
        r = lax.rsqrt(jnp.mean(o * o, axis=-1, keepdims=True) + NORM_EPS)
        sz = _silu(z)
        on = o * r
        dnw_ref[...] += jnp.sum(dob_v * sz * on, axis=1, keepdims=True)
        _store_heads(dz_ref, (dob_v * on * nw * _dsilu(z)).astype(ACT_DTYPE))
        d_on = dob_v * sz * nw
        do = r * (d_on - on * jnp.mean(d_on * on, axis=-1, keepdims=True))

        dvn = _dot(qk, do, BTN) + _dot(kt, ds1, BNN)
        dqk = _dot(do, vn, BNT)
        dqg = _dot(do, s0, BNT)
        ds_ref[...] = _dot(qg, do, BTN) + egl * ds1 - _dot(w, dvn, BTN)
        dgl = jnp.sum(s0 * ds1, axis=(1, 2), keepdims=True) * egl
        dkt = _dot(vn, ds1, BNT)
        dw = -_dot(dvn, s0, BNT)
        dq = dqg * eg
        dgc = jnp.sum(dqg * qg, axis=-1, keepdims=True)
        dk = dkt * ekt
        t_kt = jnp.sum(dkt * kt, axis=-1, keepdims=True)
        dgl = dgl + jnp.sum(t_kt, axis=1, keepdims=True)
        dgc = dgc - t_kt
        dqkr = dqk * f["dincl"]
        dq = dq + _dot(dqkr, k, BNN)
        dk = dk + _dot(dqkr, q, BTN)
        e_qk = dqk * qk
        dgc = dgc + jnp.sum(e_qk, axis=-1, keepdims=True)
        dgc_row = -jnp.sum(e_qk, axis=1, keepdims=True)
        dtm = _dot(dvn, vb, BNT) + _dot(dw, kbg, BNT)
        dvb = _dot(tm, dvn, BTN)
        dkbg = _dot(tm, dw, BTN)
        dkb = dkbg * eg
        dgc = dgc + jnp.sum(dkbg * kbg, axis=-1, keepdims=True)
        da = -_dotf(tm, _dotf(dtm, tm, BNT), BTN)
        dkk = da * f["dstrict"]
        e_a = da * a
        dgc = dgc + jnp.sum(e_a, axis=-1, keepdims=True)
        dgc_row = dgc_row - jnp.sum(e_a, axis=1, keepdims=True)
        dkb = dkb + _dot(dkk, k, BNN)
        dk = dk + _dot(dkk, kb, BTN)
        dk = dk + dkb * bcol_v
        db_ref[...] = _to_row(jnp.sum(dkb * k, axis=-1, keepdims=True) + jnp.sum(dvb * v, axis=-1, keepdims=True))
        d_qkv[2] = dvb * bcol_v
        dgc_row = dgc_row + jnp.sum(jnp.where(ri == ci, dgc, 0.0), axis=1, keepdims=True)
        dg_ref[...] = jnp.sum(jnp.where(ci <= ri, _to_col(dgc_row), 0.0), axis=1, keepdims=True) + dgl
        dqs = dq * DK_SCALE
        d_qkv[0] = f["rq"] * (dqs - qn * jnp.sum(dqs * qn, axis=-1, keepdims=True))
        d_qkv[1] = f["rk"] * (dk - k * jnp.sum(dk * k, axis=-1, keepdims=True))
        first = n < N - 1
        for part, (x_ref, p_ref) in enumerate(((xq_ref, pq_ref), (xk_ref, pk_ref), (xv_ref, pv_ref))):
            cols = slice(part * B_W, (part + 1) * B_W)
            dy = jnp.concatenate([d_qkv[part][h] for h in range(B_HEADS)], axis=1)
            x, w = x_ref[...], cw_ref[:, cols]
            dpre = dy * _dsilu(_conv(x, jnp.where(first, p_ref[...], 0.0), w))
            dx, dws = _conv_bwd(x, w, dpre, nxt_ref[:, cols])
            nxt_ref[:, cols] = dpre[:SUBLANES]
            dp_ref[:, HP_QKVB + part * B_W:HP_QKVB + (part + 1) * B_W] = dx.astype(ACT_DTYPE)
            for j in range(CONV_K):
                dcw_ref[j:j + 1, cols] += dws[j]

    rev = lambda n: N - 1 - n
    part_x = lambda p: pl.BlockSpec((C, B_W), lambda n: (rev(n), cb0 + p))
    part_prev = lambda p: pl.BlockSpec((SUBLANES, B_W), lambda n: (jnp.maximum(rev(n) * (C // SUBLANES) - 1, 0), cb0 + p))
    whole_row = pl.BlockSpec((C, HYB_PAD), lambda n: (rev(n), 0))
    return pl.pallas_call(
        body, name=name, grid=(N,),
        in_specs=[sp["q"], sp["k"], sp["v"], sp["z"], sp["nw"], sp["row"], sp["row"], sp["state"], sp["inv"], sp["dob"],
                  whole_row, part_x(0), part_x(1), part_x(2), part_prev(0), part_prev(1), part_prev(2),
                  pl.BlockSpec((CONV_K, 3 * B_W), lambda n: (0, 0))],
        out_specs=[whole_row, sp["row"], sp["row"], pl.BlockSpec((B_HEADS, 1, LANES), lambda n: (0, 0, 0)),
                   pl.BlockSpec((CONV_K, 3 * B_W), lambda n: (0, 0))],
        out_shape=[jax.ShapeDtypeStruct(dproj.shape, dproj.dtype),
                   jax.ShapeDtypeStruct((B_HEADS, N, 1, C), F32), jax.ShapeDtypeStruct((B_HEADS, N, 1, C), F32),
                   jax.ShapeDtypeStruct((B_HEADS, 1, LANES), F32), jax.ShapeDtypeStruct((CONV_K, 3 * B_W), F32)],
        input_output_aliases={10: 0},
        scratch_shapes=[pltpu.VMEM((B_HEADS, B_HEAD_DIM, B_HEAD_DIM), F32), pltpu.VMEM((SUBLANES, 3 * B_W), F32)],
        compiler_params=_cparams("arbitrary"),
    )(qkvc, qkvc, qkvc, proj, norm_w, grow, brow, states, invs, dmix, dproj, proj, proj, proj, proj, proj, proj, conv_w)


def _lru_gates(xc, wa_ref, wx_ref, ba, bx, sp):
    pre_r, pre_i = [], []
    for hb in range(LRU_BLOCKS):
        xb = xc[:, hb * LRU_BLOCK_W:(hb + 1) * LRU_BLOCK_W]
        pre_r.append(_dot(xb, wa_ref[hb]))
        pre_i.append(_dot(xb, wx_ref[hb]))
    r = _sigmoid(jnp.concatenate(pre_r, axis=1) + ba)
    i = _sigmoid(jnp.concatenate(pre_i, axis=1) + bx)
    la = -LRU_C * r * sp
    a = jnp.exp(la)
    th = jnp.tanh(la)
    s = jnp.sqrt(-2.0 * th / (1.0 - th))
    return r, i, a, s


def _scan_down(a, b, h_in):
    R, W = a.shape
    a = a.reshape(R // SUBLANES, SUBLANES, W)
    b = b.reshape(R // SUBLANES, SUBLANES, W)
    sub = lax.broadcasted_iota(jnp.int32, a.shape, 1)
    d = 1
    while d < SUBLANES:
        ok = sub >= d
        b = a * jnp.where(ok, pltpu.roll(b, d, axis=1), 0.0) + b
        a = a * jnp.where(ok, pltpu.roll(a, d, axis=1), 1.0)
        d *= 2
    out, last = [], h_in
    for g in range(R // SUBLANES):
        h = b[g] + a[g] * last
        out.append(h)
        last = h[SUBLANES - 1:SUBLANES]
    return jnp.concatenate(out, axis=0)


def _scan_up(a, b, l_in):
    R, W = a.shape
    a = a.reshape(R // SUBLANES, SUBLANES, W)
    b = b.reshape(R // SUBLANES, SUBLANES, W)
    sub = lax.broadcasted_iota(jnp.int32, a.shape, 1)
    d = 1
    while d < SUBLANES:
        ok = sub < SUBLANES - d
        b = a * jnp.where(ok, pltpu.roll(b, SUBLANES - d, axis=1), 0.0) + b
        a = a * jnp.where(ok, pltpu.roll(a, SUBLANES - d, axis=1), 1.0)
        d *= 2
    out, nxt = [], l_in
    for g in range(R // SUBLANES - 1, -1, -1):
        lam = b[g] + a[g] * nxt
        out.append(lam)
        nxt = lam[0:1]
    return jnp.concatenate(out[::-1], axis=0)


def _rglru_fwd(proj, conv_w, conv_b, wa, wx, ba, bx, sp, name):
    T = proj.shape[0]
    R = _tile(T, 256)
    W = D_MODEL

    def body(p_ref, cw_ref, cb_ref, wa_ref, wx_ref, ba_ref, bx_ref, sp_ref, hg_ref, h_ref, prev_ref, hc_ref):
        i = pl.program_id(0)

        @pl.when(i == 0)
        def _():
            prev_ref[...] = jnp.zeros_like(prev_ref)
            hc_ref[...] = jnp.zeros_like(hc_ref)

        xr = p_ref[:, :W]
        gate = p_ref[:, W:]
        xc = _conv(xr, prev_ref[...], cw_ref[...]) + cb_ref[...]
        prev_ref[...] = xr[R - SUBLANES:]
        r, ig, a, s = _lru_gates(xc, wa_ref, wx_ref, ba_ref[...], bx_ref[...], sp_ref[...])
        h = _scan_down(a, s * ig * xc, hc_ref[SUBLANES - 1:SUBLANES, :])
        h_ref[...] = h
        hg_ref[...] = (h * _gelu(gate)).astype(ACT_DTYPE)
        hc_ref[...] = h[R - SUBLANES:]

    vec = pl.BlockSpec((1, W), lambda i: (0, 0))
    wsp = pl.BlockSpec((LRU_BLOCKS, LRU_BLOCK_W, LRU_BLOCK_W), lambda i: (0, 0, 0))
    row = pl.BlockSpec((R, W), lambda i: (i, 0))
    return pl.pallas_call(
        body, name=name, grid=(T // R,),
        in_specs=[pl.BlockSpec((R, 2 * W), lambda i: (i, 0)), pl.BlockSpec((CONV_K, W), lambda i: (0, 0)),
                  vec, wsp, wsp, vec, vec, vec],
        out_specs=[row, row],
        out_shape=[jax.ShapeDtypeStruct((T, W), ACT_DTYPE), jax.ShapeDtypeStruct((T, W), F32)],
        scratch_shapes=[pltpu.VMEM((SUBLANES, W), F32), pltpu.VMEM((SUBLANES, W), F32)],
        compiler_params=_cparams("arbitrary"),
    )(proj, conv_w, conv_b, wa, wx, ba, bx, sp)


def _rglru_bwd(proj, conv_w, conv_b, wa, wx, ba, bx, sp, h, dhg, name):
    T = proj.shape[0]
    R = _tile(T, 256)
    nb = T // R
    r8 = R // SUBLANES
    W = D_MODEL

    def body(p_ref, pp_ref, cw_ref, cb_ref, wa_ref, wx_ref, ba_ref, bx_ref, sp_ref, h_ref, hp_ref, dhg_ref,
             dp_ref, dcw_ref, dcb_ref, dwa_ref, dwx_ref, dba_ref, dbx_ref, dsp_ref, lam_ref, nxt_ref):
        step = pl.program_id(0)
        blk = nb - 1 - step

        @pl.when(step == 0)
        def _():
            lam_ref[...] = jnp.zeros_like(lam_ref)
            nxt_ref[...] = jnp.zeros_like(nxt_ref)

        xr = p_ref[:, :W]
        gate = p_ref[:, W:]
        first = blk > 0
        prev8 = jnp.where(first, pp_ref[:, :W], 0.0)
        hprev8 = jnp.where(first, hp_ref[...], 0.0)
        cw = cw_ref[...]
        spv = sp_ref[...]
        xc = _conv(xr, prev8, cw) + cb_ref[...]
        r, ig, a, s = _lru_gates(xc, wa_ref, wx_ref, ba_ref[...], bx_ref[...], spv)
        hv = h_ref[...]
        dhg_v = dhg_ref[...]
        dgate = dhg_v * hv * _dgelu(gate)
        dh = dhg_v * _gelu(gate)
        row = lax.broadcasted_iota(jnp.int32, (R, W), 0)
        last = row == R - 1
        a_up = jnp.where(last, 0.0, pltpu.roll(a, R - 1, axis=0))
        lam = _scan_up(a_up, dh + jnp.where(last, lam_ref[0:1, :], 0.0), jnp.zeros((1, W), F32))
        lam_ref[...] = (a * lam)[:SUBLANES]
        h_dn = _shift_down(hv, hprev8, 1)
        da = lam * h_dn
        bx_in = ig * xc
        dsv = lam * bx_in
        dig = lam * s * xc
        dxc = lam * s * ig
        dla = da * a - dsv * (a * a) / s
        dr = dla * (-LRU_C) * spv
        dsp = jnp.sum(dla * (-LRU_C) * r, axis=0, keepdims=True)
        dpr = dr * r * (1.0 - r)
        dpi = dig * ig * (1.0 - ig)
        dxc_parts, dwa_parts, dwx_parts = [], [], []
        for hb in range(LRU_BLOCKS):
            sl = slice(hb * LRU_BLOCK_W, (hb + 1) * LRU_BLOCK_W)
            xb, gr, gi = xc[:, sl], dpr[:, sl], dpi[:, sl]
            dxc_parts.append(_dot(gr, wa_ref[hb], NT) + _dot(gi, wx_ref[hb], NT))
            dwa_parts.append(_dot(xb, gr, TN))
            dwx_parts.append(_dot(xb, gi, TN))
        dxc = dxc + jnp.concatenate(dxc_parts, axis=1)
        dxr, dcw = _conv_bwd(xr, cw, dxc, nxt_ref[...])
        nxt_ref[...] = dxc[:SUBLANES]
        dp_ref[:, :W] = dxr.astype(ACT_DTYPE)
        dp_ref[:, W:] = dgate.astype(ACT_DTYPE)
        dcb = jnp.sum(dxc, axis=0, keepdims=True)
        dba = jnp.sum(dpr, axis=0, keepdims=True)
        dbx = jnp.sum(dpi, axis=0, keepdims=True)

        @pl.when(step == 0)
        def _():
            for j in range(CONV_K):
                dcw_ref[j:j + 1, :] = dcw[j]
            dcb_ref[...] = dcb
            dba_ref[...] = dba
            dbx_ref[...] = dbx
            dsp_ref[...] = dsp
            for hb in range(LRU_BLOCKS):
                dwa_ref[hb] = dwa_parts[hb]
                dwx_ref[hb] = dwx_parts[hb]

        @pl.when(step > 0)
        def _():
            for j in range(CONV_K):
                dcw_ref[j:j + 1, :] += dcw[j]
            dcb_ref[...] += dcb
            dba_ref[...] += dba
            dbx_ref[...] += dbx
            dsp_ref[...] += dsp
            for hb in range(LRU_BLOCKS):
                dwa_ref[hb] += dwa_parts[hb]
                dwx_ref[hb] += dwx_parts[hb]

    rv = lambda i: nb - 1 - i
    before = lambda i: jnp.maximum((nb - 1 - i) * r8 - 1, 0)
    vec = pl.BlockSpec((1, W), lambda i: (0, 0))
    cws = pl.BlockSpec((CONV_K, W), lambda i: (0, 0))
    wsp = pl.BlockSpec((LRU_BLOCKS, LRU_BLOCK_W, LRU_BLOCK_W), lambda i: (0, 0, 0))
    row = pl.BlockSpec((R, W), lambda i: (rv(i), 0))
    wshape = jax.ShapeDtypeStruct((LRU_BLOCKS, LRU_BLOCK_W, LRU_BLOCK_W), F32)
    vshape = jax.ShapeDtypeStruct((1, W), F32)
    return pl.pallas_call(
        body, name=name, grid=(nb,),
        in_specs=[pl.BlockSpec((R, 2 * W), lambda i: (rv(i), 0)), pl.BlockSpec((SUBLANES, 2 * W), lambda i: (before(i), 0)),
                  cws, vec, wsp, wsp, vec, vec, vec, row, pl.BlockSpec((SUBLANES, W), lambda i: (before(i), 0)), row],
        out_specs=[pl.BlockSpec((R, 2 * W), lambda i: (rv(i), 0)), cws, vec, wsp, wsp, vec, vec, vec],
        out_shape=[jax.ShapeDtypeStruct((T, 2 * W), ACT_DTYPE), jax.ShapeDtypeStruct((CONV_K, W), F32), vshape,
                   wshape, wshape, vshape, vshape, vshape],
        scratch_shapes=[pltpu.VMEM((SUBLANES, W), F32), pltpu.VMEM((SUBLANES, W), F32)],
        compiler_params=_cparams("arbitrary"),
    )(proj, proj, conv_w, conv_b, wa, wx, ba, bx, sp, h, h, dhg)


MESH = pl.DeviceIdType.MESH
ANY = pl.BlockSpec(memory_space=pl.ANY)


def _position():
    x, y, c = lax.axis_index("x"), lax.axis_index("y"), lax.axis_index("c")
    other_chips = [(1 - x, y), (x, 1 - y), (1 - x, 1 - y)]
    return x, y, c, other_chips


def _all_gather_weights(shards, name):
    n = len(shards)

    def body(*refs):
        ins, outs = refs[:n], refs[n:2 * n]
        send_sems, recv_sems = refs[2 * n:]
        x, y, c, chips = _position()
        me = 2 * x + y
        sibling = (x, y, 1 - c)

        def rcopy(t, k, src, dst, to):
            return pltpu.make_async_remote_copy(src_ref=src, dst_ref=dst, send_sem=send_sems.at[t, k],
                                                recv_sem=recv_sems.at[t, k], device_id=to, device_id_type=MESH)

        started = []
        for t in range(n):
            for j, (cx, cy) in enumerate(chips):
                cp = rcopy(t, j, ins[t].at[c], outs[t].at[me, c], (cx, cy, c))
                cp.start()
                started.append(cp)
        for t in range(n):
            for j, (cx, cy) in enumerate(chips):
                blk = outs[t].at[2 * cx + cy, c]
                rcopy(t, j, blk, blk, (cx, cy, c)).wait_recv()
                cp = rcopy(t, 3 + j, blk, blk, sibling)
                cp.start()
                started.append(cp)
        for t in range(n):
            for j, (cx, cy) in enumerate(chips):
                blk = outs[t].at[2 * cx + cy, 1 - c]
                rcopy(t, 3 + j, blk, blk, sibling).wait_recv()
        for cp in started:
            cp.wait_send()

    return pl.pallas_call(
        body, name=name, in_specs=[ANY] * n, out_specs=[ANY] * n,
        out_shape=[jax.ShapeDtypeStruct((N_CHIPS,) + s.shape, s.dtype) for s in shards],
        scratch_shapes=[pltpu.SemaphoreType.DMA((n, 6)), pltpu.SemaphoreType.DMA((n, 6))],
    )(*shards)


HBM = pl.BlockSpec(memory_space=pltpu.HBM)
SEM = pl.BlockSpec(memory_space=pltpu.SEMAPHORE)
EFFECT = pltpu.SideEffectType.DATAFLOW_SIDE_EFFECTING


def _gather_start(own, land, after, name):
    n = len(own)

    def body(*refs):
        own_refs, land_refs = refs[:n], refs[n:2 * n]
        send_sems = refs[2 * n + 1:3 * n + 1]
        recv_sems = refs[3 * n + 1:4 * n + 1]
        x, y, c, chips = _position()
        me = 2 * x + y
        for t in range(n):
            for cx, cy in chips:
                pltpu.make_async_remote_copy(
                    src_ref=own_refs[t], dst_ref=land_refs[t].at[me], send_sem=send_sems[t], recv_sem=recv_sems[t],
                    device_id=(cx, cy, c), device_id_type=MESH).start()

    sems = (pltpu.SemaphoreType.DMA(()),) * (2 * n)
    thru = [pltpu.HBM(a.shape, a.dtype) for a in list(own) + list(land)]
    out = pl.pallas_call(
        body, name=name, out_shape=(*sems, *thru),
        in_specs=[HBM] * (2 * n) + [pl.BlockSpec(memory_space=pl.ANY)], out_specs=(SEM,) * (2 * n) + (HBM,) * (2 * n),
        input_output_aliases={i: 2 * n + i for i in range(2 * n)},
        compiler_params=pltpu.CompilerParams(has_side_effects=EFFECT),
    )(*[pltpu.with_memory_space_constraint(a, pltpu.HBM) for a in list(own) + list(land)], after)
    return list(out[:n]), list(out[n:2 * n]), list(out[2 * n:3 * n]), list(out[3 * n:])


def _gather_wait(send_sems, recv_sems, own, land, after, name):
    n = len(own)

    def body(*refs):
        land_refs = refs[n:2 * n]
        s_sems, r_sems = refs[2 * n:3 * n], refs[3 * n:4 * n]
        x, y, c, _ = _position()
        for t in range(n):
            three = land_refs[t].at[pl.ds(0, N_CHIPS - 1)]
            cp = pltpu.make_async_remote_copy(src_ref=three, dst_ref=three, send_sem=s_sems[t], recv_sem=r_sems[t],
                                              device_id=(x, y, c), device_id_type=MESH)
            cp.wait_send()
            cp.wait_recv()

    thru = [pltpu.HBM(a.shape, a.dtype) for a in list(own) + list(land)]
    out = pl.pallas_call(
        body, name=name, out_shape=tuple(thru),
        in_specs=[HBM] * (2 * n) + [SEM] * (2 * n) + [pl.BlockSpec(memory_space=pl.ANY)], out_specs=(HBM,) * (2 * n),
        input_output_aliases={i: i for i in range(2 * n)},
        compiler_params=pltpu.CompilerParams(has_side_effects=EFFECT),
    )(*own, *land, *send_sems, *recv_sems, after)
    return list(out[n:])


def _rs_to_sibling(grads, name):
    n = len(grads)

    def body(*refs):
        ins, outs = refs[:n], refs[n:2 * n]
        send_sems, recv_sems = refs[2 * n:]
        x, y, c, _ = _position()
        cps = [pltpu.make_async_remote_copy(src_ref=ins[t].at[:, 1 - c], dst_ref=outs[t], send_sem=send_sems.at[t],
                                            recv_sem=recv_sems.at[t], device_id=(x, y, 1 - c), device_id_type=MESH)
               for t in range(n)]
        for cp in cps:
            cp.start()
        for cp in cps:
            cp.wait()

    return pl.pallas_call(
        body, name=name, in_specs=[ANY] * n, out_specs=[ANY] * n,
        out_shape=[jax.ShapeDtypeStruct((N_CHIPS,) + g.shape[2:], g.dtype) for g in grads],
        scratch_shapes=[pltpu.SemaphoreType.DMA((n,)), pltpu.SemaphoreType.DMA((n,))],
    )(*grads)


def _rs_sibling_start(grads, recv, name):
    n = len(grads)

    def body(*refs):
        grad_refs, recv_refs = refs[:n], refs[n:2 * n]
        send_sems, recv_sems = refs[2 * n:3 * n], refs[3 * n:4 * n]
        token_ref = refs[-1]
        x, y, c, _ = _position()
        for t in range(n):
            pltpu.make_async_remote_copy(src_ref=grad_refs[t].at[:, 1 - c], dst_ref=recv_refs[t], send_sem=send_sems[t],
                                         recv_sem=recv_sems[t], device_id=(x, y, 1 - c), device_id_type=MESH).start()
        token_ref[...] = jnp.zeros_like(token_ref)

    sems = (pltpu.SemaphoreType.DMA(()),) * (2 * n)
    thru = [pltpu.HBM(a.shape, a.dtype) for a in list(grads) + list(recv)]
    out = pl.pallas_call(
        body, name=name, out_shape=(*sems, *thru, jax.ShapeDtypeStruct((SUBLANES, LANES), F32)),
        in_specs=[HBM] * (2 * n), out_specs=(SEM,) * (2 * n) + (HBM,) * (2 * n) + (pl.BlockSpec(memory_space=pltpu.VMEM),),
        input_output_aliases={i: 2 * n + i for i in range(2 * n)},
        compiler_params=pltpu.CompilerParams(has_side_effects=EFFECT),
    )(*[pltpu.with_memory_space_constraint(a, pltpu.HBM) for a in list(grads) + list(recv)])
    return out[:n], out[n:2 * n], list(out[2 * n:3 * n]), list(out[3 * n:4 * n]), out[-1]


def _rs_sibling_wait(send_sems, recv_sems, grads, recv, after, name):
    n = len(grads)

    def body(*refs):
        recv_refs = refs[n:2 * n]
        s_sems, r_sems = refs[2 * n:3 * n], refs[3 * n:4 * n]
        x, y, c, _ = _position()
        for t in range(n):
            cp = pltpu.make_async_remote_copy(src_ref=recv_refs[t], dst_ref=recv_refs[t], send_sem=s_sems[t],
                                              recv_sem=r_sems[t], device_id=(x, y, c), device_id_type=MESH)
            cp.wait_send()
            cp.wait_recv()

    thru = [pltpu.HBM(a.shape, a.dtype) for a in list(grads) + list(recv)]
    out = pl.pallas_call(
        body, name=name, out_shape=tuple(thru),
        in_specs=[HBM] * (2 * n) + [SEM] * (2 * n) + [pl.BlockSpec(memory_space=pl.ANY)], out_specs=(HBM,) * (2 * n),
        input_output_aliases={i: i for i in range(2 * n)},
        compiler_params=pltpu.CompilerParams(has_side_effects=EFFECT),
    )(*grads, *recv, *send_sems, *recv_sems, after)
    return list(out[:n]), list(out[n:])


def _rs_across_start(parts, recv, name):
    n = len(parts)

    def body(*refs):
        part_refs, recv_refs = refs[:n], refs[n:2 * n]
        send_sems, recv_sems = refs[2 * n:3 * n], refs[3 * n:4 * n]
        token_ref = refs[-1]
        x, y, c, chips = _position()
        me = 2 * x + y
        for t in range(n):
            for cx, cy in chips:
                pltpu.make_async_remote_copy(src_ref=part_refs[t].at[2 * cx + cy], dst_ref=recv_refs[t].at[me],
                                             send_sem=send_sems[t], recv_sem=recv_sems[t], device_id=(cx, cy, c),
                                             device_id_type=MESH).start()
        token_ref[...] = jnp.zeros_like(token_ref)

    sems = (pltpu.SemaphoreType.DMA(()),) * (2 * n)
    thru = [pltpu.HBM(a.shape, a.dtype) for a in list(parts) + list(recv)]
    out = pl.pallas_call(
        body, name=name, out_shape=(*sems, *thru, jax.ShapeDtypeStruct((SUBLANES, LANES), F32)),
        in_specs=[HBM] * (2 * n), out_specs=(SEM,) * (2 * n) + (HBM,) * (2 * n) + (pl.BlockSpec(memory_space=pltpu.VMEM),),
        input_output_aliases={i: 2 * n + i for i in range(2 * n)},
        compiler_params=pltpu.CompilerParams(has_side_effects=EFFECT),
    )(*[pltpu.with_memory_space_constraint(a, pltpu.HBM) for a in list(parts) + list(recv)])
    return out[:n], out[n:2 * n], list(out[2 * n:3 * n]), list(out[3 * n:4 * n]), out[-1]


def _rs_across_wait(send_sems, recv_sems, parts, recv, after, name):
    n = len(parts)

    def body(*refs):
        recv_refs = refs[n:2 * n]
        s_sems, r_sems = refs[2 * n:3 * n], refs[3 * n:4 * n]
        x, y, c, _ = _position()
        for t in range(n):
            three = recv_refs[t].at[pl.ds(0, N_CHIPS - 1)]
            cp = pltpu.make_async_remote_copy(src_ref=three, dst_ref=three, send_sem=s_sems[t], recv_sem=r_sems[t],
                                              device_id=(x, y, c), device_id_type=MESH)
            cp.wait_send()
            cp.wait_recv()

    thru = [pltpu.HBM(a.shape, a.dtype) for a in list(parts) + list(recv)]
    out = pl.pallas_call(
        body, name=name, out_shape=tuple(thru),
        in_specs=[HBM] * (2 * n) + [SEM] * (2 * n) + [pl.BlockSpec(memory_space=pl.ANY)], out_specs=(HBM,) * (2 * n),
        input_output_aliases={i: i for i in range(2 * n)},
        compiler_params=pltpu.CompilerParams(has_side_effects=EFFECT),
    )(*parts, *recv, *send_sems, *recv_sems, after)
    return list(out[:n]), list(out[n:])


def _rs_join_halves(halves, name):
    n = len(halves)

    def body(*refs):
        ins, outs = refs[:n], refs[n:2 * n]
        send_sems, recv_sems = refs[2 * n:]
        x, y, c, _ = _position()
        cps = [pltpu.make_async_remote_copy(src_ref=ins[t].at[c], dst_ref=outs[t].at[c], send_sem=send_sems.at[t],
                                            recv_sem=recv_sems.at[t], device_id=(x, y, 1 - c), device_id_type=MESH)
               for t in range(n)]
        for cp in cps:
            cp.start()
        for t in range(n):
            blk = outs[t].at[1 - c]
            pltpu.make_async_remote_copy(src_ref=blk, dst_ref=blk, send_sem=send_sems.at[t], recv_sem=recv_sems.at[t],
                                         device_id=(x, y, 1 - c), device_id_type=MESH).wait_recv()
        for cp in cps:
            cp.wait_send()

    return pl.pallas_call(
        body, name=name, in_specs=[ANY] * n, out_specs=[ANY] * n,
        out_shape=[jax.ShapeDtypeStruct(h.shape, h.dtype) for h in halves],
        input_output_aliases={t: t for t in range(n)},
        scratch_shapes=[pltpu.SemaphoreType.DMA((n,)), pltpu.SemaphoreType.DMA((n,))],
    )(*halves)


def _all_gather_small(block, name):
    m_per, n = block.shape

    def body(x_ref, out_ref, send_sems, recv_sems, local_sem):
        x, y, c, chips = _position()
        me, sibling = (x, y, c), (x, y, 1 - c)

        def rows(px, py, pc):
            return out_ref.at[pl.ds((4 * px + 2 * py + pc) * m_per, m_per), :]

        def copy(k, blk, to, src=None):
            return pltpu.make_async_remote_copy(
                src_ref=rows(*blk) if src is None else src, dst_ref=rows(*blk), send_sem=send_sems.at[k],
                recv_sem=recv_sems.at[k], device_id=to, device_id_type=MESH)

        mine = pltpu.make_async_copy(x_ref, rows(*me), local_sem)
        mine.start()
        first = [copy(0, me, sibling, src=x_ref)]
        first += [copy(1 + j, me, (*chip, c), src=x_ref) for j, chip in enumerate(chips)]
        for cp in first:
            cp.start()
        passed = [copy(4 + j, (*chip, c), sibling) for j, chip in enumerate(chips)]
        for j, chip in enumerate(chips):
            copy(1 + j, (*chip, c), me).wait_recv()
            passed[j].start()
        copy(0, sibling, me).wait_recv()
        for j, chip in enumerate(chips):
            copy(4 + j, (*chip, 1 - c), me).wait_recv()
        for cp in first + passed:
            cp.wait_send()
        mine.wait()

    return pl.pallas_call(
        body, name=name, out_shape=jax.ShapeDtypeStruct((N_DEV * m_per, n), block.dtype),
        in_specs=[pl.BlockSpec(memory_space=pltpu.VMEM)], out_specs=pl.BlockSpec(memory_space=pltpu.VMEM),
        scratch_shapes=[pltpu.SemaphoreType.DMA((7,)), pltpu.SemaphoreType.DMA((7,)), pltpu.SemaphoreType.DMA],
    )(block)


def _row_tile(R, n):
    budget = 1 << 19
    if R * n <= budget or R % SUBLANES:
        return R
    t = R
    while t * n > budget and t % (2 * SUBLANES) == 0:
        t //= 2
    return t


def _pair_sum(g, recv, c_arr, name):
    _, _, R, n = g.shape
    tr = _row_tile(R, n)

    def body(c_ref, g_ref, r_ref, o_ref):
        o_ref[...] = (g_ref[...] + r_ref[...]).astype(ICI_DTYPE)

    grid_spec = pltpu.PrefetchScalarGridSpec(
        num_scalar_prefetch=1, grid=(N_CHIPS, R // tr),
        in_specs=[pl.BlockSpec((None, None, tr, n), lambda p, i, c: (p, c[0], i, 0)),
                  pl.BlockSpec((None, tr, n), lambda p, i, c: (p, i, 0))],
        out_specs=pl.BlockSpec((None, tr, n), lambda p, i, c: (p, i, 0)))
    return pl.pallas_call(
        body, name=name, grid_spec=grid_spec, out_shape=jax.ShapeDtypeStruct(recv.shape, ICI_DTYPE),
        compiler_params=_cparams("parallel", "parallel"),
    )(c_arr, g, recv)


def _chip_sum(recv, own, where, name):
    _, R, n = recv.shape
    tr = _row_tile(R, n)

    def body(w_ref, r0, r1, r2, r3, own_ref, o_ref):
        me = w_ref[0]
        terms = [jnp.where(me == k, own_ref[...], r[...]).astype(F32) for k, r in enumerate((r0, r1, r2, r3))]
        o_ref[...] = ((terms[0] + terms[1]) + terms[2]) + terms[3]

    def slot(k):
        return pl.BlockSpec((None, tr, n), lambda i, w: (w[1 + k], i, 0))

    grid_spec = pltpu.PrefetchScalarGridSpec(
        num_scalar_prefetch=1, grid=(R // tr,),
        in_specs=[slot(0), slot(1), slot(2), slot(3), pl.BlockSpec((None, tr, n), lambda i, w: (w[0], i, 0))],
        out_specs=pl.BlockSpec((None, tr, n), lambda i, w: (w[5], i, 0)))
    return pl.pallas_call(
        body, name=name, grid_spec=grid_spec, out_shape=jax.ShapeDtypeStruct((2, R, n), F32),
        compiler_params=_cparams("parallel"),
    )(where, recv, recv, recv, recv, own)


ADAM_C1 = 1.0 / (1.0 - ADAM_B1 ** ADAM_STEP)
ADAM_C2 = 1.0 / (1.0 - ADAM_B2 ** ADAM_STEP)


def _adamw_math(w, g, m, v):
    m = ADAM_B1 * m + (1.0 - ADAM_B1) * g
    v = ADAM_B2 * v + (1.0 - ADAM_B2) * (g * g)
    delta = -ADAM_LR * ((m * ADAM_C1) / (jnp.sqrt(v * ADAM_C2) + ADAM_EPS) + ADAM_WD * w)
    return delta, m, v


def _adamw(w, g, m, v, name):
    R, n = w.shape
    tr = _row_tile(R, n)

    def body(w_ref, g_ref, m_ref, v_ref, d_ref, nm_ref, nv_ref):
        d_ref[...], nm_ref[...], nv_ref[...] = _adamw_math(w_ref[...], g_ref[...], m_ref[...], v_ref[...])

    spec = pl.BlockSpec((tr, n), lambda i: (i, 0))
    return pl.pallas_call(
        body, name=name, grid=(R // tr,), in_specs=[spec] * 4, out_specs=[spec] * 3,
        out_shape=[jax.ShapeDtypeStruct((R, n), F32)] * 3, compiler_params=_cparams("parallel"),
    )(w, g, m, v)


def _adamw_rows(w, g, m, v, row0, prev, name):
    R, n = w.shape
    rows = g.shape[0]
    tr = _row_tile(math.gcd(row0, rows), n)
    off = row0 // tr

    def body(*refs):
        w_ref, g_ref, m_ref, v_ref = refs[:4]
        go_ref, d_ref, nm_ref, nv_ref = refs[-4:]
        gv = g_ref[...]
        go_ref[...] = gv
        d_ref[...], nm_ref[...], nv_ref[...] = _adamw_math(w_ref[...], gv, m_ref[...], v_ref[...])

    at = pl.BlockSpec((tr, n), lambda i: (i + off, 0))
    carried = [] if prev is None else list(prev)
    return pl.pallas_call(
        body, name=name, grid=(rows // tr,),
        in_specs=[at, pl.BlockSpec((tr, n), lambda i: (i, 0)), at, at] + [ANY] * len(carried),
        out_specs=[at] * 4, out_shape=[jax.ShapeDtypeStruct((R, n), F32)] * 4,
        input_output_aliases={4 + i: i for i in range(len(carried))},
        compiler_params=_cparams("parallel"),
    )(w, g, m, v, *carried)


def _adamw_small(w, gall, m, v, name):
    M, n = w.shape

    def body(w_ref, g_ref, m_ref, v_ref, gs_ref, d_ref, nm_ref, nv_ref):
        g = g_ref[0:M, :]
        for d in range(1, N_DEV):
            g = g + g_ref[d * M:(d + 1) * M, :]
        gs_ref[...] = g
        d_ref[...], nm_ref[...], nv_ref[...] = _adamw_math(w_ref[...], g, m_ref[...], v_ref[...])

    return pl.pallas_call(
        body, name=name, out_shape=[jax.ShapeDtypeStruct((M, n), F32)] * 4,
    )(w, gall, m, v)


SMALL_ROWS = 24
MID_ROWS = 4


def _pack_small(ln1_g, ln1_b, ln2_g, ln2_b, norm_w, sinks, a_log, dt_bias):
    mixed = jnp.concatenate([norm_w.reshape(-1), sinks.reshape(-1), a_log.reshape(-1), dt_bias.reshape(-1)])
    mixed = jnp.pad(mixed, (0, D_MODEL - mixed.shape[0]))[None]
    pad = jnp.zeros((SMALL_ROWS - 4 * DEPTH - 1, D_MODEL), F32)
    return jnp.concatenate([ln1_g, ln1_b, ln2_g, ln2_b, mixed, pad], axis=0)


def _unpack_small(p):
    mixed = p[4 * DEPTH]
    return (p[0:4], p[4:8], p[8:12], p[12:16], mixed[0:256].reshape(2, 128), mixed[256:272].reshape(2, 8),
            mixed[272:280].reshape(2, 4), mixed[280:288].reshape(2, 4))


def _pack_mid(conv_w, rconv_w, rconv_b, b_a, b_x, lam):
    lead = conv_w.shape[0]
    flat = jnp.concatenate([conv_w.reshape(lead, -1), rconv_w.reshape(lead, -1), rconv_b, b_a, b_x, lam], axis=1)
    return jnp.pad(flat, ((0, 0), (0, MID_ROWS * D_MODEL - flat.shape[1]))).reshape(lead, MID_ROWS, D_MODEL)


def _unpack_mid(p):
    lead = p.shape[:-2]
    f = p.reshape(lead + (MID_ROWS * D_MODEL,))
    return (f[..., 0:1536].reshape(lead + (4, 384)), f[..., 1536:2560].reshape(lead + (4, 256)),
            f[..., 2560:2816], f[..., 2816:3072], f[..., 3072:3328], f[..., 3328:3584])


def _cols_from_chips(g):
    p, L, R, n = g.shape
    return g.transpose(1, 2, 0, 3).reshape(L, R, p * n)


def _rows_from_chips(g):
    p, L, R, n = g.shape
    return g.transpose(1, 0, 2, 3).reshape(L, p * R, n)


def _cols_to_chips(g):
    L, R, n4 = g.shape
    return g.reshape(L, R, N_CHIPS, n4 // N_CHIPS).transpose(2, 0, 1, 3)


def _rows_to_chips(g):
    L, R4, n = g.shape
    return g.reshape(L, N_CHIPS, R4 // N_CHIPS, n).transpose(1, 0, 2, 3)


def _halves(a):
    return a.reshape(2, -1, a.shape[-1])


def _pad_hyb_cols(w):
    z = jnp.zeros(w.shape[:-1] + (HYB_PAD - HP_BG - 2 * B_HEADS,), w.dtype)
    return jnp.concatenate([w[..., 0:512], w[..., 768:2304], w[..., 2304:2816], w[..., 512:768], w[..., 2816:2824], z], axis=-1)


def _unpad_hyb_cols(w):
    return jnp.concatenate([w[..., 0:512], w[..., 2560:2816], w[..., 512:2048], w[..., 2048:2560], w[..., 2816:2824]], axis=-1)


def _hybrid_fwd(x, W, j, tables, sfx, out_ready, ln):
    cos, sin_s = tables
    T = x.shape[0]
    N = T // B_CHUNK
    proj = _matmul(x, W["hyb_w_in"][j], "nn", "hyb_in" + sfx)
    sinks_b = jnp.broadcast_to(W["hyb_sinks"][j][:, None], (A_Q_HEADS, LANES))
    o_a, lse = _attn_fwd(proj, cos, sin_s, sinks_b, "attn_fwd" + sfx)
    bg = proj[:, HP_BG:HP_BG + 2 * B_HEADS]
    beta = jax.nn.sigmoid(bg[:, :B_HEADS])
    pre = bg[:, B_HEADS:] + W["hyb_dt_bias"][j][None]
    g = -jnp.exp(W["hyb_a_log"][j])[None] * jax.nn.softplus(pre)
    grow = g.T.reshape(B_HEADS, N, 1, B_CHUNK)
    brow = beta.T.reshape(B_HEADS, N, 1, B_CHUNK)
    nw = W["hyb_norm_w"][j][None]
    o_b, states, invs, qkvc = _dn_fwd(proj, W["hyb_conv_w"][j], nw, grow, brow, "dn_fwd" + sfx)
    mix = jnp.concatenate([o_a, o_b], axis=1).astype(ACT_DTYPE)
    out_ready(mix)
    out = _matmul(mix, W["hyb_w_out"][j], "nn", "hyb_out" + sfx, epi=_epi_residual_ln, extra=ln, out_dtype=_ln_out(),
                  tm=512)
    res = dict(proj=proj, o_a=o_a, lse=lse, qkvc=qkvc, beta=beta, pre=pre, g=g, grow=grow, brow=brow,
               states=states, invs=invs, mix=mix, sinks_b=sinks_b, nw=nw)
    return out, res


def _hybrid_bwd(x, du, W, j, res, tables, sfx, dx_dtype):
    cos, sin_s = tables
    T = x.shape[0]
    proj = res["proj"]
    d_wout = _matmul(res["mix"], du, "tn", "hyb_out_dw" + sfx)
    dmix = _matmul(du, W["hyb_w_out"][j], "nt", "hyb_out_dx" + sfx)
    dproj, dkc, dkp, dvc, dvp, dsink = _attn_bwd(proj, cos, sin_s, res["sinks_b"], res["o_a"], res["lse"], dmix,
                                                  "attn_bwd" + sfx)
    zpad = jnp.zeros((WINDOW, LANES), F32)
    dk = dkc + jnp.concatenate([dkp[WINDOW:], zpad], axis=0)
    dv = dvc + jnp.concatenate([dvp[WINDOW:], zpad], axis=0)
    dproj, dg4, dbeta4, dnw, dconv = _dn_bwd(res["qkvc"], proj, W["hyb_conv_w"][j], res["nw"], res["grow"], res["brow"],
                                             res["states"], res["invs"], dmix, dproj, "dn_bwd" + sfx)
    dg = dg4.reshape(B_HEADS, T).T
    dbeta = dbeta4.reshape(B_HEADS, T).T
    beta = res["beta"]
    dbeta_logit = dbeta * beta * (1.0 - beta)
    da_logit = dg * (-jnp.exp(W["hyb_a_log"][j]))[None] * jax.nn.sigmoid(res["pre"])
    d_dt_bias = jnp.sum(da_logit, axis=0)
    d_a_log = jnp.sum(dg * res["g"], axis=0)
    zcols = jnp.zeros((T, HYB_PAD - HP_BG - 2 * B_HEADS), F32)
    tail = jnp.concatenate([dk, dv, dbeta_logit, da_logit, zcols], axis=1).astype(ACT_DTYPE)
    dproj = lax.dynamic_update_slice(dproj, tail, (0, HP_K))
    d_win = _matmul(x, dproj, "tn", "hyb_in_dw" + sfx)
    dx = _matmul(dproj, W["hyb_w_in"][j], "nt", "hyb_in_dx" + sfx, epi=_epi_add_residual, extra=du, out_dtype=dx_dtype)
    grads = dict(hyb_w_in=d_win, hyb_w_out=d_wout, hyb_sinks=dsink[0], hyb_conv_w=dconv, hyb_a_log=d_a_log,
                 hyb_dt_bias=d_dt_bias, hyb_norm_w=jnp.sum(dnw[:, 0, :], axis=0))
    return dx, grads


def _rec_fwd(x, W, j, sfx, ln):
    proj = _matmul(x, W["rec_w_in"][j], "nn", "rec_in" + sfx)
    sp = jax.nn.softplus(-W["rec_lambda"][j])[None]
    hg, h = _rglru_fwd(proj, W["rec_conv_w"][j], W["rec_conv_b"][j][None], W["rec_w_a"][j], W["rec_w_x"][j],
                       W["rec_b_a"][j][None], W["rec_b_x"][j][None], sp, "rglru_fwd" + sfx)
    out = _matmul(hg, W["rec_w_out"][j], "nn", "rec_out" + sfx, epi=_epi_residual_ln, extra=ln, out_dtype=_ln_out(),
                  tm=512)
    return out, dict(proj=proj, hg=hg, h=h, sp=sp)


def _rec_bwd(x, du, W, j, res, sfx, dx_dtype):
    d_wout = _matmul(res["hg"], du, "tn", "rec_out_dw" + sfx)
    dhg = _matmul(du, W["rec_w_out"][j], "nt", "rec_out_dx" + sfx)
    dproj, dcw, dcb, dwa, dwx, dba, dbx, dsp = _rglru_bwd(
        res["proj"], W["rec_conv_w"][j], W["rec_conv_b"][j][None], W["rec_w_a"][j], W["rec_w_x"][j],
        W["rec_b_a"][j][None], W["rec_b_x"][j][None], res["sp"], res["h"], dhg, "rglru_bwd" + sfx)
    d_lam = dsp[0] * (-jax.nn.sigmoid(-W["rec_lambda"][j]))
    d_win = _matmul(x, dproj, "tn", "rec_in_dw" + sfx)
    dx = _matmul(dproj, W["rec_w_in"][j], "nt", "rec_in_dx" + sfx, epi=_epi_add_residual, extra=du, out_dtype=dx_dtype)
    grads = dict(rec_w_in=d_win, rec_w_out=d_wout, rec_conv_w=dcw, rec_conv_b=dcb[0], rec_w_a=dwa, rec_w_x=dwx,
                 rec_b_a=dba[0], rec_b_x=dbx[0], rec_lambda=d_lam)
    return dx, grads


def _local_step(x, tgt, W, mlp_w, mixer_ready, on_group):
    T = x.shape[0]
    tables = _rope_tables(T)
    acts = []
    xb = x.astype(ACT_DTYPE)
    for layer in range(DEPTH):
        j, sfx = layer // 2, ""
        last = layer == DEPTH - 1
        mixer_ready(layer, xb)
        ln1 = (xb if layer else x, W["ln1_g"][layer][None], W["ln1_b"][layer][None])
        if layer % 2 == 0:
            (x1b, u1), res = _hybrid_fwd(xb, W, j, tables, sfx,
                                         functools.partial(mixer_ready, layer, out_projection=True), ln1)
        else:
            (x1b, u1), res = _rec_fwd(xb, W, j, sfx, ln1)
        w1, w2, wl = mlp_w(layer, x1b)
        h1 = _matmul(x1b, w1, "nn", "mlp_up", out_dtype=ACT_DTYPE, b_chips=("j", wl))
        out = _matmul(h1, w2, "nn", "mlp_down", a_fn=_relu2, b_chips=("k", wl),
                      epi=_epi_residual_ln_last if last else _epi_residual_ln,
                      extra=(x1b, W["ln2_g"][layer][None], W["ln2_b"][layer][None]), out_dtype=_ln_out(last))
        acts.append(dict(xb=xb, res=res, u1=u1, x1b=x1b, h1=h1, u2=out[-1]))
        xb = out[-2]
    dx, loss = _loss_head(out[0], tgt, "loss_head")
    per_layer = [None] * DEPTH
    d_w1 = dict(lower=lax.empty((N_CHIPS, 1, D_MODEL, D_FF // N_CHIPS), F32),
                upper=lax.empty((N_CHIPS, DEPTH - 1, D_MODEL, D_FF // N_CHIPS), F32))
    d_w2 = dict(lower=lax.empty((N_CHIPS, 1, D_FF // N_CHIPS, D_MODEL), F32),
                upper=lax.empty((N_CHIPS, DEPTH - 1, D_FF // N_CHIPS, D_MODEL), F32))
    token = None
    for layer in reversed(range(DEPTH)):
        j, a = layer // 2, acts[layer]
        grp, li = ("upper", layer - 1) if layer else ("lower", 0)
        ln2_g = W["ln2_g"][layer][None]
        if token is not None:
            ln2_g = ln2_g + token
        du2, dg2, db2 = _ln_bwd(a["u2"], ln2_g, dx, "ln_bwd")
        w1, w2, wl = mlp_w(layer, du2)
        d_w2[grp] = _matmul(a["h1"], du2, "tn", "mlp_down_dw", a_fn=_relu2, out_chips=("i", li, d_w2[grp]))
        dh1 = _matmul(du2, w2, "nt", "mlp_down_dx", epi=_epi_drelu2, extra=a["h1"], out_dtype=ACT_DTYPE,
                      b_chips=("j", wl))
        d_w1[grp] = _matmul(a["x1b"], dh1, "tn", "mlp_up_dw", out_chips=("j", li, d_w1[grp]))
        dx1 = _matmul(dh1, w1, "nt", "mlp_up_dx", epi=_epi_add_residual, extra=du2, b_chips=("k", wl),
                      out_dtype=ACT_DTYPE)
        ln1_g = W["ln1_g"][layer][None]
        token = on_group("middle", dx1, None, None, None) if layer == 0 else None
        if token is not None:
            ln1_g = ln1_g + token
        du1, dg1, db1 = _ln_bwd(a["u1"], ln1_g, dx1, "ln_bwd")
        if layer % 2 == 0:
            dx, g = _hybrid_bwd(a["xb"], du1, W, j, a["res"], tables, "", ACT_DTYPE if layer else F32)
        else:
            dx, g = _rec_bwd(a["xb"], du1, W, j, a["res"], "", ACT_DTYPE)
        g.update(ln1_g=dg1[0], ln1_b=db1[0], ln2_g=dg2[0], ln2_b=db2[0])
        per_layer[layer] = g
        if layer == 1:
            token = on_group("upper", [per_layer[2]], [per_layer[1], per_layer[3]], d_w1["upper"], d_w2["upper"])
        elif layer == 0:
            on_group("lower", [per_layer[0]], [], d_w1["lower"], d_w2["lower"])
    grads = {}
    for name in ("ln1_g", "ln1_b", "ln2_g", "ln2_b"):
        grads[name] = jnp.stack([per_layer[l][name] for l in range(DEPTH)])
    for name in ("hyb_norm_w", "hyb_sinks", "hyb_a_log", "hyb_dt_bias"):
        grads[name] = jnp.stack([per_layer[l][name] for l in (0, 2)])
    return loss, dx, grads


BIG = ("hyb_w_in", "hyb_w_out", "rec_w_in", "rec_w_out", "mlp_w1", "mlp_w2", "rec_w_a", "rec_w_x")
COL_SHARDED = ("hyb_w_in", "rec_w_in", "mlp_w1")
CHIP_MAJOR = ("mlp_w1", "mlp_w2")
MID = ("hyb_conv_w", "rec_conv_w", "rec_conv_b", "rec_b_a", "rec_b_x", "rec_lambda")
SMALL = ("ln1_g", "ln1_b", "ln2_g", "ln2_b", "hyb_norm_w", "hyb_sinks", "hyb_a_log", "hyb_dt_bias")
WEIGHTS = ("hyb_w_in", "hyb_sinks", "hyb_conv_w", "hyb_a_log", "hyb_dt_bias", "hyb_norm_w", "hyb_w_out", "rec_w_in",
           "rec_conv_w", "rec_conv_b", "rec_w_a", "rec_b_a", "rec_w_x", "rec_b_x", "rec_lambda", "rec_w_out", "ln1_g",
           "ln1_b", "mlp_w1", "mlp_w2", "ln2_g", "ln2_b")


def _gather_full_weights(w):
    wb = {k: w[k].astype(MXU_DTYPE) for k in BIG}
    now = ("hyb_w_in",)
    shards = [_halves(wb[k][:1]) for k in now]
    shards.append(_pack_mid(*[w[k] for k in MID]))
    got = _all_gather_weights(shards, "all_gather_weights")
    me = 2 * lax.axis_index("x") + lax.axis_index("y")
    got = [lax.dynamic_update_slice(g, s[None], (me, 0, 0, 0)) for s, g in zip(shards, got)]

    def full(k, g):
        g = g.reshape((N_CHIPS,) + w[k].shape[1:])
        if k in CHIP_MAJOR:
            return g[:, None]
        if k in ("rec_w_a", "rec_w_x"):
            return g.transpose(1, 0, 2, 3).reshape(LRU_BLOCKS, LRU_BLOCK_W, LRU_BLOCK_W)
        f = _cols_from_chips(g[:, None])[0] if k in COL_SHARDED else _rows_from_chips(g[:, None])[0]
        return _pad_hyb_cols(f) if k == "hyb_w_in" else f

    rec = ("rec_w_in", "rec_w_out", "rec_w_a", "rec_w_x")
    groups = [(("hyb_w_out",), 0), (CHIP_MAJOR, 0), (rec, 0), (CHIP_MAJOR, 1), (("hyb_w_in", "hyb_w_out"), 1),
              (CHIP_MAJOR, 2), (rec, 1), (CHIP_MAJOR, 3)]
    own = [wb[k][j] for names, j in groups for k in names]
    land = [lax.dynamic_update_slice(lax.empty((N_CHIPS,) + o.shape, o.dtype), o[None], (me,) + (0,) * o.ndim)
            for o in own]
    send_sems, recv_sems, own, land = _gather_start(own, land, got[-1], "gather_start")
    W = {k: [None] * w[k].shape[0] for k in BIG}
    for k, g in zip(now, got[:-1]):
        W[k][0] = full(k, g)
    arrived = [0]

    def ensure(upto, after):
        while arrived[0] <= upto:
            gi = arrived[0]
            names, j = groups[gi]
            lo = sum(len(nm) for nm, _ in groups[:gi])
            sl = slice(lo, lo + len(names))
            got_g = _gather_wait(send_sems[sl], recv_sems[sl], own[sl], land[sl], after, "gather_wait_%d" % gi)
            for k, g in zip(names, got_g):
                W[k][j] = full(k, g)
            arrived[0] += 1

    def mixer_ready(layer, after, out_projection=False):
        if layer:
            ensure({1: 2, 2: 4, 3: 6}[layer], after)
        elif out_projection:
            ensure(0, after)

    def mlp_w(layer, after):
        ensure({0: 1, 1: 3, 2: 5, 3: 7}[layer], after)
        return W["mlp_w1"][layer], W["mlp_w2"][layer], 0

    conv_w, rconv_w, rconv_b, b_a, b_x, lam = _unpack_mid(got[-1])
    W["hyb_conv_w"] = conv_w.transpose(1, 2, 0, 3).reshape(2, CONV_K, 3 * B_W)
    W["rec_conv_w"] = rconv_w.transpose(1, 2, 0, 3).reshape(2, CONV_K, D_MODEL)
    for k, v in (("rec_conv_b", rconv_b), ("rec_b_a", b_a), ("rec_b_x", b_x), ("rec_lambda", lam)):
        W[k] = v.transpose(1, 0, 2).reshape(2, D_MODEL)
    for k in SMALL:
        W[k] = w[k]
    return W, mlp_w, mixer_ready


REC_VECTORS = ("rec_conv_b", "rec_b_a", "rec_b_x", "rec_lambda")


def _group_by_chip(hyb, rec, d_w1, d_w2):
    t = dict(mlp_w1=d_w1, mlp_w2=d_w2)
    if hyb:
        t["hyb_w_in"] = _cols_to_chips(jnp.stack([_unpad_hyb_cols(g["hyb_w_in"]) for g in hyb]))
        t["hyb_w_out"] = _rows_to_chips(jnp.stack([g["hyb_w_out"] for g in hyb]))
    if rec:
        t["rec_w_in"] = _cols_to_chips(jnp.stack([g["rec_w_in"] for g in rec]))
        t["rec_w_out"] = _rows_to_chips(jnp.stack([g["rec_w_out"] for g in rec]))
        for k in ("rec_w_a", "rec_w_x"):
            v = jnp.stack([g[k] for g in rec])
            t[k] = v.reshape(len(rec), LRU_BLOCKS, N_CHIPS, LRU_BLOCK_W // N_CHIPS, LRU_BLOCK_W).transpose(2, 0, 1, 3, 4)
    flat = [g["hyb_conv_w"].reshape(CONV_K, N_CHIPS, -1).transpose(1, 0, 2).reshape(N_CHIPS, -1) for g in hyb]
    for g in rec:
        flat.append(g["rec_conv_w"].reshape(CONV_K, N_CHIPS, -1).transpose(1, 0, 2).reshape(N_CHIPS, -1))
        flat += [g[k].reshape(N_CHIPS, -1) for k in REC_VECTORS]
    flat = jnp.concatenate(flat, axis=1)
    flat = jnp.pad(flat, ((0, 0), (0, -flat.shape[1] % (2 * D_MODEL))))
    names = [k for k in BIG if k in t]
    out = [t[k].reshape(N_CHIPS, 2, -1, t[k].shape[-1]) for k in names]
    return names + ["small"], out + [flat.reshape(N_CHIPS, 2, -1, D_MODEL)]


def kernel(x, hyb_w_in, hyb_sinks, hyb_conv_w, hyb_a_log, hyb_dt_bias, hyb_norm_w, hyb_w_out, rec_w_in, rec_conv_w, rec_conv_b, rec_w_a, rec_b_a, rec_w_x, rec_b_x, rec_lambda, rec_w_out, ln1_g, ln1_b, mlp_w1, mlp_w2, ln2_g, ln2_b, loss_target, m_hyb_w_in, m_hyb_sinks, m_hyb_conv_w, m_hyb_a_log, m_hyb_dt_bias, m_hyb_norm_w, m_hyb_w_out, m_rec_w_in, m_rec_conv_w, m_rec_conv_b, m_rec_w_a, m_rec_b_a, m_rec_w_x, m_rec_b_x, m_rec_lambda, m_rec_w_out, m_ln1_g, m_ln1_b, m_mlp_w1, m_mlp_w2, m_ln2_g, m_ln2_b, v_hyb_w_in, v_hyb_sinks, v_hyb_conv_w, v_hyb_a_log, v_hyb_dt_bias, v_hyb_norm_w, v_hyb_w_out, v_rec_w_in, v_rec_conv_w, v_rec_conv_b, v_rec_w_a, v_rec_b_a, v_rec_w_x, v_rec_b_x, v_rec_lambda, v_rec_w_out, v_ln1_g, v_ln1_b, v_mlp_w1, v_mlp_w2, v_ln2_g, v_ln2_b):
    args = locals()
    w = {k: args[k] for k in WEIGHTS}
    m = {k: args["m_" + k] for k in WEIGHTS}
    v = {k: args["v_" + k] for k in WEIGHTS}

    W, mlp_w, mixer_ready = _gather_full_weights(w)

    core = lax.axis_index("c").astype(jnp.int32)
    me = (2 * lax.axis_index("x") + lax.axis_index("y")).astype(jnp.int32)
    slots = jnp.arange(N_CHIPS, dtype=jnp.int32)
    where = jnp.concatenate([me[None], jnp.where(slots == me, (slots + 1) % N_CHIPS, slots), core[None]])
    state = {}

    def on_group(group, hyb, rec, d_w1, d_w2):
        if group == "middle":
            group = "upper"
            send_sems, recv_sems, by_chip, recv, _ = state["sibling"]
            by_chip, from_sibling = _rs_sibling_wait(send_sems, recv_sems, by_chip, recv, hyb, "rs_sibling_wait")
        else:
            state[group + " names"], by_chip = _group_by_chip(hyb, rec, d_w1, d_w2)
            if group == "upper":
                recv = [lax.empty((N_CHIPS,) + g.shape[2:], g.dtype) for g in by_chip]
                state["sibling"] = _rs_sibling_start(by_chip, recv, "rs_sibling_start")
                return state["sibling"][4][0, 0]
            from_sibling = _rs_to_sibling(by_chip, "rs_to_sibling")
        pair = [_pair_sum(g, r, core[None], "pair_sum") for g, r in zip(by_chip, from_sibling)]
        recv = [lax.empty(p.shape, p.dtype) for p in pair]
        state[group] = _rs_across_start(pair, recv, "rs_across_start_" + group)
        return state[group][4][0, 0]

    def finish_group(group, after, prev):
        send_sems, recv_sems, pair, recv, _ = state[group]
        pair, from_chips = _rs_across_wait(send_sems, recv_sems, pair, recv, after, "rs_across_wait_" + group)
        half = [_chip_sum(r, p, where, "chip_sum") for r, p in zip(from_chips, pair)]
        joined = _rs_join_halves(half, "rs_join_halves")
        outs = dict(prev or {})
        for k, g in zip(state[group + " names"][:-1], joined[:-1]):
            n = g.shape[-1]
            w2, g2 = w[k].reshape(-1, n), g.reshape(-1, n)
            row0 = w2.shape[0] - g2.shape[0] if group == "upper" else 0
            outs[k] = _adamw_rows(w2, g2, m[k].reshape(-1, n), v[k].reshape(-1, n), row0, outs.get(k), "adamw")
        return outs, joined[-1].reshape(-1)

    loss, dx, grads = _local_step(x[0], loss_target[0], W, mlp_w, mixer_ready, on_group)
    loss = lax.psum(loss[0, 0], ("x", "y", "c"))
    g_out, d_out, m_out, v_out = {}, {}, {}, {}

    upper, small_upper = finish_group("upper", state["lower"][4], None)
    small_g = _pack_small(*[grads[k] for k in SMALL])
    small_all = _all_gather_small(small_g, "all_gather_small")
    sw, sm, sv = (_pack_small(*[t[k] for k in SMALL]) for t in (w, m, v))
    sg, sd, snm, snv = _adamw_small(sw, small_all, sm, sv, "adamw_small")
    for dst, packed in ((g_out, sg), (d_out, sd), (m_out, snm), (v_out, snv)):
        for k, val in zip(SMALL, _unpack_small(packed)):
            dst[k] = val

    done = sum([upper[k][1][0, 0] for k in BIG], sg[0, 0]).reshape(1, 1)
    both, small_lower = finish_group("lower", done, upper)
    for k in BIG:
        shape = w[k].shape
        g_out[k], d_out[k], m_out[k], v_out[k] = (t.reshape(shape) for t in both[k])
    n_conv, n_rconv, n_vec = (w[k][0].size for k in ("hyb_conv_w", "rec_conv_w", "rec_conv_b"))
    n_rec = n_rconv + len(REC_VECTORS) * n_vec
    conv_g = jnp.stack([small_lower[:n_conv], small_upper[:n_conv]])
    rec_g = jnp.stack([small_upper[n_conv + i * n_rec:n_conv + (i + 1) * n_rec] for i in range(2)])
    vec_g = [rec_g[:, n_rconv + i * n_vec:n_rconv + (i + 1) * n_vec] for i in range(len(REC_VECTORS))]
    mid_g = _pack_mid(conv_g, rec_g[:, :n_rconv], *vec_g).reshape(-1, D_MODEL)
    mid_w, mid_m, mid_v = (_pack_mid(*[t[k] for k in MID]).reshape(-1, D_MODEL) for t in (w, m, v))
    mid_d, mid_nm, mid_nv = _adamw(mid_w, mid_g, mid_m, mid_v, "adamw_mid")
    for dst, packed in ((g_out, mid_g), (d_out, mid_d), (m_out, mid_nm), (v_out, mid_nv)):
        for k, val in zip(MID, _unpack_mid(packed.reshape(2, MID_ROWS, D_MODEL))):
            dst[k] = val.reshape(w[k].shape)

    return (loss, dx[None], *[g_out[k] for k in WEIGHTS], *[d_out[k] for k in WEIGHTS],
            *[m_out[k] for k in WEIGHTS], *[v_out[k] for k in WEIGHTS])
```

```python
import functools
import math

import jax
import jax.numpy as jnp
import numpy as np
from jax import lax
from jax.experimental import pallas as pl
from jax.experimental.pallas import tpu as pltpu

F32 = jnp.float32
MXU_DTYPE = jnp.bfloat16
ACT_DTYPE = jnp.bfloat16
ICI_DTYPE = jnp.bfloat16

D_MODEL = 1024
DEPTH = 4
A_HEAD_DIM = 64
A_Q_HEADS = 8
A_KV_HEADS = 2
WINDOW = 128
ROPE_THETA = 10000.0
B_HEADS = 4
B_HEAD_DIM = 128
B_CHUNK = 64
CONV_K = 4
LRU_BLOCKS = 4
LRU_BLOCK_W = D_MODEL // LRU_BLOCKS
LRU_C = 8.0
D_FF = 4 * D_MODEL
A_Q_W = A_Q_HEADS * A_HEAD_DIM
A_KV_W = A_KV_HEADS * A_HEAD_DIM
B_W = B_HEADS * B_HEAD_DIM
HYB_PROJ = A_Q_W + 2 * A_KV_W + 4 * B_W + 2 * B_HEADS
DN_ALPHA = (2 * DEPTH) ** 0.25
LN_EPS = 1e-5
NORM_EPS = 1e-6
ADAM_LR = 0.001
ADAM_B1 = 0.9
ADAM_B2 = 0.999
ADAM_EPS = 1e-08
ADAM_WD = 0.01
ADAM_STEP = 10

HP_Q = 0
HP_QKVB = 512
HP_Z = 2048
HP_K = 2560
HP_V = 2688
HP_BG = 2816
HYB_PAD = 3072

N_CHIPS = 4
N_DEV = 8
V7X_VMEM_LIMIT = 48 * 1024 * 1024
LANES = 128
SUBLANES = 8
NEG_BIG = -1e30

NN = (((1,), (0,)), ((), ()))
NT = (((1,), (1,)), ((), ()))
TN = (((0,), (0,)), ((), ()))


def _cparams(*sem):
    return pltpu.CompilerParams(dimension_semantics=sem, vmem_limit_bytes=V7X_VMEM_LIMIT)


def _dot(a, b, dims=NN):
    return lax.dot_general(a.astype(MXU_DTYPE), b.astype(MXU_DTYPE), dims, preferred_element_type=F32)


def _split_bf16(a):
    hi = a.astype(jnp.bfloat16)
    return hi, (a - hi.astype(F32)).astype(jnp.bfloat16)


def _dotf(a, b, dims=NN):
    ah, al = _split_bf16(a)
    bh, bl = _split_bf16(b)
    dg = functools.partial(lax.dot_general, dimension_numbers=dims, preferred_element_type=F32)
    return dg(ah, bh) + (dg(ah, bl) + dg(al, bh))


def _tile(dim, pref):
    t = min(dim, pref)
    while dim % t:
        t //= 2
    return t


def _sigmoid(x):
    return 1.0 / (1.0 + jnp.exp(-x))


def _silu(x):
    return x * _sigmoid(x)


def _dsilu(x):
    s = _sigmoid(x)
    return s * (1.0 + x * (1.0 - s))


GELU_C = 0.7978845608028654
GELU_A = 0.044715


def _gelu(x):
    return 0.5 * x * (1.0 + jnp.tanh(GELU_C * (x + GELU_A * x * x * x)))


def _dgelu(x):
    t = jnp.tanh(GELU_C * (x + GELU_A * x * x * x))
    return 0.5 * (1.0 + t) + 0.5 * x * (1.0 - t * t) * GELU_C * (1.0 + 3.0 * GELU_A * x * x)


def _matmul(a, b, mode, name, *, tm=1024, tn=1024, tk=1024, a_fn=None, epi=None, extra=None, out_dtype=F32,
            b_chips=None, out_chips=None):
    chunk = tk
    if mode == "tn":
        K, M = a.shape
        if K <= 4 * tk:
            tk, tm = K, tm // 2
    else:
        M, K = a.shape
    whole_k = False
    if b_chips is not None:
        g, b_layer = b_chips
        r, n = b.shape[2:]
        n_dim, k_dim = (r, n) if mode == "nt" else (n, r)
        N = N_CHIPS * n_dim if g == "j" else n_dim
        assert K == (N_CHIPS * k_dim if g == "k" else k_dim)
        if g == "j":
            tn = n_dim
        else:
            whole_k, tk, tm = True, K, tm // 2
    elif mode == "nt":
        N = b.shape[0]
        if tk < K <= 3 * tk:
            tk, tm = K, tm // 2
    else:
        N = b.shape[1]
    if out_chips is not None:
        og, o_layer, o_buf = out_chips
        if og == "j":
            tn = o_buf.shape[3]
        else:
            tm = min(tm, o_buf.shape[2])
    tm, tn, tk = _tile(M, tm), _tile(N, tn), _tile(K, tk)
    nk = K // tk
    if mode == "tn":
        a_spec = pl.BlockSpec((tk, tm), lambda i, j, k: (k, i))
    else:
        a_spec = pl.BlockSpec((tm, tk), lambda i, j, k: (i, k))
    b_block = (tn, tk) if mode == "nt" else (tk, tn)
    if b_chips is None:
        b_spec = pl.BlockSpec(b_block, (lambda i, j, k: (j, k)) if mode == "nt" else (lambda i, j, k: (k, j)))
    elif mode == "nt" and whole_k:
        b_spec = pl.BlockSpec((N_CHIPS, None, tn, k_dim), lambda i, j, k: (0, b_layer, j, 0))
    elif mode == "nt":
        b_spec = pl.BlockSpec((None, None) + b_block, lambda i, j, k: (j, b_layer, 0, k))
    elif whole_k:
        b_spec = pl.BlockSpec((N_CHIPS, None, k_dim, tn), lambda i, j, k: (0, b_layer, 0, j),
                              pipeline_mode=pl.Buffered(1) if N == tn else None)
    else:
        b_spec = pl.BlockSpec((None, None) + b_block, lambda i, j, k: (j, b_layer, k, 0))
    o_spec = pl.BlockSpec((tm, tn), lambda i, j, k: (i, j))
    e_spec = o_spec
    if out_chips is not None:
        per = o_buf.shape[2] // tm
        o_spec = pl.BlockSpec((None, None, tm, tn), (lambda i, j, k: (j, o_layer, i, 0)) if og == "j"
                              else (lambda i, j, k: (i // per, o_layer, i % per, j)))
    dims = {"nn": NN, "nt": NT, "tn": TN}[mode]
    extras = () if extra is None else (extra if isinstance(extra, tuple) else (extra,))
    out_dtypes = out_dtype if isinstance(out_dtype, tuple) else (out_dtype,)
    n_in = 2 + len(extras) + (out_chips is not None)

    def body(*refs):
        a_ref, b_ref = refs[0], refs[1]
        e_refs = refs[2:2 + len(extras)]
        o_refs = refs[n_in:n_in + len(out_dtypes)]
        av = a_ref[...]
        if a_fn is not None:
            av = a_fn(av)
        if whole_k and mode == "nt":
            part = _dot(av[:, :k_dim], b_ref[0], dims)
            for chip in range(1, N_CHIPS):
                part = part + _dot(av[:, chip * k_dim:(chip + 1) * k_dim], b_ref[chip], dims)
        elif mode == "tn" and tk > chunk:
            part = _dot(av[:chunk], b_ref[0:chunk, :], dims)
            for c0 in range(chunk, tk, chunk):
                part = part + _dot(av[c0:c0 + chunk], b_ref[c0:c0 + chunk, :], dims)
        else:
            bv = b_ref[...]
            if whole_k:
                bv = bv.reshape(K, tn)
            part = _dot(av, bv, dims)

        def finish(acc):
            if epi is not None:
                acc = epi(acc, *[e[...] for e in e_refs])
            for o_ref, val, dt in zip(o_refs, acc if isinstance(acc, tuple) else (acc,), out_dtypes):
                o_ref[...] = val.astype(dt)

        if nk == 1:
            finish(part)
        else:
            acc_ref = refs[-1]
            k = pl.program_id(2)

            @pl.when(k == 0)
            def _():
                acc_ref[...] = part

            @pl.when(k > 0)
            def _():
                acc_ref[...] += part

            @pl.when(k == nk - 1)
            def _():
                finish(acc_ref[...])

    row_spec = pl.BlockSpec((1, tn), lambda i, j, k: (0, j))
    in_specs = [a_spec, b_spec] + [row_spec if e.shape[0] == 1 else e_spec for e in extras]
    args = (a, b) + extras
    out_shape = [jax.ShapeDtypeStruct((M, N), dt) for dt in out_dtypes]
    out_specs = [o_spec] * len(out_dtypes)
    aliases = {}
    if out_chips is not None:
        in_specs.append(pl.BlockSpec(memory_space=pl.ANY))
        args += (o_buf,)
        out_shape = [jax.ShapeDtypeStruct(o_buf.shape, o_buf.dtype)]
        aliases = {n_in - 1: 0}
    if not isinstance(out_dtype, tuple):
        out_shape, out_specs = out_shape[0], out_specs[0]
    return pl.pallas_call(
        body, name=name, grid=(M // tm, N // tn, nk), in_specs=in_specs, out_specs=out_specs, out_shape=out_shape,
        input_output_aliases=aliases,
        scratch_shapes=[pltpu.VMEM((tm, tn), F32)] if nk > 1 else [],
        compiler_params=_cparams("parallel", "parallel", "arbitrary"),
    )(*args)


def _relu2(v):
    r = jnp.maximum(v, 0.0)
    return r * r


def _epi_drelu2(acc, h):
    return acc * (2.0 * jnp.maximum(h, 0.0))


def _epi_add_residual(acc, du):
    return acc + DN_ALPHA * du


def _epi_residual_ln(acc, x, g, b):
    u = DN_ALPHA * x + acc
    mu = jnp.mean(u, axis=-1, keepdims=True)
    d = u - mu
    var = jnp.mean(d * d, axis=-1, keepdims=True)
    return d * lax.rsqrt(var + LN_EPS) * g + b, u


def _ln_out():
    return (ACT_DTYPE, F32)


def _ln_bwd(u, g, dout, name, loss_of=None):
    T, D = u.shape
    tr = _tile(T, 512)
    with_loss = loss_of is not None

    def body(*refs):
        if with_loss:
            u_ref, g_ref, b_ref, t_ref, dub_ref, dg_ref, db_ref, l_ref = refs
        else:
            u_ref, g_ref, d_ref, dub_ref, dg_ref, db_ref = refs
        i = pl.program_id(0)
        u = u_ref[...]
        mu = jnp.mean(u, axis=-1, keepdims=True)
        d = u - mu
        rstd = lax.rsqrt(jnp.mean(d * d, axis=-1, keepdims=True) + LN_EPS)
        xhat = d * rstd
        if with_loss:
            err = xhat * g_ref[...] + b_ref[...] - t_ref[...]
            dout_v = err * (1.0 / D)
            part = jnp.sum(err * err, axis=(0, 1), keepdims=True) * (0.5 / D)

            @pl.when(i == 0)
            def _():
                l_ref[...] = part

            @pl.when(i > 0)
            def _():
                l_ref[...] += part
        else:
            dout_v = d_ref[...]
        dxh = dout_v * g_ref[...]
        m1 = jnp.mean(dxh, axis=-1, keepdims=True)
        m2 = jnp.mean(dxh * xhat, axis=-1, keepdims=True)
        du = rstd * (dxh - m1 - xhat * m2)
        dub_ref[...] = du.astype(ACT_DTYPE)
        pg = jnp.sum(dout_v * xhat, axis=0, keepdims=True)
        pb = jnp.sum(dout_v, axis=0, keepdims=True)

        @pl.when(i == 0)
        def _():
            dg_ref[...] = pg
            db_ref[...] = pb

        @pl.when(i > 0)
        def _():
            dg_ref[...] += pg
            db_ref[...] += pb

    row = pl.BlockSpec((tr, D), lambda i: (i, 0))
    vec = pl.BlockSpec((1, D), lambda i: (0, 0))
    out_specs = [row, vec, vec]
    out_shape = [jax.ShapeDtypeStruct((T, D), ACT_DTYPE), jax.ShapeDtypeStruct((1, D), F32),
                 jax.ShapeDtypeStruct((1, D), F32)]
    if with_loss:
        return pl.pallas_call(
            body, name=name, grid=(T // tr,), in_specs=[row, vec, vec, row],
            out_specs=out_specs + [pl.BlockSpec((1, 1), lambda i: (0, 0))],
            out_shape=out_shape + [jax.ShapeDtypeStruct((1, 1), F32)], compiler_params=_cparams("arbitrary"),
        )(u, g, loss_of[0], loss_of[1])
    return pl.pallas_call(
        body, name=name, grid=(T // tr,), in_specs=[row, vec, row], out_specs=out_specs, out_shape=out_shape,
        compiler_params=_cparams("arbitrary"),
    )(u, g, dout)


def _swap_half(x):
    n = x.shape[-1]
    lane = lax.broadcasted_iota(jnp.int32, x.shape, 1)
    first = (lane % A_HEAD_DIM) < (A_HEAD_DIM // 2)
    return jnp.where(first, pltpu.roll(x, n - A_HEAD_DIM // 2, axis=1), pltpu.roll(x, A_HEAD_DIM // 2, axis=1))


def _rope(x, cos, sin_signed):
    return x * cos + _swap_half(x) * sin_signed


def _rope_t(dy, cos, sin_signed):
    return dy * cos + _swap_half(dy * sin_signed)


def _rope_tables(T):
    half = A_HEAD_DIM // 2
    inv_freq = np.float32(ROPE_THETA) ** (-np.arange(half, dtype=np.float32) / np.float32(half))
    ang = np.arange(T, dtype=np.float32)[:, None] * inv_freq[None, :]
    cos = np.tile(np.cos(ang), (1, 4))
    sin = np.sin(ang)
    sin_signed = np.tile(np.concatenate([-sin, sin], axis=1), (1, 2))
    return jnp.asarray(cos, F32), jnp.asarray(sin_signed, F32)


def _band_mask(n):
    qi = lax.broadcasted_iota(jnp.int32, (WINDOW, 2 * WINDOW), 0)
    kj = lax.broadcasted_iota(jnp.int32, (WINDOW, 2 * WINDOW), 1)
    return (kj > qi) & (kj <= qi + WINDOW) & ((n > 0) | (kj >= WINDOW))


def _place(v, src_half, dst_half):
    lane = lax.broadcasted_iota(jnp.int32, v.shape, 1)
    if src_half != dst_half:
        v = pltpu.roll(v, A_HEAD_DIM, axis=1)
    keep = (lane >= A_HEAD_DIM) if dst_half else (lane < A_HEAD_DIM)
    return jnp.where(keep, v, 0.0)


def _attn_specs():
    kb, vb = HP_K // LANES, HP_V // LANES
    prev = lambda n: jnp.maximum(n - 1, 0)
    return dict(
        q=pl.BlockSpec((WINDOW, A_Q_W), lambda n: (n, 0)),
        kc=pl.BlockSpec((WINDOW, LANES), lambda n: (n, kb)),
        kp=pl.BlockSpec((WINDOW, LANES), lambda n: (prev(n), kb)),
        vc=pl.BlockSpec((WINDOW, LANES), lambda n: (n, vb)),
        vp=pl.BlockSpec((WINDOW, LANES), lambda n: (prev(n), vb)),
        tq=pl.BlockSpec((WINDOW, LANES), lambda n: (n, 0)),
        tp=pl.BlockSpec((WINDOW, LANES), lambda n: (prev(n), 0)),
        sink=pl.BlockSpec((A_Q_HEADS, LANES), lambda n: (0, 0)),
        row512=pl.BlockSpec((WINDOW, A_Q_W), lambda n: (n, 0)),
        row128=pl.BlockSpec((WINDOW, LANES), lambda n: (n, 0)),
        lse=pl.BlockSpec((WINDOW, A_Q_HEADS), lambda n: (n, 0)),
    )


def _attn_fwd(proj, cos, sin_s, sinks_b, name):
    T = proj.shape[0]
    sp = _attn_specs()

    def body(q_ref, kc_ref, kp_ref, vc_ref, vp_ref, cq_ref, sq_ref, cp_ref, sp_ref, sink_ref, o_ref, l_ref):
        n = pl.program_id(0)
        cq, sq = cq_ref[...], sq_ref[...]
        cq4, sq4 = jnp.tile(cq, (1, A_Q_W // LANES)), jnp.tile(sq, (1, A_Q_W // LANES))
        q = _rope(q_ref[...], cq4, sq4) * (A_HEAD_DIM ** -0.5)
        kc = _rope(kc_ref[...], cq, sq)
        kp = _rope(kp_ref[...], cp_ref[...], sp_ref[...])
        kk = jnp.concatenate([kp, kc], axis=0)
        vv = jnp.concatenate([vp_ref[...], vc_ref[...]], axis=0)
        mask = _band_mask(n)[None]
        lane = lax.broadcasted_iota(jnp.int32, (WINDOW, LANES), 1)
        lane8 = lax.broadcasted_iota(jnp.int32, (WINDOW, A_Q_HEADS), 1)
        qe = jnp.stack([_place(q[:, (hq // 2) * LANES:(hq // 2 + 1) * LANES], hq % 2, hq // 4) for hq in range(A_Q_HEADS)])
        kk8 = jnp.broadcast_to(kk[None], (A_Q_HEADS,) + kk.shape)
        vv8 = jnp.broadcast_to(vv[None], (A_Q_HEADS,) + vv.shape)
        sk = jnp.stack([sink_ref[hq:hq + 1, 0:1] for hq in range(A_Q_HEADS)])
        s = jnp.where(mask, _dot(qe, kk8, BNT), NEG_BIG)
        m = jnp.maximum(jnp.max(s, axis=-1, keepdims=True), sk)
        p = jnp.exp(s - m)
        den = jnp.sum(p, axis=-1, keepdims=True) + jnp.exp(sk - m)
        o = _dot(p * (1.0 / den), vv8, BNN)
        lse_h = m + jnp.log(den)
        outs = []
        lse = jnp.zeros((WINDOW, A_Q_HEADS), F32)
        for pb in range(A_Q_HEADS // 2):
            halves = [_place(o[2 * pb + e], pb // 2, e) for e in range(2)]
            outs.append(jnp.where(lane < A_HEAD_DIM, halves[0], halves[1]))
            for e in range(2):
                lse = jnp.where(lane8 == 2 * pb + e, lse_h[2 * pb + e], lse)
        o_ref[...] = jnp.concatenate(outs, axis=1)
        l_ref[...] = lse

    return pl.pallas_call(
        body, name=name, grid=(T // WINDOW,),
        in_specs=[sp["q"], sp["kc"], sp["kp"], sp["vc"], sp["vp"], sp["tq"], sp["tq"], sp["tp"], sp["tp"], sp["sink"]],
        out_specs=[sp["row512"], sp["lse"]],
        out_shape=[jax.ShapeDtypeStruct((T, A_Q_W), F32), jax.ShapeDtypeStruct((T, A_Q_HEADS), F32)],
        compiler_params=_cparams("parallel"),
    )(proj, proj, proj, proj, proj, cos, sin_s, cos, sin_s, sinks_b)


def _attn_bwd(proj, cos, sin_s, sinks_b, o, lse, dmix, name):
    T = proj.shape[0]
    sp = _attn_specs()

    def body(q_ref, kc_ref, kp_ref, vc_ref, vp_ref, cq_ref, sq_ref, cp_ref, sp_ref, sink_ref, o_ref, l_ref, do_ref,
             dq_ref, dkc_ref, dkp_ref, dvc_ref, dvp_ref, dsink_ref):
        n = pl.program_id(0)
        cq, sq = cq_ref[...], sq_ref[...]
        cp, sps = cp_ref[...], sp_ref[...]
        cq4, sq4 = jnp.tile(cq, (1, A_Q_W // LANES)), jnp.tile(sq, (1, A_Q_W // LANES))
        q = _rope(q_ref[...], cq4, sq4) * (A_HEAD_DIM ** -0.5)
        kc = _rope(kc_ref[...], cq, sq)
        kp = _rope(kp_ref[...], cp, sps)
        kk = jnp.concatenate([kp, kc], axis=0)
        vv = jnp.concatenate([vp_ref[...], vc_ref[...]], axis=0)
        mask = _band_mask(n)[None]
        lane = lax.broadcasted_iota(jnp.int32, (WINDOW, LANES), 1)
        do_all, o_all, l_all = do_ref[...], o_ref[...], l_ref[...]
        lane8 = lax.broadcasted_iota(jnp.int32, (WINDOW, A_Q_HEADS), 1)
        head8 = lax.broadcasted_iota(jnp.int32, (1, A_Q_HEADS), 1)
        prod = do_all * o_all
        qes, does, deltas, lhs = [], [], [], []
        for hq in range(A_Q_HEADS):
            pb, e, kvh = hq // 2, hq % 2, hq // 4
            blk = slice(pb * LANES, (pb + 1) * LANES)
            in_half = (lane >= A_HEAD_DIM) if e else (lane < A_HEAD_DIM)
            deltas.append(jnp.sum(jnp.where(in_half, prod[:, blk], 0.0), axis=-1, keepdims=True))
            qes.append(_place(q[:, blk], e, kvh))
            does.append(_place(do_all[:, blk], e, kvh))
            lhs.append(jnp.sum(jnp.where(lane8 == hq, l_all, 0.0), axis=-1, keepdims=True))
        qe, doe, delta, lh = jnp.stack(qes), jnp.stack(does), jnp.stack(deltas), jnp.stack(lhs)
        kk8 = jnp.broadcast_to(kk[None], (A_Q_HEADS,) + kk.shape)
        vv8 = jnp.broadcast_to(vv[None], (A_Q_HEADS,) + vv.shape)
        sk = jnp.stack([sink_ref[hq:hq + 1, 0:1] for hq in range(A_Q_HEADS)])
        s = _dot(qe, kk8, BNT)
        p = jnp.where(mask, jnp.exp(jnp.where(mask, s, NEG_BIG) - lh), 0.0)
        dvv = jnp.sum(_dot(p, doe, BTN), axis=0)
        ds = p * (_dot(doe, vv8, BNT) - delta)
        dkk = jnp.sum(_dot(ds, qe, BTN), axis=0)
        dqe = _dot(ds, kk8, BNN)
        dsink_h = -jnp.sum(jnp.exp(sk - lh) * delta, axis=(1, 2), keepdims=True)
        dqs = []
        dsk = jnp.zeros((1, A_Q_HEADS), F32)
        for pb in range(A_Q_HEADS // 2):
            halves = [_place(dqe[2 * pb + e], pb // 2, e) for e in range(2)]
            dqs.append(jnp.where(lane < A_HEAD_DIM, halves[0], halves[1]))
            for e in range(2):
                dsk = jnp.where(head8 == 2 * pb + e, dsink_h[2 * pb + e], dsk)
        dq = jnp.concatenate(dqs, axis=1) * (A_HEAD_DIM ** -0.5)
        dq_ref[...] = _rope_t(dq, cq4, sq4).astype(ACT_DTYPE)
        dkp_ref[...] = _rope_t(dkk[:WINDOW], cp, sps)
        dkc_ref[...] = _rope_t(dkk[WINDOW:], cq, sq)
        dvp_ref[...] = dvv[:WINDOW]
        dvc_ref[...] = dvv[WINDOW:]

        @pl.when(n == 0)
        def _():
            dsink_ref[...] = dsk

        @pl.when(n > 0)
        def _():
            dsink_ref[...] += dsk

    return pl.pallas_call(
        body, name=name, grid=(T // WINDOW,),
        in_specs=[sp["q"], sp["kc"], sp["kp"], sp["vc"], sp["vp"], sp["tq"], sp["tq"], sp["tp"], sp["tp"], sp["sink"],
                  sp["row512"], sp["lse"], sp["row512"]],
        out_specs=[sp["row512"], sp["row128"], sp["row128"], sp["row128"], sp["row128"],
                   pl.BlockSpec((1, A_Q_HEADS), lambda n: (0, 0))],
        out_shape=[jax.ShapeDtypeStruct((T, HYB_PAD), ACT_DTYPE)] + [jax.ShapeDtypeStruct((T, LANES), F32)] * 4
        + [jax.ShapeDtypeStruct((1, A_Q_HEADS), F32)],
        compiler_params=_cparams("arbitrary"),
    )(proj, proj, proj, proj, proj, cos, sin_s, cos, sin_s, sinks_b, o, lse, dmix)


def _shift_down(x, prev8, k):
    if k == 0:
        return x
    R, W = x.shape
    r = pltpu.roll(jnp.concatenate([prev8, x], axis=0).reshape(R // SUBLANES + 1, SUBLANES, W), k, axis=1)
    sub = lax.broadcasted_iota(jnp.int32, (R // SUBLANES, SUBLANES, W), 1)
    return jnp.where(sub < k, r[:-1], r[1:]).reshape(R, W)


def _shift_up(x, next8, k):
    if k == 0:
        return x
    R, W = x.shape
    r = pltpu.roll(jnp.concatenate([x, next8], axis=0).reshape(R // SUBLANES + 1, SUBLANES, W), SUBLANES - k, axis=1)
    sub = lax.broadcasted_iota(jnp.int32, (R // SUBLANES, SUBLANES, W), 1)
    return jnp.where(sub >= SUBLANES - k, r[1:], r[:-1]).reshape(R, W)


def _conv(x, prev8, w):
    w0, w1, w2, w3 = (w[j:j + 1] for j in range(CONV_K))
    x1 = _shift_down(x, prev8, 1)
    v = x * w1 + x1 * w0
    v_prev8 = prev8 * w1 + pltpu.roll(prev8, 1, axis=0) * w0
    return x * w3 + x1 * w2 + _shift_down(v, v_prev8, 2)


def _conv_bwd(x, w, dy, next8_dy):
    dx = dy * w[CONV_K - 1:CONV_K]
    dws = []
    for j in range(CONV_K - 1):
        up = _shift_up(dy, next8_dy, CONV_K - 1 - j)
        dx = dx + up * w[j:j + 1]
        dws.append(jnp.sum(up * x, axis=0, keepdims=True))
    dws.append(jnp.sum(dy * x, axis=0, keepdims=True))
    return dx, dws


DK_SCALE = B_HEAD_DIM ** -0.5


BNN = (((2,), (1,)), ((0,), (0,)))
BNT = (((2,), (2,)), ((0,), (0,)))
BTN = (((1,), (1,)), ((0,), (0,)))


def _tri_inv(a):
    C = a.shape[-1]
    ri = lax.broadcasted_iota(jnp.int32, (C, C), 0)
    ci = lax.broadcasted_iota(jnp.int32, (C, C), 1)
    x = jnp.where(ri == ci, 1.0, 0.0)[None] - a
    p = _dotf(a, a, BNN)
    span = 2
    while span < C:
        dot = _dotf if span <= 4 else _dot
        x = x + dot(x, p, BNN)
        span *= 2
        if span < C:
            p = dot(p, p, BNN)
    return x


def _dn_chunk(qc, kc, v, gcol, grow, bcol, s0, tm=None):
    C = B_CHUNK
    ri = lax.broadcasted_iota(jnp.int32, (C, C), 0)
    ci = lax.broadcasted_iota(jnp.int32, (C, C), 1)
    incl, strict = (ri >= ci)[None], (ri > ci)[None]
    rq = lax.rsqrt(jnp.sum(qc * qc, axis=-1, keepdims=True) + NORM_EPS)
    rk = lax.rsqrt(jnp.sum(kc * kc, axis=-1, keepdims=True) + NORM_EPS)
    qn = qc * rq
    q = qn * DK_SCALE
    k = kc * rk
    gc_col = jnp.sum(jnp.where(incl, grow, 0.0), axis=2, keepdims=True)
    gc_row = jnp.sum(jnp.where((ri <= ci)[None], gcol, 0.0), axis=1, keepdims=True)
    gl = jnp.sum(gcol, axis=1, keepdims=True)
    dincl = jnp.where(incl, jnp.exp(jnp.where(incl, gc_col - gc_row, 0.0)), 0.0)
    dstrict = jnp.where(strict, dincl, 0.0)
    eg = jnp.exp(gc_col)
    ekt = jnp.exp(gl - gc_col)
    egl = jnp.exp(gl)
    kb = k * bcol
    vb = v * bcol
    kbg = kb * eg
    a = _dot(kb, k, BNT) * dstrict
    if tm is None:
        tm = _tri_inv(a)
    u = _dot(tm, vb, BNN)
    w = _dot(tm, kbg, BNN)
    vn = u - _dot(w, s0, BNN)
    qk = _dot(q, k, BNT) * dincl
    qg = q * eg
    kt = k * ekt
    o = _dot(qg, s0, BNN) + _dot(qk, vn, BNN)
    s1 = s0 * egl + _dot(kt, vn, BTN)
    return dict(rq=rq, rk=rk, qn=qn, q=q, k=k, dincl=dincl, dstrict=dstrict, eg=eg, ekt=ekt, egl=egl, kb=kb, vb=vb,
                kbg=kbg, a=a, tm=tm, w=w, vn=vn, qk=qk, qg=qg, kt=kt, o=o, s1=s1, ri=ri[None], ci=ci[None])


def _heads(ref):
    return jnp.stack([ref[:, h * B_HEAD_DIM:(h + 1) * B_HEAD_DIM] for h in range(B_HEADS)])


def _store_heads(ref, val):
    for h in range(B_HEADS):
        ref[:, h * B_HEAD_DIM:(h + 1) * B_HEAD_DIM] = val[h]


def _dn_specs(N, rev):
    ix = (lambda n: N - 1 - n) if rev else (lambda n: n)
    wide = lambda cb: pl.BlockSpec((B_CHUNK, B_W), lambda n: (ix(n), cb))
    return dict(
        q=wide(0), k=wide(1), v=wide(2), z=wide(HP_Z // B_W), dob=wide(A_Q_W // B_W), out=wide(0),
        nw=pl.BlockSpec((1, LANES), lambda n: (0, 0)),
        row=pl.BlockSpec((B_HEADS, None, 1, B_CHUNK), lambda n: (0, ix(n), 0, 0)),
        state=pl.BlockSpec((B_HEADS, None, B_HEAD_DIM, B_HEAD_DIM), lambda n: (0, ix(n), 0, 0)),
        inv=pl.BlockSpec((B_HEADS, None, B_CHUNK, B_CHUNK), lambda n: (0, ix(n), 0, 0)),
    )


def _to_col(row):
    C = row.shape[-1]
    eye = lax.broadcasted_iota(jnp.int32, (C, C), 0) == lax.broadcasted_iota(jnp.int32, (C, C), 1)
    return jnp.sum(jnp.where(eye[None], row, 0.0), axis=2, keepdims=True)


def _to_row(col):
    C = col.shape[1]
    eye = lax.broadcasted_iota(jnp.int32, (C, C), 0) == lax.broadcasted_iota(jnp.int32, (C, C), 1)
    return jnp.sum(jnp.where(eye[None], col, 0.0), axis=1, keepdims=True)


def _dn_fwd(proj, conv_w, norm_w, grow, brow, name):
    T = proj.shape[0]
    N = T // B_CHUNK
    sp = _dn_specs(N, False)
    cb0 = HP_QKVB // B_W

    def body(xq_ref, xk_ref, xv_ref, cw_ref, z_ref, nw_ref, gr_ref, br_ref, o_ref, st_ref, tm_ref, c_ref,
             s_ref, prev_ref):
        n = pl.program_id(0)

        @pl.when(n == 0)
        def _():
            s_ref[...] = jnp.zeros_like(s_ref)
            prev_ref[...] = jnp.zeros_like(prev_ref)

        qkv = []
        for part, x_ref in enumerate((xq_ref, xk_ref, xv_ref)):
            cols = slice(part * B_W, (part + 1) * B_W)
            x = x_ref[...]
            xc = _silu(_conv(x, prev_ref[:, cols], cw_ref[:, cols]))
            prev_ref[:, cols] = x[B_CHUNK - SUBLANES:]
            c_ref[:, cols] = xc
            qkv.append(jnp.stack([xc[:, h * B_HEAD_DIM:(h + 1) * B_HEAD_DIM] for h in range(B_HEADS)]))
        s0 = s_ref[...]
        st_ref[...] = s0
        grow_v = gr_ref[...]
        f = _dn_chunk(qkv[0], qkv[1], qkv[2], _to_col(grow_v), grow_v, _to_col(br_ref[...]), s0)
        o = f["o"]
        r = lax.rsqrt(jnp.mean(o * o, axis=-1, keepdims=True) + NORM_EPS)
        _store_heads(o_ref, o * r * nw_ref[...][None] * _silu(_heads(z_ref)))
        s_ref[...] = f["s1"]
        tm_ref[...] = f["tm"]

    wide = lambda cb: pl.BlockSpec((B_CHUNK, B_W), lambda n: (n, cb))
    return pl.pallas_call(
        body, name=name, grid=(N,),
        in_specs=[wide(cb0), wide(cb0 + 1), wide(cb0 + 2), pl.BlockSpec((CONV_K, 3 * B_W), lambda n: (0, 0)),
                  sp["z"], sp["nw"], sp["row"], sp["row"]],
        out_specs=[sp["out"], sp["state"], sp["inv"], pl.BlockSpec((B_CHUNK, 3 * B_W), lambda n: (n, 0))],
        out_shape=[jax.ShapeDtypeStruct((T, B_W), F32),
                   jax.ShapeDtypeStruct((B_HEADS, N, B_HEAD_DIM, B_HEAD_DIM), F32),
                   jax.ShapeDtypeStruct((B_HEADS, N, B_CHUNK, B_CHUNK), F32),
                   jax.ShapeDtypeStruct((T, 3 * B_W), F32)],
        scratch_shapes=[pltpu.VMEM((B_HEADS, B_HEAD_DIM, B_HEAD_DIM), F32), pltpu.VMEM((SUBLANES, 3 * B_W), F32)],
        compiler_params=_cparams("arbitrary"),
    )(proj, proj, proj, conv_w, proj, norm_w, grow, brow)


def _dn_bwd(qkvc, proj, conv_w, norm_w, grow, brow, states, invs, dmix, dproj, name):
    T = qkvc.shape[0]
    N = T // B_CHUNK
    sp = _dn_specs(N, True)
    C = B_CHUNK
    cb0 = HP_QKVB // B_W

    def body(q_ref, k_ref, v_ref, z_ref, nw_ref, gr_ref, br_ref, st_ref, tm_ref, dob_ref, dproj_in,
             xq_ref, xk_ref, xv_ref, pq_ref, pk_ref, pv_ref, cw_ref,
             dp_ref, dg_ref, db_ref, dnw_ref, dcw_ref, ds_ref, nxt_ref):
        n = pl.program_id(0)
        dp_ref[...] = dproj_in[...]
        dz_ref = dp_ref.at[:, HP_Z:HP_Z + B_W]
        d_qkv = {}

        @pl.when(n == 0)
        def _():
            ds_ref[...] = jnp.zeros_like(ds_ref)
            dnw_ref[...] = jnp.zeros_like(dnw_ref)
            dcw_ref[...] = jnp.zeros_like(dcw_ref)
            nxt_ref[...] = jnp.zeros_like(nxt_ref)

        s0 = st_ref[...]
        ds1 = ds_ref[...]
        v, z, nw, bcol_v = _heads(v_ref), _heads(z_ref), nw_ref[...][None], _to_col(br_ref[...])
        grow_v = gr_ref[...]
        f = _dn_chunk(_heads(q_ref), _heads(k_ref), v, _to_col(grow_v), grow_v, bcol_v, s0, tm=tm_ref[...])
        o, q, k, qn = f["o"], f["q"], f["k"], f["qn"]
        eg, ekt, egl = f["eg"], f["ekt"], f["egl"]
        tm, w, vn, kb, vb, kbg = f["tm"], f["w"], f["vn"], f["kb"], f["vb"], f["kbg"]
        qg, kt, qk, a = f["qg"], f["kt"], f["qk"], f["a"]
        ri, ci = f["ri"], f["ci"]

        dob_v = _heads(dob_ref)
        r = lax.rsqrt(jnp.mean(o * o, axis=-1, keepdims=True) + NORM_EPS)
        sz = _silu(z)
        on = o * r
        dnw_ref[...] += jnp.sum(dob_v * sz * on, axis=1, keepdims=True)
        _store_heads(dz_ref, (dob_v * on * nw * _dsilu(z)).astype(ACT_DTYPE))
        d_on = dob_v * sz * nw
        do = r * (d_on - on * jnp.mean(d_on * on, axis=-1, keepdims=True))

        dvn = _dot(qk, do, BTN) + _dot(kt, ds1, BNN)
        dqk = _dot(do, vn, BNT)
        dqg = _dot(do, s0, BNT)
        ds_ref[...] = _dot(qg, do, BTN) + egl * ds1 - _dot(w, dvn, BTN)
        dgl = jnp.sum(s0 * ds1, axis=(1, 2), keepdims=True) * egl
        dkt = _dot(vn, ds1, BNT)
        dw = -_dot(dvn, s0, BNT)
        dq = dqg * eg
        dgc = jnp.sum(dqg * qg, axis=-1, keepdims=True)
        dk = dkt * ekt
        t_kt = jnp.sum(dkt * kt, axis=-1, keepdims=True)
        dgl = dgl + jnp.sum(t_kt, axis=1, keepdims=True)
        dgc = dgc - t_kt
        dqkr = dqk * f["dincl"]
        dq = dq + _dot(dqkr, k, BNN)
        dk = dk + _dot(dqkr, q, BTN)
        e_qk = dqk * qk
        dgc = dgc + jnp.sum(e_qk, axis=-1, keepdims=True)
        dgc_row = -jnp.sum(e_qk, axis=1, keepdims=True)
        dtm = _dot(dvn, vb, BNT) + _dot(dw, kbg, BNT)
        dvb = _dot(tm, dvn, BTN)
        dkbg = _dot(tm, dw, BTN)
        dkb = dkbg * eg
        dgc = dgc + jnp.sum(dkbg * kbg, axis=-1, keepdims=True)
        da = -_dotf(tm, _dotf(dtm, tm, BNT), BTN)
        dkk = da * f["dstrict"]
        e_a = da * a
        dgc = dgc + jnp.sum(e_a, axis=-1, keepdims=True)
        dgc_row = dgc_row - jnp.sum(e_a, axis=1, keepdims=True)
        dkb = dkb + _dot(dkk, k, BNN)
        dk = dk + _dot(dkk, kb, BTN)
        dk = dk + dkb * bcol_v
        db_ref[...] = _to_row(jnp.sum(dkb * k, axis=-1, keepdims=True) + jnp.sum(dvb * v, axis=-1, keepdims=True))
        d_qkv[2] = dvb * bcol_v
        dgc_row = dgc_row + jnp.sum(jnp.where(ri == ci, dgc, 0.0), axis=1, keepdims=True)
        dg_ref[...] = jnp.sum(jnp.where(ci <= ri, _to_col(dgc_row), 0.0), axis=1, keepdims=True) + dgl
        dqs = dq * DK_SCALE
        d_qkv[0] = f["rq"] * (dqs - qn * jnp.sum(dqs * qn, axis=-1, keepdims=True))
        d_qkv[1] = f["rk"] * (dk - k * jnp.sum(dk * k, axis=-1, keepdims=True))
        first = n < N - 1
        for part, (x_ref, p_ref) in enumerate(((xq_ref, pq_ref), (xk_ref, pk_ref), (xv_ref, pv_ref))):
            cols = slice(part * B_W, (part + 1) * B_W)
            dy = jnp.concatenate([d_qkv[part][h] for h in range(B_HEADS)], axis=1)
            x, w = x_ref[...], cw_ref[:, cols]
            dpre = dy * _dsilu(_conv(x, jnp.where(first, p_ref[...], 0.0), w))
            dx, dws = _conv_bwd(x, w, dpre, nxt_ref[:, cols])
            nxt_ref[:, cols] = dpre[:SUBLANES]
            dp_ref[:, HP_QKVB + part * B_W:HP_QKVB + (part + 1) * B_W] = dx.astype(ACT_DTYPE)
            for j in range(CONV_K):
                dcw_ref[j:j + 1, cols] += dws[j]

    rev = lambda n: N - 1 - n
    part_x = lambda p: pl.BlockSpec((C, B_W), lambda n: (rev(n), cb0 + p))
    part_prev = lambda p: pl.BlockSpec((SUBLANES, B_W), lambda n: (jnp.maximum(rev(n) * (C // SUBLANES) - 1, 0), cb0 + p))
    whole_row = pl.BlockSpec((C, HYB_PAD), lambda n: (rev(n), 0))
    return pl.pallas_call(
        body, name=name, grid=(N,),
        in_specs=[sp["q"], sp["k"], sp["v"], sp["z"], sp["nw"], sp["row"], sp["row"], sp["state"], sp["inv"], sp["dob"],
                  whole_row, part_x(0), part_x(1), part_x(2), part_prev(0), part_prev(1), part_prev(2),
                  pl.BlockSpec((CONV_K, 3 * B_W), lambda n: (0, 0))],
        out_specs=[whole_row, sp["row"], sp["row"], pl.BlockSpec((B_HEADS, 1, LANES), lambda n: (0, 0, 0)),
                   pl.BlockSpec((CONV_K, 3 * B_W), lambda n: (0, 0))],
        out_shape=[jax.ShapeDtypeStruct(dproj.shape, dproj.dtype),
                   jax.ShapeDtypeStruct((B_HEADS, N, 1, C), F32), jax.ShapeDtypeStruct((B_HEADS, N, 1, C), F32),
                   jax.ShapeDtypeStruct((B_HEADS, 1, LANES), F32), jax.ShapeDtypeStruct((CONV_K, 3 * B_W), F32)],
        input_output_aliases={10: 0},
        scratch_shapes=[pltpu.VMEM((B_HEADS, B_HEAD_DIM, B_HEAD_DIM), F32), pltpu.VMEM((SUBLANES, 3 * B_W), F32)],
        compiler_params=_cparams("arbitrary"),
    )(qkvc, qkvc, qkvc, proj, norm_w, grow, brow, states, invs, dmix, dproj, proj, proj, proj, proj, proj, proj, conv_w)


def _lru_gates(xc, wa_ref, wx_ref, ba, bx, sp):
    pre_r, pre_i = [], []
    for hb in range(LRU_BLOCKS):
        xb = xc[:, hb * LRU_BLOCK_W:(hb + 1) * LRU_BLOCK_W]
        pre_r.append(_dot(xb, wa_ref[hb]))
        pre_i.append(_dot(xb, wx_ref[hb]))
    r = _sigmoid(jnp.concatenate(pre_r, axis=1) + ba)
    i = _sigmoid(jnp.concatenate(pre_i, axis=1) + bx)
    la = -LRU_C * r * sp
    a = jnp.exp(la)
    th = jnp.tanh(la)
    s = jnp.sqrt(-2.0 * th / (1.0 - th))
    return r, i, a, s


def _scan_down(a, b, h_in):
    R, W = a.shape
    a = a.reshape(R // SUBLANES, SUBLANES, W)
    b = b.reshape(R // SUBLANES, SUBLANES, W)
    sub = lax.broadcasted_iota(jnp.int32, a.shape, 1)
    d = 1
    while d < SUBLANES:
        ok = sub >= d
        b = a * jnp.where(ok, pltpu.roll(b, d, axis=1), 0.0) + b
        a = a * jnp.where(ok, pltpu.roll(a, d, axis=1), 1.0)
        d *= 2
    out, last = [], h_in
    for g in range(R // SUBLANES):
        h = b[g] + a[g] * last
        out.append(h)
        last = h[SUBLANES - 1:SUBLANES]
    return jnp.concatenate(out, axis=0)


def _scan_up(a, b, l_in):
    R, W = a.shape
    a = a.reshape(R // SUBLANES, SUBLANES, W)
    b = b.reshape(R // SUBLANES, SUBLANES, W)
    sub = lax.broadcasted_iota(jnp.int32, a.shape, 1)
    d = 1
    while d < SUBLANES:
        ok = sub < SUBLANES - d
        b = a * jnp.where(ok, pltpu.roll(b, SUBLANES - d, axis=1), 0.0) + b
        a = a * jnp.where(ok, pltpu.roll(a, SUBLANES - d, axis=1), 1.0)
        d *= 2
    out, nxt = [], l_in
    for g in range(R // SUBLANES - 1, -1, -1):
        lam = b[g] + a[g] * nxt
        out.append(lam)
        nxt = lam[0:1]
    return jnp.concatenate(out[::-1], axis=0)


def _rglru_fwd(proj, conv_w, conv_b, wa, wx, ba, bx, sp, name):
    T = proj.shape[0]
    R = _tile(T, 256)
    W = D_MODEL

    def body(p_ref, cw_ref, cb_ref, wa_ref, wx_ref, ba_ref, bx_ref, sp_ref, hg_ref, h_ref, prev_ref, hc_ref):
        i = pl.program_id(0)

        @pl.when(i == 0)
        def _():
            prev_ref[...] = jnp.zeros_like(prev_ref)
            hc_ref[...] = jnp.zeros_like(hc_ref)

        xr = p_ref[:, :W]
        gate = p_ref[:, W:]
        xc = _conv(xr, prev_ref[...], cw_ref[...]) + cb_ref[...]
        prev_ref[...] = xr[R - SUBLANES:]
        r, ig, a, s = _lru_gates(xc, wa_ref, wx_ref, ba_ref[...], bx_ref[...], sp_ref[...])
        h = _scan_down(a, s * ig * xc, hc_ref[SUBLANES - 1:SUBLANES, :])
        h_ref[...] = h
        hg_ref[...] = (h * _gelu(gate)).astype(ACT_DTYPE)
        hc_ref[...] = h[R - SUBLANES:]

    vec = pl.BlockSpec((1, W), lambda i: (0, 0))
    wsp = pl.BlockSpec((LRU_BLOCKS, LRU_BLOCK_W, LRU_BLOCK_W), lambda i: (0, 0, 0))
    row = pl.BlockSpec((R, W), lambda i: (i, 0))
    return pl.pallas_call(
        body, name=name, grid=(T // R,),
        in_specs=[pl.BlockSpec((R, 2 * W), lambda i: (i, 0)), pl.BlockSpec((CONV_K, W), lambda i: (0, 0)),
                  vec, wsp, wsp, vec, vec, vec],
        out_specs=[row, row],
        out_shape=[jax.ShapeDtypeStruct((T, W), ACT_DTYPE), jax.ShapeDtypeStruct((T, W), F32)],
        scratch_shapes=[pltpu.VMEM((SUBLANES, W), F32), pltpu.VMEM((SUBLANES, W), F32)],
        compiler_params=_cparams("arbitrary"),
    )(proj, conv_w, conv_b, wa, wx, ba, bx, sp)


def _rglru_bwd(proj, conv_w, conv_b, wa, wx, ba, bx, sp, h, dhg, name):
    T = proj.shape[0]
    R = _tile(T, 256)
    nb = T // R
    r8 = R // SUBLANES
    W = D_MODEL

    def body(p_ref, pp_ref, cw_ref, cb_ref, wa_ref, wx_ref, ba_ref, bx_ref, sp_ref, h_ref, hp_ref, dhg_ref,
             dp_ref, dcw_ref, dcb_ref, dwa_ref, dwx_ref, dba_ref, dbx_ref, dsp_ref, lam_ref, nxt_ref):
        step = pl.program_id(0)
        blk = nb - 1 - step

        @pl.when(step == 0)
        def _():
            lam_ref[...] = jnp.zeros_like(lam_ref)
            nxt_ref[...] = jnp.zeros_like(nxt_ref)

        xr = p_ref[:, :W]
        gate = p_ref[:, W:]
        first = blk > 0
        prev8 = jnp.where(first, pp_ref[:, :W], 0.0)
        hprev8 = jnp.where(first, hp_ref[...], 0.0)
        cw = cw_ref[...]
        spv = sp_ref[...]
        xc = _conv(xr, prev8, cw) + cb_ref[...]
        r, ig, a, s = _lru_gates(xc, wa_ref, wx_ref, ba_ref[...], bx_ref[...], spv)
        hv = h_ref[...]
        dhg_v = dhg_ref[...]
        dgate = dhg_v * hv * _dgelu(gate)
        dh = dhg_v * _gelu(gate)
        row = lax.broadcasted_iota(jnp.int32, (R, W), 0)
        last = row == R - 1
        a_up = jnp.where(last, 0.0, pltpu.roll(a, R - 1, axis=0))
        lam = _scan_up(a_up, dh + jnp.where(last, lam_ref[0:1, :], 0.0), jnp.zeros((1, W), F32))
        lam_ref[...] = (a * lam)[:SUBLANES]
        h_dn = _shift_down(hv, hprev8, 1)
        da = lam * h_dn
        bx_in = ig * xc
        dsv = lam * bx_in
        dig = lam * s * xc
        dxc = lam * s * ig
        dla = da * a - dsv * (a * a) / s
        dr = dla * (-LRU_C) * spv
        dsp = jnp.sum(dla * (-LRU_C) * r, axis=0, keepdims=True)
        dpr = dr * r * (1.0 - r)
        dpi = dig * ig * (1.0 - ig)
        dxc_parts, dwa_parts, dwx_parts = [], [], []
        for hb in range(LRU_BLOCKS):
            sl = slice(hb * LRU_BLOCK_W, (hb + 1) * LRU_BLOCK_W)
            xb, gr, gi = xc[:, sl], dpr[:, sl], dpi[:, sl]
            dxc_parts.append(_dot(gr, wa_ref[hb], NT) + _dot(gi, wx_ref[hb], NT))
            dwa_parts.append(_dot(xb, gr, TN))
            dwx_parts.append(_dot(xb, gi, TN))
        dxc = dxc + jnp.concatenate(dxc_parts, axis=1)
        dxr, dcw = _conv_bwd(xr, cw, dxc, nxt_ref[...])
        nxt_ref[...] = dxc[:SUBLANES]
        dp_ref[:, :W] = dxr.astype(ACT_DTYPE)
        dp_ref[:, W:] = dgate.astype(ACT_DTYPE)
        dcb = jnp.sum(dxc, axis=0, keepdims=True)
        dba = jnp.sum(dpr, axis=0, keepdims=True)
        dbx = jnp.sum(dpi, axis=0, keepdims=True)

        @pl.when(step == 0)
        def _():
            for j in range(CONV_K):
                dcw_ref[j:j + 1, :] = dcw[j]
            dcb_ref[...] = dcb
            dba_ref[...] = dba
            dbx_ref[...] = dbx
            dsp_ref[...] = dsp
            for hb in range(LRU_BLOCKS):
                dwa_ref[hb] = dwa_parts[hb]
                dwx_ref[hb] = dwx_parts[hb]

        @pl.when(step > 0)
        def _():
            for j in range(CONV_K):
                dcw_ref[j:j + 1, :] += dcw[j]
            dcb_ref[...] += dcb
            dba_ref[...] += dba
            dbx_ref[...] += dbx
            dsp_ref[...] += dsp
            for hb in range(LRU_BLOCKS):
                dwa_ref[hb] += dwa_parts[hb]
                dwx_ref[hb] += dwx_parts[hb]

    rv = lambda i: nb - 1 - i
    before = lambda i: jnp.maximum((nb - 1 - i) * r8 - 1, 0)
    vec = pl.BlockSpec((1, W), lambda i: (0, 0))
    cws = pl.BlockSpec((CONV_K, W), lambda i: (0, 0))
    wsp = pl.BlockSpec((LRU_BLOCKS, LRU_BLOCK_W, LRU_BLOCK_W), lambda i: (0, 0, 0))
    row = pl.BlockSpec((R, W), lambda i: (rv(i), 0))
    wshape = jax.ShapeDtypeStruct((LRU_BLOCKS, LRU_BLOCK_W, LRU_BLOCK_W), F32)
    vshape = jax.ShapeDtypeStruct((1, W), F32)
    return pl.pallas_call(
        body, name=name, grid=(nb,),
        in_specs=[pl.BlockSpec((R, 2 * W), lambda i: (rv(i), 0)), pl.BlockSpec((SUBLANES, 2 * W), lambda i: (before(i), 0)),
                  cws, vec, wsp, wsp, vec, vec, vec, row, pl.BlockSpec((SUBLANES, W), lambda i: (before(i), 0)), row],
        out_specs=[pl.BlockSpec((R, 2 * W), lambda i: (rv(i), 0)), cws, vec, wsp, wsp, vec, vec, vec],
        out_shape=[jax.ShapeDtypeStruct((T, 2 * W), ACT_DTYPE), jax.ShapeDtypeStruct((CONV_K, W), F32), vshape,
                   wshape, wshape, vshape, vshape, vshape],
        scratch_shapes=[pltpu.VMEM((SUBLANES, W), F32), pltpu.VMEM((SUBLANES, W), F32)],
        compiler_params=_cparams("arbitrary"),
    )(proj, proj, conv_w, conv_b, wa, wx, ba, bx, sp, h, h, dhg)


MESH = pl.DeviceIdType.MESH
ANY = pl.BlockSpec(memory_space=pl.ANY)


def _position():
    x, y, c = lax.axis_index("x"), lax.axis_index("y"), lax.axis_index("c")
    other_chips = [(1 - x, y), (x, 1 - y), (1 - x, 1 - y)]
    return x, y, c, other_chips


def _all_gather_weights(shards, name):
    n = len(shards)

    def body(*refs):
        ins, outs = refs[:n], refs[n:2 * n]
        send_sems, recv_sems = refs[2 * n:]
        x, y, c, chips = _position()
        me = 2 * x + y
        sibling = (x, y, 1 - c)

        def rcopy(t, k, src, dst, to):
            return pltpu.make_async_remote_copy(src_ref=src, dst_ref=dst, send_sem=send_sems.at[t, k],
                                                recv_sem=recv_sems.at[t, k], device_id=to, device_id_type=MESH)

        started = []
        for t in range(n):
            for j, (cx, cy) in enumerate(chips):
                cp = rcopy(t, j, ins[t].at[c], outs[t].at[me, c], (cx, cy, c))
                cp.start()
                started.append(cp)
        for t in range(n):
            for j, (cx, cy) in enumerate(chips):
                blk = outs[t].at[2 * cx + cy, c]
                rcopy(t, j, blk, blk, (cx, cy, c)).wait_recv()
                cp = rcopy(t, 3 + j, blk, blk, sibling)
                cp.start()
                started.append(cp)
        for t in range(n):
            for j, (cx, cy) in enumerate(chips):
                blk = outs[t].at[2 * cx + cy, 1 - c]
                rcopy(t, 3 + j, blk, blk, sibling).wait_recv()
        for cp in started:
            cp.wait_send()

    return pl.pallas_call(
        body, name=name, in_specs=[ANY] * n, out_specs=[ANY] * n,
        out_shape=[jax.ShapeDtypeStruct((N_CHIPS,) + s.shape, s.dtype) for s in shards],
        scratch_shapes=[pltpu.SemaphoreType.DMA((n, 6)), pltpu.SemaphoreType.DMA((n, 6))],
    )(*shards)


HBM = pl.BlockSpec(memory_space=pltpu.HBM)
SEM = pl.BlockSpec(memory_space=pltpu.SEMAPHORE)
EFFECT = pltpu.SideEffectType.DATAFLOW_SIDE_EFFECTING


def _gather_start(own, land, after, name):
    n = len(own)

    def body(*refs):
        own_refs, land_refs = refs[:n], refs[n:2 * n]
        send_sems = refs[2 * n + 1:3 * n + 1]
        recv_sems = refs[3 * n + 1:4 * n + 1]
        x, y, c, chips = _position()
        me = 2 * x + y
        for t in range(n):
            for cx, cy in chips:
                pltpu.make_async_remote_copy(
                    src_ref=own_refs[t], dst_ref=land_refs[t].at[me], send_sem=send_sems[t], recv_sem=recv_sems[t],
                    device_id=(cx, cy, c), device_id_type=MESH).start()

    sems = (pltpu.SemaphoreType.DMA(()),) * (2 * n)
    thru = [pltpu.HBM(a.shape, a.dtype) for a in list(own) + list(land)]
    out = pl.pallas_call(
        body, name=name, out_shape=(*sems, *thru),
        in_specs=[HBM] * (2 * n) + [pl.BlockSpec(memory_space=pl.ANY)], out_specs=(SEM,) * (2 * n) + (HBM,) * (2 * n),
        input_output_aliases={i: 2 * n + i for i in range(2 * n)},
        compiler_params=pltpu.CompilerParams(has_side_effects=EFFECT),
    )(*[pltpu.with_memory_space_constraint(a, pltpu.HBM) for a in list(own) + list(land)], after)
    return list(out[:n]), list(out[n:2 * n]), list(out[2 * n:3 * n]), list(out[3 * n:])


def _gather_wait(send_sems, recv_sems, own, land, after, name):
    n = len(own)

    def body(*refs):
        land_refs = refs[n:2 * n]
        s_sems, r_sems = refs[2 * n:3 * n], refs[3 * n:4 * n]
        x, y, c, _ = _position()
        for t in range(n):
            three = land_refs[t].at[pl.ds(0, N_CHIPS - 1)]
            cp = pltpu.make_async_remote_copy(src_ref=three, dst_ref=three, send_sem=s_sems[t], recv_sem=r_sems[t],
                                              device_id=(x, y, c), device_id_type=MESH)
            cp.wait_send()
            cp.wait_recv()

    thru = [pltpu.HBM(a.shape, a.dtype) for a in list(own) + list(land)]
    out = pl.pallas_call(
        body, name=name, out_shape=tuple(thru),
        in_specs=[HBM] * (2 * n) + [SEM] * (2 * n) + [pl.BlockSpec(memory_space=pl.ANY)], out_specs=(HBM,) * (2 * n),
        input_output_aliases={i: i for i in range(2 * n)},
        compiler_params=pltpu.CompilerParams(has_side_effects=EFFECT),
    )(*own, *land, *send_sems, *recv_sems, after)
    return list(out[n:])


def _rs_to_sibling(grads, name):
    n = len(grads)

    def body(*refs):
        ins, outs = refs[:n], refs[n:2 * n]
        send_sems, recv_sems = refs[2 * n:]
        x, y, c, _ = _position()
        cps = [pltpu.make_async_remote_copy(src_ref=ins[t].at[:, 1 - c], dst_ref=outs[t], send_sem=send_sems.at[t],
                                            recv_sem=recv_sems.at[t], device_id=(x, y, 1 - c), device_id_type=MESH)
               for t in range(n)]
        for cp in cps:
            cp.start()
        for cp in cps:
            cp.wait()

    return pl.pallas_call(
        body, name=name, in_specs=[ANY] * n, out_specs=[ANY] * n,
        out_shape=[jax.ShapeDtypeStruct((N_CHIPS,) + g.shape[2:], g.dtype) for g in grads],
        scratch_shapes=[pltpu.SemaphoreType.DMA((n,)), pltpu.SemaphoreType.DMA((n,))],
    )(*grads)


def _rs_sibling_start(grads, recv, name):
    n = len(grads)

    def body(*refs):
        grad_refs, recv_refs = refs[:n], refs[n:2 * n]
        send_sems, recv_sems = refs[2 * n:3 * n], refs[3 * n:4 * n]
        token_ref = refs[-1]
        x, y, c, _ = _position()
        for t in range(n):
            pltpu.make_async_remote_copy(src_ref=grad_refs[t].at[:, 1 - c], dst_ref=recv_refs[t], send_sem=send_sems[t],
                                         recv_sem=recv_sems[t], device_id=(x, y, 1 - c), device_id_type=MESH).start()
        token_ref[...] = jnp.zeros_like(token_ref)

    sems = (pltpu.SemaphoreType.DMA(()),) * (2 * n)
    thru = [pltpu.HBM(a.shape, a.dtype) for a in list(grads) + list(recv)]
    out = pl.pallas_call(
        body, name=name, out_shape=(*sems, *thru, jax.ShapeDtypeStruct((SUBLANES, LANES), F32)),
        in_specs=[HBM] * (2 * n), out_specs=(SEM,) * (2 * n) + (HBM,) * (2 * n) + (pl.BlockSpec(memory_space=pltpu.VMEM),),
        input_output_aliases={i: 2 * n + i for i in range(2 * n)},
        compiler_params=pltpu.CompilerParams(has_side_effects=EFFECT),
    )(*[pltpu.with_memory_space_constraint(a, pltpu.HBM) for a in list(grads) + list(recv)])
    return out[:n], out[n:2 * n], list(out[2 * n:3 * n]), list(out[3 * n:4 * n]), out[-1]


def _rs_sibling_wait(send_sems, recv_sems, grads, recv, after, name):
    n = len(grads)

    def body(*refs):
        recv_refs = refs[n:2 * n]
        s_sems, r_sems = refs[2 * n:3 * n], refs[3 * n:4 * n]
        x, y, c, _ = _position()
        for t in range(n):
            cp = pltpu.make_async_remote_copy(src_ref=recv_refs[t], dst_ref=recv_refs[t], send_sem=s_sems[t],
                                              recv_sem=r_sems[t], device_id=(x, y, c), device_id_type=MESH)
            cp.wait_send()
            cp.wait_recv()

    thru = [pltpu.HBM(a.shape, a.dtype) for a in list(grads) + list(recv)]
    out = pl.pallas_call(
        body, name=name, out_shape=tuple(thru),
        in_specs=[HBM] * (2 * n) + [SEM] * (2 * n) + [pl.BlockSpec(memory_space=pl.ANY)], out_specs=(HBM,) * (2 * n),
        input_output_aliases={i: i for i in range(2 * n)},
        compiler_params=pltpu.CompilerParams(has_side_effects=EFFECT),
    )(*grads, *recv, *send_sems, *recv_sems, after)
    return list(out[:n]), list(out[n:])


def _rs_across_start(parts, recv, name):
    n = len(parts)

    def body(*refs):
        part_refs, recv_refs = refs[:n], refs[n:2 * n]
        send_sems, recv_sems = refs[2 * n:3 * n], refs[3 * n:4 * n]
        token_ref = refs[-1]
        x, y, c, chips = _position()
        me = 2 * x + y
        for t in range(n):
            for cx, cy in chips:
                pltpu.make_async_remote_copy(src_ref=part_refs[t].at[2 * cx + cy], dst_ref=recv_refs[t].at[me],
                                             send_sem=send_sems[t], recv_sem=recv_sems[t], device_id=(cx, cy, c),
                                             device_id_type=MESH).start()
        token_ref[...] = jnp.zeros_like(token_ref)

    sems = (pltpu.SemaphoreType.DMA(()),) * (2 * n)
    thru = [pltpu.HBM(a.shape, a.dtype) for a in list(parts) + list(recv)]
    out = pl.pallas_call(
        body, name=name, out_shape=(*sems, *thru, jax.ShapeDtypeStruct((SUBLANES, LANES), F32)),
        in_specs=[HBM] * (2 * n), out_specs=(SEM,) * (2 * n) + (HBM,) * (2 * n) + (pl.BlockSpec(memory_space=pltpu.VMEM),),
        input_output_aliases={i: 2 * n + i for i in range(2 * n)},
        compiler_params=pltpu.CompilerParams(has_side_effects=EFFECT),
    )(*[pltpu.with_memory_space_constraint(a, pltpu.HBM) for a in list(parts) + list(recv)])
    return out[:n], out[n:2 * n], list(out[2 * n:3 * n]), list(out[3 * n:4 * n]), out[-1]


def _rs_across_wait(send_sems, recv_sems, parts, recv, after, name):
    n = len(parts)

    def body(*refs):
        recv_refs = refs[n:2 * n]
        s_sems, r_sems = refs[2 * n:3 * n], refs[3 * n:4 * n]
        x, y, c, _ = _position()
        for t in range(n):
            three = recv_refs[t].at[pl.ds(0, N_CHIPS - 1)]
            cp = pltpu.make_async_remote_copy(src_ref=three, dst_ref=three, send_sem=s_sems[t], recv_sem=r_sems[t],
                                              device_id=(x, y, c), device_id_type=MESH)
            cp.wait_send()
            cp.wait_recv()

    thru = [pltpu.HBM(a.shape, a.dtype) for a in list(parts) + list(recv)]
    out = pl.pallas_call(
        body, name=name, out_shape=tuple(thru),
        in_specs=[HBM] * (2 * n) + [SEM] * (2 * n) + [pl.BlockSpec(memory_space=pl.ANY)], out_specs=(HBM,) * (2 * n),
        input_output_aliases={i: i for i in range(2 * n)},
        compiler_params=pltpu.CompilerParams(has_side_effects=EFFECT),
    )(*parts, *recv, *send_sems, *recv_sems, after)
    return list(out[:n]), list(out[n:])


def _rs_join_halves(halves, name):
    n = len(halves)

    def body(*refs):
        ins, outs = refs[:n], refs[n:2 * n]
        send_sems, recv_sems = refs[2 * n:]
        x, y, c, _ = _position()
        cps = [pltpu.make_async_remote_copy(src_ref=ins[t].at[c], dst_ref=outs[t].at[c], send_sem=send_sems.at[t],
                                            recv_sem=recv_sems.at[t], device_id=(x, y, 1 - c), device_id_type=MESH)
               for t in range(n)]
        for cp in cps:
            cp.start()
        for t in range(n):
            blk = outs[t].at[1 - c]
            pltpu.make_async_remote_copy(src_ref=blk, dst_ref=blk, send_sem=send_sems.at[t], recv_sem=recv_sems.at[t],
                                         device_id=(x, y, 1 - c), device_id_type=MESH).wait_recv()
        for cp in cps:
            cp.wait_send()

    return pl.pallas_call(
        body, name=name, in_specs=[ANY] * n, out_specs=[ANY] * n,
        out_shape=[jax.ShapeDtypeStruct(h.shape, h.dtype) for h in halves],
        input_output_aliases={t: t for t in range(n)},
        scratch_shapes=[pltpu.SemaphoreType.DMA((n,)), pltpu.SemaphoreType.DMA((n,))],
    )(*halves)


def _all_gather_small(block, name):
    m_per, n = block.shape

    def body(x_ref, out_ref, send_sems, recv_sems, local_sem):
        x, y, c, chips = _position()
        me, sibling = (x, y, c), (x, y, 1 - c)

        def rows(px, py, pc):
            return out_ref.at[pl.ds((4 * px + 2 * py + pc) * m_per, m_per), :]

        def copy(k, blk, to, src=None):
            return pltpu.make_async_remote_copy(
                src_ref=rows(*blk) if src is None else src, dst_ref=rows(*blk), send_sem=send_sems.at[k],
                recv_sem=recv_sems.at[k], device_id=to, device_id_type=MESH)

        mine = pltpu.make_async_copy(x_ref, rows(*me), local_sem)
        mine.start()
        first = [copy(0, me, sibling, src=x_ref)]
        first += [copy(1 + j, me, (*chip, c), src=x_ref) for j, chip in enumerate(chips)]
        for cp in first:
            cp.start()
        passed = [copy(4 + j, (*chip, c), sibling) for j, chip in enumerate(chips)]
        for j, chip in enumerate(chips):
            copy(1 + j, (*chip, c), me).wait_recv()
            passed[j].start()
        copy(0, sibling, me).wait_recv()
        for j, chip in enumerate(chips):
            copy(4 + j, (*chip, 1 - c), me).wait_recv()
        for cp in first + passed:
            cp.wait_send()
        mine.wait()

    return pl.pallas_call(
        body, name=name, out_shape=jax.ShapeDtypeStruct((N_DEV * m_per, n), block.dtype),
        in_specs=[pl.BlockSpec(memory_space=pltpu.VMEM)], out_specs=pl.BlockSpec(memory_space=pltpu.VMEM),
        scratch_shapes=[pltpu.SemaphoreType.DMA((7,)), pltpu.SemaphoreType.DMA((7,)), pltpu.SemaphoreType.DMA],
    )(block)


def _row_tile(R, n):
    budget = 1 << 19
    if R * n <= budget or R % SUBLANES:
        return R
    t = R
    while t * n > budget and t % (2 * SUBLANES) == 0:
        t //= 2
    return t


def _pair_sum(g, recv, c_arr, name):
    _, _, R, n = g.shape
    tr = _row_tile(R, n)

    def body(c_ref, g_ref, r_ref, o_ref):
        o_ref[...] = (g_ref[...] + r_ref[...]).astype(ICI_DTYPE)

    grid_spec = pltpu.PrefetchScalarGridSpec(
        num_scalar_prefetch=1, grid=(N_CHIPS, R // tr),
        in_specs=[pl.BlockSpec((None, None, tr, n), lambda p, i, c: (p, c[0], i, 0)),
                  pl.BlockSpec((None, tr, n), lambda p, i, c: (p, i, 0))],
        out_specs=pl.BlockSpec((None, tr, n), lambda p, i, c: (p, i, 0)))
    return pl.pallas_call(
        body, name=name, grid_spec=grid_spec, out_shape=jax.ShapeDtypeStruct(recv.shape, ICI_DTYPE),
        compiler_params=_cparams("parallel", "parallel"),
    )(c_arr, g, recv)


def _chip_sum(recv, own, where, name):
    _, R, n = recv.shape
    tr = _row_tile(R, n)

    def body(w_ref, r0, r1, r2, r3, own_ref, o_ref):
        me = w_ref[0]
        terms = [jnp.where(me == k, own_ref[...], r[...]).astype(F32) for k, r in enumerate((r0, r1, r2, r3))]
        o_ref[...] = ((terms[0] + terms[1]) + terms[2]) + terms[3]

    def slot(k):
        return pl.BlockSpec((None, tr, n), lambda i, w: (w[1 + k], i, 0))

    grid_spec = pltpu.PrefetchScalarGridSpec(
        num_scalar_prefetch=1, grid=(R // tr,),
        in_specs=[slot(0), slot(1), slot(2), slot(3), pl.BlockSpec((None, tr, n), lambda i, w: (w[0], i, 0))],
        out_specs=pl.BlockSpec((None, tr, n), lambda i, w: (w[5], i, 0)))
    return pl.pallas_call(
        body, name=name, grid_spec=grid_spec, out_shape=jax.ShapeDtypeStruct((2, R, n), F32),
        compiler_params=_cparams("parallel"),
    )(where, recv, recv, recv, recv, own)


ADAM_C1 = 1.0 / (1.0 - ADAM_B1 ** ADAM_STEP)
ADAM_C2 = 1.0 / (1.0 - ADAM_B2 ** ADAM_STEP)


def _adamw_math(w, g, m, v):
    m = ADAM_B1 * m + (1.0 - ADAM_B1) * g
    v = ADAM_B2 * v + (1.0 - ADAM_B2) * (g * g)
    delta = -ADAM_LR * ((m * ADAM_C1) / (jnp.sqrt(v * ADAM_C2) + ADAM_EPS) + ADAM_WD * w)
    return delta, m, v


def _adamw(w, g, m, v, name):
    R, n = w.shape
    tr = _row_tile(R, n)

    def body(w_ref, g_ref, m_ref, v_ref, d_ref, nm_ref, nv_ref):
        d_ref[...], nm_ref[...], nv_ref[...] = _adamw_math(w_ref[...], g_ref[...], m_ref[...], v_ref[...])

    spec = pl.BlockSpec((tr, n), lambda i: (i, 0))
    return pl.pallas_call(
        body, name=name, grid=(R // tr,), in_specs=[spec] * 4, out_specs=[spec] * 3,
        out_shape=[jax.ShapeDtypeStruct((R, n), F32)] * 3, compiler_params=_cparams("parallel"),
    )(w, g, m, v)


def _adamw_rows(w, g, m, v, row0, prev, name):
    R, n = w.shape
    rows = g.shape[0]
    tr = _row_tile(math.gcd(row0, rows), n)
    off = row0 // tr

    def body(*refs):
        w_ref, g_ref, m_ref, v_ref = refs[:4]
        go_ref, d_ref, nm_ref, nv_ref = refs[-4:]
        gv = g_ref[...]
        go_ref[...] = gv
        d_ref[...], nm_ref[...], nv_ref[...] = _adamw_math(w_ref[...], gv, m_ref[...], v_ref[...])

    at = pl.BlockSpec((tr, n), lambda i: (i + off, 0))
    carried = [] if prev is None else list(prev)
    return pl.pallas_call(
        body, name=name, grid=(rows // tr,),
        in_specs=[at, pl.BlockSpec((tr, n), lambda i: (i, 0)), at, at] + [ANY] * len(carried),
        out_specs=[at] * 4, out_shape=[jax.ShapeDtypeStruct((R, n), F32)] * 4,
        input_output_aliases={4 + i: i for i in range(len(carried))},
        compiler_params=_cparams("parallel"),
    )(w, g, m, v, *carried)


def _adamw_small(w, gall, m, v, name):
    M, n = w.shape

    def body(w_ref, g_ref, m_ref, v_ref, gs_ref, d_ref, nm_ref, nv_ref):
        g = g_ref[0:M, :]
        for d in range(1, N_DEV):
            g = g + g_ref[d * M:(d + 1) * M, :]
        gs_ref[...] = g
        d_ref[...], nm_ref[...], nv_ref[...] = _adamw_math(w_ref[...], g, m_ref[...], v_ref[...])

    return pl.pallas_call(
        body, name=name, out_shape=[jax.ShapeDtypeStruct((M, n), F32)] * 4,
    )(w, gall, m, v)


SMALL_ROWS = 24
MID_ROWS = 4


def _pack_small(ln1_g, ln1_b, ln2_g, ln2_b, norm_w, sinks, a_log, dt_bias):
    mixed = jnp.concatenate([norm_w.reshape(-1), sinks.reshape(-1), a_log.reshape(-1), dt_bias.reshape(-1)])
    mixed = jnp.pad(mixed, (0, D_MODEL - mixed.shape[0]))[None]
    pad = jnp.zeros((SMALL_ROWS - 4 * DEPTH - 1, D_MODEL), F32)
    return jnp.concatenate([ln1_g, ln1_b, ln2_g, ln2_b, mixed, pad], axis=0)


def _unpack_small(p):
    mixed = p[4 * DEPTH]
    return (p[0:4], p[4:8], p[8:12], p[12:16], mixed[0:256].reshape(2, 128), mixed[256:272].reshape(2, 8),
            mixed[272:280].reshape(2, 4), mixed[280:288].reshape(2, 4))


def _pack_mid(conv_w, rconv_w, rconv_b, b_a, b_x, lam):
    lead = conv_w.shape[0]
    flat = jnp.concatenate([conv_w.reshape(lead, -1), rconv_w.reshape(lead, -1), rconv_b, b_a, b_x, lam], axis=1)
    return jnp.pad(flat, ((0, 0), (0, MID_ROWS * D_MODEL - flat.shape[1]))).reshape(lead, MID_ROWS, D_MODEL)


def _unpack_mid(p):
    lead = p.shape[:-2]
    f = p.reshape(lead + (MID_ROWS * D_MODEL,))
    return (f[..., 0:1536].reshape(lead + (4, 384)), f[..., 1536:2560].reshape(lead + (4, 256)),
            f[..., 2560:2816], f[..., 2816:3072], f[..., 3072:3328], f[..., 3328:3584])


def _cols_from_chips(g):
    p, L, R, n = g.shape
    return g.transpose(1, 2, 0, 3).reshape(L, R, p * n)


def _rows_from_chips(g):
    p, L, R, n = g.shape
    return g.transpose(1, 0, 2, 3).reshape(L, p * R, n)


def _cols_to_chips(g):
    L, R, n4 = g.shape
    return g.reshape(L, R, N_CHIPS, n4 // N_CHIPS).transpose(2, 0, 1, 3)


def _rows_to_chips(g):
    L, R4, n = g.shape
    return g.reshape(L, N_CHIPS, R4 // N_CHIPS, n).transpose(1, 0, 2, 3)


def _halves(a):
    return a.reshape(2, -1, a.shape[-1])


def _pad_hyb_cols(w):
    z = jnp.zeros(w.shape[:-1] + (HYB_PAD - HP_BG - 2 * B_HEADS,), w.dtype)
    return jnp.concatenate([w[..., 0:512], w[..., 768:2304], w[..., 2304:2816], w[..., 512:768], w[..., 2816:2824], z], axis=-1)


def _unpad_hyb_cols(w):
    return jnp.concatenate([w[..., 0:512], w[..., 2560:2816], w[..., 512:2048], w[..., 2048:2560], w[..., 2816:2824]], axis=-1)


def _hybrid_fwd(x, W, j, tables, sfx, out_ready, ln):
    cos, sin_s = tables
    T = x.shape[0]
    N = T // B_CHUNK
    proj = _matmul(x, W["hyb_w_in"][j], "nn", "hyb_in" + sfx)
    sinks_b = jnp.broadcast_to(W["hyb_sinks"][j][:, None], (A_Q_HEADS, LANES))
    o_a, lse = _attn_fwd(proj, cos, sin_s, sinks_b, "attn_fwd" + sfx)
    bg = proj[:, HP_BG:HP_BG + 2 * B_HEADS]
    beta = jax.nn.sigmoid(bg[:, :B_HEADS])
    pre = bg[:, B_HEADS:] + W["hyb_dt_bias"][j][None]
    g = -jnp.exp(W["hyb_a_log"][j])[None] * jax.nn.softplus(pre)
    grow = g.T.reshape(B_HEADS, N, 1, B_CHUNK)
    brow = beta.T.reshape(B_HEADS, N, 1, B_CHUNK)
    nw = W["hyb_norm_w"][j][None]
    o_b, states, invs, qkvc = _dn_fwd(proj, W["hyb_conv_w"][j], nw, grow, brow, "dn_fwd" + sfx)
    mix = jnp.concatenate([o_a, o_b], axis=1).astype(ACT_DTYPE)
    out_ready(mix)
    out = _matmul(mix, W["hyb_w_out"][j], "nn", "hyb_out" + sfx, epi=_epi_residual_ln, extra=ln, out_dtype=_ln_out(),
                  tm=512)
    res = dict(proj=proj, o_a=o_a, lse=lse, qkvc=qkvc, beta=beta, pre=pre, g=g, grow=grow, brow=brow,
               states=states, invs=invs, mix=mix, sinks_b=sinks_b, nw=nw)
    return out, res


def _hybrid_bwd(x, du, W, j, res, tables, sfx):
    cos, sin_s = tables
    T = x.shape[0]
    proj = res["proj"]
    d_wout = _matmul(res["mix"], du, "tn", "hyb_out_dw" + sfx)
    dmix = _matmul(du, W["hyb_w_out"][j], "nt", "hyb_out_dx" + sfx)
    dproj, dkc, dkp, dvc, dvp, dsink = _attn_bwd(proj, cos, sin_s, res["sinks_b"], res["o_a"], res["lse"], dmix,
                                                  "attn_bwd" + sfx)
    zpad = jnp.zeros((WINDOW, LANES), F32)
    dk = dkc + jnp.concatenate([dkp[WINDOW:], zpad], axis=0)
    dv = dvc + jnp.concatenate([dvp[WINDOW:], zpad], axis=0)
    dproj, dg4, dbeta4, dnw, dconv = _dn_bwd(res["qkvc"], proj, W["hyb_conv_w"][j], res["nw"], res["grow"], res["brow"],
                                             res["states"], res["invs"], dmix, dproj, "dn_bwd" + sfx)
    dg = dg4.reshape(B_HEADS, T).T
    dbeta = dbeta4.reshape(B_HEADS, T).T
    beta = res["beta"]
    dbeta_logit = dbeta * beta * (1.0 - beta)
    da_logit = dg * (-jnp.exp(W["hyb_a_log"][j]))[None] * jax.nn.sigmoid(res["pre"])
    d_dt_bias = jnp.sum(da_logit, axis=0)
    d_a_log = jnp.sum(dg * res["g"], axis=0)
    zcols = jnp.zeros((T, HYB_PAD - HP_BG - 2 * B_HEADS), F32)
    tail = jnp.concatenate([dk, dv, dbeta_logit, da_logit, zcols], axis=1).astype(ACT_DTYPE)
    dproj = lax.dynamic_update_slice(dproj, tail, (0, HP_K))
    d_win = _matmul(x, dproj, "tn", "hyb_in_dw" + sfx)
    dx = _matmul(dproj, W["hyb_w_in"][j], "nt", "hyb_in_dx" + sfx, epi=_epi_add_residual, extra=du)
    grads = dict(hyb_w_in=d_win, hyb_w_out=d_wout, hyb_sinks=dsink[0], hyb_conv_w=dconv, hyb_a_log=d_a_log,
                 hyb_dt_bias=d_dt_bias, hyb_norm_w=jnp.sum(dnw[:, 0, :], axis=0))
    return dx, grads


def _rec_fwd(x, W, j, sfx, ln):
    proj = _matmul(x, W["rec_w_in"][j], "nn", "rec_in" + sfx)
    sp = jax.nn.softplus(-W["rec_lambda"][j])[None]
    hg, h = _rglru_fwd(proj, W["rec_conv_w"][j], W["rec_conv_b"][j][None], W["rec_w_a"][j], W["rec_w_x"][j],
                       W["rec_b_a"][j][None], W["rec_b_x"][j][None], sp, "rglru_fwd" + sfx)
    out = _matmul(hg, W["rec_w_out"][j], "nn", "rec_out" + sfx, epi=_epi_residual_ln, extra=ln, out_dtype=_ln_out(),
                  tm=512)
    return out, dict(proj=proj, hg=hg, h=h, sp=sp)


def _rec_bwd(x, du, W, j, res, sfx):
    d_wout = _matmul(res["hg"], du, "tn", "rec_out_dw" + sfx)
    dhg = _matmul(du, W["rec_w_out"][j], "nt", "rec_out_dx" + sfx)
    dproj, dcw, dcb, dwa, dwx, dba, dbx, dsp = _rglru_bwd(
        res["proj"], W["rec_conv_w"][j], W["rec_conv_b"][j][None], W["rec_w_a"][j], W["rec_w_x"][j],
        W["rec_b_a"][j][None], W["rec_b_x"][j][None], res["sp"], res["h"], dhg, "rglru_bwd" + sfx)
    d_lam = dsp[0] * (-jax.nn.sigmoid(-W["rec_lambda"][j]))
    d_win = _matmul(x, dproj, "tn", "rec_in_dw" + sfx)
    dx = _matmul(dproj, W["rec_w_in"][j], "nt", "rec_in_dx" + sfx, epi=_epi_add_residual, extra=du)
    grads = dict(rec_w_in=d_win, rec_w_out=d_wout, rec_conv_w=dcw, rec_conv_b=dcb[0], rec_w_a=dwa, rec_w_x=dwx,
                 rec_b_a=dba[0], rec_b_x=dbx[0], rec_lambda=d_lam)
    return dx, grads


def _local_step(x, tgt, W, mlp_w, mixer_ready, on_group):
    T = x.shape[0]
    tables = _rope_tables(T)
    acts = []
    xb = x.astype(ACT_DTYPE)
    for layer in range(DEPTH):
        j, sfx = layer // 2, ""
        mixer_ready(layer, xb)
        ln1 = (xb if layer else x, W["ln1_g"][layer][None], W["ln1_b"][layer][None])
        if layer % 2 == 0:
            (x1b, u1), res = _hybrid_fwd(xb, W, j, tables, sfx,
                                         functools.partial(mixer_ready, layer, out_projection=True), ln1)
        else:
            (x1b, u1), res = _rec_fwd(xb, W, j, sfx, ln1)
        w1, w2, wl = mlp_w(layer, x1b)
        h1 = _matmul(x1b, w1, "nn", "mlp_up", out_dtype=ACT_DTYPE, b_chips=("j", wl))
        x2b, u2 = _matmul(h1, w2, "nn", "mlp_down", a_fn=_relu2, b_chips=("k", wl), epi=_epi_residual_ln,
                          extra=(x1b, W["ln2_g"][layer][None], W["ln2_b"][layer][None]), out_dtype=_ln_out())
        acts.append(dict(xb=xb, res=res, u1=u1, x1b=x1b, h1=h1, u2=u2))
        xb = x2b
    per_layer = [None] * DEPTH
    d_w1 = dict(lower=lax.empty((N_CHIPS, 1, D_MODEL, D_FF // N_CHIPS), F32),
                upper=lax.empty((N_CHIPS, DEPTH - 1, D_MODEL, D_FF // N_CHIPS), F32))
    d_w2 = dict(lower=lax.empty((N_CHIPS, 1, D_FF // N_CHIPS, D_MODEL), F32),
                upper=lax.empty((N_CHIPS, DEPTH - 1, D_FF // N_CHIPS, D_MODEL), F32))
    token = None
    for layer in reversed(range(DEPTH)):
        j, a = layer // 2, acts[layer]
        grp, li = ("upper", layer - 1) if layer else ("lower", 0)
        ln2_g = W["ln2_g"][layer][None]
        if token is not None:
            ln2_g = ln2_g + token
        if layer == DEPTH - 1:
            du2, dg2, db2, loss = _ln_bwd(a["u2"], ln2_g, None, "ln_bwd_loss", loss_of=(W["ln2_b"][layer][None], tgt))
        else:
            du2, dg2, db2 = _ln_bwd(a["u2"], ln2_g, dx, "ln_bwd")
        w1, w2, wl = mlp_w(layer, du2)
        d_w2[grp] = _matmul(a["h1"], du2, "tn", "mlp_down_dw", a_fn=_relu2, out_chips=("i", li, d_w2[grp]))
        dh1 = _matmul(du2, w2, "nt", "mlp_down_dx", epi=_epi_drelu2, extra=a["h1"], out_dtype=ACT_DTYPE,
                      b_chips=("j", wl))
        d_w1[grp] = _matmul(a["x1b"], dh1, "tn", "mlp_up_dw", out_chips=("j", li, d_w1[grp]))
        dx1 = _matmul(dh1, w1, "nt", "mlp_up_dx", epi=_epi_add_residual, extra=du2, b_chips=("k", wl))
        ln1_g = W["ln1_g"][layer][None]
        token = on_group("middle", dx1, None, None, None) if layer == 0 else None
        if token is not None:
            ln1_g = ln1_g + token
        du1, dg1, db1 = _ln_bwd(a["u1"], ln1_g, dx1, "ln_bwd")
        if layer % 2 == 0:
            dx, g = _hybrid_bwd(a["xb"], du1, W, j, a["res"], tables, "")
        else:
            dx, g = _rec_bwd(a["xb"], du1, W, j, a["res"], "")
        g.update(ln1_g=dg1[0], ln1_b=db1[0], ln2_g=dg2[0], ln2_b=db2[0])
        per_layer[layer] = g
        if layer == 1:
            token = on_group("upper", [per_layer[2]], [per_layer[1], per_layer[3]], d_w1["upper"], d_w2["upper"])
        elif layer == 0:
            on_group("lower", [per_layer[0]], [], d_w1["lower"], d_w2["lower"])
    grads = {}
    for name in ("ln1_g", "ln1_b", "ln2_g", "ln2_b"):
        grads[name] = jnp.stack([per_layer[l][name] for l in range(DEPTH)])
    for name in ("hyb_norm_w", "hyb_sinks", "hyb_a_log", "hyb_dt_bias"):
        grads[name] = jnp.stack([per_layer[l][name] for l in (0, 2)])
    return loss, dx, grads


BIG = ("hyb_w_in", "hyb_w_out", "rec_w_in", "rec_w_out", "mlp_w1", "mlp_w2", "rec_w_a", "rec_w_x")
COL_SHARDED = ("hyb_w_in", "rec_w_in", "mlp_w1")
CHIP_MAJOR = ("mlp_w1", "mlp_w2")
MID = ("hyb_conv_w", "rec_conv_w", "rec_conv_b", "rec_b_a", "rec_b_x", "rec_lambda")
SMALL = ("ln1_g", "ln1_b", "ln2_g", "ln2_b", "hyb_norm_w", "hyb_sinks", "hyb_a_log", "hyb_dt_bias")
WEIGHTS = ("hyb_w_in", "hyb_sinks", "hyb_conv_w", "hyb_a_log", "hyb_dt_bias", "hyb_norm_w", "hyb_w_out", "rec_w_in",
           "rec_conv_w", "rec_conv_b", "rec_w_a", "rec_b_a", "rec_w_x", "rec_b_x", "rec_lambda", "rec_w_out", "ln1_g",
           "ln1_b", "mlp_w1", "mlp_w2", "ln2_g", "ln2_b")


def _gather_full_weights(w):
    wb = {k: w[k].astype(MXU_DTYPE) for k in BIG}
    now = ("hyb_w_in",)
    shards = [_halves(wb[k][:1]) for k in now]
    shards.append(_pack_mid(*[w[k] for k in MID]))
    got = _all_gather_weights(shards, "all_gather_weights")
    me = 2 * lax.axis_index("x") + lax.axis_index("y")
    got = [lax.dynamic_update_slice(g, s[None], (me, 0, 0, 0)) for s, g in zip(shards, got)]

    def full(k, g):
        g = g.reshape((N_CHIPS,) + w[k].shape[1:])
        if k in CHIP_MAJOR:
            return g[:, None]
        if k in ("rec_w_a", "rec_w_x"):
            return g.transpose(1, 0, 2, 3).reshape(LRU_BLOCKS, LRU_BLOCK_W, LRU_BLOCK_W)
        f = _cols_from_chips(g[:, None])[0] if k in COL_SHARDED else _rows_from_chips(g[:, None])[0]
        return _pad_hyb_cols(f) if k == "hyb_w_in" else f

    rec = ("rec_w_in", "rec_w_out", "rec_w_a", "rec_w_x")
    groups = [(("hyb_w_out",), 0), (CHIP_MAJOR, 0), (rec, 0), (CHIP_MAJOR, 1), (("hyb_w_in", "hyb_w_out"), 1),
              (CHIP_MAJOR, 2), (rec, 1), (CHIP_MAJOR, 3)]
    own = [wb[k][j] for names, j in groups for k in names]
    land = [lax.dynamic_update_slice(lax.empty((N_CHIPS,) + o.shape, o.dtype), o[None], (me,) + (0,) * o.ndim)
            for o in own]
    send_sems, recv_sems, own, land = _gather_start(own, land, got[-1], "gather_start")
    W = {k: [None] * w[k].shape[0] for k in BIG}
    for k, g in zip(now, got[:-1]):
        W[k][0] = full(k, g)
    arrived = [0]

    def ensure(upto, after):
        while arrived[0] <= upto:
            gi = arrived[0]
            names, j = groups[gi]
            lo = sum(len(nm) for nm, _ in groups[:gi])
            sl = slice(lo, lo + len(names))
            got_g = _gather_wait(send_sems[sl], recv_sems[sl], own[sl], land[sl], after, "gather_wait_%d" % gi)
            for k, g in zip(names, got_g):
                W[k][j] = full(k, g)
            arrived[0] += 1

    def mixer_ready(layer, after, out_projection=False):
        if layer:
            ensure({1: 2, 2: 4, 3: 6}[layer], after)
        elif out_projection:
            ensure(0, after)

    def mlp_w(layer, after):
        ensure({0: 1, 1: 3, 2: 5, 3: 7}[layer], after)
        return W["mlp_w1"][layer], W["mlp_w2"][layer], 0

    conv_w, rconv_w, rconv_b, b_a, b_x, lam = _unpack_mid(got[-1])
    W["hyb_conv_w"] = conv_w.transpose(1, 2, 0, 3).reshape(2, CONV_K, 3 * B_W)
    W["rec_conv_w"] = rconv_w.transpose(1, 2, 0, 3).reshape(2, CONV_K, D_MODEL)
    for k, v in (("rec_conv_b", rconv_b), ("rec_b_a", b_a), ("rec_b_x", b_x), ("rec_lambda", lam)):
        W[k] = v.transpose(1, 0, 2).reshape(2, D_MODEL)
    for k in SMALL:
        W[k] = w[k]
    return W, mlp_w, mixer_ready


REC_VECTORS = ("rec_conv_b", "rec_b_a", "rec_b_x", "rec_lambda")


def _group_by_chip(hyb, rec, d_w1, d_w2):
    t = dict(mlp_w1=d_w1, mlp_w2=d_w2)
    if hyb:
        t["hyb_w_in"] = _cols_to_chips(jnp.stack([_unpad_hyb_cols(g["hyb_w_in"]) for g in hyb]))
        t["hyb_w_out"] = _rows_to_chips(jnp.stack([g["hyb_w_out"] for g in hyb]))
    if rec:
        t["rec_w_in"] = _cols_to_chips(jnp.stack([g["rec_w_in"] for g in rec]))
        t["rec_w_out"] = _rows_to_chips(jnp.stack([g["rec_w_out"] for g in rec]))
        for k in ("rec_w_a", "rec_w_x"):
            v = jnp.stack([g[k] for g in rec])
            t[k] = v.reshape(len(rec), LRU_BLOCKS, N_CHIPS, LRU_BLOCK_W // N_CHIPS, LRU_BLOCK_W).transpose(2, 0, 1, 3, 4)
    flat = [g["hyb_conv_w"].reshape(CONV_K, N_CHIPS, -1).transpose(1, 0, 2).reshape(N_CHIPS, -1) for g in hyb]
    for g in rec:
        flat.append(g["rec_conv_w"].reshape(CONV_K, N_CHIPS, -1).transpose(1, 0, 2).reshape(N_CHIPS, -1))
        flat += [g[k].reshape(N_CHIPS, -1) for k in REC_VECTORS]
    flat = jnp.concatenate(flat, axis=1)
    flat = jnp.pad(flat, ((0, 0), (0, -flat.shape[1] % (2 * D_MODEL))))
    names = [k for k in BIG if k in t]
    out = [t[k].reshape(N_CHIPS, 2, -1, t[k].shape[-1]) for k in names]
    return names + ["small"], out + [flat.reshape(N_CHIPS, 2, -1, D_MODEL)]


def kernel(x, hyb_w_in, hyb_sinks, hyb_conv_w, hyb_a_log, hyb_dt_bias, hyb_norm_w, hyb_w_out, rec_w_in, rec_conv_w, rec_conv_b, rec_w_a, rec_b_a, rec_w_x, rec_b_x, rec_lambda, rec_w_out, ln1_g, ln1_b, mlp_w1, mlp_w2, ln2_g, ln2_b, loss_target, m_hyb_w_in, m_hyb_sinks, m_hyb_conv_w, m_hyb_a_log, m_hyb_dt_bias, m_hyb_norm_w, m_hyb_w_out, m_rec_w_in, m_rec_conv_w, m_rec_conv_b, m_rec_w_a, m_rec_b_a, m_rec_w_x, m_rec_b_x, m_rec_lambda, m_rec_w_out, m_ln1_g, m_ln1_b, m_mlp_w1, m_mlp_w2, m_ln2_g, m_ln2_b, v_hyb_w_in, v_hyb_sinks, v_hyb_conv_w, v_hyb_a_log, v_hyb_dt_bias, v_hyb_norm_w, v_hyb_w_out, v_rec_w_in, v_rec_conv_w, v_rec_conv_b, v_rec_w_a, v_rec_b_a, v_rec_w_x, v_rec_b_x, v_rec_lambda, v_rec_w_out, v_ln1_g, v_ln1_b, v_mlp_w1, v_mlp_w2, v_ln2_g, v_ln2_b):
    args = locals()
    w = {k: args[k] for k in WEIGHTS}
    m = {k: args["m_" + k] for k in WEIGHTS}
    v = {k: args["v_" + k] for k in WEIGHTS}

    W, mlp_w, mixer_ready = _gather_full_weights(w)

    core = lax.axis_index("c").astype(jnp.int32)
    me = (2 * lax.axis_index("x") + lax.axis_index("y")).astype(jnp.int32)
    slots = jnp.arange(N_CHIPS, dtype=jnp.int32)
    where = jnp.concatenate([me[None], jnp.where(slots == me, (slots + 1) % N_CHIPS, slots), core[None]])
    state = {}

    def on_group(group, hyb, rec, d_w1, d_w2):
        if group == "middle":
            group = "upper"
            send_sems, recv_sems, by_chip, recv, _ = state["sibling"]
            by_chip, from_sibling = _rs_sibling_wait(send_sems, recv_sems, by_chip, recv, hyb, "rs_sibling_wait")
        else:
            state[group + " names"], by_chip = _group_by_chip(hyb, rec, d_w1, d_w2)
            if group == "upper":
                recv = [lax.empty((N_CHIPS,) + g.shape[2:], g.dtype) for g in by_chip]
                state["sibling"] = _rs_sibling_start(by_chip, recv, "rs_sibling_start")
                return state["sibling"][4][0, 0]
            from_sibling = _rs_to_sibling(by_chip, "rs_to_sibling")
        pair = [_pair_sum(g, r, core[None], "pair_sum") for g, r in zip(by_chip, from_sibling)]
        recv = [lax.empty(p.shape, p.dtype) for p in pair]
        state[group] = _rs_across_start(pair, recv, "rs_across_start_" + group)
        return state[group][4][0, 0]

    def finish_group(group, after, prev):
        send_sems, recv_sems, pair, recv, _ = state[group]
        pair, from_chips = _rs_across_wait(send_sems, recv_sems, pair, recv, after, "rs_across_wait_" + group)
        half = [_chip_sum(r, p, where, "chip_sum") for r, p in zip(from_chips, pair)]
        joined = _rs_join_halves(half, "rs_join_halves")
        outs = dict(prev or {})
        for k, g in zip(state[group + " names"][:-1], joined[:-1]):
            n = g.shape[-1]
            w2, g2 = w[k].reshape(-1, n), g.reshape(-1, n)
            row0 = w2.shape[0] - g2.shape[0] if group == "upper" else 0
            outs[k] = _adamw_rows(w2, g2, m[k].reshape(-1, n), v[k].reshape(-1, n), row0, outs.get(k), "adamw")
        return outs, joined[-1].reshape(-1)

    loss, dx, grads = _local_step(x[0], loss_target[0], W, mlp_w, mixer_ready, on_group)
    loss = lax.psum(loss[0, 0], ("x", "y", "c"))
    g_out, d_out, m_out, v_out = {}, {}, {}, {}

    upper, small_upper = finish_group("upper", state["lower"][4], None)
    small_g = _pack_small(*[grads[k] for k in SMALL])
    small_all = _all_gather_small(small_g, "all_gather_small")
    sw, sm, sv = (_pack_small(*[t[k] for k in SMALL]) for t in (w, m, v))
    sg, sd, snm, snv = _adamw_small(sw, small_all, sm, sv, "adamw_small")
    for dst, packed in ((g_out, sg), (d_out, sd), (m_out, snm), (v_out, snv)):
        for k, val in zip(SMALL, _unpack_small(packed)):
            dst[k] = val

    done = sum([upper[k][1][0, 0] for k in BIG], sg[0, 0]).reshape(1, 1)
    both, small_lower = finish_group("lower", done, upper)
    for k in BIG:
        shape = w[k].shape
        g_out[k], d_out[k], m_out[k], v_out[k] = (t.reshape(shape) for t in both[k])
    n_conv, n_rconv, n_vec = (w[k][0].size for k in ("hyb_conv_w", "rec_conv_w", "rec_conv_b"))
    n_rec = n_rconv + len(REC_VECTORS) * n_vec
    conv_g = jnp.stack([small_lower[:n_conv], small_upper[:n_conv]])
    rec_g = jnp.stack([small_upper[n_conv + i * n_rec:n_conv + (i + 1) * n_rec] for i in range(2)])
    vec_g = [rec_g[:, n_rconv + i * n_vec:n_rconv + (i + 1) * n_vec] for i in range(len(REC_VECTORS))]
    mid_g = _pack_mid(conv_g, rec_g[:, :n_rconv], *vec_g).reshape(-1, D_MODEL)
    mid_w, mid_m, mid_v = (_pack_mid(*[t[k] for k in MID]).reshape(-1, D_MODEL) for t in (w, m, v))
    mid_d, mid_nm, mid_nv = _adamw(mid_w, mid_g, mid_m, mid_v, "adamw_mid")
    for dst, packed in ((g_out, mid_g), (d_out, mid_d), (m_out, mid_nm), (v_out, mid_nv)):
        for k, val in zip(MID, _unpack_mid(packed.reshape(2, MID_ROWS, D_MODEL))):
            dst[k] = val.reshape(w[k].shape)

    return (loss, dx[None], *[g_out[k] for k in WEIGHTS], *[d_out[k] for k in WEIGHTS],
            *[m_out[k] for k in WEIGHTS], *[v_out[k] for k in WEIGHTS])
```

```python
import functools
import math

import jax
import jax.numpy as jnp
import numpy as np
from jax import lax
from jax.experimental import pallas as pl
from jax.experimental.pallas import tpu as pltpu

F32 = jnp.float32
MXU_DTYPE = jnp.bfloat16
ACT_DTYPE = jnp.bfloat16
ICI_DTYPE = jnp.bfloat16

D_MODEL = 1024
DEPTH = 4
A_HEAD_DIM = 64
A_Q_HEADS = 8
A_KV_HEADS = 2
WINDOW = 128
ROPE_THETA = 10000.0
B_HEADS = 4
B_HEAD_DIM = 128
B_CHUNK = 64
CONV_K = 4
LRU_BLOCKS = 4
LRU_BLOCK_W = D_MODEL // LRU_BLOCKS
LRU_C = 8.0
D_FF = 4 * D_MODEL
A_Q_W = A_Q_HEADS * A_HEAD_DIM
A_KV_W = A_KV_HEADS * A_HEAD_DIM
B_W = B_HEADS * B_HEAD_DIM
HYB_PROJ = A_Q_W + 2 * A_KV_W + 4 * B_W + 2 * B_HEADS
DN_ALPHA = (2 * DEPTH) ** 0.25
LN_EPS = 1e-5
NORM_EPS = 1e-6
ADAM_LR = 0.001
ADAM_B1 = 0.9
ADAM_B2 = 0.999
ADAM_EPS = 1e-08
ADAM_WD = 0.01
ADAM_STEP = 10

HP_Q = 0
HP_QKVB = 512
HP_Z = 2048
HP_K = 2560
HP_V = 2688
HP_BG = 2816
HYB_PAD = 3072

N_CHIPS = 4
N_DEV = 8
V7X_VMEM_LIMIT = 48 * 1024 * 1024
LANES = 128
SUBLANES = 8
NEG_BIG = -1e30

NN = (((1,), (0,)), ((), ()))
NT = (((1,), (1,)), ((), ()))
TN = (((0,), (0,)), ((), ()))


def _cparams(*sem):
    return pltpu.CompilerParams(dimension_semantics=sem, vmem_limit_bytes=V7X_VMEM_LIMIT)


def _dot(a, b, dims=NN):
    return lax.dot_general(a.astype(MXU_DTYPE), b.astype(MXU_DTYPE), dims, preferred_element_type=F32)


def _split_bf16(a):
    hi = a.astype(jnp.bfloat16)
    return hi, (a - hi.astype(F32)).astype(jnp.bfloat16)


def _dotf(a, b, dims=NN):
    ah, al = _split_bf16(a)
    bh, bl = _split_bf16(b)
    dg = functools.partial(lax.dot_general, dimension_numbers=dims, preferred_element_type=F32)
    return dg(ah, bh) + (dg(ah, bl) + dg(al, bh))


def _tile(dim, pref):
    t = min(dim, pref)
    while dim % t:
        t //= 2
    return t


def _sigmoid(x):
    return 1.0 / (1.0 + jnp.exp(-x))


def _silu(x):
    return x * _sigmoid(x)


def _dsilu(x):
    s = _sigmoid(x)
    return s * (1.0 + x * (1.0 - s))


GELU_C = 0.7978845608028654
GELU_A = 0.044715


def _gelu(x):
    return 0.5 * x * (1.0 + jnp.tanh(GELU_C * (x + GELU_A * x * x * x)))


def _dgelu(x):
    t = jnp.tanh(GELU_C * (x + GELU_A * x * x * x))
    return 0.5 * (1.0 + t) + 0.5 * x * (1.0 - t * t) * GELU_C * (1.0 + 3.0 * GELU_A * x * x)


def _matmul(a, b, mode, name, *, tm=1024, tn=1024, tk=1024, a_fn=None, epi=None, extra=None, out_dtype=F32,
            b_chips=None, out_chips=None):
    chunk = tk
    if mode == "tn":
        K, M = a.shape
        if K <= 4 * tk:
            tk, tm = K, tm // 2
    else:
        M, K = a.shape
    whole_k = False
    if b_chips is not None:
        g, b_layer = b_chips
        r, n = b.shape[2:]
        n_dim, k_dim = (r, n) if mode == "nt" else (n, r)
        N = N_CHIPS * n_dim if g == "j" else n_dim
        assert K == (N_CHIPS * k_dim if g == "k" else k_dim)
        if g == "j":
            tn = n_dim
        else:
            whole_k, tk, tm = True, K, tm // 2
    elif mode == "nt":
        N = b.shape[0]
        if tk < K <= 3 * tk:
            tk, tm = K, tm // 2
    else:
        N = b.shape[1]
    if out_chips is not None:
        og, o_layer, o_buf = out_chips
        if og == "j":
            tn = o_buf.shape[3]
        else:
            tm = min(tm, o_buf.shape[2])
    tm, tn, tk = _tile(M, tm), _tile(N, tn), _tile(K, tk)
    nk = K // tk
    if mode == "tn":
        a_spec = pl.BlockSpec((tk, tm), lambda i, j, k: (k, i))
    else:
        a_spec = pl.BlockSpec((tm, tk), lambda i, j, k: (i, k))
    b_block = (tn, tk) if mode == "nt" else (tk, tn)
    if b_chips is None:
        b_spec = pl.BlockSpec(b_block, (lambda i, j, k: (j, k)) if mode == "nt" else (lambda i, j, k: (k, j)))
    elif mode == "nt" and whole_k:
        b_spec = pl.BlockSpec((N_CHIPS, None, tn, k_dim), lambda i, j, k: (0, b_layer, j, 0))
    elif mode == "nt":
        b_spec = pl.BlockSpec((None, None) + b_block, lambda i, j, k: (j, b_layer, 0, k))
    elif whole_k:
        b_spec = pl.BlockSpec((N_CHIPS, None, k_dim, tn), lambda i, j, k: (0, b_layer, 0, j),
                              pipeline_mode=pl.Buffered(1) if N == tn else None)
    else:
        b_spec = pl.BlockSpec((None, None) + b_block, lambda i, j, k: (j, b_layer, k, 0))
    o_spec = pl.BlockSpec((tm, tn), lambda i, j, k: (i, j))
    e_spec = o_spec
    if out_chips is not None:
        per = o_buf.shape[2] // tm
        o_spec = pl.BlockSpec((None, None, tm, tn), (lambda i, j, k: (j, o_layer, i, 0)) if og == "j"
                              else (lambda i, j, k: (i // per, o_layer, i % per, j)))
    dims = {"nn": NN, "nt": NT, "tn": TN}[mode]
    extras = () if extra is None else (extra if isinstance(extra, tuple) else (extra,))
    out_dtypes = out_dtype if isinstance(out_dtype, tuple) else (out_dtype,)
    n_in = 2 + len(extras) + (out_chips is not None)

    def body(*refs):
        a_ref, b_ref = refs[0], refs[1]
        e_refs = refs[2:2 + len(extras)]
        o_refs = refs[n_in:n_in + len(out_dtypes)]
        av = a_ref[...]
        if a_fn is not None:
            av = a_fn(av)
        if whole_k and mode == "nt":
            part = _dot(av[:, :k_dim], b_ref[0], dims)
            for chip in range(1, N_CHIPS):
                part = part + _dot(av[:, chip * k_dim:(chip + 1) * k_dim], b_ref[chip], dims)
        elif mode == "tn" and tk > chunk:
            part = _dot(av[:chunk], b_ref[0:chunk, :], dims)
            for c0 in range(chunk, tk, chunk):
                part = part + _dot(av[c0:c0 + chunk], b_ref[c0:c0 + chunk, :], dims)
        else:
            bv = b_ref[...]
            if whole_k:
                bv = bv.reshape(K, tn)
            part = _dot(av, bv, dims)

        def finish(acc):
            if epi is not None:
                acc = epi(acc, *[e[...] for e in e_refs])
            for o_ref, val, dt in zip(o_refs, acc if isinstance(acc, tuple) else (acc,), out_dtypes):
                o_ref[...] = val.astype(dt)

        if nk == 1:
            finish(part)
        else:
            acc_ref = refs[-1]
            k = pl.program_id(2)

            @pl.when(k == 0)
            def _():
                acc_ref[...] = part

            @pl.when(k > 0)
            def _():
                acc_ref[...] += part

            @pl.when(k == nk - 1)
            def _():
                finish(acc_ref[...])

    row_spec = pl.BlockSpec((1, tn), lambda i, j, k: (0, j))
    in_specs = [a_spec, b_spec] + [row_spec if e.shape[0] == 1 else e_spec for e in extras]
    args = (a, b) + extras
    out_shape = [jax.ShapeDtypeStruct((M, N), dt) for dt in out_dtypes]
    out_specs = [o_spec] * len(out_dtypes)
    aliases = {}
    if out_chips is not None:
        in_specs.append(pl.BlockSpec(memory_space=pl.ANY))
        args += (o_buf,)
        out_shape = [jax.ShapeDtypeStruct(o_buf.shape, o_buf.dtype)]
        aliases = {n_in - 1: 0}
    if not isinstance(out_dtype, tuple):
        out_shape, out_specs = out_shape[0], out_specs[0]
    return pl.pallas_call(
        body, name=name, grid=(M // tm, N // tn, nk), in_specs=in_specs, out_specs=out_specs, out_shape=out_shape,
        input_output_aliases=aliases,
        scratch_shapes=[pltpu.VMEM((tm, tn), F32)] if nk > 1 else [],
        compiler_params=_cparams("parallel", "parallel", "arbitrary"),
    )(*args)


def _relu2(v):
    r = jnp.maximum(v, 0.0)
    return r * r


def _epi_drelu2(acc, h):
    return acc * (2.0 * jnp.maximum(h, 0.0))


def _epi_add_residual(acc, du):
    return acc + DN_ALPHA * du


def _epi_residual_ln(acc, x, g, b):
    u = DN_ALPHA * x + acc
    mu = jnp.mean(u, axis=-1, keepdims=True)
    d = u - mu
    var = jnp.mean(d * d, axis=-1, keepdims=True)
    return d * lax.rsqrt(var + LN_EPS) * g + b, u


def _ln_out():
    return (ACT_DTYPE, F32)


def _ln_bwd(u, g, dout, name, loss_of=None):
    T, D = u.shape
    tr = _tile(T, 512)
    with_loss = loss_of is not None

    def body(*refs):
        if with_loss:
            u_ref, g_ref, b_ref, t_ref, dub_ref, dg_ref, db_ref, l_ref = refs
        else:
            u_ref, g_ref, d_ref, dub_ref, dg_ref, db_ref = refs
        i = pl.program_id(0)
        u = u_ref[...]
        mu = jnp.mean(u, axis=-1, keepdims=True)
        d = u - mu
        rstd = lax.rsqrt(jnp.mean(d * d, axis=-1, keepdims=True) + LN_EPS)
        xhat = d * rstd
        if with_loss:
            err = xhat * g_ref[...] + b_ref[...] - t_ref[...]
            dout_v = err * (1.0 / D)
            part = jnp.sum(err * err, axis=(0, 1), keepdims=True) * (0.5 / D)

            @pl.when(i == 0)
            def _():
                l_ref[...] = part

            @pl.when(i > 0)
            def _():
                l_ref[...] += part
        else:
            dout_v = d_ref[...]
        dxh = dout_v * g_ref[...]
        m1 = jnp.mean(dxh, axis=-1, keepdims=True)
        m2 = jnp.mean(dxh * xhat, axis=-1, keepdims=True)
        du = rstd * (dxh - m1 - xhat * m2)
        dub_ref[...] = du.astype(ACT_DTYPE)
        pg = jnp.sum(dout_v * xhat, axis=0, keepdims=True)
        pb = jnp.sum(dout_v, axis=0, keepdims=True)

        @pl.when(i == 0)
        def _():
            dg_ref[...] = pg
            db_ref[...] = pb

        @pl.when(i > 0)
        def _():
            dg_ref[...] += pg
            db_ref[...] += pb

    row = pl.BlockSpec((tr, D), lambda i: (i, 0))
    vec = pl.BlockSpec((1, D), lambda i: (0, 0))
    out_specs = [row, vec, vec]
    out_shape = [jax.ShapeDtypeStruct((T, D), ACT_DTYPE), jax.ShapeDtypeStruct((1, D), F32),
                 jax.ShapeDtypeStruct((1, D), F32)]
    if with_loss:
        return pl.pallas_call(
            body, name=name, grid=(T // tr,), in_specs=[row, vec, vec, row],
            out_specs=out_specs + [pl.BlockSpec((1, 1), lambda i: (0, 0))],
            out_shape=out_shape + [jax.ShapeDtypeStruct((1, 1), F32)], compiler_params=_cparams("arbitrary"),
        )(u, g, loss_of[0], loss_of[1])
    return pl.pallas_call(
        body, name=name, grid=(T // tr,), in_specs=[row, vec, row], out_specs=out_specs, out_shape=out_shape,
        compiler_params=_cparams("arbitrary"),
    )(u, g, dout)


def _swap_half(x):
    n = x.shape[-1]
    lane = lax.broadcasted_iota(jnp.int32, x.shape, 1)
    first = (lane % A_HEAD_DIM) < (A_HEAD_DIM // 2)
    return jnp.where(first, pltpu.roll(x, n - A_HEAD_DIM // 2, axis=1), pltpu.roll(x, A_HEAD_DIM // 2, axis=1))


def _rope(x, cos, sin_signed):
    return x * cos + _swap_half(x) * sin_signed


def _rope_t(dy, cos, sin_signed):
    return dy * cos + _swap_half(dy * sin_signed)


def _rope_tables(T):
    half = A_HEAD_DIM // 2
    inv_freq = np.float32(ROPE_THETA) ** (-np.arange(half, dtype=np.float32) / np.float32(half))
    ang = np.arange(T, dtype=np.float32)[:, None] * inv_freq[None, :]
    cos = np.tile(np.cos(ang), (1, 4))
    sin = np.sin(ang)
    sin_signed = np.tile(np.concatenate([-sin, sin], axis=1), (1, 2))
    return jnp.asarray(cos, F32), jnp.asarray(sin_signed, F32)


def _band_mask(n):
    qi = lax.broadcasted_iota(jnp.int32, (WINDOW, 2 * WINDOW), 0)
    kj = lax.broadcasted_iota(jnp.int32, (WINDOW, 2 * WINDOW), 1)
    return (kj > qi) & (kj <= qi + WINDOW) & ((n > 0) | (kj >= WINDOW))


def _place(v, src_half, dst_half):
    lane = lax.broadcasted_iota(jnp.int32, v.shape, 1)
    if src_half != dst_half:
        v = pltpu.roll(v, A_HEAD_DIM, axis=1)
    keep = (lane >= A_HEAD_DIM) if dst_half else (lane < A_HEAD_DIM)
    return jnp.where(keep, v, 0.0)


def _attn_specs():
    kb, vb = HP_K // LANES, HP_V // LANES
    prev = lambda n: jnp.maximum(n - 1, 0)
    return dict(
        q=pl.BlockSpec((WINDOW, A_Q_W), lambda n: (n, 0)),
        kc=pl.BlockSpec((WINDOW, LANES), lambda n: (n, kb)),
        kp=pl.BlockSpec((WINDOW, LANES), lambda n: (prev(n), kb)),
        vc=pl.BlockSpec((WINDOW, LANES), lambda n: (n, vb)),
        vp=pl.BlockSpec((WINDOW, LANES), lambda n: (prev(n), vb)),
        tq=pl.BlockSpec((WINDOW, LANES), lambda n: (n, 0)),
        tp=pl.BlockSpec((WINDOW, LANES), lambda n: (prev(n), 0)),
        sink=pl.BlockSpec((A_Q_HEADS, LANES), lambda n: (0, 0)),
        row512=pl.BlockSpec((WINDOW, A_Q_W), lambda n: (n, 0)),
        row128=pl.BlockSpec((WINDOW, LANES), lambda n: (n, 0)),
        lse=pl.BlockSpec((WINDOW, A_Q_HEADS), lambda n: (n, 0)),
    )


def _attn_fwd(proj, cos, sin_s, sinks_b, name):
    T = proj.shape[0]
    sp = _attn_specs()

    def body(q_ref, kc_ref, kp_ref, vc_ref, vp_ref, cq_ref, sq_ref, cp_ref, sp_ref, sink_ref, o_ref, l_ref):
        n = pl.program_id(0)
        cq, sq = cq_ref[...], sq_ref[...]
        cq4, sq4 = jnp.tile(cq, (1, A_Q_W // LANES)), jnp.tile(sq, (1, A_Q_W // LANES))
        q = _rope(q_ref[...], cq4, sq4) * (A_HEAD_DIM ** -0.5)
        kc = _rope(kc_ref[...], cq, sq)
        kp = _rope(kp_ref[...], cp_ref[...], sp_ref[...])
        kk = jnp.concatenate([kp, kc], axis=0)
        vv = jnp.concatenate([vp_ref[...], vc_ref[...]], axis=0)
        mask = _band_mask(n)[None]
        lane = lax.broadcasted_iota(jnp.int32, (WINDOW, LANES), 1)
        lane8 = lax.broadcasted_iota(jnp.int32, (WINDOW, A_Q_HEADS), 1)
        qe = jnp.stack([_place(q[:, (hq // 2) * LANES:(hq // 2 + 1) * LANES], hq % 2, hq // 4) for hq in range(A_Q_HEADS)])
        kk8 = jnp.broadcast_to(kk[None], (A_Q_HEADS,) + kk.shape)
        vv8 = jnp.broadcast_to(vv[None], (A_Q_HEADS,) + vv.shape)
        sk = jnp.stack([sink_ref[hq:hq + 1, 0:1] for hq in range(A_Q_HEADS)])
        s = jnp.where(mask, _dot(qe, kk8, BNT), NEG_BIG)
        m = jnp.maximum(jnp.max(s, axis=-1, keepdims=True), sk)
        p = jnp.exp(s - m)
        den = jnp.sum(p, axis=-1, keepdims=True) + jnp.exp(sk - m)
        o = _dot(p * (1.0 / den), vv8, BNN)
        lse_h = m + jnp.log(den)
        outs = []
        lse = jnp.zeros((WINDOW, A_Q_HEADS), F32)
        for pb in range(A_Q_HEADS // 2):
            halves = [_place(o[2 * pb + e], pb // 2, e) for e in range(2)]
            outs.append(jnp.where(lane < A_HEAD_DIM, halves[0], halves[1]))
            for e in range(2):
                lse = jnp.where(lane8 == 2 * pb + e, lse_h[2 * pb + e], lse)
        o_ref[...] = jnp.concatenate(outs, axis=1)
        l_ref[...] = lse

    return pl.pallas_call(
        body, name=name, grid=(T // WINDOW,),
        in_specs=[sp["q"], sp["kc"], sp["kp"], sp["vc"], sp["vp"], sp["tq"], sp["tq"], sp["tp"], sp["tp"], sp["sink"]],
        out_specs=[sp["row512"], sp["lse"]],
        out_shape=[jax.ShapeDtypeStruct((T, A_Q_W), F32), jax.ShapeDtypeStruct((T, A_Q_HEADS), F32)],
        compiler_params=_cparams("parallel"),
    )(proj, proj, proj, proj, proj, cos, sin_s, cos, sin_s, sinks_b)


def _attn_bwd(proj, cos, sin_s, sinks_b, o, lse, dmix, name):
    T = proj.shape[0]
    sp = _attn_specs()

    def body(q_ref, kc_ref, kp_ref, vc_ref, vp_ref, cq_ref, sq_ref, cp_ref, sp_ref, sink_ref, o_ref, l_ref, do_ref,
             dq_ref, dkc_ref, dkp_ref, dvc_ref, dvp_ref, dsink_ref):
        n = pl.program_id(0)
        cq, sq = cq_ref[...], sq_ref[...]
        cp, sps = cp_ref[...], sp_ref[...]
        cq4, sq4 = jnp.tile(cq, (1, A_Q_W // LANES)), jnp.tile(sq, (1, A_Q_W // LANES))
        q = _rope(q_ref[...], cq4, sq4) * (A_HEAD_DIM ** -0.5)
        kc = _rope(kc_ref[...], cq, sq)
        kp = _rope(kp_ref[...], cp, sps)
        kk = jnp.concatenate([kp, kc], axis=0)
        vv = jnp.concatenate([vp_ref[...], vc_ref[...]], axis=0)
        mask = _band_mask(n)[None]
        lane = lax.broadcasted_iota(jnp.int32, (WINDOW, LANES), 1)
        do_all, o_all, l_all = do_ref[...], o_ref[...], l_ref[...]
        lane8 = lax.broadcasted_iota(jnp.int32, (WINDOW, A_Q_HEADS), 1)
        head8 = lax.broadcasted_iota(jnp.int32, (1, A_Q_HEADS), 1)
        prod = do_all * o_all
        qes, does, deltas, lhs = [], [], [], []
        for hq in range(A_Q_HEADS):
            pb, e, kvh = hq // 2, hq % 2, hq // 4
            blk = slice(pb * LANES, (pb + 1) * LANES)
            in_half = (lane >= A_HEAD_DIM) if e else (lane < A_HEAD_DIM)
            deltas.append(jnp.sum(jnp.where(in_half, prod[:, blk], 0.0), axis=-1, keepdims=True))
            qes.append(_place(q[:, blk], e, kvh))
            does.append(_place(do_all[:, blk], e, kvh))
            lhs.append(jnp.sum(jnp.where(lane8 == hq, l_all, 0.0), axis=-1, keepdims=True))
        qe, doe, delta, lh = jnp.stack(qes), jnp.stack(does), jnp.stack(deltas), jnp.stack(lhs)
        kk8 = jnp.broadcast_to(kk[None], (A_Q_HEADS,) + kk.shape)
        vv8 = jnp.broadcast_to(vv[None], (A_Q_HEADS,) + vv.shape)
        sk = jnp.stack([sink_ref[hq:hq + 1, 0:1] for hq in range(A_Q_HEADS)])
        s = _dot(qe, kk8, BNT)
        p = jnp.where(mask, jnp.exp(jnp.where(mask, s, NEG_BIG) - lh), 0.0)
        dvv = jnp.sum(_dot(p, doe, BTN), axis=0)
        ds = p * (_dot(doe, vv8, BNT) - delta)
        dkk = jnp.sum(_dot(ds, qe, BTN), axis=0)
        dqe = _dot(ds, kk8, BNN)
        dsink_h = -jnp.sum(jnp.exp(sk - lh) * delta, axis=(1, 2), keepdims=True)
        dqs = []
        dsk = jnp.zeros((1, A_Q_HEADS), F32)
        for pb in range(A_Q_HEADS // 2):
            halves = [_place(dqe[2 * pb + e], pb // 2, e) for e in range(2)]
            dqs.append(jnp.where(lane < A_HEAD_DIM, halves[0], halves[1]))
            for e in range(2):
                dsk = jnp.where(head8 == 2 * pb + e, dsink_h[2 * pb + e], dsk)
        dq = jnp.concatenate(dqs, axis=1) * (A_HEAD_DIM ** -0.5)
        dq_ref[...] = _rope_t(dq, cq4, sq4).astype(ACT_DTYPE)
        dkp_ref[...] = _rope_t(dkk[:WINDOW], cp, sps)
        dkc_ref[...] = _rope_t(dkk[WINDOW:], cq, sq)
        dvp_ref[...] = dvv[:WINDOW]
        dvc_ref[...] = dvv[WINDOW:]

        @pl.when(n == 0)
        def _():
            dsink_ref[...] = dsk

        @pl.when(n > 0)
        def _():
            dsink_ref[...] += dsk

    return pl.pallas_call(
        body, name=name, grid=(T // WINDOW,),
        in_specs=[sp["q"], sp["kc"], sp["kp"], sp["vc"], sp["vp"], sp["tq"], sp["tq"], sp["tp"], sp["tp"], sp["sink"],
                  sp["row512"], sp["lse"], sp["row512"]],
        out_specs=[sp["row512"], sp["row128"], sp["row128"], sp["row128"], sp["row128"],
                   pl.BlockSpec((1, A_Q_HEADS), lambda n: (0, 0))],
        out_shape=[jax.ShapeDtypeStruct((T, HYB_PAD), ACT_DTYPE)] + [jax.ShapeDtypeStruct((T, LANES), F32)] * 4
        + [jax.ShapeDtypeStruct((1, A_Q_HEADS), F32)],
        compiler_params=_cparams("arbitrary"),
    )(proj, proj, proj, proj, proj, cos, sin_s, cos, sin_s, sinks_b, o, lse, dmix)


def _shift_down(x, prev8, k):
    if k == 0:
        return x
    R, W = x.shape
    r = pltpu.roll(jnp.concatenate([prev8, x], axis=0).reshape(R // SUBLANES + 1, SUBLANES, W), k, axis=1)
    sub = lax.broadcasted_iota(jnp.int32, (R // SUBLANES, SUBLANES, W), 1)
    return jnp.where(sub < k, r[:-1], r[1:]).reshape(R, W)


def _shift_up(x, next8, k):
    if k == 0:
        return x
    R, W = x.shape
    r = pltpu.roll(jnp.concatenate([x, next8], axis=0).reshape(R // SUBLANES + 1, SUBLANES, W), SUBLANES - k, axis=1)
    sub = lax.broadcasted_iota(jnp.int32, (R // SUBLANES, SUBLANES, W), 1)
    return jnp.where(sub >= SUBLANES - k, r[1:], r[:-1]).reshape(R, W)


def _conv(x, prev8, w):
    w0, w1, w2, w3 = (w[j:j + 1] for j in range(CONV_K))
    x1 = _shift_down(x, prev8, 1)
    v = x * w1 + x1 * w0
    v_prev8 = prev8 * w1 + pltpu.roll(prev8, 1, axis=0) * w0
    return x * w3 + x1 * w2 + _shift_down(v, v_prev8, 2)


def _conv_bwd(x, w, dy, next8_dy):
    dx = dy * w[CONV_K - 1:CONV_K]
    dws = []
    for j in range(CONV_K - 1):
        up = _shift_up(dy, next8_dy, CONV_K - 1 - j)
        dx = dx + up * w[j:j + 1]
        dws.append(jnp.sum(up * x, axis=0, keepdims=True))
    dws.append(jnp.sum(dy * x, axis=0, keepdims=True))
    return dx, dws


DK_SCALE = B_HEAD_DIM ** -0.5


BNN = (((2,), (1,)), ((0,), (0,)))
BNT = (((2,), (2,)), ((0,), (0,)))
BTN = (((1,), (1,)), ((0,), (0,)))


def _tri_inv(a):
    C = a.shape[-1]
    ri = lax.broadcasted_iota(jnp.int32, (C, C), 0)
    ci = lax.broadcasted_iota(jnp.int32, (C, C), 1)
    x = jnp.where(ri == ci, 1.0, 0.0)[None] - a
    p = _dotf(a, a, BNN)
    span = 2
    while span < C:
        dot = _dotf if span <= 4 else _dot
        x = x + dot(x, p, BNN)
        span *= 2
        if span < C:
            p = dot(p, p, BNN)
    return x


def _dn_chunk(qc, kc, v, gcol, grow, bcol, s0, tm=None):
    C = B_CHUNK
    ri = lax.broadcasted_iota(jnp.int32, (C, C), 0)
    ci = lax.broadcasted_iota(jnp.int32, (C, C), 1)
    incl, strict = (ri >= ci)[None], (ri > ci)[None]
    rq = lax.rsqrt(jnp.sum(qc * qc, axis=-1, keepdims=True) + NORM_EPS)
    rk = lax.rsqrt(jnp.sum(kc * kc, axis=-1, keepdims=True) + NORM_EPS)
    qn = qc * rq
    q = qn * DK_SCALE
    k = kc * rk
    gc_col = jnp.sum(jnp.where(incl, grow, 0.0), axis=2, keepdims=True)
    gc_row = jnp.sum(jnp.where((ri <= ci)[None], gcol, 0.0), axis=1, keepdims=True)
    gl = jnp.sum(gcol, axis=1, keepdims=True)
    dincl = jnp.where(incl, jnp.exp(jnp.where(incl, gc_col - gc_row, 0.0)), 0.0)
    dstrict = jnp.where(strict, dincl, 0.0)
    eg = jnp.exp(gc_col)
    ekt = jnp.exp(gl - gc_col)
    egl = jnp.exp(gl)
    kb = k * bcol
    vb = v * bcol
    kbg = kb * eg
    a = _dot(kb, k, BNT) * dstrict
    if tm is None:
        tm = _tri_inv(a)
    u = _dot(tm, vb, BNN)
    w = _dot(tm, kbg, BNN)
    vn = u - _dot(w, s0, BNN)
    qk = _dot(q, k, BNT) * dincl
    qg = q * eg
    kt = k * ekt
    o = _dot(qg, s0, BNN) + _dot(qk, vn, BNN)
    s1 = s0 * egl + _dot(kt, vn, BTN)
    return dict(rq=rq, rk=rk, qn=qn, q=q, k=k, dincl=dincl, dstrict=dstrict, eg=eg, ekt=ekt, egl=egl, kb=kb, vb=vb,
                kbg=kbg, a=a, tm=tm, w=w, vn=vn, qk=qk, qg=qg, kt=kt, o=o, s1=s1, ri=ri[None], ci=ci[None])


def _heads(ref):
    return jnp.stack([ref[:, h * B_HEAD_DIM:(h + 1) * B_HEAD_DIM] for h in range(B_HEADS)])


def _store_heads(ref, val):
    for h in range(B_HEADS):
        ref[:, h * B_HEAD_DIM:(h + 1) * B_HEAD_DIM] = val[h]


def _dn_specs(N, rev):
    ix = (lambda n: N - 1 - n) if rev else (lambda n: n)
    wide = lambda cb: pl.BlockSpec((B_CHUNK, B_W), lambda n: (ix(n), cb))
    return dict(
        q=wide(0), k=wide(1), v=wide(2), z=wide(HP_Z // B_W), dob=wide(A_Q_W // B_W), out=wide(0),
        nw=pl.BlockSpec((1, LANES), lambda n: (0, 0)),
        row=pl.BlockSpec((B_HEADS, None, 1, B_CHUNK), lambda n: (0, ix(n), 0, 0)),
        state=pl.BlockSpec((B_HEADS, None, B_HEAD_DIM, B_HEAD_DIM), lambda n: (0, ix(n), 0, 0)),
        inv=pl.BlockSpec((B_HEADS, None, B_CHUNK, B_CHUNK), lambda n: (0, ix(n), 0, 0)),
    )


def _to_col(row):
    C = row.shape[-1]
    eye = lax.broadcasted_iota(jnp.int32, (C, C), 0) == lax.broadcasted_iota(jnp.int32, (C, C), 1)
    return jnp.sum(jnp.where(eye[None], row, 0.0), axis=2, keepdims=True)


def _to_row(col):
    C = col.shape[1]
    eye = lax.broadcasted_iota(jnp.int32, (C, C), 0) == lax.broadcasted_iota(jnp.int32, (C, C), 1)
    return jnp.sum(jnp.where(eye[None], col, 0.0), axis=1, keepdims=True)


def _dn_fwd(proj, conv_w, norm_w, grow, brow, name):
    T = proj.shape[0]
    N = T // B_CHUNK
    sp = _dn_specs(N, False)
    cb0 = HP_QKVB // B_W

    def body(xq_ref, xk_ref, xv_ref, cw_ref, z_ref, nw_ref, gr_ref, br_ref, o_ref, st_ref, tm_ref, c_ref,
             s_ref, prev_ref):
        n = pl.program_id(0)

        @pl.when(n == 0)
        def _():
            s_ref[...] = jnp.zeros_like(s_ref)
            prev_ref[...] = jnp.zeros_like(prev_ref)

        qkv = []
        for part, x_ref in enumerate((xq_ref, xk_ref, xv_ref)):
            cols = slice(part * B_W, (part + 1) * B_W)
            x = x_ref[...]
            xc = _silu(_conv(x, prev_ref[:, cols], cw_ref[:, cols]))
            prev_ref[:, cols] = x[B_CHUNK - SUBLANES:]
            c_ref[:, cols] = xc
            qkv.append(jnp.stack([xc[:, h * B_HEAD_DIM:(h + 1) * B_HEAD_DIM] for h in range(B_HEADS)]))
        s0 = s_ref[...]
        st_ref[...] = s0
        grow_v = gr_ref[...]
        f = _dn_chunk(qkv[0], qkv[1], qkv[2], _to_col(grow_v), grow_v, _to_col(br_ref[...]), s0)
        o = f["o"]
        r = lax.rsqrt(jnp.mean(o * o, axis=-1, keepdims=True) + NORM_EPS)
        _store_heads(o_ref, o * r * nw_ref[...][None] * _silu(_heads(z_ref)))
        s_ref[...] = f["s1"]
        tm_ref[...] = f["tm"]

    wide = lambda cb: pl.BlockSpec((B_CHUNK, B_W), lambda n: (n, cb))
    return pl.pallas_call(
        body, name=name, grid=(N,),
        in_specs=[wide(cb0), wide(cb0 + 1), wide(cb0 + 2), pl.BlockSpec((CONV_K, 3 * B_W), lambda n: (0, 0)),
                  sp["z"], sp["nw"], sp["row"], sp["row"]],
        out_specs=[sp["out"], sp["state"], sp["inv"], pl.BlockSpec((B_CHUNK, 3 * B_W), lambda n: (n, 0))],
        out_shape=[jax.ShapeDtypeStruct((T, B_W), F32),
                   jax.ShapeDtypeStruct((B_HEADS, N, B_HEAD_DIM, B_HEAD_DIM), F32),
                   jax.ShapeDtypeStruct((B_HEADS, N, B_CHUNK, B_CHUNK), F32),
                   jax.ShapeDtypeStruct((T, 3 * B_W), F32)],
        scratch_shapes=[pltpu.VMEM((B_HEADS, B_HEAD_DIM, B_HEAD_DIM), F32), pltpu.VMEM((SUBLANES, 3 * B_W), F32)],
        compiler_params=_cparams("arbitrary"),
    )(proj, proj, proj, conv_w, proj, norm_w, grow, brow)


def _dn_bwd(qkvc, proj, conv_w, norm_w, grow, brow, states, invs, dmix, dproj, name):
    T = qkvc.shape[0]
    N = T // B_CHUNK
    sp = _dn_specs(N, True)
    C = B_CHUNK
    cb0 = HP_QKVB // B_W

    def body(q_ref, k_ref, v_ref, z_ref, nw_ref, gr_ref, br_ref, st_ref, tm_ref, dob_ref, dproj_in,
             xq_ref, xk_ref, xv_ref, pq_ref, pk_ref, pv_ref, cw_ref,
             dp_ref, dg_ref, db_ref, dnw_ref, dcw_ref, ds_ref, nxt_ref):
        n = pl.program_id(0)
        dp_ref[...] = dproj_in[...]
        dz_ref = dp_ref.at[:, HP_Z:HP_Z + B_W]
        d_qkv = {}

        @pl.when(n == 0)
        def _():
            ds_ref[...] = jnp.zeros_like(ds_ref)
            dnw_ref[...] = jnp.zeros_like(dnw_ref)
            dcw_ref[...] = jnp.zeros_like(dcw_ref)
            nxt_ref[...] = jnp.zeros_like(nxt_ref)

        s0 = st_ref[...]
        ds1 = ds_ref[...]
        v, z, nw, bcol_v = _heads(v_ref), _heads(z_ref), nw_ref[...][None], _to_col(br_ref[...])
        grow_v = gr_ref[...]
        f = _dn_chunk(_heads(q_ref), _heads(k_ref), v, _to_col(grow_v), grow_v, bcol_v, s0, tm=tm_ref[...])
        o, q, k, qn = f["o"], f["q"], f["k"], f["qn"]
        eg, ekt, egl = f["eg"], f["ekt"], f["egl"]
        tm, w, vn, kb, vb, kbg = f["tm"], f["w"], f["vn"], f["kb"], f["vb"], f["kbg"]
        qg, kt, qk, a = f["qg"], f["kt"], f["qk"], f["a"]
        ri, ci = f["ri"], f["ci"]

        dob_v = _heads(dob_ref)
        r = lax.rsqrt(jnp.mean(o * o, axis=-1, keepdims=True) + NORM_EPS)
        sz = _silu(z)
        on = o * r
        dnw_ref[...] += jnp.sum(dob_v * sz * on, axis=1, keepdims=True)
        _store_heads(dz_ref, (dob_v * on * nw * _dsilu(z)).astype(ACT_DTYPE))
        d_on = dob_v * sz * nw
        do = r * (d_on - on * jnp.mean(d_on * on, axis=-1, keepdims=True))

        dvn = _dot(qk, do, BTN) + _dot(kt, ds1, BNN)
        dqk = _dot(do, vn, BNT)
        dqg = _dot(do, s0, BNT)
        ds_ref[...] = _dot(qg, do, BTN) + egl * ds1 - _dot(w, dvn, BTN)
        dgl = jnp.sum(s0 * ds1, axis=(1, 2), keepdims=True) * egl
        dkt = _dot(vn, ds1, BNT)
        dw = -_dot(dvn, s0, BNT)
        dq = dqg * eg
        dgc = jnp.sum(dqg * qg, axis=-1, keepdims=True)
        dk = dkt * ekt
        t_kt = jnp.sum(dkt * kt, axis=-1, keepdims=True)
        dgl = dgl + jnp.sum(t_kt, axis=1, keepdims=True)
        dgc = dgc - t_kt
        dqkr = dqk * f["dincl"]
        dq = dq + _dot(dqkr, k, BNN)
        dk = dk + _dot(dqkr, q, BTN)
        e_qk = dqk * qk
        dgc = dgc + jnp.sum(e_qk, axis=-1, keepdims=True)
        dgc_row = -jnp.sum(e_qk, axis=1, keepdims=True)
        dtm = _dot(dvn, vb, BNT) + _dot(dw, kbg, BNT)
        dvb = _dot(tm, dvn, BTN)
        dkbg = _dot(tm, dw, BTN)
        dkb = dkbg * eg
        dgc = dgc + jnp.sum(dkbg * kbg, axis=-1, keepdims=True)
        da = -_dotf(tm, _dotf(dtm, tm, BNT), BTN)
        dkk = da * f["dstrict"]
        e_a = da * a
        dgc = dgc + jnp.sum(e_a, axis=-1, keepdims=True)
        dgc_row = dgc_row - jnp.sum(e_a, axis=1, keepdims=True)
        dkb = dkb + _dot(dkk, k, BNN)
        dk = dk + _dot(dkk, kb, BTN)
        dk = dk + dkb * bcol_v
        db_ref[...] = _to_row(jnp.sum(dkb * k, axis=-1, keepdims=True) + jnp.sum(dvb * v, axis=-1, keepdims=True))
        d_qkv[2] = dvb * bcol_v
        dgc_row = dgc_row + jnp.sum(jnp.where(ri == ci, dgc, 0.0), axis=1, keepdims=True)
        dg_ref[...] = jnp.sum(jnp.where(ci <= ri, _to_col(dgc_row), 0.0), axis=1, keepdims=True) + dgl
        dqs = dq * DK_SCALE
        d_qkv[0] = f["rq"] * (dqs - qn * jnp.sum(dqs * qn, axis=-1, keepdims=True))
        d_qkv[1] = f["rk"] * (dk - k * jnp.sum(dk * k, axis=-1, keepdims=True))
        first = n < N - 1
        for part, (x_ref, p_ref) in enumerate(((xq_ref, pq_ref), (xk_ref, pk_ref), (xv_ref, pv_ref))):
            cols = slice(part * B_W, (part + 1) * B_W)
            dy = jnp.concatenate([d_qkv[part][h] for h in range(B_HEADS)], axis=1)
            x, w = x_ref[...], cw_ref[:, cols]
            dpre = dy * _dsilu(_conv(x, jnp.where(first, p_ref[...], 0.0), w))
            dx, dws = _conv_bwd(x, w, dpre, nxt_ref[:, cols])
            nxt_ref[:, cols] = dpre[:SUBLANES]
            dp_ref[:, HP_QKVB + part * B_W:HP_QKVB + (part + 1) * B_W] = dx.astype(ACT_DTYPE)
            for j in range(CONV_K):
                dcw_ref[j:j + 1, cols] += dws[j]

    rev = lambda n: N - 1 - n
    part_x = lambda p: pl.BlockSpec((C, B_W), lambda n: (rev(n), cb0 + p))
    part_prev = lambda p: pl.BlockSpec((SUBLANES, B_W), lambda n: (jnp.maximum(rev(n) * (C // SUBLANES) - 1, 0), cb0 + p))
    whole_row = pl.BlockSpec((C, HYB_PAD), lambda n: (rev(n), 0))
    return pl.pallas_call(
        body, name=name, grid=(N,),
        in_specs=[sp["q"], sp["k"], sp["v"], sp["z"], sp["nw"], sp["row"], sp["row"], sp["state"], sp["inv"], sp["dob"],
                  whole_row, part_x(0), part_x(1), part_x(2), part_prev(0), part_prev(1), part_prev(2),
                  pl.BlockSpec((CONV_K, 3 * B_W), lambda n: (0, 0))],
        out_specs=[whole_row, sp["row"], sp["row"], pl.BlockSpec((B_HEADS, 1, LANES), lambda n: (0, 0, 0)),
                   pl.BlockSpec((CONV_K, 3 * B_W), lambda n: (0, 0))],
        out_shape=[jax.ShapeDtypeStruct(dproj.shape, dproj.dtype),
                   jax.ShapeDtypeStruct((B_HEADS, N, 1, C), F32), jax.ShapeDtypeStruct((B_HEADS, N, 1, C), F32),
                   jax.ShapeDtypeStruct((B_HEADS, 1, LANES), F32), jax.ShapeDtypeStruct((CONV_K, 3 * B_W), F32)],
        input_output_aliases={10: 0},
        scratch_shapes=[pltpu.VMEM((B_HEADS, B_HEAD_DIM, B_HEAD_DIM), F32), pltpu.VMEM((SUBLANES, 3 * B_W), F32)],
        compiler_params=_cparams("arbitrary"),
    )(qkvc, qkvc, qkvc, proj, norm_w, grow, brow, states, invs, dmix, dproj, proj, proj, proj, proj, proj, proj, conv_w)


def _lru_gates(xc, wa_ref, wx_ref, ba, bx, sp):
    pre_r, pre_i = [], []
    for hb in range(LRU_BLOCKS):
        xb = xc[:, hb * LRU_BLOCK_W:(hb + 1) * LRU_BLOCK_W]
        pre_r.append(_dot(xb, wa_ref[hb]))
        pre_i.append(_dot(xb, wx_ref[hb]))
    r = _sigmoid(jnp.concatenate(pre_r, axis=1) + ba)
    i = _sigmoid(jnp.concatenate(pre_i, axis=1) + bx)
    la = -LRU_C * r * sp
    a = jnp.exp(la)
    th = jnp.tanh(la)
    s = jnp.sqrt(-2.0 * th / (1.0 - th))
    return r, i, a, s


def _scan_down(a, b, h_in):
    R, W = a.shape
    a = a.reshape(R // SUBLANES, SUBLANES, W)
    b = b.reshape(R // SUBLANES, SUBLANES, W)
    sub = lax.broadcasted_iota(jnp.int32, a.shape, 1)
    d = 1
    while d < SUBLANES:
        ok = sub >= d
        b = a * jnp.where(ok, pltpu.roll(b, d, axis=1), 0.0) + b
        a = a * jnp.where(ok, pltpu.roll(a, d, axis=1), 1.0)
        d *= 2
    out, last = [], h_in
    for g in range(R // SUBLANES):
        h = b[g] + a[g] * last
        out.append(h)
        last = h[SUBLANES - 1:SUBLANES]
    return jnp.concatenate(out, axis=0)


def _scan_up(a, b, l_in):
    R, W = a.shape
    a = a.reshape(R // SUBLANES, SUBLANES, W)
    b = b.reshape(R // SUBLANES, SUBLANES, W)
    sub = lax.broadcasted_iota(jnp.int32, a.shape, 1)
    d = 1
    while d < SUBLANES:
        ok = sub < SUBLANES - d
        b = a * jnp.where(ok, pltpu.roll(b, SUBLANES - d, axis=1), 0.0) + b
        a = a * jnp.where(ok, pltpu.roll(a, SUBLANES - d, axis=1), 1.0)
        d *= 2
    out, nxt = [], l_in
    for g in range(R // SUBLANES - 1, -1, -1):
        lam = b[g] + a[g] * nxt
        out.append(lam)
        nxt = lam[0:1]
    return jnp.concatenate(out[::-1], axis=0)


def _rglru_fwd(proj, conv_w, conv_b, wa, wx, ba, bx, sp, name):
    T = proj.shape[0]
    R = _tile(T, 256)
    W = D_MODEL

    def body(p_ref, cw_ref, cb_ref, wa_ref, wx_ref, ba_ref, bx_ref, sp_ref, hg_ref, h_ref, prev_ref, hc_ref):
        i = pl.program_id(0)

        @pl.when(i == 0)
        def _():
            prev_ref[...] = jnp.zeros_like(prev_ref)
            hc_ref[...] = jnp.zeros_like(hc_ref)

        xr = p_ref[:, :W]
        gate = p_ref[:, W:]
        xc = _conv(xr, prev_ref[...], cw_ref[...]) + cb_ref[...]
        prev_ref[...] = xr[R - SUBLANES:]
        r, ig, a, s = _lru_gates(xc, wa_ref, wx_ref, ba_ref[...], bx_ref[...], sp_ref[...])
        h = _scan_down(a, s * ig * xc, hc_ref[SUBLANES - 1:SUBLANES, :])
        h_ref[...] = h
        hg_ref[...] = (h * _gelu(gate)).astype(ACT_DTYPE)
        hc_ref[...] = h[R - SUBLANES:]

    vec = pl.BlockSpec((1, W), lambda i: (0, 0))
    wsp = pl.BlockSpec((LRU_BLOCKS, LRU_BLOCK_W, LRU_BLOCK_W), lambda i: (0, 0, 0))
    row = pl.BlockSpec((R, W), lambda i: (i, 0))
    return pl.pallas_call(
        body, name=name, grid=(T // R,),
        in_specs=[pl.BlockSpec((R, 2 * W), lambda i: (i, 0)), pl.BlockSpec((CONV_K, W), lambda i: (0, 0)),
                  vec, wsp, wsp, vec, vec, vec],
        out_specs=[row, row],
        out_shape=[jax.ShapeDtypeStruct((T, W), ACT_DTYPE), jax.ShapeDtypeStruct((T, W), F32)],
        scratch_shapes=[pltpu.VMEM((SUBLANES, W), F32), pltpu.VMEM((SUBLANES, W), F32)],
        compiler_params=_cparams("arbitrary"),
    )(proj, conv_w, conv_b, wa, wx, ba, bx, sp)


def _rglru_bwd(proj, conv_w, conv_b, wa, wx, ba, bx, sp, h, dhg, name):
    T = proj.shape[0]
    R = _tile(T, 256)
    nb = T // R
    r8 = R // SUBLANES
    W = D_MODEL

    def body(p_ref, pp_ref, cw_ref, cb_ref, wa_ref, wx_ref, ba_ref, bx_ref, sp_ref, h_ref, hp_ref, dhg_ref,
             dp_ref, dcw_ref, dcb_ref, dwa_ref, dwx_ref, dba_ref, dbx_ref, dsp_ref, lam_ref, nxt_ref):
        step = pl.program_id(0)
        blk = nb - 1 - step

        @pl.when(step == 0)
        def _():
            lam_ref[...] = jnp.zeros_like(lam_ref)
            nxt_ref[...] = jnp.zeros_like(nxt_ref)

        xr = p_ref[:, :W]
        gate = p_ref[:, W:]
        first = blk > 0
        prev8 = jnp.where(first, pp_ref[:, :W], 0.0)
        hprev8 = jnp.where(first, hp_ref[...], 0.0)
        cw = cw_ref[...]
        spv = sp_ref[...]
        xc = _conv(xr, prev8, cw) + cb_ref[...]
        r, ig, a, s = _lru_gates(xc, wa_ref, wx_ref, ba_ref[...], bx_ref[...], spv)
        hv = h_ref[...]
        dhg_v = dhg_ref[...]
        dgate = dhg_v * hv * _dgelu(gate)
        dh = dhg_v * _gelu(gate)
        row = lax.broadcasted_iota(jnp.int32, (R, W), 0)
        last = row == R - 1
        a_up = jnp.where(last, 0.0, pltpu.roll(a, R - 1, axis=0))
        lam = _scan_up(a_up, dh + jnp.where(last, lam_ref[0:1, :], 0.0), jnp.zeros((1, W), F32))
        lam_ref[...] = (a * lam)[:SUBLANES]
        h_dn = _shift_down(hv, hprev8, 1)
        da = lam * h_dn
        bx_in = ig * xc
        dsv = lam * bx_in
        dig = lam * s * xc
        dxc = lam * s * ig
        dla = da * a - dsv * (a * a) / s
        dr = dla * (-LRU_C) * spv
        dsp = jnp.sum(dla * (-LRU_C) * r, axis=0, keepdims=True)
        dpr = dr * r * (1.0 - r)
        dpi = dig * ig * (1.0 - ig)
        dxc_parts, dwa_parts, dwx_parts = [], [], []
        for hb in range(LRU_BLOCKS):
            sl = slice(hb * LRU_BLOCK_W, (hb + 1) * LRU_BLOCK_W)
            xb, gr, gi = xc[:, sl], dpr[:, sl], dpi[:, sl]
            dxc_parts.append(_dot(gr, wa_ref[hb], NT) + _dot(gi, wx_ref[hb], NT))
            dwa_parts.append(_dot(xb, gr, TN))
            dwx_parts.append(_dot(xb, gi, TN))
        dxc = dxc + jnp.concatenate(dxc_parts, axis=1)
        dxr, dcw = _conv_bwd(xr, cw, dxc, nxt_ref[...])
        nxt_ref[...] = dxc[:SUBLANES]
        dp_ref[:, :W] = dxr.astype(ACT_DTYPE)
        dp_ref[:, W:] = dgate.astype(ACT_DTYPE)
        dcb = jnp.sum(dxc, axis=0, keepdims=True)
        dba = jnp.sum(dpr, axis=0, keepdims=True)
        dbx = jnp.sum(dpi, axis=0, keepdims=True)

        @pl.when(step == 0)
        def _():
            for j in range(CONV_K):
                dcw_ref[j:j + 1, :] = dcw[j]
            dcb_ref[...] = dcb
            dba_ref[...] = dba
            dbx_ref[...] = dbx
            dsp_ref[...] = dsp
            for hb in range(LRU_BLOCKS):
                dwa_ref[hb] = dwa_parts[hb]
                dwx_ref[hb] = dwx_parts[hb]

        @pl.when(step > 0)
        def _():
            for j in range(CONV_K):
                dcw_ref[j:j + 1, :] += dcw[j]
            dcb_ref[...] += dcb
            dba_ref[...] += dba
            dbx_ref[...] += dbx
            dsp_ref[...] += dsp
            for hb in range(LRU_BLOCKS):
                dwa_ref[hb] += dwa_parts[hb]
                dwx_ref[hb] += dwx_parts[hb]

    rv = lambda i: nb - 1 - i
    before = lambda i: jnp.maximum((nb - 1 - i) * r8 - 1, 0)
    vec = pl.BlockSpec((1, W), lambda i: (0, 0))
    cws = pl.BlockSpec((CONV_K, W), lambda i: (0, 0))
    wsp = pl.BlockSpec((LRU_BLOCKS, LRU_BLOCK_W, LRU_BLOCK_W), lambda i: (0, 0, 0))
    row = pl.BlockSpec((R, W), lambda i: (rv(i), 0))
    wshape = jax.ShapeDtypeStruct((LRU_BLOCKS, LRU_BLOCK_W, LRU_BLOCK_W), F32)
    vshape = jax.ShapeDtypeStruct((1, W), F32)
    return pl.pallas_call(
        body, name=name, grid=(nb,),
        in_specs=[pl.BlockSpec((R, 2 * W), lambda i: (rv(i), 0)), pl.BlockSpec((SUBLANES, 2 * W), lambda i: (before(i), 0)),
                  cws, vec, wsp, wsp, vec, vec, vec, row, pl.BlockSpec((SUBLANES, W), lambda i: (before(i), 0)), row],
        out_specs=[pl.BlockSpec((R, 2 * W), lambda i: (rv(i), 0)), cws, vec, wsp, wsp, vec, vec, vec],
        out_shape=[jax.ShapeDtypeStruct((T, 2 * W), ACT_DTYPE), jax.ShapeDtypeStruct((CONV_K, W), F32), vshape,
                   wshape, wshape, vshape, vshape, vshape],
        scratch_shapes=[pltpu.VMEM((SUBLANES, W), F32), pltpu.VMEM((SUBLANES, W), F32)],
        compiler_params=_cparams("arbitrary"),
    )(proj, proj, conv_w, conv_b, wa, wx, ba, bx, sp, h, h, dhg)


MESH = pl.DeviceIdType.MESH
ANY = pl.BlockSpec(memory_space=pl.ANY)


def _position():
    x, y, c = lax.axis_index("x"), lax.axis_index("y"), lax.axis_index("c")
    other_chips = [(1 - x, y), (x, 1 - y), (1 - x, 1 - y)]
    return x, y, c, other_chips


def _all_gather_weights(shards, name):
    n = len(shards)

    def body(*refs):
        ins, outs = refs[:n], refs[n:2 * n]
        send_sems, recv_sems = refs[2 * n:]
        x, y, c, chips = _position()
        me = 2 * x + y
        sibling = (x, y, 1 - c)

        def rcopy(t, k, src, dst, to):
            return pltpu.make_async_remote_copy(src_ref=src, dst_ref=dst, send_sem=send_sems.at[t, k],
                                                recv_sem=recv_sems.at[t, k], device_id=to, device_id_type=MESH)

        started = []
        for t in range(n):
            for j, (cx, cy) in enumerate(chips):
                cp = rcopy(t, j, ins[t].at[c], outs[t].at[me, c], (cx, cy, c))
                cp.start()
                started.append(cp)
        for t in range(n):
            for j, (cx, cy) in enumerate(chips):
                blk = outs[t].at[2 * cx + cy, c]
                rcopy(t, j, blk, blk, (cx, cy, c)).wait_recv()
                cp = rcopy(t, 3 + j, blk, blk, sibling)
                cp.start()
                started.append(cp)
        for t in range(n):
            for j, (cx, cy) in enumerate(chips):
                blk = outs[t].at[2 * cx + cy, 1 - c]
                rcopy(t, 3 + j, blk, blk, sibling).wait_recv()
        for cp in started:
            cp.wait_send()

    return pl.pallas_call(
        body, name=name, in_specs=[ANY] * n, out_specs=[ANY] * n,
        out_shape=[jax.ShapeDtypeStruct((N_CHIPS,) + s.shape, s.dtype) for s in shards],
        scratch_shapes=[pltpu.SemaphoreType.DMA((n, 6)), pltpu.SemaphoreType.DMA((n, 6))],
    )(*shards)


HBM = pl.BlockSpec(memory_space=pltpu.HBM)
SEM = pl.BlockSpec(memory_space=pltpu.SEMAPHORE)
EFFECT = pltpu.SideEffectType.DATAFLOW_SIDE_EFFECTING


def _gather_start(own, land, after, name):
    n = len(own)

    def body(*refs):
        own_refs, land_refs = refs[:n], refs[n:2 * n]
        send_sems = refs[2 * n + 1:3 * n + 1]
        recv_sems = refs[3 * n + 1:4 * n + 1]
        x, y, c, chips = _position()
        me = 2 * x + y
        for t in range(n):
            for cx, cy in chips:
                pltpu.make_async_remote_copy(
                    src_ref=own_refs[t], dst_ref=land_refs[t].at[me], send_sem=send_sems[t], recv_sem=recv_sems[t],
                    device_id=(cx, cy, c), device_id_type=MESH).start()

    sems = (pltpu.SemaphoreType.DMA(()),) * (2 * n)
    thru = [pltpu.HBM(a.shape, a.dtype) for a in list(own) + list(land)]
    out = pl.pallas_call(
        body, name=name, out_shape=(*sems, *thru),
        in_specs=[HBM] * (2 * n) + [pl.BlockSpec(memory_space=pl.ANY)], out_specs=(SEM,) * (2 * n) + (HBM,) * (2 * n),
        input_output_aliases={i: 2 * n + i for i in range(2 * n)},
        compiler_params=pltpu.CompilerParams(has_side_effects=EFFECT),
    )(*[pltpu.with_memory_space_constraint(a, pltpu.HBM) for a in list(own) + list(land)], after)
    return list(out[:n]), list(out[n:2 * n]), list(out[2 * n:3 * n]), list(out[3 * n:])


def _gather_wait(send_sems, recv_sems, own, land, after, name):
    n = len(own)

    def body(*refs):
        land_refs = refs[n:2 * n]
        s_sems, r_sems = refs[2 * n:3 * n], refs[3 * n:4 * n]
        x, y, c, _ = _position()
        for t in range(n):
            three = land_refs[t].at[pl.ds(0, N_CHIPS - 1)]
            cp = pltpu.make_async_remote_copy(src_ref=three, dst_ref=three, send_sem=s_sems[t], recv_sem=r_sems[t],
                                              device_id=(x, y, c), device_id_type=MESH)
            cp.wait_send()
            cp.wait_recv()

    thru = [pltpu.HBM(a.shape, a.dtype) for a in list(own) + list(land)]
    out = pl.pallas_call(
        body, name=name, out_shape=tuple(thru),
        in_specs=[HBM] * (2 * n) + [SEM] * (2 * n) + [pl.BlockSpec(memory_space=pl.ANY)], out_specs=(HBM,) * (2 * n),
        input_output_aliases={i: i for i in range(2 * n)},
        compiler_params=pltpu.CompilerParams(has_side_effects=EFFECT),
    )(*own, *land, *send_sems, *recv_sems, after)
    return list(out[n:])


def _rs_to_sibling(grads, name):
    n = len(grads)

    def body(*refs):
        ins, outs = refs[:n], refs[n:2 * n]
        send_sems, recv_sems = refs[2 * n:]
        x, y, c, _ = _position()
        cps = [pltpu.make_async_remote_copy(src_ref=ins[t].at[:, 1 - c], dst_ref=outs[t], send_sem=send_sems.at[t],
                                            recv_sem=recv_sems.at[t], device_id=(x, y, 1 - c), device_id_type=MESH)
               for t in range(n)]
        for cp in cps:
            cp.start()
        for cp in cps:
            cp.wait()

    return pl.pallas_call(
        body, name=name, in_specs=[ANY] * n, out_specs=[ANY] * n,
        out_shape=[jax.ShapeDtypeStruct((N_CHIPS,) + g.shape[2:], g.dtype) for g in grads],
        scratch_shapes=[pltpu.SemaphoreType.DMA((n,)), pltpu.SemaphoreType.DMA((n,))],
    )(*grads)


def _rs_sibling_start(grads, recv, name):
    n = len(grads)

    def body(*refs):
        grad_refs, recv_refs = refs[:n], refs[n:2 * n]
        send_sems, recv_sems = refs[2 * n:3 * n], refs[3 * n:4 * n]
        token_ref = refs[-1]
        x, y, c, _ = _position()
        for t in range(n):
            pltpu.make_async_remote_copy(src_ref=grad_refs[t].at[:, 1 - c], dst_ref=recv_refs[t], send_sem=send_sems[t],
                                         recv_sem=recv_sems[t], device_id=(x, y, 1 - c), device_id_type=MESH).start()
        token_ref[...] = jnp.zeros_like(token_ref)

    sems = (pltpu.SemaphoreType.DMA(()),) * (2 * n)
    thru = [pltpu.HBM(a.shape, a.dtype) for a in list(grads) + list(recv)]
    out = pl.pallas_call(
        body, name=name, out_shape=(*sems, *thru, jax.ShapeDtypeStruct((SUBLANES, LANES), F32)),
        in_specs=[HBM] * (2 * n), out_specs=(SEM,) * (2 * n) + (HBM,) * (2 * n) + (pl.BlockSpec(memory_space=pltpu.VMEM),),
        input_output_aliases={i: 2 * n + i for i in range(2 * n)},
        compiler_params=pltpu.CompilerParams(has_side_effects=EFFECT),
    )(*[pltpu.with_memory_space_constraint(a, pltpu.HBM) for a in list(grads) + list(recv)])
    return out[:n], out[n:2 * n], list(out[2 * n:3 * n]), list(out[3 * n:4 * n]), out[-1]


def _rs_sibling_wait(send_sems, recv_sems, grads, recv, after, name):
    n = len(grads)

    def body(*refs):
        recv_refs = refs[n:2 * n]
        s_sems, r_sems = refs[2 * n:3 * n], refs[3 * n:4 * n]
        x, y, c, _ = _position()
        for t in range(n):
            cp = pltpu.make_async_remote_copy(src_ref=recv_refs[t], dst_ref=recv_refs[t], send_sem=s_sems[t],
                                              recv_sem=r_sems[t], device_id=(x, y, c), device_id_type=MESH)
            cp.wait_send()
            cp.wait_recv()

    thru = [pltpu.HBM(a.shape, a.dtype) for a in list(grads) + list(recv)]
    out = pl.pallas_call(
        body, name=name, out_shape=tuple(thru),
        in_specs=[HBM] * (2 * n) + [SEM] * (2 * n) + [pl.BlockSpec(memory_space=pl.ANY)], out_specs=(HBM,) * (2 * n),
        input_output_aliases={i: i for i in range(2 * n)},
        compiler_params=pltpu.CompilerParams(has_side_effects=EFFECT),
    )(*grads, *recv, *send_sems, *recv_sems, after)
    return list(out[:n]), list(out[n:])


def _rs_across_start(parts, recv, name):
    n = len(parts)

    def body(*refs):
        part_refs, recv_refs = refs[:n], refs[n:2 * n]
        send_sems, recv_sems = refs[2 * n:3 * n], refs[3 * n:4 * n]
        token_ref = refs[-1]
        x, y, c, chips = _position()
        me = 2 * x + y
        for t in range(n):
            for cx, cy in chips:
                pltpu.make_async_remote_copy(src_ref=part_refs[t].at[2 * cx + cy], dst_ref=recv_refs[t].at[me],
                                             send_sem=send_sems[t], recv_sem=recv_sems[t], device_id=(cx, cy, c),
                                             device_id_type=MESH).start()
        token_ref[...] = jnp.zeros_like(token_ref)

    sems = (pltpu.SemaphoreType.DMA(()),) * (2 * n)
    thru = [pltpu.HBM(a.shape, a.dtype) for a in list(parts) + list(recv)]
    out = pl.pallas_call(
        body, name=name, out_shape=(*sems, *thru, jax.ShapeDtypeStruct((SUBLANES, LANES), F32)),
        in_specs=[HBM] * (2 * n), out_specs=(SEM,) * (2 * n) + (HBM,) * (2 * n) + (pl.BlockSpec(memory_space=pltpu.VMEM),),
        input_output_aliases={i: 2 * n + i for i in range(2 * n)},
        compiler_params=pltpu.CompilerParams(has_side_effects=EFFECT),
    )(*[pltpu.with_memory_space_constraint(a, pltpu.HBM) for a in list(parts) + list(recv)])
    return out[:n], out[n:2 * n], list(out[2 * n:3 * n]), list(out[3 * n:4 * n]), out[-1]


def _rs_across_wait(send_sems, recv_sems, parts, recv, after, name):
    n = len(parts)

    def body(*refs):
        recv_refs = refs[n:2 * n]
        s_sems, r_sems = refs[2 * n:3 * n], refs[3 * n:4 * n]
        x, y, c, _ = _position()
        for t in range(n):
            three = recv_refs[t].at[pl.ds(0, N_CHIPS - 1)]
            cp = pltpu.make_async_remote_copy(src_ref=three, dst_ref=three, send_sem=s_sems[t], recv_sem=r_sems[t],
                                              device_id=(x, y, c), device_id_type=MESH)
            cp.wait_send()
            cp.wait_recv()

    thru = [pltpu.HBM(a.shape, a.dtype) for a in list(parts) + list(recv)]
    out = pl.pallas_call(
        body, name=name, out_shape=tuple(thru),
        in_specs=[HBM] * (2 * n) + [SEM] * (2 * n) + [pl.BlockSpec(memory_space=pl.ANY)], out_specs=(HBM,) * (2 * n),
        input_output_aliases={i: i for i in range(2 * n)},
        compiler_params=pltpu.CompilerParams(has_side_effects=EFFECT),
    )(*parts, *recv, *send_sems, *recv_sems, after)
    return list(out[:n]), list(out[n:])


def _rs_join_halves(halves, name):
    n = len(halves)

    def body(*refs):
        ins, outs = refs[:n], refs[n:2 * n]
        send_sems, recv_sems = refs[2 * n:]
        x, y, c, _ = _position()
        cps = [pltpu.make_async_remote_copy(src_ref=ins[t].at[c], dst_ref=outs[t].at[c], send_sem=send_sems.at[t],
                                            recv_sem=recv_sems.at[t], device_id=(x, y, 1 - c), device_id_type=MESH)
               for t in range(n)]
        for cp in cps:
            cp.start()
        for t in range(n):
            blk = outs[t].at[1 - c]
            pltpu.make_async_remote_copy(src_ref=blk, dst_ref=blk, send_sem=send_sems.at[t], recv_sem=recv_sems.at[t],
                                         device_id=(x, y, 1 - c), device_id_type=MESH).wait_recv()
        for cp in cps:
            cp.wait_send()

    return pl.pallas_call(
        body, name=name, in_specs=[ANY] * n, out_specs=[ANY] * n,
        out_shape=[jax.ShapeDtypeStruct(h.shape, h.dtype) for h in halves],
        input_output_aliases={t: t for t in range(n)},
        scratch_shapes=[pltpu.SemaphoreType.DMA((n,)), pltpu.SemaphoreType.DMA((n,))],
    )(*halves)


def _all_gather_small(block, name):
    m_per, n = block.shape

    def body(x_ref, out_ref, send_sems, recv_sems, local_sem):
        x, y, c, chips = _position()
        me, sibling = (x, y, c), (x, y, 1 - c)

        def rows(px, py, pc):
            return out_ref.at[pl.ds((4 * px + 2 * py + pc) * m_per, m_per), :]

        def copy(k, blk, to, src=None):
            return pltpu.make_async_remote_copy(
                src_ref=rows(*blk) if src is None else src, dst_ref=rows(*blk), send_sem=send_sems.at[k],
                recv_sem=recv_sems.at[k], device_id=to, device_id_type=MESH)

        mine = pltpu.make_async_copy(x_ref, rows(*me), local_sem)
        mine.start()
        first = [copy(0, me, sibling, src=x_ref)]
        first += [copy(1 + j, me, (*chip, c), src=x_ref) for j, chip in enumerate(chips)]
        for cp in first:
            cp.start()
        passed = [copy(4 + j, (*chip, c), sibling) for j, chip in enumerate(chips)]
        for j, chip in enumerate(chips):
            copy(1 + j, (*chip, c), me).wait_recv()
            passed[j].start()
        copy(0, sibling, me).wait_recv()
        for j, chip in enumerate(chips):
            copy(4 + j, (*chip, 1 - c), me).wait_recv()
        for cp in first + passed:
            cp.wait_send()
        mine.wait()

    return pl.pallas_call(
        body, name=name, out_shape=jax.ShapeDtypeStruct((N_DEV * m_per, n), block.dtype),
        in_specs=[pl.BlockSpec(memory_space=pltpu.VMEM)], out_specs=pl.BlockSpec(memory_space=pltpu.VMEM),
        scratch_shapes=[pltpu.SemaphoreType.DMA((7,)), pltpu.SemaphoreType.DMA((7,)), pltpu.SemaphoreType.DMA],
    )(block)


def _row_tile(R, n):
    budget = 1 << 19
    if R * n <= budget or R % SUBLANES:
        return R
    t = R
    while t * n > budget and t % (2 * SUBLANES) == 0:
        t //= 2
    return t


def _pair_sum(g, recv, c_arr, name):
    _, _, R, n = g.shape
    tr = _row_tile(R, n)

    def body(c_ref, g_ref, r_ref, o_ref):
        o_ref[...] = (g_ref[...] + r_ref[...]).astype(ICI_DTYPE)

    grid_spec = pltpu.PrefetchScalarGridSpec(
        num_scalar_prefetch=1, grid=(N_CHIPS, R // tr),
        in_specs=[pl.BlockSpec((None, None, tr, n), lambda p, i, c: (p, c[0], i, 0)),
                  pl.BlockSpec((None, tr, n), lambda p, i, c: (p, i, 0))],
        out_specs=pl.BlockSpec((None, tr, n), lambda p, i, c: (p, i, 0)))
    return pl.pallas_call(
        body, name=name, grid_spec=grid_spec, out_shape=jax.ShapeDtypeStruct(recv.shape, ICI_DTYPE),
        compiler_params=_cparams("parallel", "parallel"),
    )(c_arr, g, recv)


def _chip_sum(recv, own, where, name):
    _, R, n = recv.shape
    tr = _row_tile(R, n)

    def body(w_ref, r0, r1, r2, r3, own_ref, o_ref):
        me = w_ref[0]
        terms = [jnp.where(me == k, own_ref[...], r[...]).astype(F32) for k, r in enumerate((r0, r1, r2, r3))]
        o_ref[...] = ((terms[0] + terms[1]) + terms[2]) + terms[3]

    def slot(k):
        return pl.BlockSpec((None, tr, n), lambda i, w: (w[1 + k], i, 0))

    grid_spec = pltpu.PrefetchScalarGridSpec(
        num_scalar_prefetch=1, grid=(R // tr,),
        in_specs=[slot(0), slot(1), slot(2), slot(3), pl.BlockSpec((None, tr, n), lambda i, w: (w[0], i, 0))],
        out_specs=pl.BlockSpec((None, tr, n), lambda i, w: (w[5], i, 0)))
    return pl.pallas_call(
        body, name=name, grid_spec=grid_spec, out_shape=jax.ShapeDtypeStruct((2, R, n), F32),
        compiler_params=_cparams("parallel"),
    )(where, recv, recv, recv, recv, own)


ADAM_C1 = 1.0 / (1.0 - ADAM_B1 ** ADAM_STEP)
ADAM_C2 = 1.0 / (1.0 - ADAM_B2 ** ADAM_STEP)


def _adamw_math(w, g, m, v):
    m = ADAM_B1 * m + (1.0 - ADAM_B1) * g
    v = ADAM_B2 * v + (1.0 - ADAM_B2) * (g * g)
    delta = -ADAM_LR * ((m * ADAM_C1) / (jnp.sqrt(v * ADAM_C2) + ADAM_EPS) + ADAM_WD * w)
    return delta, m, v


def _adamw(w, g, m, v, name):
    R, n = w.shape
    tr = _row_tile(R, n)

    def body(w_ref, g_ref, m_ref, v_ref, d_ref, nm_ref, nv_ref):
        d_ref[...], nm_ref[...], nv_ref[...] = _adamw_math(w_ref[...], g_ref[...], m_ref[...], v_ref[...])

    spec = pl.BlockSpec((tr, n), lambda i: (i, 0))
    return pl.pallas_call(
        body, name=name, grid=(R // tr,), in_specs=[spec] * 4, out_specs=[spec] * 3,
        out_shape=[jax.ShapeDtypeStruct((R, n), F32)] * 3, compiler_params=_cparams("parallel"),
    )(w, g, m, v)


def _adamw_rows(w, g, m, v, row0, prev, name):
    R, n = w.shape
    rows = g.shape[0]
    tr = _row_tile(math.gcd(row0, rows), n)
    off = row0 // tr

    def body(*refs):
        w_ref, g_ref, m_ref, v_ref = refs[:4]
        go_ref, d_ref, nm_ref, nv_ref = refs[-4:]
        gv = g_ref[...]
        go_ref[...] = gv
        d_ref[...], nm_ref[...], nv_ref[...] = _adamw_math(w_ref[...], gv, m_ref[...], v_ref[...])

    at = pl.BlockSpec((tr, n), lambda i: (i + off, 0))
    carried = [] if prev is None else list(prev)
    return pl.pallas_call(
        body, name=name, grid=(rows // tr,),
        in_specs=[at, pl.BlockSpec((tr, n), lambda i: (i, 0)), at, at] + [ANY] * len(carried),
        out_specs=[at] * 4, out_shape=[jax.ShapeDtypeStruct((R, n), F32)] * 4,
        input_output_aliases={4 + i: i for i in range(len(carried))},
        compiler_params=_cparams("parallel"),
    )(w, g, m, v, *carried)


def _adamw_small(w, gall, m, v, name):
    M, n = w.shape

    def body(w_ref, g_ref, m_ref, v_ref, gs_ref, d_ref, nm_ref, nv_ref):
        g = g_ref[0:M, :]
        for d in range(1, N_DEV):
            g = g + g_ref[d * M:(d + 1) * M, :]
        gs_ref[...] = g
        d_ref[...], nm_ref[...], nv_ref[...] = _adamw_math(w_ref[...], g, m_ref[...], v_ref[...])

    return pl.pallas_call(
        body, name=name, out_shape=[jax.ShapeDtypeStruct((M, n), F32)] * 4,
    )(w, gall, m, v)


SMALL_ROWS = 24
MID_ROWS = 4


def _pack_small(ln1_g, ln1_b, ln2_g, ln2_b, norm_w, sinks, a_log, dt_bias, scalar=None):
    mixed = jnp.concatenate([norm_w.reshape(-1), sinks.reshape(-1), a_log.reshape(-1), dt_bias.reshape(-1)])
    mixed = jnp.pad(mixed, (0, D_MODEL - mixed.shape[0]))[None]
    extra = jnp.zeros((1, D_MODEL), F32) if scalar is None else jnp.pad(scalar.reshape(1, 1), ((0, 0), (0, D_MODEL - 1)))
    pad = jnp.zeros((SMALL_ROWS - 4 * DEPTH - 2, D_MODEL), F32)
    return jnp.concatenate([ln1_g, ln1_b, ln2_g, ln2_b, mixed, extra, pad], axis=0)


def _unpack_small(p):
    mixed = p[4 * DEPTH]
    return (p[0:4], p[4:8], p[8:12], p[12:16], mixed[0:256].reshape(2, 128), mixed[256:272].reshape(2, 8),
            mixed[272:280].reshape(2, 4), mixed[280:288].reshape(2, 4))


def _pack_mid(conv_w, rconv_w, rconv_b, b_a, b_x, lam):
    lead = conv_w.shape[0]
    flat = jnp.concatenate([conv_w.reshape(lead, -1), rconv_w.reshape(lead, -1), rconv_b, b_a, b_x, lam], axis=1)
    return jnp.pad(flat, ((0, 0), (0, MID_ROWS * D_MODEL - flat.shape[1]))).reshape(lead, MID_ROWS, D_MODEL)


def _unpack_mid(p):
    lead = p.shape[:-2]
    f = p.reshape(lead + (MID_ROWS * D_MODEL,))
    return (f[..., 0:1536].reshape(lead + (4, 384)), f[..., 1536:2560].reshape(lead + (4, 256)),
            f[..., 2560:2816], f[..., 2816:3072], f[..., 3072:3328], f[..., 3328:3584])


def _cols_from_chips(g):
    p, L, R, n = g.shape
    return g.transpose(1, 2, 0, 3).reshape(L, R, p * n)


def _rows_from_chips(g):
    p, L, R, n = g.shape
    return g.transpose(1, 0, 2, 3).reshape(L, p * R, n)


def _cols_to_chips(g):
    L, R, n4 = g.shape
    return g.reshape(L, R, N_CHIPS, n4 // N_CHIPS).transpose(2, 0, 1, 3)


def _rows_to_chips(g):
    L, R4, n = g.shape
    return g.reshape(L, N_CHIPS, R4 // N_CHIPS, n).transpose(1, 0, 2, 3)


def _halves(a):
    return a.reshape(2, -1, a.shape[-1])


def _pad_hyb_cols(w):
    z = jnp.zeros(w.shape[:-1] + (HYB_PAD - HP_BG - 2 * B_HEADS,), w.dtype)
    return jnp.concatenate([w[..., 0:512], w[..., 768:2304], w[..., 2304:2816], w[..., 512:768], w[..., 2816:2824], z], axis=-1)


def _unpad_hyb_cols(w):
    return jnp.concatenate([w[..., 0:512], w[..., 2560:2816], w[..., 512:2048], w[..., 2048:2560], w[..., 2816:2824]], axis=-1)


def _hybrid_fwd(x, W, j, tables, sfx, out_ready, ln):
    cos, sin_s = tables
    T = x.shape[0]
    N = T // B_CHUNK
    proj = _matmul(x, W["hyb_w_in"][j], "nn", "hyb_in" + sfx)
    sinks_b = jnp.broadcast_to(W["hyb_sinks"][j][:, None], (A_Q_HEADS, LANES))
    o_a, lse = _attn_fwd(proj, cos, sin_s, sinks_b, "attn_fwd" + sfx)
    bg = proj[:, HP_BG:HP_BG + 2 * B_HEADS]
    beta = jax.nn.sigmoid(bg[:, :B_HEADS])
    pre = bg[:, B_HEADS:] + W["hyb_dt_bias"][j][None]
    g = -jnp.exp(W["hyb_a_log"][j])[None] * jax.nn.softplus(pre)
    grow = g.T.reshape(B_HEADS, N, 1, B_CHUNK)
    brow = beta.T.reshape(B_HEADS, N, 1, B_CHUNK)
    nw = W["hyb_norm_w"][j][None]
    o_b, states, invs, qkvc = _dn_fwd(proj, W["hyb_conv_w"][j], nw, grow, brow, "dn_fwd" + sfx)
    mix = jnp.concatenate([o_a, o_b], axis=1).astype(ACT_DTYPE)
    out_ready(mix)
    out = _matmul(mix, W["hyb_w_out"][j], "nn", "hyb_out" + sfx, epi=_epi_residual_ln, extra=ln, out_dtype=_ln_out(),
                  tm=512)
    res = dict(proj=proj, o_a=o_a, lse=lse, qkvc=qkvc, beta=beta, pre=pre, g=g, grow=grow, brow=brow,
               states=states, invs=invs, mix=mix, sinks_b=sinks_b, nw=nw)
    return out, res


def _hybrid_bwd(x, du, W, j, res, tables, sfx):
    cos, sin_s = tables
    T = x.shape[0]
    proj = res["proj"]
    d_wout = _matmul(res["mix"], du, "tn", "hyb_out_dw" + sfx)
    dmix = _matmul(du, W["hyb_w_out"][j], "nt", "hyb_out_dx" + sfx)
    dproj, dkc, dkp, dvc, dvp, dsink = _attn_bwd(proj, cos, sin_s, res["sinks_b"], res["o_a"], res["lse"], dmix,
                                                  "attn_bwd" + sfx)
    zpad = jnp.zeros((WINDOW, LANES), F32)
    dk = dkc + jnp.concatenate([dkp[WINDOW:], zpad], axis=0)
    dv = dvc + jnp.concatenate([dvp[WINDOW:], zpad], axis=0)
    dproj, dg4, dbeta4, dnw, dconv = _dn_bwd(res["qkvc"], proj, W["hyb_conv_w"][j], res["nw"], res["grow"], res["brow"],
                                             res["states"], res["invs"], dmix, dproj, "dn_bwd" + sfx)
    dg = dg4.reshape(B_HEADS, T).T
    dbeta = dbeta4.reshape(B_HEADS, T).T
    beta = res["beta"]
    dbeta_logit = dbeta * beta * (1.0 - beta)
    da_logit = dg * (-jnp.exp(W["hyb_a_log"][j]))[None] * jax.nn.sigmoid(res["pre"])
    d_dt_bias = jnp.sum(da_logit, axis=0)
    d_a_log = jnp.sum(dg * res["g"], axis=0)
    zcols = jnp.zeros((T, HYB_PAD - HP_BG - 2 * B_HEADS), F32)
    tail = jnp.concatenate([dk, dv, dbeta_logit, da_logit, zcols], axis=1).astype(ACT_DTYPE)
    dproj = lax.dynamic_update_slice(dproj, tail, (0, HP_K))
    d_win = _matmul(x, dproj, "tn", "hyb_in_dw" + sfx)
    dx = _matmul(dproj, W["hyb_w_in"][j], "nt", "hyb_in_dx" + sfx, epi=_epi_add_residual, extra=du)
    grads = dict(hyb_w_in=d_win, hyb_w_out=d_wout, hyb_sinks=dsink[0], hyb_conv_w=dconv, hyb_a_log=d_a_log,
                 hyb_dt_bias=d_dt_bias, hyb_norm_w=jnp.sum(dnw[:, 0, :], axis=0))
    return dx, grads


def _rec_fwd(x, W, j, sfx, ln):
    proj = _matmul(x, W["rec_w_in"][j], "nn", "rec_in" + sfx)
    sp = jax.nn.softplus(-W["rec_lambda"][j])[None]
    hg, h = _rglru_fwd(proj, W["rec_conv_w"][j], W["rec_conv_b"][j][None], W["rec_w_a"][j], W["rec_w_x"][j],
                       W["rec_b_a"][j][None], W["rec_b_x"][j][None], sp, "rglru_fwd" + sfx)
    out = _matmul(hg, W["rec_w_out"][j], "nn", "rec_out" + sfx, epi=_epi_residual_ln, extra=ln, out_dtype=_ln_out(),
                  tm=512)
    return out, dict(proj=proj, hg=hg, h=h, sp=sp)


def _rec_bwd(x, du, W, j, res, sfx):
    d_wout = _matmul(res["hg"], du, "tn", "rec_out_dw" + sfx)
    dhg = _matmul(du, W["rec_w_out"][j], "nt", "rec_out_dx" + sfx)
    dproj, dcw, dcb, dwa, dwx, dba, dbx, dsp = _rglru_bwd(
        res["proj"], W["rec_conv_w"][j], W["rec_conv_b"][j][None], W["rec_w_a"][j], W["rec_w_x"][j],
        W["rec_b_a"][j][None], W["rec_b_x"][j][None], res["sp"], res["h"], dhg, "rglru_bwd" + sfx)
    d_lam = dsp[0] * (-jax.nn.sigmoid(-W["rec_lambda"][j]))
    d_win = _matmul(x, dproj, "tn", "rec_in_dw" + sfx)
    dx = _matmul(dproj, W["rec_w_in"][j], "nt", "rec_in_dx" + sfx, epi=_epi_add_residual, extra=du)
    grads = dict(rec_w_in=d_win, rec_w_out=d_wout, rec_conv_w=dcw, rec_conv_b=dcb[0], rec_w_a=dwa, rec_w_x=dwx,
                 rec_b_a=dba[0], rec_b_x=dbx[0], rec_lambda=d_lam)
    return dx, grads


def _local_step(x, tgt, W, mlp_w, mixer_ready, on_group):
    T = x.shape[0]
    tables = _rope_tables(T)
    acts = []
    xb = x.astype(ACT_DTYPE)
    for layer in range(DEPTH):
        j, sfx = layer // 2, ""
        mixer_ready(layer, xb)
        ln1 = (xb if layer else x, W["ln1_g"][layer][None], W["ln1_b"][layer][None])
        if layer % 2 == 0:
            (x1b, u1), res = _hybrid_fwd(xb, W, j, tables, sfx,
                                         functools.partial(mixer_ready, layer, out_projection=True), ln1)
        else:
            (x1b, u1), res = _rec_fwd(xb, W, j, sfx, ln1)
        w1, w2, wl = mlp_w(layer, x1b)
        h1 = _matmul(x1b, w1, "nn", "mlp_up", out_dtype=ACT_DTYPE, b_chips=("j", wl))
        x2b, u2 = _matmul(h1, w2, "nn", "mlp_down", a_fn=_relu2, b_chips=("k", wl), epi=_epi_residual_ln,
                          extra=(x1b, W["ln2_g"][layer][None], W["ln2_b"][layer][None]), out_dtype=_ln_out())
        acts.append(dict(xb=xb, res=res, u1=u1, x1b=x1b, h1=h1, u2=u2))
        xb = x2b
    per_layer = [None] * DEPTH
    d_w1 = dict(lower=lax.empty((N_CHIPS, 1, D_MODEL, D_FF // N_CHIPS), F32),
                upper=lax.empty((N_CHIPS, DEPTH - 1, D_MODEL, D_FF // N_CHIPS), F32))
    d_w2 = dict(lower=lax.empty((N_CHIPS, 1, D_FF // N_CHIPS, D_MODEL), F32),
                upper=lax.empty((N_CHIPS, DEPTH - 1, D_FF // N_CHIPS, D_MODEL), F32))
    token = None
    for layer in reversed(range(DEPTH)):
        j, a = layer // 2, acts[layer]
        grp, li = ("upper", layer - 1) if layer else ("lower", 0)
        ln2_g = W["ln2_g"][layer][None]
        if token is not None:
            ln2_g = ln2_g + token
        if layer == DEPTH - 1:
            du2, dg2, db2, loss = _ln_bwd(a["u2"], ln2_g, None, "ln_bwd_loss", loss_of=(W["ln2_b"][layer][None], tgt))
        else:
            du2, dg2, db2 = _ln_bwd(a["u2"], ln2_g, dx, "ln_bwd")
        w1, w2, wl = mlp_w(layer, du2)
        d_w2[grp] = _matmul(a["h1"], du2, "tn", "mlp_down_dw", a_fn=_relu2, out_chips=("i", li, d_w2[grp]))
        dh1 = _matmul(du2, w2, "nt", "mlp_down_dx", epi=_epi_drelu2, extra=a["h1"], out_dtype=ACT_DTYPE,
                      b_chips=("j", wl))
        d_w1[grp] = _matmul(a["x1b"], dh1, "tn", "mlp_up_dw", out_chips=("j", li, d_w1[grp]))
        dx1 = _matmul(dh1, w1, "nt", "mlp_up_dx", epi=_epi_add_residual, extra=du2, b_chips=("k", wl))
        ln1_g = W["ln1_g"][layer][None]
        token = on_group("middle", dx1, None, None, None) if layer == 0 else None
        if token is not None:
            ln1_g = ln1_g + token
        du1, dg1, db1 = _ln_bwd(a["u1"], ln1_g, dx1, "ln_bwd")
        if layer % 2 == 0:
            dx, g = _hybrid_bwd(a["xb"], du1, W, j, a["res"], tables, "")
        else:
            dx, g = _rec_bwd(a["xb"], du1, W, j, a["res"], "")
        g.update(ln1_g=dg1[0], ln1_b=db1[0], ln2_g=dg2[0], ln2_b=db2[0])
        per_layer[layer] = g
        if layer == 1:
            token = on_group("upper", [per_layer[2]], [per_layer[1], per_layer[3]], d_w1["upper"], d_w2["upper"])
        elif layer == 0:
            on_group("lower", [per_layer[0]], [], d_w1["lower"], d_w2["lower"])
    grads = {}
    for name in ("ln1_g", "ln1_b", "ln2_g", "ln2_b"):
        grads[name] = jnp.stack([per_layer[l][name] for l in range(DEPTH)])
    for name in ("hyb_norm_w", "hyb_sinks", "hyb_a_log", "hyb_dt_bias"):
        grads[name] = jnp.stack([per_layer[l][name] for l in (0, 2)])
    return loss, dx, grads


BIG = ("hyb_w_in", "hyb_w_out", "rec_w_in", "rec_w_out", "mlp_w1", "mlp_w2", "rec_w_a", "rec_w_x")
COL_SHARDED = ("hyb_w_in", "rec_w_in", "mlp_w1")
CHIP_MAJOR = ("mlp_w1", "mlp_w2")
MID = ("hyb_conv_w", "rec_conv_w", "rec_conv_b", "rec_b_a", "rec_b_x", "rec_lambda")
SMALL = ("ln1_g", "ln1_b", "ln2_g", "ln2_b", "hyb_norm_w", "hyb_sinks", "hyb_a_log", "hyb_dt_bias")
WEIGHTS = ("hyb_w_in", "hyb_sinks", "hyb_conv_w", "hyb_a_log", "hyb_dt_bias", "hyb_norm_w", "hyb_w_out", "rec_w_in",
           "rec_conv_w", "rec_conv_b", "rec_w_a", "rec_b_a", "rec_w_x", "rec_b_x", "rec_lambda", "rec_w_out", "ln1_g",
           "ln1_b", "mlp_w1", "mlp_w2", "ln2_g", "ln2_b")


def _gather_full_weights(w):
    wb = {k: w[k].astype(MXU_DTYPE) for k in BIG}
    now = ("hyb_w_in",)
    shards = [_halves(wb[k][:1]) for k in now]
    shards.append(_pack_mid(*[w[k] for k in MID]))
    got = _all_gather_weights(shards, "all_gather_weights")
    me = 2 * lax.axis_index("x") + lax.axis_index("y")
    got = [lax.dynamic_update_slice(g, s[None], (me, 0, 0, 0)) for s, g in zip(shards, got)]

    def full(k, g):
        g = g.reshape((N_CHIPS,) + w[k].shape[1:])
        if k in CHIP_MAJOR:
            return g[:, None]
        if k in ("rec_w_a", "rec_w_x"):
            return g.transpose(1, 0, 2, 3).reshape(LRU_BLOCKS, LRU_BLOCK_W, LRU_BLOCK_W)
        f = _cols_from_chips(g[:, None])[0] if k in COL_SHARDED else _rows_from_chips(g[:, None])[0]
        return _pad_hyb_cols(f) if k == "hyb_w_in" else f

    rec = ("rec_w_in", "rec_w_out", "rec_w_a", "rec_w_x")
    groups = [(("hyb_w_out",), 0), (CHIP_MAJOR, 0), (rec, 0), (CHIP_MAJOR, 1), (("hyb_w_in", "hyb_w_out"), 1),
              (CHIP_MAJOR, 2), (rec, 1), (CHIP_MAJOR, 3)]
    own = [wb[k][j] for names, j in groups for k in names]
    land = [lax.dynamic_update_slice(lax.empty((N_CHIPS,) + o.shape, o.dtype), o[None], (me,) + (0,) * o.ndim)
            for o in own]
    send_sems, recv_sems, own, land = _gather_start(own, land, got[-1], "gather_start")
    W = {k: [None] * w[k].shape[0] for k in BIG}
    for k, g in zip(now, got[:-1]):
        W[k][0] = full(k, g)
    arrived = [0]

    def ensure(upto, after):
        while arrived[0] <= upto:
            gi = arrived[0]
            names, j = groups[gi]
            lo = sum(len(nm) for nm, _ in groups[:gi])
            sl = slice(lo, lo + len(names))
            got_g = _gather_wait(send_sems[sl], recv_sems[sl], own[sl], land[sl], after, "gather_wait_%d" % gi)
            for k, g in zip(names, got_g):
                W[k][j] = full(k, g)
            arrived[0] += 1

    def mixer_ready(layer, after, out_projection=False):
        if layer:
            ensure({1: 2, 2: 4, 3: 6}[layer], after)
        elif out_projection:
            ensure(0, after)

    def mlp_w(layer, after):
        ensure({0: 1, 1: 3, 2: 5, 3: 7}[layer], after)
        return W["mlp_w1"][layer], W["mlp_w2"][layer], 0

    conv_w, rconv_w, rconv_b, b_a, b_x, lam = _unpack_mid(got[-1])
    W["hyb_conv_w"] = conv_w.transpose(1, 2, 0, 3).reshape(2, CONV_K, 3 * B_W)
    W["rec_conv_w"] = rconv_w.transpose(1, 2, 0, 3).reshape(2, CONV_K, D_MODEL)
    for k, v in (("rec_conv_b", rconv_b), ("rec_b_a", b_a), ("rec_b_x", b_x), ("rec_lambda", lam)):
        W[k] = v.transpose(1, 0, 2).reshape(2, D_MODEL)
    for k in SMALL:
        W[k] = w[k]
    return W, mlp_w, mixer_ready


REC_VECTORS = ("rec_conv_b", "rec_b_a", "rec_b_x", "rec_lambda")


def _group_by_chip(hyb, rec, d_w1, d_w2):
    t = dict(mlp_w1=d_w1, mlp_w2=d_w2)
    if hyb:
        t["hyb_w_in"] = _cols_to_chips(jnp.stack([_unpad_hyb_cols(g["hyb_w_in"]) for g in hyb]))
        t["hyb_w_out"] = _rows_to_chips(jnp.stack([g["hyb_w_out"] for g in hyb]))
    if rec:
        t["rec_w_in"] = _cols_to_chips(jnp.stack([g["rec_w_in"] for g in rec]))
        t["rec_w_out"] = _rows_to_chips(jnp.stack([g["rec_w_out"] for g in rec]))
        for k in ("rec_w_a", "rec_w_x"):
            v = jnp.stack([g[k] for g in rec])
            t[k] = v.reshape(len(rec), LRU_BLOCKS, N_CHIPS, LRU_BLOCK_W // N_CHIPS, LRU_BLOCK_W).transpose(2, 0, 1, 3, 4)
    flat = [g["hyb_conv_w"].reshape(CONV_K, N_CHIPS, -1).transpose(1, 0, 2).reshape(N_CHIPS, -1) for g in hyb]
    for g in rec:
        flat.append(g["rec_conv_w"].reshape(CONV_K, N_CHIPS, -1).transpose(1, 0, 2).reshape(N_CHIPS, -1))
        flat += [g[k].reshape(N_CHIPS, -1) for k in REC_VECTORS]
    flat = jnp.concatenate(flat, axis=1)
    flat = jnp.pad(flat, ((0, 0), (0, -flat.shape[1] % (2 * D_MODEL))))
    names = [k for k in BIG if k in t]
    out = [t[k].reshape(N_CHIPS, 2, -1, t[k].shape[-1]) for k in names]
    return names + ["small"], out + [flat.reshape(N_CHIPS, 2, -1, D_MODEL)]


def kernel(x, hyb_w_in, hyb_sinks, hyb_conv_w, hyb_a_log, hyb_dt_bias, hyb_norm_w, hyb_w_out, rec_w_in, rec_conv_w, rec_conv_b, rec_w_a, rec_b_a, rec_w_x, rec_b_x, rec_lambda, rec_w_out, ln1_g, ln1_b, mlp_w1, mlp_w2, ln2_g, ln2_b, loss_target, m_hyb_w_in, m_hyb_sinks, m_hyb_conv_w, m_hyb_a_log, m_hyb_dt_bias, m_hyb_norm_w, m_hyb_w_out, m_rec_w_in, m_rec_conv_w, m_rec_conv_b, m_rec_w_a, m_rec_b_a, m_rec_w_x, m_rec_b_x, m_rec_lambda, m_rec_w_out, m_ln1_g, m_ln1_b, m_mlp_w1, m_mlp_w2, m_ln2_g, m_ln2_b, v_hyb_w_in, v_hyb_sinks, v_hyb_conv_w, v_hyb_a_log, v_hyb_dt_bias, v_hyb_norm_w, v_hyb_w_out, v_rec_w_in, v_rec_conv_w, v_rec_conv_b, v_rec_w_a, v_rec_b_a, v_rec_w_x, v_rec_b_x, v_rec_lambda, v_rec_w_out, v_ln1_g, v_ln1_b, v_mlp_w1, v_mlp_w2, v_ln2_g, v_ln2_b):
    args = locals()
    w = {k: args[k] for k in WEIGHTS}
    m = {k: args["m_" + k] for k in WEIGHTS}
    v = {k: args["v_" + k] for k in WEIGHTS}

    W, mlp_w, mixer_ready = _gather_full_weights(w)

    core = lax.axis_index("c").astype(jnp.int32)
    me = (2 * lax.axis_index("x") + lax.axis_index("y")).astype(jnp.int32)
    slots = jnp.arange(N_CHIPS, dtype=jnp.int32)
    where = jnp.concatenate([me[None], jnp.where(slots == me, (slots + 1) % N_CHIPS, slots), core[None]])
    state = {}

    def on_group(group, hyb, rec, d_w1, d_w2):
        if group == "middle":
            group = "upper"
            send_sems, recv_sems, by_chip, recv, _ = state["sibling"]
            by_chip, from_sibling = _rs_sibling_wait(send_sems, recv_sems, by_chip, recv, hyb, "rs_sibling_wait")
        else:
            state[group + " names"], by_chip = _group_by_chip(hyb, rec, d_w1, d_w2)
            if group == "upper":
                recv = [lax.empty((N_CHIPS,) + g.shape[2:], g.dtype) for g in by_chip]
                state["sibling"] = _rs_sibling_start(by_chip, recv, "rs_sibling_start")
                return state["sibling"][4][0, 0]
            from_sibling = _rs_to_sibling(by_chip, "rs_to_sibling")
        pair = [_pair_sum(g, r, core[None], "pair_sum") for g, r in zip(by_chip, from_sibling)]
        recv = [lax.empty(p.shape, p.dtype) for p in pair]
        state[group] = _rs_across_start(pair, recv, "rs_across_start_" + group)
        return state[group][4][0, 0]

    def finish_group(group, after, prev):
        send_sems, recv_sems, pair, recv, _ = state[group]
        pair, from_chips = _rs_across_wait(send_sems, recv_sems, pair, recv, after, "rs_across_wait_" + group)
        half = [_chip_sum(r, p, where, "chip_sum") for r, p in zip(from_chips, pair)]
        joined = _rs_join_halves(half, "rs_join_halves")
        outs = dict(prev or {})
        for k, g in zip(state[group + " names"][:-1], joined[:-1]):
            n = g.shape[-1]
            w2, g2 = w[k].reshape(-1, n), g.reshape(-1, n)
            row0 = w2.shape[0] - g2.shape[0] if group == "upper" else 0
            outs[k] = _adamw_rows(w2, g2, m[k].reshape(-1, n), v[k].reshape(-1, n), row0, outs.get(k), "adamw")
        return outs, joined[-1].reshape(-1)

    loss, dx, grads = _local_step(x[0], loss_target[0], W, mlp_w, mixer_ready, on_group)
    g_out, d_out, m_out, v_out = {}, {}, {}, {}

    upper, small_upper = finish_group("upper", state["lower"][4], None)
    small_g = _pack_small(*[grads[k] for k in SMALL], scalar=loss)
    small_all = _all_gather_small(small_g, "all_gather_small")
    sw, sm, sv = (_pack_small(*[t[k] for k in SMALL]) for t in (w, m, v))
    sg, sd, snm, snv = _adamw_small(sw, small_all, sm, sv, "adamw_small")
    loss = sg[4 * DEPTH + 1, 0]
    for dst, packed in ((g_out, sg), (d_out, sd), (m_out, snm), (v_out, snv)):
        for k, val in zip(SMALL, _unpack_small(packed)):
            dst[k] = val

    done = sum([upper[k][1][0, 0] for k in BIG], sg[0, 0]).reshape(1, 1)
    both, small_lower = finish_group("lower", done, upper)
    for k in BIG:
        shape = w[k].shape
        g_out[k], d_out[k], m_out[k], v_out[k] = (t.reshape(shape) for t in both[k])
    n_conv, n_rconv, n_vec = (w[k][0].size for k in ("hyb_conv_w", "rec_conv_w", "rec_conv_b"))
    n_rec = n_rconv + len(REC_VECTORS) * n_vec
    conv_g = jnp.stack([small_lower[:n_conv], small_upper[:n_conv]])
    rec_g = jnp.stack([small_upper[n_conv + i * n_rec:n_conv + (i + 1) * n_rec] for i in range(2)])
    vec_g = [rec_g[:, n_rconv + i * n_vec:n_rconv + (i + 1) * n_vec] for i in range(len(REC_VECTORS))]
    mid_g = _pack_mid(conv_g, rec_g[:, :n_rconv], *vec_g).reshape(-1, D_MODEL)
    mid_w, mid_m, mid_v = (_pack_mid(*[t[k] for k in MID]).reshape(-1, D_MODEL) for t in (w, m, v))
    mid_d, mid_nm, mid_nv = _adamw(mid_w, mid_g, mid_m, mid_v, "adamw_mid")
    for dst, packed in ((g_out, mid_g), (d_out, mid_d), (m_out, mid_nm), (v_out, mid_nv)):
        for k, val in zip(MID, _unpack_mid(packed.reshape(2, MID_ROWS, D_MODEL))):
            dst[k] = val.reshape(w[k].shape)

    return (loss, dx[None], *[g_out[k] for k in WEIGHTS], *[d_out[k] for k in WEIGHTS],
            *[m_out[k] for k in WEIGHTS], *[v_out[k] for k in WEIGHTS])
```
